```python
import math
import jax, jax.numpy as jnp
from jax import lax
import numpy as np

D_MODEL = 1024
BATCH = 16
SEQ = 2048
DEPTH = 2

GRID_W = 64
LRU_WIDTH = D_MODEL
LRU_BLOCKS = 8
LRU_BW = LRU_WIDTH // LRU_BLOCKS
CONV_W = 4
CONV_PAD = (2, 1)
RG_C = 8.0
HEAD_DIM = 128
N_HEADS = D_MODEL // HEAD_DIM
N_KV_HEADS = 2
GROUP = N_HEADS // N_KV_HEADS
Q_BLOCK = 128
ROPE_THETA = 10000.0
AXIS_FREQS = HEAD_DIM // 4
D_FF = 4 * D_MODEL
EPS = 1e-6
N_RG = (DEPTH + 1) // 2
N_AT = DEPTH // 2

kernel_name = "hybrid_rglru_axial_gqa_encoder"


def rms_norm(x, g):
    xf = x.astype(jnp.float32)
    y = xf * lax.rsqrt(jnp.mean(xf * xf, axis=-1, keepdims=True) + EPS)
    return (y * g.astype(jnp.float32)).astype(x.dtype)


def rglru_direction(x, w_a, b_a, w_x, b_x, lam, reverse):
    B, L, C = x.shape
    xb = x.reshape(B, L, LRU_BLOCKS, LRU_BW)
    r = jax.nn.sigmoid((jnp.einsum('blhi,hij->blhj', xb, w_a).reshape(B, L, C) + b_a).astype(jnp.float32))
    i = jax.nn.sigmoid((jnp.einsum('blhi,hij->blhj', xb, w_x).reshape(B, L, C) + b_x).astype(jnp.float32))
    log_a = -RG_C * r * jax.nn.softplus(-lam.astype(jnp.float32))
    a = jnp.exp(log_a)
    mult = jnp.sqrt(-jnp.expm1(2.0 * log_a))
    u = mult * (i * x.astype(jnp.float32))

    def combine(p, q):
        a1, b1 = p
        a2, b2 = q
        return a1 * a2, a2 * b1 + b2

    _, h = lax.associative_scan(combine, (a, u), axis=1, reverse=reverse)
    return h.astype(x.dtype)


def rglru_block(h, w_in, conv_w, conv_b, w_a, b_a, w_x, b_x, lam, w_out):
    z = h @ w_in
    gate, rec = jnp.split(z, 2, axis=-1)
    gate = jax.nn.gelu(gate)
    rec = lax.conv_general_dilated(rec, conv_w, window_strides=(1,), padding=[CONV_PAD],
                                   dimension_numbers=('NWC', 'WIO', 'NWC'),
                                   feature_group_count=LRU_WIDTH) + conv_b
    y = (rglru_direction(rec, w_a[0], b_a[0], w_x[0], b_x[0], lam[0], False)
         + rglru_direction(rec, w_a[1], b_a[1], w_x[1], b_x[1], lam[1], True))
    return (y * gate) @ w_out


def axial_rope_tables(L):
    rows = L // GRID_W
    row = jnp.repeat(jnp.arange(rows, dtype=jnp.float32), GRID_W)
    col = jnp.tile(jnp.arange(GRID_W, dtype=jnp.float32), rows)
    inv = ROPE_THETA ** (-jnp.arange(AXIS_FREQS, dtype=jnp.float32) / AXIS_FREQS)
    ang_r = row[:, None] * inv
    ang_c = col[:, None] * inv
    return (jnp.cos(ang_r)[:, None, :], jnp.sin(ang_r)[:, None, :],
            jnp.cos(ang_c)[:, None, :], jnp.sin(ang_c)[:, None, :])


def rope_half(x, cos, sin):
    x1, x2 = jnp.split(x, 2, axis=-1)
    return jnp.concatenate([x1 * cos - x2 * sin, x2 * cos + x1 * sin], axis=-1)


def apply_axial_rope(x, tabs):
    cr, sr, cc, sc = tabs
    xf = x.astype(jnp.float32)
    xr, xc = jnp.split(xf, 2, axis=-1)
    return jnp.concatenate([rope_half(xr, cr, sr), rope_half(xc, cc, sc)], axis=-1).astype(x.dtype)


def attention_block(h, w_qkv, q_g, k_g, w_o):
    B, L, _ = h.shape
    qkv = h @ w_qkv
    q, k, v = jnp.split(qkv, [N_HEADS * HEAD_DIM, (N_HEADS + N_KV_HEADS) * HEAD_DIM], axis=-1)
    q = rms_norm(q.reshape(B, L, N_HEADS, HEAD_DIM), q_g)
    k = rms_norm(k.reshape(B, L, N_KV_HEADS, HEAD_DIM), k_g)
    v = v.reshape(B, L, N_KV_HEADS, HEAD_DIM)
    tabs = axial_rope_tables(L)
    q = apply_axial_rope(q, tabs)
    k = apply_axial_rope(k, tabs)
    nb = L // Q_BLOCK
    qb = q.reshape(B, nb, Q_BLOCK, N_KV_HEADS, GROUP, HEAD_DIM).transpose(1, 0, 2, 3, 4, 5)
    scale = 1.0 / math.sqrt(HEAD_DIM)

    def attend(qblk):
        s = jnp.einsum('bqkgd,bskd->bkgqs', qblk, k).astype(jnp.float32) * scale
        p = jax.nn.softmax(s, axis=-1).astype(v.dtype)
        return jnp.einsum('bkgqs,bskd->bqkgd', p, v)

    o = lax.map(attend, qb)
    o = o.transpose(1, 0, 2, 3, 4, 5).reshape(B, L, N_HEADS * HEAD_DIM)
    return o @ w_o


def sq_relu_mlp(h, w_up, w_down):
    u = jax.nn.relu(h @ w_up)
    return (u * u) @ w_down


def _fwd_setup_inputs(seed: int = 0) -> dict:
    key = jax.random.key(seed)
    ks = jax.random.split(key, 24)
    f32 = jnp.float32
    nrm = lambda k, shape, fan_in: jax.random.normal(k, shape, f32) * (fan_in ** -0.5)
    gain = lambda k, shape: 1.0 + 0.02 * jax.random.normal(k, shape, f32)
    small = lambda k, shape: 0.01 * jax.random.normal(k, shape, f32)
    u = jax.random.uniform(ks[10], (N_RG, 2, LRU_WIDTH), f32, 0.9, 0.999)
    s = u ** (1.0 / RG_C)
    lam = jnp.log(s) - jnp.log1p(-s)
    return {
        "x": jax.random.normal(ks[0], (BATCH, SEQ, D_MODEL), f32),
        "norm_mix_g": gain(ks[1], (DEPTH, D_MODEL)),
        "norm_mlp_g": gain(ks[2], (DEPTH, D_MODEL)),
        "rg_w_in": nrm(ks[3], (N_RG, D_MODEL, 2 * LRU_WIDTH), D_MODEL),
        "rg_conv_w": nrm(ks[4], (N_RG, CONV_W, 1, LRU_WIDTH), CONV_W),
        "rg_conv_b": small(ks[5], (N_RG, LRU_WIDTH)),
        "rg_w_a": nrm(ks[6], (N_RG, 2, LRU_BLOCKS, LRU_BW, LRU_BW), LRU_BW),
        "rg_b_a": small(ks[7], (N_RG, 2, LRU_WIDTH)),
        "rg_w_x": nrm(ks[8], (N_RG, 2, LRU_BLOCKS, LRU_BW, LRU_BW), LRU_BW),
        "rg_b_x": small(ks[9], (N_RG, 2, LRU_WIDTH)),
        "rg_lam": lam,
        "rg_w_out": nrm(ks[11], (N_RG, LRU_WIDTH, D_MODEL), LRU_WIDTH),
        "at_w_qkv": nrm(ks[12], (N_AT, D_MODEL, (N_HEADS + 2 * N_KV_HEADS) * HEAD_DIM), D_MODEL),
        "at_q_g": gain(ks[13], (N_AT, HEAD_DIM)),
        "at_k_g": gain(ks[14], (N_AT, HEAD_DIM)),
        "at_w_o": nrm(ks[15], (N_AT, N_HEADS * HEAD_DIM, D_MODEL), N_HEADS * HEAD_DIM),
        "mlp_w_up": nrm(ks[16], (DEPTH, D_MODEL, D_FF), D_MODEL),
        "mlp_w_down": nrm(ks[17], (DEPTH, D_FF, D_MODEL), D_FF),
        "final_g": gain(ks[18], (D_MODEL,)),
    }


def _fwd_reference(x, norm_mix_g, norm_mlp_g, rg_w_in, rg_conv_w, rg_conv_b, rg_w_a, rg_b_a,
              rg_w_x, rg_b_x, rg_lam, rg_w_out, at_w_qkv, at_q_g, at_k_g, at_w_o,
              mlp_w_up, mlp_w_down, final_g):
    for i in range(DEPTH):
        h = rms_norm(x, norm_mix_g[i])
        j = i // 2
        if i % 2 == 0:
            mix = rglru_block(h, rg_w_in[j], rg_conv_w[j], rg_conv_b[j], rg_w_a[j], rg_b_a[j],
                              rg_w_x[j], rg_b_x[j], rg_lam[j], rg_w_out[j])
        else:
            mix = attention_block(h, at_w_qkv[j], at_q_g[j], at_k_g[j], at_w_o[j])
        x = x + mix
        x = x + sq_relu_mlp(rms_norm(x, norm_mlp_g[i]), mlp_w_up[i], mlp_w_down[i])
    return rms_norm(x, final_g)


import jax as _jax
import jax.numpy as _jnp

TWIN_FORMAT = 'train_step'
FWD_PARAMS = ['x', 'norm_mix_g', 'norm_mlp_g', 'rg_w_in', 'rg_conv_w', 'rg_conv_b', 'rg_w_a', 'rg_b_a', 'rg_w_x', 'rg_b_x', 'rg_lam', 'rg_w_out', 'at_w_qkv', 'at_q_g', 'at_k_g', 'at_w_o', 'mlp_w_up', 'mlp_w_down', 'final_g']
TWIN_WEIGHTS = ['norm_mix_g', 'norm_mlp_g', 'rg_w_in', 'rg_conv_w', 'rg_conv_b', 'rg_w_a', 'rg_b_a', 'rg_w_x', 'rg_b_x', 'rg_lam', 'rg_w_out', 'at_w_qkv', 'at_q_g', 'at_k_g', 'at_w_o', 'mlp_w_up', 'mlp_w_down', 'final_g']
TWIN_DIFF_INPUT = 'x'
TWIN_INPUTS = ['x', 'norm_mix_g', 'norm_mlp_g', 'rg_w_in', 'rg_conv_w', 'rg_conv_b', 'rg_w_a', 'rg_b_a', 'rg_w_x', 'rg_b_x', 'rg_lam', 'rg_w_out', 'at_w_qkv', 'at_q_g', 'at_k_g', 'at_w_o', 'mlp_w_up', 'mlp_w_down', 'final_g', 'loss_target', 'm_norm_mix_g', 'm_norm_mlp_g', 'm_rg_w_in', 'm_rg_conv_w', 'm_rg_conv_b', 'm_rg_w_a', 'm_rg_b_a', 'm_rg_w_x', 'm_rg_b_x', 'm_rg_lam', 'm_rg_w_out', 'm_at_w_qkv', 'm_at_q_g', 'm_at_k_g', 'm_at_w_o', 'm_mlp_w_up', 'm_mlp_w_down', 'm_final_g', 'v_norm_mix_g', 'v_norm_mlp_g', 'v_rg_w_in', 'v_rg_conv_w', 'v_rg_conv_b', 'v_rg_w_a', 'v_rg_b_a', 'v_rg_w_x', 'v_rg_b_x', 'v_rg_lam', 'v_rg_w_out', 'v_at_w_qkv', 'v_at_q_g', 'v_at_k_g', 'v_at_w_o', 'v_mlp_w_up', 'v_mlp_w_down', 'v_final_g']
TWIN_OUTPUTS = ['loss', 'grad_x', 'grad_norm_mix_g', 'grad_norm_mlp_g', 'grad_rg_w_in', 'grad_rg_conv_w', 'grad_rg_conv_b', 'grad_rg_w_a', 'grad_rg_b_a', 'grad_rg_w_x', 'grad_rg_b_x', 'grad_rg_lam', 'grad_rg_w_out', 'grad_at_w_qkv', 'grad_at_q_g', 'grad_at_k_g', 'grad_at_w_o', 'grad_mlp_w_up', 'grad_mlp_w_down', 'grad_final_g', 'delta_norm_mix_g', 'delta_norm_mlp_g', 'delta_rg_w_in', 'delta_rg_conv_w', 'delta_rg_conv_b', 'delta_rg_w_a', 'delta_rg_b_a', 'delta_rg_w_x', 'delta_rg_b_x', 'delta_rg_lam', 'delta_rg_w_out', 'delta_at_w_qkv', 'delta_at_q_g', 'delta_at_k_g', 'delta_at_w_o', 'delta_mlp_w_up', 'delta_mlp_w_down', 'delta_final_g', 'new_m_norm_mix_g', 'new_m_norm_mlp_g', 'new_m_rg_w_in', 'new_m_rg_conv_w', 'new_m_rg_conv_b', 'new_m_rg_w_a', 'new_m_rg_b_a', 'new_m_rg_w_x', 'new_m_rg_b_x', 'new_m_rg_lam', 'new_m_rg_w_out', 'new_m_at_w_qkv', 'new_m_at_q_g', 'new_m_at_k_g', 'new_m_at_w_o', 'new_m_mlp_w_up', 'new_m_mlp_w_down', 'new_m_final_g', 'new_v_norm_mix_g', 'new_v_norm_mlp_g', 'new_v_rg_w_in', 'new_v_rg_conv_w', 'new_v_rg_conv_b', 'new_v_rg_w_a', 'new_v_rg_b_a', 'new_v_rg_w_x', 'new_v_rg_b_x', 'new_v_rg_lam', 'new_v_rg_w_out', 'new_v_at_w_qkv', 'new_v_at_q_g', 'new_v_at_k_g', 'new_v_at_w_o', 'new_v_mlp_w_up', 'new_v_mlp_w_down', 'new_v_final_g']
TWIN_LEAF_KINDS = {'loss': 'loss', 'grad_x': 'grad_x', 'grad_norm_mix_g': 'grad_w', 'grad_norm_mlp_g': 'grad_w', 'grad_rg_w_in': 'grad_w', 'grad_rg_conv_w': 'grad_w', 'grad_rg_conv_b': 'grad_w', 'grad_rg_w_a': 'grad_w', 'grad_rg_b_a': 'grad_w', 'grad_rg_w_x': 'grad_w', 'grad_rg_b_x': 'grad_w', 'grad_rg_lam': 'grad_w', 'grad_rg_w_out': 'grad_w', 'grad_at_w_qkv': 'grad_w', 'grad_at_q_g': 'grad_w', 'grad_at_k_g': 'grad_w', 'grad_at_w_o': 'grad_w', 'grad_mlp_w_up': 'grad_w', 'grad_mlp_w_down': 'grad_w', 'grad_final_g': 'grad_w', 'delta_norm_mix_g': 'delta_w', 'delta_norm_mlp_g': 'delta_w', 'delta_rg_w_in': 'delta_w', 'delta_rg_conv_w': 'delta_w', 'delta_rg_conv_b': 'delta_w', 'delta_rg_w_a': 'delta_w', 'delta_rg_b_a': 'delta_w', 'delta_rg_w_x': 'delta_w', 'delta_rg_b_x': 'delta_w', 'delta_rg_lam': 'delta_w', 'delta_rg_w_out': 'delta_w', 'delta_at_w_qkv': 'delta_w', 'delta_at_q_g': 'delta_w', 'delta_at_k_g': 'delta_w', 'delta_at_w_o': 'delta_w', 'delta_mlp_w_up': 'delta_w', 'delta_mlp_w_down': 'delta_w', 'delta_final_g': 'delta_w', 'new_m_norm_mix_g': 'new_m', 'new_m_norm_mlp_g': 'new_m', 'new_m_rg_w_in': 'new_m', 'new_m_rg_conv_w': 'new_m', 'new_m_rg_conv_b': 'new_m', 'new_m_rg_w_a': 'new_m', 'new_m_rg_b_a': 'new_m', 'new_m_rg_w_x': 'new_m', 'new_m_rg_b_x': 'new_m', 'new_m_rg_lam': 'new_m', 'new_m_rg_w_out': 'new_m', 'new_m_at_w_qkv': 'new_m', 'new_m_at_q_g': 'new_m', 'new_m_at_k_g': 'new_m', 'new_m_at_w_o': 'new_m', 'new_m_mlp_w_up': 'new_m', 'new_m_mlp_w_down': 'new_m', 'new_m_final_g': 'new_m', 'new_v_norm_mix_g': 'new_v', 'new_v_norm_mlp_g': 'new_v', 'new_v_rg_w_in': 'new_v', 'new_v_rg_conv_w': 'new_v', 'new_v_rg_conv_b': 'new_v', 'new_v_rg_w_a': 'new_v', 'new_v_rg_b_a': 'new_v', 'new_v_rg_w_x': 'new_v', 'new_v_rg_b_x': 'new_v', 'new_v_rg_lam': 'new_v', 'new_v_rg_w_out': 'new_v', 'new_v_at_w_qkv': 'new_v', 'new_v_at_q_g': 'new_v', 'new_v_at_k_g': 'new_v', 'new_v_at_w_o': 'new_v', 'new_v_mlp_w_up': 'new_v', 'new_v_mlp_w_down': 'new_v', 'new_v_final_g': 'new_v'}


def _forward(args):
    return _fwd_reference(*[args[k] for k in FWD_PARAMS])


def _output_shape():
    out = _jax.eval_shape(lambda: _forward(_fwd_setup_inputs(0)))
    return out.shape, out.dtype

N_MICROBATCH = 1
ADAM_LR = 0.001
ADAM_B1 = 0.9
ADAM_B2 = 0.999
ADAM_EPS = 1e-08
ADAM_WD = 0.01
ADAM_STEP = 10
PER_EXAMPLE_BATCH_AXIS = {'x': 0, 'loss_target': 0}
SHARED_INPUTS = []
_WEIGHT_DTYPES = {'norm_mix_g': _jnp.float32, 'norm_mlp_g': _jnp.float32, 'rg_w_in': _jnp.float32, 'rg_conv_w': _jnp.float32, 'rg_conv_b': _jnp.float32, 'rg_w_a': _jnp.float32, 'rg_b_a': _jnp.float32, 'rg_w_x': _jnp.float32, 'rg_b_x': _jnp.float32, 'rg_lam': _jnp.float32, 'rg_w_out': _jnp.float32, 'at_w_qkv': _jnp.float32, 'at_q_g': _jnp.float32, 'at_k_g': _jnp.float32, 'at_w_o': _jnp.float32, 'mlp_w_up': _jnp.float32, 'mlp_w_down': _jnp.float32, 'final_g': _jnp.float32}
MOMENT_SCALE = {'norm_mix_g': 1.176897e-01, 'norm_mlp_g': 1.380896e-01, 'rg_w_in': 1.079059e-01, 'rg_conv_w': 1.091893e-01, 'rg_conv_b': 1.862660e+00, 'rg_w_a': 2.518099e-02, 'rg_b_a': 1.957429e-02, 'rg_w_x': 4.568427e-02, 'rg_b_x': 2.625747e-02, 'rg_lam': 3.709754e-02, 'rg_w_out': 1.135341e-01, 'at_w_qkv': 3.838714e-02, 'at_q_g': 2.653380e-02, 'at_k_g': 2.701262e-02, 'at_w_o': 4.289593e-02, 'mlp_w_up': 7.028382e-02, 'mlp_w_down': 1.404269e-01, 'final_g': 3.268880e+01}


def _to_microbatches(a, axis):
    t = _jnp.moveaxis(a, axis, 0)
    t = t.reshape((N_MICROBATCH, t.shape[0] // N_MICROBATCH) + t.shape[1:])
    return _jnp.moveaxis(t, 1, axis + 1)


def setup_inputs(seed: int = 0) -> dict:
    inp = _fwd_setup_inputs(seed)
    key = _jax.random.fold_in(_jax.random.key(seed), 7919)
    shape, _ = _output_shape()
    out = dict(inp)
    out["loss_target"] = _jax.random.normal(_jax.random.fold_in(key, 0), shape, _jnp.float32)
    for i, name in enumerate(TWIN_WEIGHTS):
        w = inp[name].astype(_jnp.float32)
        if MOMENT_SCALE is None:
            s = _jnp.sqrt(_jnp.mean(_jnp.square(w)) + 1e-30)
        else:
            s = MOMENT_SCALE[name]
        km, kv = _jax.random.split(_jax.random.fold_in(key, i + 1))
        out[name] = w
        out["m_" + name] = s * _jax.random.normal(km, w.shape, _jnp.float32)
        out["v_" + name] = (s * s) * _jax.random.uniform(kv, w.shape, _jnp.float32, 0.5, 1.5)
    if N_MICROBATCH > 1:
        for name, axis in PER_EXAMPLE_BATCH_AXIS.items():
            out[name] = _to_microbatches(out[name], axis)
    return {'x': out['x'], 'norm_mix_g': out['norm_mix_g'], 'norm_mlp_g': out['norm_mlp_g'], 'rg_w_in': out['rg_w_in'], 'rg_conv_w': out['rg_conv_w'], 'rg_conv_b': out['rg_conv_b'], 'rg_w_a': out['rg_w_a'], 'rg_b_a': out['rg_b_a'], 'rg_w_x': out['rg_w_x'], 'rg_b_x': out['rg_b_x'], 'rg_lam': out['rg_lam'], 'rg_w_out': out['rg_w_out'], 'at_w_qkv': out['at_w_qkv'], 'at_q_g': out['at_q_g'], 'at_k_g': out['at_k_g'], 'at_w_o': out['at_w_o'], 'mlp_w_up': out['mlp_w_up'], 'mlp_w_down': out['mlp_w_down'], 'final_g': out['final_g'], 'loss_target': out['loss_target'], 'm_norm_mix_g': out['m_norm_mix_g'], 'm_norm_mlp_g': out['m_norm_mlp_g'], 'm_rg_w_in': out['m_rg_w_in'], 'm_rg_conv_w': out['m_rg_conv_w'], 'm_rg_conv_b': out['m_rg_conv_b'], 'm_rg_w_a': out['m_rg_w_a'], 'm_rg_b_a': out['m_rg_b_a'], 'm_rg_w_x': out['m_rg_w_x'], 'm_rg_b_x': out['m_rg_b_x'], 'm_rg_lam': out['m_rg_lam'], 'm_rg_w_out': out['m_rg_w_out'], 'm_at_w_qkv': out['m_at_w_qkv'], 'm_at_q_g': out['m_at_q_g'], 'm_at_k_g': out['m_at_k_g'], 'm_at_w_o': out['m_at_w_o'], 'm_mlp_w_up': out['m_mlp_w_up'], 'm_mlp_w_down': out['m_mlp_w_down'], 'm_final_g': out['m_final_g'], 'v_norm_mix_g': out['v_norm_mix_g'], 'v_norm_mlp_g': out['v_norm_mlp_g'], 'v_rg_w_in': out['v_rg_w_in'], 'v_rg_conv_w': out['v_rg_conv_w'], 'v_rg_conv_b': out['v_rg_conv_b'], 'v_rg_w_a': out['v_rg_w_a'], 'v_rg_b_a': out['v_rg_b_a'], 'v_rg_w_x': out['v_rg_w_x'], 'v_rg_b_x': out['v_rg_b_x'], 'v_rg_lam': out['v_rg_lam'], 'v_rg_w_out': out['v_rg_w_out'], 'v_at_w_qkv': out['v_at_w_qkv'], 'v_at_q_g': out['v_at_q_g'], 'v_at_k_g': out['v_at_k_g'], 'v_at_w_o': out['v_at_w_o'], 'v_mlp_w_up': out['v_mlp_w_up'], 'v_mlp_w_down': out['v_mlp_w_down'], 'v_final_g': out['v_final_g']}


def _loss(weights, diff, rest, loss_target):
    with _jax.named_scope("forward"):
        args = {**rest, TWIN_DIFF_INPUT: diff, **{k: w.astype(_WEIGHT_DTYPES[k]) for k, w in weights.items()}}
        y = _forward(args)
    with _jax.named_scope("loss_head"):
        err = _jnp.square(y.astype(_jnp.float32) - loss_target)
        return 0.5 * _jnp.sum(_jnp.mean(err, axis=-1)) if err.ndim else 0.5 * err


def _adamw(w, g, m, v):
    m = ADAM_B1 * m + (1.0 - ADAM_B1) * g
    v = ADAM_B2 * v + (1.0 - ADAM_B2) * _jnp.square(g)
    m_hat = m / (1.0 - ADAM_B1 ** ADAM_STEP)
    v_hat = v / (1.0 - ADAM_B2 ** ADAM_STEP)
    delta = -ADAM_LR * (m_hat / (_jnp.sqrt(v_hat) + ADAM_EPS) + ADAM_WD * w)
    return delta, m, v


def reference(x, norm_mix_g, norm_mlp_g, rg_w_in, rg_conv_w, rg_conv_b, rg_w_a, rg_b_a, rg_w_x, rg_b_x, rg_lam, rg_w_out, at_w_qkv, at_q_g, at_k_g, at_w_o, mlp_w_up, mlp_w_down, final_g, loss_target, m_norm_mix_g, m_norm_mlp_g, m_rg_w_in, m_rg_conv_w, m_rg_conv_b, m_rg_w_a, m_rg_b_a, m_rg_w_x, m_rg_b_x, m_rg_lam, m_rg_w_out, m_at_w_qkv, m_at_q_g, m_at_k_g, m_at_w_o, m_mlp_w_up, m_mlp_w_down, m_final_g, v_norm_mix_g, v_norm_mlp_g, v_rg_w_in, v_rg_conv_w, v_rg_conv_b, v_rg_w_a, v_rg_b_a, v_rg_w_x, v_rg_b_x, v_rg_lam, v_rg_w_out, v_at_w_qkv, v_at_q_g, v_at_k_g, v_at_w_o, v_mlp_w_up, v_mlp_w_down, v_final_g):
    given = dict(x=x, norm_mix_g=norm_mix_g, norm_mlp_g=norm_mlp_g, rg_w_in=rg_w_in, rg_conv_w=rg_conv_w, rg_conv_b=rg_conv_b, rg_w_a=rg_w_a, rg_b_a=rg_b_a, rg_w_x=rg_w_x, rg_b_x=rg_b_x, rg_lam=rg_lam, rg_w_out=rg_w_out, at_w_qkv=at_w_qkv, at_q_g=at_q_g, at_k_g=at_k_g, at_w_o=at_w_o, mlp_w_up=mlp_w_up, mlp_w_down=mlp_w_down, final_g=final_g, loss_target=loss_target, m_norm_mix_g=m_norm_mix_g, m_norm_mlp_g=m_norm_mlp_g, m_rg_w_in=m_rg_w_in, m_rg_conv_w=m_rg_conv_w, m_rg_conv_b=m_rg_conv_b, m_rg_w_a=m_rg_w_a, m_rg_b_a=m_rg_b_a, m_rg_w_x=m_rg_w_x, m_rg_b_x=m_rg_b_x, m_rg_lam=m_rg_lam, m_rg_w_out=m_rg_w_out, m_at_w_qkv=m_at_w_qkv, m_at_q_g=m_at_q_g, m_at_k_g=m_at_k_g, m_at_w_o=m_at_w_o, m_mlp_w_up=m_mlp_w_up, m_mlp_w_down=m_mlp_w_down, m_final_g=m_final_g, v_norm_mix_g=v_norm_mix_g, v_norm_mlp_g=v_norm_mlp_g, v_rg_w_in=v_rg_w_in, v_rg_conv_w=v_rg_conv_w, v_rg_conv_b=v_rg_conv_b, v_rg_w_a=v_rg_w_a, v_rg_b_a=v_rg_b_a, v_rg_w_x=v_rg_w_x, v_rg_b_x=v_rg_b_x, v_rg_lam=v_rg_lam, v_rg_w_out=v_rg_w_out, v_at_w_qkv=v_at_w_qkv, v_at_q_g=v_at_q_g, v_at_k_g=v_at_k_g, v_at_w_o=v_at_w_o, v_mlp_w_up=v_mlp_w_up, v_mlp_w_down=v_mlp_w_down, v_final_g=v_final_g)
    weights = {n: given[n] for n in TWIN_WEIGHTS}
    shared = {n: given[n] for n in SHARED_INPUTS}
    per_example = {n: given[n] for n in ['x']}
    grad_fn = _jax.value_and_grad(_loss, argnums=(0, 1))

    def one_microbatch(ex, loss_target):
        ex = dict(ex)
        diff = ex.pop(TWIN_DIFF_INPUT)
        return grad_fn(weights, diff, {**shared, **ex}, loss_target)

    if N_MICROBATCH == 1:
        loss, (grad_w, grad_x) = one_microbatch(per_example, given["loss_target"])
    else:
        def body(carry, xs):
            loss_sum, grad_sum = carry
            l_k, (gw_k, gx_k) = one_microbatch(xs[0], xs[1])
            with _jax.named_scope("update"):
                return (loss_sum + l_k, _jax.tree.map(_jnp.add, grad_sum, gw_k)), gx_k

        init = (_jnp.zeros((), _jnp.float32), _jax.tree.map(_jnp.zeros_like, weights))
        (loss, grad_w), grad_x = _jax.lax.scan(body, init, (per_example, given["loss_target"]))
    with _jax.named_scope("update"):
        delta_w, new_m, new_v = {}, {}, {}
        for n in TWIN_WEIGHTS:
            delta_w[n], new_m[n], new_v[n] = _adamw(weights[n], grad_w[n], given["m_" + n], given["v_" + n])
    return (loss, grad_x, *[grad_w[n] for n in TWIN_WEIGHTS], *[delta_w[n] for n in TWIN_WEIGHTS],
            *[new_m[n] for n in TWIN_WEIGHTS], *[new_v[n] for n in TWIN_WEIGHTS])
```

```python
import functools
import math

import jax
import jax.numpy as jnp
from jax import lax
from jax.experimental import pallas as pl
from jax.experimental.pallas import tpu as pltpu

F32 = jnp.float32
BF16 = jnp.bfloat16

D_MODEL = 1024
HEAD_DIM = 128
N_HEADS = 8
N_KV = 2
GROUP = N_HEADS // N_KV
LRU_BLOCKS = 8
LRU_BW = 128
GRID_W = 64
ROPE_THETA = 10000.0
EPS = 1e-6
RG_C = 8.0
SCALE = 1.0 / math.sqrt(HEAD_DIM)
N_CHIPS = 4

ADAM_LR = 0.001
ADAM_B1 = 0.9
ADAM_B2 = 0.999
ADAM_EPS = 1e-08
ADAM_WD = 0.01
ADAM_STEP = 10

V7X_VMEM_BYTES = 64 * 1024 * 1024
VMEM_LIMIT = V7X_VMEM_BYTES * 3 // 4
LANES = 128
SUBLANES = 8

SMALL_ROWS = 32
GATE_ROWS = 512
PACK_ROWS = SMALL_ROWS + GATE_ROWS


def _params(sem):
    return pltpu.CompilerParams(dimension_semantics=sem, vmem_limit_bytes=VMEM_LIMIT)


def _fit(t, n):
    if n <= t:
        return n
    c = (t // LANES) * LANES
    while n % c:
        c -= LANES
    return c


def _mm(a, b, *, mode, name, out_dtypes=(F32,), b_shard=False, o_shard=False, extras=(), epi=None,
        tm=1024, tn=1024, tk=1024):
    if mode == "tn":
        K, M = a.shape
        N = b.shape[1]
    else:
        M, K = a.shape
        if mode == "nn":
            N = b.shape[0] * b.shape[2] if b_shard else b.shape[1]
        else:
            N = b.shape[1] if b_shard else b.shape[0]
    tm = _fit(tm, M)
    if b_shard and mode == "nn":
        ns = b.shape[2]
        tn = _fit(tn, ns)
    elif o_shard:
        ns = N // N_CHIPS
        tn = _fit(tn, ns)
    else:
        tn = _fit(tn, N)
    if b_shard and mode == "nt":
        ks = b.shape[2]
        tk = _fit(tk, ks)
    else:
        tk = _fit(tk, K)
    nk = K // tk
    grid = (M // tm, N // tn, nk)

    if mode == "tn":
        a_spec = pl.BlockSpec((tk, tm), lambda i, j, k: (k, i))
        b_spec = pl.BlockSpec((tk, tn), lambda i, j, k: (k, j))
        dims = (((0,), (0,)), ((), ()))
    elif mode == "nn":
        a_spec = pl.BlockSpec((tm, tk), lambda i, j, k: (i, k))
        if b_shard:
            q = ns // tn
            b_spec = pl.BlockSpec((None, tk, tn), lambda i, j, k: (j // q, k, j % q))
        else:
            b_spec = pl.BlockSpec((tk, tn), lambda i, j, k: (k, j))
        dims = (((1,), (0,)), ((), ()))
    else:
        a_spec = pl.BlockSpec((tm, tk), lambda i, j, k: (i, k))
        if b_shard:
            q = ks // tk
            b_spec = pl.BlockSpec((None, tn, tk), lambda i, j, k: (k // q, j, k % q))
        else:
            b_spec = pl.BlockSpec((tn, tk), lambda i, j, k: (j, k))
        dims = (((1,), (1,)), ((), ()))

    if o_shard:
        qo = ns // tn
        o_specs = [pl.BlockSpec((None, tm, tn), lambda i, j, k: (j // qo, i, j % qo))]
        o_shapes = [jax.ShapeDtypeStruct((N_CHIPS, M, ns), out_dtypes[0])]
    else:
        o_specs = [pl.BlockSpec((tm, tn), lambda i, j, k: (i, j)) for _ in out_dtypes]
        o_shapes = [jax.ShapeDtypeStruct((M, N), dt) for dt in out_dtypes]
    e_specs = [pl.BlockSpec((tm, tn), lambda i, j, k: (i, j)) for _ in extras]
    n_e, n_o = len(extras), len(out_dtypes)
    if epi is None:
        epi = lambda acc: (acc,)

    def body(a_ref, b_ref, *rest):
        e_refs, o_refs = rest[:n_e], rest[n_e:n_e + n_o]

        def finish(acc):
            outs = epi(acc, *[r[...] for r in e_refs])
            for r, o in zip(o_refs, outs):
                r[...] = o.astype(r.dtype)

        part = lax.dot_general(a_ref[...], b_ref[...], dims, preferred_element_type=F32)
        if nk == 1:
            finish(part)
        else:
            acc_ref = rest[n_e + n_o]
            k = pl.program_id(2)

            @pl.when(k == 0)
            def _():
                acc_ref[...] = part

            @pl.when(k > 0)
            def _():
                acc_ref[...] += part

            @pl.when(k == nk - 1)
            def _():
                finish(acc_ref[...])

    scratch = [] if nk == 1 else [pltpu.VMEM((tm, tn), F32)]
    outs = pl.pallas_call(
        body, name=name, grid=grid, in_specs=[a_spec, b_spec] + e_specs, out_specs=o_specs,
        out_shape=o_shapes, scratch_shapes=scratch,
        compiler_params=_params(("parallel", "parallel", "arbitrary")),
    )(a, b, *extras)
    return outs[0] if n_o == 1 else outs


def _rowwise(fn, rows, bcast, outs, accs=(), *, tm, name):
    def norm(r):
        return r if isinstance(r, tuple) else (r, r.shape[1], 0)

    rows = [norm(r) for r in rows]
    T = rows[0][0].shape[0]
    tm = min(tm, T)
    while T % tm:
        tm -= SUBLANES
    n_r, n_b, n_o, n_a = len(rows), len(bcast), len(outs), len(accs)
    in_specs = [pl.BlockSpec((tm, c), functools.partial(lambda i, cb: (i, cb), cb=cb)) for _, c, cb in rows]
    in_specs += [pl.BlockSpec(b.shape, lambda i: (0, 0)) for b in bcast]
    out_specs = [pl.BlockSpec((tm, c), lambda i: (i, 0)) for c, _ in outs]
    out_specs += [pl.BlockSpec(s, lambda i: (0, 0)) for s in accs]
    out_shape = [jax.ShapeDtypeStruct((T, c), dt) for c, dt in outs]
    out_shape += [jax.ShapeDtypeStruct(s, F32) for s in accs]

    def body(*refs):
        in_refs = refs[:n_r]
        b_refs = refs[n_r:n_r + n_b]
        o_refs = refs[n_r + n_b:n_r + n_b + n_o]
        a_refs = refs[n_r + n_b + n_o:]
        if n_a:
            @pl.when(pl.program_id(0) == 0)
            def _():
                for r in a_refs:
                    r[...] = jnp.zeros(r.shape, F32)
        fn(in_refs, b_refs, o_refs, a_refs)

    res = pl.pallas_call(
        body, name=name, grid=(T // tm,), in_specs=in_specs, out_specs=out_specs, out_shape=out_shape,
        compiler_params=_params(("arbitrary",) if n_a else ("parallel",)),
    )(*[r[0] for r in rows], *bcast)
    return res


def _rsum(x):
    return jnp.sum(x, axis=0, keepdims=True)


def _rms_fwd(x, g, name):
    def fn(ins, bs, outs, accs):
        xv = ins[0][...]
        r = lax.rsqrt(jnp.mean(xv * xv, axis=-1, keepdims=True) + EPS)
        outs[0][...] = (xv * r * bs[0][...]).astype(BF16)

    return _rowwise(fn, [x], [g], [(D_MODEL, BF16)], tm=512, name=name)[0]


def _rms_bwd_math(xv, dh, g):
    r = lax.rsqrt(jnp.mean(xv * xv, axis=-1, keepdims=True) + EPS)
    hn = xv * r
    dgh = dh * g
    dx = r * (dgh - hn * jnp.mean(dgh * hn, axis=-1, keepdims=True))
    return dx, _rsum(dh * hn)


def _rms_bwd(x, dh, dres, g, name):
    def fn(ins, bs, outs, accs):
        dx, dg = _rms_bwd_math(ins[0][...], ins[1][...], bs[0][...])
        dx = dx + ins[2][...]
        outs[0][...] = dx
        outs[1][...] = dx.astype(BF16)
        accs[0][...] += dg

    return _rowwise(fn, [x, dh, dres], [g], [(D_MODEL, F32), (D_MODEL, BF16)], [(1, D_MODEL)], tm=256, name=name)


def _final_loss(x, target, g):
    def fn(ins, bs, outs, accs):
        xv = ins[0][...]
        gv = bs[0][...]
        r = lax.rsqrt(jnp.mean(xv * xv, axis=-1, keepdims=True) + EPS)
        e = xv * r * gv - ins[1][...]
        tok = jnp.mean(e * e, axis=-1, keepdims=True)
        accs[0][...] += 0.5 * jnp.sum(tok, axis=0, keepdims=True) * jnp.ones((1, LANES), F32)
        dx, dg = _rms_bwd_math(xv, e * (1.0 / D_MODEL), gv)
        outs[0][...] = dx
        outs[1][...] = dx.astype(BF16)
        accs[1][...] += dg

    return _rowwise(fn, [x, target], [g], [(D_MODEL, F32), (D_MODEL, BF16)], [(1, LANES), (1, D_MODEL)],
                    tm=256, name="final_loss")


def _relu2(acc):
    r = jnp.maximum(acc, 0.0)
    return r * r, r


def _mlp_fwd(x, g, w_up, w_down, tag):
    h = _rms_fwd(x, g, f"mlp{tag}_norm")
    a, r = _mm(h, w_up, mode="nn", b_shard=True, out_dtypes=(BF16, BF16), epi=_relu2, name=f"mlp{tag}_up")
    x_out = _mm(a, w_down, mode="nn", extras=(x,), epi=lambda acc, res: (acc + res,), name=f"mlp{tag}_down")
    return x_out, (h, a, r)


def _mlp_bwd(x, g, w_up, w_down, saved, dx, dx_bf, tag):
    h, a, r = saved
    d_down = _mm(a, dx_bf, mode="tn", out_dtypes=(BF16,), name=f"mlp{tag}_dwdown")
    dup = _mm(dx_bf, w_down, mode="nt", extras=(r,), out_dtypes=(BF16,),
              epi=lambda acc, rv: (acc * (2.0 * rv.astype(F32)),), name=f"mlp{tag}_dup")
    d_up = _mm(h, dup, mode="tn", o_shard=True, out_dtypes=(BF16,), name=f"mlp{tag}_dwup")
    dh = _mm(dup, w_up, mode="nt", b_shard=True, name=f"mlp{tag}_dh")
    dx_new, dx_new_bf, dg = _rms_bwd(x, dh, dx, g, f"mlp{tag}_norm_bwd")
    return dx_new, dx_new_bf, dg, d_up, d_down


def _rope_tables(L, B):
    rows = L // GRID_W
    row = jnp.repeat(jnp.arange(rows, dtype=F32), GRID_W)
    col = jnp.tile(jnp.arange(GRID_W, dtype=F32), rows)
    inv = ROPE_THETA ** (-jnp.arange(HEAD_DIM // 4, dtype=F32) / (HEAD_DIM // 4))
    ar, ac = row[:, None] * inv, col[:, None] * inv
    cos = jnp.concatenate([jnp.cos(ar), jnp.cos(ar), jnp.cos(ac), jnp.cos(ac)], axis=-1)
    sin = jnp.concatenate([-jnp.sin(ar), jnp.sin(ar), -jnp.sin(ac), jnp.sin(ac)], axis=-1)
    return jnp.tile(cos, (B, 1)), jnp.tile(sin, (B, 1))


def _swap_halves(x):
    lane = lax.broadcasted_iota(jnp.int32, x.shape, 1)
    return jnp.where((lane % 64) < 32, pltpu.roll(x, HEAD_DIM - 32, 1), pltpu.roll(x, 32, 1))


def _qk_prep(qkv, cos, sin, q_g, k_g):
    def fn(ins, bs, outs, accs):
        c, s = ins[1][...], ins[2][...]
        for h in range(N_HEADS + N_KV):
            xv = ins[0][:, h * HEAD_DIM:(h + 1) * HEAD_DIM]
            g = bs[0][...] if h < N_HEADS else bs[1][...]
            r = lax.rsqrt(jnp.mean(xv * xv, axis=-1, keepdims=True) + EPS)
            z = xv * r * g
            y = (z * c + _swap_halves(z) * s).astype(BF16)
            if h < N_HEADS:
                outs[0][:, h * HEAD_DIM:(h + 1) * HEAD_DIM] = y
            else:
                outs[1][:, (h - N_HEADS) * HEAD_DIM:(h - N_HEADS + 1) * HEAD_DIM] = y
        outs[2][...] = ins[0][:, (N_HEADS + N_KV) * HEAD_DIM:].astype(BF16)

    kvw = N_KV * HEAD_DIM
    return _rowwise(fn, [qkv, cos, sin], [q_g, k_g], [(D_MODEL, BF16), (kvw, BF16), (kvw, BF16)], tm=512,
                    name="attn_qk_prep")


def _qk_prep_bwd(qkv, dq, dk, dv, cos, sin, q_g, k_g):
    def fn(ins, bs, outs, accs):
        c, s = ins[4][...], ins[5][...]
        for h in range(N_HEADS + N_KV):
            sl = slice(h * HEAD_DIM, (h + 1) * HEAD_DIM)
            xv = ins[0][:, sl]
            if h < N_HEADS:
                g, dy, acc = bs[0][...], ins[1][:, sl], accs[0]
            else:
                ks = slice((h - N_HEADS) * HEAD_DIM, (h - N_HEADS + 1) * HEAD_DIM)
                g, dy, acc = bs[1][...], ins[2][:, ks], accs[1]
            r = lax.rsqrt(jnp.mean(xv * xv, axis=-1, keepdims=True) + EPS)
            xn = xv * r
            dz = dy * c - _swap_halves(dy) * s
            acc[...] += _rsum(dz * xn)
            dxn = dz * g
            outs[0][:, sl] = (r * (dxn - xn * jnp.mean(dxn * xn, axis=-1, keepdims=True))).astype(BF16)
        outs[0][:, (N_HEADS + N_KV) * HEAD_DIM:] = ins[3][...].astype(BF16)

    return _rowwise(fn, [qkv, dq, dk, dv, cos, sin], [q_g, k_g], [(qkv.shape[1], BF16)],
                    [(1, HEAD_DIM), (1, HEAD_DIM)], tm=256, name="attn_qk_prep_bwd")


_NT = (((1,), (1,)), ((), ()))
_TN = (((0,), (0,)), ((), ()))


def _softmax_rows(q, k):
    s = lax.dot_general(q, k, _NT, preferred_element_type=F32) * SCALE
    p = jnp.exp(s - jnp.max(s, axis=-1, keepdims=True))
    return p * (1.0 / jnp.sum(p, axis=-1, keepdims=True))


def _attn_fwd(q, k, v, B, L, tq=512):
    tq = min(tq, L)
    nq = L // tq

    def body(q_ref, k_ref, v_ref, o_ref):
        p = _softmax_rows(q_ref[...], k_ref[...])
        o_ref[...] = jnp.dot(p.astype(BF16), v_ref[...], preferred_element_type=F32).astype(o_ref.dtype)

    return pl.pallas_call(
        body, name="attn_fwd", grid=(B, N_HEADS, nq),
        in_specs=[pl.BlockSpec((tq, HEAD_DIM), lambda b, h, i: (b * nq + i, h)),
                  pl.BlockSpec((L, HEAD_DIM), lambda b, h, i: (b, h // GROUP)),
                  pl.BlockSpec((L, HEAD_DIM), lambda b, h, i: (b, h // GROUP))],
        out_specs=pl.BlockSpec((tq, HEAD_DIM), lambda b, h, i: (b * nq + i, h)),
        out_shape=jax.ShapeDtypeStruct((B * L, D_MODEL), BF16),
        compiler_params=_params(("parallel", "parallel", "parallel")),
    )(q, k, v)


def _attn_bwd(q, k, v, do, B, L, tq=256):
    tq = min(tq, L)
    nq = L // tq

    def body(q_ref, k_ref, v_ref, do_ref, dq_ref, dk_ref, dv_ref):
        @pl.when((pl.program_id(2) == 0) & (pl.program_id(3) == 0))
        def _():
            dk_ref[...] = jnp.zeros(dk_ref.shape, F32)
            dv_ref[...] = jnp.zeros(dv_ref.shape, F32)

        qv, kv, vv, dov = q_ref[...], k_ref[...], v_ref[...], do_ref[...]
        p = _softmax_rows(qv, kv)
        dv_ref[...] += lax.dot_general(p.astype(BF16), dov, _TN, preferred_element_type=F32)
        dp = lax.dot_general(dov, vv, _NT, preferred_element_type=F32)
        ds = (p * (dp - jnp.sum(p * dp, axis=-1, keepdims=True)) * SCALE).astype(BF16)
        dq_ref[...] = jnp.dot(ds, kv, preferred_element_type=F32)
        dk_ref[...] += lax.dot_general(ds, qv, _TN, preferred_element_type=F32)

    qmap = lambda b, kh, g, i: (b * nq + i, kh * GROUP + g)
    kmap = lambda b, kh, g, i: (b, kh)
    kvw = N_KV * HEAD_DIM
    return pl.pallas_call(
        body, name="attn_bwd", grid=(B, N_KV, GROUP, nq),
        in_specs=[pl.BlockSpec((tq, HEAD_DIM), qmap), pl.BlockSpec((L, HEAD_DIM), kmap),
                  pl.BlockSpec((L, HEAD_DIM), kmap), pl.BlockSpec((tq, HEAD_DIM), qmap)],
        out_specs=[pl.BlockSpec((tq, HEAD_DIM), qmap), pl.BlockSpec((L, HEAD_DIM), kmap),
                   pl.BlockSpec((L, HEAD_DIM), kmap)],
        out_shape=[jax.ShapeDtypeStruct((B * L, D_MODEL), F32), jax.ShapeDtypeStruct((B * L, kvw), F32),
                   jax.ShapeDtypeStruct((B * L, kvw), F32)],
        compiler_params=_params(("parallel", "parallel", "arbitrary", "arbitrary")),
    )(q, k, v, do)


def _conv_shift(x, t, L, k):
    if k == 2:
        return x
    if k < 2:
        return jnp.where(t >= 2 - k, pltpu.roll(x, 2 - k, 0), 0.0)
    return jnp.where(t < L - (k - 2), pltpu.roll(x, L - (k - 2), 0), 0.0)


def _conv_fwd(z, wb, B, L, tc=256):
    noff = D_MODEL // tc

    def body(z_ref, w_ref, o_ref):
        x = z_ref[...]
        t = lax.broadcasted_iota(jnp.int32, x.shape, 0)
        acc = w_ref[4:5, :] + w_ref[2:3, :] * x
        for k in (0, 1, 3):
            acc = acc + w_ref[k:k + 1, :] * _conv_shift(x, t, L, k)
        o_ref[...] = acc

    return pl.pallas_call(
        body, name="rg_conv", grid=(B, noff),
        in_specs=[pl.BlockSpec((L, tc), lambda b, j: (b, noff + j)), pl.BlockSpec((SUBLANES, tc), lambda b, j: (0, j))],
        out_specs=pl.BlockSpec((L, tc), lambda b, j: (b, j)),
        out_shape=jax.ShapeDtypeStruct((B * L, D_MODEL), F32),
        compiler_params=_params(("parallel", "parallel")),
    )(z, wb)


def _conv_bwd(z, g, wb, B, L, tc=256):
    noff = D_MODEL // tc

    def body(z_ref, g_ref, w_ref, dx_ref, dw_ref):
        @pl.when(pl.program_id(1) == 0)
        def _():
            dw_ref[...] = jnp.zeros(dw_ref.shape, F32)

        x, gv = z_ref[...], g_ref[...]
        t = lax.broadcasted_iota(jnp.int32, x.shape, 0)
        dx = w_ref[2:3, :] * gv
        for k in (0, 1, 3):
            dx = dx + w_ref[k:k + 1, :] * _conv_shift(gv, t, L, 4 - k)
        dx_ref[...] = dx.astype(BF16)
        for k in range(4):
            dw_ref[k:k + 1, :] += _rsum(_conv_shift(x, t, L, k) * gv)
        dw_ref[4:5, :] += _rsum(gv)

    return pl.pallas_call(
        body, name="rg_conv_bwd", grid=(noff, B),
        in_specs=[pl.BlockSpec((L, tc), lambda j, b: (b, noff + j)), pl.BlockSpec((L, tc), lambda j, b: (b, j)),
                  pl.BlockSpec((SUBLANES, tc), lambda j, b: (0, j))],
        out_specs=[pl.BlockSpec((L, tc), lambda j, b: (b, j)), pl.BlockSpec((SUBLANES, tc), lambda j, b: (0, j))],
        out_shape=[jax.ShapeDtypeStruct((B * L, D_MODEL), BF16), jax.ShapeDtypeStruct((SUBLANES, D_MODEL), F32)],
        compiler_params=_params(("parallel", "arbitrary")),
    )(z, g, wb)


def _softplus(x):
    return jnp.maximum(x, 0.0) + jnp.log1p(jnp.exp(-jnp.abs(x)))


def _gate_math(xb, pre, bias_ref, lam_ref, d, sl):
    pa = pre[:, (2 * d) * LRU_BW:(2 * d + 1) * LRU_BW] + bias_ref[2 * d:2 * d + 1, sl]
    px = pre[:, (2 * d + 1) * LRU_BW:(2 * d + 2) * LRU_BW] + bias_ref[2 * d + 1:2 * d + 2, sl]
    r = jax.nn.sigmoid(pa)
    i = jax.nn.sigmoid(px)
    sp = _softplus(-lam_ref[d:d + 1, sl])
    log_a = (-RG_C) * r * sp
    a = jnp.exp(log_a)
    th = jnp.tanh(log_a)
    om = -2.0 * th / (1.0 - th)
    mult = jnp.sqrt(om)
    return a, mult * (i * xb), (r, i, sp, om, mult)


def _gate_fwd(rec, wcat, bias, lam):
    def fn(ins, bs, outs, accs):
        for blk in range(LRU_BLOCKS):
            sl = slice(blk * LRU_BW, (blk + 1) * LRU_BW)
            xb = ins[0][:, sl]
            pre = jnp.dot(xb.astype(BF16), bs[0][sl, :], preferred_element_type=F32)
            for d in range(2):
                a, u, _ = _gate_math(xb, pre, bs[1], bs[2], d, sl)
                outs[2 * d][:, sl] = a
                outs[2 * d + 1][:, sl] = u

    return _rowwise(fn, [rec], [wcat, bias, lam], [(D_MODEL, F32)] * 4, tm=256, name="rg_gate")


def _gate_bwd(rec, du_f, da_f, du_b, da_b, wcat, bias, lam):
    def fn(ins, bs, outs, accs):
        for blk in range(LRU_BLOCKS):
            sl = slice(blk * LRU_BW, (blk + 1) * LRU_BW)
            xb = ins[0][:, sl]
            xb16 = xb.astype(BF16)
            w = bs[0][sl, :]
            pre = jnp.dot(xb16, w, preferred_element_type=F32)
            dx = jnp.zeros_like(xb)
            dpre = []
            for d in range(2):
                a, _, (r, i, sp, om, mult) = _gate_math(xb, pre, bs[1], bs[2], d, sl)
                du, da = ins[1 + 2 * d][:, sl], ins[2 + 2 * d][:, sl]
                d_i = du * mult * xb
                d_mult = du * i * xb
                dx = dx + du * mult * i
                dlog = da * a - d_mult * (1.0 - om) / mult
                d_r = dlog * ((-RG_C) * sp)
                d_sp = _rsum(dlog * ((-RG_C) * r))
                accs[2][d:d + 1, sl] += d_sp * (-jax.nn.sigmoid(-bs[2][d:d + 1, sl]))
                dpa = d_r * r * (1.0 - r)
                dpx = d_i * i * (1.0 - i)
                accs[1][2 * d:2 * d + 1, sl] += _rsum(dpa)
                accs[1][2 * d + 1:2 * d + 2, sl] += _rsum(dpx)
                dpre += [dpa, dpx]
            dpre = jnp.concatenate(dpre, axis=1).astype(BF16)
            accs[0][sl, :] += lax.dot_general(xb16, dpre, _TN, preferred_element_type=F32)
            outs[0][:, sl] = dx + lax.dot_general(dpre, w, _NT, preferred_element_type=F32)

    return _rowwise(fn, [rec, du_f, da_f, du_b, da_b], [wcat, bias, lam], [(D_MODEL, F32)],
                    [(D_MODEL, 4 * LRU_BW), (SUBLANES, D_MODEL), (SUBLANES, D_MODEL)], tm=256, name="rg_gate_bwd")


def _as_time_blocks(x):
    return x.reshape(x.shape[0] // SUBLANES, SUBLANES, x.shape[1])


def _scan_call(body, ins, n_out, B, L, tc, name):
    nb = L // SUBLANES
    spec = pl.BlockSpec((nb, SUBLANES, tc), lambda b, j: (b, 0, j))
    T = ins[0].shape[0]
    outs = pl.pallas_call(
        functools.partial(body, nb), name=name, grid=(B, D_MODEL // tc),
        in_specs=[spec] * len(ins), out_specs=[spec] * n_out,
        out_shape=[jax.ShapeDtypeStruct((T // SUBLANES, SUBLANES, D_MODEL), F32)] * n_out,
        compiler_params=_params(("parallel", "parallel")),
    )(*[_as_time_blocks(x) for x in ins])
    return [o.reshape(T, D_MODEL) for o in outs]


def _scan_fwd(a_f, u_f, a_b, u_b, B, L, tc=256):
    def body(nb, af, uf, ab, ub, hf, hb):
        def step(i, carry):
            h1, h2 = carry
            ib = nb - 1 - i
            for j in range(SUBLANES):
                jb = SUBLANES - 1 - j
                h1 = af[i, j:j + 1, :] * h1 + uf[i, j:j + 1, :]
                hf[i, j:j + 1, :] = h1
                h2 = ab[ib, jb:jb + 1, :] * h2 + ub[ib, jb:jb + 1, :]
                hb[ib, jb:jb + 1, :] = h2
            return h1, h2

        zero = jnp.zeros((1, tc), F32)
        lax.fori_loop(0, nb, step, (zero, zero))

    return _scan_call(body, [a_f, u_f, a_b, u_b], 2, B, L, tc, "rg_scan")


def _scan_bwd(dy, a_f, h_f, a_b, h_b, B, L, tc=256):
    def body(nb, dy_r, af, hf, ab, hb, duf, daf, dub, dab):
        def step(i, carry):
            c1, c2 = carry
            ir = nb - 1 - i
            for j in range(SUBLANES):
                jr = SUBLANES - 1 - j
                lam1 = dy_r[ir, jr:jr + 1, :] + c1
                if jr > 0:
                    prev = hf[ir, jr - 1:jr, :]
                else:
                    prev = hf[jnp.maximum(ir - 1, 0), SUBLANES - 1:SUBLANES, :] * (ir > 0).astype(F32)
                duf[ir, jr:jr + 1, :] = lam1
                daf[ir, jr:jr + 1, :] = lam1 * prev
                c1 = af[ir, jr:jr + 1, :] * lam1
                lam2 = dy_r[i, j:j + 1, :] + c2
                if j < SUBLANES - 1:
                    nxt = hb[i, j + 1:j + 2, :]
                else:
                    nxt = hb[jnp.minimum(i + 1, nb - 1), 0:1, :] * (i < nb - 1).astype(F32)
                dub[i, j:j + 1, :] = lam2
                dab[i, j:j + 1, :] = lam2 * nxt
                c2 = ab[i, j:j + 1, :] * lam2
            return c1, c2

        zero = jnp.zeros((1, tc), F32)
        lax.fori_loop(0, nb, step, (zero, zero))

    return _scan_call(body, [dy, a_f, h_f, a_b, h_b], 4, B, L, tc, "rg_scan_bwd")


_GELU_C = math.sqrt(2.0 / math.pi)


def _gelu_parts(x):
    th = jnp.tanh(_GELU_C * (x + 0.044715 * x * x * x))
    return 0.5 * x * (1.0 + th), th


def _gated_out(h_f, h_b, z):
    def fn(ins, bs, outs, accs):
        gl, _ = _gelu_parts(ins[2][...])
        outs[0][...] = ((ins[0][...] + ins[1][...]) * gl).astype(BF16)

    return _rowwise(fn, [h_f, h_b, (z, D_MODEL, 0)], [], [(D_MODEL, BF16)], tm=512, name="rg_gated_out")[0]


def _gated_out_bwd(dyg, h_f, h_b, z):
    def fn(ins, bs, outs, accs):
        x = ins[3][...]
        gl, th = _gelu_parts(x)
        dgl = 0.5 * (1.0 + th) + 0.5 * x * (1.0 - th * th) * (_GELU_C * (1.0 + 3.0 * 0.044715 * x * x))
        g = ins[0][...]
        outs[0][...] = g * gl
        outs[1][...] = (g * (ins[1][...] + ins[2][...]) * dgl).astype(BF16)

    return _rowwise(fn, [dyg, h_f, h_b, (z, D_MODEL, 0)], [], [(D_MODEL, F32), (D_MODEL, BF16)], tm=512,
                    name="rg_gated_out_bwd")


def _make_wcat(w_a, w_x):
    g = jnp.stack([w_a[0, 0], w_x[0, 0], w_a[0, 1], w_x[0, 1]])
    return jnp.transpose(g, (1, 2, 0, 3)).reshape(D_MODEL, 4 * LRU_BW)


def _split_wcat(rows):
    g = jnp.transpose(rows.reshape(LRU_BLOCKS, LRU_BW, 4, LRU_BW), (2, 0, 1, 3))
    return jnp.stack([g[0], g[2]])[None], jnp.stack([g[1], g[3]])[None]


def _local_step(x, target, W, B, L):
    g_mix, g_mlp = W["norm_mix_g"], W["norm_mlp_g"]
    h0 = _rms_fwd(x, g_mix[0:1], "rg_norm")
    z = _mm(h0, W["rg_w_in"], mode="nn", b_shard=True, name="rg_in")
    rec = _conv_fwd(z, W["conv_wb"], B, L)
    a_f, u_f, a_b, u_b = _gate_fwd(rec, W["wcat"], W["gate_bias"], W["lam"])
    h_f, h_b = _scan_fwd(a_f, u_f, a_b, u_b, B, L)
    yg = _gated_out(h_f, h_b, z)
    x1 = _mm(yg, W["rg_w_out"], mode="nn", extras=(x,), epi=lambda acc, res: (acc + res,), name="rg_out")
    x2, mlp0 = _mlp_fwd(x1, g_mlp[0:1], W["up0"], W["down0"], 0)
    h3 = _rms_fwd(x2, g_mix[1:2], "attn_norm")
    qkv = _mm(h3, W["at_w_qkv"], mode="nn", b_shard=True, name="attn_qkv")
    cos, sin = _rope_tables(L, B)
    qh, kh, vh = _qk_prep(qkv, cos, sin, W["q_g"], W["k_g"])
    o = _attn_fwd(qh, kh, vh, B, L)
    x3 = _mm(o, W["at_w_o"], mode="nn", extras=(x2,), epi=lambda acc, res: (acc + res,), name="attn_out")
    x4, mlp1 = _mlp_fwd(x3, g_mlp[1:2], W["up1"], W["down1"], 1)
    dx4, dx4_bf, loss_acc, d_final_g = _final_loss(x4, target, W["final_g"])

    dx3, dx3_bf, dg_mlp1, d_up1, d_down1 = _mlp_bwd(x3, g_mlp[1:2], W["up1"], W["down1"], mlp1, dx4, dx4_bf, 1)
    d_wo = _mm(o, dx3_bf, mode="tn", out_dtypes=(BF16,), name="attn_dwo")
    do = _mm(dx3_bf, W["at_w_o"], mode="nt", out_dtypes=(BF16,), name="attn_do")
    dq, dk, dv = _attn_bwd(qh, kh, vh, do, B, L)
    dqkv, dq_g, dk_g = _qk_prep_bwd(qkv, dq, dk, dv, cos, sin, W["q_g"], W["k_g"])
    d_wqkv = _mm(h3, dqkv, mode="tn", o_shard=True, out_dtypes=(BF16,), name="attn_dwqkv")
    dh3 = _mm(dqkv, W["at_w_qkv"], mode="nt", b_shard=True, name="attn_dh")
    dx2, dx2_bf, dg_mix1 = _rms_bwd(x2, dh3, dx3, g_mix[1:2], "attn_norm_bwd")
    dx1, dx1_bf, dg_mlp0, d_up0, d_down0 = _mlp_bwd(x1, g_mlp[0:1], W["up0"], W["down0"], mlp0, dx2, dx2_bf, 0)
    d_wout = _mm(yg, dx1_bf, mode="tn", out_dtypes=(BF16,), name="rg_dwout")
    dyg = _mm(dx1_bf, W["rg_w_out"], mode="nt", name="rg_dyg")
    dy, dgate = _gated_out_bwd(dyg, h_f, h_b, z)
    du_f, da_f, du_b, da_b = _scan_bwd(dy, a_f, h_f, a_b, h_b, B, L)
    drec_c, d_wcat, d_bias, d_lam = _gate_bwd(rec, du_f, da_f, du_b, da_b, W["wcat"], W["gate_bias"], W["lam"])
    drec, d_convwb = _conv_bwd(z, drec_c, W["conv_wb"], B, L)
    dz = jnp.concatenate([dgate, drec], axis=1)
    d_win = _mm(h0, dz, mode="tn", o_shard=True, out_dtypes=(BF16,), name="rg_dwin")
    dh0 = _mm(dz, W["rg_w_in"], mode="nt", b_shard=True, name="rg_dh")
    grad_x, _, dg_mix0 = _rms_bwd(x, dh0, dx1, g_mix[0:1], "rg_norm_bwd")

    big = dict(rg_w_in=d_win, rg_w_out=d_wout, at_w_qkv=d_wqkv, at_w_o=d_wo,
               up0=d_up0, up1=d_up1, down0=d_down0, down1=d_down1)
    qk_row = jnp.concatenate([dq_g, dk_g, jnp.zeros((1, D_MODEL - 2 * HEAD_DIM), F32)], axis=1)
    small = jnp.concatenate([
        dg_mix0, dg_mix1, dg_mlp0, dg_mlp1,
        d_convwb[4:5], d_final_g,
        d_convwb[0:4],
        d_bias[0:1], d_bias[2:3],
        d_bias[1:2], d_bias[3:4],
        d_lam[0:2],
        qk_row,
        jnp.zeros((SMALL_ROWS - 17, D_MODEL), F32),
        d_wcat.reshape(GATE_ROWS, D_MODEL),
    ], axis=0)
    return loss_acc[0, 0], grad_x, big, small


_MESH = pl.DeviceIdType.MESH
_ANY = pl.BlockSpec(memory_space=pl.ANY)


def _place():
    x, y, c = lax.axis_index("x"), lax.axis_index("y"), lax.axis_index("c")
    peers = [((1 - x) if j & 2 else x, (1 - y) if j & 1 else y) for j in (1, 2, 3)]
    return x, y, c, peers


def _comm_call(body, ins, out_shapes, n_sem, name):
    return pl.pallas_call(
        body, name=name, in_specs=[_ANY] * len(ins), out_specs=[_ANY] * len(out_shapes), out_shape=out_shapes,
        scratch_shapes=[pltpu.SemaphoreType.DMA((n_sem,)), pltpu.SemaphoreType.DMA((n_sem,)),
                        pltpu.SemaphoreType.DMA((len(ins),))],
    )(*ins)


def _all_gather_chips(shards, name):
    n = len(shards)

    def body(*refs):
        ins, outs = refs[:n], refs[n:2 * n]
        send, recv, lsem = refs[2 * n:]
        x, y, c, peers = _place()
        me = 2 * x + y

        def copy(a, j, slot):
            px, py = peers[j]
            return pltpu.make_async_remote_copy(
                src_ref=ins[a], dst_ref=outs[a].at[slot], send_sem=send.at[3 * a + j], recv_sem=recv.at[3 * a + j],
                device_id=(px, py, c), device_id_type=_MESH)

        local = [pltpu.make_async_copy(ins[a], outs[a].at[me], lsem.at[a]) for a in range(n)]
        sends = [copy(a, j, me) for a in range(n) for j in range(3)]
        for cp in local + sends:
            cp.start()
        for a in range(n):
            for j, (px, py) in enumerate(peers):
                copy(a, j, 2 * px + py).wait_recv()
        for cp in sends:
            cp.wait_send()
        for cp in local:
            cp.wait()

    shapes = [jax.ShapeDtypeStruct((N_CHIPS,) + s.shape, s.dtype) for s in shards]
    return _comm_call(body, shards, shapes, 3 * n, name)


def _scatter_to_owner(grads, name):
    n = len(grads)

    def body(*refs):
        ins, outs = refs[:n], refs[n:2 * n]
        send, recv, lsem = refs[2 * n:]
        x, y, c, peers = _place()
        me = 2 * x + y

        def copy(a, j):
            px, py = peers[j]
            return pltpu.make_async_remote_copy(
                src_ref=ins[a].at[2 * px + py], dst_ref=outs[a].at[j], send_sem=send.at[3 * a + j],
                recv_sem=recv.at[3 * a + j], device_id=(px, py, c), device_id_type=_MESH)

        local = [pltpu.make_async_copy(ins[a].at[me], outs[a].at[3], lsem.at[a]) for a in range(n)]
        sends = [copy(a, j) for a in range(n) for j in range(3)]
        for cp in local + sends:
            cp.start()
        for cp in sends:
            cp.wait_recv()
        for cp in sends:
            cp.wait_send()
        for cp in local:
            cp.wait()

    shapes = [jax.ShapeDtypeStruct(g.shape, g.dtype) for g in grads]
    return _comm_call(body, grads, shapes, 3 * n, name)


def _swap_cores(parts, name):
    n = len(parts)

    def body(*refs):
        ins, outs = refs[:n], refs[n:2 * n]
        send, recv, _ = refs[2 * n:]
        x, y, c, _peers = _place()
        copies = [pltpu.make_async_remote_copy(
            src_ref=ins[a], dst_ref=outs[a], send_sem=send.at[a], recv_sem=recv.at[a],
            device_id=(x, y, 1 - c), device_id_type=_MESH) for a in range(n)]
        for cp in copies:
            cp.start()
        for cp in copies:
            cp.wait_recv()
        for cp in copies:
            cp.wait_send()

    shapes = [jax.ShapeDtypeStruct(p.shape, p.dtype) for p in parts]
    return _comm_call(body, parts, shapes, n, name)


def _sum_slots(r, name):
    _, rows, cols = r.shape
    tm = min(512, rows)

    def body(r_ref, o_ref):
        o_ref[...] = ((r_ref[3].astype(F32) + r_ref[0].astype(F32)) + r_ref[1].astype(F32)) + r_ref[2].astype(F32)

    return pl.pallas_call(
        body, name=name, grid=(rows // tm,),
        in_specs=[pl.BlockSpec((N_CHIPS, tm, cols), lambda i: (0, i, 0))],
        out_specs=pl.BlockSpec((tm, cols), lambda i: (i, 0)),
        out_shape=jax.ShapeDtypeStruct((rows, cols), F32), compiler_params=_params(("parallel",)),
    )(r)


def _add(p, q, name):
    def fn(ins, bs, outs, accs):
        outs[0][...] = ins[0][...] + ins[1][...]

    return _rowwise(fn, [p, q], [], [(p.shape[1], F32)], tm=512, name=name)[0]


def _adamw(w, m, v, p, q, name):
    def fn(ins, bs, outs, accs):
        g = ins[3][...] if q is None else ins[3][...] + ins[4][...]
        m1 = ADAM_B1 * ins[1][...] + (1.0 - ADAM_B1) * g
        v1 = ADAM_B2 * ins[2][...] + (1.0 - ADAM_B2) * (g * g)
        m_hat = m1 / (1.0 - ADAM_B1 ** ADAM_STEP)
        v_hat = v1 / (1.0 - ADAM_B2 ** ADAM_STEP)
        outs[0][...] = g
        outs[1][...] = (-ADAM_LR) * (m_hat / (jnp.sqrt(v_hat) + ADAM_EPS) + ADAM_WD * ins[0][...])
        outs[2][...] = m1
        outs[3][...] = v1

    rows = [w, m, v, p] + ([] if q is None else [q])
    return _rowwise(fn, rows, [], [(w.shape[1], F32)] * 4, tm=256, name=name)


def _put_cols(shard, me):
    full = jnp.zeros((shard.shape[0], D_MODEL), F32)
    return lax.dynamic_update_slice(full, shard, (0, me * (D_MODEL // N_CHIPS)))


def _pack_small(p, me):
    qk = jnp.concatenate([p["at_q_g"], p["at_k_g"], jnp.zeros((1, D_MODEL - 2 * HEAD_DIM), F32)], axis=1)
    return jnp.concatenate([
        p["norm_mix_g"], p["norm_mlp_g"], p["rg_conv_b"], p["final_g"][None],
        _put_cols(p["rg_conv_w"][0, :, 0, :], me), _put_cols(p["rg_b_a"][0], me), _put_cols(p["rg_b_x"][0], me),
        _put_cols(p["rg_lam"][0], me), qk, jnp.zeros((SMALL_ROWS - 17, D_MODEL), F32),
        _make_wcat(p["rg_w_a"], p["rg_w_x"]).reshape(GATE_ROWS, D_MODEL),
    ], axis=0)


def _unpack_small(r, me):
    def cols(rows):
        return lax.dynamic_slice(rows, (0, me * (D_MODEL // N_CHIPS)), (rows.shape[0], D_MODEL // N_CHIPS))

    w_a, w_x = _split_wcat(r[SMALL_ROWS:])
    return dict(
        norm_mix_g=r[0:2], norm_mlp_g=r[2:4], rg_conv_b=r[4:5], final_g=r[5],
        rg_conv_w=cols(r[6:10])[None, :, None, :], rg_b_a=cols(r[10:12])[None], rg_b_x=cols(r[12:14])[None],
        rg_lam=cols(r[14:16])[None], at_q_g=r[16:17, 0:HEAD_DIM], at_k_g=r[16:17, HEAD_DIM:2 * HEAD_DIM],
        rg_w_a=w_a, rg_w_x=w_x)


_WEIGHTS = ['norm_mix_g', 'norm_mlp_g', 'rg_w_in', 'rg_conv_w', 'rg_conv_b', 'rg_w_a', 'rg_b_a', 'rg_w_x', 'rg_b_x',
            'rg_lam', 'rg_w_out', 'at_w_qkv', 'at_q_g', 'at_k_g', 'at_w_o', 'mlp_w_up', 'mlp_w_down', 'final_g']
_BIG = ['rg_w_in', 'rg_w_out', 'at_w_qkv', 'at_w_o', 'mlp_w_up', 'mlp_w_down']


def kernel(x, *args):
    n_w = len(_WEIGHTS)
    w = dict(zip(_WEIGHTS, args[:n_w]))
    target = args[n_w]
    m = dict(zip(_WEIGHTS, args[n_w + 1:2 * n_w + 1]))
    v = dict(zip(_WEIGHTS, args[2 * n_w + 1:3 * n_w + 1]))
    B, L, _ = x.shape
    T = B * L
    me = 2 * lax.axis_index("x") + lax.axis_index("y")

    vec = jnp.concatenate([w["rg_b_a"][0], w["rg_b_x"][0], w["rg_lam"][0], w["rg_conv_w"][0, :, 0, :],
                           jnp.zeros((6, D_MODEL // N_CHIPS), F32)], axis=0)
    shards = [w["rg_w_in"][0], w["rg_w_out"][0], w["at_w_qkv"][0], w["at_w_o"][0],
              w["mlp_w_up"][0], w["mlp_w_up"][1], w["mlp_w_down"][0], w["mlp_w_down"][1]]
    g_in, g_out, g_qkv, g_o, g_up0, g_up1, g_dn0, g_dn1, g_vec = _all_gather_chips(
        [s.astype(BF16) for s in shards] + [vec], "gather_weights")
    vec_full = jnp.transpose(g_vec, (1, 0, 2)).reshape(16, D_MODEL)
    W = dict(
        norm_mix_g=w["norm_mix_g"], norm_mlp_g=w["norm_mlp_g"], final_g=w["final_g"][None],
        rg_w_in=g_in, rg_w_out=g_out.reshape(D_MODEL, D_MODEL), at_w_qkv=g_qkv, at_w_o=g_o.reshape(D_MODEL, D_MODEL),
        up0=g_up0, up1=g_up1, down0=g_dn0.reshape(4 * D_MODEL, D_MODEL), down1=g_dn1.reshape(4 * D_MODEL, D_MODEL),
        q_g=w["at_q_g"], k_g=w["at_k_g"],
        conv_wb=jnp.concatenate([vec_full[6:10], w["rg_conv_b"], jnp.zeros((3, D_MODEL), F32)], axis=0),
        wcat=_make_wcat(w["rg_w_a"], w["rg_w_x"]).astype(BF16),
        gate_bias=jnp.stack([vec_full[0], vec_full[2], vec_full[1], vec_full[3]]),
        lam=vec_full[4:6],
    )

    loss_part, grad_x, big, small = _local_step(x.reshape(T, D_MODEL), target.reshape(T, D_MODEL), W, B, L)
    loss = lax.psum(loss_part, ("x", "y", "c"))

    order = ["rg_w_in", "rg_w_out", "at_w_qkv", "at_w_o", "up0", "up1", "down0", "down1"]
    sends = [big[k].reshape(N_CHIPS, -1, big[k].shape[-1]) for k in order]
    sends.append(small.reshape(N_CHIPS, PACK_ROWS // N_CHIPS, D_MODEL))
    recvd = _scatter_to_owner(sends, "scatter_grads")
    parts = [_sum_slots(r, f"sum_{k}") for r, k in zip(recvd, order + ["small"])]
    theirs = _swap_cores(parts, "swap_cores")
    small_piece = _add(parts[-1], theirs[-1], "small_grad")
    small_full = _all_gather_chips([small_piece], "gather_small")[0].reshape(PACK_ROWS, D_MODEL)

    P = dict(zip(order, parts[:-1]))
    Q = dict(zip(order, theirs[:-1]))
    pq = dict(
        rg_w_in=(P["rg_w_in"], Q["rg_w_in"]), rg_w_out=(P["rg_w_out"], Q["rg_w_out"]),
        at_w_qkv=(P["at_w_qkv"], Q["at_w_qkv"]), at_w_o=(P["at_w_o"], Q["at_w_o"]),
        mlp_w_up=(jnp.concatenate([P["up0"], P["up1"]]), jnp.concatenate([Q["up0"], Q["up1"]])),
        mlp_w_down=(jnp.concatenate([P["down0"], P["down1"]]), jnp.concatenate([Q["down0"], Q["down1"]])),
    )
    res = {}
    for k in _BIG:
        shape = w[k].shape
        two_d = lambda a: a.reshape(-1, shape[-1])
        outs = _adamw(two_d(w[k]), two_d(m[k]), two_d(v[k]), pq[k][0], pq[k][1], f"adamw_{k}")
        res[k] = [o.reshape(shape) for o in outs]
    outs = _adamw(_pack_small(w, me), _pack_small(m, me), _pack_small(v, me), small_full, None, "adamw_small")
    unpacked = [_unpack_small(o, me) for o in outs]
    for k in _WEIGHTS:
        if k not in res:
            res[k] = [u[k] for u in unpacked]

    result = [loss, grad_x.reshape(B, L, D_MODEL)]
    for slot in range(4):
        result += [res[k][slot] for k in _WEIGHTS]
    return tuple(result)
```

```python
import functools
import math

import jax
import jax.numpy as jnp
from jax import lax
from jax.experimental import pallas as pl
from jax.experimental.pallas import tpu as pltpu

F32 = jnp.float32
BF16 = jnp.bfloat16

D_MODEL = 1024
HEAD_DIM = 128
N_HEADS = 8
N_KV = 2
GROUP = N_HEADS // N_KV
LRU_BLOCKS = 8
LRU_BW = 128
GRID_W = 64
ROPE_THETA = 10000.0
EPS = 1e-6
RG_C = 8.0
SCALE = 1.0 / math.sqrt(HEAD_DIM)
N_CHIPS = 4

ADAM_LR = 0.001
ADAM_B1 = 0.9
ADAM_B2 = 0.999
ADAM_EPS = 1e-08
ADAM_WD = 0.01
ADAM_STEP = 10

V7X_VMEM_BYTES = 64 * 1024 * 1024
VMEM_LIMIT = V7X_VMEM_BYTES * 3 // 4
LANES = 128
SUBLANES = 8

SMALL_ROWS = 32
GATE_ROWS = 512
PACK_ROWS = SMALL_ROWS + GATE_ROWS


def _params(sem):
    return pltpu.CompilerParams(dimension_semantics=sem, vmem_limit_bytes=VMEM_LIMIT)


_ANY = pl.BlockSpec(memory_space=pl.ANY)


def _after_operand(after):
    return [] if after is None else [after]


def _fit(t, n):
    if n <= t:
        return n
    c = (t // LANES) * LANES
    while n % c:
        c -= LANES
    return c


def _mm(a, b, *, mode, name, out_dtypes=(F32,), b_shard=False, o_shard=False, extras=(), epi=None,
        tm=1024, tn=1024, tk=1024, after=None):
    if mode == "tn":
        K, M = a.shape
        N = b.shape[1]
    else:
        M, K = a.shape
        if mode == "nn":
            N = b.shape[0] * b.shape[2] if b_shard else b.shape[1]
        else:
            N = b.shape[1] if b_shard else b.shape[0]
    tm = _fit(tm, M)
    if b_shard and mode == "nn":
        ns = b.shape[2]
        tn = _fit(tn, ns)
    elif o_shard:
        ns = N // N_CHIPS
        tn = _fit(tn, ns)
    else:
        tn = _fit(tn, N)
    if b_shard and mode == "nt":
        ks = b.shape[2]
        tk = _fit(tk, ks)
    else:
        tk = _fit(tk, K)
    nk = K // tk
    grid = (M // tm, N // tn, nk)

    if mode == "tn":
        a_spec = pl.BlockSpec((tk, tm), lambda i, j, k: (k, i))
        b_spec = pl.BlockSpec((tk, tn), lambda i, j, k: (k, j))
        dims = (((0,), (0,)), ((), ()))
    elif mode == "nn":
        a_spec = pl.BlockSpec((tm, tk), lambda i, j, k: (i, k))
        if b_shard:
            q = ns // tn
            b_spec = pl.BlockSpec((None, tk, tn), lambda i, j, k: (j // q, k, j % q))
        else:
            b_spec = pl.BlockSpec((tk, tn), lambda i, j, k: (k, j))
        dims = (((1,), (0,)), ((), ()))
    else:
        a_spec = pl.BlockSpec((tm, tk), lambda i, j, k: (i, k))
        if b_shard:
            q = ks // tk
            b_spec = pl.BlockSpec((None, tn, tk), lambda i, j, k: (k // q, j, k % q))
        else:
            b_spec = pl.BlockSpec((tn, tk), lambda i, j, k: (j, k))
        dims = (((1,), (1,)), ((), ()))

    if o_shard:
        qo = ns // tn
        o_specs = [pl.BlockSpec((None, tm, tn), lambda i, j, k: (j // qo, i, j % qo))]
        o_shapes = [jax.ShapeDtypeStruct((N_CHIPS, M, ns), out_dtypes[0])]
    else:
        o_specs = [pl.BlockSpec((tm, tn), lambda i, j, k: (i, j)) for _ in out_dtypes]
        o_shapes = [jax.ShapeDtypeStruct((M, N), dt) for dt in out_dtypes]
    e_specs = [pl.BlockSpec((tm, tn), lambda i, j, k: (i, j)) for _ in extras]
    n_e, n_o = len(extras), len(out_dtypes)
    order = _after_operand(after)
    n_x = len(order)
    if epi is None:
        epi = lambda acc: (acc,)

    def body(a_ref, b_ref, *rest):
        e_refs, o_refs = rest[:n_e], rest[n_e + n_x:n_e + n_x + n_o]

        def finish(acc):
            outs = epi(acc, *[r[...] for r in e_refs])
            for r, o in zip(o_refs, outs):
                r[...] = o.astype(r.dtype)

        part = lax.dot_general(a_ref[...], b_ref[...], dims, preferred_element_type=F32)
        if nk == 1:
            finish(part)
        else:
            acc_ref = rest[n_e + n_x + n_o]
            k = pl.program_id(2)

            @pl.when(k == 0)
            def _():
                acc_ref[...] = part

            @pl.when(k > 0)
            def _():
                acc_ref[...] += part

            @pl.when(k == nk - 1)
            def _():
                finish(acc_ref[...])

    scratch = [] if nk == 1 else [pltpu.VMEM((tm, tn), F32)]
    outs = pl.pallas_call(
        body, name=name, grid=grid, in_specs=[a_spec, b_spec] + e_specs + [_ANY] * n_x, out_specs=o_specs,
        out_shape=o_shapes, scratch_shapes=scratch,
        compiler_params=_params(("parallel", "parallel", "arbitrary")),
    )(a, b, *extras, *order)
    return outs[0] if n_o == 1 else outs


def _rowwise(fn, rows, bcast, outs, accs=(), *, tm, name, after=None):
    def norm(r):
        return r if isinstance(r, tuple) else (r, r.shape[1], 0)

    rows = [norm(r) for r in rows]
    T = rows[0][0].shape[0]
    tm = min(tm, T)
    while T % tm:
        tm -= SUBLANES
    n_r, n_b, n_o, n_a = len(rows), len(bcast), len(outs), len(accs)
    order = _after_operand(after)
    n_x = len(order)
    in_specs = [pl.BlockSpec((tm, c), functools.partial(lambda i, cb: (i, cb), cb=cb)) for _, c, cb in rows]
    in_specs += [pl.BlockSpec(b.shape, lambda i: (0, 0)) for b in bcast] + [_ANY] * n_x
    out_specs = [pl.BlockSpec((tm, c), lambda i: (i, 0)) for c, _ in outs]
    out_specs += [pl.BlockSpec(s, lambda i: (0, 0)) for s in accs]
    out_shape = [jax.ShapeDtypeStruct((T, c), dt) for c, dt in outs]
    out_shape += [jax.ShapeDtypeStruct(s, F32) for s in accs]

    def body(*refs):
        in_refs = refs[:n_r]
        b_refs = refs[n_r:n_r + n_b]
        o_refs = refs[n_r + n_b + n_x:n_r + n_b + n_x + n_o]
        a_refs = refs[n_r + n_b + n_x + n_o:]
        if n_a:
            @pl.when(pl.program_id(0) == 0)
            def _():
                for r in a_refs:
                    r[...] = jnp.zeros(r.shape, F32)
        fn(in_refs, b_refs, o_refs, a_refs)

    res = pl.pallas_call(
        body, name=name, grid=(T // tm,), in_specs=in_specs, out_specs=out_specs, out_shape=out_shape,
        compiler_params=_params(("arbitrary",) if n_a else ("parallel",)),
    )(*[r[0] for r in rows], *bcast, *order)
    return res


def _rsum(x):
    return jnp.sum(x, axis=0, keepdims=True)


def _rms_fwd(x, g, name, after=None):
    def fn(ins, bs, outs, accs):
        xv = ins[0][...]
        r = lax.rsqrt(jnp.mean(xv * xv, axis=-1, keepdims=True) + EPS)
        outs[0][...] = (xv * r * bs[0][...]).astype(BF16)

    return _rowwise(fn, [x], [g], [(D_MODEL, BF16)], tm=512, name=name, after=after)[0]


def _rms_bwd_math(xv, dh, g):
    r = lax.rsqrt(jnp.mean(xv * xv, axis=-1, keepdims=True) + EPS)
    hn = xv * r
    dgh = dh * g
    dx = r * (dgh - hn * jnp.mean(dgh * hn, axis=-1, keepdims=True))
    return dx, _rsum(dh * hn)


def _rms_bwd(x, dh, dres, g, name):
    def fn(ins, bs, outs, accs):
        dx, dg = _rms_bwd_math(ins[0][...], ins[1][...], bs[0][...])
        dx = dx + ins[2][...]
        outs[0][...] = dx
        outs[1][...] = dx.astype(BF16)
        accs[0][...] += dg

    return _rowwise(fn, [x, dh, dres], [g], [(D_MODEL, F32), (D_MODEL, BF16)], [(1, D_MODEL)], tm=256, name=name)


def _final_loss(x, target, g):
    def fn(ins, bs, outs, accs):
        xv = ins[0][...]
        gv = bs[0][...]
        r = lax.rsqrt(jnp.mean(xv * xv, axis=-1, keepdims=True) + EPS)
        e = xv * r * gv - ins[1][...]
        tok = jnp.mean(e * e, axis=-1, keepdims=True)
        accs[0][...] += 0.5 * jnp.sum(tok, axis=0, keepdims=True) * jnp.ones((1, LANES), F32)
        dx, dg = _rms_bwd_math(xv, e * (1.0 / D_MODEL), gv)
        outs[0][...] = dx
        outs[1][...] = dx.astype(BF16)
        accs[1][...] += dg

    return _rowwise(fn, [x, target], [g], [(D_MODEL, F32), (D_MODEL, BF16)], [(1, LANES), (1, D_MODEL)],
                    tm=256, name="final_loss")


def _relu2(acc):
    r = jnp.maximum(acc, 0.0)
    return r * r, r


def _mlp_fwd(x, g, fetch, tag):
    h = _rms_fwd(x, g, f"mlp{tag}_norm")
    w_up, w_down = fetch(f"mlp{tag}", h)
    a, r = _mm(h, w_up, mode="nn", b_shard=True, out_dtypes=(BF16, BF16), epi=_relu2, name=f"mlp{tag}_up")
    x_out = _mm(a, w_down, mode="nn", extras=(x,), epi=lambda acc, res: (acc + res,), name=f"mlp{tag}_down")
    return x_out, (h, a, r, w_up, w_down)


def _mlp_bwd(x, g, saved, dx, dx_bf, tag, after):
    h, a, r, w_up, w_down = saved
    d_down = _mm(a, dx_bf, mode="tn", out_dtypes=(BF16,), name=f"mlp{tag}_dwdown", after=after)
    dup = _mm(dx_bf, w_down, mode="nt", extras=(r,), out_dtypes=(BF16,),
              epi=lambda acc, rv: (acc * (2.0 * rv.astype(F32)),), name=f"mlp{tag}_dup")
    d_up = _mm(h, dup, mode="tn", o_shard=True, out_dtypes=(BF16,), name=f"mlp{tag}_dwup")
    dh = _mm(dup, w_up, mode="nt", b_shard=True, name=f"mlp{tag}_dh")
    dx_new, dx_new_bf, dg = _rms_bwd(x, dh, dx, g, f"mlp{tag}_norm_bwd")
    return dx_new, dx_new_bf, dg, d_up, d_down


def _rope_tables(L, B):
    rows = L // GRID_W
    row = jnp.repeat(jnp.arange(rows, dtype=F32), GRID_W)
    col = jnp.tile(jnp.arange(GRID_W, dtype=F32), rows)
    inv = ROPE_THETA ** (-jnp.arange(HEAD_DIM // 4, dtype=F32) / (HEAD_DIM // 4))
    ar, ac = row[:, None] * inv, col[:, None] * inv
    cos = jnp.concatenate([jnp.cos(ar), jnp.cos(ar), jnp.cos(ac), jnp.cos(ac)], axis=-1)
    sin = jnp.concatenate([-jnp.sin(ar), jnp.sin(ar), -jnp.sin(ac), jnp.sin(ac)], axis=-1)
    return jnp.tile(cos, (B, 1)), jnp.tile(sin, (B, 1))


def _swap_halves(x):
    lane = lax.broadcasted_iota(jnp.int32, x.shape, 1)
    return jnp.where((lane % 64) < 32, pltpu.roll(x, HEAD_DIM - 32, 1), pltpu.roll(x, 32, 1))


def _qk_prep(qkv, cos, sin, q_g, k_g):
    def fn(ins, bs, outs, accs):
        c, s = ins[1][...], ins[2][...]
        for h in range(N_HEADS + N_KV):
            xv = ins[0][:, h * HEAD_DIM:(h + 1) * HEAD_DIM]
            g = bs[0][...] if h < N_HEADS else bs[1][...]
            r = lax.rsqrt(jnp.mean(xv * xv, axis=-1, keepdims=True) + EPS)
            z = xv * r * g
            y = (z * c + _swap_halves(z) * s).astype(BF16)
            if h < N_HEADS:
                outs[0][:, h * HEAD_DIM:(h + 1) * HEAD_DIM] = y
            else:
                outs[1][:, (h - N_HEADS) * HEAD_DIM:(h - N_HEADS + 1) * HEAD_DIM] = y
        outs[2][...] = ins[0][:, (N_HEADS + N_KV) * HEAD_DIM:].astype(BF16)

    kvw = N_KV * HEAD_DIM
    return _rowwise(fn, [qkv, cos, sin], [q_g, k_g], [(D_MODEL, BF16), (kvw, BF16), (kvw, BF16)], tm=512,
                    name="attn_qk_prep")


def _qk_prep_bwd(qkv, dq, dk, dv, cos, sin, q_g, k_g):
    def fn(ins, bs, outs, accs):
        c, s = ins[4][...], ins[5][...]
        for h in range(N_HEADS + N_KV):
            sl = slice(h * HEAD_DIM, (h + 1) * HEAD_DIM)
            xv = ins[0][:, sl]
            if h < N_HEADS:
                g, dy, acc = bs[0][...], ins[1][:, sl], accs[0]
            else:
                ks = slice((h - N_HEADS) * HEAD_DIM, (h - N_HEADS + 1) * HEAD_DIM)
                g, dy, acc = bs[1][...], ins[2][:, ks], accs[1]
            r = lax.rsqrt(jnp.mean(xv * xv, axis=-1, keepdims=True) + EPS)
            xn = xv * r
            dz = dy * c - _swap_halves(dy) * s
            acc[...] += _rsum(dz * xn)
            dxn = dz * g
            outs[0][:, sl] = (r * (dxn - xn * jnp.mean(dxn * xn, axis=-1, keepdims=True))).astype(BF16)
        outs[0][:, (N_HEADS + N_KV) * HEAD_DIM:] = ins[3][...].astype(BF16)

    return _rowwise(fn, [qkv, dq, dk, dv, cos, sin], [q_g, k_g], [(qkv.shape[1], BF16)],
                    [(1, HEAD_DIM), (1, HEAD_DIM)], tm=256, name="attn_qk_prep_bwd")


_NT = (((1,), (1,)), ((), ()))
_TN = (((0,), (0,)), ((), ()))


def _softmax_rows(q, k):
    s = lax.dot_general(q, k, _NT, preferred_element_type=F32) * SCALE
    p = jnp.exp(s - jnp.max(s, axis=-1, keepdims=True))
    return p * (1.0 / jnp.sum(p, axis=-1, keepdims=True))


def _attn_fwd(q, k, v, B, L, tq=512):
    tq = min(tq, L)
    nq = L // tq

    def body(q_ref, k_ref, v_ref, o_ref):
        p = _softmax_rows(q_ref[...], k_ref[...])
        o_ref[...] = jnp.dot(p.astype(BF16), v_ref[...], preferred_element_type=F32).astype(o_ref.dtype)

    return pl.pallas_call(
        body, name="attn_fwd", grid=(B, N_HEADS, nq),
        in_specs=[pl.BlockSpec((tq, HEAD_DIM), lambda b, h, i: (b * nq + i, h)),
                  pl.BlockSpec((L, HEAD_DIM), lambda b, h, i: (b, h // GROUP)),
                  pl.BlockSpec((L, HEAD_DIM), lambda b, h, i: (b, h // GROUP))],
        out_specs=pl.BlockSpec((tq, HEAD_DIM), lambda b, h, i: (b * nq + i, h)),
        out_shape=jax.ShapeDtypeStruct((B * L, D_MODEL), BF16),
        compiler_params=_params(("parallel", "parallel", "parallel")),
    )(q, k, v)


def _attn_bwd(q, k, v, do, B, L, tq=256):
    tq = min(tq, L)
    nq = L // tq

    def body(q_ref, k_ref, v_ref, do_ref, dq_ref, dk_ref, dv_ref):
        @pl.when((pl.program_id(2) == 0) & (pl.program_id(3) == 0))
        def _():
            dk_ref[...] = jnp.zeros(dk_ref.shape, F32)
            dv_ref[...] = jnp.zeros(dv_ref.shape, F32)

        qv, kv, vv, dov = q_ref[...], k_ref[...], v_ref[...], do_ref[...]
        p = _softmax_rows(qv, kv)
        dv_ref[...] += lax.dot_general(p.astype(BF16), dov, _TN, preferred_element_type=F32)
        dp = lax.dot_general(dov, vv, _NT, preferred_element_type=F32)
        ds = (p * (dp - jnp.sum(p * dp, axis=-1, keepdims=True)) * SCALE).astype(BF16)
        dq_ref[...] = jnp.dot(ds, kv, preferred_element_type=F32)
        dk_ref[...] += lax.dot_general(ds, qv, _TN, preferred_element_type=F32)

    qmap = lambda b, kh, g, i: (b * nq + i, kh * GROUP + g)
    kmap = lambda b, kh, g, i: (b, kh)
    kvw = N_KV * HEAD_DIM
    return pl.pallas_call(
        body, name="attn_bwd", grid=(B, N_KV, GROUP, nq),
        in_specs=[pl.BlockSpec((tq, HEAD_DIM), qmap), pl.BlockSpec((L, HEAD_DIM), kmap),
                  pl.BlockSpec((L, HEAD_DIM), kmap), pl.BlockSpec((tq, HEAD_DIM), qmap)],
        out_specs=[pl.BlockSpec((tq, HEAD_DIM), qmap), pl.BlockSpec((L, HEAD_DIM), kmap),
                   pl.BlockSpec((L, HEAD_DIM), kmap)],
        out_shape=[jax.ShapeDtypeStruct((B * L, D_MODEL), F32), jax.ShapeDtypeStruct((B * L, kvw), F32),
                   jax.ShapeDtypeStruct((B * L, kvw), F32)],
        compiler_params=_params(("parallel", "parallel", "arbitrary", "arbitrary")),
    )(q, k, v, do)


def _conv_shift(x, t, L, k):
    if k == 2:
        return x
    if k < 2:
        return jnp.where(t >= 2 - k, pltpu.roll(x, 2 - k, 0), 0.0)
    return jnp.where(t < L - (k - 2), pltpu.roll(x, L - (k - 2), 0), 0.0)


def _conv_fwd(z, wb, B, L, tc=256):
    noff = D_MODEL // tc

    def body(z_ref, w_ref, o_ref):
        x = z_ref[...]
        t = lax.broadcasted_iota(jnp.int32, x.shape, 0)
        acc = w_ref[4:5, :] + w_ref[2:3, :] * x
        for k in (0, 1, 3):
            acc = acc + w_ref[k:k + 1, :] * _conv_shift(x, t, L, k)
        o_ref[...] = acc

    return pl.pallas_call(
        body, name="rg_conv", grid=(B, noff),
        in_specs=[pl.BlockSpec((L, tc), lambda b, j: (b, noff + j)), pl.BlockSpec((SUBLANES, tc), lambda b, j: (0, j))],
        out_specs=pl.BlockSpec((L, tc), lambda b, j: (b, j)),
        out_shape=jax.ShapeDtypeStruct((B * L, D_MODEL), F32),
        compiler_params=_params(("parallel", "parallel")),
    )(z, wb)


def _conv_bwd(z, g, wb, B, L, tc=256):
    noff = D_MODEL // tc

    def body(z_ref, g_ref, w_ref, dx_ref, dw_ref):
        @pl.when(pl.program_id(1) == 0)
        def _():
            dw_ref[...] = jnp.zeros(dw_ref.shape, F32)

        x, gv = z_ref[...], g_ref[...]
        t = lax.broadcasted_iota(jnp.int32, x.shape, 0)
        dx = w_ref[2:3, :] * gv
        for k in (0, 1, 3):
            dx = dx + w_ref[k:k + 1, :] * _conv_shift(gv, t, L, 4 - k)
        dx_ref[...] = dx.astype(BF16)
        for k in range(4):
            dw_ref[k:k + 1, :] += _rsum(_conv_shift(x, t, L, k) * gv)
        dw_ref[4:5, :] += _rsum(gv)

    return pl.pallas_call(
        body, name="rg_conv_bwd", grid=(noff, B),
        in_specs=[pl.BlockSpec((L, tc), lambda j, b: (b, noff + j)), pl.BlockSpec((L, tc), lambda j, b: (b, j)),
                  pl.BlockSpec((SUBLANES, tc), lambda j, b: (0, j))],
        out_specs=[pl.BlockSpec((L, tc), lambda j, b: (b, j)), pl.BlockSpec((SUBLANES, tc), lambda j, b: (0, j))],
        out_shape=[jax.ShapeDtypeStruct((B * L, D_MODEL), BF16), jax.ShapeDtypeStruct((SUBLANES, D_MODEL), F32)],
        compiler_params=_params(("parallel", "arbitrary")),
    )(z, g, wb)


def _softplus(x):
    return jnp.maximum(x, 0.0) + jnp.log1p(jnp.exp(-jnp.abs(x)))


_ROW_BA, _ROW_BX, _ROW_LAM = 0, 2, 4


def _gate_math(xb, pre, vec_ref, d, sl):
    pa = pre[:, (2 * d) * LRU_BW:(2 * d + 1) * LRU_BW] + vec_ref[_ROW_BA + d:_ROW_BA + d + 1, sl]
    px = pre[:, (2 * d + 1) * LRU_BW:(2 * d + 2) * LRU_BW] + vec_ref[_ROW_BX + d:_ROW_BX + d + 1, sl]
    r = jax.nn.sigmoid(pa)
    i = jax.nn.sigmoid(px)
    sp = _softplus(-vec_ref[_ROW_LAM + d:_ROW_LAM + d + 1, sl])
    log_a = (-RG_C) * r * sp
    a = jnp.exp(log_a)
    th = jnp.tanh(log_a)
    om = -2.0 * th / (1.0 - th)
    mult = jnp.sqrt(om)
    return a, mult * (i * xb), (r, i, sp, om, mult)


def _gate_fwd(rec, wcat, gvec):
    def fn(ins, bs, outs, accs):
        for blk in range(LRU_BLOCKS):
            sl = slice(blk * LRU_BW, (blk + 1) * LRU_BW)
            xb = ins[0][:, sl]
            pre = jnp.dot(xb.astype(BF16), bs[0][sl, :], preferred_element_type=F32)
            for d in range(2):
                a, u, _ = _gate_math(xb, pre, bs[1], d, sl)
                outs[2 * d][:, sl] = a
                outs[2 * d + 1][:, sl] = u

    return _rowwise(fn, [rec], [wcat, gvec], [(D_MODEL, F32)] * 4, tm=256, name="rg_gate")


def _gate_bwd(rec, du_f, da_f, du_b, da_b, wcat, gvec):
    def fn(ins, bs, outs, accs):
        for blk in range(LRU_BLOCKS):
            sl = slice(blk * LRU_BW, (blk + 1) * LRU_BW)
            xb = ins[0][:, sl]
            xb16 = xb.astype(BF16)
            w = bs[0][sl, :]
            pre = jnp.dot(xb16, w, preferred_element_type=F32)
            dx = jnp.zeros_like(xb)
            dpre = []
            for d in range(2):
                a, _, (r, i, sp, om, mult) = _gate_math(xb, pre, bs[1], d, sl)
                du, da = ins[1 + 2 * d][:, sl], ins[2 + 2 * d][:, sl]
                d_i = du * mult * xb
                d_mult = du * i * xb
                dx = dx + du * mult * i
                dlog = da * a - d_mult * (1.0 - om) / mult
                d_r = dlog * ((-RG_C) * sp)
                d_sp = _rsum(dlog * ((-RG_C) * r))
                lam = bs[1][_ROW_LAM + d:_ROW_LAM + d + 1, sl]
                accs[1][_ROW_LAM + d:_ROW_LAM + d + 1, sl] += d_sp * (-jax.nn.sigmoid(-lam))
                dpa = d_r * r * (1.0 - r)
                dpx = d_i * i * (1.0 - i)
                accs[1][_ROW_BA + d:_ROW_BA + d + 1, sl] += _rsum(dpa)
                accs[1][_ROW_BX + d:_ROW_BX + d + 1, sl] += _rsum(dpx)
                dpre += [dpa, dpx]
            dpre = jnp.concatenate(dpre, axis=1).astype(BF16)
            accs[0][sl, :] += lax.dot_general(xb16, dpre, _TN, preferred_element_type=F32)
            outs[0][:, sl] = dx + lax.dot_general(dpre, w, _NT, preferred_element_type=F32)

    return _rowwise(fn, [rec, du_f, da_f, du_b, da_b], [wcat, gvec], [(D_MODEL, F32)],
                    [(D_MODEL, 4 * LRU_BW), (SUBLANES, D_MODEL)], tm=256, name="rg_gate_bwd")


def _as_time_blocks(x):
    return x.reshape(x.shape[0] // SUBLANES, SUBLANES, x.shape[1])


def _scan_call(body, ins, n_out, B, L, tc, name):
    nb = L // SUBLANES
    spec = pl.BlockSpec((nb, SUBLANES, tc), lambda b, j: (b, 0, j))
    T = ins[0].shape[0]
    outs = pl.pallas_call(
        functools.partial(body, nb), name=name, grid=(B, D_MODEL // tc),
        in_specs=[spec] * len(ins), out_specs=[spec] * n_out,
        out_shape=[jax.ShapeDtypeStruct((T // SUBLANES, SUBLANES, D_MODEL), F32)] * n_out,
        compiler_params=_params(("parallel", "parallel")),
    )(*[_as_time_blocks(x) for x in ins])
    return [o.reshape(T, D_MODEL) for o in outs]


def _scan_fwd(a_f, u_f, a_b, u_b, B, L, tc=256):
    def body(nb, af, uf, ab, ub, hf, hb):
        def step(i, carry):
            h1, h2 = carry
            ib = nb - 1 - i
            for j in range(SUBLANES):
                jb = SUBLANES - 1 - j
                h1 = af[i, j:j + 1, :] * h1 + uf[i, j:j + 1, :]
                hf[i, j:j + 1, :] = h1
                h2 = ab[ib, jb:jb + 1, :] * h2 + ub[ib, jb:jb + 1, :]
                hb[ib, jb:jb + 1, :] = h2
            return h1, h2

        zero = jnp.zeros((1, tc), F32)
        lax.fori_loop(0, nb, step, (zero, zero))

    return _scan_call(body, [a_f, u_f, a_b, u_b], 2, B, L, tc, "rg_scan")


def _scan_bwd(dy, a_f, h_f, a_b, h_b, B, L, tc=256):
    def body(nb, dy_r, af, hf, ab, hb, duf, daf, dub, dab):
        def step(i, carry):
            c1, c2 = carry
            ir = nb - 1 - i
            for j in range(SUBLANES):
                jr = SUBLANES - 1 - j
                lam1 = dy_r[ir, jr:jr + 1, :] + c1
                if jr > 0:
                    prev = hf[ir, jr - 1:jr, :]
                else:
                    prev = hf[jnp.maximum(ir - 1, 0), SUBLANES - 1:SUBLANES, :] * (ir > 0).astype(F32)
                duf[ir, jr:jr + 1, :] = lam1
                daf[ir, jr:jr + 1, :] = lam1 * prev
                c1 = af[ir, jr:jr + 1, :] * lam1
                lam2 = dy_r[i, j:j + 1, :] + c2
                if j < SUBLANES - 1:
                    nxt = hb[i, j + 1:j + 2, :]
                else:
                    nxt = hb[jnp.minimum(i + 1, nb - 1), 0:1, :] * (i < nb - 1).astype(F32)
                dub[i, j:j + 1, :] = lam2
                dab[i, j:j + 1, :] = lam2 * nxt
                c2 = ab[i, j:j + 1, :] * lam2
            return c1, c2

        zero = jnp.zeros((1, tc), F32)
        lax.fori_loop(0, nb, step, (zero, zero))

    return _scan_call(body, [dy, a_f, h_f, a_b, h_b], 4, B, L, tc, "rg_scan_bwd")


_GELU_C = math.sqrt(2.0 / math.pi)


def _gelu_parts(x):
    th = jnp.tanh(_GELU_C * (x + 0.044715 * x * x * x))
    return 0.5 * x * (1.0 + th), th


def _gated_out(h_f, h_b, z):
    def fn(ins, bs, outs, accs):
        gl, _ = _gelu_parts(ins[2][...])
        outs[0][...] = ((ins[0][...] + ins[1][...]) * gl).astype(BF16)

    return _rowwise(fn, [h_f, h_b, (z, D_MODEL, 0)], [], [(D_MODEL, BF16)], tm=512, name="rg_gated_out")[0]


def _gated_out_bwd(dyg, h_f, h_b, z):
    def fn(ins, bs, outs, accs):
        x = ins[3][...]
        gl, th = _gelu_parts(x)
        dgl = 0.5 * (1.0 + th) + 0.5 * x * (1.0 - th * th) * (_GELU_C * (1.0 + 3.0 * 0.044715 * x * x))
        g = ins[0][...]
        outs[0][...] = g * gl
        outs[1][...] = (g * (ins[1][...] + ins[2][...]) * dgl).astype(BF16)

    return _rowwise(fn, [dyg, h_f, h_b, (z, D_MODEL, 0)], [], [(D_MODEL, F32), (D_MODEL, BF16)], tm=512,
                    name="rg_gated_out_bwd")


def _make_wcat(w_a, w_x):
    g = jnp.stack([w_a[0, 0], w_x[0, 0], w_a[0, 1], w_x[0, 1]])
    return jnp.transpose(g, (1, 2, 0, 3)).reshape(D_MODEL, 4 * LRU_BW)


def _split_wcat(rows):
    g = jnp.transpose(rows.reshape(LRU_BLOCKS, LRU_BW, 4, LRU_BW), (2, 0, 1, 3))
    return jnp.stack([g[0], g[2]])[None], jnp.stack([g[1], g[3]])[None]


def _rows_at(part, first):
    return jnp.pad(part, ((first, SUBLANES - first - part.shape[0]), (0, 0)))


def _qk_slot(q_g, k_g):
    wide = lambda v, at: jnp.pad(v, ((0, SUBLANES - 1), (at, D_MODEL - at - HEAD_DIM)))
    return wide(q_g, 0) + wide(k_g, HEAD_DIM)


def _local_step(x, target, P, fetch, emit, B, L, after=None):
    g_mix, g_mlp = P["norm_mix_g"], P["norm_mlp_g"]
    h0 = _rms_fwd(x, g_mix[0:1], "rg_norm", after=after)
    w_in, w_out, conv_wb, wcat, gvec = fetch("rg", h0)
    z = _mm(h0, w_in, mode="nn", b_shard=True, name="rg_in")
    rec = _conv_fwd(z, conv_wb, B, L)
    a_f, u_f, a_b, u_b = _gate_fwd(rec, wcat, gvec)
    h_f, h_b = _scan_fwd(a_f, u_f, a_b, u_b, B, L)
    yg = _gated_out(h_f, h_b, z)
    x1 = _mm(yg, w_out, mode="nn", extras=(x,), epi=lambda acc, res: (acc + res,), name="rg_out")
    x2, mlp0 = _mlp_fwd(x1, g_mlp[0:1], fetch, 0)
    h3 = _rms_fwd(x2, g_mix[1:2], "attn_norm")
    w_qkv, w_o = fetch("att", h3)
    qkv = _mm(h3, w_qkv, mode="nn", b_shard=True, name="attn_qkv")
    cos, sin = _rope_tables(L, B)
    qh, kh, vh = _qk_prep(qkv, cos, sin, P["q_g"], P["k_g"])
    o = _attn_fwd(qh, kh, vh, B, L)
    x3 = _mm(o, w_o, mode="nn", extras=(x2,), epi=lambda acc, res: (acc + res,), name="attn_out")
    x4, mlp1 = _mlp_fwd(x3, g_mlp[1:2], fetch, 1)
    dx4, dx4_bf, loss_acc, d_final_g = _final_loss(x4, target, P["final_g"])

    dx3, dx3_bf, dg_mlp1, d_up1, d_down1 = _mlp_bwd(x3, g_mlp[1:2], mlp1, dx4, dx4_bf, 1, None)
    tok = emit("mlp1", [d_up1, d_down1])
    d_wo = _mm(o, dx3_bf, mode="tn", out_dtypes=(BF16,), name="attn_dwo", after=tok)
    do = _mm(dx3_bf, w_o, mode="nt", out_dtypes=(BF16,), name="attn_do")
    dq, dk, dv = _attn_bwd(qh, kh, vh, do, B, L)
    dqkv, dq_g, dk_g = _qk_prep_bwd(qkv, dq, dk, dv, cos, sin, P["q_g"], P["k_g"])
    d_wqkv = _mm(h3, dqkv, mode="tn", o_shard=True, out_dtypes=(BF16,), name="attn_dwqkv")
    tok = emit("att", [d_wqkv, d_wo])
    dh3 = _mm(dqkv, w_qkv, mode="nt", b_shard=True, name="attn_dh", after=tok)
    dx2, dx2_bf, dg_mix1 = _rms_bwd(x2, dh3, dx3, g_mix[1:2], "attn_norm_bwd")
    dx1, dx1_bf, dg_mlp0, d_up0, d_down0 = _mlp_bwd(x1, g_mlp[0:1], mlp0, dx2, dx2_bf, 0, None)
    tok = emit("mlp0", [d_up0, d_down0])
    d_wout = _mm(yg, dx1_bf, mode="tn", out_dtypes=(BF16,), name="rg_dwout", after=tok)
    tok = emit("rg_out", [d_wout])
    dyg = _mm(dx1_bf, w_out, mode="nt", name="rg_dyg", after=tok)
    dy, dgate = _gated_out_bwd(dyg, h_f, h_b, z)
    du_f, da_f, du_b, da_b = _scan_bwd(dy, a_f, h_f, a_b, h_b, B, L)
    drec_c, d_wcat, d_gvec = _gate_bwd(rec, du_f, da_f, du_b, da_b, wcat, gvec)
    drec, d_convwb = _conv_bwd(z, drec_c, conv_wb, B, L)
    dz = jnp.concatenate([dgate, drec], axis=1)
    d_win = _mm(h0, dz, mode="tn", o_shard=True, out_dtypes=(BF16,), name="rg_dwin")
    tok = emit("rg_in", [d_win])
    dh0 = _mm(dz, w_in, mode="nt", b_shard=True, name="rg_dh", after=tok)
    grad_x, _, dg_mix0 = _rms_bwd(x, dh0, dx1, g_mix[0:1], "rg_norm_bwd")

    norms = (_rows_at(dg_mix0, 0) + _rows_at(dg_mix1, 1) + _rows_at(dg_mlp0, 2) + _rows_at(dg_mlp1, 3)
             + _rows_at(d_final_g, 4))
    small = jnp.concatenate([norms, d_convwb, d_gvec, _qk_slot(dq_g, dk_g), d_wcat.reshape(GATE_ROWS, D_MODEL)],
                            axis=0)
    return loss_acc[0, 0], grad_x, small


_MESH = pl.DeviceIdType.MESH


def _place():
    x, y, c = lax.axis_index("x"), lax.axis_index("y"), lax.axis_index("c")
    peers = [((1 - x) if j & 2 else x, (1 - y) if j & 1 else y) for j in (1, 2, 3)]
    return x, y, c, peers


def _comm_call(body, ins, out_shapes, n_sem, name):
    return pl.pallas_call(
        body, name=name, in_specs=[_ANY] * len(ins), out_specs=[_ANY] * len(out_shapes), out_shape=out_shapes,
        scratch_shapes=[pltpu.SemaphoreType.DMA((n_sem,)), pltpu.SemaphoreType.DMA((n_sem,)),
                        pltpu.SemaphoreType.DMA((len(ins),))],
    )(*ins)


def _all_gather_chips(shards, name):
    n = len(shards)

    def body(*refs):
        ins, outs = refs[:n], refs[n:2 * n]
        send, recv, lsem = refs[2 * n:]
        x, y, c, peers = _place()
        me = 2 * x + y

        def copy(a, j, slot):
            px, py = peers[j]
            return pltpu.make_async_remote_copy(
                src_ref=ins[a], dst_ref=outs[a].at[slot], send_sem=send.at[3 * a + j], recv_sem=recv.at[3 * a + j],
                device_id=(px, py, c), device_id_type=_MESH)

        local = [pltpu.make_async_copy(ins[a], outs[a].at[me], lsem.at[a]) for a in range(n)]
        sends = [copy(a, j, me) for a in range(n) for j in range(3)]
        for cp in local + sends:
            cp.start()
        for a in range(n):
            for j, (px, py) in enumerate(peers):
                copy(a, j, 2 * px + py).wait_recv()
        for cp in sends:
            cp.wait_send()
        for cp in local:
            cp.wait()

    shapes = [jax.ShapeDtypeStruct((N_CHIPS,) + s.shape, s.dtype) for s in shards]
    return _comm_call(body, shards, shapes, 3 * n, name)


_HBM = pl.BlockSpec(memory_space=pltpu.HBM)
_SEM = pl.BlockSpec(memory_space=pltpu.SEMAPHORE)
_EFFECT = pltpu.SideEffectType.DATAFLOW_SIDE_EFFECTING


def _split_copies(kind, srcs, lands, send, recv):
    x, y, c, peers = _place()
    me = 2 * x + y
    out = []
    for a in range(len(srcs)):
        for j, (px, py) in enumerate(peers):
            if kind == "gather":
                src, there, here = srcs[a], lands[a].at[me], lands[a].at[2 * px + py]
            else:
                src, there, here = srcs[a].at[2 * px + py], lands[a].at[j], lands[a].at[j]
            mk = functools.partial(
                pltpu.make_async_remote_copy, src_ref=src, send_sem=send.at[3 * a + j], recv_sem=recv.at[3 * a + j],
                device_id=(px, py, c), device_id_type=_MESH)
            out.append((mk(dst_ref=there), mk(dst_ref=here)))
    return out


def _exchange_start(kind, srcs, lands, name, after=None):
    n = len(srcs)
    order = _after_operand(after)
    n_x = len(order)

    def body(*refs):
        send, recv = refs[2 * n + n_x], refs[2 * n + n_x + 1]
        token = refs[-1]
        for started, _ in _split_copies(kind, refs[:n], refs[n:2 * n], send, recv):
            started.start()
        token[...] = jnp.zeros(token.shape, F32)

    res = pl.pallas_call(
        body, name=name,
        out_shape=(pltpu.SemaphoreType.DMA((3 * n,)), pltpu.SemaphoreType.DMA((3 * n,)),
                   *[pltpu.HBM(a.shape, a.dtype) for a in list(srcs) + list(lands)],
                   jax.ShapeDtypeStruct((SUBLANES, LANES), F32)),
        in_specs=[_HBM] * (2 * n) + [_ANY] * n_x,
        out_specs=(_SEM, _SEM, *[_HBM] * (2 * n), pl.BlockSpec(memory_space=pltpu.VMEM)),
        input_output_aliases={i: 2 + i for i in range(2 * n)},
        compiler_params=pltpu.CompilerParams(has_side_effects=_EFFECT),
    )(*[pltpu.with_memory_space_constraint(a, pltpu.HBM) for a in list(srcs) + list(lands)], *order)
    return (res[0], res[1], res[2:2 + n], res[2 + n:2 + 2 * n]), res[-1]


def _exchange_wait(kind, handle, after, name):
    send, recv, srcs, lands = handle
    n = len(srcs)

    def body(*refs):
        for started, landing in _split_copies(kind, refs[:n], refs[n:2 * n], refs[2 * n], refs[2 * n + 1]):
            started.wait_send()
            landing.wait_recv()

    res = pl.pallas_call(
        body, name=name, out_shape=[pltpu.HBM(a.shape, a.dtype) for a in list(srcs) + list(lands)],
        in_specs=[_HBM] * (2 * n) + [_SEM, _SEM, _ANY], out_specs=[_HBM] * (2 * n),
        input_output_aliases={i: i for i in range(2 * n)},
        compiler_params=pltpu.CompilerParams(has_side_effects=_EFFECT),
    )(*srcs, *lands, send, recv, after)
    return res[:n], res[n:]


def _swap_cores(parts, name):
    n = len(parts)

    def body(*refs):
        ins, outs = refs[:n], refs[n:2 * n]
        send, recv, _ = refs[2 * n:]
        x, y, c, _peers = _place()
        copies = [pltpu.make_async_remote_copy(
            src_ref=ins[a], dst_ref=outs[a], send_sem=send.at[a], recv_sem=recv.at[a],
            device_id=(x, y, 1 - c), device_id_type=_MESH) for a in range(n)]
        for cp in copies:
            cp.start()
        for cp in copies:
            cp.wait_recv()
        for cp in copies:
            cp.wait_send()

    shapes = [jax.ShapeDtypeStruct(p.shape, p.dtype) for p in parts]
    return _comm_call(body, parts, shapes, n, name)


def _sum_slots(own, r, name):
    _, rows, cols = r.shape
    tm = min(512, rows)

    def body(own_ref, r_ref, o_ref):
        o_ref[...] = ((own_ref[...].astype(F32) + r_ref[0].astype(F32)) + r_ref[1].astype(F32)) + r_ref[2].astype(F32)

    return pl.pallas_call(
        body, name=name, grid=(rows // tm,),
        in_specs=[pl.BlockSpec((tm, cols), lambda i: (i, 0)), pl.BlockSpec((N_CHIPS - 1, tm, cols), lambda i: (0, i, 0))],
        out_specs=pl.BlockSpec((tm, cols), lambda i: (i, 0)),
        out_shape=jax.ShapeDtypeStruct((rows, cols), F32), compiler_params=_params(("parallel",)),
    )(own, r)


def _add(p, q, name):
    def fn(ins, bs, outs, accs):
        outs[0][...] = ins[0][...] + ins[1][...]

    return _rowwise(fn, [p, q], [], [(p.shape[1], F32)], tm=512, name=name)[0]


def _adamw(w, m, v, p, q, name):
    def fn(ins, bs, outs, accs):
        g = ins[3][...] if q is None else ins[3][...] + ins[4][...]
        m1 = ADAM_B1 * ins[1][...] + (1.0 - ADAM_B1) * g
        v1 = ADAM_B2 * ins[2][...] + (1.0 - ADAM_B2) * (g * g)
        m_hat = m1 / (1.0 - ADAM_B1 ** ADAM_STEP)
        v_hat = v1 / (1.0 - ADAM_B2 ** ADAM_STEP)
        outs[0][...] = g
        outs[1][...] = (-ADAM_LR) * (m_hat / (jnp.sqrt(v_hat) + ADAM_EPS) + ADAM_WD * ins[0][...])
        outs[2][...] = m1
        outs[3][...] = v1

    rows = [w, m, v, p] + ([] if q is None else [q])
    return _rowwise(fn, rows, [], [(w.shape[1], F32)] * 4, tm=256, name=name)


def _put_cols(shard, me):
    full = jnp.zeros((shard.shape[0], D_MODEL), F32)
    return lax.dynamic_update_slice(full, shard, (0, me * (D_MODEL // N_CHIPS)))


def _gate_vec_slot(b_a, b_x, lam):
    return _rows_at(b_a, _ROW_BA) + _rows_at(b_x, _ROW_BX) + _rows_at(lam, _ROW_LAM)


def _pack_small(p, me):
    return jnp.concatenate([
        _rows_at(p["norm_mix_g"], 0) + _rows_at(p["norm_mlp_g"], 2) + _rows_at(p["final_g"][None], 4),
        _rows_at(_put_cols(p["rg_conv_w"][0, :, 0, :], me), 0) + _rows_at(p["rg_conv_b"], 4),
        _gate_vec_slot(_put_cols(p["rg_b_a"][0], me), _put_cols(p["rg_b_x"][0], me), _put_cols(p["rg_lam"][0], me)),
        _qk_slot(p["at_q_g"], p["at_k_g"]),
        _make_wcat(p["rg_w_a"], p["rg_w_x"]).reshape(GATE_ROWS, D_MODEL),
    ], axis=0)


def _unpack_small(r, me):
    def cols(rows):
        return lax.dynamic_slice(rows, (0, me * (D_MODEL // N_CHIPS)), (rows.shape[0], D_MODEL // N_CHIPS))

    w_a, w_x = _split_wcat(r[SMALL_ROWS:])
    gate = r[16:24]
    return dict(
        norm_mix_g=r[0:2], norm_mlp_g=r[2:4], final_g=r[4], rg_conv_w=cols(r[8:12])[None, :, None, :],
        rg_conv_b=r[12:13], rg_b_a=cols(gate[_ROW_BA:_ROW_BA + 2])[None], rg_b_x=cols(gate[_ROW_BX:_ROW_BX + 2])[None],
        rg_lam=cols(gate[_ROW_LAM:_ROW_LAM + 2])[None], at_q_g=r[24:25, 0:HEAD_DIM],
        at_k_g=r[24:25, HEAD_DIM:2 * HEAD_DIM], rg_w_a=w_a, rg_w_x=w_x)


_WEIGHTS = ['norm_mix_g', 'norm_mlp_g', 'rg_w_in', 'rg_conv_w', 'rg_conv_b', 'rg_w_a', 'rg_b_a', 'rg_w_x', 'rg_b_x',
            'rg_lam', 'rg_w_out', 'at_w_qkv', 'at_q_g', 'at_k_g', 'at_w_o', 'mlp_w_up', 'mlp_w_down', 'final_g']
_BIG = ['rg_w_in', 'rg_w_out', 'at_w_qkv', 'at_w_o', 'mlp_w_up', 'mlp_w_down']


def kernel(x, *args):
    n_w = len(_WEIGHTS)
    w = dict(zip(_WEIGHTS, args[:n_w]))
    target = args[n_w]
    m = dict(zip(_WEIGHTS, args[n_w + 1:2 * n_w + 1]))
    v = dict(zip(_WEIGHTS, args[2 * n_w + 1:3 * n_w + 1]))
    B, L, _ = x.shape
    T = B * L
    me = 2 * lax.axis_index("x") + lax.axis_index("y")

    bf = lambda a: a.astype(BF16)
    vec = jnp.concatenate([_gate_vec_slot(w["rg_b_a"][0], w["rg_b_x"][0], w["rg_lam"][0]),
                           _rows_at(w["rg_conv_w"][0, :, 0, :], 0)], axis=0)
    groups = [("rg", [bf(w["rg_w_in"][0]), bf(w["rg_w_out"][0]), vec]),
              ("mlp0", [bf(w["mlp_w_up"][0]), bf(w["mlp_w_down"][0])]),
              ("att", [bf(w["at_w_qkv"][0]), bf(w["at_w_o"][0])]),
              ("mlp1", [bf(w["mlp_w_up"][1]), bf(w["mlp_w_down"][1])])]
    gathers, tok = {}, None
    for group, shards in groups:
        lands = [lax.dynamic_update_slice(lax.empty((N_CHIPS,) + s.shape, s.dtype), s[None], (me,) + (0,) * s.ndim)
                 for s in shards]
        gathers[group], tok = _exchange_start("gather", shards, lands, f"gather_{group}_start", after=tok)
    wcat = bf(_make_wcat(w["rg_w_a"], w["rg_w_x"]))

    def fetch(group, after):
        _, full = _exchange_wait("gather", gathers[group], after, f"gather_{group}_wait")
        if group == "rg":
            vec_full = jnp.transpose(full[2], (1, 0, 2)).reshape(2 * SUBLANES, D_MODEL)
            conv_wb = vec_full[SUBLANES:] + _rows_at(w["rg_conv_b"], 4)
            return full[0], full[1].reshape(D_MODEL, D_MODEL), conv_wb, wcat, vec_full[:SUBLANES]
        if group == "att":
            return full[0], full[1].reshape(D_MODEL, D_MODEL)
        return full[0], full[1].reshape(4 * D_MODEL, D_MODEL)

    scatters = {}

    def emit(group, grads):
        srcs = [g.reshape(N_CHIPS, -1, g.shape[-1]) for g in grads]
        lands = [lax.empty((N_CHIPS - 1,) + s.shape[1:], s.dtype) for s in srcs]
        scatters[group], token = _exchange_start("scatter", srcs, lands, f"scatter_{group}_start")
        return token

    P_vec = dict(norm_mix_g=w["norm_mix_g"], norm_mlp_g=w["norm_mlp_g"], final_g=w["final_g"][None],
                 q_g=w["at_q_g"], k_g=w["at_k_g"])
    loss_part, grad_x, small = _local_step(x.reshape(T, D_MODEL), target.reshape(T, D_MODEL), P_vec, fetch, emit,
                                           B, L, after=tok)
    emit("small", [small])
    loss = lax.psum(loss_part, ("x", "y", "c"))

    names = dict(mlp1=["up1", "down1"], att=["at_w_qkv", "at_w_o"], mlp0=["up0", "down0"], rg_out=["rg_w_out"],
                 rg_in=["rg_w_in"], small=["small"])
    order, parts = [], []
    for group, keys in names.items():
        srcs, lands = _exchange_wait("scatter", scatters[group], grad_x, f"scatter_{group}_wait")
        for k, s, r in zip(keys, srcs, lands):
            order.append(k)
            parts.append(_sum_slots(lax.dynamic_index_in_dim(s, me, 0, keepdims=False), r, f"sum_{k}"))
    theirs = _swap_cores(parts, "swap_cores")
    small_piece = _add(parts[-1], theirs[-1], "small_grad")
    small_full = _all_gather_chips([small_piece], "gather_small")[0].reshape(PACK_ROWS, D_MODEL)

    P = dict(zip(order, parts))
    Q = dict(zip(order, theirs))
    pq = dict(
        rg_w_in=(P["rg_w_in"], Q["rg_w_in"]), rg_w_out=(P["rg_w_out"], Q["rg_w_out"]),
        at_w_qkv=(P["at_w_qkv"], Q["at_w_qkv"]), at_w_o=(P["at_w_o"], Q["at_w_o"]),
        mlp_w_up=(jnp.concatenate([P["up0"], P["up1"]]), jnp.concatenate([Q["up0"], Q["up1"]])),
        mlp_w_down=(jnp.concatenate([P["down0"], P["down1"]]), jnp.concatenate([Q["down0"], Q["down1"]])),
    )
    res = {}
    for k in _BIG:
        shape = w[k].shape
        two_d = lambda a: a.reshape(-1, shape[-1])
        outs = _adamw(two_d(w[k]), two_d(m[k]), two_d(v[k]), pq[k][0], pq[k][1], f"adamw_{k}")
        res[k] = [o.reshape(shape) for o in outs]
    outs = _adamw(_pack_small(w, me), _pack_small(m, me), _pack_small(v, me), small_full, None, "adamw_small")
    unpacked = [_unpack_small(o, me) for o in outs]
    for k in _WEIGHTS:
        if k not in res:
            res[k] = [u[k] for u in unpacked]

    result = [loss, grad_x.reshape(B, L, D_MODEL)]
    for slot in range(4):
        result += [res[k][slot] for k in _WEIGHTS]
    return tuple(result)
```

```python
import functools
import math

import jax
import jax.numpy as jnp
from jax import lax
from jax.experimental import pallas as pl
from jax.experimental.pallas import tpu as pltpu

F32 = jnp.float32
BF16 = jnp.bfloat16

D_MODEL = 1024
HEAD_DIM = 128
N_HEADS = 8
N_KV = 2
GROUP = N_HEADS // N_KV
LRU_BLOCKS = 8
LRU_BW = 128
GRID_W = 64
ROPE_THETA = 10000.0
EPS = 1e-6
RG_C = 8.0
SCALE = 1.0 / math.sqrt(HEAD_DIM)
N_CHIPS = 4

ADAM_LR = 0.001
ADAM_B1 = 0.9
ADAM_B2 = 0.999
ADAM_EPS = 1e-08
ADAM_WD = 0.01
ADAM_STEP = 10

V7X_VMEM_BYTES = 64 * 1024 * 1024
VMEM_LIMIT = V7X_VMEM_BYTES * 3 // 4
LANES = 128
SUBLANES = 8

SMALL_ROWS = 32
GATE_ROWS = 512
PACK_ROWS = SMALL_ROWS + GATE_ROWS


def _params(sem):
    return pltpu.CompilerParams(dimension_semantics=sem, vmem_limit_bytes=VMEM_LIMIT)


_ANY = pl.BlockSpec(memory_space=pl.ANY)


def _after_operand(after):
    return [] if after is None else [after]


def _fit(t, n):
    if n <= t:
        return n
    c = (t // LANES) * LANES
    while n % c:
        c -= LANES
    return c


def _mm(a, b, *, mode, name, out_dtypes=(F32,), b_shard=False, o_shard=False, extras=(), epi=None,
        tm=1024, tn=1024, tk=1024, after=None):
    if mode == "tn":
        K, M = a.shape
        N = b.shape[1]
    else:
        M, K = a.shape
        if mode == "nn":
            N = b.shape[0] * b.shape[2] if b_shard else b.shape[1]
        else:
            N = b.shape[1] if b_shard else b.shape[0]
    tm = _fit(tm, M)
    if b_shard and mode == "nn":
        ns = b.shape[2]
        tn = _fit(tn, ns)
    elif o_shard:
        ns = N // N_CHIPS
        tn = _fit(tn, ns)
    else:
        tn = _fit(tn, N)
    if b_shard and mode == "nt":
        ks = b.shape[2]
        tk = _fit(tk, ks)
    else:
        tk = _fit(tk, K)
    nk = K // tk
    grid = (M // tm, N // tn, nk)

    if mode == "tn":
        a_spec = pl.BlockSpec((tk, tm), lambda i, j, k: (k, i))
        b_spec = pl.BlockSpec((tk, tn), lambda i, j, k: (k, j))
        dims = (((0,), (0,)), ((), ()))
    elif mode == "nn":
        a_spec = pl.BlockSpec((tm, tk), lambda i, j, k: (i, k))
        if b_shard:
            q = ns // tn
            b_spec = pl.BlockSpec((None, tk, tn), lambda i, j, k: (j // q, k, j % q))
        else:
            b_spec = pl.BlockSpec((tk, tn), lambda i, j, k: (k, j))
        dims = (((1,), (0,)), ((), ()))
    else:
        a_spec = pl.BlockSpec((tm, tk), lambda i, j, k: (i, k))
        if b_shard:
            q = ks // tk
            b_spec = pl.BlockSpec((None, tn, tk), lambda i, j, k: (k // q, j, k % q))
        else:
            b_spec = pl.BlockSpec((tn, tk), lambda i, j, k: (j, k))
        dims = (((1,), (1,)), ((), ()))

    if o_shard:
        qo = ns // tn
        o_specs = [pl.BlockSpec((None, tm, tn), lambda i, j, k: (j // qo, i, j % qo))]
        o_shapes = [jax.ShapeDtypeStruct((N_CHIPS, M, ns), out_dtypes[0])]
    else:
        o_specs = [pl.BlockSpec((tm, tn), lambda i, j, k: (i, j)) for _ in out_dtypes]
        o_shapes = [jax.ShapeDtypeStruct((M, N), dt) for dt in out_dtypes]
    e_specs = [pl.BlockSpec((tm, tn), lambda i, j, k: (i, j)) for _ in extras]
    n_e, n_o = len(extras), len(out_dtypes)
    order = _after_operand(after)
    n_x = len(order)
    if epi is None:
        epi = lambda acc: (acc,)

    def body(a_ref, b_ref, *rest):
        e_refs, o_refs = rest[:n_e], rest[n_e + n_x:n_e + n_x + n_o]

        def finish(acc):
            outs = epi(acc, *[r[...] for r in e_refs])
            for r, o in zip(o_refs, outs):
                r[...] = o.astype(r.dtype)

        part = lax.dot_general(a_ref[...], b_ref[...], dims, preferred_element_type=F32)
        if nk == 1:
            finish(part)
        else:
            acc_ref = rest[n_e + n_x + n_o]
            k = pl.program_id(2)

            @pl.when(k == 0)
            def _():
                acc_ref[...] = part

            @pl.when(k > 0)
            def _():
                acc_ref[...] += part

            @pl.when(k == nk - 1)
            def _():
                finish(acc_ref[...])

    scratch = [] if nk == 1 else [pltpu.VMEM((tm, tn), F32)]
    outs = pl.pallas_call(
        body, name=name, grid=grid, in_specs=[a_spec, b_spec] + e_specs + [_ANY] * n_x, out_specs=o_specs,
        out_shape=o_shapes, scratch_shapes=scratch,
        compiler_params=_params(("parallel", "parallel", "arbitrary")),
    )(a, b, *extras, *order)
    return outs[0] if n_o == 1 else outs


def _rowwise(fn, rows, bcast, outs, accs=(), *, tm, name, after=None):
    def norm(r):
        return r if isinstance(r, tuple) else (r, r.shape[1], 0)

    rows = [norm(r) for r in rows]
    T = rows[0][0].shape[0]
    tm = min(tm, T)
    while T % tm:
        tm -= SUBLANES
    n_r, n_b, n_o, n_a = len(rows), len(bcast), len(outs), len(accs)
    order = _after_operand(after)
    n_x = len(order)
    in_specs = [pl.BlockSpec((tm, c), functools.partial(lambda i, cb: (i, cb), cb=cb)) for _, c, cb in rows]
    in_specs += [pl.BlockSpec(b.shape, lambda i: (0, 0)) for b in bcast] + [_ANY] * n_x
    out_specs = [pl.BlockSpec((tm, c), lambda i: (i, 0)) for c, _ in outs]
    out_specs += [pl.BlockSpec(s, lambda i: (0, 0)) for s in accs]
    out_shape = [jax.ShapeDtypeStruct((T, c), dt) for c, dt in outs]
    out_shape += [jax.ShapeDtypeStruct(s, F32) for s in accs]

    def body(*refs):
        in_refs = refs[:n_r]
        b_refs = refs[n_r:n_r + n_b]
        o_refs = refs[n_r + n_b + n_x:n_r + n_b + n_x + n_o]
        a_refs = refs[n_r + n_b + n_x + n_o:]
        if n_a:
            @pl.when(pl.program_id(0) == 0)
            def _():
                for r in a_refs:
                    r[...] = jnp.zeros(r.shape, F32)
        fn(in_refs, b_refs, o_refs, a_refs)

    res = pl.pallas_call(
        body, name=name, grid=(T // tm,), in_specs=in_specs, out_specs=out_specs, out_shape=out_shape,
        compiler_params=_params(("arbitrary",) if n_a else ("parallel",)),
    )(*[r[0] for r in rows], *bcast, *order)
    return res


def _rsum(x):
    return jnp.sum(x, axis=0, keepdims=True)


def _rms_fwd(x, g, name, after=None):
    def fn(ins, bs, outs, accs):
        xv = ins[0][...]
        r = lax.rsqrt(jnp.mean(xv * xv, axis=-1, keepdims=True) + EPS)
        outs[0][...] = (xv * r * bs[0][...]).astype(BF16)

    return _rowwise(fn, [x], [g], [(D_MODEL, BF16)], tm=512, name=name, after=after)[0]


def _rms_bwd_math(xv, dh, g):
    r = lax.rsqrt(jnp.mean(xv * xv, axis=-1, keepdims=True) + EPS)
    hn = xv * r
    dgh = dh * g
    dx = r * (dgh - hn * jnp.mean(dgh * hn, axis=-1, keepdims=True))
    return dx, _rsum(dh * hn)


def _rms_bwd(x, dh, dres, g, name):
    def fn(ins, bs, outs, accs):
        dx, dg = _rms_bwd_math(ins[0][...], ins[1][...], bs[0][...])
        dx = dx + ins[2][...]
        outs[0][...] = dx
        outs[1][...] = dx.astype(BF16)
        accs[0][...] += dg

    return _rowwise(fn, [x, dh, dres], [g], [(D_MODEL, F32), (D_MODEL, BF16)], [(1, D_MODEL)], tm=256, name=name)


def _final_loss(x, target, g):
    def fn(ins, bs, outs, accs):
        xv = ins[0][...]
        gv = bs[0][...]
        r = lax.rsqrt(jnp.mean(xv * xv, axis=-1, keepdims=True) + EPS)
        e = xv * r * gv - ins[1][...]
        tok = jnp.mean(e * e, axis=-1, keepdims=True)
        accs[0][...] += 0.5 * jnp.sum(tok, axis=0, keepdims=True) * jnp.ones((1, LANES), F32)
        dx, dg = _rms_bwd_math(xv, e * (1.0 / D_MODEL), gv)
        outs[0][...] = dx
        outs[1][...] = dx.astype(BF16)
        accs[1][...] += dg

    return _rowwise(fn, [x, target], [g], [(D_MODEL, F32), (D_MODEL, BF16)], [(1, LANES), (1, D_MODEL)],
                    tm=256, name="final_loss")


def _relu2(acc):
    r = jnp.maximum(acc, 0.0)
    return r * r, r


def _mlp_fwd(x, g, fetch, tag):
    h = _rms_fwd(x, g, f"mlp{tag}_norm")
    w_up, w_down = fetch(f"mlp{tag}", h)
    wide = dict(tm=2048) if tag else {}
    deep = dict(tm=512, tk=4096) if tag else {}
    a, r = _mm(h, w_up, mode="nn", b_shard=True, out_dtypes=(BF16, BF16), epi=_relu2, name=f"mlp{tag}_up", **wide)
    x_out = _mm(a, w_down, mode="nn", extras=(x,), epi=lambda acc, res: (acc + res,), name=f"mlp{tag}_down",
                **deep)
    return x_out, (h, a, r, w_up, w_down)


def _mlp_bwd(x, g, saved, dx, dx_bf, tag, after):
    h, a, r, w_up, w_down = saved
    wide = dict(tm=2048) if tag else {}
    deep = dict(tm=512, tk=4096) if tag else {}
    d_down = _mm(a, dx_bf, mode="tn", out_dtypes=(BF16,), name=f"mlp{tag}_dwdown", after=after, **deep)
    dup = _mm(dx_bf, w_down, mode="nt", extras=(r,), out_dtypes=(BF16,),
              epi=lambda acc, rv: (acc * (2.0 * rv.astype(F32)),), name=f"mlp{tag}_dup", **wide)
    d_up = _mm(h, dup, mode="tn", o_shard=True, out_dtypes=(BF16,), name=f"mlp{tag}_dwup", **deep)
    dh = _mm(dup, w_up, mode="nt", b_shard=True, name=f"mlp{tag}_dh", **deep)
    dx_new, dx_new_bf, dg = _rms_bwd(x, dh, dx, g, f"mlp{tag}_norm_bwd")
    return dx_new, dx_new_bf, dg, d_up, d_down


def _rope_tables(L, B):
    rows = L // GRID_W
    row = jnp.repeat(jnp.arange(rows, dtype=F32), GRID_W)
    col = jnp.tile(jnp.arange(GRID_W, dtype=F32), rows)
    inv = ROPE_THETA ** (-jnp.arange(HEAD_DIM // 4, dtype=F32) / (HEAD_DIM // 4))
    ar, ac = row[:, None] * inv, col[:, None] * inv
    cos = jnp.concatenate([jnp.cos(ar), jnp.cos(ar), jnp.cos(ac), jnp.cos(ac)], axis=-1)
    sin = jnp.concatenate([-jnp.sin(ar), jnp.sin(ar), -jnp.sin(ac), jnp.sin(ac)], axis=-1)
    return jnp.tile(cos, (B, 1)), jnp.tile(sin, (B, 1))


def _swap_halves(x):
    lane = lax.broadcasted_iota(jnp.int32, x.shape, 1)
    return jnp.where((lane % 64) < 32, pltpu.roll(x, HEAD_DIM - 32, 1), pltpu.roll(x, 32, 1))


def _qk_prep(qkv, cos, sin, q_g, k_g):
    def fn(ins, bs, outs, accs):
        c, s = ins[1][...], ins[2][...]
        for h in range(N_HEADS + N_KV):
            xv = ins[0][:, h * HEAD_DIM:(h + 1) * HEAD_DIM]
            g = bs[0][...] if h < N_HEADS else bs[1][...]
            r = lax.rsqrt(jnp.mean(xv * xv, axis=-1, keepdims=True) + EPS)
            z = xv * r * g
            y = (z * c + _swap_halves(z) * s).astype(BF16)
            if h < N_HEADS:
                outs[0][:, h * HEAD_DIM:(h + 1) * HEAD_DIM] = y
            else:
                outs[1][:, (h - N_HEADS) * HEAD_DIM:(h - N_HEADS + 1) * HEAD_DIM] = y
        outs[2][...] = ins[0][:, (N_HEADS + N_KV) * HEAD_DIM:].astype(BF16)

    kvw = N_KV * HEAD_DIM
    return _rowwise(fn, [qkv, cos, sin], [q_g, k_g], [(D_MODEL, BF16), (kvw, BF16), (kvw, BF16)], tm=512,
                    name="attn_qk_prep")


def _qk_prep_bwd(qkv, dq, dk, dv, cos, sin, q_g, k_g):
    def fn(ins, bs, outs, accs):
        c, s = ins[4][...], ins[5][...]
        for h in range(N_HEADS + N_KV):
            sl = slice(h * HEAD_DIM, (h + 1) * HEAD_DIM)
            xv = ins[0][:, sl]
            if h < N_HEADS:
                g, dy, acc = bs[0][...], ins[1][:, sl], accs[0]
            else:
                ks = slice((h - N_HEADS) * HEAD_DIM, (h - N_HEADS + 1) * HEAD_DIM)
                g, dy, acc = bs[1][...], ins[2][:, ks], accs[1]
            r = lax.rsqrt(jnp.mean(xv * xv, axis=-1, keepdims=True) + EPS)
            xn = xv * r
            dz = dy * c - _swap_halves(dy) * s
            acc[...] += _rsum(dz * xn)
            dxn = dz * g
            outs[0][:, sl] = (r * (dxn - xn * jnp.mean(dxn * xn, axis=-1, keepdims=True))).astype(BF16)
        outs[0][:, (N_HEADS + N_KV) * HEAD_DIM:] = ins[3][...].astype(BF16)

    return _rowwise(fn, [qkv, dq, dk, dv, cos, sin], [q_g, k_g], [(qkv.shape[1], BF16)],
                    [(1, HEAD_DIM), (1, HEAD_DIM)], tm=256, name="attn_qk_prep_bwd")


_NT = (((1,), (1,)), ((), ()))
_TN = (((0,), (0,)), ((), ()))


_EXP2_SCALE = SCALE * math.log2(math.e)


def _exp_rows(q, k):
    s = lax.dot_general(q, k, _NT, preferred_element_type=F32)
    p = jnp.exp2((s - jnp.max(s, axis=-1, keepdims=True)) * _EXP2_SCALE)
    return p, jnp.sum(p, axis=-1, keepdims=True)


def _attn_fwd(q, k, v, B, L, tq=1024, sub=256):
    tq = min(tq, L)
    sub = min(sub, tq)
    nq = L // tq

    def body(q_ref, k_ref, v_ref, o_ref):
        kv, vv = k_ref[...], v_ref[...]
        for c in range(tq // sub):
            rows = slice(c * sub, (c + 1) * sub)
            p, l = _exp_rows(q_ref[rows, :], kv)
            o = jnp.dot(p.astype(BF16), vv, preferred_element_type=F32)
            o_ref[rows, :] = (o * (1.0 / l)).astype(o_ref.dtype)

    return pl.pallas_call(
        body, name="attn_fwd", grid=(B, N_HEADS, nq),
        in_specs=[pl.BlockSpec((tq, HEAD_DIM), lambda b, h, i: (b * nq + i, h)),
                  pl.BlockSpec((L, HEAD_DIM), lambda b, h, i: (b, h // GROUP)),
                  pl.BlockSpec((L, HEAD_DIM), lambda b, h, i: (b, h // GROUP))],
        out_specs=pl.BlockSpec((tq, HEAD_DIM), lambda b, h, i: (b * nq + i, h)),
        out_shape=jax.ShapeDtypeStruct((B * L, D_MODEL), BF16),
        compiler_params=_params(("parallel", "parallel", "parallel")),
    )(q, k, v)


def _attn_bwd(q, k, v, do, B, L, tq=512, sub=256):
    tq = min(tq, L)
    sub = min(sub, tq)
    nq = L // tq

    def body(q_ref, k_ref, v_ref, do_ref, dq_ref, dk_ref, dv_ref):
        @pl.when((pl.program_id(2) == 0) & (pl.program_id(3) == 0))
        def _():
            dk_ref[...] = jnp.zeros(dk_ref.shape, F32)
            dv_ref[...] = jnp.zeros(dv_ref.shape, F32)

        kv, vv = k_ref[...], v_ref[...]
        ps, es, dos, qs = [], [], [], []
        for c in range(tq // sub):
            rows = slice(c * sub, (c + 1) * sub)
            qc, doc = q_ref[rows, :], do_ref[rows, :]
            p, l = _exp_rows(qc, kv)
            inv = 1.0 / l
            dp = lax.dot_general(doc, vv, _NT, preferred_element_type=F32)
            delta = jnp.sum(p * dp, axis=-1, keepdims=True) * inv
            e = (p * (dp - delta)).astype(BF16)
            dq_ref[rows, :] = jnp.dot(e, kv, preferred_element_type=F32) * (inv * SCALE)
            ps.append(p.astype(BF16))
            es.append(e)
            dos.append((doc.astype(F32) * inv).astype(BF16))
            qs.append((qc.astype(F32) * (inv * SCALE)).astype(BF16))
        cat = lambda xs: xs[0] if len(xs) == 1 else jnp.concatenate(xs, axis=0)
        dv_ref[...] += lax.dot_general(cat(ps), cat(dos), _TN, preferred_element_type=F32)
        dk_ref[...] += lax.dot_general(cat(es), cat(qs), _TN, preferred_element_type=F32)

    qmap = lambda b, kh, g, i: (b * nq + i, kh * GROUP + g)
    kmap = lambda b, kh, g, i: (b, kh)
    kvw = N_KV * HEAD_DIM
    return pl.pallas_call(
        body, name="attn_bwd", grid=(B, N_KV, GROUP, nq),
        in_specs=[pl.BlockSpec((tq, HEAD_DIM), qmap), pl.BlockSpec((L, HEAD_DIM), kmap),
                  pl.BlockSpec((L, HEAD_DIM), kmap), pl.BlockSpec((tq, HEAD_DIM), qmap)],
        out_specs=[pl.BlockSpec((tq, HEAD_DIM), qmap), pl.BlockSpec((L, HEAD_DIM), kmap),
                   pl.BlockSpec((L, HEAD_DIM), kmap)],
        out_shape=[jax.ShapeDtypeStruct((B * L, D_MODEL), F32), jax.ShapeDtypeStruct((B * L, kvw), F32),
                   jax.ShapeDtypeStruct((B * L, kvw), F32)],
        compiler_params=_params(("parallel", "parallel", "arbitrary", "arbitrary")),
    )(q, k, v, do)


def _conv_shift(x, t, L, k):
    if k == 2:
        return x
    if k < 2:
        return jnp.where(t >= 2 - k, pltpu.roll(x, 2 - k, 0), 0.0)
    return jnp.where(t < L - (k - 2), pltpu.roll(x, L - (k - 2), 0), 0.0)


def _conv_fwd(z, wb, B, L, tc=256):
    noff = D_MODEL // tc

    def body(z_ref, w_ref, o_ref):
        x = z_ref[...]
        t = lax.broadcasted_iota(jnp.int32, x.shape, 0)
        acc = w_ref[4:5, :] + w_ref[2:3, :] * x
        for k in (0, 1, 3):
            acc = acc + w_ref[k:k + 1, :] * _conv_shift(x, t, L, k)
        o_ref[...] = acc

    return pl.pallas_call(
        body, name="rg_conv", grid=(B, noff),
        in_specs=[pl.BlockSpec((L, tc), lambda b, j: (b, noff + j)), pl.BlockSpec((SUBLANES, tc), lambda b, j: (0, j))],
        out_specs=pl.BlockSpec((L, tc), lambda b, j: (b, j)),
        out_shape=jax.ShapeDtypeStruct((B * L, D_MODEL), F32),
        compiler_params=_params(("parallel", "parallel")),
    )(z, wb)


def _conv_bwd(z, g, wb, B, L, tc=256):
    noff = D_MODEL // tc

    def body(z_ref, g_ref, w_ref, dx_ref, dw_ref):
        @pl.when(pl.program_id(1) == 0)
        def _():
            dw_ref[...] = jnp.zeros(dw_ref.shape, F32)

        x, gv = z_ref[...], g_ref[...]
        t = lax.broadcasted_iota(jnp.int32, x.shape, 0)
        dx = w_ref[2:3, :] * gv
        for k in (0, 1, 3):
            dx = dx + w_ref[k:k + 1, :] * _conv_shift(gv, t, L, 4 - k)
        dx_ref[...] = dx.astype(BF16)
        for k in range(4):
            dw_ref[k:k + 1, :] += _rsum(_conv_shift(x, t, L, k) * gv)
        dw_ref[4:5, :] += _rsum(gv)

    return pl.pallas_call(
        body, name="rg_conv_bwd", grid=(noff, B),
        in_specs=[pl.BlockSpec((L, tc), lambda j, b: (b, noff + j)), pl.BlockSpec((L, tc), lambda j, b: (b, j)),
                  pl.BlockSpec((SUBLANES, tc), lambda j, b: (0, j))],
        out_specs=[pl.BlockSpec((L, tc), lambda j, b: (b, j)), pl.BlockSpec((SUBLANES, tc), lambda j, b: (0, j))],
        out_shape=[jax.ShapeDtypeStruct((B * L, D_MODEL), BF16), jax.ShapeDtypeStruct((SUBLANES, D_MODEL), F32)],
        compiler_params=_params(("parallel", "arbitrary")),
    )(z, g, wb)


def _softplus(x):
    return jnp.maximum(x, 0.0) + jnp.log1p(jnp.exp(-jnp.abs(x)))


_ROW_BA, _ROW_BX, _ROW_LAM = 0, 2, 4


def _gate_math(xb, pre, vec_ref, d, sl):
    pa = pre[:, (2 * d) * LRU_BW:(2 * d + 1) * LRU_BW] + vec_ref[_ROW_BA + d:_ROW_BA + d + 1, sl]
    px = pre[:, (2 * d + 1) * LRU_BW:(2 * d + 2) * LRU_BW] + vec_ref[_ROW_BX + d:_ROW_BX + d + 1, sl]
    r = jax.nn.sigmoid(pa)
    i = jax.nn.sigmoid(px)
    sp = _softplus(-vec_ref[_ROW_LAM + d:_ROW_LAM + d + 1, sl])
    log_a = (-RG_C) * r * sp
    a = jnp.exp(log_a)
    th = jnp.tanh(log_a)
    om = -2.0 * th / (1.0 - th)
    mult = jnp.sqrt(om)
    return a, mult * (i * xb), (r, i, sp, om, mult)


def _gate_fwd(rec, wcat, gvec):
    def fn(ins, bs, outs, accs):
        for blk in range(LRU_BLOCKS):
            sl = slice(blk * LRU_BW, (blk + 1) * LRU_BW)
            xb = ins[0][:, sl]
            pre = jnp.dot(xb.astype(BF16), bs[0][sl, :], preferred_element_type=F32)
            for d in range(2):
                a, u, _ = _gate_math(xb, pre, bs[1], d, sl)
                outs[2 * d][:, sl] = a
                outs[2 * d + 1][:, sl] = u

    return _rowwise(fn, [rec], [wcat, gvec], [(D_MODEL, F32)] * 4, tm=256, name="rg_gate")


def _gate_bwd(rec, du_f, da_f, du_b, da_b, wcat, gvec):
    def fn(ins, bs, outs, accs):
        for blk in range(LRU_BLOCKS):
            sl = slice(blk * LRU_BW, (blk + 1) * LRU_BW)
            xb = ins[0][:, sl]
            xb16 = xb.astype(BF16)
            w = bs[0][sl, :]
            pre = jnp.dot(xb16, w, preferred_element_type=F32)
            dx = jnp.zeros_like(xb)
            dpre = []
            for d in range(2):
                a, _, (r, i, sp, om, mult) = _gate_math(xb, pre, bs[1], d, sl)
                du, da = ins[1 + 2 * d][:, sl], ins[2 + 2 * d][:, sl]
                d_i = du * mult * xb
                d_mult = du * i * xb
                dx = dx + du * mult * i
                dlog = da * a - d_mult * (1.0 - om) / mult
                d_r = dlog * ((-RG_C) * sp)
                d_sp = _rsum(dlog * ((-RG_C) * r))
                lam = bs[1][_ROW_LAM + d:_ROW_LAM + d + 1, sl]
                accs[1][_ROW_LAM + d:_ROW_LAM + d + 1, sl] += d_sp * (-jax.nn.sigmoid(-lam))
                dpa = d_r * r * (1.0 - r)
                dpx = d_i * i * (1.0 - i)
                accs[1][_ROW_BA + d:_ROW_BA + d + 1, sl] += _rsum(dpa)
                accs[1][_ROW_BX + d:_ROW_BX + d + 1, sl] += _rsum(dpx)
                dpre += [dpa, dpx]
            dpre = jnp.concatenate(dpre, axis=1).astype(BF16)
            accs[0][sl, :] += lax.dot_general(xb16, dpre, _TN, preferred_element_type=F32)
            outs[0][:, sl] = dx + lax.dot_general(dpre, w, _NT, preferred_element_type=F32)

    return _rowwise(fn, [rec, du_f, da_f, du_b, da_b], [wcat, gvec], [(D_MODEL, F32)],
                    [(D_MODEL, 4 * LRU_BW), (SUBLANES, D_MODEL)], tm=256, name="rg_gate_bwd")


def _as_time_blocks(x):
    return x.reshape(x.shape[0] // SUBLANES, SUBLANES, x.shape[1])


def _scan_call(body, ins, n_out, B, L, tc, name):
    nb = L // SUBLANES
    spec = pl.BlockSpec((nb, SUBLANES, tc), lambda b, j: (b, 0, j))
    T = ins[0].shape[0]
    outs = pl.pallas_call(
        functools.partial(body, nb), name=name, grid=(B, D_MODEL // tc),
        in_specs=[spec] * len(ins), out_specs=[spec] * n_out,
        out_shape=[jax.ShapeDtypeStruct((T // SUBLANES, SUBLANES, D_MODEL), F32)] * n_out,
        compiler_params=_params(("parallel", "parallel")),
    )(*[_as_time_blocks(x) for x in ins])
    return [o.reshape(T, D_MODEL) for o in outs]


def _scan_fwd(a_f, u_f, a_b, u_b, B, L, tc=256):
    def body(nb, af, uf, ab, ub, hf, hb):
        def step(i, carry):
            h1, h2 = carry
            ib = nb - 1 - i
            for j in range(SUBLANES):
                jb = SUBLANES - 1 - j
                h1 = af[i, j:j + 1, :] * h1 + uf[i, j:j + 1, :]
                hf[i, j:j + 1, :] = h1
                h2 = ab[ib, jb:jb + 1, :] * h2 + ub[ib, jb:jb + 1, :]
                hb[ib, jb:jb + 1, :] = h2
            return h1, h2

        zero = jnp.zeros((1, tc), F32)
        lax.fori_loop(0, nb, step, (zero, zero))

    return _scan_call(body, [a_f, u_f, a_b, u_b], 2, B, L, tc, "rg_scan")


def _scan_bwd(dy, a_f, h_f, a_b, h_b, B, L, tc=256):
    def body(nb, dy_r, af, hf, ab, hb, duf, daf, dub, dab):
        def step(i, carry):
            c1, c2 = carry
            ir = nb - 1 - i
            for j in range(SUBLANES):
                jr = SUBLANES - 1 - j
                lam1 = dy_r[ir, jr:jr + 1, :] + c1
                if jr > 0:
                    prev = hf[ir, jr - 1:jr, :]
                else:
                    prev = hf[jnp.maximum(ir - 1, 0), SUBLANES - 1:SUBLANES, :] * (ir > 0).astype(F32)
                duf[ir, jr:jr + 1, :] = lam1
                daf[ir, jr:jr + 1, :] = lam1 * prev
                c1 = af[ir, jr:jr + 1, :] * lam1
                lam2 = dy_r[i, j:j + 1, :] + c2
                if j < SUBLANES - 1:
                    nxt = hb[i, j + 1:j + 2, :]
                else:
                    nxt = hb[jnp.minimum(i + 1, nb - 1), 0:1, :] * (i < nb - 1).astype(F32)
                dub[i, j:j + 1, :] = lam2
                dab[i, j:j + 1, :] = lam2 * nxt
                c2 = ab[i, j:j + 1, :] * lam2
            return c1, c2

        zero = jnp.zeros((1, tc), F32)
        lax.fori_loop(0, nb, step, (zero, zero))

    return _scan_call(body, [dy, a_f, h_f, a_b, h_b], 4, B, L, tc, "rg_scan_bwd")


_GELU_C = math.sqrt(2.0 / math.pi)


def _gelu_parts(x):
    th = jnp.tanh(_GELU_C * (x + 0.044715 * x * x * x))
    return 0.5 * x * (1.0 + th), th


def _gated_out(h_f, h_b, z):
    def fn(ins, bs, outs, accs):
        gl, _ = _gelu_parts(ins[2][...])
        outs[0][...] = ((ins[0][...] + ins[1][...]) * gl).astype(BF16)

    return _rowwise(fn, [h_f, h_b, (z, D_MODEL, 0)], [], [(D_MODEL, BF16)], tm=512, name="rg_gated_out")[0]


def _gated_out_bwd(dyg, h_f, h_b, z):
    def fn(ins, bs, outs, accs):
        x = ins[3][...]
        gl, th = _gelu_parts(x)
        dgl = 0.5 * (1.0 + th) + 0.5 * x * (1.0 - th * th) * (_GELU_C * (1.0 + 3.0 * 0.044715 * x * x))
        g = ins[0][...]
        outs[0][...] = g * gl
        outs[1][...] = (g * (ins[1][...] + ins[2][...]) * dgl).astype(BF16)

    return _rowwise(fn, [dyg, h_f, h_b, (z, D_MODEL, 0)], [], [(D_MODEL, F32), (D_MODEL, BF16)], tm=512,
                    name="rg_gated_out_bwd")


def _make_wcat(w_a, w_x):
    g = jnp.stack([w_a[0, 0], w_x[0, 0], w_a[0, 1], w_x[0, 1]])
    return jnp.transpose(g, (1, 2, 0, 3)).reshape(D_MODEL, 4 * LRU_BW)


def _split_wcat(rows):
    g = jnp.transpose(rows.reshape(LRU_BLOCKS, LRU_BW, 4, LRU_BW), (2, 0, 1, 3))
    return jnp.stack([g[0], g[2]])[None], jnp.stack([g[1], g[3]])[None]


def _rows_at(part, first):
    return jnp.pad(part, ((first, SUBLANES - first - part.shape[0]), (0, 0)))


def _qk_slot(q_g, k_g):
    wide = lambda v, at: jnp.pad(v, ((0, SUBLANES - 1), (at, D_MODEL - at - HEAD_DIM)))
    return wide(q_g, 0) + wide(k_g, HEAD_DIM)


def _local_step(x, target, P, fetch, emit, B, L, after=None):
    g_mix, g_mlp = P["norm_mix_g"], P["norm_mlp_g"]
    h0 = _rms_fwd(x, g_mix[0:1], "rg_norm", after=after)
    w_in, w_out, conv_wb, wcat, gvec = fetch("rg", h0)
    z = _mm(h0, w_in, mode="nn", b_shard=True, name="rg_in")
    rec = _conv_fwd(z, conv_wb, B, L)
    a_f, u_f, a_b, u_b = _gate_fwd(rec, wcat, gvec)
    h_f, h_b = _scan_fwd(a_f, u_f, a_b, u_b, B, L)
    yg = _gated_out(h_f, h_b, z)
    x1 = _mm(yg, w_out, mode="nn", extras=(x,), epi=lambda acc, res: (acc + res,), name="rg_out")
    x2, mlp0 = _mlp_fwd(x1, g_mlp[0:1], fetch, 0)
    h3 = _rms_fwd(x2, g_mix[1:2], "attn_norm")
    w_qkv, w_o = fetch("att", h3)
    qkv = _mm(h3, w_qkv, mode="nn", b_shard=True, name="attn_qkv")
    cos, sin = _rope_tables(L, B)
    qh, kh, vh = _qk_prep(qkv, cos, sin, P["q_g"], P["k_g"])
    o = _attn_fwd(qh, kh, vh, B, L)
    x3 = _mm(o, w_o, mode="nn", extras=(x2,), epi=lambda acc, res: (acc + res,), name="attn_out")
    x4, mlp1 = _mlp_fwd(x3, g_mlp[1:2], fetch, 1)
    dx4, dx4_bf, loss_acc, d_final_g = _final_loss(x4, target, P["final_g"])

    dx3, dx3_bf, dg_mlp1, d_up1, d_down1 = _mlp_bwd(x3, g_mlp[1:2], mlp1, dx4, dx4_bf, 1, None)
    tok = emit("mlp1", [d_up1, d_down1])
    d_wo = _mm(o, dx3_bf, mode="tn", out_dtypes=(BF16,), name="attn_dwo", after=tok)
    do = _mm(dx3_bf, w_o, mode="nt", out_dtypes=(BF16,), name="attn_do")
    dq, dk, dv = _attn_bwd(qh, kh, vh, do, B, L)
    dqkv, dq_g, dk_g = _qk_prep_bwd(qkv, dq, dk, dv, cos, sin, P["q_g"], P["k_g"])
    d_wqkv = _mm(h3, dqkv, mode="tn", o_shard=True, out_dtypes=(BF16,), name="attn_dwqkv")
    tok = emit("att", [d_wqkv, d_wo])
    dh3 = _mm(dqkv, w_qkv, mode="nt", b_shard=True, name="attn_dh", after=tok)
    dx2, dx2_bf, dg_mix1 = _rms_bwd(x2, dh3, dx3, g_mix[1:2], "attn_norm_bwd")
    dx1, dx1_bf, dg_mlp0, d_up0, d_down0 = _mlp_bwd(x1, g_mlp[0:1], mlp0, dx2, dx2_bf, 0, None)
    tok = emit("mlp0", [d_up0, d_down0])
    d_wout = _mm(yg, dx1_bf, mode="tn", out_dtypes=(BF16,), name="rg_dwout", after=tok)
    tok = emit("rg_out", [d_wout])
    dyg = _mm(dx1_bf, w_out, mode="nt", name="rg_dyg", after=tok)
    dy, dgate = _gated_out_bwd(dyg, h_f, h_b, z)
    du_f, da_f, du_b, da_b = _scan_bwd(dy, a_f, h_f, a_b, h_b, B, L)
    drec_c, d_wcat, d_gvec = _gate_bwd(rec, du_f, da_f, du_b, da_b, wcat, gvec)
    drec, d_convwb = _conv_bwd(z, drec_c, conv_wb, B, L)
    dz = jnp.concatenate([dgate, drec], axis=1)
    d_win = _mm(h0, dz, mode="tn", o_shard=True, out_dtypes=(BF16,), name="rg_dwin")
    tok = emit("rg_in", [d_win])
    dh0 = _mm(dz, w_in, mode="nt", b_shard=True, name="rg_dh", after=tok)
    grad_x, _, dg_mix0 = _rms_bwd(x, dh0, dx1, g_mix[0:1], "rg_norm_bwd")

    norms = (_rows_at(dg_mix0, 0) + _rows_at(dg_mix1, 1) + _rows_at(dg_mlp0, 2) + _rows_at(dg_mlp1, 3)
             + _rows_at(d_final_g, 4))
    small = jnp.concatenate([norms, d_convwb, d_gvec, _qk_slot(dq_g, dk_g), d_wcat.reshape(GATE_ROWS, D_MODEL)],
                            axis=0)
    return loss_acc[0, 0], grad_x, small


_MESH = pl.DeviceIdType.MESH


def _place():
    x, y, c = lax.axis_index("x"), lax.axis_index("y"), lax.axis_index("c")
    peers = [((1 - x) if j & 2 else x, (1 - y) if j & 1 else y) for j in (1, 2, 3)]
    return x, y, c, peers


def _comm_call(body, ins, out_shapes, n_sem, name):
    return pl.pallas_call(
        body, name=name, in_specs=[_ANY] * len(ins), out_specs=[_ANY] * len(out_shapes), out_shape=out_shapes,
        scratch_shapes=[pltpu.SemaphoreType.DMA((n_sem,)), pltpu.SemaphoreType.DMA((n_sem,)),
                        pltpu.SemaphoreType.DMA((len(ins),))],
    )(*ins)


def _all_gather_chips(shards, name):
    n = len(shards)

    def body(*refs):
        ins, outs = refs[:n], refs[n:2 * n]
        send, recv, lsem = refs[2 * n:]
        x, y, c, peers = _place()
        me = 2 * x + y

        def copy(a, j, slot):
            px, py = peers[j]
            return pltpu.make_async_remote_copy(
                src_ref=ins[a], dst_ref=outs[a].at[slot], send_sem=send.at[3 * a + j], recv_sem=recv.at[3 * a + j],
                device_id=(px, py, c), device_id_type=_MESH)

        local = [pltpu.make_async_copy(ins[a], outs[a].at[me], lsem.at[a]) for a in range(n)]
        sends = [copy(a, j, me) for a in range(n) for j in range(3)]
        for cp in local + sends:
            cp.start()
        for a in range(n):
            for j, (px, py) in enumerate(peers):
                copy(a, j, 2 * px + py).wait_recv()
        for cp in sends:
            cp.wait_send()
        for cp in local:
            cp.wait()

    shapes = [jax.ShapeDtypeStruct((N_CHIPS,) + s.shape, s.dtype) for s in shards]
    return _comm_call(body, shards, shapes, 3 * n, name)


_HBM = pl.BlockSpec(memory_space=pltpu.HBM)
_SEM = pl.BlockSpec(memory_space=pltpu.SEMAPHORE)
_EFFECT = pltpu.SideEffectType.DATAFLOW_SIDE_EFFECTING


def _split_copies(kind, srcs, lands, send, recv):
    x, y, c, peers = _place()
    me = 2 * x + y
    out = []
    for a in range(len(srcs)):
        for j, (px, py) in enumerate(peers):
            if kind == "gather":
                src, there, here = srcs[a], lands[a].at[me], lands[a].at[2 * px + py]
            else:
                src, there, here = srcs[a].at[2 * px + py], lands[a].at[j], lands[a].at[j]
            mk = functools.partial(
                pltpu.make_async_remote_copy, src_ref=src, send_sem=send.at[3 * a + j], recv_sem=recv.at[3 * a + j],
                device_id=(px, py, c), device_id_type=_MESH)
            out.append((functools.partial(mk, dst_ref=there), functools.partial(mk, dst_ref=here)))
    return out


def _exchange_start(kind, srcs, lands, name, after=None):
    n = len(srcs)
    order = _after_operand(after)
    n_x = len(order)

    def body(*refs):
        send, recv = refs[2 * n + n_x], refs[2 * n + n_x + 1]
        token = refs[-1]
        for started, _ in _split_copies(kind, refs[:n], refs[n:2 * n], send, recv):
            started().start()
        token[...] = jnp.zeros(token.shape, F32)

    res = pl.pallas_call(
        body, name=name,
        out_shape=(pltpu.SemaphoreType.DMA((3 * n,)), pltpu.SemaphoreType.DMA((3 * n,)),
                   *[pltpu.HBM(a.shape, a.dtype) for a in list(srcs) + list(lands)],
                   jax.ShapeDtypeStruct((SUBLANES, LANES), F32)),
        in_specs=[_HBM] * (2 * n) + [_ANY] * n_x,
        out_specs=(_SEM, _SEM, *[_HBM] * (2 * n), pl.BlockSpec(memory_space=pltpu.VMEM)),
        input_output_aliases={i: 2 + i for i in range(2 * n)},
        compiler_params=pltpu.CompilerParams(has_side_effects=_EFFECT),
    )(*[pltpu.with_memory_space_constraint(a, pltpu.HBM) for a in list(srcs) + list(lands)], *order)
    return (res[0], res[1], res[2:2 + n], res[2 + n:2 + 2 * n]), res[-1]


def _exchange_wait(kind, handle, after, name):
    send, recv, srcs, lands = handle
    n = len(srcs)

    def body(*refs):
        for started, landing in _split_copies(kind, refs[:n], refs[n:2 * n], refs[2 * n], refs[2 * n + 1]):
            started().wait_send()
            landing().wait_recv()

    res = pl.pallas_call(
        body, name=name, out_shape=[pltpu.HBM(a.shape, a.dtype) for a in list(srcs) + list(lands)],
        in_specs=[_HBM] * (2 * n) + [_SEM, _SEM, _ANY], out_specs=[_HBM] * (2 * n),
        input_output_aliases={i: i for i in range(2 * n)},
        compiler_params=pltpu.CompilerParams(has_side_effects=_EFFECT),
    )(*srcs, *lands, send, recv, after)
    return res[:n], res[n:]


def _swap_cores(parts, name):
    n = len(parts)

    def body(*refs):
        ins, outs = refs[:n], refs[n:2 * n]
        send, recv, _ = refs[2 * n:]
        x, y, c, _peers = _place()
        copies = [pltpu.make_async_remote_copy(
            src_ref=ins[a], dst_ref=outs[a], send_sem=send.at[a], recv_sem=recv.at[a],
            device_id=(x, y, 1 - c), device_id_type=_MESH) for a in range(n)]
        for cp in copies:
            cp.start()
        for cp in copies:
            cp.wait_recv()
        for cp in copies:
            cp.wait_send()

    shapes = [jax.ShapeDtypeStruct(p.shape, p.dtype) for p in parts]
    return _comm_call(body, parts, shapes, n, name)


def _sum_slots(own, r, name):
    _, rows, cols = r.shape
    tm = min(512, rows)

    def body(own_ref, r_ref, o_ref):
        o_ref[...] = ((own_ref[...].astype(F32) + r_ref[0].astype(F32)) + r_ref[1].astype(F32)) + r_ref[2].astype(F32)

    return pl.pallas_call(
        body, name=name, grid=(rows // tm,),
        in_specs=[pl.BlockSpec((tm, cols), lambda i: (i, 0)), pl.BlockSpec((N_CHIPS - 1, tm, cols), lambda i: (0, i, 0))],
        out_specs=pl.BlockSpec((tm, cols), lambda i: (i, 0)),
        out_shape=jax.ShapeDtypeStruct((rows, cols), F32), compiler_params=_params(("parallel",)),
    )(own, r)


def _add(p, q, name):
    def fn(ins, bs, outs, accs):
        outs[0][...] = ins[0][...] + ins[1][...]

    return _rowwise(fn, [p, q], [], [(p.shape[1], F32)], tm=512, name=name)[0]


def _adamw(w, m, v, p, q, name):
    def fn(ins, bs, outs, accs):
        g = ins[3][...] if q is None else ins[3][...] + ins[4][...]
        m1 = ADAM_B1 * ins[1][...] + (1.0 - ADAM_B1) * g
        v1 = ADAM_B2 * ins[2][...] + (1.0 - ADAM_B2) * (g * g)
        m_hat = m1 / (1.0 - ADAM_B1 ** ADAM_STEP)
        v_hat = v1 / (1.0 - ADAM_B2 ** ADAM_STEP)
        outs[0][...] = g
        outs[1][...] = (-ADAM_LR) * (m_hat / (jnp.sqrt(v_hat) + ADAM_EPS) + ADAM_WD * ins[0][...])
        outs[2][...] = m1
        outs[3][...] = v1

    rows = [w, m, v, p] + ([] if q is None else [q])
    return _rowwise(fn, rows, [], [(w.shape[1], F32)] * 4, tm=256, name=name)


def _put_cols(shard, me):
    full = jnp.zeros((shard.shape[0], D_MODEL), F32)
    return lax.dynamic_update_slice(full, shard, (0, me * (D_MODEL // N_CHIPS)))


def _gate_vec_slot(b_a, b_x, lam):
    return _rows_at(b_a, _ROW_BA) + _rows_at(b_x, _ROW_BX) + _rows_at(lam, _ROW_LAM)


def _pack_small(p, me):
    return jnp.concatenate([
        _rows_at(p["norm_mix_g"], 0) + _rows_at(p["norm_mlp_g"], 2) + _rows_at(p["final_g"][None], 4),
        _rows_at(_put_cols(p["rg_conv_w"][0, :, 0, :], me), 0) + _rows_at(p["rg_conv_b"], 4),
        _gate_vec_slot(_put_cols(p["rg_b_a"][0], me), _put_cols(p["rg_b_x"][0], me), _put_cols(p["rg_lam"][0], me)),
        _qk_slot(p["at_q_g"], p["at_k_g"]),
        _make_wcat(p["rg_w_a"], p["rg_w_x"]).reshape(GATE_ROWS, D_MODEL),
    ], axis=0)


def _unpack_small(r, me):
    def cols(rows):
        return lax.dynamic_slice(rows, (0, me * (D_MODEL // N_CHIPS)), (rows.shape[0], D_MODEL // N_CHIPS))

    w_a, w_x = _split_wcat(r[SMALL_ROWS:])
    gate = r[16:24]
    return dict(
        norm_mix_g=r[0:2], norm_mlp_g=r[2:4], final_g=r[4], rg_conv_w=cols(r[8:12])[None, :, None, :],
        rg_conv_b=r[12:13], rg_b_a=cols(gate[_ROW_BA:_ROW_BA + 2])[None], rg_b_x=cols(gate[_ROW_BX:_ROW_BX + 2])[None],
        rg_lam=cols(gate[_ROW_LAM:_ROW_LAM + 2])[None], at_q_g=r[24:25, 0:HEAD_DIM],
        at_k_g=r[24:25, HEAD_DIM:2 * HEAD_DIM], rg_w_a=w_a, rg_w_x=w_x)


_WEIGHTS = ['norm_mix_g', 'norm_mlp_g', 'rg_w_in', 'rg_conv_w', 'rg_conv_b', 'rg_w_a', 'rg_b_a', 'rg_w_x', 'rg_b_x',
            'rg_lam', 'rg_w_out', 'at_w_qkv', 'at_q_g', 'at_k_g', 'at_w_o', 'mlp_w_up', 'mlp_w_down', 'final_g']
_BIG = ['rg_w_in', 'rg_w_out', 'at_w_qkv', 'at_w_o', 'mlp_w_up', 'mlp_w_down']


def kernel(x, *args):
    n_w = len(_WEIGHTS)
    w = dict(zip(_WEIGHTS, args[:n_w]))
    target = args[n_w]
    m = dict(zip(_WEIGHTS, args[n_w + 1:2 * n_w + 1]))
    v = dict(zip(_WEIGHTS, args[2 * n_w + 1:3 * n_w + 1]))
    B, L, _ = x.shape
    T = B * L
    me = 2 * lax.axis_index("x") + lax.axis_index("y")

    bf = lambda a: a.astype(BF16)
    vec = jnp.concatenate([_gate_vec_slot(w["rg_b_a"][0], w["rg_b_x"][0], w["rg_lam"][0]),
                           _rows_at(w["rg_conv_w"][0, :, 0, :], 0)], axis=0)
    groups = [("rg", [bf(w["rg_w_in"][0]), bf(w["rg_w_out"][0]), vec]),
              ("mlp0", [bf(w["mlp_w_up"][0]), bf(w["mlp_w_down"][0])]),
              ("att", [bf(w["at_w_qkv"][0]), bf(w["at_w_o"][0])]),
              ("mlp1", [bf(w["mlp_w_up"][1]), bf(w["mlp_w_down"][1])])]
    gathers, tok = {}, None
    for group, shards in groups:
        lands = [lax.dynamic_update_slice(lax.empty((N_CHIPS,) + s.shape, s.dtype), s[None], (me,) + (0,) * s.ndim)
                 for s in shards]
        gathers[group], tok = _exchange_start("gather", shards, lands, f"gather_{group}_start", after=tok)
    wcat = bf(_make_wcat(w["rg_w_a"], w["rg_w_x"]))

    def fetch(group, after):
        _, full = _exchange_wait("gather", gathers[group], after, f"gather_{group}_wait")
        if group == "rg":
            vec_full = jnp.transpose(full[2], (1, 0, 2)).reshape(2 * SUBLANES, D_MODEL)
            conv_wb = vec_full[SUBLANES:] + _rows_at(w["rg_conv_b"], 4)
            return full[0], full[1].reshape(D_MODEL, D_MODEL), conv_wb, wcat, vec_full[:SUBLANES]
        if group == "att":
            return full[0], full[1].reshape(D_MODEL, D_MODEL)
        return full[0], full[1].reshape(4 * D_MODEL, D_MODEL)

    scatters = {}

    def emit(group, grads):
        srcs = [g.reshape(N_CHIPS, -1, g.shape[-1]) for g in grads]
        lands = [lax.empty((N_CHIPS - 1,) + s.shape[1:], s.dtype) for s in srcs]
        scatters[group], token = _exchange_start("scatter", srcs, lands, f"scatter_{group}_start")
        return token

    P_vec = dict(norm_mix_g=w["norm_mix_g"], norm_mlp_g=w["norm_mlp_g"], final_g=w["final_g"][None],
                 q_g=w["at_q_g"], k_g=w["at_k_g"])
    loss_part, grad_x, small = _local_step(x.reshape(T, D_MODEL), target.reshape(T, D_MODEL), P_vec, fetch, emit,
                                           B, L, after=tok)
    emit("small", [small])
    loss = lax.psum(loss_part, ("x", "y", "c"))

    names = dict(mlp1=["up1", "down1"], att=["at_w_qkv", "at_w_o"], mlp0=["up0", "down0"], rg_out=["rg_w_out"],
                 rg_in=["rg_w_in"], small=["small"])
    order, parts = [], []
    for group, keys in names.items():
        srcs, lands = _exchange_wait("scatter", scatters[group], grad_x, f"scatter_{group}_wait")
        for k, s, r in zip(keys, srcs, lands):
            order.append(k)
            parts.append(_sum_slots(lax.dynamic_index_in_dim(s, me, 0, keepdims=False), r, f"sum_{k}"))
    theirs = _swap_cores(parts, "swap_cores")
    small_piece = _add(parts[-1], theirs[-1], "small_grad")
    small_full = _all_gather_chips([small_piece], "gather_small")[0].reshape(PACK_ROWS, D_MODEL)

    P = dict(zip(order, parts))
    Q = dict(zip(order, theirs))
    pq = dict(
        rg_w_in=(P["rg_w_in"], Q["rg_w_in"]), rg_w_out=(P["rg_w_out"], Q["rg_w_out"]),
        at_w_qkv=(P["at_w_qkv"], Q["at_w_qkv"]), at_w_o=(P["at_w_o"], Q["at_w_o"]),
        mlp_w_up=(jnp.concatenate([P["up0"], P["up1"]]), jnp.concatenate([Q["up0"], Q["up1"]])),
        mlp_w_down=(jnp.concatenate([P["down0"], P["down1"]]), jnp.concatenate([Q["down0"], Q["down1"]])),
    )
    res = {}
    for k in _BIG:
        shape = w[k].shape
        two_d = lambda a: a.reshape(-1, shape[-1])
        outs = _adamw(two_d(w[k]), two_d(m[k]), two_d(v[k]), pq[k][0], pq[k][1], f"adamw_{k}")
        res[k] = [o.reshape(shape) for o in outs]
    outs = _adamw(_pack_small(w, me), _pack_small(m, me), _pack_small(v, me), small_full, None, "adamw_small")
    unpacked = [_unpack_small(o, me) for o in outs]
    for k in _WEIGHTS:
        if k not in res:
            res[k] = [u[k] for u in unpacked]

    result = [loss, grad_x.reshape(B, L, D_MODEL)]
    for slot in range(4):
        result += [res[k][slot] for k in _WEIGHTS]
    return tuple(result)
```

```python
import functools
import math

import jax
import jax.numpy as jnp
from jax import lax
from jax.experimental import pallas as pl
from jax.experimental.pallas import tpu as pltpu

F32 = jnp.float32
BF16 = jnp.bfloat16

D_MODEL = 1024
HEAD_DIM = 128
N_HEADS = 8
N_KV = 2
GROUP = N_HEADS // N_KV
LRU_BLOCKS = 8
LRU_BW = 128
GRID_W = 64
ROPE_THETA = 10000.0
EPS = 1e-6
RG_C = 8.0
SCALE = 1.0 / math.sqrt(HEAD_DIM)
N_CHIPS = 4

ADAM_LR = 0.001
ADAM_B1 = 0.9
ADAM_B2 = 0.999
ADAM_EPS = 1e-08
ADAM_WD = 0.01
ADAM_STEP = 10

V7X_VMEM_BYTES = 64 * 1024 * 1024
VMEM_LIMIT = V7X_VMEM_BYTES * 3 // 4
LANES = 128
SUBLANES = 8

SMALL_ROWS = 32
GATE_ROWS = 512
PACK_ROWS = SMALL_ROWS + GATE_ROWS


def _params(sem):
    return pltpu.CompilerParams(dimension_semantics=sem, vmem_limit_bytes=VMEM_LIMIT)


_ANY = pl.BlockSpec(memory_space=pl.ANY)
_NN = (((1,), (0,)), ((), ()))
_NT = (((1,), (1,)), ((), ()))
_TN = (((0,), (0,)), ((), ()))


def _after_operand(after):
    return [] if after is None else [after]


def _fit(t, n):
    if n <= t:
        return n
    c = (t // LANES) * LANES
    while n % c:
        c -= LANES
    return c


MM_VMEM_BUDGET = VMEM_LIMIT * 3 // 4
MM_TN = 1024


def _mm_tile_rows(M, K, tn, out_dtypes, extras):
    per_row = 2 * (2 * K) + 4 * tn + sum(2 * tn * jnp.dtype(d).itemsize for d in out_dtypes)
    per_row += sum(2 * tn * e.dtype.itemsize for e in extras)
    fixed = 2 * (2 * K * tn)
    for tm in (2048, 1024, 512, 256, 128):
        if M % tm == 0 and fixed + tm * per_row <= MM_VMEM_BUDGET:
            return tm
    raise ValueError(f"no row tile fits VMEM for M={M} K={K} tn={tn}")


def _mm(a, b, *, mode, name, out_dtypes=(F32,), b_shard=False, o_shard=False, extras=(), epi=None, after=None):
    if mode == "tn":
        K, M = a.shape
        N = b.shape[1]
    else:
        M, K = a.shape
        if mode == "nn":
            N = b.shape[0] * b.shape[2] if b_shard else b.shape[1]
        else:
            N = b.shape[1] if b_shard else b.shape[0]
    ns = N
    if b_shard and mode == "nn":
        ns = b.shape[2]
    elif o_shard:
        ns = N // N_CHIPS
    tn = _fit(MM_TN, ns)
    tm = _mm_tile_rows(M, K, tn, out_dtypes, extras)
    grid = (M // tm, N // tn)
    q = ns // tn

    if mode == "tn":
        a_spec = pl.BlockSpec((K, tm), lambda i, j: (0, i))
        b_spec = pl.BlockSpec((K, tn), lambda i, j: (0, j))
        dims = _TN
    elif mode == "nn":
        a_spec = pl.BlockSpec((tm, K), lambda i, j: (i, 0))
        if b_shard:
            b_spec = pl.BlockSpec((None, K, tn), lambda i, j: (j // q, 0, j % q))
        else:
            b_spec = pl.BlockSpec((K, tn), lambda i, j: (0, j))
        dims = _NN
    else:
        a_spec = pl.BlockSpec((tm, K), lambda i, j: (i, 0))
        if b_shard:
            ks = b.shape[2]
            b_spec = pl.BlockSpec((N_CHIPS, tn, ks), lambda i, j: (0, j, 0))
        else:
            b_spec = pl.BlockSpec((tn, K), lambda i, j: (j, 0))
        dims = _NT

    if o_shard:
        o_specs = [pl.BlockSpec((None, tm, tn), lambda i, j: (j // q, i, j % q))]
        o_shapes = [jax.ShapeDtypeStruct((N_CHIPS, M, ns), out_dtypes[0])]
    else:
        o_specs = [pl.BlockSpec((tm, tn), lambda i, j: (i, j)) for _ in out_dtypes]
        o_shapes = [jax.ShapeDtypeStruct((M, N), dt) for dt in out_dtypes]
    e_specs = [pl.BlockSpec((tm, tn), lambda i, j: (i, j)) for _ in extras]
    n_e, n_o = len(extras), len(out_dtypes)
    order = _after_operand(after)
    n_x = len(order)
    if epi is None:
        epi = lambda acc: (acc,)

    def body(a_ref, b_ref, *rest):
        e_refs, o_refs = rest[:n_e], rest[n_e + n_x:n_e + n_x + n_o]
        if mode == "nt" and b_shard:
            acc = None
            for s in range(N_CHIPS):
                part = lax.dot_general(a_ref[:, s * ks:(s + 1) * ks], b_ref[s], dims, preferred_element_type=F32)
                acc = part if acc is None else acc + part
        else:
            acc = lax.dot_general(a_ref[...], b_ref[...], dims, preferred_element_type=F32)
        outs = epi(acc, *[r[...] for r in e_refs])
        for r, o in zip(o_refs, outs):
            r[...] = o.astype(r.dtype)

    outs = pl.pallas_call(
        body, name=name, grid=grid, in_specs=[a_spec, b_spec] + e_specs + [_ANY] * n_x, out_specs=o_specs,
        out_shape=o_shapes, compiler_params=_params(("parallel", "parallel")),
    )(a, b, *extras, *order)
    return outs[0] if n_o == 1 else outs


def _rowwise(fn, rows, bcast, outs, accs=(), *, tm, name, after=None):
    def norm(r):
        return r if isinstance(r, tuple) else (r, r.shape[1], 0)

    rows = [norm(r) for r in rows]
    T = rows[0][0].shape[0]
    tm = min(tm, T)
    while T % tm:
        tm -= SUBLANES
    n_r, n_b, n_o, n_a = len(rows), len(bcast), len(outs), len(accs)
    order = _after_operand(after)
    n_x = len(order)
    in_specs = [pl.BlockSpec((tm, c), functools.partial(lambda i, cb: (i, cb), cb=cb)) for _, c, cb in rows]
    in_specs += [pl.BlockSpec(b.shape, lambda i: (0, 0)) for b in bcast] + [_ANY] * n_x
    out_specs = [pl.BlockSpec((tm, c), lambda i: (i, 0)) for c, _ in outs]
    out_specs += [pl.BlockSpec(s, lambda i: (0, 0)) for s in accs]
    out_shape = [jax.ShapeDtypeStruct((T, c), dt) for c, dt in outs]
    out_shape += [jax.ShapeDtypeStruct(s, F32) for s in accs]

    def body(*refs):
        in_refs = refs[:n_r]
        b_refs = refs[n_r:n_r + n_b]
        o_refs = refs[n_r + n_b + n_x:n_r + n_b + n_x + n_o]
        a_refs = refs[n_r + n_b + n_x + n_o:]
        if n_a:
            @pl.when(pl.program_id(0) == 0)
            def _():
                for r in a_refs:
                    r[...] = jnp.zeros(r.shape, F32)
        fn(in_refs, b_refs, o_refs, a_refs)

    res = pl.pallas_call(
        body, name=name, grid=(T // tm,), in_specs=in_specs, out_specs=out_specs, out_shape=out_shape,
        compiler_params=_params(("arbitrary",) if n_a else ("parallel",)),
    )(*[r[0] for r in rows], *bcast, *order)
    return res


def _rsum(x):
    return jnp.sum(x, axis=0, keepdims=True)


def _rms_fwd(x, g, name, after=None):
    def fn(ins, bs, outs, accs):
        xv = ins[0][...]
        r = lax.rsqrt(jnp.mean(xv * xv, axis=-1, keepdims=True) + EPS)
        outs[0][...] = (xv * r * bs[0][...]).astype(BF16)

    return _rowwise(fn, [x], [g], [(D_MODEL, BF16)], tm=512, name=name, after=after)[0]


def _rms_bwd_math(xv, dh, g):
    r = lax.rsqrt(jnp.mean(xv * xv, axis=-1, keepdims=True) + EPS)
    hn = xv * r
    dgh = dh * g
    dx = r * (dgh - hn * jnp.mean(dgh * hn, axis=-1, keepdims=True))
    return dx, _rsum(dh * hn)


def _rms_bwd(x, dh, dres, g, name):
    def fn(ins, bs, outs, accs):
        dx, dg = _rms_bwd_math(ins[0][...], ins[1][...], bs[0][...])
        dx = dx + ins[2][...]
        outs[0][...] = dx
        outs[1][...] = dx.astype(BF16)
        accs[0][...] += dg

    return _rowwise(fn, [x, dh, dres], [g], [(D_MODEL, F32), (D_MODEL, BF16)], [(1, D_MODEL)], tm=256, name=name)


def _final_loss(x, target, g):
    def fn(ins, bs, outs, accs):
        xv = ins[0][...]
        gv = bs[0][...]
        r = lax.rsqrt(jnp.mean(xv * xv, axis=-1, keepdims=True) + EPS)
        e = xv * r * gv - ins[1][...]
        tok = jnp.mean(e * e, axis=-1, keepdims=True)
        accs[0][...] += 0.5 * jnp.sum(tok, axis=0, keepdims=True) * jnp.ones((1, LANES), F32)
        dx, dg = _rms_bwd_math(xv, e * (1.0 / D_MODEL), gv)
        outs[0][...] = dx
        outs[1][...] = dx.astype(BF16)
        accs[1][...] += dg

    return _rowwise(fn, [x, target], [g], [(D_MODEL, F32), (D_MODEL, BF16)], [(1, LANES), (1, D_MODEL)],
                    tm=256, name="final_loss")


def _relu2(acc):
    r = jnp.maximum(acc, 0.0)
    return r * r, r


def _mlp_fwd(x, g, fetch, tag):
    h = _rms_fwd(x, g, f"mlp{tag}_norm")
    w_up, w_down = fetch(f"mlp{tag}", h)
    a, r = _mm(h, w_up, mode="nn", b_shard=True, out_dtypes=(BF16, BF16), epi=_relu2, name=f"mlp{tag}_up")
    x_out = _mm(a, w_down, mode="nn", extras=(x,), epi=lambda acc, res: (acc + res,), name=f"mlp{tag}_down")
    return x_out, (h, a, r, w_up, w_down)


def _mlp_bwd(x, g, saved, dx, dx_bf, tag, after):
    h, a, r, w_up, w_down = saved
    d_down = _mm(a, dx_bf, mode="tn", out_dtypes=(BF16,), name=f"mlp{tag}_dwdown", after=after)
    dup = _mm(dx_bf, w_down, mode="nt", extras=(r,), out_dtypes=(BF16,),
              epi=lambda acc, rv: (acc * (2.0 * rv.astype(F32)),), name=f"mlp{tag}_dup")
    d_up = _mm(h, dup, mode="tn", o_shard=True, out_dtypes=(BF16,), name=f"mlp{tag}_dwup")
    dh = _mm(dup, w_up, mode="nt", b_shard=True, name=f"mlp{tag}_dh")
    dx_new, dx_new_bf, dg = _rms_bwd(x, dh, dx, g, f"mlp{tag}_norm_bwd")
    return dx_new, dx_new_bf, dg, d_up, d_down


def _rope_tables(L, B):
    rows = L // GRID_W
    row = jnp.repeat(jnp.arange(rows, dtype=F32), GRID_W)
    col = jnp.tile(jnp.arange(GRID_W, dtype=F32), rows)
    inv = ROPE_THETA ** (-jnp.arange(HEAD_DIM // 4, dtype=F32) / (HEAD_DIM // 4))
    ar, ac = row[:, None] * inv, col[:, None] * inv
    cos = jnp.concatenate([jnp.cos(ar), jnp.cos(ar), jnp.cos(ac), jnp.cos(ac)], axis=-1)
    sin = jnp.concatenate([-jnp.sin(ar), jnp.sin(ar), -jnp.sin(ac), jnp.sin(ac)], axis=-1)
    return jnp.tile(cos, (B, 1)), jnp.tile(sin, (B, 1))


def _swap_halves(x):
    lane = lax.broadcasted_iota(jnp.int32, x.shape, 1)
    return jnp.where((lane % 64) < 32, pltpu.roll(x, HEAD_DIM - 32, 1), pltpu.roll(x, 32, 1))


def _qk_prep(qkv, cos, sin, q_g, k_g):
    def fn(ins, bs, outs, accs):
        c, s = ins[1][...], ins[2][...]
        for h in range(N_HEADS + N_KV):
            xv = ins[0][:, h * HEAD_DIM:(h + 1) * HEAD_DIM]
            g = bs[0][...] if h < N_HEADS else bs[1][...]
            r = lax.rsqrt(jnp.mean(xv * xv, axis=-1, keepdims=True) + EPS)
            z = xv * r * g
            y = (z * c + _swap_halves(z) * s).astype(BF16)
            if h < N_HEADS:
                outs[0][:, h * HEAD_DIM:(h + 1) * HEAD_DIM] = y
            else:
                outs[1][:, (h - N_HEADS) * HEAD_DIM:(h - N_HEADS + 1) * HEAD_DIM] = y
        outs[2][...] = ins[0][:, (N_HEADS + N_KV) * HEAD_DIM:].astype(BF16)

    kvw = N_KV * HEAD_DIM
    return _rowwise(fn, [qkv, cos, sin], [q_g, k_g], [(D_MODEL, BF16), (kvw, BF16), (kvw, BF16)], tm=512,
                    name="attn_qk_prep")


def _qk_prep_bwd(qkv, dq, dk, dv, cos, sin, q_g, k_g):
    def fn(ins, bs, outs, accs):
        c, s = ins[4][...], ins[5][...]
        for h in range(N_HEADS + N_KV):
            sl = slice(h * HEAD_DIM, (h + 1) * HEAD_DIM)
            xv = ins[0][:, sl]
            if h < N_HEADS:
                g, dy, acc = bs[0][...], ins[1][:, sl], accs[0]
            else:
                ks = slice((h - N_HEADS) * HEAD_DIM, (h - N_HEADS + 1) * HEAD_DIM)
                g, dy, acc = bs[1][...], ins[2][:, ks], accs[1]
            r = lax.rsqrt(jnp.mean(xv * xv, axis=-1, keepdims=True) + EPS)
            xn = xv * r
            dz = dy * c - _swap_halves(dy) * s
            acc[...] += _rsum(dz * xn)
            dxn = dz * g
            outs[0][:, sl] = (r * (dxn - xn * jnp.mean(dxn * xn, axis=-1, keepdims=True))).astype(BF16)
        outs[0][:, (N_HEADS + N_KV) * HEAD_DIM:] = ins[3][...].astype(BF16)

    return _rowwise(fn, [qkv, dq, dk, dv, cos, sin], [q_g, k_g], [(qkv.shape[1], BF16)],
                    [(1, HEAD_DIM), (1, HEAD_DIM)], tm=256, name="attn_qk_prep_bwd")


_EXP2_SCALE = SCALE * math.log2(math.e)


def _exp_rows(q, k):
    s = lax.dot_general(q, k, _NT, preferred_element_type=F32)
    p = jnp.exp2((s - jnp.max(s, axis=-1, keepdims=True)) * _EXP2_SCALE)
    return p, jnp.sum(p, axis=-1, keepdims=True)


def _attn_fwd(q, k, v, B, L, tq=1024, sub=256):
    tq = min(tq, L)
    sub = min(sub, tq)
    nq = L // tq

    def body(q_ref, k_ref, v_ref, o_ref):
        kv, vv = k_ref[...], v_ref[...]
        for c in range(tq // sub):
            rows = slice(c * sub, (c + 1) * sub)
            p, l = _exp_rows(q_ref[rows, :], kv)
            o = jnp.dot(p.astype(BF16), vv, preferred_element_type=F32)
            o_ref[rows, :] = (o * (1.0 / l)).astype(o_ref.dtype)

    return pl.pallas_call(
        body, name="attn_fwd", grid=(B, N_HEADS, nq),
        in_specs=[pl.BlockSpec((tq, HEAD_DIM), lambda b, h, i: (b * nq + i, h)),
                  pl.BlockSpec((L, HEAD_DIM), lambda b, h, i: (b, h // GROUP)),
                  pl.BlockSpec((L, HEAD_DIM), lambda b, h, i: (b, h // GROUP))],
        out_specs=pl.BlockSpec((tq, HEAD_DIM), lambda b, h, i: (b * nq + i, h)),
        out_shape=jax.ShapeDtypeStruct((B * L, D_MODEL), BF16),
        compiler_params=_params(("parallel", "parallel", "parallel")),
    )(q, k, v)


def _attn_bwd(q, k, v, do, B, L, tq=512, sub=256):
    tq = min(tq, L)
    sub = min(sub, tq)
    nq = L // tq

    def body(q_ref, k_ref, v_ref, do_ref, dq_ref, dk_ref, dv_ref):
        @pl.when((pl.program_id(2) == 0) & (pl.program_id(3) == 0))
        def _():
            dk_ref[...] = jnp.zeros(dk_ref.shape, F32)
            dv_ref[...] = jnp.zeros(dv_ref.shape, F32)

        kv, vv = k_ref[...], v_ref[...]
        ps, es, dos, qs = [], [], [], []
        for c in range(tq // sub):
            rows = slice(c * sub, (c + 1) * sub)
            qc, doc = q_ref[rows, :], do_ref[rows, :]
            p, l = _exp_rows(qc, kv)
            inv = 1.0 / l
            dp = lax.dot_general(doc, vv, _NT, preferred_element_type=F32)
            delta = jnp.sum(p * dp, axis=-1, keepdims=True) * inv
            e = (p * (dp - delta)).astype(BF16)
            dq_ref[rows, :] = jnp.dot(e, kv, preferred_element_type=F32) * (inv * SCALE)
            ps.append(p.astype(BF16))
            es.append(e)
            dos.append((doc.astype(F32) * inv).astype(BF16))
            qs.append((qc.astype(F32) * (inv * SCALE)).astype(BF16))
        cat = lambda xs: xs[0] if len(xs) == 1 else jnp.concatenate(xs, axis=0)
        dv_ref[...] += lax.dot_general(cat(ps), cat(dos), _TN, preferred_element_type=F32)
        dk_ref[...] += lax.dot_general(cat(es), cat(qs), _TN, preferred_element_type=F32)

    qmap = lambda b, kh, g, i: (b * nq + i, kh * GROUP + g)
    kmap = lambda b, kh, g, i: (b, kh)
    kvw = N_KV * HEAD_DIM
    return pl.pallas_call(
        body, name="attn_bwd", grid=(B, N_KV, GROUP, nq),
        in_specs=[pl.BlockSpec((tq, HEAD_DIM), qmap), pl.BlockSpec((L, HEAD_DIM), kmap),
                  pl.BlockSpec((L, HEAD_DIM), kmap), pl.BlockSpec((tq, HEAD_DIM), qmap)],
        out_specs=[pl.BlockSpec((tq, HEAD_DIM), qmap), pl.BlockSpec((L, HEAD_DIM), kmap),
                   pl.BlockSpec((L, HEAD_DIM), kmap)],
        out_shape=[jax.ShapeDtypeStruct((B * L, D_MODEL), F32), jax.ShapeDtypeStruct((B * L, kvw), F32),
                   jax.ShapeDtypeStruct((B * L, kvw), F32)],
        compiler_params=_params(("parallel", "parallel", "arbitrary", "arbitrary")),
    )(q, k, v, do)


def _conv_shift(x, t, L, k):
    if k == 2:
        return x
    if k < 2:
        return jnp.where(t >= 2 - k, pltpu.roll(x, 2 - k, 0), 0.0)
    return jnp.where(t < L - (k - 2), pltpu.roll(x, L - (k - 2), 0), 0.0)


def _conv_fwd(z, wb, B, L, tc=256):
    noff = D_MODEL // tc

    def body(z_ref, w_ref, o_ref):
        x = z_ref[...]
        t = lax.broadcasted_iota(jnp.int32, x.shape, 0)
        acc = w_ref[4:5, :] + w_ref[2:3, :] * x
        for k in (0, 1, 3):
            acc = acc + w_ref[k:k + 1, :] * _conv_shift(x, t, L, k)
        o_ref[...] = acc

    return pl.pallas_call(
        body, name="rg_conv", grid=(B, noff),
        in_specs=[pl.BlockSpec((L, tc), lambda b, j: (b, noff + j)), pl.BlockSpec((SUBLANES, tc), lambda b, j: (0, j))],
        out_specs=pl.BlockSpec((L, tc), lambda b, j: (b, j)),
        out_shape=jax.ShapeDtypeStruct((B * L, D_MODEL), F32),
        compiler_params=_params(("parallel", "parallel")),
    )(z, wb)


def _conv_bwd(z, g, wb, B, L, tc=256):
    noff = D_MODEL // tc

    def body(z_ref, g_ref, w_ref, dx_ref, dw_ref):
        @pl.when(pl.program_id(1) == 0)
        def _():
            dw_ref[...] = jnp.zeros(dw_ref.shape, F32)

        x, gv = z_ref[...], g_ref[...]
        t = lax.broadcasted_iota(jnp.int32, x.shape, 0)
        dx = w_ref[2:3, :] * gv
        for k in (0, 1, 3):
            dx = dx + w_ref[k:k + 1, :] * _conv_shift(gv, t, L, 4 - k)
        dx_ref[...] = dx.astype(BF16)
        for k in range(4):
            dw_ref[k:k + 1, :] += _rsum(_conv_shift(x, t, L, k) * gv)
        dw_ref[4:5, :] += _rsum(gv)

    return pl.pallas_call(
        body, name="rg_conv_bwd", grid=(noff, B),
        in_specs=[pl.BlockSpec((L, tc), lambda j, b: (b, noff + j)), pl.BlockSpec((L, tc), lambda j, b: (b, j)),
                  pl.BlockSpec((SUBLANES, tc), lambda j, b: (0, j))],
        out_specs=[pl.BlockSpec((L, tc), lambda j, b: (b, j)), pl.BlockSpec((SUBLANES, tc), lambda j, b: (0, j))],
        out_shape=[jax.ShapeDtypeStruct((B * L, D_MODEL), BF16), jax.ShapeDtypeStruct((SUBLANES, D_MODEL), F32)],
        compiler_params=_params(("parallel", "arbitrary")),
    )(z, g, wb)


def _softplus(x):
    return jnp.maximum(x, 0.0) + jnp.log1p(jnp.exp(-jnp.abs(x)))


_ROW_BA, _ROW_BX, _ROW_LAM = 0, 2, 4


def _gate_math(xb, pre, vec_ref, d, sl):
    pa = pre[:, (2 * d) * LRU_BW:(2 * d + 1) * LRU_BW] + vec_ref[_ROW_BA + d:_ROW_BA + d + 1, sl]
    px = pre[:, (2 * d + 1) * LRU_BW:(2 * d + 2) * LRU_BW] + vec_ref[_ROW_BX + d:_ROW_BX + d + 1, sl]
    r = jax.nn.sigmoid(pa)
    i = jax.nn.sigmoid(px)
    sp = _softplus(-vec_ref[_ROW_LAM + d:_ROW_LAM + d + 1, sl])
    log_a = (-RG_C) * r * sp
    a = jnp.exp(log_a)
    th = jnp.tanh(log_a)
    om = -2.0 * th / (1.0 - th)
    mult = jnp.sqrt(om)
    return a, mult * (i * xb), (r, i, sp, om, mult)


def _gate_fwd(rec, wcat, gvec):
    def fn(ins, bs, outs, accs):
        for blk in range(LRU_BLOCKS):
            sl = slice(blk * LRU_BW, (blk + 1) * LRU_BW)
            xb = ins[0][:, sl]
            pre = jnp.dot(xb.astype(BF16), bs[0][sl, :], preferred_element_type=F32)
            for d in range(2):
                a, u, _ = _gate_math(xb, pre, bs[1], d, sl)
                outs[2 * d][:, sl] = a
                outs[2 * d + 1][:, sl] = u

    return _rowwise(fn, [rec], [wcat, gvec], [(D_MODEL, F32)] * 4, tm=256, name="rg_gate")


def _gate_bwd(rec, du_f, da_f, du_b, da_b, wcat, gvec):
    def fn(ins, bs, outs, accs):
        for blk in range(LRU_BLOCKS):
            sl = slice(blk * LRU_BW, (blk + 1) * LRU_BW)
            xb = ins[0][:, sl]
            xb16 = xb.astype(BF16)
            w = bs[0][sl, :]
            pre = jnp.dot(xb16, w, preferred_element_type=F32)
            dx = jnp.zeros_like(xb)
            dpre = []
            for d in range(2):
                a, _, (r, i, sp, om, mult) = _gate_math(xb, pre, bs[1], d, sl)
                du, da = ins[1 + 2 * d][:, sl], ins[2 + 2 * d][:, sl]
                d_i = du * mult * xb
                d_mult = du * i * xb
                dx = dx + du * mult * i
                dlog = da * a - d_mult * (1.0 - om) / mult
                d_r = dlog * ((-RG_C) * sp)
                d_sp = _rsum(dlog * ((-RG_C) * r))
                lam = bs[1][_ROW_LAM + d:_ROW_LAM + d + 1, sl]
                accs[1][_ROW_LAM + d:_ROW_LAM + d + 1, sl] += d_sp * (-jax.nn.sigmoid(-lam))
                dpa = d_r * r * (1.0 - r)
                dpx = d_i * i * (1.0 - i)
                accs[1][_ROW_BA + d:_ROW_BA + d + 1, sl] += _rsum(dpa)
                accs[1][_ROW_BX + d:_ROW_BX + d + 1, sl] += _rsum(dpx)
                dpre += [dpa, dpx]
            dpre = jnp.concatenate(dpre, axis=1).astype(BF16)
            accs[0][sl, :] += lax.dot_general(xb16, dpre, _TN, preferred_element_type=F32)
            outs[0][:, sl] = dx + lax.dot_general(dpre, w, _NT, preferred_element_type=F32)

    return _rowwise(fn, [rec, du_f, da_f, du_b, da_b], [wcat, gvec], [(D_MODEL, F32)],
                    [(D_MODEL, 4 * LRU_BW), (SUBLANES, D_MODEL)], tm=256, name="rg_gate_bwd")


def _as_time_blocks(x):
    return x.reshape(x.shape[0] // SUBLANES, SUBLANES, x.shape[1])


def _scan_call(body, ins, n_out, B, L, tc, name):
    nb = L // SUBLANES
    spec = pl.BlockSpec((nb, SUBLANES, tc), lambda b, j: (b, 0, j))
    T = ins[0].shape[0]
    outs = pl.pallas_call(
        functools.partial(body, nb), name=name, grid=(B, D_MODEL // tc),
        in_specs=[spec] * len(ins), out_specs=[spec] * n_out,
        out_shape=[jax.ShapeDtypeStruct((T // SUBLANES, SUBLANES, D_MODEL), F32)] * n_out,
        compiler_params=_params(("parallel", "parallel")),
    )(*[_as_time_blocks(x) for x in ins])
    return [o.reshape(T, D_MODEL) for o in outs]


def _scan_fwd(a_f, u_f, a_b, u_b, B, L, tc=256):
    def body(nb, af, uf, ab, ub, hf, hb):
        def step(i, carry):
            h1, h2 = carry
            ib = nb - 1 - i
            for j in range(SUBLANES):
                jb = SUBLANES - 1 - j
                h1 = af[i, j:j + 1, :] * h1 + uf[i, j:j + 1, :]
                hf[i, j:j + 1, :] = h1
                h2 = ab[ib, jb:jb + 1, :] * h2 + ub[ib, jb:jb + 1, :]
                hb[ib, jb:jb + 1, :] = h2
            return h1, h2

        zero = jnp.zeros((1, tc), F32)
        lax.fori_loop(0, nb, step, (zero, zero))

    return _scan_call(body, [a_f, u_f, a_b, u_b], 2, B, L, tc, "rg_scan")


def _scan_bwd(dy, a_f, h_f, a_b, h_b, B, L, tc=256):
    def body(nb, dy_r, af, hf, ab, hb, duf, daf, dub, dab):
        def step(i, carry):
            c1, c2 = carry
            ir = nb - 1 - i
            for j in range(SUBLANES):
                jr = SUBLANES - 1 - j
                lam1 = dy_r[ir, jr:jr + 1, :] + c1
                if jr > 0:
                    prev = hf[ir, jr - 1:jr, :]
                else:
                    prev = hf[jnp.maximum(ir - 1, 0), SUBLANES - 1:SUBLANES, :] * (ir > 0).astype(F32)
                duf[ir, jr:jr + 1, :] = lam1
                daf[ir, jr:jr + 1, :] = lam1 * prev
                c1 = af[ir, jr:jr + 1, :] * lam1
                lam2 = dy_r[i, j:j + 1, :] + c2
                if j < SUBLANES - 1:
                    nxt = hb[i, j + 1:j + 2, :]
                else:
                    nxt = hb[jnp.minimum(i + 1, nb - 1), 0:1, :] * (i < nb - 1).astype(F32)
                dub[i, j:j + 1, :] = lam2
                dab[i, j:j + 1, :] = lam2 * nxt
                c2 = ab[i, j:j + 1, :] * lam2
            return c1, c2

        zero = jnp.zeros((1, tc), F32)
        lax.fori_loop(0, nb, step, (zero, zero))

    return _scan_call(body, [dy, a_f, h_f, a_b, h_b], 4, B, L, tc, "rg_scan_bwd")


_GELU_C = math.sqrt(2.0 / math.pi)


def _gelu_parts(x):
    th = jnp.tanh(_GELU_C * (x + 0.044715 * x * x * x))
    return 0.5 * x * (1.0 + th), th


def _gated_out(h_f, h_b, z):
    def fn(ins, bs, outs, accs):
        gl, _ = _gelu_parts(ins[2][...])
        outs[0][...] = ((ins[0][...] + ins[1][...]) * gl).astype(BF16)

    return _rowwise(fn, [h_f, h_b, (z, D_MODEL, 0)], [], [(D_MODEL, BF16)], tm=512, name="rg_gated_out")[0]


def _gated_out_bwd(dyg, h_f, h_b, z):
    def fn(ins, bs, outs, accs):
        x = ins[3][...]
        gl, th = _gelu_parts(x)
        dgl = 0.5 * (1.0 + th) + 0.5 * x * (1.0 - th * th) * (_GELU_C * (1.0 + 3.0 * 0.044715 * x * x))
        g = ins[0][...]
        outs[0][...] = g * gl
        outs[1][...] = (g * (ins[1][...] + ins[2][...]) * dgl).astype(BF16)

    return _rowwise(fn, [dyg, h_f, h_b, (z, D_MODEL, 0)], [], [(D_MODEL, F32), (D_MODEL, BF16)], tm=512,
                    name="rg_gated_out_bwd")


def _make_wcat(w_a, w_x):
    g = jnp.stack([w_a[0, 0], w_x[0, 0], w_a[0, 1], w_x[0, 1]])
    return jnp.transpose(g, (1, 2, 0, 3)).reshape(D_MODEL, 4 * LRU_BW)


def _split_wcat(rows):
    g = jnp.transpose(rows.reshape(LRU_BLOCKS, LRU_BW, 4, LRU_BW), (2, 0, 1, 3))
    return jnp.stack([g[0], g[2]])[None], jnp.stack([g[1], g[3]])[None]


def _rows_at(part, first):
    return jnp.pad(part, ((first, SUBLANES - first - part.shape[0]), (0, 0)))


def _qk_slot(q_g, k_g):
    wide = lambda v, at: jnp.pad(v, ((0, SUBLANES - 1), (at, D_MODEL - at - HEAD_DIM)))
    return wide(q_g, 0) + wide(k_g, HEAD_DIM)


def _local_step(x, target, P, fetch, emit, B, L, after=None):
    g_mix, g_mlp = P["norm_mix_g"], P["norm_mlp_g"]
    h0 = _rms_fwd(x, g_mix[0:1], "rg_norm", after=after)
    w_in, w_out, conv_wb, wcat, gvec = fetch("rg", h0)
    z = _mm(h0, w_in, mode="nn", b_shard=True, name="rg_in")
    rec = _conv_fwd(z, conv_wb, B, L)
    a_f, u_f, a_b, u_b = _gate_fwd(rec, wcat, gvec)
    h_f, h_b = _scan_fwd(a_f, u_f, a_b, u_b, B, L)
    yg = _gated_out(h_f, h_b, z)
    x1 = _mm(yg, w_out, mode="nn", extras=(x,), epi=lambda acc, res: (acc + res,), name="rg_out")
    x2, mlp0 = _mlp_fwd(x1, g_mlp[0:1], fetch, 0)
    h3 = _rms_fwd(x2, g_mix[1:2], "attn_norm")
    w_qkv, w_o = fetch("att", h3)
    qkv = _mm(h3, w_qkv, mode="nn", b_shard=True, name="attn_qkv")
    cos, sin = _rope_tables(L, B)
    qh, kh, vh = _qk_prep(qkv, cos, sin, P["q_g"], P["k_g"])
    o = _attn_fwd(qh, kh, vh, B, L)
    x3 = _mm(o, w_o, mode="nn", extras=(x2,), epi=lambda acc, res: (acc + res,), name="attn_out")
    x4, mlp1 = _mlp_fwd(x3, g_mlp[1:2], fetch, 1)
    dx4, dx4_bf, loss_acc, d_final_g = _final_loss(x4, target, P["final_g"])

    dx3, dx3_bf, dg_mlp1, d_up1, d_down1 = _mlp_bwd(x3, g_mlp[1:2], mlp1, dx4, dx4_bf, 1, None)
    tok = emit("mlp1", [d_up1, d_down1])
    d_wo = _mm(o, dx3_bf, mode="tn", out_dtypes=(BF16,), name="attn_dwo", after=tok)
    do = _mm(dx3_bf, w_o, mode="nt", out_dtypes=(BF16,), name="attn_do")
    dq, dk, dv = _attn_bwd(qh, kh, vh, do, B, L)
    dqkv, dq_g, dk_g = _qk_prep_bwd(qkv, dq, dk, dv, cos, sin, P["q_g"], P["k_g"])
    d_wqkv = _mm(h3, dqkv, mode="tn", o_shard=True, out_dtypes=(BF16,), name="attn_dwqkv")
    tok = emit("att", [d_wqkv, d_wo])
    dh3 = _mm(dqkv, w_qkv, mode="nt", b_shard=True, name="attn_dh", after=tok)
    dx2, dx2_bf, dg_mix1 = _rms_bwd(x2, dh3, dx3, g_mix[1:2], "attn_norm_bwd")
    tok = emit("point_attn_done", [dx2_bf])
    dx1, dx1_bf, dg_mlp0, d_up0, d_down0 = _mlp_bwd(x1, g_mlp[0:1], mlp0, dx2, dx2_bf, 0, tok)
    tok = emit("mlp0", [d_up0, d_down0])
    d_wout = _mm(yg, dx1_bf, mode="tn", out_dtypes=(BF16,), name="rg_dwout", after=tok)
    tok = emit("rg_out", [d_wout])
    dyg = _mm(dx1_bf, w_out, mode="nt", name="rg_dyg", after=tok)
    dy, dgate = _gated_out_bwd(dyg, h_f, h_b, z)
    du_f, da_f, du_b, da_b = _scan_bwd(dy, a_f, h_f, a_b, h_b, B, L)
    drec_c, d_wcat, d_gvec = _gate_bwd(rec, du_f, da_f, du_b, da_b, wcat, gvec)
    drec, d_convwb = _conv_bwd(z, drec_c, conv_wb, B, L)
    dz = jnp.concatenate([dgate, drec], axis=1)
    tok = emit("point_conv_done", [dz])
    d_win = _mm(h0, dz, mode="tn", o_shard=True, out_dtypes=(BF16,), name="rg_dwin", after=tok)
    tok = emit("rg_in", [d_win])
    dh0 = _mm(dz, w_in, mode="nt", b_shard=True, name="rg_dh", after=tok)
    grad_x, _, dg_mix0 = _rms_bwd(x, dh0, dx1, g_mix[0:1], "rg_norm_bwd")

    norms = (_rows_at(dg_mix0, 0) + _rows_at(dg_mix1, 1) + _rows_at(dg_mlp0, 2) + _rows_at(dg_mlp1, 3)
             + _rows_at(d_final_g, 4))
    small = jnp.concatenate([norms, d_convwb, d_gvec, _qk_slot(dq_g, dk_g), d_wcat.reshape(GATE_ROWS, D_MODEL)],
                            axis=0)
    return loss_acc[0, 0], grad_x, small


_MESH = pl.DeviceIdType.MESH


def _place():
    x, y, c = lax.axis_index("x"), lax.axis_index("y"), lax.axis_index("c")
    peers = [((1 - x) if j & 2 else x, (1 - y) if j & 1 else y) for j in (1, 2, 3)]
    return x, y, c, peers


def _comm_call(body, ins, out_shapes, n_sem, name):
    return pl.pallas_call(
        body, name=name, in_specs=[_ANY] * len(ins), out_specs=[_ANY] * len(out_shapes), out_shape=out_shapes,
        scratch_shapes=[pltpu.SemaphoreType.DMA((n_sem,)), pltpu.SemaphoreType.DMA((n_sem,)),
                        pltpu.SemaphoreType.DMA((len(ins),))],
    )(*ins)


def _all_gather_chips(shards, name):
    n = len(shards)

    def body(*refs):
        ins, outs = refs[:n], refs[n:2 * n]
        send, recv, lsem = refs[2 * n:]
        x, y, c, peers = _place()
        me = 2 * x + y

        def copy(a, j, slot):
            px, py = peers[j]
            return pltpu.make_async_remote_copy(
                src_ref=ins[a], dst_ref=outs[a].at[slot], send_sem=send.at[3 * a + j], recv_sem=recv.at[3 * a + j],
                device_id=(px, py, c), device_id_type=_MESH)

        local = [pltpu.make_async_copy(ins[a], outs[a].at[me], lsem.at[a]) for a in range(n)]
        sends = [copy(a, j, me) for a in range(n) for j in range(3)]
        for cp in local + sends:
            cp.start()
        for a in range(n):
            for j, (px, py) in enumerate(peers):
                copy(a, j, 2 * px + py).wait_recv()
        for cp in sends:
            cp.wait_send()
        for cp in local:
            cp.wait()

    shapes = [jax.ShapeDtypeStruct((N_CHIPS,) + s.shape, s.dtype) for s in shards]
    return _comm_call(body, shards, shapes, 3 * n, name)


_HBM = pl.BlockSpec(memory_space=pltpu.HBM)
_SEM = pl.BlockSpec(memory_space=pltpu.SEMAPHORE)
_EFFECT = pltpu.SideEffectType.DATAFLOW_SIDE_EFFECTING


_COPIES = dict(gather=N_CHIPS - 1, scatter=N_CHIPS - 1, swap=1)


def _split_copies(kind, srcs, lands, send, recv):
    x, y, c, peers = _place()
    me = 2 * x + y
    per = _COPIES[kind]
    out = []
    for a in range(len(srcs)):
        for j in range(per):
            if kind == "swap":
                src, there, here, dev = srcs[a], lands[a], lands[a], (x, y, 1 - c)
            else:
                px, py = peers[j]
                dev = (px, py, c)
                if kind == "gather":
                    src, there, here = srcs[a], lands[a].at[me], lands[a].at[2 * px + py]
                else:
                    src, there, here = srcs[a].at[2 * px + py], lands[a].at[j], lands[a].at[j]
            mk = functools.partial(
                pltpu.make_async_remote_copy, src_ref=src, send_sem=send.at[per * a + j],
                recv_sem=recv.at[per * a + j], device_id=dev, device_id_type=_MESH)
            out.append((functools.partial(mk, dst_ref=there), functools.partial(mk, dst_ref=here)))
    return out


def _exchange_start(kind, srcs, lands, name, after=None):
    n = len(srcs)
    order = _after_operand(after)
    n_x = len(order)

    def body(*refs):
        send, recv = refs[2 * n + n_x], refs[2 * n + n_x + 1]
        token = refs[-1]
        for started, _ in _split_copies(kind, refs[:n], refs[n:2 * n], send, recv):
            started().start()
        token[...] = jnp.zeros(token.shape, F32)

    res = pl.pallas_call(
        body, name=name,
        out_shape=(pltpu.SemaphoreType.DMA((_COPIES[kind] * n,)), pltpu.SemaphoreType.DMA((_COPIES[kind] * n,)),
                   *[pltpu.HBM(a.shape, a.dtype) for a in list(srcs) + list(lands)],
                   jax.ShapeDtypeStruct((SUBLANES, LANES), F32)),
        in_specs=[_HBM] * (2 * n) + [_ANY] * n_x,
        out_specs=(_SEM, _SEM, *[_HBM] * (2 * n), pl.BlockSpec(memory_space=pltpu.VMEM)),
        input_output_aliases={i: 2 + i for i in range(2 * n)},
        compiler_params=pltpu.CompilerParams(has_side_effects=_EFFECT),
    )(*[pltpu.with_memory_space_constraint(a, pltpu.HBM) for a in list(srcs) + list(lands)], *order)
    return (res[0], res[1], res[2:2 + n], res[2 + n:2 + 2 * n]), res[-1]


def _exchange_wait(kind, handle, after, name):
    send, recv, srcs, lands = handle
    n = len(srcs)

    def body(*refs):
        for started, landing in _split_copies(kind, refs[:n], refs[n:2 * n], refs[2 * n], refs[2 * n + 1]):
            started().wait_send()
            landing().wait_recv()

    res = pl.pallas_call(
        body, name=name, out_shape=[pltpu.HBM(a.shape, a.dtype) for a in list(srcs) + list(lands)],
        in_specs=[_HBM] * (2 * n) + [_SEM, _SEM, _ANY], out_specs=[_HBM] * (2 * n),
        input_output_aliases={i: i for i in range(2 * n)},
        compiler_params=pltpu.CompilerParams(has_side_effects=_EFFECT),
    )(*srcs, *lands, send, recv, after)
    return res[:n], res[n:]


def _swap_cores(parts, name):
    n = len(parts)

    def body(*refs):
        ins, outs = refs[:n], refs[n:2 * n]
        send, recv, _ = refs[2 * n:]
        x, y, c, _peers = _place()
        copies = [pltpu.make_async_remote_copy(
            src_ref=ins[a], dst_ref=outs[a], send_sem=send.at[a], recv_sem=recv.at[a],
            device_id=(x, y, 1 - c), device_id_type=_MESH) for a in range(n)]
        for cp in copies:
            cp.start()
        for cp in copies:
            cp.wait_recv()
        for cp in copies:
            cp.wait_send()

    shapes = [jax.ShapeDtypeStruct(p.shape, p.dtype) for p in parts]
    return _comm_call(body, parts, shapes, n, name)


def _sum_slots(own, r, name):
    _, rows, cols = r.shape
    tm = min(512, rows)

    def body(own_ref, r_ref, o_ref):
        o_ref[...] = ((own_ref[...].astype(F32) + r_ref[0].astype(F32)) + r_ref[1].astype(F32)) + r_ref[2].astype(F32)

    return pl.pallas_call(
        body, name=name, grid=(rows // tm,),
        in_specs=[pl.BlockSpec((tm, cols), lambda i: (i, 0)), pl.BlockSpec((N_CHIPS - 1, tm, cols), lambda i: (0, i, 0))],
        out_specs=pl.BlockSpec((tm, cols), lambda i: (i, 0)),
        out_shape=jax.ShapeDtypeStruct((rows, cols), F32), compiler_params=_params(("parallel",)),
    )(own, r)


def _add(p, q, name):
    def fn(ins, bs, outs, accs):
        outs[0][...] = ins[0][...] + ins[1][...]

    return _rowwise(fn, [p, q], [], [(p.shape[1], F32)], tm=512, name=name)[0]


def _adamw(w, m, v, p, q, name):
    def fn(ins, bs, outs, accs):
        g = ins[3][...] if q is None else ins[3][...] + ins[4][...]
        m1 = ADAM_B1 * ins[1][...] + (1.0 - ADAM_B1) * g
        v1 = ADAM_B2 * ins[2][...] + (1.0 - ADAM_B2) * (g * g)
        m_hat = m1 / (1.0 - ADAM_B1 ** ADAM_STEP)
        v_hat = v1 / (1.0 - ADAM_B2 ** ADAM_STEP)
        outs[0][...] = g
        outs[1][...] = (-ADAM_LR) * (m_hat / (jnp.sqrt(v_hat) + ADAM_EPS) + ADAM_WD * ins[0][...])
        outs[2][...] = m1
        outs[3][...] = v1

    rows = [w, m, v, p] + ([] if q is None else [q])
    return _rowwise(fn, rows, [], [(w.shape[1], F32)] * 4, tm=256, name=name)


def _put_cols(shard, me):
    full = jnp.zeros((shard.shape[0], D_MODEL), F32)
    return lax.dynamic_update_slice(full, shard, (0, me * (D_MODEL // N_CHIPS)))


def _gate_vec_slot(b_a, b_x, lam):
    return _rows_at(b_a, _ROW_BA) + _rows_at(b_x, _ROW_BX) + _rows_at(lam, _ROW_LAM)


def _pack_small(p, me):
    return jnp.concatenate([
        _rows_at(p["norm_mix_g"], 0) + _rows_at(p["norm_mlp_g"], 2) + _rows_at(p["final_g"][None], 4),
        _rows_at(_put_cols(p["rg_conv_w"][0, :, 0, :], me), 0) + _rows_at(p["rg_conv_b"], 4),
        _gate_vec_slot(_put_cols(p["rg_b_a"][0], me), _put_cols(p["rg_b_x"][0], me), _put_cols(p["rg_lam"][0], me)),
        _qk_slot(p["at_q_g"], p["at_k_g"]),
        _make_wcat(p["rg_w_a"], p["rg_w_x"]).reshape(GATE_ROWS, D_MODEL),
    ], axis=0)


def _unpack_small(r, me):
    def cols(rows):
        return lax.dynamic_slice(rows, (0, me * (D_MODEL // N_CHIPS)), (rows.shape[0], D_MODEL // N_CHIPS))

    w_a, w_x = _split_wcat(r[SMALL_ROWS:])
    gate = r[16:24]
    return dict(
        norm_mix_g=r[0:2], norm_mlp_g=r[2:4], final_g=r[4], rg_conv_w=cols(r[8:12])[None, :, None, :],
        rg_conv_b=r[12:13], rg_b_a=cols(gate[_ROW_BA:_ROW_BA + 2])[None], rg_b_x=cols(gate[_ROW_BX:_ROW_BX + 2])[None],
        rg_lam=cols(gate[_ROW_LAM:_ROW_LAM + 2])[None], at_q_g=r[24:25, 0:HEAD_DIM],
        at_k_g=r[24:25, HEAD_DIM:2 * HEAD_DIM], rg_w_a=w_a, rg_w_x=w_x)


_WEIGHTS = ['norm_mix_g', 'norm_mlp_g', 'rg_w_in', 'rg_conv_w', 'rg_conv_b', 'rg_w_a', 'rg_b_a', 'rg_w_x', 'rg_b_x',
            'rg_lam', 'rg_w_out', 'at_w_qkv', 'at_q_g', 'at_k_g', 'at_w_o', 'mlp_w_up', 'mlp_w_down', 'final_g']
_BIG = dict(rg_w_in=["rg_w_in"], rg_w_out=["rg_w_out"], at_w_qkv=["at_w_qkv"], at_w_o=["at_w_o"],
            mlp_w_up=["up0", "up1"], mlp_w_down=["down0", "down1"])


def kernel(x, *args):
    n_w = len(_WEIGHTS)
    w = dict(zip(_WEIGHTS, args[:n_w]))
    target = args[n_w]
    m = dict(zip(_WEIGHTS, args[n_w + 1:2 * n_w + 1]))
    v = dict(zip(_WEIGHTS, args[2 * n_w + 1:3 * n_w + 1]))
    B, L, _ = x.shape
    T = B * L
    me = 2 * lax.axis_index("x") + lax.axis_index("y")

    bf = lambda a: a.astype(BF16)
    vec = jnp.concatenate([_gate_vec_slot(w["rg_b_a"][0], w["rg_b_x"][0], w["rg_lam"][0]),
                           _rows_at(w["rg_conv_w"][0, :, 0, :], 0)], axis=0)
    groups = [("rg", [bf(w["rg_w_in"][0]), bf(w["rg_w_out"][0]), vec]),
              ("mlp0", [bf(w["mlp_w_up"][0]), bf(w["mlp_w_down"][0])]),
              ("att", [bf(w["at_w_qkv"][0]), bf(w["at_w_o"][0])]),
              ("mlp1", [bf(w["mlp_w_up"][1]), bf(w["mlp_w_down"][1])])]
    gathers, tok = {}, None
    for group, shards in groups:
        lands = [lax.dynamic_update_slice(lax.empty((N_CHIPS,) + s.shape, s.dtype), s[None], (me,) + (0,) * s.ndim)
                 for s in shards]
        gathers[group], tok = _exchange_start("gather", shards, lands, f"gather_{group}_start", after=tok)
    wcat = bf(_make_wcat(w["rg_w_a"], w["rg_w_x"]))

    def fetch(group, after):
        _, full = _exchange_wait("gather", gathers[group], after, f"gather_{group}_wait")
        if group == "rg":
            vec_full = jnp.transpose(full[2], (1, 0, 2)).reshape(2 * SUBLANES, D_MODEL)
            conv_wb = vec_full[SUBLANES:] + _rows_at(w["rg_conv_b"], 4)
            return full[0], full[1].reshape(D_MODEL, D_MODEL), conv_wb, wcat, vec_full[:SUBLANES]
        if group == "att":
            return full[0], full[1].reshape(D_MODEL, D_MODEL)
        return full[0], full[1].reshape(4 * D_MODEL, D_MODEL)

    names = dict(mlp1=["up1", "down1"], att=["at_w_qkv", "at_w_o"], mlp0=["up0", "down0"], rg_out=["rg_w_out"],
                 rg_in=["rg_w_in"], small=["small"])
    scatters, swaps, P, Q, res = {}, [], {}, {}, {}

    def start_scatter(group, grads):
        srcs = [g.reshape(N_CHIPS, -1, g.shape[-1]) for g in grads]
        lands = [lax.empty((N_CHIPS - 1,) + s.shape[1:], s.dtype) for s in srcs]
        scatters[group], token = _exchange_start("scatter", srcs, lands, f"scatter_{group}_start")
        return token

    def settle(groups, after):
        keys, parts = [], []
        for group in groups:
            srcs, lands = _exchange_wait("scatter", scatters[group], after, f"scatter_{group}_wait")
            for k, s, r in zip(names[group], srcs, lands):
                keys.append(k)
                parts.append(_sum_slots(lax.dynamic_index_in_dim(s, me, 0, keepdims=False), r, f"sum_{k}"))
        handle, token = _exchange_start("swap", parts, [lax.empty(p.shape, F32) for p in parts],
                                        f"swap_{groups[0]}_start")
        swaps.append((keys, handle, f"swap_{groups[0]}_wait"))
        return token

    def finish(after):
        for keys, handle, name in swaps:
            mine, theirs = _exchange_wait("swap", handle, after, name)
            P.update(zip(keys, mine))
            Q.update(zip(keys, theirs))
        swaps.clear()
        last = after
        for k, parts in _BIG.items():
            if k in res or any(p not in P for p in parts):
                continue
            shape = w[k].shape
            two_d = lambda a: a.reshape(-1, shape[-1])
            cat = lambda d: d[parts[0]] if len(parts) == 1 else jnp.concatenate([d[p] for p in parts])
            outs = _adamw(two_d(w[k]), two_d(m[k]), two_d(v[k]), cat(P), cat(Q), f"adamw_{k}")
            res[k] = [o.reshape(shape) for o in outs]
            last = outs[0]
        return last

    def emit(event, arrays):
        if event == "point_attn_done":
            return settle(["mlp1"], arrays[0])
        if event == "point_conv_done":
            return settle(["att", "mlp0", "rg_out"], arrays[0])
        token = start_scatter(event, arrays)
        return finish(token) if event == "rg_in" else token

    P_vec = dict(norm_mix_g=w["norm_mix_g"], norm_mlp_g=w["norm_mlp_g"], final_g=w["final_g"][None],
                 q_g=w["at_q_g"], k_g=w["at_k_g"])
    loss_part, grad_x, small = _local_step(x.reshape(T, D_MODEL), target.reshape(T, D_MODEL), P_vec, fetch, emit,
                                           B, L, after=tok)
    start_scatter("small", [small])
    loss = lax.psum(loss_part, ("x", "y", "c"))

    finish(settle(["rg_in", "small"], grad_x))
    small_piece = _add(P["small"], Q["small"], "small_grad")
    small_full = _all_gather_chips([small_piece], "gather_small")[0].reshape(PACK_ROWS, D_MODEL)
    outs = _adamw(_pack_small(w, me), _pack_small(m, me), _pack_small(v, me), small_full, None, "adamw_small")
    unpacked = [_unpack_small(o, me) for o in outs]
    for k in _WEIGHTS:
        if k not in res:
            res[k] = [u[k] for u in unpacked]

    result = [loss, grad_x.reshape(B, L, D_MODEL)]
    for slot in range(4):
        result += [res[k][slot] for k in _WEIGHTS]
    return tuple(result)
```

```python
import functools
import math

import jax
import jax.numpy as jnp
from jax import lax
from jax.experimental import pallas as pl
from jax.experimental.pallas import tpu as pltpu

F32 = jnp.float32
BF16 = jnp.bfloat16

D_MODEL = 1024
HEAD_DIM = 128
N_HEADS = 8
N_KV = 2
GROUP = N_HEADS // N_KV
LRU_BLOCKS = 8
LRU_BW = 128
GRID_W = 64
ROPE_THETA = 10000.0
EPS = 1e-6
RG_C = 8.0
SCALE = 1.0 / math.sqrt(HEAD_DIM)
N_CHIPS = 4

ADAM_LR = 0.001
ADAM_B1 = 0.9
ADAM_B2 = 0.999
ADAM_EPS = 1e-08
ADAM_WD = 0.01
ADAM_STEP = 10

V7X_VMEM_BYTES = 64 * 1024 * 1024
VMEM_LIMIT = V7X_VMEM_BYTES * 3 // 4
LANES = 128
SUBLANES = 8

SMALL_ROWS = 32
GATE_ROWS = 512
PACK_ROWS = SMALL_ROWS + GATE_ROWS


def _params(sem):
    return pltpu.CompilerParams(dimension_semantics=sem, vmem_limit_bytes=VMEM_LIMIT)


_ANY = pl.BlockSpec(memory_space=pl.ANY)
_NN = (((1,), (0,)), ((), ()))
_NT = (((1,), (1,)), ((), ()))
_TN = (((0,), (0,)), ((), ()))


def _after_operand(after):
    return [] if after is None else [after]


def _fit(t, n):
    if n <= t:
        return n
    c = (t // LANES) * LANES
    while n % c:
        c -= LANES
    return c


MM_VMEM_BUDGET = VMEM_LIMIT * 3 // 4
MM_TN = 1024


def _mm_tile_rows(M, K, tn, out_dtypes, extras):
    per_row = 2 * (2 * K) + 4 * tn + sum(2 * tn * jnp.dtype(d).itemsize for d in out_dtypes)
    per_row += sum(2 * tn * e.dtype.itemsize for e in extras)
    fixed = 2 * (2 * K * tn)
    for tm in (2048, 1024, 512, 256, 128):
        if M % tm == 0 and fixed + tm * per_row <= MM_VMEM_BUDGET:
            return tm
    raise ValueError(f"no row tile fits VMEM for M={M} K={K} tn={tn}")


def _mm(a, b, *, mode, name, out_dtypes=(F32,), b_shard=False, o_shard=False, extras=(), epi=None, after=None,
        bcast=(), accs=(), ref_epi=None):
    if mode == "tn":
        K, M = a.shape
        N = b.shape[1]
    else:
        M, K = a.shape
        if mode == "nn":
            N = b.shape[0] * b.shape[2] if b_shard else b.shape[1]
        else:
            N = b.shape[1] if b_shard else b.shape[0]
    ns = N
    if b_shard and mode == "nn":
        ns = b.shape[2]
    elif o_shard:
        ns = N // N_CHIPS
    tn = _fit(MM_TN, ns)
    tm = _mm_tile_rows(M, K, tn, out_dtypes, extras)
    if ref_epi is not None:
        tm = min(tm, 512)
    grid = (M // tm, N // tn)
    q = ns // tn

    if mode == "tn":
        a_spec = pl.BlockSpec((K, tm), lambda i, j: (0, i))
        b_spec = pl.BlockSpec((K, tn), lambda i, j: (0, j))
        dims = _TN
    elif mode == "nn":
        a_spec = pl.BlockSpec((tm, K), lambda i, j: (i, 0))
        if b_shard:
            b_spec = pl.BlockSpec((None, K, tn), lambda i, j: (j // q, 0, j % q))
        else:
            b_spec = pl.BlockSpec((K, tn), lambda i, j: (0, j))
        dims = _NN
    else:
        a_spec = pl.BlockSpec((tm, K), lambda i, j: (i, 0))
        if b_shard:
            ks = b.shape[2]
            b_spec = pl.BlockSpec((N_CHIPS, tn, ks), lambda i, j: (0, j, 0))
        else:
            b_spec = pl.BlockSpec((tn, K), lambda i, j: (j, 0))
        dims = _NT

    if o_shard:
        o_specs = [pl.BlockSpec((None, tm, tn), lambda i, j: (j // q, i, j % q))]
        o_shapes = [jax.ShapeDtypeStruct((N_CHIPS, M, ns), out_dtypes[0])]
    else:
        o_specs = [pl.BlockSpec((tm, tn), lambda i, j: (i, j)) for _ in out_dtypes]
        o_shapes = [jax.ShapeDtypeStruct((M, N), dt) for dt in out_dtypes]
    e_specs = [pl.BlockSpec((tm, tn), lambda i, j: (i, j)) for _ in extras]
    e_specs += [pl.BlockSpec(v.shape, lambda i, j: (0, 0)) for v in bcast]
    o_specs += [pl.BlockSpec(s, lambda i, j: (0, 0)) for s in accs]
    o_shapes += [jax.ShapeDtypeStruct(s, F32) for s in accs]
    n_e, n_b, n_o, n_a = len(extras), len(bcast), len(out_dtypes), len(accs)
    order = _after_operand(after)
    n_x = len(order)
    if epi is None:
        epi = lambda acc: (acc,)

    def body(a_ref, b_ref, *rest):
        e_refs, b_refs = rest[:n_e], rest[n_e:n_e + n_b]
        o_refs = rest[n_e + n_b + n_x:n_e + n_b + n_x + n_o]
        a_refs = rest[n_e + n_b + n_x + n_o:]
        if n_a:
            @pl.when((pl.program_id(0) == 0) & (pl.program_id(1) == 0))
            def _():
                for r in a_refs:
                    r[...] = jnp.zeros(r.shape, F32)
        if mode == "nt" and b_shard:
            acc = None
            for s in range(N_CHIPS):
                part = lax.dot_general(a_ref[:, s * ks:(s + 1) * ks], b_ref[s], dims, preferred_element_type=F32)
                acc = part if acc is None else acc + part
        else:
            acc = lax.dot_general(a_ref[...], b_ref[...], dims, preferred_element_type=F32)
        if ref_epi is not None:
            ref_epi(acc, e_refs, b_refs, o_refs, a_refs)
            return
        outs = epi(acc, *[r[...] for r in e_refs])
        for r, o in zip(o_refs, outs):
            r[...] = o.astype(r.dtype)

    outs = pl.pallas_call(
        body, name=name, grid=grid, in_specs=[a_spec, b_spec] + e_specs + [_ANY] * n_x, out_specs=o_specs,
        out_shape=o_shapes, compiler_params=_params(("arbitrary", "arbitrary") if n_a else ("parallel", "parallel")),
    )(a, b, *extras, *bcast, *order)
    return outs[0] if n_o + n_a == 1 else outs


def _rowwise(fn, rows, bcast, outs, accs=(), *, tm, name, after=None):
    def norm(r):
        return r if isinstance(r, tuple) else (r, r.shape[1], 0)

    rows = [norm(r) for r in rows]
    T = rows[0][0].shape[0]
    tm = min(tm, T)
    while T % tm:
        tm -= SUBLANES
    n_r, n_b, n_o, n_a = len(rows), len(bcast), len(outs), len(accs)
    order = _after_operand(after)
    n_x = len(order)
    in_specs = [pl.BlockSpec((tm, c), functools.partial(lambda i, cb: (i, cb), cb=cb)) for _, c, cb in rows]
    in_specs += [pl.BlockSpec(b.shape, lambda i: (0, 0)) for b in bcast] + [_ANY] * n_x
    out_specs = [pl.BlockSpec((tm, o[0]), lambda i: (i, 0)) for o in outs]
    out_specs += [pl.BlockSpec(s, lambda i: (0, 0)) for s in accs]
    out_shape = [jax.ShapeDtypeStruct((T, o[2] if len(o) > 2 else o[0]), o[1]) for o in outs]
    out_shape += [jax.ShapeDtypeStruct(s, F32) for s in accs]

    def body(*refs):
        in_refs = refs[:n_r]
        b_refs = refs[n_r:n_r + n_b]
        o_refs = refs[n_r + n_b + n_x:n_r + n_b + n_x + n_o]
        a_refs = refs[n_r + n_b + n_x + n_o:]
        if n_a:
            @pl.when(pl.program_id(0) == 0)
            def _():
                for r in a_refs:
                    r[...] = jnp.zeros(r.shape, F32)
        fn(in_refs, b_refs, o_refs, a_refs)

    res = pl.pallas_call(
        body, name=name, grid=(T // tm,), in_specs=in_specs, out_specs=out_specs, out_shape=out_shape,
        compiler_params=_params(("arbitrary",) if n_a else ("parallel",)),
    )(*[r[0] for r in rows], *bcast, *order)
    return res


def _rsum(x):
    return jnp.sum(x, axis=0, keepdims=True)


def _rms_fwd(x, g, name, after=None):
    def fn(ins, bs, outs, accs):
        xv = ins[0][...]
        r = lax.rsqrt(jnp.mean(xv * xv, axis=-1, keepdims=True) + EPS)
        outs[0][...] = (xv * r * bs[0][...]).astype(BF16)

    return _rowwise(fn, [x], [g], [(D_MODEL, BF16)], tm=512, name=name, after=after)[0]


def _rms_bwd_math(xv, dh, g):
    r = lax.rsqrt(jnp.mean(xv * xv, axis=-1, keepdims=True) + EPS)
    hn = xv * r
    dgh = dh * g
    dx = r * (dgh - hn * jnp.mean(dgh * hn, axis=-1, keepdims=True))
    return dx, _rsum(dh * hn)


def _mm_norm_bwd(dy, w, x, dres, g, name, after=None):
    def epilogue(acc, e_refs, b_refs, o_refs, a_refs):
        dx, dg = _rms_bwd_math(e_refs[0][...], acc, b_refs[0][...])
        dx = dx + e_refs[1][...]
        o_refs[0][...] = dx
        o_refs[1][...] = dx.astype(BF16)
        a_refs[0][...] += dg

    return _mm(dy, w, mode="nt", b_shard=True, out_dtypes=(F32, BF16), extras=(x, dres), bcast=(g,),
               accs=((1, D_MODEL),), ref_epi=epilogue, name=name, after=after)


def _final_loss(x, target, g):
    def fn(ins, bs, outs, accs):
        xv = ins[0][...]
        gv = bs[0][...]
        r = lax.rsqrt(jnp.mean(xv * xv, axis=-1, keepdims=True) + EPS)
        e = xv * r * gv - ins[1][...]
        tok = jnp.mean(e * e, axis=-1, keepdims=True)
        accs[0][...] += 0.5 * jnp.sum(tok, axis=0, keepdims=True) * jnp.ones((1, LANES), F32)
        dx, dg = _rms_bwd_math(xv, e * (1.0 / D_MODEL), gv)
        outs[0][...] = dx
        outs[1][...] = dx.astype(BF16)
        accs[1][...] += dg

    return _rowwise(fn, [x, target], [g], [(D_MODEL, F32), (D_MODEL, BF16)], [(1, LANES), (1, D_MODEL)],
                    tm=256, name="final_loss")


def _relu2(acc):
    r = jnp.maximum(acc, 0.0)
    return r * r, r


def _mlp_fwd(x, g, fetch, tag):
    h = _rms_fwd(x, g, f"mlp{tag}_norm")
    w_up, w_down = fetch(f"mlp{tag}", h)
    a, r = _mm(h, w_up, mode="nn", b_shard=True, out_dtypes=(BF16, BF16), epi=_relu2, name=f"mlp{tag}_up")
    x_out = _mm(a, w_down, mode="nn", extras=(x,), epi=lambda acc, res: (acc + res,), name=f"mlp{tag}_down")
    return x_out, (h, a, r, w_up, w_down)


def _mlp_bwd(x, g, saved, dx, dx_bf, tag, after):
    h, a, r, w_up, w_down = saved
    d_down = _mm(a, dx_bf, mode="tn", out_dtypes=(BF16,), name=f"mlp{tag}_dwdown", after=after)
    dup = _mm(dx_bf, w_down, mode="nt", extras=(r,), out_dtypes=(BF16,),
              epi=lambda acc, rv: (acc * (2.0 * rv.astype(F32)),), name=f"mlp{tag}_dup")
    d_up = _mm(h, dup, mode="tn", o_shard=True, out_dtypes=(BF16,), name=f"mlp{tag}_dwup")
    dx_new, dx_new_bf, dg = _mm_norm_bwd(dup, w_up, x, dx, g, f"mlp{tag}_dh")
    return dx_new, dx_new_bf, dg, d_up, d_down


def _rope_tables(L, B):
    rows = L // GRID_W
    row = jnp.repeat(jnp.arange(rows, dtype=F32), GRID_W)
    col = jnp.tile(jnp.arange(GRID_W, dtype=F32), rows)
    inv = ROPE_THETA ** (-jnp.arange(HEAD_DIM // 4, dtype=F32) / (HEAD_DIM // 4))
    ar, ac = row[:, None] * inv, col[:, None] * inv
    cos = jnp.concatenate([jnp.cos(ar), jnp.cos(ar), jnp.cos(ac), jnp.cos(ac)], axis=-1)
    sin = jnp.concatenate([-jnp.sin(ar), jnp.sin(ar), -jnp.sin(ac), jnp.sin(ac)], axis=-1)
    return jnp.tile(cos, (B, 1)), jnp.tile(sin, (B, 1))


def _swap_halves(x):
    lane = lax.broadcasted_iota(jnp.int32, x.shape, 1)
    return jnp.where((lane % 64) < 32, pltpu.roll(x, HEAD_DIM - 32, 1), pltpu.roll(x, 32, 1))


def _qk_prep(qkv, cos, sin, q_g, k_g):
    def fn(ins, bs, outs, accs):
        c, s = ins[1][...], ins[2][...]
        for h in range(N_HEADS + N_KV):
            xv = ins[0][:, h * HEAD_DIM:(h + 1) * HEAD_DIM]
            g = bs[0][...] if h < N_HEADS else bs[1][...]
            r = lax.rsqrt(jnp.mean(xv * xv, axis=-1, keepdims=True) + EPS)
            z = xv * r * g
            y = (z * c + _swap_halves(z) * s).astype(BF16)
            if h < N_HEADS:
                outs[0][:, h * HEAD_DIM:(h + 1) * HEAD_DIM] = y
            else:
                outs[1][:, (h - N_HEADS) * HEAD_DIM:(h - N_HEADS + 1) * HEAD_DIM] = y
        outs[2][...] = ins[0][:, (N_HEADS + N_KV) * HEAD_DIM:].astype(BF16)

    kvw = N_KV * HEAD_DIM
    return _rowwise(fn, [qkv, cos, sin], [q_g, k_g], [(D_MODEL, BF16), (kvw, BF16), (kvw, BF16)], tm=512,
                    name="attn_qk_prep")


def _qk_prep_bwd(qkv, dq, dk, dv, cos, sin, q_g, k_g):
    def fn(ins, bs, outs, accs):
        c, s = ins[4][...], ins[5][...]
        for h in range(N_HEADS + N_KV):
            sl = slice(h * HEAD_DIM, (h + 1) * HEAD_DIM)
            xv = ins[0][:, sl]
            if h < N_HEADS:
                g, dy, acc = bs[0][...], ins[1][:, sl], accs[0]
            else:
                ks = slice((h - N_HEADS) * HEAD_DIM, (h - N_HEADS + 1) * HEAD_DIM)
                g, dy, acc = bs[1][...], ins[2][:, ks], accs[1]
            r = lax.rsqrt(jnp.mean(xv * xv, axis=-1, keepdims=True) + EPS)
            xn = xv * r
            dz = dy * c - _swap_halves(dy) * s
            acc[...] += _rsum(dz * xn)
            dxn = dz * g
            outs[0][:, sl] = (r * (dxn - xn * jnp.mean(dxn * xn, axis=-1, keepdims=True))).astype(BF16)
        outs[0][:, (N_HEADS + N_KV) * HEAD_DIM:] = ins[3][...].astype(BF16)

    return _rowwise(fn, [qkv, dq, dk, dv, cos, sin], [q_g, k_g], [(qkv.shape[1], BF16)],
                    [(1, HEAD_DIM), (1, HEAD_DIM)], tm=256, name="attn_qk_prep_bwd")


_EXP2_SCALE = SCALE * math.log2(math.e)


def _exp_rows(q, k):
    s = lax.dot_general(q, k, _NT, preferred_element_type=F32)
    p = jnp.exp2((s - jnp.max(s, axis=-1, keepdims=True)) * _EXP2_SCALE)
    return p, jnp.sum(p, axis=-1, keepdims=True)


def _attn_fwd(q, k, v, B, L, tq=1024, sub=256):
    tq = min(tq, L)
    sub = min(sub, tq)
    nq = L // tq

    def body(q_ref, k_ref, v_ref, o_ref):
        kv, vv = k_ref[...], v_ref[...]
        for c in range(tq // sub):
            rows = slice(c * sub, (c + 1) * sub)
            p, l = _exp_rows(q_ref[rows, :], kv)
            o = jnp.dot(p.astype(BF16), vv, preferred_element_type=F32)
            o_ref[rows, :] = (o * (1.0 / l)).astype(o_ref.dtype)

    return pl.pallas_call(
        body, name="attn_fwd", grid=(B, N_HEADS, nq),
        in_specs=[pl.BlockSpec((tq, HEAD_DIM), lambda b, h, i: (b * nq + i, h)),
                  pl.BlockSpec((L, HEAD_DIM), lambda b, h, i: (b, h // GROUP)),
                  pl.BlockSpec((L, HEAD_DIM), lambda b, h, i: (b, h // GROUP))],
        out_specs=pl.BlockSpec((tq, HEAD_DIM), lambda b, h, i: (b * nq + i, h)),
        out_shape=jax.ShapeDtypeStruct((B * L, D_MODEL), BF16),
        compiler_params=_params(("parallel", "parallel", "parallel")),
    )(q, k, v)


def _attn_bwd(q, k, v, do, B, L, tq=512, sub=256):
    tq = min(tq, L)
    sub = min(sub, tq)
    nq = L // tq

    def body(q_ref, k_ref, v_ref, do_ref, dq_ref, dk_ref, dv_ref):
        @pl.when((pl.program_id(2) == 0) & (pl.program_id(3) == 0))
        def _():
            dk_ref[...] = jnp.zeros(dk_ref.shape, F32)
            dv_ref[...] = jnp.zeros(dv_ref.shape, F32)

        kv, vv = k_ref[...], v_ref[...]
        ps, es, dos, qs = [], [], [], []
        for c in range(tq // sub):
            rows = slice(c * sub, (c + 1) * sub)
            qc, doc = q_ref[rows, :], do_ref[rows, :]
            p, l = _exp_rows(qc, kv)
            inv = 1.0 / l
            dp = lax.dot_general(doc, vv, _NT, preferred_element_type=F32)
            delta = jnp.sum(p * dp, axis=-1, keepdims=True) * inv
            e = (p * (dp - delta)).astype(BF16)
            dq_ref[rows, :] = jnp.dot(e, kv, preferred_element_type=F32) * (inv * SCALE)
            ps.append(p.astype(BF16))
            es.append(e)
            dos.append((doc.astype(F32) * inv).astype(BF16))
            qs.append((qc.astype(F32) * (inv * SCALE)).astype(BF16))
        cat = lambda xs: xs[0] if len(xs) == 1 else jnp.concatenate(xs, axis=0)
        dv_ref[...] += lax.dot_general(cat(ps), cat(dos), _TN, preferred_element_type=F32)
        dk_ref[...] += lax.dot_general(cat(es), cat(qs), _TN, preferred_element_type=F32)

    qmap = lambda b, kh, g, i: (b * nq + i, kh * GROUP + g)
    kmap = lambda b, kh, g, i: (b, kh)
    kvw = N_KV * HEAD_DIM
    return pl.pallas_call(
        body, name="attn_bwd", grid=(B, N_KV, GROUP, nq),
        in_specs=[pl.BlockSpec((tq, HEAD_DIM), qmap), pl.BlockSpec((L, HEAD_DIM), kmap),
                  pl.BlockSpec((L, HEAD_DIM), kmap), pl.BlockSpec((tq, HEAD_DIM), qmap)],
        out_specs=[pl.BlockSpec((tq, HEAD_DIM), qmap), pl.BlockSpec((L, HEAD_DIM), kmap),
                   pl.BlockSpec((L, HEAD_DIM), kmap)],
        out_shape=[jax.ShapeDtypeStruct((B * L, D_MODEL), F32), jax.ShapeDtypeStruct((B * L, kvw), F32),
                   jax.ShapeDtypeStruct((B * L, kvw), F32)],
        compiler_params=_params(("parallel", "parallel", "arbitrary", "arbitrary")),
    )(q, k, v, do)


def _conv_shift(x, t, L, k):
    if k == 2:
        return x
    if k < 2:
        return jnp.where(t >= 2 - k, pltpu.roll(x, 2 - k, 0), 0.0)
    return jnp.where(t < L - (k - 2), pltpu.roll(x, L - (k - 2), 0), 0.0)


def _conv_fwd(z, wb, B, L, tc=256):
    noff = D_MODEL // tc

    def body(z_ref, w_ref, o_ref):
        x = z_ref[...]
        t = lax.broadcasted_iota(jnp.int32, x.shape, 0)
        acc = w_ref[4:5, :] + w_ref[2:3, :] * x
        for k in (0, 1, 3):
            acc = acc + w_ref[k:k + 1, :] * _conv_shift(x, t, L, k)
        o_ref[...] = acc

    return pl.pallas_call(
        body, name="rg_conv", grid=(B, noff),
        in_specs=[pl.BlockSpec((L, tc), lambda b, j: (b, noff + j)), pl.BlockSpec((SUBLANES, tc), lambda b, j: (0, j))],
        out_specs=pl.BlockSpec((L, tc), lambda b, j: (b, j)),
        out_shape=jax.ShapeDtypeStruct((B * L, D_MODEL), F32),
        compiler_params=_params(("parallel", "parallel")),
    )(z, wb)


def _conv_bwd(z, g, wb, dz, B, L, tc=256):
    noff = D_MODEL // tc

    def body(z_ref, g_ref, w_ref, dz_in, dx_ref, dw_ref):
        @pl.when(pl.program_id(1) == 0)
        def _():
            dw_ref[...] = jnp.zeros(dw_ref.shape, F32)

        x, gv = z_ref[...], g_ref[...]
        t = lax.broadcasted_iota(jnp.int32, x.shape, 0)
        dx = w_ref[2:3, :] * gv
        for k in (0, 1, 3):
            dx = dx + w_ref[k:k + 1, :] * _conv_shift(gv, t, L, 4 - k)
        dx_ref[...] = dx.astype(BF16)
        for k in range(4):
            dw_ref[k:k + 1, :] += _rsum(_conv_shift(x, t, L, k) * gv)
        dw_ref[4:5, :] += _rsum(gv)

    return pl.pallas_call(
        body, name="rg_conv_bwd", grid=(noff, B),
        in_specs=[pl.BlockSpec((L, tc), lambda j, b: (b, noff + j)), pl.BlockSpec((L, tc), lambda j, b: (b, j)),
                  pl.BlockSpec((SUBLANES, tc), lambda j, b: (0, j)), _ANY],
        out_specs=[pl.BlockSpec((L, tc), lambda j, b: (b, noff + j)),
                   pl.BlockSpec((SUBLANES, tc), lambda j, b: (0, j))],
        out_shape=[jax.ShapeDtypeStruct(dz.shape, dz.dtype), jax.ShapeDtypeStruct((SUBLANES, D_MODEL), F32)],
        input_output_aliases={3: 0},
        compiler_params=_params(("parallel", "arbitrary")),
    )(z, g, wb, dz)


def _softplus(x):
    return jnp.maximum(x, 0.0) + jnp.log1p(jnp.exp(-jnp.abs(x)))


_ROW_BA, _ROW_BX, _ROW_LAM = 0, 2, 4


def _gate_math(xb, pre, vec_ref, d, sl):
    pa = pre[:, (2 * d) * LRU_BW:(2 * d + 1) * LRU_BW] + vec_ref[_ROW_BA + d:_ROW_BA + d + 1, sl]
    px = pre[:, (2 * d + 1) * LRU_BW:(2 * d + 2) * LRU_BW] + vec_ref[_ROW_BX + d:_ROW_BX + d + 1, sl]
    r = jax.nn.sigmoid(pa)
    i = jax.nn.sigmoid(px)
    sp = _softplus(-vec_ref[_ROW_LAM + d:_ROW_LAM + d + 1, sl])
    log_a = (-RG_C) * r * sp
    a = jnp.exp(log_a)
    th = jnp.tanh(log_a)
    om = -2.0 * th / (1.0 - th)
    mult = jnp.sqrt(om)
    return a, mult * (i * xb), (r, i, sp, om, mult)


def _gate_fwd(rec, wcat, gvec):
    def fn(ins, bs, outs, accs):
        for blk in range(LRU_BLOCKS):
            sl = slice(blk * LRU_BW, (blk + 1) * LRU_BW)
            xb = ins[0][:, sl]
            pre = jnp.dot(xb.astype(BF16), bs[0][sl, :], preferred_element_type=F32)
            for d in range(2):
                a, u, _ = _gate_math(xb, pre, bs[1], d, sl)
                outs[2 * d][:, sl] = a
                outs[2 * d + 1][:, sl] = u

    return _rowwise(fn, [rec], [wcat, gvec], [(D_MODEL, F32)] * 4, tm=256, name="rg_gate")


def _gate_bwd(rec, du_f, da_f, du_b, da_b, wcat, gvec):
    def fn(ins, bs, outs, accs):
        for blk in range(LRU_BLOCKS):
            sl = slice(blk * LRU_BW, (blk + 1) * LRU_BW)
            xb = ins[0][:, sl]
            xb16 = xb.astype(BF16)
            w = bs[0][sl, :]
            pre = jnp.dot(xb16, w, preferred_element_type=F32)
            dx = jnp.zeros_like(xb)
            dpre = []
            for d in range(2):
                a, _, (r, i, sp, om, mult) = _gate_math(xb, pre, bs[1], d, sl)
                du, da = ins[1 + 2 * d][:, sl], ins[2 + 2 * d][:, sl]
                d_i = du * mult * xb
                d_mult = du * i * xb
                dx = dx + du * mult * i
                dlog = da * a - d_mult * (1.0 - om) / mult
                d_r = dlog * ((-RG_C) * sp)
                d_sp = _rsum(dlog * ((-RG_C) * r))
                lam = bs[1][_ROW_LAM + d:_ROW_LAM + d + 1, sl]
                accs[1][_ROW_LAM + d:_ROW_LAM + d + 1, sl] += d_sp * (-jax.nn.sigmoid(-lam))
                dpa = d_r * r * (1.0 - r)
                dpx = d_i * i * (1.0 - i)
                accs[1][_ROW_BA + d:_ROW_BA + d + 1, sl] += _rsum(dpa)
                accs[1][_ROW_BX + d:_ROW_BX + d + 1, sl] += _rsum(dpx)
                dpre += [dpa, dpx]
            dpre = jnp.concatenate(dpre, axis=1).astype(BF16)
            accs[0][sl, :] += lax.dot_general(xb16, dpre, _TN, preferred_element_type=F32)
            outs[0][:, sl] = dx + lax.dot_general(dpre, w, _NT, preferred_element_type=F32)

    return _rowwise(fn, [rec, du_f, da_f, du_b, da_b], [wcat, gvec], [(D_MODEL, F32)],
                    [(D_MODEL, 4 * LRU_BW), (SUBLANES, D_MODEL)], tm=256, name="rg_gate_bwd")


def _as_time_blocks(x):
    return x.reshape(x.shape[0] // SUBLANES, SUBLANES, x.shape[1])


def _scan_call(body, ins, n_out, B, L, tc, name):
    nb = L // SUBLANES
    spec = pl.BlockSpec((nb, SUBLANES, tc), lambda b, j: (b, 0, j))
    T = ins[0].shape[0]
    outs = pl.pallas_call(
        functools.partial(body, nb), name=name, grid=(B, D_MODEL // tc),
        in_specs=[spec] * len(ins), out_specs=[spec] * n_out,
        out_shape=[jax.ShapeDtypeStruct((T // SUBLANES, SUBLANES, D_MODEL), F32)] * n_out,
        compiler_params=_params(("parallel", "parallel")),
    )(*[_as_time_blocks(x) for x in ins])
    return [o.reshape(T, D_MODEL) for o in outs]


def _scan_fwd(a_f, u_f, a_b, u_b, B, L, tc=256):
    def body(nb, af, uf, ab, ub, hf, hb):
        def step(i, carry):
            h1, h2 = carry
            ib = nb - 1 - i
            for j in range(SUBLANES):
                jb = SUBLANES - 1 - j
                h1 = af[i, j:j + 1, :] * h1 + uf[i, j:j + 1, :]
                hf[i, j:j + 1, :] = h1
                h2 = ab[ib, jb:jb + 1, :] * h2 + ub[ib, jb:jb + 1, :]
                hb[ib, jb:jb + 1, :] = h2
            return h1, h2

        zero = jnp.zeros((1, tc), F32)
        lax.fori_loop(0, nb, step, (zero, zero))

    return _scan_call(body, [a_f, u_f, a_b, u_b], 2, B, L, tc, "rg_scan")


def _scan_bwd(dy, a_f, h_f, a_b, h_b, B, L, tc=256):
    def body(nb, dy_r, af, hf, ab, hb, duf, daf, dub, dab):
        def step(i, carry):
            c1, c2 = carry
            ir = nb - 1 - i
            for j in range(SUBLANES):
                jr = SUBLANES - 1 - j
                lam1 = dy_r[ir, jr:jr + 1, :] + c1
                if jr > 0:
                    prev = hf[ir, jr - 1:jr, :]
                else:
                    prev = hf[jnp.maximum(ir - 1, 0), SUBLANES - 1:SUBLANES, :] * (ir > 0).astype(F32)
                duf[ir, jr:jr + 1, :] = lam1
                daf[ir, jr:jr + 1, :] = lam1 * prev
                c1 = af[ir, jr:jr + 1, :] * lam1
                lam2 = dy_r[i, j:j + 1, :] + c2
                if j < SUBLANES - 1:
                    nxt = hb[i, j + 1:j + 2, :]
                else:
                    nxt = hb[jnp.minimum(i + 1, nb - 1), 0:1, :] * (i < nb - 1).astype(F32)
                dub[i, j:j + 1, :] = lam2
                dab[i, j:j + 1, :] = lam2 * nxt
                c2 = ab[i, j:j + 1, :] * lam2
            return c1, c2

        zero = jnp.zeros((1, tc), F32)
        lax.fori_loop(0, nb, step, (zero, zero))

    return _scan_call(body, [dy, a_f, h_f, a_b, h_b], 4, B, L, tc, "rg_scan_bwd")


_GELU_C = math.sqrt(2.0 / math.pi)


def _gelu_parts(x):
    th = jnp.tanh(_GELU_C * (x + 0.044715 * x * x * x))
    return 0.5 * x * (1.0 + th), th


def _gated_out(h_f, h_b, z):
    def fn(ins, bs, outs, accs):
        gl, _ = _gelu_parts(ins[2][...])
        outs[0][...] = ((ins[0][...] + ins[1][...]) * gl).astype(BF16)

    return _rowwise(fn, [h_f, h_b, (z, D_MODEL, 0)], [], [(D_MODEL, BF16)], tm=512, name="rg_gated_out")[0]


def _gated_out_bwd(dyg, h_f, h_b, z):
    def fn(ins, bs, outs, accs):
        x = ins[3][...]
        gl, th = _gelu_parts(x)
        dgl = 0.5 * (1.0 + th) + 0.5 * x * (1.0 - th * th) * (_GELU_C * (1.0 + 3.0 * 0.044715 * x * x))
        g = ins[0][...]
        outs[0][...] = g * gl
        outs[1][...] = (g * (ins[1][...] + ins[2][...]) * dgl).astype(BF16)

    return _rowwise(fn, [dyg, h_f, h_b, (z, D_MODEL, 0)], [], [(D_MODEL, F32), (D_MODEL, BF16, 2 * D_MODEL)], tm=512,
                    name="rg_gated_out_bwd")


def _make_wcat(w_a, w_x):
    g = jnp.stack([w_a[0, 0], w_x[0, 0], w_a[0, 1], w_x[0, 1]])
    return jnp.transpose(g, (1, 2, 0, 3)).reshape(D_MODEL, 4 * LRU_BW)


def _split_wcat(rows):
    g = jnp.transpose(rows.reshape(LRU_BLOCKS, LRU_BW, 4, LRU_BW), (2, 0, 1, 3))
    return jnp.stack([g[0], g[2]])[None], jnp.stack([g[1], g[3]])[None]


def _rows_at(part, first):
    return jnp.pad(part, ((first, SUBLANES - first - part.shape[0]), (0, 0)))


def _qk_slot(q_g, k_g):
    wide = lambda v, at: jnp.pad(v, ((0, SUBLANES - 1), (at, D_MODEL - at - HEAD_DIM)))
    return wide(q_g, 0) + wide(k_g, HEAD_DIM)


def _local_step(x, target, P, fetch, emit, B, L, after=None):
    g_mix, g_mlp = P["norm_mix_g"], P["norm_mlp_g"]
    h0 = _rms_fwd(x, g_mix[0:1], "rg_norm", after=after)
    w_in, w_out, conv_wb, wcat, gvec = fetch("rg", h0)
    z = _mm(h0, w_in, mode="nn", b_shard=True, name="rg_in")
    rec = _conv_fwd(z, conv_wb, B, L)
    a_f, u_f, a_b, u_b = _gate_fwd(rec, wcat, gvec)
    h_f, h_b = _scan_fwd(a_f, u_f, a_b, u_b, B, L)
    yg = _gated_out(h_f, h_b, z)
    x1 = _mm(yg, w_out, mode="nn", extras=(x,), epi=lambda acc, res: (acc + res,), name="rg_out")
    x2, mlp0 = _mlp_fwd(x1, g_mlp[0:1], fetch, 0)
    h3 = _rms_fwd(x2, g_mix[1:2], "attn_norm")
    w_qkv, w_o = fetch("att", h3)
    qkv = _mm(h3, w_qkv, mode="nn", b_shard=True, name="attn_qkv")
    cos, sin = _rope_tables(L, B)
    qh, kh, vh = _qk_prep(qkv, cos, sin, P["q_g"], P["k_g"])
    o = _attn_fwd(qh, kh, vh, B, L)
    x3 = _mm(o, w_o, mode="nn", extras=(x2,), epi=lambda acc, res: (acc + res,), name="attn_out")
    x4, mlp1 = _mlp_fwd(x3, g_mlp[1:2], fetch, 1)
    dx4, dx4_bf, loss_acc, d_final_g = _final_loss(x4, target, P["final_g"])

    dx3, dx3_bf, dg_mlp1, d_up1, d_down1 = _mlp_bwd(x3, g_mlp[1:2], mlp1, dx4, dx4_bf, 1, None)
    tok = emit("mlp1", [d_up1, d_down1])
    d_wo = _mm(o, dx3_bf, mode="tn", out_dtypes=(BF16,), name="attn_dwo", after=tok)
    do = _mm(dx3_bf, w_o, mode="nt", out_dtypes=(BF16,), name="attn_do")
    dq, dk, dv = _attn_bwd(qh, kh, vh, do, B, L)
    dqkv, dq_g, dk_g = _qk_prep_bwd(qkv, dq, dk, dv, cos, sin, P["q_g"], P["k_g"])
    d_wqkv = _mm(h3, dqkv, mode="tn", o_shard=True, out_dtypes=(BF16,), name="attn_dwqkv")
    tok = emit("att", [d_wqkv, d_wo])
    dx2, dx2_bf, dg_mix1 = _mm_norm_bwd(dqkv, w_qkv, x2, dx3, g_mix[1:2], "attn_dh", after=tok)
    tok = emit("point_attn_done", [dx2_bf])
    dx1, dx1_bf, dg_mlp0, d_up0, d_down0 = _mlp_bwd(x1, g_mlp[0:1], mlp0, dx2, dx2_bf, 0, tok)
    tok = emit("mlp0", [d_up0, d_down0])
    d_wout = _mm(yg, dx1_bf, mode="tn", out_dtypes=(BF16,), name="rg_dwout", after=tok)
    tok = emit("rg_out", [d_wout])
    dyg = _mm(dx1_bf, w_out, mode="nt", name="rg_dyg", after=tok)
    dy, dgate = _gated_out_bwd(dyg, h_f, h_b, z)
    du_f, da_f, du_b, da_b = _scan_bwd(dy, a_f, h_f, a_b, h_b, B, L)
    drec_c, d_wcat, d_gvec = _gate_bwd(rec, du_f, da_f, du_b, da_b, wcat, gvec)
    dz, d_convwb = _conv_bwd(z, drec_c, conv_wb, dgate, B, L)
    tok = emit("point_conv_done", [dz])
    d_win = _mm(h0, dz, mode="tn", o_shard=True, out_dtypes=(BF16,), name="rg_dwin", after=tok)
    tok = emit("rg_in", [d_win])
    grad_x, _, dg_mix0 = _mm_norm_bwd(dz, w_in, x, dx1, g_mix[0:1], "rg_dh", after=tok)

    norms = (_rows_at(dg_mix0, 0) + _rows_at(dg_mix1, 1) + _rows_at(dg_mlp0, 2) + _rows_at(dg_mlp1, 3)
             + _rows_at(d_final_g, 4))
    small = jnp.concatenate([norms, d_convwb, d_gvec, _qk_slot(dq_g, dk_g), d_wcat.reshape(GATE_ROWS, D_MODEL)],
                            axis=0)
    return loss_acc[0, 0], grad_x, small


_MESH = pl.DeviceIdType.MESH


def _place():
    x, y, c = lax.axis_index("x"), lax.axis_index("y"), lax.axis_index("c")
    peers = [((1 - x) if j & 2 else x, (1 - y) if j & 1 else y) for j in (1, 2, 3)]
    return x, y, c, peers


def _comm_call(body, ins, out_shapes, n_sem, name):
    return pl.pallas_call(
        body, name=name, in_specs=[_ANY] * len(ins), out_specs=[_ANY] * len(out_shapes), out_shape=out_shapes,
        scratch_shapes=[pltpu.SemaphoreType.DMA((n_sem,)), pltpu.SemaphoreType.DMA((n_sem,)),
                        pltpu.SemaphoreType.DMA((len(ins),))],
    )(*ins)


def _all_gather_chips(shards, name):
    n = len(shards)

    def body(*refs):
        ins, outs = refs[:n], refs[n:2 * n]
        send, recv, lsem = refs[2 * n:]
        x, y, c, peers = _place()
        me = 2 * x + y

        def copy(a, j, slot):
            px, py = peers[j]
            return pltpu.make_async_remote_copy(
                src_ref=ins[a], dst_ref=outs[a].at[slot], send_sem=send.at[3 * a + j], recv_sem=recv.at[3 * a + j],
                device_id=(px, py, c), device_id_type=_MESH)

        local = [pltpu.make_async_copy(ins[a], outs[a].at[me], lsem.at[a]) for a in range(n)]
        sends = [copy(a, j, me) for a in range(n) for j in range(3)]
        for cp in local + sends:
            cp.start()
        for a in range(n):
            for j, (px, py) in enumerate(peers):
                copy(a, j, 2 * px + py).wait_recv()
        for cp in sends:
            cp.wait_send()
        for cp in local:
            cp.wait()

    shapes = [jax.ShapeDtypeStruct((N_CHIPS,) + s.shape, s.dtype) for s in shards]
    return _comm_call(body, shards, shapes, 3 * n, name)


_HBM = pl.BlockSpec(memory_space=pltpu.HBM)
_SEM = pl.BlockSpec(memory_space=pltpu.SEMAPHORE)
_EFFECT = pltpu.SideEffectType.DATAFLOW_SIDE_EFFECTING


_COPIES = dict(gather=N_CHIPS - 1, scatter=N_CHIPS - 1, swap=1)


def _split_copies(kind, srcs, lands, send, recv):
    x, y, c, peers = _place()
    me = 2 * x + y
    per = _COPIES[kind]
    out = []
    for a in range(len(lands)):
        for j in range(per):
            if kind == "swap":
                src, there, here, dev = srcs[a], lands[a], lands[a], (x, y, 1 - c)
            else:
                px, py = peers[j]
                dev = (px, py, c)
                if kind == "gather":
                    src, there, here = lands[a].at[me], lands[a].at[me], lands[a].at[2 * px + py]
                else:
                    src, there, here = srcs[a].at[2 * px + py], lands[a].at[j], lands[a].at[j]
            mk = functools.partial(
                pltpu.make_async_remote_copy, src_ref=src, send_sem=send.at[per * a + j],
                recv_sem=recv.at[per * a + j], device_id=dev, device_id_type=_MESH)
            out.append((functools.partial(mk, dst_ref=there), functools.partial(mk, dst_ref=here)))
    return out


def _exchange_start(kind, srcs, lands, name, after=None):
    arrays = list(srcs) + list(lands)
    n_s, n, n_all = len(srcs), len(lands), len(srcs) + len(lands)
    n_sem = _COPIES[kind] * n
    order = _after_operand(after)
    n_x = len(order)

    def body(*refs):
        send, recv = refs[n_all + n_x], refs[n_all + n_x + 1]
        token = refs[-1]
        for started, _ in _split_copies(kind, refs[:n_s], refs[n_s:n_all], send, recv):
            started().start()
        token[...] = jnp.zeros(token.shape, F32)

    res = pl.pallas_call(
        body, name=name,
        out_shape=(pltpu.SemaphoreType.DMA((n_sem,)), pltpu.SemaphoreType.DMA((n_sem,)),
                   *[pltpu.HBM(a.shape, a.dtype) for a in arrays], jax.ShapeDtypeStruct((SUBLANES, LANES), F32)),
        in_specs=[_HBM] * n_all + [_ANY] * n_x,
        out_specs=(_SEM, _SEM, *[_HBM] * n_all, pl.BlockSpec(memory_space=pltpu.VMEM)),
        input_output_aliases={i: 2 + i for i in range(n_all)},
        compiler_params=pltpu.CompilerParams(has_side_effects=_EFFECT),
    )(*[pltpu.with_memory_space_constraint(a, pltpu.HBM) for a in arrays], *order)
    return (res[0], res[1], res[2:2 + n_s], res[2 + n_s:2 + n_all]), res[-1]


def _exchange_wait(kind, handle, after, name):
    send, recv, srcs, lands = handle
    arrays = list(srcs) + list(lands)
    n_s, n_all = len(srcs), len(arrays)

    def body(*refs):
        for started, landing in _split_copies(kind, refs[:n_s], refs[n_s:n_all], refs[n_all], refs[n_all + 1]):
            started().wait_send()
            landing().wait_recv()

    res = pl.pallas_call(
        body, name=name, out_shape=[pltpu.HBM(a.shape, a.dtype) for a in arrays],
        in_specs=[_HBM] * n_all + [_SEM, _SEM, _ANY], out_specs=[_HBM] * n_all,
        input_output_aliases={i: i for i in range(n_all)},
        compiler_params=pltpu.CompilerParams(has_side_effects=_EFFECT),
    )(*arrays, send, recv, after)
    return res[:n_s], res[n_s:]


def _index_operand(i):
    return jnp.reshape(i, (1,)).astype(jnp.int32)


def _cast_into_slot(src, row0, rows, me, dtype, name, after=None):
    cols = src.shape[1]
    tm = min(512, rows)
    order = _after_operand(after)

    def body(me_ref, x_ref, *rest):
        rest[-1][...] = x_ref[...].astype(dtype)

    return pl.pallas_call(
        body, name=name,
        grid_spec=pltpu.PrefetchScalarGridSpec(
            num_scalar_prefetch=1, grid=(rows // tm,),
            in_specs=[pl.BlockSpec((tm, cols), lambda i, me_ref: (i + row0 // tm, 0))] + [_ANY] * len(order),
            out_specs=pl.BlockSpec((None, tm, cols), lambda i, me_ref: (me_ref[0], i, 0))),
        out_shape=jax.ShapeDtypeStruct((N_CHIPS, rows, cols), dtype), compiler_params=_params(("parallel",)),
    )(_index_operand(me), src, *order)


def _sum_slots(mine, r, me, name):
    _, rows, cols = r.shape
    tm = min(512, rows)

    def body(me_ref, own_ref, r_ref, o_ref):
        o_ref[...] = ((own_ref[...].astype(F32) + r_ref[0].astype(F32)) + r_ref[1].astype(F32)) + r_ref[2].astype(F32)

    return pl.pallas_call(
        body, name=name,
        grid_spec=pltpu.PrefetchScalarGridSpec(
            num_scalar_prefetch=1, grid=(rows // tm,),
            in_specs=[pl.BlockSpec((None, tm, cols), lambda i, me_ref: (me_ref[0], i, 0)),
                      pl.BlockSpec((N_CHIPS - 1, tm, cols), lambda i, me_ref: (0, i, 0))],
            out_specs=pl.BlockSpec((tm, cols), lambda i, me_ref: (i, 0))),
        out_shape=jax.ShapeDtypeStruct((rows, cols), F32), compiler_params=_params(("parallel",)),
    )(_index_operand(me), mine, r)


def _add(p, q, name):
    def fn(ins, bs, outs, accs):
        outs[0][...] = ins[0][...] + ins[1][...]

    return _rowwise(fn, [p, q], [], [(p.shape[1], F32)], tm=512, name=name)[0]


def _adamw(w, m, v, p, q, name):
    def fn(ins, bs, outs, accs):
        g = ins[3][...] if q is None else ins[3][...] + ins[4][...]
        m1 = ADAM_B1 * ins[1][...] + (1.0 - ADAM_B1) * g
        v1 = ADAM_B2 * ins[2][...] + (1.0 - ADAM_B2) * (g * g)
        m_hat = m1 / (1.0 - ADAM_B1 ** ADAM_STEP)
        v_hat = v1 / (1.0 - ADAM_B2 ** ADAM_STEP)
        outs[0][...] = g
        outs[1][...] = (-ADAM_LR) * (m_hat / (jnp.sqrt(v_hat) + ADAM_EPS) + ADAM_WD * ins[0][...])
        outs[2][...] = m1
        outs[3][...] = v1

    rows = [w, m, v, p] + ([] if q is None else [q])
    return _rowwise(fn, rows, [], [(w.shape[1], F32)] * 4, tm=256, name=name)


def _put_cols(shard, me):
    full = jnp.zeros((shard.shape[0], D_MODEL), F32)
    return lax.dynamic_update_slice(full, shard, (0, me * (D_MODEL // N_CHIPS)))


def _gate_vec_slot(b_a, b_x, lam):
    return _rows_at(b_a, _ROW_BA) + _rows_at(b_x, _ROW_BX) + _rows_at(lam, _ROW_LAM)


def _pack_small(p, me):
    return jnp.concatenate([
        _rows_at(p["norm_mix_g"], 0) + _rows_at(p["norm_mlp_g"], 2) + _rows_at(p["final_g"][None], 4),
        _rows_at(_put_cols(p["rg_conv_w"][0, :, 0, :], me), 0) + _rows_at(p["rg_conv_b"], 4),
        _gate_vec_slot(_put_cols(p["rg_b_a"][0], me), _put_cols(p["rg_b_x"][0], me), _put_cols(p["rg_lam"][0], me)),
        _qk_slot(p["at_q_g"], p["at_k_g"]),
        _make_wcat(p["rg_w_a"], p["rg_w_x"]).reshape(GATE_ROWS, D_MODEL),
    ], axis=0)


def _unpack_small(r, me):
    def cols(rows):
        return lax.dynamic_slice(rows, (0, me * (D_MODEL // N_CHIPS)), (rows.shape[0], D_MODEL // N_CHIPS))

    w_a, w_x = _split_wcat(r[SMALL_ROWS:])
    gate = r[16:24]
    return dict(
        norm_mix_g=r[0:2], norm_mlp_g=r[2:4], final_g=r[4], rg_conv_w=cols(r[8:12])[None, :, None, :],
        rg_conv_b=r[12:13], rg_b_a=cols(gate[_ROW_BA:_ROW_BA + 2])[None], rg_b_x=cols(gate[_ROW_BX:_ROW_BX + 2])[None],
        rg_lam=cols(gate[_ROW_LAM:_ROW_LAM + 2])[None], at_q_g=r[24:25, 0:HEAD_DIM],
        at_k_g=r[24:25, HEAD_DIM:2 * HEAD_DIM], rg_w_a=w_a, rg_w_x=w_x)


_WEIGHTS = ['norm_mix_g', 'norm_mlp_g', 'rg_w_in', 'rg_conv_w', 'rg_conv_b', 'rg_w_a', 'rg_b_a', 'rg_w_x', 'rg_b_x',
            'rg_lam', 'rg_w_out', 'at_w_qkv', 'at_q_g', 'at_k_g', 'at_w_o', 'mlp_w_up', 'mlp_w_down', 'final_g']
_BIG = dict(rg_w_in=["rg_w_in"], rg_w_out=["rg_w_out"], at_w_qkv=["at_w_qkv"], at_w_o=["at_w_o"],
            mlp_w_up=["up0", "up1"], mlp_w_down=["down0", "down1"])


def kernel(x, *args):
    n_w = len(_WEIGHTS)
    w = dict(zip(_WEIGHTS, args[:n_w]))
    target = args[n_w]
    m = dict(zip(_WEIGHTS, args[n_w + 1:2 * n_w + 1]))
    v = dict(zip(_WEIGHTS, args[2 * n_w + 1:3 * n_w + 1]))
    B, L, _ = x.shape
    T = B * L
    me = 2 * lax.axis_index("x") + lax.axis_index("y")

    vec = jnp.concatenate([_gate_vec_slot(w["rg_b_a"][0], w["rg_b_x"][0], w["rg_lam"][0]),
                           _rows_at(w["rg_conv_w"][0, :, 0, :], 0)], axis=0)
    flat = lambda a: a.reshape(-1, a.shape[-1])
    rows_of = lambda k: w[k].shape[-2]
    groups = [("rg", [("rg_w_in", 0, BF16), ("rg_w_out", 0, BF16), (vec, 0, F32)]),
              ("mlp0", [("mlp_w_up", 0, BF16), ("mlp_w_down", 0, BF16)]),
              ("att", [("at_w_qkv", 0, BF16), ("at_w_o", 0, BF16)]),
              ("mlp1", [("mlp_w_up", 1, BF16), ("mlp_w_down", 1, BF16)])]
    gathers, tok = {}, None
    for group, members in groups:
        lands = []
        for n, (k, layer, dtype) in enumerate(members):
            src, rows = (flat(w[k]), rows_of(k)) if isinstance(k, str) else (k, k.shape[0])
            lands.append(_cast_into_slot(src, layer * rows, rows, me, dtype, f"place_{group}{n}", after=tok))
        gathers[group], tok = _exchange_start("gather", [], lands, f"gather_{group}_start", after=tok)
    wcat = _make_wcat(w["rg_w_a"], w["rg_w_x"]).astype(BF16)

    def fetch(group, after):
        _, full = _exchange_wait("gather", gathers[group], after, f"gather_{group}_wait")
        if group == "rg":
            vec_full = jnp.transpose(full[2], (1, 0, 2)).reshape(2 * SUBLANES, D_MODEL)
            conv_wb = vec_full[SUBLANES:] + _rows_at(w["rg_conv_b"], 4)
            return full[0], full[1].reshape(D_MODEL, D_MODEL), conv_wb, wcat, vec_full[:SUBLANES]
        if group == "att":
            return full[0], full[1].reshape(D_MODEL, D_MODEL)
        return full[0], full[1].reshape(4 * D_MODEL, D_MODEL)

    names = dict(mlp1=["up1", "down1"], att=["at_w_qkv", "at_w_o"], mlp0=["up0", "down0"], rg_out=["rg_w_out"],
                 rg_in=["rg_w_in"], small=["small"])
    scatters, swaps, P, Q, res = {}, [], {}, {}, {}

    def start_scatter(group, grads):
        srcs = [g.reshape(N_CHIPS, -1, g.shape[-1]) for g in grads]
        lands = [lax.empty((N_CHIPS - 1,) + s.shape[1:], s.dtype) for s in srcs]
        scatters[group], token = _exchange_start("scatter", srcs, lands, f"scatter_{group}_start")
        return token

    def settle(groups, after):
        keys, parts = [], []
        for group in groups:
            srcs, lands = _exchange_wait("scatter", scatters[group], after, f"scatter_{group}_wait")
            for k, s, r in zip(names[group], srcs, lands):
                keys.append(k)
                parts.append(_sum_slots(s, r, me, f"sum_{k}"))
        handle, token = _exchange_start("swap", parts, [lax.empty(p.shape, F32) for p in parts],
                                        f"swap_{groups[0]}_start")
        swaps.append((keys, handle, f"swap_{groups[0]}_wait"))
        return token

    def finish(after):
        for keys, handle, name in swaps:
            mine, theirs = _exchange_wait("swap", handle, after, name)
            P.update(zip(keys, mine))
            Q.update(zip(keys, theirs))
        swaps.clear()
        last = after
        for k, parts in _BIG.items():
            if k in res or any(p not in P for p in parts):
                continue
            shape = w[k].shape
            two_d = lambda a: a.reshape(-1, shape[-1])
            cat = lambda d: d[parts[0]] if len(parts) == 1 else jnp.concatenate([d[p] for p in parts])
            outs = _adamw(two_d(w[k]), two_d(m[k]), two_d(v[k]), cat(P), cat(Q), f"adamw_{k}")
            res[k] = [o.reshape(shape) for o in outs]
            last = outs[0]
        return last

    def emit(event, arrays):
        if event == "point_attn_done":
            return settle(["mlp1"], arrays[0])
        if event == "point_conv_done":
            return settle(["att", "mlp0", "rg_out"], arrays[0])
        token = start_scatter(event, arrays)
        return finish(token) if event == "rg_in" else token

    P_vec = dict(norm_mix_g=w["norm_mix_g"], norm_mlp_g=w["norm_mlp_g"], final_g=w["final_g"][None],
                 q_g=w["at_q_g"], k_g=w["at_k_g"])
    loss_part, grad_x, small = _local_step(x.reshape(T, D_MODEL), target.reshape(T, D_MODEL), P_vec, fetch, emit,
                                           B, L, after=tok)
    start_scatter("small", [small])
    loss = lax.psum(loss_part, ("x", "y", "c"))

    finish(settle(["rg_in", "small"], grad_x))
    small_piece = _add(P["small"], Q["small"], "small_grad")
    small_full = _all_gather_chips([small_piece], "gather_small")[0].reshape(PACK_ROWS, D_MODEL)
    outs = _adamw(_pack_small(w, me), _pack_small(m, me), _pack_small(v, me), small_full, None, "adamw_small")
    unpacked = [_unpack_small(o, me) for o in outs]
    for k in _WEIGHTS:
        if k not in res:
            res[k] = [u[k] for u in unpacked]

    result = [loss, grad_x.reshape(B, L, D_MODEL)]
    for slot in range(4):
        result += [res[k][slot] for k in _WEIGHTS]
    return tuple(result)
```

```python
import functools
import math

import jax
import jax.numpy as jnp
import numpy as np
from jax import lax
from jax.experimental import pallas as pl
from jax.experimental.pallas import tpu as pltpu

F32 = jnp.float32
BF16 = jnp.bfloat16

D_MODEL = 1024
HEAD_DIM = 128
N_HEADS = 8
N_KV = 2
GROUP = N_HEADS // N_KV
LRU_BLOCKS = 8
LRU_BW = 128
GRID_W = 64
ROPE_THETA = 10000.0
EPS = 1e-6
RG_C = 8.0
SCALE = 1.0 / math.sqrt(HEAD_DIM)
N_CHIPS = 4

ADAM_LR = 0.001
ADAM_B1 = 0.9
ADAM_B2 = 0.999
ADAM_EPS = 1e-08
ADAM_WD = 0.01
ADAM_STEP = 10

V7X_VMEM_BYTES = 64 * 1024 * 1024
VMEM_LIMIT = V7X_VMEM_BYTES * 3 // 4
LANES = 128
SUBLANES = 8

SMALL_ROWS = 32
GATE_ROWS = 512
PACK_ROWS = SMALL_ROWS + GATE_ROWS


def _params(sem):
    return pltpu.CompilerParams(dimension_semantics=sem, vmem_limit_bytes=VMEM_LIMIT)


_ANY = pl.BlockSpec(memory_space=pl.ANY)
_NN = (((1,), (0,)), ((), ()))
_NT = (((1,), (1,)), ((), ()))
_TN = (((0,), (0,)), ((), ()))


def _after_operand(after):
    return [] if after is None else [after]


def _fit(t, n):
    if n <= t:
        return n
    c = (t // LANES) * LANES
    while n % c:
        c -= LANES
    return c


MM_VMEM_BUDGET = VMEM_LIMIT * 3 // 4
def _mm_tiles(M, K, ns, out_dtypes, extras, whole_rows):
    for tm in (2048, 1024, 512, 256, 128):
        for tn in ((ns,) if whole_rows else (1024, 512, 256)):
            tn = _fit(tn, ns)
            per_row = 2 * (2 * K) + 4 * tn + sum(2 * tn * jnp.dtype(d).itemsize for d in out_dtypes)
            per_row += sum(2 * tn * e.dtype.itemsize for e in extras)
            if M % tm == 0 and 2 * (2 * K * tn) + tm * per_row <= MM_VMEM_BUDGET:
                return tm, tn
    raise ValueError(f"no tile fits VMEM for M={M} K={K} N={ns}")


def _mm(a, b, *, mode, name, out_dtypes=(F32,), b_shard=False, o_shard=False, extras=(), epi=None, after=None,
        bcast=(), accs=(), ref_epi=None):
    if mode == "tn":
        K, M = a.shape
        N = b.shape[1]
    else:
        M, K = a.shape
        if mode == "nn":
            N = b.shape[0] * b.shape[2] if b_shard else b.shape[1]
        else:
            N = b.shape[1] if b_shard else b.shape[0]
    ns = N
    if b_shard and mode == "nn":
        ns = b.shape[2]
    elif o_shard:
        ns = N // N_CHIPS
    tm, tn = _mm_tiles(M, K, ns, out_dtypes, extras, whole_rows=ref_epi is not None)
    if ref_epi is not None:
        tm = min(tm, 512)
    grid = (M // tm, N // tn)
    q = ns // tn

    if mode == "tn":
        a_spec = pl.BlockSpec((K, tm), lambda i, j: (0, i))
        b_spec = pl.BlockSpec((K, tn), lambda i, j: (0, j))
        dims = _TN
    elif mode == "nn":
        a_spec = pl.BlockSpec((tm, K), lambda i, j: (i, 0))
        if b_shard:
            b_spec = pl.BlockSpec((None, K, tn), lambda i, j: (j // q, 0, j % q))
        else:
            b_spec = pl.BlockSpec((K, tn), lambda i, j: (0, j))
        dims = _NN
    else:
        a_spec = pl.BlockSpec((tm, K), lambda i, j: (i, 0))
        if b_shard:
            ks = b.shape[2]
            b_spec = pl.BlockSpec((N_CHIPS, tn, ks), lambda i, j: (0, j, 0))
        else:
            b_spec = pl.BlockSpec((tn, K), lambda i, j: (j, 0))
        dims = _NT

    if o_shard:
        o_specs = [pl.BlockSpec((None, tm, tn), lambda i, j: (j // q, i, j % q))]
        o_shapes = [jax.ShapeDtypeStruct((N_CHIPS, M, ns), out_dtypes[0])]
    else:
        o_specs = [pl.BlockSpec((tm, tn), lambda i, j: (i, j)) for _ in out_dtypes]
        o_shapes = [jax.ShapeDtypeStruct((M, N), dt) for dt in out_dtypes]
    e_specs = [pl.BlockSpec((tm, tn), lambda i, j: (i, j)) for _ in extras]
    e_specs += [pl.BlockSpec(v.shape, lambda i, j: (0, 0)) for v in bcast]
    o_specs += [pl.BlockSpec(s, lambda i, j: (0, 0)) for s in accs]
    o_shapes += [jax.ShapeDtypeStruct(s, F32) for s in accs]
    n_e, n_b, n_o, n_a = len(extras), len(bcast), len(out_dtypes), len(accs)
    order = _after_operand(after)
    n_x = len(order)
    if epi is None:
        epi = lambda acc: (acc,)

    def body(a_ref, b_ref, *rest):
        e_refs, b_refs = rest[:n_e], rest[n_e:n_e + n_b]
        o_refs = rest[n_e + n_b + n_x:n_e + n_b + n_x + n_o]
        a_refs = rest[n_e + n_b + n_x + n_o:]
        if n_a:
            @pl.when((pl.program_id(0) == 0) & (pl.program_id(1) == 0))
            def _():
                for r in a_refs:
                    r[...] = jnp.zeros(r.shape, F32)
        if mode == "nt" and b_shard:
            acc = None
            for s in range(N_CHIPS):
                part = lax.dot_general(a_ref[:, s * ks:(s + 1) * ks], b_ref[s], dims, preferred_element_type=F32)
                acc = part if acc is None else acc + part
        else:
            acc = lax.dot_general(a_ref[...], b_ref[...], dims, preferred_element_type=F32)
        if ref_epi is not None:
            ref_epi(acc, e_refs, b_refs, o_refs, a_refs)
            return
        outs = epi(acc, *[r[...] for r in e_refs])
        for r, o in zip(o_refs, outs):
            r[...] = o.astype(r.dtype)

    outs = pl.pallas_call(
        body, name=name, grid=grid, in_specs=[a_spec, b_spec] + e_specs + [_ANY] * n_x, out_specs=o_specs,
        out_shape=o_shapes, compiler_params=_params(("arbitrary", "arbitrary") if n_a else ("parallel", "parallel")),
    )(a, b, *extras, *bcast, *order)
    return outs[0] if n_o + n_a == 1 else outs


def _rowwise(fn, rows, bcast, outs, accs=(), *, tm, name, after=None):
    def norm(r):
        return r if isinstance(r, tuple) else (r, r.shape[1], 0)

    rows = [norm(r) for r in rows]
    T = rows[0][0].shape[0]
    tm = min(tm, T)
    while T % tm:
        tm -= SUBLANES
    n_r, n_b, n_o, n_a = len(rows), len(bcast), len(outs), len(accs)
    order = _after_operand(after)
    n_x = len(order)
    in_specs = [pl.BlockSpec((tm, c), functools.partial(lambda i, cb: (i, cb), cb=cb)) for _, c, cb in rows]
    in_specs += [pl.BlockSpec(b.shape, lambda i: (0, 0)) for b in bcast] + [_ANY] * n_x
    out_specs = [pl.BlockSpec((tm, o[0]), lambda i: (i, 0)) for o in outs]
    out_specs += [pl.BlockSpec(s, lambda i: (0, 0)) for s in accs]
    out_shape = [jax.ShapeDtypeStruct((T, o[2] if len(o) > 2 else o[0]), o[1]) for o in outs]
    out_shape += [jax.ShapeDtypeStruct(s, F32) for s in accs]

    def body(*refs):
        in_refs = refs[:n_r]
        b_refs = refs[n_r:n_r + n_b]
        o_refs = refs[n_r + n_b + n_x:n_r + n_b + n_x + n_o]
        a_refs = refs[n_r + n_b + n_x + n_o:]
        if n_a:
            @pl.when(pl.program_id(0) == 0)
            def _():
                for r in a_refs:
                    r[...] = jnp.zeros(r.shape, F32)
        fn(in_refs, b_refs, o_refs, a_refs)

    res = pl.pallas_call(
        body, name=name, grid=(T // tm,), in_specs=in_specs, out_specs=out_specs, out_shape=out_shape,
        compiler_params=_params(("arbitrary",) if n_a else ("parallel",)),
    )(*[r[0] for r in rows], *bcast, *order)
    return res


def _rsum(x):
    return jnp.sum(x, axis=0, keepdims=True)


def _rms_fwd(x, g, name, after=None):
    def fn(ins, bs, outs, accs):
        xv = ins[0][...]
        r = lax.rsqrt(jnp.mean(xv * xv, axis=-1, keepdims=True) + EPS)
        outs[0][...] = (xv * r * bs[0][...]).astype(BF16)

    return _rowwise(fn, [x], [g], [(D_MODEL, BF16)], tm=512, name=name, after=after)[0]


def _rms_bwd_math(xv, dh, g):
    r = lax.rsqrt(jnp.mean(xv * xv, axis=-1, keepdims=True) + EPS)
    hn = xv * r
    dgh = dh * g
    dx = r * (dgh - hn * jnp.mean(dgh * hn, axis=-1, keepdims=True))
    return dx, _rsum(dh * hn)


def _mm_norm_bwd(dy, w, x, dres, g, name, after=None):
    def epilogue(acc, e_refs, b_refs, o_refs, a_refs):
        dx, dg = _rms_bwd_math(e_refs[0][...], acc, b_refs[0][...])
        dx = dx + e_refs[1][...]
        o_refs[0][...] = dx
        o_refs[1][...] = dx.astype(BF16)
        a_refs[0][...] += dg

    return _mm(dy, w, mode="nt", b_shard=True, out_dtypes=(F32, BF16), extras=(x, dres), bcast=(g,),
               accs=((1, D_MODEL),), ref_epi=epilogue, name=name, after=after)


def _mm_res_norm(a, w, res, g, name):
    def epilogue(acc, e_refs, b_refs, o_refs, a_refs):
        xv = acc + e_refs[0][...]
        o_refs[0][...] = xv
        r = lax.rsqrt(jnp.mean(xv * xv, axis=-1, keepdims=True) + EPS)
        o_refs[1][...] = (xv * r * b_refs[0][...]).astype(BF16)

    return _mm(a, w, mode="nn", out_dtypes=(F32, BF16), extras=(res,), bcast=(g,), ref_epi=epilogue, name=name)


def _mm_final_loss(a, w, res, target, g, name):
    def epilogue(acc, e_refs, b_refs, o_refs, a_refs):
        xv = acc + e_refs[0][...]
        gv = b_refs[0][...]
        r = lax.rsqrt(jnp.mean(xv * xv, axis=-1, keepdims=True) + EPS)
        e = xv * r * gv - e_refs[1][...]
        tok = jnp.mean(e * e, axis=-1, keepdims=True)
        a_refs[0][...] += 0.5 * jnp.sum(tok, axis=0, keepdims=True) * jnp.ones((1, LANES), F32)
        dx, dg = _rms_bwd_math(xv, e * (1.0 / D_MODEL), gv)
        o_refs[0][...] = dx
        o_refs[1][...] = dx.astype(BF16)
        a_refs[1][...] += dg

    return _mm(a, w, mode="nn", out_dtypes=(F32, BF16), extras=(res, target), bcast=(g,),
               accs=((1, LANES), (1, D_MODEL)), ref_epi=epilogue, name=name)


def _relu2(acc):
    r = jnp.maximum(acc, 0.0)
    return r * r, r


def _mlp_fwd(x, h, fetch, tag, finish):
    w_up, w_down = fetch(f"mlp{tag}", h)
    a, r = _mm(h, w_up, mode="nn", b_shard=True, out_dtypes=(BF16, BF16), epi=_relu2, name=f"mlp{tag}_up")
    return finish(a, w_down, x, f"mlp{tag}_down"), (h, a, r, w_up, w_down)


def _mlp_bwd(x, g, saved, dx, dx_bf, tag, after):
    h, a, r, w_up, w_down = saved
    d_down = _mm(a, dx_bf, mode="tn", out_dtypes=(BF16,), name=f"mlp{tag}_dwdown", after=after)
    dup = _mm(dx_bf, w_down, mode="nt", extras=(r,), out_dtypes=(BF16,),
              epi=lambda acc, rv: (acc * (2.0 * rv.astype(F32)),), name=f"mlp{tag}_dup")
    d_up = _mm(h, dup, mode="tn", o_shard=True, out_dtypes=(BF16,), name=f"mlp{tag}_dwup")
    dx_new, dx_new_bf, dg = _mm_norm_bwd(dup, w_up, x, dx, g, f"mlp{tag}_dh")
    return dx_new, dx_new_bf, dg, d_up, d_down


def _rope_tables(L, B):
    rows = L // GRID_W
    row = np.repeat(np.arange(rows, dtype=np.float32), GRID_W)
    col = np.tile(np.arange(GRID_W, dtype=np.float32), rows)
    inv = (ROPE_THETA ** (-np.arange(HEAD_DIM // 4, dtype=np.float32) / (HEAD_DIM // 4))).astype(np.float32)
    ar, ac = row[:, None] * inv, col[:, None] * inv
    cos = np.concatenate([np.cos(ar), np.cos(ar), np.cos(ac), np.cos(ac)], axis=-1)
    sin = np.concatenate([-np.sin(ar), np.sin(ar), -np.sin(ac), np.sin(ac)], axis=-1)
    return jnp.asarray(np.tile(cos, (B, 1)), F32), jnp.asarray(np.tile(sin, (B, 1)), F32)


def _swap_halves(x):
    lane = lax.broadcasted_iota(jnp.int32, x.shape, 1)
    return jnp.where((lane % 64) < 32, pltpu.roll(x, HEAD_DIM - 32, 1), pltpu.roll(x, 32, 1))


def _qk_prep(qkv, cos, sin, q_g, k_g):
    def fn(ins, bs, outs, accs):
        c, s = ins[1][...], ins[2][...]
        for h in range(N_HEADS + N_KV):
            xv = ins[0][:, h * HEAD_DIM:(h + 1) * HEAD_DIM]
            g = bs[0][...] if h < N_HEADS else bs[1][...]
            r = lax.rsqrt(jnp.mean(xv * xv, axis=-1, keepdims=True) + EPS)
            z = xv * r * g
            y = (z * c + _swap_halves(z) * s).astype(BF16)
            if h < N_HEADS:
                outs[0][:, h * HEAD_DIM:(h + 1) * HEAD_DIM] = y
            else:
                outs[1][:, (h - N_HEADS) * HEAD_DIM:(h - N_HEADS + 1) * HEAD_DIM] = y
        outs[2][...] = ins[0][:, (N_HEADS + N_KV) * HEAD_DIM:].astype(BF16)

    kvw = N_KV * HEAD_DIM
    return _rowwise(fn, [qkv, cos, sin], [q_g, k_g], [(D_MODEL, BF16), (kvw, BF16), (kvw, BF16)], tm=512,
                    name="attn_qk_prep")


def _qk_prep_bwd(qkv, dq, dk, dv, cos, sin, q_g, k_g):
    def fn(ins, bs, outs, accs):
        c, s = ins[4][...], ins[5][...]
        for h in range(N_HEADS + N_KV):
            sl = slice(h * HEAD_DIM, (h + 1) * HEAD_DIM)
            xv = ins[0][:, sl]
            if h < N_HEADS:
                g, dy, acc = bs[0][...], ins[1][:, sl], accs[0]
            else:
                ks = slice((h - N_HEADS) * HEAD_DIM, (h - N_HEADS + 1) * HEAD_DIM)
                g, dy, acc = bs[1][...], ins[2][:, ks], accs[1]
            r = lax.rsqrt(jnp.mean(xv * xv, axis=-1, keepdims=True) + EPS)
            xn = xv * r
            dz = dy * c - _swap_halves(dy) * s
            acc[...] += _rsum(dz * xn)
            dxn = dz * g
            outs[0][:, sl] = (r * (dxn - xn * jnp.mean(dxn * xn, axis=-1, keepdims=True))).astype(BF16)
        outs[0][:, (N_HEADS + N_KV) * HEAD_DIM:] = ins[3][...].astype(BF16)

    return _rowwise(fn, [qkv, dq, dk, dv, cos, sin], [q_g, k_g], [(qkv.shape[1], BF16)],
                    [(1, HEAD_DIM), (1, HEAD_DIM)], tm=256, name="attn_qk_prep_bwd")


_EXP2_SCALE = SCALE * math.log2(math.e)


def _exp_rows(q, k):
    s = lax.dot_general(q, k, _NT, preferred_element_type=F32)
    p = jnp.exp2((s - jnp.max(s, axis=-1, keepdims=True)) * _EXP2_SCALE)
    return p, jnp.sum(p, axis=-1, keepdims=True)


def _attn_fwd(q, k, v, B, L, tq=1024, sub=256):
    tq = min(tq, L)
    sub = min(sub, tq)
    nq = L // tq

    def body(q_ref, k_ref, v_ref, o_ref):
        kv, vv = k_ref[...], v_ref[...]
        for c in range(tq // sub):
            rows = slice(c * sub, (c + 1) * sub)
            p, l = _exp_rows(q_ref[rows, :], kv)
            o = jnp.dot(p.astype(BF16), vv, preferred_element_type=F32)
            o_ref[rows, :] = (o * (1.0 / l)).astype(o_ref.dtype)

    return pl.pallas_call(
        body, name="attn_fwd", grid=(B, N_HEADS, nq),
        in_specs=[pl.BlockSpec((tq, HEAD_DIM), lambda b, h, i: (b * nq + i, h)),
                  pl.BlockSpec((L, HEAD_DIM), lambda b, h, i: (b, h // GROUP)),
                  pl.BlockSpec((L, HEAD_DIM), lambda b, h, i: (b, h // GROUP))],
        out_specs=pl.BlockSpec((tq, HEAD_DIM), lambda b, h, i: (b * nq + i, h)),
        out_shape=jax.ShapeDtypeStruct((B * L, D_MODEL), BF16),
        compiler_params=_params(("parallel", "parallel", "parallel")),
    )(q, k, v)


def _attn_bwd(q, k, v, do, B, L, tq=512, sub=256):
    tq = min(tq, L)
    sub = min(sub, tq)
    nq = L // tq

    def body(q_ref, k_ref, v_ref, do_ref, dq_ref, dk_ref, dv_ref):
        @pl.when((pl.program_id(2) == 0) & (pl.program_id(3) == 0))
        def _():
            dk_ref[...] = jnp.zeros(dk_ref.shape, F32)
            dv_ref[...] = jnp.zeros(dv_ref.shape, F32)

        kv, vv = k_ref[...], v_ref[...]
        ps, es, dos, qs = [], [], [], []
        for c in range(tq // sub):
            rows = slice(c * sub, (c + 1) * sub)
            qc, doc = q_ref[rows, :], do_ref[rows, :]
            p, l = _exp_rows(qc, kv)
            inv = 1.0 / l
            dp = lax.dot_general(doc, vv, _NT, preferred_element_type=F32)
            delta = jnp.sum(p * dp, axis=-1, keepdims=True) * inv
            e = (p * (dp - delta)).astype(BF16)
            dq_ref[rows, :] = jnp.dot(e, kv, preferred_element_type=F32) * (inv * SCALE)
            ps.append(p.astype(BF16))
            es.append(e)
            dos.append((doc.astype(F32) * inv).astype(BF16))
            qs.append((qc.astype(F32) * (inv * SCALE)).astype(BF16))
        cat = lambda xs: xs[0] if len(xs) == 1 else jnp.concatenate(xs, axis=0)
        dv_ref[...] += lax.dot_general(cat(ps), cat(dos), _TN, preferred_element_type=F32)
        dk_ref[...] += lax.dot_general(cat(es), cat(qs), _TN, preferred_element_type=F32)

    qmap = lambda b, kh, g, i: (b * nq + i, kh * GROUP + g)
    kmap = lambda b, kh, g, i: (b, kh)
    kvw = N_KV * HEAD_DIM
    return pl.pallas_call(
        body, name="attn_bwd", grid=(B, N_KV, GROUP, nq),
        in_specs=[pl.BlockSpec((tq, HEAD_DIM), qmap), pl.BlockSpec((L, HEAD_DIM), kmap),
                  pl.BlockSpec((L, HEAD_DIM), kmap), pl.BlockSpec((tq, HEAD_DIM), qmap)],
        out_specs=[pl.BlockSpec((tq, HEAD_DIM), qmap), pl.BlockSpec((L, HEAD_DIM), kmap),
                   pl.BlockSpec((L, HEAD_DIM), kmap)],
        out_shape=[jax.ShapeDtypeStruct((B * L, D_MODEL), F32), jax.ShapeDtypeStruct((B * L, kvw), F32),
                   jax.ShapeDtypeStruct((B * L, kvw), F32)],
        compiler_params=_params(("parallel", "parallel", "arbitrary", "arbitrary")),
    )(q, k, v, do)


def _conv_shift(x, t, L, k):
    if k == 2:
        return x
    if k < 2:
        return jnp.where(t >= 2 - k, pltpu.roll(x, 2 - k, 0), 0.0)
    return jnp.where(t < L - (k - 2), pltpu.roll(x, L - (k - 2), 0), 0.0)


def _conv_fwd(z, wb, B, L, tc=256):
    noff = D_MODEL // tc

    def body(z_ref, w_ref, o_ref):
        x = z_ref[...]
        t = lax.broadcasted_iota(jnp.int32, x.shape, 0)
        acc = w_ref[4:5, :] + w_ref[2:3, :] * x
        for k in (0, 1, 3):
            acc = acc + w_ref[k:k + 1, :] * _conv_shift(x, t, L, k)
        o_ref[...] = acc

    return pl.pallas_call(
        body, name="rg_conv", grid=(B, noff),
        in_specs=[pl.BlockSpec((L, tc), lambda b, j: (b, noff + j)), pl.BlockSpec((SUBLANES, tc), lambda b, j: (0, j))],
        out_specs=pl.BlockSpec((L, tc), lambda b, j: (b, j)),
        out_shape=jax.ShapeDtypeStruct((B * L, D_MODEL), F32),
        compiler_params=_params(("parallel", "parallel")),
    )(z, wb)


def _conv_bwd(z, g, wb, dz, B, L, tc=256):
    noff = D_MODEL // tc

    def body(z_ref, g_ref, w_ref, dz_in, dx_ref, dw_ref):
        @pl.when(pl.program_id(1) == 0)
        def _():
            dw_ref[...] = jnp.zeros(dw_ref.shape, F32)

        x, gv = z_ref[...], g_ref[...]
        t = lax.broadcasted_iota(jnp.int32, x.shape, 0)
        dx = w_ref[2:3, :] * gv
        for k in (0, 1, 3):
            dx = dx + w_ref[k:k + 1, :] * _conv_shift(gv, t, L, 4 - k)
        dx_ref[...] = dx.astype(BF16)
        for k in range(4):
            dw_ref[k:k + 1, :] += _rsum(_conv_shift(x, t, L, k) * gv)
        dw_ref[4:5, :] += _rsum(gv)

    return pl.pallas_call(
        body, name="rg_conv_bwd", grid=(noff, B),
        in_specs=[pl.BlockSpec((L, tc), lambda j, b: (b, noff + j)), pl.BlockSpec((L, tc), lambda j, b: (b, j)),
                  pl.BlockSpec((SUBLANES, tc), lambda j, b: (0, j)), _ANY],
        out_specs=[pl.BlockSpec((L, tc), lambda j, b: (b, noff + j)),
                   pl.BlockSpec((SUBLANES, tc), lambda j, b: (0, j))],
        out_shape=[jax.ShapeDtypeStruct(dz.shape, dz.dtype), jax.ShapeDtypeStruct((SUBLANES, D_MODEL), F32)],
        input_output_aliases={3: 0},
        compiler_params=_params(("parallel", "arbitrary")),
    )(z, g, wb, dz)


def _softplus(x):
    return jnp.maximum(x, 0.0) + jnp.log1p(jnp.exp(-jnp.abs(x)))


_ROW_BA, _ROW_BX, _ROW_LAM = 0, 2, 4


def _gate_math(xb, pre, vec_ref, d, sl):
    pa = pre[:, (2 * d) * LRU_BW:(2 * d + 1) * LRU_BW] + vec_ref[_ROW_BA + d:_ROW_BA + d + 1, sl]
    px = pre[:, (2 * d + 1) * LRU_BW:(2 * d + 2) * LRU_BW] + vec_ref[_ROW_BX + d:_ROW_BX + d + 1, sl]
    r = jax.nn.sigmoid(pa)
    i = jax.nn.sigmoid(px)
    sp = _softplus(-vec_ref[_ROW_LAM + d:_ROW_LAM + d + 1, sl])
    log_a = (-RG_C) * r * sp
    a = jnp.exp(log_a)
    th = jnp.tanh(log_a)
    om = -2.0 * th / (1.0 - th)
    mult = jnp.sqrt(om)
    return a, mult * (i * xb), (r, i, sp, om, mult)


def _gate_fwd(rec, wcat, gvec):
    def fn(ins, bs, outs, accs):
        for blk in range(LRU_BLOCKS):
            sl = slice(blk * LRU_BW, (blk + 1) * LRU_BW)
            xb = ins[0][:, sl]
            pre = jnp.dot(xb.astype(BF16), bs[0][sl, :], preferred_element_type=F32)
            for d in range(2):
                a, u, _ = _gate_math(xb, pre, bs[1], d, sl)
                outs[2 * d][:, sl] = a
                outs[2 * d + 1][:, sl] = u

    return _rowwise(fn, [rec], [wcat, gvec], [(D_MODEL, F32)] * 4, tm=256, name="rg_gate")


def _gate_bwd(rec, du_f, da_f, du_b, da_b, wcat, gvec):
    def fn(ins, bs, outs, accs):
        for blk in range(LRU_BLOCKS):
            sl = slice(blk * LRU_BW, (blk + 1) * LRU_BW)
            xb = ins[0][:, sl]
            xb16 = xb.astype(BF16)
            w = bs[0][sl, :]
            pre = jnp.dot(xb16, w, preferred_element_type=F32)
            dx = jnp.zeros_like(xb)
            dpre = []
            for d in range(2):
                a, _, (r, i, sp, om, mult) = _gate_math(xb, pre, bs[1], d, sl)
                du, da = ins[1 + 2 * d][:, sl], ins[2 + 2 * d][:, sl]
                d_i = du * mult * xb
                d_mult = du * i * xb
                dx = dx + du * mult * i
                dlog = da * a - d_mult * (1.0 - om) / mult
                d_r = dlog * ((-RG_C) * sp)
                d_sp = _rsum(dlog * ((-RG_C) * r))
                lam = bs[1][_ROW_LAM + d:_ROW_LAM + d + 1, sl]
                accs[1][_ROW_LAM + d:_ROW_LAM + d + 1, sl] += d_sp * (-jax.nn.sigmoid(-lam))
                dpa = d_r * r * (1.0 - r)
                dpx = d_i * i * (1.0 - i)
                accs[1][_ROW_BA + d:_ROW_BA + d + 1, sl] += _rsum(dpa)
                accs[1][_ROW_BX + d:_ROW_BX + d + 1, sl] += _rsum(dpx)
                dpre += [dpa, dpx]
            dpre = jnp.concatenate(dpre, axis=1).astype(BF16)
            accs[0][sl, :] += lax.dot_general(xb16, dpre, _TN, preferred_element_type=F32)
            outs[0][:, sl] = dx + lax.dot_general(dpre, w, _NT, preferred_element_type=F32)

    return _rowwise(fn, [rec, du_f, da_f, du_b, da_b], [wcat, gvec], [(D_MODEL, F32)],
                    [(D_MODEL, 4 * LRU_BW), (SUBLANES, D_MODEL)], tm=256, name="rg_gate_bwd")


def _as_time_blocks(x):
    return x.reshape(x.shape[0] // SUBLANES, SUBLANES, x.shape[1])


def _scan_call(body, ins, n_out, B, L, tc, name):
    nb = L // SUBLANES
    spec = pl.BlockSpec((nb, SUBLANES, tc), lambda b, j: (b, 0, j))
    T = ins[0].shape[0]
    outs = pl.pallas_call(
        functools.partial(body, nb), name=name, grid=(B, D_MODEL // tc),
        in_specs=[spec] * len(ins), out_specs=[spec] * n_out,
        out_shape=[jax.ShapeDtypeStruct((T // SUBLANES, SUBLANES, D_MODEL), F32)] * n_out,
        compiler_params=_params(("parallel", "parallel")),
    )(*[_as_time_blocks(x) for x in ins])
    return [o.reshape(T, D_MODEL) for o in outs]


def _scan_fwd(a_f, u_f, a_b, u_b, B, L, tc=256):
    def body(nb, af, uf, ab, ub, hf, hb):
        def step(i, carry):
            h1, h2 = carry
            ib = nb - 1 - i
            for j in range(SUBLANES):
                jb = SUBLANES - 1 - j
                h1 = af[i, j:j + 1, :] * h1 + uf[i, j:j + 1, :]
                hf[i, j:j + 1, :] = h1
                h2 = ab[ib, jb:jb + 1, :] * h2 + ub[ib, jb:jb + 1, :]
                hb[ib, jb:jb + 1, :] = h2
            return h1, h2

        zero = jnp.zeros((1, tc), F32)
        lax.fori_loop(0, nb, step, (zero, zero))

    return _scan_call(body, [a_f, u_f, a_b, u_b], 2, B, L, tc, "rg_scan")


def _scan_bwd(dy, a_f, h_f, a_b, h_b, B, L, tc=256):
    def body(nb, dy_r, af, hf, ab, hb, duf, daf, dub, dab):
        def step(i, carry):
            c1, c2 = carry
            ir = nb - 1 - i
            for j in range(SUBLANES):
                jr = SUBLANES - 1 - j
                lam1 = dy_r[ir, jr:jr + 1, :] + c1
                if jr > 0:
                    prev = hf[ir, jr - 1:jr, :]
                else:
                    prev = hf[jnp.maximum(ir - 1, 0), SUBLANES - 1:SUBLANES, :] * (ir > 0).astype(F32)
                duf[ir, jr:jr + 1, :] = lam1
                daf[ir, jr:jr + 1, :] = lam1 * prev
                c1 = af[ir, jr:jr + 1, :] * lam1
                lam2 = dy_r[i, j:j + 1, :] + c2
                if j < SUBLANES - 1:
                    nxt = hb[i, j + 1:j + 2, :]
                else:
                    nxt = hb[jnp.minimum(i + 1, nb - 1), 0:1, :] * (i < nb - 1).astype(F32)
                dub[i, j:j + 1, :] = lam2
                dab[i, j:j + 1, :] = lam2 * nxt
                c2 = ab[i, j:j + 1, :] * lam2
            return c1, c2

        zero = jnp.zeros((1, tc), F32)
        lax.fori_loop(0, nb, step, (zero, zero))

    return _scan_call(body, [dy, a_f, h_f, a_b, h_b], 4, B, L, tc, "rg_scan_bwd")


_GELU_C = math.sqrt(2.0 / math.pi)


def _gelu_parts(x):
    th = jnp.tanh(_GELU_C * (x + 0.044715 * x * x * x))
    return 0.5 * x * (1.0 + th), th


def _gated_out(h_f, h_b, z):
    def fn(ins, bs, outs, accs):
        gl, _ = _gelu_parts(ins[2][...])
        outs[0][...] = ((ins[0][...] + ins[1][...]) * gl).astype(BF16)

    return _rowwise(fn, [h_f, h_b, (z, D_MODEL, 0)], [], [(D_MODEL, BF16)], tm=512, name="rg_gated_out")[0]


def _gated_out_bwd(dyg, h_f, h_b, z):
    def fn(ins, bs, outs, accs):
        x = ins[3][...]
        gl, th = _gelu_parts(x)
        dgl = 0.5 * (1.0 + th) + 0.5 * x * (1.0 - th * th) * (_GELU_C * (1.0 + 3.0 * 0.044715 * x * x))
        g = ins[0][...]
        outs[0][...] = g * gl
        outs[1][...] = (g * (ins[1][...] + ins[2][...]) * dgl).astype(BF16)

    return _rowwise(fn, [dyg, h_f, h_b, (z, D_MODEL, 0)], [], [(D_MODEL, F32), (D_MODEL, BF16, 2 * D_MODEL)], tm=512,
                    name="rg_gated_out_bwd")


def _make_wcat(w_a, w_x):
    g = jnp.stack([w_a[0, 0], w_x[0, 0], w_a[0, 1], w_x[0, 1]])
    return jnp.transpose(g, (1, 2, 0, 3)).reshape(D_MODEL, 4 * LRU_BW)


def _split_wcat(rows):
    g = jnp.transpose(rows.reshape(LRU_BLOCKS, LRU_BW, 4, LRU_BW), (2, 0, 1, 3))
    return jnp.stack([g[0], g[2]])[None], jnp.stack([g[1], g[3]])[None]


def _rows_at(part, first):
    return jnp.pad(part, ((first, SUBLANES - first - part.shape[0]), (0, 0)))


def _qk_slot(q_g, k_g):
    wide = lambda v, at: jnp.pad(v, ((0, SUBLANES - 1), (at, D_MODEL - at - HEAD_DIM)))
    return wide(q_g, 0) + wide(k_g, HEAD_DIM)


def _local_step(x, target, P, fetch, emit, B, L, after=None):
    g_mix, g_mlp = P["norm_mix_g"], P["norm_mlp_g"]
    h0 = _rms_fwd(x, g_mix[0:1], "rg_norm", after=after)
    w_in, w_out, conv_wb, wcat, gvec = fetch("rg", h0)
    z = _mm(h0, w_in, mode="nn", b_shard=True, name="rg_in")
    rec = _conv_fwd(z, conv_wb, B, L)
    a_f, u_f, a_b, u_b = _gate_fwd(rec, wcat, gvec)
    h_f, h_b = _scan_fwd(a_f, u_f, a_b, u_b, B, L)
    yg = _gated_out(h_f, h_b, z)
    x1, h1 = _mm_res_norm(yg, w_out, x, g_mlp[0:1], "rg_out")
    (x2, h3), mlp0 = _mlp_fwd(x1, h1, fetch, 0, lambda a, w, res, name: _mm_res_norm(a, w, res, g_mix[1:2], name))
    w_qkv, w_o = fetch("att", h3)
    qkv = _mm(h3, w_qkv, mode="nn", b_shard=True, name="attn_qkv")
    cos, sin = _rope_tables(L, B)
    qh, kh, vh = _qk_prep(qkv, cos, sin, P["q_g"], P["k_g"])
    o = _attn_fwd(qh, kh, vh, B, L)
    x3, h4 = _mm_res_norm(o, w_o, x2, g_mlp[1:2], "attn_out")
    (dx4, dx4_bf, loss_acc, d_final_g), mlp1 = _mlp_fwd(
        x3, h4, fetch, 1, lambda a, w, res, name: _mm_final_loss(a, w, res, target, P["final_g"], name))

    dx3, dx3_bf, dg_mlp1, d_up1, d_down1 = _mlp_bwd(x3, g_mlp[1:2], mlp1, dx4, dx4_bf, 1, None)
    tok = emit("mlp1", [d_up1, d_down1])
    d_wo = _mm(o, dx3_bf, mode="tn", out_dtypes=(BF16,), name="attn_dwo", after=tok)
    do = _mm(dx3_bf, w_o, mode="nt", out_dtypes=(BF16,), name="attn_do")
    dq, dk, dv = _attn_bwd(qh, kh, vh, do, B, L)
    dqkv, dq_g, dk_g = _qk_prep_bwd(qkv, dq, dk, dv, cos, sin, P["q_g"], P["k_g"])
    d_wqkv = _mm(h3, dqkv, mode="tn", o_shard=True, out_dtypes=(BF16,), name="attn_dwqkv")
    tok = emit("att", [d_wqkv, d_wo])
    dx2, dx2_bf, dg_mix1 = _mm_norm_bwd(dqkv, w_qkv, x2, dx3, g_mix[1:2], "attn_dh", after=tok)
    tok = emit("point_attn_done", [dx2_bf])
    dx1, dx1_bf, dg_mlp0, d_up0, d_down0 = _mlp_bwd(x1, g_mlp[0:1], mlp0, dx2, dx2_bf, 0, tok)
    tok = emit("mlp0", [d_up0, d_down0])
    d_wout = _mm(yg, dx1_bf, mode="tn", out_dtypes=(BF16,), name="rg_dwout", after=tok)
    tok = emit("rg_out", [d_wout])
    dyg = _mm(dx1_bf, w_out, mode="nt", name="rg_dyg", after=tok)
    dy, dgate = _gated_out_bwd(dyg, h_f, h_b, z)
    du_f, da_f, du_b, da_b = _scan_bwd(dy, a_f, h_f, a_b, h_b, B, L)
    drec_c, d_wcat, d_gvec = _gate_bwd(rec, du_f, da_f, du_b, da_b, wcat, gvec)
    dz, d_convwb = _conv_bwd(z, drec_c, conv_wb, dgate, B, L)
    tok = emit("point_conv_done", [dz])
    d_win = _mm(h0, dz, mode="tn", o_shard=True, out_dtypes=(BF16,), name="rg_dwin", after=tok)
    tok = emit("rg_in", [d_win])
    grad_x, _, dg_mix0 = _mm_norm_bwd(dz, w_in, x, dx1, g_mix[0:1], "rg_dh", after=tok)

    norms = (_rows_at(dg_mix0, 0) + _rows_at(dg_mix1, 1) + _rows_at(dg_mlp0, 2) + _rows_at(dg_mlp1, 3)
             + _rows_at(d_final_g, 4))
    small = jnp.concatenate([norms, d_convwb, d_gvec, _qk_slot(dq_g, dk_g), d_wcat.reshape(GATE_ROWS, D_MODEL)],
                            axis=0)
    return loss_acc[0, 0], grad_x, small


_MESH = pl.DeviceIdType.MESH


def _place():
    x, y, c = lax.axis_index("x"), lax.axis_index("y"), lax.axis_index("c")
    peers = [((1 - x) if j & 2 else x, (1 - y) if j & 1 else y) for j in (1, 2, 3)]
    return x, y, c, peers


def _comm_call(body, ins, out_shapes, n_sem, name):
    return pl.pallas_call(
        body, name=name, in_specs=[_ANY] * len(ins), out_specs=[_ANY] * len(out_shapes), out_shape=out_shapes,
        scratch_shapes=[pltpu.SemaphoreType.DMA((n_sem,)), pltpu.SemaphoreType.DMA((n_sem,)),
                        pltpu.SemaphoreType.DMA((len(ins),))],
    )(*ins)


def _all_gather_chips(shards, name):
    n = len(shards)

    def body(*refs):
        ins, outs = refs[:n], refs[n:2 * n]
        send, recv, lsem = refs[2 * n:]
        x, y, c, peers = _place()
        me = 2 * x + y

        def copy(a, j, slot):
            px, py = peers[j]
            return pltpu.make_async_remote_copy(
                src_ref=ins[a], dst_ref=outs[a].at[slot], send_sem=send.at[3 * a + j], recv_sem=recv.at[3 * a + j],
                device_id=(px, py, c), device_id_type=_MESH)

        local = [pltpu.make_async_copy(ins[a], outs[a].at[me], lsem.at[a]) for a in range(n)]
        sends = [copy(a, j, me) for a in range(n) for j in range(3)]
        for cp in local + sends:
            cp.start()
        for a in range(n):
            for j, (px, py) in enumerate(peers):
                copy(a, j, 2 * px + py).wait_recv()
        for cp in sends:
            cp.wait_send()
        for cp in local:
            cp.wait()

    shapes = [jax.ShapeDtypeStruct((N_CHIPS,) + s.shape, s.dtype) for s in shards]
    return _comm_call(body, shards, shapes, 3 * n, name)


_HBM = pl.BlockSpec(memory_space=pltpu.HBM)
_SEM = pl.BlockSpec(memory_space=pltpu.SEMAPHORE)
_EFFECT = pltpu.SideEffectType.DATAFLOW_SIDE_EFFECTING


_COPIES = dict(gather=N_CHIPS - 1, scatter=N_CHIPS - 1, swap=1)


def _split_copies(kind, srcs, lands, send, recv):
    x, y, c, peers = _place()
    me = 2 * x + y
    per = _COPIES[kind]
    out = []
    for a in range(len(lands)):
        for j in range(per):
            if kind == "swap":
                src, there, here, dev = srcs[a], lands[a], lands[a], (x, y, 1 - c)
            else:
                px, py = peers[j]
                dev = (px, py, c)
                if kind == "gather":
                    src, there, here = lands[a].at[me], lands[a].at[me], lands[a].at[2 * px + py]
                else:
                    src, there, here = srcs[a].at[2 * px + py], lands[a].at[j], lands[a].at[j]
            mk = functools.partial(
                pltpu.make_async_remote_copy, src_ref=src, send_sem=send.at[per * a + j],
                recv_sem=recv.at[per * a + j], device_id=dev, device_id_type=_MESH)
            out.append((functools.partial(mk, dst_ref=there), functools.partial(mk, dst_ref=here)))
    return out


def _exchange_start(kind, srcs, lands, name, after=None):
    arrays = list(srcs) + list(lands)
    n_s, n, n_all = len(srcs), len(lands), len(srcs) + len(lands)
    n_sem = _COPIES[kind] * n
    order = _after_operand(after)
    n_x = len(order)

    def body(*refs):
        send, recv = refs[n_all + n_x], refs[n_all + n_x + 1]
        token = refs[-1]
        for started, _ in _split_copies(kind, refs[:n_s], refs[n_s:n_all], send, recv):
            started().start()
        token[...] = jnp.zeros(token.shape, F32)

    res = pl.pallas_call(
        body, name=name,
        out_shape=(pltpu.SemaphoreType.DMA((n_sem,)), pltpu.SemaphoreType.DMA((n_sem,)),
                   *[pltpu.HBM(a.shape, a.dtype) for a in arrays], jax.ShapeDtypeStruct((SUBLANES, LANES), F32)),
        in_specs=[_HBM] * n_all + [_ANY] * n_x,
        out_specs=(_SEM, _SEM, *[_HBM] * n_all, pl.BlockSpec(memory_space=pltpu.VMEM)),
        input_output_aliases={i: 2 + i for i in range(n_all)},
        compiler_params=pltpu.CompilerParams(has_side_effects=_EFFECT),
    )(*[pltpu.with_memory_space_constraint(a, pltpu.HBM) for a in arrays], *order)
    return (res[0], res[1], res[2:2 + n_s], res[2 + n_s:2 + n_all]), res[-1]


def _exchange_wait(kind, handle, after, name):
    send, recv, srcs, lands = handle
    arrays = list(srcs) + list(lands)
    n_s, n_all = len(srcs), len(arrays)

    def body(*refs):
        for started, landing in _split_copies(kind, refs[:n_s], refs[n_s:n_all], refs[n_all], refs[n_all + 1]):
            started().wait_send()
            landing().wait_recv()

    res = pl.pallas_call(
        body, name=name, out_shape=[pltpu.HBM(a.shape, a.dtype) for a in arrays],
        in_specs=[_HBM] * n_all + [_SEM, _SEM, _ANY], out_specs=[_HBM] * n_all,
        input_output_aliases={i: i for i in range(n_all)},
        compiler_params=pltpu.CompilerParams(has_side_effects=_EFFECT),
    )(*arrays, send, recv, after)
    return res[:n_s], res[n_s:]


def _index_operand(i):
    return jnp.reshape(i, (1,)).astype(jnp.int32)


def _cast_into_slot(src, row0, rows, me, dtype, name, after=None):
    cols = src.shape[1]
    tm = min(512, rows)
    order = _after_operand(after)

    def body(me_ref, x_ref, *rest):
        rest[-1][...] = x_ref[...].astype(dtype)

    return pl.pallas_call(
        body, name=name,
        grid_spec=pltpu.PrefetchScalarGridSpec(
            num_scalar_prefetch=1, grid=(rows // tm,),
            in_specs=[pl.BlockSpec((tm, cols), lambda i, me_ref: (i + row0 // tm, 0))] + [_ANY] * len(order),
            out_specs=pl.BlockSpec((None, tm, cols), lambda i, me_ref: (me_ref[0], i, 0))),
        out_shape=jax.ShapeDtypeStruct((N_CHIPS, rows, cols), dtype), compiler_params=_params(("parallel",)),
    )(_index_operand(me), src, *order)


def _sum_slots(mine, r, me, name):
    _, rows, cols = r.shape
    tm = min(512, rows)

    def body(me_ref, own_ref, r_ref, o_ref):
        o_ref[...] = ((own_ref[...].astype(F32) + r_ref[0].astype(F32)) + r_ref[1].astype(F32)) + r_ref[2].astype(F32)

    return pl.pallas_call(
        body, name=name,
        grid_spec=pltpu.PrefetchScalarGridSpec(
            num_scalar_prefetch=1, grid=(rows // tm,),
            in_specs=[pl.BlockSpec((None, tm, cols), lambda i, me_ref: (me_ref[0], i, 0)),
                      pl.BlockSpec((N_CHIPS - 1, tm, cols), lambda i, me_ref: (0, i, 0))],
            out_specs=pl.BlockSpec((tm, cols), lambda i, me_ref: (i, 0))),
        out_shape=jax.ShapeDtypeStruct((rows, cols), F32), compiler_params=_params(("parallel",)),
    )(_index_operand(me), mine, r)


def _add(p, q, name):
    def fn(ins, bs, outs, accs):
        outs[0][...] = ins[0][...] + ins[1][...]

    return _rowwise(fn, [p, q], [], [(p.shape[1], F32)], tm=512, name=name)[0]


def _adamw(w, m, v, ps, qs, name):
    rows, cols = w.shape
    seg_rows = ps[0].shape[0]
    tm = min(256, seg_rows)
    while seg_rows % tm:
        tm -= SUBLANES
    per, n_seg = seg_rows // tm, len(ps)
    parts = list(ps) + ([] if qs is None else list(qs))

    def body(w_ref, m_ref, v_ref, *rest):
        g_refs, outs = rest[:len(parts)], rest[len(parts):]
        grad = lambda s: g_refs[s][...] if qs is None else g_refs[s][...] + g_refs[n_seg + s][...]
        g = grad(0)
        for s in range(1, n_seg):
            g = jnp.where(pl.program_id(0) >= s * per, grad(s), g)
        m1 = ADAM_B1 * m_ref[...] + (1.0 - ADAM_B1) * g
        v1 = ADAM_B2 * v_ref[...] + (1.0 - ADAM_B2) * (g * g)
        m_hat = m1 / (1.0 - ADAM_B1 ** ADAM_STEP)
        v_hat = v1 / (1.0 - ADAM_B2 ** ADAM_STEP)
        outs[0][...] = g
        outs[1][...] = (-ADAM_LR) * (m_hat / (jnp.sqrt(v_hat) + ADAM_EPS) + ADAM_WD * w_ref[...])
        outs[2][...] = m1
        outs[3][...] = v1

    row_spec = pl.BlockSpec((tm, cols), lambda i: (i, 0))
    seg_spec = lambda s: pl.BlockSpec((tm, cols), lambda i: (jnp.clip(i - s * per, 0, per - 1), 0))
    return pl.pallas_call(
        body, name=name, grid=(rows // tm,),
        in_specs=[row_spec] * 3 + [seg_spec(s) for s in range(n_seg)] * (1 if qs is None else 2),
        out_specs=[row_spec] * 4, out_shape=[jax.ShapeDtypeStruct((rows, cols), F32)] * 4,
        compiler_params=_params(("arbitrary",)),
    )(w, m, v, *parts)


def _put_cols(shard, me):
    full = jnp.zeros((shard.shape[0], D_MODEL), F32)
    return lax.dynamic_update_slice(full, shard, (0, me * (D_MODEL // N_CHIPS)))


def _gate_vec_slot(b_a, b_x, lam):
    return _rows_at(b_a, _ROW_BA) + _rows_at(b_x, _ROW_BX) + _rows_at(lam, _ROW_LAM)


def _pack_small(p, me):
    return jnp.concatenate([
        _rows_at(p["norm_mix_g"], 0) + _rows_at(p["norm_mlp_g"], 2) + _rows_at(p["final_g"][None], 4),
        _rows_at(_put_cols(p["rg_conv_w"][0, :, 0, :], me), 0) + _rows_at(p["rg_conv_b"], 4),
        _gate_vec_slot(_put_cols(p["rg_b_a"][0], me), _put_cols(p["rg_b_x"][0], me), _put_cols(p["rg_lam"][0], me)),
        _qk_slot(p["at_q_g"], p["at_k_g"]),
        _make_wcat(p["rg_w_a"], p["rg_w_x"]).reshape(GATE_ROWS, D_MODEL),
    ], axis=0)


def _unpack_small(r, me):
    def cols(rows):
        return lax.dynamic_slice(rows, (0, me * (D_MODEL // N_CHIPS)), (rows.shape[0], D_MODEL // N_CHIPS))

    w_a, w_x = _split_wcat(r[SMALL_ROWS:])
    gate = r[16:24]
    return dict(
        norm_mix_g=r[0:2], norm_mlp_g=r[2:4], final_g=r[4], rg_conv_w=cols(r[8:12])[None, :, None, :],
        rg_conv_b=r[12:13], rg_b_a=cols(gate[_ROW_BA:_ROW_BA + 2])[None], rg_b_x=cols(gate[_ROW_BX:_ROW_BX + 2])[None],
        rg_lam=cols(gate[_ROW_LAM:_ROW_LAM + 2])[None], at_q_g=r[24:25, 0:HEAD_DIM],
        at_k_g=r[24:25, HEAD_DIM:2 * HEAD_DIM], rg_w_a=w_a, rg_w_x=w_x)


_WEIGHTS = ['norm_mix_g', 'norm_mlp_g', 'rg_w_in', 'rg_conv_w', 'rg_conv_b', 'rg_w_a', 'rg_b_a', 'rg_w_x', 'rg_b_x',
            'rg_lam', 'rg_w_out', 'at_w_qkv', 'at_q_g', 'at_k_g', 'at_w_o', 'mlp_w_up', 'mlp_w_down', 'final_g']
_BIG = dict(rg_w_in=["rg_w_in"], rg_w_out=["rg_w_out"], at_w_qkv=["at_w_qkv"], at_w_o=["at_w_o"],
            mlp_w_up=["up0", "up1"], mlp_w_down=["down0", "down1"])


def kernel(x, *args):
    n_w = len(_WEIGHTS)
    w = dict(zip(_WEIGHTS, args[:n_w]))
    target = args[n_w]
    m = dict(zip(_WEIGHTS, args[n_w + 1:2 * n_w + 1]))
    v = dict(zip(_WEIGHTS, args[2 * n_w + 1:3 * n_w + 1]))
    B, L, _ = x.shape
    T = B * L
    me = 2 * lax.axis_index("x") + lax.axis_index("y")

    vec = jnp.concatenate([_gate_vec_slot(w["rg_b_a"][0], w["rg_b_x"][0], w["rg_lam"][0]),
                           _rows_at(w["rg_conv_w"][0, :, 0, :], 0)], axis=0)
    flat = lambda a: a.reshape(-1, a.shape[-1])
    rows_of = lambda k: w[k].shape[-2]
    groups = [("rg", [("rg_w_in", 0, BF16), ("rg_w_out", 0, BF16), (vec, 0, F32)]),
              ("mlp0", [("mlp_w_up", 0, BF16), ("mlp_w_down", 0, BF16)]),
              ("att", [("at_w_qkv", 0, BF16), ("at_w_o", 0, BF16)]),
              ("mlp1", [("mlp_w_up", 1, BF16), ("mlp_w_down", 1, BF16)])]
    gathers, tok = {}, None
    for group, members in groups:
        lands = []
        for n, (k, layer, dtype) in enumerate(members):
            src, rows = (flat(w[k]), rows_of(k)) if isinstance(k, str) else (k, k.shape[0])
            lands.append(_cast_into_slot(src, layer * rows, rows, me, dtype, f"place_{group}{n}", after=tok))
        gathers[group], tok = _exchange_start("gather", [], lands, f"gather_{group}_start", after=tok)
    wcat = _make_wcat(w["rg_w_a"], w["rg_w_x"]).astype(BF16)

    def fetch(group, after):
        _, full = _exchange_wait("gather", gathers[group], after, f"gather_{group}_wait")
        if group == "rg":
            vec_full = jnp.transpose(full[2], (1, 0, 2)).reshape(2 * SUBLANES, D_MODEL)
            conv_wb = vec_full[SUBLANES:] + _rows_at(w["rg_conv_b"], 4)
            return full[0], full[1].reshape(D_MODEL, D_MODEL), conv_wb, wcat, vec_full[:SUBLANES]
        if group == "att":
            return full[0], full[1].reshape(D_MODEL, D_MODEL)
        return full[0], full[1].reshape(4 * D_MODEL, D_MODEL)

    names = dict(mlp1=["up1", "down1"], att=["at_w_qkv", "at_w_o"], mlp0=["up0", "down0"], rg_out=["rg_w_out"],
                 rg_in=["rg_w_in"], small=["small"])
    scatters, swaps, P, Q, res = {}, [], {}, {}, {}

    def start_scatter(group, grads):
        srcs = [g.reshape(N_CHIPS, -1, g.shape[-1]) for g in grads]
        lands = [lax.empty((N_CHIPS - 1,) + s.shape[1:], s.dtype) for s in srcs]
        scatters[group], token = _exchange_start("scatter", srcs, lands, f"scatter_{group}_start")
        return token

    def settle(groups, after):
        keys, parts = [], []
        for group in groups:
            srcs, lands = _exchange_wait("scatter", scatters[group], after, f"scatter_{group}_wait")
            for k, s, r in zip(names[group], srcs, lands):
                keys.append(k)
                parts.append(_sum_slots(s, r, me, f"sum_{k}"))
        handle, token = _exchange_start("swap", parts, [lax.empty(p.shape, F32) for p in parts],
                                        f"swap_{groups[0]}_start")
        swaps.append((keys, handle, f"swap_{groups[0]}_wait"))
        return token

    def finish(after):
        for keys, handle, name in swaps:
            mine, theirs = _exchange_wait("swap", handle, after, name)
            P.update(zip(keys, mine))
            Q.update(zip(keys, theirs))
        swaps.clear()
        last = after
        for k, parts in _BIG.items():
            if k in res or any(p not in P for p in parts):
                continue
            shape = w[k].shape
            two_d = lambda a: a.reshape(-1, shape[-1])
            outs = _adamw(two_d(w[k]), two_d(m[k]), two_d(v[k]), [P[p] for p in parts], [Q[p] for p in parts],
                          f"adamw_{k}")
            res[k] = [o.reshape(shape) for o in outs]
            last = outs[0]
        return last

    def emit(event, arrays):
        if event == "point_attn_done":
            return settle(["mlp1"], arrays[0])
        if event == "point_conv_done":
            return settle(["att", "mlp0", "rg_out"], arrays[0])
        token = start_scatter(event, arrays)
        return finish(token) if event == "rg_in" else token

    P_vec = dict(norm_mix_g=w["norm_mix_g"], norm_mlp_g=w["norm_mlp_g"], final_g=w["final_g"][None],
                 q_g=w["at_q_g"], k_g=w["at_k_g"])
    loss_part, grad_x, small = _local_step(x.reshape(T, D_MODEL), target.reshape(T, D_MODEL), P_vec, fetch, emit,
                                           B, L, after=tok)
    start_scatter("small", [small])
    loss = lax.psum(loss_part, ("x", "y", "c"))

    finish(settle(["rg_in", "small"], grad_x))
    small_piece = _add(P["small"], Q["small"], "small_grad")
    small_full = _all_gather_chips([small_piece], "gather_small")[0].reshape(PACK_ROWS, D_MODEL)
    outs = _adamw(_pack_small(w, me), _pack_small(m, me), _pack_small(v, me), [small_full], None, "adamw_small")
    unpacked = [_unpack_small(o, me) for o in outs]
    for k in _WEIGHTS:
        if k not in res:
            res[k] = [u[k] for u in unpacked]

    result = [loss, grad_x.reshape(B, L, D_MODEL)]
    for slot in range(4):
        result += [res[k][slot] for k in _WEIGHTS]
    return tuple(result)
```

```python
import functools
import math

import jax
import jax.numpy as jnp
import numpy as np
from jax import lax
from jax.experimental import pallas as pl
from jax.experimental.pallas import tpu as pltpu

F32 = jnp.float32
BF16 = jnp.bfloat16

D_MODEL = 1024
HEAD_DIM = 128
N_HEADS = 8
N_KV = 2
GROUP = N_HEADS // N_KV
LRU_BLOCKS = 8
LRU_BW = 128
GRID_W = 64
ROPE_THETA = 10000.0
EPS = 1e-6
RG_C = 8.0
SCALE = 1.0 / math.sqrt(HEAD_DIM)
N_CHIPS = 4

ADAM_LR = 0.001
ADAM_B1 = 0.9
ADAM_B2 = 0.999
ADAM_EPS = 1e-08
ADAM_WD = 0.01
ADAM_STEP = 10

V7X_VMEM_BYTES = 64 * 1024 * 1024
VMEM_LIMIT = V7X_VMEM_BYTES * 3 // 4
LANES = 128
SUBLANES = 8

N_DEVICES = 8
VEC_ROWS = 32


def _params(sem):
    return pltpu.CompilerParams(dimension_semantics=sem, vmem_limit_bytes=VMEM_LIMIT)


_ANY = pl.BlockSpec(memory_space=pl.ANY)
_NN = (((1,), (0,)), ((), ()))
_NT = (((1,), (1,)), ((), ()))
_TN = (((0,), (0,)), ((), ()))


def _after_operand(after):
    return [] if after is None else [after]


def _fit(t, n):
    if n <= t:
        return n
    c = (t // LANES) * LANES
    while n % c:
        c -= LANES
    return c


MM_VMEM_BUDGET = VMEM_LIMIT * 3 // 4
def _mm_tiles(M, K, ns, out_dtypes, extras, whole_rows):
    for tm in (2048, 1024, 512, 256, 128):
        for tn in ((ns,) if whole_rows else (1024, 512, 256)):
            tn = _fit(tn, ns)
            per_row = 2 * (2 * K) + 4 * tn + sum(2 * tn * jnp.dtype(d).itemsize for d in out_dtypes)
            per_row += sum(2 * tn * e.dtype.itemsize for e in extras)
            if M % tm == 0 and 2 * (2 * K * tn) + tm * per_row <= MM_VMEM_BUDGET:
                return tm, tn
    raise ValueError(f"no tile fits VMEM for M={M} K={K} N={ns}")


def _mm(a, b, *, mode, name, out_dtypes=(F32,), b_shard=False, o_shard=False, extras=(), epi=None, after=None,
        bcast=(), accs=(), ref_epi=None):
    if mode == "tn":
        K, M = a.shape
        N = b.shape[1]
    else:
        M, K = a.shape
        if mode == "nn":
            N = b.shape[0] * b.shape[2] if b_shard else b.shape[1]
        else:
            N = b.shape[1] if b_shard else b.shape[0]
    ns = N
    if b_shard and mode == "nn":
        ns = b.shape[2]
    elif o_shard:
        ns = N // N_CHIPS
    tm, tn = _mm_tiles(M, K, ns, out_dtypes, extras, whole_rows=ref_epi is not None)
    if ref_epi is not None:
        tm = min(tm, 512)
    grid = (M // tm, N // tn)
    q = ns // tn

    if mode == "tn":
        a_spec = pl.BlockSpec((K, tm), lambda i, j: (0, i))
        b_spec = pl.BlockSpec((K, tn), lambda i, j: (0, j))
        dims = _TN
    elif mode == "nn":
        a_spec = pl.BlockSpec((tm, K), lambda i, j: (i, 0))
        if b_shard:
            b_spec = pl.BlockSpec((None, K, tn), lambda i, j: (j // q, 0, j % q))
        else:
            b_spec = pl.BlockSpec((K, tn), lambda i, j: (0, j))
        dims = _NN
    else:
        a_spec = pl.BlockSpec((tm, K), lambda i, j: (i, 0))
        if b_shard:
            ks = b.shape[2]
            b_spec = pl.BlockSpec((N_CHIPS, tn, ks), lambda i, j: (0, j, 0))
        else:
            b_spec = pl.BlockSpec((tn, K), lambda i, j: (j, 0))
        dims = _NT

    if o_shard:
        o_specs = [pl.BlockSpec((None, tm, tn), lambda i, j: (j // q, i, j % q))]
        o_shapes = [jax.ShapeDtypeStruct((N_CHIPS, M, ns), out_dtypes[0])]
    else:
        o_specs = [pl.BlockSpec((tm, tn), lambda i, j: (i, j)) for _ in out_dtypes]
        o_shapes = [jax.ShapeDtypeStruct((M, N), dt) for dt in out_dtypes]
    e_specs = [pl.BlockSpec((tm, tn), lambda i, j: (i, j)) for _ in extras]
    e_specs += [pl.BlockSpec(v.shape, lambda i, j: (0, 0)) for v in bcast]
    o_specs += [pl.BlockSpec(s, lambda i, j: (0, 0)) for s in accs]
    o_shapes += [jax.ShapeDtypeStruct(s, F32) for s in accs]
    n_e, n_b, n_o, n_a = len(extras), len(bcast), len(out_dtypes), len(accs)
    order = _after_operand(after)
    n_x = len(order)
    if epi is None:
        epi = lambda acc: (acc,)

    def body(a_ref, b_ref, *rest):
        e_refs, b_refs = rest[:n_e], rest[n_e:n_e + n_b]
        o_refs = rest[n_e + n_b + n_x:n_e + n_b + n_x + n_o]
        a_refs = rest[n_e + n_b + n_x + n_o:]
        if n_a:
            @pl.when((pl.program_id(0) == 0) & (pl.program_id(1) == 0))
            def _():
                for r in a_refs:
                    r[...] = jnp.zeros(r.shape, F32)
        if mode == "nt" and b_shard:
            acc = None
            for s in range(N_CHIPS):
                part = lax.dot_general(a_ref[:, s * ks:(s + 1) * ks], b_ref[s], dims, preferred_element_type=F32)
                acc = part if acc is None else acc + part
        else:
            acc = lax.dot_general(a_ref[...], b_ref[...], dims, preferred_element_type=F32)
        if ref_epi is not None:
            ref_epi(acc, e_refs, b_refs, o_refs, a_refs)
            return
        outs = epi(acc, *[r[...] for r in e_refs])
        for r, o in zip(o_refs, outs):
            r[...] = o.astype(r.dtype)

    outs = pl.pallas_call(
        body, name=name, grid=grid, in_specs=[a_spec, b_spec] + e_specs + [_ANY] * n_x, out_specs=o_specs,
        out_shape=o_shapes, compiler_params=_params(("arbitrary", "arbitrary") if n_a else ("parallel", "parallel")),
    )(a, b, *extras, *bcast, *order)
    return outs[0] if n_o + n_a == 1 else outs


def _rowwise(fn, rows, bcast, outs, accs=(), *, tm, name, after=None):
    def norm(r):
        return r if isinstance(r, tuple) else (r, r.shape[1], 0)

    rows = [norm(r) for r in rows]
    T = rows[0][0].shape[0]
    tm = min(tm, T)
    while T % tm:
        tm -= SUBLANES
    n_r, n_b, n_o, n_a = len(rows), len(bcast), len(outs), len(accs)
    order = _after_operand(after)
    n_x = len(order)
    in_specs = [pl.BlockSpec((tm, c), functools.partial(lambda i, cb: (i, cb), cb=cb)) for _, c, cb in rows]
    in_specs += [pl.BlockSpec(b.shape, lambda i: (0, 0)) for b in bcast] + [_ANY] * n_x
    out_specs = [pl.BlockSpec((tm, o[0]), lambda i: (i, 0)) for o in outs]
    out_specs += [pl.BlockSpec(s, lambda i: (0, 0)) for s in accs]
    out_shape = [jax.ShapeDtypeStruct((T, o[2] if len(o) > 2 else o[0]), o[1]) for o in outs]
    out_shape += [jax.ShapeDtypeStruct(s, F32) for s in accs]

    def body(*refs):
        in_refs = refs[:n_r]
        b_refs = refs[n_r:n_r + n_b]
        o_refs = refs[n_r + n_b + n_x:n_r + n_b + n_x + n_o]
        a_refs = refs[n_r + n_b + n_x + n_o:]
        if n_a:
            @pl.when(pl.program_id(0) == 0)
            def _():
                for r in a_refs:
                    r[...] = jnp.zeros(r.shape, F32)
        fn(in_refs, b_refs, o_refs, a_refs)

    res = pl.pallas_call(
        body, name=name, grid=(T // tm,), in_specs=in_specs, out_specs=out_specs, out_shape=out_shape,
        compiler_params=_params(("arbitrary",) if n_a else ("parallel",)),
    )(*[r[0] for r in rows], *bcast, *order)
    return res


def _rsum(x):
    return jnp.sum(x, axis=0, keepdims=True)


def _rms_fwd(x, g, name, after=None):
    def fn(ins, bs, outs, accs):
        xv = ins[0][...]
        r = lax.rsqrt(jnp.mean(xv * xv, axis=-1, keepdims=True) + EPS)
        outs[0][...] = (xv * r * bs[0][...]).astype(BF16)

    return _rowwise(fn, [x], [g], [(D_MODEL, BF16)], tm=512, name=name, after=after)[0]


def _rms_bwd_math(xv, dh, g):
    r = lax.rsqrt(jnp.mean(xv * xv, axis=-1, keepdims=True) + EPS)
    hn = xv * r
    dgh = dh * g
    dx = r * (dgh - hn * jnp.mean(dgh * hn, axis=-1, keepdims=True))
    return dx, _rsum(dh * hn)


def _mm_norm_bwd(dy, w, x, dres, g, name, after=None):
    def epilogue(acc, e_refs, b_refs, o_refs, a_refs):
        dx, dg = _rms_bwd_math(e_refs[0][...], acc, b_refs[0][...])
        dx = dx + e_refs[1][...]
        o_refs[0][...] = dx
        o_refs[1][...] = dx.astype(BF16)
        a_refs[0][...] += dg

    return _mm(dy, w, mode="nt", b_shard=True, out_dtypes=(F32, BF16), extras=(x, dres), bcast=(g,),
               accs=((1, D_MODEL),), ref_epi=epilogue, name=name, after=after)


def _mm_res_norm(a, w, res, g, name):
    def epilogue(acc, e_refs, b_refs, o_refs, a_refs):
        xv = acc + e_refs[0][...]
        o_refs[0][...] = xv
        r = lax.rsqrt(jnp.mean(xv * xv, axis=-1, keepdims=True) + EPS)
        o_refs[1][...] = (xv * r * b_refs[0][...]).astype(BF16)

    return _mm(a, w, mode="nn", out_dtypes=(F32, BF16), extras=(res,), bcast=(g,), ref_epi=epilogue, name=name)


def _mm_final_loss(a, w, res, target, g, name):
    def epilogue(acc, e_refs, b_refs, o_refs, a_refs):
        xv = acc + e_refs[0][...]
        gv = b_refs[0][...]
        r = lax.rsqrt(jnp.mean(xv * xv, axis=-1, keepdims=True) + EPS)
        e = xv * r * gv - e_refs[1][...]
        tok = jnp.mean(e * e, axis=-1, keepdims=True)
        a_refs[0][...] += 0.5 * jnp.sum(tok, axis=0, keepdims=True) * jnp.ones((1, LANES), F32)
        dx, dg = _rms_bwd_math(xv, e * (1.0 / D_MODEL), gv)
        o_refs[0][...] = dx
        o_refs[1][...] = dx.astype(BF16)
        a_refs[1][...] += dg

    return _mm(a, w, mode="nn", out_dtypes=(F32, BF16), extras=(res, target), bcast=(g,),
               accs=((1, LANES), (1, D_MODEL)), ref_epi=epilogue, name=name)


def _relu2(acc):
    r = jnp.maximum(acc, 0.0)
    return r * r, r


def _mlp_fwd(x, h, fetch, tag, finish):
    w_up, w_down = fetch(f"mlp{tag}", h)
    a, r = _mm(h, w_up, mode="nn", b_shard=True, out_dtypes=(BF16, BF16), epi=_relu2, name=f"mlp{tag}_up")
    return finish(a, w_down, x, f"mlp{tag}_down"), (h, a, r, w_up, w_down)


def _mlp_bwd(x, g, saved, dx, dx_bf, tag, after):
    h, a, r, w_up, w_down = saved
    d_down = _mm(a, dx_bf, mode="tn", out_dtypes=(BF16,), name=f"mlp{tag}_dwdown", after=after)
    dup = _mm(dx_bf, w_down, mode="nt", extras=(r,), out_dtypes=(BF16,),
              epi=lambda acc, rv: (acc * (2.0 * rv.astype(F32)),), name=f"mlp{tag}_dup")
    d_up = _mm(h, dup, mode="tn", o_shard=True, out_dtypes=(BF16,), name=f"mlp{tag}_dwup")
    dx_new, dx_new_bf, dg = _mm_norm_bwd(dup, w_up, x, dx, g, f"mlp{tag}_dh")
    return dx_new, dx_new_bf, dg, d_up, d_down


def _rope_tables(L, B):
    rows = L // GRID_W
    row = np.repeat(np.arange(rows, dtype=np.float32), GRID_W)
    col = np.tile(np.arange(GRID_W, dtype=np.float32), rows)
    inv = (ROPE_THETA ** (-np.arange(HEAD_DIM // 4, dtype=np.float32) / (HEAD_DIM // 4))).astype(np.float32)
    ar, ac = row[:, None] * inv, col[:, None] * inv
    cos = np.concatenate([np.cos(ar), np.cos(ar), np.cos(ac), np.cos(ac)], axis=-1)
    sin = np.concatenate([-np.sin(ar), np.sin(ar), -np.sin(ac), np.sin(ac)], axis=-1)
    return jnp.asarray(np.tile(cos, (B, 1)), F32), jnp.asarray(np.tile(sin, (B, 1)), F32)


def _swap_halves(x):
    lane = lax.broadcasted_iota(jnp.int32, x.shape, 1)
    return jnp.where((lane % 64) < 32, pltpu.roll(x, HEAD_DIM - 32, 1), pltpu.roll(x, 32, 1))


def _qk_prep(qkv, cos, sin, q_g, k_g):
    def fn(ins, bs, outs, accs):
        c, s = ins[1][...], ins[2][...]
        for h in range(N_HEADS + N_KV):
            xv = ins[0][:, h * HEAD_DIM:(h + 1) * HEAD_DIM]
            g = bs[0][...] if h < N_HEADS else bs[1][...]
            r = lax.rsqrt(jnp.mean(xv * xv, axis=-1, keepdims=True) + EPS)
            z = xv * r * g
            y = (z * c + _swap_halves(z) * s).astype(BF16)
            if h < N_HEADS:
                outs[0][:, h * HEAD_DIM:(h + 1) * HEAD_DIM] = y
            else:
                outs[1][:, (h - N_HEADS) * HEAD_DIM:(h - N_HEADS + 1) * HEAD_DIM] = y
        outs[2][...] = ins[0][:, (N_HEADS + N_KV) * HEAD_DIM:].astype(BF16)

    kvw = N_KV * HEAD_DIM
    return _rowwise(fn, [qkv, cos, sin], [q_g, k_g], [(D_MODEL, BF16), (kvw, BF16), (kvw, BF16)], tm=512,
                    name="attn_qk_prep")


def _qk_prep_bwd(qkv, dq, dk, dv, cos, sin, q_g, k_g):
    def fn(ins, bs, outs, accs):
        c, s = ins[4][...], ins[5][...]
        for h in range(N_HEADS + N_KV):
            sl = slice(h * HEAD_DIM, (h + 1) * HEAD_DIM)
            xv = ins[0][:, sl]
            if h < N_HEADS:
                g, dy, acc = bs[0][...], ins[1][:, sl], accs[0]
            else:
                ks = slice((h - N_HEADS) * HEAD_DIM, (h - N_HEADS + 1) * HEAD_DIM)
                g, dy, acc = bs[1][...], ins[2][:, ks], accs[1]
            r = lax.rsqrt(jnp.mean(xv * xv, axis=-1, keepdims=True) + EPS)
            xn = xv * r
            dz = dy * c - _swap_halves(dy) * s
            acc[...] += _rsum(dz * xn)
            dxn = dz * g
            outs[0][:, sl] = (r * (dxn - xn * jnp.mean(dxn * xn, axis=-1, keepdims=True))).astype(BF16)
        outs[0][:, (N_HEADS + N_KV) * HEAD_DIM:] = ins[3][...].astype(BF16)

    return _rowwise(fn, [qkv, dq, dk, dv, cos, sin], [q_g, k_g], [(qkv.shape[1], BF16)],
                    [(1, HEAD_DIM), (1, HEAD_DIM)], tm=256, name="attn_qk_prep_bwd")


_EXP2_SCALE = SCALE * math.log2(math.e)


def _exp_rows(q, k):
    s = lax.dot_general(q, k, _NT, preferred_element_type=F32)
    p = jnp.exp2((s - jnp.max(s, axis=-1, keepdims=True)) * _EXP2_SCALE)
    return p, jnp.sum(p, axis=-1, keepdims=True)


def _attn_fwd(q, k, v, B, L, tq=1024, sub=256):
    tq = min(tq, L)
    sub = min(sub, tq)
    nq = L // tq

    def body(q_ref, k_ref, v_ref, o_ref):
        kv, vv = k_ref[...], v_ref[...]
        for c in range(tq // sub):
            rows = slice(c * sub, (c + 1) * sub)
            p, l = _exp_rows(q_ref[rows, :], kv)
            o = jnp.dot(p.astype(BF16), vv, preferred_element_type=F32)
            o_ref[rows, :] = (o * (1.0 / l)).astype(o_ref.dtype)

    return pl.pallas_call(
        body, name="attn_fwd", grid=(B, N_HEADS, nq),
        in_specs=[pl.BlockSpec((tq, HEAD_DIM), lambda b, h, i: (b * nq + i, h)),
                  pl.BlockSpec((L, HEAD_DIM), lambda b, h, i: (b, h // GROUP)),
                  pl.BlockSpec((L, HEAD_DIM), lambda b, h, i: (b, h // GROUP))],
        out_specs=pl.BlockSpec((tq, HEAD_DIM), lambda b, h, i: (b * nq + i, h)),
        out_shape=jax.ShapeDtypeStruct((B * L, D_MODEL), BF16),
        compiler_params=_params(("parallel", "parallel", "parallel")),
    )(q, k, v)


def _attn_bwd(q, k, v, do, B, L, tq=512, sub=256):
    tq = min(tq, L)
    sub = min(sub, tq)
    nq = L // tq

    def body(q_ref, k_ref, v_ref, do_ref, dq_ref, dk_ref, dv_ref):
        @pl.when((pl.program_id(2) == 0) & (pl.program_id(3) == 0))
        def _():
            dk_ref[...] = jnp.zeros(dk_ref.shape, F32)
            dv_ref[...] = jnp.zeros(dv_ref.shape, F32)

        kv, vv = k_ref[...], v_ref[...]
        ps, es, dos, qs = [], [], [], []
        for c in range(tq // sub):
            rows = slice(c * sub, (c + 1) * sub)
            qc, doc = q_ref[rows, :], do_ref[rows, :]
            p, l = _exp_rows(qc, kv)
            inv = 1.0 / l
            dp = lax.dot_general(doc, vv, _NT, preferred_element_type=F32)
            delta = jnp.sum(p * dp, axis=-1, keepdims=True) * inv
            e = (p * (dp - delta)).astype(BF16)
            dq_ref[rows, :] = jnp.dot(e, kv, preferred_element_type=F32) * (inv * SCALE)
            ps.append(p.astype(BF16))
            es.append(e)
            dos.append((doc.astype(F32) * inv).astype(BF16))
            qs.append((qc.astype(F32) * (inv * SCALE)).astype(BF16))
        cat = lambda xs: xs[0] if len(xs) == 1 else jnp.concatenate(xs, axis=0)
        dv_ref[...] += lax.dot_general(cat(ps), cat(dos), _TN, preferred_element_type=F32)
        dk_ref[...] += lax.dot_general(cat(es), cat(qs), _TN, preferred_element_type=F32)

    qmap = lambda b, kh, g, i: (b * nq + i, kh * GROUP + g)
    kmap = lambda b, kh, g, i: (b, kh)
    kvw = N_KV * HEAD_DIM
    return pl.pallas_call(
        body, name="attn_bwd", grid=(B, N_KV, GROUP, nq),
        in_specs=[pl.BlockSpec((tq, HEAD_DIM), qmap), pl.BlockSpec((L, HEAD_DIM), kmap),
                  pl.BlockSpec((L, HEAD_DIM), kmap), pl.BlockSpec((tq, HEAD_DIM), qmap)],
        out_specs=[pl.BlockSpec((tq, HEAD_DIM), qmap), pl.BlockSpec((L, HEAD_DIM), kmap),
                   pl.BlockSpec((L, HEAD_DIM), kmap)],
        out_shape=[jax.ShapeDtypeStruct((B * L, D_MODEL), F32), jax.ShapeDtypeStruct((B * L, kvw), F32),
                   jax.ShapeDtypeStruct((B * L, kvw), F32)],
        compiler_params=_params(("parallel", "parallel", "arbitrary", "arbitrary")),
    )(q, k, v, do)


def _conv_shift(x, t, L, k):
    if k == 2:
        return x
    if k < 2:
        return jnp.where(t >= 2 - k, pltpu.roll(x, 2 - k, 0), 0.0)
    return jnp.where(t < L - (k - 2), pltpu.roll(x, L - (k - 2), 0), 0.0)


def _conv_fwd(z, wb, B, L, tc=256):
    noff = D_MODEL // tc

    def body(z_ref, w_ref, o_ref):
        x = z_ref[...]
        t = lax.broadcasted_iota(jnp.int32, x.shape, 0)
        acc = w_ref[4:5, :] + w_ref[2:3, :] * x
        for k in (0, 1, 3):
            acc = acc + w_ref[k:k + 1, :] * _conv_shift(x, t, L, k)
        o_ref[...] = acc

    return pl.pallas_call(
        body, name="rg_conv", grid=(B, noff),
        in_specs=[pl.BlockSpec((L, tc), lambda b, j: (b, noff + j)), pl.BlockSpec((SUBLANES, tc), lambda b, j: (0, j))],
        out_specs=pl.BlockSpec((L, tc), lambda b, j: (b, j)),
        out_shape=jax.ShapeDtypeStruct((B * L, D_MODEL), F32),
        compiler_params=_params(("parallel", "parallel")),
    )(z, wb)


def _conv_bwd(z, g, wb, dz, B, L, tc=256, after=None):
    noff = D_MODEL // tc
    order = _after_operand(after)

    def body(z_ref, g_ref, w_ref, dz_in, *rest):
        dx_ref, dw_ref = rest[len(order):]

        @pl.when(pl.program_id(1) == 0)
        def _():
            dw_ref[...] = jnp.zeros(dw_ref.shape, F32)

        x, gv = z_ref[...], g_ref[...]
        t = lax.broadcasted_iota(jnp.int32, x.shape, 0)
        dx = w_ref[2:3, :] * gv
        for k in (0, 1, 3):
            dx = dx + w_ref[k:k + 1, :] * _conv_shift(gv, t, L, 4 - k)
        dx_ref[...] = dx.astype(BF16)
        for k in range(4):
            dw_ref[k:k + 1, :] += _rsum(_conv_shift(x, t, L, k) * gv)
        dw_ref[4:5, :] += _rsum(gv)

    return pl.pallas_call(
        body, name="rg_conv_bwd", grid=(noff, B),
        in_specs=[pl.BlockSpec((L, tc), lambda j, b: (b, noff + j)), pl.BlockSpec((L, tc), lambda j, b: (b, j)),
                  pl.BlockSpec((SUBLANES, tc), lambda j, b: (0, j)), _ANY] + [_ANY] * len(order),
        out_specs=[pl.BlockSpec((L, tc), lambda j, b: (b, noff + j)),
                   pl.BlockSpec((SUBLANES, tc), lambda j, b: (0, j))],
        out_shape=[jax.ShapeDtypeStruct(dz.shape, dz.dtype), jax.ShapeDtypeStruct((SUBLANES, D_MODEL), F32)],
        input_output_aliases={3: 0},
        compiler_params=_params(("parallel", "arbitrary")),
    )(z, g, wb, dz, *order)


def _softplus(x):
    return jnp.maximum(x, 0.0) + jnp.log1p(jnp.exp(-jnp.abs(x)))


_ROW_BA, _ROW_BX, _ROW_LAM = 0, 2, 4


def _gate_math(xb, pre, vec_ref, d, sl):
    pa = pre[:, (2 * d) * LRU_BW:(2 * d + 1) * LRU_BW] + vec_ref[_ROW_BA + d:_ROW_BA + d + 1, sl]
    px = pre[:, (2 * d + 1) * LRU_BW:(2 * d + 2) * LRU_BW] + vec_ref[_ROW_BX + d:_ROW_BX + d + 1, sl]
    r = 0.5 * jnp.tanh(0.5 * pa) + 0.5
    i = 0.5 * jnp.tanh(0.5 * px) + 0.5
    sp = _softplus(-vec_ref[_ROW_LAM + d:_ROW_LAM + d + 1, sl])
    log_a = (-RG_C) * r * sp
    a = jnp.exp(log_a)
    th = jnp.tanh(log_a)
    om = -2.0 * th / (1.0 - th)
    mult = jnp.sqrt(om)
    return a, mult * (i * xb), (r, i, sp, om, mult)


def _gate_fwd(rec, wcat, gvec):
    def fn(ins, bs, outs, accs):
        for blk in range(LRU_BLOCKS):
            sl = slice(blk * LRU_BW, (blk + 1) * LRU_BW)
            xb = ins[0][:, sl]
            pre = jnp.dot(xb.astype(BF16), bs[0][sl, :], preferred_element_type=F32)
            for d in range(2):
                a, u, _ = _gate_math(xb, pre, bs[1], d, sl)
                outs[2 * d][:, sl] = a
                outs[2 * d + 1][:, sl] = u

    return _rowwise(fn, [rec], [wcat, gvec], [(D_MODEL, F32)] * 4, tm=256, name="rg_gate")


def _gate_bwd(rec, du_f, da_f, du_b, da_b, wcat, gvec):
    def fn(ins, bs, outs, accs):
        for blk in range(LRU_BLOCKS):
            sl = slice(blk * LRU_BW, (blk + 1) * LRU_BW)
            xb = ins[0][:, sl]
            xb16 = xb.astype(BF16)
            w = bs[0][sl, :]
            pre = jnp.dot(xb16, w, preferred_element_type=F32)
            dx = jnp.zeros_like(xb)
            dpre = []
            for d in range(2):
                a, _, (r, i, sp, om, mult) = _gate_math(xb, pre, bs[1], d, sl)
                du, da = ins[1 + 2 * d][:, sl], ins[2 + 2 * d][:, sl]
                d_i = du * mult * xb
                d_mult = du * i * xb
                dx = dx + du * mult * i
                dlog = da * a - d_mult * (1.0 - om) / mult
                d_r = dlog * ((-RG_C) * sp)
                d_sp = _rsum(dlog * ((-RG_C) * r))
                lam = bs[1][_ROW_LAM + d:_ROW_LAM + d + 1, sl]
                accs[2][_ROW_LAM + d:_ROW_LAM + d + 1, sl] += d_sp * (-jax.nn.sigmoid(-lam))
                dpa = d_r * r * (1.0 - r)
                dpx = d_i * i * (1.0 - i)
                accs[2][_ROW_BA + d:_ROW_BA + d + 1, sl] += _rsum(dpa)
                accs[2][_ROW_BX + d:_ROW_BX + d + 1, sl] += _rsum(dpx)
                dpre += [dpa, dpx]
            dpre = jnp.concatenate(dpre, axis=1).astype(BF16)
            dw = lax.dot_general(xb16, dpre, _TN, preferred_element_type=F32)
            for d in range(2):
                rows = slice(d * D_MODEL + blk * LRU_BW, d * D_MODEL + (blk + 1) * LRU_BW)
                accs[0][rows, :] += dw[:, (2 * d) * LRU_BW:(2 * d + 1) * LRU_BW]
                accs[1][rows, :] += dw[:, (2 * d + 1) * LRU_BW:(2 * d + 2) * LRU_BW]
            outs[0][:, sl] = dx + lax.dot_general(dpre, w, _NT, preferred_element_type=F32)

    gate_shape = (2 * D_MODEL, LRU_BW)
    return _rowwise(fn, [rec, du_f, da_f, du_b, da_b], [wcat, gvec], [(D_MODEL, F32)],
                    [gate_shape, gate_shape, (SUBLANES, D_MODEL)], tm=256, name="rg_gate_bwd")


def _as_time_blocks(x):
    return x.reshape(x.shape[0] // SUBLANES, SUBLANES, x.shape[1])


def _scan_call(body, ins, n_out, B, L, tc, name):
    nb = L // SUBLANES
    spec = pl.BlockSpec((nb, SUBLANES, tc), lambda b, j: (b, 0, j))
    T = ins[0].shape[0]
    outs = pl.pallas_call(
        functools.partial(body, nb), name=name, grid=(B, D_MODEL // tc),
        in_specs=[spec] * len(ins), out_specs=[spec] * n_out,
        out_shape=[jax.ShapeDtypeStruct((T // SUBLANES, SUBLANES, D_MODEL), F32)] * n_out,
        compiler_params=_params(("parallel", "parallel")),
    )(*[_as_time_blocks(x) for x in ins])
    return [o.reshape(T, D_MODEL) for o in outs]


def _scan_fwd(a_f, u_f, a_b, u_b, B, L, tc=256):
    def body(nb, af, uf, ab, ub, hf, hb):
        def step(i, carry):
            h1, h2 = carry
            ib = nb - 1 - i
            for j in range(SUBLANES):
                jb = SUBLANES - 1 - j
                h1 = af[i, j:j + 1, :] * h1 + uf[i, j:j + 1, :]
                hf[i, j:j + 1, :] = h1
                h2 = ab[ib, jb:jb + 1, :] * h2 + ub[ib, jb:jb + 1, :]
                hb[ib, jb:jb + 1, :] = h2
            return h1, h2

        zero = jnp.zeros((1, tc), F32)
        lax.fori_loop(0, nb, step, (zero, zero))

    return _scan_call(body, [a_f, u_f, a_b, u_b], 2, B, L, tc, "rg_scan")


def _scan_bwd(dy, a_f, h_f, a_b, h_b, B, L, tc=256):
    def body(nb, dy_r, af, hf, ab, hb, duf, daf, dub, dab):
        def step(i, carry):
            c1, c2 = carry
            ir = nb - 1 - i
            for j in range(SUBLANES):
                jr = SUBLANES - 1 - j
                lam1 = dy_r[ir, jr:jr + 1, :] + c1
                if jr > 0:
                    prev = hf[ir, jr - 1:jr, :]
                else:
                    prev = hf[jnp.maximum(ir - 1, 0), SUBLANES - 1:SUBLANES, :] * (ir > 0).astype(F32)
                duf[ir, jr:jr + 1, :] = lam1
                daf[ir, jr:jr + 1, :] = lam1 * prev
                c1 = af[ir, jr:jr + 1, :] * lam1
                lam2 = dy_r[i, j:j + 1, :] + c2
                if j < SUBLANES - 1:
                    nxt = hb[i, j + 1:j + 2, :]
                else:
                    nxt = hb[jnp.minimum(i + 1, nb - 1), 0:1, :] * (i < nb - 1).astype(F32)
                dub[i, j:j + 1, :] = lam2
                dab[i, j:j + 1, :] = lam2 * nxt
                c2 = ab[i, j:j + 1, :] * lam2
            return c1, c2

        zero = jnp.zeros((1, tc), F32)
        lax.fori_loop(0, nb, step, (zero, zero))

    return _scan_call(body, [dy, a_f, h_f, a_b, h_b], 4, B, L, tc, "rg_scan_bwd")


_GELU_C = math.sqrt(2.0 / math.pi)


def _gelu_parts(x):
    th = jnp.tanh(_GELU_C * (x + 0.044715 * x * x * x))
    return 0.5 * x * (1.0 + th), th


def _gated_out(h_f, h_b, z):
    def fn(ins, bs, outs, accs):
        gl, _ = _gelu_parts(ins[2][...])
        outs[0][...] = ((ins[0][...] + ins[1][...]) * gl).astype(BF16)

    return _rowwise(fn, [h_f, h_b, (z, D_MODEL, 0)], [], [(D_MODEL, BF16)], tm=512, name="rg_gated_out")[0]


def _gated_out_bwd(dyg, h_f, h_b, z):
    def fn(ins, bs, outs, accs):
        x = ins[3][...]
        gl, th = _gelu_parts(x)
        dgl = 0.5 * (1.0 + th) + 0.5 * x * (1.0 - th * th) * (_GELU_C * (1.0 + 3.0 * 0.044715 * x * x))
        g = ins[0][...]
        outs[0][...] = g * gl
        outs[1][...] = (g * (ins[1][...] + ins[2][...]) * dgl).astype(BF16)

    return _rowwise(fn, [dyg, h_f, h_b, (z, D_MODEL, 0)], [], [(D_MODEL, F32), (D_MODEL, BF16, 2 * D_MODEL)], tm=512,
                    name="rg_gated_out_bwd")


def _make_wcat(w_a, w_x):
    g = jnp.stack([w_a[0, 0], w_x[0, 0], w_a[0, 1], w_x[0, 1]])
    return jnp.transpose(g, (1, 2, 0, 3)).reshape(D_MODEL, 4 * LRU_BW)


def _rows_at(part, first):
    return jnp.pad(part, ((first, SUBLANES - first - part.shape[0]), (0, 0)))


def _qk_slot(q_g, k_g):
    wide = lambda v, at: jnp.pad(v, ((0, SUBLANES - 1), (at, D_MODEL - at - HEAD_DIM)))
    return wide(q_g, 0) + wide(k_g, HEAD_DIM)


def _local_step(x, target, P, fetch, emit, B, L, after=None):
    g_mix, g_mlp = P["norm_mix_g"], P["norm_mlp_g"]
    h0 = _rms_fwd(x, g_mix[0:1], "rg_norm", after=after)
    w_in, w_out, conv_wb, wcat, gvec = fetch("rg", h0)
    z = _mm(h0, w_in, mode="nn", b_shard=True, name="rg_in")
    rec = _conv_fwd(z, conv_wb, B, L)
    a_f, u_f, a_b, u_b = _gate_fwd(rec, wcat, gvec)
    h_f, h_b = _scan_fwd(a_f, u_f, a_b, u_b, B, L)
    yg = _gated_out(h_f, h_b, z)
    x1, h1 = _mm_res_norm(yg, w_out, x, g_mlp[0:1], "rg_out")
    (x2, h3), mlp0 = _mlp_fwd(x1, h1, fetch, 0, lambda a, w, res, name: _mm_res_norm(a, w, res, g_mix[1:2], name))
    w_qkv, w_o = fetch("att", h3)
    qkv = _mm(h3, w_qkv, mode="nn", b_shard=True, name="attn_qkv")
    cos, sin = _rope_tables(L, B)
    qh, kh, vh = _qk_prep(qkv, cos, sin, P["q_g"], P["k_g"])
    o = _attn_fwd(qh, kh, vh, B, L)
    x3, h4 = _mm_res_norm(o, w_o, x2, g_mlp[1:2], "attn_out")
    (dx4, dx4_bf, loss_acc, d_final_g), mlp1 = _mlp_fwd(
        x3, h4, fetch, 1, lambda a, w, res, name: _mm_final_loss(a, w, res, target, P["final_g"], name))

    dx3, dx3_bf, dg_mlp1, d_up1, d_down1 = _mlp_bwd(x3, g_mlp[1:2], mlp1, dx4, dx4_bf, 1, None)
    tok = emit("mlp1", [d_up1, d_down1])
    d_wo = _mm(o, dx3_bf, mode="tn", out_dtypes=(BF16,), name="attn_dwo", after=tok)
    do = _mm(dx3_bf, w_o, mode="nt", out_dtypes=(BF16,), name="attn_do")
    dq, dk, dv = _attn_bwd(qh, kh, vh, do, B, L)
    dqkv, dq_g, dk_g = _qk_prep_bwd(qkv, dq, dk, dv, cos, sin, P["q_g"], P["k_g"])
    d_wqkv = _mm(h3, dqkv, mode="tn", o_shard=True, out_dtypes=(BF16,), name="attn_dwqkv")
    tok = emit("att", [d_wqkv, d_wo])
    dx2, dx2_bf, dg_mix1 = _mm_norm_bwd(dqkv, w_qkv, x2, dx3, g_mix[1:2], "attn_dh", after=tok)
    tok = emit("point_attn_done", [dx2_bf])
    dx1, dx1_bf, dg_mlp0, d_up0, d_down0 = _mlp_bwd(x1, g_mlp[0:1], mlp0, dx2, dx2_bf, 0, tok)
    tok = emit("mlp0", [d_up0, d_down0])
    d_wout = _mm(yg, dx1_bf, mode="tn", out_dtypes=(BF16,), name="rg_dwout", after=tok)
    tok = emit("rg_out", [d_wout])
    dyg = _mm(dx1_bf, w_out, mode="nt", name="rg_dyg", after=tok)
    dy, dgate = _gated_out_bwd(dyg, h_f, h_b, z)
    du_f, da_f, du_b, da_b = _scan_bwd(dy, a_f, h_f, a_b, h_b, B, L)
    drec_c, d_wa, d_wx, d_gvec = _gate_bwd(rec, du_f, da_f, du_b, da_b, wcat, gvec)
    tok = emit("gates", [d_wa, d_wx])
    dz, d_convwb = _conv_bwd(z, drec_c, conv_wb, dgate, B, L, after=tok)
    tok = emit("point_conv_done", [dz])
    d_win = _mm(h0, dz, mode="tn", o_shard=True, out_dtypes=(BF16,), name="rg_dwin", after=tok)
    tok = emit("rg_in", [d_win])
    grad_x, _, dg_mix0 = _mm_norm_bwd(dz, w_in, x, dx1, g_mix[0:1], "rg_dh", after=tok)

    norms = (_rows_at(dg_mix0, 0) + _rows_at(dg_mix1, 1) + _rows_at(dg_mlp0, 2) + _rows_at(dg_mlp1, 3)
             + _rows_at(d_final_g, 4))
    vec = jnp.concatenate([norms, d_convwb, d_gvec, _qk_slot(dq_g, dk_g)], axis=0)
    return loss_acc[0, 0], grad_x, vec


_MESH = pl.DeviceIdType.MESH


def _place():
    x, y, c = lax.axis_index("x"), lax.axis_index("y"), lax.axis_index("c")
    peers = [((1 - x) if j & 2 else x, (1 - y) if j & 1 else y) for j in (1, 2, 3)]
    return x, y, c, peers


def _comm_call(body, ins, out_shapes, n_sem, name):
    return pl.pallas_call(
        body, name=name, in_specs=[_ANY] * len(ins), out_specs=[_ANY] * len(out_shapes), out_shape=out_shapes,
        scratch_shapes=[pltpu.SemaphoreType.DMA((n_sem,)), pltpu.SemaphoreType.DMA((n_sem,)),
                        pltpu.SemaphoreType.DMA((len(ins),))],
    )(*ins)


def _all_devices_slots(v, name):
    def body(v_ref, out_ref, send, recv, lsem):
        x, y, c = lax.axis_index("x"), lax.axis_index("y"), lax.axis_index("c")
        me = 4 * x + 2 * y + c

        def peer(j):
            return (1 - x) if j & 4 else x, (1 - y) if j & 2 else y, (1 - c) if j & 1 else c

        def copy(j, slot):
            return pltpu.make_async_remote_copy(
                src_ref=v_ref, dst_ref=out_ref.at[slot], send_sem=send.at[j - 1], recv_sem=recv.at[j - 1],
                device_id=peer(j), device_id_type=_MESH)

        local = pltpu.make_async_copy(v_ref, out_ref.at[me], lsem.at[0])
        sends = [copy(j, me) for j in range(1, N_DEVICES)]
        for cp in [local] + sends:
            cp.start()
        for j in range(1, N_DEVICES):
            px, py, pc = peer(j)
            copy(j, 4 * px + 2 * py + pc).wait_recv()
        for cp in sends:
            cp.wait_send()
        local.wait()

    shape = jax.ShapeDtypeStruct((N_DEVICES,) + v.shape, v.dtype)
    return _comm_call(body, [v], [shape], N_DEVICES - 1, name)[0]


def _sum_leading(slots, name):
    def body(s_ref, o_ref):
        acc = s_ref[0]
        for d in range(1, slots.shape[0]):
            acc = acc + s_ref[d]
        o_ref[...] = acc

    return pl.pallas_call(body, name=name, out_shape=jax.ShapeDtypeStruct(slots.shape[1:], slots.dtype))(slots)


_HBM = pl.BlockSpec(memory_space=pltpu.HBM)
_SEM = pl.BlockSpec(memory_space=pltpu.SEMAPHORE)
_EFFECT = pltpu.SideEffectType.DATAFLOW_SIDE_EFFECTING


_COPIES = dict(gather=N_CHIPS - 1, scatter=N_CHIPS - 1, swap=1)


def _split_copies(kind, srcs, lands, send, recv):
    x, y, c, peers = _place()
    me = 2 * x + y
    per = _COPIES[kind]
    out = []
    for a in range(len(lands)):
        for j in range(per):
            if kind == "swap":
                src, there, here, dev = srcs[a], lands[a], lands[a], (x, y, 1 - c)
            else:
                px, py = peers[j]
                dev = (px, py, c)
                if kind == "gather":
                    src, there, here = lands[a].at[me], lands[a].at[me], lands[a].at[2 * px + py]
                else:
                    src, there, here = srcs[a].at[2 * px + py], lands[a].at[j], lands[a].at[j]
            mk = functools.partial(
                pltpu.make_async_remote_copy, src_ref=src, send_sem=send.at[per * a + j],
                recv_sem=recv.at[per * a + j], device_id=dev, device_id_type=_MESH)
            out.append((functools.partial(mk, dst_ref=there), functools.partial(mk, dst_ref=here)))
    return out


def _exchange_start(kind, srcs, lands, name, after=None):
    arrays = list(srcs) + list(lands)
    n_s, n, n_all = len(srcs), len(lands), len(srcs) + len(lands)
    n_sem = _COPIES[kind] * n
    order = _after_operand(after)
    n_x = len(order)

    def body(*refs):
        send, recv = refs[n_all + n_x], refs[n_all + n_x + 1]
        token = refs[-1]
        for started, _ in _split_copies(kind, refs[:n_s], refs[n_s:n_all], send, recv):
            started().start()
        token[...] = jnp.zeros(token.shape, F32)

    res = pl.pallas_call(
        body, name=name,
        out_shape=(pltpu.SemaphoreType.DMA((n_sem,)), pltpu.SemaphoreType.DMA((n_sem,)),
                   *[pltpu.HBM(a.shape, a.dtype) for a in arrays], jax.ShapeDtypeStruct((SUBLANES, LANES), F32)),
        in_specs=[_HBM] * n_all + [_ANY] * n_x,
        out_specs=(_SEM, _SEM, *[_HBM] * n_all, pl.BlockSpec(memory_space=pltpu.VMEM)),
        input_output_aliases={i: 2 + i for i in range(n_all)},
        compiler_params=pltpu.CompilerParams(has_side_effects=_EFFECT),
    )(*[pltpu.with_memory_space_constraint(a, pltpu.HBM) for a in arrays], *order)
    return (res[0], res[1], res[2:2 + n_s], res[2 + n_s:2 + n_all]), res[-1]


def _exchange_wait(kind, handle, after, name):
    send, recv, srcs, lands = handle
    arrays = list(srcs) + list(lands)
    n_s, n_all = len(srcs), len(arrays)

    def body(*refs):
        for started, landing in _split_copies(kind, refs[:n_s], refs[n_s:n_all], refs[n_all], refs[n_all + 1]):
            started().wait_send()
            landing().wait_recv()

    res = pl.pallas_call(
        body, name=name, out_shape=[pltpu.HBM(a.shape, a.dtype) for a in arrays],
        in_specs=[_HBM] * n_all + [_SEM, _SEM, _ANY], out_specs=[_HBM] * n_all,
        input_output_aliases={i: i for i in range(n_all)},
        compiler_params=pltpu.CompilerParams(has_side_effects=_EFFECT),
    )(*arrays, send, recv, after)
    return res[:n_s], res[n_s:]


def _index_operand(i):
    return jnp.reshape(i, (1,)).astype(jnp.int32)


def _cast_into_slot(src, row0, rows, me, dtype, name, after=None, add=None):
    cols = src.shape[1]
    tm = min(512, rows)
    order = _after_operand(after)
    terms = [src] + ([] if add is None else [add])

    def body(me_ref, *rest):
        val = rest[0][...]
        if add is not None:
            val = val + rest[1][...]
        rest[-1][...] = val.astype(dtype)

    return pl.pallas_call(
        body, name=name,
        grid_spec=pltpu.PrefetchScalarGridSpec(
            num_scalar_prefetch=1, grid=(rows // tm,),
            in_specs=[pl.BlockSpec((tm, cols), lambda i, me_ref: (i + row0 // tm, 0))] * len(terms)
            + [_ANY] * len(order),
            out_specs=pl.BlockSpec((None, tm, cols), lambda i, me_ref: (me_ref[0], i, 0))),
        out_shape=jax.ShapeDtypeStruct((N_CHIPS, rows, cols), dtype), compiler_params=_params(("parallel",)),
    )(_index_operand(me), *terms, *order)


def _sum_slots(mine, r, me, name):
    _, rows, cols = r.shape
    tm = min(512, rows)

    def body(me_ref, own_ref, r_ref, o_ref):
        o_ref[...] = ((own_ref[...].astype(F32) + r_ref[0].astype(F32)) + r_ref[1].astype(F32)) + r_ref[2].astype(F32)

    return pl.pallas_call(
        body, name=name,
        grid_spec=pltpu.PrefetchScalarGridSpec(
            num_scalar_prefetch=1, grid=(rows // tm,),
            in_specs=[pl.BlockSpec((None, tm, cols), lambda i, me_ref: (me_ref[0], i, 0)),
                      pl.BlockSpec((N_CHIPS - 1, tm, cols), lambda i, me_ref: (0, i, 0))],
            out_specs=pl.BlockSpec((tm, cols), lambda i, me_ref: (i, 0))),
        out_shape=jax.ShapeDtypeStruct((rows, cols), F32), compiler_params=_params(("parallel",)),
    )(_index_operand(me), mine, r)


def _adamw(w, m, v, ps, qs, name):
    rows, cols = w.shape
    seg_rows = ps[0].shape[0]
    tm = min(256, seg_rows)
    while seg_rows % tm:
        tm -= SUBLANES
    per, n_seg = seg_rows // tm, len(ps)
    parts = list(ps) + ([] if qs is None else list(qs))

    def body(w_ref, m_ref, v_ref, *rest):
        g_refs, outs = rest[:len(parts)], rest[len(parts):]
        grad = lambda s: g_refs[s][...] if qs is None else g_refs[s][...] + g_refs[n_seg + s][...]
        g = grad(0)
        for s in range(1, n_seg):
            g = jnp.where(pl.program_id(0) >= s * per, grad(s), g)
        m1 = ADAM_B1 * m_ref[...] + (1.0 - ADAM_B1) * g
        v1 = ADAM_B2 * v_ref[...] + (1.0 - ADAM_B2) * (g * g)
        m_hat = m1 / (1.0 - ADAM_B1 ** ADAM_STEP)
        v_hat = v1 / (1.0 - ADAM_B2 ** ADAM_STEP)
        outs[0][...] = g
        outs[1][...] = (-ADAM_LR) * (m_hat / (jnp.sqrt(v_hat) + ADAM_EPS) + ADAM_WD * w_ref[...])
        outs[2][...] = m1
        outs[3][...] = v1

    row_spec = pl.BlockSpec((tm, cols), lambda i: (i, 0))
    seg_spec = lambda s: pl.BlockSpec((tm, cols), lambda i: (jnp.clip(i - s * per, 0, per - 1), 0))
    return pl.pallas_call(
        body, name=name, grid=(rows // tm,),
        in_specs=[row_spec] * 3 + [seg_spec(s) for s in range(n_seg)] * (1 if qs is None else 2),
        out_specs=[row_spec] * 4, out_shape=[jax.ShapeDtypeStruct((rows, cols), F32)] * 4,
        compiler_params=_params(("arbitrary",)),
    )(w, m, v, *parts)


def _put_cols(shard, me):
    full = jnp.zeros((shard.shape[0], D_MODEL), F32)
    return lax.dynamic_update_slice(full, shard, (0, me * (D_MODEL // N_CHIPS)))


def _gate_vec_slot(b_a, b_x, lam):
    return _rows_at(b_a, _ROW_BA) + _rows_at(b_x, _ROW_BX) + _rows_at(lam, _ROW_LAM)


def _pack_vec(p, me):
    return jnp.concatenate([
        _rows_at(p["norm_mix_g"], 0) + _rows_at(p["norm_mlp_g"], 2) + _rows_at(p["final_g"][None], 4),
        _rows_at(_put_cols(p["rg_conv_w"][0, :, 0, :], me), 0) + _rows_at(p["rg_conv_b"], 4),
        _gate_vec_slot(_put_cols(p["rg_b_a"][0], me), _put_cols(p["rg_b_x"][0], me), _put_cols(p["rg_lam"][0], me)),
        _qk_slot(p["at_q_g"], p["at_k_g"]),
    ], axis=0)


def _unpack_vec(r, me):
    def cols(rows):
        return lax.dynamic_slice(rows, (0, me * (D_MODEL // N_CHIPS)), (rows.shape[0], D_MODEL // N_CHIPS))

    gate = r[16:24]
    return dict(
        norm_mix_g=r[0:2], norm_mlp_g=r[2:4], final_g=r[4], rg_conv_w=cols(r[8:12])[None, :, None, :],
        rg_conv_b=r[12:13], rg_b_a=cols(gate[_ROW_BA:_ROW_BA + 2])[None], rg_b_x=cols(gate[_ROW_BX:_ROW_BX + 2])[None],
        rg_lam=cols(gate[_ROW_LAM:_ROW_LAM + 2])[None], at_q_g=r[24:25, 0:HEAD_DIM],
        at_k_g=r[24:25, HEAD_DIM:2 * HEAD_DIM])


_WEIGHTS = ['norm_mix_g', 'norm_mlp_g', 'rg_w_in', 'rg_conv_w', 'rg_conv_b', 'rg_w_a', 'rg_b_a', 'rg_w_x', 'rg_b_x',
            'rg_lam', 'rg_w_out', 'at_w_qkv', 'at_q_g', 'at_k_g', 'at_w_o', 'mlp_w_up', 'mlp_w_down', 'final_g']
_BIG = dict(rg_w_in=["rg_w_in"], rg_w_out=["rg_w_out"], at_w_qkv=["at_w_qkv"], at_w_o=["at_w_o"],
            mlp_w_up=["up0", "up1"], mlp_w_down=["down0", "down1"])


def kernel(x, *args):
    n_w = len(_WEIGHTS)
    w = dict(zip(_WEIGHTS, args[:n_w]))
    target = args[n_w]
    m = dict(zip(_WEIGHTS, args[n_w + 1:2 * n_w + 1]))
    v = dict(zip(_WEIGHTS, args[2 * n_w + 1:3 * n_w + 1]))
    B, L, _ = x.shape
    T = B * L
    me = 2 * lax.axis_index("x") + lax.axis_index("y")

    vec = jnp.concatenate([_gate_vec_slot(w["rg_b_a"][0], w["rg_b_x"][0], w["rg_lam"][0]),
                           _rows_at(w["rg_conv_w"][0, :, 0, :], 0)], axis=0)
    flat = lambda a: a.reshape(-1, a.shape[-1])
    rows_of = lambda k: w[k].shape[-2]
    groups = [("rg", [("rg_w_in", 0, BF16), ("rg_w_out", 0, BF16), (vec, 0, F32)]),
              ("mlp0", [("mlp_w_up", 0, BF16), ("mlp_w_down", 0, BF16)]),
              ("att", [("at_w_qkv", 0, BF16), ("at_w_o", 0, BF16)]),
              ("mlp1", [("mlp_w_up", 1, BF16), ("mlp_w_down", 1, BF16)])]
    gathers, tok = {}, None
    for group, members in groups:
        lands = []
        for n, (k, layer, dtype) in enumerate(members):
            src, rows = (flat(w[k]), rows_of(k)) if isinstance(k, str) else (k, k.shape[0])
            lands.append(_cast_into_slot(src, layer * rows, rows, me, dtype, f"place_{group}{n}", after=tok))
        gathers[group], tok = _exchange_start("gather", [], lands, f"gather_{group}_start", after=tok)
    wcat = _make_wcat(w["rg_w_a"], w["rg_w_x"]).astype(BF16)

    def fetch(group, after):
        _, full = _exchange_wait("gather", gathers[group], after, f"gather_{group}_wait")
        if group == "rg":
            vec_full = jnp.transpose(full[2], (1, 0, 2)).reshape(2 * SUBLANES, D_MODEL)
            conv_wb = vec_full[SUBLANES:] + _rows_at(w["rg_conv_b"], 4)
            return full[0], full[1].reshape(D_MODEL, D_MODEL), conv_wb, wcat, vec_full[:SUBLANES]
        if group == "att":
            return full[0], full[1].reshape(D_MODEL, D_MODEL)
        return full[0], full[1].reshape(4 * D_MODEL, D_MODEL)

    names = dict(mlp1=["up1", "down1"], att=["at_w_qkv", "at_w_o"], mlp0=["up0", "down0"], rg_out=["rg_w_out"],
                 rg_in=["rg_w_in"], gates=["rg_w_a", "rg_w_x"])
    scatters, swaps, P, Q, res = {}, [], {}, {}, {}

    def start_scatter(group, grads):
        srcs = [g.reshape(N_CHIPS, -1, g.shape[-1]) for g in grads]
        lands = [lax.empty((N_CHIPS - 1,) + s.shape[1:], s.dtype) for s in srcs]
        scatters[group], token = _exchange_start("scatter", srcs, lands, f"scatter_{group}_start")
        return token

    def settle(groups, after):
        keys, parts = [], []
        for group in groups:
            srcs, lands = _exchange_wait("scatter", scatters[group], after, f"scatter_{group}_wait")
            for k, s, r in zip(names[group], srcs, lands):
                keys.append(k)
                parts.append(_sum_slots(s, r, me, f"sum_{k}"))
        handle, token = _exchange_start("swap", parts, [lax.empty(p.shape, F32) for p in parts],
                                        f"swap_{groups[0]}_start")
        swaps.append((keys, handle, f"swap_{groups[0]}_wait"))
        return token

    def finish(after):
        for keys, handle, name in swaps:
            mine, theirs = _exchange_wait("swap", handle, after, name)
            P.update(zip(keys, mine))
            Q.update(zip(keys, theirs))
        swaps.clear()
        last = after
        for k, parts in _BIG.items():
            if k in res or any(p not in P for p in parts):
                continue
            shape = w[k].shape
            two_d = lambda a: a.reshape(-1, shape[-1])
            outs = _adamw(two_d(w[k]), two_d(m[k]), two_d(v[k]), [P[p] for p in parts], [Q[p] for p in parts],
                          f"adamw_{k}")
            res[k] = [o.reshape(shape) for o in outs]
            last = outs[0]
        if "rg_w_a" in P and "gates" not in gathers:
            lands = [_cast_into_slot(P[k], 0, P[k].shape[0], me, F32, f"place_{k}", after=last, add=Q[k])
                     for k in names["gates"]]
            gathers["gates"], last = _exchange_start("gather", [], lands, "gather_gates_start", after=last)
        return last

    def emit(event, arrays):
        if event == "point_attn_done":
            return settle(["mlp1"], arrays[0])
        if event == "point_conv_done":
            return settle(["att", "mlp0", "rg_out", "gates"], arrays[0])
        token = start_scatter(event, arrays)
        return finish(token) if event == "rg_in" else token

    P_vec = dict(norm_mix_g=w["norm_mix_g"], norm_mlp_g=w["norm_mlp_g"], final_g=w["final_g"][None],
                 q_g=w["at_q_g"], k_g=w["at_k_g"])
    loss_part, grad_x, vec_part = _local_step(x.reshape(T, D_MODEL), target.reshape(T, D_MODEL), P_vec, fetch, emit,
                                              B, L, after=tok)
    loss = lax.psum(loss_part, ("x", "y", "c"))

    finish(settle(["rg_in"], grad_x))
    _, gate_grads = _exchange_wait("gather", gathers["gates"], grad_x, "gather_gates_wait")
    for k, g in zip(names["gates"], gate_grads):
        two_d = lambda a: a.reshape(g.shape[0] * g.shape[1], g.shape[2])
        outs = _adamw(two_d(w[k]), two_d(m[k]), two_d(v[k]), [two_d(g)], None, f"adamw_{k}")
        res[k] = [o.reshape(w[k].shape) for o in outs]
    vec_grad = _sum_leading(_all_devices_slots(vec_part, "allreduce_vec"), "sum_vec")
    outs = _adamw(_pack_vec(w, me), _pack_vec(m, me), _pack_vec(v, me), [vec_grad], None, "adamw_vec")
    unpacked = [_unpack_vec(o, me) for o in outs]
    for k in _WEIGHTS:
        if k not in res:
            res[k] = [u[k] for u in unpacked]

    result = [loss, grad_x.reshape(B, L, D_MODEL)]
    for slot in range(4):
        result += [res[k][slot] for k in _WEIGHTS]
    return tuple(result)
```

```python
import functools
import math

import jax
import jax.numpy as jnp
import numpy as np
from jax import lax
from jax.experimental import pallas as pl
from jax.experimental.pallas import tpu as pltpu

F32 = jnp.float32
BF16 = jnp.bfloat16

D_MODEL = 1024
HEAD_DIM = 128
N_HEADS = 8
N_KV = 2
GROUP = N_HEADS // N_KV
LRU_BLOCKS = 8
LRU_BW = 128
GRID_W = 64
ROPE_THETA = 10000.0
EPS = 1e-6
RG_C = 8.0
SCALE = 1.0 / math.sqrt(HEAD_DIM)
N_CHIPS = 4

ADAM_LR = 0.001
ADAM_B1 = 0.9
ADAM_B2 = 0.999
ADAM_EPS = 1e-08
ADAM_WD = 0.01
ADAM_STEP = 10

V7X_VMEM_BYTES = 64 * 1024 * 1024
VMEM_LIMIT = V7X_VMEM_BYTES * 3 // 4
LANES = 128
SUBLANES = 8

N_DEVICES = 8
VEC_ROWS = 32


def _params(sem):
    return pltpu.CompilerParams(dimension_semantics=sem, vmem_limit_bytes=VMEM_LIMIT)


_ANY = pl.BlockSpec(memory_space=pl.ANY)
_NN = (((1,), (0,)), ((), ()))
_NT = (((1,), (1,)), ((), ()))
_TN = (((0,), (0,)), ((), ()))


def _after_operand(after):
    return [] if after is None else [after]


def _fit(t, n):
    if n <= t:
        return n
    c = (t // LANES) * LANES
    while n % c:
        c -= LANES
    return c


MM_VMEM_BUDGET = VMEM_LIMIT * 3 // 4
def _mm_tiles(M, K, ns, out_dtypes, extras, whole_rows):
    for tm in (2048, 1024, 512, 256, 128):
        for tn in ((ns,) if whole_rows else (1024, 512, 256)):
            tn = _fit(tn, ns)
            per_row = 2 * (2 * K) + 4 * tn + sum(2 * tn * jnp.dtype(d).itemsize for d in out_dtypes)
            per_row += sum(2 * tn * e.dtype.itemsize for e in extras)
            if M % tm == 0 and 2 * (2 * K * tn) + tm * per_row <= MM_VMEM_BUDGET:
                return tm, tn
    raise ValueError(f"no tile fits VMEM for M={M} K={K} N={ns}")


def _mm(a, b, *, mode, name, out_dtypes=(F32,), b_shard=False, o_shard=False, extras=(), epi=None, after=None,
        bcast=(), accs=(), ref_epi=None):
    if mode == "tn":
        K, M = a.shape
        N = b.shape[1]
    else:
        M, K = a.shape
        if mode == "nn":
            N = b.shape[0] * b.shape[2] if b_shard else b.shape[1]
        else:
            N = b.shape[1] if b_shard else b.shape[0]
    ns = N
    if b_shard and mode == "nn":
        ns = b.shape[2]
    elif o_shard:
        ns = N // N_CHIPS
    tm, tn = _mm_tiles(M, K, ns, out_dtypes, extras, whole_rows=ref_epi is not None)
    if ref_epi is not None:
        tm = min(tm, 512)
    grid = (M // tm, N // tn)
    q = ns // tn

    if mode == "tn":
        a_spec = pl.BlockSpec((K, tm), lambda i, j: (0, i))
        b_spec = pl.BlockSpec((K, tn), lambda i, j: (0, j))
        dims = _TN
    elif mode == "nn":
        a_spec = pl.BlockSpec((tm, K), lambda i, j: (i, 0))
        if b_shard:
            b_spec = pl.BlockSpec((None, K, tn), lambda i, j: (j // q, 0, j % q))
        else:
            b_spec = pl.BlockSpec((K, tn), lambda i, j: (0, j))
        dims = _NN
    else:
        a_spec = pl.BlockSpec((tm, K), lambda i, j: (i, 0))
        if b_shard:
            ks = b.shape[2]
            b_spec = pl.BlockSpec((N_CHIPS, tn, ks), lambda i, j: (0, j, 0))
        else:
            b_spec = pl.BlockSpec((tn, K), lambda i, j: (j, 0))
        dims = _NT

    if o_shard:
        o_specs = [pl.BlockSpec((None, tm, tn), lambda i, j: (j // q, i, j % q))]
        o_shapes = [jax.ShapeDtypeStruct((N_CHIPS, M, ns), out_dtypes[0])]
    else:
        o_specs = [pl.BlockSpec((tm, tn), lambda i, j: (i, j)) for _ in out_dtypes]
        o_shapes = [jax.ShapeDtypeStruct((M, N), dt) for dt in out_dtypes]
    e_specs = [pl.BlockSpec((tm, tn), lambda i, j: (i, j)) for _ in extras]
    e_specs += [pl.BlockSpec(v.shape, lambda i, j: (0, 0)) for v in bcast]
    o_specs += [pl.BlockSpec(s, lambda i, j: (0, 0)) for s in accs]
    o_shapes += [jax.ShapeDtypeStruct(s, F32) for s in accs]
    n_e, n_b, n_o, n_a = len(extras), len(bcast), len(out_dtypes), len(accs)
    order = _after_operand(after)
    n_x = len(order)
    if epi is None:
        epi = lambda acc: (acc,)

    def body(a_ref, b_ref, *rest):
        e_refs, b_refs = rest[:n_e], rest[n_e:n_e + n_b]
        o_refs = rest[n_e + n_b + n_x:n_e + n_b + n_x + n_o]
        a_refs = rest[n_e + n_b + n_x + n_o:]
        if n_a:
            @pl.when((pl.program_id(0) == 0) & (pl.program_id(1) == 0))
            def _():
                for r in a_refs:
                    r[...] = jnp.zeros(r.shape, F32)
        if mode == "nt" and b_shard:
            acc = None
            for s in range(N_CHIPS):
                part = lax.dot_general(a_ref[:, s * ks:(s + 1) * ks], b_ref[s], dims, preferred_element_type=F32)
                acc = part if acc is None else acc + part
        else:
            acc = lax.dot_general(a_ref[...], b_ref[...], dims, preferred_element_type=F32)
        if ref_epi is not None:
            ref_epi(acc, e_refs, b_refs, o_refs, a_refs)
            return
        outs = epi(acc, *[r[...] for r in e_refs])
        for r, o in zip(o_refs, outs):
            r[...] = o.astype(r.dtype)

    outs = pl.pallas_call(
        body, name=name, grid=grid, in_specs=[a_spec, b_spec] + e_specs + [_ANY] * n_x, out_specs=o_specs,
        out_shape=o_shapes, compiler_params=_params(("arbitrary", "arbitrary") if n_a else ("parallel", "parallel")),
    )(a, b, *extras, *bcast, *order)
    return outs[0] if n_o + n_a == 1 else outs


def _rowwise(fn, rows, bcast, outs, accs=(), *, tm, name, after=None):
    def norm(r):
        return r if isinstance(r, tuple) else (r, r.shape[1], 0)

    rows = [norm(r) for r in rows]
    T = rows[0][0].shape[0]
    tm = min(tm, T)
    while T % tm:
        tm -= SUBLANES
    n_r, n_b, n_o, n_a = len(rows), len(bcast), len(outs), len(accs)
    order = _after_operand(after)
    n_x = len(order)
    in_specs = [pl.BlockSpec((tm, c), functools.partial(lambda i, cb: (i, cb), cb=cb)) for _, c, cb in rows]
    in_specs += [pl.BlockSpec(b.shape, lambda i: (0, 0)) for b in bcast] + [_ANY] * n_x
    out_specs = [pl.BlockSpec((tm, o[0]), lambda i: (i, 0)) for o in outs]
    out_specs += [pl.BlockSpec(s, lambda i: (0, 0)) for s in accs]
    out_shape = [jax.ShapeDtypeStruct((T, o[2] if len(o) > 2 else o[0]), o[1]) for o in outs]
    out_shape += [jax.ShapeDtypeStruct(s, F32) for s in accs]

    def body(*refs):
        in_refs = refs[:n_r]
        b_refs = refs[n_r:n_r + n_b]
        o_refs = refs[n_r + n_b + n_x:n_r + n_b + n_x + n_o]
        a_refs = refs[n_r + n_b + n_x + n_o:]
        if n_a:
            @pl.when(pl.program_id(0) == 0)
            def _():
                for r in a_refs:
                    r[...] = jnp.zeros(r.shape, F32)
        fn(in_refs, b_refs, o_refs, a_refs)

    res = pl.pallas_call(
        body, name=name, grid=(T // tm,), in_specs=in_specs, out_specs=out_specs, out_shape=out_shape,
        compiler_params=_params(("arbitrary",) if n_a else ("parallel",)),
    )(*[r[0] for r in rows], *bcast, *order)
    return res


def _rsum(x):
    return jnp.sum(x, axis=0, keepdims=True)


def _rms_fwd(x, g, name, after=None):
    def fn(ins, bs, outs, accs):
        xv = ins[0][...]
        r = lax.rsqrt(jnp.mean(xv * xv, axis=-1, keepdims=True) + EPS)
        outs[0][...] = (xv * r * bs[0][...]).astype(BF16)

    return _rowwise(fn, [x], [g], [(D_MODEL, BF16)], tm=512, name=name, after=after)[0]


def _rms_bwd_math(xv, dh, g):
    r = lax.rsqrt(jnp.mean(xv * xv, axis=-1, keepdims=True) + EPS)
    hn = xv * r
    dgh = dh * g
    dx = r * (dgh - hn * jnp.mean(dgh * hn, axis=-1, keepdims=True))
    return dx, _rsum(dh * hn)


def _mm_norm_bwd(dy, w, x, dres, g, name, after=None):
    def epilogue(acc, e_refs, b_refs, o_refs, a_refs):
        dx, dg = _rms_bwd_math(e_refs[0][...], acc, b_refs[0][...])
        dx = dx + e_refs[1][...]
        o_refs[0][...] = dx
        o_refs[1][...] = dx.astype(BF16)
        a_refs[0][...] += dg

    return _mm(dy, w, mode="nt", b_shard=True, out_dtypes=(F32, BF16), extras=(x, dres), bcast=(g,),
               accs=((1, D_MODEL),), ref_epi=epilogue, name=name, after=after)


def _mm_res_norm(a, w, res, g, name):
    def epilogue(acc, e_refs, b_refs, o_refs, a_refs):
        xv = acc + e_refs[0][...]
        o_refs[0][...] = xv
        r = lax.rsqrt(jnp.mean(xv * xv, axis=-1, keepdims=True) + EPS)
        o_refs[1][...] = (xv * r * b_refs[0][...]).astype(BF16)

    return _mm(a, w, mode="nn", out_dtypes=(F32, BF16), extras=(res,), bcast=(g,), ref_epi=epilogue, name=name)


def _mm_final_loss(a, w, res, target, g, name):
    def epilogue(acc, e_refs, b_refs, o_refs, a_refs):
        xv = acc + e_refs[0][...]
        gv = b_refs[0][...]
        r = lax.rsqrt(jnp.mean(xv * xv, axis=-1, keepdims=True) + EPS)
        e = xv * r * gv - e_refs[1][...]
        tok = jnp.mean(e * e, axis=-1, keepdims=True)
        a_refs[0][...] += 0.5 * jnp.sum(tok, axis=0, keepdims=True) * jnp.ones((1, LANES), F32)
        dx, dg = _rms_bwd_math(xv, e * (1.0 / D_MODEL), gv)
        o_refs[0][...] = dx
        o_refs[1][...] = dx.astype(BF16)
        a_refs[1][...] += dg

    return _mm(a, w, mode="nn", out_dtypes=(F32, BF16), extras=(res, target), bcast=(g,),
               accs=((1, LANES), (1, D_MODEL)), ref_epi=epilogue, name=name)


def _relu2(acc):
    r = jnp.maximum(acc, 0.0)
    return r * r, r


def _mlp_fwd(x, h, fetch, tag, finish):
    w_up, w_down = fetch(f"mlp{tag}", h)
    a, r = _mm(h, w_up, mode="nn", b_shard=True, out_dtypes=(BF16, BF16), epi=_relu2, name=f"mlp{tag}_up")
    return finish(a, w_down, x, f"mlp{tag}_down"), (h, a, r, w_up, w_down)


def _mlp_bwd(x, g, saved, dx, dx_bf, tag, after):
    h, a, r, w_up, w_down = saved
    d_down = _mm(a, dx_bf, mode="tn", out_dtypes=(BF16,), name=f"mlp{tag}_dwdown", after=after)
    dup = _mm(dx_bf, w_down, mode="nt", extras=(r,), out_dtypes=(BF16,),
              epi=lambda acc, rv: (acc * (2.0 * rv.astype(F32)),), name=f"mlp{tag}_dup")
    d_up = _mm(h, dup, mode="tn", o_shard=True, out_dtypes=(BF16,), name=f"mlp{tag}_dwup")
    dx_new, dx_new_bf, dg = _mm_norm_bwd(dup, w_up, x, dx, g, f"mlp{tag}_dh")
    return dx_new, dx_new_bf, dg, d_up, d_down


def _rope_tables(L, B):
    rows = L // GRID_W
    row = np.repeat(np.arange(rows, dtype=np.float32), GRID_W)
    col = np.tile(np.arange(GRID_W, dtype=np.float32), rows)
    inv = (ROPE_THETA ** (-np.arange(HEAD_DIM // 4, dtype=np.float32) / (HEAD_DIM // 4))).astype(np.float32)
    ar, ac = row[:, None] * inv, col[:, None] * inv
    cos = np.concatenate([np.cos(ar), np.cos(ar), np.cos(ac), np.cos(ac)], axis=-1)
    sin = np.concatenate([-np.sin(ar), np.sin(ar), -np.sin(ac), np.sin(ac)], axis=-1)
    return jnp.asarray(np.tile(cos, (B, 1)), F32), jnp.asarray(np.tile(sin, (B, 1)), F32)


def _swap_halves(x):
    lane = lax.broadcasted_iota(jnp.int32, x.shape, 1)
    return jnp.where((lane % 64) < 32, pltpu.roll(x, HEAD_DIM - 32, 1), pltpu.roll(x, 32, 1))


def _qk_prep(qkv, cos, sin, q_g, k_g):
    def fn(ins, bs, outs, accs):
        c, s = ins[1][...], ins[2][...]
        for h in range(N_HEADS + N_KV):
            xv = ins[0][:, h * HEAD_DIM:(h + 1) * HEAD_DIM]
            g = bs[0][...] if h < N_HEADS else bs[1][...]
            r = lax.rsqrt(jnp.mean(xv * xv, axis=-1, keepdims=True) + EPS)
            z = xv * r * g
            y = (z * c + _swap_halves(z) * s).astype(BF16)
            if h < N_HEADS:
                outs[0][:, h * HEAD_DIM:(h + 1) * HEAD_DIM] = y
            else:
                outs[1][:, (h - N_HEADS) * HEAD_DIM:(h - N_HEADS + 1) * HEAD_DIM] = y
        outs[2][...] = ins[0][:, (N_HEADS + N_KV) * HEAD_DIM:].astype(BF16)

    kvw = N_KV * HEAD_DIM
    return _rowwise(fn, [qkv, cos, sin], [q_g, k_g], [(D_MODEL, BF16), (kvw, BF16), (kvw, BF16)], tm=512,
                    name="attn_qk_prep")


def _qk_prep_bwd(qkv, dq, dk, dv, cos, sin, q_g, k_g):
    def fn(ins, bs, outs, accs):
        c, s = ins[4][...], ins[5][...]
        for h in range(N_HEADS + N_KV):
            sl = slice(h * HEAD_DIM, (h + 1) * HEAD_DIM)
            xv = ins[0][:, sl]
            if h < N_HEADS:
                g, dy, acc = bs[0][...], ins[1][:, sl], accs[0]
            else:
                ks = slice((h - N_HEADS) * HEAD_DIM, (h - N_HEADS + 1) * HEAD_DIM)
                g, dy, acc = bs[1][...], ins[2][:, ks], accs[1]
            r = lax.rsqrt(jnp.mean(xv * xv, axis=-1, keepdims=True) + EPS)
            xn = xv * r
            dz = dy * c - _swap_halves(dy) * s
            acc[...] += _rsum(dz * xn)
            dxn = dz * g
            outs[0][:, sl] = (r * (dxn - xn * jnp.mean(dxn * xn, axis=-1, keepdims=True))).astype(BF16)
        outs[0][:, (N_HEADS + N_KV) * HEAD_DIM:] = ins[3][...].astype(BF16)

    return _rowwise(fn, [qkv, dq, dk, dv, cos, sin], [q_g, k_g], [(qkv.shape[1], BF16)],
                    [(1, HEAD_DIM), (1, HEAD_DIM)], tm=256, name="attn_qk_prep_bwd")


_EXP2_SCALE = SCALE * math.log2(math.e)


def _exp_rows(q, k):
    s = lax.dot_general(q, k, _NT, preferred_element_type=F32)
    p = jnp.exp2((s - jnp.max(s, axis=-1, keepdims=True)) * _EXP2_SCALE)
    return p, jnp.sum(p, axis=-1, keepdims=True)


def _attn_fwd(q, k, v, B, L, tq=1024, sub=256):
    tq = min(tq, L)
    sub = min(sub, tq)
    nq = L // tq

    def body(q_ref, k_ref, v_ref, o_ref):
        kv, vv = k_ref[...], v_ref[...]
        for c in range(tq // sub):
            rows = slice(c * sub, (c + 1) * sub)
            p, l = _exp_rows(q_ref[rows, :], kv)
            o = jnp.dot(p.astype(BF16), vv, preferred_element_type=F32)
            o_ref[rows, :] = (o * (1.0 / l)).astype(o_ref.dtype)

    return pl.pallas_call(
        body, name="attn_fwd", grid=(B, N_HEADS, nq),
        in_specs=[pl.BlockSpec((tq, HEAD_DIM), lambda b, h, i: (b * nq + i, h)),
                  pl.BlockSpec((L, HEAD_DIM), lambda b, h, i: (b, h // GROUP)),
                  pl.BlockSpec((L, HEAD_DIM), lambda b, h, i: (b, h // GROUP))],
        out_specs=pl.BlockSpec((tq, HEAD_DIM), lambda b, h, i: (b * nq + i, h)),
        out_shape=jax.ShapeDtypeStruct((B * L, D_MODEL), BF16),
        compiler_params=_params(("parallel", "parallel", "parallel")),
    )(q, k, v)


def _attn_bwd(q, k, v, do, B, L, tq=512, sub=256):
    tq = min(tq, L)
    sub = min(sub, tq)
    nq = L // tq

    def body(q_ref, k_ref, v_ref, do_ref, dq_ref, dk_ref, dv_ref):
        @pl.when((pl.program_id(2) == 0) & (pl.program_id(3) == 0))
        def _():
            dk_ref[...] = jnp.zeros(dk_ref.shape, F32)
            dv_ref[...] = jnp.zeros(dv_ref.shape, F32)

        kv, vv = k_ref[...], v_ref[...]
        ps, es, dos, qs = [], [], [], []
        for c in range(tq // sub):
            rows = slice(c * sub, (c + 1) * sub)
            qc, doc = q_ref[rows, :], do_ref[rows, :]
            p, l = _exp_rows(qc, kv)
            inv = 1.0 / l
            dp = lax.dot_general(doc, vv, _NT, preferred_element_type=F32)
            delta = jnp.sum(p * dp, axis=-1, keepdims=True) * inv
            e = (p * (dp - delta)).astype(BF16)
            dq_ref[rows, :] = jnp.dot(e, kv, preferred_element_type=F32) * (inv * SCALE)
            ps.append(p.astype(BF16))
            es.append(e)
            dos.append((doc.astype(F32) * inv).astype(BF16))
            qs.append((qc.astype(F32) * (inv * SCALE)).astype(BF16))
        cat = lambda xs: xs[0] if len(xs) == 1 else jnp.concatenate(xs, axis=0)
        dv_ref[...] += lax.dot_general(cat(ps), cat(dos), _TN, preferred_element_type=F32)
        dk_ref[...] += lax.dot_general(cat(es), cat(qs), _TN, preferred_element_type=F32)

    qmap = lambda b, kh, g, i: (b * nq + i, kh * GROUP + g)
    kmap = lambda b, kh, g, i: (b, kh)
    kvw = N_KV * HEAD_DIM
    return pl.pallas_call(
        body, name="attn_bwd", grid=(B, N_KV, GROUP, nq),
        in_specs=[pl.BlockSpec((tq, HEAD_DIM), qmap), pl.BlockSpec((L, HEAD_DIM), kmap),
                  pl.BlockSpec((L, HEAD_DIM), kmap), pl.BlockSpec((tq, HEAD_DIM), qmap)],
        out_specs=[pl.BlockSpec((tq, HEAD_DIM), qmap), pl.BlockSpec((L, HEAD_DIM), kmap),
                   pl.BlockSpec((L, HEAD_DIM), kmap)],
        out_shape=[jax.ShapeDtypeStruct((B * L, D_MODEL), F32), jax.ShapeDtypeStruct((B * L, kvw), F32),
                   jax.ShapeDtypeStruct((B * L, kvw), F32)],
        compiler_params=_params(("parallel", "parallel", "arbitrary", "arbitrary")),
    )(q, k, v, do)


def _conv_shift(x, t, L, k):
    if k == 2:
        return x
    if k < 2:
        return jnp.where(t >= 2 - k, pltpu.roll(x, 2 - k, 0), 0.0)
    return jnp.where(t < L - (k - 2), pltpu.roll(x, L - (k - 2), 0), 0.0)


def _conv_fwd(z, wb, B, L, tc=256):
    noff = D_MODEL // tc

    def body(z_ref, w_ref, o_ref):
        x = z_ref[...]
        t = lax.broadcasted_iota(jnp.int32, x.shape, 0)
        acc = w_ref[4:5, :] + w_ref[2:3, :] * x
        for k in (0, 1, 3):
            acc = acc + w_ref[k:k + 1, :] * _conv_shift(x, t, L, k)
        o_ref[...] = acc

    return pl.pallas_call(
        body, name="rg_conv", grid=(B, noff),
        in_specs=[pl.BlockSpec((L, tc), lambda b, j: (b, noff + j)), pl.BlockSpec((SUBLANES, tc), lambda b, j: (0, j))],
        out_specs=pl.BlockSpec((L, tc), lambda b, j: (b, j)),
        out_shape=jax.ShapeDtypeStruct((B * L, D_MODEL), F32),
        compiler_params=_params(("parallel", "parallel")),
    )(z, wb)


def _conv_bwd(z, g, wb, dz, B, L, tc=256, after=None):
    noff = D_MODEL // tc
    order = _after_operand(after)

    def body(z_ref, g_ref, w_ref, dz_in, *rest):
        dx_ref, dw_ref = rest[len(order):]

        @pl.when(pl.program_id(1) == 0)
        def _():
            dw_ref[...] = jnp.zeros(dw_ref.shape, F32)

        x, gv = z_ref[...], g_ref[...]
        t = lax.broadcasted_iota(jnp.int32, x.shape, 0)
        dx = w_ref[2:3, :] * gv
        for k in (0, 1, 3):
            dx = dx + w_ref[k:k + 1, :] * _conv_shift(gv, t, L, 4 - k)
        dx_ref[...] = dx.astype(BF16)
        for k in range(4):
            dw_ref[k:k + 1, :] += _rsum(_conv_shift(x, t, L, k) * gv)
        dw_ref[4:5, :] += _rsum(gv)

    return pl.pallas_call(
        body, name="rg_conv_bwd", grid=(noff, B),
        in_specs=[pl.BlockSpec((L, tc), lambda j, b: (b, noff + j)), pl.BlockSpec((L, tc), lambda j, b: (b, j)),
                  pl.BlockSpec((SUBLANES, tc), lambda j, b: (0, j)), _ANY] + [_ANY] * len(order),
        out_specs=[pl.BlockSpec((L, tc), lambda j, b: (b, noff + j)),
                   pl.BlockSpec((SUBLANES, tc), lambda j, b: (0, j))],
        out_shape=[jax.ShapeDtypeStruct(dz.shape, dz.dtype), jax.ShapeDtypeStruct((SUBLANES, D_MODEL), F32)],
        input_output_aliases={3: 0},
        compiler_params=_params(("parallel", "arbitrary")),
    )(z, g, wb, dz, *order)


def _softplus(x):
    return jnp.maximum(x, 0.0) + jnp.log1p(jnp.exp(-jnp.abs(x)))


_ROW_BA, _ROW_BX, _ROW_LAM = 0, 2, 4


def _gate_math(xb, pre, vec_ref, d, sl):
    pa = pre[:, (2 * d) * LRU_BW:(2 * d + 1) * LRU_BW] + vec_ref[_ROW_BA + d:_ROW_BA + d + 1, sl]
    px = pre[:, (2 * d + 1) * LRU_BW:(2 * d + 2) * LRU_BW] + vec_ref[_ROW_BX + d:_ROW_BX + d + 1, sl]
    r = 0.5 * jnp.tanh(0.5 * pa) + 0.5
    i = 0.5 * jnp.tanh(0.5 * px) + 0.5
    sp = _softplus(-vec_ref[_ROW_LAM + d:_ROW_LAM + d + 1, sl])
    log_a = (-RG_C) * r * sp
    a = jnp.exp(log_a)
    th = jnp.tanh(log_a)
    om = -2.0 * th / (1.0 - th)
    mult = jnp.sqrt(om)
    return a, mult * (i * xb), (r, i, sp, om, mult)


def _gate_fwd(rec, wcat, gvec):
    def fn(ins, bs, outs, accs):
        for blk in range(LRU_BLOCKS):
            sl = slice(blk * LRU_BW, (blk + 1) * LRU_BW)
            xb = ins[0][:, sl]
            pre = jnp.dot(xb.astype(BF16), bs[0][sl, :], preferred_element_type=F32)
            for d in range(2):
                a, u, _ = _gate_math(xb, pre, bs[1], d, sl)
                outs[2 * d][:, sl] = a
                outs[2 * d + 1][:, sl] = u

    return _rowwise(fn, [rec], [wcat, gvec], [(D_MODEL, F32)] * 4, tm=256, name="rg_gate")


def _gate_bwd(rec, du_f, da_f, du_b, da_b, wcat, gvec):
    def fn(ins, bs, outs, accs):
        for blk in range(LRU_BLOCKS):
            sl = slice(blk * LRU_BW, (blk + 1) * LRU_BW)
            xb = ins[0][:, sl]
            xb16 = xb.astype(BF16)
            w = bs[0][sl, :]
            pre = jnp.dot(xb16, w, preferred_element_type=F32)
            dx = jnp.zeros_like(xb)
            dpre = []
            for d in range(2):
                a, _, (r, i, sp, om, mult) = _gate_math(xb, pre, bs[1], d, sl)
                du, da = ins[1 + 2 * d][:, sl], ins[2 + 2 * d][:, sl]
                d_i = du * mult * xb
                d_mult = du * i * xb
                dx = dx + du * mult * i
                dlog = da * a - d_mult * (1.0 - om) / mult
                d_r = dlog * ((-RG_C) * sp)
                d_sp = _rsum(dlog * ((-RG_C) * r))
                lam = bs[1][_ROW_LAM + d:_ROW_LAM + d + 1, sl]
                accs[2][_ROW_LAM + d:_ROW_LAM + d + 1, sl] += d_sp * (-jax.nn.sigmoid(-lam))
                dpa = d_r * r * (1.0 - r)
                dpx = d_i * i * (1.0 - i)
                accs[2][_ROW_BA + d:_ROW_BA + d + 1, sl] += _rsum(dpa)
                accs[2][_ROW_BX + d:_ROW_BX + d + 1, sl] += _rsum(dpx)
                dpre += [dpa, dpx]
            dpre = jnp.concatenate(dpre, axis=1).astype(BF16)
            dw = lax.dot_general(xb16, dpre, _TN, preferred_element_type=F32)
            for d in range(2):
                rows = slice(d * D_MODEL + blk * LRU_BW, d * D_MODEL + (blk + 1) * LRU_BW)
                accs[0][rows, :] += dw[:, (2 * d) * LRU_BW:(2 * d + 1) * LRU_BW]
                accs[1][rows, :] += dw[:, (2 * d + 1) * LRU_BW:(2 * d + 2) * LRU_BW]
            outs[0][:, sl] = dx + lax.dot_general(dpre, w, _NT, preferred_element_type=F32)

    gate_shape = (2 * D_MODEL, LRU_BW)
    return _rowwise(fn, [rec, du_f, da_f, du_b, da_b], [wcat, gvec], [(D_MODEL, F32)],
                    [gate_shape, gate_shape, (SUBLANES, D_MODEL)], tm=256, name="rg_gate_bwd")


def _as_time_blocks(x):
    return x.reshape(x.shape[0] // SUBLANES, SUBLANES, x.shape[1])


def _scan_call(body, ins, n_out, B, L, tc, name):
    nb = L // SUBLANES
    spec = pl.BlockSpec((nb, SUBLANES, tc), lambda b, j: (b, 0, j))
    T = ins[0].shape[0]
    outs = pl.pallas_call(
        functools.partial(body, nb), name=name, grid=(B, D_MODEL // tc),
        in_specs=[spec] * len(ins), out_specs=[spec] * n_out,
        out_shape=[jax.ShapeDtypeStruct((T // SUBLANES, SUBLANES, D_MODEL), F32)] * n_out,
        compiler_params=_params(("parallel", "parallel")),
    )(*[_as_time_blocks(x) for x in ins])
    return [o.reshape(T, D_MODEL) for o in outs]


def _block_scan(A, U, reverse):
    row = lax.broadcasted_iota(jnp.int32, A.shape, 0)
    for s in (1, 2, 4):
        shift = SUBLANES - s if reverse else s
        valid = (row < SUBLANES - s) if reverse else (row >= s)
        a_sh = jnp.where(valid, pltpu.roll(A, shift, 0), 1.0)
        u_sh = jnp.where(valid, pltpu.roll(U, shift, 0), 0.0)
        U = A * u_sh + U
        A = A * a_sh
    return A, U


_LAST = SUBLANES - 1
SCAN_UNROLL = 8


def _loop_blocks(nb, step, init):
    def group(g, carry):
        for k in range(SCAN_UNROLL):
            carry = step(g * SCAN_UNROLL + k, carry)
        return carry

    return lax.fori_loop(0, nb // SCAN_UNROLL, group, init)


def _scan_fwd(a_f, u_f, a_b, u_b, B, L, tc=256):
    def body(nb, af, uf, ab, ub, hf, hb):
        def step(i, carry):
            c1, c2 = carry
            ib = nb - 1 - i
            p, h = _block_scan(af[i], uf[i], False)
            h = h + p * c1
            hf[i] = h
            p2, h2 = _block_scan(ab[ib], ub[ib], True)
            h2 = h2 + p2 * c2
            hb[ib] = h2
            return h[_LAST:, :], h2[:1, :]

        zero = jnp.zeros((1, tc), F32)
        _loop_blocks(nb, step, (zero, zero))

    return _scan_call(body, [a_f, u_f, a_b, u_b], 2, B, L, tc, "rg_scan")


def _scan_bwd(dy, a_f, h_f, a_b, h_b, B, L, tc=256):
    def body(nb, dy_r, af, hf, ab, hb, duf, daf, dub, dab):
        def step(i, carry):
            c1, c2 = carry
            ir = nb - 1 - i
            row = lax.broadcasted_iota(jnp.int32, (SUBLANES, tc), 0)
            a_up = jnp.where(row == _LAST, af[jnp.minimum(ir + 1, nb - 1), :1, :], pltpu.roll(af[ir], _LAST, 0))
            p, lam = _block_scan(a_up, dy_r[ir], True)
            lam = lam + p * c1
            before = hf[jnp.maximum(ir - 1, 0), _LAST:, :] * (ir > 0).astype(F32)
            duf[ir] = lam
            daf[ir] = lam * jnp.where(row == 0, before, pltpu.roll(hf[ir], 1, 0))
            a_dn = jnp.where(row == 0, ab[jnp.maximum(i - 1, 0), _LAST:, :], pltpu.roll(ab[i], 1, 0))
            p2, lam2 = _block_scan(a_dn, dy_r[i], False)
            lam2 = lam2 + p2 * c2
            after = hb[jnp.minimum(i + 1, nb - 1), :1, :] * (i < nb - 1).astype(F32)
            dub[i] = lam2
            dab[i] = lam2 * jnp.where(row == _LAST, after, pltpu.roll(hb[i], _LAST, 0))
            return lam[:1, :], lam2[_LAST:, :]

        zero = jnp.zeros((1, tc), F32)
        _loop_blocks(nb, step, (zero, zero))

    return _scan_call(body, [dy, a_f, h_f, a_b, h_b], 4, B, L, tc, "rg_scan_bwd")


_GELU_C = math.sqrt(2.0 / math.pi)


def _gelu_parts(x):
    th = jnp.tanh(_GELU_C * (x + 0.044715 * x * x * x))
    return 0.5 * x * (1.0 + th), th


def _gated_out(h_f, h_b, z):
    def fn(ins, bs, outs, accs):
        gl, _ = _gelu_parts(ins[2][...])
        outs[0][...] = ((ins[0][...] + ins[1][...]) * gl).astype(BF16)

    return _rowwise(fn, [h_f, h_b, (z, D_MODEL, 0)], [], [(D_MODEL, BF16)], tm=512, name="rg_gated_out")[0]


def _gated_out_bwd(dyg, h_f, h_b, z):
    def fn(ins, bs, outs, accs):
        x = ins[3][...]
        gl, th = _gelu_parts(x)
        dgl = 0.5 * (1.0 + th) + 0.5 * x * (1.0 - th * th) * (_GELU_C * (1.0 + 3.0 * 0.044715 * x * x))
        g = ins[0][...]
        outs[0][...] = g * gl
        outs[1][...] = (g * (ins[1][...] + ins[2][...]) * dgl).astype(BF16)

    return _rowwise(fn, [dyg, h_f, h_b, (z, D_MODEL, 0)], [], [(D_MODEL, F32), (D_MODEL, BF16, 2 * D_MODEL)], tm=512,
                    name="rg_gated_out_bwd")


def _make_wcat(w_a, w_x):
    g = jnp.stack([w_a[0, 0], w_x[0, 0], w_a[0, 1], w_x[0, 1]])
    return jnp.transpose(g, (1, 2, 0, 3)).reshape(D_MODEL, 4 * LRU_BW)


def _rows_at(part, first):
    return jnp.pad(part, ((first, SUBLANES - first - part.shape[0]), (0, 0)))


def _qk_slot(q_g, k_g):
    wide = lambda v, at: jnp.pad(v, ((0, SUBLANES - 1), (at, D_MODEL - at - HEAD_DIM)))
    return wide(q_g, 0) + wide(k_g, HEAD_DIM)


def _local_step(x, target, P, fetch, emit, B, L, after=None):
    g_mix, g_mlp = P["norm_mix_g"], P["norm_mlp_g"]
    h0 = _rms_fwd(x, g_mix[0:1], "rg_norm", after=after)
    w_in, w_out, conv_wb, wcat, gvec = fetch("rg", h0)
    z = _mm(h0, w_in, mode="nn", b_shard=True, name="rg_in")
    rec = _conv_fwd(z, conv_wb, B, L)
    a_f, u_f, a_b, u_b = _gate_fwd(rec, wcat, gvec)
    h_f, h_b = _scan_fwd(a_f, u_f, a_b, u_b, B, L)
    yg = _gated_out(h_f, h_b, z)
    x1, h1 = _mm_res_norm(yg, w_out, x, g_mlp[0:1], "rg_out")
    (x2, h3), mlp0 = _mlp_fwd(x1, h1, fetch, 0, lambda a, w, res, name: _mm_res_norm(a, w, res, g_mix[1:2], name))
    w_qkv, w_o = fetch("att", h3)
    qkv = _mm(h3, w_qkv, mode="nn", b_shard=True, name="attn_qkv")
    cos, sin = _rope_tables(L, B)
    qh, kh, vh = _qk_prep(qkv, cos, sin, P["q_g"], P["k_g"])
    o = _attn_fwd(qh, kh, vh, B, L)
    x3, h4 = _mm_res_norm(o, w_o, x2, g_mlp[1:2], "attn_out")
    (dx4, dx4_bf, loss_acc, d_final_g), mlp1 = _mlp_fwd(
        x3, h4, fetch, 1, lambda a, w, res, name: _mm_final_loss(a, w, res, target, P["final_g"], name))

    dx3, dx3_bf, dg_mlp1, d_up1, d_down1 = _mlp_bwd(x3, g_mlp[1:2], mlp1, dx4, dx4_bf, 1, None)
    tok = emit("mlp1", [d_up1, d_down1])
    d_wo = _mm(o, dx3_bf, mode="tn", out_dtypes=(BF16,), name="attn_dwo", after=tok)
    do = _mm(dx3_bf, w_o, mode="nt", out_dtypes=(BF16,), name="attn_do")
    dq, dk, dv = _attn_bwd(qh, kh, vh, do, B, L)
    dqkv, dq_g, dk_g = _qk_prep_bwd(qkv, dq, dk, dv, cos, sin, P["q_g"], P["k_g"])
    d_wqkv = _mm(h3, dqkv, mode="tn", o_shard=True, out_dtypes=(BF16,), name="attn_dwqkv")
    tok = emit("att", [d_wqkv, d_wo])
    dx2, dx2_bf, dg_mix1 = _mm_norm_bwd(dqkv, w_qkv, x2, dx3, g_mix[1:2], "attn_dh", after=tok)
    tok = emit("point_attn_done", [dx2_bf])
    dx1, dx1_bf, dg_mlp0, d_up0, d_down0 = _mlp_bwd(x1, g_mlp[0:1], mlp0, dx2, dx2_bf, 0, tok)
    tok = emit("mlp0", [d_up0, d_down0])
    d_wout = _mm(yg, dx1_bf, mode="tn", out_dtypes=(BF16,), name="rg_dwout", after=tok)
    tok = emit("rg_out", [d_wout])
    dyg = _mm(dx1_bf, w_out, mode="nt", name="rg_dyg", after=tok)
    dy, dgate = _gated_out_bwd(dyg, h_f, h_b, z)
    du_f, da_f, du_b, da_b = _scan_bwd(dy, a_f, h_f, a_b, h_b, B, L)
    drec_c, d_wa, d_wx, d_gvec = _gate_bwd(rec, du_f, da_f, du_b, da_b, wcat, gvec)
    tok = emit("gates", [d_wa, d_wx])
    dz, d_convwb = _conv_bwd(z, drec_c, conv_wb, dgate, B, L, after=tok)
    tok = emit("point_conv_done", [dz])
    d_win = _mm(h0, dz, mode="tn", o_shard=True, out_dtypes=(BF16,), name="rg_dwin", after=tok)
    tok = emit("rg_in", [d_win])
    grad_x, _, dg_mix0 = _mm_norm_bwd(dz, w_in, x, dx1, g_mix[0:1], "rg_dh", after=tok)

    norms = (_rows_at(dg_mix0, 0) + _rows_at(dg_mix1, 1) + _rows_at(dg_mlp0, 2) + _rows_at(dg_mlp1, 3)
             + _rows_at(d_final_g, 4))
    vec = jnp.concatenate([norms, d_convwb, d_gvec, _qk_slot(dq_g, dk_g)], axis=0)
    return loss_acc[0, 0], grad_x, vec


_MESH = pl.DeviceIdType.MESH


def _place():
    x, y, c = lax.axis_index("x"), lax.axis_index("y"), lax.axis_index("c")
    peers = [((1 - x) if j & 2 else x, (1 - y) if j & 1 else y) for j in (1, 2, 3)]
    return x, y, c, peers


def _comm_call(body, ins, out_shapes, n_sem, name):
    return pl.pallas_call(
        body, name=name, in_specs=[_ANY] * len(ins), out_specs=[_ANY] * len(out_shapes), out_shape=out_shapes,
        scratch_shapes=[pltpu.SemaphoreType.DMA((n_sem,)), pltpu.SemaphoreType.DMA((n_sem,)),
                        pltpu.SemaphoreType.DMA((len(ins),))],
    )(*ins)


def _all_devices_slots(v, name):
    def body(v_ref, out_ref, send, recv, lsem):
        x, y, c = lax.axis_index("x"), lax.axis_index("y"), lax.axis_index("c")
        me = 4 * x + 2 * y + c

        def peer(j):
            return (1 - x) if j & 4 else x, (1 - y) if j & 2 else y, (1 - c) if j & 1 else c

        def copy(j, slot):
            return pltpu.make_async_remote_copy(
                src_ref=v_ref, dst_ref=out_ref.at[slot], send_sem=send.at[j - 1], recv_sem=recv.at[j - 1],
                device_id=peer(j), device_id_type=_MESH)

        local = pltpu.make_async_copy(v_ref, out_ref.at[me], lsem.at[0])
        sends = [copy(j, me) for j in range(1, N_DEVICES)]
        for cp in [local] + sends:
            cp.start()
        for j in range(1, N_DEVICES):
            px, py, pc = peer(j)
            copy(j, 4 * px + 2 * py + pc).wait_recv()
        for cp in sends:
            cp.wait_send()
        local.wait()

    shape = jax.ShapeDtypeStruct((N_DEVICES,) + v.shape, v.dtype)
    return _comm_call(body, [v], [shape], N_DEVICES - 1, name)[0]


def _sum_leading(slots, name):
    def body(s_ref, o_ref):
        acc = s_ref[0]
        for d in range(1, slots.shape[0]):
            acc = acc + s_ref[d]
        o_ref[...] = acc

    return pl.pallas_call(body, name=name, out_shape=jax.ShapeDtypeStruct(slots.shape[1:], slots.dtype))(slots)


_HBM = pl.BlockSpec(memory_space=pltpu.HBM)
_SEM = pl.BlockSpec(memory_space=pltpu.SEMAPHORE)
_EFFECT = pltpu.SideEffectType.DATAFLOW_SIDE_EFFECTING


_COPIES = dict(gather=N_CHIPS - 1, scatter=N_CHIPS - 1, swap=1)


def _split_copies(kind, srcs, lands, send, recv):
    x, y, c, peers = _place()
    me = 2 * x + y
    per = _COPIES[kind]
    out = []
    for a in range(len(lands)):
        for j in range(per):
            if kind == "swap":
                src, there, here, dev = srcs[a], lands[a], lands[a], (x, y, 1 - c)
            else:
                px, py = peers[j]
                dev = (px, py, c)
                if kind == "gather":
                    src, there, here = lands[a].at[me], lands[a].at[me], lands[a].at[2 * px + py]
                else:
                    src, there, here = srcs[a].at[2 * px + py], lands[a].at[j], lands[a].at[j]
            mk = functools.partial(
                pltpu.make_async_remote_copy, src_ref=src, send_sem=send.at[per * a + j],
                recv_sem=recv.at[per * a + j], device_id=dev, device_id_type=_MESH)
            out.append((functools.partial(mk, dst_ref=there), functools.partial(mk, dst_ref=here)))
    return out


def _exchange_start(kind, srcs, lands, name, after=None):
    arrays = list(srcs) + list(lands)
    n_s, n, n_all = len(srcs), len(lands), len(srcs) + len(lands)
    n_sem = _COPIES[kind] * n
    order = _after_operand(after)
    n_x = len(order)

    def body(*refs):
        send, recv = refs[n_all + n_x], refs[n_all + n_x + 1]
        token = refs[-1]
        for started, _ in _split_copies(kind, refs[:n_s], refs[n_s:n_all], send, recv):
            started().start()
        token[...] = jnp.zeros(token.shape, F32)

    res = pl.pallas_call(
        body, name=name,
        out_shape=(pltpu.SemaphoreType.DMA((n_sem,)), pltpu.SemaphoreType.DMA((n_sem,)),
                   *[pltpu.HBM(a.shape, a.dtype) for a in arrays], jax.ShapeDtypeStruct((SUBLANES, LANES), F32)),
        in_specs=[_HBM] * n_all + [_ANY] * n_x,
        out_specs=(_SEM, _SEM, *[_HBM] * n_all, pl.BlockSpec(memory_space=pltpu.VMEM)),
        input_output_aliases={i: 2 + i for i in range(n_all)},
        compiler_params=pltpu.CompilerParams(has_side_effects=_EFFECT),
    )(*[pltpu.with_memory_space_constraint(a, pltpu.HBM) for a in arrays], *order)
    return (res[0], res[1], res[2:2 + n_s], res[2 + n_s:2 + n_all]), res[-1]


def _exchange_wait(kind, handle, after, name):
    send, recv, srcs, lands = handle
    arrays = list(srcs) + list(lands)
    n_s, n_all = len(srcs), len(arrays)

    def body(*refs):
        for started, landing in _split_copies(kind, refs[:n_s], refs[n_s:n_all], refs[n_all], refs[n_all + 1]):
            started().wait_send()
            landing().wait_recv()

    res = pl.pallas_call(
        body, name=name, out_shape=[pltpu.HBM(a.shape, a.dtype) for a in arrays],
        in_specs=[_HBM] * n_all + [_SEM, _SEM, _ANY], out_specs=[_HBM] * n_all,
        input_output_aliases={i: i for i in range(n_all)},
        compiler_params=pltpu.CompilerParams(has_side_effects=_EFFECT),
    )(*arrays, send, recv, after)
    return res[:n_s], res[n_s:]


def _index_operand(i):
    return jnp.reshape(i, (1,)).astype(jnp.int32)


def _cast_into_slot(src, row0, rows, me, dtype, name, after=None, add=None):
    cols = src.shape[1]
    tm = min(512, rows)
    order = _after_operand(after)
    terms = [src] + ([] if add is None else [add])

    def body(me_ref, *rest):
        val = rest[0][...]
        if add is not None:
            val = val + rest[1][...]
        rest[-1][...] = val.astype(dtype)

    return pl.pallas_call(
        body, name=name,
        grid_spec=pltpu.PrefetchScalarGridSpec(
            num_scalar_prefetch=1, grid=(rows // tm,),
            in_specs=[pl.BlockSpec((tm, cols), lambda i, me_ref: (i + row0 // tm, 0))] * len(terms)
            + [_ANY] * len(order),
            out_specs=pl.BlockSpec((None, tm, cols), lambda i, me_ref: (me_ref[0], i, 0))),
        out_shape=jax.ShapeDtypeStruct((N_CHIPS, rows, cols), dtype), compiler_params=_params(("parallel",)),
    )(_index_operand(me), *terms, *order)


def _sum_slots(mine, r, me, name):
    _, rows, cols = r.shape
    tm = min(512, rows)

    def body(me_ref, own_ref, r_ref, o_ref):
        o_ref[...] = ((own_ref[...].astype(F32) + r_ref[0].astype(F32)) + r_ref[1].astype(F32)) + r_ref[2].astype(F32)

    return pl.pallas_call(
        body, name=name,
        grid_spec=pltpu.PrefetchScalarGridSpec(
            num_scalar_prefetch=1, grid=(rows // tm,),
            in_specs=[pl.BlockSpec((None, tm, cols), lambda i, me_ref: (me_ref[0], i, 0)),
                      pl.BlockSpec((N_CHIPS - 1, tm, cols), lambda i, me_ref: (0, i, 0))],
            out_specs=pl.BlockSpec((tm, cols), lambda i, me_ref: (i, 0))),
        out_shape=jax.ShapeDtypeStruct((rows, cols), F32), compiler_params=_params(("parallel",)),
    )(_index_operand(me), mine, r)


def _adamw(w, m, v, ps, qs, name):
    rows, cols = w.shape
    seg_rows = ps[0].shape[0]
    tm = min(256, seg_rows)
    while seg_rows % tm:
        tm -= SUBLANES
    per, n_seg = seg_rows // tm, len(ps)
    parts = list(ps) + ([] if qs is None else list(qs))

    def body(w_ref, m_ref, v_ref, *rest):
        g_refs, outs = rest[:len(parts)], rest[len(parts):]
        grad = lambda s: g_refs[s][...] if qs is None else g_refs[s][...] + g_refs[n_seg + s][...]
        g = grad(0)
        for s in range(1, n_seg):
            g = jnp.where(pl.program_id(0) >= s * per, grad(s), g)
        m1 = ADAM_B1 * m_ref[...] + (1.0 - ADAM_B1) * g
        v1 = ADAM_B2 * v_ref[...] + (1.0 - ADAM_B2) * (g * g)
        m_hat = m1 / (1.0 - ADAM_B1 ** ADAM_STEP)
        v_hat = v1 / (1.0 - ADAM_B2 ** ADAM_STEP)
        outs[0][...] = g
        outs[1][...] = (-ADAM_LR) * (m_hat / (jnp.sqrt(v_hat) + ADAM_EPS) + ADAM_WD * w_ref[...])
        outs[2][...] = m1
        outs[3][...] = v1

    row_spec = pl.BlockSpec((tm, cols), lambda i: (i, 0))
    seg_spec = lambda s: pl.BlockSpec((tm, cols), lambda i: (jnp.clip(i - s * per, 0, per - 1), 0))
    return pl.pallas_call(
        body, name=name, grid=(rows // tm,),
        in_specs=[row_spec] * 3 + [seg_spec(s) for s in range(n_seg)] * (1 if qs is None else 2),
        out_specs=[row_spec] * 4, out_shape=[jax.ShapeDtypeStruct((rows, cols), F32)] * 4,
        compiler_params=_params(("arbitrary",)),
    )(w, m, v, *parts)


def _put_cols(shard, me):
    full = jnp.zeros((shard.shape[0], D_MODEL), F32)
    return lax.dynamic_update_slice(full, shard, (0, me * (D_MODEL // N_CHIPS)))


def _gate_vec_slot(b_a, b_x, lam):
    return _rows_at(b_a, _ROW_BA) + _rows_at(b_x, _ROW_BX) + _rows_at(lam, _ROW_LAM)


def _pack_vec(p, me):
    return jnp.concatenate([
        _rows_at(p["norm_mix_g"], 0) + _rows_at(p["norm_mlp_g"], 2) + _rows_at(p["final_g"][None], 4),
        _rows_at(_put_cols(p["rg_conv_w"][0, :, 0, :], me), 0) + _rows_at(p["rg_conv_b"], 4),
        _gate_vec_slot(_put_cols(p["rg_b_a"][0], me), _put_cols(p["rg_b_x"][0], me), _put_cols(p["rg_lam"][0], me)),
        _qk_slot(p["at_q_g"], p["at_k_g"]),
    ], axis=0)


def _unpack_vec(r, me):
    def cols(rows):
        return lax.dynamic_slice(rows, (0, me * (D_MODEL // N_CHIPS)), (rows.shape[0], D_MODEL // N_CHIPS))

    gate = r[16:24]
    return dict(
        norm_mix_g=r[0:2], norm_mlp_g=r[2:4], final_g=r[4], rg_conv_w=cols(r[8:12])[None, :, None, :],
        rg_conv_b=r[12:13], rg_b_a=cols(gate[_ROW_BA:_ROW_BA + 2])[None], rg_b_x=cols(gate[_ROW_BX:_ROW_BX + 2])[None],
        rg_lam=cols(gate[_ROW_LAM:_ROW_LAM + 2])[None], at_q_g=r[24:25, 0:HEAD_DIM],
        at_k_g=r[24:25, HEAD_DIM:2 * HEAD_DIM])


_WEIGHTS = ['norm_mix_g', 'norm_mlp_g', 'rg_w_in', 'rg_conv_w', 'rg_conv_b', 'rg_w_a', 'rg_b_a', 'rg_w_x', 'rg_b_x',
            'rg_lam', 'rg_w_out', 'at_w_qkv', 'at_q_g', 'at_k_g', 'at_w_o', 'mlp_w_up', 'mlp_w_down', 'final_g']
_BIG = dict(rg_w_in=["rg_w_in"], rg_w_out=["rg_w_out"], at_w_qkv=["at_w_qkv"], at_w_o=["at_w_o"],
            mlp_w_up=["up0", "up1"], mlp_w_down=["down0", "down1"])


def kernel(x, *args):
    n_w = len(_WEIGHTS)
    w = dict(zip(_WEIGHTS, args[:n_w]))
    target = args[n_w]
    m = dict(zip(_WEIGHTS, args[n_w + 1:2 * n_w + 1]))
    v = dict(zip(_WEIGHTS, args[2 * n_w + 1:3 * n_w + 1]))
    B, L, _ = x.shape
    T = B * L
    me = 2 * lax.axis_index("x") + lax.axis_index("y")

    vec = jnp.concatenate([_gate_vec_slot(w["rg_b_a"][0], w["rg_b_x"][0], w["rg_lam"][0]),
                           _rows_at(w["rg_conv_w"][0, :, 0, :], 0)], axis=0)
    flat = lambda a: a.reshape(-1, a.shape[-1])
    rows_of = lambda k: w[k].shape[-2]
    groups = [("rg", [("rg_w_in", 0, BF16), ("rg_w_out", 0, BF16), (vec, 0, F32)]),
              ("mlp0", [("mlp_w_up", 0, BF16), ("mlp_w_down", 0, BF16)]),
              ("att", [("at_w_qkv", 0, BF16), ("at_w_o", 0, BF16)]),
              ("mlp1", [("mlp_w_up", 1, BF16), ("mlp_w_down", 1, BF16)])]
    gathers, tok = {}, None
    for group, members in groups:
        lands = []
        for n, (k, layer, dtype) in enumerate(members):
            src, rows = (flat(w[k]), rows_of(k)) if isinstance(k, str) else (k, k.shape[0])
            lands.append(_cast_into_slot(src, layer * rows, rows, me, dtype, f"place_{group}{n}", after=tok))
        gathers[group], tok = _exchange_start("gather", [], lands, f"gather_{group}_start", after=tok)
    wcat = _make_wcat(w["rg_w_a"], w["rg_w_x"]).astype(BF16)

    def fetch(group, after):
        _, full = _exchange_wait("gather", gathers[group], after, f"gather_{group}_wait")
        if group == "rg":
            vec_full = jnp.transpose(full[2], (1, 0, 2)).reshape(2 * SUBLANES, D_MODEL)
            conv_wb = vec_full[SUBLANES:] + _rows_at(w["rg_conv_b"], 4)
            return full[0], full[1].reshape(D_MODEL, D_MODEL), conv_wb, wcat, vec_full[:SUBLANES]
        if group == "att":
            return full[0], full[1].reshape(D_MODEL, D_MODEL)
        return full[0], full[1].reshape(4 * D_MODEL, D_MODEL)

    names = dict(mlp1=["up1", "down1"], att=["at_w_qkv", "at_w_o"], mlp0=["up0", "down0"], rg_out=["rg_w_out"],
                 rg_in=["rg_w_in"], gates=["rg_w_a", "rg_w_x"])
    scatters, swaps, P, Q, res = {}, [], {}, {}, {}

    def start_scatter(group, grads):
        srcs = [g.reshape(N_CHIPS, -1, g.shape[-1]) for g in grads]
        lands = [lax.empty((N_CHIPS - 1,) + s.shape[1:], s.dtype) for s in srcs]
        scatters[group], token = _exchange_start("scatter", srcs, lands, f"scatter_{group}_start")
        return token

    def settle(groups, after):
        keys, parts = [], []
        for group in groups:
            srcs, lands = _exchange_wait("scatter", scatters[group], after, f"scatter_{group}_wait")
            for k, s, r in zip(names[group], srcs, lands):
                keys.append(k)
                parts.append(_sum_slots(s, r, me, f"sum_{k}"))
        handle, token = _exchange_start("swap", parts, [lax.empty(p.shape, F32) for p in parts],
                                        f"swap_{groups[0]}_start")
        swaps.append((keys, handle, f"swap_{groups[0]}_wait"))
        return token

    def finish(after):
        for keys, handle, name in swaps:
            mine, theirs = _exchange_wait("swap", handle, after, name)
            P.update(zip(keys, mine))
            Q.update(zip(keys, theirs))
        swaps.clear()
        last = after
        for k, parts in _BIG.items():
            if k in res or any(p not in P for p in parts):
                continue
            shape = w[k].shape
            two_d = lambda a: a.reshape(-1, shape[-1])
            outs = _adamw(two_d(w[k]), two_d(m[k]), two_d(v[k]), [P[p] for p in parts], [Q[p] for p in parts],
                          f"adamw_{k}")
            res[k] = [o.reshape(shape) for o in outs]
            last = outs[0]
        if "rg_w_a" in P and "gates" not in gathers:
            lands = [_cast_into_slot(P[k], 0, P[k].shape[0], me, F32, f"place_{k}", after=last, add=Q[k])
                     for k in names["gates"]]
            gathers["gates"], last = _exchange_start("gather", [], lands, "gather_gates_start", after=last)
        return last

    def emit(event, arrays):
        if event == "point_attn_done":
            return settle(["mlp1"], arrays[0])
        if event == "point_conv_done":
            return settle(["att", "mlp0", "rg_out", "gates"], arrays[0])
        token = start_scatter(event, arrays)
        return finish(token) if event == "rg_in" else token

    P_vec = dict(norm_mix_g=w["norm_mix_g"], norm_mlp_g=w["norm_mlp_g"], final_g=w["final_g"][None],
                 q_g=w["at_q_g"], k_g=w["at_k_g"])
    loss_part, grad_x, vec_part = _local_step(x.reshape(T, D_MODEL), target.reshape(T, D_MODEL), P_vec, fetch, emit,
                                              B, L, after=tok)
    loss = lax.psum(loss_part, ("x", "y", "c"))

    finish(settle(["rg_in"], grad_x))
    _, gate_grads = _exchange_wait("gather", gathers["gates"], grad_x, "gather_gates_wait")
    for k, g in zip(names["gates"], gate_grads):
        two_d = lambda a: a.reshape(g.shape[0] * g.shape[1], g.shape[2])
        outs = _adamw(two_d(w[k]), two_d(m[k]), two_d(v[k]), [two_d(g)], None, f"adamw_{k}")
        res[k] = [o.reshape(w[k].shape) for o in outs]
    vec_grad = _sum_leading(_all_devices_slots(vec_part, "allreduce_vec"), "sum_vec")
    outs = _adamw(_pack_vec(w, me), _pack_vec(m, me), _pack_vec(v, me), [vec_grad], None, "adamw_vec")
    unpacked = [_unpack_vec(o, me) for o in outs]
    for k in _WEIGHTS:
        if k not in res:
            res[k] = [u[k] for u in unpacked]

    result = [loss, grad_x.reshape(B, L, D_MODEL)]
    for slot in range(4):
        result += [res[k][slot] for k in _WEIGHTS]
    return tuple(result)
```

```python
import functools
import math

import jax
import jax.numpy as jnp
import numpy as np
from jax import lax
from jax.experimental import pallas as pl
from jax.experimental.pallas import tpu as pltpu

F32 = jnp.float32
BF16 = jnp.bfloat16

D_MODEL = 1024
HEAD_DIM = 128
N_HEADS = 8
N_KV = 2
GROUP = N_HEADS // N_KV
LRU_BLOCKS = 8
LRU_BW = 128
GRID_W = 64
ROPE_THETA = 10000.0
EPS = 1e-6
RG_C = 8.0
SCALE = 1.0 / math.sqrt(HEAD_DIM)
N_CHIPS = 4

ADAM_LR = 0.001
ADAM_B1 = 0.9
ADAM_B2 = 0.999
ADAM_EPS = 1e-08
ADAM_WD = 0.01
ADAM_STEP = 10

V7X_VMEM_BYTES = 64 * 1024 * 1024
VMEM_LIMIT = V7X_VMEM_BYTES * 3 // 4
LANES = 128
SUBLANES = 8

N_DEVICES = 8
VEC_ROWS = 32
LOSS_ROW = 5


def _params(sem):
    return pltpu.CompilerParams(dimension_semantics=sem, vmem_limit_bytes=VMEM_LIMIT)


_ANY = pl.BlockSpec(memory_space=pl.ANY)
_NN = (((1,), (0,)), ((), ()))
_NT = (((1,), (1,)), ((), ()))
_TN = (((0,), (0,)), ((), ()))


def _after_operand(after):
    return [] if after is None else [after]


def _fit(t, n):
    if n <= t:
        return n
    c = (t // LANES) * LANES
    while n % c:
        c -= LANES
    return c


MM_VMEM_BUDGET = VMEM_LIMIT * 3 // 4
def _mm_tiles(M, K, ns, out_dtypes, extras, whole_rows):
    for tm in (2048, 1024, 512, 256, 128):
        for tn in ((ns,) if whole_rows else (1024, 512, 256)):
            tn = _fit(tn, ns)
            per_row = 2 * (2 * K) + 4 * tn + sum(2 * tn * jnp.dtype(d).itemsize for d in out_dtypes)
            per_row += sum(2 * tn * e.dtype.itemsize for e in extras)
            if M % tm == 0 and 2 * (2 * K * tn) + tm * per_row <= MM_VMEM_BUDGET:
                return tm, tn
    raise ValueError(f"no tile fits VMEM for M={M} K={K} N={ns}")


def _mm(a, b, *, mode, name, out_dtypes=(F32,), b_shard=False, o_shard=False, extras=(), epi=None, after=None,
        bcast=(), accs=(), ref_epi=None):
    if mode == "tn":
        K, M = a.shape
        N = b.shape[1]
    else:
        M, K = a.shape
        if mode == "nn":
            N = b.shape[0] * b.shape[2] if b_shard else b.shape[1]
        else:
            N = b.shape[1] if b_shard else b.shape[0]
    ns = N
    if b_shard and mode == "nn":
        ns = b.shape[2]
    elif o_shard:
        ns = N // N_CHIPS
    tm, tn = _mm_tiles(M, K, ns, out_dtypes, extras, whole_rows=ref_epi is not None)
    if ref_epi is not None:
        tm = min(tm, 512)
    grid = (M // tm, N // tn)
    q = ns // tn

    if mode == "tn":
        a_spec = pl.BlockSpec((K, tm), lambda i, j: (0, i))
        b_spec = pl.BlockSpec((K, tn), lambda i, j: (0, j))
        dims = _TN
    elif mode == "nn":
        a_spec = pl.BlockSpec((tm, K), lambda i, j: (i, 0))
        if b_shard:
            b_spec = pl.BlockSpec((None, K, tn), lambda i, j: (j // q, 0, j % q))
        else:
            b_spec = pl.BlockSpec((K, tn), lambda i, j: (0, j))
        dims = _NN
    else:
        a_spec = pl.BlockSpec((tm, K), lambda i, j: (i, 0))
        if b_shard:
            ks = b.shape[2]
            b_spec = pl.BlockSpec((N_CHIPS, tn, ks), lambda i, j: (0, j, 0))
        else:
            b_spec = pl.BlockSpec((tn, K), lambda i, j: (j, 0))
        dims = _NT

    if o_shard:
        o_specs = [pl.BlockSpec((None, tm, tn), lambda i, j: (j // q, i, j % q))]
        o_shapes = [jax.ShapeDtypeStruct((N_CHIPS, M, ns), out_dtypes[0])]
    else:
        o_specs = [pl.BlockSpec((tm, tn), lambda i, j: (i, j)) for _ in out_dtypes]
        o_shapes = [jax.ShapeDtypeStruct((M, N), dt) for dt in out_dtypes]
    e_specs = [pl.BlockSpec((tm, tn), lambda i, j: (i, j)) for _ in extras]
    e_specs += [pl.BlockSpec(v.shape, lambda i, j: (0, 0)) for v in bcast]
    o_specs += [pl.BlockSpec(s, lambda i, j: (0, 0)) for s in accs]
    o_shapes += [jax.ShapeDtypeStruct(s, F32) for s in accs]
    n_e, n_b, n_o, n_a = len(extras), len(bcast), len(out_dtypes), len(accs)
    order = _after_operand(after)
    n_x = len(order)
    if epi is None:
        epi = lambda acc: (acc,)

    def body(a_ref, b_ref, *rest):
        e_refs, b_refs = rest[:n_e], rest[n_e:n_e + n_b]
        o_refs = rest[n_e + n_b + n_x:n_e + n_b + n_x + n_o]
        a_refs = rest[n_e + n_b + n_x + n_o:]
        if n_a:
            @pl.when((pl.program_id(0) == 0) & (pl.program_id(1) == 0))
            def _():
                for r in a_refs:
                    r[...] = jnp.zeros(r.shape, F32)
        if mode == "nt" and b_shard:
            acc = None
            for s in range(N_CHIPS):
                part = lax.dot_general(a_ref[:, s * ks:(s + 1) * ks], b_ref[s], dims, preferred_element_type=F32)
                acc = part if acc is None else acc + part
        else:
            acc = lax.dot_general(a_ref[...], b_ref[...], dims, preferred_element_type=F32)
        if ref_epi is not None:
            ref_epi(acc, e_refs, b_refs, o_refs, a_refs)
            return
        outs = epi(acc, *[r[...] for r in e_refs])
        for r, o in zip(o_refs, outs):
            r[...] = o.astype(r.dtype)

    outs = pl.pallas_call(
        body, name=name, grid=grid, in_specs=[a_spec, b_spec] + e_specs + [_ANY] * n_x, out_specs=o_specs,
        out_shape=o_shapes, compiler_params=_params(("arbitrary", "arbitrary") if n_a else ("parallel", "parallel")),
    )(a, b, *extras, *bcast, *order)
    return outs[0] if n_o + n_a == 1 else outs


def _rowwise(fn, rows, bcast, outs, accs=(), *, tm, name, after=None):
    def norm(r):
        return r if isinstance(r, tuple) else (r, r.shape[1], 0)

    rows = [norm(r) for r in rows]
    T = rows[0][0].shape[0]
    tm = min(tm, T)
    while T % tm:
        tm -= SUBLANES
    n_r, n_b, n_o, n_a = len(rows), len(bcast), len(outs), len(accs)
    order = _after_operand(after)
    n_x = len(order)
    in_specs = [pl.BlockSpec((tm, c), functools.partial(lambda i, cb: (i, cb), cb=cb)) for _, c, cb in rows]
    in_specs += [pl.BlockSpec(b.shape, lambda i: (0, 0)) for b in bcast] + [_ANY] * n_x
    out_specs = [pl.BlockSpec((tm, o[0]), lambda i: (i, 0)) for o in outs]
    out_specs += [pl.BlockSpec(s, lambda i: (0, 0)) for s in accs]
    out_shape = [jax.ShapeDtypeStruct((T, o[2] if len(o) > 2 else o[0]), o[1]) for o in outs]
    out_shape += [jax.ShapeDtypeStruct(s, F32) for s in accs]

    def body(*refs):
        in_refs = refs[:n_r]
        b_refs = refs[n_r:n_r + n_b]
        o_refs = refs[n_r + n_b + n_x:n_r + n_b + n_x + n_o]
        a_refs = refs[n_r + n_b + n_x + n_o:]
        if n_a:
            @pl.when(pl.program_id(0) == 0)
            def _():
                for r in a_refs:
                    r[...] = jnp.zeros(r.shape, F32)
        fn(in_refs, b_refs, o_refs, a_refs)

    res = pl.pallas_call(
        body, name=name, grid=(T // tm,), in_specs=in_specs, out_specs=out_specs, out_shape=out_shape,
        compiler_params=_params(("arbitrary",) if n_a else ("parallel",)),
    )(*[r[0] for r in rows], *bcast, *order)
    return res


def _rsum(x):
    return jnp.sum(x, axis=0, keepdims=True)


def _rms_fwd(x, g, name, after=None):
    def fn(ins, bs, outs, accs):
        xv = ins[0][...]
        r = lax.rsqrt(jnp.mean(xv * xv, axis=-1, keepdims=True) + EPS)
        outs[0][...] = (xv * r * bs[0][...]).astype(BF16)

    return _rowwise(fn, [x], [g], [(D_MODEL, BF16)], tm=512, name=name, after=after)[0]


def _rms_bwd_math(xv, dh, g):
    r = lax.rsqrt(jnp.mean(xv * xv, axis=-1, keepdims=True) + EPS)
    hn = xv * r
    dgh = dh * g
    dx = r * (dgh - hn * jnp.mean(dgh * hn, axis=-1, keepdims=True))
    return dx, _rsum(dh * hn)


def _mm_norm_bwd(dy, w, x, dres, g, name, after=None):
    def epilogue(acc, e_refs, b_refs, o_refs, a_refs):
        dx, dg = _rms_bwd_math(e_refs[0][...], acc, b_refs[0][...])
        dx = dx + e_refs[1][...]
        o_refs[0][...] = dx
        o_refs[1][...] = dx.astype(BF16)
        a_refs[0][...] += dg

    return _mm(dy, w, mode="nt", b_shard=True, out_dtypes=(F32, BF16), extras=(x, dres), bcast=(g,),
               accs=((1, D_MODEL),), ref_epi=epilogue, name=name, after=after)


def _mm_res_norm(a, w, res, g, name):
    def epilogue(acc, e_refs, b_refs, o_refs, a_refs):
        xv = acc + e_refs[0][...]
        o_refs[0][...] = xv
        r = lax.rsqrt(jnp.mean(xv * xv, axis=-1, keepdims=True) + EPS)
        o_refs[1][...] = (xv * r * b_refs[0][...]).astype(BF16)

    return _mm(a, w, mode="nn", out_dtypes=(F32, BF16), extras=(res,), bcast=(g,), ref_epi=epilogue, name=name)


def _mm_final_loss(a, w, res, target, g, name):
    def epilogue(acc, e_refs, b_refs, o_refs, a_refs):
        xv = acc + e_refs[0][...]
        gv = b_refs[0][...]
        r = lax.rsqrt(jnp.mean(xv * xv, axis=-1, keepdims=True) + EPS)
        e = xv * r * gv - e_refs[1][...]
        tok = jnp.mean(e * e, axis=-1, keepdims=True)
        a_refs[0][...] += 0.5 * jnp.sum(tok, axis=0, keepdims=True) * jnp.ones((1, LANES), F32)
        dx, dg = _rms_bwd_math(xv, e * (1.0 / D_MODEL), gv)
        o_refs[0][...] = dx
        o_refs[1][...] = dx.astype(BF16)
        a_refs[1][...] += dg

    return _mm(a, w, mode="nn", out_dtypes=(F32, BF16), extras=(res, target), bcast=(g,),
               accs=((1, LANES), (1, D_MODEL)), ref_epi=epilogue, name=name)


def _relu2(acc):
    r = jnp.maximum(acc, 0.0)
    return r * r, r


def _mlp_fwd(x, h, fetch, tag, finish):
    w_up, w_down = fetch(f"mlp{tag}", h)
    a, r = _mm(h, w_up, mode="nn", b_shard=True, out_dtypes=(BF16, BF16), epi=_relu2, name=f"mlp{tag}_up")
    return finish(a, w_down, x, f"mlp{tag}_down"), (h, a, r, w_up, w_down)


def _mlp_bwd(x, g, saved, dx, dx_bf, tag, after):
    h, a, r, w_up, w_down = saved
    d_down = _mm(a, dx_bf, mode="tn", out_dtypes=(BF16,), name=f"mlp{tag}_dwdown", after=after)
    dup = _mm(dx_bf, w_down, mode="nt", extras=(r,), out_dtypes=(BF16,),
              epi=lambda acc, rv: (acc * (2.0 * rv.astype(F32)),), name=f"mlp{tag}_dup")
    d_up = _mm(h, dup, mode="tn", o_shard=True, out_dtypes=(BF16,), name=f"mlp{tag}_dwup")
    dx_new, dx_new_bf, dg = _mm_norm_bwd(dup, w_up, x, dx, g, f"mlp{tag}_dh")
    return dx_new, dx_new_bf, dg, d_up, d_down


def _rope_tables(L, B):
    rows = L // GRID_W
    row = np.repeat(np.arange(rows, dtype=np.float32), GRID_W)
    col = np.tile(np.arange(GRID_W, dtype=np.float32), rows)
    inv = (ROPE_THETA ** (-np.arange(HEAD_DIM // 4, dtype=np.float32) / (HEAD_DIM // 4))).astype(np.float32)
    ar, ac = row[:, None] * inv, col[:, None] * inv
    cos = np.concatenate([np.cos(ar), np.cos(ar), np.cos(ac), np.cos(ac)], axis=-1)
    sin = np.concatenate([-np.sin(ar), np.sin(ar), -np.sin(ac), np.sin(ac)], axis=-1)
    return jnp.asarray(np.tile(cos, (B, 1)), F32), jnp.asarray(np.tile(sin, (B, 1)), F32)


def _swap_halves(x):
    lane = lax.broadcasted_iota(jnp.int32, x.shape, 1)
    return jnp.where((lane % 64) < 32, pltpu.roll(x, HEAD_DIM - 32, 1), pltpu.roll(x, 32, 1))


def _qk_prep(qkv, cos, sin, q_g, k_g):
    def fn(ins, bs, outs, accs):
        c, s = ins[1][...], ins[2][...]
        for h in range(N_HEADS + N_KV):
            xv = ins[0][:, h * HEAD_DIM:(h + 1) * HEAD_DIM]
            g = bs[0][...] if h < N_HEADS else bs[1][...]
            r = lax.rsqrt(jnp.mean(xv * xv, axis=-1, keepdims=True) + EPS)
            z = xv * r * g
            y = (z * c + _swap_halves(z) * s).astype(BF16)
            if h < N_HEADS:
                outs[0][:, h * HEAD_DIM:(h + 1) * HEAD_DIM] = y
            else:
                outs[1][:, (h - N_HEADS) * HEAD_DIM:(h - N_HEADS + 1) * HEAD_DIM] = y
        outs[2][...] = ins[0][:, (N_HEADS + N_KV) * HEAD_DIM:].astype(BF16)

    kvw = N_KV * HEAD_DIM
    return _rowwise(fn, [qkv, cos, sin], [q_g, k_g], [(D_MODEL, BF16), (kvw, BF16), (kvw, BF16)], tm=512,
                    name="attn_qk_prep")


def _qk_prep_bwd(qkv, dq, dk, dv, cos, sin, q_g, k_g):
    def fn(ins, bs, outs, accs):
        c, s = ins[4][...], ins[5][...]
        for h in range(N_HEADS + N_KV):
            sl = slice(h * HEAD_DIM, (h + 1) * HEAD_DIM)
            xv = ins[0][:, sl]
            if h < N_HEADS:
                g, dy, acc = bs[0][...], ins[1][:, sl], accs[0]
            else:
                ks = slice((h - N_HEADS) * HEAD_DIM, (h - N_HEADS + 1) * HEAD_DIM)
                g, dy, acc = bs[1][...], ins[2][:, ks], accs[1]
            r = lax.rsqrt(jnp.mean(xv * xv, axis=-1, keepdims=True) + EPS)
            xn = xv * r
            dz = dy * c - _swap_halves(dy) * s
            acc[...] += _rsum(dz * xn)
            dxn = dz * g
            outs[0][:, sl] = (r * (dxn - xn * jnp.mean(dxn * xn, axis=-1, keepdims=True))).astype(BF16)
        outs[0][:, (N_HEADS + N_KV) * HEAD_DIM:] = ins[3][...].astype(BF16)

    return _rowwise(fn, [qkv, dq, dk, dv, cos, sin], [q_g, k_g], [(qkv.shape[1], BF16)],
                    [(1, HEAD_DIM), (1, HEAD_DIM)], tm=256, name="attn_qk_prep_bwd")


_EXP2_SCALE = SCALE * math.log2(math.e)


def _exp_rows(q, k):
    s = lax.dot_general(q, k, _NT, preferred_element_type=F32)
    p = jnp.exp2((s - jnp.max(s, axis=-1, keepdims=True)) * _EXP2_SCALE)
    return p, jnp.sum(p, axis=-1, keepdims=True)


def _attn_fwd(q, k, v, B, L, tq=1024, sub=256):
    tq = min(tq, L)
    sub = min(sub, tq)
    nq = L // tq

    def body(q_ref, k_ref, v_ref, o_ref):
        kv, vv = k_ref[...], v_ref[...]
        for c in range(tq // sub):
            rows = slice(c * sub, (c + 1) * sub)
            p, l = _exp_rows(q_ref[rows, :], kv)
            o = jnp.dot(p.astype(BF16), vv, preferred_element_type=F32)
            o_ref[rows, :] = (o * (1.0 / l)).astype(o_ref.dtype)

    return pl.pallas_call(
        body, name="attn_fwd", grid=(B, N_HEADS, nq),
        in_specs=[pl.BlockSpec((tq, HEAD_DIM), lambda b, h, i: (b * nq + i, h)),
                  pl.BlockSpec((L, HEAD_DIM), lambda b, h, i: (b, h // GROUP)),
                  pl.BlockSpec((L, HEAD_DIM), lambda b, h, i: (b, h // GROUP))],
        out_specs=pl.BlockSpec((tq, HEAD_DIM), lambda b, h, i: (b * nq + i, h)),
        out_shape=jax.ShapeDtypeStruct((B * L, D_MODEL), BF16),
        compiler_params=_params(("parallel", "parallel", "parallel")),
    )(q, k, v)


def _attn_bwd(q, k, v, do, B, L, tq=512, sub=256):
    tq = min(tq, L)
    sub = min(sub, tq)
    nq = L // tq

    def body(q_ref, k_ref, v_ref, do_ref, dq_ref, dk_ref, dv_ref):
        @pl.when((pl.program_id(2) == 0) & (pl.program_id(3) == 0))
        def _():
            dk_ref[...] = jnp.zeros(dk_ref.shape, F32)
            dv_ref[...] = jnp.zeros(dv_ref.shape, F32)

        kv, vv = k_ref[...], v_ref[...]
        ps, es, dos, qs = [], [], [], []
        for c in range(tq // sub):
            rows = slice(c * sub, (c + 1) * sub)
            qc, doc = q_ref[rows, :], do_ref[rows, :]
            p, l = _exp_rows(qc, kv)
            inv = 1.0 / l
            dp = lax.dot_general(doc, vv, _NT, preferred_element_type=F32)
            delta = jnp.sum(p * dp, axis=-1, keepdims=True) * inv
            e = (p * (dp - delta)).astype(BF16)
            dq_ref[rows, :] = jnp.dot(e, kv, preferred_element_type=F32) * (inv * SCALE)
            ps.append(p.astype(BF16))
            es.append(e)
            dos.append((doc.astype(F32) * inv).astype(BF16))
            qs.append((qc.astype(F32) * (inv * SCALE)).astype(BF16))
        cat = lambda xs: xs[0] if len(xs) == 1 else jnp.concatenate(xs, axis=0)
        dv_ref[...] += lax.dot_general(cat(ps), cat(dos), _TN, preferred_element_type=F32)
        dk_ref[...] += lax.dot_general(cat(es), cat(qs), _TN, preferred_element_type=F32)

    qmap = lambda b, kh, g, i: (b * nq + i, kh * GROUP + g)
    kmap = lambda b, kh, g, i: (b, kh)
    kvw = N_KV * HEAD_DIM
    return pl.pallas_call(
        body, name="attn_bwd", grid=(B, N_KV, GROUP, nq),
        in_specs=[pl.BlockSpec((tq, HEAD_DIM), qmap), pl.BlockSpec((L, HEAD_DIM), kmap),
                  pl.BlockSpec((L, HEAD_DIM), kmap), pl.BlockSpec((tq, HEAD_DIM), qmap)],
        out_specs=[pl.BlockSpec((tq, HEAD_DIM), qmap), pl.BlockSpec((L, HEAD_DIM), kmap),
                   pl.BlockSpec((L, HEAD_DIM), kmap)],
        out_shape=[jax.ShapeDtypeStruct((B * L, D_MODEL), F32), jax.ShapeDtypeStruct((B * L, kvw), F32),
                   jax.ShapeDtypeStruct((B * L, kvw), F32)],
        compiler_params=_params(("parallel", "parallel", "arbitrary", "arbitrary")),
    )(q, k, v, do)


def _conv_shift(x, t, L, k):
    if k == 2:
        return x
    if k < 2:
        return jnp.where(t >= 2 - k, pltpu.roll(x, 2 - k, 0), 0.0)
    return jnp.where(t < L - (k - 2), pltpu.roll(x, L - (k - 2), 0), 0.0)


def _conv_fwd(z, wb, B, L, tc=256):
    noff = D_MODEL // tc

    def body(z_ref, w_ref, o_ref):
        x = z_ref[...]
        t = lax.broadcasted_iota(jnp.int32, x.shape, 0)
        acc = w_ref[4:5, :] + w_ref[2:3, :] * x
        for k in (0, 1, 3):
            acc = acc + w_ref[k:k + 1, :] * _conv_shift(x, t, L, k)
        o_ref[...] = acc

    return pl.pallas_call(
        body, name="rg_conv", grid=(B, noff),
        in_specs=[pl.BlockSpec((L, tc), lambda b, j: (b, noff + j)), pl.BlockSpec((SUBLANES, tc), lambda b, j: (0, j))],
        out_specs=pl.BlockSpec((L, tc), lambda b, j: (b, j)),
        out_shape=jax.ShapeDtypeStruct((B * L, D_MODEL), F32),
        compiler_params=_params(("parallel", "parallel")),
    )(z, wb)


def _conv_bwd(z, g, wb, dz, B, L, tc=256, after=None):
    noff = D_MODEL // tc
    order = _after_operand(after)

    def body(z_ref, g_ref, w_ref, dz_in, *rest):
        dx_ref, dw_ref = rest[len(order):]

        @pl.when(pl.program_id(1) == 0)
        def _():
            dw_ref[...] = jnp.zeros(dw_ref.shape, F32)

        x, gv = z_ref[...], g_ref[...]
        t = lax.broadcasted_iota(jnp.int32, x.shape, 0)
        dx = w_ref[2:3, :] * gv
        for k in (0, 1, 3):
            dx = dx + w_ref[k:k + 1, :] * _conv_shift(gv, t, L, 4 - k)
        dx_ref[...] = dx.astype(BF16)
        for k in range(4):
            dw_ref[k:k + 1, :] += _rsum(_conv_shift(x, t, L, k) * gv)
        dw_ref[4:5, :] += _rsum(gv)

    return pl.pallas_call(
        body, name="rg_conv_bwd", grid=(noff, B),
        in_specs=[pl.BlockSpec((L, tc), lambda j, b: (b, noff + j)), pl.BlockSpec((L, tc), lambda j, b: (b, j)),
                  pl.BlockSpec((SUBLANES, tc), lambda j, b: (0, j)), _ANY] + [_ANY] * len(order),
        out_specs=[pl.BlockSpec((L, tc), lambda j, b: (b, noff + j)),
                   pl.BlockSpec((SUBLANES, tc), lambda j, b: (0, j))],
        out_shape=[jax.ShapeDtypeStruct(dz.shape, dz.dtype), jax.ShapeDtypeStruct((SUBLANES, D_MODEL), F32)],
        input_output_aliases={3: 0},
        compiler_params=_params(("parallel", "arbitrary")),
    )(z, g, wb, dz, *order)


def _softplus(x):
    return jnp.maximum(x, 0.0) + jnp.log1p(jnp.exp(-jnp.abs(x)))


_ROW_BA, _ROW_BX, _ROW_LAM = 0, 2, 4


def _gate_math(xb, pre, vec_ref, d, sl):
    pa = pre[:, (2 * d) * LRU_BW:(2 * d + 1) * LRU_BW] + vec_ref[_ROW_BA + d:_ROW_BA + d + 1, sl]
    px = pre[:, (2 * d + 1) * LRU_BW:(2 * d + 2) * LRU_BW] + vec_ref[_ROW_BX + d:_ROW_BX + d + 1, sl]
    r = 0.5 * jnp.tanh(0.5 * pa) + 0.5
    i = 0.5 * jnp.tanh(0.5 * px) + 0.5
    sp = _softplus(-vec_ref[_ROW_LAM + d:_ROW_LAM + d + 1, sl])
    log_a = (-RG_C) * r * sp
    a = jnp.exp(log_a)
    th = jnp.tanh(log_a)
    om = -2.0 * th / (1.0 - th)
    mult = jnp.sqrt(om)
    return a, mult * (i * xb), (r, i, sp, om, mult)


def _gate_fwd(rec, wcat, gvec):
    def fn(ins, bs, outs, accs):
        for blk in range(LRU_BLOCKS):
            sl = slice(blk * LRU_BW, (blk + 1) * LRU_BW)
            xb = ins[0][:, sl]
            pre = jnp.dot(xb.astype(BF16), bs[0][sl, :], preferred_element_type=F32)
            for d in range(2):
                a, u, _ = _gate_math(xb, pre, bs[1], d, sl)
                outs[2 * d][:, sl] = a
                outs[2 * d + 1][:, sl] = u

    return _rowwise(fn, [rec], [wcat, gvec], [(D_MODEL, F32)] * 4, tm=256, name="rg_gate")


def _gate_bwd(rec, du_f, da_f, du_b, da_b, wcat, gvec):
    def fn(ins, bs, outs, accs):
        for blk in range(LRU_BLOCKS):
            sl = slice(blk * LRU_BW, (blk + 1) * LRU_BW)
            xb = ins[0][:, sl]
            xb16 = xb.astype(BF16)
            w = bs[0][sl, :]
            pre = jnp.dot(xb16, w, preferred_element_type=F32)
            dx = jnp.zeros_like(xb)
            dpre = []
            for d in range(2):
                a, _, (r, i, sp, om, mult) = _gate_math(xb, pre, bs[1], d, sl)
                du, da = ins[1 + 2 * d][:, sl], ins[2 + 2 * d][:, sl]
                d_i = du * mult * xb
                d_mult = du * i * xb
                dx = dx + du * mult * i
                dlog = da * a - d_mult * (1.0 - om) / mult
                d_r = dlog * ((-RG_C) * sp)
                d_sp = _rsum(dlog * ((-RG_C) * r))
                lam = bs[1][_ROW_LAM + d:_ROW_LAM + d + 1, sl]
                accs[2][_ROW_LAM + d:_ROW_LAM + d + 1, sl] += d_sp * (-jax.nn.sigmoid(-lam))
                dpa = d_r * r * (1.0 - r)
                dpx = d_i * i * (1.0 - i)
                accs[2][_ROW_BA + d:_ROW_BA + d + 1, sl] += _rsum(dpa)
                accs[2][_ROW_BX + d:_ROW_BX + d + 1, sl] += _rsum(dpx)
                dpre += [dpa, dpx]
            dpre = jnp.concatenate(dpre, axis=1).astype(BF16)
            dw = lax.dot_general(xb16, dpre, _TN, preferred_element_type=F32)
            for d in range(2):
                rows = slice(d * D_MODEL + blk * LRU_BW, d * D_MODEL + (blk + 1) * LRU_BW)
                accs[0][rows, :] += dw[:, (2 * d) * LRU_BW:(2 * d + 1) * LRU_BW]
                accs[1][rows, :] += dw[:, (2 * d + 1) * LRU_BW:(2 * d + 2) * LRU_BW]
            outs[0][:, sl] = dx + lax.dot_general(dpre, w, _NT, preferred_element_type=F32)

    gate_shape = (2 * D_MODEL, LRU_BW)
    return _rowwise(fn, [rec, du_f, da_f, du_b, da_b], [wcat, gvec], [(D_MODEL, F32)],
                    [gate_shape, gate_shape, (SUBLANES, D_MODEL)], tm=256, name="rg_gate_bwd")


def _as_time_blocks(x):
    return x.reshape(x.shape[0] // SUBLANES, SUBLANES, x.shape[1])


def _scan_call(body, ins, n_out, B, L, tc, name):
    nb = L // SUBLANES
    spec = pl.BlockSpec((nb, SUBLANES, tc), lambda b, j: (b, 0, j))
    T = ins[0].shape[0]
    outs = pl.pallas_call(
        functools.partial(body, nb), name=name, grid=(B, D_MODEL // tc),
        in_specs=[spec] * len(ins), out_specs=[spec] * n_out,
        out_shape=[jax.ShapeDtypeStruct((T // SUBLANES, SUBLANES, D_MODEL), F32)] * n_out,
        compiler_params=_params(("parallel", "parallel")),
    )(*[_as_time_blocks(x) for x in ins])
    return [o.reshape(T, D_MODEL) for o in outs]


def _block_scan(A, U, reverse):
    row = lax.broadcasted_iota(jnp.int32, A.shape, 0)
    for s in (1, 2, 4):
        shift = SUBLANES - s if reverse else s
        valid = (row < SUBLANES - s) if reverse else (row >= s)
        a_sh = jnp.where(valid, pltpu.roll(A, shift, 0), 1.0)
        u_sh = jnp.where(valid, pltpu.roll(U, shift, 0), 0.0)
        U = A * u_sh + U
        A = A * a_sh
    return A, U


_LAST = SUBLANES - 1
SCAN_UNROLL = 8


def _loop_blocks(nb, step, init):
    def group(g, carry):
        for k in range(SCAN_UNROLL):
            carry = step(g * SCAN_UNROLL + k, carry)
        return carry

    return lax.fori_loop(0, nb // SCAN_UNROLL, group, init)


def _scan_fwd(a_f, u_f, a_b, u_b, B, L, tc=256):
    def body(nb, af, uf, ab, ub, hf, hb):
        def step(i, carry):
            c1, c2 = carry
            ib = nb - 1 - i
            p, h = _block_scan(af[i], uf[i], False)
            h = h + p * c1
            hf[i] = h
            p2, h2 = _block_scan(ab[ib], ub[ib], True)
            h2 = h2 + p2 * c2
            hb[ib] = h2
            return h[_LAST:, :], h2[:1, :]

        zero = jnp.zeros((1, tc), F32)
        _loop_blocks(nb, step, (zero, zero))

    return _scan_call(body, [a_f, u_f, a_b, u_b], 2, B, L, tc, "rg_scan")


def _scan_bwd(dy, a_f, h_f, a_b, h_b, B, L, tc=256):
    def body(nb, dy_r, af, hf, ab, hb, duf, daf, dub, dab):
        def step(i, carry):
            c1, c2 = carry
            ir = nb - 1 - i
            row = lax.broadcasted_iota(jnp.int32, (SUBLANES, tc), 0)
            a_up = jnp.where(row == _LAST, af[jnp.minimum(ir + 1, nb - 1), :1, :], pltpu.roll(af[ir], _LAST, 0))
            p, lam = _block_scan(a_up, dy_r[ir], True)
            lam = lam + p * c1
            before = hf[jnp.maximum(ir - 1, 0), _LAST:, :] * (ir > 0).astype(F32)
            duf[ir] = lam
            daf[ir] = lam * jnp.where(row == 0, before, pltpu.roll(hf[ir], 1, 0))
            a_dn = jnp.where(row == 0, ab[jnp.maximum(i - 1, 0), _LAST:, :], pltpu.roll(ab[i], 1, 0))
            p2, lam2 = _block_scan(a_dn, dy_r[i], False)
            lam2 = lam2 + p2 * c2
            after = hb[jnp.minimum(i + 1, nb - 1), :1, :] * (i < nb - 1).astype(F32)
            dub[i] = lam2
            dab[i] = lam2 * jnp.where(row == _LAST, after, pltpu.roll(hb[i], _LAST, 0))
            return lam[:1, :], lam2[_LAST:, :]

        zero = jnp.zeros((1, tc), F32)
        _loop_blocks(nb, step, (zero, zero))

    return _scan_call(body, [dy, a_f, h_f, a_b, h_b], 4, B, L, tc, "rg_scan_bwd")


_GELU_C = math.sqrt(2.0 / math.pi)


def _gelu_parts(x):
    th = jnp.tanh(_GELU_C * (x + 0.044715 * x * x * x))
    return 0.5 * x * (1.0 + th), th


def _gated_out(h_f, h_b, z):
    def fn(ins, bs, outs, accs):
        gl, _ = _gelu_parts(ins[2][...])
        outs[0][...] = ((ins[0][...] + ins[1][...]) * gl).astype(BF16)

    return _rowwise(fn, [h_f, h_b, (z, D_MODEL, 0)], [], [(D_MODEL, BF16)], tm=512, name="rg_gated_out")[0]


def _gated_out_bwd(dyg, h_f, h_b, z):
    def fn(ins, bs, outs, accs):
        x = ins[3][...]
        gl, th = _gelu_parts(x)
        dgl = 0.5 * (1.0 + th) + 0.5 * x * (1.0 - th * th) * (_GELU_C * (1.0 + 3.0 * 0.044715 * x * x))
        g = ins[0][...]
        outs[0][...] = g * gl
        outs[1][...] = (g * (ins[1][...] + ins[2][...]) * dgl).astype(BF16)

    return _rowwise(fn, [dyg, h_f, h_b, (z, D_MODEL, 0)], [], [(D_MODEL, F32), (D_MODEL, BF16, 2 * D_MODEL)], tm=512,
                    name="rg_gated_out_bwd")


def _make_wcat(w_a, w_x):
    g = jnp.stack([w_a[0, 0], w_x[0, 0], w_a[0, 1], w_x[0, 1]])
    return jnp.transpose(g, (1, 2, 0, 3)).reshape(D_MODEL, 4 * LRU_BW)


def _rows_at(part, first):
    return jnp.pad(part, ((first, SUBLANES - first - part.shape[0]), (0, 0)))


def _qk_slot(q_g, k_g):
    wide = lambda v, at: jnp.pad(v, ((0, SUBLANES - 1), (at, D_MODEL - at - HEAD_DIM)))
    return wide(q_g, 0) + wide(k_g, HEAD_DIM)


def _local_step(x, target, P, fetch, emit, B, L, after=None):
    g_mix, g_mlp = P["norm_mix_g"], P["norm_mlp_g"]
    h0 = _rms_fwd(x, g_mix[0:1], "rg_norm", after=after)
    w_in, w_out, conv_wb, wcat, gvec = fetch("rg", h0)
    z = _mm(h0, w_in, mode="nn", b_shard=True, name="rg_in")
    rec = _conv_fwd(z, conv_wb, B, L)
    a_f, u_f, a_b, u_b = _gate_fwd(rec, wcat, gvec)
    h_f, h_b = _scan_fwd(a_f, u_f, a_b, u_b, B, L)
    yg = _gated_out(h_f, h_b, z)
    x1, h1 = _mm_res_norm(yg, w_out, x, g_mlp[0:1], "rg_out")
    (x2, h3), mlp0 = _mlp_fwd(x1, h1, fetch, 0, lambda a, w, res, name: _mm_res_norm(a, w, res, g_mix[1:2], name))
    w_qkv, w_o = fetch("att", h3)
    qkv = _mm(h3, w_qkv, mode="nn", b_shard=True, name="attn_qkv")
    cos, sin = _rope_tables(L, B)
    qh, kh, vh = _qk_prep(qkv, cos, sin, P["q_g"], P["k_g"])
    o = _attn_fwd(qh, kh, vh, B, L)
    x3, h4 = _mm_res_norm(o, w_o, x2, g_mlp[1:2], "attn_out")
    (dx4, dx4_bf, loss_acc, d_final_g), mlp1 = _mlp_fwd(
        x3, h4, fetch, 1, lambda a, w, res, name: _mm_final_loss(a, w, res, target, P["final_g"], name))

    dx3, dx3_bf, dg_mlp1, d_up1, d_down1 = _mlp_bwd(x3, g_mlp[1:2], mlp1, dx4, dx4_bf, 1, None)
    tok = emit("mlp1", [d_up1, d_down1])
    d_wo = _mm(o, dx3_bf, mode="tn", out_dtypes=(BF16,), name="attn_dwo", after=tok)
    do = _mm(dx3_bf, w_o, mode="nt", out_dtypes=(BF16,), name="attn_do")
    dq, dk, dv = _attn_bwd(qh, kh, vh, do, B, L)
    dqkv, dq_g, dk_g = _qk_prep_bwd(qkv, dq, dk, dv, cos, sin, P["q_g"], P["k_g"])
    d_wqkv = _mm(h3, dqkv, mode="tn", o_shard=True, out_dtypes=(BF16,), name="attn_dwqkv")
    tok = emit("att", [d_wqkv, d_wo])
    dx2, dx2_bf, dg_mix1 = _mm_norm_bwd(dqkv, w_qkv, x2, dx3, g_mix[1:2], "attn_dh", after=tok)
    tok = emit("point_attn_done", [dx2_bf])
    dx1, dx1_bf, dg_mlp0, d_up0, d_down0 = _mlp_bwd(x1, g_mlp[0:1], mlp0, dx2, dx2_bf, 0, tok)
    tok = emit("mlp0", [d_up0, d_down0])
    d_wout = _mm(yg, dx1_bf, mode="tn", out_dtypes=(BF16,), name="rg_dwout", after=tok)
    tok = emit("rg_out", [d_wout])
    dyg = _mm(dx1_bf, w_out, mode="nt", name="rg_dyg", after=tok)
    dy, dgate = _gated_out_bwd(dyg, h_f, h_b, z)
    du_f, da_f, du_b, da_b = _scan_bwd(dy, a_f, h_f, a_b, h_b, B, L)
    drec_c, d_wa, d_wx, d_gvec = _gate_bwd(rec, du_f, da_f, du_b, da_b, wcat, gvec)
    tok = emit("gates", [d_wa, d_wx])
    dz, d_convwb = _conv_bwd(z, drec_c, conv_wb, dgate, B, L, after=tok)
    tok = emit("point_conv_done", [dz])
    d_win = _mm(h0, dz, mode="tn", o_shard=True, out_dtypes=(BF16,), name="rg_dwin", after=tok)
    tok = emit("rg_in", [d_win])
    grad_x, _, dg_mix0 = _mm_norm_bwd(dz, w_in, x, dx1, g_mix[0:1], "rg_dh", after=tok)

    norms = (_rows_at(dg_mix0, 0) + _rows_at(dg_mix1, 1) + _rows_at(dg_mlp0, 2) + _rows_at(dg_mlp1, 3)
             + _rows_at(d_final_g, 4)
             + jnp.pad(loss_acc, ((LOSS_ROW, SUBLANES - 1 - LOSS_ROW), (0, D_MODEL - LANES))))
    vec = jnp.concatenate([norms, d_convwb, d_gvec, _qk_slot(dq_g, dk_g)], axis=0)
    return grad_x, vec


_MESH = pl.DeviceIdType.MESH


def _place():
    x, y, c = lax.axis_index("x"), lax.axis_index("y"), lax.axis_index("c")
    peers = [((1 - x) if j & 2 else x, (1 - y) if j & 1 else y) for j in (1, 2, 3)]
    return x, y, c, peers


def _comm_call(body, ins, out_shapes, n_sem, name):
    return pl.pallas_call(
        body, name=name, in_specs=[_ANY] * len(ins), out_specs=[_ANY] * len(out_shapes), out_shape=out_shapes,
        scratch_shapes=[pltpu.SemaphoreType.DMA((n_sem,)), pltpu.SemaphoreType.DMA((n_sem,)),
                        pltpu.SemaphoreType.DMA((len(ins),))],
    )(*ins)


def _all_devices_slots(v, name):
    def body(v_ref, out_ref, send, recv, lsem):
        x, y, c = lax.axis_index("x"), lax.axis_index("y"), lax.axis_index("c")
        me = 4 * x + 2 * y + c

        def peer(j):
            return (1 - x) if j & 4 else x, (1 - y) if j & 2 else y, (1 - c) if j & 1 else c

        def copy(j, slot):
            return pltpu.make_async_remote_copy(
                src_ref=v_ref, dst_ref=out_ref.at[slot], send_sem=send.at[j - 1], recv_sem=recv.at[j - 1],
                device_id=peer(j), device_id_type=_MESH)

        local = pltpu.make_async_copy(v_ref, out_ref.at[me], lsem.at[0])
        sends = [copy(j, me) for j in range(1, N_DEVICES)]
        for cp in [local] + sends:
            cp.start()
        for j in range(1, N_DEVICES):
            px, py, pc = peer(j)
            copy(j, 4 * px + 2 * py + pc).wait_recv()
        for cp in sends:
            cp.wait_send()
        local.wait()

    shape = jax.ShapeDtypeStruct((N_DEVICES,) + v.shape, v.dtype)
    return _comm_call(body, [v], [shape], N_DEVICES - 1, name)[0]


def _sum_leading(slots, name):
    def body(s_ref, o_ref):
        acc = s_ref[0]
        for d in range(1, slots.shape[0]):
            acc = acc + s_ref[d]
        o_ref[...] = acc

    return pl.pallas_call(body, name=name, out_shape=jax.ShapeDtypeStruct(slots.shape[1:], slots.dtype))(slots)


_HBM = pl.BlockSpec(memory_space=pltpu.HBM)
_SEM = pl.BlockSpec(memory_space=pltpu.SEMAPHORE)
_EFFECT = pltpu.SideEffectType.DATAFLOW_SIDE_EFFECTING


_COPIES = dict(gather=N_CHIPS - 1, scatter=N_CHIPS - 1, swap=1)


def _split_copies(kind, srcs, lands, send, recv):
    x, y, c, peers = _place()
    me = 2 * x + y
    per = _COPIES[kind]
    out = []
    for a in range(len(lands)):
        for j in range(per):
            if kind == "swap":
                src, there, here, dev = srcs[a], lands[a], lands[a], (x, y, 1 - c)
            else:
                px, py = peers[j]
                dev = (px, py, c)
                if kind == "gather":
                    src, there, here = lands[a].at[me], lands[a].at[me], lands[a].at[2 * px + py]
                else:
                    src, there, here = srcs[a].at[2 * px + py], lands[a].at[j], lands[a].at[j]
            mk = functools.partial(
                pltpu.make_async_remote_copy, src_ref=src, send_sem=send.at[per * a + j],
                recv_sem=recv.at[per * a + j], device_id=dev, device_id_type=_MESH)
            out.append((functools.partial(mk, dst_ref=there), functools.partial(mk, dst_ref=here)))
    return out


def _exchange_start(kind, srcs, lands, name, after=None):
    arrays = list(srcs) + list(lands)
    n_s, n, n_all = len(srcs), len(lands), len(srcs) + len(lands)
    n_sem = _COPIES[kind] * n
    order = _after_operand(after)
    n_x = len(order)

    def body(*refs):
        send, recv = refs[n_all + n_x], refs[n_all + n_x + 1]
        token = refs[-1]
        for started, _ in _split_copies(kind, refs[:n_s], refs[n_s:n_all], send, recv):
            started().start()
        token[...] = jnp.zeros(token.shape, F32)

    res = pl.pallas_call(
        body, name=name,
        out_shape=(pltpu.SemaphoreType.DMA((n_sem,)), pltpu.SemaphoreType.DMA((n_sem,)),
                   *[pltpu.HBM(a.shape, a.dtype) for a in arrays], jax.ShapeDtypeStruct((SUBLANES, LANES), F32)),
        in_specs=[_HBM] * n_all + [_ANY] * n_x,
        out_specs=(_SEM, _SEM, *[_HBM] * n_all, pl.BlockSpec(memory_space=pltpu.VMEM)),
        input_output_aliases={i: 2 + i for i in range(n_all)},
        compiler_params=pltpu.CompilerParams(has_side_effects=_EFFECT),
    )(*[pltpu.with_memory_space_constraint(a, pltpu.HBM) for a in arrays], *order)
    return (res[0], res[1], res[2:2 + n_s], res[2 + n_s:2 + n_all]), res[-1]


def _exchange_wait(kind, handle, after, name):
    send, recv, srcs, lands = handle
    arrays = list(srcs) + list(lands)
    n_s, n_all = len(srcs), len(arrays)
    order = list(after) if isinstance(after, (list, tuple)) else [after]

    def body(*refs):
        for started, landing in _split_copies(kind, refs[:n_s], refs[n_s:n_all], refs[n_all], refs[n_all + 1]):
            started().wait_send()
            landing().wait_recv()

    res = pl.pallas_call(
        body, name=name, out_shape=[pltpu.HBM(a.shape, a.dtype) for a in arrays],
        in_specs=[_HBM] * n_all + [_SEM, _SEM] + [_ANY] * len(order), out_specs=[_HBM] * n_all,
        input_output_aliases={i: i for i in range(n_all)},
        compiler_params=pltpu.CompilerParams(has_side_effects=_EFFECT),
    )(*arrays, send, recv, *order)
    return res[:n_s], res[n_s:]


def _index_operand(i):
    return jnp.reshape(i, (1,)).astype(jnp.int32)


def _cast_into_slot(src, row0, rows, me, dtype, name, after=None, add=None):
    cols = src.shape[1]
    tm = min(512, rows)
    order = _after_operand(after)
    terms = [src] + ([] if add is None else [add])

    def body(me_ref, *rest):
        val = rest[0][...]
        if add is not None:
            val = val + rest[1][...]
        rest[-1][...] = val.astype(dtype)

    return pl.pallas_call(
        body, name=name,
        grid_spec=pltpu.PrefetchScalarGridSpec(
            num_scalar_prefetch=1, grid=(rows // tm,),
            in_specs=[pl.BlockSpec((tm, cols), lambda i, me_ref: (i + row0 // tm, 0))] * len(terms)
            + [_ANY] * len(order),
            out_specs=pl.BlockSpec((None, tm, cols), lambda i, me_ref: (me_ref[0], i, 0))),
        out_shape=jax.ShapeDtypeStruct((N_CHIPS, rows, cols), dtype), compiler_params=_params(("parallel",)),
    )(_index_operand(me), *terms, *order)


def _sum_slots(mine, r, me, name):
    _, rows, cols = r.shape
    tm = min(512, rows)

    def body(me_ref, own_ref, r_ref, o_ref):
        o_ref[...] = ((own_ref[...].astype(F32) + r_ref[0].astype(F32)) + r_ref[1].astype(F32)) + r_ref[2].astype(F32)

    return pl.pallas_call(
        body, name=name,
        grid_spec=pltpu.PrefetchScalarGridSpec(
            num_scalar_prefetch=1, grid=(rows // tm,),
            in_specs=[pl.BlockSpec((None, tm, cols), lambda i, me_ref: (me_ref[0], i, 0)),
                      pl.BlockSpec((N_CHIPS - 1, tm, cols), lambda i, me_ref: (0, i, 0))],
            out_specs=pl.BlockSpec((tm, cols), lambda i, me_ref: (i, 0))),
        out_shape=jax.ShapeDtypeStruct((rows, cols), F32), compiler_params=_params(("parallel",)),
    )(_index_operand(me), mine, r)


def _adamw(w, m, v, ps, qs, name):
    rows, cols = w.shape
    seg_rows = ps[0].shape[0]
    tm = min(256, seg_rows)
    while seg_rows % tm:
        tm -= SUBLANES
    per, n_seg = seg_rows // tm, len(ps)
    parts = list(ps) + ([] if qs is None else list(qs))

    def body(w_ref, m_ref, v_ref, *rest):
        g_refs, outs = rest[:len(parts)], rest[len(parts):]
        grad = lambda s: g_refs[s][...] if qs is None else g_refs[s][...] + g_refs[n_seg + s][...]
        g = grad(0)
        for s in range(1, n_seg):
            g = jnp.where(pl.program_id(0) >= s * per, grad(s), g)
        m1 = ADAM_B1 * m_ref[...] + (1.0 - ADAM_B1) * g
        v1 = ADAM_B2 * v_ref[...] + (1.0 - ADAM_B2) * (g * g)
        m_hat = m1 / (1.0 - ADAM_B1 ** ADAM_STEP)
        v_hat = v1 / (1.0 - ADAM_B2 ** ADAM_STEP)
        outs[0][...] = g
        outs[1][...] = (-ADAM_LR) * (m_hat / (jnp.sqrt(v_hat) + ADAM_EPS) + ADAM_WD * w_ref[...])
        outs[2][...] = m1
        outs[3][...] = v1

    row_spec = pl.BlockSpec((tm, cols), lambda i: (i, 0))
    seg_spec = lambda s: pl.BlockSpec((tm, cols), lambda i: (jnp.clip(i - s * per, 0, per - 1), 0))
    return pl.pallas_call(
        body, name=name, grid=(rows // tm,),
        in_specs=[row_spec] * 3 + [seg_spec(s) for s in range(n_seg)] * (1 if qs is None else 2),
        out_specs=[row_spec] * 4, out_shape=[jax.ShapeDtypeStruct((rows, cols), F32)] * 4,
        compiler_params=_params(("arbitrary",)),
    )(w, m, v, *parts)


def _put_cols(shard, me):
    full = jnp.zeros((shard.shape[0], D_MODEL), F32)
    return lax.dynamic_update_slice(full, shard, (0, me * (D_MODEL // N_CHIPS)))


def _gate_vec_slot(b_a, b_x, lam):
    return _rows_at(b_a, _ROW_BA) + _rows_at(b_x, _ROW_BX) + _rows_at(lam, _ROW_LAM)


def _pack_vec(p, me):
    return jnp.concatenate([
        _rows_at(p["norm_mix_g"], 0) + _rows_at(p["norm_mlp_g"], 2) + _rows_at(p["final_g"][None], 4),
        _rows_at(_put_cols(p["rg_conv_w"][0, :, 0, :], me), 0) + _rows_at(p["rg_conv_b"], 4),
        _gate_vec_slot(_put_cols(p["rg_b_a"][0], me), _put_cols(p["rg_b_x"][0], me), _put_cols(p["rg_lam"][0], me)),
        _qk_slot(p["at_q_g"], p["at_k_g"]),
    ], axis=0)


def _unpack_vec(r, me):
    def cols(rows):
        return lax.dynamic_slice(rows, (0, me * (D_MODEL // N_CHIPS)), (rows.shape[0], D_MODEL // N_CHIPS))

    gate = r[16:24]
    return dict(
        norm_mix_g=r[0:2], norm_mlp_g=r[2:4], final_g=r[4], rg_conv_w=cols(r[8:12])[None, :, None, :],
        rg_conv_b=r[12:13], rg_b_a=cols(gate[_ROW_BA:_ROW_BA + 2])[None], rg_b_x=cols(gate[_ROW_BX:_ROW_BX + 2])[None],
        rg_lam=cols(gate[_ROW_LAM:_ROW_LAM + 2])[None], at_q_g=r[24:25, 0:HEAD_DIM],
        at_k_g=r[24:25, HEAD_DIM:2 * HEAD_DIM])


_WEIGHTS = ['norm_mix_g', 'norm_mlp_g', 'rg_w_in', 'rg_conv_w', 'rg_conv_b', 'rg_w_a', 'rg_b_a', 'rg_w_x', 'rg_b_x',
            'rg_lam', 'rg_w_out', 'at_w_qkv', 'at_q_g', 'at_k_g', 'at_w_o', 'mlp_w_up', 'mlp_w_down', 'final_g']
_BIG = dict(rg_w_in=["rg_w_in"], rg_w_out=["rg_w_out"], at_w_qkv=["at_w_qkv"], at_w_o=["at_w_o"],
            mlp_w_up=["up0", "up1"], mlp_w_down=["down0", "down1"])


def kernel(x, *args):
    n_w = len(_WEIGHTS)
    w = dict(zip(_WEIGHTS, args[:n_w]))
    target = args[n_w]
    m = dict(zip(_WEIGHTS, args[n_w + 1:2 * n_w + 1]))
    v = dict(zip(_WEIGHTS, args[2 * n_w + 1:3 * n_w + 1]))
    B, L, _ = x.shape
    T = B * L
    me = 2 * lax.axis_index("x") + lax.axis_index("y")

    vec = jnp.concatenate([_gate_vec_slot(w["rg_b_a"][0], w["rg_b_x"][0], w["rg_lam"][0]),
                           _rows_at(w["rg_conv_w"][0, :, 0, :], 0)], axis=0)
    flat = lambda a: a.reshape(-1, a.shape[-1])
    rows_of = lambda k: w[k].shape[-2]
    groups = [("rg", [("rg_w_in", 0, BF16), ("rg_w_out", 0, BF16), (vec, 0, F32)]),
              ("mlp0", [("mlp_w_up", 0, BF16), ("mlp_w_down", 0, BF16)]),
              ("att", [("at_w_qkv", 0, BF16), ("at_w_o", 0, BF16)]),
              ("mlp1", [("mlp_w_up", 1, BF16), ("mlp_w_down", 1, BF16)])]
    gathers, tok = {}, None
    for group, members in groups:
        lands = []
        for n, (k, layer, dtype) in enumerate(members):
            src, rows = (flat(w[k]), rows_of(k)) if isinstance(k, str) else (k, k.shape[0])
            lands.append(_cast_into_slot(src, layer * rows, rows, me, dtype, f"place_{group}{n}", after=tok))
        gathers[group], tok = _exchange_start("gather", [], lands, f"gather_{group}_start", after=tok)
    wcat = _make_wcat(w["rg_w_a"], w["rg_w_x"]).astype(BF16)

    packs = [_pack_vec(p, me) for p in (w, m, v)]

    def fetch(group, after):
        order = [after, wcat] + packs if group == "rg" else after
        _, full = _exchange_wait("gather", gathers[group], order, f"gather_{group}_wait")
        if group == "rg":
            vec_full = jnp.transpose(full[2], (1, 0, 2)).reshape(2 * SUBLANES, D_MODEL)
            conv_wb = vec_full[SUBLANES:] + _rows_at(w["rg_conv_b"], 4)
            return full[0], full[1].reshape(D_MODEL, D_MODEL), conv_wb, wcat, vec_full[:SUBLANES]
        if group == "att":
            return full[0], full[1].reshape(D_MODEL, D_MODEL)
        return full[0], full[1].reshape(4 * D_MODEL, D_MODEL)

    names = dict(mlp1=["up1", "down1"], att=["at_w_qkv", "at_w_o"], mlp0=["up0", "down0"], rg_out=["rg_w_out"],
                 rg_in=["rg_w_in"], gates=["rg_w_a", "rg_w_x"])
    scatters, swaps, P, Q, res = {}, [], {}, {}, {}

    def start_scatter(group, grads):
        srcs = [g.reshape(N_CHIPS, -1, g.shape[-1]) for g in grads]
        lands = [lax.empty((N_CHIPS - 1,) + s.shape[1:], s.dtype) for s in srcs]
        scatters[group], token = _exchange_start("scatter", srcs, lands, f"scatter_{group}_start")
        return token

    def settle(groups, after):
        keys, parts = [], []
        for group in groups:
            srcs, lands = _exchange_wait("scatter", scatters[group], after, f"scatter_{group}_wait")
            for k, s, r in zip(names[group], srcs, lands):
                keys.append(k)
                parts.append(_sum_slots(s, r, me, f"sum_{k}"))
        handle, token = _exchange_start("swap", parts, [lax.empty(p.shape, F32) for p in parts],
                                        f"swap_{groups[0]}_start")
        swaps.append((keys, handle, f"swap_{groups[0]}_wait"))
        return token

    def finish(after):
        for keys, handle, name in swaps:
            mine, theirs = _exchange_wait("swap", handle, after, name)
            P.update(zip(keys, mine))
            Q.update(zip(keys, theirs))
        swaps.clear()
        last = after
        for k, parts in _BIG.items():
            if k in res or any(p not in P for p in parts):
                continue
            shape = w[k].shape
            two_d = lambda a: a.reshape(-1, shape[-1])
            outs = _adamw(two_d(w[k]), two_d(m[k]), two_d(v[k]), [P[p] for p in parts], [Q[p] for p in parts],
                          f"adamw_{k}")
            res[k] = [o.reshape(shape) for o in outs]
            last = outs[0]
        if "rg_w_a" in P and "gates" not in gathers:
            lands = [_cast_into_slot(P[k], 0, P[k].shape[0], me, F32, f"place_{k}", after=last, add=Q[k])
                     for k in names["gates"]]
            gathers["gates"], last = _exchange_start("gather", [], lands, "gather_gates_start", after=last)
        return last

    def emit(event, arrays):
        if event == "point_attn_done":
            return settle(["mlp1"], arrays[0])
        if event == "point_conv_done":
            return settle(["att", "mlp0", "rg_out", "gates"], arrays[0])
        token = start_scatter(event, arrays)
        return finish(token) if event == "rg_in" else token

    P_vec = dict(norm_mix_g=w["norm_mix_g"], norm_mlp_g=w["norm_mlp_g"], final_g=w["final_g"][None],
                 q_g=w["at_q_g"], k_g=w["at_k_g"])
    grad_x, vec_part = _local_step(x.reshape(T, D_MODEL), target.reshape(T, D_MODEL), P_vec, fetch, emit, B, L,
                                   after=tok)

    finish(settle(["rg_in"], grad_x))
    _, gate_grads = _exchange_wait("gather", gathers["gates"], grad_x, "gather_gates_wait")
    for k, g in zip(names["gates"], gate_grads):
        two_d = lambda a: a.reshape(g.shape[0] * g.shape[1], g.shape[2])
        outs = _adamw(two_d(w[k]), two_d(m[k]), two_d(v[k]), [two_d(g)], None, f"adamw_{k}")
        res[k] = [o.reshape(w[k].shape) for o in outs]
    vec_grad = _sum_leading(_all_devices_slots(vec_part, "allreduce_vec"), "sum_vec")
    loss = vec_grad[LOSS_ROW, 0]
    outs = _adamw(*packs, [vec_grad], None, "adamw_vec")
    unpacked = [_unpack_vec(o, me) for o in outs]
    for k in _WEIGHTS:
        if k not in res:
            res[k] = [u[k] for u in unpacked]

    result = [loss, grad_x.reshape(B, L, D_MODEL)]
    for slot in range(4):
        result += [res[k][slot] for k in _WEIGHTS]
    return tuple(result)
```

```python
import functools
import math

import jax
import jax.numpy as jnp
import numpy as np
from jax import lax
from jax.experimental import pallas as pl
from jax.experimental.pallas import tpu as pltpu

F32 = jnp.float32
BF16 = jnp.bfloat16

D_MODEL = 1024
HEAD_DIM = 128
N_HEADS = 8
N_KV = 2
GROUP = N_HEADS // N_KV
LRU_BLOCKS = 8
LRU_BW = 128
GRID_W = 64
ROPE_THETA = 10000.0
EPS = 1e-6
RG_C = 8.0
SCALE = 1.0 / math.sqrt(HEAD_DIM)
N_CHIPS = 4

ADAM_LR = 0.001
ADAM_B1 = 0.9
ADAM_B2 = 0.999
ADAM_EPS = 1e-08
ADAM_WD = 0.01
ADAM_STEP = 10

V7X_VMEM_BYTES = 64 * 1024 * 1024
VMEM_LIMIT = V7X_VMEM_BYTES * 3 // 4
LANES = 128
SUBLANES = 8

N_DEVICES = 8
VEC_ROWS = 32
LOSS_ROW = 5


def _params(sem):
    return pltpu.CompilerParams(dimension_semantics=sem, vmem_limit_bytes=VMEM_LIMIT)


_ANY = pl.BlockSpec(memory_space=pl.ANY)
_NN = (((1,), (0,)), ((), ()))
_NT = (((1,), (1,)), ((), ()))
_TN = (((0,), (0,)), ((), ()))


def _after_operand(after):
    return [] if after is None else [after]


def _fit(t, n):
    if n <= t:
        return n
    c = (t // LANES) * LANES
    while n % c:
        c -= LANES
    return c


MM_VMEM_BUDGET = VMEM_LIMIT * 3 // 4
def _mm_tiles(M, K, ns, out_dtypes, extras, whole_rows):
    for tm in (2048, 1024, 512, 256, 128):
        for tn in ((ns,) if whole_rows else (1024, 512, 256)):
            tn = _fit(tn, ns)
            per_row = 2 * (2 * K) + 4 * tn + sum(2 * tn * jnp.dtype(d).itemsize for d in out_dtypes)
            per_row += sum(2 * tn * e.dtype.itemsize for e in extras)
            if M % tm == 0 and 2 * (2 * K * tn) + tm * per_row <= MM_VMEM_BUDGET:
                return tm, tn
    raise ValueError(f"no tile fits VMEM for M={M} K={K} N={ns}")


def _mm(a, b, *, mode, name, out_dtypes=(F32,), b_shard=False, o_shard=False, extras=(), epi=None, after=None,
        bcast=(), accs=(), ref_epi=None):
    if mode == "tn":
        K, M = a.shape
        N = b.shape[1]
    else:
        M, K = a.shape
        if mode == "nn":
            N = b.shape[0] * b.shape[2] if b_shard else b.shape[1]
        else:
            N = b.shape[1] if b_shard else b.shape[0]
    ns = N
    if b_shard and mode == "nn":
        ns = b.shape[2]
    elif o_shard:
        ns = N // N_CHIPS
    tm, tn = _mm_tiles(M, K, ns, out_dtypes, extras, whole_rows=ref_epi is not None)
    if ref_epi is not None:
        tm = min(tm, 512)
    grid = (M // tm, N // tn)
    q = ns // tn

    if mode == "tn":
        a_spec = pl.BlockSpec((K, tm), lambda i, j: (0, i))
        b_spec = pl.BlockSpec((K, tn), lambda i, j: (0, j))
        dims = _TN
    elif mode == "nn":
        a_spec = pl.BlockSpec((tm, K), lambda i, j: (i, 0))
        if b_shard:
            b_spec = pl.BlockSpec((None, K, tn), lambda i, j: (j // q, 0, j % q))
        else:
            b_spec = pl.BlockSpec((K, tn), lambda i, j: (0, j))
        dims = _NN
    else:
        a_spec = pl.BlockSpec((tm, K), lambda i, j: (i, 0))
        if b_shard:
            ks = b.shape[2]
            b_spec = pl.BlockSpec((N_CHIPS, tn, ks), lambda i, j: (0, j, 0))
        else:
            b_spec = pl.BlockSpec((tn, K), lambda i, j: (j, 0))
        dims = _NT

    if o_shard:
        o_specs = [pl.BlockSpec((None, tm, tn), lambda i, j: (j // q, i, j % q))]
        o_shapes = [jax.ShapeDtypeStruct((N_CHIPS, M, ns), out_dtypes[0])]
    else:
        o_specs = [pl.BlockSpec((tm, tn), lambda i, j: (i, j)) for _ in out_dtypes]
        o_shapes = [jax.ShapeDtypeStruct((M, N), dt) for dt in out_dtypes]
    e_specs = [pl.BlockSpec((tm, tn), lambda i, j: (i, j)) for _ in extras]
    e_specs += [pl.BlockSpec(v.shape, lambda i, j: (0, 0)) for v in bcast]
    o_specs += [pl.BlockSpec(s, lambda i, j: (0, 0)) for s in accs]
    o_shapes += [jax.ShapeDtypeStruct(s, F32) for s in accs]
    n_e, n_b, n_o, n_a = len(extras), len(bcast), len(out_dtypes), len(accs)
    order = _after_operand(after)
    n_x = len(order)
    if epi is None:
        epi = lambda acc: (acc,)

    def body(a_ref, b_ref, *rest):
        e_refs, b_refs = rest[:n_e], rest[n_e:n_e + n_b]
        o_refs = rest[n_e + n_b + n_x:n_e + n_b + n_x + n_o]
        a_refs = rest[n_e + n_b + n_x + n_o:]
        if n_a:
            @pl.when((pl.program_id(0) == 0) & (pl.program_id(1) == 0))
            def _():
                for r in a_refs:
                    r[...] = jnp.zeros(r.shape, F32)
        if mode == "nt" and b_shard:
            acc = None
            for s in range(N_CHIPS):
                part = lax.dot_general(a_ref[:, s * ks:(s + 1) * ks], b_ref[s], dims, preferred_element_type=F32)
                acc = part if acc is None else acc + part
        else:
            acc = lax.dot_general(a_ref[...], b_ref[...], dims, preferred_element_type=F32)
        if ref_epi is not None:
            ref_epi(acc, e_refs, b_refs, o_refs, a_refs)
            return
        outs = epi(acc, *[r[...] for r in e_refs])
        for r, o in zip(o_refs, outs):
            r[...] = o.astype(r.dtype)

    outs = pl.pallas_call(
        body, name=name, grid=grid, in_specs=[a_spec, b_spec] + e_specs + [_ANY] * n_x, out_specs=o_specs,
        out_shape=o_shapes, compiler_params=_params(("arbitrary", "arbitrary") if n_a else ("parallel", "parallel")),
    )(a, b, *extras, *bcast, *order)
    return outs[0] if n_o + n_a == 1 else outs


def _rowwise(fn, rows, bcast, outs, accs=(), *, tm, name, after=None):
    def norm(r):
        return r if isinstance(r, tuple) else (r, r.shape[1], 0)

    rows = [norm(r) for r in rows]
    T = rows[0][0].shape[0]
    tm = min(tm, T)
    while T % tm:
        tm -= SUBLANES
    n_r, n_b, n_o, n_a = len(rows), len(bcast), len(outs), len(accs)
    order = _after_operand(after)
    n_x = len(order)
    in_specs = [pl.BlockSpec((tm, c), functools.partial(lambda i, cb: (i, cb), cb=cb)) for _, c, cb in rows]
    in_specs += [pl.BlockSpec(b.shape, lambda i: (0, 0)) for b in bcast] + [_ANY] * n_x
    out_specs = [pl.BlockSpec((tm, o[0]), lambda i: (i, 0)) for o in outs]
    out_specs += [pl.BlockSpec(s, lambda i: (0, 0)) for s in accs]
    out_shape = [jax.ShapeDtypeStruct((T, o[2] if len(o) > 2 else o[0]), o[1]) for o in outs]
    out_shape += [jax.ShapeDtypeStruct(s, F32) for s in accs]

    def body(*refs):
        in_refs = refs[:n_r]
        b_refs = refs[n_r:n_r + n_b]
        o_refs = refs[n_r + n_b + n_x:n_r + n_b + n_x + n_o]
        a_refs = refs[n_r + n_b + n_x + n_o:]
        if n_a:
            @pl.when(pl.program_id(0) == 0)
            def _():
                for r in a_refs:
                    r[...] = jnp.zeros(r.shape, F32)
        fn(in_refs, b_refs, o_refs, a_refs)

    res = pl.pallas_call(
        body, name=name, grid=(T // tm,), in_specs=in_specs, out_specs=out_specs, out_shape=out_shape,
        compiler_params=_params(("arbitrary",) if n_a else ("parallel",)),
    )(*[r[0] for r in rows], *bcast, *order)
    return res


def _rsum(x):
    return jnp.sum(x, axis=0, keepdims=True)


def _rms_fwd(x, g, name, after=None):
    def fn(ins, bs, outs, accs):
        xv = ins[0][...]
        r = lax.rsqrt(jnp.mean(xv * xv, axis=-1, keepdims=True) + EPS)
        outs[0][...] = (xv * r * bs[0][...]).astype(BF16)

    return _rowwise(fn, [x], [g], [(D_MODEL, BF16)], tm=512, name=name, after=after)[0]


def _rms_bwd_math(xv, dh, g):
    r = lax.rsqrt(jnp.mean(xv * xv, axis=-1, keepdims=True) + EPS)
    hn = xv * r
    dgh = dh * g
    dx = r * (dgh - hn * jnp.mean(dgh * hn, axis=-1, keepdims=True))
    return dx, _rsum(dh * hn)


def _mm_norm_bwd(dy, w, x, dres, g, name, after=None):
    def epilogue(acc, e_refs, b_refs, o_refs, a_refs):
        dx, dg = _rms_bwd_math(e_refs[0][...], acc, b_refs[0][...])
        dx = dx + e_refs[1][...]
        o_refs[0][...] = dx
        o_refs[1][...] = dx.astype(BF16)
        a_refs[0][...] += dg

    return _mm(dy, w, mode="nt", b_shard=True, out_dtypes=(F32, BF16), extras=(x, dres), bcast=(g,),
               accs=((1, D_MODEL),), ref_epi=epilogue, name=name, after=after)


def _mm_res_norm(a, w, res, g, name):
    def epilogue(acc, e_refs, b_refs, o_refs, a_refs):
        xv = acc + e_refs[0][...]
        o_refs[0][...] = xv
        r = lax.rsqrt(jnp.mean(xv * xv, axis=-1, keepdims=True) + EPS)
        o_refs[1][...] = (xv * r * b_refs[0][...]).astype(BF16)

    return _mm(a, w, mode="nn", out_dtypes=(F32, BF16), extras=(res,), bcast=(g,), ref_epi=epilogue, name=name)


def _mm_final_loss(a, w, res, target, g, name):
    def epilogue(acc, e_refs, b_refs, o_refs, a_refs):
        xv = acc + e_refs[0][...]
        gv = b_refs[0][...]
        r = lax.rsqrt(jnp.mean(xv * xv, axis=-1, keepdims=True) + EPS)
        e = xv * r * gv - e_refs[1][...]
        tok = jnp.mean(e * e, axis=-1, keepdims=True)
        a_refs[0][...] += 0.5 * jnp.sum(tok, axis=0, keepdims=True) * jnp.ones((1, LANES), F32)
        dx, dg = _rms_bwd_math(xv, e * (1.0 / D_MODEL), gv)
        o_refs[0][...] = dx
        o_refs[1][...] = dx.astype(BF16)
        a_refs[1][...] += dg

    return _mm(a, w, mode="nn", out_dtypes=(F32, BF16), extras=(res, target), bcast=(g,),
               accs=((1, LANES), (1, D_MODEL)), ref_epi=epilogue, name=name)


def _relu2(acc):
    r = jnp.maximum(acc, 0.0)
    return r * r, r


def _mlp_fwd(x, h, fetch, tag, finish):
    w_up, w_down = fetch(f"mlp{tag}", h)
    a, r = _mm(h, w_up, mode="nn", b_shard=True, out_dtypes=(BF16, BF16), epi=_relu2, name=f"mlp{tag}_up")
    return finish(a, w_down, x, f"mlp{tag}_down"), (h, a, r, w_up, w_down)


def _mlp_bwd(x, g, saved, dx, dx_bf, tag, after):
    h, a, r, w_up, w_down = saved
    d_down = _mm(a, dx_bf, mode="tn", out_dtypes=(BF16,), name=f"mlp{tag}_dwdown", after=after)
    dup = _mm(dx_bf, w_down, mode="nt", extras=(r,), out_dtypes=(BF16,),
              epi=lambda acc, rv: (acc * (2.0 * rv.astype(F32)),), name=f"mlp{tag}_dup")
    d_up = _mm(h, dup, mode="tn", o_shard=True, out_dtypes=(BF16,), name=f"mlp{tag}_dwup")
    dx_new, dx_new_bf, dg = _mm_norm_bwd(dup, w_up, x, dx, g, f"mlp{tag}_dh")
    return dx_new, dx_new_bf, dg, d_up, d_down


def _rope_tables(L, B):
    rows = L // GRID_W
    row = np.repeat(np.arange(rows, dtype=np.float32), GRID_W)
    col = np.tile(np.arange(GRID_W, dtype=np.float32), rows)
    inv = (ROPE_THETA ** (-np.arange(HEAD_DIM // 4, dtype=np.float32) / (HEAD_DIM // 4))).astype(np.float32)
    ar, ac = row[:, None] * inv, col[:, None] * inv
    cos = np.concatenate([np.cos(ar), np.cos(ar), np.cos(ac), np.cos(ac)], axis=-1)
    sin = np.concatenate([-np.sin(ar), np.sin(ar), -np.sin(ac), np.sin(ac)], axis=-1)
    return jnp.asarray(np.tile(cos, (B, 1)), F32), jnp.asarray(np.tile(sin, (B, 1)), F32)


def _swap_halves(x):
    lane = lax.broadcasted_iota(jnp.int32, x.shape, 1)
    return jnp.where((lane % 64) < 32, pltpu.roll(x, HEAD_DIM - 32, 1), pltpu.roll(x, 32, 1))


def _qk_prep(qkv, cos, sin, q_g, k_g):
    def fn(ins, bs, outs, accs):
        c, s = ins[1][...], ins[2][...]
        for h in range(N_HEADS + N_KV):
            xv = ins[0][:, h * HEAD_DIM:(h + 1) * HEAD_DIM]
            g = bs[0][...] if h < N_HEADS else bs[1][...]
            r = lax.rsqrt(jnp.mean(xv * xv, axis=-1, keepdims=True) + EPS)
            z = xv * r * g
            y = (z * c + _swap_halves(z) * s).astype(BF16)
            if h < N_HEADS:
                outs[0][:, h * HEAD_DIM:(h + 1) * HEAD_DIM] = y
            else:
                outs[1][:, (h - N_HEADS) * HEAD_DIM:(h - N_HEADS + 1) * HEAD_DIM] = y
        outs[2][...] = ins[0][:, (N_HEADS + N_KV) * HEAD_DIM:].astype(BF16)

    kvw = N_KV * HEAD_DIM
    return _rowwise(fn, [qkv, cos, sin], [q_g, k_g], [(D_MODEL, BF16), (kvw, BF16), (kvw, BF16)], tm=512,
                    name="attn_qk_prep")


def _qk_prep_bwd(qkv, dq, dk, dv, cos, sin, q_g, k_g):
    def fn(ins, bs, outs, accs):
        c, s = ins[4][...], ins[5][...]
        for h in range(N_HEADS + N_KV):
            sl = slice(h * HEAD_DIM, (h + 1) * HEAD_DIM)
            xv = ins[0][:, sl]
            if h < N_HEADS:
                g, dy, acc = bs[0][...], ins[1][:, sl], accs[0]
            else:
                ks = slice((h - N_HEADS) * HEAD_DIM, (h - N_HEADS + 1) * HEAD_DIM)
                g, dy, acc = bs[1][...], ins[2][:, ks], accs[1]
            r = lax.rsqrt(jnp.mean(xv * xv, axis=-1, keepdims=True) + EPS)
            xn = xv * r
            dz = dy * c - _swap_halves(dy) * s
            acc[...] += _rsum(dz * xn)
            dxn = dz * g
            outs[0][:, sl] = (r * (dxn - xn * jnp.mean(dxn * xn, axis=-1, keepdims=True))).astype(BF16)
        outs[0][:, (N_HEADS + N_KV) * HEAD_DIM:] = ins[3][...].astype(BF16)

    return _rowwise(fn, [qkv, dq, dk, dv, cos, sin], [q_g, k_g], [(qkv.shape[1], BF16)],
                    [(1, HEAD_DIM), (1, HEAD_DIM)], tm=256, name="attn_qk_prep_bwd")


_EXP2_SCALE = SCALE * math.log2(math.e)


def _exp_rows(q, k):
    s = lax.dot_general(q, k, _NT, preferred_element_type=F32)
    p = jnp.exp2((s - jnp.max(s, axis=-1, keepdims=True)) * _EXP2_SCALE)
    return p, jnp.sum(p, axis=-1, keepdims=True)


def _attn_fwd(q, k, v, B, L, tq=2048, sub=256):
    tq = min(tq, L)
    sub = min(sub, tq)
    nq = L // tq

    def body(q_ref, k_ref, v_ref, o_ref):
        kv, vv = k_ref[...], v_ref[...]
        for c in range(tq // sub):
            rows = slice(c * sub, (c + 1) * sub)
            p, l = _exp_rows(q_ref[rows, :], kv)
            o = jnp.dot(p.astype(BF16), vv, preferred_element_type=F32)
            o_ref[rows, :] = (o * (1.0 / l)).astype(o_ref.dtype)

    return pl.pallas_call(
        body, name="attn_fwd", grid=(B, N_HEADS, nq),
        in_specs=[pl.BlockSpec((tq, HEAD_DIM), lambda b, h, i: (b * nq + i, h)),
                  pl.BlockSpec((L, HEAD_DIM), lambda b, h, i: (b, h // GROUP)),
                  pl.BlockSpec((L, HEAD_DIM), lambda b, h, i: (b, h // GROUP))],
        out_specs=pl.BlockSpec((tq, HEAD_DIM), lambda b, h, i: (b * nq + i, h)),
        out_shape=jax.ShapeDtypeStruct((B * L, D_MODEL), BF16),
        compiler_params=_params(("parallel", "parallel", "parallel")),
    )(q, k, v)


def _attn_bwd(q, k, v, do, B, L, tq=2048, sub=512):
    tq = min(tq, L)
    sub = min(sub, tq)
    nq = L // tq

    def body(q_ref, k_ref, v_ref, do_ref, dq_ref, dk_ref, dv_ref):
        @pl.when((pl.program_id(2) == 0) & (pl.program_id(3) == 0))
        def _():
            dk_ref[...] = jnp.zeros(dk_ref.shape, F32)
            dv_ref[...] = jnp.zeros(dv_ref.shape, F32)

        kv, vv = k_ref[...], v_ref[...]
        ps, es, dos, qs = [], [], [], []
        for c in range(tq // sub):
            rows = slice(c * sub, (c + 1) * sub)
            qc, doc = q_ref[rows, :], do_ref[rows, :]
            p, l = _exp_rows(qc, kv)
            inv = 1.0 / l
            dp = lax.dot_general(doc, vv, _NT, preferred_element_type=F32)
            delta = jnp.sum(p * dp, axis=-1, keepdims=True) * inv
            e = (p * (dp - delta)).astype(BF16)
            dq_ref[rows, :] = jnp.dot(e, kv, preferred_element_type=F32) * (inv * SCALE)
            ps.append(p.astype(BF16))
            es.append(e)
            dos.append((doc.astype(F32) * inv).astype(BF16))
            qs.append((qc.astype(F32) * (inv * SCALE)).astype(BF16))
        cat = lambda xs: xs[0] if len(xs) == 1 else jnp.concatenate(xs, axis=0)
        dv_ref[...] += lax.dot_general(cat(ps), cat(dos), _TN, preferred_element_type=F32)
        dk_ref[...] += lax.dot_general(cat(es), cat(qs), _TN, preferred_element_type=F32)

    qmap = lambda b, kh, g, i: (b * nq + i, kh * GROUP + g)
    kmap = lambda b, kh, g, i: (b, kh)
    kvw = N_KV * HEAD_DIM
    return pl.pallas_call(
        body, name="attn_bwd", grid=(B, N_KV, GROUP, nq),
        in_specs=[pl.BlockSpec((tq, HEAD_DIM), qmap), pl.BlockSpec((L, HEAD_DIM), kmap),
                  pl.BlockSpec((L, HEAD_DIM), kmap), pl.BlockSpec((tq, HEAD_DIM), qmap)],
        out_specs=[pl.BlockSpec((tq, HEAD_DIM), qmap), pl.BlockSpec((L, HEAD_DIM), kmap),
                   pl.BlockSpec((L, HEAD_DIM), kmap)],
        out_shape=[jax.ShapeDtypeStruct((B * L, D_MODEL), F32), jax.ShapeDtypeStruct((B * L, kvw), F32),
                   jax.ShapeDtypeStruct((B * L, kvw), F32)],
        compiler_params=_params(("parallel", "parallel", "arbitrary", "arbitrary")),
    )(q, k, v, do)


def _conv_shift(x, t, L, k):
    if k == 2:
        return x
    if k < 2:
        return jnp.where(t >= 2 - k, pltpu.roll(x, 2 - k, 0), 0.0)
    return jnp.where(t < L - (k - 2), pltpu.roll(x, L - (k - 2), 0), 0.0)


def _conv_fwd(z, wb, B, L, tc=256):
    noff = D_MODEL // tc

    def body(z_ref, w_ref, o_ref):
        x = z_ref[...]
        t = lax.broadcasted_iota(jnp.int32, x.shape, 0)
        acc = w_ref[4:5, :] + w_ref[2:3, :] * x
        for k in (0, 1, 3):
            acc = acc + w_ref[k:k + 1, :] * _conv_shift(x, t, L, k)
        o_ref[...] = acc

    return pl.pallas_call(
        body, name="rg_conv", grid=(B, noff),
        in_specs=[pl.BlockSpec((L, tc), lambda b, j: (b, noff + j)), pl.BlockSpec((SUBLANES, tc), lambda b, j: (0, j))],
        out_specs=pl.BlockSpec((L, tc), lambda b, j: (b, j)),
        out_shape=jax.ShapeDtypeStruct((B * L, D_MODEL), F32),
        compiler_params=_params(("parallel", "parallel")),
    )(z, wb)


def _conv_bwd(z, g, wb, dz, B, L, tc=256, after=None):
    noff = D_MODEL // tc
    order = _after_operand(after)

    def body(z_ref, g_ref, w_ref, dz_in, *rest):
        dx_ref, dw_ref = rest[len(order):]

        @pl.when(pl.program_id(1) == 0)
        def _():
            dw_ref[...] = jnp.zeros(dw_ref.shape, F32)

        x, gv = z_ref[...], g_ref[...]
        t = lax.broadcasted_iota(jnp.int32, x.shape, 0)
        dx = w_ref[2:3, :] * gv
        for k in (0, 1, 3):
            dx = dx + w_ref[k:k + 1, :] * _conv_shift(gv, t, L, 4 - k)
        dx_ref[...] = dx.astype(BF16)
        for k in range(4):
            dw_ref[k:k + 1, :] += _rsum(_conv_shift(x, t, L, k) * gv)
        dw_ref[4:5, :] += _rsum(gv)

    return pl.pallas_call(
        body, name="rg_conv_bwd", grid=(noff, B),
        in_specs=[pl.BlockSpec((L, tc), lambda j, b: (b, noff + j)), pl.BlockSpec((L, tc), lambda j, b: (b, j)),
                  pl.BlockSpec((SUBLANES, tc), lambda j, b: (0, j)), _ANY] + [_ANY] * len(order),
        out_specs=[pl.BlockSpec((L, tc), lambda j, b: (b, noff + j)),
                   pl.BlockSpec((SUBLANES, tc), lambda j, b: (0, j))],
        out_shape=[jax.ShapeDtypeStruct(dz.shape, dz.dtype), jax.ShapeDtypeStruct((SUBLANES, D_MODEL), F32)],
        input_output_aliases={3: 0},
        compiler_params=_params(("parallel", "arbitrary")),
    )(z, g, wb, dz, *order)


def _softplus(x):
    return jnp.maximum(x, 0.0) + jnp.log1p(jnp.exp(-jnp.abs(x)))


_ROW_BA, _ROW_BX, _ROW_LAM = 0, 2, 4


def _gate_math(xb, pre, vec_ref, d, sl):
    pa = pre[:, (2 * d) * LRU_BW:(2 * d + 1) * LRU_BW] + vec_ref[_ROW_BA + d:_ROW_BA + d + 1, sl]
    px = pre[:, (2 * d + 1) * LRU_BW:(2 * d + 2) * LRU_BW] + vec_ref[_ROW_BX + d:_ROW_BX + d + 1, sl]
    r = 0.5 * jnp.tanh(0.5 * pa) + 0.5
    i = 0.5 * jnp.tanh(0.5 * px) + 0.5
    sp = _softplus(-vec_ref[_ROW_LAM + d:_ROW_LAM + d + 1, sl])
    log_a = (-RG_C) * r * sp
    a = jnp.exp(log_a)
    th = jnp.tanh(log_a)
    om = -2.0 * th / (1.0 - th)
    mult = jnp.sqrt(om)
    return a, mult * (i * xb), (r, i, sp, om, mult)


def _gate_fwd(rec, wcat, gvec):
    def fn(ins, bs, outs, accs):
        for blk in range(LRU_BLOCKS):
            sl = slice(blk * LRU_BW, (blk + 1) * LRU_BW)
            xb = ins[0][:, sl]
            pre = jnp.dot(xb.astype(BF16), bs[0][sl, :], preferred_element_type=F32)
            for d in range(2):
                a, u, _ = _gate_math(xb, pre, bs[1], d, sl)
                outs[2 * d][:, sl] = a
                outs[2 * d + 1][:, sl] = u

    return _rowwise(fn, [rec], [wcat, gvec], [(D_MODEL, F32)] * 4, tm=256, name="rg_gate")


def _gate_bwd(rec, du_f, da_f, du_b, da_b, wcat, gvec):
    def fn(ins, bs, outs, accs):
        for blk in range(LRU_BLOCKS):
            sl = slice(blk * LRU_BW, (blk + 1) * LRU_BW)
            xb = ins[0][:, sl]
            xb16 = xb.astype(BF16)
            w = bs[0][sl, :]
            pre = jnp.dot(xb16, w, preferred_element_type=F32)
            dx = jnp.zeros_like(xb)
            dpre = []
            for d in range(2):
                a, _, (r, i, sp, om, mult) = _gate_math(xb, pre, bs[1], d, sl)
                du, da = ins[1 + 2 * d][:, sl], ins[2 + 2 * d][:, sl]
                d_i = du * mult * xb
                d_mult = du * i * xb
                dx = dx + du * mult * i
                dlog = da * a - d_mult * (1.0 - om) / mult
                d_r = dlog * ((-RG_C) * sp)
                d_sp = _rsum(dlog * ((-RG_C) * r))
                lam = bs[1][_ROW_LAM + d:_ROW_LAM + d + 1, sl]
                accs[2][_ROW_LAM + d:_ROW_LAM + d + 1, sl] += d_sp * (-jax.nn.sigmoid(-lam))
                dpa = d_r * r * (1.0 - r)
                dpx = d_i * i * (1.0 - i)
                accs[2][_ROW_BA + d:_ROW_BA + d + 1, sl] += _rsum(dpa)
                accs[2][_ROW_BX + d:_ROW_BX + d + 1, sl] += _rsum(dpx)
                dpre += [dpa, dpx]
            dpre = jnp.concatenate(dpre, axis=1).astype(BF16)
            dw = lax.dot_general(xb16, dpre, _TN, preferred_element_type=F32)
            for d in range(2):
                rows = slice(d * D_MODEL + blk * LRU_BW, d * D_MODEL + (blk + 1) * LRU_BW)
                accs[0][rows, :] += dw[:, (2 * d) * LRU_BW:(2 * d + 1) * LRU_BW]
                accs[1][rows, :] += dw[:, (2 * d + 1) * LRU_BW:(2 * d + 2) * LRU_BW]
            outs[0][:, sl] = dx + lax.dot_general(dpre, w, _NT, preferred_element_type=F32)

    gate_shape = (2 * D_MODEL, LRU_BW)
    return _rowwise(fn, [rec, du_f, da_f, du_b, da_b], [wcat, gvec], [(D_MODEL, F32)],
                    [gate_shape, gate_shape, (SUBLANES, D_MODEL)], tm=256, name="rg_gate_bwd")


def _as_time_blocks(x):
    return x.reshape(x.shape[0] // SUBLANES, SUBLANES, x.shape[1])


def _scan_call(body, ins, n_out, B, L, tc, name):
    nb = L // SUBLANES
    spec = pl.BlockSpec((nb, SUBLANES, tc), lambda b, j: (b, 0, j))
    T = ins[0].shape[0]
    outs = pl.pallas_call(
        functools.partial(body, nb), name=name, grid=(B, D_MODEL // tc),
        in_specs=[spec] * len(ins), out_specs=[spec] * n_out,
        out_shape=[jax.ShapeDtypeStruct((T // SUBLANES, SUBLANES, D_MODEL), F32)] * n_out,
        compiler_params=_params(("parallel", "parallel")),
    )(*[_as_time_blocks(x) for x in ins])
    return [o.reshape(T, D_MODEL) for o in outs]


def _block_scan(A, U, reverse):
    row = lax.broadcasted_iota(jnp.int32, A.shape, 0)
    for s in (1, 2, 4):
        shift = SUBLANES - s if reverse else s
        valid = (row < SUBLANES - s) if reverse else (row >= s)
        a_sh = jnp.where(valid, pltpu.roll(A, shift, 0), 1.0)
        u_sh = jnp.where(valid, pltpu.roll(U, shift, 0), 0.0)
        U = A * u_sh + U
        A = A * a_sh
    return A, U


_LAST = SUBLANES - 1
SCAN_UNROLL = 8


def _loop_blocks(nb, step, init):
    def group(g, carry):
        for k in range(SCAN_UNROLL):
            carry = step(g * SCAN_UNROLL + k, carry)
        return carry

    return lax.fori_loop(0, nb // SCAN_UNROLL, group, init)


def _scan_fwd(a_f, u_f, a_b, u_b, B, L, tc=256):
    def body(nb, af, uf, ab, ub, hf, hb):
        def step(i, carry):
            c1, c2 = carry
            ib = nb - 1 - i
            p, h = _block_scan(af[i], uf[i], False)
            h = h + p * c1
            hf[i] = h
            p2, h2 = _block_scan(ab[ib], ub[ib], True)
            h2 = h2 + p2 * c2
            hb[ib] = h2
            return h[_LAST:, :], h2[:1, :]

        zero = jnp.zeros((1, tc), F32)
        _loop_blocks(nb, step, (zero, zero))

    return _scan_call(body, [a_f, u_f, a_b, u_b], 2, B, L, tc, "rg_scan")


def _scan_bwd(dy, a_f, h_f, a_b, h_b, B, L, tc=256):
    def body(nb, dy_r, af, hf, ab, hb, duf, daf, dub, dab):
        def step(i, carry):
            c1, c2 = carry
            ir = nb - 1 - i
            row = lax.broadcasted_iota(jnp.int32, (SUBLANES, tc), 0)
            a_up = jnp.where(row == _LAST, af[jnp.minimum(ir + 1, nb - 1), :1, :], pltpu.roll(af[ir], _LAST, 0))
            p, lam = _block_scan(a_up, dy_r[ir], True)
            lam = lam + p * c1
            before = hf[jnp.maximum(ir - 1, 0), _LAST:, :] * (ir > 0).astype(F32)
            duf[ir] = lam
            daf[ir] = lam * jnp.where(row == 0, before, pltpu.roll(hf[ir], 1, 0))
            a_dn = jnp.where(row == 0, ab[jnp.maximum(i - 1, 0), _LAST:, :], pltpu.roll(ab[i], 1, 0))
            p2, lam2 = _block_scan(a_dn, dy_r[i], False)
            lam2 = lam2 + p2 * c2
            after = hb[jnp.minimum(i + 1, nb - 1), :1, :] * (i < nb - 1).astype(F32)
            dub[i] = lam2
            dab[i] = lam2 * jnp.where(row == _LAST, after, pltpu.roll(hb[i], _LAST, 0))
            return lam[:1, :], lam2[_LAST:, :]

        zero = jnp.zeros((1, tc), F32)
        _loop_blocks(nb, step, (zero, zero))

    return _scan_call(body, [dy, a_f, h_f, a_b, h_b], 4, B, L, tc, "rg_scan_bwd")


_GELU_C = math.sqrt(2.0 / math.pi)


def _gelu_parts(x):
    th = jnp.tanh(_GELU_C * (x + 0.044715 * x * x * x))
    return 0.5 * x * (1.0 + th), th


def _gated_out(h_f, h_b, z):
    def fn(ins, bs, outs, accs):
        gl, _ = _gelu_parts(ins[2][...])
        outs[0][...] = ((ins[0][...] + ins[1][...]) * gl).astype(BF16)

    return _rowwise(fn, [h_f, h_b, (z, D_MODEL, 0)], [], [(D_MODEL, BF16)], tm=512, name="rg_gated_out")[0]


def _gated_out_bwd(dyg, h_f, h_b, z):
    def fn(ins, bs, outs, accs):
        x = ins[3][...]
        gl, th = _gelu_parts(x)
        dgl = 0.5 * (1.0 + th) + 0.5 * x * (1.0 - th * th) * (_GELU_C * (1.0 + 3.0 * 0.044715 * x * x))
        g = ins[0][...]
        outs[0][...] = g * gl
        outs[1][...] = (g * (ins[1][...] + ins[2][...]) * dgl).astype(BF16)

    return _rowwise(fn, [dyg, h_f, h_b, (z, D_MODEL, 0)], [], [(D_MODEL, F32), (D_MODEL, BF16, 2 * D_MODEL)], tm=512,
                    name="rg_gated_out_bwd")


def _make_wcat(w_a, w_x):
    g = jnp.stack([w_a[0, 0], w_x[0, 0], w_a[0, 1], w_x[0, 1]])
    return jnp.transpose(g, (1, 2, 0, 3)).reshape(D_MODEL, 4 * LRU_BW)


def _rows_at(part, first):
    return jnp.pad(part, ((first, SUBLANES - first - part.shape[0]), (0, 0)))


def _qk_slot(q_g, k_g):
    wide = lambda v, at: jnp.pad(v, ((0, SUBLANES - 1), (at, D_MODEL - at - HEAD_DIM)))
    return wide(q_g, 0) + wide(k_g, HEAD_DIM)


def _local_step(x, target, P, fetch, emit, B, L, after=None):
    g_mix, g_mlp = P["norm_mix_g"], P["norm_mlp_g"]
    h0 = _rms_fwd(x, g_mix[0:1], "rg_norm", after=after)
    w_in, w_out, conv_wb, wcat, gvec = fetch("rg", h0)
    z = _mm(h0, w_in, mode="nn", b_shard=True, name="rg_in")
    rec = _conv_fwd(z, conv_wb, B, L)
    a_f, u_f, a_b, u_b = _gate_fwd(rec, wcat, gvec)
    h_f, h_b = _scan_fwd(a_f, u_f, a_b, u_b, B, L)
    yg = _gated_out(h_f, h_b, z)
    x1, h1 = _mm_res_norm(yg, w_out, x, g_mlp[0:1], "rg_out")
    (x2, h3), mlp0 = _mlp_fwd(x1, h1, fetch, 0, lambda a, w, res, name: _mm_res_norm(a, w, res, g_mix[1:2], name))
    w_qkv, w_o = fetch("att", h3)
    qkv = _mm(h3, w_qkv, mode="nn", b_shard=True, name="attn_qkv")
    cos, sin = _rope_tables(L, B)
    qh, kh, vh = _qk_prep(qkv, cos, sin, P["q_g"], P["k_g"])
    o = _attn_fwd(qh, kh, vh, B, L)
    x3, h4 = _mm_res_norm(o, w_o, x2, g_mlp[1:2], "attn_out")
    (dx4, dx4_bf, loss_acc, d_final_g), mlp1 = _mlp_fwd(
        x3, h4, fetch, 1, lambda a, w, res, name: _mm_final_loss(a, w, res, target, P["final_g"], name))

    dx3, dx3_bf, dg_mlp1, d_up1, d_down1 = _mlp_bwd(x3, g_mlp[1:2], mlp1, dx4, dx4_bf, 1, None)
    tok = emit("mlp1", [d_up1, d_down1])
    d_wo = _mm(o, dx3_bf, mode="tn", out_dtypes=(BF16,), name="attn_dwo", after=tok)
    do = _mm(dx3_bf, w_o, mode="nt", out_dtypes=(BF16,), name="attn_do")
    dq, dk, dv = _attn_bwd(qh, kh, vh, do, B, L)
    dqkv, dq_g, dk_g = _qk_prep_bwd(qkv, dq, dk, dv, cos, sin, P["q_g"], P["k_g"])
    d_wqkv = _mm(h3, dqkv, mode="tn", o_shard=True, out_dtypes=(BF16,), name="attn_dwqkv")
    tok = emit("att", [d_wqkv, d_wo])
    dx2, dx2_bf, dg_mix1 = _mm_norm_bwd(dqkv, w_qkv, x2, dx3, g_mix[1:2], "attn_dh", after=tok)
    tok = emit("point_attn_done", [dx2_bf])
    dx1, dx1_bf, dg_mlp0, d_up0, d_down0 = _mlp_bwd(x1, g_mlp[0:1], mlp0, dx2, dx2_bf, 0, tok)
    tok = emit("mlp0", [d_up0, d_down0])
    d_wout = _mm(yg, dx1_bf, mode="tn", out_dtypes=(BF16,), name="rg_dwout", after=tok)
    tok = emit("rg_out", [d_wout])
    dyg = _mm(dx1_bf, w_out, mode="nt", name="rg_dyg", after=tok)
    dy, dgate = _gated_out_bwd(dyg, h_f, h_b, z)
    du_f, da_f, du_b, da_b = _scan_bwd(dy, a_f, h_f, a_b, h_b, B, L)
    drec_c, d_wa, d_wx, d_gvec = _gate_bwd(rec, du_f, da_f, du_b, da_b, wcat, gvec)
    tok = emit("gates", [d_wa, d_wx])
    dz, d_convwb = _conv_bwd(z, drec_c, conv_wb, dgate, B, L, after=tok)
    tok = emit("point_conv_done", [dz])
    d_win = _mm(h0, dz, mode="tn", o_shard=True, out_dtypes=(BF16,), name="rg_dwin", after=tok)
    tok = emit("rg_in", [d_win])
    grad_x, _, dg_mix0 = _mm_norm_bwd(dz, w_in, x, dx1, g_mix[0:1], "rg_dh", after=tok)

    norms = (_rows_at(dg_mix0, 0) + _rows_at(dg_mix1, 1) + _rows_at(dg_mlp0, 2) + _rows_at(dg_mlp1, 3)
             + _rows_at(d_final_g, 4)
             + jnp.pad(loss_acc, ((LOSS_ROW, SUBLANES - 1 - LOSS_ROW), (0, D_MODEL - LANES))))
    vec = jnp.concatenate([norms, d_convwb, d_gvec, _qk_slot(dq_g, dk_g)], axis=0)
    return grad_x, vec


_MESH = pl.DeviceIdType.MESH


def _place():
    x, y, c = lax.axis_index("x"), lax.axis_index("y"), lax.axis_index("c")
    peers = [((1 - x) if j & 2 else x, (1 - y) if j & 1 else y) for j in (1, 2, 3)]
    return x, y, c, peers


def _comm_call(body, ins, out_shapes, n_sem, name):
    return pl.pallas_call(
        body, name=name, in_specs=[_ANY] * len(ins), out_specs=[_ANY] * len(out_shapes), out_shape=out_shapes,
        scratch_shapes=[pltpu.SemaphoreType.DMA((n_sem,)), pltpu.SemaphoreType.DMA((n_sem,)),
                        pltpu.SemaphoreType.DMA((len(ins),))],
    )(*ins)


def _all_devices_slots(v, name):
    def body(v_ref, out_ref, send, recv, lsem):
        x, y, c = lax.axis_index("x"), lax.axis_index("y"), lax.axis_index("c")
        me = 4 * x + 2 * y + c

        def peer(j):
            return (1 - x) if j & 4 else x, (1 - y) if j & 2 else y, (1 - c) if j & 1 else c

        def copy(j, slot):
            return pltpu.make_async_remote_copy(
                src_ref=v_ref, dst_ref=out_ref.at[slot], send_sem=send.at[j - 1], recv_sem=recv.at[j - 1],
                device_id=peer(j), device_id_type=_MESH)

        local = pltpu.make_async_copy(v_ref, out_ref.at[me], lsem.at[0])
        sends = [copy(j, me) for j in range(1, N_DEVICES)]
        for cp in [local] + sends:
            cp.start()
        for j in range(1, N_DEVICES):
            px, py, pc = peer(j)
            copy(j, 4 * px + 2 * py + pc).wait_recv()
        for cp in sends:
            cp.wait_send()
        local.wait()

    shape = jax.ShapeDtypeStruct((N_DEVICES,) + v.shape, v.dtype)
    return _comm_call(body, [v], [shape], N_DEVICES - 1, name)[0]


def _sum_leading(slots, name):
    def body(s_ref, o_ref):
        acc = s_ref[0]
        for d in range(1, slots.shape[0]):
            acc = acc + s_ref[d]
        o_ref[...] = acc

    return pl.pallas_call(body, name=name, out_shape=jax.ShapeDtypeStruct(slots.shape[1:], slots.dtype))(slots)


_HBM = pl.BlockSpec(memory_space=pltpu.HBM)
_SEM = pl.BlockSpec(memory_space=pltpu.SEMAPHORE)
_EFFECT = pltpu.SideEffectType.DATAFLOW_SIDE_EFFECTING


_COPIES = dict(gather=N_CHIPS - 1, scatter=N_CHIPS - 1, swap=1)


def _split_copies(kind, srcs, lands, send, recv):
    x, y, c, peers = _place()
    me = 2 * x + y
    per = _COPIES[kind]
    out = []
    for a in range(len(lands)):
        for j in range(per):
            if kind == "swap":
                src, there, here, dev = srcs[a], lands[a], lands[a], (x, y, 1 - c)
            else:
                px, py = peers[j]
                dev = (px, py, c)
                if kind == "gather":
                    src, there, here = lands[a].at[me], lands[a].at[me], lands[a].at[2 * px + py]
                else:
                    src, there, here = srcs[a].at[2 * px + py], lands[a].at[j], lands[a].at[j]
            mk = functools.partial(
                pltpu.make_async_remote_copy, src_ref=src, send_sem=send.at[per * a + j],
                recv_sem=recv.at[per * a + j], device_id=dev, device_id_type=_MESH)
            out.append((functools.partial(mk, dst_ref=there), functools.partial(mk, dst_ref=here)))
    return out


def _exchange_start(kind, srcs, lands, name, after=None):
    arrays = list(srcs) + list(lands)
    n_s, n, n_all = len(srcs), len(lands), len(srcs) + len(lands)
    n_sem = _COPIES[kind] * n
    order = _after_operand(after)
    n_x = len(order)

    def body(*refs):
        send, recv = refs[n_all + n_x], refs[n_all + n_x + 1]
        token = refs[-1]
        for started, _ in _split_copies(kind, refs[:n_s], refs[n_s:n_all], send, recv):
            started().start()
        token[...] = jnp.zeros(token.shape, F32)

    res = pl.pallas_call(
        body, name=name,
        out_shape=(pltpu.SemaphoreType.DMA((n_sem,)), pltpu.SemaphoreType.DMA((n_sem,)),
                   *[pltpu.HBM(a.shape, a.dtype) for a in arrays], jax.ShapeDtypeStruct((SUBLANES, LANES), F32)),
        in_specs=[_HBM] * n_all + [_ANY] * n_x,
        out_specs=(_SEM, _SEM, *[_HBM] * n_all, pl.BlockSpec(memory_space=pltpu.VMEM)),
        input_output_aliases={i: 2 + i for i in range(n_all)},
        compiler_params=pltpu.CompilerParams(has_side_effects=_EFFECT),
    )(*[pltpu.with_memory_space_constraint(a, pltpu.HBM) for a in arrays], *order)
    return (res[0], res[1], res[2:2 + n_s], res[2 + n_s:2 + n_all]), res[-1]


def _exchange_wait(kind, handle, after, name):
    send, recv, srcs, lands = handle
    arrays = list(srcs) + list(lands)
    n_s, n_all = len(srcs), len(arrays)
    order = list(after) if isinstance(after, (list, tuple)) else [after]

    def body(*refs):
        for started, landing in _split_copies(kind, refs[:n_s], refs[n_s:n_all], refs[n_all], refs[n_all + 1]):
            started().wait_send()
            landing().wait_recv()

    res = pl.pallas_call(
        body, name=name, out_shape=[pltpu.HBM(a.shape, a.dtype) for a in arrays],
        in_specs=[_HBM] * n_all + [_SEM, _SEM] + [_ANY] * len(order), out_specs=[_HBM] * n_all,
        input_output_aliases={i: i for i in range(n_all)},
        compiler_params=pltpu.CompilerParams(has_side_effects=_EFFECT),
    )(*arrays, send, recv, *order)
    return res[:n_s], res[n_s:]


def _index_operand(i):
    return jnp.reshape(i, (1,)).astype(jnp.int32)


def _cast_into_slot(src, row0, rows, me, dtype, name, after=None, add=None):
    cols = src.shape[1]
    tm = min(512, rows)
    order = _after_operand(after)
    terms = [src] + ([] if add is None else [add])

    def body(me_ref, *rest):
        val = rest[0][...]
        if add is not None:
            val = val + rest[1][...]
        rest[-1][...] = val.astype(dtype)

    return pl.pallas_call(
        body, name=name,
        grid_spec=pltpu.PrefetchScalarGridSpec(
            num_scalar_prefetch=1, grid=(rows // tm,),
            in_specs=[pl.BlockSpec((tm, cols), lambda i, me_ref: (i + row0 // tm, 0))] * len(terms)
            + [_ANY] * len(order),
            out_specs=pl.BlockSpec((None, tm, cols), lambda i, me_ref: (me_ref[0], i, 0))),
        out_shape=jax.ShapeDtypeStruct((N_CHIPS, rows, cols), dtype), compiler_params=_params(("parallel",)),
    )(_index_operand(me), *terms, *order)


def _sum_slots(mine, r, me, name):
    _, rows, cols = r.shape
    tm = min(512, rows)

    def body(me_ref, own_ref, r_ref, o_ref):
        o_ref[...] = ((own_ref[...].astype(F32) + r_ref[0].astype(F32)) + r_ref[1].astype(F32)) + r_ref[2].astype(F32)

    return pl.pallas_call(
        body, name=name,
        grid_spec=pltpu.PrefetchScalarGridSpec(
            num_scalar_prefetch=1, grid=(rows // tm,),
            in_specs=[pl.BlockSpec((None, tm, cols), lambda i, me_ref: (me_ref[0], i, 0)),
                      pl.BlockSpec((N_CHIPS - 1, tm, cols), lambda i, me_ref: (0, i, 0))],
            out_specs=pl.BlockSpec((tm, cols), lambda i, me_ref: (i, 0))),
        out_shape=jax.ShapeDtypeStruct((rows, cols), F32), compiler_params=_params(("parallel",)),
    )(_index_operand(me), mine, r)


def _adamw(w, m, v, ps, qs, name):
    rows, cols = w.shape
    seg_rows = ps[0].shape[0]
    tm = min(256, seg_rows)
    while seg_rows % tm:
        tm -= SUBLANES
    per, n_seg = seg_rows // tm, len(ps)
    parts = list(ps) + ([] if qs is None else list(qs))

    def body(w_ref, m_ref, v_ref, *rest):
        g_refs, outs = rest[:len(parts)], rest[len(parts):]
        grad = lambda s: g_refs[s][...] if qs is None else g_refs[s][...] + g_refs[n_seg + s][...]
        g = grad(0)
        for s in range(1, n_seg):
            g = jnp.where(pl.program_id(0) >= s * per, grad(s), g)
        m1 = ADAM_B1 * m_ref[...] + (1.0 - ADAM_B1) * g
        v1 = ADAM_B2 * v_ref[...] + (1.0 - ADAM_B2) * (g * g)
        m_hat = m1 / (1.0 - ADAM_B1 ** ADAM_STEP)
        v_hat = v1 / (1.0 - ADAM_B2 ** ADAM_STEP)
        outs[0][...] = g
        outs[1][...] = (-ADAM_LR) * (m_hat / (jnp.sqrt(v_hat) + ADAM_EPS) + ADAM_WD * w_ref[...])
        outs[2][...] = m1
        outs[3][...] = v1

    row_spec = pl.BlockSpec((tm, cols), lambda i: (i, 0))
    seg_spec = lambda s: pl.BlockSpec((tm, cols), lambda i: (jnp.clip(i - s * per, 0, per - 1), 0))
    return pl.pallas_call(
        body, name=name, grid=(rows // tm,),
        in_specs=[row_spec] * 3 + [seg_spec(s) for s in range(n_seg)] * (1 if qs is None else 2),
        out_specs=[row_spec] * 4, out_shape=[jax.ShapeDtypeStruct((rows, cols), F32)] * 4,
        compiler_params=_params(("arbitrary",)),
    )(w, m, v, *parts)


def _put_cols(shard, me):
    full = jnp.zeros((shard.shape[0], D_MODEL), F32)
    return lax.dynamic_update_slice(full, shard, (0, me * (D_MODEL // N_CHIPS)))


def _gate_vec_slot(b_a, b_x, lam):
    return _rows_at(b_a, _ROW_BA) + _rows_at(b_x, _ROW_BX) + _rows_at(lam, _ROW_LAM)


def _pack_vec(p, me):
    return jnp.concatenate([
        _rows_at(p["norm_mix_g"], 0) + _rows_at(p["norm_mlp_g"], 2) + _rows_at(p["final_g"][None], 4),
        _rows_at(_put_cols(p["rg_conv_w"][0, :, 0, :], me), 0) + _rows_at(p["rg_conv_b"], 4),
        _gate_vec_slot(_put_cols(p["rg_b_a"][0], me), _put_cols(p["rg_b_x"][0], me), _put_cols(p["rg_lam"][0], me)),
        _qk_slot(p["at_q_g"], p["at_k_g"]),
    ], axis=0)


def _unpack_vec(r, me):
    def cols(rows):
        return lax.dynamic_slice(rows, (0, me * (D_MODEL // N_CHIPS)), (rows.shape[0], D_MODEL // N_CHIPS))

    gate = r[16:24]
    return dict(
        norm_mix_g=r[0:2], norm_mlp_g=r[2:4], final_g=r[4], rg_conv_w=cols(r[8:12])[None, :, None, :],
        rg_conv_b=r[12:13], rg_b_a=cols(gate[_ROW_BA:_ROW_BA + 2])[None], rg_b_x=cols(gate[_ROW_BX:_ROW_BX + 2])[None],
        rg_lam=cols(gate[_ROW_LAM:_ROW_LAM + 2])[None], at_q_g=r[24:25, 0:HEAD_DIM],
        at_k_g=r[24:25, HEAD_DIM:2 * HEAD_DIM])


_WEIGHTS = ['norm_mix_g', 'norm_mlp_g', 'rg_w_in', 'rg_conv_w', 'rg_conv_b', 'rg_w_a', 'rg_b_a', 'rg_w_x', 'rg_b_x',
            'rg_lam', 'rg_w_out', 'at_w_qkv', 'at_q_g', 'at_k_g', 'at_w_o', 'mlp_w_up', 'mlp_w_down', 'final_g']
_BIG = dict(rg_w_in=["rg_w_in"], rg_w_out=["rg_w_out"], at_w_qkv=["at_w_qkv"], at_w_o=["at_w_o"],
            mlp_w_up=["up0", "up1"], mlp_w_down=["down0", "down1"])


def kernel(x, *args):
    n_w = len(_WEIGHTS)
    w = dict(zip(_WEIGHTS, args[:n_w]))
    target = args[n_w]
    m = dict(zip(_WEIGHTS, args[n_w + 1:2 * n_w + 1]))
    v = dict(zip(_WEIGHTS, args[2 * n_w + 1:3 * n_w + 1]))
    B, L, _ = x.shape
    T = B * L
    me = 2 * lax.axis_index("x") + lax.axis_index("y")

    vec = jnp.concatenate([_gate_vec_slot(w["rg_b_a"][0], w["rg_b_x"][0], w["rg_lam"][0]),
                           _rows_at(w["rg_conv_w"][0, :, 0, :], 0)], axis=0)
    flat = lambda a: a.reshape(-1, a.shape[-1])
    rows_of = lambda k: w[k].shape[-2]
    groups = [("rg", [("rg_w_in", 0, BF16), ("rg_w_out", 0, BF16), (vec, 0, F32)]),
              ("mlp0", [("mlp_w_up", 0, BF16), ("mlp_w_down", 0, BF16)]),
              ("att", [("at_w_qkv", 0, BF16), ("at_w_o", 0, BF16)]),
              ("mlp1", [("mlp_w_up", 1, BF16), ("mlp_w_down", 1, BF16)])]
    gathers, tok = {}, None
    for group, members in groups:
        lands = []
        for n, (k, layer, dtype) in enumerate(members):
            src, rows = (flat(w[k]), rows_of(k)) if isinstance(k, str) else (k, k.shape[0])
            lands.append(_cast_into_slot(src, layer * rows, rows, me, dtype, f"place_{group}{n}", after=tok))
        gathers[group], tok = _exchange_start("gather", [], lands, f"gather_{group}_start", after=tok)
    wcat = _make_wcat(w["rg_w_a"], w["rg_w_x"]).astype(BF16)

    packs = [_pack_vec(p, me) for p in (w, m, v)]

    def fetch(group, after):
        order = [after, wcat] + packs if group == "rg" else after
        _, full = _exchange_wait("gather", gathers[group], order, f"gather_{group}_wait")
        if group == "rg":
            vec_full = jnp.transpose(full[2], (1, 0, 2)).reshape(2 * SUBLANES, D_MODEL)
            conv_wb = vec_full[SUBLANES:] + _rows_at(w["rg_conv_b"], 4)
            return full[0], full[1].reshape(D_MODEL, D_MODEL), conv_wb, wcat, vec_full[:SUBLANES]
        if group == "att":
            return full[0], full[1].reshape(D_MODEL, D_MODEL)
        return full[0], full[1].reshape(4 * D_MODEL, D_MODEL)

    names = dict(mlp1=["up1", "down1"], att=["at_w_qkv", "at_w_o"], mlp0=["up0", "down0"], rg_out=["rg_w_out"],
                 rg_in=["rg_w_in"], gates=["rg_w_a", "rg_w_x"])
    scatters, swaps, P, Q, res = {}, [], {}, {}, {}

    def start_scatter(group, grads):
        srcs = [g.reshape(N_CHIPS, -1, g.shape[-1]) for g in grads]
        lands = [lax.empty((N_CHIPS - 1,) + s.shape[1:], s.dtype) for s in srcs]
        scatters[group], token = _exchange_start("scatter", srcs, lands, f"scatter_{group}_start")
        return token

    def settle(groups, after):
        keys, parts = [], []
        for group in groups:
            srcs, lands = _exchange_wait("scatter", scatters[group], after, f"scatter_{group}_wait")
            for k, s, r in zip(names[group], srcs, lands):
                keys.append(k)
                parts.append(_sum_slots(s, r, me, f"sum_{k}"))
        handle, token = _exchange_start("swap", parts, [lax.empty(p.shape, F32) for p in parts],
                                        f"swap_{groups[0]}_start")
        swaps.append((keys, handle, f"swap_{groups[0]}_wait"))
        return token

    def finish(after):
        for keys, handle, name in swaps:
            mine, theirs = _exchange_wait("swap", handle, after, name)
            P.update(zip(keys, mine))
            Q.update(zip(keys, theirs))
        swaps.clear()
        last = after
        for k, parts in _BIG.items():
            if k in res or any(p not in P for p in parts):
                continue
            shape = w[k].shape
            two_d = lambda a: a.reshape(-1, shape[-1])
            outs = _adamw(two_d(w[k]), two_d(m[k]), two_d(v[k]), [P[p] for p in parts], [Q[p] for p in parts],
                          f"adamw_{k}")
            res[k] = [o.reshape(shape) for o in outs]
            last = outs[0]
        if "rg_w_a" in P and "gates" not in gathers:
            lands = [_cast_into_slot(P[k], 0, P[k].shape[0], me, F32, f"place_{k}", after=last, add=Q[k])
                     for k in names["gates"]]
            gathers["gates"], last = _exchange_start("gather", [], lands, "gather_gates_start", after=last)
        return last

    def emit(event, arrays):
        if event == "point_attn_done":
            return settle(["mlp1"], arrays[0])
        if event == "point_conv_done":
            return settle(["att", "mlp0", "rg_out", "gates"], arrays[0])
        token = start_scatter(event, arrays)
        return finish(token) if event == "rg_in" else token

    P_vec = dict(norm_mix_g=w["norm_mix_g"], norm_mlp_g=w["norm_mlp_g"], final_g=w["final_g"][None],
                 q_g=w["at_q_g"], k_g=w["at_k_g"])
    grad_x, vec_part = _local_step(x.reshape(T, D_MODEL), target.reshape(T, D_MODEL), P_vec, fetch, emit, B, L,
                                   after=tok)

    finish(settle(["rg_in"], grad_x))
    _, gate_grads = _exchange_wait("gather", gathers["gates"], grad_x, "gather_gates_wait")
    for k, g in zip(names["gates"], gate_grads):
        two_d = lambda a: a.reshape(g.shape[0] * g.shape[1], g.shape[2])
        outs = _adamw(two_d(w[k]), two_d(m[k]), two_d(v[k]), [two_d(g)], None, f"adamw_{k}")
        res[k] = [o.reshape(w[k].shape) for o in outs]
    vec_grad = _sum_leading(_all_devices_slots(vec_part, "allreduce_vec"), "sum_vec")
    loss = vec_grad[LOSS_ROW, 0]
    outs = _adamw(*packs, [vec_grad], None, "adamw_vec")
    unpacked = [_unpack_vec(o, me) for o in outs]
    for k in _WEIGHTS:
        if k not in res:
            res[k] = [u[k] for u in unpacked]

    result = [loss, grad_x.reshape(B, L, D_MODEL)]
    for slot in range(4):
        result += [res[k][slot] for k in _WEIGHTS]
    return tuple(result)
```

```python
import functools
import math

import jax
import jax.numpy as jnp
import numpy as np
from jax import lax
from jax.experimental import pallas as pl
from jax.experimental.pallas import tpu as pltpu

F32 = jnp.float32
BF16 = jnp.bfloat16

D_MODEL = 1024
HEAD_DIM = 128
N_HEADS = 8
N_KV = 2
GROUP = N_HEADS // N_KV
LRU_BLOCKS = 8
LRU_BW = 128
GRID_W = 64
ROPE_THETA = 10000.0
EPS = 1e-6
RG_C = 8.0
SCALE = 1.0 / math.sqrt(HEAD_DIM)
N_CHIPS = 4

ADAM_LR = 0.001
ADAM_B1 = 0.9
ADAM_B2 = 0.999
ADAM_EPS = 1e-08
ADAM_WD = 0.01
ADAM_STEP = 10

V7X_VMEM_BYTES = 64 * 1024 * 1024
VMEM_LIMIT = V7X_VMEM_BYTES * 3 // 4
LANES = 128
SUBLANES = 8

N_DEVICES = 8
VEC_ROWS = 32
LOSS_ROW = 5


def _params(sem):
    return pltpu.CompilerParams(dimension_semantics=sem, vmem_limit_bytes=VMEM_LIMIT)


_ANY = pl.BlockSpec(memory_space=pl.ANY)
_NN = (((1,), (0,)), ((), ()))
_NT = (((1,), (1,)), ((), ()))
_TN = (((0,), (0,)), ((), ()))


def _after_operand(after):
    return [] if after is None else [after]


def _fit(t, n):
    if n <= t:
        return n
    c = (t // LANES) * LANES
    while n % c:
        c -= LANES
    return c


MM_VMEM_BUDGET = VMEM_LIMIT * 3 // 4
def _mm_tiles(M, K, ns, n_total, out_dtypes, extras, whole_rows):
    for tm in (2048, 1024, 512, 256, 128):
        for tn in ((ns,) if whole_rows else (1024, 512, 256)):
            tn = _fit(tn, ns)
            per_row = 2 * (2 * K) + 4 * tn + sum(2 * tn * jnp.dtype(d).itemsize for d in out_dtypes)
            per_row += sum(2 * tn * e.dtype.itemsize for e in extras)
            b_buffers = 1 if tn == n_total else 2
            if M % tm == 0 and b_buffers * (2 * K * tn) + tm * per_row <= MM_VMEM_BUDGET:
                return tm, tn
    raise ValueError(f"no tile fits VMEM for M={M} K={K} N={ns}")


def _mm(a, b, *, mode, name, out_dtypes=(F32,), b_shard=False, o_shard=False, extras=(), epi=None, after=None,
        bcast=(), accs=(), ref_epi=None):
    if mode == "tn":
        K, M = a.shape
        N = b.shape[1]
    else:
        M, K = a.shape
        if mode == "nn":
            N = b.shape[0] * b.shape[2] if b_shard else b.shape[1]
        else:
            N = b.shape[1] if b_shard else b.shape[0]
    ns = N
    if b_shard and mode == "nn":
        ns = b.shape[2]
    elif o_shard:
        ns = N // N_CHIPS
    tm, tn = _mm_tiles(M, K, ns, N, out_dtypes, extras, whole_rows=ref_epi is not None)
    if ref_epi is not None:
        tm = min(tm, 512)
    grid = (M // tm, N // tn)
    q = ns // tn
    once = dict(pipeline_mode=pl.Buffered(1)) if tn == N else {}

    if mode == "tn":
        a_spec = pl.BlockSpec((K, tm), lambda i, j: (0, i))
        b_spec = pl.BlockSpec((K, tn), lambda i, j: (0, j), **once)
        dims = _TN
    elif mode == "nn":
        a_spec = pl.BlockSpec((tm, K), lambda i, j: (i, 0))
        if b_shard:
            b_spec = pl.BlockSpec((None, K, tn), lambda i, j: (j // q, 0, j % q), **once)
        else:
            b_spec = pl.BlockSpec((K, tn), lambda i, j: (0, j), **once)
        dims = _NN
    else:
        a_spec = pl.BlockSpec((tm, K), lambda i, j: (i, 0))
        if b_shard:
            ks = b.shape[2]
            b_spec = pl.BlockSpec((N_CHIPS, tn, ks), lambda i, j: (0, j, 0), **once)
        else:
            b_spec = pl.BlockSpec((tn, K), lambda i, j: (j, 0), **once)
        dims = _NT

    if o_shard:
        o_specs = [pl.BlockSpec((None, tm, tn), lambda i, j: (j // q, i, j % q))]
        o_shapes = [jax.ShapeDtypeStruct((N_CHIPS, M, ns), out_dtypes[0])]
    else:
        o_specs = [pl.BlockSpec((tm, tn), lambda i, j: (i, j)) for _ in out_dtypes]
        o_shapes = [jax.ShapeDtypeStruct((M, N), dt) for dt in out_dtypes]
    e_specs = [pl.BlockSpec((tm, tn), lambda i, j: (i, j)) for _ in extras]
    e_specs += [pl.BlockSpec(v.shape, lambda i, j: (0, 0)) for v in bcast]
    o_specs += [pl.BlockSpec(s, lambda i, j: (0, 0)) for s in accs]
    o_shapes += [jax.ShapeDtypeStruct(s, F32) for s in accs]
    n_e, n_b, n_o, n_a = len(extras), len(bcast), len(out_dtypes), len(accs)
    order = _after_operand(after)
    n_x = len(order)
    if epi is None:
        epi = lambda acc: (acc,)

    def body(a_ref, b_ref, *rest):
        e_refs, b_refs = rest[:n_e], rest[n_e:n_e + n_b]
        o_refs = rest[n_e + n_b + n_x:n_e + n_b + n_x + n_o]
        a_refs = rest[n_e + n_b + n_x + n_o:]
        if n_a:
            @pl.when((pl.program_id(0) == 0) & (pl.program_id(1) == 0))
            def _():
                for r in a_refs:
                    r[...] = jnp.zeros(r.shape, F32)
        if mode == "nt" and b_shard:
            acc = None
            for s in range(N_CHIPS):
                part = lax.dot_general(a_ref[:, s * ks:(s + 1) * ks], b_ref[s], dims, preferred_element_type=F32)
                acc = part if acc is None else acc + part
        else:
            acc = lax.dot_general(a_ref[...], b_ref[...], dims, preferred_element_type=F32)
        if ref_epi is not None:
            ref_epi(acc, e_refs, b_refs, o_refs, a_refs)
            return
        outs = epi(acc, *[r[...] for r in e_refs])
        for r, o in zip(o_refs, outs):
            r[...] = o.astype(r.dtype)

    outs = pl.pallas_call(
        body, name=name, grid=grid, in_specs=[a_spec, b_spec] + e_specs + [_ANY] * n_x, out_specs=o_specs,
        out_shape=o_shapes, compiler_params=_params(("arbitrary", "arbitrary") if n_a else ("parallel", "parallel")),
    )(a, b, *extras, *bcast, *order)
    return outs[0] if n_o + n_a == 1 else outs


def _rowwise(fn, rows, bcast, outs, accs=(), *, tm, name, after=None):
    def norm(r):
        return r if isinstance(r, tuple) else (r, r.shape[1], 0)

    rows = [norm(r) for r in rows]
    T = rows[0][0].shape[0]
    tm = min(tm, T)
    while T % tm:
        tm -= SUBLANES
    n_r, n_b, n_o, n_a = len(rows), len(bcast), len(outs), len(accs)
    order = _after_operand(after)
    n_x = len(order)
    in_specs = [pl.BlockSpec((tm, c), functools.partial(lambda i, cb: (i, cb), cb=cb)) for _, c, cb in rows]
    in_specs += [pl.BlockSpec(b.shape, lambda i: (0, 0)) for b in bcast] + [_ANY] * n_x
    out_specs = [pl.BlockSpec((tm, o[0]), lambda i: (i, 0)) for o in outs]
    out_specs += [pl.BlockSpec(s, lambda i: (0, 0)) for s in accs]
    out_shape = [jax.ShapeDtypeStruct((T, o[2] if len(o) > 2 else o[0]), o[1]) for o in outs]
    out_shape += [jax.ShapeDtypeStruct(s, F32) for s in accs]

    def body(*refs):
        in_refs = refs[:n_r]
        b_refs = refs[n_r:n_r + n_b]
        o_refs = refs[n_r + n_b + n_x:n_r + n_b + n_x + n_o]
        a_refs = refs[n_r + n_b + n_x + n_o:]
        if n_a:
            @pl.when(pl.program_id(0) == 0)
            def _():
                for r in a_refs:
                    r[...] = jnp.zeros(r.shape, F32)
        fn(in_refs, b_refs, o_refs, a_refs)

    res = pl.pallas_call(
        body, name=name, grid=(T // tm,), in_specs=in_specs, out_specs=out_specs, out_shape=out_shape,
        compiler_params=_params(("arbitrary",) if n_a else ("parallel",)),
    )(*[r[0] for r in rows], *bcast, *order)
    return res


def _rsum(x):
    return jnp.sum(x, axis=0, keepdims=True)


def _rms_fwd(x, g, name, after=None):
    def fn(ins, bs, outs, accs):
        xv = ins[0][...]
        r = lax.rsqrt(jnp.mean(xv * xv, axis=-1, keepdims=True) + EPS)
        outs[0][...] = (xv * r * bs[0][...]).astype(BF16)

    return _rowwise(fn, [x], [g], [(D_MODEL, BF16)], tm=512, name=name, after=after)[0]


def _rms_bwd_math(xv, dh, g):
    r = lax.rsqrt(jnp.mean(xv * xv, axis=-1, keepdims=True) + EPS)
    hn = xv * r
    dgh = dh * g
    dx = r * (dgh - hn * jnp.mean(dgh * hn, axis=-1, keepdims=True))
    return dx, _rsum(dh * hn)


def _mm_norm_bwd(dy, w, x, dres, g, name, after=None):
    def epilogue(acc, e_refs, b_refs, o_refs, a_refs):
        dx, dg = _rms_bwd_math(e_refs[0][...], acc, b_refs[0][...])
        dx = dx + e_refs[1][...]
        o_refs[0][...] = dx
        o_refs[1][...] = dx.astype(BF16)
        a_refs[0][...] += dg

    return _mm(dy, w, mode="nt", b_shard=True, out_dtypes=(F32, BF16), extras=(x, dres), bcast=(g,),
               accs=((1, D_MODEL),), ref_epi=epilogue, name=name, after=after)


def _mm_res_norm(a, w, res, g, name):
    def epilogue(acc, e_refs, b_refs, o_refs, a_refs):
        xv = acc + e_refs[0][...]
        o_refs[0][...] = xv
        r = lax.rsqrt(jnp.mean(xv * xv, axis=-1, keepdims=True) + EPS)
        o_refs[1][...] = (xv * r * b_refs[0][...]).astype(BF16)

    return _mm(a, w, mode="nn", out_dtypes=(F32, BF16), extras=(res,), bcast=(g,), ref_epi=epilogue, name=name)


def _mm_final_loss(a, w, res, target, g, name):
    def epilogue(acc, e_refs, b_refs, o_refs, a_refs):
        xv = acc + e_refs[0][...]
        gv = b_refs[0][...]
        r = lax.rsqrt(jnp.mean(xv * xv, axis=-1, keepdims=True) + EPS)
        e = xv * r * gv - e_refs[1][...]
        tok = jnp.mean(e * e, axis=-1, keepdims=True)
        a_refs[0][...] += 0.5 * jnp.sum(tok, axis=0, keepdims=True) * jnp.ones((1, LANES), F32)
        dx, dg = _rms_bwd_math(xv, e * (1.0 / D_MODEL), gv)
        o_refs[0][...] = dx
        o_refs[1][...] = dx.astype(BF16)
        a_refs[1][...] += dg

    return _mm(a, w, mode="nn", out_dtypes=(F32, BF16), extras=(res, target), bcast=(g,),
               accs=((1, LANES), (1, D_MODEL)), ref_epi=epilogue, name=name)


def _relu2(acc):
    r = jnp.maximum(acc, 0.0)
    return r * r, r


def _mlp_fwd(x, h, fetch, tag, finish):
    w_up, w_down = fetch(f"mlp{tag}", h)
    a, r = _mm(h, w_up, mode="nn", b_shard=True, out_dtypes=(BF16, BF16), epi=_relu2, name=f"mlp{tag}_up")
    return finish(a, w_down, x, f"mlp{tag}_down"), (h, a, r, w_up, w_down)


def _mlp_bwd(x, g, saved, dx, dx_bf, tag, after):
    h, a, r, w_up, w_down = saved
    d_down = _mm(a, dx_bf, mode="tn", out_dtypes=(BF16,), name=f"mlp{tag}_dwdown", after=after)
    dup = _mm(dx_bf, w_down, mode="nt", extras=(r,), out_dtypes=(BF16,),
              epi=lambda acc, rv: (acc * (2.0 * rv.astype(F32)),), name=f"mlp{tag}_dup")
    d_up = _mm(h, dup, mode="tn", o_shard=True, out_dtypes=(BF16,), name=f"mlp{tag}_dwup")
    dx_new, dx_new_bf, dg = _mm_norm_bwd(dup, w_up, x, dx, g, f"mlp{tag}_dh")
    return dx_new, dx_new_bf, dg, d_up, d_down


def _rope_tables(L, B):
    rows = L // GRID_W
    row = np.repeat(np.arange(rows, dtype=np.float32), GRID_W)
    col = np.tile(np.arange(GRID_W, dtype=np.float32), rows)
    inv = (ROPE_THETA ** (-np.arange(HEAD_DIM // 4, dtype=np.float32) / (HEAD_DIM // 4))).astype(np.float32)
    ar, ac = row[:, None] * inv, col[:, None] * inv
    cos = np.concatenate([np.cos(ar), np.cos(ar), np.cos(ac), np.cos(ac)], axis=-1)
    sin = np.concatenate([-np.sin(ar), np.sin(ar), -np.sin(ac), np.sin(ac)], axis=-1)
    return jnp.asarray(np.tile(cos, (B, 1)), F32), jnp.asarray(np.tile(sin, (B, 1)), F32)


def _swap_halves(x):
    lane = lax.broadcasted_iota(jnp.int32, x.shape, 1)
    return jnp.where((lane % 64) < 32, pltpu.roll(x, HEAD_DIM - 32, 1), pltpu.roll(x, 32, 1))


def _qk_prep(qkv, cos, sin, q_g, k_g):
    def fn(ins, bs, outs, accs):
        c, s = ins[1][...], ins[2][...]
        for h in range(N_HEADS + N_KV):
            xv = ins[0][:, h * HEAD_DIM:(h + 1) * HEAD_DIM]
            g = bs[0][...] if h < N_HEADS else bs[1][...]
            r = lax.rsqrt(jnp.mean(xv * xv, axis=-1, keepdims=True) + EPS)
            z = xv * r * g
            y = (z * c + _swap_halves(z) * s).astype(BF16)
            if h < N_HEADS:
                outs[0][:, h * HEAD_DIM:(h + 1) * HEAD_DIM] = y
            else:
                outs[1][:, (h - N_HEADS) * HEAD_DIM:(h - N_HEADS + 1) * HEAD_DIM] = y
        outs[2][...] = ins[0][:, (N_HEADS + N_KV) * HEAD_DIM:].astype(BF16)

    kvw = N_KV * HEAD_DIM
    return _rowwise(fn, [qkv, cos, sin], [q_g, k_g], [(D_MODEL, BF16), (kvw, BF16), (kvw, BF16)], tm=512,
                    name="attn_qk_prep")


def _qk_prep_bwd(qkv, dq, dk, dv, cos, sin, q_g, k_g):
    def fn(ins, bs, outs, accs):
        c, s = ins[4][...], ins[5][...]
        for h in range(N_HEADS + N_KV):
            sl = slice(h * HEAD_DIM, (h + 1) * HEAD_DIM)
            xv = ins[0][:, sl]
            if h < N_HEADS:
                g, dy, acc = bs[0][...], ins[1][:, sl], accs[0]
            else:
                ks = slice((h - N_HEADS) * HEAD_DIM, (h - N_HEADS + 1) * HEAD_DIM)
                g, dy, acc = bs[1][...], ins[2][:, ks], accs[1]
            r = lax.rsqrt(jnp.mean(xv * xv, axis=-1, keepdims=True) + EPS)
            xn = xv * r
            dz = dy * c - _swap_halves(dy) * s
            acc[...] += _rsum(dz * xn)
            dxn = dz * g
            outs[0][:, sl] = (r * (dxn - xn * jnp.mean(dxn * xn, axis=-1, keepdims=True))).astype(BF16)
        outs[0][:, (N_HEADS + N_KV) * HEAD_DIM:] = ins[3][...].astype(BF16)

    return _rowwise(fn, [qkv, dq, dk, dv, cos, sin], [q_g, k_g], [(qkv.shape[1], BF16)],
                    [(1, HEAD_DIM), (1, HEAD_DIM)], tm=256, name="attn_qk_prep_bwd")


_EXP2_SCALE = SCALE * math.log2(math.e)


def _exp_rows(q, k):
    s = lax.dot_general(q, k, _NT, preferred_element_type=F32)
    p = jnp.exp2((s - jnp.max(s, axis=-1, keepdims=True)) * _EXP2_SCALE)
    return p, jnp.sum(p, axis=-1, keepdims=True)


def _attn_fwd(q, k, v, B, L, tq=2048, sub=256):
    tq = min(tq, L)
    sub = min(sub, tq)
    nq = L // tq

    def body(q_ref, k_ref, v_ref, o_ref):
        kv, vv = k_ref[...], v_ref[...]
        for c in range(tq // sub):
            rows = slice(c * sub, (c + 1) * sub)
            p, l = _exp_rows(q_ref[rows, :], kv)
            o = jnp.dot(p.astype(BF16), vv, preferred_element_type=F32)
            o_ref[rows, :] = (o * (1.0 / l)).astype(o_ref.dtype)

    return pl.pallas_call(
        body, name="attn_fwd", grid=(B, N_HEADS, nq),
        in_specs=[pl.BlockSpec((tq, HEAD_DIM), lambda b, h, i: (b * nq + i, h)),
                  pl.BlockSpec((L, HEAD_DIM), lambda b, h, i: (b, h // GROUP)),
                  pl.BlockSpec((L, HEAD_DIM), lambda b, h, i: (b, h // GROUP))],
        out_specs=pl.BlockSpec((tq, HEAD_DIM), lambda b, h, i: (b * nq + i, h)),
        out_shape=jax.ShapeDtypeStruct((B * L, D_MODEL), BF16),
        compiler_params=_params(("parallel", "parallel", "parallel")),
    )(q, k, v)


def _attn_bwd(q, k, v, do, B, L, tq=2048, sub=512):
    tq = min(tq, L)
    sub = min(sub, tq)
    nq = L // tq

    def body(q_ref, k_ref, v_ref, do_ref, dq_ref, dk_ref, dv_ref):
        @pl.when((pl.program_id(2) == 0) & (pl.program_id(3) == 0))
        def _():
            dk_ref[...] = jnp.zeros(dk_ref.shape, F32)
            dv_ref[...] = jnp.zeros(dv_ref.shape, F32)

        kv, vv = k_ref[...], v_ref[...]
        ps, es, dos, qs = [], [], [], []
        for c in range(tq // sub):
            rows = slice(c * sub, (c + 1) * sub)
            qc, doc = q_ref[rows, :], do_ref[rows, :]
            p, l = _exp_rows(qc, kv)
            inv = 1.0 / l
            dp = lax.dot_general(doc, vv, _NT, preferred_element_type=F32)
            delta = jnp.sum(p * dp, axis=-1, keepdims=True) * inv
            e = (p * (dp - delta)).astype(BF16)
            dq_ref[rows, :] = jnp.dot(e, kv, preferred_element_type=F32) * (inv * SCALE)
            ps.append(p.astype(BF16))
            es.append(e)
            dos.append((doc.astype(F32) * inv).astype(BF16))
            qs.append((qc.astype(F32) * (inv * SCALE)).astype(BF16))
        cat = lambda xs: xs[0] if len(xs) == 1 else jnp.concatenate(xs, axis=0)
        dv_ref[...] += lax.dot_general(cat(ps), cat(dos), _TN, preferred_element_type=F32)
        dk_ref[...] += lax.dot_general(cat(es), cat(qs), _TN, preferred_element_type=F32)

    qmap = lambda b, kh, g, i: (b * nq + i, kh * GROUP + g)
    kmap = lambda b, kh, g, i: (b, kh)
    kvw = N_KV * HEAD_DIM
    return pl.pallas_call(
        body, name="attn_bwd", grid=(B, N_KV, GROUP, nq),
        in_specs=[pl.BlockSpec((tq, HEAD_DIM), qmap), pl.BlockSpec((L, HEAD_DIM), kmap),
                  pl.BlockSpec((L, HEAD_DIM), kmap), pl.BlockSpec((tq, HEAD_DIM), qmap)],
        out_specs=[pl.BlockSpec((tq, HEAD_DIM), qmap), pl.BlockSpec((L, HEAD_DIM), kmap),
                   pl.BlockSpec((L, HEAD_DIM), kmap)],
        out_shape=[jax.ShapeDtypeStruct((B * L, D_MODEL), F32), jax.ShapeDtypeStruct((B * L, kvw), F32),
                   jax.ShapeDtypeStruct((B * L, kvw), F32)],
        compiler_params=_params(("parallel", "parallel", "arbitrary", "arbitrary")),
    )(q, k, v, do)


def _conv_shift(x, t, L, k):
    if k == 2:
        return x
    if k < 2:
        return jnp.where(t >= 2 - k, pltpu.roll(x, 2 - k, 0), 0.0)
    return jnp.where(t < L - (k - 2), pltpu.roll(x, L - (k - 2), 0), 0.0)


def _conv_fwd(z, wb, B, L, tc=256):
    noff = D_MODEL // tc

    def body(z_ref, w_ref, o_ref):
        x = z_ref[...]
        t = lax.broadcasted_iota(jnp.int32, x.shape, 0)
        acc = w_ref[4:5, :] + w_ref[2:3, :] * x
        for k in (0, 1, 3):
            acc = acc + w_ref[k:k + 1, :] * _conv_shift(x, t, L, k)
        o_ref[...] = acc

    return pl.pallas_call(
        body, name="rg_conv", grid=(B, noff),
        in_specs=[pl.BlockSpec((L, tc), lambda b, j: (b, noff + j)), pl.BlockSpec((SUBLANES, tc), lambda b, j: (0, j))],
        out_specs=pl.BlockSpec((L, tc), lambda b, j: (b, j)),
        out_shape=jax.ShapeDtypeStruct((B * L, D_MODEL), F32),
        compiler_params=_params(("parallel", "parallel")),
    )(z, wb)


def _conv_bwd(z, g, wb, dz, B, L, tc=256, after=None):
    noff = D_MODEL // tc
    order = _after_operand(after)

    def body(z_ref, g_ref, w_ref, dz_in, *rest):
        dx_ref, dw_ref = rest[len(order):]

        @pl.when(pl.program_id(1) == 0)
        def _():
            dw_ref[...] = jnp.zeros(dw_ref.shape, F32)

        x, gv = z_ref[...], g_ref[...]
        t = lax.broadcasted_iota(jnp.int32, x.shape, 0)
        dx = w_ref[2:3, :] * gv
        for k in (0, 1, 3):
            dx = dx + w_ref[k:k + 1, :] * _conv_shift(gv, t, L, 4 - k)
        dx_ref[...] = dx.astype(BF16)
        for k in range(4):
            dw_ref[k:k + 1, :] += _rsum(_conv_shift(x, t, L, k) * gv)
        dw_ref[4:5, :] += _rsum(gv)

    return pl.pallas_call(
        body, name="rg_conv_bwd", grid=(noff, B),
        in_specs=[pl.BlockSpec((L, tc), lambda j, b: (b, noff + j)), pl.BlockSpec((L, tc), lambda j, b: (b, j)),
                  pl.BlockSpec((SUBLANES, tc), lambda j, b: (0, j)), _ANY] + [_ANY] * len(order),
        out_specs=[pl.BlockSpec((L, tc), lambda j, b: (b, noff + j)),
                   pl.BlockSpec((SUBLANES, tc), lambda j, b: (0, j))],
        out_shape=[jax.ShapeDtypeStruct(dz.shape, dz.dtype), jax.ShapeDtypeStruct((SUBLANES, D_MODEL), F32)],
        input_output_aliases={3: 0},
        compiler_params=_params(("parallel", "arbitrary")),
    )(z, g, wb, dz, *order)


def _softplus(x):
    return jnp.maximum(x, 0.0) + jnp.log1p(jnp.exp(-jnp.abs(x)))


_ROW_BA, _ROW_BX, _ROW_LAM = 0, 2, 4


def _gate_math(xb, pre, vec_ref, d, sl):
    pa = pre[:, (2 * d) * LRU_BW:(2 * d + 1) * LRU_BW] + vec_ref[_ROW_BA + d:_ROW_BA + d + 1, sl]
    px = pre[:, (2 * d + 1) * LRU_BW:(2 * d + 2) * LRU_BW] + vec_ref[_ROW_BX + d:_ROW_BX + d + 1, sl]
    r = 0.5 * jnp.tanh(0.5 * pa) + 0.5
    i = 0.5 * jnp.tanh(0.5 * px) + 0.5
    sp = _softplus(-vec_ref[_ROW_LAM + d:_ROW_LAM + d + 1, sl])
    log_a = (-RG_C) * r * sp
    a = jnp.exp(log_a)
    th = jnp.tanh(log_a)
    om = -2.0 * th / (1.0 - th)
    mult = jnp.sqrt(om)
    return a, mult * (i * xb), (r, i, sp, om, mult)


def _gate_fwd(rec, wcat, gvec):
    def fn(ins, bs, outs, accs):
        for blk in range(LRU_BLOCKS):
            sl = slice(blk * LRU_BW, (blk + 1) * LRU_BW)
            xb = ins[0][:, sl]
            pre = jnp.dot(xb.astype(BF16), bs[0][sl, :], preferred_element_type=F32)
            for d in range(2):
                a, u, _ = _gate_math(xb, pre, bs[1], d, sl)
                outs[2 * d][:, sl] = a
                outs[2 * d + 1][:, sl] = u

    return _rowwise(fn, [rec], [wcat, gvec], [(D_MODEL, F32)] * 4, tm=256, name="rg_gate")


def _gate_bwd(rec, du_f, da_f, du_b, da_b, wcat, gvec):
    def fn(ins, bs, outs, accs):
        for blk in range(LRU_BLOCKS):
            sl = slice(blk * LRU_BW, (blk + 1) * LRU_BW)
            xb = ins[0][:, sl]
            xb16 = xb.astype(BF16)
            w = bs[0][sl, :]
            pre = jnp.dot(xb16, w, preferred_element_type=F32)
            dx = jnp.zeros_like(xb)
            dpre = []
            for d in range(2):
                a, _, (r, i, sp, om, mult) = _gate_math(xb, pre, bs[1], d, sl)
                du, da = ins[1 + 2 * d][:, sl], ins[2 + 2 * d][:, sl]
                d_i = du * mult * xb
                d_mult = du * i * xb
                dx = dx + du * mult * i
                dlog = da * a - d_mult * (1.0 - om) / mult
                d_r = dlog * ((-RG_C) * sp)
                d_sp = _rsum(dlog * ((-RG_C) * r))
                lam = bs[1][_ROW_LAM + d:_ROW_LAM + d + 1, sl]
                accs[2][_ROW_LAM + d:_ROW_LAM + d + 1, sl] += d_sp * (-jax.nn.sigmoid(-lam))
                dpa = d_r * r * (1.0 - r)
                dpx = d_i * i * (1.0 - i)
                accs[2][_ROW_BA + d:_ROW_BA + d + 1, sl] += _rsum(dpa)
                accs[2][_ROW_BX + d:_ROW_BX + d + 1, sl] += _rsum(dpx)
                dpre += [dpa, dpx]
            dpre = jnp.concatenate(dpre, axis=1).astype(BF16)
            dw = lax.dot_general(xb16, dpre, _TN, preferred_element_type=F32)
            for d in range(2):
                rows = slice(d * D_MODEL + blk * LRU_BW, d * D_MODEL + (blk + 1) * LRU_BW)
                accs[0][rows, :] += dw[:, (2 * d) * LRU_BW:(2 * d + 1) * LRU_BW]
                accs[1][rows, :] += dw[:, (2 * d + 1) * LRU_BW:(2 * d + 2) * LRU_BW]
            outs[0][:, sl] = dx + lax.dot_general(dpre, w, _NT, preferred_element_type=F32)

    gate_shape = (2 * D_MODEL, LRU_BW)
    return _rowwise(fn, [rec, du_f, da_f, du_b, da_b], [wcat, gvec], [(D_MODEL, F32)],
                    [gate_shape, gate_shape, (SUBLANES, D_MODEL)], tm=256, name="rg_gate_bwd")


def _as_time_blocks(x):
    return x.reshape(x.shape[0] // SUBLANES, SUBLANES, x.shape[1])


def _scan_call(body, ins, n_out, B, L, tc, name):
    nb = L // SUBLANES
    spec = pl.BlockSpec((nb, SUBLANES, tc), lambda b, j: (b, 0, j))
    T = ins[0].shape[0]
    outs = pl.pallas_call(
        functools.partial(body, nb), name=name, grid=(B, D_MODEL // tc),
        in_specs=[spec] * len(ins), out_specs=[spec] * n_out,
        out_shape=[jax.ShapeDtypeStruct((T // SUBLANES, SUBLANES, D_MODEL), F32)] * n_out,
        compiler_params=_params(("parallel", "parallel")),
    )(*[_as_time_blocks(x) for x in ins])
    return [o.reshape(T, D_MODEL) for o in outs]


def _block_scan(A, U, reverse):
    row = lax.broadcasted_iota(jnp.int32, A.shape, 0)
    for s in (1, 2, 4):
        shift = SUBLANES - s if reverse else s
        valid = (row < SUBLANES - s) if reverse else (row >= s)
        a_sh = jnp.where(valid, pltpu.roll(A, shift, 0), 1.0)
        u_sh = jnp.where(valid, pltpu.roll(U, shift, 0), 0.0)
        U = A * u_sh + U
        A = A * a_sh
    return A, U


_LAST = SUBLANES - 1
SCAN_UNROLL = 8


def _loop_blocks(nb, step, init):
    def group(g, carry):
        for k in range(SCAN_UNROLL):
            carry = step(g * SCAN_UNROLL + k, carry)
        return carry

    return lax.fori_loop(0, nb // SCAN_UNROLL, group, init)


def _scan_fwd(a_f, u_f, a_b, u_b, B, L, tc=256):
    def body(nb, af, uf, ab, ub, hf, hb):
        def step(i, carry):
            c1, c2 = carry
            ib = nb - 1 - i
            p, h = _block_scan(af[i], uf[i], False)
            h = h + p * c1
            hf[i] = h
            p2, h2 = _block_scan(ab[ib], ub[ib], True)
            h2 = h2 + p2 * c2
            hb[ib] = h2
            return h[_LAST:, :], h2[:1, :]

        zero = jnp.zeros((1, tc), F32)
        _loop_blocks(nb, step, (zero, zero))

    return _scan_call(body, [a_f, u_f, a_b, u_b], 2, B, L, tc, "rg_scan")


def _scan_bwd(dy, a_f, h_f, a_b, h_b, B, L, tc=256):
    def body(nb, dy_r, af, hf, ab, hb, duf, daf, dub, dab):
        def step(i, carry):
            c1, c2 = carry
            ir = nb - 1 - i
            row = lax.broadcasted_iota(jnp.int32, (SUBLANES, tc), 0)
            a_up = jnp.where(row == _LAST, af[jnp.minimum(ir + 1, nb - 1), :1, :], pltpu.roll(af[ir], _LAST, 0))
            p, lam = _block_scan(a_up, dy_r[ir], True)
            lam = lam + p * c1
            before = hf[jnp.maximum(ir - 1, 0), _LAST:, :] * (ir > 0).astype(F32)
            duf[ir] = lam
            daf[ir] = lam * jnp.where(row == 0, before, pltpu.roll(hf[ir], 1, 0))
            a_dn = jnp.where(row == 0, ab[jnp.maximum(i - 1, 0), _LAST:, :], pltpu.roll(ab[i], 1, 0))
            p2, lam2 = _block_scan(a_dn, dy_r[i], False)
            lam2 = lam2 + p2 * c2
            after = hb[jnp.minimum(i + 1, nb - 1), :1, :] * (i < nb - 1).astype(F32)
            dub[i] = lam2
            dab[i] = lam2 * jnp.where(row == _LAST, after, pltpu.roll(hb[i], _LAST, 0))
            return lam[:1, :], lam2[_LAST:, :]

        zero = jnp.zeros((1, tc), F32)
        _loop_blocks(nb, step, (zero, zero))

    return _scan_call(body, [dy, a_f, h_f, a_b, h_b], 4, B, L, tc, "rg_scan_bwd")


_GELU_C = math.sqrt(2.0 / math.pi)


def _gelu_parts(x):
    th = jnp.tanh(_GELU_C * (x + 0.044715 * x * x * x))
    return 0.5 * x * (1.0 + th), th


def _gated_out(h_f, h_b, z):
    def fn(ins, bs, outs, accs):
        gl, _ = _gelu_parts(ins[2][...])
        outs[0][...] = ((ins[0][...] + ins[1][...]) * gl).astype(BF16)

    return _rowwise(fn, [h_f, h_b, (z, D_MODEL, 0)], [], [(D_MODEL, BF16)], tm=512, name="rg_gated_out")[0]


def _gated_out_bwd(dyg, h_f, h_b, z):
    def fn(ins, bs, outs, accs):
        x = ins[3][...]
        gl, th = _gelu_parts(x)
        dgl = 0.5 * (1.0 + th) + 0.5 * x * (1.0 - th * th) * (_GELU_C * (1.0 + 3.0 * 0.044715 * x * x))
        g = ins[0][...]
        outs[0][...] = g * gl
        outs[1][...] = (g * (ins[1][...] + ins[2][...]) * dgl).astype(BF16)

    return _rowwise(fn, [dyg, h_f, h_b, (z, D_MODEL, 0)], [], [(D_MODEL, F32), (D_MODEL, BF16, 2 * D_MODEL)], tm=512,
                    name="rg_gated_out_bwd")


def _make_wcat(w_a, w_x):
    g = jnp.stack([w_a[0, 0], w_x[0, 0], w_a[0, 1], w_x[0, 1]])
    return jnp.transpose(g, (1, 2, 0, 3)).reshape(D_MODEL, 4 * LRU_BW)


def _rows_at(part, first):
    return jnp.pad(part, ((first, SUBLANES - first - part.shape[0]), (0, 0)))


def _qk_slot(q_g, k_g):
    wide = lambda v, at: jnp.pad(v, ((0, SUBLANES - 1), (at, D_MODEL - at - HEAD_DIM)))
    return wide(q_g, 0) + wide(k_g, HEAD_DIM)


def _local_step(x, target, P, fetch, emit, B, L, after=None):
    g_mix, g_mlp = P["norm_mix_g"], P["norm_mlp_g"]
    h0 = _rms_fwd(x, g_mix[0:1], "rg_norm", after=after)
    w_in, w_out, conv_wb, wcat, gvec = fetch("rg", h0)
    z = _mm(h0, w_in, mode="nn", b_shard=True, name="rg_in")
    rec = _conv_fwd(z, conv_wb, B, L)
    a_f, u_f, a_b, u_b = _gate_fwd(rec, wcat, gvec)
    h_f, h_b = _scan_fwd(a_f, u_f, a_b, u_b, B, L)
    yg = _gated_out(h_f, h_b, z)
    x1, h1 = _mm_res_norm(yg, w_out, x, g_mlp[0:1], "rg_out")
    (x2, h3), mlp0 = _mlp_fwd(x1, h1, fetch, 0, lambda a, w, res, name: _mm_res_norm(a, w, res, g_mix[1:2], name))
    w_qkv, w_o = fetch("att", h3)
    qkv = _mm(h3, w_qkv, mode="nn", b_shard=True, name="attn_qkv")
    cos, sin = _rope_tables(L, B)
    qh, kh, vh = _qk_prep(qkv, cos, sin, P["q_g"], P["k_g"])
    o = _attn_fwd(qh, kh, vh, B, L)
    x3, h4 = _mm_res_norm(o, w_o, x2, g_mlp[1:2], "attn_out")
    (dx4, dx4_bf, loss_acc, d_final_g), mlp1 = _mlp_fwd(
        x3, h4, fetch, 1, lambda a, w, res, name: _mm_final_loss(a, w, res, target, P["final_g"], name))

    dx3, dx3_bf, dg_mlp1, d_up1, d_down1 = _mlp_bwd(x3, g_mlp[1:2], mlp1, dx4, dx4_bf, 1, None)
    tok = emit("mlp1", [d_up1, d_down1])
    d_wo = _mm(o, dx3_bf, mode="tn", out_dtypes=(BF16,), name="attn_dwo", after=tok)
    do = _mm(dx3_bf, w_o, mode="nt", out_dtypes=(BF16,), name="attn_do")
    dq, dk, dv = _attn_bwd(qh, kh, vh, do, B, L)
    dqkv, dq_g, dk_g = _qk_prep_bwd(qkv, dq, dk, dv, cos, sin, P["q_g"], P["k_g"])
    d_wqkv = _mm(h3, dqkv, mode="tn", o_shard=True, out_dtypes=(BF16,), name="attn_dwqkv")
    tok = emit("att", [d_wqkv, d_wo])
    dx2, dx2_bf, dg_mix1 = _mm_norm_bwd(dqkv, w_qkv, x2, dx3, g_mix[1:2], "attn_dh", after=tok)
    tok = emit("point_attn_done", [dx2_bf])
    dx1, dx1_bf, dg_mlp0, d_up0, d_down0 = _mlp_bwd(x1, g_mlp[0:1], mlp0, dx2, dx2_bf, 0, tok)
    tok = emit("mlp0", [d_up0, d_down0])
    d_wout = _mm(yg, dx1_bf, mode="tn", out_dtypes=(BF16,), name="rg_dwout", after=tok)
    tok = emit("rg_out", [d_wout])
    dyg = _mm(dx1_bf, w_out, mode="nt", name="rg_dyg", after=tok)
    dy, dgate = _gated_out_bwd(dyg, h_f, h_b, z)
    du_f, da_f, du_b, da_b = _scan_bwd(dy, a_f, h_f, a_b, h_b, B, L)
    drec_c, d_wa, d_wx, d_gvec = _gate_bwd(rec, du_f, da_f, du_b, da_b, wcat, gvec)
    tok = emit("gates", [d_wa, d_wx])
    dz, d_convwb = _conv_bwd(z, drec_c, conv_wb, dgate, B, L, after=tok)
    tok = emit("point_conv_done", [dz])
    d_win = _mm(h0, dz, mode="tn", o_shard=True, out_dtypes=(BF16,), name="rg_dwin", after=tok)
    tok = emit("rg_in", [d_win])
    grad_x, _, dg_mix0 = _mm_norm_bwd(dz, w_in, x, dx1, g_mix[0:1], "rg_dh", after=tok)

    norms = (_rows_at(dg_mix0, 0) + _rows_at(dg_mix1, 1) + _rows_at(dg_mlp0, 2) + _rows_at(dg_mlp1, 3)
             + _rows_at(d_final_g, 4)
             + jnp.pad(loss_acc, ((LOSS_ROW, SUBLANES - 1 - LOSS_ROW), (0, D_MODEL - LANES))))
    vec = jnp.concatenate([norms, d_convwb, d_gvec, _qk_slot(dq_g, dk_g)], axis=0)
    return grad_x, vec


_MESH = pl.DeviceIdType.MESH


def _place():
    x, y, c = lax.axis_index("x"), lax.axis_index("y"), lax.axis_index("c")
    peers = [((1 - x) if j & 2 else x, (1 - y) if j & 1 else y) for j in (1, 2, 3)]
    return x, y, c, peers


def _comm_call(body, ins, out_shapes, n_sem, name):
    return pl.pallas_call(
        body, name=name, in_specs=[_ANY] * len(ins), out_specs=[_ANY] * len(out_shapes), out_shape=out_shapes,
        scratch_shapes=[pltpu.SemaphoreType.DMA((n_sem,)), pltpu.SemaphoreType.DMA((n_sem,)),
                        pltpu.SemaphoreType.DMA((len(ins),))],
    )(*ins)


def _all_devices_slots(v, name):
    def body(v_ref, out_ref, send, recv, lsem):
        x, y, c = lax.axis_index("x"), lax.axis_index("y"), lax.axis_index("c")
        me = 4 * x + 2 * y + c

        def peer(j):
            return (1 - x) if j & 4 else x, (1 - y) if j & 2 else y, (1 - c) if j & 1 else c

        def copy(j, slot):
            return pltpu.make_async_remote_copy(
                src_ref=v_ref, dst_ref=out_ref.at[slot], send_sem=send.at[j - 1], recv_sem=recv.at[j - 1],
                device_id=peer(j), device_id_type=_MESH)

        local = pltpu.make_async_copy(v_ref, out_ref.at[me], lsem.at[0])
        sends = [copy(j, me) for j in range(1, N_DEVICES)]
        for cp in [local] + sends:
            cp.start()
        for j in range(1, N_DEVICES):
            px, py, pc = peer(j)
            copy(j, 4 * px + 2 * py + pc).wait_recv()
        for cp in sends:
            cp.wait_send()
        local.wait()

    shape = jax.ShapeDtypeStruct((N_DEVICES,) + v.shape, v.dtype)
    return _comm_call(body, [v], [shape], N_DEVICES - 1, name)[0]


def _sum_leading(slots, name):
    def body(s_ref, o_ref):
        acc = s_ref[0]
        for d in range(1, slots.shape[0]):
            acc = acc + s_ref[d]
        o_ref[...] = acc

    return pl.pallas_call(body, name=name, out_shape=jax.ShapeDtypeStruct(slots.shape[1:], slots.dtype))(slots)


_HBM = pl.BlockSpec(memory_space=pltpu.HBM)
_SEM = pl.BlockSpec(memory_space=pltpu.SEMAPHORE)
_EFFECT = pltpu.SideEffectType.DATAFLOW_SIDE_EFFECTING


_COPIES = dict(gather=N_CHIPS - 1, scatter=N_CHIPS - 1, swap=1)


def _split_copies(kind, srcs, lands, send, recv):
    x, y, c, peers = _place()
    me = 2 * x + y
    per = _COPIES[kind]
    out = []
    for a in range(len(lands)):
        for j in range(per):
            if kind == "swap":
                src, there, here, dev = srcs[a], lands[a], lands[a], (x, y, 1 - c)
            else:
                px, py = peers[j]
                dev = (px, py, c)
                if kind == "gather":
                    src, there, here = lands[a].at[me], lands[a].at[me], lands[a].at[2 * px + py]
                else:
                    src, there, here = srcs[a].at[2 * px + py], lands[a].at[j], lands[a].at[j]
            mk = functools.partial(
                pltpu.make_async_remote_copy, src_ref=src, send_sem=send.at[per * a + j],
                recv_sem=recv.at[per * a + j], device_id=dev, device_id_type=_MESH)
            out.append((functools.partial(mk, dst_ref=there), functools.partial(mk, dst_ref=here)))
    return out


def _exchange_start(kind, srcs, lands, name, after=None):
    arrays = list(srcs) + list(lands)
    n_s, n, n_all = len(srcs), len(lands), len(srcs) + len(lands)
    n_sem = _COPIES[kind] * n
    order = _after_operand(after)
    n_x = len(order)

    def body(*refs):
        send, recv = refs[n_all + n_x], refs[n_all + n_x + 1]
        token = refs[-1]
        for started, _ in _split_copies(kind, refs[:n_s], refs[n_s:n_all], send, recv):
            started().start()
        token[...] = jnp.zeros(token.shape, F32)

    res = pl.pallas_call(
        body, name=name,
        out_shape=(pltpu.SemaphoreType.DMA((n_sem,)), pltpu.SemaphoreType.DMA((n_sem,)),
                   *[pltpu.HBM(a.shape, a.dtype) for a in arrays], jax.ShapeDtypeStruct((SUBLANES, LANES), F32)),
        in_specs=[_HBM] * n_all + [_ANY] * n_x,
        out_specs=(_SEM, _SEM, *[_HBM] * n_all, pl.BlockSpec(memory_space=pltpu.VMEM)),
        input_output_aliases={i: 2 + i for i in range(n_all)},
        compiler_params=pltpu.CompilerParams(has_side_effects=_EFFECT),
    )(*[pltpu.with_memory_space_constraint(a, pltpu.HBM) for a in arrays], *order)
    return (res[0], res[1], res[2:2 + n_s], res[2 + n_s:2 + n_all]), res[-1]


def _exchange_wait(kind, handle, after, name):
    send, recv, srcs, lands = handle
    arrays = list(srcs) + list(lands)
    n_s, n_all = len(srcs), len(arrays)
    order = list(after) if isinstance(after, (list, tuple)) else [after]

    def body(*refs):
        for started, landing in _split_copies(kind, refs[:n_s], refs[n_s:n_all], refs[n_all], refs[n_all + 1]):
            started().wait_send()
            landing().wait_recv()

    res = pl.pallas_call(
        body, name=name, out_shape=[pltpu.HBM(a.shape, a.dtype) for a in arrays],
        in_specs=[_HBM] * n_all + [_SEM, _SEM] + [_ANY] * len(order), out_specs=[_HBM] * n_all,
        input_output_aliases={i: i for i in range(n_all)},
        compiler_params=pltpu.CompilerParams(has_side_effects=_EFFECT),
    )(*arrays, send, recv, *order)
    return res[:n_s], res[n_s:]


def _index_operand(i):
    return jnp.reshape(i, (1,)).astype(jnp.int32)


def _cast_into_slot(src, row0, rows, me, dtype, name, after=None, add=None):
    cols = src.shape[1]
    tm = min(512, rows)
    order = _after_operand(after)
    terms = [src] + ([] if add is None else [add])

    def body(me_ref, *rest):
        val = rest[0][...]
        if add is not None:
            val = val + rest[1][...]
        rest[-1][...] = val.astype(dtype)

    return pl.pallas_call(
        body, name=name,
        grid_spec=pltpu.PrefetchScalarGridSpec(
            num_scalar_prefetch=1, grid=(rows // tm,),
            in_specs=[pl.BlockSpec((tm, cols), lambda i, me_ref: (i + row0 // tm, 0))] * len(terms)
            + [_ANY] * len(order),
            out_specs=pl.BlockSpec((None, tm, cols), lambda i, me_ref: (me_ref[0], i, 0))),
        out_shape=jax.ShapeDtypeStruct((N_CHIPS, rows, cols), dtype), compiler_params=_params(("parallel",)),
    )(_index_operand(me), *terms, *order)


def _sum_slots(mine, r, me, name):
    _, rows, cols = r.shape
    tm = min(512, rows)

    def body(me_ref, own_ref, r_ref, o_ref):
        o_ref[...] = ((own_ref[...].astype(F32) + r_ref[0].astype(F32)) + r_ref[1].astype(F32)) + r_ref[2].astype(F32)

    return pl.pallas_call(
        body, name=name,
        grid_spec=pltpu.PrefetchScalarGridSpec(
            num_scalar_prefetch=1, grid=(rows // tm,),
            in_specs=[pl.BlockSpec((None, tm, cols), lambda i, me_ref: (me_ref[0], i, 0)),
                      pl.BlockSpec((N_CHIPS - 1, tm, cols), lambda i, me_ref: (0, i, 0))],
            out_specs=pl.BlockSpec((tm, cols), lambda i, me_ref: (i, 0))),
        out_shape=jax.ShapeDtypeStruct((rows, cols), F32), compiler_params=_params(("parallel",)),
    )(_index_operand(me), mine, r)


def _adamw(w, m, v, ps, qs, name):
    rows, cols = w.shape
    seg_rows = ps[0].shape[0]
    tm = min(256, seg_rows)
    while seg_rows % tm:
        tm -= SUBLANES
    per, n_seg = seg_rows // tm, len(ps)
    parts = list(ps) + ([] if qs is None else list(qs))

    def body(w_ref, m_ref, v_ref, *rest):
        g_refs, outs = rest[:len(parts)], rest[len(parts):]
        grad = lambda s: g_refs[s][...] if qs is None else g_refs[s][...] + g_refs[n_seg + s][...]
        g = grad(0)
        for s in range(1, n_seg):
            g = jnp.where(pl.program_id(0) >= s * per, grad(s), g)
        m1 = ADAM_B1 * m_ref[...] + (1.0 - ADAM_B1) * g
        v1 = ADAM_B2 * v_ref[...] + (1.0 - ADAM_B2) * (g * g)
        m_hat = m1 / (1.0 - ADAM_B1 ** ADAM_STEP)
        v_hat = v1 / (1.0 - ADAM_B2 ** ADAM_STEP)
        outs[0][...] = g
        outs[1][...] = (-ADAM_LR) * (m_hat / (jnp.sqrt(v_hat) + ADAM_EPS) + ADAM_WD * w_ref[...])
        outs[2][...] = m1
        outs[3][...] = v1

    row_spec = pl.BlockSpec((tm, cols), lambda i: (i, 0))
    seg_spec = lambda s: pl.BlockSpec((tm, cols), lambda i: (jnp.clip(i - s * per, 0, per - 1), 0))
    return pl.pallas_call(
        body, name=name, grid=(rows // tm,),
        in_specs=[row_spec] * 3 + [seg_spec(s) for s in range(n_seg)] * (1 if qs is None else 2),
        out_specs=[row_spec] * 4, out_shape=[jax.ShapeDtypeStruct((rows, cols), F32)] * 4,
        compiler_params=_params(("arbitrary",)),
    )(w, m, v, *parts)


def _put_cols(shard, me):
    full = jnp.zeros((shard.shape[0], D_MODEL), F32)
    return lax.dynamic_update_slice(full, shard, (0, me * (D_MODEL // N_CHIPS)))


def _gate_vec_slot(b_a, b_x, lam):
    return _rows_at(b_a, _ROW_BA) + _rows_at(b_x, _ROW_BX) + _rows_at(lam, _ROW_LAM)


def _pack_vec(p, me):
    return jnp.concatenate([
        _rows_at(p["norm_mix_g"], 0) + _rows_at(p["norm_mlp_g"], 2) + _rows_at(p["final_g"][None], 4),
        _rows_at(_put_cols(p["rg_conv_w"][0, :, 0, :], me), 0) + _rows_at(p["rg_conv_b"], 4),
        _gate_vec_slot(_put_cols(p["rg_b_a"][0], me), _put_cols(p["rg_b_x"][0], me), _put_cols(p["rg_lam"][0], me)),
        _qk_slot(p["at_q_g"], p["at_k_g"]),
    ], axis=0)


def _unpack_vec(r, me):
    def cols(rows):
        return lax.dynamic_slice(rows, (0, me * (D_MODEL // N_CHIPS)), (rows.shape[0], D_MODEL // N_CHIPS))

    gate = r[16:24]
    return dict(
        norm_mix_g=r[0:2], norm_mlp_g=r[2:4], final_g=r[4], rg_conv_w=cols(r[8:12])[None, :, None, :],
        rg_conv_b=r[12:13], rg_b_a=cols(gate[_ROW_BA:_ROW_BA + 2])[None], rg_b_x=cols(gate[_ROW_BX:_ROW_BX + 2])[None],
        rg_lam=cols(gate[_ROW_LAM:_ROW_LAM + 2])[None], at_q_g=r[24:25, 0:HEAD_DIM],
        at_k_g=r[24:25, HEAD_DIM:2 * HEAD_DIM])


_WEIGHTS = ['norm_mix_g', 'norm_mlp_g', 'rg_w_in', 'rg_conv_w', 'rg_conv_b', 'rg_w_a', 'rg_b_a', 'rg_w_x', 'rg_b_x',
            'rg_lam', 'rg_w_out', 'at_w_qkv', 'at_q_g', 'at_k_g', 'at_w_o', 'mlp_w_up', 'mlp_w_down', 'final_g']
_BIG = dict(rg_w_in=["rg_w_in"], rg_w_out=["rg_w_out"], at_w_qkv=["at_w_qkv"], at_w_o=["at_w_o"],
            mlp_w_up=["up0", "up1"], mlp_w_down=["down0", "down1"])


def kernel(x, *args):
    n_w = len(_WEIGHTS)
    w = dict(zip(_WEIGHTS, args[:n_w]))
    target = args[n_w]
    m = dict(zip(_WEIGHTS, args[n_w + 1:2 * n_w + 1]))
    v = dict(zip(_WEIGHTS, args[2 * n_w + 1:3 * n_w + 1]))
    B, L, _ = x.shape
    T = B * L
    me = 2 * lax.axis_index("x") + lax.axis_index("y")

    vec = jnp.concatenate([_gate_vec_slot(w["rg_b_a"][0], w["rg_b_x"][0], w["rg_lam"][0]),
                           _rows_at(w["rg_conv_w"][0, :, 0, :], 0)], axis=0)
    flat = lambda a: a.reshape(-1, a.shape[-1])
    rows_of = lambda k: w[k].shape[-2]
    groups = [("rg", [("rg_w_in", 0, BF16), ("rg_w_out", 0, BF16), (vec, 0, F32)]),
              ("mlp0", [("mlp_w_up", 0, BF16), ("mlp_w_down", 0, BF16)]),
              ("att", [("at_w_qkv", 0, BF16), ("at_w_o", 0, BF16)]),
              ("mlp1", [("mlp_w_up", 1, BF16), ("mlp_w_down", 1, BF16)])]
    gathers, tok = {}, None
    for group, members in groups:
        lands = []
        for n, (k, layer, dtype) in enumerate(members):
            src, rows = (flat(w[k]), rows_of(k)) if isinstance(k, str) else (k, k.shape[0])
            lands.append(_cast_into_slot(src, layer * rows, rows, me, dtype, f"place_{group}{n}", after=tok))
        gathers[group], tok = _exchange_start("gather", [], lands, f"gather_{group}_start", after=tok)
    wcat = _make_wcat(w["rg_w_a"], w["rg_w_x"]).astype(BF16)

    packs = [_pack_vec(p, me) for p in (w, m, v)]

    def fetch(group, after):
        order = [after, wcat] + packs if group == "rg" else after
        _, full = _exchange_wait("gather", gathers[group], order, f"gather_{group}_wait")
        if group == "rg":
            vec_full = jnp.transpose(full[2], (1, 0, 2)).reshape(2 * SUBLANES, D_MODEL)
            conv_wb = vec_full[SUBLANES:] + _rows_at(w["rg_conv_b"], 4)
            return full[0], full[1].reshape(D_MODEL, D_MODEL), conv_wb, wcat, vec_full[:SUBLANES]
        if group == "att":
            return full[0], full[1].reshape(D_MODEL, D_MODEL)
        return full[0], full[1].reshape(4 * D_MODEL, D_MODEL)

    names = dict(mlp1=["up1", "down1"], att=["at_w_qkv", "at_w_o"], mlp0=["up0", "down0"], rg_out=["rg_w_out"],
                 rg_in=["rg_w_in"], gates=["rg_w_a", "rg_w_x"])
    scatters, swaps, P, Q, res = {}, [], {}, {}, {}

    def start_scatter(group, grads):
        srcs = [g.reshape(N_CHIPS, -1, g.shape[-1]) for g in grads]
        lands = [lax.empty((N_CHIPS - 1,) + s.shape[1:], s.dtype) for s in srcs]
        scatters[group], token = _exchange_start("scatter", srcs, lands, f"scatter_{group}_start")
        return token

    def settle(groups, after):
        keys, parts = [], []
        for group in groups:
            srcs, lands = _exchange_wait("scatter", scatters[group], after, f"scatter_{group}_wait")
            for k, s, r in zip(names[group], srcs, lands):
                keys.append(k)
                parts.append(_sum_slots(s, r, me, f"sum_{k}"))
        handle, token = _exchange_start("swap", parts, [lax.empty(p.shape, F32) for p in parts],
                                        f"swap_{groups[0]}_start")
        swaps.append((keys, handle, f"swap_{groups[0]}_wait"))
        return token

    def finish(after):
        for keys, handle, name in swaps:
            mine, theirs = _exchange_wait("swap", handle, after, name)
            P.update(zip(keys, mine))
            Q.update(zip(keys, theirs))
        swaps.clear()
        last = after
        for k, parts in _BIG.items():
            if k in res or any(p not in P for p in parts):
                continue
            shape = w[k].shape
            two_d = lambda a: a.reshape(-1, shape[-1])
            outs = _adamw(two_d(w[k]), two_d(m[k]), two_d(v[k]), [P[p] for p in parts], [Q[p] for p in parts],
                          f"adamw_{k}")
            res[k] = [o.reshape(shape) for o in outs]
            last = outs[0]
        if "rg_w_a" in P and "gates" not in gathers:
            lands = [_cast_into_slot(P[k], 0, P[k].shape[0], me, F32, f"place_{k}", after=last, add=Q[k])
                     for k in names["gates"]]
            gathers["gates"], last = _exchange_start("gather", [], lands, "gather_gates_start", after=last)
        return last

    def emit(event, arrays):
        if event == "point_attn_done":
            return settle(["mlp1"], arrays[0])
        if event == "point_conv_done":
            return settle(["att", "mlp0", "rg_out", "gates"], arrays[0])
        token = start_scatter(event, arrays)
        return finish(token) if event == "rg_in" else token

    P_vec = dict(norm_mix_g=w["norm_mix_g"], norm_mlp_g=w["norm_mlp_g"], final_g=w["final_g"][None],
                 q_g=w["at_q_g"], k_g=w["at_k_g"])
    grad_x, vec_part = _local_step(x.reshape(T, D_MODEL), target.reshape(T, D_MODEL), P_vec, fetch, emit, B, L,
                                   after=tok)

    finish(settle(["rg_in"], grad_x))
    _, gate_grads = _exchange_wait("gather", gathers["gates"], grad_x, "gather_gates_wait")
    for k, g in zip(names["gates"], gate_grads):
        two_d = lambda a: a.reshape(g.shape[0] * g.shape[1], g.shape[2])
        outs = _adamw(two_d(w[k]), two_d(m[k]), two_d(v[k]), [two_d(g)], None, f"adamw_{k}")
        res[k] = [o.reshape(w[k].shape) for o in outs]
    vec_grad = _sum_leading(_all_devices_slots(vec_part, "allreduce_vec"), "sum_vec")
    loss = vec_grad[LOSS_ROW, 0]
    outs = _adamw(*packs, [vec_grad], None, "adamw_vec")
    unpacked = [_unpack_vec(o, me) for o in outs]
    for k in _WEIGHTS:
        if k not in res:
            res[k] = [u[k] for u in unpacked]

    result = [loss, grad_x.reshape(B, L, D_MODEL)]
    for slot in range(4):
        result += [res[k][slot] for k in _WEIGHTS]
    return tuple(result)
```

```python
import functools
import math

import jax
import jax.numpy as jnp
import numpy as np
from jax import lax
from jax.experimental import pallas as pl
from jax.experimental.pallas import tpu as pltpu

F32 = jnp.float32
BF16 = jnp.bfloat16

D_MODEL = 1024
HEAD_DIM = 128
N_HEADS = 8
N_KV = 2
GROUP = N_HEADS // N_KV
LRU_BLOCKS = 8
LRU_BW = 128
GRID_W = 64
ROPE_THETA = 10000.0
EPS = 1e-6
RG_C = 8.0
SCALE = 1.0 / math.sqrt(HEAD_DIM)
N_CHIPS = 4

ADAM_LR = 0.001
ADAM_B1 = 0.9
ADAM_B2 = 0.999
ADAM_EPS = 1e-08
ADAM_WD = 0.01
ADAM_STEP = 10

V7X_VMEM_BYTES = 64 * 1024 * 1024
VMEM_LIMIT = V7X_VMEM_BYTES * 3 // 4
LANES = 128
SUBLANES = 8

N_DEVICES = 8
VEC_ROWS = 32
LOSS_ROW = 5


def _params(sem):
    return pltpu.CompilerParams(dimension_semantics=sem, vmem_limit_bytes=VMEM_LIMIT)


_ANY = pl.BlockSpec(memory_space=pl.ANY)
_NN = (((1,), (0,)), ((), ()))
_NT = (((1,), (1,)), ((), ()))
_TN = (((0,), (0,)), ((), ()))


def _after_operand(after):
    return [] if after is None else [after]


def _fit(t, n):
    if n <= t:
        return n
    c = (t // LANES) * LANES
    while n % c:
        c -= LANES
    return c


MM_VMEM_BUDGET = VMEM_LIMIT * 3 // 4
def _mm_tiles(M, K, ns, n_total, out_dtypes, extras, whole_rows):
    for tm in (2048, 1024, 512, 256, 128):
        for tn in ((ns,) if whole_rows else (1024, 512, 256)):
            tn = _fit(tn, ns)
            per_row = 2 * (2 * K) + 4 * tn + sum(2 * tn * jnp.dtype(d).itemsize for d in out_dtypes)
            per_row += sum(2 * tn * e.dtype.itemsize for e in extras)
            b_buffers = 1 if tn == n_total else 2
            if M % tm == 0 and b_buffers * (2 * K * tn) + tm * per_row <= MM_VMEM_BUDGET:
                return tm, tn
    raise ValueError(f"no tile fits VMEM for M={M} K={K} N={ns}")


def _mm(a, b, *, mode, name, out_dtypes=(F32,), b_shard=False, o_shard=False, extras=(), epi=None, after=None,
        bcast=(), accs=(), ref_epi=None):
    if mode == "tn":
        K, M = a.shape
        N = b.shape[1]
    else:
        M, K = a.shape
        if mode == "nn":
            N = b.shape[0] * b.shape[2] if b_shard else b.shape[1]
        else:
            N = b.shape[1] if b_shard else b.shape[0]
    ns = N
    if b_shard and mode == "nn":
        ns = b.shape[2]
    elif o_shard:
        ns = N // N_CHIPS
    tm, tn = _mm_tiles(M, K, ns, N, out_dtypes, extras, whole_rows=ref_epi is not None)
    if ref_epi is not None:
        tm = min(tm, 512)
    grid = (M // tm, N // tn)
    q = ns // tn
    once = dict(pipeline_mode=pl.Buffered(1)) if tn == N else {}

    if mode == "tn":
        a_spec = pl.BlockSpec((K, tm), lambda i, j: (0, i))
        b_spec = pl.BlockSpec((K, tn), lambda i, j: (0, j), **once)
        dims = _TN
    elif mode == "nn":
        a_spec = pl.BlockSpec((tm, K), lambda i, j: (i, 0))
        if b_shard:
            b_spec = pl.BlockSpec((None, K, tn), lambda i, j: (j // q, 0, j % q), **once)
        else:
            b_spec = pl.BlockSpec((K, tn), lambda i, j: (0, j), **once)
        dims = _NN
    else:
        a_spec = pl.BlockSpec((tm, K), lambda i, j: (i, 0))
        if b_shard:
            ks = b.shape[2]
            b_spec = pl.BlockSpec((N_CHIPS, tn, ks), lambda i, j: (0, j, 0), **once)
        else:
            b_spec = pl.BlockSpec((tn, K), lambda i, j: (j, 0), **once)
        dims = _NT

    if o_shard:
        o_specs = [pl.BlockSpec((None, tm, tn), lambda i, j: (j // q, i, j % q))]
        o_shapes = [jax.ShapeDtypeStruct((N_CHIPS, M, ns), out_dtypes[0])]
    else:
        o_specs = [pl.BlockSpec((tm, tn), lambda i, j: (i, j)) for _ in out_dtypes]
        o_shapes = [jax.ShapeDtypeStruct((M, N), dt) for dt in out_dtypes]
    e_specs = [pl.BlockSpec((tm, tn), lambda i, j: (i, j)) for _ in extras]
    e_specs += [pl.BlockSpec(v.shape, lambda i, j: (0, 0)) for v in bcast]
    o_specs += [pl.BlockSpec(s, lambda i, j: (0, 0)) for s in accs]
    o_shapes += [jax.ShapeDtypeStruct(s, F32) for s in accs]
    n_e, n_b, n_o, n_a = len(extras), len(bcast), len(out_dtypes), len(accs)
    order = _after_operand(after)
    n_x = len(order)
    if epi is None:
        epi = lambda acc: (acc,)

    def body(a_ref, b_ref, *rest):
        e_refs, b_refs = rest[:n_e], rest[n_e:n_e + n_b]
        o_refs = rest[n_e + n_b + n_x:n_e + n_b + n_x + n_o]
        a_refs = rest[n_e + n_b + n_x + n_o:]
        if n_a:
            @pl.when((pl.program_id(0) == 0) & (pl.program_id(1) == 0))
            def _():
                for r in a_refs:
                    r[...] = jnp.zeros(r.shape, F32)
        if mode == "nt" and b_shard:
            acc = None
            for s in range(N_CHIPS):
                part = lax.dot_general(a_ref[:, s * ks:(s + 1) * ks], b_ref[s], dims, preferred_element_type=F32)
                acc = part if acc is None else acc + part
        else:
            acc = lax.dot_general(a_ref[...], b_ref[...], dims, preferred_element_type=F32)
        if ref_epi is not None:
            ref_epi(acc, e_refs, b_refs, o_refs, a_refs)
            return
        outs = epi(acc, *[r[...] for r in e_refs])
        for r, o in zip(o_refs, outs):
            r[...] = o.astype(r.dtype)

    outs = pl.pallas_call(
        body, name=name, grid=grid, in_specs=[a_spec, b_spec] + e_specs + [_ANY] * n_x, out_specs=o_specs,
        out_shape=o_shapes, compiler_params=_params(("arbitrary", "arbitrary") if n_a else ("parallel", "parallel")),
    )(a, b, *extras, *bcast, *order)
    return outs[0] if n_o + n_a == 1 else outs


def _rowwise(fn, rows, bcast, outs, accs=(), *, tm, name, after=None):
    def norm(r):
        return r if isinstance(r, tuple) else (r, r.shape[1], 0)

    rows = [norm(r) for r in rows]
    T = rows[0][0].shape[0]
    tm = min(tm, T)
    while T % tm:
        tm -= SUBLANES
    n_r, n_b, n_o, n_a = len(rows), len(bcast), len(outs), len(accs)
    order = _after_operand(after)
    n_x = len(order)
    in_specs = [pl.BlockSpec((tm, c), functools.partial(lambda i, cb: (i, cb), cb=cb)) for _, c, cb in rows]
    in_specs += [pl.BlockSpec(b.shape, lambda i: (0, 0)) for b in bcast] + [_ANY] * n_x
    out_specs = [pl.BlockSpec((tm, o[0]), lambda i: (i, 0)) for o in outs]
    out_specs += [pl.BlockSpec(s, lambda i: (0, 0)) for s in accs]
    out_shape = [jax.ShapeDtypeStruct((T, o[2] if len(o) > 2 else o[0]), o[1]) for o in outs]
    out_shape += [jax.ShapeDtypeStruct(s, F32) for s in accs]

    def body(*refs):
        in_refs = refs[:n_r]
        b_refs = refs[n_r:n_r + n_b]
        o_refs = refs[n_r + n_b + n_x:n_r + n_b + n_x + n_o]
        a_refs = refs[n_r + n_b + n_x + n_o:]
        if n_a:
            @pl.when(pl.program_id(0) == 0)
            def _():
                for r in a_refs:
                    r[...] = jnp.zeros(r.shape, F32)
        fn(in_refs, b_refs, o_refs, a_refs)

    res = pl.pallas_call(
        body, name=name, grid=(T // tm,), in_specs=in_specs, out_specs=out_specs, out_shape=out_shape,
        compiler_params=_params(("arbitrary",) if n_a else ("parallel",)),
    )(*[r[0] for r in rows], *bcast, *order)
    return res


def _rsum(x):
    return jnp.sum(x, axis=0, keepdims=True)


def _rms_fwd(x, g, name, after=None):
    def fn(ins, bs, outs, accs):
        xv = ins[0][...]
        r = lax.rsqrt(jnp.mean(xv * xv, axis=-1, keepdims=True) + EPS)
        outs[0][...] = (xv * r * bs[0][...]).astype(BF16)

    return _rowwise(fn, [x], [g], [(D_MODEL, BF16)], tm=512, name=name, after=after)[0]


def _rms_bwd_math(xv, dh, g):
    r = lax.rsqrt(jnp.mean(xv * xv, axis=-1, keepdims=True) + EPS)
    hn = xv * r
    dgh = dh * g
    dx = r * (dgh - hn * jnp.mean(dgh * hn, axis=-1, keepdims=True))
    return dx, _rsum(dh * hn)


def _mm_norm_bwd(dy, w, x, dres, g, name, after=None):
    def epilogue(acc, e_refs, b_refs, o_refs, a_refs):
        dx, dg = _rms_bwd_math(e_refs[0][...], acc, b_refs[0][...])
        dx = dx + e_refs[1][...]
        o_refs[0][...] = dx
        o_refs[1][...] = dx.astype(BF16)
        a_refs[0][...] += dg

    return _mm(dy, w, mode="nt", b_shard=True, out_dtypes=(F32, BF16), extras=(x, dres), bcast=(g,),
               accs=((1, D_MODEL),), ref_epi=epilogue, name=name, after=after)


def _mm_res_norm(a, w, res, g, name):
    def epilogue(acc, e_refs, b_refs, o_refs, a_refs):
        xv = acc + e_refs[0][...]
        o_refs[0][...] = xv
        r = lax.rsqrt(jnp.mean(xv * xv, axis=-1, keepdims=True) + EPS)
        o_refs[1][...] = (xv * r * b_refs[0][...]).astype(BF16)

    return _mm(a, w, mode="nn", out_dtypes=(F32, BF16), extras=(res,), bcast=(g,), ref_epi=epilogue, name=name)


def _mm_final_loss(a, w, res, target, g, name):
    def epilogue(acc, e_refs, b_refs, o_refs, a_refs):
        xv = acc + e_refs[0][...]
        gv = b_refs[0][...]
        r = lax.rsqrt(jnp.mean(xv * xv, axis=-1, keepdims=True) + EPS)
        e = xv * r * gv - e_refs[1][...]
        tok = jnp.mean(e * e, axis=-1, keepdims=True)
        a_refs[0][...] += 0.5 * jnp.sum(tok, axis=0, keepdims=True) * jnp.ones((1, LANES), F32)
        dx, dg = _rms_bwd_math(xv, e * (1.0 / D_MODEL), gv)
        o_refs[0][...] = dx
        o_refs[1][...] = dx.astype(BF16)
        a_refs[1][...] += dg

    return _mm(a, w, mode="nn", out_dtypes=(F32, BF16), extras=(res, target), bcast=(g,),
               accs=((1, LANES), (1, D_MODEL)), ref_epi=epilogue, name=name)


def _relu2(acc):
    r = jnp.maximum(acc, 0.0)
    return r * r, r


def _mlp_fwd(x, h, fetch, tag, finish):
    w_up, w_down = fetch(f"mlp{tag}", h)
    a, r = _mm(h, w_up, mode="nn", b_shard=True, out_dtypes=(BF16, BF16), epi=_relu2, name=f"mlp{tag}_up")
    return finish(a, w_down, x, f"mlp{tag}_down"), (h, a, r, w_up, w_down)


def _mlp_bwd(x, g, saved, dx, dx_bf, tag, after):
    h, a, r, w_up, w_down = saved
    d_down = _mm(a, dx_bf, mode="tn", out_dtypes=(BF16,), name=f"mlp{tag}_dwdown", after=after)
    dup = _mm(dx_bf, w_down, mode="nt", extras=(r,), out_dtypes=(BF16,),
              epi=lambda acc, rv: (acc * (2.0 * rv.astype(F32)),), name=f"mlp{tag}_dup")
    d_up = _mm(h, dup, mode="tn", o_shard=True, out_dtypes=(BF16,), name=f"mlp{tag}_dwup")
    dx_new, dx_new_bf, dg = _mm_norm_bwd(dup, w_up, x, dx, g, f"mlp{tag}_dh")
    return dx_new, dx_new_bf, dg, d_up, d_down


def _rope_tables(L, B):
    rows = L // GRID_W
    row = np.repeat(np.arange(rows, dtype=np.float32), GRID_W)
    col = np.tile(np.arange(GRID_W, dtype=np.float32), rows)
    inv = (ROPE_THETA ** (-np.arange(HEAD_DIM // 4, dtype=np.float32) / (HEAD_DIM // 4))).astype(np.float32)
    ar, ac = row[:, None] * inv, col[:, None] * inv
    cos = np.concatenate([np.cos(ar), np.cos(ar), np.cos(ac), np.cos(ac)], axis=-1)
    sin = np.concatenate([-np.sin(ar), np.sin(ar), -np.sin(ac), np.sin(ac)], axis=-1)
    return jnp.asarray(np.tile(cos, (B, 1)), F32), jnp.asarray(np.tile(sin, (B, 1)), F32)


def _swap_halves(x):
    lane = lax.broadcasted_iota(jnp.int32, x.shape, 1)
    return jnp.where((lane % 64) < 32, pltpu.roll(x, HEAD_DIM - 32, 1), pltpu.roll(x, 32, 1))


def _qk_prep(qkv, cos, sin, q_g, k_g):
    def fn(ins, bs, outs, accs):
        c, s = ins[1][...], ins[2][...]
        for h in range(N_HEADS + N_KV):
            xv = ins[0][:, h * HEAD_DIM:(h + 1) * HEAD_DIM]
            g = bs[0][...] if h < N_HEADS else bs[1][...]
            r = lax.rsqrt(jnp.mean(xv * xv, axis=-1, keepdims=True) + EPS)
            z = xv * r * g
            y = (z * c + _swap_halves(z) * s).astype(BF16)
            if h < N_HEADS:
                outs[0][:, h * HEAD_DIM:(h + 1) * HEAD_DIM] = y
            else:
                outs[1][:, (h - N_HEADS) * HEAD_DIM:(h - N_HEADS + 1) * HEAD_DIM] = y
        outs[2][...] = ins[0][:, (N_HEADS + N_KV) * HEAD_DIM:].astype(BF16)

    kvw = N_KV * HEAD_DIM
    return _rowwise(fn, [qkv, cos, sin], [q_g, k_g], [(D_MODEL, BF16), (kvw, BF16), (kvw, BF16)], tm=512,
                    name="attn_qk_prep")


def _qk_prep_bwd(qkv, dq, dk, dv, cos, sin, q_g, k_g):
    def fn(ins, bs, outs, accs):
        c, s = ins[4][...], ins[5][...]
        for h in range(N_HEADS + N_KV):
            sl = slice(h * HEAD_DIM, (h + 1) * HEAD_DIM)
            xv = ins[0][:, sl]
            if h < N_HEADS:
                g, dy, acc = bs[0][...], ins[1][:, sl], accs[0]
            else:
                ks = slice((h - N_HEADS) * HEAD_DIM, (h - N_HEADS + 1) * HEAD_DIM)
                g, dy, acc = bs[1][...], ins[2][:, ks], accs[1]
            r = lax.rsqrt(jnp.mean(xv * xv, axis=-1, keepdims=True) + EPS)
            xn = xv * r
            dz = dy * c - _swap_halves(dy) * s
            acc[...] += _rsum(dz * xn)
            dxn = dz * g
            outs[0][:, sl] = (r * (dxn - xn * jnp.mean(dxn * xn, axis=-1, keepdims=True))).astype(BF16)
        outs[0][:, (N_HEADS + N_KV) * HEAD_DIM:] = ins[3][...].astype(BF16)

    return _rowwise(fn, [qkv, dq, dk, dv, cos, sin], [q_g, k_g], [(qkv.shape[1], BF16)],
                    [(1, HEAD_DIM), (1, HEAD_DIM)], tm=256, name="attn_qk_prep_bwd")


_EXP2_SCALE = SCALE * math.log2(math.e)


def _exp_rows(q, k):
    s = lax.dot_general(q, k, _NT, preferred_element_type=F32)
    p = jnp.exp2((s - jnp.max(s, axis=-1, keepdims=True)) * _EXP2_SCALE)
    return p, jnp.sum(p, axis=-1, keepdims=True)


def _attn_fwd(q, k, v, B, L, tq=2048, sub=256):
    tq = min(tq, L)
    sub = min(sub, tq)
    nq = L // tq

    def body(q_ref, k_ref, v_ref, o_ref):
        kv, vv = k_ref[...], v_ref[...]
        for c in range(tq // sub):
            rows = slice(c * sub, (c + 1) * sub)
            p, l = _exp_rows(q_ref[rows, :], kv)
            o = jnp.dot(p.astype(BF16), vv, preferred_element_type=F32)
            o_ref[rows, :] = (o * (1.0 / l)).astype(o_ref.dtype)

    return pl.pallas_call(
        body, name="attn_fwd", grid=(B, N_HEADS, nq),
        in_specs=[pl.BlockSpec((tq, HEAD_DIM), lambda b, h, i: (b * nq + i, h)),
                  pl.BlockSpec((L, HEAD_DIM), lambda b, h, i: (b, h // GROUP)),
                  pl.BlockSpec((L, HEAD_DIM), lambda b, h, i: (b, h // GROUP))],
        out_specs=pl.BlockSpec((tq, HEAD_DIM), lambda b, h, i: (b * nq + i, h)),
        out_shape=jax.ShapeDtypeStruct((B * L, D_MODEL), BF16),
        compiler_params=_params(("parallel", "parallel", "parallel")),
    )(q, k, v)


def _attn_bwd(q, k, v, do, B, L, tq=2048, sub=512):
    tq = min(tq, L)
    sub = min(sub, tq)
    nq = L // tq

    def body(q_ref, k_ref, v_ref, do_ref, dq_ref, dk_ref, dv_ref):
        @pl.when((pl.program_id(2) == 0) & (pl.program_id(3) == 0))
        def _():
            dk_ref[...] = jnp.zeros(dk_ref.shape, F32)
            dv_ref[...] = jnp.zeros(dv_ref.shape, F32)

        kv, vv = k_ref[...], v_ref[...]
        ps, es, dos, qs = [], [], [], []
        for c in range(tq // sub):
            rows = slice(c * sub, (c + 1) * sub)
            qc, doc = q_ref[rows, :], do_ref[rows, :]
            p, l = _exp_rows(qc, kv)
            inv = 1.0 / l
            dp = lax.dot_general(doc, vv, _NT, preferred_element_type=F32)
            delta = jnp.sum(p * dp, axis=-1, keepdims=True) * inv
            e = (p * (dp - delta)).astype(BF16)
            dq_ref[rows, :] = jnp.dot(e, kv, preferred_element_type=F32) * (inv * SCALE)
            ps.append(p.astype(BF16))
            es.append(e)
            dos.append((doc.astype(F32) * inv).astype(BF16))
            qs.append((qc.astype(F32) * (inv * SCALE)).astype(BF16))
        cat = lambda xs: xs[0] if len(xs) == 1 else jnp.concatenate(xs, axis=0)
        dv_ref[...] += lax.dot_general(cat(ps), cat(dos), _TN, preferred_element_type=F32)
        dk_ref[...] += lax.dot_general(cat(es), cat(qs), _TN, preferred_element_type=F32)

    qmap = lambda b, kh, g, i: (b * nq + i, kh * GROUP + g)
    kmap = lambda b, kh, g, i: (b, kh)
    kvw = N_KV * HEAD_DIM
    return pl.pallas_call(
        body, name="attn_bwd", grid=(B, N_KV, GROUP, nq),
        in_specs=[pl.BlockSpec((tq, HEAD_DIM), qmap), pl.BlockSpec((L, HEAD_DIM), kmap),
                  pl.BlockSpec((L, HEAD_DIM), kmap), pl.BlockSpec((tq, HEAD_DIM), qmap)],
        out_specs=[pl.BlockSpec((tq, HEAD_DIM), qmap), pl.BlockSpec((L, HEAD_DIM), kmap),
                   pl.BlockSpec((L, HEAD_DIM), kmap)],
        out_shape=[jax.ShapeDtypeStruct((B * L, D_MODEL), F32), jax.ShapeDtypeStruct((B * L, kvw), F32),
                   jax.ShapeDtypeStruct((B * L, kvw), F32)],
        compiler_params=_params(("parallel", "parallel", "arbitrary", "arbitrary")),
    )(q, k, v, do)


def _conv_shift(x, t, L, k):
    if k == 2:
        return x
    if k < 2:
        return jnp.where(t >= 2 - k, pltpu.roll(x, 2 - k, 0), 0.0)
    return jnp.where(t < L - (k - 2), pltpu.roll(x, L - (k - 2), 0), 0.0)


def _conv_apply(x, w_ref, L):
    t = lax.broadcasted_iota(jnp.int32, x.shape, 0)
    acc = w_ref[4:5, :] + w_ref[2:3, :] * x
    for k in (0, 1, 3):
        acc = acc + w_ref[k:k + 1, :] * _conv_shift(x, t, L, k)
    return acc


def _conv_fwd(z, wb, B, L, tc=256):
    noff = D_MODEL // tc

    def body(z_ref, w_ref, o_ref):
        o_ref[...] = _conv_apply(z_ref[...], w_ref, L)

    return pl.pallas_call(
        body, name="rg_conv", grid=(B, noff),
        in_specs=[pl.BlockSpec((L, tc), lambda b, j: (b, noff + j)), pl.BlockSpec((SUBLANES, tc), lambda b, j: (0, j))],
        out_specs=pl.BlockSpec((L, tc), lambda b, j: (b, j)),
        out_shape=jax.ShapeDtypeStruct((B * L, D_MODEL), F32),
        compiler_params=_params(("parallel", "parallel")),
    )(z, wb)


def _conv_bwd(z, g, wb, dz, B, L, tc=256, after=None):
    noff = D_MODEL // tc
    order = _after_operand(after)

    def body(z_ref, g_ref, w_ref, dz_in, *rest):
        dx_ref, dw_ref = rest[len(order):]

        @pl.when(pl.program_id(1) == 0)
        def _():
            dw_ref[...] = jnp.zeros(dw_ref.shape, F32)

        x, gv = z_ref[...], g_ref[...]
        t = lax.broadcasted_iota(jnp.int32, x.shape, 0)
        dx = w_ref[2:3, :] * gv
        for k in (0, 1, 3):
            dx = dx + w_ref[k:k + 1, :] * _conv_shift(gv, t, L, 4 - k)
        dx_ref[...] = dx.astype(BF16)
        for k in range(4):
            dw_ref[k:k + 1, :] += _rsum(_conv_shift(x, t, L, k) * gv)
        dw_ref[4:5, :] += _rsum(gv)

    return pl.pallas_call(
        body, name="rg_conv_bwd", grid=(noff, B),
        in_specs=[pl.BlockSpec((L, tc), lambda j, b: (b, noff + j)), pl.BlockSpec((L, tc), lambda j, b: (b, j)),
                  pl.BlockSpec((SUBLANES, tc), lambda j, b: (0, j)), _ANY] + [_ANY] * len(order),
        out_specs=[pl.BlockSpec((L, tc), lambda j, b: (b, noff + j)),
                   pl.BlockSpec((SUBLANES, tc), lambda j, b: (0, j))],
        out_shape=[jax.ShapeDtypeStruct(dz.shape, dz.dtype), jax.ShapeDtypeStruct((SUBLANES, D_MODEL), F32)],
        input_output_aliases={3: 0},
        compiler_params=_params(("parallel", "arbitrary")),
    )(z, g, wb, dz, *order)


def _softplus(x):
    return jnp.maximum(x, 0.0) + jnp.log1p(jnp.exp(-jnp.abs(x)))


_ROW_BA, _ROW_BX, _ROW_LAM = 0, 2, 4


def _gate_math(xb, pre, vec_ref, d, sl):
    pa = pre[:, (2 * d) * LRU_BW:(2 * d + 1) * LRU_BW] + vec_ref[_ROW_BA + d:_ROW_BA + d + 1, sl]
    px = pre[:, (2 * d + 1) * LRU_BW:(2 * d + 2) * LRU_BW] + vec_ref[_ROW_BX + d:_ROW_BX + d + 1, sl]
    r = 0.5 * jnp.tanh(0.5 * pa) + 0.5
    i = 0.5 * jnp.tanh(0.5 * px) + 0.5
    sp = _softplus(-vec_ref[_ROW_LAM + d:_ROW_LAM + d + 1, sl])
    log_a = (-RG_C) * r * sp
    a = jnp.exp(log_a)
    th = jnp.tanh(log_a)
    om = -2.0 * th / (1.0 - th)
    mult = jnp.sqrt(om)
    return a, mult * (i * xb), (r, i, sp, om, mult)


def _gate_fwd(rec, wcat, gvec):
    def fn(ins, bs, outs, accs):
        for blk in range(LRU_BLOCKS):
            sl = slice(blk * LRU_BW, (blk + 1) * LRU_BW)
            xb = ins[0][:, sl]
            pre = jnp.dot(xb.astype(BF16), bs[0][sl, :], preferred_element_type=F32)
            for d in range(2):
                a, u, _ = _gate_math(xb, pre, bs[1], d, sl)
                outs[2 * d][:, sl] = a
                outs[2 * d + 1][:, sl] = u

    return _rowwise(fn, [rec], [wcat, gvec], [(D_MODEL, F32)] * 4, tm=256, name="rg_gate")


def _gate_bwd(rec, du_f, da_f, du_b, da_b, wcat, gvec):
    def fn(ins, bs, outs, accs):
        for blk in range(LRU_BLOCKS):
            sl = slice(blk * LRU_BW, (blk + 1) * LRU_BW)
            xb = ins[0][:, sl]
            xb16 = xb.astype(BF16)
            w = bs[0][sl, :]
            pre = jnp.dot(xb16, w, preferred_element_type=F32)
            dx = jnp.zeros_like(xb)
            dpre = []
            for d in range(2):
                a, _, (r, i, sp, om, mult) = _gate_math(xb, pre, bs[1], d, sl)
                du, da = ins[1 + 2 * d][:, sl], ins[2 + 2 * d][:, sl]
                d_i = du * mult * xb
                d_mult = du * i * xb
                dx = dx + du * mult * i
                dlog = da * a - d_mult * (1.0 - om) / mult
                d_r = dlog * ((-RG_C) * sp)
                d_sp = _rsum(dlog * ((-RG_C) * r))
                lam = bs[1][_ROW_LAM + d:_ROW_LAM + d + 1, sl]
                accs[2][_ROW_LAM + d:_ROW_LAM + d + 1, sl] += d_sp * (-jax.nn.sigmoid(-lam))
                dpa = d_r * r * (1.0 - r)
                dpx = d_i * i * (1.0 - i)
                accs[2][_ROW_BA + d:_ROW_BA + d + 1, sl] += _rsum(dpa)
                accs[2][_ROW_BX + d:_ROW_BX + d + 1, sl] += _rsum(dpx)
                dpre += [dpa, dpx]
            dpre = jnp.concatenate(dpre, axis=1).astype(BF16)
            dw = lax.dot_general(xb16, dpre, _TN, preferred_element_type=F32)
            for d in range(2):
                rows = slice(d * D_MODEL + blk * LRU_BW, d * D_MODEL + (blk + 1) * LRU_BW)
                accs[0][rows, :] += dw[:, (2 * d) * LRU_BW:(2 * d + 1) * LRU_BW]
                accs[1][rows, :] += dw[:, (2 * d + 1) * LRU_BW:(2 * d + 2) * LRU_BW]
            outs[0][:, sl] = dx + lax.dot_general(dpre, w, _NT, preferred_element_type=F32)

    gate_shape = (2 * D_MODEL, LRU_BW)
    return _rowwise(fn, [rec, du_f, da_f, du_b, da_b], [wcat, gvec], [(D_MODEL, F32)],
                    [gate_shape, gate_shape, (SUBLANES, D_MODEL)], tm=256, name="rg_gate_bwd")


def _as_time_blocks(x):
    return x.reshape(x.shape[0] // SUBLANES, SUBLANES, x.shape[1])


def _scan_call(body, ins, n_out, B, L, tc, name):
    nb = L // SUBLANES
    spec = pl.BlockSpec((nb, SUBLANES, tc), lambda b, j: (b, 0, j))
    T = ins[0].shape[0]
    outs = pl.pallas_call(
        functools.partial(body, nb), name=name, grid=(B, D_MODEL // tc),
        in_specs=[spec] * len(ins), out_specs=[spec] * n_out,
        out_shape=[jax.ShapeDtypeStruct((T // SUBLANES, SUBLANES, D_MODEL), F32)] * n_out,
        compiler_params=_params(("parallel", "parallel")),
    )(*[_as_time_blocks(x) for x in ins])
    return [o.reshape(T, D_MODEL) for o in outs]


def _block_scan(A, U, reverse):
    row = lax.broadcasted_iota(jnp.int32, A.shape, 0)
    for s in (1, 2, 4):
        shift = SUBLANES - s if reverse else s
        valid = (row < SUBLANES - s) if reverse else (row >= s)
        a_sh = jnp.where(valid, pltpu.roll(A, shift, 0), 1.0)
        u_sh = jnp.where(valid, pltpu.roll(U, shift, 0), 0.0)
        U = A * u_sh + U
        A = A * a_sh
    return A, U


_LAST = SUBLANES - 1
SCAN_UNROLL = 8


def _loop_blocks(nb, step, init):
    def group(g, carry):
        for k in range(SCAN_UNROLL):
            carry = step(g * SCAN_UNROLL + k, carry)
        return carry

    return lax.fori_loop(0, nb // SCAN_UNROLL, group, init)


def _scan_fwd(a_f, u_f, a_b, u_b, B, L, tc=256):
    def body(nb, af, uf, ab, ub, hf, hb):
        def step(i, carry):
            c1, c2 = carry
            ib = nb - 1 - i
            p, h = _block_scan(af[i], uf[i], False)
            h = h + p * c1
            hf[i] = h
            p2, h2 = _block_scan(ab[ib], ub[ib], True)
            h2 = h2 + p2 * c2
            hb[ib] = h2
            return h[_LAST:, :], h2[:1, :]

        zero = jnp.zeros((1, tc), F32)
        _loop_blocks(nb, step, (zero, zero))

    return _scan_call(body, [a_f, u_f, a_b, u_b], 2, B, L, tc, "rg_scan")


def _scan_bwd(dy, a_f, h_f, a_b, h_b, B, L, tc=256):
    def body(nb, dy_r, af, hf, ab, hb, duf, daf, dub, dab):
        def step(i, carry):
            c1, c2 = carry
            ir = nb - 1 - i
            row = lax.broadcasted_iota(jnp.int32, (SUBLANES, tc), 0)
            a_up = jnp.where(row == _LAST, af[jnp.minimum(ir + 1, nb - 1), :1, :], pltpu.roll(af[ir], _LAST, 0))
            p, lam = _block_scan(a_up, dy_r[ir], True)
            lam = lam + p * c1
            before = hf[jnp.maximum(ir - 1, 0), _LAST:, :] * (ir > 0).astype(F32)
            duf[ir] = lam
            daf[ir] = lam * jnp.where(row == 0, before, pltpu.roll(hf[ir], 1, 0))
            a_dn = jnp.where(row == 0, ab[jnp.maximum(i - 1, 0), _LAST:, :], pltpu.roll(ab[i], 1, 0))
            p2, lam2 = _block_scan(a_dn, dy_r[i], False)
            lam2 = lam2 + p2 * c2
            after = hb[jnp.minimum(i + 1, nb - 1), :1, :] * (i < nb - 1).astype(F32)
            dub[i] = lam2
            dab[i] = lam2 * jnp.where(row == _LAST, after, pltpu.roll(hb[i], _LAST, 0))
            return lam[:1, :], lam2[_LAST:, :]

        zero = jnp.zeros((1, tc), F32)
        _loop_blocks(nb, step, (zero, zero))

    return _scan_call(body, [dy, a_f, h_f, a_b, h_b], 4, B, L, tc, "rg_scan_bwd")


_GELU_C = math.sqrt(2.0 / math.pi)


def _gelu_parts(x):
    th = jnp.tanh(_GELU_C * (x + 0.044715 * x * x * x))
    return 0.5 * x * (1.0 + th), th


def _gated_out(h_f, h_b, z):
    def fn(ins, bs, outs, accs):
        gl, _ = _gelu_parts(ins[2][...])
        outs[0][...] = ((ins[0][...] + ins[1][...]) * gl).astype(BF16)

    return _rowwise(fn, [h_f, h_b, (z, D_MODEL, 0)], [], [(D_MODEL, BF16)], tm=512, name="rg_gated_out")[0]


def _gated_out_bwd(dyg, h_f, h_b, z):
    def fn(ins, bs, outs, accs):
        x = ins[3][...]
        gl, th = _gelu_parts(x)
        dgl = 0.5 * (1.0 + th) + 0.5 * x * (1.0 - th * th) * (_GELU_C * (1.0 + 3.0 * 0.044715 * x * x))
        g = ins[0][...]
        outs[0][...] = g * gl
        outs[1][...] = (g * (ins[1][...] + ins[2][...]) * dgl).astype(BF16)

    return _rowwise(fn, [dyg, h_f, h_b, (z, D_MODEL, 0)], [], [(D_MODEL, F32), (D_MODEL, BF16, 2 * D_MODEL)], tm=512,
                    name="rg_gated_out_bwd")


def _row_block(i):
    return pl.ds(pl.multiple_of(i * SUBLANES, SUBLANES), SUBLANES)


def _rg_mix_fwd(z, conv_wb, wcat, gvec, B, L):
    nb = L // SUBLANES
    n_g = D_MODEL // LRU_BW

    def body(zg_ref, zr_ref, cw_ref, w_ref, gv_ref, hf_ref, hb_ref, yg_ref, af_s, uf_s, ab_s, ub_s):
        rec = _conv_apply(zr_ref[...], cw_ref, L)
        pre = jnp.dot(rec.astype(BF16), w_ref[...], preferred_element_type=F32)
        for d, (a_s, u_s) in enumerate(((af_s, uf_s), (ab_s, ub_s))):
            a, u, _ = _gate_math(rec, pre, gv_ref, d, slice(None))
            a_s[...] = a
            u_s[...] = u

        def step(i, carry):
            c1, c2 = carry
            rows, rows_b = _row_block(i), _row_block(nb - 1 - i)
            p, h = _block_scan(af_s[rows, :], uf_s[rows, :], False)
            h = h + p * c1
            hf_ref[rows, :] = h
            p2, h2 = _block_scan(ab_s[rows_b, :], ub_s[rows_b, :], True)
            h2 = h2 + p2 * c2
            hb_ref[rows_b, :] = h2
            return h[_LAST:, :], h2[:1, :]

        zero = jnp.zeros((1, LRU_BW), F32)
        _loop_blocks(nb, step, (zero, zero))
        gl, _ = _gelu_parts(zg_ref[...])
        yg_ref[...] = ((hf_ref[...] + hb_ref[...]) * gl).astype(BF16)

    seq = lambda off: pl.BlockSpec((L, LRU_BW), lambda b, g: (b, off + g))
    vec = pl.BlockSpec((SUBLANES, LRU_BW), lambda b, g: (0, g))
    T = B * L
    return pl.pallas_call(
        body, name="rg_mix", grid=(B, n_g),
        in_specs=[seq(0), seq(n_g), vec, pl.BlockSpec((LRU_BW, 4 * LRU_BW), lambda b, g: (g, 0)), vec],
        out_specs=[seq(0)] * 3,
        out_shape=[jax.ShapeDtypeStruct((T, D_MODEL), F32)] * 2 + [jax.ShapeDtypeStruct((T, D_MODEL), BF16)],
        scratch_shapes=[pltpu.VMEM((L, LRU_BW), F32)] * 4,
        compiler_params=_params(("parallel", "parallel")),
    )(z, z, conv_wb, wcat, gvec)


def _gate_bwd_math(xb, w, vec_ref, sl, grads):
    xb16 = xb.astype(BF16)
    pre = jnp.dot(xb16, w, preferred_element_type=F32)
    dx = jnp.zeros_like(xb)
    dpre, vec_rows = [], []
    for d, (du, da) in enumerate(grads):
        a, _, (r, i, sp, om, mult) = _gate_math(xb, pre, vec_ref, d, sl)
        t = du * xb
        dx = dx + du * mult * i
        dlog = da * a - (t * i) * (1.0 - om) / mult
        lam = vec_ref[_ROW_LAM + d:_ROW_LAM + d + 1, sl]
        dpa = dlog * ((-RG_C) * sp) * r * (1.0 - r)
        dpx = (t * mult) * i * (1.0 - i)
        vec_rows.append((_rsum(dpa), _rsum(dpx), _rsum(dlog * ((-RG_C) * r)) * (-jax.nn.sigmoid(-lam))))
        dpre += [dpa, dpx]
    dpre = jnp.concatenate(dpre, axis=1).astype(BF16)
    dw = lax.dot_general(xb16, dpre, _TN, preferred_element_type=F32)
    return dx + lax.dot_general(dpre, w, _NT, preferred_element_type=F32), dw, vec_rows


GATE_BWD_ROWS = 256


def _rg_mix_bwd(dyg, z, h_f, h_b, conv_wb, wcat, gvec, B, L, after=None):
    nb = L // SUBLANES
    n_g = D_MODEL // LRU_BW
    tile = min(GATE_BWD_ROWS, L)
    n_t = L // tile
    order = _after_operand(after)

    def body(dyg_ref, zg_ref, zr_ref, hf_ref, hb_ref, cw_ref, w_ref, gv_ref, *rest):
        dz_ref, dwa_ref, dwx_ref, dgv_ref, dcw_ref = rest[len(order):len(order) + 5]
        dy_s, rec_s, af_s, ab_s, duf_s, daf_s, dub_s, dab_s, drc_s, drec_s = rest[len(order) + 5:]
        first = (pl.program_id(1) == 0) & (pl.program_id(2) == 0)

        @pl.when(first)
        def _():
            for r in (dwa_ref, dwx_ref, dgv_ref, dcw_ref):
                r[...] = jnp.zeros(r.shape, F32)

        @pl.when(pl.program_id(2) == 1)
        def _():
            dz_ref[...] = drec_s[...]

        @pl.when(pl.program_id(2) == 0)
        def _():
            zg = zg_ref[...]
            gl, th = _gelu_parts(zg)
            dgl = 0.5 * (1.0 + th) + 0.5 * zg * (1.0 - th * th) * (_GELU_C * (1.0 + 3.0 * 0.044715 * zg * zg))
            dyg_v = dyg_ref[...]
            dy_s[...] = dyg_v * gl
            dz_ref[...] = (dyg_v * (hf_ref[...] + hb_ref[...]) * dgl).astype(BF16)
            rec_s[...] = _conv_apply(zr_ref[...], cw_ref, L)
            w = w_ref[...]

            def decay(t, _):
                rows = pl.ds(pl.multiple_of(t * tile, tile), tile)
                xb = rec_s[rows, :]
                pre = jnp.dot(xb.astype(BF16), w, preferred_element_type=F32)
                af_s[rows, :] = _gate_math(xb, pre, gv_ref, 0, slice(None))[0]
                ab_s[rows, :] = _gate_math(xb, pre, gv_ref, 1, slice(None))[0]
                return 0

            lax.fori_loop(0, n_t, decay, 0)

            row = lax.broadcasted_iota(jnp.int32, (SUBLANES, LRU_BW), 0)

            def step(i, carry):
                c1, c2 = carry
                ir = nb - 1 - i
                here, there = _row_block(ir), _row_block(i)
                a_next = af_s[_row_block(jnp.minimum(ir + 1, nb - 1)), :]
                a_up = jnp.where(row == _LAST, a_next[:1, :], pltpu.roll(af_s[here, :], _LAST, 0))
                p, lam = _block_scan(a_up, dy_s[here, :], True)
                lam = lam + p * c1
                h_prev = hf_ref[_row_block(jnp.maximum(ir - 1, 0)), :]
                before = h_prev[_LAST:, :] * (ir > 0).astype(F32)
                duf_s[here, :] = lam
                daf_s[here, :] = lam * jnp.where(row == 0, before, pltpu.roll(hf_ref[here, :], 1, 0))
                a_prev = ab_s[_row_block(jnp.maximum(i - 1, 0)), :]
                a_dn = jnp.where(row == 0, a_prev[_LAST:, :], pltpu.roll(ab_s[there, :], 1, 0))
                p2, lam2 = _block_scan(a_dn, dy_s[there, :], False)
                lam2 = lam2 + p2 * c2
                h_next = hb_ref[_row_block(jnp.minimum(i + 1, nb - 1)), :]
                after_ = h_next[:1, :] * (i < nb - 1).astype(F32)
                dub_s[there, :] = lam2
                dab_s[there, :] = lam2 * jnp.where(row == _LAST, after_, pltpu.roll(hb_ref[there, :], _LAST, 0))
                return lam[:1, :], lam2[_LAST:, :]

            zero = jnp.zeros((1, LRU_BW), F32)
            _loop_blocks(nb, step, (zero, zero))

            def gates(t, _):
                rows = pl.ds(pl.multiple_of(t * tile, tile), tile)
                grads = ((duf_s[rows, :], daf_s[rows, :]), (dub_s[rows, :], dab_s[rows, :]))
                dx, dw, vec_rows = _gate_bwd_math(rec_s[rows, :], w, gv_ref, slice(None), grads)
                drc_s[rows, :] = dx
                for d in range(2):
                    dwa_ref[d] += dw[:, (2 * d) * LRU_BW:(2 * d + 1) * LRU_BW]
                    dwx_ref[d] += dw[:, (2 * d + 1) * LRU_BW:(2 * d + 2) * LRU_BW]
                    for first_row, val in zip((_ROW_BA, _ROW_BX, _ROW_LAM), vec_rows[d]):
                        dgv_ref[first_row + d:first_row + d + 1, :] += val
                return 0

            lax.fori_loop(0, n_t, gates, 0)

            x, gv = zr_ref[...], drc_s[...]
            t = lax.broadcasted_iota(jnp.int32, x.shape, 0)
            dx = cw_ref[2:3, :] * gv
            for k in (0, 1, 3):
                dx = dx + cw_ref[k:k + 1, :] * _conv_shift(gv, t, L, 4 - k)
            drec_s[...] = dx.astype(BF16)
            for k in range(4):
                dcw_ref[k:k + 1, :] += _rsum(_conv_shift(x, t, L, k) * gv)
            dcw_ref[4:5, :] += _rsum(gv)

    seq = lambda off: pl.BlockSpec((L, LRU_BW), lambda g, b, h: (b, off + g))
    vec = pl.BlockSpec((SUBLANES, LRU_BW), lambda g, b, h: (0, g))
    gate = pl.BlockSpec((2, LRU_BW, LRU_BW), lambda g, b, h: (0, g, 0))
    T = B * L
    f32_seq = pltpu.VMEM((L, LRU_BW), F32)
    dz, d_wa, d_wx, d_gvec, d_convwb = pl.pallas_call(
        body, name="rg_mix_bwd", grid=(n_g, B, 2),
        in_specs=[seq(0), seq(0), seq(n_g), seq(0), seq(0), vec,
                  pl.BlockSpec((LRU_BW, 4 * LRU_BW), lambda g, b, h: (g, 0)), vec] + [_ANY] * len(order),
        out_specs=[pl.BlockSpec((L, LRU_BW), lambda g, b, h: (b, h * n_g + g)), gate, gate, vec, vec],
        out_shape=[jax.ShapeDtypeStruct((T, 2 * D_MODEL), BF16),
                   jax.ShapeDtypeStruct((2, D_MODEL, LRU_BW), F32), jax.ShapeDtypeStruct((2, D_MODEL, LRU_BW), F32),
                   jax.ShapeDtypeStruct((SUBLANES, D_MODEL), F32), jax.ShapeDtypeStruct((SUBLANES, D_MODEL), F32)],
        scratch_shapes=[f32_seq] * 9 + [pltpu.VMEM((L, LRU_BW), BF16)],
        compiler_params=_params(("arbitrary", "arbitrary", "arbitrary")),
    )(dyg, z, z, h_f, h_b, conv_wb, wcat, gvec, *order)
    return dz, d_wa.reshape(2 * D_MODEL, LRU_BW), d_wx.reshape(2 * D_MODEL, LRU_BW), d_gvec, d_convwb


def _make_wcat(w_a, w_x):
    g = jnp.stack([w_a[0, 0], w_x[0, 0], w_a[0, 1], w_x[0, 1]])
    return jnp.transpose(g, (1, 2, 0, 3)).reshape(D_MODEL, 4 * LRU_BW)


def _rows_at(part, first):
    return jnp.pad(part, ((first, SUBLANES - first - part.shape[0]), (0, 0)))


def _qk_slot(q_g, k_g):
    wide = lambda v, at: jnp.pad(v, ((0, SUBLANES - 1), (at, D_MODEL - at - HEAD_DIM)))
    return wide(q_g, 0) + wide(k_g, HEAD_DIM)


def _local_step(x, target, P, fetch, emit, B, L, after=None):
    g_mix, g_mlp = P["norm_mix_g"], P["norm_mlp_g"]
    h0 = _rms_fwd(x, g_mix[0:1], "rg_norm", after=after)
    w_in, w_out, conv_wb, wcat, gvec = fetch("rg", h0)
    z = _mm(h0, w_in, mode="nn", b_shard=True, name="rg_in")
    h_f, h_b, yg = _rg_mix_fwd(z, conv_wb, wcat, gvec, B, L)
    x1, h1 = _mm_res_norm(yg, w_out, x, g_mlp[0:1], "rg_out")
    (x2, h3), mlp0 = _mlp_fwd(x1, h1, fetch, 0, lambda a, w, res, name: _mm_res_norm(a, w, res, g_mix[1:2], name))
    w_qkv, w_o = fetch("att", h3)
    qkv = _mm(h3, w_qkv, mode="nn", b_shard=True, name="attn_qkv")
    cos, sin = _rope_tables(L, B)
    qh, kh, vh = _qk_prep(qkv, cos, sin, P["q_g"], P["k_g"])
    o = _attn_fwd(qh, kh, vh, B, L)
    x3, h4 = _mm_res_norm(o, w_o, x2, g_mlp[1:2], "attn_out")
    (dx4, dx4_bf, loss_acc, d_final_g), mlp1 = _mlp_fwd(
        x3, h4, fetch, 1, lambda a, w, res, name: _mm_final_loss(a, w, res, target, P["final_g"], name))

    dx3, dx3_bf, dg_mlp1, d_up1, d_down1 = _mlp_bwd(x3, g_mlp[1:2], mlp1, dx4, dx4_bf, 1, None)
    tok = emit("mlp1", [d_up1, d_down1])
    d_wo = _mm(o, dx3_bf, mode="tn", out_dtypes=(BF16,), name="attn_dwo", after=tok)
    do = _mm(dx3_bf, w_o, mode="nt", out_dtypes=(BF16,), name="attn_do")
    dq, dk, dv = _attn_bwd(qh, kh, vh, do, B, L)
    dqkv, dq_g, dk_g = _qk_prep_bwd(qkv, dq, dk, dv, cos, sin, P["q_g"], P["k_g"])
    d_wqkv = _mm(h3, dqkv, mode="tn", o_shard=True, out_dtypes=(BF16,), name="attn_dwqkv")
    tok = emit("att", [d_wqkv, d_wo])
    dx2, dx2_bf, dg_mix1 = _mm_norm_bwd(dqkv, w_qkv, x2, dx3, g_mix[1:2], "attn_dh", after=tok)
    tok = emit("point_attn_done", [dx2_bf])
    dx1, dx1_bf, dg_mlp0, d_up0, d_down0 = _mlp_bwd(x1, g_mlp[0:1], mlp0, dx2, dx2_bf, 0, tok)
    tok = emit("mlp0", [d_up0, d_down0])
    d_wout = _mm(yg, dx1_bf, mode="tn", out_dtypes=(BF16,), name="rg_dwout", after=tok)
    tok = emit("rg_out", [d_wout])
    dyg = _mm(dx1_bf, w_out, mode="nt", name="rg_dyg", after=tok)
    dz, d_wa, d_wx, d_gvec, d_convwb = _rg_mix_bwd(dyg, z, h_f, h_b, conv_wb, wcat, gvec, B, L)
    tok = emit("gates", [d_wa, d_wx])
    tok = emit("point_mix_done", [dz if tok is None else tok])
    d_win = _mm(h0, dz, mode="tn", o_shard=True, out_dtypes=(BF16,), name="rg_dwin", after=tok)
    tok = emit("rg_in", [d_win])
    grad_x, _, dg_mix0 = _mm_norm_bwd(dz, w_in, x, dx1, g_mix[0:1], "rg_dh", after=tok)

    norms = (_rows_at(dg_mix0, 0) + _rows_at(dg_mix1, 1) + _rows_at(dg_mlp0, 2) + _rows_at(dg_mlp1, 3)
             + _rows_at(d_final_g, 4)
             + jnp.pad(loss_acc, ((LOSS_ROW, SUBLANES - 1 - LOSS_ROW), (0, D_MODEL - LANES))))
    vec = jnp.concatenate([norms, d_convwb, d_gvec, _qk_slot(dq_g, dk_g)], axis=0)
    return grad_x, vec


_MESH = pl.DeviceIdType.MESH


def _place():
    x, y, c = lax.axis_index("x"), lax.axis_index("y"), lax.axis_index("c")
    peers = [((1 - x) if j & 2 else x, (1 - y) if j & 1 else y) for j in (1, 2, 3)]
    return x, y, c, peers


def _comm_call(body, ins, out_shapes, n_sem, name):
    return pl.pallas_call(
        body, name=name, in_specs=[_ANY] * len(ins), out_specs=[_ANY] * len(out_shapes), out_shape=out_shapes,
        scratch_shapes=[pltpu.SemaphoreType.DMA((n_sem,)), pltpu.SemaphoreType.DMA((n_sem,)),
                        pltpu.SemaphoreType.DMA((len(ins),))],
    )(*ins)


def _all_devices_slots(v, name):
    def body(v_ref, out_ref, send, recv, lsem):
        x, y, c = lax.axis_index("x"), lax.axis_index("y"), lax.axis_index("c")
        me = 4 * x + 2 * y + c

        def peer(j):
            return (1 - x) if j & 4 else x, (1 - y) if j & 2 else y, (1 - c) if j & 1 else c

        def copy(j, slot):
            return pltpu.make_async_remote_copy(
                src_ref=v_ref, dst_ref=out_ref.at[slot], send_sem=send.at[j - 1], recv_sem=recv.at[j - 1],
                device_id=peer(j), device_id_type=_MESH)

        local = pltpu.make_async_copy(v_ref, out_ref.at[me], lsem.at[0])
        sends = [copy(j, me) for j in range(1, N_DEVICES)]
        for cp in [local] + sends:
            cp.start()
        for j in range(1, N_DEVICES):
            px, py, pc = peer(j)
            copy(j, 4 * px + 2 * py + pc).wait_recv()
        for cp in sends:
            cp.wait_send()
        local.wait()

    shape = jax.ShapeDtypeStruct((N_DEVICES,) + v.shape, v.dtype)
    return _comm_call(body, [v], [shape], N_DEVICES - 1, name)[0]


def _sum_leading(slots, name):
    def body(s_ref, o_ref):
        acc = s_ref[0]
        for d in range(1, slots.shape[0]):
            acc = acc + s_ref[d]
        o_ref[...] = acc

    return pl.pallas_call(body, name=name, out_shape=jax.ShapeDtypeStruct(slots.shape[1:], slots.dtype))(slots)


_HBM = pl.BlockSpec(memory_space=pltpu.HBM)
_SEM = pl.BlockSpec(memory_space=pltpu.SEMAPHORE)
_EFFECT = pltpu.SideEffectType.DATAFLOW_SIDE_EFFECTING


_COPIES = dict(gather=N_CHIPS - 1, scatter=N_CHIPS - 1, swap=1)


def _split_copies(kind, srcs, lands, send, recv):
    x, y, c, peers = _place()
    me = 2 * x + y
    per = _COPIES[kind]
    out = []
    for a in range(len(lands)):
        for j in range(per):
            if kind == "swap":
                src, there, here, dev = srcs[a], lands[a], lands[a], (x, y, 1 - c)
            else:
                px, py = peers[j]
                dev = (px, py, c)
                if kind == "gather":
                    src, there, here = lands[a].at[me], lands[a].at[me], lands[a].at[2 * px + py]
                else:
                    src, there, here = srcs[a].at[2 * px + py], lands[a].at[j], lands[a].at[j]
            mk = functools.partial(
                pltpu.make_async_remote_copy, src_ref=src, send_sem=send.at[per * a + j],
                recv_sem=recv.at[per * a + j], device_id=dev, device_id_type=_MESH)
            out.append((functools.partial(mk, dst_ref=there), functools.partial(mk, dst_ref=here)))
    return out


def _exchange_start(kind, srcs, lands, name, after=None):
    arrays = list(srcs) + list(lands)
    n_s, n, n_all = len(srcs), len(lands), len(srcs) + len(lands)
    n_sem = _COPIES[kind] * n
    order = _after_operand(after)
    n_x = len(order)

    def body(*refs):
        send, recv = refs[n_all + n_x], refs[n_all + n_x + 1]
        token = refs[-1]
        for started, _ in _split_copies(kind, refs[:n_s], refs[n_s:n_all], send, recv):
            started().start()
        token[...] = jnp.zeros(token.shape, F32)

    res = pl.pallas_call(
        body, name=name,
        out_shape=(pltpu.SemaphoreType.DMA((n_sem,)), pltpu.SemaphoreType.DMA((n_sem,)),
                   *[pltpu.HBM(a.shape, a.dtype) for a in arrays], jax.ShapeDtypeStruct((SUBLANES, LANES), F32)),
        in_specs=[_HBM] * n_all + [_ANY] * n_x,
        out_specs=(_SEM, _SEM, *[_HBM] * n_all, pl.BlockSpec(memory_space=pltpu.VMEM)),
        input_output_aliases={i: 2 + i for i in range(n_all)},
        compiler_params=pltpu.CompilerParams(has_side_effects=_EFFECT),
    )(*[pltpu.with_memory_space_constraint(a, pltpu.HBM) for a in arrays], *order)
    return (res[0], res[1], res[2:2 + n_s], res[2 + n_s:2 + n_all]), res[-1]


def _exchange_wait(kind, handle, after, name):
    send, recv, srcs, lands = handle
    arrays = list(srcs) + list(lands)
    n_s, n_all = len(srcs), len(arrays)
    order = list(after) if isinstance(after, (list, tuple)) else [after]

    def body(*refs):
        for started, landing in _split_copies(kind, refs[:n_s], refs[n_s:n_all], refs[n_all], refs[n_all + 1]):
            started().wait_send()
            landing().wait_recv()

    res = pl.pallas_call(
        body, name=name, out_shape=[pltpu.HBM(a.shape, a.dtype) for a in arrays],
        in_specs=[_HBM] * n_all + [_SEM, _SEM] + [_ANY] * len(order), out_specs=[_HBM] * n_all,
        input_output_aliases={i: i for i in range(n_all)},
        compiler_params=pltpu.CompilerParams(has_side_effects=_EFFECT),
    )(*arrays, send, recv, *order)
    return res[:n_s], res[n_s:]


def _index_operand(i):
    return jnp.reshape(i, (1,)).astype(jnp.int32)


def _cast_into_slot(src, row0, rows, me, dtype, name, after=None, add=None):
    cols = src.shape[1]
    tm = min(512, rows)
    order = _after_operand(after)
    terms = [src] + ([] if add is None else [add])

    def body(me_ref, *rest):
        val = rest[0][...]
        if add is not None:
            val = val + rest[1][...]
        rest[-1][...] = val.astype(dtype)

    return pl.pallas_call(
        body, name=name,
        grid_spec=pltpu.PrefetchScalarGridSpec(
            num_scalar_prefetch=1, grid=(rows // tm,),
            in_specs=[pl.BlockSpec((tm, cols), lambda i, me_ref: (i + row0 // tm, 0))] * len(terms)
            + [_ANY] * len(order),
            out_specs=pl.BlockSpec((None, tm, cols), lambda i, me_ref: (me_ref[0], i, 0))),
        out_shape=jax.ShapeDtypeStruct((N_CHIPS, rows, cols), dtype), compiler_params=_params(("parallel",)),
    )(_index_operand(me), *terms, *order)


def _sum_slots(mine, r, me, name):
    _, rows, cols = r.shape
    tm = min(512, rows)

    def body(me_ref, own_ref, r_ref, o_ref):
        o_ref[...] = ((own_ref[...].astype(F32) + r_ref[0].astype(F32)) + r_ref[1].astype(F32)) + r_ref[2].astype(F32)

    return pl.pallas_call(
        body, name=name,
        grid_spec=pltpu.PrefetchScalarGridSpec(
            num_scalar_prefetch=1, grid=(rows // tm,),
            in_specs=[pl.BlockSpec((None, tm, cols), lambda i, me_ref: (me_ref[0], i, 0)),
                      pl.BlockSpec((N_CHIPS - 1, tm, cols), lambda i, me_ref: (0, i, 0))],
            out_specs=pl.BlockSpec((tm, cols), lambda i, me_ref: (i, 0))),
        out_shape=jax.ShapeDtypeStruct((rows, cols), F32), compiler_params=_params(("parallel",)),
    )(_index_operand(me), mine, r)


def _adamw(w, m, v, ps, qs, name):
    rows, cols = w.shape
    seg_rows = ps[0].shape[0]
    tm = min(256, seg_rows)
    while seg_rows % tm:
        tm -= SUBLANES
    per, n_seg = seg_rows // tm, len(ps)
    parts = list(ps) + ([] if qs is None else list(qs))

    def body(w_ref, m_ref, v_ref, *rest):
        g_refs, outs = rest[:len(parts)], rest[len(parts):]
        grad = lambda s: g_refs[s][...] if qs is None else g_refs[s][...] + g_refs[n_seg + s][...]
        g = grad(0)
        for s in range(1, n_seg):
            g = jnp.where(pl.program_id(0) >= s * per, grad(s), g)
        m1 = ADAM_B1 * m_ref[...] + (1.0 - ADAM_B1) * g
        v1 = ADAM_B2 * v_ref[...] + (1.0 - ADAM_B2) * (g * g)
        m_hat = m1 / (1.0 - ADAM_B1 ** ADAM_STEP)
        v_hat = v1 / (1.0 - ADAM_B2 ** ADAM_STEP)
        outs[0][...] = g
        outs[1][...] = (-ADAM_LR) * (m_hat / (jnp.sqrt(v_hat) + ADAM_EPS) + ADAM_WD * w_ref[...])
        outs[2][...] = m1
        outs[3][...] = v1

    row_spec = pl.BlockSpec((tm, cols), lambda i: (i, 0))
    seg_spec = lambda s: pl.BlockSpec((tm, cols), lambda i: (jnp.clip(i - s * per, 0, per - 1), 0))
    return pl.pallas_call(
        body, name=name, grid=(rows // tm,),
        in_specs=[row_spec] * 3 + [seg_spec(s) for s in range(n_seg)] * (1 if qs is None else 2),
        out_specs=[row_spec] * 4, out_shape=[jax.ShapeDtypeStruct((rows, cols), F32)] * 4,
        compiler_params=_params(("arbitrary",)),
    )(w, m, v, *parts)


def _put_cols(shard, me):
    full = jnp.zeros((shard.shape[0], D_MODEL), F32)
    return lax.dynamic_update_slice(full, shard, (0, me * (D_MODEL // N_CHIPS)))


def _gate_vec_slot(b_a, b_x, lam):
    return _rows_at(b_a, _ROW_BA) + _rows_at(b_x, _ROW_BX) + _rows_at(lam, _ROW_LAM)


def _pack_vec(p, me):
    return jnp.concatenate([
        _rows_at(p["norm_mix_g"], 0) + _rows_at(p["norm_mlp_g"], 2) + _rows_at(p["final_g"][None], 4),
        _rows_at(_put_cols(p["rg_conv_w"][0, :, 0, :], me), 0) + _rows_at(p["rg_conv_b"], 4),
        _gate_vec_slot(_put_cols(p["rg_b_a"][0], me), _put_cols(p["rg_b_x"][0], me), _put_cols(p["rg_lam"][0], me)),
        _qk_slot(p["at_q_g"], p["at_k_g"]),
    ], axis=0)


def _unpack_vec(r, me):
    def cols(rows):
        return lax.dynamic_slice(rows, (0, me * (D_MODEL // N_CHIPS)), (rows.shape[0], D_MODEL // N_CHIPS))

    gate = r[16:24]
    return dict(
        norm_mix_g=r[0:2], norm_mlp_g=r[2:4], final_g=r[4], rg_conv_w=cols(r[8:12])[None, :, None, :],
        rg_conv_b=r[12:13], rg_b_a=cols(gate[_ROW_BA:_ROW_BA + 2])[None], rg_b_x=cols(gate[_ROW_BX:_ROW_BX + 2])[None],
        rg_lam=cols(gate[_ROW_LAM:_ROW_LAM + 2])[None], at_q_g=r[24:25, 0:HEAD_DIM],
        at_k_g=r[24:25, HEAD_DIM:2 * HEAD_DIM])


_WEIGHTS = ['norm_mix_g', 'norm_mlp_g', 'rg_w_in', 'rg_conv_w', 'rg_conv_b', 'rg_w_a', 'rg_b_a', 'rg_w_x', 'rg_b_x',
            'rg_lam', 'rg_w_out', 'at_w_qkv', 'at_q_g', 'at_k_g', 'at_w_o', 'mlp_w_up', 'mlp_w_down', 'final_g']
_BIG = dict(rg_w_in=["rg_w_in"], rg_w_out=["rg_w_out"], at_w_qkv=["at_w_qkv"], at_w_o=["at_w_o"],
            mlp_w_up=["up0", "up1"], mlp_w_down=["down0", "down1"])


def kernel(x, *args):
    n_w = len(_WEIGHTS)
    w = dict(zip(_WEIGHTS, args[:n_w]))
    target = args[n_w]
    m = dict(zip(_WEIGHTS, args[n_w + 1:2 * n_w + 1]))
    v = dict(zip(_WEIGHTS, args[2 * n_w + 1:3 * n_w + 1]))
    B, L, _ = x.shape
    T = B * L
    me = 2 * lax.axis_index("x") + lax.axis_index("y")

    vec = jnp.concatenate([_gate_vec_slot(w["rg_b_a"][0], w["rg_b_x"][0], w["rg_lam"][0]),
                           _rows_at(w["rg_conv_w"][0, :, 0, :], 0)], axis=0)
    flat = lambda a: a.reshape(-1, a.shape[-1])
    rows_of = lambda k: w[k].shape[-2]
    groups = [("rg", [("rg_w_in", 0, BF16), ("rg_w_out", 0, BF16), (vec, 0, F32)]),
              ("mlp0", [("mlp_w_up", 0, BF16), ("mlp_w_down", 0, BF16)]),
              ("att", [("at_w_qkv", 0, BF16), ("at_w_o", 0, BF16)]),
              ("mlp1", [("mlp_w_up", 1, BF16), ("mlp_w_down", 1, BF16)])]
    gathers, tok = {}, None
    for group, members in groups:
        lands = []
        for n, (k, layer, dtype) in enumerate(members):
            src, rows = (flat(w[k]), rows_of(k)) if isinstance(k, str) else (k, k.shape[0])
            lands.append(_cast_into_slot(src, layer * rows, rows, me, dtype, f"place_{group}{n}", after=tok))
        gathers[group], tok = _exchange_start("gather", [], lands, f"gather_{group}_start", after=tok)
    wcat = _make_wcat(w["rg_w_a"], w["rg_w_x"]).astype(BF16)

    packs = [_pack_vec(p, me) for p in (w, m, v)]

    def fetch(group, after):
        order = [after, wcat] + packs if group == "rg" else after
        _, full = _exchange_wait("gather", gathers[group], order, f"gather_{group}_wait")
        if group == "rg":
            vec_full = jnp.transpose(full[2], (1, 0, 2)).reshape(2 * SUBLANES, D_MODEL)
            conv_wb = vec_full[SUBLANES:] + _rows_at(w["rg_conv_b"], 4)
            return full[0], full[1].reshape(D_MODEL, D_MODEL), conv_wb, wcat, vec_full[:SUBLANES]
        if group == "att":
            return full[0], full[1].reshape(D_MODEL, D_MODEL)
        return full[0], full[1].reshape(4 * D_MODEL, D_MODEL)

    names = dict(mlp1=["up1", "down1"], att=["at_w_qkv", "at_w_o"], mlp0=["up0", "down0"], rg_out=["rg_w_out"],
                 rg_in=["rg_w_in"], gates=["rg_w_a", "rg_w_x"])
    scatters, swaps, P, Q, res = {}, [], {}, {}, {}

    def start_scatter(group, grads):
        srcs = [g.reshape(N_CHIPS, -1, g.shape[-1]) for g in grads]
        lands = [lax.empty((N_CHIPS - 1,) + s.shape[1:], s.dtype) for s in srcs]
        scatters[group], token = _exchange_start("scatter", srcs, lands, f"scatter_{group}_start")
        return token

    def settle(groups, after):
        keys, parts = [], []
        for group in groups:
            srcs, lands = _exchange_wait("scatter", scatters[group], after, f"scatter_{group}_wait")
            for k, s, r in zip(names[group], srcs, lands):
                keys.append(k)
                parts.append(_sum_slots(s, r, me, f"sum_{k}"))
        handle, token = _exchange_start("swap", parts, [lax.empty(p.shape, F32) for p in parts],
                                        f"swap_{groups[0]}_start")
        swaps.append((keys, handle, f"swap_{groups[0]}_wait"))
        return token

    def finish(after):
        for keys, handle, name in swaps:
            mine, theirs = _exchange_wait("swap", handle, after, name)
            P.update(zip(keys, mine))
            Q.update(zip(keys, theirs))
        swaps.clear()
        last = after
        for k, parts in _BIG.items():
            if k in res or any(p not in P for p in parts):
                continue
            shape = w[k].shape
            two_d = lambda a: a.reshape(-1, shape[-1])
            outs = _adamw(two_d(w[k]), two_d(m[k]), two_d(v[k]), [P[p] for p in parts], [Q[p] for p in parts],
                          f"adamw_{k}")
            res[k] = [o.reshape(shape) for o in outs]
            last = outs[0]
        if "rg_w_a" in P and "gates" not in gathers:
            lands = [_cast_into_slot(P[k], 0, P[k].shape[0], me, F32, f"place_{k}", after=last, add=Q[k])
                     for k in names["gates"]]
            gathers["gates"], last = _exchange_start("gather", [], lands, "gather_gates_start", after=last)
        return last

    def emit(event, arrays):
        if event == "point_attn_done":
            return settle(["mlp1"], arrays[0])
        if event == "point_mix_done":
            return settle(["att", "mlp0", "rg_out"], arrays[0])
        token = start_scatter(event, arrays)
        if event == "rg_in":
            return finish(settle(["gates"], token))
        return token

    P_vec = dict(norm_mix_g=w["norm_mix_g"], norm_mlp_g=w["norm_mlp_g"], final_g=w["final_g"][None],
                 q_g=w["at_q_g"], k_g=w["at_k_g"])
    grad_x, vec_part = _local_step(x.reshape(T, D_MODEL), target.reshape(T, D_MODEL), P_vec, fetch, emit, B, L,
                                   after=tok)

    finish(settle(["rg_in"], grad_x))
    _, gate_grads = _exchange_wait("gather", gathers["gates"], grad_x, "gather_gates_wait")
    for k, g in zip(names["gates"], gate_grads):
        two_d = lambda a: a.reshape(g.shape[0] * g.shape[1], g.shape[2])
        outs = _adamw(two_d(w[k]), two_d(m[k]), two_d(v[k]), [two_d(g)], None, f"adamw_{k}")
        res[k] = [o.reshape(w[k].shape) for o in outs]
    vec_grad = _sum_leading(_all_devices_slots(vec_part, "allreduce_vec"), "sum_vec")
    loss = vec_grad[LOSS_ROW, 0]
    outs = _adamw(*packs, [vec_grad], None, "adamw_vec")
    unpacked = [_unpack_vec(o, me) for o in outs]
    for k in _WEIGHTS:
        if k not in res:
            res[k] = [u[k] for u in unpacked]

    result = [loss, grad_x.reshape(B, L, D_MODEL)]
    for slot in range(4):
        result += [res[k][slot] for k in _WEIGHTS]
    return tuple(result)
```

```python
import functools
import math

import jax
import jax.numpy as jnp
import numpy as np
from jax import lax
from jax.experimental import pallas as pl
from jax.experimental.pallas import tpu as pltpu

F32 = jnp.float32
BF16 = jnp.bfloat16

D_MODEL = 1024
HEAD_DIM = 128
N_HEADS = 8
N_KV = 2
GROUP = N_HEADS // N_KV
LRU_BLOCKS = 8
LRU_BW = 128
GRID_W = 64
ROPE_THETA = 10000.0
EPS = 1e-6
RG_C = 8.0
SCALE = 1.0 / math.sqrt(HEAD_DIM)
N_CHIPS = 4

ADAM_LR = 0.001
ADAM_B1 = 0.9
ADAM_B2 = 0.999
ADAM_EPS = 1e-08
ADAM_WD = 0.01
ADAM_STEP = 10

V7X_VMEM_BYTES = 64 * 1024 * 1024
VMEM_LIMIT = V7X_VMEM_BYTES * 3 // 4
LANES = 128
SUBLANES = 8

N_DEVICES = 8
VEC_ROWS = 32
LOSS_ROW = 5


def _params(sem):
    return pltpu.CompilerParams(dimension_semantics=sem, vmem_limit_bytes=VMEM_LIMIT)


_ANY = pl.BlockSpec(memory_space=pl.ANY)
_NN = (((1,), (0,)), ((), ()))
_NT = (((1,), (1,)), ((), ()))
_TN = (((0,), (0,)), ((), ()))


def _after_operand(after):
    return [] if after is None else [after]


def _fit(t, n):
    if n <= t:
        return n
    c = (t // LANES) * LANES
    while n % c:
        c -= LANES
    return c


MM_VMEM_BUDGET = VMEM_LIMIT * 3 // 4
def _mm_tiles(M, K, ns, n_total, out_dtypes, extras, whole_rows):
    for tm in (2048, 1024, 512, 256, 128):
        for tn in ((ns,) if whole_rows else (1024, 512, 256)):
            tn = _fit(tn, ns)
            per_row = 2 * (2 * K) + 4 * tn + sum(2 * tn * jnp.dtype(d).itemsize for d in out_dtypes)
            per_row += sum(2 * tn * e.dtype.itemsize for e in extras)
            b_buffers = 1 if tn == n_total else 2
            if M % tm == 0 and b_buffers * (2 * K * tn) + tm * per_row <= MM_VMEM_BUDGET:
                return tm, tn
    raise ValueError(f"no tile fits VMEM for M={M} K={K} N={ns}")


def _mm(a, b, *, mode, name, out_dtypes=(F32,), b_shard=False, o_shard=False, extras=(), epi=None, after=None,
        bcast=(), accs=(), ref_epi=None):
    if mode == "tn":
        K, M = a.shape
        N = b.shape[1]
    else:
        M, K = a.shape
        if mode == "nn":
            N = b.shape[0] * b.shape[2] if b_shard else b.shape[1]
        else:
            N = b.shape[1] if b_shard else b.shape[0]
    ns = N
    if b_shard and mode == "nn":
        ns = b.shape[2]
    elif o_shard:
        ns = N // N_CHIPS
    tm, tn = _mm_tiles(M, K, ns, N, out_dtypes, extras, whole_rows=ref_epi is not None)
    if ref_epi is not None:
        tm = min(tm, 512)
    grid = (M // tm, N // tn)
    q = ns // tn
    once = dict(pipeline_mode=pl.Buffered(1)) if tn == N else {}

    if mode == "tn":
        a_spec = pl.BlockSpec((K, tm), lambda i, j: (0, i))
        b_spec = pl.BlockSpec((K, tn), lambda i, j: (0, j), **once)
        dims = _TN
    elif mode == "nn":
        a_spec = pl.BlockSpec((tm, K), lambda i, j: (i, 0))
        if b_shard:
            b_spec = pl.BlockSpec((None, K, tn), lambda i, j: (j // q, 0, j % q), **once)
        else:
            b_spec = pl.BlockSpec((K, tn), lambda i, j: (0, j), **once)
        dims = _NN
    else:
        a_spec = pl.BlockSpec((tm, K), lambda i, j: (i, 0))
        if b_shard:
            ks = b.shape[2]
            b_spec = pl.BlockSpec((N_CHIPS, tn, ks), lambda i, j: (0, j, 0), **once)
        else:
            b_spec = pl.BlockSpec((tn, K), lambda i, j: (j, 0), **once)
        dims = _NT

    if o_shard:
        o_specs = [pl.BlockSpec((None, tm, tn), lambda i, j: (j // q, i, j % q))]
        o_shapes = [jax.ShapeDtypeStruct((N_CHIPS, M, ns), out_dtypes[0])]
    else:
        o_specs = [pl.BlockSpec((tm, tn), lambda i, j: (i, j)) for _ in out_dtypes]
        o_shapes = [jax.ShapeDtypeStruct((M, N), dt) for dt in out_dtypes]
    e_specs = [pl.BlockSpec((tm, tn), lambda i, j: (i, j)) for _ in extras]
    e_specs += [pl.BlockSpec(v.shape, lambda i, j: (0, 0)) for v in bcast]
    o_specs += [pl.BlockSpec(s, lambda i, j: (0, 0)) for s in accs]
    o_shapes += [jax.ShapeDtypeStruct(s, F32) for s in accs]
    n_e, n_b, n_o, n_a = len(extras), len(bcast), len(out_dtypes), len(accs)
    order = _after_operand(after)
    n_x = len(order)
    if epi is None:
        epi = lambda acc: (acc,)

    def body(a_ref, b_ref, *rest):
        e_refs, b_refs = rest[:n_e], rest[n_e:n_e + n_b]
        o_refs = rest[n_e + n_b + n_x:n_e + n_b + n_x + n_o]
        a_refs = rest[n_e + n_b + n_x + n_o:]
        if n_a:
            @pl.when((pl.program_id(0) == 0) & (pl.program_id(1) == 0))
            def _():
                for r in a_refs:
                    r[...] = jnp.zeros(r.shape, F32)
        if mode == "nt" and b_shard:
            acc = None
            for s in range(N_CHIPS):
                part = lax.dot_general(a_ref[:, s * ks:(s + 1) * ks], b_ref[s], dims, preferred_element_type=F32)
                acc = part if acc is None else acc + part
        else:
            acc = lax.dot_general(a_ref[...], b_ref[...], dims, preferred_element_type=F32)
        if ref_epi is not None:
            ref_epi(acc, e_refs, b_refs, o_refs, a_refs)
            return
        outs = epi(acc, *[r[...] for r in e_refs])
        for r, o in zip(o_refs, outs):
            r[...] = o.astype(r.dtype)

    outs = pl.pallas_call(
        body, name=name, grid=grid, in_specs=[a_spec, b_spec] + e_specs + [_ANY] * n_x, out_specs=o_specs,
        out_shape=o_shapes, compiler_params=_params(("arbitrary", "arbitrary") if n_a else ("parallel", "parallel")),
    )(a, b, *extras, *bcast, *order)
    return outs[0] if n_o + n_a == 1 else outs


def _rowwise(fn, rows, bcast, outs, accs=(), *, tm, name, after=None):
    def norm(r):
        return r if isinstance(r, tuple) else (r, r.shape[1], 0)

    rows = [norm(r) for r in rows]
    T = rows[0][0].shape[0]
    tm = min(tm, T)
    while T % tm:
        tm -= SUBLANES
    n_r, n_b, n_o, n_a = len(rows), len(bcast), len(outs), len(accs)
    order = _after_operand(after)
    n_x = len(order)
    in_specs = [pl.BlockSpec((tm, c), functools.partial(lambda i, cb: (i, cb), cb=cb)) for _, c, cb in rows]
    in_specs += [pl.BlockSpec(b.shape, lambda i: (0, 0)) for b in bcast] + [_ANY] * n_x
    out_specs = [pl.BlockSpec((tm, o[0]), lambda i: (i, 0)) for o in outs]
    out_specs += [pl.BlockSpec(s, lambda i: (0, 0)) for s in accs]
    out_shape = [jax.ShapeDtypeStruct((T, o[2] if len(o) > 2 else o[0]), o[1]) for o in outs]
    out_shape += [jax.ShapeDtypeStruct(s, F32) for s in accs]

    def body(*refs):
        in_refs = refs[:n_r]
        b_refs = refs[n_r:n_r + n_b]
        o_refs = refs[n_r + n_b + n_x:n_r + n_b + n_x + n_o]
        a_refs = refs[n_r + n_b + n_x + n_o:]
        if n_a:
            @pl.when(pl.program_id(0) == 0)
            def _():
                for r in a_refs:
                    r[...] = jnp.zeros(r.shape, F32)
        fn(in_refs, b_refs, o_refs, a_refs)

    res = pl.pallas_call(
        body, name=name, grid=(T // tm,), in_specs=in_specs, out_specs=out_specs, out_shape=out_shape,
        compiler_params=_params(("arbitrary",) if n_a else ("parallel",)),
    )(*[r[0] for r in rows], *bcast, *order)
    return res


def _rsum(x):
    return jnp.sum(x, axis=0, keepdims=True)


def _rms_fwd(x, g, name, after=None):
    def fn(ins, bs, outs, accs):
        xv = ins[0][...]
        r = lax.rsqrt(jnp.mean(xv * xv, axis=-1, keepdims=True) + EPS)
        outs[0][...] = (xv * r * bs[0][...]).astype(BF16)

    return _rowwise(fn, [x], [g], [(D_MODEL, BF16)], tm=512, name=name, after=after)[0]


def _rms_bwd_math(xv, dh, g):
    r = lax.rsqrt(jnp.mean(xv * xv, axis=-1, keepdims=True) + EPS)
    hn = xv * r
    dgh = dh * g
    dx = r * (dgh - hn * jnp.mean(dgh * hn, axis=-1, keepdims=True))
    return dx, _rsum(dh * hn)


def _mm_norm_bwd(dy, w, x, dres, g, name, after=None):
    def epilogue(acc, e_refs, b_refs, o_refs, a_refs):
        dx, dg = _rms_bwd_math(e_refs[0][...], acc, b_refs[0][...])
        dx = dx + e_refs[1][...]
        o_refs[0][...] = dx
        o_refs[1][...] = dx.astype(BF16)
        a_refs[0][...] += dg

    return _mm(dy, w, mode="nt", b_shard=True, out_dtypes=(F32, BF16), extras=(x, dres), bcast=(g,),
               accs=((1, D_MODEL),), ref_epi=epilogue, name=name, after=after)


def _mm_res_norm(a, w, res, g, name):
    def epilogue(acc, e_refs, b_refs, o_refs, a_refs):
        xv = acc + e_refs[0][...]
        o_refs[0][...] = xv
        r = lax.rsqrt(jnp.mean(xv * xv, axis=-1, keepdims=True) + EPS)
        o_refs[1][...] = (xv * r * b_refs[0][...]).astype(BF16)

    return _mm(a, w, mode="nn", out_dtypes=(F32, BF16), extras=(res,), bcast=(g,), ref_epi=epilogue, name=name)


def _mm_final_loss(a, w, res, target, g, name):
    def epilogue(acc, e_refs, b_refs, o_refs, a_refs):
        xv = acc + e_refs[0][...]
        gv = b_refs[0][...]
        r = lax.rsqrt(jnp.mean(xv * xv, axis=-1, keepdims=True) + EPS)
        e = xv * r * gv - e_refs[1][...]
        tok = jnp.mean(e * e, axis=-1, keepdims=True)
        a_refs[0][...] += 0.5 * jnp.sum(tok, axis=0, keepdims=True) * jnp.ones((1, LANES), F32)
        dx, dg = _rms_bwd_math(xv, e * (1.0 / D_MODEL), gv)
        o_refs[0][...] = dx
        o_refs[1][...] = dx.astype(BF16)
        a_refs[1][...] += dg

    return _mm(a, w, mode="nn", out_dtypes=(F32, BF16), extras=(res, target), bcast=(g,),
               accs=((1, LANES), (1, D_MODEL)), ref_epi=epilogue, name=name)


def _relu2(acc):
    r = jnp.maximum(acc, 0.0)
    return r * r, r


def _mlp_fwd(x, h, fetch, tag, finish):
    w_up, w_down = fetch(f"mlp{tag}", h)
    a, r = _mm(h, w_up, mode="nn", b_shard=True, out_dtypes=(BF16, BF16), epi=_relu2, name=f"mlp{tag}_up")
    return finish(a, w_down, x, f"mlp{tag}_down"), (h, a, r, w_up, w_down)


def _mlp_bwd(x, g, saved, dx, dx_bf, tag, after):
    h, a, r, w_up, w_down = saved
    d_down = _mm(a, dx_bf, mode="tn", out_dtypes=(BF16,), name=f"mlp{tag}_dwdown", after=after)
    dup = _mm(dx_bf, w_down, mode="nt", extras=(r,), out_dtypes=(BF16,),
              epi=lambda acc, rv: (acc * (2.0 * rv.astype(F32)),), name=f"mlp{tag}_dup")
    d_up = _mm(h, dup, mode="tn", o_shard=True, out_dtypes=(BF16,), name=f"mlp{tag}_dwup")
    dx_new, dx_new_bf, dg = _mm_norm_bwd(dup, w_up, x, dx, g, f"mlp{tag}_dh")
    return dx_new, dx_new_bf, dg, d_up, d_down


def _rope_tables(L, B):
    rows = L // GRID_W
    row = np.repeat(np.arange(rows, dtype=np.float32), GRID_W)
    col = np.tile(np.arange(GRID_W, dtype=np.float32), rows)
    inv = (ROPE_THETA ** (-np.arange(HEAD_DIM // 4, dtype=np.float32) / (HEAD_DIM // 4))).astype(np.float32)
    ar, ac = row[:, None] * inv, col[:, None] * inv
    cos = np.concatenate([np.cos(ar), np.cos(ar), np.cos(ac), np.cos(ac)], axis=-1)
    sin = np.concatenate([-np.sin(ar), np.sin(ar), -np.sin(ac), np.sin(ac)], axis=-1)
    return jnp.asarray(np.tile(cos, (B, 1)), F32), jnp.asarray(np.tile(sin, (B, 1)), F32)


def _swap_halves(x):
    lane = lax.broadcasted_iota(jnp.int32, x.shape, 1)
    return jnp.where((lane % 64) < 32, pltpu.roll(x, HEAD_DIM - 32, 1), pltpu.roll(x, 32, 1))


def _qk_prep(qkv, cos, sin, q_g, k_g):
    def fn(ins, bs, outs, accs):
        c, s = ins[1][...], ins[2][...]
        for h in range(N_HEADS + N_KV):
            xv = ins[0][:, h * HEAD_DIM:(h + 1) * HEAD_DIM]
            g = bs[0][...] if h < N_HEADS else bs[1][...]
            r = lax.rsqrt(jnp.mean(xv * xv, axis=-1, keepdims=True) + EPS)
            z = xv * r * g
            y = (z * c + _swap_halves(z) * s).astype(BF16)
            if h < N_HEADS:
                outs[0][:, h * HEAD_DIM:(h + 1) * HEAD_DIM] = y
            else:
                outs[1][:, (h - N_HEADS) * HEAD_DIM:(h - N_HEADS + 1) * HEAD_DIM] = y
        outs[2][...] = ins[0][:, (N_HEADS + N_KV) * HEAD_DIM:].astype(BF16)

    kvw = N_KV * HEAD_DIM
    return _rowwise(fn, [qkv, cos, sin], [q_g, k_g], [(D_MODEL, BF16), (kvw, BF16), (kvw, BF16)], tm=512,
                    name="attn_qk_prep")


def _qk_prep_bwd(qkv, dq, dk, dv, cos, sin, q_g, k_g):
    def fn(ins, bs, outs, accs):
        c, s = ins[4][...], ins[5][...]
        for h in range(N_HEADS + N_KV):
            sl = slice(h * HEAD_DIM, (h + 1) * HEAD_DIM)
            xv = ins[0][:, sl]
            if h < N_HEADS:
                g, dy, acc = bs[0][...], ins[1][:, sl], accs[0]
            else:
                ks = slice((h - N_HEADS) * HEAD_DIM, (h - N_HEADS + 1) * HEAD_DIM)
                g, dy, acc = bs[1][...], ins[2][:, ks], accs[1]
            r = lax.rsqrt(jnp.mean(xv * xv, axis=-1, keepdims=True) + EPS)
            xn = xv * r
            dz = dy * c - _swap_halves(dy) * s
            acc[...] += _rsum(dz * xn)
            dxn = dz * g
            outs[0][:, sl] = (r * (dxn - xn * jnp.mean(dxn * xn, axis=-1, keepdims=True))).astype(BF16)
        outs[0][:, (N_HEADS + N_KV) * HEAD_DIM:] = ins[3][...].astype(BF16)

    return _rowwise(fn, [qkv, dq, dk, dv, cos, sin], [q_g, k_g], [(qkv.shape[1], BF16)],
                    [(1, HEAD_DIM), (1, HEAD_DIM)], tm=256, name="attn_qk_prep_bwd")


_EXP2_SCALE = SCALE * math.log2(math.e)


def _exp_rows(q, k):
    s = lax.dot_general(q, k, _NT, preferred_element_type=F32)
    p = jnp.exp2((s - jnp.max(s, axis=-1, keepdims=True)) * _EXP2_SCALE)
    return p, jnp.sum(p, axis=-1, keepdims=True)


def _attn_fwd(q, k, v, B, L, tq=2048, sub=256):
    tq = min(tq, L)
    sub = min(sub, tq)
    nq = L // tq

    def body(q_ref, k_ref, v_ref, o_ref):
        kv, vv = k_ref[...], v_ref[...]
        for c in range(tq // sub):
            rows = slice(c * sub, (c + 1) * sub)
            p, l = _exp_rows(q_ref[rows, :], kv)
            o = jnp.dot(p.astype(BF16), vv, preferred_element_type=F32)
            o_ref[rows, :] = (o * (1.0 / l)).astype(o_ref.dtype)

    return pl.pallas_call(
        body, name="attn_fwd", grid=(B, N_HEADS, nq),
        in_specs=[pl.BlockSpec((tq, HEAD_DIM), lambda b, h, i: (b * nq + i, h)),
                  pl.BlockSpec((L, HEAD_DIM), lambda b, h, i: (b, h // GROUP)),
                  pl.BlockSpec((L, HEAD_DIM), lambda b, h, i: (b, h // GROUP))],
        out_specs=pl.BlockSpec((tq, HEAD_DIM), lambda b, h, i: (b * nq + i, h)),
        out_shape=jax.ShapeDtypeStruct((B * L, D_MODEL), BF16),
        compiler_params=_params(("parallel", "parallel", "parallel")),
    )(q, k, v)


def _attn_bwd(q, k, v, do, B, L, tq=2048, sub=512):
    tq = min(tq, L)
    sub = min(sub, tq)
    nq = L // tq

    def body(q_ref, k_ref, v_ref, do_ref, dq_ref, dk_ref, dv_ref):
        @pl.when((pl.program_id(2) == 0) & (pl.program_id(3) == 0))
        def _():
            dk_ref[...] = jnp.zeros(dk_ref.shape, F32)
            dv_ref[...] = jnp.zeros(dv_ref.shape, F32)

        kv, vv = k_ref[...], v_ref[...]
        ps, es, dos, qs = [], [], [], []
        for c in range(tq // sub):
            rows = slice(c * sub, (c + 1) * sub)
            qc, doc = q_ref[rows, :], do_ref[rows, :]
            p, l = _exp_rows(qc, kv)
            inv = 1.0 / l
            dp = lax.dot_general(doc, vv, _NT, preferred_element_type=F32)
            delta = jnp.sum(p * dp, axis=-1, keepdims=True) * inv
            e = (p * (dp - delta)).astype(BF16)
            dq_ref[rows, :] = jnp.dot(e, kv, preferred_element_type=F32) * (inv * SCALE)
            ps.append(p.astype(BF16))
            es.append(e)
            dos.append((doc.astype(F32) * inv).astype(BF16))
            qs.append((qc.astype(F32) * (inv * SCALE)).astype(BF16))
        cat = lambda xs: xs[0] if len(xs) == 1 else jnp.concatenate(xs, axis=0)
        dv_ref[...] += lax.dot_general(cat(ps), cat(dos), _TN, preferred_element_type=F32)
        dk_ref[...] += lax.dot_general(cat(es), cat(qs), _TN, preferred_element_type=F32)

    qmap = lambda b, kh, g, i: (b * nq + i, kh * GROUP + g)
    kmap = lambda b, kh, g, i: (b, kh)
    kvw = N_KV * HEAD_DIM
    return pl.pallas_call(
        body, name="attn_bwd", grid=(B, N_KV, GROUP, nq),
        in_specs=[pl.BlockSpec((tq, HEAD_DIM), qmap), pl.BlockSpec((L, HEAD_DIM), kmap),
                  pl.BlockSpec((L, HEAD_DIM), kmap), pl.BlockSpec((tq, HEAD_DIM), qmap)],
        out_specs=[pl.BlockSpec((tq, HEAD_DIM), qmap), pl.BlockSpec((L, HEAD_DIM), kmap),
                   pl.BlockSpec((L, HEAD_DIM), kmap)],
        out_shape=[jax.ShapeDtypeStruct((B * L, D_MODEL), F32), jax.ShapeDtypeStruct((B * L, kvw), F32),
                   jax.ShapeDtypeStruct((B * L, kvw), F32)],
        compiler_params=_params(("parallel", "parallel", "arbitrary", "arbitrary")),
    )(q, k, v, do)


def _conv_shift(x, t, L, k):
    if k == 2:
        return x
    if k < 2:
        return jnp.where(t >= 2 - k, pltpu.roll(x, 2 - k, 0), 0.0)
    return jnp.where(t < L - (k - 2), pltpu.roll(x, L - (k - 2), 0), 0.0)


def _conv_apply(x, w_ref, L):
    t = lax.broadcasted_iota(jnp.int32, x.shape, 0)
    acc = w_ref[4:5, :] + w_ref[2:3, :] * x
    for k in (0, 1, 3):
        acc = acc + w_ref[k:k + 1, :] * _conv_shift(x, t, L, k)
    return acc


def _conv_fwd(z, wb, B, L, tc=256):
    noff = D_MODEL // tc

    def body(z_ref, w_ref, o_ref):
        o_ref[...] = _conv_apply(z_ref[...], w_ref, L)

    return pl.pallas_call(
        body, name="rg_conv", grid=(B, noff),
        in_specs=[pl.BlockSpec((L, tc), lambda b, j: (b, noff + j)), pl.BlockSpec((SUBLANES, tc), lambda b, j: (0, j))],
        out_specs=pl.BlockSpec((L, tc), lambda b, j: (b, j)),
        out_shape=jax.ShapeDtypeStruct((B * L, D_MODEL), F32),
        compiler_params=_params(("parallel", "parallel")),
    )(z, wb)


def _conv_bwd(z, g, wb, dz, B, L, tc=256, after=None):
    noff = D_MODEL // tc
    order = _after_operand(after)

    def body(z_ref, g_ref, w_ref, dz_in, *rest):
        dx_ref, dw_ref = rest[len(order):]

        @pl.when(pl.program_id(1) == 0)
        def _():
            dw_ref[...] = jnp.zeros(dw_ref.shape, F32)

        x, gv = z_ref[...], g_ref[...]
        t = lax.broadcasted_iota(jnp.int32, x.shape, 0)
        dx = w_ref[2:3, :] * gv
        for k in (0, 1, 3):
            dx = dx + w_ref[k:k + 1, :] * _conv_shift(gv, t, L, 4 - k)
        dx_ref[...] = dx.astype(BF16)
        for k in range(4):
            dw_ref[k:k + 1, :] += _rsum(_conv_shift(x, t, L, k) * gv)
        dw_ref[4:5, :] += _rsum(gv)

    return pl.pallas_call(
        body, name="rg_conv_bwd", grid=(noff, B),
        in_specs=[pl.BlockSpec((L, tc), lambda j, b: (b, noff + j)), pl.BlockSpec((L, tc), lambda j, b: (b, j)),
                  pl.BlockSpec((SUBLANES, tc), lambda j, b: (0, j)), _ANY] + [_ANY] * len(order),
        out_specs=[pl.BlockSpec((L, tc), lambda j, b: (b, noff + j)),
                   pl.BlockSpec((SUBLANES, tc), lambda j, b: (0, j))],
        out_shape=[jax.ShapeDtypeStruct(dz.shape, dz.dtype), jax.ShapeDtypeStruct((SUBLANES, D_MODEL), F32)],
        input_output_aliases={3: 0},
        compiler_params=_params(("parallel", "arbitrary")),
    )(z, g, wb, dz, *order)


def _softplus(x):
    return jnp.maximum(x, 0.0) + jnp.log1p(jnp.exp(-jnp.abs(x)))


_ROW_BA, _ROW_BX, _ROW_LAM = 0, 2, 4


def _gate_math(xb, pre, vec_ref, d, sl):
    pa = pre[:, (2 * d) * LRU_BW:(2 * d + 1) * LRU_BW] + vec_ref[_ROW_BA + d:_ROW_BA + d + 1, sl]
    px = pre[:, (2 * d + 1) * LRU_BW:(2 * d + 2) * LRU_BW] + vec_ref[_ROW_BX + d:_ROW_BX + d + 1, sl]
    r = 0.5 * jnp.tanh(0.5 * pa) + 0.5
    i = 0.5 * jnp.tanh(0.5 * px) + 0.5
    sp = _softplus(-vec_ref[_ROW_LAM + d:_ROW_LAM + d + 1, sl])
    log_a = (-RG_C) * r * sp
    a = jnp.exp(log_a)
    th = jnp.tanh(log_a)
    om = -2.0 * th / (1.0 - th)
    mult = jnp.sqrt(om)
    return a, mult * (i * xb), (r, i, sp, om, mult)


def _gate_fwd(rec, wcat, gvec):
    def fn(ins, bs, outs, accs):
        for blk in range(LRU_BLOCKS):
            sl = slice(blk * LRU_BW, (blk + 1) * LRU_BW)
            xb = ins[0][:, sl]
            pre = jnp.dot(xb.astype(BF16), bs[0][sl, :], preferred_element_type=F32)
            for d in range(2):
                a, u, _ = _gate_math(xb, pre, bs[1], d, sl)
                outs[2 * d][:, sl] = a
                outs[2 * d + 1][:, sl] = u

    return _rowwise(fn, [rec], [wcat, gvec], [(D_MODEL, F32)] * 4, tm=256, name="rg_gate")


def _gate_bwd(rec, du_f, da_f, du_b, da_b, wcat, gvec):
    def fn(ins, bs, outs, accs):
        for blk in range(LRU_BLOCKS):
            sl = slice(blk * LRU_BW, (blk + 1) * LRU_BW)
            xb = ins[0][:, sl]
            xb16 = xb.astype(BF16)
            w = bs[0][sl, :]
            pre = jnp.dot(xb16, w, preferred_element_type=F32)
            dx = jnp.zeros_like(xb)
            dpre = []
            for d in range(2):
                a, _, (r, i, sp, om, mult) = _gate_math(xb, pre, bs[1], d, sl)
                du, da = ins[1 + 2 * d][:, sl], ins[2 + 2 * d][:, sl]
                d_i = du * mult * xb
                d_mult = du * i * xb
                dx = dx + du * mult * i
                dlog = da * a - d_mult * (1.0 - om) / mult
                d_r = dlog * ((-RG_C) * sp)
                d_sp = _rsum(dlog * ((-RG_C) * r))
                lam = bs[1][_ROW_LAM + d:_ROW_LAM + d + 1, sl]
                accs[2][_ROW_LAM + d:_ROW_LAM + d + 1, sl] += d_sp * (-jax.nn.sigmoid(-lam))
                dpa = d_r * r * (1.0 - r)
                dpx = d_i * i * (1.0 - i)
                accs[2][_ROW_BA + d:_ROW_BA + d + 1, sl] += _rsum(dpa)
                accs[2][_ROW_BX + d:_ROW_BX + d + 1, sl] += _rsum(dpx)
                dpre += [dpa, dpx]
            dpre = jnp.concatenate(dpre, axis=1).astype(BF16)
            dw = lax.dot_general(xb16, dpre, _TN, preferred_element_type=F32)
            for d in range(2):
                rows = slice(d * D_MODEL + blk * LRU_BW, d * D_MODEL + (blk + 1) * LRU_BW)
                accs[0][rows, :] += dw[:, (2 * d) * LRU_BW:(2 * d + 1) * LRU_BW]
                accs[1][rows, :] += dw[:, (2 * d + 1) * LRU_BW:(2 * d + 2) * LRU_BW]
            outs[0][:, sl] = dx + lax.dot_general(dpre, w, _NT, preferred_element_type=F32)

    gate_shape = (2 * D_MODEL, LRU_BW)
    return _rowwise(fn, [rec, du_f, da_f, du_b, da_b], [wcat, gvec], [(D_MODEL, F32)],
                    [gate_shape, gate_shape, (SUBLANES, D_MODEL)], tm=256, name="rg_gate_bwd")


def _as_time_blocks(x):
    return x.reshape(x.shape[0] // SUBLANES, SUBLANES, x.shape[1])


def _scan_call(body, ins, n_out, B, L, tc, name):
    nb = L // SUBLANES
    spec = pl.BlockSpec((nb, SUBLANES, tc), lambda b, j: (b, 0, j))
    T = ins[0].shape[0]
    outs = pl.pallas_call(
        functools.partial(body, nb), name=name, grid=(B, D_MODEL // tc),
        in_specs=[spec] * len(ins), out_specs=[spec] * n_out,
        out_shape=[jax.ShapeDtypeStruct((T // SUBLANES, SUBLANES, D_MODEL), F32)] * n_out,
        compiler_params=_params(("parallel", "parallel")),
    )(*[_as_time_blocks(x) for x in ins])
    return [o.reshape(T, D_MODEL) for o in outs]


def _block_scan(A, U, reverse):
    row = lax.broadcasted_iota(jnp.int32, A.shape, 0)
    for s in (1, 2, 4):
        shift = SUBLANES - s if reverse else s
        valid = (row < SUBLANES - s) if reverse else (row >= s)
        a_sh = jnp.where(valid, pltpu.roll(A, shift, 0), 1.0)
        u_sh = jnp.where(valid, pltpu.roll(U, shift, 0), 0.0)
        U = A * u_sh + U
        A = A * a_sh
    return A, U


_LAST = SUBLANES - 1
SCAN_UNROLL = 8


def _loop_blocks(nb, step, init):
    def group(g, carry):
        for k in range(SCAN_UNROLL):
            carry = step(g * SCAN_UNROLL + k, carry)
        return carry

    return lax.fori_loop(0, nb // SCAN_UNROLL, group, init)


def _scan_fwd(a_f, u_f, a_b, u_b, B, L, tc=256):
    def body(nb, af, uf, ab, ub, hf, hb):
        def step(i, carry):
            c1, c2 = carry
            ib = nb - 1 - i
            p, h = _block_scan(af[i], uf[i], False)
            h = h + p * c1
            hf[i] = h
            p2, h2 = _block_scan(ab[ib], ub[ib], True)
            h2 = h2 + p2 * c2
            hb[ib] = h2
            return h[_LAST:, :], h2[:1, :]

        zero = jnp.zeros((1, tc), F32)
        _loop_blocks(nb, step, (zero, zero))

    return _scan_call(body, [a_f, u_f, a_b, u_b], 2, B, L, tc, "rg_scan")


def _scan_bwd(dy, a_f, h_f, a_b, h_b, B, L, tc=256):
    def body(nb, dy_r, af, hf, ab, hb, duf, daf, dub, dab):
        def step(i, carry):
            c1, c2 = carry
            ir = nb - 1 - i
            row = lax.broadcasted_iota(jnp.int32, (SUBLANES, tc), 0)
            a_up = jnp.where(row == _LAST, af[jnp.minimum(ir + 1, nb - 1), :1, :], pltpu.roll(af[ir], _LAST, 0))
            p, lam = _block_scan(a_up, dy_r[ir], True)
            lam = lam + p * c1
            before = hf[jnp.maximum(ir - 1, 0), _LAST:, :] * (ir > 0).astype(F32)
            duf[ir] = lam
            daf[ir] = lam * jnp.where(row == 0, before, pltpu.roll(hf[ir], 1, 0))
            a_dn = jnp.where(row == 0, ab[jnp.maximum(i - 1, 0), _LAST:, :], pltpu.roll(ab[i], 1, 0))
            p2, lam2 = _block_scan(a_dn, dy_r[i], False)
            lam2 = lam2 + p2 * c2
            after = hb[jnp.minimum(i + 1, nb - 1), :1, :] * (i < nb - 1).astype(F32)
            dub[i] = lam2
            dab[i] = lam2 * jnp.where(row == _LAST, after, pltpu.roll(hb[i], _LAST, 0))
            return lam[:1, :], lam2[_LAST:, :]

        zero = jnp.zeros((1, tc), F32)
        _loop_blocks(nb, step, (zero, zero))

    return _scan_call(body, [dy, a_f, h_f, a_b, h_b], 4, B, L, tc, "rg_scan_bwd")


_GELU_C = math.sqrt(2.0 / math.pi)


def _gelu_parts(x):
    th = jnp.tanh(_GELU_C * (x + 0.044715 * x * x * x))
    return 0.5 * x * (1.0 + th), th


def _gated_out(h_f, h_b, z):
    def fn(ins, bs, outs, accs):
        gl, _ = _gelu_parts(ins[2][...])
        outs[0][...] = ((ins[0][...] + ins[1][...]) * gl).astype(BF16)

    return _rowwise(fn, [h_f, h_b, (z, D_MODEL, 0)], [], [(D_MODEL, BF16)], tm=512, name="rg_gated_out")[0]


def _gated_out_bwd(dyg, h_f, h_b, z):
    def fn(ins, bs, outs, accs):
        x = ins[3][...]
        gl, th = _gelu_parts(x)
        dgl = 0.5 * (1.0 + th) + 0.5 * x * (1.0 - th * th) * (_GELU_C * (1.0 + 3.0 * 0.044715 * x * x))
        g = ins[0][...]
        outs[0][...] = g * gl
        outs[1][...] = (g * (ins[1][...] + ins[2][...]) * dgl).astype(BF16)

    return _rowwise(fn, [dyg, h_f, h_b, (z, D_MODEL, 0)], [], [(D_MODEL, F32), (D_MODEL, BF16, 2 * D_MODEL)], tm=512,
                    name="rg_gated_out_bwd")


def _row_block(i):
    return pl.ds(pl.multiple_of(i * SUBLANES, SUBLANES), SUBLANES)


def _rg_mix_fwd(z, conv_wb, wcat, gvec, B, L):
    nb = L // SUBLANES
    n_g = D_MODEL // LRU_BW

    def body(zg_ref, zr_ref, cw_ref, w_ref, gv_ref, rec_ref, af_s, ab_s, hf_ref, hb_ref, yg_ref, uf_s, ub_s):
        rec = _conv_apply(zr_ref[...], cw_ref, L)
        rec_ref[...] = rec
        pre = jnp.dot(rec.astype(BF16), w_ref[...], preferred_element_type=F32)
        for d, (a_s, u_s) in enumerate(((af_s, uf_s), (ab_s, ub_s))):
            a, u, _ = _gate_math(rec, pre, gv_ref, d, slice(None))
            a_s[...] = a
            u_s[...] = u

        def step(i, carry):
            c1, c2 = carry
            rows, rows_b = _row_block(i), _row_block(nb - 1 - i)
            p, h = _block_scan(af_s[rows, :], uf_s[rows, :], False)
            h = h + p * c1
            hf_ref[rows, :] = h
            p2, h2 = _block_scan(ab_s[rows_b, :], ub_s[rows_b, :], True)
            h2 = h2 + p2 * c2
            hb_ref[rows_b, :] = h2
            return h[_LAST:, :], h2[:1, :]

        zero = jnp.zeros((1, LRU_BW), F32)
        _loop_blocks(nb, step, (zero, zero))
        gl, _ = _gelu_parts(zg_ref[...])
        yg_ref[...] = ((hf_ref[...] + hb_ref[...]) * gl).astype(BF16)

    seq = lambda off: pl.BlockSpec((L, LRU_BW), lambda b, g: (b, off + g))
    vec = pl.BlockSpec((SUBLANES, LRU_BW), lambda b, g: (0, g))
    T = B * L
    return pl.pallas_call(
        body, name="rg_mix", grid=(B, n_g),
        in_specs=[seq(0), seq(n_g), vec, pl.BlockSpec((LRU_BW, 4 * LRU_BW), lambda b, g: (g, 0)), vec],
        out_specs=[seq(0)] * 6,
        out_shape=[jax.ShapeDtypeStruct((T, D_MODEL), F32)] * 5 + [jax.ShapeDtypeStruct((T, D_MODEL), BF16)],
        scratch_shapes=[pltpu.VMEM((L, LRU_BW), F32)] * 2,
        compiler_params=_params(("parallel", "parallel")),
    )(z, z, conv_wb, wcat, gvec)


def _make_wcat(w_a, w_x):
    g = jnp.stack([w_a[0, 0], w_x[0, 0], w_a[0, 1], w_x[0, 1]])
    return jnp.transpose(g, (1, 2, 0, 3)).reshape(D_MODEL, 4 * LRU_BW)


def _rows_at(part, first):
    return jnp.pad(part, ((first, SUBLANES - first - part.shape[0]), (0, 0)))


def _qk_slot(q_g, k_g):
    wide = lambda v, at: jnp.pad(v, ((0, SUBLANES - 1), (at, D_MODEL - at - HEAD_DIM)))
    return wide(q_g, 0) + wide(k_g, HEAD_DIM)


def _local_step(x, target, P, fetch, emit, B, L, after=None):
    g_mix, g_mlp = P["norm_mix_g"], P["norm_mlp_g"]
    h0 = _rms_fwd(x, g_mix[0:1], "rg_norm", after=after)
    w_in, w_out, conv_wb, wcat, gvec = fetch("rg", h0)
    z = _mm(h0, w_in, mode="nn", b_shard=True, name="rg_in")
    rec, a_f, a_b, h_f, h_b, yg = _rg_mix_fwd(z, conv_wb, wcat, gvec, B, L)
    x1, h1 = _mm_res_norm(yg, w_out, x, g_mlp[0:1], "rg_out")
    (x2, h3), mlp0 = _mlp_fwd(x1, h1, fetch, 0, lambda a, w, res, name: _mm_res_norm(a, w, res, g_mix[1:2], name))
    w_qkv, w_o = fetch("att", h3)
    qkv = _mm(h3, w_qkv, mode="nn", b_shard=True, name="attn_qkv")
    cos, sin = _rope_tables(L, B)
    qh, kh, vh = _qk_prep(qkv, cos, sin, P["q_g"], P["k_g"])
    o = _attn_fwd(qh, kh, vh, B, L)
    x3, h4 = _mm_res_norm(o, w_o, x2, g_mlp[1:2], "attn_out")
    (dx4, dx4_bf, loss_acc, d_final_g), mlp1 = _mlp_fwd(
        x3, h4, fetch, 1, lambda a, w, res, name: _mm_final_loss(a, w, res, target, P["final_g"], name))

    dx3, dx3_bf, dg_mlp1, d_up1, d_down1 = _mlp_bwd(x3, g_mlp[1:2], mlp1, dx4, dx4_bf, 1, None)
    tok = emit("mlp1", [d_up1, d_down1])
    d_wo = _mm(o, dx3_bf, mode="tn", out_dtypes=(BF16,), name="attn_dwo", after=tok)
    do = _mm(dx3_bf, w_o, mode="nt", out_dtypes=(BF16,), name="attn_do")
    dq, dk, dv = _attn_bwd(qh, kh, vh, do, B, L)
    dqkv, dq_g, dk_g = _qk_prep_bwd(qkv, dq, dk, dv, cos, sin, P["q_g"], P["k_g"])
    d_wqkv = _mm(h3, dqkv, mode="tn", o_shard=True, out_dtypes=(BF16,), name="attn_dwqkv")
    tok = emit("att", [d_wqkv, d_wo])
    dx2, dx2_bf, dg_mix1 = _mm_norm_bwd(dqkv, w_qkv, x2, dx3, g_mix[1:2], "attn_dh", after=tok)
    tok = emit("point_attn_done", [dx2_bf])
    dx1, dx1_bf, dg_mlp0, d_up0, d_down0 = _mlp_bwd(x1, g_mlp[0:1], mlp0, dx2, dx2_bf, 0, tok)
    tok = emit("mlp0", [d_up0, d_down0])
    d_wout = _mm(yg, dx1_bf, mode="tn", out_dtypes=(BF16,), name="rg_dwout", after=tok)
    tok = emit("rg_out", [d_wout])
    dyg = _mm(dx1_bf, w_out, mode="nt", name="rg_dyg", after=tok)
    dy, dgate = _gated_out_bwd(dyg, h_f, h_b, z)
    du_f, da_f, du_b, da_b = _scan_bwd(dy, a_f, h_f, a_b, h_b, B, L)
    drec_c, d_wa, d_wx, d_gvec = _gate_bwd(rec, du_f, da_f, du_b, da_b, wcat, gvec)
    tok = emit("gates", [d_wa, d_wx])
    dz, d_convwb = _conv_bwd(z, drec_c, conv_wb, dgate, B, L, after=tok)
    tok = emit("point_mix_done", [dz])
    d_win = _mm(h0, dz, mode="tn", o_shard=True, out_dtypes=(BF16,), name="rg_dwin", after=tok)
    tok = emit("rg_in", [d_win])
    grad_x, _, dg_mix0 = _mm_norm_bwd(dz, w_in, x, dx1, g_mix[0:1], "rg_dh", after=tok)

    norms = (_rows_at(dg_mix0, 0) + _rows_at(dg_mix1, 1) + _rows_at(dg_mlp0, 2) + _rows_at(dg_mlp1, 3)
             + _rows_at(d_final_g, 4)
             + jnp.pad(loss_acc, ((LOSS_ROW, SUBLANES - 1 - LOSS_ROW), (0, D_MODEL - LANES))))
    vec = jnp.concatenate([norms, d_convwb, d_gvec, _qk_slot(dq_g, dk_g)], axis=0)
    return grad_x, vec


_MESH = pl.DeviceIdType.MESH


def _place():
    x, y, c = lax.axis_index("x"), lax.axis_index("y"), lax.axis_index("c")
    peers = [((1 - x) if j & 2 else x, (1 - y) if j & 1 else y) for j in (1, 2, 3)]
    return x, y, c, peers


def _comm_call(body, ins, out_shapes, n_sem, name):
    return pl.pallas_call(
        body, name=name, in_specs=[_ANY] * len(ins), out_specs=[_ANY] * len(out_shapes), out_shape=out_shapes,
        scratch_shapes=[pltpu.SemaphoreType.DMA((n_sem,)), pltpu.SemaphoreType.DMA((n_sem,)),
                        pltpu.SemaphoreType.DMA((len(ins),))],
    )(*ins)


def _all_devices_slots(v, name):
    def body(v_ref, out_ref, send, recv, lsem):
        x, y, c = lax.axis_index("x"), lax.axis_index("y"), lax.axis_index("c")
        me = 4 * x + 2 * y + c

        def peer(j):
            return (1 - x) if j & 4 else x, (1 - y) if j & 2 else y, (1 - c) if j & 1 else c

        def copy(j, slot):
            return pltpu.make_async_remote_copy(
                src_ref=v_ref, dst_ref=out_ref.at[slot], send_sem=send.at[j - 1], recv_sem=recv.at[j - 1],
                device_id=peer(j), device_id_type=_MESH)

        local = pltpu.make_async_copy(v_ref, out_ref.at[me], lsem.at[0])
        sends = [copy(j, me) for j in range(1, N_DEVICES)]
        for cp in [local] + sends:
            cp.start()
        for j in range(1, N_DEVICES):
            px, py, pc = peer(j)
            copy(j, 4 * px + 2 * py + pc).wait_recv()
        for cp in sends:
            cp.wait_send()
        local.wait()

    shape = jax.ShapeDtypeStruct((N_DEVICES,) + v.shape, v.dtype)
    return _comm_call(body, [v], [shape], N_DEVICES - 1, name)[0]


def _sum_leading(slots, name):
    def body(s_ref, o_ref):
        acc = s_ref[0]
        for d in range(1, slots.shape[0]):
            acc = acc + s_ref[d]
        o_ref[...] = acc

    return pl.pallas_call(body, name=name, out_shape=jax.ShapeDtypeStruct(slots.shape[1:], slots.dtype))(slots)


_HBM = pl.BlockSpec(memory_space=pltpu.HBM)
_SEM = pl.BlockSpec(memory_space=pltpu.SEMAPHORE)
_EFFECT = pltpu.SideEffectType.DATAFLOW_SIDE_EFFECTING


_COPIES = dict(gather=N_CHIPS - 1, scatter=N_CHIPS - 1, swap=1)


def _split_copies(kind, srcs, lands, send, recv):
    x, y, c, peers = _place()
    me = 2 * x + y
    per = _COPIES[kind]
    out = []
    for a in range(len(lands)):
        for j in range(per):
            if kind == "swap":
                src, there, here, dev = srcs[a], lands[a], lands[a], (x, y, 1 - c)
            else:
                px, py = peers[j]
                dev = (px, py, c)
                if kind == "gather":
                    src, there, here = lands[a].at[me], lands[a].at[me], lands[a].at[2 * px + py]
                else:
                    src, there, here = srcs[a].at[2 * px + py], lands[a].at[j], lands[a].at[j]
            mk = functools.partial(
                pltpu.make_async_remote_copy, src_ref=src, send_sem=send.at[per * a + j],
                recv_sem=recv.at[per * a + j], device_id=dev, device_id_type=_MESH)
            out.append((functools.partial(mk, dst_ref=there), functools.partial(mk, dst_ref=here)))
    return out


def _exchange_start(kind, srcs, lands, name, after=None):
    arrays = list(srcs) + list(lands)
    n_s, n, n_all = len(srcs), len(lands), len(srcs) + len(lands)
    n_sem = _COPIES[kind] * n
    order = _after_operand(after)
    n_x = len(order)

    def body(*refs):
        send, recv = refs[n_all + n_x], refs[n_all + n_x + 1]
        token = refs[-1]
        for started, _ in _split_copies(kind, refs[:n_s], refs[n_s:n_all], send, recv):
            started().start()
        token[...] = jnp.zeros(token.shape, F32)

    res = pl.pallas_call(
        body, name=name,
        out_shape=(pltpu.SemaphoreType.DMA((n_sem,)), pltpu.SemaphoreType.DMA((n_sem,)),
                   *[pltpu.HBM(a.shape, a.dtype) for a in arrays], jax.ShapeDtypeStruct((SUBLANES, LANES), F32)),
        in_specs=[_HBM] * n_all + [_ANY] * n_x,
        out_specs=(_SEM, _SEM, *[_HBM] * n_all, pl.BlockSpec(memory_space=pltpu.VMEM)),
        input_output_aliases={i: 2 + i for i in range(n_all)},
        compiler_params=pltpu.CompilerParams(has_side_effects=_EFFECT),
    )(*[pltpu.with_memory_space_constraint(a, pltpu.HBM) for a in arrays], *order)
    return (res[0], res[1], res[2:2 + n_s], res[2 + n_s:2 + n_all]), res[-1]


def _exchange_wait(kind, handle, after, name):
    send, recv, srcs, lands = handle
    arrays = list(srcs) + list(lands)
    n_s, n_all = len(srcs), len(arrays)
    order = list(after) if isinstance(after, (list, tuple)) else [after]

    def body(*refs):
        for started, landing in _split_copies(kind, refs[:n_s], refs[n_s:n_all], refs[n_all], refs[n_all + 1]):
            started().wait_send()
            landing().wait_recv()

    res = pl.pallas_call(
        body, name=name, out_shape=[pltpu.HBM(a.shape, a.dtype) for a in arrays],
        in_specs=[_HBM] * n_all + [_SEM, _SEM] + [_ANY] * len(order), out_specs=[_HBM] * n_all,
        input_output_aliases={i: i for i in range(n_all)},
        compiler_params=pltpu.CompilerParams(has_side_effects=_EFFECT),
    )(*arrays, send, recv, *order)
    return res[:n_s], res[n_s:]


def _index_operand(i):
    return jnp.reshape(i, (1,)).astype(jnp.int32)


def _cast_into_slot(src, row0, rows, me, dtype, name, after=None, add=None):
    cols = src.shape[1]
    tm = min(512, rows)
    order = _after_operand(after)
    terms = [src] + ([] if add is None else [add])

    def body(me_ref, *rest):
        val = rest[0][...]
        if add is not None:
            val = val + rest[1][...]
        rest[-1][...] = val.astype(dtype)

    return pl.pallas_call(
        body, name=name,
        grid_spec=pltpu.PrefetchScalarGridSpec(
            num_scalar_prefetch=1, grid=(rows // tm,),
            in_specs=[pl.BlockSpec((tm, cols), lambda i, me_ref: (i + row0 // tm, 0))] * len(terms)
            + [_ANY] * len(order),
            out_specs=pl.BlockSpec((None, tm, cols), lambda i, me_ref: (me_ref[0], i, 0))),
        out_shape=jax.ShapeDtypeStruct((N_CHIPS, rows, cols), dtype), compiler_params=_params(("parallel",)),
    )(_index_operand(me), *terms, *order)


def _sum_slots(mine, r, me, name):
    _, rows, cols = r.shape
    tm = min(512, rows)

    def body(me_ref, own_ref, r_ref, o_ref):
        o_ref[...] = ((own_ref[...].astype(F32) + r_ref[0].astype(F32)) + r_ref[1].astype(F32)) + r_ref[2].astype(F32)

    return pl.pallas_call(
        body, name=name,
        grid_spec=pltpu.PrefetchScalarGridSpec(
            num_scalar_prefetch=1, grid=(rows // tm,),
            in_specs=[pl.BlockSpec((None, tm, cols), lambda i, me_ref: (me_ref[0], i, 0)),
                      pl.BlockSpec((N_CHIPS - 1, tm, cols), lambda i, me_ref: (0, i, 0))],
            out_specs=pl.BlockSpec((tm, cols), lambda i, me_ref: (i, 0))),
        out_shape=jax.ShapeDtypeStruct((rows, cols), F32), compiler_params=_params(("parallel",)),
    )(_index_operand(me), mine, r)


def _adamw(w, m, v, ps, qs, name):
    rows, cols = w.shape
    seg_rows = ps[0].shape[0]
    tm = min(256, seg_rows)
    while seg_rows % tm:
        tm -= SUBLANES
    per, n_seg = seg_rows // tm, len(ps)
    parts = list(ps) + ([] if qs is None else list(qs))

    def body(w_ref, m_ref, v_ref, *rest):
        g_refs, outs = rest[:len(parts)], rest[len(parts):]
        grad = lambda s: g_refs[s][...] if qs is None else g_refs[s][...] + g_refs[n_seg + s][...]
        g = grad(0)
        for s in range(1, n_seg):
            g = jnp.where(pl.program_id(0) >= s * per, grad(s), g)
        m1 = ADAM_B1 * m_ref[...] + (1.0 - ADAM_B1) * g
        v1 = ADAM_B2 * v_ref[...] + (1.0 - ADAM_B2) * (g * g)
        m_hat = m1 / (1.0 - ADAM_B1 ** ADAM_STEP)
        v_hat = v1 / (1.0 - ADAM_B2 ** ADAM_STEP)
        outs[0][...] = g
        outs[1][...] = (-ADAM_LR) * (m_hat / (jnp.sqrt(v_hat) + ADAM_EPS) + ADAM_WD * w_ref[...])
        outs[2][...] = m1
        outs[3][...] = v1

    row_spec = pl.BlockSpec((tm, cols), lambda i: (i, 0))
    seg_spec = lambda s: pl.BlockSpec((tm, cols), lambda i: (jnp.clip(i - s * per, 0, per - 1), 0))
    return pl.pallas_call(
        body, name=name, grid=(rows // tm,),
        in_specs=[row_spec] * 3 + [seg_spec(s) for s in range(n_seg)] * (1 if qs is None else 2),
        out_specs=[row_spec] * 4, out_shape=[jax.ShapeDtypeStruct((rows, cols), F32)] * 4,
        compiler_params=_params(("arbitrary",)),
    )(w, m, v, *parts)


def _put_cols(shard, me):
    full = jnp.zeros((shard.shape[0], D_MODEL), F32)
    return lax.dynamic_update_slice(full, shard, (0, me * (D_MODEL // N_CHIPS)))


def _gate_vec_slot(b_a, b_x, lam):
    return _rows_at(b_a, _ROW_BA) + _rows_at(b_x, _ROW_BX) + _rows_at(lam, _ROW_LAM)


def _pack_vec(p, me):
    return jnp.concatenate([
        _rows_at(p["norm_mix_g"], 0) + _rows_at(p["norm_mlp_g"], 2) + _rows_at(p["final_g"][None], 4),
        _rows_at(_put_cols(p["rg_conv_w"][0, :, 0, :], me), 0) + _rows_at(p["rg_conv_b"], 4),
        _gate_vec_slot(_put_cols(p["rg_b_a"][0], me), _put_cols(p["rg_b_x"][0], me), _put_cols(p["rg_lam"][0], me)),
        _qk_slot(p["at_q_g"], p["at_k_g"]),
    ], axis=0)


def _unpack_vec(r, me):
    def cols(rows):
        return lax.dynamic_slice(rows, (0, me * (D_MODEL // N_CHIPS)), (rows.shape[0], D_MODEL // N_CHIPS))

    gate = r[16:24]
    return dict(
        norm_mix_g=r[0:2], norm_mlp_g=r[2:4], final_g=r[4], rg_conv_w=cols(r[8:12])[None, :, None, :],
        rg_conv_b=r[12:13], rg_b_a=cols(gate[_ROW_BA:_ROW_BA + 2])[None], rg_b_x=cols(gate[_ROW_BX:_ROW_BX + 2])[None],
        rg_lam=cols(gate[_ROW_LAM:_ROW_LAM + 2])[None], at_q_g=r[24:25, 0:HEAD_DIM],
        at_k_g=r[24:25, HEAD_DIM:2 * HEAD_DIM])


_WEIGHTS = ['norm_mix_g', 'norm_mlp_g', 'rg_w_in', 'rg_conv_w', 'rg_conv_b', 'rg_w_a', 'rg_b_a', 'rg_w_x', 'rg_b_x',
            'rg_lam', 'rg_w_out', 'at_w_qkv', 'at_q_g', 'at_k_g', 'at_w_o', 'mlp_w_up', 'mlp_w_down', 'final_g']
_BIG = dict(rg_w_in=["rg_w_in"], rg_w_out=["rg_w_out"], at_w_qkv=["at_w_qkv"], at_w_o=["at_w_o"],
            mlp_w_up=["up0", "up1"], mlp_w_down=["down0", "down1"])


def kernel(x, *args):
    n_w = len(_WEIGHTS)
    w = dict(zip(_WEIGHTS, args[:n_w]))
    target = args[n_w]
    m = dict(zip(_WEIGHTS, args[n_w + 1:2 * n_w + 1]))
    v = dict(zip(_WEIGHTS, args[2 * n_w + 1:3 * n_w + 1]))
    B, L, _ = x.shape
    T = B * L
    me = 2 * lax.axis_index("x") + lax.axis_index("y")

    vec = jnp.concatenate([_gate_vec_slot(w["rg_b_a"][0], w["rg_b_x"][0], w["rg_lam"][0]),
                           _rows_at(w["rg_conv_w"][0, :, 0, :], 0)], axis=0)
    flat = lambda a: a.reshape(-1, a.shape[-1])
    rows_of = lambda k: w[k].shape[-2]
    groups = [("rg", [("rg_w_in", 0, BF16), ("rg_w_out", 0, BF16), (vec, 0, F32)]),
              ("mlp0", [("mlp_w_up", 0, BF16), ("mlp_w_down", 0, BF16)]),
              ("att", [("at_w_qkv", 0, BF16), ("at_w_o", 0, BF16)]),
              ("mlp1", [("mlp_w_up", 1, BF16), ("mlp_w_down", 1, BF16)])]
    gathers, tok = {}, None
    for group, members in groups:
        lands = []
        for n, (k, layer, dtype) in enumerate(members):
            src, rows = (flat(w[k]), rows_of(k)) if isinstance(k, str) else (k, k.shape[0])
            lands.append(_cast_into_slot(src, layer * rows, rows, me, dtype, f"place_{group}{n}", after=tok))
        gathers[group], tok = _exchange_start("gather", [], lands, f"gather_{group}_start", after=tok)
    wcat = _make_wcat(w["rg_w_a"], w["rg_w_x"]).astype(BF16)

    packs = [_pack_vec(p, me) for p in (w, m, v)]

    def fetch(group, after):
        order = [after, wcat] + packs if group == "rg" else after
        _, full = _exchange_wait("gather", gathers[group], order, f"gather_{group}_wait")
        if group == "rg":
            vec_full = jnp.transpose(full[2], (1, 0, 2)).reshape(2 * SUBLANES, D_MODEL)
            conv_wb = vec_full[SUBLANES:] + _rows_at(w["rg_conv_b"], 4)
            return full[0], full[1].reshape(D_MODEL, D_MODEL), conv_wb, wcat, vec_full[:SUBLANES]
        if group == "att":
            return full[0], full[1].reshape(D_MODEL, D_MODEL)
        return full[0], full[1].reshape(4 * D_MODEL, D_MODEL)

    names = dict(mlp1=["up1", "down1"], att=["at_w_qkv", "at_w_o"], mlp0=["up0", "down0"], rg_out=["rg_w_out"],
                 rg_in=["rg_w_in"], gates=["rg_w_a", "rg_w_x"])
    scatters, swaps, P, Q, res = {}, [], {}, {}, {}

    def start_scatter(group, grads):
        srcs = [g.reshape(N_CHIPS, -1, g.shape[-1]) for g in grads]
        lands = [lax.empty((N_CHIPS - 1,) + s.shape[1:], s.dtype) for s in srcs]
        scatters[group], token = _exchange_start("scatter", srcs, lands, f"scatter_{group}_start")
        return token

    def settle(groups, after):
        keys, parts = [], []
        for group in groups:
            srcs, lands = _exchange_wait("scatter", scatters[group], after, f"scatter_{group}_wait")
            for k, s, r in zip(names[group], srcs, lands):
                keys.append(k)
                parts.append(_sum_slots(s, r, me, f"sum_{k}"))
        handle, token = _exchange_start("swap", parts, [lax.empty(p.shape, F32) for p in parts],
                                        f"swap_{groups[0]}_start")
        swaps.append((keys, handle, f"swap_{groups[0]}_wait"))
        return token

    def finish(after):
        for keys, handle, name in swaps:
            mine, theirs = _exchange_wait("swap", handle, after, name)
            P.update(zip(keys, mine))
            Q.update(zip(keys, theirs))
        swaps.clear()
        last = after
        for k, parts in _BIG.items():
            if k in res or any(p not in P for p in parts):
                continue
            shape = w[k].shape
            two_d = lambda a: a.reshape(-1, shape[-1])
            outs = _adamw(two_d(w[k]), two_d(m[k]), two_d(v[k]), [P[p] for p in parts], [Q[p] for p in parts],
                          f"adamw_{k}")
            res[k] = [o.reshape(shape) for o in outs]
            last = outs[0]
        if "rg_w_a" in P and "gates" not in gathers:
            lands = [_cast_into_slot(P[k], 0, P[k].shape[0], me, F32, f"place_{k}", after=last, add=Q[k])
                     for k in names["gates"]]
            gathers["gates"], last = _exchange_start("gather", [], lands, "gather_gates_start", after=last)
        return last

    def emit(event, arrays):
        if event == "point_attn_done":
            return settle(["mlp1"], arrays[0])
        if event == "point_mix_done":
            return settle(["att", "mlp0", "rg_out"], arrays[0])
        token = start_scatter(event, arrays)
        if event == "rg_in":
            return finish(settle(["gates"], token))
        return token

    P_vec = dict(norm_mix_g=w["norm_mix_g"], norm_mlp_g=w["norm_mlp_g"], final_g=w["final_g"][None],
                 q_g=w["at_q_g"], k_g=w["at_k_g"])
    grad_x, vec_part = _local_step(x.reshape(T, D_MODEL), target.reshape(T, D_MODEL), P_vec, fetch, emit, B, L,
                                   after=tok)

    finish(settle(["rg_in"], grad_x))
    _, gate_grads = _exchange_wait("gather", gathers["gates"], grad_x, "gather_gates_wait")
    for k, g in zip(names["gates"], gate_grads):
        two_d = lambda a: a.reshape(g.shape[0] * g.shape[1], g.shape[2])
        outs = _adamw(two_d(w[k]), two_d(m[k]), two_d(v[k]), [two_d(g)], None, f"adamw_{k}")
        res[k] = [o.reshape(w[k].shape) for o in outs]
    vec_grad = _sum_leading(_all_devices_slots(vec_part, "allreduce_vec"), "sum_vec")
    loss = vec_grad[LOSS_ROW, 0]
    outs = _adamw(*packs, [vec_grad], None, "adamw_vec")
    unpacked = [_unpack_vec(o, me) for o in outs]
    for k in _WEIGHTS:
        if k not in res:
            res[k] = [u[k] for u in unpacked]

    result = [loss, grad_x.reshape(B, L, D_MODEL)]
    for slot in range(4):
        result += [res[k][slot] for k in _WEIGHTS]
    return tuple(result)
```

```python
import functools
import math

import jax
import jax.numpy as jnp
import numpy as np
from jax import lax
from jax.experimental import pallas as pl
from jax.experimental.pallas import tpu as pltpu

F32 = jnp.float32
BF16 = jnp.bfloat16

D_MODEL = 1024
HEAD_DIM = 128
N_HEADS = 8
N_KV = 2
GROUP = N_HEADS // N_KV
LRU_BLOCKS = 8
LRU_BW = 128
GRID_W = 64
ROPE_THETA = 10000.0
EPS = 1e-6
RG_C = 8.0
SCALE = 1.0 / math.sqrt(HEAD_DIM)
N_CHIPS = 4

ADAM_LR = 0.001
ADAM_B1 = 0.9
ADAM_B2 = 0.999
ADAM_EPS = 1e-08
ADAM_WD = 0.01
ADAM_STEP = 10

V7X_VMEM_BYTES = 64 * 1024 * 1024
VMEM_LIMIT = V7X_VMEM_BYTES * 3 // 4
LANES = 128
SUBLANES = 8

N_DEVICES = 8
VEC_ROWS = 32
LOSS_ROW = 5


def _params(sem):
    return pltpu.CompilerParams(dimension_semantics=sem, vmem_limit_bytes=VMEM_LIMIT)


_ANY = pl.BlockSpec(memory_space=pl.ANY)
_NN = (((1,), (0,)), ((), ()))
_NT = (((1,), (1,)), ((), ()))
_TN = (((0,), (0,)), ((), ()))


def _after_operand(after):
    return [] if after is None else [after]


def _fit(t, n):
    if n <= t:
        return n
    c = (t // LANES) * LANES
    while n % c:
        c -= LANES
    return c


MM_VMEM_BUDGET = VMEM_LIMIT * 3 // 4
def _mm_tiles(M, K, ns, n_total, out_dtypes, extras, whole_rows):
    for tm in (2048, 1024, 512, 256, 128):
        for tn in ((ns,) if whole_rows else (1024, 512, 256)):
            tn = _fit(tn, ns)
            per_row = 2 * (2 * K) + 4 * tn + sum(2 * tn * jnp.dtype(d).itemsize for d in out_dtypes)
            per_row += sum(2 * tn * e.dtype.itemsize for e in extras)
            b_buffers = 1 if tn == n_total else 2
            if M % tm == 0 and b_buffers * (2 * K * tn) + tm * per_row <= MM_VMEM_BUDGET:
                return tm, tn
    raise ValueError(f"no tile fits VMEM for M={M} K={K} N={ns}")


def _mm(a, b, *, mode, name, out_dtypes=(F32,), b_shard=False, o_shard=False, extras=(), epi=None, after=None,
        bcast=(), accs=(), ref_epi=None):
    if mode == "tn":
        K, M = a.shape
        N = b.shape[1]
    else:
        M, K = a.shape
        if mode == "nn":
            N = b.shape[0] * b.shape[2] if b_shard else b.shape[1]
        else:
            N = b.shape[1] if b_shard else b.shape[0]
    ns = N
    if b_shard and mode == "nn":
        ns = b.shape[2]
    elif o_shard:
        ns = N // N_CHIPS
    tm, tn = _mm_tiles(M, K, ns, N, out_dtypes, extras, whole_rows=ref_epi is not None)
    if ref_epi is not None:
        tm = min(tm, 512)
    grid = (M // tm, N // tn)
    q = ns // tn
    once = dict(pipeline_mode=pl.Buffered(1)) if tn == N else {}

    if mode == "tn":
        a_spec = pl.BlockSpec((K, tm), lambda i, j: (0, i))
        b_spec = pl.BlockSpec((K, tn), lambda i, j: (0, j), **once)
        dims = _TN
    elif mode == "nn":
        a_spec = pl.BlockSpec((tm, K), lambda i, j: (i, 0))
        if b_shard:
            b_spec = pl.BlockSpec((None, K, tn), lambda i, j: (j // q, 0, j % q), **once)
        else:
            b_spec = pl.BlockSpec((K, tn), lambda i, j: (0, j), **once)
        dims = _NN
    else:
        a_spec = pl.BlockSpec((tm, K), lambda i, j: (i, 0))
        if b_shard:
            ks = b.shape[2]
            b_spec = pl.BlockSpec((N_CHIPS, tn, ks), lambda i, j: (0, j, 0), **once)
        else:
            b_spec = pl.BlockSpec((tn, K), lambda i, j: (j, 0), **once)
        dims = _NT

    if o_shard:
        o_specs = [pl.BlockSpec((None, tm, tn), lambda i, j: (j // q, i, j % q))]
        o_shapes = [jax.ShapeDtypeStruct((N_CHIPS, M, ns), out_dtypes[0])]
    else:
        o_specs = [pl.BlockSpec((tm, tn), lambda i, j: (i, j)) for _ in out_dtypes]
        o_shapes = [jax.ShapeDtypeStruct((M, N), dt) for dt in out_dtypes]
    e_specs = [pl.BlockSpec((tm, tn), lambda i, j: (i, j)) for _ in extras]
    e_specs += [pl.BlockSpec(v.shape, lambda i, j: (0, 0)) for v in bcast]
    o_specs += [pl.BlockSpec(s, lambda i, j: (0, 0)) for s in accs]
    o_shapes += [jax.ShapeDtypeStruct(s, F32) for s in accs]
    n_e, n_b, n_o, n_a = len(extras), len(bcast), len(out_dtypes), len(accs)
    order = _after_operand(after)
    n_x = len(order)
    if epi is None:
        epi = lambda acc: (acc,)

    def body(a_ref, b_ref, *rest):
        e_refs, b_refs = rest[:n_e], rest[n_e:n_e + n_b]
        o_refs = rest[n_e + n_b + n_x:n_e + n_b + n_x + n_o]
        a_refs = rest[n_e + n_b + n_x + n_o:]
        if n_a:
            @pl.when((pl.program_id(0) == 0) & (pl.program_id(1) == 0))
            def _():
                for r in a_refs:
                    r[...] = jnp.zeros(r.shape, F32)
        if mode == "nt" and b_shard:
            acc = None
            for s in range(N_CHIPS):
                part = lax.dot_general(a_ref[:, s * ks:(s + 1) * ks], b_ref[s], dims, preferred_element_type=F32)
                acc = part if acc is None else acc + part
        else:
            acc = lax.dot_general(a_ref[...], b_ref[...], dims, preferred_element_type=F32)
        if ref_epi is not None:
            ref_epi(acc, e_refs, b_refs, o_refs, a_refs)
            return
        outs = epi(acc, *[r[...] for r in e_refs])
        for r, o in zip(o_refs, outs):
            r[...] = o.astype(r.dtype)

    outs = pl.pallas_call(
        body, name=name, grid=grid, in_specs=[a_spec, b_spec] + e_specs + [_ANY] * n_x, out_specs=o_specs,
        out_shape=o_shapes, compiler_params=_params(("arbitrary", "arbitrary") if n_a else ("parallel", "parallel")),
    )(a, b, *extras, *bcast, *order)
    return outs[0] if n_o + n_a == 1 else outs


def _rowwise(fn, rows, bcast, outs, accs=(), *, tm, name, after=None):
    def norm(r):
        return r if isinstance(r, tuple) else (r, r.shape[1], 0)

    rows = [norm(r) for r in rows]
    T = rows[0][0].shape[0]
    tm = min(tm, T)
    while T % tm:
        tm -= SUBLANES
    n_r, n_b, n_o, n_a = len(rows), len(bcast), len(outs), len(accs)
    order = _after_operand(after)
    n_x = len(order)
    in_specs = [pl.BlockSpec((tm, c), functools.partial(lambda i, cb: (i, cb), cb=cb)) for _, c, cb in rows]
    in_specs += [pl.BlockSpec(b.shape, lambda i: (0, 0)) for b in bcast] + [_ANY] * n_x
    out_specs = [pl.BlockSpec((tm, o[0]), lambda i: (i, 0)) for o in outs]
    out_specs += [pl.BlockSpec(s, lambda i: (0, 0)) for s in accs]
    out_shape = [jax.ShapeDtypeStruct((T, o[2] if len(o) > 2 else o[0]), o[1]) for o in outs]
    out_shape += [jax.ShapeDtypeStruct(s, F32) for s in accs]

    def body(*refs):
        in_refs = refs[:n_r]
        b_refs = refs[n_r:n_r + n_b]
        o_refs = refs[n_r + n_b + n_x:n_r + n_b + n_x + n_o]
        a_refs = refs[n_r + n_b + n_x + n_o:]
        if n_a:
            @pl.when(pl.program_id(0) == 0)
            def _():
                for r in a_refs:
                    r[...] = jnp.zeros(r.shape, F32)
        fn(in_refs, b_refs, o_refs, a_refs)

    res = pl.pallas_call(
        body, name=name, grid=(T // tm,), in_specs=in_specs, out_specs=out_specs, out_shape=out_shape,
        compiler_params=_params(("arbitrary",) if n_a else ("parallel",)),
    )(*[r[0] for r in rows], *bcast, *order)
    return res


def _rsum(x):
    return jnp.sum(x, axis=0, keepdims=True)


def _rms_fwd(x, g, name, after=None):
    def fn(ins, bs, outs, accs):
        xv = ins[0][...]
        r = lax.rsqrt(jnp.mean(xv * xv, axis=-1, keepdims=True) + EPS)
        outs[0][...] = (xv * r * bs[0][...]).astype(BF16)

    return _rowwise(fn, [x], [g], [(D_MODEL, BF16)], tm=512, name=name, after=after)[0]


def _rms_bwd_math(xv, dh, g):
    r = lax.rsqrt(jnp.mean(xv * xv, axis=-1, keepdims=True) + EPS)
    hn = xv * r
    dgh = dh * g
    dx = r * (dgh - hn * jnp.mean(dgh * hn, axis=-1, keepdims=True))
    return dx, _rsum(dh * hn)


def _mm_norm_bwd(dy, w, x, dres, g, name, after=None):
    def epilogue(acc, e_refs, b_refs, o_refs, a_refs):
        dx, dg = _rms_bwd_math(e_refs[0][...], acc, b_refs[0][...])
        dx = dx + e_refs[1][...]
        o_refs[0][...] = dx
        o_refs[1][...] = dx.astype(BF16)
        a_refs[0][...] += dg

    return _mm(dy, w, mode="nt", b_shard=True, out_dtypes=(F32, BF16), extras=(x, dres), bcast=(g,),
               accs=((1, D_MODEL),), ref_epi=epilogue, name=name, after=after)


def _mm_res_norm(a, w, res, g, name):
    def epilogue(acc, e_refs, b_refs, o_refs, a_refs):
        xv = acc + e_refs[0][...]
        o_refs[0][...] = xv
        r = lax.rsqrt(jnp.mean(xv * xv, axis=-1, keepdims=True) + EPS)
        o_refs[1][...] = (xv * r * b_refs[0][...]).astype(BF16)

    return _mm(a, w, mode="nn", out_dtypes=(F32, BF16), extras=(res,), bcast=(g,), ref_epi=epilogue, name=name)


def _mm_final_loss(a, w, res, target, g, name):
    def epilogue(acc, e_refs, b_refs, o_refs, a_refs):
        xv = acc + e_refs[0][...]
        gv = b_refs[0][...]
        r = lax.rsqrt(jnp.mean(xv * xv, axis=-1, keepdims=True) + EPS)
        e = xv * r * gv - e_refs[1][...]
        tok = jnp.mean(e * e, axis=-1, keepdims=True)
        a_refs[0][...] += 0.5 * jnp.sum(tok, axis=0, keepdims=True) * jnp.ones((1, LANES), F32)
        dx, dg = _rms_bwd_math(xv, e * (1.0 / D_MODEL), gv)
        o_refs[0][...] = dx
        o_refs[1][...] = dx.astype(BF16)
        a_refs[1][...] += dg

    return _mm(a, w, mode="nn", out_dtypes=(F32, BF16), extras=(res, target), bcast=(g,),
               accs=((1, LANES), (1, D_MODEL)), ref_epi=epilogue, name=name)


def _relu2(acc):
    r = jnp.maximum(acc, 0.0)
    return r * r, r


def _mlp_fwd(x, h, fetch, tag, finish):
    w_up = fetch(f"mlp{tag}_up", h)
    a, r = _mm(h, w_up, mode="nn", b_shard=True, out_dtypes=(BF16, BF16), epi=_relu2, name=f"mlp{tag}_up")
    w_down = fetch(f"mlp{tag}_down", a)
    return finish(a, w_down, x, f"mlp{tag}_down"), (h, a, r, w_up, w_down)


def _mlp_bwd(x, g, saved, dx, dx_bf, tag, after):
    h, a, r, w_up, w_down = saved
    d_down = _mm(a, dx_bf, mode="tn", out_dtypes=(BF16,), name=f"mlp{tag}_dwdown", after=after)
    dup = _mm(dx_bf, w_down, mode="nt", extras=(r,), out_dtypes=(BF16,),
              epi=lambda acc, rv: (acc * (2.0 * rv.astype(F32)),), name=f"mlp{tag}_dup")
    d_up = _mm(h, dup, mode="tn", o_shard=True, out_dtypes=(BF16,), name=f"mlp{tag}_dwup")
    dx_new, dx_new_bf, dg = _mm_norm_bwd(dup, w_up, x, dx, g, f"mlp{tag}_dh")
    return dx_new, dx_new_bf, dg, d_up, d_down


def _rope_tables(L, B):
    rows = L // GRID_W
    row = np.repeat(np.arange(rows, dtype=np.float32), GRID_W)
    col = np.tile(np.arange(GRID_W, dtype=np.float32), rows)
    inv = (ROPE_THETA ** (-np.arange(HEAD_DIM // 4, dtype=np.float32) / (HEAD_DIM // 4))).astype(np.float32)
    ar, ac = row[:, None] * inv, col[:, None] * inv
    cos = np.concatenate([np.cos(ar), np.cos(ar), np.cos(ac), np.cos(ac)], axis=-1)
    sin = np.concatenate([-np.sin(ar), np.sin(ar), -np.sin(ac), np.sin(ac)], axis=-1)
    return jnp.asarray(np.tile(cos, (B, 1)), F32), jnp.asarray(np.tile(sin, (B, 1)), F32)


def _swap_halves(x):
    lane = lax.broadcasted_iota(jnp.int32, x.shape, 1)
    return jnp.where((lane % 64) < 32, pltpu.roll(x, HEAD_DIM - 32, 1), pltpu.roll(x, 32, 1))


def _qk_prep(qkv, cos, sin, q_g, k_g):
    def fn(ins, bs, outs, accs):
        c, s = ins[1][...], ins[2][...]
        for h in range(N_HEADS + N_KV):
            xv = ins[0][:, h * HEAD_DIM:(h + 1) * HEAD_DIM]
            g = bs[0][...] if h < N_HEADS else bs[1][...]
            r = lax.rsqrt(jnp.mean(xv * xv, axis=-1, keepdims=True) + EPS)
            z = xv * r * g
            y = (z * c + _swap_halves(z) * s).astype(BF16)
            if h < N_HEADS:
                outs[0][:, h * HEAD_DIM:(h + 1) * HEAD_DIM] = y
            else:
                outs[1][:, (h - N_HEADS) * HEAD_DIM:(h - N_HEADS + 1) * HEAD_DIM] = y
        outs[2][...] = ins[0][:, (N_HEADS + N_KV) * HEAD_DIM:].astype(BF16)

    kvw = N_KV * HEAD_DIM
    return _rowwise(fn, [qkv, cos, sin], [q_g, k_g], [(D_MODEL, BF16), (kvw, BF16), (kvw, BF16)], tm=512,
                    name="attn_qk_prep")


def _qk_prep_bwd(qkv, dq, dk, dv, cos, sin, q_g, k_g):
    def fn(ins, bs, outs, accs):
        c, s = ins[4][...], ins[5][...]
        for h in range(N_HEADS + N_KV):
            sl = slice(h * HEAD_DIM, (h + 1) * HEAD_DIM)
            xv = ins[0][:, sl]
            if h < N_HEADS:
                g, dy, acc = bs[0][...], ins[1][:, sl], accs[0]
            else:
                ks = slice((h - N_HEADS) * HEAD_DIM, (h - N_HEADS + 1) * HEAD_DIM)
                g, dy, acc = bs[1][...], ins[2][:, ks], accs[1]
            r = lax.rsqrt(jnp.mean(xv * xv, axis=-1, keepdims=True) + EPS)
            xn = xv * r
            dz = dy * c - _swap_halves(dy) * s
            acc[...] += _rsum(dz * xn)
            dxn = dz * g
            outs[0][:, sl] = (r * (dxn - xn * jnp.mean(dxn * xn, axis=-1, keepdims=True))).astype(BF16)
        outs[0][:, (N_HEADS + N_KV) * HEAD_DIM:] = ins[3][...].astype(BF16)

    return _rowwise(fn, [qkv, dq, dk, dv, cos, sin], [q_g, k_g], [(qkv.shape[1], BF16)],
                    [(1, HEAD_DIM), (1, HEAD_DIM)], tm=256, name="attn_qk_prep_bwd")


_EXP2_SCALE = SCALE * math.log2(math.e)


def _exp_rows(q, k):
    s = lax.dot_general(q, k, _NT, preferred_element_type=F32)
    p = jnp.exp2((s - jnp.max(s, axis=-1, keepdims=True)) * _EXP2_SCALE)
    return p, jnp.sum(p, axis=-1, keepdims=True)


def _attn_fwd(q, k, v, B, L, tq=2048, sub=256):
    tq = min(tq, L)
    sub = min(sub, tq)
    nq = L // tq

    def body(q_ref, k_ref, v_ref, o_ref):
        kv, vv = k_ref[...], v_ref[...]
        for c in range(tq // sub):
            rows = slice(c * sub, (c + 1) * sub)
            p, l = _exp_rows(q_ref[rows, :], kv)
            o = jnp.dot(p.astype(BF16), vv, preferred_element_type=F32)
            o_ref[rows, :] = (o * (1.0 / l)).astype(o_ref.dtype)

    return pl.pallas_call(
        body, name="attn_fwd", grid=(B, N_HEADS, nq),
        in_specs=[pl.BlockSpec((tq, HEAD_DIM), lambda b, h, i: (b * nq + i, h)),
                  pl.BlockSpec((L, HEAD_DIM), lambda b, h, i: (b, h // GROUP)),
                  pl.BlockSpec((L, HEAD_DIM), lambda b, h, i: (b, h // GROUP))],
        out_specs=pl.BlockSpec((tq, HEAD_DIM), lambda b, h, i: (b * nq + i, h)),
        out_shape=jax.ShapeDtypeStruct((B * L, D_MODEL), BF16),
        compiler_params=_params(("parallel", "parallel", "parallel")),
    )(q, k, v)


def _attn_bwd(q, k, v, do, B, L, tq=2048, sub=512):
    tq = min(tq, L)
    sub = min(sub, tq)
    nq = L // tq

    def body(q_ref, k_ref, v_ref, do_ref, dq_ref, dk_ref, dv_ref):
        @pl.when((pl.program_id(2) == 0) & (pl.program_id(3) == 0))
        def _():
            dk_ref[...] = jnp.zeros(dk_ref.shape, F32)
            dv_ref[...] = jnp.zeros(dv_ref.shape, F32)

        kv, vv = k_ref[...], v_ref[...]
        ps, es, dos, qs = [], [], [], []
        for c in range(tq // sub):
            rows = slice(c * sub, (c + 1) * sub)
            qc, doc = q_ref[rows, :], do_ref[rows, :]
            p, l = _exp_rows(qc, kv)
            inv = 1.0 / l
            dp = lax.dot_general(doc, vv, _NT, preferred_element_type=F32)
            delta = jnp.sum(p * dp, axis=-1, keepdims=True) * inv
            e = (p * (dp - delta)).astype(BF16)
            dq_ref[rows, :] = jnp.dot(e, kv, preferred_element_type=F32) * (inv * SCALE)
            ps.append(p.astype(BF16))
            es.append(e)
            dos.append((doc.astype(F32) * inv).astype(BF16))
            qs.append((qc.astype(F32) * (inv * SCALE)).astype(BF16))
        cat = lambda xs: xs[0] if len(xs) == 1 else jnp.concatenate(xs, axis=0)
        dv_ref[...] += lax.dot_general(cat(ps), cat(dos), _TN, preferred_element_type=F32)
        dk_ref[...] += lax.dot_general(cat(es), cat(qs), _TN, preferred_element_type=F32)

    qmap = lambda b, kh, g, i: (b * nq + i, kh * GROUP + g)
    kmap = lambda b, kh, g, i: (b, kh)
    kvw = N_KV * HEAD_DIM
    return pl.pallas_call(
        body, name="attn_bwd", grid=(B, N_KV, GROUP, nq),
        in_specs=[pl.BlockSpec((tq, HEAD_DIM), qmap), pl.BlockSpec((L, HEAD_DIM), kmap),
                  pl.BlockSpec((L, HEAD_DIM), kmap), pl.BlockSpec((tq, HEAD_DIM), qmap)],
        out_specs=[pl.BlockSpec((tq, HEAD_DIM), qmap), pl.BlockSpec((L, HEAD_DIM), kmap),
                   pl.BlockSpec((L, HEAD_DIM), kmap)],
        out_shape=[jax.ShapeDtypeStruct((B * L, D_MODEL), F32), jax.ShapeDtypeStruct((B * L, kvw), F32),
                   jax.ShapeDtypeStruct((B * L, kvw), F32)],
        compiler_params=_params(("parallel", "parallel", "arbitrary", "arbitrary")),
    )(q, k, v, do)


def _conv_shift(x, t, L, k):
    if k == 2:
        return x
    if k < 2:
        return jnp.where(t >= 2 - k, pltpu.roll(x, 2 - k, 0), 0.0)
    return jnp.where(t < L - (k - 2), pltpu.roll(x, L - (k - 2), 0), 0.0)


def _conv_apply(x, w_ref, L):
    t = lax.broadcasted_iota(jnp.int32, x.shape, 0)
    acc = w_ref[4:5, :] + w_ref[2:3, :] * x
    for k in (0, 1, 3):
        acc = acc + w_ref[k:k + 1, :] * _conv_shift(x, t, L, k)
    return acc


def _conv_fwd(z, wb, B, L, tc=256):
    noff = D_MODEL // tc

    def body(z_ref, w_ref, o_ref):
        o_ref[...] = _conv_apply(z_ref[...], w_ref, L)

    return pl.pallas_call(
        body, name="rg_conv", grid=(B, noff),
        in_specs=[pl.BlockSpec((L, tc), lambda b, j: (b, noff + j)), pl.BlockSpec((SUBLANES, tc), lambda b, j: (0, j))],
        out_specs=pl.BlockSpec((L, tc), lambda b, j: (b, j)),
        out_shape=jax.ShapeDtypeStruct((B * L, D_MODEL), F32),
        compiler_params=_params(("parallel", "parallel")),
    )(z, wb)


def _conv_bwd(z, g, wb, dz, B, L, tc=256, after=None):
    noff = D_MODEL // tc
    order = _after_operand(after)

    def body(z_ref, g_ref, w_ref, dz_in, *rest):
        dx_ref, dw_ref = rest[len(order):]

        @pl.when(pl.program_id(1) == 0)
        def _():
            dw_ref[...] = jnp.zeros(dw_ref.shape, F32)

        x, gv = z_ref[...], g_ref[...]
        t = lax.broadcasted_iota(jnp.int32, x.shape, 0)
        dx = w_ref[2:3, :] * gv
        for k in (0, 1, 3):
            dx = dx + w_ref[k:k + 1, :] * _conv_shift(gv, t, L, 4 - k)
        dx_ref[...] = dx.astype(BF16)
        for k in range(4):
            dw_ref[k:k + 1, :] += _rsum(_conv_shift(x, t, L, k) * gv)
        dw_ref[4:5, :] += _rsum(gv)

    return pl.pallas_call(
        body, name="rg_conv_bwd", grid=(noff, B),
        in_specs=[pl.BlockSpec((L, tc), lambda j, b: (b, noff + j)), pl.BlockSpec((L, tc), lambda j, b: (b, j)),
                  pl.BlockSpec((SUBLANES, tc), lambda j, b: (0, j)), _ANY] + [_ANY] * len(order),
        out_specs=[pl.BlockSpec((L, tc), lambda j, b: (b, noff + j)),
                   pl.BlockSpec((SUBLANES, tc), lambda j, b: (0, j))],
        out_shape=[jax.ShapeDtypeStruct(dz.shape, dz.dtype), jax.ShapeDtypeStruct((SUBLANES, D_MODEL), F32)],
        input_output_aliases={3: 0},
        compiler_params=_params(("parallel", "arbitrary")),
    )(z, g, wb, dz, *order)


def _softplus(x):
    return jnp.maximum(x, 0.0) + jnp.log1p(jnp.exp(-jnp.abs(x)))


_ROW_BA, _ROW_BX, _ROW_LAM = 0, 2, 4


def _gate_math(xb, pre, vec_ref, d, sl):
    pa = pre[:, (2 * d) * LRU_BW:(2 * d + 1) * LRU_BW] + vec_ref[_ROW_BA + d:_ROW_BA + d + 1, sl]
    px = pre[:, (2 * d + 1) * LRU_BW:(2 * d + 2) * LRU_BW] + vec_ref[_ROW_BX + d:_ROW_BX + d + 1, sl]
    r = 0.5 * jnp.tanh(0.5 * pa) + 0.5
    i = 0.5 * jnp.tanh(0.5 * px) + 0.5
    sp = _softplus(-vec_ref[_ROW_LAM + d:_ROW_LAM + d + 1, sl])
    log_a = (-RG_C) * r * sp
    a = jnp.exp(log_a)
    th = jnp.tanh(log_a)
    om = -2.0 * th / (1.0 - th)
    mult = jnp.sqrt(om)
    return a, mult * (i * xb), (r, i, sp, om, mult)


def _gate_fwd(rec, wcat, gvec):
    def fn(ins, bs, outs, accs):
        for blk in range(LRU_BLOCKS):
            sl = slice(blk * LRU_BW, (blk + 1) * LRU_BW)
            xb = ins[0][:, sl]
            pre = jnp.dot(xb.astype(BF16), bs[0][sl, :], preferred_element_type=F32)
            for d in range(2):
                a, u, _ = _gate_math(xb, pre, bs[1], d, sl)
                outs[2 * d][:, sl] = a
                outs[2 * d + 1][:, sl] = u

    return _rowwise(fn, [rec], [wcat, gvec], [(D_MODEL, F32)] * 4, tm=256, name="rg_gate")


def _gate_bwd(rec, du_f, da_f, du_b, da_b, wcat, gvec):
    def fn(ins, bs, outs, accs):
        for blk in range(LRU_BLOCKS):
            sl = slice(blk * LRU_BW, (blk + 1) * LRU_BW)
            xb = ins[0][:, sl]
            xb16 = xb.astype(BF16)
            w = bs[0][sl, :]
            pre = jnp.dot(xb16, w, preferred_element_type=F32)
            dx = jnp.zeros_like(xb)
            dpre = []
            for d in range(2):
                a, _, (r, i, sp, om, mult) = _gate_math(xb, pre, bs[1], d, sl)
                du, da = ins[1 + 2 * d][:, sl], ins[2 + 2 * d][:, sl]
                d_i = du * mult * xb
                d_mult = du * i * xb
                dx = dx + du * mult * i
                dlog = da * a - d_mult * (1.0 - om) / mult
                d_r = dlog * ((-RG_C) * sp)
                d_sp = _rsum(dlog * ((-RG_C) * r))
                lam = bs[1][_ROW_LAM + d:_ROW_LAM + d + 1, sl]
                accs[2][_ROW_LAM + d:_ROW_LAM + d + 1, sl] += d_sp * (-jax.nn.sigmoid(-lam))
                dpa = d_r * r * (1.0 - r)
                dpx = d_i * i * (1.0 - i)
                accs[2][_ROW_BA + d:_ROW_BA + d + 1, sl] += _rsum(dpa)
                accs[2][_ROW_BX + d:_ROW_BX + d + 1, sl] += _rsum(dpx)
                dpre += [dpa, dpx]
            dpre = jnp.concatenate(dpre, axis=1).astype(BF16)
            dw = lax.dot_general(xb16, dpre, _TN, preferred_element_type=F32)
            for d in range(2):
                rows = slice(d * D_MODEL + blk * LRU_BW, d * D_MODEL + (blk + 1) * LRU_BW)
                accs[0][rows, :] += dw[:, (2 * d) * LRU_BW:(2 * d + 1) * LRU_BW]
                accs[1][rows, :] += dw[:, (2 * d + 1) * LRU_BW:(2 * d + 2) * LRU_BW]
            outs[0][:, sl] = dx + lax.dot_general(dpre, w, _NT, preferred_element_type=F32)

    gate_shape = (2 * D_MODEL, LRU_BW)
    return _rowwise(fn, [rec, du_f, da_f, du_b, da_b], [wcat, gvec], [(D_MODEL, F32)],
                    [gate_shape, gate_shape, (SUBLANES, D_MODEL)], tm=256, name="rg_gate_bwd")


def _as_time_blocks(x):
    return x.reshape(x.shape[0] // SUBLANES, SUBLANES, x.shape[1])


def _scan_call(body, ins, n_out, B, L, tc, name):
    nb = L // SUBLANES
    spec = pl.BlockSpec((nb, SUBLANES, tc), lambda b, j: (b, 0, j))
    T = ins[0].shape[0]
    outs = pl.pallas_call(
        functools.partial(body, nb), name=name, grid=(B, D_MODEL // tc),
        in_specs=[spec] * len(ins), out_specs=[spec] * n_out,
        out_shape=[jax.ShapeDtypeStruct((T // SUBLANES, SUBLANES, D_MODEL), F32)] * n_out,
        compiler_params=_params(("parallel", "parallel")),
    )(*[_as_time_blocks(x) for x in ins])
    return [o.reshape(T, D_MODEL) for o in outs]


def _block_scan(A, U, reverse):
    row = lax.broadcasted_iota(jnp.int32, A.shape, 0)
    for s in (1, 2, 4):
        shift = SUBLANES - s if reverse else s
        valid = (row < SUBLANES - s) if reverse else (row >= s)
        a_sh = jnp.where(valid, pltpu.roll(A, shift, 0), 1.0)
        u_sh = jnp.where(valid, pltpu.roll(U, shift, 0), 0.0)
        U = A * u_sh + U
        A = A * a_sh
    return A, U


_LAST = SUBLANES - 1
SCAN_UNROLL = 8


def _loop_blocks(nb, step, init):
    def group(g, carry):
        for k in range(SCAN_UNROLL):
            carry = step(g * SCAN_UNROLL + k, carry)
        return carry

    return lax.fori_loop(0, nb // SCAN_UNROLL, group, init)


def _scan_fwd(a_f, u_f, a_b, u_b, B, L, tc=256):
    def body(nb, af, uf, ab, ub, hf, hb):
        def step(i, carry):
            c1, c2 = carry
            ib = nb - 1 - i
            p, h = _block_scan(af[i], uf[i], False)
            h = h + p * c1
            hf[i] = h
            p2, h2 = _block_scan(ab[ib], ub[ib], True)
            h2 = h2 + p2 * c2
            hb[ib] = h2
            return h[_LAST:, :], h2[:1, :]

        zero = jnp.zeros((1, tc), F32)
        _loop_blocks(nb, step, (zero, zero))

    return _scan_call(body, [a_f, u_f, a_b, u_b], 2, B, L, tc, "rg_scan")


def _scan_bwd(dy, a_f, h_f, a_b, h_b, B, L, tc=256):
    def body(nb, dy_r, af, hf, ab, hb, duf, daf, dub, dab):
        def step(i, carry):
            c1, c2 = carry
            ir = nb - 1 - i
            row = lax.broadcasted_iota(jnp.int32, (SUBLANES, tc), 0)
            a_up = jnp.where(row == _LAST, af[jnp.minimum(ir + 1, nb - 1), :1, :], pltpu.roll(af[ir], _LAST, 0))
            p, lam = _block_scan(a_up, dy_r[ir], True)
            lam = lam + p * c1
            before = hf[jnp.maximum(ir - 1, 0), _LAST:, :] * (ir > 0).astype(F32)
            duf[ir] = lam
            daf[ir] = lam * jnp.where(row == 0, before, pltpu.roll(hf[ir], 1, 0))
            a_dn = jnp.where(row == 0, ab[jnp.maximum(i - 1, 0), _LAST:, :], pltpu.roll(ab[i], 1, 0))
            p2, lam2 = _block_scan(a_dn, dy_r[i], False)
            lam2 = lam2 + p2 * c2
            after = hb[jnp.minimum(i + 1, nb - 1), :1, :] * (i < nb - 1).astype(F32)
            dub[i] = lam2
            dab[i] = lam2 * jnp.where(row == _LAST, after, pltpu.roll(hb[i], _LAST, 0))
            return lam[:1, :], lam2[_LAST:, :]

        zero = jnp.zeros((1, tc), F32)
        _loop_blocks(nb, step, (zero, zero))

    return _scan_call(body, [dy, a_f, h_f, a_b, h_b], 4, B, L, tc, "rg_scan_bwd")


_GELU_C = math.sqrt(2.0 / math.pi)


def _gelu_parts(x):
    th = jnp.tanh(_GELU_C * (x + 0.044715 * x * x * x))
    return 0.5 * x * (1.0 + th), th


def _gated_out(h_f, h_b, z):
    def fn(ins, bs, outs, accs):
        gl, _ = _gelu_parts(ins[2][...])
        outs[0][...] = ((ins[0][...] + ins[1][...]) * gl).astype(BF16)

    return _rowwise(fn, [h_f, h_b, (z, D_MODEL, 0)], [], [(D_MODEL, BF16)], tm=512, name="rg_gated_out")[0]


def _gated_out_bwd(dyg, h_f, h_b, z):
    def fn(ins, bs, outs, accs):
        x = ins[3][...]
        gl, th = _gelu_parts(x)
        dgl = 0.5 * (1.0 + th) + 0.5 * x * (1.0 - th * th) * (_GELU_C * (1.0 + 3.0 * 0.044715 * x * x))
        g = ins[0][...]
        outs[0][...] = g * gl
        outs[1][...] = (g * (ins[1][...] + ins[2][...]) * dgl).astype(BF16)

    return _rowwise(fn, [dyg, h_f, h_b, (z, D_MODEL, 0)], [], [(D_MODEL, F32), (D_MODEL, BF16, 2 * D_MODEL)], tm=512,
                    name="rg_gated_out_bwd")


def _row_block(i):
    return pl.ds(pl.multiple_of(i * SUBLANES, SUBLANES), SUBLANES)


def _rg_mix_fwd(z, conv_wb, wcat, gvec, B, L):
    nb = L // SUBLANES
    n_g = D_MODEL // LRU_BW

    def body(zg_ref, zr_ref, cw_ref, w_ref, gv_ref, rec_ref, af_s, ab_s, hf_ref, hb_ref, yg_ref, uf_s, ub_s):
        rec = _conv_apply(zr_ref[...], cw_ref, L)
        rec_ref[...] = rec
        pre = jnp.dot(rec.astype(BF16), w_ref[...], preferred_element_type=F32)
        for d, (a_s, u_s) in enumerate(((af_s, uf_s), (ab_s, ub_s))):
            a, u, _ = _gate_math(rec, pre, gv_ref, d, slice(None))
            a_s[...] = a
            u_s[...] = u

        def step(i, carry):
            c1, c2 = carry
            rows, rows_b = _row_block(i), _row_block(nb - 1 - i)
            p, h = _block_scan(af_s[rows, :], uf_s[rows, :], False)
            h = h + p * c1
            hf_ref[rows, :] = h
            p2, h2 = _block_scan(ab_s[rows_b, :], ub_s[rows_b, :], True)
            h2 = h2 + p2 * c2
            hb_ref[rows_b, :] = h2
            return h[_LAST:, :], h2[:1, :]

        zero = jnp.zeros((1, LRU_BW), F32)
        _loop_blocks(nb, step, (zero, zero))
        gl, _ = _gelu_parts(zg_ref[...])
        yg_ref[...] = ((hf_ref[...] + hb_ref[...]) * gl).astype(BF16)

    seq = lambda off: pl.BlockSpec((L, LRU_BW), lambda b, g: (b, off + g))
    vec = pl.BlockSpec((SUBLANES, LRU_BW), lambda b, g: (0, g))
    T = B * L
    return pl.pallas_call(
        body, name="rg_mix", grid=(B, n_g),
        in_specs=[seq(0), seq(n_g), vec, pl.BlockSpec((LRU_BW, 4 * LRU_BW), lambda b, g: (g, 0)), vec],
        out_specs=[seq(0)] * 6,
        out_shape=[jax.ShapeDtypeStruct((T, D_MODEL), F32)] * 5 + [jax.ShapeDtypeStruct((T, D_MODEL), BF16)],
        scratch_shapes=[pltpu.VMEM((L, LRU_BW), F32)] * 2,
        compiler_params=_params(("parallel", "parallel")),
    )(z, z, conv_wb, wcat, gvec)


def _make_wcat(w_a, w_x):
    g = jnp.stack([w_a[0, 0], w_x[0, 0], w_a[0, 1], w_x[0, 1]])
    return jnp.transpose(g, (1, 2, 0, 3)).reshape(D_MODEL, 4 * LRU_BW)


def _rows_at(part, first):
    return jnp.pad(part, ((first, SUBLANES - first - part.shape[0]), (0, 0)))


def _qk_slot(q_g, k_g):
    wide = lambda v, at: jnp.pad(v, ((0, SUBLANES - 1), (at, D_MODEL - at - HEAD_DIM)))
    return wide(q_g, 0) + wide(k_g, HEAD_DIM)


def _local_step(x, target, P, fetch, emit, B, L, after=None):
    g_mix, g_mlp = P["norm_mix_g"], P["norm_mlp_g"]
    h0 = _rms_fwd(x, g_mix[0:1], "rg_norm", after=after)
    w_in, w_out, conv_wb, wcat, gvec = fetch("rg", h0)
    z = _mm(h0, w_in, mode="nn", b_shard=True, name="rg_in")
    rec, a_f, a_b, h_f, h_b, yg = _rg_mix_fwd(z, conv_wb, wcat, gvec, B, L)
    x1, h1 = _mm_res_norm(yg, w_out, x, g_mlp[0:1], "rg_out")
    (x2, h3), mlp0 = _mlp_fwd(x1, h1, fetch, 0, lambda a, w, res, name: _mm_res_norm(a, w, res, g_mix[1:2], name))
    w_qkv, w_o = fetch("att", h3)
    qkv = _mm(h3, w_qkv, mode="nn", b_shard=True, name="attn_qkv")
    cos, sin = _rope_tables(L, B)
    qh, kh, vh = _qk_prep(qkv, cos, sin, P["q_g"], P["k_g"])
    o = _attn_fwd(qh, kh, vh, B, L)
    x3, h4 = _mm_res_norm(o, w_o, x2, g_mlp[1:2], "attn_out")
    (dx4, dx4_bf, loss_acc, d_final_g), mlp1 = _mlp_fwd(
        x3, h4, fetch, 1, lambda a, w, res, name: _mm_final_loss(a, w, res, target, P["final_g"], name))

    dx3, dx3_bf, dg_mlp1, d_up1, d_down1 = _mlp_bwd(x3, g_mlp[1:2], mlp1, dx4, dx4_bf, 1, None)
    tok = emit("mlp1", [d_up1, d_down1])
    d_wo = _mm(o, dx3_bf, mode="tn", out_dtypes=(BF16,), name="attn_dwo", after=tok)
    do = _mm(dx3_bf, w_o, mode="nt", out_dtypes=(BF16,), name="attn_do")
    dq, dk, dv = _attn_bwd(qh, kh, vh, do, B, L)
    dqkv, dq_g, dk_g = _qk_prep_bwd(qkv, dq, dk, dv, cos, sin, P["q_g"], P["k_g"])
    d_wqkv = _mm(h3, dqkv, mode="tn", o_shard=True, out_dtypes=(BF16,), name="attn_dwqkv")
    tok = emit("att", [d_wqkv, d_wo])
    dx2, dx2_bf, dg_mix1 = _mm_norm_bwd(dqkv, w_qkv, x2, dx3, g_mix[1:2], "attn_dh", after=tok)
    tok = emit("point_attn_done", [dx2_bf])
    dx1, dx1_bf, dg_mlp0, d_up0, d_down0 = _mlp_bwd(x1, g_mlp[0:1], mlp0, dx2, dx2_bf, 0, tok)
    tok = emit("mlp0", [d_up0, d_down0])
    d_wout = _mm(yg, dx1_bf, mode="tn", out_dtypes=(BF16,), name="rg_dwout", after=tok)
    tok = emit("rg_out", [d_wout])
    dyg = _mm(dx1_bf, w_out, mode="nt", name="rg_dyg", after=tok)
    dy, dgate = _gated_out_bwd(dyg, h_f, h_b, z)
    du_f, da_f, du_b, da_b = _scan_bwd(dy, a_f, h_f, a_b, h_b, B, L)
    drec_c, d_wa, d_wx, d_gvec = _gate_bwd(rec, du_f, da_f, du_b, da_b, wcat, gvec)
    tok = emit("gates", [d_wa, d_wx])
    dz, d_convwb = _conv_bwd(z, drec_c, conv_wb, dgate, B, L, after=tok)
    tok = emit("point_mix_done", [dz])
    d_win = _mm(h0, dz, mode="tn", o_shard=True, out_dtypes=(BF16,), name="rg_dwin", after=tok)
    tok = emit("rg_in", [d_win])
    grad_x, _, dg_mix0 = _mm_norm_bwd(dz, w_in, x, dx1, g_mix[0:1], "rg_dh", after=tok)

    norms = (_rows_at(dg_mix0, 0) + _rows_at(dg_mix1, 1) + _rows_at(dg_mlp0, 2) + _rows_at(dg_mlp1, 3)
             + _rows_at(d_final_g, 4)
             + jnp.pad(loss_acc, ((LOSS_ROW, SUBLANES - 1 - LOSS_ROW), (0, D_MODEL - LANES))))
    vec = jnp.concatenate([norms, d_convwb, d_gvec, _qk_slot(dq_g, dk_g)], axis=0)
    return grad_x, vec


_MESH = pl.DeviceIdType.MESH


def _place():
    x, y, c = lax.axis_index("x"), lax.axis_index("y"), lax.axis_index("c")
    peers = [((1 - x) if j & 2 else x, (1 - y) if j & 1 else y) for j in (1, 2, 3)]
    return x, y, c, peers


def _comm_call(body, ins, out_shapes, n_sem, name):
    return pl.pallas_call(
        body, name=name, in_specs=[_ANY] * len(ins), out_specs=[_ANY] * len(out_shapes), out_shape=out_shapes,
        scratch_shapes=[pltpu.SemaphoreType.DMA((n_sem,)), pltpu.SemaphoreType.DMA((n_sem,)),
                        pltpu.SemaphoreType.DMA((len(ins),))],
    )(*ins)


def _all_devices_slots(v, name):
    def body(v_ref, out_ref, send, recv, lsem):
        x, y, c = lax.axis_index("x"), lax.axis_index("y"), lax.axis_index("c")
        me = 4 * x + 2 * y + c

        def peer(j):
            return (1 - x) if j & 4 else x, (1 - y) if j & 2 else y, (1 - c) if j & 1 else c

        def copy(j, slot):
            return pltpu.make_async_remote_copy(
                src_ref=v_ref, dst_ref=out_ref.at[slot], send_sem=send.at[j - 1], recv_sem=recv.at[j - 1],
                device_id=peer(j), device_id_type=_MESH)

        local = pltpu.make_async_copy(v_ref, out_ref.at[me], lsem.at[0])
        sends = [copy(j, me) for j in range(1, N_DEVICES)]
        for cp in [local] + sends:
            cp.start()
        for j in range(1, N_DEVICES):
            px, py, pc = peer(j)
            copy(j, 4 * px + 2 * py + pc).wait_recv()
        for cp in sends:
            cp.wait_send()
        local.wait()

    shape = jax.ShapeDtypeStruct((N_DEVICES,) + v.shape, v.dtype)
    return _comm_call(body, [v], [shape], N_DEVICES - 1, name)[0]


def _sum_leading(slots, name):
    def body(s_ref, o_ref):
        acc = s_ref[0]
        for d in range(1, slots.shape[0]):
            acc = acc + s_ref[d]
        o_ref[...] = acc

    return pl.pallas_call(body, name=name, out_shape=jax.ShapeDtypeStruct(slots.shape[1:], slots.dtype))(slots)


_HBM = pl.BlockSpec(memory_space=pltpu.HBM)
_SEM = pl.BlockSpec(memory_space=pltpu.SEMAPHORE)
_EFFECT = pltpu.SideEffectType.DATAFLOW_SIDE_EFFECTING


_COPIES = dict(gather=N_CHIPS - 1, scatter=N_CHIPS - 1, swap=1)


def _split_copies(kind, srcs, lands, send, recv):
    x, y, c, peers = _place()
    me = 2 * x + y
    per = _COPIES[kind]
    out = []
    for a in range(len(lands)):
        for j in range(per):
            if kind == "swap":
                src, there, here, dev = srcs[a], lands[a], lands[a], (x, y, 1 - c)
            else:
                px, py = peers[j]
                dev = (px, py, c)
                if kind == "gather":
                    src, there, here = lands[a].at[me], lands[a].at[me], lands[a].at[2 * px + py]
                else:
                    src, there, here = srcs[a].at[2 * px + py], lands[a].at[j], lands[a].at[j]
            mk = functools.partial(
                pltpu.make_async_remote_copy, src_ref=src, send_sem=send.at[per * a + j],
                recv_sem=recv.at[per * a + j], device_id=dev, device_id_type=_MESH)
            out.append((functools.partial(mk, dst_ref=there), functools.partial(mk, dst_ref=here)))
    return out


def _exchange_start(kind, srcs, lands, name, after=None):
    arrays = list(srcs) + list(lands)
    n_s, n, n_all = len(srcs), len(lands), len(srcs) + len(lands)
    n_sem = _COPIES[kind] * n
    order = _after_operand(after)
    n_x = len(order)

    def body(*refs):
        send, recv = refs[n_all + n_x], refs[n_all + n_x + 1]
        token = refs[-1]
        for started, _ in _split_copies(kind, refs[:n_s], refs[n_s:n_all], send, recv):
            started().start()
        token[...] = jnp.zeros(token.shape, F32)

    res = pl.pallas_call(
        body, name=name,
        out_shape=(pltpu.SemaphoreType.DMA((n_sem,)), pltpu.SemaphoreType.DMA((n_sem,)),
                   *[pltpu.HBM(a.shape, a.dtype) for a in arrays], jax.ShapeDtypeStruct((SUBLANES, LANES), F32)),
        in_specs=[_HBM] * n_all + [_ANY] * n_x,
        out_specs=(_SEM, _SEM, *[_HBM] * n_all, pl.BlockSpec(memory_space=pltpu.VMEM)),
        input_output_aliases={i: 2 + i for i in range(n_all)},
        compiler_params=pltpu.CompilerParams(has_side_effects=_EFFECT),
    )(*[pltpu.with_memory_space_constraint(a, pltpu.HBM) for a in arrays], *order)
    return (res[0], res[1], res[2:2 + n_s], res[2 + n_s:2 + n_all]), res[-1]


def _exchange_wait(kind, handle, after, name):
    send, recv, srcs, lands = handle
    arrays = list(srcs) + list(lands)
    n_s, n_all = len(srcs), len(arrays)
    order = list(after) if isinstance(after, (list, tuple)) else [after]

    def body(*refs):
        for started, landing in _split_copies(kind, refs[:n_s], refs[n_s:n_all], refs[n_all], refs[n_all + 1]):
            started().wait_send()
            landing().wait_recv()

    res = pl.pallas_call(
        body, name=name, out_shape=[pltpu.HBM(a.shape, a.dtype) for a in arrays],
        in_specs=[_HBM] * n_all + [_SEM, _SEM] + [_ANY] * len(order), out_specs=[_HBM] * n_all,
        input_output_aliases={i: i for i in range(n_all)},
        compiler_params=pltpu.CompilerParams(has_side_effects=_EFFECT),
    )(*arrays, send, recv, *order)
    return res[:n_s], res[n_s:]


def _index_operand(i):
    return jnp.reshape(i, (1,)).astype(jnp.int32)


def _cast_into_slot(src, row0, rows, me, dtype, name, after=None, add=None):
    cols = src.shape[1]
    tm = min(512, rows)
    order = _after_operand(after)
    terms = [src] + ([] if add is None else [add])

    def body(me_ref, *rest):
        val = rest[0][...]
        if add is not None:
            val = val + rest[1][...]
        rest[-1][...] = val.astype(dtype)

    return pl.pallas_call(
        body, name=name,
        grid_spec=pltpu.PrefetchScalarGridSpec(
            num_scalar_prefetch=1, grid=(rows // tm,),
            in_specs=[pl.BlockSpec((tm, cols), lambda i, me_ref: (i + row0 // tm, 0))] * len(terms)
            + [_ANY] * len(order),
            out_specs=pl.BlockSpec((None, tm, cols), lambda i, me_ref: (me_ref[0], i, 0))),
        out_shape=jax.ShapeDtypeStruct((N_CHIPS, rows, cols), dtype), compiler_params=_params(("parallel",)),
    )(_index_operand(me), *terms, *order)


def _sum_slots(mine, r, me, name):
    _, rows, cols = r.shape
    tm = min(512, rows)

    def body(me_ref, own_ref, r_ref, o_ref):
        o_ref[...] = ((own_ref[...].astype(F32) + r_ref[0].astype(F32)) + r_ref[1].astype(F32)) + r_ref[2].astype(F32)

    return pl.pallas_call(
        body, name=name,
        grid_spec=pltpu.PrefetchScalarGridSpec(
            num_scalar_prefetch=1, grid=(rows // tm,),
            in_specs=[pl.BlockSpec((None, tm, cols), lambda i, me_ref: (me_ref[0], i, 0)),
                      pl.BlockSpec((N_CHIPS - 1, tm, cols), lambda i, me_ref: (0, i, 0))],
            out_specs=pl.BlockSpec((tm, cols), lambda i, me_ref: (i, 0))),
        out_shape=jax.ShapeDtypeStruct((rows, cols), F32), compiler_params=_params(("parallel",)),
    )(_index_operand(me), mine, r)


def _adamw(w, m, v, ps, qs, name):
    rows, cols = w.shape
    seg_rows = ps[0].shape[0]
    tm = min(256, seg_rows)
    while seg_rows % tm:
        tm -= SUBLANES
    per, n_seg = seg_rows // tm, len(ps)
    parts = list(ps) + ([] if qs is None else list(qs))

    def body(w_ref, m_ref, v_ref, *rest):
        g_refs, outs = rest[:len(parts)], rest[len(parts):]
        grad = lambda s: g_refs[s][...] if qs is None else g_refs[s][...] + g_refs[n_seg + s][...]
        g = grad(0)
        for s in range(1, n_seg):
            g = jnp.where(pl.program_id(0) >= s * per, grad(s), g)
        m1 = ADAM_B1 * m_ref[...] + (1.0 - ADAM_B1) * g
        v1 = ADAM_B2 * v_ref[...] + (1.0 - ADAM_B2) * (g * g)
        m_hat = m1 / (1.0 - ADAM_B1 ** ADAM_STEP)
        v_hat = v1 / (1.0 - ADAM_B2 ** ADAM_STEP)
        outs[0][...] = g
        outs[1][...] = (-ADAM_LR) * (m_hat / (jnp.sqrt(v_hat) + ADAM_EPS) + ADAM_WD * w_ref[...])
        outs[2][...] = m1
        outs[3][...] = v1

    row_spec = pl.BlockSpec((tm, cols), lambda i: (i, 0))
    seg_spec = lambda s: pl.BlockSpec((tm, cols), lambda i: (jnp.clip(i - s * per, 0, per - 1), 0))
    return pl.pallas_call(
        body, name=name, grid=(rows // tm,),
        in_specs=[row_spec] * 3 + [seg_spec(s) for s in range(n_seg)] * (1 if qs is None else 2),
        out_specs=[row_spec] * 4, out_shape=[jax.ShapeDtypeStruct((rows, cols), F32)] * 4,
        compiler_params=_params(("arbitrary",)),
    )(w, m, v, *parts)


def _put_cols(shard, me):
    full = jnp.zeros((shard.shape[0], D_MODEL), F32)
    return lax.dynamic_update_slice(full, shard, (0, me * (D_MODEL // N_CHIPS)))


def _gate_vec_slot(b_a, b_x, lam):
    return _rows_at(b_a, _ROW_BA) + _rows_at(b_x, _ROW_BX) + _rows_at(lam, _ROW_LAM)


def _pack_vec(p, me):
    return jnp.concatenate([
        _rows_at(p["norm_mix_g"], 0) + _rows_at(p["norm_mlp_g"], 2) + _rows_at(p["final_g"][None], 4),
        _rows_at(_put_cols(p["rg_conv_w"][0, :, 0, :], me), 0) + _rows_at(p["rg_conv_b"], 4),
        _gate_vec_slot(_put_cols(p["rg_b_a"][0], me), _put_cols(p["rg_b_x"][0], me), _put_cols(p["rg_lam"][0], me)),
        _qk_slot(p["at_q_g"], p["at_k_g"]),
    ], axis=0)


def _unpack_vec(r, me):
    def cols(rows):
        return lax.dynamic_slice(rows, (0, me * (D_MODEL // N_CHIPS)), (rows.shape[0], D_MODEL // N_CHIPS))

    gate = r[16:24]
    return dict(
        norm_mix_g=r[0:2], norm_mlp_g=r[2:4], final_g=r[4], rg_conv_w=cols(r[8:12])[None, :, None, :],
        rg_conv_b=r[12:13], rg_b_a=cols(gate[_ROW_BA:_ROW_BA + 2])[None], rg_b_x=cols(gate[_ROW_BX:_ROW_BX + 2])[None],
        rg_lam=cols(gate[_ROW_LAM:_ROW_LAM + 2])[None], at_q_g=r[24:25, 0:HEAD_DIM],
        at_k_g=r[24:25, HEAD_DIM:2 * HEAD_DIM])


_WEIGHTS = ['norm_mix_g', 'norm_mlp_g', 'rg_w_in', 'rg_conv_w', 'rg_conv_b', 'rg_w_a', 'rg_b_a', 'rg_w_x', 'rg_b_x',
            'rg_lam', 'rg_w_out', 'at_w_qkv', 'at_q_g', 'at_k_g', 'at_w_o', 'mlp_w_up', 'mlp_w_down', 'final_g']
_BIG = dict(rg_w_in=["rg_w_in"], rg_w_out=["rg_w_out"], at_w_qkv=["at_w_qkv"], at_w_o=["at_w_o"],
            mlp_w_up=["up0", "up1"], mlp_w_down=["down0", "down1"])


def kernel(x, *args):
    n_w = len(_WEIGHTS)
    w = dict(zip(_WEIGHTS, args[:n_w]))
    target = args[n_w]
    m = dict(zip(_WEIGHTS, args[n_w + 1:2 * n_w + 1]))
    v = dict(zip(_WEIGHTS, args[2 * n_w + 1:3 * n_w + 1]))
    B, L, _ = x.shape
    T = B * L
    me = 2 * lax.axis_index("x") + lax.axis_index("y")

    vec = jnp.concatenate([_gate_vec_slot(w["rg_b_a"][0], w["rg_b_x"][0], w["rg_lam"][0]),
                           _rows_at(w["rg_conv_w"][0, :, 0, :], 0)], axis=0)
    flat = lambda a: a.reshape(-1, a.shape[-1])
    rows_of = lambda k: w[k].shape[-2]
    groups = [("rg", [("rg_w_in", 0, BF16), ("rg_w_out", 0, BF16), (vec, 0, F32)]),
              ("mlp0_up", [("mlp_w_up", 0, BF16)]), ("mlp0_down", [("mlp_w_down", 0, BF16)]),
              ("att", [("at_w_qkv", 0, BF16), ("at_w_o", 0, BF16)]),
              ("mlp1", [("mlp_w_up", 1, BF16), ("mlp_w_down", 1, BF16)])]
    gathers, tok = {}, None
    for group, members in groups:
        lands = []
        for n, (k, layer, dtype) in enumerate(members):
            src, rows = (flat(w[k]), rows_of(k)) if isinstance(k, str) else (k, k.shape[0])
            lands.append(_cast_into_slot(src, layer * rows, rows, me, dtype, f"place_{group}{n}", after=tok))
        gathers[group], tok = _exchange_start("gather", [], lands, f"gather_{group}_start", after=tok)
    wcat = _make_wcat(w["rg_w_a"], w["rg_w_x"]).astype(BF16)

    packs = [_pack_vec(p, me) for p in (w, m, v)]

    ready = {}

    def fetch(what, after):
        if what in ready:
            return ready[what]
        group = "mlp1" if what.startswith("mlp1") else what
        order = [after, wcat] + packs if group == "rg" else after
        _, full = _exchange_wait("gather", gathers[group], order, f"gather_{group}_wait")
        if group == "rg":
            vec_full = jnp.transpose(full[2], (1, 0, 2)).reshape(2 * SUBLANES, D_MODEL)
            conv_wb = vec_full[SUBLANES:] + _rows_at(w["rg_conv_b"], 4)
            return full[0], full[1].reshape(D_MODEL, D_MODEL), conv_wb, wcat, vec_full[:SUBLANES]
        if group == "att":
            return full[0], full[1].reshape(D_MODEL, D_MODEL)
        if group == "mlp1":
            ready["mlp1_up"], ready["mlp1_down"] = full[0], full[1].reshape(4 * D_MODEL, D_MODEL)
            return ready[what]
        return full[0] if group == "mlp0_up" else full[0].reshape(4 * D_MODEL, D_MODEL)

    names = dict(mlp1=["up1", "down1"], att=["at_w_qkv", "at_w_o"], mlp0=["up0", "down0"], rg_out=["rg_w_out"],
                 rg_in=["rg_w_in"], gates=["rg_w_a", "rg_w_x"])
    scatters, swaps, P, Q, res = {}, [], {}, {}, {}

    def start_scatter(group, grads):
        srcs = [g.reshape(N_CHIPS, -1, g.shape[-1]) for g in grads]
        lands = [lax.empty((N_CHIPS - 1,) + s.shape[1:], s.dtype) for s in srcs]
        scatters[group], token = _exchange_start("scatter", srcs, lands, f"scatter_{group}_start")
        return token

    def settle(groups, after):
        keys, parts = [], []
        for group in groups:
            srcs, lands = _exchange_wait("scatter", scatters[group], after, f"scatter_{group}_wait")
            for k, s, r in zip(names[group], srcs, lands):
                keys.append(k)
                parts.append(_sum_slots(s, r, me, f"sum_{k}"))
        handle, token = _exchange_start("swap", parts, [lax.empty(p.shape, F32) for p in parts],
                                        f"swap_{groups[0]}_start")
        swaps.append((keys, handle, f"swap_{groups[0]}_wait"))
        return token

    def finish(after):
        for keys, handle, name in swaps:
            mine, theirs = _exchange_wait("swap", handle, after, name)
            P.update(zip(keys, mine))
            Q.update(zip(keys, theirs))
        swaps.clear()
        last = after
        for k, parts in _BIG.items():
            if k in res or any(p not in P for p in parts):
                continue
            shape = w[k].shape
            two_d = lambda a: a.reshape(-1, shape[-1])
            outs = _adamw(two_d(w[k]), two_d(m[k]), two_d(v[k]), [P[p] for p in parts], [Q[p] for p in parts],
                          f"adamw_{k}")
            res[k] = [o.reshape(shape) for o in outs]
            last = outs[0]
        if "rg_w_a" in P and "gates" not in gathers:
            lands = [_cast_into_slot(P[k], 0, P[k].shape[0], me, F32, f"place_{k}", after=last, add=Q[k])
                     for k in names["gates"]]
            gathers["gates"], last = _exchange_start("gather", [], lands, "gather_gates_start", after=last)
        return last

    def emit(event, arrays):
        if event == "point_attn_done":
            return settle(["mlp1"], arrays[0])
        if event == "point_mix_done":
            return settle(["att", "mlp0", "rg_out"], arrays[0])
        token = start_scatter(event, arrays)
        if event == "rg_in":
            return finish(settle(["gates"], token))
        return token

    P_vec = dict(norm_mix_g=w["norm_mix_g"], norm_mlp_g=w["norm_mlp_g"], final_g=w["final_g"][None],
                 q_g=w["at_q_g"], k_g=w["at_k_g"])
    grad_x, vec_part = _local_step(x.reshape(T, D_MODEL), target.reshape(T, D_MODEL), P_vec, fetch, emit, B, L,
                                   after=tok)

    finish(settle(["rg_in"], grad_x))
    _, gate_grads = _exchange_wait("gather", gathers["gates"], grad_x, "gather_gates_wait")
    for k, g in zip(names["gates"], gate_grads):
        two_d = lambda a: a.reshape(g.shape[0] * g.shape[1], g.shape[2])
        outs = _adamw(two_d(w[k]), two_d(m[k]), two_d(v[k]), [two_d(g)], None, f"adamw_{k}")
        res[k] = [o.reshape(w[k].shape) for o in outs]
    vec_grad = _sum_leading(_all_devices_slots(vec_part, "allreduce_vec"), "sum_vec")
    loss = vec_grad[LOSS_ROW, 0]
    outs = _adamw(*packs, [vec_grad], None, "adamw_vec")
    unpacked = [_unpack_vec(o, me) for o in outs]
    for k in _WEIGHTS:
        if k not in res:
            res[k] = [u[k] for u in unpacked]

    result = [loss, grad_x.reshape(B, L, D_MODEL)]
    for slot in range(4):
        result += [res[k][slot] for k in _WEIGHTS]
    return tuple(result)
```

```python
import functools
import math

import jax
import jax.numpy as jnp
import numpy as np
from jax import lax
from jax.experimental import pallas as pl
from jax.experimental.pallas import tpu as pltpu

F32 = jnp.float32
BF16 = jnp.bfloat16

D_MODEL = 1024
HEAD_DIM = 128
N_HEADS = 8
N_KV = 2
GROUP = N_HEADS // N_KV
LRU_BLOCKS = 8
LRU_BW = 128
GRID_W = 64
ROPE_THETA = 10000.0
EPS = 1e-6
RG_C = 8.0
SCALE = 1.0 / math.sqrt(HEAD_DIM)
N_CHIPS = 4

ADAM_LR = 0.001
ADAM_B1 = 0.9
ADAM_B2 = 0.999
ADAM_EPS = 1e-08
ADAM_WD = 0.01
ADAM_STEP = 10

V7X_VMEM_BYTES = 64 * 1024 * 1024
VMEM_LIMIT = V7X_VMEM_BYTES * 3 // 4
LANES = 128
SUBLANES = 8

N_DEVICES = 8
VEC_ROWS = 32
LOSS_ROW = 5


def _params(sem):
    return pltpu.CompilerParams(dimension_semantics=sem, vmem_limit_bytes=VMEM_LIMIT)


_ANY = pl.BlockSpec(memory_space=pl.ANY)
_NN = (((1,), (0,)), ((), ()))
_NT = (((1,), (1,)), ((), ()))
_TN = (((0,), (0,)), ((), ()))


def _after_operand(after):
    return [] if after is None else [after]


def _fit(t, n):
    if n <= t:
        return n
    c = (t // LANES) * LANES
    while n % c:
        c -= LANES
    return c


MM_VMEM_BUDGET = VMEM_LIMIT * 3 // 4
def _mm_tiles(M, K, ns, n_total, out_dtypes, extras, whole_rows):
    for tm in (2048, 1024, 512, 256, 128):
        for tn in ((ns,) if whole_rows else (1024, 512, 256)):
            tn = _fit(tn, ns)
            per_row = 2 * (2 * K) + 4 * tn + sum(2 * tn * jnp.dtype(d).itemsize for d in out_dtypes)
            per_row += sum(2 * tn * e.dtype.itemsize for e in extras)
            b_buffers = 1 if tn == n_total else 2
            if M % tm == 0 and b_buffers * (2 * K * tn) + tm * per_row <= MM_VMEM_BUDGET:
                return tm, tn
    raise ValueError(f"no tile fits VMEM for M={M} K={K} N={ns}")


def _mm(a, b, *, mode, name, out_dtypes=(F32,), b_shard=False, o_shard=False, extras=(), epi=None, after=None,
        bcast=(), accs=(), ref_epi=None, out_cols=None):
    if mode == "tn":
        K, M = a.shape
        N = b.shape[1]
    else:
        M, K = a.shape
        if mode == "nn":
            N = b.shape[0] * b.shape[2] if b_shard else b.shape[1]
        else:
            N = b.shape[1] if b_shard else b.shape[0]
    ns = N
    if b_shard and mode == "nn":
        ns = b.shape[2]
    elif o_shard:
        ns = N // N_CHIPS
    tm, tn = _mm_tiles(M, K, ns, N, out_dtypes, extras, whole_rows=ref_epi is not None)
    if ref_epi is not None:
        tm = min(tm, 512)
    grid = (M // tm, N // tn)
    q = ns // tn
    once = dict(pipeline_mode=pl.Buffered(1)) if tn == N else {}

    if mode == "tn":
        a_spec = pl.BlockSpec((K, tm), lambda i, j: (0, i))
        b_spec = pl.BlockSpec((K, tn), lambda i, j: (0, j), **once)
        dims = _TN
    elif mode == "nn":
        a_spec = pl.BlockSpec((tm, K), lambda i, j: (i, 0))
        if b_shard:
            b_spec = pl.BlockSpec((None, K, tn), lambda i, j: (j // q, 0, j % q), **once)
        else:
            b_spec = pl.BlockSpec((K, tn), lambda i, j: (0, j), **once)
        dims = _NN
    else:
        a_spec = pl.BlockSpec((tm, K), lambda i, j: (i, 0))
        if b_shard:
            ks = b.shape[2]
            b_spec = pl.BlockSpec((N_CHIPS, tn, ks), lambda i, j: (0, j, 0), **once)
        else:
            b_spec = pl.BlockSpec((tn, K), lambda i, j: (j, 0), **once)
        dims = _NT

    if o_shard:
        o_specs = [pl.BlockSpec((None, tm, tn), lambda i, j: (j // q, i, j % q))]
        o_shapes = [jax.ShapeDtypeStruct((N_CHIPS, M, ns), out_dtypes[0])]
    else:
        o_specs = [pl.BlockSpec((tm, tn), lambda i, j: (i, j)) for _ in out_dtypes]
        o_shapes = [jax.ShapeDtypeStruct((M, N if out_cols is None else out_cols[n]), dt)
                    for n, dt in enumerate(out_dtypes)]
    e_specs = [pl.BlockSpec((tm, tn), lambda i, j: (i, j)) for _ in extras]
    e_specs += [pl.BlockSpec(v.shape, lambda i, j: (0, 0)) for v in bcast]
    o_specs += [pl.BlockSpec(s, lambda i, j: (0, 0)) for s in accs]
    o_shapes += [jax.ShapeDtypeStruct(s, F32) for s in accs]
    n_e, n_b, n_o, n_a = len(extras), len(bcast), len(out_dtypes), len(accs)
    order = _after_operand(after)
    n_x = len(order)
    if epi is None:
        epi = lambda acc: (acc,)

    def body(a_ref, b_ref, *rest):
        e_refs, b_refs = rest[:n_e], rest[n_e:n_e + n_b]
        o_refs = rest[n_e + n_b + n_x:n_e + n_b + n_x + n_o]
        a_refs = rest[n_e + n_b + n_x + n_o:]
        if n_a:
            @pl.when((pl.program_id(0) == 0) & (pl.program_id(1) == 0))
            def _():
                for r in a_refs:
                    r[...] = jnp.zeros(r.shape, F32)
        if mode == "nt" and b_shard:
            acc = None
            for s in range(N_CHIPS):
                part = lax.dot_general(a_ref[:, s * ks:(s + 1) * ks], b_ref[s], dims, preferred_element_type=F32)
                acc = part if acc is None else acc + part
        else:
            acc = lax.dot_general(a_ref[...], b_ref[...], dims, preferred_element_type=F32)
        if ref_epi is not None:
            ref_epi(acc, e_refs, b_refs, o_refs, a_refs)
            return
        outs = epi(acc, *[r[...] for r in e_refs])
        for r, o in zip(o_refs, outs):
            r[...] = o.astype(r.dtype)

    outs = pl.pallas_call(
        body, name=name, grid=grid, in_specs=[a_spec, b_spec] + e_specs + [_ANY] * n_x, out_specs=o_specs,
        out_shape=o_shapes, compiler_params=_params(("arbitrary", "arbitrary") if n_a else ("parallel", "parallel")),
    )(a, b, *extras, *bcast, *order)
    return outs[0] if n_o + n_a == 1 else outs


def _rowwise(fn, rows, bcast, outs, accs=(), *, tm, name, after=None):
    def norm(r):
        return r if isinstance(r, tuple) else (r, r.shape[1], 0)

    rows = [norm(r) for r in rows]
    T = rows[0][0].shape[0]
    tm = min(tm, T)
    while T % tm:
        tm -= SUBLANES
    n_r, n_b, n_o, n_a = len(rows), len(bcast), len(outs), len(accs)
    order = _after_operand(after)
    n_x = len(order)
    in_specs = [pl.BlockSpec((tm, c), functools.partial(lambda i, cb: (i, cb), cb=cb)) for _, c, cb in rows]
    in_specs += [pl.BlockSpec(b.shape, lambda i: (0, 0)) for b in bcast] + [_ANY] * n_x
    out_specs = [pl.BlockSpec((tm, o[0]), lambda i: (i, 0)) for o in outs]
    out_specs += [pl.BlockSpec(s, lambda i: (0, 0)) for s in accs]
    out_shape = [jax.ShapeDtypeStruct((T, o[2] if len(o) > 2 else o[0]), o[1]) for o in outs]
    out_shape += [jax.ShapeDtypeStruct(s, F32) for s in accs]

    def body(*refs):
        in_refs = refs[:n_r]
        b_refs = refs[n_r:n_r + n_b]
        o_refs = refs[n_r + n_b + n_x:n_r + n_b + n_x + n_o]
        a_refs = refs[n_r + n_b + n_x + n_o:]
        if n_a:
            @pl.when(pl.program_id(0) == 0)
            def _():
                for r in a_refs:
                    r[...] = jnp.zeros(r.shape, F32)
        fn(in_refs, b_refs, o_refs, a_refs)

    res = pl.pallas_call(
        body, name=name, grid=(T // tm,), in_specs=in_specs, out_specs=out_specs, out_shape=out_shape,
        compiler_params=_params(("arbitrary",) if n_a else ("parallel",)),
    )(*[r[0] for r in rows], *bcast, *order)
    return res


def _rsum(x):
    return jnp.sum(x, axis=0, keepdims=True)


def _rms_fwd(x, g, name, after=None):
    def fn(ins, bs, outs, accs):
        xv = ins[0][...]
        r = lax.rsqrt(jnp.mean(xv * xv, axis=-1, keepdims=True) + EPS)
        outs[0][...] = (xv * r * bs[0][...]).astype(BF16)

    return _rowwise(fn, [x], [g], [(D_MODEL, BF16)], tm=512, name=name, after=after)[0]


def _rms_bwd_math(xv, dh, g):
    r = lax.rsqrt(jnp.mean(xv * xv, axis=-1, keepdims=True) + EPS)
    hn = xv * r
    dgh = dh * g
    dx = r * (dgh - hn * jnp.mean(dgh * hn, axis=-1, keepdims=True))
    return dx, _rsum(dh * hn)


def _mm_norm_bwd(dy, w, x, dres, g, name, after=None):
    def epilogue(acc, e_refs, b_refs, o_refs, a_refs):
        dx, dg = _rms_bwd_math(e_refs[0][...], acc, b_refs[0][...])
        dx = dx + e_refs[1][...]
        o_refs[0][...] = dx
        o_refs[1][...] = dx.astype(BF16)
        a_refs[0][...] += dg

    return _mm(dy, w, mode="nt", b_shard=True, out_dtypes=(F32, BF16), extras=(x, dres), bcast=(g,),
               accs=((1, D_MODEL),), ref_epi=epilogue, name=name, after=after)


def _mm_res_norm(a, w, res, g, name):
    def epilogue(acc, e_refs, b_refs, o_refs, a_refs):
        xv = acc + e_refs[0][...]
        o_refs[0][...] = xv
        r = lax.rsqrt(jnp.mean(xv * xv, axis=-1, keepdims=True) + EPS)
        o_refs[1][...] = (xv * r * b_refs[0][...]).astype(BF16)

    return _mm(a, w, mode="nn", out_dtypes=(F32, BF16), extras=(res,), bcast=(g,), ref_epi=epilogue, name=name)


def _mm_final_loss(a, w, res, target, g, name):
    def epilogue(acc, e_refs, b_refs, o_refs, a_refs):
        xv = acc + e_refs[0][...]
        gv = b_refs[0][...]
        r = lax.rsqrt(jnp.mean(xv * xv, axis=-1, keepdims=True) + EPS)
        e = xv * r * gv - e_refs[1][...]
        tok = jnp.mean(e * e, axis=-1, keepdims=True)
        a_refs[0][...] += 0.5 * jnp.sum(tok, axis=0, keepdims=True) * jnp.ones((1, LANES), F32)
        dx, dg = _rms_bwd_math(xv, e * (1.0 / D_MODEL), gv)
        o_refs[0][...] = dx
        o_refs[1][...] = dx.astype(BF16)
        a_refs[1][...] += dg

    return _mm(a, w, mode="nn", out_dtypes=(F32, BF16), extras=(res, target), bcast=(g,),
               accs=((1, LANES), (1, D_MODEL)), ref_epi=epilogue, name=name)


def _relu2(acc):
    r = jnp.maximum(acc, 0.0)
    return r * r, r


def _mlp_fwd(x, h, fetch, tag, finish):
    w_up = fetch(f"mlp{tag}_up", h)
    a, r = _mm(h, w_up, mode="nn", b_shard=True, out_dtypes=(BF16, BF16), epi=_relu2, name=f"mlp{tag}_up")
    w_down = fetch(f"mlp{tag}_down", a)
    return finish(a, w_down, x, f"mlp{tag}_down"), (h, a, r, w_up, w_down)


def _mlp_bwd(x, g, saved, dx, dx_bf, tag, after):
    h, a, r, w_up, w_down = saved
    d_down = _mm(a, dx_bf, mode="tn", out_dtypes=(BF16,), name=f"mlp{tag}_dwdown", after=after)
    dup = _mm(dx_bf, w_down, mode="nt", extras=(r,), out_dtypes=(BF16,),
              epi=lambda acc, rv: (acc * (2.0 * rv.astype(F32)),), name=f"mlp{tag}_dup")
    d_up = _mm(h, dup, mode="tn", o_shard=True, out_dtypes=(BF16,), name=f"mlp{tag}_dwup")
    dx_new, dx_new_bf, dg = _mm_norm_bwd(dup, w_up, x, dx, g, f"mlp{tag}_dh")
    return dx_new, dx_new_bf, dg, d_up, d_down


def _rope_tables(L, B):
    rows = L // GRID_W
    row = np.repeat(np.arange(rows, dtype=np.float32), GRID_W)
    col = np.tile(np.arange(GRID_W, dtype=np.float32), rows)
    inv = (ROPE_THETA ** (-np.arange(HEAD_DIM // 4, dtype=np.float32) / (HEAD_DIM // 4))).astype(np.float32)
    ar, ac = row[:, None] * inv, col[:, None] * inv
    cos = np.concatenate([np.cos(ar), np.cos(ar), np.cos(ac), np.cos(ac)], axis=-1)
    sin = np.concatenate([-np.sin(ar), np.sin(ar), -np.sin(ac), np.sin(ac)], axis=-1)
    return jnp.asarray(np.tile(cos, (B, 1)), F32), jnp.asarray(np.tile(sin, (B, 1)), F32)


def _swap_halves(x):
    lane = lax.broadcasted_iota(jnp.int32, x.shape, 1)
    return jnp.where((lane % 64) < 32, pltpu.roll(x, HEAD_DIM - 32, 1), pltpu.roll(x, 32, 1))


def _qk_prep(qkv, cos, sin, q_g, k_g):
    def fn(ins, bs, outs, accs):
        c, s = ins[1][...], ins[2][...]
        for h in range(N_HEADS + N_KV):
            xv = ins[0][:, h * HEAD_DIM:(h + 1) * HEAD_DIM]
            g = bs[0][...] if h < N_HEADS else bs[1][...]
            r = lax.rsqrt(jnp.mean(xv * xv, axis=-1, keepdims=True) + EPS)
            z = xv * r * g
            y = (z * c + _swap_halves(z) * s).astype(BF16)
            if h < N_HEADS:
                outs[0][:, h * HEAD_DIM:(h + 1) * HEAD_DIM] = y
            else:
                outs[1][:, (h - N_HEADS) * HEAD_DIM:(h - N_HEADS + 1) * HEAD_DIM] = y
        outs[2][...] = ins[0][:, (N_HEADS + N_KV) * HEAD_DIM:].astype(BF16)

    kvw = N_KV * HEAD_DIM
    return _rowwise(fn, [qkv, cos, sin], [q_g, k_g], [(D_MODEL, BF16), (kvw, BF16), (kvw, BF16)], tm=512,
                    name="attn_qk_prep")


def _qk_prep_bwd(qkv, dq, dk, dv, cos, sin, q_g, k_g):
    def fn(ins, bs, outs, accs):
        c, s = ins[4][...], ins[5][...]
        for h in range(N_HEADS + N_KV):
            sl = slice(h * HEAD_DIM, (h + 1) * HEAD_DIM)
            xv = ins[0][:, sl]
            if h < N_HEADS:
                g, dy, acc = bs[0][...], ins[1][:, sl], accs[0]
            else:
                ks = slice((h - N_HEADS) * HEAD_DIM, (h - N_HEADS + 1) * HEAD_DIM)
                g, dy, acc = bs[1][...], ins[2][:, ks], accs[1]
            r = lax.rsqrt(jnp.mean(xv * xv, axis=-1, keepdims=True) + EPS)
            xn = xv * r
            dz = dy * c - _swap_halves(dy) * s
            acc[...] += _rsum(dz * xn)
            dxn = dz * g
            outs[0][:, sl] = (r * (dxn - xn * jnp.mean(dxn * xn, axis=-1, keepdims=True))).astype(BF16)
        outs[0][:, (N_HEADS + N_KV) * HEAD_DIM:] = ins[3][...].astype(BF16)

    return _rowwise(fn, [qkv, dq, dk, dv, cos, sin], [q_g, k_g], [(qkv.shape[1], BF16)],
                    [(1, HEAD_DIM), (1, HEAD_DIM)], tm=256, name="attn_qk_prep_bwd")


_EXP2_SCALE = SCALE * math.log2(math.e)


def _exp_rows(q, k):
    s = lax.dot_general(q, k, _NT, preferred_element_type=F32)
    p = jnp.exp2((s - jnp.max(s, axis=-1, keepdims=True)) * _EXP2_SCALE)
    return p, jnp.sum(p, axis=-1, keepdims=True)


def _attn_fwd(q, k, v, B, L, tq=2048, sub=256):
    tq = min(tq, L)
    sub = min(sub, tq)
    nq = L // tq

    def body(q_ref, k_ref, v_ref, o_ref):
        kv, vv = k_ref[...], v_ref[...]
        for c in range(tq // sub):
            rows = slice(c * sub, (c + 1) * sub)
            p, l = _exp_rows(q_ref[rows, :], kv)
            o = jnp.dot(p.astype(BF16), vv, preferred_element_type=F32)
            o_ref[rows, :] = (o * (1.0 / l)).astype(o_ref.dtype)

    return pl.pallas_call(
        body, name="attn_fwd", grid=(B, N_HEADS, nq),
        in_specs=[pl.BlockSpec((tq, HEAD_DIM), lambda b, h, i: (b * nq + i, h)),
                  pl.BlockSpec((L, HEAD_DIM), lambda b, h, i: (b, h // GROUP)),
                  pl.BlockSpec((L, HEAD_DIM), lambda b, h, i: (b, h // GROUP))],
        out_specs=pl.BlockSpec((tq, HEAD_DIM), lambda b, h, i: (b * nq + i, h)),
        out_shape=jax.ShapeDtypeStruct((B * L, D_MODEL), BF16),
        compiler_params=_params(("parallel", "parallel", "parallel")),
    )(q, k, v)


def _attn_bwd(q, k, v, do, B, L, tq=2048, sub=512):
    tq = min(tq, L)
    sub = min(sub, tq)
    nq = L // tq

    def body(q_ref, k_ref, v_ref, do_ref, dq_ref, dk_ref, dv_ref):
        @pl.when((pl.program_id(2) == 0) & (pl.program_id(3) == 0))
        def _():
            dk_ref[...] = jnp.zeros(dk_ref.shape, F32)
            dv_ref[...] = jnp.zeros(dv_ref.shape, F32)

        kv, vv = k_ref[...], v_ref[...]
        ps, es, dos, qs = [], [], [], []
        for c in range(tq // sub):
            rows = slice(c * sub, (c + 1) * sub)
            qc, doc = q_ref[rows, :], do_ref[rows, :]
            p, l = _exp_rows(qc, kv)
            inv = 1.0 / l
            dp = lax.dot_general(doc, vv, _NT, preferred_element_type=F32)
            delta = jnp.sum(p * dp, axis=-1, keepdims=True) * inv
            e = (p * (dp - delta)).astype(BF16)
            dq_ref[rows, :] = jnp.dot(e, kv, preferred_element_type=F32) * (inv * SCALE)
            ps.append(p.astype(BF16))
            es.append(e)
            dos.append((doc.astype(F32) * inv).astype(BF16))
            qs.append((qc.astype(F32) * (inv * SCALE)).astype(BF16))
        cat = lambda xs: xs[0] if len(xs) == 1 else jnp.concatenate(xs, axis=0)
        dv_ref[...] += lax.dot_general(cat(ps), cat(dos), _TN, preferred_element_type=F32)
        dk_ref[...] += lax.dot_general(cat(es), cat(qs), _TN, preferred_element_type=F32)

    qmap = lambda b, kh, g, i: (b * nq + i, kh * GROUP + g)
    kmap = lambda b, kh, g, i: (b, kh)
    kvw = N_KV * HEAD_DIM
    return pl.pallas_call(
        body, name="attn_bwd", grid=(B, N_KV, GROUP, nq),
        in_specs=[pl.BlockSpec((tq, HEAD_DIM), qmap), pl.BlockSpec((L, HEAD_DIM), kmap),
                  pl.BlockSpec((L, HEAD_DIM), kmap), pl.BlockSpec((tq, HEAD_DIM), qmap)],
        out_specs=[pl.BlockSpec((tq, HEAD_DIM), qmap), pl.BlockSpec((L, HEAD_DIM), kmap),
                   pl.BlockSpec((L, HEAD_DIM), kmap)],
        out_shape=[jax.ShapeDtypeStruct((B * L, D_MODEL), F32), jax.ShapeDtypeStruct((B * L, kvw), F32),
                   jax.ShapeDtypeStruct((B * L, kvw), F32)],
        compiler_params=_params(("parallel", "parallel", "arbitrary", "arbitrary")),
    )(q, k, v, do)


def _conv_shift(x, t, L, k):
    if k == 2:
        return x
    if k < 2:
        return jnp.where(t >= 2 - k, pltpu.roll(x, 2 - k, 0), 0.0)
    return jnp.where(t < L - (k - 2), pltpu.roll(x, L - (k - 2), 0), 0.0)


def _conv_apply(x, w_ref, L):
    t = lax.broadcasted_iota(jnp.int32, x.shape, 0)
    acc = w_ref[4:5, :] + w_ref[2:3, :] * x
    for k in (0, 1, 3):
        acc = acc + w_ref[k:k + 1, :] * _conv_shift(x, t, L, k)
    return acc


def _conv_fwd(z, wb, B, L, tc=256):
    noff = D_MODEL // tc

    def body(z_ref, w_ref, o_ref):
        o_ref[...] = _conv_apply(z_ref[...], w_ref, L)

    return pl.pallas_call(
        body, name="rg_conv", grid=(B, noff),
        in_specs=[pl.BlockSpec((L, tc), lambda b, j: (b, noff + j)), pl.BlockSpec((SUBLANES, tc), lambda b, j: (0, j))],
        out_specs=pl.BlockSpec((L, tc), lambda b, j: (b, j)),
        out_shape=jax.ShapeDtypeStruct((B * L, D_MODEL), F32),
        compiler_params=_params(("parallel", "parallel")),
    )(z, wb)


def _conv_bwd(z, g, wb, dz, B, L, tc=256, after=None):
    noff = D_MODEL // tc
    order = _after_operand(after)

    def body(z_ref, g_ref, w_ref, dz_in, *rest):
        dx_ref, dw_ref = rest[len(order):]

        @pl.when(pl.program_id(1) == 0)
        def _():
            dw_ref[...] = jnp.zeros(dw_ref.shape, F32)

        x, gv = z_ref[...], g_ref[...]
        t = lax.broadcasted_iota(jnp.int32, x.shape, 0)
        dx = w_ref[2:3, :] * gv
        for k in (0, 1, 3):
            dx = dx + w_ref[k:k + 1, :] * _conv_shift(gv, t, L, 4 - k)
        dx_ref[...] = dx.astype(BF16)
        for k in range(4):
            dw_ref[k:k + 1, :] += _rsum(_conv_shift(x, t, L, k) * gv)
        dw_ref[4:5, :] += _rsum(gv)

    return pl.pallas_call(
        body, name="rg_conv_bwd", grid=(noff, B),
        in_specs=[pl.BlockSpec((L, tc), lambda j, b: (b, noff + j)), pl.BlockSpec((L, tc), lambda j, b: (b, j)),
                  pl.BlockSpec((SUBLANES, tc), lambda j, b: (0, j)), _ANY] + [_ANY] * len(order),
        out_specs=[pl.BlockSpec((L, tc), lambda j, b: (b, noff + j)),
                   pl.BlockSpec((SUBLANES, tc), lambda j, b: (0, j))],
        out_shape=[jax.ShapeDtypeStruct(dz.shape, dz.dtype), jax.ShapeDtypeStruct((SUBLANES, D_MODEL), F32)],
        input_output_aliases={3: 0},
        compiler_params=_params(("parallel", "arbitrary")),
    )(z, g, wb, dz, *order)


def _softplus(x):
    return jnp.maximum(x, 0.0) + jnp.log1p(jnp.exp(-jnp.abs(x)))


_ROW_BA, _ROW_BX, _ROW_LAM = 0, 2, 4


def _gate_math(xb, pre, vec_ref, d, sl):
    pa = pre[:, (2 * d) * LRU_BW:(2 * d + 1) * LRU_BW] + vec_ref[_ROW_BA + d:_ROW_BA + d + 1, sl]
    px = pre[:, (2 * d + 1) * LRU_BW:(2 * d + 2) * LRU_BW] + vec_ref[_ROW_BX + d:_ROW_BX + d + 1, sl]
    r = 0.5 * jnp.tanh(0.5 * pa) + 0.5
    i = 0.5 * jnp.tanh(0.5 * px) + 0.5
    sp = _softplus(-vec_ref[_ROW_LAM + d:_ROW_LAM + d + 1, sl])
    log_a = (-RG_C) * r * sp
    a = jnp.exp(log_a)
    th = jnp.tanh(log_a)
    om = -2.0 * th / (1.0 - th)
    mult = jnp.sqrt(om)
    return a, mult * (i * xb), (r, i, sp, om, mult)


def _gate_fwd(rec, wcat, gvec):
    def fn(ins, bs, outs, accs):
        for blk in range(LRU_BLOCKS):
            sl = slice(blk * LRU_BW, (blk + 1) * LRU_BW)
            xb = ins[0][:, sl]
            pre = jnp.dot(xb.astype(BF16), bs[0][sl, :], preferred_element_type=F32)
            for d in range(2):
                a, u, _ = _gate_math(xb, pre, bs[1], d, sl)
                outs[2 * d][:, sl] = a
                outs[2 * d + 1][:, sl] = u

    return _rowwise(fn, [rec], [wcat, gvec], [(D_MODEL, F32)] * 4, tm=256, name="rg_gate")


def _gate_bwd(rec, du_f, da_f, du_b, da_b, wcat, gvec):
    def fn(ins, bs, outs, accs):
        for blk in range(LRU_BLOCKS):
            sl = slice(blk * LRU_BW, (blk + 1) * LRU_BW)
            xb = ins[0][:, sl]
            xb16 = xb.astype(BF16)
            w = bs[0][sl, :]
            pre = jnp.dot(xb16, w, preferred_element_type=F32)
            dx = jnp.zeros_like(xb)
            dpre = []
            for d in range(2):
                a, _, (r, i, sp, om, mult) = _gate_math(xb, pre, bs[1], d, sl)
                du, da = ins[1 + 2 * d][:, sl], ins[2 + 2 * d][:, sl]
                d_i = du * mult * xb
                d_mult = du * i * xb
                dx = dx + du * mult * i
                dlog = da * a - d_mult * (1.0 - om) / mult
                d_r = dlog * ((-RG_C) * sp)
                d_sp = _rsum(dlog * ((-RG_C) * r))
                lam = bs[1][_ROW_LAM + d:_ROW_LAM + d + 1, sl]
                accs[2][_ROW_LAM + d:_ROW_LAM + d + 1, sl] += d_sp * (-jax.nn.sigmoid(-lam))
                dpa = d_r * r * (1.0 - r)
                dpx = d_i * i * (1.0 - i)
                accs[2][_ROW_BA + d:_ROW_BA + d + 1, sl] += _rsum(dpa)
                accs[2][_ROW_BX + d:_ROW_BX + d + 1, sl] += _rsum(dpx)
                dpre += [dpa, dpx]
            dpre = jnp.concatenate(dpre, axis=1).astype(BF16)
            dw = lax.dot_general(xb16, dpre, _TN, preferred_element_type=F32)
            for d in range(2):
                rows = slice(d * D_MODEL + blk * LRU_BW, d * D_MODEL + (blk + 1) * LRU_BW)
                accs[0][rows, :] += dw[:, (2 * d) * LRU_BW:(2 * d + 1) * LRU_BW]
                accs[1][rows, :] += dw[:, (2 * d + 1) * LRU_BW:(2 * d + 2) * LRU_BW]
            outs[0][:, sl] = dx + lax.dot_general(dpre, w, _NT, preferred_element_type=F32)

    gate_shape = (2 * D_MODEL, LRU_BW)
    return _rowwise(fn, [rec, du_f, da_f, du_b, da_b], [wcat, gvec], [(D_MODEL, F32)],
                    [gate_shape, gate_shape, (SUBLANES, D_MODEL)], tm=256, name="rg_gate_bwd")


def _as_time_blocks(x):
    return x.reshape(x.shape[0] // SUBLANES, SUBLANES, x.shape[1])


def _scan_call(body, ins, n_out, B, L, tc, name):
    nb = L // SUBLANES
    spec = pl.BlockSpec((nb, SUBLANES, tc), lambda b, j: (b, 0, j))
    T = ins[0].shape[0]
    outs = pl.pallas_call(
        functools.partial(body, nb), name=name, grid=(B, D_MODEL // tc),
        in_specs=[spec] * len(ins), out_specs=[spec] * n_out,
        out_shape=[jax.ShapeDtypeStruct((T // SUBLANES, SUBLANES, D_MODEL), F32)] * n_out,
        compiler_params=_params(("parallel", "parallel")),
    )(*[_as_time_blocks(x) for x in ins])
    return [o.reshape(T, D_MODEL) for o in outs]


def _block_scan(A, U, reverse):
    row = lax.broadcasted_iota(jnp.int32, A.shape, 0)
    for s in (1, 2, 4):
        shift = SUBLANES - s if reverse else s
        valid = (row < SUBLANES - s) if reverse else (row >= s)
        a_sh = jnp.where(valid, pltpu.roll(A, shift, 0), 1.0)
        u_sh = jnp.where(valid, pltpu.roll(U, shift, 0), 0.0)
        U = A * u_sh + U
        A = A * a_sh
    return A, U


_LAST = SUBLANES - 1
SCAN_UNROLL = 8


def _loop_blocks(nb, step, init):
    def group(g, carry):
        for k in range(SCAN_UNROLL):
            carry = step(g * SCAN_UNROLL + k, carry)
        return carry

    return lax.fori_loop(0, nb // SCAN_UNROLL, group, init)


def _scan_fwd(a_f, u_f, a_b, u_b, B, L, tc=256):
    def body(nb, af, uf, ab, ub, hf, hb):
        def step(i, carry):
            c1, c2 = carry
            ib = nb - 1 - i
            p, h = _block_scan(af[i], uf[i], False)
            h = h + p * c1
            hf[i] = h
            p2, h2 = _block_scan(ab[ib], ub[ib], True)
            h2 = h2 + p2 * c2
            hb[ib] = h2
            return h[_LAST:, :], h2[:1, :]

        zero = jnp.zeros((1, tc), F32)
        _loop_blocks(nb, step, (zero, zero))

    return _scan_call(body, [a_f, u_f, a_b, u_b], 2, B, L, tc, "rg_scan")


def _scan_bwd(dy, a_f, h_f, a_b, h_b, B, L, tc=256):
    def body(nb, dy_r, af, hf, ab, hb, duf, daf, dub, dab):
        def step(i, carry):
            c1, c2 = carry
            ir = nb - 1 - i
            row = lax.broadcasted_iota(jnp.int32, (SUBLANES, tc), 0)
            a_up = jnp.where(row == _LAST, af[jnp.minimum(ir + 1, nb - 1), :1, :], pltpu.roll(af[ir], _LAST, 0))
            p, lam = _block_scan(a_up, dy_r[ir], True)
            lam = lam + p * c1
            before = hf[jnp.maximum(ir - 1, 0), _LAST:, :] * (ir > 0).astype(F32)
            duf[ir] = lam
            daf[ir] = lam * jnp.where(row == 0, before, pltpu.roll(hf[ir], 1, 0))
            a_dn = jnp.where(row == 0, ab[jnp.maximum(i - 1, 0), _LAST:, :], pltpu.roll(ab[i], 1, 0))
            p2, lam2 = _block_scan(a_dn, dy_r[i], False)
            lam2 = lam2 + p2 * c2
            after = hb[jnp.minimum(i + 1, nb - 1), :1, :] * (i < nb - 1).astype(F32)
            dub[i] = lam2
            dab[i] = lam2 * jnp.where(row == _LAST, after, pltpu.roll(hb[i], _LAST, 0))
            return lam[:1, :], lam2[_LAST:, :]

        zero = jnp.zeros((1, tc), F32)
        _loop_blocks(nb, step, (zero, zero))

    return _scan_call(body, [dy, a_f, h_f, a_b, h_b], 4, B, L, tc, "rg_scan_bwd")


_GELU_C = math.sqrt(2.0 / math.pi)


def _gelu_parts(x):
    th = jnp.tanh(_GELU_C * (x + 0.044715 * x * x * x))
    return 0.5 * x * (1.0 + th), th


def _gated_out(h_f, h_b, z):
    def fn(ins, bs, outs, accs):
        gl, _ = _gelu_parts(ins[2][...])
        outs[0][...] = ((ins[0][...] + ins[1][...]) * gl).astype(BF16)

    return _rowwise(fn, [h_f, h_b, (z, D_MODEL, 0)], [], [(D_MODEL, BF16)], tm=512, name="rg_gated_out")[0]


def _mm_gated_out_bwd(dx, w_out, h_f, h_b, z, name, after=None):
    def epilogue(acc, e_refs, b_refs, o_refs, a_refs):
        x = e_refs[2][...]
        gl, th = _gelu_parts(x)
        dgl = 0.5 * (1.0 + th) + 0.5 * x * (1.0 - th * th) * (_GELU_C * (1.0 + 3.0 * 0.044715 * x * x))
        o_refs[0][...] = acc * gl
        o_refs[1][...] = (acc * (e_refs[0][...] + e_refs[1][...]) * dgl).astype(BF16)

    return _mm(dx, w_out, mode="nt", out_dtypes=(F32, BF16), out_cols=(D_MODEL, 2 * D_MODEL), extras=(h_f, h_b, z),
               ref_epi=epilogue, name=name, after=after)


def _row_block(i):
    return pl.ds(pl.multiple_of(i * SUBLANES, SUBLANES), SUBLANES)


def _rg_mix_fwd(z, conv_wb, wcat, gvec, B, L):
    nb = L // SUBLANES
    n_g = D_MODEL // LRU_BW

    def body(zg_ref, zr_ref, cw_ref, w_ref, gv_ref, rec_ref, af_s, ab_s, hf_ref, hb_ref, yg_ref, uf_s, ub_s):
        rec = _conv_apply(zr_ref[...], cw_ref, L)
        rec_ref[...] = rec
        pre = jnp.dot(rec.astype(BF16), w_ref[...], preferred_element_type=F32)
        for d, (a_s, u_s) in enumerate(((af_s, uf_s), (ab_s, ub_s))):
            a, u, _ = _gate_math(rec, pre, gv_ref, d, slice(None))
            a_s[...] = a
            u_s[...] = u

        def step(i, carry):
            c1, c2 = carry
            rows, rows_b = _row_block(i), _row_block(nb - 1 - i)
            p, h = _block_scan(af_s[rows, :], uf_s[rows, :], False)
            h = h + p * c1
            hf_ref[rows, :] = h
            p2, h2 = _block_scan(ab_s[rows_b, :], ub_s[rows_b, :], True)
            h2 = h2 + p2 * c2
            hb_ref[rows_b, :] = h2
            return h[_LAST:, :], h2[:1, :]

        zero = jnp.zeros((1, LRU_BW), F32)
        _loop_blocks(nb, step, (zero, zero))
        gl, _ = _gelu_parts(zg_ref[...])
        yg_ref[...] = ((hf_ref[...] + hb_ref[...]) * gl).astype(BF16)

    seq = lambda off: pl.BlockSpec((L, LRU_BW), lambda b, g: (b, off + g))
    vec = pl.BlockSpec((SUBLANES, LRU_BW), lambda b, g: (0, g))
    T = B * L
    return pl.pallas_call(
        body, name="rg_mix", grid=(B, n_g),
        in_specs=[seq(0), seq(n_g), vec, pl.BlockSpec((LRU_BW, 4 * LRU_BW), lambda b, g: (g, 0)), vec],
        out_specs=[seq(0)] * 6,
        out_shape=[jax.ShapeDtypeStruct((T, D_MODEL), F32)] * 5 + [jax.ShapeDtypeStruct((T, D_MODEL), BF16)],
        scratch_shapes=[pltpu.VMEM((L, LRU_BW), F32)] * 2,
        compiler_params=_params(("parallel", "parallel")),
    )(z, z, conv_wb, wcat, gvec)


def _make_wcat(w_a, w_x):
    g = jnp.stack([w_a[0, 0], w_x[0, 0], w_a[0, 1], w_x[0, 1]])
    return jnp.transpose(g, (1, 2, 0, 3)).reshape(D_MODEL, 4 * LRU_BW)


def _rows_at(part, first):
    return jnp.pad(part, ((first, SUBLANES - first - part.shape[0]), (0, 0)))


def _qk_slot(q_g, k_g):
    wide = lambda v, at: jnp.pad(v, ((0, SUBLANES - 1), (at, D_MODEL - at - HEAD_DIM)))
    return wide(q_g, 0) + wide(k_g, HEAD_DIM)


def _local_step(x, target, P, fetch, emit, B, L, after=None):
    g_mix, g_mlp = P["norm_mix_g"], P["norm_mlp_g"]
    h0 = _rms_fwd(x, g_mix[0:1], "rg_norm", after=after)
    w_in, w_out, conv_wb, wcat, gvec = fetch("rg", h0)
    z = _mm(h0, w_in, mode="nn", b_shard=True, name="rg_in")
    rec, a_f, a_b, h_f, h_b, yg = _rg_mix_fwd(z, conv_wb, wcat, gvec, B, L)
    x1, h1 = _mm_res_norm(yg, w_out, x, g_mlp[0:1], "rg_out")
    (x2, h3), mlp0 = _mlp_fwd(x1, h1, fetch, 0, lambda a, w, res, name: _mm_res_norm(a, w, res, g_mix[1:2], name))
    w_qkv, w_o = fetch("att", h3)
    qkv = _mm(h3, w_qkv, mode="nn", b_shard=True, name="attn_qkv")
    cos, sin = _rope_tables(L, B)
    qh, kh, vh = _qk_prep(qkv, cos, sin, P["q_g"], P["k_g"])
    o = _attn_fwd(qh, kh, vh, B, L)
    x3, h4 = _mm_res_norm(o, w_o, x2, g_mlp[1:2], "attn_out")
    (dx4, dx4_bf, loss_acc, d_final_g), mlp1 = _mlp_fwd(
        x3, h4, fetch, 1, lambda a, w, res, name: _mm_final_loss(a, w, res, target, P["final_g"], name))

    dx3, dx3_bf, dg_mlp1, d_up1, d_down1 = _mlp_bwd(x3, g_mlp[1:2], mlp1, dx4, dx4_bf, 1, None)
    tok = emit("mlp1", [d_up1, d_down1])
    d_wo = _mm(o, dx3_bf, mode="tn", out_dtypes=(BF16,), name="attn_dwo", after=tok)
    do = _mm(dx3_bf, w_o, mode="nt", out_dtypes=(BF16,), name="attn_do")
    dq, dk, dv = _attn_bwd(qh, kh, vh, do, B, L)
    dqkv, dq_g, dk_g = _qk_prep_bwd(qkv, dq, dk, dv, cos, sin, P["q_g"], P["k_g"])
    d_wqkv = _mm(h3, dqkv, mode="tn", o_shard=True, out_dtypes=(BF16,), name="attn_dwqkv")
    tok = emit("att", [d_wqkv, d_wo])
    dx2, dx2_bf, dg_mix1 = _mm_norm_bwd(dqkv, w_qkv, x2, dx3, g_mix[1:2], "attn_dh", after=tok)
    tok = emit("point_attn_done", [dx2_bf])
    dx1, dx1_bf, dg_mlp0, d_up0, d_down0 = _mlp_bwd(x1, g_mlp[0:1], mlp0, dx2, dx2_bf, 0, tok)
    d_wout = _mm(yg, dx1_bf, mode="tn", out_dtypes=(BF16,), name="rg_dwout")
    tok = emit("mlp0", [d_up0, d_down0, d_wout])
    dy, dgate = _mm_gated_out_bwd(dx1_bf, w_out, h_f, h_b, z, "rg_dyg", after=tok)
    du_f, da_f, du_b, da_b = _scan_bwd(dy, a_f, h_f, a_b, h_b, B, L)
    drec_c, d_wa, d_wx, d_gvec = _gate_bwd(rec, du_f, da_f, du_b, da_b, wcat, gvec)
    tok = emit("gates", [d_wa, d_wx])
    dz, d_convwb = _conv_bwd(z, drec_c, conv_wb, dgate, B, L, after=tok)
    tok = emit("point_mix_done", [dz])
    d_win = _mm(h0, dz, mode="tn", o_shard=True, out_dtypes=(BF16,), name="rg_dwin", after=tok)
    tok = emit("rg_in", [d_win])
    grad_x, _, dg_mix0 = _mm_norm_bwd(dz, w_in, x, dx1, g_mix[0:1], "rg_dh", after=tok)

    norms = (_rows_at(dg_mix0, 0) + _rows_at(dg_mix1, 1) + _rows_at(dg_mlp0, 2) + _rows_at(dg_mlp1, 3)
             + _rows_at(d_final_g, 4)
             + jnp.pad(loss_acc, ((LOSS_ROW, SUBLANES - 1 - LOSS_ROW), (0, D_MODEL - LANES))))
    vec = jnp.concatenate([norms, d_convwb, d_gvec, _qk_slot(dq_g, dk_g)], axis=0)
    return grad_x, vec


_MESH = pl.DeviceIdType.MESH


def _place():
    x, y, c = lax.axis_index("x"), lax.axis_index("y"), lax.axis_index("c")
    peers = [((1 - x) if j & 2 else x, (1 - y) if j & 1 else y) for j in (1, 2, 3)]
    return x, y, c, peers


def _comm_call(body, ins, out_shapes, n_sem, name):
    return pl.pallas_call(
        body, name=name, in_specs=[_ANY] * len(ins), out_specs=[_ANY] * len(out_shapes), out_shape=out_shapes,
        scratch_shapes=[pltpu.SemaphoreType.DMA((n_sem,)), pltpu.SemaphoreType.DMA((n_sem,)),
                        pltpu.SemaphoreType.DMA((len(ins),))],
    )(*ins)


def _all_devices_slots(v, name):
    def body(v_ref, out_ref, send, recv, lsem):
        x, y, c = lax.axis_index("x"), lax.axis_index("y"), lax.axis_index("c")
        me = 4 * x + 2 * y + c

        def peer(j):
            return (1 - x) if j & 4 else x, (1 - y) if j & 2 else y, (1 - c) if j & 1 else c

        def copy(j, slot):
            return pltpu.make_async_remote_copy(
                src_ref=v_ref, dst_ref=out_ref.at[slot], send_sem=send.at[j - 1], recv_sem=recv.at[j - 1],
                device_id=peer(j), device_id_type=_MESH)

        local = pltpu.make_async_copy(v_ref, out_ref.at[me], lsem.at[0])
        sends = [copy(j, me) for j in range(1, N_DEVICES)]
        for cp in [local] + sends:
            cp.start()
        for j in range(1, N_DEVICES):
            px, py, pc = peer(j)
            copy(j, 4 * px + 2 * py + pc).wait_recv()
        for cp in sends:
            cp.wait_send()
        local.wait()

    shape = jax.ShapeDtypeStruct((N_DEVICES,) + v.shape, v.dtype)
    return _comm_call(body, [v], [shape], N_DEVICES - 1, name)[0]


def _sum_leading(slots, name):
    def body(s_ref, o_ref):
        acc = s_ref[0]
        for d in range(1, slots.shape[0]):
            acc = acc + s_ref[d]
        o_ref[...] = acc

    return pl.pallas_call(body, name=name, out_shape=jax.ShapeDtypeStruct(slots.shape[1:], slots.dtype))(slots)


_HBM = pl.BlockSpec(memory_space=pltpu.HBM)
_SEM = pl.BlockSpec(memory_space=pltpu.SEMAPHORE)
_EFFECT = pltpu.SideEffectType.DATAFLOW_SIDE_EFFECTING


_COPIES = dict(gather=N_CHIPS - 1, scatter=N_CHIPS - 1, swap=1)


def _split_copies(kind, srcs, lands, send, recv):
    x, y, c, peers = _place()
    me = 2 * x + y
    per = _COPIES[kind]
    out = []
    for a in range(len(lands)):
        for j in range(per):
            if kind == "swap":
                src, there, here, dev = srcs[a], lands[a], lands[a], (x, y, 1 - c)
            else:
                px, py = peers[j]
                dev = (px, py, c)
                if kind == "gather":
                    src, there, here = lands[a].at[me], lands[a].at[me], lands[a].at[2 * px + py]
                else:
                    src, there, here = srcs[a].at[2 * px + py], lands[a].at[j], lands[a].at[j]
            mk = functools.partial(
                pltpu.make_async_remote_copy, src_ref=src, send_sem=send.at[per * a + j],
                recv_sem=recv.at[per * a + j], device_id=dev, device_id_type=_MESH)
            out.append((functools.partial(mk, dst_ref=there), functools.partial(mk, dst_ref=here)))
    return out


def _exchange_start(kind, srcs, lands, name, after=None):
    arrays = list(srcs) + list(lands)
    n_s, n, n_all = len(srcs), len(lands), len(srcs) + len(lands)
    n_sem = _COPIES[kind] * n
    order = _after_operand(after)
    n_x = len(order)

    def body(*refs):
        send, recv = refs[n_all + n_x], refs[n_all + n_x + 1]
        token = refs[-1]
        for started, _ in _split_copies(kind, refs[:n_s], refs[n_s:n_all], send, recv):
            started().start()
        token[...] = jnp.zeros(token.shape, F32)

    res = pl.pallas_call(
        body, name=name,
        out_shape=(pltpu.SemaphoreType.DMA((n_sem,)), pltpu.SemaphoreType.DMA((n_sem,)),
                   *[pltpu.HBM(a.shape, a.dtype) for a in arrays], jax.ShapeDtypeStruct((SUBLANES, LANES), F32)),
        in_specs=[_HBM] * n_all + [_ANY] * n_x,
        out_specs=(_SEM, _SEM, *[_HBM] * n_all, pl.BlockSpec(memory_space=pltpu.VMEM)),
        input_output_aliases={i: 2 + i for i in range(n_all)},
        compiler_params=pltpu.CompilerParams(has_side_effects=_EFFECT),
    )(*[pltpu.with_memory_space_constraint(a, pltpu.HBM) for a in arrays], *order)
    return (res[0], res[1], res[2:2 + n_s], res[2 + n_s:2 + n_all]), res[-1]


def _exchange_wait(kind, handle, after, name):
    send, recv, srcs, lands = handle
    arrays = list(srcs) + list(lands)
    n_s, n_all = len(srcs), len(arrays)
    order = list(after) if isinstance(after, (list, tuple)) else [after]

    def body(*refs):
        for started, landing in _split_copies(kind, refs[:n_s], refs[n_s:n_all], refs[n_all], refs[n_all + 1]):
            started().wait_send()
            landing().wait_recv()

    res = pl.pallas_call(
        body, name=name, out_shape=[pltpu.HBM(a.shape, a.dtype) for a in arrays],
        in_specs=[_HBM] * n_all + [_SEM, _SEM] + [_ANY] * len(order), out_specs=[_HBM] * n_all,
        input_output_aliases={i: i for i in range(n_all)},
        compiler_params=pltpu.CompilerParams(has_side_effects=_EFFECT),
    )(*arrays, send, recv, *order)
    return res[:n_s], res[n_s:]


def _index_operand(i):
    return jnp.reshape(i, (1,)).astype(jnp.int32)


def _cast_into_slot(src, row0, rows, me, dtype, name, after=None, add=None):
    cols = src.shape[1]
    tm = min(512, rows)
    order = _after_operand(after)
    terms = [src] + ([] if add is None else [add])

    def body(me_ref, *rest):
        val = rest[0][...]
        if add is not None:
            val = val + rest[1][...]
        rest[-1][...] = val.astype(dtype)

    return pl.pallas_call(
        body, name=name,
        grid_spec=pltpu.PrefetchScalarGridSpec(
            num_scalar_prefetch=1, grid=(rows // tm,),
            in_specs=[pl.BlockSpec((tm, cols), lambda i, me_ref: (i + row0 // tm, 0))] * len(terms)
            + [_ANY] * len(order),
            out_specs=pl.BlockSpec((None, tm, cols), lambda i, me_ref: (me_ref[0], i, 0))),
        out_shape=jax.ShapeDtypeStruct((N_CHIPS, rows, cols), dtype), compiler_params=_params(("parallel",)),
    )(_index_operand(me), *terms, *order)


def _sum_slots(mine, r, me, name):
    _, rows, cols = r.shape
    tm = min(512, rows)

    def body(me_ref, own_ref, r_ref, o_ref):
        o_ref[...] = ((own_ref[...].astype(F32) + r_ref[0].astype(F32)) + r_ref[1].astype(F32)) + r_ref[2].astype(F32)

    return pl.pallas_call(
        body, name=name,
        grid_spec=pltpu.PrefetchScalarGridSpec(
            num_scalar_prefetch=1, grid=(rows // tm,),
            in_specs=[pl.BlockSpec((None, tm, cols), lambda i, me_ref: (me_ref[0], i, 0)),
                      pl.BlockSpec((N_CHIPS - 1, tm, cols), lambda i, me_ref: (0, i, 0))],
            out_specs=pl.BlockSpec((tm, cols), lambda i, me_ref: (i, 0))),
        out_shape=jax.ShapeDtypeStruct((rows, cols), F32), compiler_params=_params(("parallel",)),
    )(_index_operand(me), mine, r)


def _adamw(w, m, v, ps, qs, name):
    rows, cols = w.shape
    seg_rows = ps[0].shape[0]
    tm = min(256, seg_rows)
    while seg_rows % tm:
        tm -= SUBLANES
    per, n_seg = seg_rows // tm, len(ps)
    parts = list(ps) + ([] if qs is None else list(qs))

    def body(w_ref, m_ref, v_ref, *rest):
        g_refs, outs = rest[:len(parts)], rest[len(parts):]
        grad = lambda s: g_refs[s][...] if qs is None else g_refs[s][...] + g_refs[n_seg + s][...]
        g = grad(0)
        for s in range(1, n_seg):
            g = jnp.where(pl.program_id(0) >= s * per, grad(s), g)
        m1 = ADAM_B1 * m_ref[...] + (1.0 - ADAM_B1) * g
        v1 = ADAM_B2 * v_ref[...] + (1.0 - ADAM_B2) * (g * g)
        m_hat = m1 / (1.0 - ADAM_B1 ** ADAM_STEP)
        v_hat = v1 / (1.0 - ADAM_B2 ** ADAM_STEP)
        outs[0][...] = g
        outs[1][...] = (-ADAM_LR) * (m_hat / (jnp.sqrt(v_hat) + ADAM_EPS) + ADAM_WD * w_ref[...])
        outs[2][...] = m1
        outs[3][...] = v1

    row_spec = pl.BlockSpec((tm, cols), lambda i: (i, 0))
    seg_spec = lambda s: pl.BlockSpec((tm, cols), lambda i: (jnp.clip(i - s * per, 0, per - 1), 0))
    return pl.pallas_call(
        body, name=name, grid=(rows // tm,),
        in_specs=[row_spec] * 3 + [seg_spec(s) for s in range(n_seg)] * (1 if qs is None else 2),
        out_specs=[row_spec] * 4, out_shape=[jax.ShapeDtypeStruct((rows, cols), F32)] * 4,
        compiler_params=_params(("arbitrary",)),
    )(w, m, v, *parts)


def _put_cols(shard, me):
    full = jnp.zeros((shard.shape[0], D_MODEL), F32)
    return lax.dynamic_update_slice(full, shard, (0, me * (D_MODEL // N_CHIPS)))


def _gate_vec_slot(b_a, b_x, lam):
    return _rows_at(b_a, _ROW_BA) + _rows_at(b_x, _ROW_BX) + _rows_at(lam, _ROW_LAM)


def _pack_vec(p, me):
    return jnp.concatenate([
        _rows_at(p["norm_mix_g"], 0) + _rows_at(p["norm_mlp_g"], 2) + _rows_at(p["final_g"][None], 4),
        _rows_at(_put_cols(p["rg_conv_w"][0, :, 0, :], me), 0) + _rows_at(p["rg_conv_b"], 4),
        _gate_vec_slot(_put_cols(p["rg_b_a"][0], me), _put_cols(p["rg_b_x"][0], me), _put_cols(p["rg_lam"][0], me)),
        _qk_slot(p["at_q_g"], p["at_k_g"]),
    ], axis=0)


def _unpack_vec(r, me):
    def cols(rows):
        return lax.dynamic_slice(rows, (0, me * (D_MODEL // N_CHIPS)), (rows.shape[0], D_MODEL // N_CHIPS))

    gate = r[16:24]
    return dict(
        norm_mix_g=r[0:2], norm_mlp_g=r[2:4], final_g=r[4], rg_conv_w=cols(r[8:12])[None, :, None, :],
        rg_conv_b=r[12:13], rg_b_a=cols(gate[_ROW_BA:_ROW_BA + 2])[None], rg_b_x=cols(gate[_ROW_BX:_ROW_BX + 2])[None],
        rg_lam=cols(gate[_ROW_LAM:_ROW_LAM + 2])[None], at_q_g=r[24:25, 0:HEAD_DIM],
        at_k_g=r[24:25, HEAD_DIM:2 * HEAD_DIM])


_WEIGHTS = ['norm_mix_g', 'norm_mlp_g', 'rg_w_in', 'rg_conv_w', 'rg_conv_b', 'rg_w_a', 'rg_b_a', 'rg_w_x', 'rg_b_x',
            'rg_lam', 'rg_w_out', 'at_w_qkv', 'at_q_g', 'at_k_g', 'at_w_o', 'mlp_w_up', 'mlp_w_down', 'final_g']
_BIG = dict(rg_w_in=["rg_w_in"], rg_w_out=["rg_w_out"], at_w_qkv=["at_w_qkv"], at_w_o=["at_w_o"],
            mlp_w_up=["up0", "up1"], mlp_w_down=["down0", "down1"])


def kernel(x, *args):
    n_w = len(_WEIGHTS)
    w = dict(zip(_WEIGHTS, args[:n_w]))
    target = args[n_w]
    m = dict(zip(_WEIGHTS, args[n_w + 1:2 * n_w + 1]))
    v = dict(zip(_WEIGHTS, args[2 * n_w + 1:3 * n_w + 1]))
    B, L, _ = x.shape
    T = B * L
    me = 2 * lax.axis_index("x") + lax.axis_index("y")

    vec = jnp.concatenate([_gate_vec_slot(w["rg_b_a"][0], w["rg_b_x"][0], w["rg_lam"][0]),
                           _rows_at(w["rg_conv_w"][0, :, 0, :], 0)], axis=0)
    flat = lambda a: a.reshape(-1, a.shape[-1])
    rows_of = lambda k: w[k].shape[-2]
    groups = [("rg", [("rg_w_in", 0, BF16), ("rg_w_out", 0, BF16), (vec, 0, F32)]),
              ("mlp0_up", [("mlp_w_up", 0, BF16)]), ("mlp0_down", [("mlp_w_down", 0, BF16)]),
              ("att", [("at_w_qkv", 0, BF16), ("at_w_o", 0, BF16)]),
              ("mlp1", [("mlp_w_up", 1, BF16), ("mlp_w_down", 1, BF16)])]
    gathers, tok = {}, None
    for group, members in groups:
        lands = []
        for n, (k, layer, dtype) in enumerate(members):
            src, rows = (flat(w[k]), rows_of(k)) if isinstance(k, str) else (k, k.shape[0])
            lands.append(_cast_into_slot(src, layer * rows, rows, me, dtype, f"place_{group}{n}", after=tok))
        gathers[group], tok = _exchange_start("gather", [], lands, f"gather_{group}_start", after=tok)
    wcat = _make_wcat(w["rg_w_a"], w["rg_w_x"]).astype(BF16)

    packs = [_pack_vec(p, me) for p in (w, m, v)]

    ready = {}

    def fetch(what, after):
        if what in ready:
            return ready[what]
        group = "mlp1" if what.startswith("mlp1") else what
        order = [after, wcat] + packs if group == "rg" else after
        _, full = _exchange_wait("gather", gathers[group], order, f"gather_{group}_wait")
        if group == "rg":
            vec_full = jnp.transpose(full[2], (1, 0, 2)).reshape(2 * SUBLANES, D_MODEL)
            conv_wb = vec_full[SUBLANES:] + _rows_at(w["rg_conv_b"], 4)
            return full[0], full[1].reshape(D_MODEL, D_MODEL), conv_wb, wcat, vec_full[:SUBLANES]
        if group == "att":
            return full[0], full[1].reshape(D_MODEL, D_MODEL)
        if group == "mlp1":
            ready["mlp1_up"], ready["mlp1_down"] = full[0], full[1].reshape(4 * D_MODEL, D_MODEL)
            return ready[what]
        return full[0] if group == "mlp0_up" else full[0].reshape(4 * D_MODEL, D_MODEL)

    names = dict(mlp1=["up1", "down1"], att=["at_w_qkv", "at_w_o"], mlp0=["up0", "down0", "rg_w_out"],
                 rg_in=["rg_w_in"], gates=["rg_w_a", "rg_w_x"])
    scatters, swaps, P, Q, res = {}, [], {}, {}, {}

    def start_scatter(group, grads):
        srcs = [g.reshape(N_CHIPS, -1, g.shape[-1]) for g in grads]
        lands = [lax.empty((N_CHIPS - 1,) + s.shape[1:], s.dtype) for s in srcs]
        scatters[group], token = _exchange_start("scatter", srcs, lands, f"scatter_{group}_start")
        return token

    def settle(groups, after):
        keys, parts = [], []
        for group in groups:
            srcs, lands = _exchange_wait("scatter", scatters[group], after, f"scatter_{group}_wait")
            for k, s, r in zip(names[group], srcs, lands):
                keys.append(k)
                parts.append(_sum_slots(s, r, me, f"sum_{k}"))
        handle, token = _exchange_start("swap", parts, [lax.empty(p.shape, F32) for p in parts],
                                        f"swap_{groups[0]}_start")
        swaps.append((keys, handle, f"swap_{groups[0]}_wait"))
        return token

    def finish(after):
        for keys, handle, name in swaps:
            mine, theirs = _exchange_wait("swap", handle, after, name)
            P.update(zip(keys, mine))
            Q.update(zip(keys, theirs))
        swaps.clear()
        last = after
        for k, parts in _BIG.items():
            if k in res or any(p not in P for p in parts):
                continue
            shape = w[k].shape
            two_d = lambda a: a.reshape(-1, shape[-1])
            outs = _adamw(two_d(w[k]), two_d(m[k]), two_d(v[k]), [P[p] for p in parts], [Q[p] for p in parts],
                          f"adamw_{k}")
            res[k] = [o.reshape(shape) for o in outs]
            last = outs[0]
        if "rg_w_a" in P and "gates" not in gathers:
            lands = [_cast_into_slot(P[k], 0, P[k].shape[0], me, F32, f"place_{k}", after=last, add=Q[k])
                     for k in names["gates"]]
            gathers["gates"], last = _exchange_start("gather", [], lands, "gather_gates_start", after=last)
        return last

    def emit(event, arrays):
        if event == "point_attn_done":
            return settle(["mlp1"], arrays[0])
        if event == "point_mix_done":
            return settle(["att", "mlp0"], arrays[0])
        token = start_scatter(event, arrays)
        if event == "rg_in":
            return finish(settle(["gates"], token))
        return token

    P_vec = dict(norm_mix_g=w["norm_mix_g"], norm_mlp_g=w["norm_mlp_g"], final_g=w["final_g"][None],
                 q_g=w["at_q_g"], k_g=w["at_k_g"])
    grad_x, vec_part = _local_step(x.reshape(T, D_MODEL), target.reshape(T, D_MODEL), P_vec, fetch, emit, B, L,
                                   after=tok)

    finish(settle(["rg_in"], grad_x))
    _, gate_grads = _exchange_wait("gather", gathers["gates"], grad_x, "gather_gates_wait")
    for k, g in zip(names["gates"], gate_grads):
        two_d = lambda a: a.reshape(g.shape[0] * g.shape[1], g.shape[2])
        outs = _adamw(two_d(w[k]), two_d(m[k]), two_d(v[k]), [two_d(g)], None, f"adamw_{k}")
        res[k] = [o.reshape(w[k].shape) for o in outs]
    vec_grad = _sum_leading(_all_devices_slots(vec_part, "allreduce_vec"), "sum_vec")
    loss = vec_grad[LOSS_ROW, 0]
    outs = _adamw(*packs, [vec_grad], None, "adamw_vec")
    unpacked = [_unpack_vec(o, me) for o in outs]
    for k in _WEIGHTS:
        if k not in res:
            res[k] = [u[k] for u in unpacked]

    result = [loss, grad_x.reshape(B, L, D_MODEL)]
    for slot in range(4):
        result += [res[k][slot] for k in _WEIGHTS]
    return tuple(result)
```

```python
import functools
import math

import jax
import jax.numpy as jnp
import numpy as np
from jax import lax
from jax.experimental import pallas as pl
from jax.experimental.pallas import tpu as pltpu

F32 = jnp.float32
BF16 = jnp.bfloat16

D_MODEL = 1024
HEAD_DIM = 128
N_HEADS = 8
N_KV = 2
GROUP = N_HEADS // N_KV
LRU_BLOCKS = 8
LRU_BW = 128
GRID_W = 64
ROPE_THETA = 10000.0
EPS = 1e-6
RG_C = 8.0
SCALE = 1.0 / math.sqrt(HEAD_DIM)
N_CHIPS = 4

ADAM_LR = 0.001
ADAM_B1 = 0.9
ADAM_B2 = 0.999
ADAM_EPS = 1e-08
ADAM_WD = 0.01
ADAM_STEP = 10

V7X_VMEM_BYTES = 64 * 1024 * 1024
VMEM_LIMIT = V7X_VMEM_BYTES * 3 // 4
LANES = 128
SUBLANES = 8

N_DEVICES = 8
VEC_ROWS = 32
LOSS_ROW = 5


def _params(sem):
    return pltpu.CompilerParams(dimension_semantics=sem, vmem_limit_bytes=VMEM_LIMIT)


_ANY = pl.BlockSpec(memory_space=pl.ANY)
_NN = (((1,), (0,)), ((), ()))
_NT = (((1,), (1,)), ((), ()))
_TN = (((0,), (0,)), ((), ()))


def _after_operand(after):
    return [] if after is None else [after]


def _fit(t, n):
    if n <= t:
        return n
    c = (t // LANES) * LANES
    while n % c:
        c -= LANES
    return c


MM_VMEM_BUDGET = VMEM_LIMIT * 3 // 4
def _mm_tiles(M, K, ns, n_total, out_dtypes, extras, whole_rows):
    for tm in (2048, 1024, 512, 256, 128):
        for tn in ((ns,) if whole_rows else (1024, 512, 256)):
            tn = _fit(tn, ns)
            per_row = 2 * (2 * K) + 4 * tn + sum(2 * tn * jnp.dtype(d).itemsize for d in out_dtypes)
            per_row += sum(2 * tn * e.dtype.itemsize for e in extras)
            b_buffers = 1 if tn == n_total else 2
            if M % tm == 0 and b_buffers * (2 * K * tn) + tm * per_row <= MM_VMEM_BUDGET:
                return tm, tn
    raise ValueError(f"no tile fits VMEM for M={M} K={K} N={ns}")


def _mm(a, b, *, mode, name, out_dtypes=(F32,), b_shard=False, o_shard=False, extras=(), epi=None, after=None,
        bcast=(), accs=(), ref_epi=None, out_cols=None):
    if mode == "tn":
        K, M = a.shape
        N = b.shape[1]
    else:
        M, K = a.shape
        if mode == "nn":
            N = b.shape[0] * b.shape[2] if b_shard else b.shape[1]
        else:
            N = b.shape[1] if b_shard else b.shape[0]
    ns = N
    if b_shard and mode == "nn":
        ns = b.shape[2]
    elif o_shard:
        ns = N // N_CHIPS
    tm, tn = _mm_tiles(M, K, ns, N, out_dtypes, extras, whole_rows=ref_epi is not None)
    if ref_epi is not None:
        tm = min(tm, 512)
    grid = (M // tm, N // tn)
    q = ns // tn
    once = dict(pipeline_mode=pl.Buffered(1)) if tn == N else {}

    if mode == "tn":
        a_spec = pl.BlockSpec((K, tm), lambda i, j: (0, i))
        b_spec = pl.BlockSpec((K, tn), lambda i, j: (0, j), **once)
        dims = _TN
    elif mode == "nn":
        a_spec = pl.BlockSpec((tm, K), lambda i, j: (i, 0))
        if b_shard:
            b_spec = pl.BlockSpec((None, K, tn), lambda i, j: (j // q, 0, j % q), **once)
        else:
            b_spec = pl.BlockSpec((K, tn), lambda i, j: (0, j), **once)
        dims = _NN
    else:
        a_spec = pl.BlockSpec((tm, K), lambda i, j: (i, 0))
        if b_shard:
            ks = b.shape[2]
            b_spec = pl.BlockSpec((N_CHIPS, tn, ks), lambda i, j: (0, j, 0), **once)
        else:
            b_spec = pl.BlockSpec((tn, K), lambda i, j: (j, 0), **once)
        dims = _NT

    if o_shard:
        o_specs = [pl.BlockSpec((None, tm, tn), lambda i, j: (j // q, i, j % q))]
        o_shapes = [jax.ShapeDtypeStruct((N_CHIPS, M, ns), out_dtypes[0])]
    else:
        o_specs = [pl.BlockSpec((tm, tn), lambda i, j: (i, j)) for _ in out_dtypes]
        o_shapes = [jax.ShapeDtypeStruct((M, N if out_cols is None else out_cols[n]), dt)
                    for n, dt in enumerate(out_dtypes)]
    e_specs = [pl.BlockSpec((tm, tn), lambda i, j: (i, j)) for _ in extras]
    e_specs += [pl.BlockSpec(v.shape, lambda i, j: (0, 0)) for v in bcast]
    o_specs += [pl.BlockSpec(s, lambda i, j: (0, 0)) for s in accs]
    o_shapes += [jax.ShapeDtypeStruct(s, F32) for s in accs]
    n_e, n_b, n_o, n_a = len(extras), len(bcast), len(out_dtypes), len(accs)
    order = _after_operand(after)
    n_x = len(order)
    if epi is None:
        epi = lambda acc: (acc,)

    def body(a_ref, b_ref, *rest):
        e_refs, b_refs = rest[:n_e], rest[n_e:n_e + n_b]
        o_refs = rest[n_e + n_b + n_x:n_e + n_b + n_x + n_o]
        a_refs = rest[n_e + n_b + n_x + n_o:]
        if n_a:
            @pl.when((pl.program_id(0) == 0) & (pl.program_id(1) == 0))
            def _():
                for r in a_refs:
                    r[...] = jnp.zeros(r.shape, F32)
        if mode == "nt" and b_shard:
            acc = None
            for s in range(N_CHIPS):
                part = lax.dot_general(a_ref[:, s * ks:(s + 1) * ks], b_ref[s], dims, preferred_element_type=F32)
                acc = part if acc is None else acc + part
        else:
            acc = lax.dot_general(a_ref[...], b_ref[...], dims, preferred_element_type=F32)
        if ref_epi is not None:
            ref_epi(acc, e_refs, b_refs, o_refs, a_refs)
            return
        outs = epi(acc, *[r[...] for r in e_refs])
        for r, o in zip(o_refs, outs):
            r[...] = o.astype(r.dtype)

    outs = pl.pallas_call(
        body, name=name, grid=grid, in_specs=[a_spec, b_spec] + e_specs + [_ANY] * n_x, out_specs=o_specs,
        out_shape=o_shapes, compiler_params=_params(("arbitrary", "arbitrary") if n_a else ("parallel", "parallel")),
    )(a, b, *extras, *bcast, *order)
    return outs[0] if n_o + n_a == 1 else outs


def _rowwise(fn, rows, bcast, outs, accs=(), *, tm, name, after=None):
    def norm(r):
        return r if isinstance(r, tuple) else (r, r.shape[1], 0)

    rows = [norm(r) for r in rows]
    T = rows[0][0].shape[0]
    tm = min(tm, T)
    while T % tm:
        tm -= SUBLANES
    n_r, n_b, n_o, n_a = len(rows), len(bcast), len(outs), len(accs)
    order = _after_operand(after)
    n_x = len(order)
    in_specs = [pl.BlockSpec((tm, c), functools.partial(lambda i, cb: (i, cb), cb=cb)) for _, c, cb in rows]
    in_specs += [pl.BlockSpec(b.shape, lambda i: (0, 0)) for b in bcast] + [_ANY] * n_x
    out_specs = [pl.BlockSpec((tm, o[0]), lambda i: (i, 0)) for o in outs]
    out_specs += [pl.BlockSpec(s, lambda i: (0, 0)) for s in accs]
    out_shape = [jax.ShapeDtypeStruct((T, o[2] if len(o) > 2 else o[0]), o[1]) for o in outs]
    out_shape += [jax.ShapeDtypeStruct(s, F32) for s in accs]

    def body(*refs):
        in_refs = refs[:n_r]
        b_refs = refs[n_r:n_r + n_b]
        o_refs = refs[n_r + n_b + n_x:n_r + n_b + n_x + n_o]
        a_refs = refs[n_r + n_b + n_x + n_o:]
        if n_a:
            @pl.when(pl.program_id(0) == 0)
            def _():
                for r in a_refs:
                    r[...] = jnp.zeros(r.shape, F32)
        fn(in_refs, b_refs, o_refs, a_refs)

    res = pl.pallas_call(
        body, name=name, grid=(T // tm,), in_specs=in_specs, out_specs=out_specs, out_shape=out_shape,
        compiler_params=_params(("arbitrary",) if n_a else ("parallel",)),
    )(*[r[0] for r in rows], *bcast, *order)
    return res


def _rsum(x):
    return jnp.sum(x, axis=0, keepdims=True)


def _rms_fwd(x, g, name, after=None):
    def fn(ins, bs, outs, accs):
        xv = ins[0][...]
        r = lax.rsqrt(jnp.mean(xv * xv, axis=-1, keepdims=True) + EPS)
        outs[0][...] = (xv * r * bs[0][...]).astype(BF16)

    return _rowwise(fn, [x], [g], [(D_MODEL, BF16)], tm=512, name=name, after=after)[0]


def _rms_bwd_math(xv, dh, g):
    r = lax.rsqrt(jnp.mean(xv * xv, axis=-1, keepdims=True) + EPS)
    hn = xv * r
    dgh = dh * g
    dx = r * (dgh - hn * jnp.mean(dgh * hn, axis=-1, keepdims=True))
    return dx, _rsum(dh * hn)


def _mm_norm_bwd(dy, w, x, dres, g, name, after=None):
    def epilogue(acc, e_refs, b_refs, o_refs, a_refs):
        dx, dg = _rms_bwd_math(e_refs[0][...], acc, b_refs[0][...])
        dx = dx + e_refs[1][...]
        o_refs[0][...] = dx
        o_refs[1][...] = dx.astype(BF16)
        a_refs[0][...] += dg

    return _mm(dy, w, mode="nt", b_shard=True, out_dtypes=(F32, BF16), extras=(x, dres), bcast=(g,),
               accs=((1, D_MODEL),), ref_epi=epilogue, name=name, after=after)


def _mm_res_norm(a, w, res, g, name):
    def epilogue(acc, e_refs, b_refs, o_refs, a_refs):
        xv = acc + e_refs[0][...]
        o_refs[0][...] = xv
        r = lax.rsqrt(jnp.mean(xv * xv, axis=-1, keepdims=True) + EPS)
        o_refs[1][...] = (xv * r * b_refs[0][...]).astype(BF16)

    return _mm(a, w, mode="nn", out_dtypes=(F32, BF16), extras=(res,), bcast=(g,), ref_epi=epilogue, name=name)


def _mm_final_loss(a, w, res, target, g, name):
    def epilogue(acc, e_refs, b_refs, o_refs, a_refs):
        xv = acc + e_refs[0][...]
        gv = b_refs[0][...]
        r = lax.rsqrt(jnp.mean(xv * xv, axis=-1, keepdims=True) + EPS)
        e = xv * r * gv - e_refs[1][...]
        tok = jnp.mean(e * e, axis=-1, keepdims=True)
        a_refs[0][...] += 0.5 * jnp.sum(tok, axis=0, keepdims=True) * jnp.ones((1, LANES), F32)
        dx, dg = _rms_bwd_math(xv, e * (1.0 / D_MODEL), gv)
        o_refs[0][...] = dx
        o_refs[1][...] = dx.astype(BF16)
        a_refs[1][...] += dg

    return _mm(a, w, mode="nn", out_dtypes=(F32, BF16), extras=(res, target), bcast=(g,),
               accs=((1, LANES), (1, D_MODEL)), ref_epi=epilogue, name=name)


def _relu2(acc):
    r = jnp.maximum(acc, 0.0)
    return r * r, r


def _mlp_fwd(x, h, fetch, tag, finish):
    w_up = fetch(f"mlp{tag}_up", h)
    a, r = _mm(h, w_up, mode="nn", b_shard=True, out_dtypes=(BF16, BF16), epi=_relu2, name=f"mlp{tag}_up")
    w_down = fetch(f"mlp{tag}_down", a)
    return finish(a, w_down, x, f"mlp{tag}_down"), (h, a, r, w_up, w_down)


def _mlp_bwd(x, g, saved, dx, dx_bf, tag, after):
    h, a, r, w_up, w_down = saved
    d_down = _mm(a, dx_bf, mode="tn", out_dtypes=(BF16,), name=f"mlp{tag}_dwdown", after=after)
    dup = _mm(dx_bf, w_down, mode="nt", extras=(r,), out_dtypes=(BF16,),
              epi=lambda acc, rv: (acc * (2.0 * rv.astype(F32)),), name=f"mlp{tag}_dup")
    d_up = _mm(h, dup, mode="tn", o_shard=True, out_dtypes=(BF16,), name=f"mlp{tag}_dwup")
    dx_new, dx_new_bf, dg = _mm_norm_bwd(dup, w_up, x, dx, g, f"mlp{tag}_dh")
    return dx_new, dx_new_bf, dg, d_up, d_down


def _rope_tables(L, B):
    rows = L // GRID_W
    row = np.repeat(np.arange(rows, dtype=np.float32), GRID_W)
    col = np.tile(np.arange(GRID_W, dtype=np.float32), rows)
    inv = (ROPE_THETA ** (-np.arange(HEAD_DIM // 4, dtype=np.float32) / (HEAD_DIM // 4))).astype(np.float32)
    ar, ac = row[:, None] * inv, col[:, None] * inv
    cos = np.concatenate([np.cos(ar), np.cos(ar), np.cos(ac), np.cos(ac)], axis=-1)
    sin = np.concatenate([-np.sin(ar), np.sin(ar), -np.sin(ac), np.sin(ac)], axis=-1)
    return jnp.asarray(np.tile(cos, (B, 1)), F32), jnp.asarray(np.tile(sin, (B, 1)), F32)


def _swap_halves(x):
    lane = lax.broadcasted_iota(jnp.int32, x.shape, 1)
    return jnp.where((lane % 64) < 32, pltpu.roll(x, HEAD_DIM - 32, 1), pltpu.roll(x, 32, 1))


def _qk_prep(qkv, cos, sin, q_g, k_g):
    def fn(ins, bs, outs, accs):
        c, s = ins[1][...], ins[2][...]
        for h in range(N_HEADS + N_KV):
            xv = ins[0][:, h * HEAD_DIM:(h + 1) * HEAD_DIM]
            g = bs[0][...] if h < N_HEADS else bs[1][...]
            r = lax.rsqrt(jnp.mean(xv * xv, axis=-1, keepdims=True) + EPS)
            z = xv * r * g
            y = (z * c + _swap_halves(z) * s).astype(BF16)
            if h < N_HEADS:
                outs[0][:, h * HEAD_DIM:(h + 1) * HEAD_DIM] = y
            else:
                outs[1][:, (h - N_HEADS) * HEAD_DIM:(h - N_HEADS + 1) * HEAD_DIM] = y
        outs[2][...] = ins[0][:, (N_HEADS + N_KV) * HEAD_DIM:].astype(BF16)

    kvw = N_KV * HEAD_DIM
    return _rowwise(fn, [qkv, cos, sin], [q_g, k_g], [(D_MODEL, BF16), (kvw, BF16), (kvw, BF16)], tm=512,
                    name="attn_qk_prep")


def _qk_prep_bwd(qkv, dq, dk, dv, cos, sin, q_g, k_g):
    def fn(ins, bs, outs, accs):
        c, s = ins[4][...], ins[5][...]
        for h in range(N_HEADS + N_KV):
            sl = slice(h * HEAD_DIM, (h + 1) * HEAD_DIM)
            xv = ins[0][:, sl]
            if h < N_HEADS:
                g, dy, acc = bs[0][...], ins[1][:, sl], accs[0]
            else:
                ks = slice((h - N_HEADS) * HEAD_DIM, (h - N_HEADS + 1) * HEAD_DIM)
                g, dy, acc = bs[1][...], ins[2][:, ks], accs[1]
            r = lax.rsqrt(jnp.mean(xv * xv, axis=-1, keepdims=True) + EPS)
            xn = xv * r
            dz = dy * c - _swap_halves(dy) * s
            acc[...] += _rsum(dz * xn)
            dxn = dz * g
            outs[0][:, sl] = (r * (dxn - xn * jnp.mean(dxn * xn, axis=-1, keepdims=True))).astype(BF16)
        outs[0][:, (N_HEADS + N_KV) * HEAD_DIM:] = ins[3][...].astype(BF16)

    return _rowwise(fn, [qkv, dq, dk, dv, cos, sin], [q_g, k_g], [(qkv.shape[1], BF16)],
                    [(1, HEAD_DIM), (1, HEAD_DIM)], tm=256, name="attn_qk_prep_bwd")


_EXP2_SCALE = SCALE * math.log2(math.e)


def _exp_rows(q, k):
    s = lax.dot_general(q, k, _NT, preferred_element_type=F32)
    p = jnp.exp2((s - jnp.max(s, axis=-1, keepdims=True)) * _EXP2_SCALE)
    return p, jnp.sum(p, axis=-1, keepdims=True)


def _attn_fwd(q, k, v, B, L, tq=2048, sub=256):
    tq = min(tq, L)
    sub = min(sub, tq)
    nq = L // tq

    def body(q_ref, k_ref, v_ref, o_ref):
        kv, vv = k_ref[...], v_ref[...]
        for c in range(tq // sub):
            rows = slice(c * sub, (c + 1) * sub)
            p, l = _exp_rows(q_ref[rows, :], kv)
            o = jnp.dot(p.astype(BF16), vv, preferred_element_type=F32)
            o_ref[rows, :] = (o * (1.0 / l)).astype(o_ref.dtype)

    return pl.pallas_call(
        body, name="attn_fwd", grid=(B, N_HEADS, nq),
        in_specs=[pl.BlockSpec((tq, HEAD_DIM), lambda b, h, i: (b * nq + i, h)),
                  pl.BlockSpec((L, HEAD_DIM), lambda b, h, i: (b, h // GROUP)),
                  pl.BlockSpec((L, HEAD_DIM), lambda b, h, i: (b, h // GROUP))],
        out_specs=pl.BlockSpec((tq, HEAD_DIM), lambda b, h, i: (b * nq + i, h)),
        out_shape=jax.ShapeDtypeStruct((B * L, D_MODEL), BF16),
        compiler_params=_params(("parallel", "parallel", "parallel")),
    )(q, k, v)


def _attn_bwd(q, k, v, do, B, L, tq=2048, sub=512):
    tq = min(tq, L)
    sub = min(sub, tq)
    nq = L // tq

    def body(q_ref, k_ref, v_ref, do_ref, dq_ref, dk_ref, dv_ref):
        @pl.when((pl.program_id(2) == 0) & (pl.program_id(3) == 0))
        def _():
            dk_ref[...] = jnp.zeros(dk_ref.shape, F32)
            dv_ref[...] = jnp.zeros(dv_ref.shape, F32)

        kv, vv = k_ref[...], v_ref[...]
        ps, es, dos, qs = [], [], [], []
        for c in range(tq // sub):
            rows = slice(c * sub, (c + 1) * sub)
            qc, doc = q_ref[rows, :], do_ref[rows, :]
            p, l = _exp_rows(qc, kv)
            inv = 1.0 / l
            dp = lax.dot_general(doc, vv, _NT, preferred_element_type=F32)
            delta = jnp.sum(p * dp, axis=-1, keepdims=True) * inv
            e = (p * (dp - delta)).astype(BF16)
            dq_ref[rows, :] = jnp.dot(e, kv, preferred_element_type=F32) * (inv * SCALE)
            ps.append(p.astype(BF16))
            es.append(e)
            dos.append((doc.astype(F32) * inv).astype(BF16))
            qs.append((qc.astype(F32) * (inv * SCALE)).astype(BF16))
        cat = lambda xs: xs[0] if len(xs) == 1 else jnp.concatenate(xs, axis=0)
        dv_ref[...] += lax.dot_general(cat(ps), cat(dos), _TN, preferred_element_type=F32)
        dk_ref[...] += lax.dot_general(cat(es), cat(qs), _TN, preferred_element_type=F32)

    qmap = lambda b, kh, g, i: (b * nq + i, kh * GROUP + g)
    kmap = lambda b, kh, g, i: (b, kh)
    kvw = N_KV * HEAD_DIM
    return pl.pallas_call(
        body, name="attn_bwd", grid=(B, N_KV, GROUP, nq),
        in_specs=[pl.BlockSpec((tq, HEAD_DIM), qmap), pl.BlockSpec((L, HEAD_DIM), kmap),
                  pl.BlockSpec((L, HEAD_DIM), kmap), pl.BlockSpec((tq, HEAD_DIM), qmap)],
        out_specs=[pl.BlockSpec((tq, HEAD_DIM), qmap), pl.BlockSpec((L, HEAD_DIM), kmap),
                   pl.BlockSpec((L, HEAD_DIM), kmap)],
        out_shape=[jax.ShapeDtypeStruct((B * L, D_MODEL), F32), jax.ShapeDtypeStruct((B * L, kvw), F32),
                   jax.ShapeDtypeStruct((B * L, kvw), F32)],
        compiler_params=_params(("parallel", "parallel", "arbitrary", "arbitrary")),
    )(q, k, v, do)


def _conv_shift(x, t, L, k):
    if k == 2:
        return x
    if k < 2:
        return jnp.where(t >= 2 - k, pltpu.roll(x, 2 - k, 0), 0.0)
    return jnp.where(t < L - (k - 2), pltpu.roll(x, L - (k - 2), 0), 0.0)


def _conv_apply(x, w_ref, L):
    t = lax.broadcasted_iota(jnp.int32, x.shape, 0)
    acc = w_ref[4:5, :] + w_ref[2:3, :] * x
    for k in (0, 1, 3):
        acc = acc + w_ref[k:k + 1, :] * _conv_shift(x, t, L, k)
    return acc


def _conv_fwd(z, wb, B, L, tc=256):
    noff = D_MODEL // tc

    def body(z_ref, w_ref, o_ref):
        o_ref[...] = _conv_apply(z_ref[...], w_ref, L)

    return pl.pallas_call(
        body, name="rg_conv", grid=(B, noff),
        in_specs=[pl.BlockSpec((L, tc), lambda b, j: (b, noff + j)), pl.BlockSpec((SUBLANES, tc), lambda b, j: (0, j))],
        out_specs=pl.BlockSpec((L, tc), lambda b, j: (b, j)),
        out_shape=jax.ShapeDtypeStruct((B * L, D_MODEL), F32),
        compiler_params=_params(("parallel", "parallel")),
    )(z, wb)


def _conv_bwd(z, g, wb, dz, B, L, tc=256, after=None):
    noff = D_MODEL // tc
    order = _after_operand(after)

    def body(z_ref, g_ref, w_ref, dz_in, *rest):
        dx_ref, dw_ref = rest[len(order):]

        @pl.when(pl.program_id(1) == 0)
        def _():
            dw_ref[...] = jnp.zeros(dw_ref.shape, F32)

        x, gv = z_ref[...], g_ref[...]
        t = lax.broadcasted_iota(jnp.int32, x.shape, 0)
        dx = w_ref[2:3, :] * gv
        for k in (0, 1, 3):
            dx = dx + w_ref[k:k + 1, :] * _conv_shift(gv, t, L, 4 - k)
        dx_ref[...] = dx.astype(BF16)
        for k in range(4):
            dw_ref[k:k + 1, :] += _rsum(_conv_shift(x, t, L, k) * gv)
        dw_ref[4:5, :] += _rsum(gv)

    return pl.pallas_call(
        body, name="rg_conv_bwd", grid=(noff, B),
        in_specs=[pl.BlockSpec((L, tc), lambda j, b: (b, noff + j)), pl.BlockSpec((L, tc), lambda j, b: (b, j)),
                  pl.BlockSpec((SUBLANES, tc), lambda j, b: (0, j)), _ANY] + [_ANY] * len(order),
        out_specs=[pl.BlockSpec((L, tc), lambda j, b: (b, noff + j)),
                   pl.BlockSpec((SUBLANES, tc), lambda j, b: (0, j))],
        out_shape=[jax.ShapeDtypeStruct(dz.shape, dz.dtype), jax.ShapeDtypeStruct((SUBLANES, D_MODEL), F32)],
        input_output_aliases={3: 0},
        compiler_params=_params(("parallel", "arbitrary")),
    )(z, g, wb, dz, *order)


def _softplus(x):
    return jnp.maximum(x, 0.0) + jnp.log1p(jnp.exp(-jnp.abs(x)))


_ROW_BA, _ROW_BX, _ROW_LAM = 0, 2, 4


def _gate_math(xb, pre, vec_ref, d, sl):
    pa = pre[:, (2 * d) * LRU_BW:(2 * d + 1) * LRU_BW] + vec_ref[_ROW_BA + d:_ROW_BA + d + 1, sl]
    px = pre[:, (2 * d + 1) * LRU_BW:(2 * d + 2) * LRU_BW] + vec_ref[_ROW_BX + d:_ROW_BX + d + 1, sl]
    r = 0.5 * jnp.tanh(0.5 * pa) + 0.5
    i = 0.5 * jnp.tanh(0.5 * px) + 0.5
    sp = _softplus(-vec_ref[_ROW_LAM + d:_ROW_LAM + d + 1, sl])
    log_a = (-RG_C) * r * sp
    a = jnp.exp(log_a)
    th = jnp.tanh(log_a)
    om = -2.0 * th / (1.0 - th)
    mult = jnp.sqrt(om)
    return a, mult * (i * xb), (r, i, sp, om, mult)


def _gate_fwd(rec, wcat, gvec):
    def fn(ins, bs, outs, accs):
        for blk in range(LRU_BLOCKS):
            sl = slice(blk * LRU_BW, (blk + 1) * LRU_BW)
            xb = ins[0][:, sl]
            pre = jnp.dot(xb.astype(BF16), bs[0][sl, :], preferred_element_type=F32)
            for d in range(2):
                a, u, _ = _gate_math(xb, pre, bs[1], d, sl)
                outs[2 * d][:, sl] = a
                outs[2 * d + 1][:, sl] = u

    return _rowwise(fn, [rec], [wcat, gvec], [(D_MODEL, F32)] * 4, tm=256, name="rg_gate")


def _gate_bwd(rec, du_f, da_f, du_b, da_b, wcat, gvec):
    def fn(ins, bs, outs, accs):
        for blk in range(LRU_BLOCKS):
            sl = slice(blk * LRU_BW, (blk + 1) * LRU_BW)
            xb = ins[0][:, sl]
            xb16 = xb.astype(BF16)
            w = bs[0][sl, :]
            pre = jnp.dot(xb16, w, preferred_element_type=F32)
            dx = jnp.zeros_like(xb)
            dpre = []
            for d in range(2):
                a, _, (r, i, sp, om, mult) = _gate_math(xb, pre, bs[1], d, sl)
                du, da = ins[1 + 2 * d][:, sl], ins[2 + 2 * d][:, sl]
                d_i = du * mult * xb
                d_mult = du * i * xb
                dx = dx + du * mult * i
                dlog = da * a - d_mult * (1.0 - om) / mult
                d_r = dlog * ((-RG_C) * sp)
                d_sp = _rsum(dlog * ((-RG_C) * r))
                lam = bs[1][_ROW_LAM + d:_ROW_LAM + d + 1, sl]
                accs[2][_ROW_LAM + d:_ROW_LAM + d + 1, sl] += d_sp * (-jax.nn.sigmoid(-lam))
                dpa = d_r * r * (1.0 - r)
                dpx = d_i * i * (1.0 - i)
                accs[2][_ROW_BA + d:_ROW_BA + d + 1, sl] += _rsum(dpa)
                accs[2][_ROW_BX + d:_ROW_BX + d + 1, sl] += _rsum(dpx)
                dpre += [dpa, dpx]
            dpre = jnp.concatenate(dpre, axis=1).astype(BF16)
            dw = lax.dot_general(xb16, dpre, _TN, preferred_element_type=F32)
            for d in range(2):
                rows = slice(d * D_MODEL + blk * LRU_BW, d * D_MODEL + (blk + 1) * LRU_BW)
                accs[0][rows, :] += dw[:, (2 * d) * LRU_BW:(2 * d + 1) * LRU_BW]
                accs[1][rows, :] += dw[:, (2 * d + 1) * LRU_BW:(2 * d + 2) * LRU_BW]
            outs[0][:, sl] = dx + lax.dot_general(dpre, w, _NT, preferred_element_type=F32)

    gate_shape = (2 * D_MODEL, LRU_BW)
    return _rowwise(fn, [rec, du_f, da_f, du_b, da_b], [wcat, gvec], [(D_MODEL, F32)],
                    [gate_shape, gate_shape, (SUBLANES, D_MODEL)], tm=256, name="rg_gate_bwd")


def _as_time_blocks(x):
    return x.reshape(x.shape[0] // SUBLANES, SUBLANES, x.shape[1])


def _scan_call(body, ins, n_out, B, L, tc, name):
    nb = L // SUBLANES
    spec = pl.BlockSpec((nb, SUBLANES, tc), lambda b, j: (b, 0, j))
    T = ins[0].shape[0]
    outs = pl.pallas_call(
        functools.partial(body, nb), name=name, grid=(B, D_MODEL // tc),
        in_specs=[spec] * len(ins), out_specs=[spec] * n_out,
        out_shape=[jax.ShapeDtypeStruct((T // SUBLANES, SUBLANES, D_MODEL), F32)] * n_out,
        compiler_params=_params(("parallel", "parallel")),
    )(*[_as_time_blocks(x) for x in ins])
    return [o.reshape(T, D_MODEL) for o in outs]


def _block_scan(A, U, reverse):
    row = lax.broadcasted_iota(jnp.int32, A.shape, 0)
    for s in (1, 2, 4):
        shift = SUBLANES - s if reverse else s
        valid = (row < SUBLANES - s) if reverse else (row >= s)
        a_sh = jnp.where(valid, pltpu.roll(A, shift, 0), 1.0)
        u_sh = jnp.where(valid, pltpu.roll(U, shift, 0), 0.0)
        U = A * u_sh + U
        A = A * a_sh
    return A, U


_LAST = SUBLANES - 1
SCAN_UNROLL = 8


def _loop_blocks(nb, step, init):
    def group(g, carry):
        for k in range(SCAN_UNROLL):
            carry = step(g * SCAN_UNROLL + k, carry)
        return carry

    return lax.fori_loop(0, nb // SCAN_UNROLL, group, init)


def _scan_fwd(a_f, u_f, a_b, u_b, B, L, tc=256):
    def body(nb, af, uf, ab, ub, hf, hb):
        def step(i, carry):
            c1, c2 = carry
            ib = nb - 1 - i
            p, h = _block_scan(af[i], uf[i], False)
            h = h + p * c1
            hf[i] = h
            p2, h2 = _block_scan(ab[ib], ub[ib], True)
            h2 = h2 + p2 * c2
            hb[ib] = h2
            return h[_LAST:, :], h2[:1, :]

        zero = jnp.zeros((1, tc), F32)
        _loop_blocks(nb, step, (zero, zero))

    return _scan_call(body, [a_f, u_f, a_b, u_b], 2, B, L, tc, "rg_scan")


def _scan_bwd(dy, a_f, h_f, a_b, h_b, B, L, tc=256):
    def body(nb, dy_r, af, hf, ab, hb, duf, daf, dub, dab):
        def step(i, carry):
            c1, c2 = carry
            ir = nb - 1 - i
            row = lax.broadcasted_iota(jnp.int32, (SUBLANES, tc), 0)
            a_up = jnp.where(row == _LAST, af[jnp.minimum(ir + 1, nb - 1), :1, :], pltpu.roll(af[ir], _LAST, 0))
            p, lam = _block_scan(a_up, dy_r[ir], True)
            lam = lam + p * c1
            before = hf[jnp.maximum(ir - 1, 0), _LAST:, :] * (ir > 0).astype(F32)
            duf[ir] = lam
            daf[ir] = lam * jnp.where(row == 0, before, pltpu.roll(hf[ir], 1, 0))
            a_dn = jnp.where(row == 0, ab[jnp.maximum(i - 1, 0), _LAST:, :], pltpu.roll(ab[i], 1, 0))
            p2, lam2 = _block_scan(a_dn, dy_r[i], False)
            lam2 = lam2 + p2 * c2
            after = hb[jnp.minimum(i + 1, nb - 1), :1, :] * (i < nb - 1).astype(F32)
            dub[i] = lam2
            dab[i] = lam2 * jnp.where(row == _LAST, after, pltpu.roll(hb[i], _LAST, 0))
            return lam[:1, :], lam2[_LAST:, :]

        zero = jnp.zeros((1, tc), F32)
        _loop_blocks(nb, step, (zero, zero))

    return _scan_call(body, [dy, a_f, h_f, a_b, h_b], 4, B, L, tc, "rg_scan_bwd")


_GELU_C = math.sqrt(2.0 / math.pi)


def _gelu_parts(x):
    th = jnp.tanh(_GELU_C * (x + 0.044715 * x * x * x))
    return 0.5 * x * (1.0 + th), th


def _gated_out(h_f, h_b, z):
    def fn(ins, bs, outs, accs):
        gl, _ = _gelu_parts(ins[2][...])
        outs[0][...] = ((ins[0][...] + ins[1][...]) * gl).astype(BF16)

    return _rowwise(fn, [h_f, h_b, (z, D_MODEL, 0)], [], [(D_MODEL, BF16)], tm=512, name="rg_gated_out")[0]


def _mm_gated_out_bwd(dx, w_out, h_f, h_b, z, name, after=None):
    def epilogue(acc, e_refs, b_refs, o_refs, a_refs):
        x = e_refs[2][...]
        gl, th = _gelu_parts(x)
        dgl = 0.5 * (1.0 + th) + 0.5 * x * (1.0 - th * th) * (_GELU_C * (1.0 + 3.0 * 0.044715 * x * x))
        o_refs[0][...] = acc * gl
        o_refs[1][...] = (acc * (e_refs[0][...] + e_refs[1][...]) * dgl).astype(BF16)

    return _mm(dx, w_out, mode="nt", out_dtypes=(F32, BF16), out_cols=(D_MODEL, 2 * D_MODEL), extras=(h_f, h_b, z),
               ref_epi=epilogue, name=name, after=after)


def _row_block(i):
    return pl.ds(pl.multiple_of(i * SUBLANES, SUBLANES), SUBLANES)


def _rg_mix_fwd(z, conv_wb, wcat, gvec, B, L):
    nb = L // SUBLANES
    n_g = D_MODEL // LRU_BW

    def body(zg_ref, zr_ref, cw_ref, w_ref, gv_ref, rec_ref, af_s, ab_s, hf_ref, hb_ref, yg_ref, uf_s, ub_s):
        rec = _conv_apply(zr_ref[...], cw_ref, L)
        rec_ref[...] = rec
        pre = jnp.dot(rec.astype(BF16), w_ref[...], preferred_element_type=F32)
        for d, (a_s, u_s) in enumerate(((af_s, uf_s), (ab_s, ub_s))):
            a, u, _ = _gate_math(rec, pre, gv_ref, d, slice(None))
            a_s[...] = a
            u_s[...] = u

        def step(i, carry):
            c1, c2 = carry
            rows, rows_b = _row_block(i), _row_block(nb - 1 - i)
            p, h = _block_scan(af_s[rows, :], uf_s[rows, :], False)
            h = h + p * c1
            hf_ref[rows, :] = h
            p2, h2 = _block_scan(ab_s[rows_b, :], ub_s[rows_b, :], True)
            h2 = h2 + p2 * c2
            hb_ref[rows_b, :] = h2
            return h[_LAST:, :], h2[:1, :]

        zero = jnp.zeros((1, LRU_BW), F32)
        _loop_blocks(nb, step, (zero, zero))
        gl, _ = _gelu_parts(zg_ref[...])
        yg_ref[...] = ((hf_ref[...] + hb_ref[...]) * gl).astype(BF16)

    seq = lambda off: pl.BlockSpec((L, LRU_BW), lambda b, g: (b, off + g))
    vec = pl.BlockSpec((SUBLANES, LRU_BW), lambda b, g: (0, g))
    T = B * L
    return pl.pallas_call(
        body, name="rg_mix", grid=(B, n_g),
        in_specs=[seq(0), seq(n_g), vec, pl.BlockSpec((LRU_BW, 4 * LRU_BW), lambda b, g: (g, 0)), vec],
        out_specs=[seq(0)] * 6,
        out_shape=[jax.ShapeDtypeStruct((T, D_MODEL), F32)] * 5 + [jax.ShapeDtypeStruct((T, D_MODEL), BF16)],
        scratch_shapes=[pltpu.VMEM((L, LRU_BW), F32)] * 2,
        compiler_params=_params(("parallel", "parallel")),
    )(z, z, conv_wb, wcat, gvec)


def _make_wcat(w_a, w_x):
    g = jnp.stack([w_a[0, 0], w_x[0, 0], w_a[0, 1], w_x[0, 1]])
    return jnp.transpose(g, (1, 2, 0, 3)).reshape(D_MODEL, 4 * LRU_BW)


def _rows_at(part, first):
    return jnp.pad(part, ((first, SUBLANES - first - part.shape[0]), (0, 0)))


def _qk_slot(q_g, k_g):
    wide = lambda v, at: jnp.pad(v, ((0, SUBLANES - 1), (at, D_MODEL - at - HEAD_DIM)))
    return wide(q_g, 0) + wide(k_g, HEAD_DIM)


def _local_step(x, target, P, fetch, emit, B, L, after=None):
    g_mix, g_mlp = P["norm_mix_g"], P["norm_mlp_g"]
    h0 = _rms_fwd(x, g_mix[0:1], "rg_norm", after=after)
    w_in, w_out, conv_wb, wcat, gvec = fetch("rg", h0)
    z = _mm(h0, w_in, mode="nn", b_shard=True, name="rg_in")
    rec, a_f, a_b, h_f, h_b, yg = _rg_mix_fwd(z, conv_wb, wcat, gvec, B, L)
    x1, h1 = _mm_res_norm(yg, w_out, x, g_mlp[0:1], "rg_out")
    (x2, h3), mlp0 = _mlp_fwd(x1, h1, fetch, 0, lambda a, w, res, name: _mm_res_norm(a, w, res, g_mix[1:2], name))
    w_qkv, w_o = fetch("att", h3)
    qkv = _mm(h3, w_qkv, mode="nn", b_shard=True, name="attn_qkv")
    cos, sin = _rope_tables(L, B)
    qh, kh, vh = _qk_prep(qkv, cos, sin, P["q_g"], P["k_g"])
    o = _attn_fwd(qh, kh, vh, B, L)
    x3, h4 = _mm_res_norm(o, w_o, x2, g_mlp[1:2], "attn_out")
    (dx4, dx4_bf, loss_acc, d_final_g), mlp1 = _mlp_fwd(
        x3, h4, fetch, 1, lambda a, w, res, name: _mm_final_loss(a, w, res, target, P["final_g"], name))

    dx3, dx3_bf, dg_mlp1, d_up1, d_down1 = _mlp_bwd(x3, g_mlp[1:2], mlp1, dx4, dx4_bf, 1, None)
    tok = emit("mlp1", [d_up1, d_down1])
    d_wo = _mm(o, dx3_bf, mode="tn", out_dtypes=(BF16,), name="attn_dwo", after=tok)
    do = _mm(dx3_bf, w_o, mode="nt", out_dtypes=(BF16,), name="attn_do")
    dq, dk, dv = _attn_bwd(qh, kh, vh, do, B, L)
    dqkv, dq_g, dk_g = _qk_prep_bwd(qkv, dq, dk, dv, cos, sin, P["q_g"], P["k_g"])
    d_wqkv = _mm(h3, dqkv, mode="tn", o_shard=True, out_dtypes=(BF16,), name="attn_dwqkv")
    tok = emit("att", [d_wqkv, d_wo])
    dx2, dx2_bf, dg_mix1 = _mm_norm_bwd(dqkv, w_qkv, x2, dx3, g_mix[1:2], "attn_dh", after=tok)
    tok = emit("point_attn_done", [dx2_bf])
    dx1, dx1_bf, dg_mlp0, d_up0, d_down0 = _mlp_bwd(x1, g_mlp[0:1], mlp0, dx2, dx2_bf, 0, tok)
    d_wout = _mm(yg, dx1_bf, mode="tn", out_dtypes=(BF16,), name="rg_dwout")
    tok = emit("mlp0", [d_up0, d_down0, d_wout])
    dy, dgate = _mm_gated_out_bwd(dx1_bf, w_out, h_f, h_b, z, "rg_dyg", after=tok)
    du_f, da_f, du_b, da_b = _scan_bwd(dy, a_f, h_f, a_b, h_b, B, L)
    drec_c, d_wa, d_wx, d_gvec = _gate_bwd(rec, du_f, da_f, du_b, da_b, wcat, gvec)
    tok = emit("gates", [d_wa, d_wx])
    dz, d_convwb = _conv_bwd(z, drec_c, conv_wb, dgate, B, L, after=tok)
    tok = emit("point_mix_done", [dz])
    d_win = _mm(h0, dz, mode="tn", o_shard=True, out_dtypes=(BF16,), name="rg_dwin", after=tok)
    tok = emit("rg_in", [d_win])
    grad_x, _, dg_mix0 = _mm_norm_bwd(dz, w_in, x, dx1, g_mix[0:1], "rg_dh", after=tok)

    norms = (_rows_at(dg_mix0, 0) + _rows_at(dg_mix1, 1) + _rows_at(dg_mlp0, 2) + _rows_at(dg_mlp1, 3)
             + _rows_at(d_final_g, 4)
             + jnp.pad(loss_acc, ((LOSS_ROW, SUBLANES - 1 - LOSS_ROW), (0, D_MODEL - LANES))))
    vec = jnp.concatenate([norms, d_convwb, d_gvec, _qk_slot(dq_g, dk_g)], axis=0)
    return grad_x, vec


_MESH = pl.DeviceIdType.MESH


def _place():
    x, y, c = lax.axis_index("x"), lax.axis_index("y"), lax.axis_index("c")
    peers = [((1 - x) if j & 2 else x, (1 - y) if j & 1 else y) for j in (1, 2, 3)]
    return x, y, c, peers


def _sum_leading(slots, name):
    def body(s_ref, o_ref):
        acc = s_ref[0]
        for d in range(1, slots.shape[0]):
            acc = acc + s_ref[d]
        o_ref[...] = acc

    return pl.pallas_call(body, name=name, out_shape=jax.ShapeDtypeStruct(slots.shape[1:], slots.dtype))(slots)


_HBM = pl.BlockSpec(memory_space=pltpu.HBM)
_SEM = pl.BlockSpec(memory_space=pltpu.SEMAPHORE)
_EFFECT = pltpu.SideEffectType.DATAFLOW_SIDE_EFFECTING


_COPIES = dict(gather=N_CHIPS - 1, scatter=N_CHIPS - 1, swap=1, spread=N_DEVICES - 1)


def _split_copies(kind, srcs, lands, send, recv):
    x, y, c, peers = _place()
    me = 2 * x + y
    per = _COPIES[kind]
    out = []
    for a in range(len(lands)):
        for j in range(per):
            if kind == "swap":
                src, there, here, dev = srcs[a], lands[a], lands[a], (x, y, 1 - c)
            elif kind == "spread":
                k = j + 1
                dev = ((1 - x) if k & 4 else x, (1 - y) if k & 2 else y, (1 - c) if k & 1 else c)
                mine = lands[a].at[4 * x + 2 * y + c]
                src, there, here = mine, mine, lands[a].at[4 * dev[0] + 2 * dev[1] + dev[2]]
            else:
                px, py = peers[j]
                dev = (px, py, c)
                if kind == "gather":
                    src, there, here = lands[a].at[me], lands[a].at[me], lands[a].at[2 * px + py]
                else:
                    src, there, here = srcs[a].at[2 * px + py], lands[a].at[j], lands[a].at[j]
            mk = functools.partial(
                pltpu.make_async_remote_copy, src_ref=src, send_sem=send.at[per * a + j],
                recv_sem=recv.at[per * a + j], device_id=dev, device_id_type=_MESH)
            out.append((functools.partial(mk, dst_ref=there), functools.partial(mk, dst_ref=here)))
    return out


def _exchange_start(kind, srcs, lands, name, after=None):
    arrays = list(srcs) + list(lands)
    n_s, n, n_all = len(srcs), len(lands), len(srcs) + len(lands)
    n_sem = _COPIES[kind] * n
    order = _after_operand(after)
    n_x = len(order)

    def body(*refs):
        send, recv = refs[n_all + n_x], refs[n_all + n_x + 1]
        token = refs[-1]
        for started, _ in _split_copies(kind, refs[:n_s], refs[n_s:n_all], send, recv):
            started().start()
        token[...] = jnp.zeros(token.shape, F32)

    res = pl.pallas_call(
        body, name=name,
        out_shape=(pltpu.SemaphoreType.DMA((n_sem,)), pltpu.SemaphoreType.DMA((n_sem,)),
                   *[pltpu.HBM(a.shape, a.dtype) for a in arrays], jax.ShapeDtypeStruct((SUBLANES, LANES), F32)),
        in_specs=[_HBM] * n_all + [_ANY] * n_x,
        out_specs=(_SEM, _SEM, *[_HBM] * n_all, pl.BlockSpec(memory_space=pltpu.VMEM)),
        input_output_aliases={i: 2 + i for i in range(n_all)},
        compiler_params=pltpu.CompilerParams(has_side_effects=_EFFECT),
    )(*[pltpu.with_memory_space_constraint(a, pltpu.HBM) for a in arrays], *order)
    return (res[0], res[1], res[2:2 + n_s], res[2 + n_s:2 + n_all]), res[-1]


def _exchange_wait(kind, handle, after, name):
    send, recv, srcs, lands = handle
    arrays = list(srcs) + list(lands)
    n_s, n_all = len(srcs), len(arrays)
    order = list(after) if isinstance(after, (list, tuple)) else [after]

    def body(*refs):
        for started, landing in _split_copies(kind, refs[:n_s], refs[n_s:n_all], refs[n_all], refs[n_all + 1]):
            started().wait_send()
            landing().wait_recv()

    res = pl.pallas_call(
        body, name=name, out_shape=[pltpu.HBM(a.shape, a.dtype) for a in arrays],
        in_specs=[_HBM] * n_all + [_SEM, _SEM] + [_ANY] * len(order), out_specs=[_HBM] * n_all,
        input_output_aliases={i: i for i in range(n_all)},
        compiler_params=pltpu.CompilerParams(has_side_effects=_EFFECT),
    )(*arrays, send, recv, *order)
    return res[:n_s], res[n_s:]


def _index_operand(i):
    return jnp.reshape(i, (1,)).astype(jnp.int32)


def _cast_into_slot(src, row0, rows, me, dtype, name, after=None, add=None, n_slots=N_CHIPS):
    cols = src.shape[1]
    tm = min(512, rows)
    order = _after_operand(after)
    terms = [src] + ([] if add is None else [add])

    def body(me_ref, *rest):
        val = rest[0][...]
        if add is not None:
            val = val + rest[1][...]
        rest[-1][...] = val.astype(dtype)

    return pl.pallas_call(
        body, name=name,
        grid_spec=pltpu.PrefetchScalarGridSpec(
            num_scalar_prefetch=1, grid=(rows // tm,),
            in_specs=[pl.BlockSpec((tm, cols), lambda i, me_ref: (i + row0 // tm, 0))] * len(terms)
            + [_ANY] * len(order),
            out_specs=pl.BlockSpec((None, tm, cols), lambda i, me_ref: (me_ref[0], i, 0))),
        out_shape=jax.ShapeDtypeStruct((n_slots, rows, cols), dtype), compiler_params=_params(("parallel",)),
    )(_index_operand(me), *terms, *order)


def _sum_slots(mine, r, me, name):
    _, rows, cols = r.shape
    tm = min(512, rows)

    def body(me_ref, own_ref, r_ref, o_ref):
        o_ref[...] = ((own_ref[...].astype(F32) + r_ref[0].astype(F32)) + r_ref[1].astype(F32)) + r_ref[2].astype(F32)

    return pl.pallas_call(
        body, name=name,
        grid_spec=pltpu.PrefetchScalarGridSpec(
            num_scalar_prefetch=1, grid=(rows // tm,),
            in_specs=[pl.BlockSpec((None, tm, cols), lambda i, me_ref: (me_ref[0], i, 0)),
                      pl.BlockSpec((N_CHIPS - 1, tm, cols), lambda i, me_ref: (0, i, 0))],
            out_specs=pl.BlockSpec((tm, cols), lambda i, me_ref: (i, 0))),
        out_shape=jax.ShapeDtypeStruct((rows, cols), F32), compiler_params=_params(("parallel",)),
    )(_index_operand(me), mine, r)


def _adamw(w, m, v, ps, qs, name):
    rows, cols = w.shape
    seg_rows = ps[0].shape[0]
    tm = min(256, seg_rows)
    while seg_rows % tm:
        tm -= SUBLANES
    per, n_seg = seg_rows // tm, len(ps)
    parts = list(ps) + ([] if qs is None else list(qs))

    def body(w_ref, m_ref, v_ref, *rest):
        g_refs, outs = rest[:len(parts)], rest[len(parts):]
        grad = lambda s: g_refs[s][...] if qs is None else g_refs[s][...] + g_refs[n_seg + s][...]
        g = grad(0)
        for s in range(1, n_seg):
            g = jnp.where(pl.program_id(0) >= s * per, grad(s), g)
        m1 = ADAM_B1 * m_ref[...] + (1.0 - ADAM_B1) * g
        v1 = ADAM_B2 * v_ref[...] + (1.0 - ADAM_B2) * (g * g)
        m_hat = m1 / (1.0 - ADAM_B1 ** ADAM_STEP)
        v_hat = v1 / (1.0 - ADAM_B2 ** ADAM_STEP)
        outs[0][...] = g
        outs[1][...] = (-ADAM_LR) * (m_hat / (jnp.sqrt(v_hat) + ADAM_EPS) + ADAM_WD * w_ref[...])
        outs[2][...] = m1
        outs[3][...] = v1

    row_spec = pl.BlockSpec((tm, cols), lambda i: (i, 0))
    seg_spec = lambda s: pl.BlockSpec((tm, cols), lambda i: (jnp.clip(i - s * per, 0, per - 1), 0))
    return pl.pallas_call(
        body, name=name, grid=(rows // tm,),
        in_specs=[row_spec] * 3 + [seg_spec(s) for s in range(n_seg)] * (1 if qs is None else 2),
        out_specs=[row_spec] * 4, out_shape=[jax.ShapeDtypeStruct((rows, cols), F32)] * 4,
        compiler_params=_params(("arbitrary",)),
    )(w, m, v, *parts)


def _put_cols(shard, me):
    full = jnp.zeros((shard.shape[0], D_MODEL), F32)
    return lax.dynamic_update_slice(full, shard, (0, me * (D_MODEL // N_CHIPS)))


def _gate_vec_slot(b_a, b_x, lam):
    return _rows_at(b_a, _ROW_BA) + _rows_at(b_x, _ROW_BX) + _rows_at(lam, _ROW_LAM)


def _pack_vec(p, me):
    return jnp.concatenate([
        _rows_at(p["norm_mix_g"], 0) + _rows_at(p["norm_mlp_g"], 2) + _rows_at(p["final_g"][None], 4),
        _rows_at(_put_cols(p["rg_conv_w"][0, :, 0, :], me), 0) + _rows_at(p["rg_conv_b"], 4),
        _gate_vec_slot(_put_cols(p["rg_b_a"][0], me), _put_cols(p["rg_b_x"][0], me), _put_cols(p["rg_lam"][0], me)),
        _qk_slot(p["at_q_g"], p["at_k_g"]),
    ], axis=0)


def _unpack_vec(r, me):
    def cols(rows):
        return lax.dynamic_slice(rows, (0, me * (D_MODEL // N_CHIPS)), (rows.shape[0], D_MODEL // N_CHIPS))

    gate = r[16:24]
    return dict(
        norm_mix_g=r[0:2], norm_mlp_g=r[2:4], final_g=r[4], rg_conv_w=cols(r[8:12])[None, :, None, :],
        rg_conv_b=r[12:13], rg_b_a=cols(gate[_ROW_BA:_ROW_BA + 2])[None], rg_b_x=cols(gate[_ROW_BX:_ROW_BX + 2])[None],
        rg_lam=cols(gate[_ROW_LAM:_ROW_LAM + 2])[None], at_q_g=r[24:25, 0:HEAD_DIM],
        at_k_g=r[24:25, HEAD_DIM:2 * HEAD_DIM])


_WEIGHTS = ['norm_mix_g', 'norm_mlp_g', 'rg_w_in', 'rg_conv_w', 'rg_conv_b', 'rg_w_a', 'rg_b_a', 'rg_w_x', 'rg_b_x',
            'rg_lam', 'rg_w_out', 'at_w_qkv', 'at_q_g', 'at_k_g', 'at_w_o', 'mlp_w_up', 'mlp_w_down', 'final_g']
_BIG = dict(rg_w_in=["rg_w_in"], rg_w_out=["rg_w_out"], at_w_qkv=["at_w_qkv"], at_w_o=["at_w_o"],
            mlp_w_up=["up0", "up1"], mlp_w_down=["down0", "down1"])


def kernel(x, *args):
    n_w = len(_WEIGHTS)
    w = dict(zip(_WEIGHTS, args[:n_w]))
    target = args[n_w]
    m = dict(zip(_WEIGHTS, args[n_w + 1:2 * n_w + 1]))
    v = dict(zip(_WEIGHTS, args[2 * n_w + 1:3 * n_w + 1]))
    B, L, _ = x.shape
    T = B * L
    me = 2 * lax.axis_index("x") + lax.axis_index("y")

    vec = jnp.concatenate([_gate_vec_slot(w["rg_b_a"][0], w["rg_b_x"][0], w["rg_lam"][0]),
                           _rows_at(w["rg_conv_w"][0, :, 0, :], 0)], axis=0)
    flat = lambda a: a.reshape(-1, a.shape[-1])
    rows_of = lambda k: w[k].shape[-2]
    groups = [("rg", [("rg_w_in", 0, BF16), ("rg_w_out", 0, BF16), (vec, 0, F32)]),
              ("mlp0_up", [("mlp_w_up", 0, BF16)]), ("mlp0_down", [("mlp_w_down", 0, BF16)]),
              ("att", [("at_w_qkv", 0, BF16), ("at_w_o", 0, BF16)]),
              ("mlp1", [("mlp_w_up", 1, BF16), ("mlp_w_down", 1, BF16)])]
    gathers, tok = {}, None
    for group, members in groups:
        lands = []
        for n, (k, layer, dtype) in enumerate(members):
            src, rows = (flat(w[k]), rows_of(k)) if isinstance(k, str) else (k, k.shape[0])
            lands.append(_cast_into_slot(src, layer * rows, rows, me, dtype, f"place_{group}{n}", after=tok))
        gathers[group], tok = _exchange_start("gather", [], lands, f"gather_{group}_start", after=tok)
    wcat = _make_wcat(w["rg_w_a"], w["rg_w_x"]).astype(BF16)

    packs = [_pack_vec(p, me) for p in (w, m, v)]

    ready = {}

    def fetch(what, after):
        if what in ready:
            return ready[what]
        group = "mlp1" if what.startswith("mlp1") else what
        order = [after, wcat] + packs if group == "rg" else after
        _, full = _exchange_wait("gather", gathers[group], order, f"gather_{group}_wait")
        if group == "rg":
            vec_full = jnp.transpose(full[2], (1, 0, 2)).reshape(2 * SUBLANES, D_MODEL)
            conv_wb = vec_full[SUBLANES:] + _rows_at(w["rg_conv_b"], 4)
            return full[0], full[1].reshape(D_MODEL, D_MODEL), conv_wb, wcat, vec_full[:SUBLANES]
        if group == "att":
            return full[0], full[1].reshape(D_MODEL, D_MODEL)
        if group == "mlp1":
            ready["mlp1_up"], ready["mlp1_down"] = full[0], full[1].reshape(4 * D_MODEL, D_MODEL)
            return ready[what]
        return full[0] if group == "mlp0_up" else full[0].reshape(4 * D_MODEL, D_MODEL)

    names = dict(mlp1=["up1", "down1"], att=["at_w_qkv", "at_w_o"], mlp0=["up0", "down0", "rg_w_out"],
                 rg_in=["rg_w_in"], gates=["rg_w_a", "rg_w_x"])
    scatters, swaps, P, Q, res = {}, [], {}, {}, {}

    def start_scatter(group, grads):
        srcs = [g.reshape(N_CHIPS, -1, g.shape[-1]) for g in grads]
        lands = [lax.empty((N_CHIPS - 1,) + s.shape[1:], s.dtype) for s in srcs]
        scatters[group], token = _exchange_start("scatter", srcs, lands, f"scatter_{group}_start")
        return token

    def settle(groups, after):
        keys, parts = [], []
        for group in groups:
            srcs, lands = _exchange_wait("scatter", scatters[group], after, f"scatter_{group}_wait")
            for k, s, r in zip(names[group], srcs, lands):
                keys.append(k)
                parts.append(_sum_slots(s, r, me, f"sum_{k}"))
        handle, token = _exchange_start("swap", parts, [lax.empty(p.shape, F32) for p in parts],
                                        f"swap_{groups[0]}_start")
        swaps.append((keys, handle, f"swap_{groups[0]}_wait"))
        return token

    def finish(after):
        for keys, handle, name in swaps:
            mine, theirs = _exchange_wait("swap", handle, after, name)
            P.update(zip(keys, mine))
            Q.update(zip(keys, theirs))
        swaps.clear()
        last = after
        for k, parts in _BIG.items():
            if k in res or any(p not in P for p in parts):
                continue
            shape = w[k].shape
            two_d = lambda a: a.reshape(-1, shape[-1])
            outs = _adamw(two_d(w[k]), two_d(m[k]), two_d(v[k]), [P[p] for p in parts], [Q[p] for p in parts],
                          f"adamw_{k}")
            res[k] = [o.reshape(shape) for o in outs]
            last = outs[0]
        if "rg_w_a" in P and "gates" not in gathers:
            lands = [_cast_into_slot(P[k], 0, P[k].shape[0], me, F32, f"place_{k}", after=last, add=Q[k])
                     for k in names["gates"]]
            gathers["gates"], last = _exchange_start("gather", [], lands, "gather_gates_start", after=last)
        return last

    def emit(event, arrays):
        if event == "point_attn_done":
            return settle(["mlp1"], arrays[0])
        if event == "point_mix_done":
            return settle(["att", "mlp0"], arrays[0])
        token = start_scatter(event, arrays)
        if event == "rg_in":
            return finish(settle(["gates"], token))
        return token

    P_vec = dict(norm_mix_g=w["norm_mix_g"], norm_mlp_g=w["norm_mlp_g"], final_g=w["final_g"][None],
                 q_g=w["at_q_g"], k_g=w["at_k_g"])
    grad_x, vec_part = _local_step(x.reshape(T, D_MODEL), target.reshape(T, D_MODEL), P_vec, fetch, emit, B, L,
                                   after=tok)

    me8 = 2 * me + lax.axis_index("c")
    vec_slots = _cast_into_slot(vec_part, 0, VEC_ROWS, me8, F32, "place_vec", n_slots=N_DEVICES)
    spread, tok = _exchange_start("spread", [], [vec_slots], "spread_vec_start")
    last = finish(settle(["rg_in"], tok))
    _, gate_grads = _exchange_wait("gather", gathers["gates"], last, "gather_gates_wait")
    for k, g in zip(names["gates"], gate_grads):
        two_d = lambda a: a.reshape(g.shape[0] * g.shape[1], g.shape[2])
        outs = _adamw(two_d(w[k]), two_d(m[k]), two_d(v[k]), [two_d(g)], None, f"adamw_{k}")
        res[k] = [o.reshape(w[k].shape) for o in outs]
        last = outs[0]
    _, (vec_all,) = _exchange_wait("spread", spread, last, "spread_vec_wait")
    vec_grad = _sum_leading(vec_all, "sum_vec")
    loss = vec_grad[LOSS_ROW, 0]
    outs = _adamw(*packs, [vec_grad], None, "adamw_vec")
    unpacked = [_unpack_vec(o, me) for o in outs]
    for k in _WEIGHTS:
        if k not in res:
            res[k] = [u[k] for u in unpacked]

    result = [loss, grad_x.reshape(B, L, D_MODEL)]
    for slot in range(4):
        result += [res[k][slot] for k in _WEIGHTS]
    return tuple(result)
```

```python
import functools
import math

import jax
import jax.numpy as jnp
import numpy as np
from jax import lax
from jax.experimental import pallas as pl
from jax.experimental.pallas import tpu as pltpu

F32 = jnp.float32
BF16 = jnp.bfloat16

D_MODEL = 1024
HEAD_DIM = 128
N_HEADS = 8
N_KV = 2
GROUP = N_HEADS // N_KV
LRU_BLOCKS = 8
LRU_BW = 128
GRID_W = 64
ROPE_THETA = 10000.0
EPS = 1e-6
RG_C = 8.0
SCALE = 1.0 / math.sqrt(HEAD_DIM)
N_CHIPS = 4

ADAM_LR = 0.001
ADAM_B1 = 0.9
ADAM_B2 = 0.999
ADAM_EPS = 1e-08
ADAM_WD = 0.01
ADAM_STEP = 10

V7X_VMEM_BYTES = 64 * 1024 * 1024
VMEM_LIMIT = V7X_VMEM_BYTES * 3 // 4
LANES = 128
SUBLANES = 8

N_DEVICES = 8
VEC_ROWS = 32
LOSS_ROW = 5


def _params(sem):
    return pltpu.CompilerParams(dimension_semantics=sem, vmem_limit_bytes=VMEM_LIMIT)


_ANY = pl.BlockSpec(memory_space=pl.ANY)
_NN = (((1,), (0,)), ((), ()))
_NT = (((1,), (1,)), ((), ()))
_TN = (((0,), (0,)), ((), ()))


def _after_operand(after):
    return [] if after is None else [after]


def _fit(t, n):
    if n <= t:
        return n
    c = (t // LANES) * LANES
    while n % c:
        c -= LANES
    return c


MM_VMEM_BUDGET = VMEM_LIMIT * 3 // 4
def _mm_tiles(M, K, ns, n_total, out_dtypes, extras, whole_rows):
    for tm in (2048, 1024, 512, 256, 128):
        for tn in ((ns,) if whole_rows else (1024, 512, 256)):
            tn = _fit(tn, ns)
            per_row = 2 * (2 * K) + 4 * tn + sum(2 * tn * jnp.dtype(d).itemsize for d in out_dtypes)
            per_row += sum(2 * tn * e.dtype.itemsize for e in extras)
            b_buffers = 1 if tn == n_total else 2
            if M % tm == 0 and b_buffers * (2 * K * tn) + tm * per_row <= MM_VMEM_BUDGET:
                return tm, tn
    raise ValueError(f"no tile fits VMEM for M={M} K={K} N={ns}")


def _mm(a, b, *, mode, name, out_dtypes=(F32,), b_shard=False, o_shard=False, extras=(), epi=None, after=None,
        bcast=(), accs=(), ref_epi=None, out_cols=None):
    if mode == "tn":
        K, M = a.shape
        N = b.shape[1]
    else:
        M, K = a.shape
        if mode == "nn":
            N = b.shape[0] * b.shape[2] if b_shard else b.shape[1]
        else:
            N = b.shape[1] if b_shard else b.shape[0]
    ns = N
    if b_shard and mode == "nn":
        ns = b.shape[2]
    elif o_shard:
        ns = N // N_CHIPS
    tm, tn = _mm_tiles(M, K, ns, N, out_dtypes, extras, whole_rows=ref_epi is not None)
    if ref_epi is not None:
        tm = min(tm, 512)
    grid = (M // tm, N // tn)
    q = ns // tn
    once = dict(pipeline_mode=pl.Buffered(1)) if tn == N else {}

    if mode == "tn":
        a_spec = pl.BlockSpec((K, tm), lambda i, j: (0, i))
        b_spec = pl.BlockSpec((K, tn), lambda i, j: (0, j), **once)
        dims = _TN
    elif mode == "nn":
        a_spec = pl.BlockSpec((tm, K), lambda i, j: (i, 0))
        if b_shard:
            b_spec = pl.BlockSpec((None, K, tn), lambda i, j: (j // q, 0, j % q), **once)
        else:
            b_spec = pl.BlockSpec((K, tn), lambda i, j: (0, j), **once)
        dims = _NN
    else:
        a_spec = pl.BlockSpec((tm, K), lambda i, j: (i, 0))
        if b_shard:
            ks = b.shape[2]
            b_spec = pl.BlockSpec((N_CHIPS, tn, ks), lambda i, j: (0, j, 0), **once)
        else:
            b_spec = pl.BlockSpec((tn, K), lambda i, j: (j, 0), **once)
        dims = _NT

    if o_shard:
        o_specs = [pl.BlockSpec((None, tm, tn), lambda i, j: (j // q, i, j % q))]
        o_shapes = [jax.ShapeDtypeStruct((N_CHIPS, M, ns), out_dtypes[0])]
    else:
        o_specs = [pl.BlockSpec((tm, tn), lambda i, j: (i, j)) for _ in out_dtypes]
        o_shapes = [jax.ShapeDtypeStruct((M, N if out_cols is None else out_cols[n]), dt)
                    for n, dt in enumerate(out_dtypes)]
    e_specs = [pl.BlockSpec((tm, tn), lambda i, j: (i, j)) for _ in extras]
    e_specs += [pl.BlockSpec(v.shape, lambda i, j: (0, 0)) for v in bcast]
    o_specs += [pl.BlockSpec(s, lambda i, j: (0, 0)) for s in accs]
    o_shapes += [jax.ShapeDtypeStruct(s, F32) for s in accs]
    n_e, n_b, n_o, n_a = len(extras), len(bcast), len(out_dtypes), len(accs)
    order = _after_operand(after)
    n_x = len(order)
    if epi is None:
        epi = lambda acc: (acc,)

    def body(a_ref, b_ref, *rest):
        e_refs, b_refs = rest[:n_e], rest[n_e:n_e + n_b]
        o_refs = rest[n_e + n_b + n_x:n_e + n_b + n_x + n_o]
        a_refs = rest[n_e + n_b + n_x + n_o:]
        if n_a:
            @pl.when((pl.program_id(0) == 0) & (pl.program_id(1) == 0))
            def _():
                for r in a_refs:
                    r[...] = jnp.zeros(r.shape, F32)
        if mode == "nt" and b_shard:
            acc = None
            for s in range(N_CHIPS):
                part = lax.dot_general(a_ref[:, s * ks:(s + 1) * ks], b_ref[s], dims, preferred_element_type=F32)
                acc = part if acc is None else acc + part
        else:
            acc = lax.dot_general(a_ref[...], b_ref[...], dims, preferred_element_type=F32)
        if ref_epi is not None:
            ref_epi(acc, e_refs, b_refs, o_refs, a_refs)
            return
        outs = epi(acc, *[r[...] for r in e_refs])
        for r, o in zip(o_refs, outs):
            r[...] = o.astype(r.dtype)

    outs = pl.pallas_call(
        body, name=name, grid=grid, in_specs=[a_spec, b_spec] + e_specs + [_ANY] * n_x, out_specs=o_specs,
        out_shape=o_shapes, compiler_params=_params(("arbitrary", "arbitrary") if n_a else ("parallel", "parallel")),
    )(a, b, *extras, *bcast, *order)
    return outs[0] if n_o + n_a == 1 else outs


def _rowwise(fn, rows, bcast, outs, accs=(), *, tm, name, after=None):
    def norm(r):
        return r if isinstance(r, tuple) else (r, r.shape[1], 0)

    rows = [norm(r) for r in rows]
    T = rows[0][0].shape[0]
    tm = min(tm, T)
    while T % tm:
        tm -= SUBLANES
    n_r, n_b, n_o, n_a = len(rows), len(bcast), len(outs), len(accs)
    order = _after_operand(after)
    n_x = len(order)
    in_specs = [pl.BlockSpec((tm, c), functools.partial(lambda i, cb: (i, cb), cb=cb)) for _, c, cb in rows]
    in_specs += [pl.BlockSpec(b.shape, lambda i: (0, 0)) for b in bcast] + [_ANY] * n_x
    out_specs = [pl.BlockSpec((tm, o[0]), lambda i: (i, 0)) for o in outs]
    out_specs += [pl.BlockSpec(s, lambda i: (0, 0)) for s in accs]
    out_shape = [jax.ShapeDtypeStruct((T, o[2] if len(o) > 2 else o[0]), o[1]) for o in outs]
    out_shape += [jax.ShapeDtypeStruct(s, F32) for s in accs]

    def body(*refs):
        in_refs = refs[:n_r]
        b_refs = refs[n_r:n_r + n_b]
        o_refs = refs[n_r + n_b + n_x:n_r + n_b + n_x + n_o]
        a_refs = refs[n_r + n_b + n_x + n_o:]
        if n_a:
            @pl.when(pl.program_id(0) == 0)
            def _():
                for r in a_refs:
                    r[...] = jnp.zeros(r.shape, F32)
        fn(in_refs, b_refs, o_refs, a_refs)

    res = pl.pallas_call(
        body, name=name, grid=(T // tm,), in_specs=in_specs, out_specs=out_specs, out_shape=out_shape,
        compiler_params=_params(("arbitrary",) if n_a else ("parallel",)),
    )(*[r[0] for r in rows], *bcast, *order)
    return res


def _rsum(x):
    return jnp.sum(x, axis=0, keepdims=True)


def _rms_fwd(x, g, name, after=None):
    def fn(ins, bs, outs, accs):
        xv = ins[0][...]
        r = lax.rsqrt(jnp.mean(xv * xv, axis=-1, keepdims=True) + EPS)
        outs[0][...] = (xv * r * bs[0][...]).astype(BF16)

    return _rowwise(fn, [x], [g], [(D_MODEL, BF16)], tm=512, name=name, after=after)[0]


def _rms_bwd_math(xv, dh, g):
    r = lax.rsqrt(jnp.mean(xv * xv, axis=-1, keepdims=True) + EPS)
    hn = xv * r
    dgh = dh * g
    dx = r * (dgh - hn * jnp.mean(dgh * hn, axis=-1, keepdims=True))
    return dx, _rsum(dh * hn)


def _mm_norm_bwd(dy, w, x, dres, g, name, after=None):
    def epilogue(acc, e_refs, b_refs, o_refs, a_refs):
        dx, dg = _rms_bwd_math(e_refs[0][...], acc, b_refs[0][...])
        dx = dx + e_refs[1][...]
        o_refs[0][...] = dx
        o_refs[1][...] = dx.astype(BF16)
        a_refs[0][...] += dg

    return _mm(dy, w, mode="nt", b_shard=True, out_dtypes=(F32, BF16), extras=(x, dres), bcast=(g,),
               accs=((1, D_MODEL),), ref_epi=epilogue, name=name, after=after)


def _mm_res_norm(a, w, res, g, name):
    def epilogue(acc, e_refs, b_refs, o_refs, a_refs):
        xv = acc + e_refs[0][...]
        o_refs[0][...] = xv
        r = lax.rsqrt(jnp.mean(xv * xv, axis=-1, keepdims=True) + EPS)
        o_refs[1][...] = (xv * r * b_refs[0][...]).astype(BF16)

    return _mm(a, w, mode="nn", out_dtypes=(F32, BF16), extras=(res,), bcast=(g,), ref_epi=epilogue, name=name)


def _mm_final_loss(a, w, res, target, g, name):
    def epilogue(acc, e_refs, b_refs, o_refs, a_refs):
        xv = acc + e_refs[0][...]
        gv = b_refs[0][...]
        r = lax.rsqrt(jnp.mean(xv * xv, axis=-1, keepdims=True) + EPS)
        e = xv * r * gv - e_refs[1][...]
        tok = jnp.mean(e * e, axis=-1, keepdims=True)
        a_refs[0][...] += 0.5 * jnp.sum(tok, axis=0, keepdims=True) * jnp.ones((1, LANES), F32)
        dx, dg = _rms_bwd_math(xv, e * (1.0 / D_MODEL), gv)
        o_refs[0][...] = dx
        o_refs[1][...] = dx.astype(BF16)
        a_refs[1][...] += dg

    return _mm(a, w, mode="nn", out_dtypes=(F32, BF16), extras=(res, target), bcast=(g,),
               accs=((1, LANES), (1, D_MODEL)), ref_epi=epilogue, name=name)


def _relu2(acc):
    r = jnp.maximum(acc, 0.0)
    return r * r, r


def _mlp_fwd(x, h, fetch, tag, finish):
    w_up = fetch(f"mlp{tag}_up", h)
    a, r = _mm(h, w_up, mode="nn", b_shard=True, out_dtypes=(BF16, BF16), epi=_relu2, name=f"mlp{tag}_up")
    w_down = fetch(f"mlp{tag}_down", a)
    return finish(a, w_down, x, f"mlp{tag}_down"), (h, a, r, w_up, w_down)


def _mlp_bwd(x, g, saved, dx, dx_bf, tag, after):
    h, a, r, w_up, w_down = saved
    d_down = _mm(a, dx_bf, mode="tn", out_dtypes=(BF16,), name=f"mlp{tag}_dwdown", after=after)
    dup = _mm(dx_bf, w_down, mode="nt", extras=(r,), out_dtypes=(BF16,),
              epi=lambda acc, rv: (acc * (2.0 * rv.astype(F32)),), name=f"mlp{tag}_dup")
    d_up = _mm(h, dup, mode="tn", o_shard=True, out_dtypes=(BF16,), name=f"mlp{tag}_dwup")
    dx_new, dx_new_bf, dg = _mm_norm_bwd(dup, w_up, x, dx, g, f"mlp{tag}_dh")
    return dx_new, dx_new_bf, dg, d_up, d_down


def _rope_tables(L, B):
    rows = L // GRID_W
    row = np.repeat(np.arange(rows, dtype=np.float32), GRID_W)
    col = np.tile(np.arange(GRID_W, dtype=np.float32), rows)
    inv = (ROPE_THETA ** (-np.arange(HEAD_DIM // 4, dtype=np.float32) / (HEAD_DIM // 4))).astype(np.float32)
    ar, ac = row[:, None] * inv, col[:, None] * inv
    cos = np.concatenate([np.cos(ar), np.cos(ar), np.cos(ac), np.cos(ac)], axis=-1)
    sin = np.concatenate([-np.sin(ar), np.sin(ar), -np.sin(ac), np.sin(ac)], axis=-1)
    return jnp.asarray(np.tile(cos, (B, 1)), F32), jnp.asarray(np.tile(sin, (B, 1)), F32)


def _swap_halves(x):
    lane = lax.broadcasted_iota(jnp.int32, x.shape, 1)
    return jnp.where((lane % 64) < 32, pltpu.roll(x, HEAD_DIM - 32, 1), pltpu.roll(x, 32, 1))


def _qk_prep(qkv, cos, sin, q_g, k_g):
    def fn(ins, bs, outs, accs):
        c, s = ins[1][...], ins[2][...]
        for h in range(N_HEADS + N_KV):
            xv = ins[0][:, h * HEAD_DIM:(h + 1) * HEAD_DIM]
            g = bs[0][...] if h < N_HEADS else bs[1][...]
            r = lax.rsqrt(jnp.mean(xv * xv, axis=-1, keepdims=True) + EPS)
            z = xv * r * g
            y = (z * c + _swap_halves(z) * s).astype(BF16)
            if h < N_HEADS:
                outs[0][:, h * HEAD_DIM:(h + 1) * HEAD_DIM] = y
            else:
                outs[1][:, (h - N_HEADS) * HEAD_DIM:(h - N_HEADS + 1) * HEAD_DIM] = y
        outs[2][...] = ins[0][:, (N_HEADS + N_KV) * HEAD_DIM:].astype(BF16)

    kvw = N_KV * HEAD_DIM
    return _rowwise(fn, [qkv, cos, sin], [q_g, k_g], [(D_MODEL, BF16), (kvw, BF16), (kvw, BF16)], tm=512,
                    name="attn_qk_prep")


def _qk_prep_bwd(qkv, dq, dk, dv, cos, sin, q_g, k_g):
    def fn(ins, bs, outs, accs):
        c, s = ins[4][...], ins[5][...]
        for h in range(N_HEADS + N_KV):
            sl = slice(h * HEAD_DIM, (h + 1) * HEAD_DIM)
            xv = ins[0][:, sl]
            if h < N_HEADS:
                g, dy, acc = bs[0][...], ins[1][:, sl], accs[0]
            else:
                ks = slice((h - N_HEADS) * HEAD_DIM, (h - N_HEADS + 1) * HEAD_DIM)
                g, dy, acc = bs[1][...], ins[2][:, ks], accs[1]
            r = lax.rsqrt(jnp.mean(xv * xv, axis=-1, keepdims=True) + EPS)
            xn = xv * r
            dz = dy * c - _swap_halves(dy) * s
            acc[...] += _rsum(dz * xn)
            dxn = dz * g
            outs[0][:, sl] = (r * (dxn - xn * jnp.mean(dxn * xn, axis=-1, keepdims=True))).astype(BF16)
        outs[0][:, (N_HEADS + N_KV) * HEAD_DIM:] = ins[3][...].astype(BF16)

    return _rowwise(fn, [qkv, dq, dk, dv, cos, sin], [q_g, k_g], [(qkv.shape[1], BF16)],
                    [(1, HEAD_DIM), (1, HEAD_DIM)], tm=256, name="attn_qk_prep_bwd")


_EXP2_SCALE = SCALE * math.log2(math.e)


def _exp_rows(q, k):
    s = lax.dot_general(q, k, _NT, preferred_element_type=F32)
    p = jnp.exp2((s - jnp.max(s, axis=-1, keepdims=True)) * _EXP2_SCALE)
    return p, jnp.sum(p, axis=-1, keepdims=True)


def _attn_fwd(q, k, v, B, L, tq=2048, sub=256):
    tq = min(tq, L)
    sub = min(sub, tq)
    nq = L // tq

    def body(q_ref, k_ref, v_ref, o_ref):
        kv, vv = k_ref[...], v_ref[...]
        for c in range(tq // sub):
            rows = slice(c * sub, (c + 1) * sub)
            p, l = _exp_rows(q_ref[rows, :], kv)
            o = jnp.dot(p.astype(BF16), vv, preferred_element_type=F32)
            o_ref[rows, :] = (o * (1.0 / l)).astype(o_ref.dtype)

    return pl.pallas_call(
        body, name="attn_fwd", grid=(B, N_HEADS, nq),
        in_specs=[pl.BlockSpec((tq, HEAD_DIM), lambda b, h, i: (b * nq + i, h)),
                  pl.BlockSpec((L, HEAD_DIM), lambda b, h, i: (b, h // GROUP)),
                  pl.BlockSpec((L, HEAD_DIM), lambda b, h, i: (b, h // GROUP))],
        out_specs=pl.BlockSpec((tq, HEAD_DIM), lambda b, h, i: (b * nq + i, h)),
        out_shape=jax.ShapeDtypeStruct((B * L, D_MODEL), BF16),
        compiler_params=_params(("parallel", "parallel", "parallel")),
    )(q, k, v)


def _attn_bwd(q, k, v, do, B, L, tq=2048, sub=512):
    tq = min(tq, L)
    sub = min(sub, tq)
    nq = L // tq

    def body(q_ref, k_ref, v_ref, do_ref, dq_ref, dk_ref, dv_ref):
        @pl.when((pl.program_id(2) == 0) & (pl.program_id(3) == 0))
        def _():
            dk_ref[...] = jnp.zeros(dk_ref.shape, F32)
            dv_ref[...] = jnp.zeros(dv_ref.shape, F32)

        kv, vv = k_ref[...], v_ref[...]
        ps, es, dos, qs = [], [], [], []
        for c in range(tq // sub):
            rows = slice(c * sub, (c + 1) * sub)
            qc, doc = q_ref[rows, :], do_ref[rows, :]
            p, l = _exp_rows(qc, kv)
            inv = 1.0 / l
            dp = lax.dot_general(doc, vv, _NT, preferred_element_type=F32)
            delta = jnp.sum(p * dp, axis=-1, keepdims=True) * inv
            e = (p * (dp - delta)).astype(BF16)
            dq_ref[rows, :] = jnp.dot(e, kv, preferred_element_type=F32) * (inv * SCALE)
            ps.append(p.astype(BF16))
            es.append(e)
            dos.append((doc.astype(F32) * inv).astype(BF16))
            qs.append((qc.astype(F32) * (inv * SCALE)).astype(BF16))
        cat = lambda xs: xs[0] if len(xs) == 1 else jnp.concatenate(xs, axis=0)
        dv_ref[...] += lax.dot_general(cat(ps), cat(dos), _TN, preferred_element_type=F32)
        dk_ref[...] += lax.dot_general(cat(es), cat(qs), _TN, preferred_element_type=F32)

    qmap = lambda b, kh, g, i: (b * nq + i, kh * GROUP + g)
    kmap = lambda b, kh, g, i: (b, kh)
    kvw = N_KV * HEAD_DIM
    return pl.pallas_call(
        body, name="attn_bwd", grid=(B, N_KV, GROUP, nq),
        in_specs=[pl.BlockSpec((tq, HEAD_DIM), qmap), pl.BlockSpec((L, HEAD_DIM), kmap),
                  pl.BlockSpec((L, HEAD_DIM), kmap), pl.BlockSpec((tq, HEAD_DIM), qmap)],
        out_specs=[pl.BlockSpec((tq, HEAD_DIM), qmap), pl.BlockSpec((L, HEAD_DIM), kmap),
                   pl.BlockSpec((L, HEAD_DIM), kmap)],
        out_shape=[jax.ShapeDtypeStruct((B * L, D_MODEL), F32), jax.ShapeDtypeStruct((B * L, kvw), F32),
                   jax.ShapeDtypeStruct((B * L, kvw), F32)],
        compiler_params=_params(("parallel", "parallel", "arbitrary", "arbitrary")),
    )(q, k, v, do)


def _conv_shift(x, t, L, k):
    if k == 2:
        return x
    if k < 2:
        return jnp.where(t >= 2 - k, pltpu.roll(x, 2 - k, 0), 0.0)
    return jnp.where(t < L - (k - 2), pltpu.roll(x, L - (k - 2), 0), 0.0)


def _conv_apply(x, w_ref, L):
    t = lax.broadcasted_iota(jnp.int32, x.shape, 0)
    acc = w_ref[4:5, :] + w_ref[2:3, :] * x
    for k in (0, 1, 3):
        acc = acc + w_ref[k:k + 1, :] * _conv_shift(x, t, L, k)
    return acc


def _conv_fwd(z, wb, B, L, tc=256):
    noff = D_MODEL // tc

    def body(z_ref, w_ref, o_ref):
        o_ref[...] = _conv_apply(z_ref[...], w_ref, L)

    return pl.pallas_call(
        body, name="rg_conv", grid=(B, noff),
        in_specs=[pl.BlockSpec((L, tc), lambda b, j: (b, noff + j)), pl.BlockSpec((SUBLANES, tc), lambda b, j: (0, j))],
        out_specs=pl.BlockSpec((L, tc), lambda b, j: (b, j)),
        out_shape=jax.ShapeDtypeStruct((B * L, D_MODEL), F32),
        compiler_params=_params(("parallel", "parallel")),
    )(z, wb)


def _conv_bwd(z, g, wb, dz, B, L, tc=256, after=None):
    noff = D_MODEL // tc
    order = _after_operand(after)

    def body(z_ref, g_ref, w_ref, dz_in, *rest):
        dx_ref, dw_ref = rest[len(order):]

        @pl.when(pl.program_id(1) == 0)
        def _():
            dw_ref[...] = jnp.zeros(dw_ref.shape, F32)

        x, gv = z_ref[...], g_ref[...]
        t = lax.broadcasted_iota(jnp.int32, x.shape, 0)
        dx = w_ref[2:3, :] * gv
        for k in (0, 1, 3):
            dx = dx + w_ref[k:k + 1, :] * _conv_shift(gv, t, L, 4 - k)
        dx_ref[...] = dx.astype(BF16)
        for k in range(4):
            dw_ref[k:k + 1, :] += _rsum(_conv_shift(x, t, L, k) * gv)
        dw_ref[4:5, :] += _rsum(gv)

    return pl.pallas_call(
        body, name="rg_conv_bwd", grid=(noff, B),
        in_specs=[pl.BlockSpec((L, tc), lambda j, b: (b, noff + j)), pl.BlockSpec((L, tc), lambda j, b: (b, j)),
                  pl.BlockSpec((SUBLANES, tc), lambda j, b: (0, j)), _ANY] + [_ANY] * len(order),
        out_specs=[pl.BlockSpec((L, tc), lambda j, b: (b, noff + j)),
                   pl.BlockSpec((SUBLANES, tc), lambda j, b: (0, j))],
        out_shape=[jax.ShapeDtypeStruct(dz.shape, dz.dtype), jax.ShapeDtypeStruct((SUBLANES, D_MODEL), F32)],
        input_output_aliases={3: 0},
        compiler_params=_params(("parallel", "arbitrary")),
    )(z, g, wb, dz, *order)


def _softplus(x):
    return jnp.maximum(x, 0.0) + jnp.log1p(jnp.exp(-jnp.abs(x)))


_ROW_BA, _ROW_BX, _ROW_LAM = 0, 2, 4


def _gate_math(xb, pre, vec_ref, d, sl):
    pa = pre[:, (2 * d) * LRU_BW:(2 * d + 1) * LRU_BW] + vec_ref[_ROW_BA + d:_ROW_BA + d + 1, sl]
    px = pre[:, (2 * d + 1) * LRU_BW:(2 * d + 2) * LRU_BW] + vec_ref[_ROW_BX + d:_ROW_BX + d + 1, sl]
    r = 0.5 * jnp.tanh(0.5 * pa) + 0.5
    i = 0.5 * jnp.tanh(0.5 * px) + 0.5
    sp = _softplus(-vec_ref[_ROW_LAM + d:_ROW_LAM + d + 1, sl])
    log_a = (-RG_C) * r * sp
    a = jnp.exp(log_a)
    th = jnp.tanh(log_a)
    om = -2.0 * th / (1.0 - th)
    mult = jnp.sqrt(om)
    return a, mult * (i * xb), (r, i, sp, om, mult)


def _gate_fwd(rec, wcat, gvec):
    def fn(ins, bs, outs, accs):
        for blk in range(LRU_BLOCKS):
            sl = slice(blk * LRU_BW, (blk + 1) * LRU_BW)
            xb = ins[0][:, sl]
            pre = jnp.dot(xb.astype(BF16), bs[0][sl, :], preferred_element_type=F32)
            for d in range(2):
                a, u, _ = _gate_math(xb, pre, bs[1], d, sl)
                outs[2 * d][:, sl] = a
                outs[2 * d + 1][:, sl] = u

    return _rowwise(fn, [rec], [wcat, gvec], [(D_MODEL, F32)] * 4, tm=256, name="rg_gate")


def _gate_bwd(rec, du_f, da_f, du_b, da_b, wcat, gvec):
    def fn(ins, bs, outs, accs):
        for blk in range(LRU_BLOCKS):
            sl = slice(blk * LRU_BW, (blk + 1) * LRU_BW)
            xb = ins[0][:, sl]
            xb16 = xb.astype(BF16)
            w = bs[0][sl, :]
            pre = jnp.dot(xb16, w, preferred_element_type=F32)
            dx = jnp.zeros_like(xb)
            dpre = []
            for d in range(2):
                a, _, (r, i, sp, om, mult) = _gate_math(xb, pre, bs[1], d, sl)
                du, da = ins[1 + 2 * d][:, sl], ins[2 + 2 * d][:, sl]
                d_i = du * mult * xb
                d_mult = du * i * xb
                dx = dx + du * mult * i
                dlog = da * a - d_mult * (1.0 - om) / mult
                d_r = dlog * ((-RG_C) * sp)
                d_sp = _rsum(dlog * ((-RG_C) * r))
                lam = bs[1][_ROW_LAM + d:_ROW_LAM + d + 1, sl]
                accs[2][_ROW_LAM + d:_ROW_LAM + d + 1, sl] += d_sp * (-jax.nn.sigmoid(-lam))
                dpa = d_r * r * (1.0 - r)
                dpx = d_i * i * (1.0 - i)
                accs[2][_ROW_BA + d:_ROW_BA + d + 1, sl] += _rsum(dpa)
                accs[2][_ROW_BX + d:_ROW_BX + d + 1, sl] += _rsum(dpx)
                dpre += [dpa, dpx]
            dpre = jnp.concatenate(dpre, axis=1).astype(BF16)
            dw = lax.dot_general(xb16, dpre, _TN, preferred_element_type=F32)
            for d in range(2):
                rows = slice(d * D_MODEL + blk * LRU_BW, d * D_MODEL + (blk + 1) * LRU_BW)
                accs[0][rows, :] += dw[:, (2 * d) * LRU_BW:(2 * d + 1) * LRU_BW]
                accs[1][rows, :] += dw[:, (2 * d + 1) * LRU_BW:(2 * d + 2) * LRU_BW]
            outs[0][:, sl] = dx + lax.dot_general(dpre, w, _NT, preferred_element_type=F32)

    gate_shape = (2 * D_MODEL, LRU_BW)
    return _rowwise(fn, [rec, du_f, da_f, du_b, da_b], [wcat, gvec], [(D_MODEL, F32)],
                    [gate_shape, gate_shape, (SUBLANES, D_MODEL)], tm=256, name="rg_gate_bwd")


def _as_time_blocks(x):
    return x.reshape(x.shape[0] // SUBLANES, SUBLANES, x.shape[1])


def _scan_call(body, ins, n_out, B, L, tc, name):
    nb = L // SUBLANES
    spec = pl.BlockSpec((nb, SUBLANES, tc), lambda b, j: (b, 0, j))
    T = ins[0].shape[0]
    outs = pl.pallas_call(
        functools.partial(body, nb), name=name, grid=(B, D_MODEL // tc),
        in_specs=[spec] * len(ins), out_specs=[spec] * n_out,
        out_shape=[jax.ShapeDtypeStruct((T // SUBLANES, SUBLANES, D_MODEL), F32)] * n_out,
        compiler_params=_params(("parallel", "parallel")),
    )(*[_as_time_blocks(x) for x in ins])
    return [o.reshape(T, D_MODEL) for o in outs]


def _block_scan(A, U, reverse):
    row = lax.broadcasted_iota(jnp.int32, A.shape, 0)
    for s in (1, 2, 4):
        shift = SUBLANES - s if reverse else s
        valid = (row < SUBLANES - s) if reverse else (row >= s)
        a_sh = jnp.where(valid, pltpu.roll(A, shift, 0), 1.0)
        u_sh = jnp.where(valid, pltpu.roll(U, shift, 0), 0.0)
        U = A * u_sh + U
        A = A * a_sh
    return A, U


_LAST = SUBLANES - 1
SCAN_UNROLL = 8


def _loop_blocks(nb, step, init):
    def group(g, carry):
        for k in range(SCAN_UNROLL):
            carry = step(g * SCAN_UNROLL + k, carry)
        return carry

    return lax.fori_loop(0, nb // SCAN_UNROLL, group, init)


def _scan_fwd(a_f, u_f, a_b, u_b, B, L, tc=256):
    def body(nb, af, uf, ab, ub, hf, hb):
        def step(i, carry):
            c1, c2 = carry
            ib = nb - 1 - i
            p, h = _block_scan(af[i], uf[i], False)
            h = h + p * c1
            hf[i] = h
            p2, h2 = _block_scan(ab[ib], ub[ib], True)
            h2 = h2 + p2 * c2
            hb[ib] = h2
            return h[_LAST:, :], h2[:1, :]

        zero = jnp.zeros((1, tc), F32)
        _loop_blocks(nb, step, (zero, zero))

    return _scan_call(body, [a_f, u_f, a_b, u_b], 2, B, L, tc, "rg_scan")


def _scan_bwd(dy, a_f, h_f, a_b, h_b, B, L, tc=256):
    def body(nb, dy_r, af, hf, ab, hb, duf, daf, dub, dab):
        def step(i, carry):
            c1, c2 = carry
            ir = nb - 1 - i
            row = lax.broadcasted_iota(jnp.int32, (SUBLANES, tc), 0)
            a_up = jnp.where(row == _LAST, af[jnp.minimum(ir + 1, nb - 1), :1, :], pltpu.roll(af[ir], _LAST, 0))
            p, lam = _block_scan(a_up, dy_r[ir], True)
            lam = lam + p * c1
            before = hf[jnp.maximum(ir - 1, 0), _LAST:, :] * (ir > 0).astype(F32)
            duf[ir] = lam
            daf[ir] = lam * jnp.where(row == 0, before, pltpu.roll(hf[ir], 1, 0))
            a_dn = jnp.where(row == 0, ab[jnp.maximum(i - 1, 0), _LAST:, :], pltpu.roll(ab[i], 1, 0))
            p2, lam2 = _block_scan(a_dn, dy_r[i], False)
            lam2 = lam2 + p2 * c2
            after = hb[jnp.minimum(i + 1, nb - 1), :1, :] * (i < nb - 1).astype(F32)
            dub[i] = lam2
            dab[i] = lam2 * jnp.where(row == _LAST, after, pltpu.roll(hb[i], _LAST, 0))
            return lam[:1, :], lam2[_LAST:, :]

        zero = jnp.zeros((1, tc), F32)
        _loop_blocks(nb, step, (zero, zero))

    return _scan_call(body, [dy, a_f, h_f, a_b, h_b], 4, B, L, tc, "rg_scan_bwd")


_GELU_C = math.sqrt(2.0 / math.pi)


def _gelu_parts(x):
    th = jnp.tanh(_GELU_C * (x + 0.044715 * x * x * x))
    return 0.5 * x * (1.0 + th), th


def _gated_out(h_f, h_b, z):
    def fn(ins, bs, outs, accs):
        gl, _ = _gelu_parts(ins[2][...])
        outs[0][...] = ((ins[0][...] + ins[1][...]) * gl).astype(BF16)

    return _rowwise(fn, [h_f, h_b, (z, D_MODEL, 0)], [], [(D_MODEL, BF16)], tm=512, name="rg_gated_out")[0]


def _mm_gated_out_bwd(dx, w_out, h_f, h_b, z, name, after=None):
    def epilogue(acc, e_refs, b_refs, o_refs, a_refs):
        x = e_refs[2][...]
        gl, th = _gelu_parts(x)
        dgl = 0.5 * (1.0 + th) + 0.5 * x * (1.0 - th * th) * (_GELU_C * (1.0 + 3.0 * 0.044715 * x * x))
        o_refs[0][...] = acc * gl
        o_refs[1][...] = (acc * (e_refs[0][...] + e_refs[1][...]) * dgl).astype(BF16)

    return _mm(dx, w_out, mode="nt", out_dtypes=(F32, BF16), out_cols=(D_MODEL, 2 * D_MODEL), extras=(h_f, h_b, z),
               ref_epi=epilogue, name=name, after=after)


def _row_block(i):
    return pl.ds(pl.multiple_of(i * SUBLANES, SUBLANES), SUBLANES)


def _rg_mix_fwd(z, conv_wb, wcat, gvec, B, L):
    nb = L // SUBLANES
    n_g = D_MODEL // LRU_BW

    def body(zg_ref, zr_ref, cw_ref, w_ref, gv_ref, rec_ref, af_s, ab_s, hf_ref, hb_ref, yg_ref, uf_s, ub_s):
        rec = _conv_apply(zr_ref[...], cw_ref, L)
        rec_ref[...] = rec
        pre = jnp.dot(rec.astype(BF16), w_ref[...], preferred_element_type=F32)
        for d, (a_s, u_s) in enumerate(((af_s, uf_s), (ab_s, ub_s))):
            a, u, _ = _gate_math(rec, pre, gv_ref, d, slice(None))
            a_s[...] = a
            u_s[...] = u

        def step(i, carry):
            c1, c2 = carry
            rows, rows_b = _row_block(i), _row_block(nb - 1 - i)
            p, h = _block_scan(af_s[rows, :], uf_s[rows, :], False)
            h = h + p * c1
            hf_ref[rows, :] = h
            p2, h2 = _block_scan(ab_s[rows_b, :], ub_s[rows_b, :], True)
            h2 = h2 + p2 * c2
            hb_ref[rows_b, :] = h2
            return h[_LAST:, :], h2[:1, :]

        zero = jnp.zeros((1, LRU_BW), F32)
        _loop_blocks(nb, step, (zero, zero))
        gl, _ = _gelu_parts(zg_ref[...])
        yg_ref[...] = ((hf_ref[...] + hb_ref[...]) * gl).astype(BF16)

    seq = lambda off: pl.BlockSpec((L, LRU_BW), lambda b, g: (b, off + g))
    vec = pl.BlockSpec((SUBLANES, LRU_BW), lambda b, g: (0, g))
    T = B * L
    return pl.pallas_call(
        body, name="rg_mix", grid=(B, n_g),
        in_specs=[seq(0), seq(n_g), vec, pl.BlockSpec((LRU_BW, 4 * LRU_BW), lambda b, g: (g, 0)), vec],
        out_specs=[seq(0)] * 6,
        out_shape=[jax.ShapeDtypeStruct((T, D_MODEL), F32)] * 5 + [jax.ShapeDtypeStruct((T, D_MODEL), BF16)],
        scratch_shapes=[pltpu.VMEM((L, LRU_BW), F32)] * 2,
        compiler_params=_params(("parallel", "parallel")),
    )(z, z, conv_wb, wcat, gvec)


def _make_wcat(w_a, w_x):
    g = jnp.stack([w_a[0, 0], w_x[0, 0], w_a[0, 1], w_x[0, 1]])
    return jnp.transpose(g, (1, 2, 0, 3)).reshape(D_MODEL, 4 * LRU_BW)


def _rows_at(part, first):
    return jnp.pad(part, ((first, SUBLANES - first - part.shape[0]), (0, 0)))


def _qk_slot(q_g, k_g):
    wide = lambda v, at: jnp.pad(v, ((0, SUBLANES - 1), (at, D_MODEL - at - HEAD_DIM)))
    return wide(q_g, 0) + wide(k_g, HEAD_DIM)


def _local_step(x, target, P, fetch, emit, B, L, after=None):
    g_mix, g_mlp = P["norm_mix_g"], P["norm_mlp_g"]
    h0 = _rms_fwd(x, g_mix[0:1], "rg_norm", after=after)
    w_in, conv_wb, wcat, gvec = fetch("rg", h0)
    z = _mm(h0, w_in, mode="nn", b_shard=True, name="rg_in")
    rec, a_f, a_b, h_f, h_b, yg = _rg_mix_fwd(z, conv_wb, wcat, gvec, B, L)
    w_out = fetch("rg_out", yg)
    x1, h1 = _mm_res_norm(yg, w_out, x, g_mlp[0:1], "rg_out")
    (x2, h3), mlp0 = _mlp_fwd(x1, h1, fetch, 0, lambda a, w, res, name: _mm_res_norm(a, w, res, g_mix[1:2], name))
    w_qkv, w_o = fetch("att", h3)
    qkv = _mm(h3, w_qkv, mode="nn", b_shard=True, name="attn_qkv")
    cos, sin = _rope_tables(L, B)
    qh, kh, vh = _qk_prep(qkv, cos, sin, P["q_g"], P["k_g"])
    o = _attn_fwd(qh, kh, vh, B, L)
    x3, h4 = _mm_res_norm(o, w_o, x2, g_mlp[1:2], "attn_out")
    (dx4, dx4_bf, loss_acc, d_final_g), mlp1 = _mlp_fwd(
        x3, h4, fetch, 1, lambda a, w, res, name: _mm_final_loss(a, w, res, target, P["final_g"], name))

    dx3, dx3_bf, dg_mlp1, d_up1, d_down1 = _mlp_bwd(x3, g_mlp[1:2], mlp1, dx4, dx4_bf, 1, None)
    tok = emit("mlp1", [d_up1, d_down1])
    d_wo = _mm(o, dx3_bf, mode="tn", out_dtypes=(BF16,), name="attn_dwo", after=tok)
    do = _mm(dx3_bf, w_o, mode="nt", out_dtypes=(BF16,), name="attn_do")
    dq, dk, dv = _attn_bwd(qh, kh, vh, do, B, L)
    dqkv, dq_g, dk_g = _qk_prep_bwd(qkv, dq, dk, dv, cos, sin, P["q_g"], P["k_g"])
    d_wqkv = _mm(h3, dqkv, mode="tn", o_shard=True, out_dtypes=(BF16,), name="attn_dwqkv")
    tok = emit("att", [d_wqkv, d_wo])
    dx2, dx2_bf, dg_mix1 = _mm_norm_bwd(dqkv, w_qkv, x2, dx3, g_mix[1:2], "attn_dh", after=tok)
    tok = emit("point_attn_done", [dx2_bf])
    dx1, dx1_bf, dg_mlp0, d_up0, d_down0 = _mlp_bwd(x1, g_mlp[0:1], mlp0, dx2, dx2_bf, 0, tok)
    d_wout = _mm(yg, dx1_bf, mode="tn", out_dtypes=(BF16,), name="rg_dwout")
    tok = emit("mlp0", [d_up0, d_down0, d_wout])
    dy, dgate = _mm_gated_out_bwd(dx1_bf, w_out, h_f, h_b, z, "rg_dyg", after=tok)
    du_f, da_f, du_b, da_b = _scan_bwd(dy, a_f, h_f, a_b, h_b, B, L)
    drec_c, d_wa, d_wx, d_gvec = _gate_bwd(rec, du_f, da_f, du_b, da_b, wcat, gvec)
    tok = emit("gates", [d_wa, d_wx])
    dz, d_convwb = _conv_bwd(z, drec_c, conv_wb, dgate, B, L, after=tok)
    tok = emit("point_mix_done", [dz])
    d_win = _mm(h0, dz, mode="tn", o_shard=True, out_dtypes=(BF16,), name="rg_dwin", after=tok)
    tok = emit("rg_in", [d_win])
    grad_x, _, dg_mix0 = _mm_norm_bwd(dz, w_in, x, dx1, g_mix[0:1], "rg_dh", after=tok)

    norms = (_rows_at(dg_mix0, 0) + _rows_at(dg_mix1, 1) + _rows_at(dg_mlp0, 2) + _rows_at(dg_mlp1, 3)
             + _rows_at(d_final_g, 4)
             + jnp.pad(loss_acc, ((LOSS_ROW, SUBLANES - 1 - LOSS_ROW), (0, D_MODEL - LANES))))
    vec = jnp.concatenate([norms, d_convwb, d_gvec, _qk_slot(dq_g, dk_g)], axis=0)
    return grad_x, vec


_MESH = pl.DeviceIdType.MESH


def _place():
    x, y, c = lax.axis_index("x"), lax.axis_index("y"), lax.axis_index("c")
    peers = [((1 - x) if j & 2 else x, (1 - y) if j & 1 else y) for j in (1, 2, 3)]
    return x, y, c, peers


def _sum_leading(slots, name):
    def body(s_ref, o_ref):
        acc = s_ref[0]
        for d in range(1, slots.shape[0]):
            acc = acc + s_ref[d]
        o_ref[...] = acc

    return pl.pallas_call(body, name=name, out_shape=jax.ShapeDtypeStruct(slots.shape[1:], slots.dtype))(slots)


_HBM = pl.BlockSpec(memory_space=pltpu.HBM)
_SEM = pl.BlockSpec(memory_space=pltpu.SEMAPHORE)
_EFFECT = pltpu.SideEffectType.DATAFLOW_SIDE_EFFECTING


_COPIES = dict(gather=N_CHIPS - 1, scatter=N_CHIPS - 1, swap=1, spread=N_DEVICES - 1)


def _split_copies(kind, srcs, lands, send, recv):
    x, y, c, peers = _place()
    me = 2 * x + y
    per = _COPIES[kind]
    out = []
    for a in range(len(lands)):
        for j in range(per):
            if kind == "swap":
                src, there, here, dev = srcs[a], lands[a], lands[a], (x, y, 1 - c)
            elif kind == "spread":
                k = j + 1
                dev = ((1 - x) if k & 4 else x, (1 - y) if k & 2 else y, (1 - c) if k & 1 else c)
                mine = lands[a].at[4 * x + 2 * y + c]
                src, there, here = mine, mine, lands[a].at[4 * dev[0] + 2 * dev[1] + dev[2]]
            else:
                px, py = peers[j]
                dev = (px, py, c)
                if kind == "gather":
                    src, there, here = lands[a].at[me], lands[a].at[me], lands[a].at[2 * px + py]
                else:
                    src, there, here = srcs[a].at[2 * px + py], lands[a].at[j], lands[a].at[j]
            mk = functools.partial(
                pltpu.make_async_remote_copy, src_ref=src, send_sem=send.at[per * a + j],
                recv_sem=recv.at[per * a + j], device_id=dev, device_id_type=_MESH)
            out.append((functools.partial(mk, dst_ref=there), functools.partial(mk, dst_ref=here)))
    return out


def _exchange_start(kind, srcs, lands, name, after=None):
    arrays = list(srcs) + list(lands)
    n_s, n, n_all = len(srcs), len(lands), len(srcs) + len(lands)
    n_sem = _COPIES[kind] * n
    order = _after_operand(after)
    n_x = len(order)

    def body(*refs):
        send, recv = refs[n_all + n_x], refs[n_all + n_x + 1]
        token = refs[-1]
        for started, _ in _split_copies(kind, refs[:n_s], refs[n_s:n_all], send, recv):
            started().start()
        token[...] = jnp.zeros(token.shape, F32)

    res = pl.pallas_call(
        body, name=name,
        out_shape=(pltpu.SemaphoreType.DMA((n_sem,)), pltpu.SemaphoreType.DMA((n_sem,)),
                   *[pltpu.HBM(a.shape, a.dtype) for a in arrays], jax.ShapeDtypeStruct((SUBLANES, LANES), F32)),
        in_specs=[_HBM] * n_all + [_ANY] * n_x,
        out_specs=(_SEM, _SEM, *[_HBM] * n_all, pl.BlockSpec(memory_space=pltpu.VMEM)),
        input_output_aliases={i: 2 + i for i in range(n_all)},
        compiler_params=pltpu.CompilerParams(has_side_effects=_EFFECT),
    )(*[pltpu.with_memory_space_constraint(a, pltpu.HBM) for a in arrays], *order)
    return (res[0], res[1], res[2:2 + n_s], res[2 + n_s:2 + n_all]), res[-1]


def _gather_start_groups(land_groups, name, after=None):
    arrays = [a for group in land_groups for a in group]
    n_all, n_g = len(arrays), len(land_groups)
    order = _after_operand(after)
    n_x = len(order)

    def body(*refs):
        first = 0
        for gi, group in enumerate(land_groups):
            send, recv = refs[n_all + n_x + 2 * gi], refs[n_all + n_x + 2 * gi + 1]
            for started, _ in _split_copies("gather", [], refs[first:first + len(group)], send, recv):
                started().start()
            first += len(group)
        refs[-1][...] = jnp.zeros(refs[-1].shape, F32)

    sems = [pltpu.SemaphoreType.DMA((_COPIES["gather"] * len(group),)) for group in land_groups for _ in range(2)]
    res = pl.pallas_call(
        body, name=name,
        out_shape=(*sems, *[pltpu.HBM(a.shape, a.dtype) for a in arrays], jax.ShapeDtypeStruct((SUBLANES, LANES), F32)),
        in_specs=[_HBM] * n_all + [_ANY] * n_x,
        out_specs=(*[_SEM] * (2 * n_g), *[_HBM] * n_all, pl.BlockSpec(memory_space=pltpu.VMEM)),
        input_output_aliases={i: 2 * n_g + i for i in range(n_all)},
        compiler_params=pltpu.CompilerParams(has_side_effects=_EFFECT),
    )(*[pltpu.with_memory_space_constraint(a, pltpu.HBM) for a in arrays], *order)
    handles, first = [], 2 * n_g
    for gi, group in enumerate(land_groups):
        handles.append((res[2 * gi], res[2 * gi + 1], [], res[first:first + len(group)]))
        first += len(group)
    return handles, res[-1]


def _exchange_wait(kind, handle, after, name):
    send, recv, srcs, lands = handle
    arrays = list(srcs) + list(lands)
    n_s, n_all = len(srcs), len(arrays)
    order = list(after) if isinstance(after, (list, tuple)) else [after]

    def body(*refs):
        for started, landing in _split_copies(kind, refs[:n_s], refs[n_s:n_all], refs[n_all], refs[n_all + 1]):
            started().wait_send()
            landing().wait_recv()

    res = pl.pallas_call(
        body, name=name, out_shape=[pltpu.HBM(a.shape, a.dtype) for a in arrays],
        in_specs=[_HBM] * n_all + [_SEM, _SEM] + [_ANY] * len(order), out_specs=[_HBM] * n_all,
        input_output_aliases={i: i for i in range(n_all)},
        compiler_params=pltpu.CompilerParams(has_side_effects=_EFFECT),
    )(*arrays, send, recv, *order)
    return res[:n_s], res[n_s:]


def _index_operand(i):
    return jnp.reshape(i, (1,)).astype(jnp.int32)


def _cast_into_slot(src, row0, rows, me, dtype, name, after=None, add=None, n_slots=N_CHIPS):
    cols = src.shape[1]
    tm = min(512, rows)
    order = _after_operand(after)
    terms = [src] + ([] if add is None else [add])

    def body(me_ref, *rest):
        val = rest[0][...]
        if add is not None:
            val = val + rest[1][...]
        rest[-1][...] = val.astype(dtype)

    return pl.pallas_call(
        body, name=name,
        grid_spec=pltpu.PrefetchScalarGridSpec(
            num_scalar_prefetch=1, grid=(rows // tm,),
            in_specs=[pl.BlockSpec((tm, cols), lambda i, me_ref: (i + row0 // tm, 0))] * len(terms)
            + [_ANY] * len(order),
            out_specs=pl.BlockSpec((None, tm, cols), lambda i, me_ref: (me_ref[0], i, 0))),
        out_shape=jax.ShapeDtypeStruct((n_slots, rows, cols), dtype), compiler_params=_params(("parallel",)),
    )(_index_operand(me), *terms, *order)


def _sum_slots(mine, r, me, name):
    _, rows, cols = r.shape
    tm = min(512, rows)

    def body(me_ref, own_ref, r_ref, o_ref):
        o_ref[...] = ((own_ref[...].astype(F32) + r_ref[0].astype(F32)) + r_ref[1].astype(F32)) + r_ref[2].astype(F32)

    return pl.pallas_call(
        body, name=name,
        grid_spec=pltpu.PrefetchScalarGridSpec(
            num_scalar_prefetch=1, grid=(rows // tm,),
            in_specs=[pl.BlockSpec((None, tm, cols), lambda i, me_ref: (me_ref[0], i, 0)),
                      pl.BlockSpec((N_CHIPS - 1, tm, cols), lambda i, me_ref: (0, i, 0))],
            out_specs=pl.BlockSpec((tm, cols), lambda i, me_ref: (i, 0))),
        out_shape=jax.ShapeDtypeStruct((rows, cols), F32), compiler_params=_params(("parallel",)),
    )(_index_operand(me), mine, r)


def _adamw(w, m, v, ps, qs, name):
    rows, cols = w.shape
    seg_rows = ps[0].shape[0]
    tm = min(256, seg_rows)
    while seg_rows % tm:
        tm -= SUBLANES
    per, n_seg = seg_rows // tm, len(ps)
    parts = list(ps) + ([] if qs is None else list(qs))

    def body(w_ref, m_ref, v_ref, *rest):
        g_refs, outs = rest[:len(parts)], rest[len(parts):]
        grad = lambda s: g_refs[s][...] if qs is None else g_refs[s][...] + g_refs[n_seg + s][...]
        g = grad(0)
        for s in range(1, n_seg):
            g = jnp.where(pl.program_id(0) >= s * per, grad(s), g)
        m1 = ADAM_B1 * m_ref[...] + (1.0 - ADAM_B1) * g
        v1 = ADAM_B2 * v_ref[...] + (1.0 - ADAM_B2) * (g * g)
        m_hat = m1 / (1.0 - ADAM_B1 ** ADAM_STEP)
        v_hat = v1 / (1.0 - ADAM_B2 ** ADAM_STEP)
        outs[0][...] = g
        outs[1][...] = (-ADAM_LR) * (m_hat / (jnp.sqrt(v_hat) + ADAM_EPS) + ADAM_WD * w_ref[...])
        outs[2][...] = m1
        outs[3][...] = v1

    row_spec = pl.BlockSpec((tm, cols), lambda i: (i, 0))
    seg_spec = lambda s: pl.BlockSpec((tm, cols), lambda i: (jnp.clip(i - s * per, 0, per - 1), 0))
    return pl.pallas_call(
        body, name=name, grid=(rows // tm,),
        in_specs=[row_spec] * 3 + [seg_spec(s) for s in range(n_seg)] * (1 if qs is None else 2),
        out_specs=[row_spec] * 4, out_shape=[jax.ShapeDtypeStruct((rows, cols), F32)] * 4,
        compiler_params=_params(("arbitrary",)),
    )(w, m, v, *parts)


def _put_cols(shard, me):
    full = jnp.zeros((shard.shape[0], D_MODEL), F32)
    return lax.dynamic_update_slice(full, shard, (0, me * (D_MODEL // N_CHIPS)))


def _gate_vec_slot(b_a, b_x, lam):
    return _rows_at(b_a, _ROW_BA) + _rows_at(b_x, _ROW_BX) + _rows_at(lam, _ROW_LAM)


def _pack_vec(p, me):
    return jnp.concatenate([
        _rows_at(p["norm_mix_g"], 0) + _rows_at(p["norm_mlp_g"], 2) + _rows_at(p["final_g"][None], 4),
        _rows_at(_put_cols(p["rg_conv_w"][0, :, 0, :], me), 0) + _rows_at(p["rg_conv_b"], 4),
        _gate_vec_slot(_put_cols(p["rg_b_a"][0], me), _put_cols(p["rg_b_x"][0], me), _put_cols(p["rg_lam"][0], me)),
        _qk_slot(p["at_q_g"], p["at_k_g"]),
    ], axis=0)


def _unpack_vec(r, me):
    def cols(rows):
        return lax.dynamic_slice(rows, (0, me * (D_MODEL // N_CHIPS)), (rows.shape[0], D_MODEL // N_CHIPS))

    gate = r[16:24]
    return dict(
        norm_mix_g=r[0:2], norm_mlp_g=r[2:4], final_g=r[4], rg_conv_w=cols(r[8:12])[None, :, None, :],
        rg_conv_b=r[12:13], rg_b_a=cols(gate[_ROW_BA:_ROW_BA + 2])[None], rg_b_x=cols(gate[_ROW_BX:_ROW_BX + 2])[None],
        rg_lam=cols(gate[_ROW_LAM:_ROW_LAM + 2])[None], at_q_g=r[24:25, 0:HEAD_DIM],
        at_k_g=r[24:25, HEAD_DIM:2 * HEAD_DIM])


_WEIGHTS = ['norm_mix_g', 'norm_mlp_g', 'rg_w_in', 'rg_conv_w', 'rg_conv_b', 'rg_w_a', 'rg_b_a', 'rg_w_x', 'rg_b_x',
            'rg_lam', 'rg_w_out', 'at_w_qkv', 'at_q_g', 'at_k_g', 'at_w_o', 'mlp_w_up', 'mlp_w_down', 'final_g']
_BIG = dict(rg_w_in=["rg_w_in"], rg_w_out=["rg_w_out"], at_w_qkv=["at_w_qkv"], at_w_o=["at_w_o"],
            mlp_w_up=["up0", "up1"], mlp_w_down=["down0", "down1"])


def kernel(x, *args):
    n_w = len(_WEIGHTS)
    w = dict(zip(_WEIGHTS, args[:n_w]))
    target = args[n_w]
    m = dict(zip(_WEIGHTS, args[n_w + 1:2 * n_w + 1]))
    v = dict(zip(_WEIGHTS, args[2 * n_w + 1:3 * n_w + 1]))
    B, L, _ = x.shape
    T = B * L
    me = 2 * lax.axis_index("x") + lax.axis_index("y")

    vec = jnp.concatenate([_gate_vec_slot(w["rg_b_a"][0], w["rg_b_x"][0], w["rg_lam"][0]),
                           _rows_at(w["rg_conv_w"][0, :, 0, :], 0)], axis=0)
    flat = lambda a: a.reshape(-1, a.shape[-1])
    rows_of = lambda k: w[k].shape[-2]
    groups = [("rg", [("rg_w_in", 0, BF16), (vec, 0, F32)]), ("rg_out", [("rg_w_out", 0, BF16)]),
              ("mlp0_up", [("mlp_w_up", 0, BF16)]), ("mlp0_down", [("mlp_w_down", 0, BF16)]),
              ("att", [("at_w_qkv", 0, BF16), ("at_w_o", 0, BF16)]),
              ("mlp1", [("mlp_w_up", 1, BF16), ("mlp_w_down", 1, BF16)])]

    def landing_zones(group, members, after):
        lands = []
        for n, (k, layer, dtype) in enumerate(members):
            src, rows = (flat(w[k]), rows_of(k)) if isinstance(k, str) else (k, k.shape[0])
            lands.append(_cast_into_slot(src, layer * rows, rows, me, dtype, f"place_{group}{n}", after=after))
        return lands

    gathers = {}
    gathers["rg"], tok = _exchange_start("gather", [], landing_zones(*groups[0], None), "gather_rg_start")
    handles, tok = _gather_start_groups([landing_zones(g, members, tok) for g, members in groups[1:]],
                                        "gather_rest_start", after=tok)
    gathers.update(zip([g for g, _ in groups[1:]], handles))
    wcat = _make_wcat(w["rg_w_a"], w["rg_w_x"]).astype(BF16)

    packs = [_pack_vec(p, me) for p in (w, m, v)]

    ready = {}

    def fetch(what, after):
        if what in ready:
            return ready[what]
        group = "mlp1" if what.startswith("mlp1") else what
        order = [after, wcat] + packs if group == "rg" else after
        _, full = _exchange_wait("gather", gathers[group], order, f"gather_{group}_wait")
        if group == "rg":
            vec_full = jnp.transpose(full[1], (1, 0, 2)).reshape(2 * SUBLANES, D_MODEL)
            conv_wb = vec_full[SUBLANES:] + _rows_at(w["rg_conv_b"], 4)
            return full[0], conv_wb, wcat, vec_full[:SUBLANES]
        if group == "rg_out":
            return full[0].reshape(D_MODEL, D_MODEL)
        if group == "att":
            return full[0], full[1].reshape(D_MODEL, D_MODEL)
        if group == "mlp1":
            ready["mlp1_up"], ready["mlp1_down"] = full[0], full[1].reshape(4 * D_MODEL, D_MODEL)
            return ready[what]
        return full[0] if group == "mlp0_up" else full[0].reshape(4 * D_MODEL, D_MODEL)

    names = dict(mlp1=["up1", "down1"], att=["at_w_qkv", "at_w_o"], mlp0=["up0", "down0", "rg_w_out"],
                 rg_in=["rg_w_in"], gates=["rg_w_a", "rg_w_x"])
    scatters, swaps, P, Q, res = {}, [], {}, {}, {}

    def start_scatter(group, grads):
        srcs = [g.reshape(N_CHIPS, -1, g.shape[-1]) for g in grads]
        lands = [lax.empty((N_CHIPS - 1,) + s.shape[1:], s.dtype) for s in srcs]
        scatters[group], token = _exchange_start("scatter", srcs, lands, f"scatter_{group}_start")
        return token

    def settle(groups, after):
        keys, parts = [], []
        for group in groups:
            srcs, lands = _exchange_wait("scatter", scatters[group], after, f"scatter_{group}_wait")
            for k, s, r in zip(names[group], srcs, lands):
                keys.append(k)
                parts.append(_sum_slots(s, r, me, f"sum_{k}"))
        handle, token = _exchange_start("swap", parts, [lax.empty(p.shape, F32) for p in parts],
                                        f"swap_{groups[0]}_start")
        swaps.append((keys, handle, f"swap_{groups[0]}_wait"))
        return token

    def finish(after):
        for keys, handle, name in swaps:
            mine, theirs = _exchange_wait("swap", handle, after, name)
            P.update(zip(keys, mine))
            Q.update(zip(keys, theirs))
        swaps.clear()
        last = after
        for k, parts in _BIG.items():
            if k in res or any(p not in P for p in parts):
                continue
            shape = w[k].shape
            two_d = lambda a: a.reshape(-1, shape[-1])
            outs = _adamw(two_d(w[k]), two_d(m[k]), two_d(v[k]), [P[p] for p in parts], [Q[p] for p in parts],
                          f"adamw_{k}")
            res[k] = [o.reshape(shape) for o in outs]
            last = outs[0]
        if "rg_w_a" in P and "gates" not in gathers:
            lands = [_cast_into_slot(P[k], 0, P[k].shape[0], me, F32, f"place_{k}", after=last, add=Q[k])
                     for k in names["gates"]]
            gathers["gates"], last = _exchange_start("gather", [], lands, "gather_gates_start", after=last)
        return last

    def emit(event, arrays):
        if event == "point_attn_done":
            return None
        if event == "point_mix_done":
            return settle(["mlp1", "att", "mlp0"], arrays[0])
        token = start_scatter(event, arrays)
        if event == "rg_in":
            return finish(settle(["gates"], token))
        return token

    P_vec = dict(norm_mix_g=w["norm_mix_g"], norm_mlp_g=w["norm_mlp_g"], final_g=w["final_g"][None],
                 q_g=w["at_q_g"], k_g=w["at_k_g"])
    grad_x, vec_part = _local_step(x.reshape(T, D_MODEL), target.reshape(T, D_MODEL), P_vec, fetch, emit, B, L,
                                   after=tok)

    me8 = 2 * me + lax.axis_index("c")
    vec_slots = _cast_into_slot(vec_part, 0, VEC_ROWS, me8, F32, "place_vec", n_slots=N_DEVICES)
    spread, tok = _exchange_start("spread", [], [vec_slots], "spread_vec_start")
    last = finish(settle(["rg_in"], tok))
    _, gate_grads = _exchange_wait("gather", gathers["gates"], last, "gather_gates_wait")
    for k, g in zip(names["gates"], gate_grads):
        two_d = lambda a: a.reshape(g.shape[0] * g.shape[1], g.shape[2])
        outs = _adamw(two_d(w[k]), two_d(m[k]), two_d(v[k]), [two_d(g)], None, f"adamw_{k}")
        res[k] = [o.reshape(w[k].shape) for o in outs]
        last = outs[0]
    _, (vec_all,) = _exchange_wait("spread", spread, last, "spread_vec_wait")
    vec_grad = _sum_leading(vec_all, "sum_vec")
    loss = vec_grad[LOSS_ROW, 0]
    outs = _adamw(*packs, [vec_grad], None, "adamw_vec")
    unpacked = [_unpack_vec(o, me) for o in outs]
    for k in _WEIGHTS:
        if k not in res:
            res[k] = [u[k] for u in unpacked]

    result = [loss, grad_x.reshape(B, L, D_MODEL)]
    for slot in range(4):
        result += [res[k][slot] for k in _WEIGHTS]
    return tuple(result)
```

```python
import functools
import math

import jax
import jax.numpy as jnp
import numpy as np
from jax import lax
from jax.experimental import pallas as pl
from jax.experimental.pallas import tpu as pltpu

F32 = jnp.float32
BF16 = jnp.bfloat16

D_MODEL = 1024
HEAD_DIM = 128
N_HEADS = 8
N_KV = 2
GROUP = N_HEADS // N_KV
LRU_BLOCKS = 8
LRU_BW = 128
GRID_W = 64
ROPE_THETA = 10000.0
EPS = 1e-6
RG_C = 8.0
SCALE = 1.0 / math.sqrt(HEAD_DIM)
N_CHIPS = 4

ADAM_LR = 0.001
ADAM_B1 = 0.9
ADAM_B2 = 0.999
ADAM_EPS = 1e-08
ADAM_WD = 0.01
ADAM_STEP = 10

V7X_VMEM_BYTES = 64 * 1024 * 1024
VMEM_LIMIT = V7X_VMEM_BYTES * 3 // 4
LANES = 128
SUBLANES = 8

N_DEVICES = 8
VEC_ROWS = 32
LOSS_ROW = 5


def _params(sem):
    return pltpu.CompilerParams(dimension_semantics=sem, vmem_limit_bytes=VMEM_LIMIT)


_ANY = pl.BlockSpec(memory_space=pl.ANY)
_NN = (((1,), (0,)), ((), ()))
_NT = (((1,), (1,)), ((), ()))
_TN = (((0,), (0,)), ((), ()))


def _after_operand(after):
    return [] if after is None else [after]


def _fit(t, n):
    if n <= t:
        return n
    c = (t // LANES) * LANES
    while n % c:
        c -= LANES
    return c


MM_VMEM_BUDGET = VMEM_LIMIT * 3 // 4
def _mm_tiles(M, K, ns, n_total, out_dtypes, extras, whole_rows):
    for tm in (2048, 1024, 512, 256, 128):
        for tn in ((ns,) if whole_rows else (1024, 512, 256)):
            tn = _fit(tn, ns)
            per_row = 2 * (2 * K) + 4 * tn + sum(2 * tn * jnp.dtype(d).itemsize for d in out_dtypes)
            per_row += sum(2 * tn * e.dtype.itemsize for e in extras)
            b_buffers = 1 if tn == n_total else 2
            if M % tm == 0 and b_buffers * (2 * K * tn) + tm * per_row <= MM_VMEM_BUDGET:
                return tm, tn
    raise ValueError(f"no tile fits VMEM for M={M} K={K} N={ns}")


def _mm(a, b, *, mode, name, out_dtypes=(F32,), b_shard=False, o_shard=False, extras=(), epi=None, after=None,
        bcast=(), accs=(), ref_epi=None, out_cols=None):
    if mode == "tn":
        K, M = a.shape
        N = b.shape[1]
    else:
        M, K = a.shape
        if mode == "nn":
            N = b.shape[0] * b.shape[2] if b_shard else b.shape[1]
        else:
            N = b.shape[1] if b_shard else b.shape[0]
    ns = N
    if b_shard and mode == "nn":
        ns = b.shape[2]
    elif o_shard:
        ns = N // N_CHIPS
    tm, tn = _mm_tiles(M, K, ns, N, out_dtypes, extras, whole_rows=ref_epi is not None)
    if ref_epi is not None:
        tm = min(tm, 512)
    grid = (M // tm, N // tn)
    q = ns // tn
    once = dict(pipeline_mode=pl.Buffered(1)) if tn == N else {}

    if mode == "tn":
        a_spec = pl.BlockSpec((K, tm), lambda i, j: (0, i))
        b_spec = pl.BlockSpec((K, tn), lambda i, j: (0, j), **once)
        dims = _TN
    elif mode == "nn":
        a_spec = pl.BlockSpec((tm, K), lambda i, j: (i, 0))
        if b_shard:
            b_spec = pl.BlockSpec((None, K, tn), lambda i, j: (j // q, 0, j % q), **once)
        else:
            b_spec = pl.BlockSpec((K, tn), lambda i, j: (0, j), **once)
        dims = _NN
    else:
        a_spec = pl.BlockSpec((tm, K), lambda i, j: (i, 0))
        if b_shard:
            ks = b.shape[2]
            b_spec = pl.BlockSpec((N_CHIPS, tn, ks), lambda i, j: (0, j, 0), **once)
        else:
            b_spec = pl.BlockSpec((tn, K), lambda i, j: (j, 0), **once)
        dims = _NT

    if o_shard:
        o_specs = [pl.BlockSpec((None, tm, tn), lambda i, j: (j // q, i, j % q))]
        o_shapes = [jax.ShapeDtypeStruct((N_CHIPS, M, ns), out_dtypes[0])]
    else:
        o_specs = [pl.BlockSpec((tm, tn), lambda i, j: (i, j)) for _ in out_dtypes]
        o_shapes = [jax.ShapeDtypeStruct((M, N if out_cols is None else out_cols[n]), dt)
                    for n, dt in enumerate(out_dtypes)]
    e_specs = [pl.BlockSpec((tm, tn), lambda i, j: (i, j)) for _ in extras]
    e_specs += [pl.BlockSpec(v.shape, lambda i, j: (0, 0)) for v in bcast]
    o_specs += [pl.BlockSpec(s, lambda i, j: (0, 0)) for s in accs]
    o_shapes += [jax.ShapeDtypeStruct(s, F32) for s in accs]
    n_e, n_b, n_o, n_a = len(extras), len(bcast), len(out_dtypes), len(accs)
    order = _after_operand(after)
    n_x = len(order)
    if epi is None:
        epi = lambda acc: (acc,)

    def body(a_ref, b_ref, *rest):
        e_refs, b_refs = rest[:n_e], rest[n_e:n_e + n_b]
        o_refs = rest[n_e + n_b + n_x:n_e + n_b + n_x + n_o]
        a_refs = rest[n_e + n_b + n_x + n_o:]
        if n_a:
            @pl.when((pl.program_id(0) == 0) & (pl.program_id(1) == 0))
            def _():
                for r in a_refs:
                    r[...] = jnp.zeros(r.shape, F32)
        if mode == "nt" and b_shard:
            acc = None
            for s in range(N_CHIPS):
                part = lax.dot_general(a_ref[:, s * ks:(s + 1) * ks], b_ref[s], dims, preferred_element_type=F32)
                acc = part if acc is None else acc + part
        else:
            acc = lax.dot_general(a_ref[...], b_ref[...], dims, preferred_element_type=F32)
        if ref_epi is not None:
            ref_epi(acc, e_refs, b_refs, o_refs, a_refs)
            return
        outs = epi(acc, *[r[...] for r in e_refs])
        for r, o in zip(o_refs, outs):
            r[...] = o.astype(r.dtype)

    outs = pl.pallas_call(
        body, name=name, grid=grid, in_specs=[a_spec, b_spec] + e_specs + [_ANY] * n_x, out_specs=o_specs,
        out_shape=o_shapes, compiler_params=_params(("arbitrary", "arbitrary") if n_a else ("parallel", "parallel")),
    )(a, b, *extras, *bcast, *order)
    return outs[0] if n_o + n_a == 1 else outs


def _rowwise(fn, rows, bcast, outs, accs=(), *, tm, name, after=None):
    def norm(r):
        return r if isinstance(r, tuple) else (r, r.shape[1], 0)

    rows = [norm(r) for r in rows]
    T = rows[0][0].shape[0]
    tm = min(tm, T)
    while T % tm:
        tm -= SUBLANES
    n_r, n_b, n_o, n_a = len(rows), len(bcast), len(outs), len(accs)
    order = _after_operand(after)
    n_x = len(order)
    in_specs = [pl.BlockSpec((tm, c), functools.partial(lambda i, cb: (i, cb), cb=cb)) for _, c, cb in rows]
    in_specs += [pl.BlockSpec(b.shape, lambda i: (0, 0)) for b in bcast] + [_ANY] * n_x
    out_specs = [pl.BlockSpec((tm, o[0]), lambda i: (i, 0)) for o in outs]
    out_specs += [pl.BlockSpec(s, lambda i: (0, 0)) for s in accs]
    out_shape = [jax.ShapeDtypeStruct((T, o[2] if len(o) > 2 else o[0]), o[1]) for o in outs]
    out_shape += [jax.ShapeDtypeStruct(s, F32) for s in accs]

    def body(*refs):
        in_refs = refs[:n_r]
        b_refs = refs[n_r:n_r + n_b]
        o_refs = refs[n_r + n_b + n_x:n_r + n_b + n_x + n_o]
        a_refs = refs[n_r + n_b + n_x + n_o:]
        if n_a:
            @pl.when(pl.program_id(0) == 0)
            def _():
                for r in a_refs:
                    r[...] = jnp.zeros(r.shape, F32)
        fn(in_refs, b_refs, o_refs, a_refs)

    res = pl.pallas_call(
        body, name=name, grid=(T // tm,), in_specs=in_specs, out_specs=out_specs, out_shape=out_shape,
        compiler_params=_params(("arbitrary",) if n_a else ("parallel",)),
    )(*[r[0] for r in rows], *bcast, *order)
    return res


def _rsum(x):
    return jnp.sum(x, axis=0, keepdims=True)


def _rms_fwd(x, g, name, after=None):
    def fn(ins, bs, outs, accs):
        xv = ins[0][...]
        r = lax.rsqrt(jnp.mean(xv * xv, axis=-1, keepdims=True) + EPS)
        outs[0][...] = (xv * r * bs[0][...]).astype(BF16)

    return _rowwise(fn, [x], [g], [(D_MODEL, BF16)], tm=512, name=name, after=after)[0]


def _rms_bwd_math(xv, dh, g):
    r = lax.rsqrt(jnp.mean(xv * xv, axis=-1, keepdims=True) + EPS)
    hn = xv * r
    dgh = dh * g
    dx = r * (dgh - hn * jnp.mean(dgh * hn, axis=-1, keepdims=True))
    return dx, _rsum(dh * hn)


def _mm_norm_bwd(dy, w, x, dres, g, name, after=None):
    def epilogue(acc, e_refs, b_refs, o_refs, a_refs):
        dx, dg = _rms_bwd_math(e_refs[0][...], acc, b_refs[0][...])
        dx = dx + e_refs[1][...]
        o_refs[0][...] = dx
        o_refs[1][...] = dx.astype(BF16)
        a_refs[0][...] += dg

    return _mm(dy, w, mode="nt", b_shard=True, out_dtypes=(F32, BF16), extras=(x, dres), bcast=(g,),
               accs=((1, D_MODEL),), ref_epi=epilogue, name=name, after=after)


def _mm_res_norm(a, w, res, g, name):
    def epilogue(acc, e_refs, b_refs, o_refs, a_refs):
        xv = acc + e_refs[0][...]
        o_refs[0][...] = xv
        r = lax.rsqrt(jnp.mean(xv * xv, axis=-1, keepdims=True) + EPS)
        o_refs[1][...] = (xv * r * b_refs[0][...]).astype(BF16)

    return _mm(a, w, mode="nn", out_dtypes=(F32, BF16), extras=(res,), bcast=(g,), ref_epi=epilogue, name=name)


def _mm_final_loss(a, w, res, target, g, name):
    def epilogue(acc, e_refs, b_refs, o_refs, a_refs):
        xv = acc + e_refs[0][...]
        gv = b_refs[0][...]
        r = lax.rsqrt(jnp.mean(xv * xv, axis=-1, keepdims=True) + EPS)
        e = xv * r * gv - e_refs[1][...]
        tok = jnp.mean(e * e, axis=-1, keepdims=True)
        a_refs[0][...] += 0.5 * jnp.sum(tok, axis=0, keepdims=True) * jnp.ones((1, LANES), F32)
        dx, dg = _rms_bwd_math(xv, e * (1.0 / D_MODEL), gv)
        o_refs[0][...] = dx
        o_refs[1][...] = dx.astype(BF16)
        a_refs[1][...] += dg

    return _mm(a, w, mode="nn", out_dtypes=(F32, BF16), extras=(res, target), bcast=(g,),
               accs=((1, LANES), (1, D_MODEL)), ref_epi=epilogue, name=name)


def _relu2(acc):
    r = jnp.maximum(acc, 0.0)
    return r * r, r


def _mlp_fwd(x, h, fetch, tag, finish):
    w_up = fetch(f"mlp{tag}_up", h)
    a, r = _mm(h, w_up, mode="nn", b_shard=True, out_dtypes=(BF16, BF16), epi=_relu2, name=f"mlp{tag}_up")
    w_down = fetch(f"mlp{tag}_down", a)
    return finish(a, w_down, x, f"mlp{tag}_down"), (h, a, r, w_up, w_down)


def _mlp_bwd(x, g, saved, dx, dx_bf, tag, after):
    h, a, r, w_up, w_down = saved
    d_down = _mm(a, dx_bf, mode="tn", out_dtypes=(BF16,), name=f"mlp{tag}_dwdown", after=after)
    dup = _mm(dx_bf, w_down, mode="nt", extras=(r,), out_dtypes=(BF16,),
              epi=lambda acc, rv: (acc * (2.0 * rv.astype(F32)),), name=f"mlp{tag}_dup")
    d_up = _mm(h, dup, mode="tn", o_shard=True, out_dtypes=(BF16,), name=f"mlp{tag}_dwup")
    dx_new, dx_new_bf, dg = _mm_norm_bwd(dup, w_up, x, dx, g, f"mlp{tag}_dh")
    return dx_new, dx_new_bf, dg, d_up, d_down


def _rope_tables(L, B):
    rows = L // GRID_W
    row = np.repeat(np.arange(rows, dtype=np.float32), GRID_W)
    col = np.tile(np.arange(GRID_W, dtype=np.float32), rows)
    inv = (ROPE_THETA ** (-np.arange(HEAD_DIM // 4, dtype=np.float32) / (HEAD_DIM // 4))).astype(np.float32)
    ar, ac = row[:, None] * inv, col[:, None] * inv
    cos = np.concatenate([np.cos(ar), np.cos(ar), np.cos(ac), np.cos(ac)], axis=-1)
    sin = np.concatenate([-np.sin(ar), np.sin(ar), -np.sin(ac), np.sin(ac)], axis=-1)
    return jnp.asarray(np.tile(cos, (B, 1)), F32), jnp.asarray(np.tile(sin, (B, 1)), F32)


def _swap_halves(x):
    lane = lax.broadcasted_iota(jnp.int32, x.shape, 1)
    return jnp.where((lane % 64) < 32, pltpu.roll(x, HEAD_DIM - 32, 1), pltpu.roll(x, 32, 1))


def _qk_prep(qkv, cos, sin, q_g, k_g):
    def fn(ins, bs, outs, accs):
        c, s = ins[1][...], ins[2][...]
        for h in range(N_HEADS + N_KV):
            xv = ins[0][:, h * HEAD_DIM:(h + 1) * HEAD_DIM]
            g = bs[0][...] if h < N_HEADS else bs[1][...]
            r = lax.rsqrt(jnp.mean(xv * xv, axis=-1, keepdims=True) + EPS)
            z = xv * r * g
            y = (z * c + _swap_halves(z) * s).astype(BF16)
            if h < N_HEADS:
                outs[0][:, h * HEAD_DIM:(h + 1) * HEAD_DIM] = y
            else:
                outs[1][:, (h - N_HEADS) * HEAD_DIM:(h - N_HEADS + 1) * HEAD_DIM] = y
        outs[2][...] = ins[0][:, (N_HEADS + N_KV) * HEAD_DIM:].astype(BF16)

    kvw = N_KV * HEAD_DIM
    return _rowwise(fn, [qkv, cos, sin], [q_g, k_g], [(D_MODEL, BF16), (kvw, BF16), (kvw, BF16)], tm=512,
                    name="attn_qk_prep")


def _qk_prep_bwd(qkv, dq, dk, dv, cos, sin, q_g, k_g):
    def fn(ins, bs, outs, accs):
        c, s = ins[4][...], ins[5][...]
        for h in range(N_HEADS + N_KV):
            sl = slice(h * HEAD_DIM, (h + 1) * HEAD_DIM)
            xv = ins[0][:, sl]
            if h < N_HEADS:
                g, dy, acc = bs[0][...], ins[1][:, sl], accs[0]
            else:
                ks = slice((h - N_HEADS) * HEAD_DIM, (h - N_HEADS + 1) * HEAD_DIM)
                g, dy, acc = bs[1][...], ins[2][:, ks], accs[1]
            r = lax.rsqrt(jnp.mean(xv * xv, axis=-1, keepdims=True) + EPS)
            xn = xv * r
            dz = dy * c - _swap_halves(dy) * s
            acc[...] += _rsum(dz * xn)
            dxn = dz * g
            outs[0][:, sl] = (r * (dxn - xn * jnp.mean(dxn * xn, axis=-1, keepdims=True))).astype(BF16)
        outs[0][:, (N_HEADS + N_KV) * HEAD_DIM:] = ins[3][...].astype(BF16)

    return _rowwise(fn, [qkv, dq, dk, dv, cos, sin], [q_g, k_g], [(qkv.shape[1], BF16)],
                    [(1, HEAD_DIM), (1, HEAD_DIM)], tm=256, name="attn_qk_prep_bwd")


_EXP2_SCALE = SCALE * math.log2(math.e)


def _exp_rows(q, k):
    s = lax.dot_general(q, k, _NT, preferred_element_type=F32)
    p = jnp.exp2((s - jnp.max(s, axis=-1, keepdims=True)) * _EXP2_SCALE)
    return p, jnp.sum(p, axis=-1, keepdims=True)


def _attn_fwd(q, k, v, B, L, tq=2048, sub=256):
    tq = min(tq, L)
    sub = min(sub, tq)
    nq = L // tq

    def body(q_ref, k_ref, v_ref, o_ref):
        kv, vv = k_ref[...], v_ref[...]
        for c in range(tq // sub):
            rows = slice(c * sub, (c + 1) * sub)
            p, l = _exp_rows(q_ref[rows, :], kv)
            o = jnp.dot(p.astype(BF16), vv, preferred_element_type=F32)
            o_ref[rows, :] = (o * (1.0 / l)).astype(o_ref.dtype)

    return pl.pallas_call(
        body, name="attn_fwd", grid=(B, N_HEADS, nq),
        in_specs=[pl.BlockSpec((tq, HEAD_DIM), lambda b, h, i: (b * nq + i, h)),
                  pl.BlockSpec((L, HEAD_DIM), lambda b, h, i: (b, h // GROUP)),
                  pl.BlockSpec((L, HEAD_DIM), lambda b, h, i: (b, h // GROUP))],
        out_specs=pl.BlockSpec((tq, HEAD_DIM), lambda b, h, i: (b * nq + i, h)),
        out_shape=jax.ShapeDtypeStruct((B * L, D_MODEL), BF16),
        compiler_params=_params(("parallel", "parallel", "parallel")),
    )(q, k, v)


def _attn_bwd(q, k, v, do, B, L, tq=2048, sub=512):
    tq = min(tq, L)
    sub = min(sub, tq)
    nq = L // tq

    def body(q_ref, k_ref, v_ref, do_ref, dq_ref, dk_ref, dv_ref):
        @pl.when((pl.program_id(2) == 0) & (pl.program_id(3) == 0))
        def _():
            dk_ref[...] = jnp.zeros(dk_ref.shape, F32)
            dv_ref[...] = jnp.zeros(dv_ref.shape, F32)

        kv, vv = k_ref[...], v_ref[...]
        ps, es, dos, qs = [], [], [], []
        for c in range(tq // sub):
            rows = slice(c * sub, (c + 1) * sub)
            qc, doc = q_ref[rows, :], do_ref[rows, :]
            p, l = _exp_rows(qc, kv)
            inv = 1.0 / l
            dp = lax.dot_general(doc, vv, _NT, preferred_element_type=F32)
            delta = jnp.sum(p * dp, axis=-1, keepdims=True) * inv
            e = (p * (dp - delta)).astype(BF16)
            dq_ref[rows, :] = jnp.dot(e, kv, preferred_element_type=F32) * (inv * SCALE)
            ps.append(p.astype(BF16))
            es.append(e)
            dos.append((doc.astype(F32) * inv).astype(BF16))
            qs.append((qc.astype(F32) * (inv * SCALE)).astype(BF16))
        cat = lambda xs: xs[0] if len(xs) == 1 else jnp.concatenate(xs, axis=0)
        dv_ref[...] += lax.dot_general(cat(ps), cat(dos), _TN, preferred_element_type=F32)
        dk_ref[...] += lax.dot_general(cat(es), cat(qs), _TN, preferred_element_type=F32)

    qmap = lambda b, kh, g, i: (b * nq + i, kh * GROUP + g)
    kmap = lambda b, kh, g, i: (b, kh)
    kvw = N_KV * HEAD_DIM
    return pl.pallas_call(
        body, name="attn_bwd", grid=(B, N_KV, GROUP, nq),
        in_specs=[pl.BlockSpec((tq, HEAD_DIM), qmap), pl.BlockSpec((L, HEAD_DIM), kmap),
                  pl.BlockSpec((L, HEAD_DIM), kmap), pl.BlockSpec((tq, HEAD_DIM), qmap)],
        out_specs=[pl.BlockSpec((tq, HEAD_DIM), qmap), pl.BlockSpec((L, HEAD_DIM), kmap),
                   pl.BlockSpec((L, HEAD_DIM), kmap)],
        out_shape=[jax.ShapeDtypeStruct((B * L, D_MODEL), F32), jax.ShapeDtypeStruct((B * L, kvw), F32),
                   jax.ShapeDtypeStruct((B * L, kvw), F32)],
        compiler_params=_params(("parallel", "parallel", "arbitrary", "arbitrary")),
    )(q, k, v, do)


def _conv_shift(x, t, L, k):
    if k == 2:
        return x
    if k < 2:
        return jnp.where(t >= 2 - k, pltpu.roll(x, 2 - k, 0), 0.0)
    return jnp.where(t < L - (k - 2), pltpu.roll(x, L - (k - 2), 0), 0.0)


def _conv_apply(x, w_ref, L):
    t = lax.broadcasted_iota(jnp.int32, x.shape, 0)
    acc = w_ref[4:5, :] + w_ref[2:3, :] * x
    for k in (0, 1, 3):
        acc = acc + w_ref[k:k + 1, :] * _conv_shift(x, t, L, k)
    return acc


def _conv_fwd(z, wb, B, L, tc=256):
    noff = D_MODEL // tc

    def body(z_ref, w_ref, o_ref):
        o_ref[...] = _conv_apply(z_ref[...], w_ref, L)

    return pl.pallas_call(
        body, name="rg_conv", grid=(B, noff),
        in_specs=[pl.BlockSpec((L, tc), lambda b, j: (b, noff + j)), pl.BlockSpec((SUBLANES, tc), lambda b, j: (0, j))],
        out_specs=pl.BlockSpec((L, tc), lambda b, j: (b, j)),
        out_shape=jax.ShapeDtypeStruct((B * L, D_MODEL), F32),
        compiler_params=_params(("parallel", "parallel")),
    )(z, wb)


def _conv_bwd(z, g, wb, dz, B, L, tc=256, after=None):
    noff = D_MODEL // tc
    order = _after_operand(after)

    def body(z_ref, g_ref, w_ref, dz_in, *rest):
        dx_ref, dw_ref = rest[len(order):]

        @pl.when(pl.program_id(1) == 0)
        def _():
            dw_ref[...] = jnp.zeros(dw_ref.shape, F32)

        x, gv = z_ref[...], g_ref[...]
        t = lax.broadcasted_iota(jnp.int32, x.shape, 0)
        dx = w_ref[2:3, :] * gv
        for k in (0, 1, 3):
            dx = dx + w_ref[k:k + 1, :] * _conv_shift(gv, t, L, 4 - k)
        dx_ref[...] = dx.astype(BF16)
        for k in range(4):
            dw_ref[k:k + 1, :] += _rsum(_conv_shift(x, t, L, k) * gv)
        dw_ref[4:5, :] += _rsum(gv)

    return pl.pallas_call(
        body, name="rg_conv_bwd", grid=(noff, B),
        in_specs=[pl.BlockSpec((L, tc), lambda j, b: (b, noff + j)), pl.BlockSpec((L, tc), lambda j, b: (b, j)),
                  pl.BlockSpec((SUBLANES, tc), lambda j, b: (0, j)), _ANY] + [_ANY] * len(order),
        out_specs=[pl.BlockSpec((L, tc), lambda j, b: (b, noff + j)),
                   pl.BlockSpec((SUBLANES, tc), lambda j, b: (0, j))],
        out_shape=[jax.ShapeDtypeStruct(dz.shape, dz.dtype), jax.ShapeDtypeStruct((SUBLANES, D_MODEL), F32)],
        input_output_aliases={3: 0},
        compiler_params=_params(("parallel", "arbitrary")),
    )(z, g, wb, dz, *order)


def _softplus(x):
    return jnp.maximum(x, 0.0) + jnp.log1p(jnp.exp(-jnp.abs(x)))


_ROW_BA, _ROW_BX, _ROW_LAM = 0, 2, 4


def _gate_math(xb, pre, vec_ref, d, sl):
    pa = pre[:, (2 * d) * LRU_BW:(2 * d + 1) * LRU_BW] + vec_ref[_ROW_BA + d:_ROW_BA + d + 1, sl]
    px = pre[:, (2 * d + 1) * LRU_BW:(2 * d + 2) * LRU_BW] + vec_ref[_ROW_BX + d:_ROW_BX + d + 1, sl]
    r = 0.5 * jnp.tanh(0.5 * pa) + 0.5
    i = 0.5 * jnp.tanh(0.5 * px) + 0.5
    sp = _softplus(-vec_ref[_ROW_LAM + d:_ROW_LAM + d + 1, sl])
    log_a = (-RG_C) * r * sp
    a = jnp.exp(log_a)
    th = jnp.tanh(log_a)
    om = -2.0 * th / (1.0 - th)
    mult = jnp.sqrt(om)
    return a, mult * (i * xb), (r, i, sp, om, mult)


def _gate_fwd(rec, wcat, gvec):
    def fn(ins, bs, outs, accs):
        for blk in range(LRU_BLOCKS):
            sl = slice(blk * LRU_BW, (blk + 1) * LRU_BW)
            xb = ins[0][:, sl]
            pre = jnp.dot(xb.astype(BF16), bs[0][sl, :], preferred_element_type=F32)
            for d in range(2):
                a, u, _ = _gate_math(xb, pre, bs[1], d, sl)
                outs[2 * d][:, sl] = a
                outs[2 * d + 1][:, sl] = u

    return _rowwise(fn, [rec], [wcat, gvec], [(D_MODEL, F32)] * 4, tm=256, name="rg_gate")


def _gate_bwd(rec, du_f, da_f, du_b, da_b, wcat, gvec):
    def fn(ins, bs, outs, accs):
        for blk in range(LRU_BLOCKS):
            sl = slice(blk * LRU_BW, (blk + 1) * LRU_BW)
            xb = ins[0][:, sl]
            xb16 = xb.astype(BF16)
            w = bs[0][sl, :]
            pre = jnp.dot(xb16, w, preferred_element_type=F32)
            dx = jnp.zeros_like(xb)
            dpre = []
            for d in range(2):
                a, _, (r, i, sp, om, mult) = _gate_math(xb, pre, bs[1], d, sl)
                du, da = ins[1 + 2 * d][:, sl], ins[2 + 2 * d][:, sl]
                d_i = du * mult * xb
                d_mult = du * i * xb
                dx = dx + du * mult * i
                dlog = da * a - d_mult * (1.0 - om) / mult
                d_r = dlog * ((-RG_C) * sp)
                d_sp = _rsum(dlog * ((-RG_C) * r))
                lam = bs[1][_ROW_LAM + d:_ROW_LAM + d + 1, sl]
                accs[2][_ROW_LAM + d:_ROW_LAM + d + 1, sl] += d_sp * (-jax.nn.sigmoid(-lam))
                dpa = d_r * r * (1.0 - r)
                dpx = d_i * i * (1.0 - i)
                accs[2][_ROW_BA + d:_ROW_BA + d + 1, sl] += _rsum(dpa)
                accs[2][_ROW_BX + d:_ROW_BX + d + 1, sl] += _rsum(dpx)
                dpre += [dpa, dpx]
            dpre = jnp.concatenate(dpre, axis=1).astype(BF16)
            dw = lax.dot_general(xb16, dpre, _TN, preferred_element_type=F32)
            for d in range(2):
                rows = slice(d * D_MODEL + blk * LRU_BW, d * D_MODEL + (blk + 1) * LRU_BW)
                accs[0][rows, :] += dw[:, (2 * d) * LRU_BW:(2 * d + 1) * LRU_BW]
                accs[1][rows, :] += dw[:, (2 * d + 1) * LRU_BW:(2 * d + 2) * LRU_BW]
            outs[0][:, sl] = dx + lax.dot_general(dpre, w, _NT, preferred_element_type=F32)

    gate_shape = (2 * D_MODEL, LRU_BW)
    return _rowwise(fn, [rec, du_f, da_f, du_b, da_b], [wcat, gvec], [(D_MODEL, F32)],
                    [gate_shape, gate_shape, (SUBLANES, D_MODEL)], tm=256, name="rg_gate_bwd")


def _as_time_blocks(x):
    return x.reshape(x.shape[0] // SUBLANES, SUBLANES, x.shape[1])


def _scan_call(body, ins, n_out, B, L, tc, name):
    nb = L // SUBLANES
    spec = pl.BlockSpec((nb, SUBLANES, tc), lambda b, j: (b, 0, j))
    T = ins[0].shape[0]
    outs = pl.pallas_call(
        functools.partial(body, nb), name=name, grid=(B, D_MODEL // tc),
        in_specs=[spec] * len(ins), out_specs=[spec] * n_out,
        out_shape=[jax.ShapeDtypeStruct((T // SUBLANES, SUBLANES, D_MODEL), F32)] * n_out,
        compiler_params=_params(("parallel", "parallel")),
    )(*[_as_time_blocks(x) for x in ins])
    return [o.reshape(T, D_MODEL) for o in outs]


def _block_scan(A, U, reverse):
    row = lax.broadcasted_iota(jnp.int32, A.shape, 0)
    for s in (1, 2, 4):
        shift = SUBLANES - s if reverse else s
        valid = (row < SUBLANES - s) if reverse else (row >= s)
        a_sh = jnp.where(valid, pltpu.roll(A, shift, 0), 1.0)
        u_sh = jnp.where(valid, pltpu.roll(U, shift, 0), 0.0)
        U = A * u_sh + U
        A = A * a_sh
    return A, U


_LAST = SUBLANES - 1
SCAN_UNROLL = 8


def _loop_blocks(nb, step, init):
    def group(g, carry):
        for k in range(SCAN_UNROLL):
            carry = step(g * SCAN_UNROLL + k, carry)
        return carry

    return lax.fori_loop(0, nb // SCAN_UNROLL, group, init)


def _scan_fwd(a_f, u_f, a_b, u_b, B, L, tc=256):
    def body(nb, af, uf, ab, ub, hf, hb):
        def step(i, carry):
            c1, c2 = carry
            ib = nb - 1 - i
            p, h = _block_scan(af[i], uf[i], False)
            h = h + p * c1
            hf[i] = h
            p2, h2 = _block_scan(ab[ib], ub[ib], True)
            h2 = h2 + p2 * c2
            hb[ib] = h2
            return h[_LAST:, :], h2[:1, :]

        zero = jnp.zeros((1, tc), F32)
        _loop_blocks(nb, step, (zero, zero))

    return _scan_call(body, [a_f, u_f, a_b, u_b], 2, B, L, tc, "rg_scan")


def _scan_bwd(dy, a_f, h_f, a_b, h_b, B, L, tc=256):
    def body(nb, dy_r, af, hf, ab, hb, duf, daf, dub, dab):
        def step(i, carry):
            c1, c2 = carry
            ir = nb - 1 - i
            row = lax.broadcasted_iota(jnp.int32, (SUBLANES, tc), 0)
            a_up = jnp.where(row == _LAST, af[jnp.minimum(ir + 1, nb - 1), :1, :], pltpu.roll(af[ir], _LAST, 0))
            p, lam = _block_scan(a_up, dy_r[ir], True)
            lam = lam + p * c1
            before = hf[jnp.maximum(ir - 1, 0), _LAST:, :] * (ir > 0).astype(F32)
            duf[ir] = lam
            daf[ir] = lam * jnp.where(row == 0, before, pltpu.roll(hf[ir], 1, 0))
            a_dn = jnp.where(row == 0, ab[jnp.maximum(i - 1, 0), _LAST:, :], pltpu.roll(ab[i], 1, 0))
            p2, lam2 = _block_scan(a_dn, dy_r[i], False)
            lam2 = lam2 + p2 * c2
            after = hb[jnp.minimum(i + 1, nb - 1), :1, :] * (i < nb - 1).astype(F32)
            dub[i] = lam2
            dab[i] = lam2 * jnp.where(row == _LAST, after, pltpu.roll(hb[i], _LAST, 0))
            return lam[:1, :], lam2[_LAST:, :]

        zero = jnp.zeros((1, tc), F32)
        _loop_blocks(nb, step, (zero, zero))

    return _scan_call(body, [dy, a_f, h_f, a_b, h_b], 4, B, L, tc, "rg_scan_bwd")


_GELU_C = math.sqrt(2.0 / math.pi)


def _gelu_parts(x):
    th = jnp.tanh(_GELU_C * (x + 0.044715 * x * x * x))
    return 0.5 * x * (1.0 + th), th


def _gated_out(h_f, h_b, z):
    def fn(ins, bs, outs, accs):
        gl, _ = _gelu_parts(ins[2][...])
        outs[0][...] = ((ins[0][...] + ins[1][...]) * gl).astype(BF16)

    return _rowwise(fn, [h_f, h_b, (z, D_MODEL, 0)], [], [(D_MODEL, BF16)], tm=512, name="rg_gated_out")[0]


def _mm_gated_out_bwd(dx, w_out, h_f, h_b, z, name, after=None):
    def epilogue(acc, e_refs, b_refs, o_refs, a_refs):
        x = e_refs[2][...]
        gl, th = _gelu_parts(x)
        dgl = 0.5 * (1.0 + th) + 0.5 * x * (1.0 - th * th) * (_GELU_C * (1.0 + 3.0 * 0.044715 * x * x))
        o_refs[0][...] = acc * gl
        o_refs[1][...] = (acc * (e_refs[0][...] + e_refs[1][...]) * dgl).astype(BF16)

    return _mm(dx, w_out, mode="nt", out_dtypes=(F32, BF16), out_cols=(D_MODEL, 2 * D_MODEL), extras=(h_f, h_b, z),
               ref_epi=epilogue, name=name, after=after)


def _row_block(i):
    return pl.ds(pl.multiple_of(i * SUBLANES, SUBLANES), SUBLANES)


def _rg_mix_fwd(z, conv_wb, wcat, gvec, B, L):
    nb = L // SUBLANES
    n_g = D_MODEL // LRU_BW

    def body(zg_ref, zr_ref, cw_ref, w_ref, gv_ref, rec_ref, af_s, ab_s, hf_ref, hb_ref, yg_ref, uf_s, ub_s):
        rec = _conv_apply(zr_ref[...], cw_ref, L)
        rec_ref[...] = rec
        pre = jnp.dot(rec.astype(BF16), w_ref[...], preferred_element_type=F32)
        for d, (a_s, u_s) in enumerate(((af_s, uf_s), (ab_s, ub_s))):
            a, u, _ = _gate_math(rec, pre, gv_ref, d, slice(None))
            a_s[...] = a
            u_s[...] = u

        def step(i, carry):
            c1, c2 = carry
            rows, rows_b = _row_block(i), _row_block(nb - 1 - i)
            p, h = _block_scan(af_s[rows, :], uf_s[rows, :], False)
            h = h + p * c1
            hf_ref[rows, :] = h
            p2, h2 = _block_scan(ab_s[rows_b, :], ub_s[rows_b, :], True)
            h2 = h2 + p2 * c2
            hb_ref[rows_b, :] = h2
            return h[_LAST:, :], h2[:1, :]

        zero = jnp.zeros((1, LRU_BW), F32)
        _loop_blocks(nb, step, (zero, zero))
        gl, _ = _gelu_parts(zg_ref[...])
        yg_ref[...] = ((hf_ref[...] + hb_ref[...]) * gl).astype(BF16)

    seq = lambda off: pl.BlockSpec((L, LRU_BW), lambda b, g: (b, off + g))
    vec = pl.BlockSpec((SUBLANES, LRU_BW), lambda b, g: (0, g))
    T = B * L
    return pl.pallas_call(
        body, name="rg_mix", grid=(B, n_g),
        in_specs=[seq(0), seq(n_g), vec, pl.BlockSpec((LRU_BW, 4 * LRU_BW), lambda b, g: (g, 0)), vec],
        out_specs=[seq(0)] * 6,
        out_shape=[jax.ShapeDtypeStruct((T, D_MODEL), F32)] * 5 + [jax.ShapeDtypeStruct((T, D_MODEL), BF16)],
        scratch_shapes=[pltpu.VMEM((L, LRU_BW), F32)] * 2,
        compiler_params=_params(("parallel", "parallel")),
    )(z, z, conv_wb, wcat, gvec)


def _make_wcat(w_a, w_x):
    g = jnp.stack([w_a[0, 0], w_x[0, 0], w_a[0, 1], w_x[0, 1]])
    return jnp.transpose(g, (1, 2, 0, 3)).reshape(D_MODEL, 4 * LRU_BW)


def _rows_at(part, first):
    return jnp.pad(part, ((first, SUBLANES - first - part.shape[0]), (0, 0)))


def _qk_slot(q_g, k_g):
    wide = lambda v, at: jnp.pad(v, ((0, SUBLANES - 1), (at, D_MODEL - at - HEAD_DIM)))
    return wide(q_g, 0) + wide(k_g, HEAD_DIM)


def _local_step(x, target, P, fetch, emit, B, L, after=None):
    g_mix, g_mlp = P["norm_mix_g"], P["norm_mlp_g"]
    h0 = _rms_fwd(x, g_mix[0:1], "rg_norm", after=after)
    w_in, conv_wb, wcat, gvec = fetch("rg", h0)
    z = _mm(h0, w_in, mode="nn", b_shard=True, name="rg_in")
    rec, a_f, a_b, h_f, h_b, yg = _rg_mix_fwd(z, conv_wb, wcat, gvec, B, L)
    w_out = fetch("rg_out", yg)
    x1, h1 = _mm_res_norm(yg, w_out, x, g_mlp[0:1], "rg_out")
    (x2, h3), mlp0 = _mlp_fwd(x1, h1, fetch, 0, lambda a, w, res, name: _mm_res_norm(a, w, res, g_mix[1:2], name))
    w_qkv, w_o = fetch("att", h3)
    qkv = _mm(h3, w_qkv, mode="nn", b_shard=True, name="attn_qkv")
    cos, sin = _rope_tables(L, B)
    qh, kh, vh = _qk_prep(qkv, cos, sin, P["q_g"], P["k_g"])
    o = _attn_fwd(qh, kh, vh, B, L)
    x3, h4 = _mm_res_norm(o, w_o, x2, g_mlp[1:2], "attn_out")
    (dx4, dx4_bf, loss_acc, d_final_g), mlp1 = _mlp_fwd(
        x3, h4, fetch, 1, lambda a, w, res, name: _mm_final_loss(a, w, res, target, P["final_g"], name))

    dx3, dx3_bf, dg_mlp1, d_up1, d_down1 = _mlp_bwd(x3, g_mlp[1:2], mlp1, dx4, dx4_bf, 1, None)
    tok = emit("mlp1", [d_up1, d_down1])
    d_wo = _mm(o, dx3_bf, mode="tn", out_dtypes=(BF16,), name="attn_dwo", after=tok)
    do = _mm(dx3_bf, w_o, mode="nt", out_dtypes=(BF16,), name="attn_do")
    dq, dk, dv = _attn_bwd(qh, kh, vh, do, B, L)
    dqkv, dq_g, dk_g = _qk_prep_bwd(qkv, dq, dk, dv, cos, sin, P["q_g"], P["k_g"])
    d_wqkv = _mm(h3, dqkv, mode="tn", o_shard=True, out_dtypes=(BF16,), name="attn_dwqkv")
    tok = emit("att", [d_wqkv, d_wo])
    dx2, dx2_bf, dg_mix1 = _mm_norm_bwd(dqkv, w_qkv, x2, dx3, g_mix[1:2], "attn_dh", after=tok)
    tok = emit("point_attn_done", [dx2_bf])
    dx1, dx1_bf, dg_mlp0, d_up0, d_down0 = _mlp_bwd(x1, g_mlp[0:1], mlp0, dx2, dx2_bf, 0, tok)
    d_wout = _mm(yg, dx1_bf, mode="tn", out_dtypes=(BF16,), name="rg_dwout")
    tok = emit("mlp0", [d_up0, d_down0, d_wout])
    dy, dgate = _mm_gated_out_bwd(dx1_bf, w_out, h_f, h_b, z, "rg_dyg", after=tok)
    du_f, da_f, du_b, da_b = _scan_bwd(dy, a_f, h_f, a_b, h_b, B, L)
    drec_c, d_wa, d_wx, d_gvec = _gate_bwd(rec, du_f, da_f, du_b, da_b, wcat, gvec)
    tok = emit("gates", [d_wa, d_wx])
    dz, d_convwb = _conv_bwd(z, drec_c, conv_wb, dgate, B, L, after=tok)
    tok = emit("point_mix_done", [dz])
    d_win = _mm(h0, dz, mode="tn", o_shard=True, out_dtypes=(BF16,), name="rg_dwin", after=tok)
    tok = emit("rg_in", [d_win])
    grad_x, _, dg_mix0 = _mm_norm_bwd(dz, w_in, x, dx1, g_mix[0:1], "rg_dh", after=tok)

    norms = (_rows_at(dg_mix0, 0) + _rows_at(dg_mix1, 1) + _rows_at(dg_mlp0, 2) + _rows_at(dg_mlp1, 3)
             + _rows_at(d_final_g, 4)
             + jnp.pad(loss_acc, ((LOSS_ROW, SUBLANES - 1 - LOSS_ROW), (0, D_MODEL - LANES))))
    vec = jnp.concatenate([norms, d_convwb, d_gvec, _qk_slot(dq_g, dk_g)], axis=0)
    return grad_x, vec


_MESH = pl.DeviceIdType.MESH


def _place():
    x, y, c = lax.axis_index("x"), lax.axis_index("y"), lax.axis_index("c")
    peers = [((1 - x) if j & 2 else x, (1 - y) if j & 1 else y) for j in (1, 2, 3)]
    return x, y, c, peers


def _sum_leading(slots, name):
    def body(s_ref, o_ref):
        acc = s_ref[0]
        for d in range(1, slots.shape[0]):
            acc = acc + s_ref[d]
        o_ref[...] = acc

    return pl.pallas_call(body, name=name, out_shape=jax.ShapeDtypeStruct(slots.shape[1:], slots.dtype))(slots)


_HBM = pl.BlockSpec(memory_space=pltpu.HBM)
_SEM = pl.BlockSpec(memory_space=pltpu.SEMAPHORE)
_EFFECT = pltpu.SideEffectType.DATAFLOW_SIDE_EFFECTING


_COPIES = dict(gather=N_CHIPS - 1, scatter=N_CHIPS - 1, swap=1, spread=N_DEVICES - 1,
               gather_half=N_CHIPS - 1, share_half=N_CHIPS - 1)


def _split_copies(kind, srcs, lands, send, recv):
    x, y, c, peers = _place()
    me = 2 * x + y
    per = _COPIES[kind]
    out = []
    for a in range(len(lands)):
        for j in range(per):
            if kind == "swap":
                src, there, here, dev = srcs[a], lands[a], lands[a], (x, y, 1 - c)
            elif kind == "spread":
                k = j + 1
                dev = ((1 - x) if k & 4 else x, (1 - y) if k & 2 else y, (1 - c) if k & 1 else c)
                mine = lands[a].at[4 * x + 2 * y + c]
                src, there, here = mine, mine, lands[a].at[4 * dev[0] + 2 * dev[1] + dev[2]]
            else:
                px, py = peers[j]
                dev = (px, py, c)
                if kind == "gather":
                    src, there, here = lands[a].at[me], lands[a].at[me], lands[a].at[2 * px + py]
                elif kind in ("gather_half", "share_half"):
                    half = lands[a].shape[1] // 2
                    mine, other = pl.ds(c * half, half), pl.ds((1 - c) * half, half)
                    if kind == "gather_half":
                        src = there = lands[a].at[me, mine]
                        here = lands[a].at[2 * px + py, mine]
                    else:
                        src = there = lands[a].at[2 * px + py, mine]
                        here = lands[a].at[2 * px + py, other]
                        dev = (x, y, 1 - c)
                else:
                    src, there, here = srcs[a].at[2 * px + py], lands[a].at[j], lands[a].at[j]
            mk = functools.partial(
                pltpu.make_async_remote_copy, src_ref=src, send_sem=send.at[per * a + j],
                recv_sem=recv.at[per * a + j], device_id=dev, device_id_type=_MESH)
            out.append((functools.partial(mk, dst_ref=there), functools.partial(mk, dst_ref=here)))
    return out


def _exchange_start(kind, srcs, lands, name, after=None):
    arrays = list(srcs) + list(lands)
    n_s, n, n_all = len(srcs), len(lands), len(srcs) + len(lands)
    n_sem = _COPIES[kind] * n
    order = _after_operand(after)
    n_x = len(order)

    def body(*refs):
        send, recv = refs[n_all + n_x], refs[n_all + n_x + 1]
        token = refs[-1]
        for started, _ in _split_copies(kind, refs[:n_s], refs[n_s:n_all], send, recv):
            started().start()
        token[...] = jnp.zeros(token.shape, F32)

    res = pl.pallas_call(
        body, name=name,
        out_shape=(pltpu.SemaphoreType.DMA((n_sem,)), pltpu.SemaphoreType.DMA((n_sem,)),
                   *[pltpu.HBM(a.shape, a.dtype) for a in arrays], jax.ShapeDtypeStruct((SUBLANES, LANES), F32)),
        in_specs=[_HBM] * n_all + [_ANY] * n_x,
        out_specs=(_SEM, _SEM, *[_HBM] * n_all, pl.BlockSpec(memory_space=pltpu.VMEM)),
        input_output_aliases={i: 2 + i for i in range(n_all)},
        compiler_params=pltpu.CompilerParams(has_side_effects=_EFFECT),
    )(*[pltpu.with_memory_space_constraint(a, pltpu.HBM) for a in arrays], *order)
    return (res[0], res[1], res[2:2 + n_s], res[2 + n_s:2 + n_all]), res[-1]


def _gather_start_groups(land_groups, name, after=None, kind="gather"):
    arrays = [a for group in land_groups for a in group]
    n_all, n_g = len(arrays), len(land_groups)
    order = _after_operand(after)
    n_x = len(order)

    def body(*refs):
        first = 0
        for gi, group in enumerate(land_groups):
            send, recv = refs[n_all + n_x + 2 * gi], refs[n_all + n_x + 2 * gi + 1]
            for started, _ in _split_copies(kind, [], refs[first:first + len(group)], send, recv):
                started().start()
            first += len(group)
        refs[-1][...] = jnp.zeros(refs[-1].shape, F32)

    sems = [pltpu.SemaphoreType.DMA((_COPIES[kind] * len(group),)) for group in land_groups for _ in range(2)]
    res = pl.pallas_call(
        body, name=name,
        out_shape=(*sems, *[pltpu.HBM(a.shape, a.dtype) for a in arrays], jax.ShapeDtypeStruct((SUBLANES, LANES), F32)),
        in_specs=[_HBM] * n_all + [_ANY] * n_x,
        out_specs=(*[_SEM] * (2 * n_g), *[_HBM] * n_all, pl.BlockSpec(memory_space=pltpu.VMEM)),
        input_output_aliases={i: 2 * n_g + i for i in range(n_all)},
        compiler_params=pltpu.CompilerParams(has_side_effects=_EFFECT),
    )(*[pltpu.with_memory_space_constraint(a, pltpu.HBM) for a in arrays], *order)
    handles, first = [], 2 * n_g
    for gi, group in enumerate(land_groups):
        handles.append((res[2 * gi], res[2 * gi + 1], [], res[first:first + len(group)]))
        first += len(group)
    return handles, res[-1]


def _exchange_wait(kind, handle, after, name):
    send, recv, srcs, lands = handle
    arrays = list(srcs) + list(lands)
    n_s, n_all = len(srcs), len(arrays)
    order = list(after) if isinstance(after, (list, tuple)) else [after]

    def body(*refs):
        for started, landing in _split_copies(kind, refs[:n_s], refs[n_s:n_all], refs[n_all], refs[n_all + 1]):
            started().wait_send()
            landing().wait_recv()

    res = pl.pallas_call(
        body, name=name, out_shape=[pltpu.HBM(a.shape, a.dtype) for a in arrays],
        in_specs=[_HBM] * n_all + [_SEM, _SEM] + [_ANY] * len(order), out_specs=[_HBM] * n_all,
        input_output_aliases={i: i for i in range(n_all)},
        compiler_params=pltpu.CompilerParams(has_side_effects=_EFFECT),
    )(*arrays, send, recv, *order)
    return res[:n_s], res[n_s:]


def _index_operand(i):
    return jnp.reshape(i, (1,)).astype(jnp.int32)


def _cast_into_slot(src, row0, rows, me, dtype, name, after=None, add=None, n_slots=N_CHIPS):
    cols = src.shape[1]
    tm = min(512, rows)
    order = _after_operand(after)
    terms = [src] + ([] if add is None else [add])

    def body(me_ref, *rest):
        val = rest[0][...]
        if add is not None:
            val = val + rest[1][...]
        rest[-1][...] = val.astype(dtype)

    return pl.pallas_call(
        body, name=name,
        grid_spec=pltpu.PrefetchScalarGridSpec(
            num_scalar_prefetch=1, grid=(rows // tm,),
            in_specs=[pl.BlockSpec((tm, cols), lambda i, me_ref: (i + row0 // tm, 0))] * len(terms)
            + [_ANY] * len(order),
            out_specs=pl.BlockSpec((None, tm, cols), lambda i, me_ref: (me_ref[0], i, 0))),
        out_shape=jax.ShapeDtypeStruct((n_slots, rows, cols), dtype), compiler_params=_params(("parallel",)),
    )(_index_operand(me), *terms, *order)


def _sum_slots(mine, r, me, name):
    _, rows, cols = r.shape
    tm = min(512, rows)

    def body(me_ref, own_ref, r_ref, o_ref):
        o_ref[...] = ((own_ref[...].astype(F32) + r_ref[0].astype(F32)) + r_ref[1].astype(F32)) + r_ref[2].astype(F32)

    return pl.pallas_call(
        body, name=name,
        grid_spec=pltpu.PrefetchScalarGridSpec(
            num_scalar_prefetch=1, grid=(rows // tm,),
            in_specs=[pl.BlockSpec((None, tm, cols), lambda i, me_ref: (me_ref[0], i, 0)),
                      pl.BlockSpec((N_CHIPS - 1, tm, cols), lambda i, me_ref: (0, i, 0))],
            out_specs=pl.BlockSpec((tm, cols), lambda i, me_ref: (i, 0))),
        out_shape=jax.ShapeDtypeStruct((rows, cols), F32), compiler_params=_params(("parallel",)),
    )(_index_operand(me), mine, r)


def _adamw(w, m, v, ps, qs, name):
    rows, cols = w.shape
    seg_rows = ps[0].shape[0]
    tm = min(256, seg_rows)
    while seg_rows % tm:
        tm -= SUBLANES
    per, n_seg = seg_rows // tm, len(ps)
    parts = list(ps) + ([] if qs is None else list(qs))

    def body(w_ref, m_ref, v_ref, *rest):
        g_refs, outs = rest[:len(parts)], rest[len(parts):]
        grad = lambda s: g_refs[s][...] if qs is None else g_refs[s][...] + g_refs[n_seg + s][...]
        g = grad(0)
        for s in range(1, n_seg):
            g = jnp.where(pl.program_id(0) >= s * per, grad(s), g)
        m1 = ADAM_B1 * m_ref[...] + (1.0 - ADAM_B1) * g
        v1 = ADAM_B2 * v_ref[...] + (1.0 - ADAM_B2) * (g * g)
        m_hat = m1 / (1.0 - ADAM_B1 ** ADAM_STEP)
        v_hat = v1 / (1.0 - ADAM_B2 ** ADAM_STEP)
        outs[0][...] = g
        outs[1][...] = (-ADAM_LR) * (m_hat / (jnp.sqrt(v_hat) + ADAM_EPS) + ADAM_WD * w_ref[...])
        outs[2][...] = m1
        outs[3][...] = v1

    row_spec = pl.BlockSpec((tm, cols), lambda i: (i, 0))
    seg_spec = lambda s: pl.BlockSpec((tm, cols), lambda i: (jnp.clip(i - s * per, 0, per - 1), 0))
    return pl.pallas_call(
        body, name=name, grid=(rows // tm,),
        in_specs=[row_spec] * 3 + [seg_spec(s) for s in range(n_seg)] * (1 if qs is None else 2),
        out_specs=[row_spec] * 4, out_shape=[jax.ShapeDtypeStruct((rows, cols), F32)] * 4,
        compiler_params=_params(("arbitrary",)),
    )(w, m, v, *parts)


def _put_cols(shard, me):
    full = jnp.zeros((shard.shape[0], D_MODEL), F32)
    return lax.dynamic_update_slice(full, shard, (0, me * (D_MODEL // N_CHIPS)))


def _gate_vec_slot(b_a, b_x, lam):
    return _rows_at(b_a, _ROW_BA) + _rows_at(b_x, _ROW_BX) + _rows_at(lam, _ROW_LAM)


def _pack_vec(p, me):
    return jnp.concatenate([
        _rows_at(p["norm_mix_g"], 0) + _rows_at(p["norm_mlp_g"], 2) + _rows_at(p["final_g"][None], 4),
        _rows_at(_put_cols(p["rg_conv_w"][0, :, 0, :], me), 0) + _rows_at(p["rg_conv_b"], 4),
        _gate_vec_slot(_put_cols(p["rg_b_a"][0], me), _put_cols(p["rg_b_x"][0], me), _put_cols(p["rg_lam"][0], me)),
        _qk_slot(p["at_q_g"], p["at_k_g"]),
    ], axis=0)


def _unpack_vec(r, me):
    def cols(rows):
        return lax.dynamic_slice(rows, (0, me * (D_MODEL // N_CHIPS)), (rows.shape[0], D_MODEL // N_CHIPS))

    gate = r[16:24]
    return dict(
        norm_mix_g=r[0:2], norm_mlp_g=r[2:4], final_g=r[4], rg_conv_w=cols(r[8:12])[None, :, None, :],
        rg_conv_b=r[12:13], rg_b_a=cols(gate[_ROW_BA:_ROW_BA + 2])[None], rg_b_x=cols(gate[_ROW_BX:_ROW_BX + 2])[None],
        rg_lam=cols(gate[_ROW_LAM:_ROW_LAM + 2])[None], at_q_g=r[24:25, 0:HEAD_DIM],
        at_k_g=r[24:25, HEAD_DIM:2 * HEAD_DIM])


_WEIGHTS = ['norm_mix_g', 'norm_mlp_g', 'rg_w_in', 'rg_conv_w', 'rg_conv_b', 'rg_w_a', 'rg_b_a', 'rg_w_x', 'rg_b_x',
            'rg_lam', 'rg_w_out', 'at_w_qkv', 'at_q_g', 'at_k_g', 'at_w_o', 'mlp_w_up', 'mlp_w_down', 'final_g']
_BIG = dict(rg_w_in=["rg_w_in"], rg_w_out=["rg_w_out"], at_w_qkv=["at_w_qkv"], at_w_o=["at_w_o"],
            mlp_w_up=["up0", "up1"], mlp_w_down=["down0", "down1"])


def kernel(x, *args):
    n_w = len(_WEIGHTS)
    w = dict(zip(_WEIGHTS, args[:n_w]))
    target = args[n_w]
    m = dict(zip(_WEIGHTS, args[n_w + 1:2 * n_w + 1]))
    v = dict(zip(_WEIGHTS, args[2 * n_w + 1:3 * n_w + 1]))
    B, L, _ = x.shape
    T = B * L
    me = 2 * lax.axis_index("x") + lax.axis_index("y")

    vec = jnp.concatenate([_gate_vec_slot(w["rg_b_a"][0], w["rg_b_x"][0], w["rg_lam"][0]),
                           _rows_at(w["rg_conv_w"][0, :, 0, :], 0)], axis=0)
    flat = lambda a: a.reshape(-1, a.shape[-1])
    rows_of = lambda k: w[k].shape[-2]
    groups = [("rg", [("rg_w_in", 0, BF16), (vec, 0, F32)]), ("rg_out", [("rg_w_out", 0, BF16)]),
              ("mlp0_up", [("mlp_w_up", 0, BF16)]), ("mlp0_down", [("mlp_w_down", 0, BF16)]),
              ("att", [("at_w_qkv", 0, BF16), ("at_w_o", 0, BF16)]),
              ("mlp1", [("mlp_w_up", 1, BF16), ("mlp_w_down", 1, BF16)])]

    def landing_zones(group, members, after):
        lands = []
        for n, (k, layer, dtype) in enumerate(members):
            src, rows = (flat(w[k]), rows_of(k)) if isinstance(k, str) else (k, k.shape[0])
            lands.append(_cast_into_slot(src, layer * rows, rows, me, dtype, f"place_{group}{n}", after=after))
        return lands

    halves, gathers = {}, {}
    halves["rg"], tok = _exchange_start("gather_half", [], landing_zones(*groups[0], None), "gather_rg_start")
    handles, tok = _gather_start_groups([landing_zones(g, members, tok) for g, members in groups[1:]],
                                        "gather_rest_start", after=tok, kind="gather_half")
    halves.update(zip([g for g, _ in groups[1:]], handles))
    wcat = _make_wcat(w["rg_w_a"], w["rg_w_x"]).astype(BF16)

    packs = [_pack_vec(p, me) for p in (w, m, v)]

    ready = {}

    def share(some, after, name):
        landed = [_exchange_wait("gather_half", halves[g], after, f"gather_{g}_landed")[1] for g in some]
        handles, _ = _gather_start_groups(landed, name, kind="share_half")
        gathers.update(zip(some, handles))

    def fetch(what, after):
        if what in ready:
            return ready[what]
        group = "mlp1" if what.startswith("mlp1") else what
        if group == "rg":
            share(["rg"], [after, wcat] + packs, "share_rg_start")
        elif group == "rg_out":
            share(["rg_out", "mlp0_up", "mlp0_down", "att"], after, "share_early_start")
        _, full = _exchange_wait("share_half", gathers[group], after, f"gather_{group}_wait")
        if group == "att":
            share(["mlp1"], after, "share_mlp1_start")
        if group == "rg":
            vec_full = jnp.transpose(full[1], (1, 0, 2)).reshape(2 * SUBLANES, D_MODEL)
            conv_wb = vec_full[SUBLANES:] + _rows_at(w["rg_conv_b"], 4)
            return full[0], conv_wb, wcat, vec_full[:SUBLANES]
        if group == "rg_out":
            return full[0].reshape(D_MODEL, D_MODEL)
        if group == "att":
            return full[0], full[1].reshape(D_MODEL, D_MODEL)
        if group == "mlp1":
            ready["mlp1_up"], ready["mlp1_down"] = full[0], full[1].reshape(4 * D_MODEL, D_MODEL)
            return ready[what]
        return full[0] if group == "mlp0_up" else full[0].reshape(4 * D_MODEL, D_MODEL)

    names = dict(mlp1=["up1", "down1"], att=["at_w_qkv", "at_w_o"], mlp0=["up0", "down0", "rg_w_out"],
                 rg_in=["rg_w_in"], gates=["rg_w_a", "rg_w_x"])
    scatters, swaps, P, Q, res = {}, [], {}, {}, {}

    def start_scatter(group, grads):
        srcs = [g.reshape(N_CHIPS, -1, g.shape[-1]) for g in grads]
        lands = [lax.empty((N_CHIPS - 1,) + s.shape[1:], s.dtype) for s in srcs]
        scatters[group], token = _exchange_start("scatter", srcs, lands, f"scatter_{group}_start")
        return token

    def settle(groups, after):
        keys, parts = [], []
        for group in groups:
            srcs, lands = _exchange_wait("scatter", scatters[group], after, f"scatter_{group}_wait")
            for k, s, r in zip(names[group], srcs, lands):
                keys.append(k)
                parts.append(_sum_slots(s, r, me, f"sum_{k}"))
        handle, token = _exchange_start("swap", parts, [lax.empty(p.shape, F32) for p in parts],
                                        f"swap_{groups[0]}_start")
        swaps.append((keys, handle, f"swap_{groups[0]}_wait"))
        return token

    def finish(after):
        for keys, handle, name in swaps:
            mine, theirs = _exchange_wait("swap", handle, after, name)
            P.update(zip(keys, mine))
            Q.update(zip(keys, theirs))
        swaps.clear()
        last = after
        for k, parts in _BIG.items():
            if k in res or any(p not in P for p in parts):
                continue
            shape = w[k].shape
            two_d = lambda a: a.reshape(-1, shape[-1])
            outs = _adamw(two_d(w[k]), two_d(m[k]), two_d(v[k]), [P[p] for p in parts], [Q[p] for p in parts],
                          f"adamw_{k}")
            res[k] = [o.reshape(shape) for o in outs]
            last = outs[0]
        if "rg_w_a" in P and "gates" not in gathers:
            lands = [_cast_into_slot(P[k], 0, P[k].shape[0], me, F32, f"place_{k}", after=last, add=Q[k])
                     for k in names["gates"]]
            gathers["gates"], last = _exchange_start("gather", [], lands, "gather_gates_start", after=last)
        return last

    def emit(event, arrays):
        if event == "point_attn_done":
            return None
        if event == "point_mix_done":
            return settle(["mlp1", "att", "mlp0"], arrays[0])
        token = start_scatter(event, arrays)
        if event == "rg_in":
            return finish(settle(["gates"], token))
        return token

    P_vec = dict(norm_mix_g=w["norm_mix_g"], norm_mlp_g=w["norm_mlp_g"], final_g=w["final_g"][None],
                 q_g=w["at_q_g"], k_g=w["at_k_g"])
    grad_x, vec_part = _local_step(x.reshape(T, D_MODEL), target.reshape(T, D_MODEL), P_vec, fetch, emit, B, L,
                                   after=tok)

    me8 = 2 * me + lax.axis_index("c")
    vec_slots = _cast_into_slot(vec_part, 0, VEC_ROWS, me8, F32, "place_vec", n_slots=N_DEVICES)
    spread, tok = _exchange_start("spread", [], [vec_slots], "spread_vec_start")
    last = finish(settle(["rg_in"], tok))
    _, gate_grads = _exchange_wait("gather", gathers["gates"], last, "gather_gates_wait")
    for k, g in zip(names["gates"], gate_grads):
        two_d = lambda a: a.reshape(g.shape[0] * g.shape[1], g.shape[2])
        outs = _adamw(two_d(w[k]), two_d(m[k]), two_d(v[k]), [two_d(g)], None, f"adamw_{k}")
        res[k] = [o.reshape(w[k].shape) for o in outs]
        last = outs[0]
    _, (vec_all,) = _exchange_wait("spread", spread, last, "spread_vec_wait")
    vec_grad = _sum_leading(vec_all, "sum_vec")
    loss = vec_grad[LOSS_ROW, 0]
    outs = _adamw(*packs, [vec_grad], None, "adamw_vec")
    unpacked = [_unpack_vec(o, me) for o in outs]
    for k in _WEIGHTS:
        if k not in res:
            res[k] = [u[k] for u in unpacked]

    result = [loss, grad_x.reshape(B, L, D_MODEL)]
    for slot in range(4):
        result += [res[k][slot] for k in _WEIGHTS]
    return tuple(result)
```

```python
import functools
import math

import jax
import jax.numpy as jnp
import numpy as np
from jax import lax
from jax.experimental import pallas as pl
from jax.experimental.pallas import tpu as pltpu

F32 = jnp.float32
BF16 = jnp.bfloat16

D_MODEL = 1024
HEAD_DIM = 128
N_HEADS = 8
N_KV = 2
GROUP = N_HEADS // N_KV
LRU_BLOCKS = 8
LRU_BW = 128
GRID_W = 64
ROPE_THETA = 10000.0
EPS = 1e-6
RG_C = 8.0
SCALE = 1.0 / math.sqrt(HEAD_DIM)
N_CHIPS = 4

ADAM_LR = 0.001
ADAM_B1 = 0.9
ADAM_B2 = 0.999
ADAM_EPS = 1e-08
ADAM_WD = 0.01
ADAM_STEP = 10

V7X_VMEM_BYTES = 64 * 1024 * 1024
VMEM_LIMIT = V7X_VMEM_BYTES * 3 // 4
LANES = 128
SUBLANES = 8

N_DEVICES = 8
VEC_ROWS = 32
LOSS_ROW = 5


def _params(sem):
    return pltpu.CompilerParams(dimension_semantics=sem, vmem_limit_bytes=VMEM_LIMIT)


_ANY = pl.BlockSpec(memory_space=pl.ANY)


def _out(shape, dtype):
    return pltpu.HBM(shape, dtype)


def _hbm(a):
    return pltpu.with_memory_space_constraint(a, pltpu.HBM)
_NN = (((1,), (0,)), ((), ()))
_NT = (((1,), (1,)), ((), ()))
_TN = (((0,), (0,)), ((), ()))


def _after_operand(after):
    return [] if after is None else [after]


def _fit(t, n):
    if n <= t:
        return n
    c = (t // LANES) * LANES
    while n % c:
        c -= LANES
    return c


MM_VMEM_BUDGET = VMEM_LIMIT * 3 // 4
def _mm_tiles(M, K, ns, n_total, out_dtypes, extras, whole_rows):
    for tm in (2048, 1024, 512, 256, 128):
        for tn in ((ns,) if whole_rows else (1024, 512, 256)):
            tn = _fit(tn, ns)
            per_row = 2 * (2 * K) + 4 * tn + sum(2 * tn * jnp.dtype(d).itemsize for d in out_dtypes)
            per_row += sum(2 * tn * e.dtype.itemsize for e in extras)
            b_buffers = 1 if tn == n_total else 2
            if M % tm == 0 and b_buffers * (2 * K * tn) + tm * per_row <= MM_VMEM_BUDGET:
                return tm, tn
    raise ValueError(f"no tile fits VMEM for M={M} K={K} N={ns}")


def _mm(a, b, *, mode, name, out_dtypes=(F32,), b_shard=False, o_shard=False, extras=(), epi=None, after=None,
        bcast=(), accs=(), ref_epi=None, out_cols=None):
    if mode == "tn":
        K, M = a.shape
        N = b.shape[1]
    else:
        M, K = a.shape
        if mode == "nn":
            N = b.shape[0] * b.shape[2] if b_shard else b.shape[1]
        else:
            N = b.shape[1] if b_shard else b.shape[0]
    ns = N
    if b_shard and mode == "nn":
        ns = b.shape[2]
    elif o_shard:
        ns = N // N_CHIPS
    tm, tn = _mm_tiles(M, K, ns, N, out_dtypes, extras, whole_rows=ref_epi is not None)
    if ref_epi is not None:
        tm = min(tm, 512)
    grid = (M // tm, N // tn)
    q = ns // tn
    once = dict(pipeline_mode=pl.Buffered(1)) if tn == N else {}

    if mode == "tn":
        a_spec = pl.BlockSpec((K, tm), lambda i, j: (0, i))
        b_spec = pl.BlockSpec((K, tn), lambda i, j: (0, j), **once)
        dims = _TN
    elif mode == "nn":
        a_spec = pl.BlockSpec((tm, K), lambda i, j: (i, 0))
        if b_shard:
            b_spec = pl.BlockSpec((None, K, tn), lambda i, j: (j // q, 0, j % q), **once)
        else:
            b_spec = pl.BlockSpec((K, tn), lambda i, j: (0, j), **once)
        dims = _NN
    else:
        a_spec = pl.BlockSpec((tm, K), lambda i, j: (i, 0))
        if b_shard:
            ks = b.shape[2]
            b_spec = pl.BlockSpec((N_CHIPS, tn, ks), lambda i, j: (0, j, 0), **once)
        else:
            b_spec = pl.BlockSpec((tn, K), lambda i, j: (j, 0), **once)
        dims = _NT

    if o_shard:
        o_specs = [pl.BlockSpec((None, tm, tn), lambda i, j: (j // q, i, j % q))]
        o_shapes = [_out((N_CHIPS, M, ns), out_dtypes[0])]
    else:
        o_specs = [pl.BlockSpec((tm, tn), lambda i, j: (i, j)) for _ in out_dtypes]
        o_shapes = [_out((M, N if out_cols is None else out_cols[n]), dt)
                    for n, dt in enumerate(out_dtypes)]
    e_specs = [pl.BlockSpec((tm, tn), lambda i, j: (i, j)) for _ in extras]
    e_specs += [pl.BlockSpec(v.shape, lambda i, j: (0, 0)) for v in bcast]
    o_specs += [pl.BlockSpec(s, lambda i, j: (0, 0)) for s in accs]
    o_shapes += [_out(s, F32) for s in accs]
    n_e, n_b, n_o, n_a = len(extras), len(bcast), len(out_dtypes), len(accs)
    order = _after_operand(after)
    n_x = len(order)
    if epi is None:
        epi = lambda acc: (acc,)

    def body(a_ref, b_ref, *rest):
        e_refs, b_refs = rest[:n_e], rest[n_e:n_e + n_b]
        o_refs = rest[n_e + n_b + n_x:n_e + n_b + n_x + n_o]
        a_refs = rest[n_e + n_b + n_x + n_o:]
        if n_a:
            @pl.when((pl.program_id(0) == 0) & (pl.program_id(1) == 0))
            def _():
                for r in a_refs:
                    r[...] = jnp.zeros(r.shape, F32)
        if mode == "nt" and b_shard:
            acc = None
            for s in range(N_CHIPS):
                part = lax.dot_general(a_ref[:, s * ks:(s + 1) * ks], b_ref[s], dims, preferred_element_type=F32)
                acc = part if acc is None else acc + part
        else:
            acc = lax.dot_general(a_ref[...], b_ref[...], dims, preferred_element_type=F32)
        if ref_epi is not None:
            ref_epi(acc, e_refs, b_refs, o_refs, a_refs)
            return
        outs = epi(acc, *[r[...] for r in e_refs])
        for r, o in zip(o_refs, outs):
            r[...] = o.astype(r.dtype)

    outs = pl.pallas_call(
        body, name=name, grid=grid, in_specs=[a_spec, b_spec] + e_specs + [_ANY] * n_x, out_specs=o_specs,
        out_shape=o_shapes, compiler_params=_params(("arbitrary", "arbitrary") if n_a else ("parallel", "parallel")),
    )(_hbm(a), _hbm(b), *map(_hbm, extras), *bcast, *order)
    return outs[0] if n_o + n_a == 1 else outs


def _rowwise(fn, rows, bcast, outs, accs=(), *, tm, name, after=None):
    def norm(r):
        return r if isinstance(r, tuple) else (r, r.shape[1], 0)

    rows = [norm(r) for r in rows]
    T = rows[0][0].shape[0]
    tm = min(tm, T)
    while T % tm:
        tm -= SUBLANES
    n_r, n_b, n_o, n_a = len(rows), len(bcast), len(outs), len(accs)
    order = _after_operand(after)
    n_x = len(order)
    in_specs = [pl.BlockSpec((tm, c), functools.partial(lambda i, cb: (i, cb), cb=cb)) for _, c, cb in rows]
    in_specs += [pl.BlockSpec(b.shape, lambda i: (0, 0)) for b in bcast] + [_ANY] * n_x
    out_specs = [pl.BlockSpec((tm, o[0]), lambda i: (i, 0)) for o in outs]
    out_specs += [pl.BlockSpec(s, lambda i: (0, 0)) for s in accs]
    out_shape = [_out((T, o[2] if len(o) > 2 else o[0]), o[1]) for o in outs]
    out_shape += [_out(s, F32) for s in accs]

    def body(*refs):
        in_refs = refs[:n_r]
        b_refs = refs[n_r:n_r + n_b]
        o_refs = refs[n_r + n_b + n_x:n_r + n_b + n_x + n_o]
        a_refs = refs[n_r + n_b + n_x + n_o:]
        if n_a:
            @pl.when(pl.program_id(0) == 0)
            def _():
                for r in a_refs:
                    r[...] = jnp.zeros(r.shape, F32)
        fn(in_refs, b_refs, o_refs, a_refs)

    res = pl.pallas_call(
        body, name=name, grid=(T // tm,), in_specs=in_specs, out_specs=out_specs, out_shape=out_shape,
        compiler_params=_params(("arbitrary",) if n_a else ("parallel",)),
    )(*[_hbm(r[0]) for r in rows], *bcast, *order)
    return res


def _rsum(x):
    return jnp.sum(x, axis=0, keepdims=True)


def _rms_fwd(x, g, name, after=None):
    def fn(ins, bs, outs, accs):
        xv = ins[0][...]
        r = lax.rsqrt(jnp.mean(xv * xv, axis=-1, keepdims=True) + EPS)
        outs[0][...] = (xv * r * bs[0][...]).astype(BF16)

    return _rowwise(fn, [x], [g], [(D_MODEL, BF16)], tm=512, name=name, after=after)[0]


def _rms_bwd_math(xv, dh, g):
    r = lax.rsqrt(jnp.mean(xv * xv, axis=-1, keepdims=True) + EPS)
    hn = xv * r
    dgh = dh * g
    dx = r * (dgh - hn * jnp.mean(dgh * hn, axis=-1, keepdims=True))
    return dx, _rsum(dh * hn)


def _mm_norm_bwd(dy, w, x, dres, g, name, after=None):
    def epilogue(acc, e_refs, b_refs, o_refs, a_refs):
        dx, dg = _rms_bwd_math(e_refs[0][...], acc, b_refs[0][...])
        dx = dx + e_refs[1][...]
        o_refs[0][...] = dx
        o_refs[1][...] = dx.astype(BF16)
        a_refs[0][...] += dg

    return _mm(dy, w, mode="nt", b_shard=True, out_dtypes=(F32, BF16), extras=(x, dres), bcast=(g,),
               accs=((1, D_MODEL),), ref_epi=epilogue, name=name, after=after)


def _mm_res_norm(a, w, res, g, name):
    def epilogue(acc, e_refs, b_refs, o_refs, a_refs):
        xv = acc + e_refs[0][...]
        o_refs[0][...] = xv
        r = lax.rsqrt(jnp.mean(xv * xv, axis=-1, keepdims=True) + EPS)
        o_refs[1][...] = (xv * r * b_refs[0][...]).astype(BF16)

    return _mm(a, w, mode="nn", out_dtypes=(F32, BF16), extras=(res,), bcast=(g,), ref_epi=epilogue, name=name)


def _mm_final_loss(a, w, res, target, g, name):
    def epilogue(acc, e_refs, b_refs, o_refs, a_refs):
        xv = acc + e_refs[0][...]
        gv = b_refs[0][...]
        r = lax.rsqrt(jnp.mean(xv * xv, axis=-1, keepdims=True) + EPS)
        e = xv * r * gv - e_refs[1][...]
        tok = jnp.mean(e * e, axis=-1, keepdims=True)
        a_refs[0][...] += 0.5 * jnp.sum(tok, axis=0, keepdims=True) * jnp.ones((1, LANES), F32)
        dx, dg = _rms_bwd_math(xv, e * (1.0 / D_MODEL), gv)
        o_refs[0][...] = dx
        o_refs[1][...] = dx.astype(BF16)
        a_refs[1][...] += dg

    return _mm(a, w, mode="nn", out_dtypes=(F32, BF16), extras=(res, target), bcast=(g,),
               accs=((1, LANES), (1, D_MODEL)), ref_epi=epilogue, name=name)


def _relu2(acc):
    r = jnp.maximum(acc, 0.0)
    return r * r, r


def _mlp_fwd(x, h, fetch, tag, finish):
    w_up = fetch(f"mlp{tag}_up", h)
    a, r = _mm(h, w_up, mode="nn", b_shard=True, out_dtypes=(BF16, BF16), epi=_relu2, name=f"mlp{tag}_up")
    w_down = fetch(f"mlp{tag}_down", a)
    return finish(a, w_down, x, f"mlp{tag}_down"), (h, a, r, w_up, w_down)


def _mlp_bwd(x, g, saved, dx, dx_bf, tag, after):
    h, a, r, w_up, w_down = saved
    d_down = _mm(a, dx_bf, mode="tn", out_dtypes=(BF16,), name=f"mlp{tag}_dwdown", after=after)
    dup = _mm(dx_bf, w_down, mode="nt", extras=(r,), out_dtypes=(BF16,),
              epi=lambda acc, rv: (acc * (2.0 * rv.astype(F32)),), name=f"mlp{tag}_dup")
    d_up = _mm(h, dup, mode="tn", o_shard=True, out_dtypes=(BF16,), name=f"mlp{tag}_dwup")
    dx_new, dx_new_bf, dg = _mm_norm_bwd(dup, w_up, x, dx, g, f"mlp{tag}_dh")
    return dx_new, dx_new_bf, dg, d_up, d_down


def _rope_tables(L, B):
    rows = L // GRID_W
    row = np.repeat(np.arange(rows, dtype=np.float32), GRID_W)
    col = np.tile(np.arange(GRID_W, dtype=np.float32), rows)
    inv = (ROPE_THETA ** (-np.arange(HEAD_DIM // 4, dtype=np.float32) / (HEAD_DIM // 4))).astype(np.float32)
    ar, ac = row[:, None] * inv, col[:, None] * inv
    cos = np.concatenate([np.cos(ar), np.cos(ar), np.cos(ac), np.cos(ac)], axis=-1)
    sin = np.concatenate([-np.sin(ar), np.sin(ar), -np.sin(ac), np.sin(ac)], axis=-1)
    return jnp.asarray(np.tile(cos, (B, 1)), F32), jnp.asarray(np.tile(sin, (B, 1)), F32)


def _swap_halves(x):
    lane = lax.broadcasted_iota(jnp.int32, x.shape, 1)
    return jnp.where((lane % 64) < 32, pltpu.roll(x, HEAD_DIM - 32, 1), pltpu.roll(x, 32, 1))


def _qk_prep(qkv, cos, sin, q_g, k_g):
    def fn(ins, bs, outs, accs):
        c, s = ins[1][...], ins[2][...]
        for h in range(N_HEADS + N_KV):
            xv = ins[0][:, h * HEAD_DIM:(h + 1) * HEAD_DIM]
            g = bs[0][...] if h < N_HEADS else bs[1][...]
            r = lax.rsqrt(jnp.mean(xv * xv, axis=-1, keepdims=True) + EPS)
            z = xv * r * g
            y = (z * c + _swap_halves(z) * s).astype(BF16)
            if h < N_HEADS:
                outs[0][:, h * HEAD_DIM:(h + 1) * HEAD_DIM] = y
            else:
                outs[1][:, (h - N_HEADS) * HEAD_DIM:(h - N_HEADS + 1) * HEAD_DIM] = y
        outs[2][...] = ins[0][:, (N_HEADS + N_KV) * HEAD_DIM:].astype(BF16)

    kvw = N_KV * HEAD_DIM
    return _rowwise(fn, [qkv, cos, sin], [q_g, k_g], [(D_MODEL, BF16), (kvw, BF16), (kvw, BF16)], tm=512,
                    name="attn_qk_prep")


def _qk_prep_bwd(qkv, dq, dk, dv, cos, sin, q_g, k_g):
    def fn(ins, bs, outs, accs):
        c, s = ins[4][...], ins[5][...]
        for h in range(N_HEADS + N_KV):
            sl = slice(h * HEAD_DIM, (h + 1) * HEAD_DIM)
            xv = ins[0][:, sl]
            if h < N_HEADS:
                g, dy, acc = bs[0][...], ins[1][:, sl], accs[0]
            else:
                ks = slice((h - N_HEADS) * HEAD_DIM, (h - N_HEADS + 1) * HEAD_DIM)
                g, dy, acc = bs[1][...], ins[2][:, ks], accs[1]
            r = lax.rsqrt(jnp.mean(xv * xv, axis=-1, keepdims=True) + EPS)
            xn = xv * r
            dz = dy * c - _swap_halves(dy) * s
            acc[...] += _rsum(dz * xn)
            dxn = dz * g
            outs[0][:, sl] = (r * (dxn - xn * jnp.mean(dxn * xn, axis=-1, keepdims=True))).astype(BF16)
        outs[0][:, (N_HEADS + N_KV) * HEAD_DIM:] = ins[3][...].astype(BF16)

    return _rowwise(fn, [qkv, dq, dk, dv, cos, sin], [q_g, k_g], [(qkv.shape[1], BF16)],
                    [(1, HEAD_DIM), (1, HEAD_DIM)], tm=256, name="attn_qk_prep_bwd")


_EXP2_SCALE = SCALE * math.log2(math.e)


def _exp_rows(q, k):
    s = lax.dot_general(q, k, _NT, preferred_element_type=F32)
    p = jnp.exp2((s - jnp.max(s, axis=-1, keepdims=True)) * _EXP2_SCALE)
    return p, jnp.sum(p, axis=-1, keepdims=True)


def _attn_fwd(q, k, v, B, L, tq=2048, sub=256):
    tq = min(tq, L)
    sub = min(sub, tq)
    nq = L // tq

    def body(q_ref, k_ref, v_ref, o_ref):
        kv, vv = k_ref[...], v_ref[...]
        for c in range(tq // sub):
            rows = slice(c * sub, (c + 1) * sub)
            p, l = _exp_rows(q_ref[rows, :], kv)
            o = jnp.dot(p.astype(BF16), vv, preferred_element_type=F32)
            o_ref[rows, :] = (o * (1.0 / l)).astype(o_ref.dtype)

    return pl.pallas_call(
        body, name="attn_fwd", grid=(B, N_HEADS, nq),
        in_specs=[pl.BlockSpec((tq, HEAD_DIM), lambda b, h, i: (b * nq + i, h)),
                  pl.BlockSpec((L, HEAD_DIM), lambda b, h, i: (b, h // GROUP)),
                  pl.BlockSpec((L, HEAD_DIM), lambda b, h, i: (b, h // GROUP))],
        out_specs=pl.BlockSpec((tq, HEAD_DIM), lambda b, h, i: (b * nq + i, h)),
        out_shape=_out((B * L, D_MODEL), BF16),
        compiler_params=_params(("parallel", "parallel", "parallel")),
    )(_hbm(q), _hbm(k), _hbm(v))


def _attn_bwd(q, k, v, do, B, L, tq=2048, sub=512):
    tq = min(tq, L)
    sub = min(sub, tq)
    nq = L // tq

    def body(q_ref, k_ref, v_ref, do_ref, dq_ref, dk_ref, dv_ref):
        @pl.when((pl.program_id(2) == 0) & (pl.program_id(3) == 0))
        def _():
            dk_ref[...] = jnp.zeros(dk_ref.shape, F32)
            dv_ref[...] = jnp.zeros(dv_ref.shape, F32)

        kv, vv = k_ref[...], v_ref[...]
        ps, es, dos, qs = [], [], [], []
        for c in range(tq // sub):
            rows = slice(c * sub, (c + 1) * sub)
            qc, doc = q_ref[rows, :], do_ref[rows, :]
            p, l = _exp_rows(qc, kv)
            inv = 1.0 / l
            dp = lax.dot_general(doc, vv, _NT, preferred_element_type=F32)
            delta = jnp.sum(p * dp, axis=-1, keepdims=True) * inv
            e = (p * (dp - delta)).astype(BF16)
            dq_ref[rows, :] = jnp.dot(e, kv, preferred_element_type=F32) * (inv * SCALE)
            ps.append(p.astype(BF16))
            es.append(e)
            dos.append((doc.astype(F32) * inv).astype(BF16))
            qs.append((qc.astype(F32) * (inv * SCALE)).astype(BF16))
        cat = lambda xs: xs[0] if len(xs) == 1 else jnp.concatenate(xs, axis=0)
        dv_ref[...] += lax.dot_general(cat(ps), cat(dos), _TN, preferred_element_type=F32)
        dk_ref[...] += lax.dot_general(cat(es), cat(qs), _TN, preferred_element_type=F32)

    qmap = lambda b, kh, g, i: (b * nq + i, kh * GROUP + g)
    kmap = lambda b, kh, g, i: (b, kh)
    kvw = N_KV * HEAD_DIM
    return pl.pallas_call(
        body, name="attn_bwd", grid=(B, N_KV, GROUP, nq),
        in_specs=[pl.BlockSpec((tq, HEAD_DIM), qmap), pl.BlockSpec((L, HEAD_DIM), kmap),
                  pl.BlockSpec((L, HEAD_DIM), kmap), pl.BlockSpec((tq, HEAD_DIM), qmap)],
        out_specs=[pl.BlockSpec((tq, HEAD_DIM), qmap), pl.BlockSpec((L, HEAD_DIM), kmap),
                   pl.BlockSpec((L, HEAD_DIM), kmap)],
        out_shape=[_out((B * L, D_MODEL), F32), _out((B * L, kvw), F32),
                   _out((B * L, kvw), F32)],
        compiler_params=_params(("parallel", "parallel", "arbitrary", "arbitrary")),
    )(_hbm(q), _hbm(k), _hbm(v), _hbm(do))


def _conv_shift(x, t, L, k):
    if k == 2:
        return x
    if k < 2:
        return jnp.where(t >= 2 - k, pltpu.roll(x, 2 - k, 0), 0.0)
    return jnp.where(t < L - (k - 2), pltpu.roll(x, L - (k - 2), 0), 0.0)


def _conv_apply(x, w_ref, L):
    t = lax.broadcasted_iota(jnp.int32, x.shape, 0)
    acc = w_ref[4:5, :] + w_ref[2:3, :] * x
    for k in (0, 1, 3):
        acc = acc + w_ref[k:k + 1, :] * _conv_shift(x, t, L, k)
    return acc


def _conv_fwd(z, wb, B, L, tc=256):
    noff = D_MODEL // tc

    def body(z_ref, w_ref, o_ref):
        o_ref[...] = _conv_apply(z_ref[...], w_ref, L)

    return pl.pallas_call(
        body, name="rg_conv", grid=(B, noff),
        in_specs=[pl.BlockSpec((L, tc), lambda b, j: (b, noff + j)), pl.BlockSpec((SUBLANES, tc), lambda b, j: (0, j))],
        out_specs=pl.BlockSpec((L, tc), lambda b, j: (b, j)),
        out_shape=_out((B * L, D_MODEL), F32),
        compiler_params=_params(("parallel", "parallel")),
    )(z, wb)


def _conv_bwd(z, g, wb, dz, B, L, tc=256, after=None):
    noff = D_MODEL // tc
    order = _after_operand(after)

    def body(z_ref, g_ref, w_ref, dz_in, *rest):
        dx_ref, dw_ref = rest[len(order):]

        @pl.when(pl.program_id(1) == 0)
        def _():
            dw_ref[...] = jnp.zeros(dw_ref.shape, F32)

        x, gv = z_ref[...], g_ref[...]
        t = lax.broadcasted_iota(jnp.int32, x.shape, 0)
        dx = w_ref[2:3, :] * gv
        for k in (0, 1, 3):
            dx = dx + w_ref[k:k + 1, :] * _conv_shift(gv, t, L, 4 - k)
        dx_ref[...] = dx.astype(BF16)
        for k in range(4):
            dw_ref[k:k + 1, :] += _rsum(_conv_shift(x, t, L, k) * gv)
        dw_ref[4:5, :] += _rsum(gv)

    return pl.pallas_call(
        body, name="rg_conv_bwd", grid=(noff, B),
        in_specs=[pl.BlockSpec((L, tc), lambda j, b: (b, noff + j)), pl.BlockSpec((L, tc), lambda j, b: (b, j)),
                  pl.BlockSpec((SUBLANES, tc), lambda j, b: (0, j)), _ANY] + [_ANY] * len(order),
        out_specs=[pl.BlockSpec((L, tc), lambda j, b: (b, noff + j)),
                   pl.BlockSpec((SUBLANES, tc), lambda j, b: (0, j))],
        out_shape=[_out(dz.shape, dz.dtype), _out((SUBLANES, D_MODEL), F32)],
        input_output_aliases={3: 0},
        compiler_params=_params(("parallel", "arbitrary")),
    )(_hbm(z), _hbm(g), wb, _hbm(dz), *order)


def _softplus(x):
    return jnp.maximum(x, 0.0) + jnp.log1p(jnp.exp(-jnp.abs(x)))


_ROW_BA, _ROW_BX, _ROW_LAM = 0, 2, 4


def _gate_math(xb, pre, vec_ref, d, sl):
    pa = pre[:, (2 * d) * LRU_BW:(2 * d + 1) * LRU_BW] + vec_ref[_ROW_BA + d:_ROW_BA + d + 1, sl]
    px = pre[:, (2 * d + 1) * LRU_BW:(2 * d + 2) * LRU_BW] + vec_ref[_ROW_BX + d:_ROW_BX + d + 1, sl]
    r = 0.5 * jnp.tanh(0.5 * pa) + 0.5
    i = 0.5 * jnp.tanh(0.5 * px) + 0.5
    sp = _softplus(-vec_ref[_ROW_LAM + d:_ROW_LAM + d + 1, sl])
    log_a = (-RG_C) * r * sp
    a = jnp.exp(log_a)
    th = jnp.tanh(log_a)
    om = -2.0 * th / (1.0 - th)
    mult = jnp.sqrt(om)
    return a, mult * (i * xb), (r, i, sp, om, mult)


def _gate_fwd(rec, wcat, gvec):
    def fn(ins, bs, outs, accs):
        for blk in range(LRU_BLOCKS):
            sl = slice(blk * LRU_BW, (blk + 1) * LRU_BW)
            xb = ins[0][:, sl]
            pre = jnp.dot(xb.astype(BF16), bs[0][sl, :], preferred_element_type=F32)
            for d in range(2):
                a, u, _ = _gate_math(xb, pre, bs[1], d, sl)
                outs[2 * d][:, sl] = a
                outs[2 * d + 1][:, sl] = u

    return _rowwise(fn, [rec], [wcat, gvec], [(D_MODEL, F32)] * 4, tm=256, name="rg_gate")


def _gate_bwd(rec, du_f, da_f, du_b, da_b, wcat, gvec):
    def fn(ins, bs, outs, accs):
        for blk in range(LRU_BLOCKS):
            sl = slice(blk * LRU_BW, (blk + 1) * LRU_BW)
            xb = ins[0][:, sl]
            xb16 = xb.astype(BF16)
            w = bs[0][sl, :]
            pre = jnp.dot(xb16, w, preferred_element_type=F32)
            dx = jnp.zeros_like(xb)
            dpre = []
            for d in range(2):
                a, _, (r, i, sp, om, mult) = _gate_math(xb, pre, bs[1], d, sl)
                du, da = ins[1 + 2 * d][:, sl], ins[2 + 2 * d][:, sl]
                d_i = du * mult * xb
                d_mult = du * i * xb
                dx = dx + du * mult * i
                dlog = da * a - d_mult * (1.0 - om) / mult
                d_r = dlog * ((-RG_C) * sp)
                d_sp = _rsum(dlog * ((-RG_C) * r))
                lam = bs[1][_ROW_LAM + d:_ROW_LAM + d + 1, sl]
                accs[2][_ROW_LAM + d:_ROW_LAM + d + 1, sl] += d_sp * (-jax.nn.sigmoid(-lam))
                dpa = d_r * r * (1.0 - r)
                dpx = d_i * i * (1.0 - i)
                accs[2][_ROW_BA + d:_ROW_BA + d + 1, sl] += _rsum(dpa)
                accs[2][_ROW_BX + d:_ROW_BX + d + 1, sl] += _rsum(dpx)
                dpre += [dpa, dpx]
            dpre = jnp.concatenate(dpre, axis=1).astype(BF16)
            dw = lax.dot_general(xb16, dpre, _TN, preferred_element_type=F32)
            for d in range(2):
                rows = slice(d * D_MODEL + blk * LRU_BW, d * D_MODEL + (blk + 1) * LRU_BW)
                accs[0][rows, :] += dw[:, (2 * d) * LRU_BW:(2 * d + 1) * LRU_BW]
                accs[1][rows, :] += dw[:, (2 * d + 1) * LRU_BW:(2 * d + 2) * LRU_BW]
            outs[0][:, sl] = dx + lax.dot_general(dpre, w, _NT, preferred_element_type=F32)

    gate_shape = (2 * D_MODEL, LRU_BW)
    return _rowwise(fn, [rec, du_f, da_f, du_b, da_b], [wcat, gvec], [(D_MODEL, F32)],
                    [gate_shape, gate_shape, (SUBLANES, D_MODEL)], tm=256, name="rg_gate_bwd")


def _as_time_blocks(x):
    return x.reshape(x.shape[0] // SUBLANES, SUBLANES, x.shape[1])


def _scan_call(body, ins, n_out, B, L, tc, name):
    nb = L // SUBLANES
    spec = pl.BlockSpec((nb, SUBLANES, tc), lambda b, j: (b, 0, j))
    T = ins[0].shape[0]
    outs = pl.pallas_call(
        functools.partial(body, nb), name=name, grid=(B, D_MODEL // tc),
        in_specs=[spec] * len(ins), out_specs=[spec] * n_out,
        out_shape=[_out((T // SUBLANES, SUBLANES, D_MODEL), F32)] * n_out,
        compiler_params=_params(("parallel", "parallel")),
    )(*[_hbm(_as_time_blocks(x)) for x in ins])
    return [o.reshape(T, D_MODEL) for o in outs]


def _block_scan(A, U, reverse):
    row = lax.broadcasted_iota(jnp.int32, A.shape, 0)
    for s in (1, 2, 4):
        shift = SUBLANES - s if reverse else s
        valid = (row < SUBLANES - s) if reverse else (row >= s)
        a_sh = jnp.where(valid, pltpu.roll(A, shift, 0), 1.0)
        u_sh = jnp.where(valid, pltpu.roll(U, shift, 0), 0.0)
        U = A * u_sh + U
        A = A * a_sh
    return A, U


_LAST = SUBLANES - 1
SCAN_UNROLL = 8


def _loop_blocks(nb, step, init):
    def group(g, carry):
        for k in range(SCAN_UNROLL):
            carry = step(g * SCAN_UNROLL + k, carry)
        return carry

    return lax.fori_loop(0, nb // SCAN_UNROLL, group, init)


def _scan_fwd(a_f, u_f, a_b, u_b, B, L, tc=256):
    def body(nb, af, uf, ab, ub, hf, hb):
        def step(i, carry):
            c1, c2 = carry
            ib = nb - 1 - i
            p, h = _block_scan(af[i], uf[i], False)
            h = h + p * c1
            hf[i] = h
            p2, h2 = _block_scan(ab[ib], ub[ib], True)
            h2 = h2 + p2 * c2
            hb[ib] = h2
            return h[_LAST:, :], h2[:1, :]

        zero = jnp.zeros((1, tc), F32)
        _loop_blocks(nb, step, (zero, zero))

    return _scan_call(body, [a_f, u_f, a_b, u_b], 2, B, L, tc, "rg_scan")


def _scan_bwd(dy, a_f, h_f, a_b, h_b, B, L, tc=256):
    def body(nb, dy_r, af, hf, ab, hb, duf, daf, dub, dab):
        def step(i, carry):
            c1, c2 = carry
            ir = nb - 1 - i
            row = lax.broadcasted_iota(jnp.int32, (SUBLANES, tc), 0)
            a_up = jnp.where(row == _LAST, af[jnp.minimum(ir + 1, nb - 1), :1, :], pltpu.roll(af[ir], _LAST, 0))
            p, lam = _block_scan(a_up, dy_r[ir], True)
            lam = lam + p * c1
            before = hf[jnp.maximum(ir - 1, 0), _LAST:, :] * (ir > 0).astype(F32)
            duf[ir] = lam
            daf[ir] = lam * jnp.where(row == 0, before, pltpu.roll(hf[ir], 1, 0))
            a_dn = jnp.where(row == 0, ab[jnp.maximum(i - 1, 0), _LAST:, :], pltpu.roll(ab[i], 1, 0))
            p2, lam2 = _block_scan(a_dn, dy_r[i], False)
            lam2 = lam2 + p2 * c2
            after = hb[jnp.minimum(i + 1, nb - 1), :1, :] * (i < nb - 1).astype(F32)
            dub[i] = lam2
            dab[i] = lam2 * jnp.where(row == _LAST, after, pltpu.roll(hb[i], _LAST, 0))
            return lam[:1, :], lam2[_LAST:, :]

        zero = jnp.zeros((1, tc), F32)
        _loop_blocks(nb, step, (zero, zero))

    return _scan_call(body, [dy, a_f, h_f, a_b, h_b], 4, B, L, tc, "rg_scan_bwd")


_GELU_C = math.sqrt(2.0 / math.pi)


def _gelu_parts(x):
    th = jnp.tanh(_GELU_C * (x + 0.044715 * x * x * x))
    return 0.5 * x * (1.0 + th), th


def _gated_out(h_f, h_b, z):
    def fn(ins, bs, outs, accs):
        gl, _ = _gelu_parts(ins[2][...])
        outs[0][...] = ((ins[0][...] + ins[1][...]) * gl).astype(BF16)

    return _rowwise(fn, [h_f, h_b, (z, D_MODEL, 0)], [], [(D_MODEL, BF16)], tm=512, name="rg_gated_out")[0]


def _mm_gated_out_bwd(dx, w_out, h_f, h_b, z, name, after=None):
    def epilogue(acc, e_refs, b_refs, o_refs, a_refs):
        x = e_refs[2][...]
        gl, th = _gelu_parts(x)
        dgl = 0.5 * (1.0 + th) + 0.5 * x * (1.0 - th * th) * (_GELU_C * (1.0 + 3.0 * 0.044715 * x * x))
        o_refs[0][...] = acc * gl
        o_refs[1][...] = (acc * (e_refs[0][...] + e_refs[1][...]) * dgl).astype(BF16)

    return _mm(dx, w_out, mode="nt", out_dtypes=(F32, BF16), out_cols=(D_MODEL, 2 * D_MODEL), extras=(h_f, h_b, z),
               ref_epi=epilogue, name=name, after=after)


def _row_block(i):
    return pl.ds(pl.multiple_of(i * SUBLANES, SUBLANES), SUBLANES)


def _rg_mix_fwd(z, conv_wb, wcat, gvec, B, L):
    nb = L // SUBLANES
    n_g = D_MODEL // LRU_BW

    def body(zg_ref, zr_ref, cw_ref, w_ref, gv_ref, rec_ref, af_s, ab_s, hf_ref, hb_ref, yg_ref, uf_s, ub_s):
        rec = _conv_apply(zr_ref[...], cw_ref, L)
        rec_ref[...] = rec
        pre = jnp.dot(rec.astype(BF16), w_ref[...], preferred_element_type=F32)
        for d, (a_s, u_s) in enumerate(((af_s, uf_s), (ab_s, ub_s))):
            a, u, _ = _gate_math(rec, pre, gv_ref, d, slice(None))
            a_s[...] = a
            u_s[...] = u

        def step(i, carry):
            c1, c2 = carry
            rows, rows_b = _row_block(i), _row_block(nb - 1 - i)
            p, h = _block_scan(af_s[rows, :], uf_s[rows, :], False)
            h = h + p * c1
            hf_ref[rows, :] = h
            p2, h2 = _block_scan(ab_s[rows_b, :], ub_s[rows_b, :], True)
            h2 = h2 + p2 * c2
            hb_ref[rows_b, :] = h2
            return h[_LAST:, :], h2[:1, :]

        zero = jnp.zeros((1, LRU_BW), F32)
        _loop_blocks(nb, step, (zero, zero))
        gl, _ = _gelu_parts(zg_ref[...])
        yg_ref[...] = ((hf_ref[...] + hb_ref[...]) * gl).astype(BF16)

    seq = lambda off: pl.BlockSpec((L, LRU_BW), lambda b, g: (b, off + g))
    vec = pl.BlockSpec((SUBLANES, LRU_BW), lambda b, g: (0, g))
    T = B * L
    return pl.pallas_call(
        body, name="rg_mix", grid=(B, n_g),
        in_specs=[seq(0), seq(n_g), vec, pl.BlockSpec((LRU_BW, 4 * LRU_BW), lambda b, g: (g, 0)), vec],
        out_specs=[seq(0)] * 6,
        out_shape=[_out((T, D_MODEL), F32)] * 5 + [_out((T, D_MODEL), BF16)],
        scratch_shapes=[pltpu.VMEM((L, LRU_BW), F32)] * 2,
        compiler_params=_params(("parallel", "parallel")),
    )(_hbm(z), _hbm(z), conv_wb, wcat, gvec)


def _make_wcat(w_a, w_x):
    g = jnp.stack([w_a[0, 0], w_x[0, 0], w_a[0, 1], w_x[0, 1]])
    return jnp.transpose(g, (1, 2, 0, 3)).reshape(D_MODEL, 4 * LRU_BW)


def _rows_at(part, first):
    return jnp.pad(part, ((first, SUBLANES - first - part.shape[0]), (0, 0)))


def _qk_slot(q_g, k_g):
    wide = lambda v, at: jnp.pad(v, ((0, SUBLANES - 1), (at, D_MODEL - at - HEAD_DIM)))
    return wide(q_g, 0) + wide(k_g, HEAD_DIM)


def _local_step(x, target, P, fetch, emit, B, L, after=None):
    g_mix, g_mlp = P["norm_mix_g"], P["norm_mlp_g"]
    h0 = _rms_fwd(x, g_mix[0:1], "rg_norm", after=after)
    w_in, conv_wb, wcat, gvec = fetch("rg", h0)
    z = _mm(h0, w_in, mode="nn", b_shard=True, name="rg_in")
    rec, a_f, a_b, h_f, h_b, yg = _rg_mix_fwd(z, conv_wb, wcat, gvec, B, L)
    w_out = fetch("rg_out", yg)
    x1, h1 = _mm_res_norm(yg, w_out, x, g_mlp[0:1], "rg_out")
    (x2, h3), mlp0 = _mlp_fwd(x1, h1, fetch, 0, lambda a, w, res, name: _mm_res_norm(a, w, res, g_mix[1:2], name))
    w_qkv, w_o = fetch("att", h3)
    qkv = _mm(h3, w_qkv, mode="nn", b_shard=True, name="attn_qkv")
    cos, sin = _rope_tables(L, B)
    qh, kh, vh = _qk_prep(qkv, cos, sin, P["q_g"], P["k_g"])
    o = _attn_fwd(qh, kh, vh, B, L)
    x3, h4 = _mm_res_norm(o, w_o, x2, g_mlp[1:2], "attn_out")
    (dx4, dx4_bf, loss_acc, d_final_g), mlp1 = _mlp_fwd(
        x3, h4, fetch, 1, lambda a, w, res, name: _mm_final_loss(a, w, res, target, P["final_g"], name))

    dx3, dx3_bf, dg_mlp1, d_up1, d_down1 = _mlp_bwd(x3, g_mlp[1:2], mlp1, dx4, dx4_bf, 1, None)
    tok = emit("mlp1", [d_up1, d_down1])
    d_wo = _mm(o, dx3_bf, mode="tn", out_dtypes=(BF16,), name="attn_dwo", after=tok)
    do = _mm(dx3_bf, w_o, mode="nt", out_dtypes=(BF16,), name="attn_do")
    dq, dk, dv = _attn_bwd(qh, kh, vh, do, B, L)
    dqkv, dq_g, dk_g = _qk_prep_bwd(qkv, dq, dk, dv, cos, sin, P["q_g"], P["k_g"])
    d_wqkv = _mm(h3, dqkv, mode="tn", o_shard=True, out_dtypes=(BF16,), name="attn_dwqkv")
    tok = emit("att", [d_wqkv, d_wo])
    dx2, dx2_bf, dg_mix1 = _mm_norm_bwd(dqkv, w_qkv, x2, dx3, g_mix[1:2], "attn_dh", after=tok)
    tok = emit("point_attn_done", [dx2_bf])
    dx1, dx1_bf, dg_mlp0, d_up0, d_down0 = _mlp_bwd(x1, g_mlp[0:1], mlp0, dx2, dx2_bf, 0, tok)
    d_wout = _mm(yg, dx1_bf, mode="tn", out_dtypes=(BF16,), name="rg_dwout")
    tok = emit("mlp0", [d_up0, d_down0, d_wout])
    dy, dgate = _mm_gated_out_bwd(dx1_bf, w_out, h_f, h_b, z, "rg_dyg", after=tok)
    du_f, da_f, du_b, da_b = _scan_bwd(dy, a_f, h_f, a_b, h_b, B, L)
    drec_c, d_wa, d_wx, d_gvec = _gate_bwd(rec, du_f, da_f, du_b, da_b, wcat, gvec)
    tok = emit("gates", [d_wa, d_wx])
    dz, d_convwb = _conv_bwd(z, drec_c, conv_wb, dgate, B, L, after=tok)
    tok = emit("point_mix_done", [dz])
    d_win = _mm(h0, dz, mode="tn", o_shard=True, out_dtypes=(BF16,), name="rg_dwin", after=tok)
    tok = emit("rg_in", [d_win])
    grad_x, _, dg_mix0 = _mm_norm_bwd(dz, w_in, x, dx1, g_mix[0:1], "rg_dh", after=tok)

    norms = (_rows_at(dg_mix0, 0) + _rows_at(dg_mix1, 1) + _rows_at(dg_mlp0, 2) + _rows_at(dg_mlp1, 3)
             + _rows_at(d_final_g, 4)
             + jnp.pad(loss_acc, ((LOSS_ROW, SUBLANES - 1 - LOSS_ROW), (0, D_MODEL - LANES))))
    vec = jnp.concatenate([norms, d_convwb, d_gvec, _qk_slot(dq_g, dk_g)], axis=0)
    return grad_x, vec


_MESH = pl.DeviceIdType.MESH


def _place():
    x, y, c = lax.axis_index("x"), lax.axis_index("y"), lax.axis_index("c")
    peers = [((1 - x) if j & 2 else x, (1 - y) if j & 1 else y) for j in (1, 2, 3)]
    return x, y, c, peers


def _sum_leading(slots, name):
    def body(s_ref, o_ref):
        acc = s_ref[0]
        for d in range(1, slots.shape[0]):
            acc = acc + s_ref[d]
        o_ref[...] = acc

    return pl.pallas_call(body, name=name, out_shape=jax.ShapeDtypeStruct(slots.shape[1:], slots.dtype))(slots)


_HBM = pl.BlockSpec(memory_space=pltpu.HBM)
_SEM = pl.BlockSpec(memory_space=pltpu.SEMAPHORE)
_EFFECT = pltpu.SideEffectType.DATAFLOW_SIDE_EFFECTING


_COPIES = dict(gather=N_CHIPS - 1, scatter=N_CHIPS - 1, swap=1, spread=N_DEVICES - 1,
               gather_half=N_CHIPS - 1, share_half=N_CHIPS - 1)


def _split_copies(kind, srcs, lands, send, recv):
    x, y, c, peers = _place()
    me = 2 * x + y
    per = _COPIES[kind]
    out = []
    for a in range(len(lands)):
        for j in range(per):
            if kind == "swap":
                src, there, here, dev = srcs[a], lands[a], lands[a], (x, y, 1 - c)
            elif kind == "spread":
                k = j + 1
                dev = ((1 - x) if k & 4 else x, (1 - y) if k & 2 else y, (1 - c) if k & 1 else c)
                mine = lands[a].at[4 * x + 2 * y + c]
                src, there, here = mine, mine, lands[a].at[4 * dev[0] + 2 * dev[1] + dev[2]]
            else:
                px, py = peers[j]
                dev = (px, py, c)
                if kind == "gather":
                    src, there, here = lands[a].at[me], lands[a].at[me], lands[a].at[2 * px + py]
                elif kind in ("gather_half", "share_half"):
                    half = lands[a].shape[1] // 2
                    mine, other = pl.ds(c * half, half), pl.ds((1 - c) * half, half)
                    if kind == "gather_half":
                        src = there = lands[a].at[me, mine]
                        here = lands[a].at[2 * px + py, mine]
                    else:
                        src = there = lands[a].at[2 * px + py, mine]
                        here = lands[a].at[2 * px + py, other]
                        dev = (x, y, 1 - c)
                else:
                    src, there, here = srcs[a].at[2 * px + py], lands[a].at[j], lands[a].at[j]
            mk = functools.partial(
                pltpu.make_async_remote_copy, src_ref=src, send_sem=send.at[per * a + j],
                recv_sem=recv.at[per * a + j], device_id=dev, device_id_type=_MESH)
            out.append((functools.partial(mk, dst_ref=there), functools.partial(mk, dst_ref=here)))
    return out


def _exchange_start(kind, srcs, lands, name, after=None):
    arrays = list(srcs) + list(lands)
    n_s, n, n_all = len(srcs), len(lands), len(srcs) + len(lands)
    n_sem = _COPIES[kind] * n
    order = _after_operand(after)
    n_x = len(order)

    def body(*refs):
        send, recv = refs[n_all + n_x], refs[n_all + n_x + 1]
        token = refs[-1]
        for started, _ in _split_copies(kind, refs[:n_s], refs[n_s:n_all], send, recv):
            started().start()
        token[...] = jnp.zeros(token.shape, F32)

    res = pl.pallas_call(
        body, name=name,
        out_shape=(pltpu.SemaphoreType.DMA((n_sem,)), pltpu.SemaphoreType.DMA((n_sem,)),
                   *[pltpu.HBM(a.shape, a.dtype) for a in arrays], jax.ShapeDtypeStruct((SUBLANES, LANES), F32)),
        in_specs=[_HBM] * n_all + [_ANY] * n_x,
        out_specs=(_SEM, _SEM, *[_HBM] * n_all, pl.BlockSpec(memory_space=pltpu.VMEM)),
        input_output_aliases={i: 2 + i for i in range(n_all)},
        compiler_params=pltpu.CompilerParams(has_side_effects=_EFFECT),
    )(*[pltpu.with_memory_space_constraint(a, pltpu.HBM) for a in arrays], *order)
    return (res[0], res[1], res[2:2 + n_s], res[2 + n_s:2 + n_all]), res[-1]


def _gather_start_groups(land_groups, name, after=None, kind="gather"):
    arrays = [a for group in land_groups for a in group]
    n_all, n_g = len(arrays), len(land_groups)
    order = _after_operand(after)
    n_x = len(order)

    def body(*refs):
        first = 0
        for gi, group in enumerate(land_groups):
            send, recv = refs[n_all + n_x + 2 * gi], refs[n_all + n_x + 2 * gi + 1]
            for started, _ in _split_copies(kind, [], refs[first:first + len(group)], send, recv):
                started().start()
            first += len(group)
        refs[-1][...] = jnp.zeros(refs[-1].shape, F32)

    sems = [pltpu.SemaphoreType.DMA((_COPIES[kind] * len(group),)) for group in land_groups for _ in range(2)]
    res = pl.pallas_call(
        body, name=name,
        out_shape=(*sems, *[pltpu.HBM(a.shape, a.dtype) for a in arrays], jax.ShapeDtypeStruct((SUBLANES, LANES), F32)),
        in_specs=[_HBM] * n_all + [_ANY] * n_x,
        out_specs=(*[_SEM] * (2 * n_g), *[_HBM] * n_all, pl.BlockSpec(memory_space=pltpu.VMEM)),
        input_output_aliases={i: 2 * n_g + i for i in range(n_all)},
        compiler_params=pltpu.CompilerParams(has_side_effects=_EFFECT),
    )(*[pltpu.with_memory_space_constraint(a, pltpu.HBM) for a in arrays], *order)
    handles, first = [], 2 * n_g
    for gi, group in enumerate(land_groups):
        handles.append((res[2 * gi], res[2 * gi + 1], [], res[first:first + len(group)]))
        first += len(group)
    return handles, res[-1]


def _exchange_wait(kind, handle, after, name):
    send, recv, srcs, lands = handle
    arrays = list(srcs) + list(lands)
    n_s, n_all = len(srcs), len(arrays)
    order = list(after) if isinstance(after, (list, tuple)) else [after]

    def body(*refs):
        for started, landing in _split_copies(kind, refs[:n_s], refs[n_s:n_all], refs[n_all], refs[n_all + 1]):
            started().wait_send()
            landing().wait_recv()

    res = pl.pallas_call(
        body, name=name, out_shape=[pltpu.HBM(a.shape, a.dtype) for a in arrays],
        in_specs=[_HBM] * n_all + [_SEM, _SEM] + [_ANY] * len(order), out_specs=[_HBM] * n_all,
        input_output_aliases={i: i for i in range(n_all)},
        compiler_params=pltpu.CompilerParams(has_side_effects=_EFFECT),
    )(*arrays, send, recv, *order)
    return res[:n_s], res[n_s:]


def _index_operand(i):
    return jnp.reshape(i, (1,)).astype(jnp.int32)


def _cast_into_slot(src, row0, rows, me, dtype, name, after=None, add=None, n_slots=N_CHIPS):
    cols = src.shape[1]
    tm = min(512, rows)
    order = _after_operand(after)
    terms = [src] + ([] if add is None else [add])

    def body(me_ref, *rest):
        val = rest[0][...]
        if add is not None:
            val = val + rest[1][...]
        rest[-1][...] = val.astype(dtype)

    return pl.pallas_call(
        body, name=name,
        grid_spec=pltpu.PrefetchScalarGridSpec(
            num_scalar_prefetch=1, grid=(rows // tm,),
            in_specs=[pl.BlockSpec((tm, cols), lambda i, me_ref: (i + row0 // tm, 0))] * len(terms)
            + [_ANY] * len(order),
            out_specs=pl.BlockSpec((None, tm, cols), lambda i, me_ref: (me_ref[0], i, 0))),
        out_shape=_out((n_slots, rows, cols), dtype), compiler_params=_params(("parallel",)),
    )(_index_operand(me), *map(_hbm, terms), *order)


def _sum_slots(mine, r, me, name):
    _, rows, cols = r.shape
    tm = min(512, rows)

    def body(me_ref, own_ref, r_ref, o_ref):
        o_ref[...] = ((own_ref[...].astype(F32) + r_ref[0].astype(F32)) + r_ref[1].astype(F32)) + r_ref[2].astype(F32)

    return pl.pallas_call(
        body, name=name,
        grid_spec=pltpu.PrefetchScalarGridSpec(
            num_scalar_prefetch=1, grid=(rows // tm,),
            in_specs=[pl.BlockSpec((None, tm, cols), lambda i, me_ref: (me_ref[0], i, 0)),
                      pl.BlockSpec((N_CHIPS - 1, tm, cols), lambda i, me_ref: (0, i, 0))],
            out_specs=pl.BlockSpec((tm, cols), lambda i, me_ref: (i, 0))),
        out_shape=_out((rows, cols), F32), compiler_params=_params(("parallel",)),
    )(_index_operand(me), _hbm(mine), _hbm(r))


def _adamw(w, m, v, ps, qs, name):
    rows, cols = w.shape
    seg_rows = ps[0].shape[0]
    tm = min(256, seg_rows)
    while seg_rows % tm:
        tm -= SUBLANES
    per, n_seg = seg_rows // tm, len(ps)
    parts = list(ps) + ([] if qs is None else list(qs))

    def body(w_ref, m_ref, v_ref, *rest):
        g_refs, outs = rest[:len(parts)], rest[len(parts):]
        grad = lambda s: g_refs[s][...] if qs is None else g_refs[s][...] + g_refs[n_seg + s][...]
        g = grad(0)
        for s in range(1, n_seg):
            g = jnp.where(pl.program_id(0) >= s * per, grad(s), g)
        m1 = ADAM_B1 * m_ref[...] + (1.0 - ADAM_B1) * g
        v1 = ADAM_B2 * v_ref[...] + (1.0 - ADAM_B2) * (g * g)
        m_hat = m1 / (1.0 - ADAM_B1 ** ADAM_STEP)
        v_hat = v1 / (1.0 - ADAM_B2 ** ADAM_STEP)
        outs[0][...] = g
        outs[1][...] = (-ADAM_LR) * (m_hat / (jnp.sqrt(v_hat) + ADAM_EPS) + ADAM_WD * w_ref[...])
        outs[2][...] = m1
        outs[3][...] = v1

    row_spec = pl.BlockSpec((tm, cols), lambda i: (i, 0))
    seg_spec = lambda s: pl.BlockSpec((tm, cols), lambda i: (jnp.clip(i - s * per, 0, per - 1), 0))
    return pl.pallas_call(
        body, name=name, grid=(rows // tm,),
        in_specs=[row_spec] * 3 + [seg_spec(s) for s in range(n_seg)] * (1 if qs is None else 2),
        out_specs=[row_spec] * 4, out_shape=[_out((rows, cols), F32)] * 4,
        compiler_params=_params(("arbitrary",)),
    )(*map(_hbm, (w, m, v, *parts)))


def _put_cols(shard, me):
    full = jnp.zeros((shard.shape[0], D_MODEL), F32)
    return lax.dynamic_update_slice(full, shard, (0, me * (D_MODEL // N_CHIPS)))


def _gate_vec_slot(b_a, b_x, lam):
    return _rows_at(b_a, _ROW_BA) + _rows_at(b_x, _ROW_BX) + _rows_at(lam, _ROW_LAM)


def _pack_vec(p, me):
    return jnp.concatenate([
        _rows_at(p["norm_mix_g"], 0) + _rows_at(p["norm_mlp_g"], 2) + _rows_at(p["final_g"][None], 4),
        _rows_at(_put_cols(p["rg_conv_w"][0, :, 0, :], me), 0) + _rows_at(p["rg_conv_b"], 4),
        _gate_vec_slot(_put_cols(p["rg_b_a"][0], me), _put_cols(p["rg_b_x"][0], me), _put_cols(p["rg_lam"][0], me)),
        _qk_slot(p["at_q_g"], p["at_k_g"]),
    ], axis=0)


def _unpack_vec(r, me):
    def cols(rows):
        return lax.dynamic_slice(rows, (0, me * (D_MODEL // N_CHIPS)), (rows.shape[0], D_MODEL // N_CHIPS))

    gate = r[16:24]
    return dict(
        norm_mix_g=r[0:2], norm_mlp_g=r[2:4], final_g=r[4], rg_conv_w=cols(r[8:12])[None, :, None, :],
        rg_conv_b=r[12:13], rg_b_a=cols(gate[_ROW_BA:_ROW_BA + 2])[None], rg_b_x=cols(gate[_ROW_BX:_ROW_BX + 2])[None],
        rg_lam=cols(gate[_ROW_LAM:_ROW_LAM + 2])[None], at_q_g=r[24:25, 0:HEAD_DIM],
        at_k_g=r[24:25, HEAD_DIM:2 * HEAD_DIM])


_WEIGHTS = ['norm_mix_g', 'norm_mlp_g', 'rg_w_in', 'rg_conv_w', 'rg_conv_b', 'rg_w_a', 'rg_b_a', 'rg_w_x', 'rg_b_x',
            'rg_lam', 'rg_w_out', 'at_w_qkv', 'at_q_g', 'at_k_g', 'at_w_o', 'mlp_w_up', 'mlp_w_down', 'final_g']
_BIG = dict(rg_w_in=["rg_w_in"], rg_w_out=["rg_w_out"], at_w_qkv=["at_w_qkv"], at_w_o=["at_w_o"],
            mlp_w_up=["up0", "up1"], mlp_w_down=["down0", "down1"])


def kernel(x, *args):
    n_w = len(_WEIGHTS)
    w = dict(zip(_WEIGHTS, args[:n_w]))
    target = args[n_w]
    m = dict(zip(_WEIGHTS, args[n_w + 1:2 * n_w + 1]))
    v = dict(zip(_WEIGHTS, args[2 * n_w + 1:3 * n_w + 1]))
    B, L, _ = x.shape
    T = B * L
    me = 2 * lax.axis_index("x") + lax.axis_index("y")

    vec = jnp.concatenate([_gate_vec_slot(w["rg_b_a"][0], w["rg_b_x"][0], w["rg_lam"][0]),
                           _rows_at(w["rg_conv_w"][0, :, 0, :], 0)], axis=0)
    flat = lambda a: a.reshape(-1, a.shape[-1])
    rows_of = lambda k: w[k].shape[-2]
    groups = [("rg", [("rg_w_in", 0, BF16), (vec, 0, F32)]), ("rg_out", [("rg_w_out", 0, BF16)]),
              ("mlp0_up", [("mlp_w_up", 0, BF16)]), ("mlp0_down", [("mlp_w_down", 0, BF16)]),
              ("att", [("at_w_qkv", 0, BF16), ("at_w_o", 0, BF16)]),
              ("mlp1", [("mlp_w_up", 1, BF16), ("mlp_w_down", 1, BF16)])]

    def landing_zones(group, members, after):
        lands = []
        for n, (k, layer, dtype) in enumerate(members):
            src, rows = (flat(w[k]), rows_of(k)) if isinstance(k, str) else (k, k.shape[0])
            lands.append(_cast_into_slot(src, layer * rows, rows, me, dtype, f"place_{group}{n}", after=after))
        return lands

    halves, gathers = {}, {}
    halves["rg"], tok = _exchange_start("gather_half", [], landing_zones(*groups[0], None), "gather_rg_start")
    handles, tok = _gather_start_groups([landing_zones(g, members, tok) for g, members in groups[1:]],
                                        "gather_rest_start", after=tok, kind="gather_half")
    halves.update(zip([g for g, _ in groups[1:]], handles))
    wcat = _make_wcat(w["rg_w_a"], w["rg_w_x"]).astype(BF16)

    packs = [_pack_vec(p, me) for p in (w, m, v)]

    ready = {}

    def share(some, after, name):
        landed = [_exchange_wait("gather_half", halves[g], after, f"gather_{g}_landed")[1] for g in some]
        handles, _ = _gather_start_groups(landed, name, kind="share_half")
        gathers.update(zip(some, handles))

    def fetch(what, after):
        if what in ready:
            return ready[what]
        group = "mlp1" if what.startswith("mlp1") else what
        if group == "rg":
            share(["rg"], [after, wcat] + packs, "share_rg_start")
        elif group == "rg_out":
            share(["rg_out", "mlp0_up", "mlp0_down", "att"], after, "share_early_start")
        _, full = _exchange_wait("share_half", gathers[group], after, f"gather_{group}_wait")
        if group == "att":
            share(["mlp1"], after, "share_mlp1_start")
        if group == "rg":
            vec_full = jnp.transpose(full[1], (1, 0, 2)).reshape(2 * SUBLANES, D_MODEL)
            conv_wb = vec_full[SUBLANES:] + _rows_at(w["rg_conv_b"], 4)
            return full[0], conv_wb, wcat, vec_full[:SUBLANES]
        if group == "rg_out":
            return full[0].reshape(D_MODEL, D_MODEL)
        if group == "att":
            return full[0], full[1].reshape(D_MODEL, D_MODEL)
        if group == "mlp1":
            ready["mlp1_up"], ready["mlp1_down"] = full[0], full[1].reshape(4 * D_MODEL, D_MODEL)
            return ready[what]
        return full[0] if group == "mlp0_up" else full[0].reshape(4 * D_MODEL, D_MODEL)

    names = dict(mlp1=["up1", "down1"], att=["at_w_qkv", "at_w_o"], mlp0=["up0", "down0", "rg_w_out"],
                 rg_in=["rg_w_in"], gates=["rg_w_a", "rg_w_x"])
    scatters, swaps, P, Q, res = {}, [], {}, {}, {}

    def start_scatter(group, grads):
        srcs = [g.reshape(N_CHIPS, -1, g.shape[-1]) for g in grads]
        lands = [lax.empty((N_CHIPS - 1,) + s.shape[1:], s.dtype) for s in srcs]
        scatters[group], token = _exchange_start("scatter", srcs, lands, f"scatter_{group}_start")
        return token

    def settle(groups, after):
        keys, parts = [], []
        for group in groups:
            srcs, lands = _exchange_wait("scatter", scatters[group], after, f"scatter_{group}_wait")
            for k, s, r in zip(names[group], srcs, lands):
                keys.append(k)
                parts.append(_sum_slots(s, r, me, f"sum_{k}"))
        handle, token = _exchange_start("swap", parts, [lax.empty(p.shape, F32) for p in parts],
                                        f"swap_{groups[0]}_start")
        swaps.append((keys, handle, f"swap_{groups[0]}_wait"))
        return token

    def finish(after):
        for keys, handle, name in swaps:
            mine, theirs = _exchange_wait("swap", handle, after, name)
            P.update(zip(keys, mine))
            Q.update(zip(keys, theirs))
        swaps.clear()
        last = after
        for k, parts in _BIG.items():
            if k in res or any(p not in P for p in parts):
                continue
            shape = w[k].shape
            two_d = lambda a: a.reshape(-1, shape[-1])
            outs = _adamw(two_d(w[k]), two_d(m[k]), two_d(v[k]), [P[p] for p in parts], [Q[p] for p in parts],
                          f"adamw_{k}")
            res[k] = [o.reshape(shape) for o in outs]
            last = outs[0]
        if "rg_w_a" in P and "gates" not in gathers:
            lands = [_cast_into_slot(P[k], 0, P[k].shape[0], me, F32, f"place_{k}", after=last, add=Q[k])
                     for k in names["gates"]]
            gathers["gates"], last = _exchange_start("gather", [], lands, "gather_gates_start", after=last)
        return last

    def emit(event, arrays):
        if event == "point_attn_done":
            return None
        if event == "point_mix_done":
            return settle(["mlp1", "att", "mlp0"], arrays[0])
        token = start_scatter(event, arrays)
        if event == "rg_in":
            return finish(settle(["gates"], token))
        return token

    P_vec = dict(norm_mix_g=w["norm_mix_g"], norm_mlp_g=w["norm_mlp_g"], final_g=w["final_g"][None],
                 q_g=w["at_q_g"], k_g=w["at_k_g"])
    grad_x, vec_part = _local_step(x.reshape(T, D_MODEL), target.reshape(T, D_MODEL), P_vec, fetch, emit, B, L,
                                   after=tok)

    me8 = 2 * me + lax.axis_index("c")
    vec_slots = _cast_into_slot(vec_part, 0, VEC_ROWS, me8, F32, "place_vec", n_slots=N_DEVICES)
    spread, tok = _exchange_start("spread", [], [vec_slots], "spread_vec_start")
    last = finish(settle(["rg_in"], tok))
    _, gate_grads = _exchange_wait("gather", gathers["gates"], last, "gather_gates_wait")
    for k, g in zip(names["gates"], gate_grads):
        two_d = lambda a: a.reshape(g.shape[0] * g.shape[1], g.shape[2])
        outs = _adamw(two_d(w[k]), two_d(m[k]), two_d(v[k]), [two_d(g)], None, f"adamw_{k}")
        res[k] = [o.reshape(w[k].shape) for o in outs]
        last = outs[0]
    _, (vec_all,) = _exchange_wait("spread", spread, last, "spread_vec_wait")
    vec_grad = _sum_leading(vec_all, "sum_vec")
    loss = vec_grad[LOSS_ROW, 0]
    outs = _adamw(*packs, [vec_grad], None, "adamw_vec")
    unpacked = [_unpack_vec(o, me) for o in outs]
    for k in _WEIGHTS:
        if k not in res:
            res[k] = [u[k] for u in unpacked]

    result = [loss, grad_x.reshape(B, L, D_MODEL)]
    for slot in range(4):
        result += [res[k][slot] for k in _WEIGHTS]
    return tuple(result)
```

```python
import functools
import math

import jax
import jax.numpy as jnp
import numpy as np
from jax import lax
from jax.experimental import pallas as pl
from jax.experimental.pallas import tpu as pltpu

F32 = jnp.float32
BF16 = jnp.bfloat16

D_MODEL = 1024
HEAD_DIM = 128
N_HEADS = 8
N_KV = 2
GROUP = N_HEADS // N_KV
LRU_BLOCKS = 8
LRU_BW = 128
GRID_W = 64
ROPE_THETA = 10000.0
EPS = 1e-6
RG_C = 8.0
SCALE = 1.0 / math.sqrt(HEAD_DIM)
N_CHIPS = 4

ADAM_LR = 0.001
ADAM_B1 = 0.9
ADAM_B2 = 0.999
ADAM_EPS = 1e-08
ADAM_WD = 0.01
ADAM_STEP = 10

V7X_VMEM_BYTES = 64 * 1024 * 1024
VMEM_LIMIT = V7X_VMEM_BYTES * 3 // 4
LANES = 128
SUBLANES = 8

N_DEVICES = 8
VEC_ROWS = 32
LOSS_ROW = 5


def _params(sem):
    return pltpu.CompilerParams(dimension_semantics=sem, vmem_limit_bytes=VMEM_LIMIT)


_ANY = pl.BlockSpec(memory_space=pl.ANY)
_NN = (((1,), (0,)), ((), ()))
_NT = (((1,), (1,)), ((), ()))
_TN = (((0,), (0,)), ((), ()))


def _after_operand(after):
    return [] if after is None else [after]


def _fit(t, n):
    if n <= t:
        return n
    c = (t // LANES) * LANES
    while n % c:
        c -= LANES
    return c


MM_VMEM_BUDGET = VMEM_LIMIT * 3 // 4
def _mm_tiles(M, K, ns, n_total, out_dtypes, extras, whole_rows):
    for tm in (2048, 1024, 512, 256, 128):
        for tn in ((ns,) if whole_rows else (1024, 512, 256)):
            tn = _fit(tn, ns)
            per_row = 2 * (2 * K) + 4 * tn + sum(2 * tn * jnp.dtype(d).itemsize for d in out_dtypes)
            per_row += sum(2 * tn * e.dtype.itemsize for e in extras)
            b_buffers = 1 if tn == n_total else 2
            if M % tm == 0 and b_buffers * (2 * K * tn) + tm * per_row <= MM_VMEM_BUDGET:
                return tm, tn
    raise ValueError(f"no tile fits VMEM for M={M} K={K} N={ns}")


def _mm(a, b, *, mode, name, out_dtypes=(F32,), b_shard=False, o_shard=False, extras=(), epi=None, after=None,
        bcast=(), accs=(), ref_epi=None, out_cols=None):
    if mode == "tn":
        K, M = a.shape
        N = b.shape[1]
    else:
        M, K = a.shape
        if mode == "nn":
            N = b.shape[0] * b.shape[2] if b_shard else b.shape[1]
        else:
            N = b.shape[1] if b_shard else b.shape[0]
    ns = N
    if b_shard and mode == "nn":
        ns = b.shape[2]
    elif o_shard:
        ns = N // N_CHIPS
    tm, tn = _mm_tiles(M, K, ns, N, out_dtypes, extras, whole_rows=ref_epi is not None)
    if ref_epi is not None:
        tm = min(tm, 512)
    grid = (M // tm, N // tn)
    q = ns // tn
    once = dict(pipeline_mode=pl.Buffered(1)) if tn == N else {}

    if mode == "tn":
        a_spec = pl.BlockSpec((K, tm), lambda i, j: (0, i))
        b_spec = pl.BlockSpec((K, tn), lambda i, j: (0, j), **once)
        dims = _TN
    elif mode == "nn":
        a_spec = pl.BlockSpec((tm, K), lambda i, j: (i, 0))
        if b_shard:
            b_spec = pl.BlockSpec((None, K, tn), lambda i, j: (j // q, 0, j % q), **once)
        else:
            b_spec = pl.BlockSpec((K, tn), lambda i, j: (0, j), **once)
        dims = _NN
    else:
        a_spec = pl.BlockSpec((tm, K), lambda i, j: (i, 0))
        if b_shard:
            ks = b.shape[2]
            b_spec = pl.BlockSpec((N_CHIPS, tn, ks), lambda i, j: (0, j, 0), **once)
        else:
            b_spec = pl.BlockSpec((tn, K), lambda i, j: (j, 0), **once)
        dims = _NT

    if o_shard:
        o_specs = [pl.BlockSpec((None, tm, tn), lambda i, j: (j // q, i, j % q))]
        o_shapes = [jax.ShapeDtypeStruct((N_CHIPS, M, ns), out_dtypes[0])]
    else:
        o_specs = [pl.BlockSpec((tm, tn), lambda i, j: (i, j)) for _ in out_dtypes]
        o_shapes = [jax.ShapeDtypeStruct((M, N if out_cols is None else out_cols[n]), dt)
                    for n, dt in enumerate(out_dtypes)]
    e_specs = [pl.BlockSpec((tm, tn), lambda i, j: (i, j)) for _ in extras]
    e_specs += [pl.BlockSpec(v.shape, lambda i, j: (0, 0)) for v in bcast]
    o_specs += [pl.BlockSpec(s, lambda i, j: (0, 0)) for s in accs]
    o_shapes += [jax.ShapeDtypeStruct(s, F32) for s in accs]
    n_e, n_b, n_o, n_a = len(extras), len(bcast), len(out_dtypes), len(accs)
    order = _after_operand(after)
    n_x = len(order)
    if epi is None:
        epi = lambda acc: (acc,)

    def body(a_ref, b_ref, *rest):
        e_refs, b_refs = rest[:n_e], rest[n_e:n_e + n_b]
        o_refs = rest[n_e + n_b + n_x:n_e + n_b + n_x + n_o]
        a_refs = rest[n_e + n_b + n_x + n_o:]
        if n_a:
            @pl.when((pl.program_id(0) == 0) & (pl.program_id(1) == 0))
            def _():
                for r in a_refs:
                    r[...] = jnp.zeros(r.shape, F32)
        if mode == "nt" and b_shard:
            acc = None
            for s in range(N_CHIPS):
                part = lax.dot_general(a_ref[:, s * ks:(s + 1) * ks], b_ref[s], dims, preferred_element_type=F32)
                acc = part if acc is None else acc + part
        else:
            acc = lax.dot_general(a_ref[...], b_ref[...], dims, preferred_element_type=F32)
        if ref_epi is not None:
            ref_epi(acc, e_refs, b_refs, o_refs, a_refs)
            return
        outs = epi(acc, *[r[...] for r in e_refs])
        for r, o in zip(o_refs, outs):
            r[...] = o.astype(r.dtype)

    outs = pl.pallas_call(
        body, name=name, grid=grid, in_specs=[a_spec, b_spec] + e_specs + [_ANY] * n_x, out_specs=o_specs,
        out_shape=o_shapes, compiler_params=_params(("arbitrary", "arbitrary") if n_a else ("parallel", "parallel")),
    )(a, b, *extras, *bcast, *order)
    return outs[0] if n_o + n_a == 1 else outs


def _rowwise(fn, rows, bcast, outs, accs=(), *, tm, name, after=None):
    def norm(r):
        return r if isinstance(r, tuple) else (r, r.shape[1], 0)

    rows = [norm(r) for r in rows]
    T = rows[0][0].shape[0]
    tm = min(tm, T)
    while T % tm:
        tm -= SUBLANES
    n_r, n_b, n_o, n_a = len(rows), len(bcast), len(outs), len(accs)
    order = _after_operand(after)
    n_x = len(order)
    in_specs = [pl.BlockSpec((tm, c), functools.partial(lambda i, cb: (i, cb), cb=cb)) for _, c, cb in rows]
    in_specs += [pl.BlockSpec(b.shape, lambda i: (0, 0)) for b in bcast] + [_ANY] * n_x
    out_specs = [pl.BlockSpec((tm, o[0]), lambda i: (i, 0)) for o in outs]
    out_specs += [pl.BlockSpec(s, lambda i: (0, 0)) for s in accs]
    out_shape = [jax.ShapeDtypeStruct((T, o[2] if len(o) > 2 else o[0]), o[1]) for o in outs]
    out_shape += [jax.ShapeDtypeStruct(s, F32) for s in accs]

    def body(*refs):
        in_refs = refs[:n_r]
        b_refs = refs[n_r:n_r + n_b]
        o_refs = refs[n_r + n_b + n_x:n_r + n_b + n_x + n_o]
        a_refs = refs[n_r + n_b + n_x + n_o:]
        if n_a:
            @pl.when(pl.program_id(0) == 0)
            def _():
                for r in a_refs:
                    r[...] = jnp.zeros(r.shape, F32)
        fn(in_refs, b_refs, o_refs, a_refs)

    res = pl.pallas_call(
        body, name=name, grid=(T // tm,), in_specs=in_specs, out_specs=out_specs, out_shape=out_shape,
        compiler_params=_params(("arbitrary",) if n_a else ("parallel",)),
    )(*[r[0] for r in rows], *bcast, *order)
    return res


def _rsum(x):
    return jnp.sum(x, axis=0, keepdims=True)


def _rms_fwd(x, g, name, after=None):
    def fn(ins, bs, outs, accs):
        xv = ins[0][...]
        r = lax.rsqrt(jnp.mean(xv * xv, axis=-1, keepdims=True) + EPS)
        outs[0][...] = (xv * r * bs[0][...]).astype(BF16)

    return _rowwise(fn, [x], [g], [(D_MODEL, BF16)], tm=512, name=name, after=after)[0]


def _rms_bwd_math(xv, dh, g):
    r = lax.rsqrt(jnp.mean(xv * xv, axis=-1, keepdims=True) + EPS)
    hn = xv * r
    dgh = dh * g
    dx = r * (dgh - hn * jnp.mean(dgh * hn, axis=-1, keepdims=True))
    return dx, _rsum(dh * hn)


def _mm_norm_bwd(dy, w, x, dres, g, name, after=None):
    def epilogue(acc, e_refs, b_refs, o_refs, a_refs):
        dx, dg = _rms_bwd_math(e_refs[0][...], acc, b_refs[0][...])
        dx = dx + e_refs[1][...]
        o_refs[0][...] = dx
        o_refs[1][...] = dx.astype(BF16)
        a_refs[0][...] += dg

    return _mm(dy, w, mode="nt", b_shard=True, out_dtypes=(F32, BF16), extras=(x, dres), bcast=(g,),
               accs=((1, D_MODEL),), ref_epi=epilogue, name=name, after=after)


def _mm_res_norm(a, w, res, g, name):
    def epilogue(acc, e_refs, b_refs, o_refs, a_refs):
        xv = acc + e_refs[0][...]
        o_refs[0][...] = xv
        r = lax.rsqrt(jnp.mean(xv * xv, axis=-1, keepdims=True) + EPS)
        o_refs[1][...] = (xv * r * b_refs[0][...]).astype(BF16)

    return _mm(a, w, mode="nn", out_dtypes=(F32, BF16), extras=(res,), bcast=(g,), ref_epi=epilogue, name=name)


def _mm_final_loss(a, w, res, target, g, name):
    def epilogue(acc, e_refs, b_refs, o_refs, a_refs):
        xv = acc + e_refs[0][...]
        gv = b_refs[0][...]
        r = lax.rsqrt(jnp.mean(xv * xv, axis=-1, keepdims=True) + EPS)
        e = xv * r * gv - e_refs[1][...]
        tok = jnp.mean(e * e, axis=-1, keepdims=True)
        a_refs[0][...] += 0.5 * jnp.sum(tok, axis=0, keepdims=True) * jnp.ones((1, LANES), F32)
        dx, dg = _rms_bwd_math(xv, e * (1.0 / D_MODEL), gv)
        o_refs[0][...] = dx
        o_refs[1][...] = dx.astype(BF16)
        a_refs[1][...] += dg

    return _mm(a, w, mode="nn", out_dtypes=(F32, BF16), extras=(res, target), bcast=(g,),
               accs=((1, LANES), (1, D_MODEL)), ref_epi=epilogue, name=name)


def _relu2(acc):
    r = jnp.maximum(acc, 0.0)
    return r * r, r


def _mlp_fwd(x, h, fetch, tag, finish):
    w_up = fetch(f"mlp{tag}_up", h)
    a, r = _mm(h, w_up, mode="nn", b_shard=True, out_dtypes=(BF16, BF16), epi=_relu2, name=f"mlp{tag}_up")
    w_down = fetch(f"mlp{tag}_down", a)
    return finish(a, w_down, x, f"mlp{tag}_down"), (h, a, r, w_up, w_down)


def _mlp_bwd(x, g, saved, dx, dx_bf, tag, after):
    h, a, r, w_up, w_down = saved
    d_down = _mm(a, dx_bf, mode="tn", out_dtypes=(BF16,), name=f"mlp{tag}_dwdown", after=after)
    dup = _mm(dx_bf, w_down, mode="nt", extras=(r,), out_dtypes=(BF16,),
              epi=lambda acc, rv: (acc * (2.0 * rv.astype(F32)),), name=f"mlp{tag}_dup")
    d_up = _mm(h, dup, mode="tn", o_shard=True, out_dtypes=(BF16,), name=f"mlp{tag}_dwup")
    dx_new, dx_new_bf, dg = _mm_norm_bwd(dup, w_up, x, dx, g, f"mlp{tag}_dh")
    return dx_new, dx_new_bf, dg, d_up, d_down


def _rope_tables(L, B):
    rows = L // GRID_W
    row = np.repeat(np.arange(rows, dtype=np.float32), GRID_W)
    col = np.tile(np.arange(GRID_W, dtype=np.float32), rows)
    inv = (ROPE_THETA ** (-np.arange(HEAD_DIM // 4, dtype=np.float32) / (HEAD_DIM // 4))).astype(np.float32)
    ar, ac = row[:, None] * inv, col[:, None] * inv
    cos = np.concatenate([np.cos(ar), np.cos(ar), np.cos(ac), np.cos(ac)], axis=-1)
    sin = np.concatenate([-np.sin(ar), np.sin(ar), -np.sin(ac), np.sin(ac)], axis=-1)
    return jnp.asarray(np.tile(cos, (B, 1)), F32), jnp.asarray(np.tile(sin, (B, 1)), F32)


def _swap_halves(x):
    lane = lax.broadcasted_iota(jnp.int32, x.shape, 1)
    return jnp.where((lane % 64) < 32, pltpu.roll(x, HEAD_DIM - 32, 1), pltpu.roll(x, 32, 1))


def _qk_prep(qkv, cos, sin, q_g, k_g):
    def fn(ins, bs, outs, accs):
        c, s = ins[1][...], ins[2][...]
        for h in range(N_HEADS + N_KV):
            xv = ins[0][:, h * HEAD_DIM:(h + 1) * HEAD_DIM]
            g = bs[0][...] if h < N_HEADS else bs[1][...]
            r = lax.rsqrt(jnp.mean(xv * xv, axis=-1, keepdims=True) + EPS)
            z = xv * r * g
            y = (z * c + _swap_halves(z) * s).astype(BF16)
            if h < N_HEADS:
                outs[0][:, h * HEAD_DIM:(h + 1) * HEAD_DIM] = y
            else:
                outs[1][:, (h - N_HEADS) * HEAD_DIM:(h - N_HEADS + 1) * HEAD_DIM] = y
        outs[2][...] = ins[0][:, (N_HEADS + N_KV) * HEAD_DIM:].astype(BF16)

    kvw = N_KV * HEAD_DIM
    return _rowwise(fn, [qkv, cos, sin], [q_g, k_g], [(D_MODEL, BF16), (kvw, BF16), (kvw, BF16)], tm=512,
                    name="attn_qk_prep")


def _qk_prep_bwd(qkv, dq, dk, dv, cos, sin, q_g, k_g):
    def fn(ins, bs, outs, accs):
        c, s = ins[4][...], ins[5][...]
        for h in range(N_HEADS + N_KV):
            sl = slice(h * HEAD_DIM, (h + 1) * HEAD_DIM)
            xv = ins[0][:, sl]
            if h < N_HEADS:
                g, dy, acc = bs[0][...], ins[1][:, sl], accs[0]
            else:
                ks = slice((h - N_HEADS) * HEAD_DIM, (h - N_HEADS + 1) * HEAD_DIM)
                g, dy, acc = bs[1][...], ins[2][:, ks], accs[1]
            r = lax.rsqrt(jnp.mean(xv * xv, axis=-1, keepdims=True) + EPS)
            xn = xv * r
            dz = dy * c - _swap_halves(dy) * s
            acc[...] += _rsum(dz * xn)
            dxn = dz * g
            outs[0][:, sl] = (r * (dxn - xn * jnp.mean(dxn * xn, axis=-1, keepdims=True))).astype(BF16)
        outs[0][:, (N_HEADS + N_KV) * HEAD_DIM:] = ins[3][...].astype(BF16)

    return _rowwise(fn, [qkv, dq, dk, dv, cos, sin], [q_g, k_g], [(qkv.shape[1], BF16)],
                    [(1, HEAD_DIM), (1, HEAD_DIM)], tm=256, name="attn_qk_prep_bwd")


_EXP2_SCALE = SCALE * math.log2(math.e)


def _exp_rows(q, k):
    s = lax.dot_general(q, k, _NT, preferred_element_type=F32)
    p = jnp.exp2((s - jnp.max(s, axis=-1, keepdims=True)) * _EXP2_SCALE)
    return p, jnp.sum(p, axis=-1, keepdims=True)


def _attn_fwd(q, k, v, B, L, tq=2048, sub=256):
    tq = min(tq, L)
    sub = min(sub, tq)
    nq = L // tq

    def body(q_ref, k_ref, v_ref, o_ref):
        kv, vv = k_ref[...], v_ref[...]
        for c in range(tq // sub):
            rows = slice(c * sub, (c + 1) * sub)
            p, l = _exp_rows(q_ref[rows, :], kv)
            o = jnp.dot(p.astype(BF16), vv, preferred_element_type=F32)
            o_ref[rows, :] = (o * (1.0 / l)).astype(o_ref.dtype)

    return pl.pallas_call(
        body, name="attn_fwd", grid=(B, N_HEADS, nq),
        in_specs=[pl.BlockSpec((tq, HEAD_DIM), lambda b, h, i: (b * nq + i, h)),
                  pl.BlockSpec((L, HEAD_DIM), lambda b, h, i: (b, h // GROUP)),
                  pl.BlockSpec((L, HEAD_DIM), lambda b, h, i: (b, h // GROUP))],
        out_specs=pl.BlockSpec((tq, HEAD_DIM), lambda b, h, i: (b * nq + i, h)),
        out_shape=jax.ShapeDtypeStruct((B * L, D_MODEL), BF16),
        compiler_params=_params(("parallel", "parallel", "parallel")),
    )(q, k, v)


def _attn_bwd(q, k, v, o, do, B, L, tq=2048, sub=512):
    tq = min(tq, L)
    sub = min(sub, tq)
    nq = L // tq

    def body(q_ref, k_ref, v_ref, o_ref, do_ref, dq_ref, dk_ref, dv_ref):
        @pl.when((pl.program_id(2) == 0) & (pl.program_id(3) == 0))
        def _():
            dk_ref[...] = jnp.zeros(dk_ref.shape, F32)
            dv_ref[...] = jnp.zeros(dv_ref.shape, F32)

        kv, vv = k_ref[...], v_ref[...]
        ps, es, dos, qs = [], [], [], []
        for c in range(tq // sub):
            rows = slice(c * sub, (c + 1) * sub)
            qc, doc = q_ref[rows, :], do_ref[rows, :]
            p, l = _exp_rows(qc, kv)
            inv = 1.0 / l
            dp = lax.dot_general(doc, vv, _NT, preferred_element_type=F32)
            delta = jnp.sum(doc.astype(F32) * o_ref[rows, :].astype(F32), axis=-1, keepdims=True)
            e = (p * (dp - delta)).astype(BF16)
            dq_ref[rows, :] = jnp.dot(e, kv, preferred_element_type=F32) * (inv * SCALE)
            ps.append(p.astype(BF16))
            es.append(e)
            dos.append((doc.astype(F32) * inv).astype(BF16))
            qs.append((qc.astype(F32) * (inv * SCALE)).astype(BF16))
        cat = lambda xs: xs[0] if len(xs) == 1 else jnp.concatenate(xs, axis=0)
        dv_ref[...] += lax.dot_general(cat(ps), cat(dos), _TN, preferred_element_type=F32)
        dk_ref[...] += lax.dot_general(cat(es), cat(qs), _TN, preferred_element_type=F32)

    qmap = lambda b, kh, g, i: (b * nq + i, kh * GROUP + g)
    kmap = lambda b, kh, g, i: (b, kh)
    kvw = N_KV * HEAD_DIM
    return pl.pallas_call(
        body, name="attn_bwd", grid=(B, N_KV, GROUP, nq),
        in_specs=[pl.BlockSpec((tq, HEAD_DIM), qmap), pl.BlockSpec((L, HEAD_DIM), kmap),
                  pl.BlockSpec((L, HEAD_DIM), kmap), pl.BlockSpec((tq, HEAD_DIM), qmap),
                  pl.BlockSpec((tq, HEAD_DIM), qmap)],
        out_specs=[pl.BlockSpec((tq, HEAD_DIM), qmap), pl.BlockSpec((L, HEAD_DIM), kmap),
                   pl.BlockSpec((L, HEAD_DIM), kmap)],
        out_shape=[jax.ShapeDtypeStruct((B * L, D_MODEL), F32), jax.ShapeDtypeStruct((B * L, kvw), F32),
                   jax.ShapeDtypeStruct((B * L, kvw), F32)],
        compiler_params=_params(("parallel", "parallel", "arbitrary", "arbitrary")),
    )(q, k, v, o, do)


def _conv_shift(x, t, L, k):
    if k == 2:
        return x
    if k < 2:
        return jnp.where(t >= 2 - k, pltpu.roll(x, 2 - k, 0), 0.0)
    return jnp.where(t < L - (k - 2), pltpu.roll(x, L - (k - 2), 0), 0.0)


def _conv_apply(x, w_ref, L):
    t = lax.broadcasted_iota(jnp.int32, x.shape, 0)
    acc = w_ref[4:5, :] + w_ref[2:3, :] * x
    for k in (0, 1, 3):
        acc = acc + w_ref[k:k + 1, :] * _conv_shift(x, t, L, k)
    return acc


def _conv_bwd(z, g, wb, dz, B, L, tc=256, after=None):
    noff = D_MODEL // tc
    order = _after_operand(after)

    def body(z_ref, g_ref, w_ref, dz_in, *rest):
        dx_ref, dw_ref = rest[len(order):]

        @pl.when(pl.program_id(1) == 0)
        def _():
            dw_ref[...] = jnp.zeros(dw_ref.shape, F32)

        x, gv = z_ref[...], g_ref[...]
        t = lax.broadcasted_iota(jnp.int32, x.shape, 0)
        dx = w_ref[2:3, :] * gv
        for k in (0, 1, 3):
            dx = dx + w_ref[k:k + 1, :] * _conv_shift(gv, t, L, 4 - k)
        dx_ref[...] = dx.astype(BF16)
        for k in range(4):
            dw_ref[k:k + 1, :] += _rsum(_conv_shift(x, t, L, k) * gv)
        dw_ref[4:5, :] += _rsum(gv)

    return pl.pallas_call(
        body, name="rg_conv_bwd", grid=(noff, B),
        in_specs=[pl.BlockSpec((L, tc), lambda j, b: (b, noff + j)), pl.BlockSpec((L, tc), lambda j, b: (b, j)),
                  pl.BlockSpec((SUBLANES, tc), lambda j, b: (0, j)), _ANY] + [_ANY] * len(order),
        out_specs=[pl.BlockSpec((L, tc), lambda j, b: (b, noff + j)),
                   pl.BlockSpec((SUBLANES, tc), lambda j, b: (0, j))],
        out_shape=[jax.ShapeDtypeStruct(dz.shape, dz.dtype), jax.ShapeDtypeStruct((SUBLANES, D_MODEL), F32)],
        input_output_aliases={3: 0},
        compiler_params=_params(("parallel", "arbitrary")),
    )(z, g, wb, dz, *order)


def _softplus(x):
    return jnp.maximum(x, 0.0) + jnp.log1p(jnp.exp(-jnp.abs(x)))


_ROW_BA, _ROW_BX, _ROW_LAM = 0, 2, 4


def _gate_math(xb, pre, vec_ref, d, sl):
    pa = pre[:, (2 * d) * LRU_BW:(2 * d + 1) * LRU_BW] + vec_ref[_ROW_BA + d:_ROW_BA + d + 1, sl]
    px = pre[:, (2 * d + 1) * LRU_BW:(2 * d + 2) * LRU_BW] + vec_ref[_ROW_BX + d:_ROW_BX + d + 1, sl]
    r = 0.5 * jnp.tanh(0.5 * pa) + 0.5
    i = 0.5 * jnp.tanh(0.5 * px) + 0.5
    slope = (-RG_C) * _softplus(-vec_ref[_ROW_LAM + d:_ROW_LAM + d + 1, sl])
    log_a = r * slope
    a = jnp.exp(log_a)
    om = -jnp.tanh(log_a) * (1.0 + a * a)
    rs = lax.rsqrt(om)
    mult = jnp.where(om > 0.0, om * rs, 0.0)
    return a, mult * (i * xb), (r, i, slope, om, mult, rs)


def _gate_bwd(rec, du_f, da_f, du_b, da_b, wcat, gvec):
    def fn(ins, bs, outs, accs):
        for blk in range(LRU_BLOCKS):
            sl = slice(blk * LRU_BW, (blk + 1) * LRU_BW)
            xb = ins[0][:, sl]
            xb16 = xb.astype(BF16)
            w = bs[0][sl, :]
            pre = jnp.dot(xb16, w, preferred_element_type=F32)
            dx = jnp.zeros_like(xb)
            dpre = []
            for d in range(2):
                a, _, (r, i, slope, om, mult, rs) = _gate_math(xb, pre, bs[1], d, sl)
                du, da = ins[1 + 2 * d][:, sl], ins[2 + 2 * d][:, sl]
                t = du * xb
                d_i = t * mult
                dx = dx + du * mult * i
                dlog = da * a - (t * i) * ((1.0 - om) * rs)
                d_r = dlog * slope
                d_sp = _rsum(dlog * r) * (-RG_C)
                lam = bs[1][_ROW_LAM + d:_ROW_LAM + d + 1, sl]
                accs[2][_ROW_LAM + d:_ROW_LAM + d + 1, sl] += d_sp * (-jax.nn.sigmoid(-lam))
                dpa = d_r * r * (1.0 - r)
                dpx = d_i * i * (1.0 - i)
                accs[2][_ROW_BA + d:_ROW_BA + d + 1, sl] += _rsum(dpa)
                accs[2][_ROW_BX + d:_ROW_BX + d + 1, sl] += _rsum(dpx)
                dpre += [dpa, dpx]
            dpre = jnp.concatenate(dpre, axis=1).astype(BF16)
            dw = lax.dot_general(xb16, dpre, _TN, preferred_element_type=F32)
            for d in range(2):
                rows = slice(d * D_MODEL + blk * LRU_BW, d * D_MODEL + (blk + 1) * LRU_BW)
                accs[0][rows, :] += dw[:, (2 * d) * LRU_BW:(2 * d + 1) * LRU_BW]
                accs[1][rows, :] += dw[:, (2 * d + 1) * LRU_BW:(2 * d + 2) * LRU_BW]
            outs[0][:, sl] = dx + lax.dot_general(dpre, w, _NT, preferred_element_type=F32)

    gate_shape = (2 * D_MODEL, LRU_BW)
    return _rowwise(fn, [rec, du_f, da_f, du_b, da_b], [wcat, gvec], [(D_MODEL, F32)],
                    [gate_shape, gate_shape, (SUBLANES, D_MODEL)], tm=256, name="rg_gate_bwd")


def _as_time_blocks(x):
    return x.reshape(x.shape[0] // SUBLANES, SUBLANES, x.shape[1])


def _scan_call(body, ins, n_out, B, L, tc, name):
    nb = L // SUBLANES
    spec = pl.BlockSpec((nb, SUBLANES, tc), lambda b, j: (b, 0, j))
    T = ins[0].shape[0]
    outs = pl.pallas_call(
        functools.partial(body, nb), name=name, grid=(B, D_MODEL // tc),
        in_specs=[spec] * len(ins), out_specs=[spec] * n_out,
        out_shape=[jax.ShapeDtypeStruct((T // SUBLANES, SUBLANES, D_MODEL), F32)] * n_out,
        compiler_params=_params(("parallel", "parallel")),
    )(*[_as_time_blocks(x) for x in ins])
    return [o.reshape(T, D_MODEL) for o in outs]


def _block_scan(A, U, reverse):
    row = lax.broadcasted_iota(jnp.int32, A.shape, 0)
    for s in (1, 2, 4):
        shift = SUBLANES - s if reverse else s
        valid = (row < SUBLANES - s) if reverse else (row >= s)
        a_sh = jnp.where(valid, pltpu.roll(A, shift, 0), 1.0)
        u_sh = jnp.where(valid, pltpu.roll(U, shift, 0), 0.0)
        U = A * u_sh + U
        A = A * a_sh
    return A, U


_LAST = SUBLANES - 1
SCAN_UNROLL = 8


def _loop_blocks(nb, step, init):
    def group(g, carry):
        for k in range(SCAN_UNROLL):
            carry = step(g * SCAN_UNROLL + k, carry)
        return carry

    return lax.fori_loop(0, nb // SCAN_UNROLL, group, init)


def _scan_bwd(dy, a_f, h_f, a_b, h_b, B, L, tc=256):
    def body(nb, dy_r, af, hf, ab, hb, duf, daf, dub, dab):
        def step(i, carry):
            c1, c2 = carry
            ir = nb - 1 - i
            row = lax.broadcasted_iota(jnp.int32, (SUBLANES, tc), 0)
            a_up = jnp.where(row == _LAST, af[jnp.minimum(ir + 1, nb - 1), :1, :], pltpu.roll(af[ir], _LAST, 0))
            p, lam = _block_scan(a_up, dy_r[ir], True)
            lam = lam + p * c1
            before = hf[jnp.maximum(ir - 1, 0), _LAST:, :] * (ir > 0).astype(F32)
            duf[ir] = lam
            daf[ir] = lam * jnp.where(row == 0, before, pltpu.roll(hf[ir], 1, 0))
            a_dn = jnp.where(row == 0, ab[jnp.maximum(i - 1, 0), _LAST:, :], pltpu.roll(ab[i], 1, 0))
            p2, lam2 = _block_scan(a_dn, dy_r[i], False)
            lam2 = lam2 + p2 * c2
            after = hb[jnp.minimum(i + 1, nb - 1), :1, :] * (i < nb - 1).astype(F32)
            dub[i] = lam2
            dab[i] = lam2 * jnp.where(row == _LAST, after, pltpu.roll(hb[i], _LAST, 0))
            return lam[:1, :], lam2[_LAST:, :]

        zero = jnp.zeros((1, tc), F32)
        _loop_blocks(nb, step, (zero, zero))

    return _scan_call(body, [dy, a_f, h_f, a_b, h_b], 4, B, L, tc, "rg_scan_bwd")


_GELU_C = math.sqrt(2.0 / math.pi)


def _gelu_parts(x):
    th = jnp.tanh(_GELU_C * (x + 0.044715 * x * x * x))
    return 0.5 * x * (1.0 + th), th


def _mm_gated_out_bwd(dx, w_out, h_f, h_b, z, name, after=None):
    def epilogue(acc, e_refs, b_refs, o_refs, a_refs):
        x = e_refs[2][...]
        gl, th = _gelu_parts(x)
        dgl = 0.5 * (1.0 + th) + 0.5 * x * (1.0 - th * th) * (_GELU_C * (1.0 + 3.0 * 0.044715 * x * x))
        o_refs[0][...] = acc * gl
        o_refs[1][...] = (acc * (e_refs[0][...] + e_refs[1][...]) * dgl).astype(BF16)

    return _mm(dx, w_out, mode="nt", out_dtypes=(F32, BF16), out_cols=(D_MODEL, 2 * D_MODEL), extras=(h_f, h_b, z),
               ref_epi=epilogue, name=name, after=after)


def _row_block(i):
    return pl.ds(pl.multiple_of(i * SUBLANES, SUBLANES), SUBLANES)


def _rg_mix_fwd(z, conv_wb, wcat, gvec, B, L):
    nb = L // SUBLANES
    n_g = D_MODEL // LRU_BW

    def body(zg_ref, zr_ref, cw_ref, w_ref, gv_ref, rec_ref, af_s, ab_s, hf_ref, hb_ref, yg_ref, uf_s, ub_s):
        rec = _conv_apply(zr_ref[...], cw_ref, L)
        rec_ref[...] = rec
        pre = jnp.dot(rec.astype(BF16), w_ref[...], preferred_element_type=F32)
        for d, (a_s, u_s) in enumerate(((af_s, uf_s), (ab_s, ub_s))):
            a, u, _ = _gate_math(rec, pre, gv_ref, d, slice(None))
            a_s[...] = a
            u_s[...] = u

        def step(i, carry):
            c1, c2 = carry
            rows, rows_b = _row_block(i), _row_block(nb - 1 - i)
            p, h = _block_scan(af_s[rows, :], uf_s[rows, :], False)
            h = h + p * c1
            hf_ref[rows, :] = h
            p2, h2 = _block_scan(ab_s[rows_b, :], ub_s[rows_b, :], True)
            h2 = h2 + p2 * c2
            hb_ref[rows_b, :] = h2
            return h[_LAST:, :], h2[:1, :]

        zero = jnp.zeros((1, LRU_BW), F32)
        _loop_blocks(nb, step, (zero, zero))
        gl, _ = _gelu_parts(zg_ref[...])
        yg_ref[...] = ((hf_ref[...] + hb_ref[...]) * gl).astype(BF16)

    seq = lambda off: pl.BlockSpec((L, LRU_BW), lambda b, g: (b, off + g))
    vec = pl.BlockSpec((SUBLANES, LRU_BW), lambda b, g: (0, g))
    T = B * L
    return pl.pallas_call(
        body, name="rg_mix", grid=(B, n_g),
        in_specs=[seq(0), seq(n_g), vec, pl.BlockSpec((LRU_BW, 4 * LRU_BW), lambda b, g: (g, 0)), vec],
        out_specs=[seq(0)] * 6,
        out_shape=[jax.ShapeDtypeStruct((T, D_MODEL), F32)] * 5 + [jax.ShapeDtypeStruct((T, D_MODEL), BF16)],
        scratch_shapes=[pltpu.VMEM((L, LRU_BW), F32)] * 2,
        compiler_params=_params(("parallel", "parallel")),
    )(z, z, conv_wb, wcat, gvec)


def _make_wcat(w_a, w_x):
    g = jnp.stack([w_a[0, 0], w_x[0, 0], w_a[0, 1], w_x[0, 1]])
    return jnp.transpose(g, (1, 2, 0, 3)).reshape(D_MODEL, 4 * LRU_BW)


def _rows_at(part, first):
    return jnp.pad(part, ((first, SUBLANES - first - part.shape[0]), (0, 0)))


def _qk_slot(q_g, k_g):
    wide = lambda v, at: jnp.pad(v, ((0, SUBLANES - 1), (at, D_MODEL - at - HEAD_DIM)))
    return wide(q_g, 0) + wide(k_g, HEAD_DIM)


def _local_step(x, target, P, fetch, emit, B, L, after=None):
    g_mix, g_mlp = P["norm_mix_g"], P["norm_mlp_g"]
    h0 = _rms_fwd(x, g_mix[0:1], "rg_norm", after=after)
    w_in, conv_wb, wcat, gvec = fetch("rg", h0)
    z = _mm(h0, w_in, mode="nn", b_shard=True, name="rg_in")
    rec, a_f, a_b, h_f, h_b, yg = _rg_mix_fwd(z, conv_wb, wcat, gvec, B, L)
    w_out = fetch("rg_out", yg)
    x1, h1 = _mm_res_norm(yg, w_out, x, g_mlp[0:1], "rg_out")
    (x2, h3), mlp0 = _mlp_fwd(x1, h1, fetch, 0, lambda a, w, res, name: _mm_res_norm(a, w, res, g_mix[1:2], name))
    w_qkv, w_o = fetch("att", h3)
    qkv = _mm(h3, w_qkv, mode="nn", b_shard=True, name="attn_qkv")
    cos, sin = _rope_tables(L, B)
    qh, kh, vh = _qk_prep(qkv, cos, sin, P["q_g"], P["k_g"])
    o = _attn_fwd(qh, kh, vh, B, L)
    x3, h4 = _mm_res_norm(o, w_o, x2, g_mlp[1:2], "attn_out")
    (dx4, dx4_bf, loss_acc, d_final_g), mlp1 = _mlp_fwd(
        x3, h4, fetch, 1, lambda a, w, res, name: _mm_final_loss(a, w, res, target, P["final_g"], name))

    dx3, dx3_bf, dg_mlp1, d_up1, d_down1 = _mlp_bwd(x3, g_mlp[1:2], mlp1, dx4, dx4_bf, 1, None)
    tok = emit("mlp1", [d_up1, d_down1])
    d_wo = _mm(o, dx3_bf, mode="tn", out_dtypes=(BF16,), name="attn_dwo", after=tok)
    do = _mm(dx3_bf, w_o, mode="nt", out_dtypes=(BF16,), name="attn_do")
    dq, dk, dv = _attn_bwd(qh, kh, vh, o, do, B, L)
    dqkv, dq_g, dk_g = _qk_prep_bwd(qkv, dq, dk, dv, cos, sin, P["q_g"], P["k_g"])
    d_wqkv = _mm(h3, dqkv, mode="tn", o_shard=True, out_dtypes=(BF16,), name="attn_dwqkv")
    tok = emit("att", [d_wqkv, d_wo])
    dx2, dx2_bf, dg_mix1 = _mm_norm_bwd(dqkv, w_qkv, x2, dx3, g_mix[1:2], "attn_dh", after=tok)
    tok = emit("point_attn_done", [dx2_bf])
    dx1, dx1_bf, dg_mlp0, d_up0, d_down0 = _mlp_bwd(x1, g_mlp[0:1], mlp0, dx2, dx2_bf, 0, tok)
    d_wout = _mm(yg, dx1_bf, mode="tn", out_dtypes=(BF16,), name="rg_dwout")
    tok = emit("mlp0", [d_up0, d_down0, d_wout])
    dy, dgate = _mm_gated_out_bwd(dx1_bf, w_out, h_f, h_b, z, "rg_dyg", after=tok)
    du_f, da_f, du_b, da_b = _scan_bwd(dy, a_f, h_f, a_b, h_b, B, L)
    drec_c, d_wa, d_wx, d_gvec = _gate_bwd(rec, du_f, da_f, du_b, da_b, wcat, gvec)
    tok = emit("gates", [d_wa, d_wx])
    dz, d_convwb = _conv_bwd(z, drec_c, conv_wb, dgate, B, L, after=tok)
    tok = emit("point_mix_done", [dz])
    d_win = _mm(h0, dz, mode="tn", o_shard=True, out_dtypes=(BF16,), name="rg_dwin", after=tok)
    tok = emit("rg_in", [d_win])
    grad_x, _, dg_mix0 = _mm_norm_bwd(dz, w_in, x, dx1, g_mix[0:1], "rg_dh", after=tok)

    norms = (_rows_at(dg_mix0, 0) + _rows_at(dg_mix1, 1) + _rows_at(dg_mlp0, 2) + _rows_at(dg_mlp1, 3)
             + _rows_at(d_final_g, 4)
             + jnp.pad(loss_acc, ((LOSS_ROW, SUBLANES - 1 - LOSS_ROW), (0, D_MODEL - LANES))))
    vec = jnp.concatenate([norms, d_convwb, d_gvec, _qk_slot(dq_g, dk_g)], axis=0)
    return grad_x, vec


_MESH = pl.DeviceIdType.MESH


def _place():
    x, y, c = lax.axis_index("x"), lax.axis_index("y"), lax.axis_index("c")
    peers = [((1 - x) if j & 2 else x, (1 - y) if j & 1 else y) for j in (1, 2, 3)]
    return x, y, c, peers


def _sum_leading(slots, name):
    def body(s_ref, o_ref):
        acc = s_ref[0]
        for d in range(1, slots.shape[0]):
            acc = acc + s_ref[d]
        o_ref[...] = acc

    return pl.pallas_call(body, name=name, out_shape=jax.ShapeDtypeStruct(slots.shape[1:], slots.dtype))(slots)


_HBM = pl.BlockSpec(memory_space=pltpu.HBM)
_SEM = pl.BlockSpec(memory_space=pltpu.SEMAPHORE)
_EFFECT = pltpu.SideEffectType.DATAFLOW_SIDE_EFFECTING


_COPIES = dict(gather=N_CHIPS - 1, scatter=N_CHIPS - 1, swap=1, spread=N_DEVICES - 1,
               gather_half=N_CHIPS - 1, share_half=N_CHIPS - 1)


def _split_copies(kind, srcs, lands, send, recv):
    x, y, c, peers = _place()
    me = 2 * x + y
    per = _COPIES[kind]
    out = []
    for a in range(len(lands)):
        for j in range(per):
            if kind == "swap":
                src, there, here, dev = srcs[a], lands[a], lands[a], (x, y, 1 - c)
            elif kind == "spread":
                k = j + 1
                dev = ((1 - x) if k & 4 else x, (1 - y) if k & 2 else y, (1 - c) if k & 1 else c)
                mine = lands[a].at[4 * x + 2 * y + c]
                src, there, here = mine, mine, lands[a].at[4 * dev[0] + 2 * dev[1] + dev[2]]
            else:
                px, py = peers[j]
                dev = (px, py, c)
                if kind == "gather":
                    src, there, here = lands[a].at[me], lands[a].at[me], lands[a].at[2 * px + py]
                elif kind in ("gather_half", "share_half"):
                    half = lands[a].shape[1] // 2
                    mine, other = pl.ds(c * half, half), pl.ds((1 - c) * half, half)
                    if kind == "gather_half":
                        src = there = lands[a].at[me, mine]
                        here = lands[a].at[2 * px + py, mine]
                    else:
                        src = there = lands[a].at[2 * px + py, mine]
                        here = lands[a].at[2 * px + py, other]
                        dev = (x, y, 1 - c)
                else:
                    src, there, here = srcs[a].at[2 * px + py], lands[a].at[j], lands[a].at[j]
            mk = functools.partial(
                pltpu.make_async_remote_copy, src_ref=src, send_sem=send.at[per * a + j],
                recv_sem=recv.at[per * a + j], device_id=dev, device_id_type=_MESH)
            out.append((functools.partial(mk, dst_ref=there), functools.partial(mk, dst_ref=here)))
    return out


def _exchange_start(kind, srcs, lands, name, after=None):
    arrays = list(srcs) + list(lands)
    n_s, n, n_all = len(srcs), len(lands), len(srcs) + len(lands)
    n_sem = _COPIES[kind] * n
    order = _after_operand(after)
    n_x = len(order)

    def body(*refs):
        send, recv = refs[n_all + n_x], refs[n_all + n_x + 1]
        token = refs[-1]
        for started, _ in _split_copies(kind, refs[:n_s], refs[n_s:n_all], send, recv):
            started().start()
        token[...] = jnp.zeros(token.shape, F32)

    res = pl.pallas_call(
        body, name=name,
        out_shape=(pltpu.SemaphoreType.DMA((n_sem,)), pltpu.SemaphoreType.DMA((n_sem,)),
                   *[pltpu.HBM(a.shape, a.dtype) for a in arrays], jax.ShapeDtypeStruct((SUBLANES, LANES), F32)),
        in_specs=[_HBM] * n_all + [_ANY] * n_x,
        out_specs=(_SEM, _SEM, *[_HBM] * n_all, pl.BlockSpec(memory_space=pltpu.VMEM)),
        input_output_aliases={i: 2 + i for i in range(n_all)},
        compiler_params=pltpu.CompilerParams(has_side_effects=_EFFECT),
    )(*[pltpu.with_memory_space_constraint(a, pltpu.HBM) for a in arrays], *order)
    return (res[0], res[1], res[2:2 + n_s], res[2 + n_s:2 + n_all]), res[-1]


def _gather_start_groups(land_groups, name, after=None, kind="gather"):
    arrays = [a for group in land_groups for a in group]
    n_all, n_g = len(arrays), len(land_groups)
    order = _after_operand(after)
    n_x = len(order)

    def body(*refs):
        first = 0
        for gi, group in enumerate(land_groups):
            send, recv = refs[n_all + n_x + 2 * gi], refs[n_all + n_x + 2 * gi + 1]
            for started, _ in _split_copies(kind, [], refs[first:first + len(group)], send, recv):
                started().start()
            first += len(group)
        refs[-1][...] = jnp.zeros(refs[-1].shape, F32)

    sems = [pltpu.SemaphoreType.DMA((_COPIES[kind] * len(group),)) for group in land_groups for _ in range(2)]
    res = pl.pallas_call(
        body, name=name,
        out_shape=(*sems, *[pltpu.HBM(a.shape, a.dtype) for a in arrays], jax.ShapeDtypeStruct((SUBLANES, LANES), F32)),
        in_specs=[_HBM] * n_all + [_ANY] * n_x,
        out_specs=(*[_SEM] * (2 * n_g), *[_HBM] * n_all, pl.BlockSpec(memory_space=pltpu.VMEM)),
        input_output_aliases={i: 2 * n_g + i for i in range(n_all)},
        compiler_params=pltpu.CompilerParams(has_side_effects=_EFFECT),
    )(*[pltpu.with_memory_space_constraint(a, pltpu.HBM) for a in arrays], *order)
    handles, first = [], 2 * n_g
    for gi, group in enumerate(land_groups):
        handles.append((res[2 * gi], res[2 * gi + 1], [], res[first:first + len(group)]))
        first += len(group)
    return handles, res[-1]


def _exchange_wait(kind, handle, after, name):
    send, recv, srcs, lands = handle
    arrays = list(srcs) + list(lands)
    n_s, n_all = len(srcs), len(arrays)
    order = list(after) if isinstance(after, (list, tuple)) else [after]

    def body(*refs):
        for started, landing in _split_copies(kind, refs[:n_s], refs[n_s:n_all], refs[n_all], refs[n_all + 1]):
            started().wait_send()
            landing().wait_recv()

    res = pl.pallas_call(
        body, name=name, out_shape=[pltpu.HBM(a.shape, a.dtype) for a in arrays],
        in_specs=[_HBM] * n_all + [_SEM, _SEM] + [_ANY] * len(order), out_specs=[_HBM] * n_all,
        input_output_aliases={i: i for i in range(n_all)},
        compiler_params=pltpu.CompilerParams(has_side_effects=_EFFECT),
    )(*arrays, send, recv, *order)
    return res[:n_s], res[n_s:]


def _index_operand(i):
    return jnp.reshape(i, (1,)).astype(jnp.int32)


def _cast_into_slot(src, row0, rows, me, dtype, name, after=None, add=None, n_slots=N_CHIPS):
    cols = src.shape[1]
    tm = min(512, rows)
    order = _after_operand(after)
    terms = [src] + ([] if add is None else [add])

    def body(me_ref, *rest):
        val = rest[0][...]
        if add is not None:
            val = val + rest[1][...]
        rest[-1][...] = val.astype(dtype)

    return pl.pallas_call(
        body, name=name,
        grid_spec=pltpu.PrefetchScalarGridSpec(
            num_scalar_prefetch=1, grid=(rows // tm,),
            in_specs=[pl.BlockSpec((tm, cols), lambda i, me_ref: (i + row0 // tm, 0))] * len(terms)
            + [_ANY] * len(order),
            out_specs=pl.BlockSpec((None, tm, cols), lambda i, me_ref: (me_ref[0], i, 0))),
        out_shape=jax.ShapeDtypeStruct((n_slots, rows, cols), dtype), compiler_params=_params(("parallel",)),
    )(_index_operand(me), *terms, *order)


def _sum_slots(mine, r, me, name):
    _, rows, cols = r.shape
    tm = min(512, rows)

    def body(me_ref, own_ref, r_ref, o_ref):
        o_ref[...] = ((own_ref[...].astype(F32) + r_ref[0].astype(F32)) + r_ref[1].astype(F32)) + r_ref[2].astype(F32)

    return pl.pallas_call(
        body, name=name,
        grid_spec=pltpu.PrefetchScalarGridSpec(
            num_scalar_prefetch=1, grid=(rows // tm,),
            in_specs=[pl.BlockSpec((None, tm, cols), lambda i, me_ref: (me_ref[0], i, 0)),
                      pl.BlockSpec((N_CHIPS - 1, tm, cols), lambda i, me_ref: (0, i, 0))],
            out_specs=pl.BlockSpec((tm, cols), lambda i, me_ref: (i, 0))),
        out_shape=jax.ShapeDtypeStruct((rows, cols), F32), compiler_params=_params(("parallel",)),
    )(_index_operand(me), mine, r)


def _adamw(w, m, v, ps, qs, name):
    rows, cols = w.shape
    seg_rows = ps[0].shape[0]
    tm = min(256, seg_rows)
    while seg_rows % tm:
        tm -= SUBLANES
    per, n_seg = seg_rows // tm, len(ps)
    parts = list(ps) + ([] if qs is None else list(qs))

    def body(w_ref, m_ref, v_ref, *rest):
        g_refs, outs = rest[:len(parts)], rest[len(parts):]
        grad = lambda s: g_refs[s][...] if qs is None else g_refs[s][...] + g_refs[n_seg + s][...]
        g = grad(0)
        for s in range(1, n_seg):
            g = jnp.where(pl.program_id(0) >= s * per, grad(s), g)
        m1 = ADAM_B1 * m_ref[...] + (1.0 - ADAM_B1) * g
        v1 = ADAM_B2 * v_ref[...] + (1.0 - ADAM_B2) * (g * g)
        m_hat = m1 / (1.0 - ADAM_B1 ** ADAM_STEP)
        v_hat = v1 / (1.0 - ADAM_B2 ** ADAM_STEP)
        outs[0][...] = g
        outs[1][...] = (-ADAM_LR) * (m_hat / (jnp.sqrt(v_hat) + ADAM_EPS) + ADAM_WD * w_ref[...])
        outs[2][...] = m1
        outs[3][...] = v1

    row_spec = pl.BlockSpec((tm, cols), lambda i: (i, 0))
    seg_spec = lambda s: pl.BlockSpec((tm, cols), lambda i: (jnp.clip(i - s * per, 0, per - 1), 0))
    return pl.pallas_call(
        body, name=name, grid=(rows // tm,),
        in_specs=[row_spec] * 3 + [seg_spec(s) for s in range(n_seg)] * (1 if qs is None else 2),
        out_specs=[row_spec] * 4, out_shape=[jax.ShapeDtypeStruct((rows, cols), F32)] * 4,
        compiler_params=_params(("arbitrary",)),
    )(w, m, v, *parts)


def _put_cols(shard, me):
    full = jnp.zeros((shard.shape[0], D_MODEL), F32)
    return lax.dynamic_update_slice(full, shard, (0, me * (D_MODEL // N_CHIPS)))


def _gate_vec_slot(b_a, b_x, lam):
    return _rows_at(b_a, _ROW_BA) + _rows_at(b_x, _ROW_BX) + _rows_at(lam, _ROW_LAM)


def _pack_vec(p, me):
    return jnp.concatenate([
        _rows_at(p["norm_mix_g"], 0) + _rows_at(p["norm_mlp_g"], 2) + _rows_at(p["final_g"][None], 4),
        _rows_at(_put_cols(p["rg_conv_w"][0, :, 0, :], me), 0) + _rows_at(p["rg_conv_b"], 4),
        _gate_vec_slot(_put_cols(p["rg_b_a"][0], me), _put_cols(p["rg_b_x"][0], me), _put_cols(p["rg_lam"][0], me)),
        _qk_slot(p["at_q_g"], p["at_k_g"]),
    ], axis=0)


def _unpack_vec(r, me):
    def cols(rows):
        return lax.dynamic_slice(rows, (0, me * (D_MODEL // N_CHIPS)), (rows.shape[0], D_MODEL // N_CHIPS))

    gate = r[16:24]
    return dict(
        norm_mix_g=r[0:2], norm_mlp_g=r[2:4], final_g=r[4], rg_conv_w=cols(r[8:12])[None, :, None, :],
        rg_conv_b=r[12:13], rg_b_a=cols(gate[_ROW_BA:_ROW_BA + 2])[None], rg_b_x=cols(gate[_ROW_BX:_ROW_BX + 2])[None],
        rg_lam=cols(gate[_ROW_LAM:_ROW_LAM + 2])[None], at_q_g=r[24:25, 0:HEAD_DIM],
        at_k_g=r[24:25, HEAD_DIM:2 * HEAD_DIM])


_WEIGHTS = ['norm_mix_g', 'norm_mlp_g', 'rg_w_in', 'rg_conv_w', 'rg_conv_b', 'rg_w_a', 'rg_b_a', 'rg_w_x', 'rg_b_x',
            'rg_lam', 'rg_w_out', 'at_w_qkv', 'at_q_g', 'at_k_g', 'at_w_o', 'mlp_w_up', 'mlp_w_down', 'final_g']
_BIG = dict(rg_w_in=["rg_w_in"], rg_w_out=["rg_w_out"], at_w_qkv=["at_w_qkv"], at_w_o=["at_w_o"],
            mlp_w_up=["up0", "up1"], mlp_w_down=["down0", "down1"])


def kernel(x, *args):
    n_w = len(_WEIGHTS)
    w = dict(zip(_WEIGHTS, args[:n_w]))
    target = args[n_w]
    m = dict(zip(_WEIGHTS, args[n_w + 1:2 * n_w + 1]))
    v = dict(zip(_WEIGHTS, args[2 * n_w + 1:3 * n_w + 1]))
    B, L, _ = x.shape
    T = B * L
    me = 2 * lax.axis_index("x") + lax.axis_index("y")

    vec = jnp.concatenate([_gate_vec_slot(w["rg_b_a"][0], w["rg_b_x"][0], w["rg_lam"][0]),
                           _rows_at(w["rg_conv_w"][0, :, 0, :], 0)], axis=0)
    flat = lambda a: a.reshape(-1, a.shape[-1])
    rows_of = lambda k: w[k].shape[-2]
    groups = [("rg", [("rg_w_in", 0, BF16), (vec, 0, F32)]), ("rg_out", [("rg_w_out", 0, BF16)]),
              ("mlp0_up", [("mlp_w_up", 0, BF16)]), ("mlp0_down", [("mlp_w_down", 0, BF16)]),
              ("att", [("at_w_qkv", 0, BF16), ("at_w_o", 0, BF16)]),
              ("mlp1", [("mlp_w_up", 1, BF16), ("mlp_w_down", 1, BF16)])]

    def landing_zones(group, members, after):
        lands = []
        for n, (k, layer, dtype) in enumerate(members):
            src, rows = (flat(w[k]), rows_of(k)) if isinstance(k, str) else (k, k.shape[0])
            lands.append(_cast_into_slot(src, layer * rows, rows, me, dtype, f"place_{group}{n}", after=after))
        return lands

    halves, gathers = {}, {}
    halves["rg"], tok = _exchange_start("gather_half", [], landing_zones(*groups[0], None), "gather_rg_start")
    handles, tok = _gather_start_groups([landing_zones(g, members, tok) for g, members in groups[1:]],
                                        "gather_rest_start", after=tok, kind="gather_half")
    halves.update(zip([g for g, _ in groups[1:]], handles))
    wcat = _make_wcat(w["rg_w_a"], w["rg_w_x"]).astype(BF16)

    packs = [_pack_vec(p, me) for p in (w, m, v)]

    ready = {}

    def share(some, after, name):
        landed = [_exchange_wait("gather_half", halves[g], after, f"gather_{g}_landed")[1] for g in some]
        handles, _ = _gather_start_groups(landed, name, kind="share_half")
        gathers.update(zip(some, handles))

    def fetch(what, after):
        if what in ready:
            return ready[what]
        group = "mlp1" if what.startswith("mlp1") else what
        if group == "rg":
            share(["rg"], [after, wcat] + packs, "share_rg_start")
        elif group == "rg_out":
            share(["rg_out", "mlp0_up", "mlp0_down", "att"], after, "share_early_start")
        _, full = _exchange_wait("share_half", gathers[group], after, f"gather_{group}_wait")
        if group == "att":
            share(["mlp1"], after, "share_mlp1_start")
        if group == "rg":
            vec_full = jnp.transpose(full[1], (1, 0, 2)).reshape(2 * SUBLANES, D_MODEL)
            conv_wb = vec_full[SUBLANES:] + _rows_at(w["rg_conv_b"], 4)
            return full[0], conv_wb, wcat, vec_full[:SUBLANES]
        if group == "rg_out":
            return full[0].reshape(D_MODEL, D_MODEL)
        if group == "att":
            return full[0], full[1].reshape(D_MODEL, D_MODEL)
        if group == "mlp1":
            ready["mlp1_up"], ready["mlp1_down"] = full[0], full[1].reshape(4 * D_MODEL, D_MODEL)
            return ready[what]
        return full[0] if group == "mlp0_up" else full[0].reshape(4 * D_MODEL, D_MODEL)

    names = dict(mlp1=["up1", "down1"], att=["at_w_qkv", "at_w_o"], mlp0=["up0", "down0", "rg_w_out"],
                 rg_in=["rg_w_in"], gates=["rg_w_a", "rg_w_x"])
    scatters, swaps, P, Q, res = {}, [], {}, {}, {}

    def start_scatter(group, grads):
        srcs = [g.reshape(N_CHIPS, -1, g.shape[-1]) for g in grads]
        lands = [lax.empty((N_CHIPS - 1,) + s.shape[1:], s.dtype) for s in srcs]
        scatters[group], token = _exchange_start("scatter", srcs, lands, f"scatter_{group}_start")
        return token

    def settle(groups, after):
        keys, parts = [], []
        for group in groups:
            srcs, lands = _exchange_wait("scatter", scatters[group], after, f"scatter_{group}_wait")
            for k, s, r in zip(names[group], srcs, lands):
                keys.append(k)
                parts.append(_sum_slots(s, r, me, f"sum_{k}"))
        handle, token = _exchange_start("swap", parts, [lax.empty(p.shape, F32) for p in parts],
                                        f"swap_{groups[0]}_start")
        swaps.append((keys, handle, f"swap_{groups[0]}_wait"))
        return token

    def finish(after):
        for keys, handle, name in swaps:
            mine, theirs = _exchange_wait("swap", handle, after, name)
            P.update(zip(keys, mine))
            Q.update(zip(keys, theirs))
        swaps.clear()
        last = after
        for k, parts in _BIG.items():
            if k in res or any(p not in P for p in parts):
                continue
            shape = w[k].shape
            two_d = lambda a: a.reshape(-1, shape[-1])
            outs = _adamw(two_d(w[k]), two_d(m[k]), two_d(v[k]), [P[p] for p in parts], [Q[p] for p in parts],
                          f"adamw_{k}")
            res[k] = [o.reshape(shape) for o in outs]
            last = outs[0]
        if "rg_w_a" in P and "gates" not in gathers:
            lands = [_cast_into_slot(P[k], 0, P[k].shape[0], me, F32, f"place_{k}", after=last, add=Q[k])
                     for k in names["gates"]]
            gathers["gates"], last = _exchange_start("gather", [], lands, "gather_gates_start", after=last)
        return last

    def emit(event, arrays):
        if event == "point_attn_done":
            return None
        if event == "point_mix_done":
            return settle(["mlp1", "att", "mlp0"], arrays[0])
        token = start_scatter(event, arrays)
        if event == "rg_in":
            return finish(settle(["gates"], token))
        return token

    P_vec = dict(norm_mix_g=w["norm_mix_g"], norm_mlp_g=w["norm_mlp_g"], final_g=w["final_g"][None],
                 q_g=w["at_q_g"], k_g=w["at_k_g"])
    grad_x, vec_part = _local_step(x.reshape(T, D_MODEL), target.reshape(T, D_MODEL), P_vec, fetch, emit, B, L,
                                   after=tok)

    me8 = 2 * me + lax.axis_index("c")
    vec_slots = _cast_into_slot(vec_part, 0, VEC_ROWS, me8, F32, "place_vec", n_slots=N_DEVICES)
    spread, tok = _exchange_start("spread", [], [vec_slots], "spread_vec_start")
    last = finish(settle(["rg_in"], tok))
    _, gate_grads = _exchange_wait("gather", gathers["gates"], last, "gather_gates_wait")
    for k, g in zip(names["gates"], gate_grads):
        two_d = lambda a: a.reshape(g.shape[0] * g.shape[1], g.shape[2])
        outs = _adamw(two_d(w[k]), two_d(m[k]), two_d(v[k]), [two_d(g)], None, f"adamw_{k}")
        res[k] = [o.reshape(w[k].shape) for o in outs]
        last = outs[0]
    _, (vec_all,) = _exchange_wait("spread", spread, last, "spread_vec_wait")
    vec_grad = _sum_leading(vec_all, "sum_vec")
    loss = vec_grad[LOSS_ROW, 0]
    outs = _adamw(*packs, [vec_grad], None, "adamw_vec")
    unpacked = [_unpack_vec(o, me) for o in outs]
    for k in _WEIGHTS:
        if k not in res:
            res[k] = [u[k] for u in unpacked]

    result = [loss, grad_x.reshape(B, L, D_MODEL)]
    for slot in range(4):
        result += [res[k][slot] for k in _WEIGHTS]
    return tuple(result)
```

```python
import functools
import math

import jax
import jax.numpy as jnp
import numpy as np
from jax import lax
from jax.experimental import pallas as pl
from jax.experimental.pallas import tpu as pltpu

F32 = jnp.float32
BF16 = jnp.bfloat16

D_MODEL = 1024
HEAD_DIM = 128
N_HEADS = 8
N_KV = 2
GROUP = N_HEADS // N_KV
LRU_BLOCKS = 8
LRU_BW = 128
GRID_W = 64
ROPE_THETA = 10000.0
EPS = 1e-6
RG_C = 8.0
SCALE = 1.0 / math.sqrt(HEAD_DIM)
N_CHIPS = 4

ADAM_LR = 0.001
ADAM_B1 = 0.9
ADAM_B2 = 0.999
ADAM_EPS = 1e-08
ADAM_WD = 0.01
ADAM_STEP = 10

V7X_VMEM_BYTES = 64 * 1024 * 1024
VMEM_LIMIT = V7X_VMEM_BYTES * 3 // 4
LANES = 128
SUBLANES = 8

N_DEVICES = 8
VEC_ROWS = 32
LOSS_ROW = 5


def _params(sem):
    return pltpu.CompilerParams(dimension_semantics=sem, vmem_limit_bytes=VMEM_LIMIT)


_ANY = pl.BlockSpec(memory_space=pl.ANY)
_NN = (((1,), (0,)), ((), ()))
_NT = (((1,), (1,)), ((), ()))
_TN = (((0,), (0,)), ((), ()))


def _after_operand(after):
    return [] if after is None else [after]


def _fit(t, n):
    if n <= t:
        return n
    c = (t // LANES) * LANES
    while n % c:
        c -= LANES
    return c


MM_VMEM_BUDGET = VMEM_LIMIT * 3 // 4
def _mm_tiles(M, K, ns, n_total, out_dtypes, extras, whole_rows):
    for tm in (2048, 1024, 512, 256, 128):
        for tn in ((ns,) if whole_rows else (1024, 512, 256)):
            tn = _fit(tn, ns)
            per_row = 2 * (2 * K) + 4 * tn + sum(2 * tn * jnp.dtype(d).itemsize for d in out_dtypes)
            per_row += sum(2 * tn * e.dtype.itemsize for e in extras)
            b_buffers = 1 if tn == n_total else 2
            if M % tm == 0 and b_buffers * (2 * K * tn) + tm * per_row <= MM_VMEM_BUDGET:
                return tm, tn
    raise ValueError(f"no tile fits VMEM for M={M} K={K} N={ns}")


def _mm(a, b, *, mode, name, out_dtypes=(F32,), b_shard=False, o_shard=False, extras=(), epi=None, after=None,
        bcast=(), accs=(), ref_epi=None, out_cols=None):
    if mode == "tn":
        K, M = a.shape
        N = b.shape[1]
    else:
        M, K = a.shape
        if mode == "nn":
            N = b.shape[0] * b.shape[2] if b_shard else b.shape[1]
        else:
            N = b.shape[1] if b_shard else b.shape[0]
    ns = N
    if b_shard and mode == "nn":
        ns = b.shape[2]
    elif o_shard:
        ns = N // N_CHIPS
    tm, tn = _mm_tiles(M, K, ns, N, out_dtypes, extras, whole_rows=ref_epi is not None)
    if ref_epi is not None:
        tm = min(tm, 512)
    grid = (M // tm, N // tn)
    q = ns // tn
    once = dict(pipeline_mode=pl.Buffered(1)) if tn == N else {}

    if mode == "tn":
        a_spec = pl.BlockSpec((K, tm), lambda i, j: (0, i))
        b_spec = pl.BlockSpec((K, tn), lambda i, j: (0, j), **once)
        dims = _TN
    elif mode == "nn":
        a_spec = pl.BlockSpec((tm, K), lambda i, j: (i, 0))
        if b_shard:
            b_spec = pl.BlockSpec((None, K, tn), lambda i, j: (j // q, 0, j % q), **once)
        else:
            b_spec = pl.BlockSpec((K, tn), lambda i, j: (0, j), **once)
        dims = _NN
    else:
        a_spec = pl.BlockSpec((tm, K), lambda i, j: (i, 0))
        if b_shard:
            ks = b.shape[2]
            b_spec = pl.BlockSpec((N_CHIPS, tn, ks), lambda i, j: (0, j, 0), **once)
        else:
            b_spec = pl.BlockSpec((tn, K), lambda i, j: (j, 0), **once)
        dims = _NT

    if o_shard:
        o_specs = [pl.BlockSpec((None, tm, tn), lambda i, j: (j // q, i, j % q))]
        o_shapes = [jax.ShapeDtypeStruct((N_CHIPS, M, ns), out_dtypes[0])]
    else:
        o_specs = [pl.BlockSpec((tm, tn), lambda i, j: (i, j)) for _ in out_dtypes]
        o_shapes = [jax.ShapeDtypeStruct((M, N if out_cols is None else out_cols[n]), dt)
                    for n, dt in enumerate(out_dtypes)]
    e_specs = [pl.BlockSpec((tm, tn), lambda i, j: (i, j)) for _ in extras]
    e_specs += [pl.BlockSpec(v.shape, lambda i, j: (0, 0)) for v in bcast]
    o_specs += [pl.BlockSpec(s, lambda i, j: (0, 0)) for s in accs]
    o_shapes += [jax.ShapeDtypeStruct(s, F32) for s in accs]
    n_e, n_b, n_o, n_a = len(extras), len(bcast), len(out_dtypes), len(accs)
    order = _after_operand(after)
    n_x = len(order)
    if epi is None:
        epi = lambda acc: (acc,)

    def body(a_ref, b_ref, *rest):
        e_refs, b_refs = rest[:n_e], rest[n_e:n_e + n_b]
        o_refs = rest[n_e + n_b + n_x:n_e + n_b + n_x + n_o]
        a_refs = rest[n_e + n_b + n_x + n_o:]
        if n_a:
            @pl.when((pl.program_id(0) == 0) & (pl.program_id(1) == 0))
            def _():
                for r in a_refs:
                    r[...] = jnp.zeros(r.shape, F32)
        if mode == "nt" and b_shard:
            acc = None
            for s in range(N_CHIPS):
                part = lax.dot_general(a_ref[:, s * ks:(s + 1) * ks], b_ref[s], dims, preferred_element_type=F32)
                acc = part if acc is None else acc + part
        else:
            acc = lax.dot_general(a_ref[...], b_ref[...], dims, preferred_element_type=F32)
        if ref_epi is not None:
            ref_epi(acc, e_refs, b_refs, o_refs, a_refs)
            return
        outs = epi(acc, *[r[...] for r in e_refs])
        for r, o in zip(o_refs, outs):
            r[...] = o.astype(r.dtype)

    outs = pl.pallas_call(
        body, name=name, grid=grid, in_specs=[a_spec, b_spec] + e_specs + [_ANY] * n_x, out_specs=o_specs,
        out_shape=o_shapes, compiler_params=_params(("arbitrary", "arbitrary") if n_a else ("parallel", "parallel")),
    )(a, b, *extras, *bcast, *order)
    return outs[0] if n_o + n_a == 1 else outs


def _rowwise(fn, rows, bcast, outs, accs=(), *, tm, name, after=None):
    def norm(r):
        return r if isinstance(r, tuple) else (r, r.shape[1], 0)

    rows = [norm(r) for r in rows]
    T = rows[0][0].shape[0]
    tm = min(tm, T)
    while T % tm:
        tm -= SUBLANES
    n_r, n_b, n_o, n_a = len(rows), len(bcast), len(outs), len(accs)
    order = _after_operand(after)
    n_x = len(order)
    in_specs = [pl.BlockSpec((tm, c), functools.partial(lambda i, cb: (i, cb), cb=cb)) for _, c, cb in rows]
    in_specs += [pl.BlockSpec(b.shape, lambda i: (0, 0)) for b in bcast] + [_ANY] * n_x
    out_specs = [pl.BlockSpec((tm, o[0]), lambda i: (i, 0)) for o in outs]
    out_specs += [pl.BlockSpec(s, lambda i: (0, 0)) for s in accs]
    out_shape = [jax.ShapeDtypeStruct((T, o[2] if len(o) > 2 else o[0]), o[1]) for o in outs]
    out_shape += [jax.ShapeDtypeStruct(s, F32) for s in accs]

    def body(*refs):
        in_refs = refs[:n_r]
        b_refs = refs[n_r:n_r + n_b]
        o_refs = refs[n_r + n_b + n_x:n_r + n_b + n_x + n_o]
        a_refs = refs[n_r + n_b + n_x + n_o:]
        if n_a:
            @pl.when(pl.program_id(0) == 0)
            def _():
                for r in a_refs:
                    r[...] = jnp.zeros(r.shape, F32)
        fn(in_refs, b_refs, o_refs, a_refs)

    res = pl.pallas_call(
        body, name=name, grid=(T // tm,), in_specs=in_specs, out_specs=out_specs, out_shape=out_shape,
        compiler_params=_params(("arbitrary",) if n_a else ("parallel",)),
    )(*[r[0] for r in rows], *bcast, *order)
    return res


def _rsum(x):
    return jnp.sum(x, axis=0, keepdims=True)


def _rms_fwd(x, g, name, after=None):
    def fn(ins, bs, outs, accs):
        xv = ins[0][...]
        r = lax.rsqrt(jnp.mean(xv * xv, axis=-1, keepdims=True) + EPS)
        outs[0][...] = (xv * r * bs[0][...]).astype(BF16)

    return _rowwise(fn, [x], [g], [(D_MODEL, BF16)], tm=512, name=name, after=after)[0]


def _rms_bwd_math(xv, dh, g):
    r = lax.rsqrt(jnp.mean(xv * xv, axis=-1, keepdims=True) + EPS)
    hn = xv * r
    dgh = dh * g
    dx = r * (dgh - hn * jnp.mean(dgh * hn, axis=-1, keepdims=True))
    return dx, _rsum(dh * hn)


def _mm_norm_bwd(dy, w, x, dres, g, name, after=None):
    def epilogue(acc, e_refs, b_refs, o_refs, a_refs):
        dx, dg = _rms_bwd_math(e_refs[0][...], acc, b_refs[0][...])
        dx = dx + e_refs[1][...]
        o_refs[0][...] = dx
        o_refs[1][...] = dx.astype(BF16)
        a_refs[0][...] += dg

    return _mm(dy, w, mode="nt", b_shard=True, out_dtypes=(F32, BF16), extras=(x, dres), bcast=(g,),
               accs=((1, D_MODEL),), ref_epi=epilogue, name=name, after=after)


def _mm_res_norm(a, w, res, g, name):
    def epilogue(acc, e_refs, b_refs, o_refs, a_refs):
        xv = acc + e_refs[0][...]
        o_refs[0][...] = xv
        r = lax.rsqrt(jnp.mean(xv * xv, axis=-1, keepdims=True) + EPS)
        o_refs[1][...] = (xv * r * b_refs[0][...]).astype(BF16)

    return _mm(a, w, mode="nn", out_dtypes=(F32, BF16), extras=(res,), bcast=(g,), ref_epi=epilogue, name=name)


def _mm_final_loss(a, w, res, target, g, name):
    def epilogue(acc, e_refs, b_refs, o_refs, a_refs):
        xv = acc + e_refs[0][...]
        gv = b_refs[0][...]
        r = lax.rsqrt(jnp.mean(xv * xv, axis=-1, keepdims=True) + EPS)
        e = xv * r * gv - e_refs[1][...]
        tok = jnp.mean(e * e, axis=-1, keepdims=True)
        a_refs[0][...] += 0.5 * jnp.sum(tok, axis=0, keepdims=True) * jnp.ones((1, LANES), F32)
        dx, dg = _rms_bwd_math(xv, e * (1.0 / D_MODEL), gv)
        o_refs[0][...] = dx
        o_refs[1][...] = dx.astype(BF16)
        a_refs[1][...] += dg

    return _mm(a, w, mode="nn", out_dtypes=(F32, BF16), extras=(res, target), bcast=(g,),
               accs=((1, LANES), (1, D_MODEL)), ref_epi=epilogue, name=name)


def _relu2(acc):
    r = jnp.maximum(acc, 0.0)
    return r * r, r


def _mlp_fwd(x, h, fetch, tag, finish):
    w_up = fetch(f"mlp{tag}_up", h)
    a, r = _mm(h, w_up, mode="nn", b_shard=True, out_dtypes=(BF16, BF16), epi=_relu2, name=f"mlp{tag}_up")
    w_down = fetch(f"mlp{tag}_down", a)
    return finish(a, w_down, x, f"mlp{tag}_down"), (h, a, r, w_up, w_down)


def _mlp_bwd(x, g, saved, dx, dx_bf, tag, after):
    h, a, r, w_up, w_down = saved
    d_down = _mm(a, dx_bf, mode="tn", out_dtypes=(BF16,), name=f"mlp{tag}_dwdown", after=after)
    dup = _mm(dx_bf, w_down, mode="nt", extras=(r,), out_dtypes=(BF16,),
              epi=lambda acc, rv: (acc * (2.0 * rv.astype(F32)),), name=f"mlp{tag}_dup")
    d_up = _mm(h, dup, mode="tn", o_shard=True, out_dtypes=(BF16,), name=f"mlp{tag}_dwup")
    dx_new, dx_new_bf, dg = _mm_norm_bwd(dup, w_up, x, dx, g, f"mlp{tag}_dh")
    return dx_new, dx_new_bf, dg, d_up, d_down


def _rope_tables(L, B):
    rows = L // GRID_W
    row = np.repeat(np.arange(rows, dtype=np.float32), GRID_W)
    col = np.tile(np.arange(GRID_W, dtype=np.float32), rows)
    inv = (ROPE_THETA ** (-np.arange(HEAD_DIM // 4, dtype=np.float32) / (HEAD_DIM // 4))).astype(np.float32)
    ar, ac = row[:, None] * inv, col[:, None] * inv
    cos = np.concatenate([np.cos(ar), np.cos(ar), np.cos(ac), np.cos(ac)], axis=-1)
    sin = np.concatenate([-np.sin(ar), np.sin(ar), -np.sin(ac), np.sin(ac)], axis=-1)
    return jnp.asarray(np.tile(cos, (B, 1)), F32), jnp.asarray(np.tile(sin, (B, 1)), F32)


def _swap_halves(x):
    lane = lax.broadcasted_iota(jnp.int32, x.shape, 1)
    return jnp.where((lane % 64) < 32, pltpu.roll(x, HEAD_DIM - 32, 1), pltpu.roll(x, 32, 1))


def _qk_prep(qkv, cos, sin, q_g, k_g):
    def fn(ins, bs, outs, accs):
        c, s = ins[1][...], ins[2][...]
        for h in range(N_HEADS + N_KV):
            xv = ins[0][:, h * HEAD_DIM:(h + 1) * HEAD_DIM]
            g = bs[0][...] if h < N_HEADS else bs[1][...]
            r = lax.rsqrt(jnp.mean(xv * xv, axis=-1, keepdims=True) + EPS)
            z = xv * r * g
            y = (z * c + _swap_halves(z) * s).astype(BF16)
            if h < N_HEADS:
                outs[0][:, h * HEAD_DIM:(h + 1) * HEAD_DIM] = y
            else:
                outs[1][:, (h - N_HEADS) * HEAD_DIM:(h - N_HEADS + 1) * HEAD_DIM] = y
        outs[2][...] = ins[0][:, (N_HEADS + N_KV) * HEAD_DIM:].astype(BF16)

    kvw = N_KV * HEAD_DIM
    return _rowwise(fn, [qkv, cos, sin], [q_g, k_g], [(D_MODEL, BF16), (kvw, BF16), (kvw, BF16)], tm=512,
                    name="attn_qk_prep")


def _qk_prep_bwd(qkv, dq, dk, dv, cos, sin, q_g, k_g):
    def fn(ins, bs, outs, accs):
        c, s = ins[4][...], ins[5][...]
        for h in range(N_HEADS + N_KV):
            sl = slice(h * HEAD_DIM, (h + 1) * HEAD_DIM)
            xv = ins[0][:, sl]
            if h < N_HEADS:
                g, dy, acc = bs[0][...], ins[1][:, sl], accs[0]
            else:
                ks = slice((h - N_HEADS) * HEAD_DIM, (h - N_HEADS + 1) * HEAD_DIM)
                g, dy, acc = bs[1][...], ins[2][:, ks], accs[1]
            r = lax.rsqrt(jnp.mean(xv * xv, axis=-1, keepdims=True) + EPS)
            xn = xv * r
            dz = dy * c - _swap_halves(dy) * s
            acc[...] += _rsum(dz * xn)
            dxn = dz * g
            outs[0][:, sl] = (r * (dxn - xn * jnp.mean(dxn * xn, axis=-1, keepdims=True))).astype(BF16)
        outs[0][:, (N_HEADS + N_KV) * HEAD_DIM:] = ins[3][...].astype(BF16)

    return _rowwise(fn, [qkv, dq, dk, dv, cos, sin], [q_g, k_g], [(qkv.shape[1], BF16)],
                    [(1, HEAD_DIM), (1, HEAD_DIM)], tm=512, name="attn_qk_prep_bwd")


_EXP2_SCALE = SCALE * math.log2(math.e)


def _exp_rows(q, k):
    s = lax.dot_general(q, k, _NT, preferred_element_type=F32)
    p = jnp.exp2((s - jnp.max(s, axis=-1, keepdims=True)) * _EXP2_SCALE)
    return p, jnp.sum(p, axis=-1, keepdims=True)


def _attn_fwd(q, k, v, B, L, tq=2048, sub=256):
    tq = min(tq, L)
    sub = min(sub, tq)
    nq = L // tq

    def body(q_ref, k_ref, v_ref, o_ref):
        kv, vv = k_ref[...], v_ref[...]
        for c in range(tq // sub):
            rows = slice(c * sub, (c + 1) * sub)
            p, l = _exp_rows(q_ref[rows, :], kv)
            o = jnp.dot(p.astype(BF16), vv, preferred_element_type=F32)
            o_ref[rows, :] = (o * (1.0 / l)).astype(o_ref.dtype)

    return pl.pallas_call(
        body, name="attn_fwd", grid=(B, N_HEADS, nq),
        in_specs=[pl.BlockSpec((tq, HEAD_DIM), lambda b, h, i: (b * nq + i, h)),
                  pl.BlockSpec((L, HEAD_DIM), lambda b, h, i: (b, h // GROUP)),
                  pl.BlockSpec((L, HEAD_DIM), lambda b, h, i: (b, h // GROUP))],
        out_specs=pl.BlockSpec((tq, HEAD_DIM), lambda b, h, i: (b * nq + i, h)),
        out_shape=jax.ShapeDtypeStruct((B * L, D_MODEL), BF16),
        compiler_params=_params(("parallel", "parallel", "parallel")),
    )(q, k, v)


def _attn_bwd(q, k, v, o, do, B, L, tq=2048, sub=512):
    tq = min(tq, L)
    sub = min(sub, tq)
    nq = L // tq

    def body(q_ref, k_ref, v_ref, o_ref, do_ref, dq_ref, dk_ref, dv_ref):
        @pl.when((pl.program_id(2) == 0) & (pl.program_id(3) == 0))
        def _():
            dk_ref[...] = jnp.zeros(dk_ref.shape, F32)
            dv_ref[...] = jnp.zeros(dv_ref.shape, F32)

        kv, vv = k_ref[...], v_ref[...]
        ps, es, dos, qs = [], [], [], []
        for c in range(tq // sub):
            rows = slice(c * sub, (c + 1) * sub)
            qc, doc = q_ref[rows, :], do_ref[rows, :]
            p, l = _exp_rows(qc, kv)
            inv = 1.0 / l
            dp = lax.dot_general(doc, vv, _NT, preferred_element_type=F32)
            delta = jnp.sum(doc.astype(F32) * o_ref[rows, :].astype(F32), axis=-1, keepdims=True)
            e = (p * (dp - delta)).astype(BF16)
            dq_ref[rows, :] = jnp.dot(e, kv, preferred_element_type=F32) * (inv * SCALE)
            ps.append(p.astype(BF16))
            es.append(e)
            dos.append((doc.astype(F32) * inv).astype(BF16))
            qs.append((qc.astype(F32) * (inv * SCALE)).astype(BF16))
        cat = lambda xs: xs[0] if len(xs) == 1 else jnp.concatenate(xs, axis=0)
        dv_ref[...] += lax.dot_general(cat(ps), cat(dos), _TN, preferred_element_type=F32)
        dk_ref[...] += lax.dot_general(cat(es), cat(qs), _TN, preferred_element_type=F32)

    qmap = lambda b, kh, g, i: (b * nq + i, kh * GROUP + g)
    kmap = lambda b, kh, g, i: (b, kh)
    kvw = N_KV * HEAD_DIM
    return pl.pallas_call(
        body, name="attn_bwd", grid=(B, N_KV, GROUP, nq),
        in_specs=[pl.BlockSpec((tq, HEAD_DIM), qmap), pl.BlockSpec((L, HEAD_DIM), kmap),
                  pl.BlockSpec((L, HEAD_DIM), kmap), pl.BlockSpec((tq, HEAD_DIM), qmap),
                  pl.BlockSpec((tq, HEAD_DIM), qmap)],
        out_specs=[pl.BlockSpec((tq, HEAD_DIM), qmap), pl.BlockSpec((L, HEAD_DIM), kmap),
                   pl.BlockSpec((L, HEAD_DIM), kmap)],
        out_shape=[jax.ShapeDtypeStruct((B * L, D_MODEL), F32), jax.ShapeDtypeStruct((B * L, kvw), F32),
                   jax.ShapeDtypeStruct((B * L, kvw), F32)],
        compiler_params=_params(("parallel", "parallel", "arbitrary", "arbitrary")),
    )(q, k, v, o, do)


def _conv_shift(x, t, L, k):
    if k == 2:
        return x
    if k < 2:
        return jnp.where(t >= 2 - k, pltpu.roll(x, 2 - k, 0), 0.0)
    return jnp.where(t < L - (k - 2), pltpu.roll(x, L - (k - 2), 0), 0.0)


def _conv_apply(x, w_ref, L):
    t = lax.broadcasted_iota(jnp.int32, x.shape, 0)
    acc = w_ref[4:5, :] + w_ref[2:3, :] * x
    for k in (0, 1, 3):
        acc = acc + w_ref[k:k + 1, :] * _conv_shift(x, t, L, k)
    return acc


def _conv_bwd(z, g, wb, dz, B, L, tc=256, after=None):
    noff = D_MODEL // tc
    order = _after_operand(after)

    def body(z_ref, g_ref, w_ref, dz_in, *rest):
        dx_ref, dw_ref = rest[len(order):]

        @pl.when(pl.program_id(1) == 0)
        def _():
            dw_ref[...] = jnp.zeros(dw_ref.shape, F32)

        x, gv = z_ref[...], g_ref[...]
        t = lax.broadcasted_iota(jnp.int32, x.shape, 0)
        dx = w_ref[2:3, :] * gv
        for k in (0, 1, 3):
            dx = dx + w_ref[k:k + 1, :] * _conv_shift(gv, t, L, 4 - k)
        dx_ref[...] = dx.astype(BF16)
        for k in range(4):
            dw_ref[k:k + 1, :] += _rsum(_conv_shift(x, t, L, k) * gv)
        dw_ref[4:5, :] += _rsum(gv)

    return pl.pallas_call(
        body, name="rg_conv_bwd", grid=(noff, B),
        in_specs=[pl.BlockSpec((L, tc), lambda j, b: (b, noff + j)), pl.BlockSpec((L, tc), lambda j, b: (b, j)),
                  pl.BlockSpec((SUBLANES, tc), lambda j, b: (0, j)), _ANY] + [_ANY] * len(order),
        out_specs=[pl.BlockSpec((L, tc), lambda j, b: (b, noff + j)),
                   pl.BlockSpec((SUBLANES, tc), lambda j, b: (0, j))],
        out_shape=[jax.ShapeDtypeStruct(dz.shape, dz.dtype), jax.ShapeDtypeStruct((SUBLANES, D_MODEL), F32)],
        input_output_aliases={3: 0},
        compiler_params=_params(("parallel", "arbitrary")),
    )(z, g, wb, dz, *order)


def _softplus(x):
    return jnp.maximum(x, 0.0) + jnp.log1p(jnp.exp(-jnp.abs(x)))


_ROW_BA, _ROW_BX, _ROW_LAM = 0, 2, 4


def _gate_math(xb, pre, vec_ref, d, sl):
    pa = pre[:, (2 * d) * LRU_BW:(2 * d + 1) * LRU_BW] + vec_ref[_ROW_BA + d:_ROW_BA + d + 1, sl]
    px = pre[:, (2 * d + 1) * LRU_BW:(2 * d + 2) * LRU_BW] + vec_ref[_ROW_BX + d:_ROW_BX + d + 1, sl]
    r = 0.5 * jnp.tanh(0.5 * pa) + 0.5
    i = 0.5 * jnp.tanh(0.5 * px) + 0.5
    slope = (-RG_C) * _softplus(-vec_ref[_ROW_LAM + d:_ROW_LAM + d + 1, sl])
    log_a = r * slope
    a = jnp.exp(log_a)
    om = -jnp.tanh(log_a) * (1.0 + a * a)
    rs = lax.rsqrt(om)
    mult = jnp.where(om > 0.0, om * rs, 0.0)
    return a, mult * (i * xb), (r, i, slope, om, mult, rs)


def _gate_bwd(rec, du_f, da_f, du_b, da_b, wcat, gvec):
    def fn(ins, bs, outs, accs):
        for blk in range(LRU_BLOCKS):
            sl = slice(blk * LRU_BW, (blk + 1) * LRU_BW)
            xb = ins[0][:, sl]
            xb16 = xb.astype(BF16)
            w = bs[0][sl, :]
            pre = jnp.dot(xb16, w, preferred_element_type=F32)
            dx = jnp.zeros_like(xb)
            dpre = []
            for d in range(2):
                a, _, (r, i, slope, om, mult, rs) = _gate_math(xb, pre, bs[1], d, sl)
                du, da = ins[1 + 2 * d][:, sl], ins[2 + 2 * d][:, sl]
                t = du * xb
                d_i = t * mult
                dx = dx + du * mult * i
                dlog = da * a - (t * i) * ((1.0 - om) * rs)
                d_r = dlog * slope
                d_sp = _rsum(dlog * r) * (-RG_C)
                lam = bs[1][_ROW_LAM + d:_ROW_LAM + d + 1, sl]
                accs[2][_ROW_LAM + d:_ROW_LAM + d + 1, sl] += d_sp * (-jax.nn.sigmoid(-lam))
                dpa = d_r * r * (1.0 - r)
                dpx = d_i * i * (1.0 - i)
                accs[2][_ROW_BA + d:_ROW_BA + d + 1, sl] += _rsum(dpa)
                accs[2][_ROW_BX + d:_ROW_BX + d + 1, sl] += _rsum(dpx)
                dpre += [dpa, dpx]
            dpre = jnp.concatenate(dpre, axis=1).astype(BF16)
            dw = lax.dot_general(xb16, dpre, _TN, preferred_element_type=F32)
            for d in range(2):
                rows = slice(d * D_MODEL + blk * LRU_BW, d * D_MODEL + (blk + 1) * LRU_BW)
                accs[0][rows, :] += dw[:, (2 * d) * LRU_BW:(2 * d + 1) * LRU_BW]
                accs[1][rows, :] += dw[:, (2 * d + 1) * LRU_BW:(2 * d + 2) * LRU_BW]
            outs[0][:, sl] = dx + lax.dot_general(dpre, w, _NT, preferred_element_type=F32)

    gate_shape = (2 * D_MODEL, LRU_BW)
    return _rowwise(fn, [rec, du_f, da_f, du_b, da_b], [wcat, gvec], [(D_MODEL, F32)],
                    [gate_shape, gate_shape, (SUBLANES, D_MODEL)], tm=512, name="rg_gate_bwd")


def _as_time_blocks(x):
    return x.reshape(x.shape[0] // SUBLANES, SUBLANES, x.shape[1])


def _scan_call(body, ins, n_out, B, L, tc, name):
    nb = L // SUBLANES
    spec = pl.BlockSpec((nb, SUBLANES, tc), lambda b, j: (b, 0, j))
    T = ins[0].shape[0]
    outs = pl.pallas_call(
        functools.partial(body, nb), name=name, grid=(B, D_MODEL // tc),
        in_specs=[spec] * len(ins), out_specs=[spec] * n_out,
        out_shape=[jax.ShapeDtypeStruct((T // SUBLANES, SUBLANES, D_MODEL), F32)] * n_out,
        compiler_params=_params(("parallel", "parallel")),
    )(*[_as_time_blocks(x) for x in ins])
    return [o.reshape(T, D_MODEL) for o in outs]


def _block_scan(A, U, reverse):
    row = lax.broadcasted_iota(jnp.int32, A.shape, 0)
    for s in (1, 2, 4):
        shift = SUBLANES - s if reverse else s
        valid = (row < SUBLANES - s) if reverse else (row >= s)
        a_sh = jnp.where(valid, pltpu.roll(A, shift, 0), 1.0)
        u_sh = jnp.where(valid, pltpu.roll(U, shift, 0), 0.0)
        U = A * u_sh + U
        A = A * a_sh
    return A, U


_LAST = SUBLANES - 1
SCAN_UNROLL = 8


def _loop_blocks(nb, step, init):
    def group(g, carry):
        for k in range(SCAN_UNROLL):
            carry = step(g * SCAN_UNROLL + k, carry)
        return carry

    return lax.fori_loop(0, nb // SCAN_UNROLL, group, init)


def _scan_bwd(dy, a_f, h_f, a_b, h_b, B, L, tc=256):
    def body(nb, dy_r, af, hf, ab, hb, duf, daf, dub, dab):
        def step(i, carry):
            c1, c2 = carry
            ir = nb - 1 - i
            row = lax.broadcasted_iota(jnp.int32, (SUBLANES, tc), 0)
            a_up = jnp.where(row == _LAST, af[jnp.minimum(ir + 1, nb - 1), :1, :], pltpu.roll(af[ir], _LAST, 0))
            p, lam = _block_scan(a_up, dy_r[ir], True)
            lam = lam + p * c1
            before = hf[jnp.maximum(ir - 1, 0), _LAST:, :] * (ir > 0).astype(F32)
            duf[ir] = lam
            daf[ir] = lam * jnp.where(row == 0, before, pltpu.roll(hf[ir], 1, 0))
            a_dn = jnp.where(row == 0, ab[jnp.maximum(i - 1, 0), _LAST:, :], pltpu.roll(ab[i], 1, 0))
            p2, lam2 = _block_scan(a_dn, dy_r[i], False)
            lam2 = lam2 + p2 * c2
            after = hb[jnp.minimum(i + 1, nb - 1), :1, :] * (i < nb - 1).astype(F32)
            dub[i] = lam2
            dab[i] = lam2 * jnp.where(row == _LAST, after, pltpu.roll(hb[i], _LAST, 0))
            return lam[:1, :], lam2[_LAST:, :]

        zero = jnp.zeros((1, tc), F32)
        _loop_blocks(nb, step, (zero, zero))

    return _scan_call(body, [dy, a_f, h_f, a_b, h_b], 4, B, L, tc, "rg_scan_bwd")


_GELU_C = math.sqrt(2.0 / math.pi)


def _gelu_parts(x):
    th = jnp.tanh(_GELU_C * (x + 0.044715 * x * x * x))
    return 0.5 * x * (1.0 + th), th


def _mm_gated_out_bwd(dx, w_out, h_f, h_b, z, name, after=None):
    def epilogue(acc, e_refs, b_refs, o_refs, a_refs):
        x = e_refs[2][...]
        gl, th = _gelu_parts(x)
        dgl = 0.5 * (1.0 + th) + 0.5 * x * (1.0 - th * th) * (_GELU_C * (1.0 + 3.0 * 0.044715 * x * x))
        o_refs[0][...] = acc * gl
        o_refs[1][...] = (acc * (e_refs[0][...] + e_refs[1][...]) * dgl).astype(BF16)

    return _mm(dx, w_out, mode="nt", out_dtypes=(F32, BF16), out_cols=(D_MODEL, 2 * D_MODEL), extras=(h_f, h_b, z),
               ref_epi=epilogue, name=name, after=after)


def _row_block(i):
    return pl.ds(pl.multiple_of(i * SUBLANES, SUBLANES), SUBLANES)


def _rg_mix_fwd(z, conv_wb, wcat, gvec, B, L):
    nb = L // SUBLANES
    n_g = D_MODEL // LRU_BW

    def body(zg_ref, zr_ref, cw_ref, w_ref, gv_ref, rec_ref, af_s, ab_s, hf_ref, hb_ref, yg_ref, uf_s, ub_s):
        rec = _conv_apply(zr_ref[...], cw_ref, L)
        rec_ref[...] = rec
        pre = jnp.dot(rec.astype(BF16), w_ref[...], preferred_element_type=F32)
        for d, (a_s, u_s) in enumerate(((af_s, uf_s), (ab_s, ub_s))):
            a, u, _ = _gate_math(rec, pre, gv_ref, d, slice(None))
            a_s[...] = a
            u_s[...] = u

        def step(i, carry):
            c1, c2 = carry
            rows, rows_b = _row_block(i), _row_block(nb - 1 - i)
            p, h = _block_scan(af_s[rows, :], uf_s[rows, :], False)
            h = h + p * c1
            hf_ref[rows, :] = h
            p2, h2 = _block_scan(ab_s[rows_b, :], ub_s[rows_b, :], True)
            h2 = h2 + p2 * c2
            hb_ref[rows_b, :] = h2
            return h[_LAST:, :], h2[:1, :]

        zero = jnp.zeros((1, LRU_BW), F32)
        _loop_blocks(nb, step, (zero, zero))
        gl, _ = _gelu_parts(zg_ref[...])
        yg_ref[...] = ((hf_ref[...] + hb_ref[...]) * gl).astype(BF16)

    seq = lambda off: pl.BlockSpec((L, LRU_BW), lambda b, g: (b, off + g))
    vec = pl.BlockSpec((SUBLANES, LRU_BW), lambda b, g: (0, g))
    T = B * L
    return pl.pallas_call(
        body, name="rg_mix", grid=(B, n_g),
        in_specs=[seq(0), seq(n_g), vec, pl.BlockSpec((LRU_BW, 4 * LRU_BW), lambda b, g: (g, 0)), vec],
        out_specs=[seq(0)] * 6,
        out_shape=[jax.ShapeDtypeStruct((T, D_MODEL), F32)] * 5 + [jax.ShapeDtypeStruct((T, D_MODEL), BF16)],
        scratch_shapes=[pltpu.VMEM((L, LRU_BW), F32)] * 2,
        compiler_params=_params(("parallel", "parallel")),
    )(z, z, conv_wb, wcat, gvec)


def _make_wcat(w_a, w_x):
    g = jnp.stack([w_a[0, 0], w_x[0, 0], w_a[0, 1], w_x[0, 1]])
    return jnp.transpose(g, (1, 2, 0, 3)).reshape(D_MODEL, 4 * LRU_BW)


def _rows_at(part, first):
    return jnp.pad(part, ((first, SUBLANES - first - part.shape[0]), (0, 0)))


def _qk_slot(q_g, k_g):
    wide = lambda v, at: jnp.pad(v, ((0, SUBLANES - 1), (at, D_MODEL - at - HEAD_DIM)))
    return wide(q_g, 0) + wide(k_g, HEAD_DIM)


def _local_step(x, target, P, fetch, emit, B, L, after=None):
    g_mix, g_mlp = P["norm_mix_g"], P["norm_mlp_g"]
    h0 = _rms_fwd(x, g_mix[0:1], "rg_norm", after=after)
    w_in, conv_wb, wcat, gvec = fetch("rg", h0)
    z = _mm(h0, w_in, mode="nn", b_shard=True, name="rg_in")
    rec, a_f, a_b, h_f, h_b, yg = _rg_mix_fwd(z, conv_wb, wcat, gvec, B, L)
    w_out = fetch("rg_out", yg)
    x1, h1 = _mm_res_norm(yg, w_out, x, g_mlp[0:1], "rg_out")
    (x2, h3), mlp0 = _mlp_fwd(x1, h1, fetch, 0, lambda a, w, res, name: _mm_res_norm(a, w, res, g_mix[1:2], name))
    w_qkv, w_o = fetch("att", h3)
    qkv = _mm(h3, w_qkv, mode="nn", b_shard=True, name="attn_qkv")
    cos, sin = _rope_tables(L, B)
    qh, kh, vh = _qk_prep(qkv, cos, sin, P["q_g"], P["k_g"])
    o = _attn_fwd(qh, kh, vh, B, L)
    x3, h4 = _mm_res_norm(o, w_o, x2, g_mlp[1:2], "attn_out")
    (dx4, dx4_bf, loss_acc, d_final_g), mlp1 = _mlp_fwd(
        x3, h4, fetch, 1, lambda a, w, res, name: _mm_final_loss(a, w, res, target, P["final_g"], name))

    dx3, dx3_bf, dg_mlp1, d_up1, d_down1 = _mlp_bwd(x3, g_mlp[1:2], mlp1, dx4, dx4_bf, 1, None)
    tok = emit("mlp1", [d_up1, d_down1])
    d_wo = _mm(o, dx3_bf, mode="tn", out_dtypes=(BF16,), name="attn_dwo", after=tok)
    do = _mm(dx3_bf, w_o, mode="nt", out_dtypes=(BF16,), name="attn_do")
    dq, dk, dv = _attn_bwd(qh, kh, vh, o, do, B, L)
    dqkv, dq_g, dk_g = _qk_prep_bwd(qkv, dq, dk, dv, cos, sin, P["q_g"], P["k_g"])
    d_wqkv = _mm(h3, dqkv, mode="tn", o_shard=True, out_dtypes=(BF16,), name="attn_dwqkv")
    tok = emit("att", [d_wqkv, d_wo])
    dx2, dx2_bf, dg_mix1 = _mm_norm_bwd(dqkv, w_qkv, x2, dx3, g_mix[1:2], "attn_dh", after=tok)
    tok = emit("point_attn_done", [dx2_bf])
    dx1, dx1_bf, dg_mlp0, d_up0, d_down0 = _mlp_bwd(x1, g_mlp[0:1], mlp0, dx2, dx2_bf, 0, tok)
    d_wout = _mm(yg, dx1_bf, mode="tn", out_dtypes=(BF16,), name="rg_dwout")
    tok = emit("mlp0", [d_up0, d_down0, d_wout])
    dy, dgate = _mm_gated_out_bwd(dx1_bf, w_out, h_f, h_b, z, "rg_dyg", after=tok)
    du_f, da_f, du_b, da_b = _scan_bwd(dy, a_f, h_f, a_b, h_b, B, L)
    drec_c, d_wa, d_wx, d_gvec = _gate_bwd(rec, du_f, da_f, du_b, da_b, wcat, gvec)
    tok = emit("gates", [d_wa, d_wx])
    dz, d_convwb = _conv_bwd(z, drec_c, conv_wb, dgate, B, L, after=tok)
    tok = emit("point_mix_done", [dz])
    d_win = _mm(h0, dz, mode="tn", o_shard=True, out_dtypes=(BF16,), name="rg_dwin", after=tok)
    tok = emit("rg_in", [d_win])
    grad_x, _, dg_mix0 = _mm_norm_bwd(dz, w_in, x, dx1, g_mix[0:1], "rg_dh", after=tok)

    norms = (_rows_at(dg_mix0, 0) + _rows_at(dg_mix1, 1) + _rows_at(dg_mlp0, 2) + _rows_at(dg_mlp1, 3)
             + _rows_at(d_final_g, 4)
             + jnp.pad(loss_acc, ((LOSS_ROW, SUBLANES - 1 - LOSS_ROW), (0, D_MODEL - LANES))))
    vec = jnp.concatenate([norms, d_convwb, d_gvec, _qk_slot(dq_g, dk_g)], axis=0)
    return grad_x, vec


_MESH = pl.DeviceIdType.MESH


def _place():
    x, y, c = lax.axis_index("x"), lax.axis_index("y"), lax.axis_index("c")
    peers = [((1 - x) if j & 2 else x, (1 - y) if j & 1 else y) for j in (1, 2, 3)]
    return x, y, c, peers


def _sum_leading(slots, name):
    def body(s_ref, o_ref):
        acc = s_ref[0]
        for d in range(1, slots.shape[0]):
            acc = acc + s_ref[d]
        o_ref[...] = acc

    return pl.pallas_call(body, name=name, out_shape=jax.ShapeDtypeStruct(slots.shape[1:], slots.dtype))(slots)


_HBM = pl.BlockSpec(memory_space=pltpu.HBM)
_SEM = pl.BlockSpec(memory_space=pltpu.SEMAPHORE)
_EFFECT = pltpu.SideEffectType.DATAFLOW_SIDE_EFFECTING


_COPIES = dict(gather=N_CHIPS - 1, scatter=N_CHIPS - 1, swap=1, spread=N_DEVICES - 1,
               gather_half=N_CHIPS - 1, share_half=N_CHIPS - 1)


def _split_copies(kind, srcs, lands, send, recv):
    x, y, c, peers = _place()
    me = 2 * x + y
    per = _COPIES[kind]
    out = []
    for a in range(len(lands)):
        for j in range(per):
            if kind == "swap":
                src, there, here, dev = srcs[a], lands[a], lands[a], (x, y, 1 - c)
            elif kind == "spread":
                k = j + 1
                dev = ((1 - x) if k & 4 else x, (1 - y) if k & 2 else y, (1 - c) if k & 1 else c)
                mine = lands[a].at[4 * x + 2 * y + c]
                src, there, here = mine, mine, lands[a].at[4 * dev[0] + 2 * dev[1] + dev[2]]
            else:
                px, py = peers[j]
                dev = (px, py, c)
                if kind == "gather":
                    src, there, here = lands[a].at[me], lands[a].at[me], lands[a].at[2 * px + py]
                elif kind in ("gather_half", "share_half"):
                    half = lands[a].shape[1] // 2
                    mine, other = pl.ds(c * half, half), pl.ds((1 - c) * half, half)
                    if kind == "gather_half":
                        src = there = lands[a].at[me, mine]
                        here = lands[a].at[2 * px + py, mine]
                    else:
                        src = there = lands[a].at[2 * px + py, mine]
                        here = lands[a].at[2 * px + py, other]
                        dev = (x, y, 1 - c)
                else:
                    src, there, here = srcs[a].at[2 * px + py], lands[a].at[j], lands[a].at[j]
            mk = functools.partial(
                pltpu.make_async_remote_copy, src_ref=src, send_sem=send.at[per * a + j],
                recv_sem=recv.at[per * a + j], device_id=dev, device_id_type=_MESH)
            out.append((functools.partial(mk, dst_ref=there), functools.partial(mk, dst_ref=here)))
    return out


def _exchange_start(kind, srcs, lands, name, after=None):
    arrays = list(srcs) + list(lands)
    n_s, n, n_all = len(srcs), len(lands), len(srcs) + len(lands)
    n_sem = _COPIES[kind] * n
    order = _after_operand(after)
    n_x = len(order)

    def body(*refs):
        send, recv = refs[n_all + n_x], refs[n_all + n_x + 1]
        token = refs[-1]
        for started, _ in _split_copies(kind, refs[:n_s], refs[n_s:n_all], send, recv):
            started().start()
        token[...] = jnp.zeros(token.shape, F32)

    res = pl.pallas_call(
        body, name=name,
        out_shape=(pltpu.SemaphoreType.DMA((n_sem,)), pltpu.SemaphoreType.DMA((n_sem,)),
                   *[pltpu.HBM(a.shape, a.dtype) for a in arrays], jax.ShapeDtypeStruct((SUBLANES, LANES), F32)),
        in_specs=[_HBM] * n_all + [_ANY] * n_x,
        out_specs=(_SEM, _SEM, *[_HBM] * n_all, pl.BlockSpec(memory_space=pltpu.VMEM)),
        input_output_aliases={i: 2 + i for i in range(n_all)},
        compiler_params=pltpu.CompilerParams(has_side_effects=_EFFECT),
    )(*[pltpu.with_memory_space_constraint(a, pltpu.HBM) for a in arrays], *order)
    return (res[0], res[1], res[2:2 + n_s], res[2 + n_s:2 + n_all]), res[-1]


def _gather_start_groups(land_groups, name, after=None, kind="gather"):
    arrays = [a for group in land_groups for a in group]
    n_all, n_g = len(arrays), len(land_groups)
    order = _after_operand(after)
    n_x = len(order)

    def body(*refs):
        first = 0
        for gi, group in enumerate(land_groups):
            send, recv = refs[n_all + n_x + 2 * gi], refs[n_all + n_x + 2 * gi + 1]
            for started, _ in _split_copies(kind, [], refs[first:first + len(group)], send, recv):
                started().start()
            first += len(group)
        refs[-1][...] = jnp.zeros(refs[-1].shape, F32)

    sems = [pltpu.SemaphoreType.DMA((_COPIES[kind] * len(group),)) for group in land_groups for _ in range(2)]
    res = pl.pallas_call(
        body, name=name,
        out_shape=(*sems, *[pltpu.HBM(a.shape, a.dtype) for a in arrays], jax.ShapeDtypeStruct((SUBLANES, LANES), F32)),
        in_specs=[_HBM] * n_all + [_ANY] * n_x,
        out_specs=(*[_SEM] * (2 * n_g), *[_HBM] * n_all, pl.BlockSpec(memory_space=pltpu.VMEM)),
        input_output_aliases={i: 2 * n_g + i for i in range(n_all)},
        compiler_params=pltpu.CompilerParams(has_side_effects=_EFFECT),
    )(*[pltpu.with_memory_space_constraint(a, pltpu.HBM) for a in arrays], *order)
    handles, first = [], 2 * n_g
    for gi, group in enumerate(land_groups):
        handles.append((res[2 * gi], res[2 * gi + 1], [], res[first:first + len(group)]))
        first += len(group)
    return handles, res[-1]


def _exchange_wait(kind, handle, after, name):
    send, recv, srcs, lands = handle
    arrays = list(srcs) + list(lands)
    n_s, n_all = len(srcs), len(arrays)
    order = list(after) if isinstance(after, (list, tuple)) else [after]

    def body(*refs):
        for started, landing in _split_copies(kind, refs[:n_s], refs[n_s:n_all], refs[n_all], refs[n_all + 1]):
            started().wait_send()
            landing().wait_recv()

    res = pl.pallas_call(
        body, name=name, out_shape=[pltpu.HBM(a.shape, a.dtype) for a in arrays],
        in_specs=[_HBM] * n_all + [_SEM, _SEM] + [_ANY] * len(order), out_specs=[_HBM] * n_all,
        input_output_aliases={i: i for i in range(n_all)},
        compiler_params=pltpu.CompilerParams(has_side_effects=_EFFECT),
    )(*arrays, send, recv, *order)
    return res[:n_s], res[n_s:]


def _index_operand(i):
    return jnp.reshape(i, (1,)).astype(jnp.int32)


def _cast_into_slot(src, row0, rows, me, dtype, name, after=None, add=None, n_slots=N_CHIPS):
    cols = src.shape[1]
    tm = min(512, rows)
    order = _after_operand(after)
    terms = [src] + ([] if add is None else [add])

    def body(me_ref, *rest):
        val = rest[0][...]
        if add is not None:
            val = val + rest[1][...]
        rest[-1][...] = val.astype(dtype)

    return pl.pallas_call(
        body, name=name,
        grid_spec=pltpu.PrefetchScalarGridSpec(
            num_scalar_prefetch=1, grid=(rows // tm,),
            in_specs=[pl.BlockSpec((tm, cols), lambda i, me_ref: (i + row0 // tm, 0))] * len(terms)
            + [_ANY] * len(order),
            out_specs=pl.BlockSpec((None, tm, cols), lambda i, me_ref: (me_ref[0], i, 0))),
        out_shape=jax.ShapeDtypeStruct((n_slots, rows, cols), dtype), compiler_params=_params(("parallel",)),
    )(_index_operand(me), *terms, *order)


def _sum_slots(mine, r, me, name):
    _, rows, cols = r.shape
    tm = min(512, rows)

    def body(me_ref, own_ref, r_ref, o_ref):
        o_ref[...] = ((own_ref[...].astype(F32) + r_ref[0].astype(F32)) + r_ref[1].astype(F32)) + r_ref[2].astype(F32)

    return pl.pallas_call(
        body, name=name,
        grid_spec=pltpu.PrefetchScalarGridSpec(
            num_scalar_prefetch=1, grid=(rows // tm,),
            in_specs=[pl.BlockSpec((None, tm, cols), lambda i, me_ref: (me_ref[0], i, 0)),
                      pl.BlockSpec((N_CHIPS - 1, tm, cols), lambda i, me_ref: (0, i, 0))],
            out_specs=pl.BlockSpec((tm, cols), lambda i, me_ref: (i, 0))),
        out_shape=jax.ShapeDtypeStruct((rows, cols), F32), compiler_params=_params(("parallel",)),
    )(_index_operand(me), mine, r)


def _adamw(w, m, v, ps, qs, name):
    rows, cols = w.shape
    seg_rows = ps[0].shape[0]
    tm = min(512, seg_rows)
    while seg_rows % tm:
        tm -= SUBLANES
    per, n_seg = seg_rows // tm, len(ps)
    parts = list(ps) + ([] if qs is None else list(qs))

    def body(w_ref, m_ref, v_ref, *rest):
        g_refs, outs = rest[:len(parts)], rest[len(parts):]
        grad = lambda s: g_refs[s][...] if qs is None else g_refs[s][...] + g_refs[n_seg + s][...]
        g = grad(0)
        for s in range(1, n_seg):
            g = jnp.where(pl.program_id(0) >= s * per, grad(s), g)
        m1 = ADAM_B1 * m_ref[...] + (1.0 - ADAM_B1) * g
        v1 = ADAM_B2 * v_ref[...] + (1.0 - ADAM_B2) * (g * g)
        m_hat = m1 / (1.0 - ADAM_B1 ** ADAM_STEP)
        v_hat = v1 / (1.0 - ADAM_B2 ** ADAM_STEP)
        outs[0][...] = g
        outs[1][...] = (-ADAM_LR) * (m_hat / (jnp.sqrt(v_hat) + ADAM_EPS) + ADAM_WD * w_ref[...])
        outs[2][...] = m1
        outs[3][...] = v1

    row_spec = pl.BlockSpec((tm, cols), lambda i: (i, 0))
    seg_spec = lambda s: pl.BlockSpec((tm, cols), lambda i: (jnp.clip(i - s * per, 0, per - 1), 0))
    return pl.pallas_call(
        body, name=name, grid=(rows // tm,),
        in_specs=[row_spec] * 3 + [seg_spec(s) for s in range(n_seg)] * (1 if qs is None else 2),
        out_specs=[row_spec] * 4, out_shape=[jax.ShapeDtypeStruct((rows, cols), F32)] * 4,
        compiler_params=_params(("arbitrary",)),
    )(w, m, v, *parts)


def _put_cols(shard, me):
    full = jnp.zeros((shard.shape[0], D_MODEL), F32)
    return lax.dynamic_update_slice(full, shard, (0, me * (D_MODEL // N_CHIPS)))


def _gate_vec_slot(b_a, b_x, lam):
    return _rows_at(b_a, _ROW_BA) + _rows_at(b_x, _ROW_BX) + _rows_at(lam, _ROW_LAM)


def _pack_vec(p, me):
    return jnp.concatenate([
        _rows_at(p["norm_mix_g"], 0) + _rows_at(p["norm_mlp_g"], 2) + _rows_at(p["final_g"][None], 4),
        _rows_at(_put_cols(p["rg_conv_w"][0, :, 0, :], me), 0) + _rows_at(p["rg_conv_b"], 4),
        _gate_vec_slot(_put_cols(p["rg_b_a"][0], me), _put_cols(p["rg_b_x"][0], me), _put_cols(p["rg_lam"][0], me)),
        _qk_slot(p["at_q_g"], p["at_k_g"]),
    ], axis=0)


def _unpack_vec(r, me):
    def cols(rows):
        return lax.dynamic_slice(rows, (0, me * (D_MODEL // N_CHIPS)), (rows.shape[0], D_MODEL // N_CHIPS))

    gate = r[16:24]
    return dict(
        norm_mix_g=r[0:2], norm_mlp_g=r[2:4], final_g=r[4], rg_conv_w=cols(r[8:12])[None, :, None, :],
        rg_conv_b=r[12:13], rg_b_a=cols(gate[_ROW_BA:_ROW_BA + 2])[None], rg_b_x=cols(gate[_ROW_BX:_ROW_BX + 2])[None],
        rg_lam=cols(gate[_ROW_LAM:_ROW_LAM + 2])[None], at_q_g=r[24:25, 0:HEAD_DIM],
        at_k_g=r[24:25, HEAD_DIM:2 * HEAD_DIM])


_WEIGHTS = ['norm_mix_g', 'norm_mlp_g', 'rg_w_in', 'rg_conv_w', 'rg_conv_b', 'rg_w_a', 'rg_b_a', 'rg_w_x', 'rg_b_x',
            'rg_lam', 'rg_w_out', 'at_w_qkv', 'at_q_g', 'at_k_g', 'at_w_o', 'mlp_w_up', 'mlp_w_down', 'final_g']
_BIG = dict(rg_w_in=["rg_w_in"], rg_w_out=["rg_w_out"], at_w_qkv=["at_w_qkv"], at_w_o=["at_w_o"],
            mlp_w_up=["up0", "up1"], mlp_w_down=["down0", "down1"])


def kernel(x, *args):
    n_w = len(_WEIGHTS)
    w = dict(zip(_WEIGHTS, args[:n_w]))
    target = args[n_w]
    m = dict(zip(_WEIGHTS, args[n_w + 1:2 * n_w + 1]))
    v = dict(zip(_WEIGHTS, args[2 * n_w + 1:3 * n_w + 1]))
    B, L, _ = x.shape
    T = B * L
    me = 2 * lax.axis_index("x") + lax.axis_index("y")

    vec = jnp.concatenate([_gate_vec_slot(w["rg_b_a"][0], w["rg_b_x"][0], w["rg_lam"][0]),
                           _rows_at(w["rg_conv_w"][0, :, 0, :], 0)], axis=0)
    flat = lambda a: a.reshape(-1, a.shape[-1])
    rows_of = lambda k: w[k].shape[-2]
    groups = [("rg", [("rg_w_in", 0, BF16), (vec, 0, F32)]), ("rg_out", [("rg_w_out", 0, BF16)]),
              ("mlp0_up", [("mlp_w_up", 0, BF16)]), ("mlp0_down", [("mlp_w_down", 0, BF16)]),
              ("att", [("at_w_qkv", 0, BF16), ("at_w_o", 0, BF16)]),
              ("mlp1", [("mlp_w_up", 1, BF16), ("mlp_w_down", 1, BF16)])]

    def landing_zones(group, members, after):
        lands = []
        for n, (k, layer, dtype) in enumerate(members):
            src, rows = (flat(w[k]), rows_of(k)) if isinstance(k, str) else (k, k.shape[0])
            lands.append(_cast_into_slot(src, layer * rows, rows, me, dtype, f"place_{group}{n}", after=after))
        return lands

    halves, gathers = {}, {}
    halves["rg"], tok = _exchange_start("gather_half", [], landing_zones(*groups[0], None), "gather_rg_start")
    handles, tok = _gather_start_groups([landing_zones(g, members, tok) for g, members in groups[1:]],
                                        "gather_rest_start", after=tok, kind="gather_half")
    halves.update(zip([g for g, _ in groups[1:]], handles))
    wcat = _make_wcat(w["rg_w_a"], w["rg_w_x"]).astype(BF16)

    packs = [_pack_vec(p, me) for p in (w, m, v)]

    ready = {}

    def share(some, after, name):
        landed = [_exchange_wait("gather_half", halves[g], after, f"gather_{g}_landed")[1] for g in some]
        handles, _ = _gather_start_groups(landed, name, kind="share_half")
        gathers.update(zip(some, handles))

    def fetch(what, after):
        if what in ready:
            return ready[what]
        group = "mlp1" if what.startswith("mlp1") else what
        if group == "rg":
            share(["rg"], [after, wcat] + packs, "share_rg_start")
        elif group == "rg_out":
            share(["rg_out", "mlp0_up", "mlp0_down", "att"], after, "share_early_start")
        _, full = _exchange_wait("share_half", gathers[group], after, f"gather_{group}_wait")
        if group == "att":
            share(["mlp1"], after, "share_mlp1_start")
        if group == "rg":
            vec_full = jnp.transpose(full[1], (1, 0, 2)).reshape(2 * SUBLANES, D_MODEL)
            conv_wb = vec_full[SUBLANES:] + _rows_at(w["rg_conv_b"], 4)
            return full[0], conv_wb, wcat, vec_full[:SUBLANES]
        if group == "rg_out":
            return full[0].reshape(D_MODEL, D_MODEL)
        if group == "att":
            return full[0], full[1].reshape(D_MODEL, D_MODEL)
        if group == "mlp1":
            ready["mlp1_up"], ready["mlp1_down"] = full[0], full[1].reshape(4 * D_MODEL, D_MODEL)
            return ready[what]
        return full[0] if group == "mlp0_up" else full[0].reshape(4 * D_MODEL, D_MODEL)

    names = dict(mlp1=["up1", "down1"], att=["at_w_qkv", "at_w_o"], mlp0=["up0", "down0", "rg_w_out"],
                 rg_in=["rg_w_in"], gates=["rg_w_a", "rg_w_x"])
    scatters, swaps, P, Q, res = {}, [], {}, {}, {}

    def start_scatter(group, grads):
        srcs = [g.reshape(N_CHIPS, -1, g.shape[-1]) for g in grads]
        lands = [lax.empty((N_CHIPS - 1,) + s.shape[1:], s.dtype) for s in srcs]
        scatters[group], token = _exchange_start("scatter", srcs, lands, f"scatter_{group}_start")
        return token

    def settle(groups, after):
        keys, parts = [], []
        for group in groups:
            srcs, lands = _exchange_wait("scatter", scatters[group], after, f"scatter_{group}_wait")
            for k, s, r in zip(names[group], srcs, lands):
                keys.append(k)
                parts.append(_sum_slots(s, r, me, f"sum_{k}"))
        handle, token = _exchange_start("swap", parts, [lax.empty(p.shape, F32) for p in parts],
                                        f"swap_{groups[0]}_start")
        swaps.append((keys, handle, f"swap_{groups[0]}_wait"))
        return token

    def finish(after):
        for keys, handle, name in swaps:
            mine, theirs = _exchange_wait("swap", handle, after, name)
            P.update(zip(keys, mine))
            Q.update(zip(keys, theirs))
        swaps.clear()
        last = after
        for k, parts in _BIG.items():
            if k in res or any(p not in P for p in parts):
                continue
            shape = w[k].shape
            two_d = lambda a: a.reshape(-1, shape[-1])
            outs = _adamw(two_d(w[k]), two_d(m[k]), two_d(v[k]), [P[p] for p in parts], [Q[p] for p in parts],
                          f"adamw_{k}")
            res[k] = [o.reshape(shape) for o in outs]
            last = outs[0]
        if "rg_w_a" in P and "gates" not in gathers:
            lands = [_cast_into_slot(P[k], 0, P[k].shape[0], me, F32, f"place_{k}", after=last, add=Q[k])
                     for k in names["gates"]]
            gathers["gates"], last = _exchange_start("gather", [], lands, "gather_gates_start", after=last)
        return last

    def emit(event, arrays):
        if event == "point_attn_done":
            return None
        if event == "point_mix_done":
            return settle(["mlp1", "att", "mlp0"], arrays[0])
        token = start_scatter(event, arrays)
        if event == "rg_in":
            return finish(settle(["gates"], token))
        return token

    P_vec = dict(norm_mix_g=w["norm_mix_g"], norm_mlp_g=w["norm_mlp_g"], final_g=w["final_g"][None],
                 q_g=w["at_q_g"], k_g=w["at_k_g"])
    grad_x, vec_part = _local_step(x.reshape(T, D_MODEL), target.reshape(T, D_MODEL), P_vec, fetch, emit, B, L,
                                   after=tok)

    me8 = 2 * me + lax.axis_index("c")
    vec_slots = _cast_into_slot(vec_part, 0, VEC_ROWS, me8, F32, "place_vec", n_slots=N_DEVICES)
    spread, tok = _exchange_start("spread", [], [vec_slots], "spread_vec_start")
    last = finish(settle(["rg_in"], tok))
    _, gate_grads = _exchange_wait("gather", gathers["gates"], last, "gather_gates_wait")
    for k, g in zip(names["gates"], gate_grads):
        two_d = lambda a: a.reshape(g.shape[0] * g.shape[1], g.shape[2])
        outs = _adamw(two_d(w[k]), two_d(m[k]), two_d(v[k]), [two_d(g)], None, f"adamw_{k}")
        res[k] = [o.reshape(w[k].shape) for o in outs]
        last = outs[0]
    _, (vec_all,) = _exchange_wait("spread", spread, last, "spread_vec_wait")
    vec_grad = _sum_leading(vec_all, "sum_vec")
    loss = vec_grad[LOSS_ROW, 0]
    outs = _adamw(*packs, [vec_grad], None, "adamw_vec")
    unpacked = [_unpack_vec(o, me) for o in outs]
    for k in _WEIGHTS:
        if k not in res:
            res[k] = [u[k] for u in unpacked]

    result = [loss, grad_x.reshape(B, L, D_MODEL)]
    for slot in range(4):
        result += [res[k][slot] for k in _WEIGHTS]
    return tuple(result)
```

```python
import functools
import math

import jax
import jax.numpy as jnp
import numpy as np
from jax import lax
from jax.experimental import pallas as pl
from jax.experimental.pallas import tpu as pltpu

F32 = jnp.float32
BF16 = jnp.bfloat16

D_MODEL = 1024
HEAD_DIM = 128
N_HEADS = 8
N_KV = 2
GROUP = N_HEADS // N_KV
LRU_BLOCKS = 8
LRU_BW = 128
GRID_W = 64
ROPE_THETA = 10000.0
EPS = 1e-6
RG_C = 8.0
SCALE = 1.0 / math.sqrt(HEAD_DIM)
N_CHIPS = 4

ADAM_LR = 0.001
ADAM_B1 = 0.9
ADAM_B2 = 0.999
ADAM_EPS = 1e-08
ADAM_WD = 0.01
ADAM_STEP = 10

V7X_VMEM_BYTES = 64 * 1024 * 1024
VMEM_LIMIT = V7X_VMEM_BYTES * 3 // 4
LANES = 128
SUBLANES = 8

N_DEVICES = 8
VEC_ROWS = 32
LOSS_ROW = 5


def _params(sem):
    return pltpu.CompilerParams(dimension_semantics=sem, vmem_limit_bytes=VMEM_LIMIT)


_ANY = pl.BlockSpec(memory_space=pl.ANY)
_NN = (((1,), (0,)), ((), ()))
_NT = (((1,), (1,)), ((), ()))
_TN = (((0,), (0,)), ((), ()))


def _after_operand(after):
    return [] if after is None else [after]


def _fit(t, n):
    if n <= t:
        return n
    c = (t // LANES) * LANES
    while n % c:
        c -= LANES
    return c


MM_VMEM_BUDGET = VMEM_LIMIT * 3 // 4
def _mm_tiles(M, K, ns, n_total, out_dtypes, extras, whole_rows):
    for tm in (2048, 1024, 512, 256, 128):
        for tn in ((ns,) if whole_rows else (1024, 512, 256)):
            tn = _fit(tn, ns)
            per_row = 2 * (2 * K) + 4 * tn + sum(2 * tn * jnp.dtype(d).itemsize for d in out_dtypes)
            per_row += sum(2 * tn * e.dtype.itemsize for e in extras)
            b_buffers = 1 if tn == n_total else 2
            if M % tm == 0 and b_buffers * (2 * K * tn) + tm * per_row <= MM_VMEM_BUDGET:
                return tm, tn
    raise ValueError(f"no tile fits VMEM for M={M} K={K} N={ns}")


def _mm(a, b, *, mode, name, out_dtypes=(F32,), b_shard=False, o_shard=False, extras=(), epi=None, after=None,
        bcast=(), accs=(), ref_epi=None, out_cols=None):
    if mode == "tn":
        K, M = a.shape
        N = b.shape[1]
    else:
        M, K = a.shape
        if mode == "nn":
            N = b.shape[0] * b.shape[2] if b_shard else b.shape[1]
        else:
            N = b.shape[1] if b_shard else b.shape[0]
    ns = N
    if b_shard and mode == "nn":
        ns = b.shape[2]
    elif o_shard:
        ns = N // N_CHIPS
    tm, tn = _mm_tiles(M, K, ns, N, out_dtypes, extras, whole_rows=ref_epi is not None)
    if ref_epi is not None:
        tm = min(tm, 1024)
    grid = (M // tm, N // tn)
    q = ns // tn
    once = dict(pipeline_mode=pl.Buffered(1)) if tn == N else {}

    if mode == "tn":
        a_spec = pl.BlockSpec((K, tm), lambda i, j: (0, i))
        b_spec = pl.BlockSpec((K, tn), lambda i, j: (0, j), **once)
        dims = _TN
    elif mode == "nn":
        a_spec = pl.BlockSpec((tm, K), lambda i, j: (i, 0))
        if b_shard:
            b_spec = pl.BlockSpec((None, K, tn), lambda i, j: (j // q, 0, j % q), **once)
        else:
            b_spec = pl.BlockSpec((K, tn), lambda i, j: (0, j), **once)
        dims = _NN
    else:
        a_spec = pl.BlockSpec((tm, K), lambda i, j: (i, 0))
        if b_shard:
            ks = b.shape[2]
            b_spec = pl.BlockSpec((N_CHIPS, tn, ks), lambda i, j: (0, j, 0), **once)
        else:
            b_spec = pl.BlockSpec((tn, K), lambda i, j: (j, 0), **once)
        dims = _NT

    if o_shard:
        o_specs = [pl.BlockSpec((None, tm, tn), lambda i, j: (j // q, i, j % q))]
        o_shapes = [jax.ShapeDtypeStruct((N_CHIPS, M, ns), out_dtypes[0])]
    else:
        o_specs = [pl.BlockSpec((tm, tn), lambda i, j: (i, j)) for _ in out_dtypes]
        o_shapes = [jax.ShapeDtypeStruct((M, N if out_cols is None else out_cols[n]), dt)
                    for n, dt in enumerate(out_dtypes)]
    e_specs = [pl.BlockSpec((tm, tn), lambda i, j: (i, j)) for _ in extras]
    e_specs += [pl.BlockSpec(v.shape, lambda i, j: (0, 0)) for v in bcast]
    o_specs += [pl.BlockSpec(s, lambda i, j: (0, 0)) for s in accs]
    o_shapes += [jax.ShapeDtypeStruct(s, F32) for s in accs]
    n_e, n_b, n_o, n_a = len(extras), len(bcast), len(out_dtypes), len(accs)
    order = _after_operand(after)
    n_x = len(order)
    if epi is None:
        epi = lambda acc: (acc,)

    def body(a_ref, b_ref, *rest):
        e_refs, b_refs = rest[:n_e], rest[n_e:n_e + n_b]
        o_refs = rest[n_e + n_b + n_x:n_e + n_b + n_x + n_o]
        a_refs = rest[n_e + n_b + n_x + n_o:]
        if n_a:
            @pl.when((pl.program_id(0) == 0) & (pl.program_id(1) == 0))
            def _():
                for r in a_refs:
                    r[...] = jnp.zeros(r.shape, F32)
        if mode == "nt" and b_shard:
            acc = None
            for s in range(N_CHIPS):
                part = lax.dot_general(a_ref[:, s * ks:(s + 1) * ks], b_ref[s], dims, preferred_element_type=F32)
                acc = part if acc is None else acc + part
        else:
            acc = lax.dot_general(a_ref[...], b_ref[...], dims, preferred_element_type=F32)
        if ref_epi is not None:
            ref_epi(acc, e_refs, b_refs, o_refs, a_refs)
            return
        outs = epi(acc, *[r[...] for r in e_refs])
        for r, o in zip(o_refs, outs):
            r[...] = o.astype(r.dtype)

    outs = pl.pallas_call(
        body, name=name, grid=grid, in_specs=[a_spec, b_spec] + e_specs + [_ANY] * n_x, out_specs=o_specs,
        out_shape=o_shapes, compiler_params=_params(("arbitrary", "arbitrary") if n_a else ("parallel", "parallel")),
    )(a, b, *extras, *bcast, *order)
    return outs[0] if n_o + n_a == 1 else outs


def _rowwise(fn, rows, bcast, outs, accs=(), *, tm, name, after=None):
    def norm(r):
        return r if isinstance(r, tuple) else (r, r.shape[1], 0)

    rows = [norm(r) for r in rows]
    T = rows[0][0].shape[0]
    tm = min(tm, T)
    while T % tm:
        tm -= SUBLANES
    n_r, n_b, n_o, n_a = len(rows), len(bcast), len(outs), len(accs)
    order = _after_operand(after)
    n_x = len(order)
    in_specs = [pl.BlockSpec((tm, c), functools.partial(lambda i, cb: (i, cb), cb=cb)) for _, c, cb in rows]
    in_specs += [pl.BlockSpec(b.shape, lambda i: (0, 0)) for b in bcast] + [_ANY] * n_x
    out_specs = [pl.BlockSpec((tm, o[0]), lambda i: (i, 0)) for o in outs]
    out_specs += [pl.BlockSpec(s, lambda i: (0, 0)) for s in accs]
    out_shape = [jax.ShapeDtypeStruct((T, o[2] if len(o) > 2 else o[0]), o[1]) for o in outs]
    out_shape += [jax.ShapeDtypeStruct(s, F32) for s in accs]

    def body(*refs):
        in_refs = refs[:n_r]
        b_refs = refs[n_r:n_r + n_b]
        o_refs = refs[n_r + n_b + n_x:n_r + n_b + n_x + n_o]
        a_refs = refs[n_r + n_b + n_x + n_o:]
        if n_a:
            @pl.when(pl.program_id(0) == 0)
            def _():
                for r in a_refs:
                    r[...] = jnp.zeros(r.shape, F32)
        fn(in_refs, b_refs, o_refs, a_refs)

    res = pl.pallas_call(
        body, name=name, grid=(T // tm,), in_specs=in_specs, out_specs=out_specs, out_shape=out_shape,
        compiler_params=_params(("arbitrary",) if n_a else ("parallel",)),
    )(*[r[0] for r in rows], *bcast, *order)
    return res


def _rsum(x):
    return jnp.sum(x, axis=0, keepdims=True)


def _rms_fwd(x, g, name, after=None):
    def fn(ins, bs, outs, accs):
        xv = ins[0][...]
        r = lax.rsqrt(jnp.mean(xv * xv, axis=-1, keepdims=True) + EPS)
        outs[0][...] = (xv * r * bs[0][...]).astype(BF16)

    return _rowwise(fn, [x], [g], [(D_MODEL, BF16)], tm=1024, name=name, after=after)[0]


def _rms_bwd_math(xv, dh, g):
    r = lax.rsqrt(jnp.mean(xv * xv, axis=-1, keepdims=True) + EPS)
    hn = xv * r
    dgh = dh * g
    dx = r * (dgh - hn * jnp.mean(dgh * hn, axis=-1, keepdims=True))
    return dx, _rsum(dh * hn)


def _mm_norm_bwd(dy, w, x, dres, g, name, after=None):
    def epilogue(acc, e_refs, b_refs, o_refs, a_refs):
        dx, dg = _rms_bwd_math(e_refs[0][...], acc, b_refs[0][...])
        dx = dx + e_refs[1][...]
        o_refs[0][...] = dx
        o_refs[1][...] = dx.astype(BF16)
        a_refs[0][...] += dg

    return _mm(dy, w, mode="nt", b_shard=True, out_dtypes=(F32, BF16), extras=(x, dres), bcast=(g,),
               accs=((1, D_MODEL),), ref_epi=epilogue, name=name, after=after)


def _mm_res_norm(a, w, res, g, name):
    def epilogue(acc, e_refs, b_refs, o_refs, a_refs):
        xv = acc + e_refs[0][...]
        o_refs[0][...] = xv
        r = lax.rsqrt(jnp.mean(xv * xv, axis=-1, keepdims=True) + EPS)
        o_refs[1][...] = (xv * r * b_refs[0][...]).astype(BF16)

    return _mm(a, w, mode="nn", out_dtypes=(F32, BF16), extras=(res,), bcast=(g,), ref_epi=epilogue, name=name)


def _mm_final_loss(a, w, res, target, g, name):
    def epilogue(acc, e_refs, b_refs, o_refs, a_refs):
        xv = acc + e_refs[0][...]
        gv = b_refs[0][...]
        r = lax.rsqrt(jnp.mean(xv * xv, axis=-1, keepdims=True) + EPS)
        e = xv * r * gv - e_refs[1][...]
        tok = jnp.mean(e * e, axis=-1, keepdims=True)
        a_refs[0][...] += 0.5 * jnp.sum(tok, axis=0, keepdims=True) * jnp.ones((1, LANES), F32)
        dx, dg = _rms_bwd_math(xv, e * (1.0 / D_MODEL), gv)
        o_refs[0][...] = dx
        o_refs[1][...] = dx.astype(BF16)
        a_refs[1][...] += dg

    return _mm(a, w, mode="nn", out_dtypes=(F32, BF16), extras=(res, target), bcast=(g,),
               accs=((1, LANES), (1, D_MODEL)), ref_epi=epilogue, name=name)


def _relu2(acc):
    r = jnp.maximum(acc, 0.0)
    return r * r, r


def _mlp_fwd(x, h, fetch, tag, finish):
    w_up = fetch(f"mlp{tag}_up", h)
    a, r = _mm(h, w_up, mode="nn", b_shard=True, out_dtypes=(BF16, BF16), epi=_relu2, name=f"mlp{tag}_up")
    w_down = fetch(f"mlp{tag}_down", a)
    return finish(a, w_down, x, f"mlp{tag}_down"), (h, a, r, w_up, w_down)


def _mlp_bwd(x, g, saved, dx, dx_bf, tag, after):
    h, a, r, w_up, w_down = saved
    d_down = _mm(a, dx_bf, mode="tn", out_dtypes=(BF16,), name=f"mlp{tag}_dwdown", after=after)
    dup = _mm(dx_bf, w_down, mode="nt", extras=(r,), out_dtypes=(BF16,),
              epi=lambda acc, rv: (acc * (2.0 * rv.astype(F32)),), name=f"mlp{tag}_dup")
    d_up = _mm(h, dup, mode="tn", o_shard=True, out_dtypes=(BF16,), name=f"mlp{tag}_dwup")
    dx_new, dx_new_bf, dg = _mm_norm_bwd(dup, w_up, x, dx, g, f"mlp{tag}_dh")
    return dx_new, dx_new_bf, dg, d_up, d_down


def _rope_tables(L, B):
    rows = L // GRID_W
    row = np.repeat(np.arange(rows, dtype=np.float32), GRID_W)
    col = np.tile(np.arange(GRID_W, dtype=np.float32), rows)
    inv = (ROPE_THETA ** (-np.arange(HEAD_DIM // 4, dtype=np.float32) / (HEAD_DIM // 4))).astype(np.float32)
    ar, ac = row[:, None] * inv, col[:, None] * inv
    cos = np.concatenate([np.cos(ar), np.cos(ar), np.cos(ac), np.cos(ac)], axis=-1)
    sin = np.concatenate([-np.sin(ar), np.sin(ar), -np.sin(ac), np.sin(ac)], axis=-1)
    return jnp.asarray(np.tile(cos, (B, 1)), F32), jnp.asarray(np.tile(sin, (B, 1)), F32)


def _swap_halves(x):
    lane = lax.broadcasted_iota(jnp.int32, x.shape, 1)
    return jnp.where((lane % 64) < 32, pltpu.roll(x, HEAD_DIM - 32, 1), pltpu.roll(x, 32, 1))


def _qk_prep(qkv, cos, sin, q_g, k_g):
    def fn(ins, bs, outs, accs):
        c, s = ins[1][...], ins[2][...]
        for h in range(N_HEADS + N_KV):
            xv = ins[0][:, h * HEAD_DIM:(h + 1) * HEAD_DIM]
            g = bs[0][...] if h < N_HEADS else bs[1][...]
            r = lax.rsqrt(jnp.mean(xv * xv, axis=-1, keepdims=True) + EPS)
            z = xv * r * g
            y = (z * c + _swap_halves(z) * s).astype(BF16)
            if h < N_HEADS:
                outs[0][:, h * HEAD_DIM:(h + 1) * HEAD_DIM] = y
            else:
                outs[1][:, (h - N_HEADS) * HEAD_DIM:(h - N_HEADS + 1) * HEAD_DIM] = y
        outs[2][...] = ins[0][:, (N_HEADS + N_KV) * HEAD_DIM:].astype(BF16)

    kvw = N_KV * HEAD_DIM
    return _rowwise(fn, [qkv, cos, sin], [q_g, k_g], [(D_MODEL, BF16), (kvw, BF16), (kvw, BF16)], tm=1024,
                    name="attn_qk_prep")


def _qk_prep_bwd(qkv, dq, dk, dv, cos, sin, q_g, k_g):
    def fn(ins, bs, outs, accs):
        c, s = ins[4][...], ins[5][...]
        for h in range(N_HEADS + N_KV):
            sl = slice(h * HEAD_DIM, (h + 1) * HEAD_DIM)
            xv = ins[0][:, sl]
            if h < N_HEADS:
                g, dy, acc = bs[0][...], ins[1][:, sl], accs[0]
            else:
                ks = slice((h - N_HEADS) * HEAD_DIM, (h - N_HEADS + 1) * HEAD_DIM)
                g, dy, acc = bs[1][...], ins[2][:, ks], accs[1]
            r = lax.rsqrt(jnp.mean(xv * xv, axis=-1, keepdims=True) + EPS)
            xn = xv * r
            dz = dy * c - _swap_halves(dy) * s
            acc[...] += _rsum(dz * xn)
            dxn = dz * g
            outs[0][:, sl] = (r * (dxn - xn * jnp.mean(dxn * xn, axis=-1, keepdims=True))).astype(BF16)
        outs[0][:, (N_HEADS + N_KV) * HEAD_DIM:] = ins[3][...].astype(BF16)

    return _rowwise(fn, [qkv, dq, dk, dv, cos, sin], [q_g, k_g], [(qkv.shape[1], BF16)],
                    [(1, HEAD_DIM), (1, HEAD_DIM)], tm=512, name="attn_qk_prep_bwd")


_EXP2_SCALE = SCALE * math.log2(math.e)


def _exp_rows(q, k):
    s = lax.dot_general(q, k, _NT, preferred_element_type=F32)
    p = jnp.exp2((s - jnp.max(s, axis=-1, keepdims=True)) * _EXP2_SCALE)
    return p, jnp.sum(p, axis=-1, keepdims=True)


def _attn_fwd(q, k, v, B, L, tq=2048, sub=256):
    tq = min(tq, L)
    sub = min(sub, tq)
    nq = L // tq

    def body(q_ref, k_ref, v_ref, o_ref):
        kv, vv = k_ref[...], v_ref[...]
        for c in range(tq // sub):
            rows = slice(c * sub, (c + 1) * sub)
            p, l = _exp_rows(q_ref[rows, :], kv)
            o = jnp.dot(p.astype(BF16), vv, preferred_element_type=F32)
            o_ref[rows, :] = (o * (1.0 / l)).astype(o_ref.dtype)

    return pl.pallas_call(
        body, name="attn_fwd", grid=(B, N_HEADS, nq),
        in_specs=[pl.BlockSpec((tq, HEAD_DIM), lambda b, h, i: (b * nq + i, h)),
                  pl.BlockSpec((L, HEAD_DIM), lambda b, h, i: (b, h // GROUP)),
                  pl.BlockSpec((L, HEAD_DIM), lambda b, h, i: (b, h // GROUP))],
        out_specs=pl.BlockSpec((tq, HEAD_DIM), lambda b, h, i: (b * nq + i, h)),
        out_shape=jax.ShapeDtypeStruct((B * L, D_MODEL), BF16),
        compiler_params=_params(("parallel", "parallel", "parallel")),
    )(q, k, v)


def _attn_bwd(q, k, v, o, do, B, L, tq=2048, sub=512):
    tq = min(tq, L)
    sub = min(sub, tq)
    nq = L // tq

    def body(q_ref, k_ref, v_ref, o_ref, do_ref, dq_ref, dk_ref, dv_ref):
        @pl.when((pl.program_id(2) == 0) & (pl.program_id(3) == 0))
        def _():
            dk_ref[...] = jnp.zeros(dk_ref.shape, F32)
            dv_ref[...] = jnp.zeros(dv_ref.shape, F32)

        kv, vv = k_ref[...], v_ref[...]
        ps, es, dos, qs = [], [], [], []
        for c in range(tq // sub):
            rows = slice(c * sub, (c + 1) * sub)
            qc, doc = q_ref[rows, :], do_ref[rows, :]
            p, l = _exp_rows(qc, kv)
            inv = 1.0 / l
            dp = lax.dot_general(doc, vv, _NT, preferred_element_type=F32)
            delta = jnp.sum(doc.astype(F32) * o_ref[rows, :].astype(F32), axis=-1, keepdims=True)
            e = (p * (dp - delta)).astype(BF16)
            dq_ref[rows, :] = jnp.dot(e, kv, preferred_element_type=F32) * (inv * SCALE)
            ps.append(p.astype(BF16))
            es.append(e)
            dos.append((doc.astype(F32) * inv).astype(BF16))
            qs.append((qc.astype(F32) * (inv * SCALE)).astype(BF16))
        cat = lambda xs: xs[0] if len(xs) == 1 else jnp.concatenate(xs, axis=0)
        dv_ref[...] += lax.dot_general(cat(ps), cat(dos), _TN, preferred_element_type=F32)
        dk_ref[...] += lax.dot_general(cat(es), cat(qs), _TN, preferred_element_type=F32)

    qmap = lambda b, kh, g, i: (b * nq + i, kh * GROUP + g)
    kmap = lambda b, kh, g, i: (b, kh)
    kvw = N_KV * HEAD_DIM
    return pl.pallas_call(
        body, name="attn_bwd", grid=(B, N_KV, GROUP, nq),
        in_specs=[pl.BlockSpec((tq, HEAD_DIM), qmap), pl.BlockSpec((L, HEAD_DIM), kmap),
                  pl.BlockSpec((L, HEAD_DIM), kmap), pl.BlockSpec((tq, HEAD_DIM), qmap),
                  pl.BlockSpec((tq, HEAD_DIM), qmap)],
        out_specs=[pl.BlockSpec((tq, HEAD_DIM), qmap), pl.BlockSpec((L, HEAD_DIM), kmap),
                   pl.BlockSpec((L, HEAD_DIM), kmap)],
        out_shape=[jax.ShapeDtypeStruct((B * L, D_MODEL), F32), jax.ShapeDtypeStruct((B * L, kvw), F32),
                   jax.ShapeDtypeStruct((B * L, kvw), F32)],
        compiler_params=_params(("parallel", "parallel", "arbitrary", "arbitrary")),
    )(q, k, v, o, do)


def _conv_shift(x, t, L, k):
    if k == 2:
        return x
    if k < 2:
        return jnp.where(t >= 2 - k, pltpu.roll(x, 2 - k, 0), 0.0)
    return jnp.where(t < L - (k - 2), pltpu.roll(x, L - (k - 2), 0), 0.0)


def _conv_apply(x, w_ref, L):
    t = lax.broadcasted_iota(jnp.int32, x.shape, 0)
    acc = w_ref[4:5, :] + w_ref[2:3, :] * x
    for k in (0, 1, 3):
        acc = acc + w_ref[k:k + 1, :] * _conv_shift(x, t, L, k)
    return acc


def _conv_bwd(z, g, wb, dz, B, L, tc=512, after=None):
    noff = D_MODEL // tc
    order = _after_operand(after)

    def body(z_ref, g_ref, w_ref, dz_in, *rest):
        dx_ref, dw_ref = rest[len(order):]

        @pl.when(pl.program_id(1) == 0)
        def _():
            dw_ref[...] = jnp.zeros(dw_ref.shape, F32)

        x, gv = z_ref[...], g_ref[...]
        t = lax.broadcasted_iota(jnp.int32, x.shape, 0)
        dx = w_ref[2:3, :] * gv
        for k in (0, 1, 3):
            dx = dx + w_ref[k:k + 1, :] * _conv_shift(gv, t, L, 4 - k)
        dx_ref[...] = dx.astype(BF16)
        for k in range(4):
            dw_ref[k:k + 1, :] += _rsum(_conv_shift(x, t, L, k) * gv)
        dw_ref[4:5, :] += _rsum(gv)

    return pl.pallas_call(
        body, name="rg_conv_bwd", grid=(noff, B),
        in_specs=[pl.BlockSpec((L, tc), lambda j, b: (b, noff + j)), pl.BlockSpec((L, tc), lambda j, b: (b, j)),
                  pl.BlockSpec((SUBLANES, tc), lambda j, b: (0, j)), _ANY] + [_ANY] * len(order),
        out_specs=[pl.BlockSpec((L, tc), lambda j, b: (b, noff + j)),
                   pl.BlockSpec((SUBLANES, tc), lambda j, b: (0, j))],
        out_shape=[jax.ShapeDtypeStruct(dz.shape, dz.dtype), jax.ShapeDtypeStruct((SUBLANES, D_MODEL), F32)],
        input_output_aliases={3: 0},
        compiler_params=_params(("parallel", "arbitrary")),
    )(z, g, wb, dz, *order)


def _softplus(x):
    return jnp.maximum(x, 0.0) + jnp.log1p(jnp.exp(-jnp.abs(x)))


_ROW_BA, _ROW_BX, _ROW_LAM = 0, 2, 4


def _gate_math(xb, pre, vec_ref, d, sl):
    pa = pre[:, (2 * d) * LRU_BW:(2 * d + 1) * LRU_BW] + vec_ref[_ROW_BA + d:_ROW_BA + d + 1, sl]
    px = pre[:, (2 * d + 1) * LRU_BW:(2 * d + 2) * LRU_BW] + vec_ref[_ROW_BX + d:_ROW_BX + d + 1, sl]
    r = 0.5 * jnp.tanh(0.5 * pa) + 0.5
    i = 0.5 * jnp.tanh(0.5 * px) + 0.5
    slope = (-RG_C) * _softplus(-vec_ref[_ROW_LAM + d:_ROW_LAM + d + 1, sl])
    log_a = r * slope
    a = jnp.exp(log_a)
    om = -jnp.tanh(log_a) * (1.0 + a * a)
    rs = lax.rsqrt(om)
    mult = jnp.where(om > 0.0, om * rs, 0.0)
    return a, mult * (i * xb), (r, i, slope, om, mult, rs)


def _gate_bwd(rec, du_f, da_f, du_b, da_b, wcat, gvec):
    def fn(ins, bs, outs, accs):
        for blk in range(LRU_BLOCKS):
            sl = slice(blk * LRU_BW, (blk + 1) * LRU_BW)
            xb = ins[0][:, sl]
            xb16 = xb.astype(BF16)
            w = bs[0][sl, :]
            pre = jnp.dot(xb16, w, preferred_element_type=F32)
            dx = jnp.zeros_like(xb)
            dpre = []
            for d in range(2):
                a, _, (r, i, slope, om, mult, rs) = _gate_math(xb, pre, bs[1], d, sl)
                du, da = ins[1 + 2 * d][:, sl], ins[2 + 2 * d][:, sl]
                t = du * xb
                d_i = t * mult
                dx = dx + du * mult * i
                dlog = da * a - (t * i) * ((1.0 - om) * rs)
                d_r = dlog * slope
                d_sp = _rsum(dlog * r) * (-RG_C)
                lam = bs[1][_ROW_LAM + d:_ROW_LAM + d + 1, sl]
                accs[2][_ROW_LAM + d:_ROW_LAM + d + 1, sl] += d_sp * (-jax.nn.sigmoid(-lam))
                dpa = d_r * r * (1.0 - r)
                dpx = d_i * i * (1.0 - i)
                accs[2][_ROW_BA + d:_ROW_BA + d + 1, sl] += _rsum(dpa)
                accs[2][_ROW_BX + d:_ROW_BX + d + 1, sl] += _rsum(dpx)
                dpre += [dpa, dpx]
            dpre = jnp.concatenate(dpre, axis=1).astype(BF16)
            dw = lax.dot_general(xb16, dpre, _TN, preferred_element_type=F32)
            for d in range(2):
                rows = slice(d * D_MODEL + blk * LRU_BW, d * D_MODEL + (blk + 1) * LRU_BW)
                accs[0][rows, :] += dw[:, (2 * d) * LRU_BW:(2 * d + 1) * LRU_BW]
                accs[1][rows, :] += dw[:, (2 * d + 1) * LRU_BW:(2 * d + 2) * LRU_BW]
            outs[0][:, sl] = dx + lax.dot_general(dpre, w, _NT, preferred_element_type=F32)

    gate_shape = (2 * D_MODEL, LRU_BW)
    return _rowwise(fn, [rec, du_f, da_f, du_b, da_b], [wcat, gvec], [(D_MODEL, F32)],
                    [gate_shape, gate_shape, (SUBLANES, D_MODEL)], tm=512, name="rg_gate_bwd")


def _as_time_blocks(x):
    return x.reshape(x.shape[0] // SUBLANES, SUBLANES, x.shape[1])


def _scan_call(body, ins, n_out, B, L, tc, name):
    nb = L // SUBLANES
    spec = pl.BlockSpec((nb, SUBLANES, tc), lambda b, j: (b, 0, j))
    T = ins[0].shape[0]
    outs = pl.pallas_call(
        functools.partial(body, nb), name=name, grid=(B, D_MODEL // tc),
        in_specs=[spec] * len(ins), out_specs=[spec] * n_out,
        out_shape=[jax.ShapeDtypeStruct((T // SUBLANES, SUBLANES, D_MODEL), F32)] * n_out,
        compiler_params=_params(("parallel", "parallel")),
    )(*[_as_time_blocks(x) for x in ins])
    return [o.reshape(T, D_MODEL) for o in outs]


def _block_scan(A, U, reverse):
    row = lax.broadcasted_iota(jnp.int32, A.shape, 0)
    for s in (1, 2, 4):
        shift = SUBLANES - s if reverse else s
        valid = (row < SUBLANES - s) if reverse else (row >= s)
        a_sh = jnp.where(valid, pltpu.roll(A, shift, 0), 1.0)
        u_sh = jnp.where(valid, pltpu.roll(U, shift, 0), 0.0)
        U = A * u_sh + U
        A = A * a_sh
    return A, U


_LAST = SUBLANES - 1
SCAN_UNROLL = 8


def _loop_blocks(nb, step, init):
    def group(g, carry):
        for k in range(SCAN_UNROLL):
            carry = step(g * SCAN_UNROLL + k, carry)
        return carry

    return lax.fori_loop(0, nb // SCAN_UNROLL, group, init)


def _scan_bwd(dy, a_f, h_f, a_b, h_b, B, L, tc=256):
    def body(nb, dy_r, af, hf, ab, hb, duf, daf, dub, dab):
        def step(i, carry):
            c1, c2 = carry
            ir = nb - 1 - i
            row = lax.broadcasted_iota(jnp.int32, (SUBLANES, tc), 0)
            a_up = jnp.where(row == _LAST, af[jnp.minimum(ir + 1, nb - 1), :1, :], pltpu.roll(af[ir], _LAST, 0))
            p, lam = _block_scan(a_up, dy_r[ir], True)
            lam = lam + p * c1
            before = hf[jnp.maximum(ir - 1, 0), _LAST:, :] * (ir > 0).astype(F32)
            duf[ir] = lam
            daf[ir] = lam * jnp.where(row == 0, before, pltpu.roll(hf[ir], 1, 0))
            a_dn = jnp.where(row == 0, ab[jnp.maximum(i - 1, 0), _LAST:, :], pltpu.roll(ab[i], 1, 0))
            p2, lam2 = _block_scan(a_dn, dy_r[i], False)
            lam2 = lam2 + p2 * c2
            after = hb[jnp.minimum(i + 1, nb - 1), :1, :] * (i < nb - 1).astype(F32)
            dub[i] = lam2
            dab[i] = lam2 * jnp.where(row == _LAST, after, pltpu.roll(hb[i], _LAST, 0))
            return lam[:1, :], lam2[_LAST:, :]

        zero = jnp.zeros((1, tc), F32)
        _loop_blocks(nb, step, (zero, zero))

    return _scan_call(body, [dy, a_f, h_f, a_b, h_b], 4, B, L, tc, "rg_scan_bwd")


_GELU_C = math.sqrt(2.0 / math.pi)


def _gelu_parts(x):
    th = jnp.tanh(_GELU_C * (x + 0.044715 * x * x * x))
    return 0.5 * x * (1.0 + th), th


def _mm_gated_out_bwd(dx, w_out, h_f, h_b, z, name, after=None):
    def epilogue(acc, e_refs, b_refs, o_refs, a_refs):
        x = e_refs[2][...]
        gl, th = _gelu_parts(x)
        dgl = 0.5 * (1.0 + th) + 0.5 * x * (1.0 - th * th) * (_GELU_C * (1.0 + 3.0 * 0.044715 * x * x))
        o_refs[0][...] = acc * gl
        o_refs[1][...] = (acc * (e_refs[0][...] + e_refs[1][...]) * dgl).astype(BF16)

    return _mm(dx, w_out, mode="nt", out_dtypes=(F32, BF16), out_cols=(D_MODEL, 2 * D_MODEL), extras=(h_f, h_b, z),
               ref_epi=epilogue, name=name, after=after)


def _row_block(i):
    return pl.ds(pl.multiple_of(i * SUBLANES, SUBLANES), SUBLANES)


def _rg_mix_fwd(z, conv_wb, wcat, gvec, B, L):
    nb = L // SUBLANES
    n_g = D_MODEL // LRU_BW

    def body(zg_ref, zr_ref, cw_ref, w_ref, gv_ref, rec_ref, af_s, ab_s, hf_ref, hb_ref, yg_ref, uf_s, ub_s):
        rec = _conv_apply(zr_ref[...], cw_ref, L)
        rec_ref[...] = rec
        pre = jnp.dot(rec.astype(BF16), w_ref[...], preferred_element_type=F32)
        for d, (a_s, u_s) in enumerate(((af_s, uf_s), (ab_s, ub_s))):
            a, u, _ = _gate_math(rec, pre, gv_ref, d, slice(None))
            a_s[...] = a
            u_s[...] = u

        def step(i, carry):
            c1, c2 = carry
            rows, rows_b = _row_block(i), _row_block(nb - 1 - i)
            p, h = _block_scan(af_s[rows, :], uf_s[rows, :], False)
            h = h + p * c1
            hf_ref[rows, :] = h
            p2, h2 = _block_scan(ab_s[rows_b, :], ub_s[rows_b, :], True)
            h2 = h2 + p2 * c2
            hb_ref[rows_b, :] = h2
            return h[_LAST:, :], h2[:1, :]

        zero = jnp.zeros((1, LRU_BW), F32)
        _loop_blocks(nb, step, (zero, zero))
        gl, _ = _gelu_parts(zg_ref[...])
        yg_ref[...] = ((hf_ref[...] + hb_ref[...]) * gl).astype(BF16)

    seq = lambda off: pl.BlockSpec((L, LRU_BW), lambda b, g: (b, off + g))
    vec = pl.BlockSpec((SUBLANES, LRU_BW), lambda b, g: (0, g))
    T = B * L
    return pl.pallas_call(
        body, name="rg_mix", grid=(B, n_g),
        in_specs=[seq(0), seq(n_g), vec, pl.BlockSpec((LRU_BW, 4 * LRU_BW), lambda b, g: (g, 0)), vec],
        out_specs=[seq(0)] * 6,
        out_shape=[jax.ShapeDtypeStruct((T, D_MODEL), F32)] * 5 + [jax.ShapeDtypeStruct((T, D_MODEL), BF16)],
        scratch_shapes=[pltpu.VMEM((L, LRU_BW), F32)] * 2,
        compiler_params=_params(("parallel", "parallel")),
    )(z, z, conv_wb, wcat, gvec)


def _make_wcat(w_a, w_x):
    g = jnp.stack([w_a[0, 0], w_x[0, 0], w_a[0, 1], w_x[0, 1]])
    return jnp.transpose(g, (1, 2, 0, 3)).reshape(D_MODEL, 4 * LRU_BW)


def _rows_at(part, first):
    return jnp.pad(part, ((first, SUBLANES - first - part.shape[0]), (0, 0)))


def _qk_slot(q_g, k_g):
    wide = lambda v, at: jnp.pad(v, ((0, SUBLANES - 1), (at, D_MODEL - at - HEAD_DIM)))
    return wide(q_g, 0) + wide(k_g, HEAD_DIM)


def _local_step(x, target, P, fetch, emit, B, L, after=None):
    g_mix, g_mlp = P["norm_mix_g"], P["norm_mlp_g"]
    h0 = _rms_fwd(x, g_mix[0:1], "rg_norm", after=after)
    w_in, conv_wb, wcat, gvec = fetch("rg", h0)
    z = _mm(h0, w_in, mode="nn", b_shard=True, name="rg_in")
    rec, a_f, a_b, h_f, h_b, yg = _rg_mix_fwd(z, conv_wb, wcat, gvec, B, L)
    w_out = fetch("rg_out", yg)
    x1, h1 = _mm_res_norm(yg, w_out, x, g_mlp[0:1], "rg_out")
    (x2, h3), mlp0 = _mlp_fwd(x1, h1, fetch, 0, lambda a, w, res, name: _mm_res_norm(a, w, res, g_mix[1:2], name))
    w_qkv, w_o = fetch("att", h3)
    qkv = _mm(h3, w_qkv, mode="nn", b_shard=True, name="attn_qkv")
    cos, sin = _rope_tables(L, B)
    qh, kh, vh = _qk_prep(qkv, cos, sin, P["q_g"], P["k_g"])
    o = _attn_fwd(qh, kh, vh, B, L)
    x3, h4 = _mm_res_norm(o, w_o, x2, g_mlp[1:2], "attn_out")
    (dx4, dx4_bf, loss_acc, d_final_g), mlp1 = _mlp_fwd(
        x3, h4, fetch, 1, lambda a, w, res, name: _mm_final_loss(a, w, res, target, P["final_g"], name))

    dx3, dx3_bf, dg_mlp1, d_up1, d_down1 = _mlp_bwd(x3, g_mlp[1:2], mlp1, dx4, dx4_bf, 1, None)
    tok = emit("mlp1", [d_up1, d_down1])
    d_wo = _mm(o, dx3_bf, mode="tn", out_dtypes=(BF16,), name="attn_dwo", after=tok)
    do = _mm(dx3_bf, w_o, mode="nt", out_dtypes=(BF16,), name="attn_do")
    dq, dk, dv = _attn_bwd(qh, kh, vh, o, do, B, L)
    dqkv, dq_g, dk_g = _qk_prep_bwd(qkv, dq, dk, dv, cos, sin, P["q_g"], P["k_g"])
    d_wqkv = _mm(h3, dqkv, mode="tn", o_shard=True, out_dtypes=(BF16,), name="attn_dwqkv")
    tok = emit("att", [d_wqkv, d_wo])
    dx2, dx2_bf, dg_mix1 = _mm_norm_bwd(dqkv, w_qkv, x2, dx3, g_mix[1:2], "attn_dh", after=tok)
    tok = emit("point_attn_done", [dx2_bf])
    dx1, dx1_bf, dg_mlp0, d_up0, d_down0 = _mlp_bwd(x1, g_mlp[0:1], mlp0, dx2, dx2_bf, 0, tok)
    d_wout = _mm(yg, dx1_bf, mode="tn", out_dtypes=(BF16,), name="rg_dwout")
    tok = emit("mlp0", [d_up0, d_down0, d_wout])
    dy, dgate = _mm_gated_out_bwd(dx1_bf, w_out, h_f, h_b, z, "rg_dyg", after=tok)
    du_f, da_f, du_b, da_b = _scan_bwd(dy, a_f, h_f, a_b, h_b, B, L)
    drec_c, d_wa, d_wx, d_gvec = _gate_bwd(rec, du_f, da_f, du_b, da_b, wcat, gvec)
    tok = emit("gates", [d_wa, d_wx])
    dz, d_convwb = _conv_bwd(z, drec_c, conv_wb, dgate, B, L, after=tok)
    tok = emit("point_mix_done", [dz])
    d_win = _mm(h0, dz, mode="tn", o_shard=True, out_dtypes=(BF16,), name="rg_dwin", after=tok)
    tok = emit("rg_in", [d_win])
    grad_x, _, dg_mix0 = _mm_norm_bwd(dz, w_in, x, dx1, g_mix[0:1], "rg_dh", after=tok)

    norms = (_rows_at(dg_mix0, 0) + _rows_at(dg_mix1, 1) + _rows_at(dg_mlp0, 2) + _rows_at(dg_mlp1, 3)
             + _rows_at(d_final_g, 4)
             + jnp.pad(loss_acc, ((LOSS_ROW, SUBLANES - 1 - LOSS_ROW), (0, D_MODEL - LANES))))
    vec = jnp.concatenate([norms, d_convwb, d_gvec, _qk_slot(dq_g, dk_g)], axis=0)
    return grad_x, vec


_MESH = pl.DeviceIdType.MESH


def _place():
    x, y, c = lax.axis_index("x"), lax.axis_index("y"), lax.axis_index("c")
    peers = [((1 - x) if j & 2 else x, (1 - y) if j & 1 else y) for j in (1, 2, 3)]
    return x, y, c, peers


def _sum_leading(slots, name):
    def body(s_ref, o_ref):
        acc = s_ref[0]
        for d in range(1, slots.shape[0]):
            acc = acc + s_ref[d]
        o_ref[...] = acc

    return pl.pallas_call(body, name=name, out_shape=jax.ShapeDtypeStruct(slots.shape[1:], slots.dtype))(slots)


_HBM = pl.BlockSpec(memory_space=pltpu.HBM)
_SEM = pl.BlockSpec(memory_space=pltpu.SEMAPHORE)
_EFFECT = pltpu.SideEffectType.DATAFLOW_SIDE_EFFECTING


_COPIES = dict(gather=N_CHIPS - 1, scatter=N_CHIPS - 1, swap=1, spread=N_DEVICES - 1,
               gather_half=N_CHIPS - 1, share_half=N_CHIPS - 1)


def _split_copies(kind, srcs, lands, send, recv):
    x, y, c, peers = _place()
    me = 2 * x + y
    per = _COPIES[kind]
    out = []
    for a in range(len(lands)):
        for j in range(per):
            if kind == "swap":
                src, there, here, dev = srcs[a], lands[a], lands[a], (x, y, 1 - c)
            elif kind == "spread":
                k = j + 1
                dev = ((1 - x) if k & 4 else x, (1 - y) if k & 2 else y, (1 - c) if k & 1 else c)
                mine = lands[a].at[4 * x + 2 * y + c]
                src, there, here = mine, mine, lands[a].at[4 * dev[0] + 2 * dev[1] + dev[2]]
            else:
                px, py = peers[j]
                dev = (px, py, c)
                if kind == "gather":
                    src, there, here = lands[a].at[me], lands[a].at[me], lands[a].at[2 * px + py]
                elif kind in ("gather_half", "share_half"):
                    half = lands[a].shape[1] // 2
                    mine, other = pl.ds(c * half, half), pl.ds((1 - c) * half, half)
                    if kind == "gather_half":
                        src = there = lands[a].at[me, mine]
                        here = lands[a].at[2 * px + py, mine]
                    else:
                        src = there = lands[a].at[2 * px + py, mine]
                        here = lands[a].at[2 * px + py, other]
                        dev = (x, y, 1 - c)
                else:
                    src, there, here = srcs[a].at[2 * px + py], lands[a].at[j], lands[a].at[j]
            mk = functools.partial(
                pltpu.make_async_remote_copy, src_ref=src, send_sem=send.at[per * a + j],
                recv_sem=recv.at[per * a + j], device_id=dev, device_id_type=_MESH)
            out.append((functools.partial(mk, dst_ref=there), functools.partial(mk, dst_ref=here)))
    return out


def _exchange_start(kind, srcs, lands, name, after=None):
    arrays = list(srcs) + list(lands)
    n_s, n, n_all = len(srcs), len(lands), len(srcs) + len(lands)
    n_sem = _COPIES[kind] * n
    order = _after_operand(after)
    n_x = len(order)

    def body(*refs):
        send, recv = refs[n_all + n_x], refs[n_all + n_x + 1]
        token = refs[-1]
        for started, _ in _split_copies(kind, refs[:n_s], refs[n_s:n_all], send, recv):
            started().start()
        token[...] = jnp.zeros(token.shape, F32)

    res = pl.pallas_call(
        body, name=name,
        out_shape=(pltpu.SemaphoreType.DMA((n_sem,)), pltpu.SemaphoreType.DMA((n_sem,)),
                   *[pltpu.HBM(a.shape, a.dtype) for a in arrays], jax.ShapeDtypeStruct((SUBLANES, LANES), F32)),
        in_specs=[_HBM] * n_all + [_ANY] * n_x,
        out_specs=(_SEM, _SEM, *[_HBM] * n_all, pl.BlockSpec(memory_space=pltpu.VMEM)),
        input_output_aliases={i: 2 + i for i in range(n_all)},
        compiler_params=pltpu.CompilerParams(has_side_effects=_EFFECT),
    )(*[pltpu.with_memory_space_constraint(a, pltpu.HBM) for a in arrays], *order)
    return (res[0], res[1], res[2:2 + n_s], res[2 + n_s:2 + n_all]), res[-1]


def _gather_start_groups(land_groups, name, after=None, kind="gather"):
    arrays = [a for group in land_groups for a in group]
    n_all, n_g = len(arrays), len(land_groups)
    order = _after_operand(after)
    n_x = len(order)

    def body(*refs):
        first = 0
        for gi, group in enumerate(land_groups):
            send, recv = refs[n_all + n_x + 2 * gi], refs[n_all + n_x + 2 * gi + 1]
            for started, _ in _split_copies(kind, [], refs[first:first + len(group)], send, recv):
                started().start()
            first += len(group)
        refs[-1][...] = jnp.zeros(refs[-1].shape, F32)

    sems = [pltpu.SemaphoreType.DMA((_COPIES[kind] * len(group),)) for group in land_groups for _ in range(2)]
    res = pl.pallas_call(
        body, name=name,
        out_shape=(*sems, *[pltpu.HBM(a.shape, a.dtype) for a in arrays], jax.ShapeDtypeStruct((SUBLANES, LANES), F32)),
        in_specs=[_HBM] * n_all + [_ANY] * n_x,
        out_specs=(*[_SEM] * (2 * n_g), *[_HBM] * n_all, pl.BlockSpec(memory_space=pltpu.VMEM)),
        input_output_aliases={i: 2 * n_g + i for i in range(n_all)},
        compiler_params=pltpu.CompilerParams(has_side_effects=_EFFECT),
    )(*[pltpu.with_memory_space_constraint(a, pltpu.HBM) for a in arrays], *order)
    handles, first = [], 2 * n_g
    for gi, group in enumerate(land_groups):
        handles.append((res[2 * gi], res[2 * gi + 1], [], res[first:first + len(group)]))
        first += len(group)
    return handles, res[-1]


def _exchange_wait(kind, handle, after, name):
    send, recv, srcs, lands = handle
    arrays = list(srcs) + list(lands)
    n_s, n_all = len(srcs), len(arrays)
    order = list(after) if isinstance(after, (list, tuple)) else [after]

    def body(*refs):
        for started, landing in _split_copies(kind, refs[:n_s], refs[n_s:n_all], refs[n_all], refs[n_all + 1]):
            started().wait_send()
            landing().wait_recv()

    res = pl.pallas_call(
        body, name=name, out_shape=[pltpu.HBM(a.shape, a.dtype) for a in arrays],
        in_specs=[_HBM] * n_all + [_SEM, _SEM] + [_ANY] * len(order), out_specs=[_HBM] * n_all,
        input_output_aliases={i: i for i in range(n_all)},
        compiler_params=pltpu.CompilerParams(has_side_effects=_EFFECT),
    )(*arrays, send, recv, *order)
    return res[:n_s], res[n_s:]


def _index_operand(i):
    return jnp.reshape(i, (1,)).astype(jnp.int32)


def _cast_into_slot(src, row0, rows, me, dtype, name, after=None, add=None, n_slots=N_CHIPS):
    cols = src.shape[1]
    tm = min(512, rows)
    order = _after_operand(after)
    terms = [src] + ([] if add is None else [add])

    def body(me_ref, *rest):
        val = rest[0][...]
        if add is not None:
            val = val + rest[1][...]
        rest[-1][...] = val.astype(dtype)

    return pl.pallas_call(
        body, name=name,
        grid_spec=pltpu.PrefetchScalarGridSpec(
            num_scalar_prefetch=1, grid=(rows // tm,),
            in_specs=[pl.BlockSpec((tm, cols), lambda i, me_ref: (i + row0 // tm, 0))] * len(terms)
            + [_ANY] * len(order),
            out_specs=pl.BlockSpec((None, tm, cols), lambda i, me_ref: (me_ref[0], i, 0))),
        out_shape=jax.ShapeDtypeStruct((n_slots, rows, cols), dtype), compiler_params=_params(("parallel",)),
    )(_index_operand(me), *terms, *order)


def _sum_slots(mine, r, me, name):
    _, rows, cols = r.shape
    tm = min(512, rows)

    def body(me_ref, own_ref, r_ref, o_ref):
        o_ref[...] = ((own_ref[...].astype(F32) + r_ref[0].astype(F32)) + r_ref[1].astype(F32)) + r_ref[2].astype(F32)

    return pl.pallas_call(
        body, name=name,
        grid_spec=pltpu.PrefetchScalarGridSpec(
            num_scalar_prefetch=1, grid=(rows // tm,),
            in_specs=[pl.BlockSpec((None, tm, cols), lambda i, me_ref: (me_ref[0], i, 0)),
                      pl.BlockSpec((N_CHIPS - 1, tm, cols), lambda i, me_ref: (0, i, 0))],
            out_specs=pl.BlockSpec((tm, cols), lambda i, me_ref: (i, 0))),
        out_shape=jax.ShapeDtypeStruct((rows, cols), F32), compiler_params=_params(("parallel",)),
    )(_index_operand(me), mine, r)


def _adamw(w, m, v, ps, qs, name):
    rows, cols = w.shape
    seg_rows = ps[0].shape[0]
    tm = min(512, seg_rows)
    while seg_rows % tm:
        tm -= SUBLANES
    per, n_seg = seg_rows // tm, len(ps)
    parts = list(ps) + ([] if qs is None else list(qs))

    def body(w_ref, m_ref, v_ref, *rest):
        g_refs, outs = rest[:len(parts)], rest[len(parts):]
        grad = lambda s: g_refs[s][...] if qs is None else g_refs[s][...] + g_refs[n_seg + s][...]
        g = grad(0)
        for s in range(1, n_seg):
            g = jnp.where(pl.program_id(0) >= s * per, grad(s), g)
        m1 = ADAM_B1 * m_ref[...] + (1.0 - ADAM_B1) * g
        v1 = ADAM_B2 * v_ref[...] + (1.0 - ADAM_B2) * (g * g)
        m_hat = m1 / (1.0 - ADAM_B1 ** ADAM_STEP)
        v_hat = v1 / (1.0 - ADAM_B2 ** ADAM_STEP)
        outs[0][...] = g
        outs[1][...] = (-ADAM_LR) * (m_hat / (jnp.sqrt(v_hat) + ADAM_EPS) + ADAM_WD * w_ref[...])
        outs[2][...] = m1
        outs[3][...] = v1

    row_spec = pl.BlockSpec((tm, cols), lambda i: (i, 0))
    seg_spec = lambda s: pl.BlockSpec((tm, cols), lambda i: (jnp.clip(i - s * per, 0, per - 1), 0))
    return pl.pallas_call(
        body, name=name, grid=(rows // tm,),
        in_specs=[row_spec] * 3 + [seg_spec(s) for s in range(n_seg)] * (1 if qs is None else 2),
        out_specs=[row_spec] * 4, out_shape=[jax.ShapeDtypeStruct((rows, cols), F32)] * 4,
        compiler_params=_params(("arbitrary",)),
    )(w, m, v, *parts)


def _put_cols(shard, me):
    full = jnp.zeros((shard.shape[0], D_MODEL), F32)
    return lax.dynamic_update_slice(full, shard, (0, me * (D_MODEL // N_CHIPS)))


def _gate_vec_slot(b_a, b_x, lam):
    return _rows_at(b_a, _ROW_BA) + _rows_at(b_x, _ROW_BX) + _rows_at(lam, _ROW_LAM)


def _pack_vec(p, me):
    return jnp.concatenate([
        _rows_at(p["norm_mix_g"], 0) + _rows_at(p["norm_mlp_g"], 2) + _rows_at(p["final_g"][None], 4),
        _rows_at(_put_cols(p["rg_conv_w"][0, :, 0, :], me), 0) + _rows_at(p["rg_conv_b"], 4),
        _gate_vec_slot(_put_cols(p["rg_b_a"][0], me), _put_cols(p["rg_b_x"][0], me), _put_cols(p["rg_lam"][0], me)),
        _qk_slot(p["at_q_g"], p["at_k_g"]),
    ], axis=0)


def _unpack_vec(r, me):
    def cols(rows):
        return lax.dynamic_slice(rows, (0, me * (D_MODEL // N_CHIPS)), (rows.shape[0], D_MODEL // N_CHIPS))

    gate = r[16:24]
    return dict(
        norm_mix_g=r[0:2], norm_mlp_g=r[2:4], final_g=r[4], rg_conv_w=cols(r[8:12])[None, :, None, :],
        rg_conv_b=r[12:13], rg_b_a=cols(gate[_ROW_BA:_ROW_BA + 2])[None], rg_b_x=cols(gate[_ROW_BX:_ROW_BX + 2])[None],
        rg_lam=cols(gate[_ROW_LAM:_ROW_LAM + 2])[None], at_q_g=r[24:25, 0:HEAD_DIM],
        at_k_g=r[24:25, HEAD_DIM:2 * HEAD_DIM])


_WEIGHTS = ['norm_mix_g', 'norm_mlp_g', 'rg_w_in', 'rg_conv_w', 'rg_conv_b', 'rg_w_a', 'rg_b_a', 'rg_w_x', 'rg_b_x',
            'rg_lam', 'rg_w_out', 'at_w_qkv', 'at_q_g', 'at_k_g', 'at_w_o', 'mlp_w_up', 'mlp_w_down', 'final_g']
_BIG = dict(rg_w_in=["rg_w_in"], rg_w_out=["rg_w_out"], at_w_qkv=["at_w_qkv"], at_w_o=["at_w_o"],
            mlp_w_up=["up0", "up1"], mlp_w_down=["down0", "down1"])


def kernel(x, *args):
    n_w = len(_WEIGHTS)
    w = dict(zip(_WEIGHTS, args[:n_w]))
    target = args[n_w]
    m = dict(zip(_WEIGHTS, args[n_w + 1:2 * n_w + 1]))
    v = dict(zip(_WEIGHTS, args[2 * n_w + 1:3 * n_w + 1]))
    B, L, _ = x.shape
    T = B * L
    me = 2 * lax.axis_index("x") + lax.axis_index("y")

    vec = jnp.concatenate([_gate_vec_slot(w["rg_b_a"][0], w["rg_b_x"][0], w["rg_lam"][0]),
                           _rows_at(w["rg_conv_w"][0, :, 0, :], 0)], axis=0)
    flat = lambda a: a.reshape(-1, a.shape[-1])
    rows_of = lambda k: w[k].shape[-2]
    groups = [("rg", [("rg_w_in", 0, BF16), (vec, 0, F32)]), ("rg_out", [("rg_w_out", 0, BF16)]),
              ("mlp0_up", [("mlp_w_up", 0, BF16)]), ("mlp0_down", [("mlp_w_down", 0, BF16)]),
              ("att", [("at_w_qkv", 0, BF16), ("at_w_o", 0, BF16)]),
              ("mlp1", [("mlp_w_up", 1, BF16), ("mlp_w_down", 1, BF16)])]

    def landing_zones(group, members, after):
        lands = []
        for n, (k, layer, dtype) in enumerate(members):
            src, rows = (flat(w[k]), rows_of(k)) if isinstance(k, str) else (k, k.shape[0])
            lands.append(_cast_into_slot(src, layer * rows, rows, me, dtype, f"place_{group}{n}", after=after))
        return lands

    halves, gathers = {}, {}
    halves["rg"], tok = _exchange_start("gather_half", [], landing_zones(*groups[0], None), "gather_rg_start")
    handles, tok = _gather_start_groups([landing_zones(g, members, tok) for g, members in groups[1:]],
                                        "gather_rest_start", after=tok, kind="gather_half")
    halves.update(zip([g for g, _ in groups[1:]], handles))
    wcat = _make_wcat(w["rg_w_a"], w["rg_w_x"]).astype(BF16)

    packs = [_pack_vec(p, me) for p in (w, m, v)]

    ready = {}

    def share(some, after, name):
        landed = [_exchange_wait("gather_half", halves[g], after, f"gather_{g}_landed")[1] for g in some]
        handles, _ = _gather_start_groups(landed, name, kind="share_half")
        gathers.update(zip(some, handles))

    def fetch(what, after):
        if what in ready:
            return ready[what]
        group = "mlp1" if what.startswith("mlp1") else what
        if group == "rg":
            share(["rg"], [after, wcat] + packs, "share_rg_start")
        elif group == "rg_out":
            share(["rg_out", "mlp0_up", "mlp0_down", "att"], after, "share_early_start")
        _, full = _exchange_wait("share_half", gathers[group], after, f"gather_{group}_wait")
        if group == "att":
            share(["mlp1"], after, "share_mlp1_start")
        if group == "rg":
            vec_full = jnp.transpose(full[1], (1, 0, 2)).reshape(2 * SUBLANES, D_MODEL)
            conv_wb = vec_full[SUBLANES:] + _rows_at(w["rg_conv_b"], 4)
            return full[0], conv_wb, wcat, vec_full[:SUBLANES]
        if group == "rg_out":
            return full[0].reshape(D_MODEL, D_MODEL)
        if group == "att":
            return full[0], full[1].reshape(D_MODEL, D_MODEL)
        if group == "mlp1":
            ready["mlp1_up"], ready["mlp1_down"] = full[0], full[1].reshape(4 * D_MODEL, D_MODEL)
            return ready[what]
        return full[0] if group == "mlp0_up" else full[0].reshape(4 * D_MODEL, D_MODEL)

    names = dict(mlp1=["up1", "down1"], att=["at_w_qkv", "at_w_o"], mlp0=["up0", "down0", "rg_w_out"],
                 rg_in=["rg_w_in"], gates=["rg_w_a", "rg_w_x"])
    scatters, swaps, P, Q, res = {}, [], {}, {}, {}

    def start_scatter(group, grads):
        srcs = [g.reshape(N_CHIPS, -1, g.shape[-1]) for g in grads]
        lands = [lax.empty((N_CHIPS - 1,) + s.shape[1:], s.dtype) for s in srcs]
        scatters[group], token = _exchange_start("scatter", srcs, lands, f"scatter_{group}_start")
        return token

    def settle(groups, after):
        keys, parts = [], []
        for group in groups:
            srcs, lands = _exchange_wait("scatter", scatters[group], after, f"scatter_{group}_wait")
            for k, s, r in zip(names[group], srcs, lands):
                keys.append(k)
                parts.append(_sum_slots(s, r, me, f"sum_{k}"))
        handle, token = _exchange_start("swap", parts, [lax.empty(p.shape, F32) for p in parts],
                                        f"swap_{groups[0]}_start")
        swaps.append((keys, handle, f"swap_{groups[0]}_wait"))
        return token

    def finish(after):
        for keys, handle, name in swaps:
            mine, theirs = _exchange_wait("swap", handle, after, name)
            P.update(zip(keys, mine))
            Q.update(zip(keys, theirs))
        swaps.clear()
        last = after
        for k, parts in _BIG.items():
            if k in res or any(p not in P for p in parts):
                continue
            shape = w[k].shape
            two_d = lambda a: a.reshape(-1, shape[-1])
            outs = _adamw(two_d(w[k]), two_d(m[k]), two_d(v[k]), [P[p] for p in parts], [Q[p] for p in parts],
                          f"adamw_{k}")
            res[k] = [o.reshape(shape) for o in outs]
            last = outs[0]
        if "rg_w_a" in P and "gates" not in gathers:
            lands = [_cast_into_slot(P[k], 0, P[k].shape[0], me, F32, f"place_{k}", after=last, add=Q[k])
                     for k in names["gates"]]
            gathers["gates"], last = _exchange_start("gather", [], lands, "gather_gates_start", after=last)
        return last

    def emit(event, arrays):
        if event == "point_attn_done":
            return None
        if event == "point_mix_done":
            return settle(["mlp1", "att", "mlp0"], arrays[0])
        token = start_scatter(event, arrays)
        if event == "rg_in":
            return finish(settle(["gates"], token))
        return token

    P_vec = dict(norm_mix_g=w["norm_mix_g"], norm_mlp_g=w["norm_mlp_g"], final_g=w["final_g"][None],
                 q_g=w["at_q_g"], k_g=w["at_k_g"])
    grad_x, vec_part = _local_step(x.reshape(T, D_MODEL), target.reshape(T, D_MODEL), P_vec, fetch, emit, B, L,
                                   after=tok)

    me8 = 2 * me + lax.axis_index("c")
    vec_slots = _cast_into_slot(vec_part, 0, VEC_ROWS, me8, F32, "place_vec", n_slots=N_DEVICES)
    spread, tok = _exchange_start("spread", [], [vec_slots], "spread_vec_start")
    last = finish(settle(["rg_in"], tok))
    _, gate_grads = _exchange_wait("gather", gathers["gates"], last, "gather_gates_wait")
    for k, g in zip(names["gates"], gate_grads):
        two_d = lambda a: a.reshape(g.shape[0] * g.shape[1], g.shape[2])
        outs = _adamw(two_d(w[k]), two_d(m[k]), two_d(v[k]), [two_d(g)], None, f"adamw_{k}")
        res[k] = [o.reshape(w[k].shape) for o in outs]
        last = outs[0]
    _, (vec_all,) = _exchange_wait("spread", spread, last, "spread_vec_wait")
    vec_grad = _sum_leading(vec_all, "sum_vec")
    loss = vec_grad[LOSS_ROW, 0]
    outs = _adamw(*packs, [vec_grad], None, "adamw_vec")
    unpacked = [_unpack_vec(o, me) for o in outs]
    for k in _WEIGHTS:
        if k not in res:
            res[k] = [u[k] for u in unpacked]

    result = [loss, grad_x.reshape(B, L, D_MODEL)]
    for slot in range(4):
        result += [res[k][slot] for k in _WEIGHTS]
    return tuple(result)
```

```python
import functools
import math

import jax
import jax.numpy as jnp
import numpy as np
from jax import lax
from jax.experimental import pallas as pl
from jax.experimental.pallas import tpu as pltpu

F32 = jnp.float32
BF16 = jnp.bfloat16

D_MODEL = 1024
HEAD_DIM = 128
N_HEADS = 8
N_KV = 2
GROUP = N_HEADS // N_KV
LRU_BLOCKS = 8
LRU_BW = 128
GRID_W = 64
ROPE_THETA = 10000.0
EPS = 1e-6
RG_C = 8.0
SCALE = 1.0 / math.sqrt(HEAD_DIM)
N_CHIPS = 4

ADAM_LR = 0.001
ADAM_B1 = 0.9
ADAM_B2 = 0.999
ADAM_EPS = 1e-08
ADAM_WD = 0.01
ADAM_STEP = 10

V7X_VMEM_BYTES = 64 * 1024 * 1024
VMEM_LIMIT = V7X_VMEM_BYTES * 3 // 4
LANES = 128
SUBLANES = 8

N_DEVICES = 8
VEC_ROWS = 32
LOSS_ROW = 5


def _params(sem):
    return pltpu.CompilerParams(dimension_semantics=sem, vmem_limit_bytes=VMEM_LIMIT)


_ANY = pl.BlockSpec(memory_space=pl.ANY)
_NN = (((1,), (0,)), ((), ()))
_NT = (((1,), (1,)), ((), ()))
_TN = (((0,), (0,)), ((), ()))


def _after_operand(after):
    return [] if after is None else [after]


def _fit(t, n):
    if n <= t:
        return n
    c = (t // LANES) * LANES
    while n % c:
        c -= LANES
    return c


MM_VMEM_BUDGET = VMEM_LIMIT * 3 // 4
def _mm_tiles(M, K, ns, n_total, out_dtypes, extras, whole_rows):
    for tm in (2048, 1024, 512, 256, 128):
        for tn in ((ns,) if whole_rows else (1024, 512, 256)):
            tn = _fit(tn, ns)
            per_row = 2 * (2 * K) + 4 * tn + sum(2 * tn * jnp.dtype(d).itemsize for d in out_dtypes)
            per_row += sum(2 * tn * e.dtype.itemsize for e in extras)
            b_buffers = 1 if tn == n_total else 2
            if M % tm == 0 and b_buffers * (2 * K * tn) + tm * per_row <= MM_VMEM_BUDGET:
                return tm, tn
    raise ValueError(f"no tile fits VMEM for M={M} K={K} N={ns}")


def _mm(a, b, *, mode, name, out_dtypes=(F32,), b_shard=False, o_shard=False, extras=(), epi=None, after=None,
        bcast=(), accs=(), ref_epi=None, out_cols=None):
    if mode == "tn":
        K, M = a.shape
        N = b.shape[1]
    else:
        M, K = a.shape
        if mode == "nn":
            N = b.shape[0] * b.shape[2] if b_shard else b.shape[1]
        else:
            N = b.shape[1] if b_shard else b.shape[0]
    ns = N
    if b_shard and mode == "nn":
        ns = b.shape[2]
    elif o_shard:
        ns = N // N_CHIPS
    tm, tn = _mm_tiles(M, K, ns, N, out_dtypes, extras, whole_rows=ref_epi is not None)
    if ref_epi is not None:
        tm = min(tm, 512)
    grid = (M // tm, N // tn)
    q = ns // tn
    once = dict(pipeline_mode=pl.Buffered(1)) if tn == N else {}

    if mode == "tn":
        a_spec = pl.BlockSpec((K, tm), lambda i, j: (0, i))
        b_spec = pl.BlockSpec((K, tn), lambda i, j: (0, j), **once)
        dims = _TN
    elif mode == "nn":
        a_spec = pl.BlockSpec((tm, K), lambda i, j: (i, 0))
        if b_shard:
            b_spec = pl.BlockSpec((None, K, tn), lambda i, j: (j // q, 0, j % q), **once)
        else:
            b_spec = pl.BlockSpec((K, tn), lambda i, j: (0, j), **once)
        dims = _NN
    else:
        a_spec = pl.BlockSpec((tm, K), lambda i, j: (i, 0))
        if b_shard:
            ks = b.shape[2]
            b_spec = pl.BlockSpec((N_CHIPS, tn, ks), lambda i, j: (0, j, 0), **once)
        else:
            b_spec = pl.BlockSpec((tn, K), lambda i, j: (j, 0), **once)
        dims = _NT

    if o_shard:
        o_specs = [pl.BlockSpec((None, tm, tn), lambda i, j: (j // q, i, j % q))]
        o_shapes = [jax.ShapeDtypeStruct((N_CHIPS, M, ns), out_dtypes[0])]
    else:
        o_specs = [pl.BlockSpec((tm, tn), lambda i, j: (i, j)) for _ in out_dtypes]
        o_shapes = [jax.ShapeDtypeStruct((M, N if out_cols is None else out_cols[n]), dt)
                    for n, dt in enumerate(out_dtypes)]
    e_specs = [pl.BlockSpec((tm, tn), lambda i, j: (i, j)) for _ in extras]
    e_specs += [pl.BlockSpec(v.shape, lambda i, j: (0, 0)) for v in bcast]
    o_specs += [pl.BlockSpec(s, lambda i, j: (0, 0)) for s in accs]
    o_shapes += [jax.ShapeDtypeStruct(s, F32) for s in accs]
    n_e, n_b, n_o, n_a = len(extras), len(bcast), len(out_dtypes), len(accs)
    order = _after_operand(after)
    n_x = len(order)
    if epi is None:
        epi = lambda acc: (acc,)

    def body(a_ref, b_ref, *rest):
        e_refs, b_refs = rest[:n_e], rest[n_e:n_e + n_b]
        o_refs = rest[n_e + n_b + n_x:n_e + n_b + n_x + n_o]
        a_refs = rest[n_e + n_b + n_x + n_o:]
        if n_a:
            @pl.when((pl.program_id(0) == 0) & (pl.program_id(1) == 0))
            def _():
                for r in a_refs:
                    r[...] = jnp.zeros(r.shape, F32)
        if mode == "nt" and b_shard:
            acc = None
            for s in range(N_CHIPS):
                part = lax.dot_general(a_ref[:, s * ks:(s + 1) * ks], b_ref[s], dims, preferred_element_type=F32)
                acc = part if acc is None else acc + part
        else:
            acc = lax.dot_general(a_ref[...], b_ref[...], dims, preferred_element_type=F32)
        if ref_epi is not None:
            ref_epi(acc, e_refs, b_refs, o_refs, a_refs)
            return
        outs = epi(acc, *[r[...] for r in e_refs])
        for r, o in zip(o_refs, outs):
            r[...] = o.astype(r.dtype)

    outs = pl.pallas_call(
        body, name=name, grid=grid, in_specs=[a_spec, b_spec] + e_specs + [_ANY] * n_x, out_specs=o_specs,
        out_shape=o_shapes, compiler_params=_params(("arbitrary", "arbitrary") if n_a else ("parallel", "parallel")),
    )(a, b, *extras, *bcast, *order)
    return outs[0] if n_o + n_a == 1 else outs


def _rowwise(fn, rows, bcast, outs, accs=(), *, tm, name, after=None):
    def norm(r):
        return r if isinstance(r, tuple) else (r, r.shape[1], 0)

    rows = [norm(r) for r in rows]
    T = rows[0][0].shape[0]
    tm = min(tm, T)
    while T % tm:
        tm -= SUBLANES
    n_r, n_b, n_o, n_a = len(rows), len(bcast), len(outs), len(accs)
    order = _after_operand(after)
    n_x = len(order)
    in_specs = [pl.BlockSpec((tm, c), functools.partial(lambda i, cb: (i, cb), cb=cb)) for _, c, cb in rows]
    in_specs += [pl.BlockSpec(b.shape, lambda i: (0, 0)) for b in bcast] + [_ANY] * n_x
    out_specs = [pl.BlockSpec((tm, o[0]), lambda i: (i, 0)) for o in outs]
    out_specs += [pl.BlockSpec(s, lambda i: (0, 0)) for s in accs]
    out_shape = [jax.ShapeDtypeStruct((T, o[2] if len(o) > 2 else o[0]), o[1]) for o in outs]
    out_shape += [jax.ShapeDtypeStruct(s, F32) for s in accs]

    def body(*refs):
        in_refs = refs[:n_r]
        b_refs = refs[n_r:n_r + n_b]
        o_refs = refs[n_r + n_b + n_x:n_r + n_b + n_x + n_o]
        a_refs = refs[n_r + n_b + n_x + n_o:]
        if n_a:
            @pl.when(pl.program_id(0) == 0)
            def _():
                for r in a_refs:
                    r[...] = jnp.zeros(r.shape, F32)
        fn(in_refs, b_refs, o_refs, a_refs)

    res = pl.pallas_call(
        body, name=name, grid=(T // tm,), in_specs=in_specs, out_specs=out_specs, out_shape=out_shape,
        compiler_params=_params(("arbitrary",) if n_a else ("parallel",)),
    )(*[r[0] for r in rows], *bcast, *order)
    return res


def _rsum(x):
    return jnp.sum(x, axis=0, keepdims=True)


def _rms_fwd(x, g, name, after=None):
    def fn(ins, bs, outs, accs):
        xv = ins[0][...]
        r = lax.rsqrt(jnp.mean(xv * xv, axis=-1, keepdims=True) + EPS)
        outs[0][...] = (xv * r * bs[0][...]).astype(BF16)

    return _rowwise(fn, [x], [g], [(D_MODEL, BF16)], tm=1024, name=name, after=after)[0]


def _rms_bwd_math(xv, dh, g):
    r = lax.rsqrt(jnp.mean(xv * xv, axis=-1, keepdims=True) + EPS)
    hn = xv * r
    dgh = dh * g
    dx = r * (dgh - hn * jnp.mean(dgh * hn, axis=-1, keepdims=True))
    return dx, _rsum(dh * hn)


def _mm_norm_bwd(dy, w, x, dres, g, name, after=None):
    def epilogue(acc, e_refs, b_refs, o_refs, a_refs):
        dx, dg = _rms_bwd_math(e_refs[0][...], acc, b_refs[0][...])
        dx = dx + e_refs[1][...]
        o_refs[0][...] = dx
        o_refs[1][...] = dx.astype(BF16)
        a_refs[0][...] += dg

    return _mm(dy, w, mode="nt", b_shard=True, out_dtypes=(F32, BF16), extras=(x, dres), bcast=(g,),
               accs=((1, D_MODEL),), ref_epi=epilogue, name=name, after=after)


def _mm_res_norm(a, w, res, g, name):
    def epilogue(acc, e_refs, b_refs, o_refs, a_refs):
        xv = acc + e_refs[0][...]
        o_refs[0][...] = xv
        r = lax.rsqrt(jnp.mean(xv * xv, axis=-1, keepdims=True) + EPS)
        o_refs[1][...] = (xv * r * b_refs[0][...]).astype(BF16)

    return _mm(a, w, mode="nn", out_dtypes=(F32, BF16), extras=(res,), bcast=(g,), ref_epi=epilogue, name=name)


def _mm_final_loss(a, w, res, target, g, name):
    def epilogue(acc, e_refs, b_refs, o_refs, a_refs):
        xv = acc + e_refs[0][...]
        gv = b_refs[0][...]
        r = lax.rsqrt(jnp.mean(xv * xv, axis=-1, keepdims=True) + EPS)
        e = xv * r * gv - e_refs[1][...]
        tok = jnp.mean(e * e, axis=-1, keepdims=True)
        a_refs[0][...] += 0.5 * jnp.sum(tok, axis=0, keepdims=True) * jnp.ones((1, LANES), F32)
        dx, dg = _rms_bwd_math(xv, e * (1.0 / D_MODEL), gv)
        o_refs[0][...] = dx
        o_refs[1][...] = dx.astype(BF16)
        a_refs[1][...] += dg

    return _mm(a, w, mode="nn", out_dtypes=(F32, BF16), extras=(res, target), bcast=(g,),
               accs=((1, LANES), (1, D_MODEL)), ref_epi=epilogue, name=name)


def _relu2(acc):
    r = jnp.maximum(acc, 0.0)
    return r * r, r


def _mlp_fwd(x, h, fetch, tag, finish):
    w_up = fetch(f"mlp{tag}_up", h)
    a, r = _mm(h, w_up, mode="nn", b_shard=True, out_dtypes=(BF16, BF16), epi=_relu2, name=f"mlp{tag}_up")
    w_down = fetch(f"mlp{tag}_down", a)
    return finish(a, w_down, x, f"mlp{tag}_down"), (h, a, r, w_up, w_down)


def _mlp_bwd(x, g, saved, dx, dx_bf, tag, after):
    h, a, r, w_up, w_down = saved
    d_down = _mm(a, dx_bf, mode="tn", out_dtypes=(BF16,), name=f"mlp{tag}_dwdown", after=after)
    dup = _mm(dx_bf, w_down, mode="nt", extras=(r,), out_dtypes=(BF16,),
              epi=lambda acc, rv: (acc * (2.0 * rv.astype(F32)),), name=f"mlp{tag}_dup")
    d_up = _mm(h, dup, mode="tn", o_shard=True, out_dtypes=(BF16,), name=f"mlp{tag}_dwup")
    dx_new, dx_new_bf, dg = _mm_norm_bwd(dup, w_up, x, dx, g, f"mlp{tag}_dh")
    return dx_new, dx_new_bf, dg, d_up, d_down


def _rope_tables(L, B):
    rows = L // GRID_W
    row = np.repeat(np.arange(rows, dtype=np.float32), GRID_W)
    col = np.tile(np.arange(GRID_W, dtype=np.float32), rows)
    inv = (ROPE_THETA ** (-np.arange(HEAD_DIM // 4, dtype=np.float32) / (HEAD_DIM // 4))).astype(np.float32)
    ar, ac = row[:, None] * inv, col[:, None] * inv
    cos = np.concatenate([np.cos(ar), np.cos(ar), np.cos(ac), np.cos(ac)], axis=-1)
    sin = np.concatenate([-np.sin(ar), np.sin(ar), -np.sin(ac), np.sin(ac)], axis=-1)
    return jnp.asarray(np.tile(cos, (B, 1)), F32), jnp.asarray(np.tile(sin, (B, 1)), F32)


def _swap_halves(x):
    lane = lax.broadcasted_iota(jnp.int32, x.shape, 1)
    return jnp.where((lane % 64) < 32, pltpu.roll(x, HEAD_DIM - 32, 1), pltpu.roll(x, 32, 1))


def _qk_prep(qkv, cos, sin, q_g, k_g):
    def fn(ins, bs, outs, accs):
        c, s = ins[1][...], ins[2][...]
        for h in range(N_HEADS + N_KV):
            xv = ins[0][:, h * HEAD_DIM:(h + 1) * HEAD_DIM]
            g = bs[0][...] if h < N_HEADS else bs[1][...]
            r = lax.rsqrt(jnp.mean(xv * xv, axis=-1, keepdims=True) + EPS)
            z = xv * r * g
            y = (z * c + _swap_halves(z) * s).astype(BF16)
            if h < N_HEADS:
                outs[0][:, h * HEAD_DIM:(h + 1) * HEAD_DIM] = y
            else:
                outs[1][:, (h - N_HEADS) * HEAD_DIM:(h - N_HEADS + 1) * HEAD_DIM] = y
        outs[2][...] = ins[0][:, (N_HEADS + N_KV) * HEAD_DIM:].astype(BF16)

    kvw = N_KV * HEAD_DIM
    return _rowwise(fn, [qkv, cos, sin], [q_g, k_g], [(D_MODEL, BF16), (kvw, BF16), (kvw, BF16)], tm=1024,
                    name="attn_qk_prep")


def _qk_prep_bwd(qkv, dq, dk, dv, cos, sin, q_g, k_g):
    def fn(ins, bs, outs, accs):
        c, s = ins[4][...], ins[5][...]
        for h in range(N_HEADS + N_KV):
            sl = slice(h * HEAD_DIM, (h + 1) * HEAD_DIM)
            xv = ins[0][:, sl]
            if h < N_HEADS:
                g, dy, acc = bs[0][...], ins[1][:, sl], accs[0]
            else:
                ks = slice((h - N_HEADS) * HEAD_DIM, (h - N_HEADS + 1) * HEAD_DIM)
                g, dy, acc = bs[1][...], ins[2][:, ks], accs[1]
            r = lax.rsqrt(jnp.mean(xv * xv, axis=-1, keepdims=True) + EPS)
            xn = xv * r
            dz = dy * c - _swap_halves(dy) * s
            acc[...] += _rsum(dz * xn)
            dxn = dz * g
            outs[0][:, sl] = (r * (dxn - xn * jnp.mean(dxn * xn, axis=-1, keepdims=True))).astype(BF16)
        outs[0][:, (N_HEADS + N_KV) * HEAD_DIM:] = ins[3][...].astype(BF16)

    return _rowwise(fn, [qkv, dq, dk, dv, cos, sin], [q_g, k_g], [(qkv.shape[1], BF16)],
                    [(1, HEAD_DIM), (1, HEAD_DIM)], tm=512, name="attn_qk_prep_bwd")


_EXP2_SCALE = SCALE * math.log2(math.e)


def _exp_rows(q, k):
    s = lax.dot_general(q, k, _NT, preferred_element_type=F32)
    p = jnp.exp2((s - jnp.max(s, axis=-1, keepdims=True)) * _EXP2_SCALE)
    return p, jnp.sum(p, axis=-1, keepdims=True)


def _attn_fwd(q, k, v, B, L, tq=2048, sub=256):
    tq = min(tq, L)
    sub = min(sub, tq)
    nq = L // tq

    def body(q_ref, k_ref, v_ref, o_ref):
        kv, vv = k_ref[...], v_ref[...]
        for c in range(tq // sub):
            rows = slice(c * sub, (c + 1) * sub)
            p, l = _exp_rows(q_ref[rows, :], kv)
            o = jnp.dot(p.astype(BF16), vv, preferred_element_type=F32)
            o_ref[rows, :] = (o * (1.0 / l)).astype(o_ref.dtype)

    return pl.pallas_call(
        body, name="attn_fwd", grid=(B, N_HEADS, nq),
        in_specs=[pl.BlockSpec((tq, HEAD_DIM), lambda b, h, i: (b * nq + i, h)),
                  pl.BlockSpec((L, HEAD_DIM), lambda b, h, i: (b, h // GROUP)),
                  pl.BlockSpec((L, HEAD_DIM), lambda b, h, i: (b, h // GROUP))],
        out_specs=pl.BlockSpec((tq, HEAD_DIM), lambda b, h, i: (b * nq + i, h)),
        out_shape=jax.ShapeDtypeStruct((B * L, D_MODEL), BF16),
        compiler_params=_params(("parallel", "parallel", "parallel")),
    )(q, k, v)


def _attn_bwd(q, k, v, o, do, B, L, tq=2048, sub=512):
    tq = min(tq, L)
    sub = min(sub, tq)
    nq = L // tq

    def body(q_ref, k_ref, v_ref, o_ref, do_ref, dq_ref, dk_ref, dv_ref):
        @pl.when((pl.program_id(2) == 0) & (pl.program_id(3) == 0))
        def _():
            dk_ref[...] = jnp.zeros(dk_ref.shape, F32)
            dv_ref[...] = jnp.zeros(dv_ref.shape, F32)

        kv, vv = k_ref[...], v_ref[...]
        ps, es, dos, qs = [], [], [], []
        for c in range(tq // sub):
            rows = slice(c * sub, (c + 1) * sub)
            qc, doc = q_ref[rows, :], do_ref[rows, :]
            p, l = _exp_rows(qc, kv)
            inv = 1.0 / l
            dp = lax.dot_general(doc, vv, _NT, preferred_element_type=F32)
            delta = jnp.sum(doc.astype(F32) * o_ref[rows, :].astype(F32), axis=-1, keepdims=True)
            e = (p * (dp - delta)).astype(BF16)
            dq_ref[rows, :] = jnp.dot(e, kv, preferred_element_type=F32) * (inv * SCALE)
            ps.append(p.astype(BF16))
            es.append(e)
            dos.append((doc.astype(F32) * inv).astype(BF16))
            qs.append((qc.astype(F32) * (inv * SCALE)).astype(BF16))
        cat = lambda xs: xs[0] if len(xs) == 1 else jnp.concatenate(xs, axis=0)
        dv_ref[...] += lax.dot_general(cat(ps), cat(dos), _TN, preferred_element_type=F32)
        dk_ref[...] += lax.dot_general(cat(es), cat(qs), _TN, preferred_element_type=F32)

    qmap = lambda b, kh, g, i: (b * nq + i, kh * GROUP + g)
    kmap = lambda b, kh, g, i: (b, kh)
    kvw = N_KV * HEAD_DIM
    return pl.pallas_call(
        body, name="attn_bwd", grid=(B, N_KV, GROUP, nq),
        in_specs=[pl.BlockSpec((tq, HEAD_DIM), qmap), pl.BlockSpec((L, HEAD_DIM), kmap),
                  pl.BlockSpec((L, HEAD_DIM), kmap), pl.BlockSpec((tq, HEAD_DIM), qmap),
                  pl.BlockSpec((tq, HEAD_DIM), qmap)],
        out_specs=[pl.BlockSpec((tq, HEAD_DIM), qmap), pl.BlockSpec((L, HEAD_DIM), kmap),
                   pl.BlockSpec((L, HEAD_DIM), kmap)],
        out_shape=[jax.ShapeDtypeStruct((B * L, D_MODEL), F32), jax.ShapeDtypeStruct((B * L, kvw), F32),
                   jax.ShapeDtypeStruct((B * L, kvw), F32)],
        compiler_params=_params(("parallel", "parallel", "arbitrary", "arbitrary")),
    )(q, k, v, o, do)


def _conv_shift(x, t, L, k):
    if k == 2:
        return x
    if k < 2:
        return jnp.where(t >= 2 - k, pltpu.roll(x, 2 - k, 0), 0.0)
    return jnp.where(t < L - (k - 2), pltpu.roll(x, L - (k - 2), 0), 0.0)


def _conv_apply(x, w_ref, L):
    t = lax.broadcasted_iota(jnp.int32, x.shape, 0)
    acc = w_ref[4:5, :] + w_ref[2:3, :] * x
    for k in (0, 1, 3):
        acc = acc + w_ref[k:k + 1, :] * _conv_shift(x, t, L, k)
    return acc


def _conv_bwd(z, g, wb, dz, B, L, tc=256, after=None):
    noff = D_MODEL // tc
    order = _after_operand(after)

    def body(z_ref, g_ref, w_ref, dz_in, *rest):
        dx_ref, dw_ref = rest[len(order):]

        @pl.when(pl.program_id(1) == 0)
        def _():
            dw_ref[...] = jnp.zeros(dw_ref.shape, F32)

        x, gv = z_ref[...], g_ref[...]
        t = lax.broadcasted_iota(jnp.int32, x.shape, 0)
        dx = w_ref[2:3, :] * gv
        for k in (0, 1, 3):
            dx = dx + w_ref[k:k + 1, :] * _conv_shift(gv, t, L, 4 - k)
        dx_ref[...] = dx.astype(BF16)
        for k in range(4):
            dw_ref[k:k + 1, :] += _rsum(_conv_shift(x, t, L, k) * gv)
        dw_ref[4:5, :] += _rsum(gv)

    return pl.pallas_call(
        body, name="rg_conv_bwd", grid=(noff, B),
        in_specs=[pl.BlockSpec((L, tc), lambda j, b: (b, noff + j)), pl.BlockSpec((L, tc), lambda j, b: (b, j)),
                  pl.BlockSpec((SUBLANES, tc), lambda j, b: (0, j)), _ANY] + [_ANY] * len(order),
        out_specs=[pl.BlockSpec((L, tc), lambda j, b: (b, noff + j)),
                   pl.BlockSpec((SUBLANES, tc), lambda j, b: (0, j))],
        out_shape=[jax.ShapeDtypeStruct(dz.shape, dz.dtype), jax.ShapeDtypeStruct((SUBLANES, D_MODEL), F32)],
        input_output_aliases={3: 0},
        compiler_params=_params(("parallel", "arbitrary")),
    )(z, g, wb, dz, *order)


def _softplus(x):
    return jnp.maximum(x, 0.0) + jnp.log1p(jnp.exp(-jnp.abs(x)))


_ROW_BA, _ROW_BX, _ROW_LAM = 0, 2, 4


def _gate_math(xb, pre, vec_ref, d, sl):
    pa = pre[:, (2 * d) * LRU_BW:(2 * d + 1) * LRU_BW] + vec_ref[_ROW_BA + d:_ROW_BA + d + 1, sl]
    px = pre[:, (2 * d + 1) * LRU_BW:(2 * d + 2) * LRU_BW] + vec_ref[_ROW_BX + d:_ROW_BX + d + 1, sl]
    r = 0.5 * jnp.tanh(0.5 * pa) + 0.5
    i = 0.5 * jnp.tanh(0.5 * px) + 0.5
    slope = (-RG_C) * _softplus(-vec_ref[_ROW_LAM + d:_ROW_LAM + d + 1, sl])
    log_a = r * slope
    a = jnp.exp(log_a)
    om = -jnp.tanh(log_a) * (1.0 + a * a)
    rs = lax.rsqrt(om)
    mult = jnp.where(om > 0.0, om * rs, 0.0)
    return a, mult * (i * xb), (r, i, slope, om, mult, rs)


def _gate_bwd(rec, du_f, da_f, du_b, da_b, wcat, gvec):
    def fn(ins, bs, outs, accs):
        for blk in range(LRU_BLOCKS):
            sl = slice(blk * LRU_BW, (blk + 1) * LRU_BW)
            xb = ins[0][:, sl]
            xb16 = xb.astype(BF16)
            w = bs[0][sl, :]
            pre = jnp.dot(xb16, w, preferred_element_type=F32)
            dx = jnp.zeros_like(xb)
            dpre = []
            for d in range(2):
                a, _, (r, i, slope, om, mult, rs) = _gate_math(xb, pre, bs[1], d, sl)
                du, da = ins[1 + 2 * d][:, sl], ins[2 + 2 * d][:, sl]
                t = du * xb
                d_i = t * mult
                dx = dx + du * mult * i
                dlog = da * a - (t * i) * ((1.0 - om) * rs)
                d_r = dlog * slope
                d_sp = _rsum(dlog * r) * (-RG_C)
                lam = bs[1][_ROW_LAM + d:_ROW_LAM + d + 1, sl]
                accs[2][_ROW_LAM + d:_ROW_LAM + d + 1, sl] += d_sp * (-jax.nn.sigmoid(-lam))
                dpa = d_r * r * (1.0 - r)
                dpx = d_i * i * (1.0 - i)
                accs[2][_ROW_BA + d:_ROW_BA + d + 1, sl] += _rsum(dpa)
                accs[2][_ROW_BX + d:_ROW_BX + d + 1, sl] += _rsum(dpx)
                dpre += [dpa, dpx]
            dpre = jnp.concatenate(dpre, axis=1).astype(BF16)
            dw = lax.dot_general(xb16, dpre, _TN, preferred_element_type=F32)
            for d in range(2):
                rows = slice(d * D_MODEL + blk * LRU_BW, d * D_MODEL + (blk + 1) * LRU_BW)
                accs[0][rows, :] += dw[:, (2 * d) * LRU_BW:(2 * d + 1) * LRU_BW]
                accs[1][rows, :] += dw[:, (2 * d + 1) * LRU_BW:(2 * d + 2) * LRU_BW]
            outs[0][:, sl] = dx + lax.dot_general(dpre, w, _NT, preferred_element_type=F32)

    gate_shape = (2 * D_MODEL, LRU_BW)
    return _rowwise(fn, [rec, du_f, da_f, du_b, da_b], [wcat, gvec], [(D_MODEL, F32)],
                    [gate_shape, gate_shape, (SUBLANES, D_MODEL)], tm=512, name="rg_gate_bwd")


def _as_time_blocks(x):
    return x.reshape(x.shape[0] // SUBLANES, SUBLANES, x.shape[1])


def _scan_call(body, ins, n_out, B, L, tc, name):
    nb = L // SUBLANES
    spec = pl.BlockSpec((nb, SUBLANES, tc), lambda b, j: (b, 0, j))
    T = ins[0].shape[0]
    outs = pl.pallas_call(
        functools.partial(body, nb), name=name, grid=(B, D_MODEL // tc),
        in_specs=[spec] * len(ins), out_specs=[spec] * n_out,
        out_shape=[jax.ShapeDtypeStruct((T // SUBLANES, SUBLANES, D_MODEL), F32)] * n_out,
        compiler_params=_params(("parallel", "parallel")),
    )(*[_as_time_blocks(x) for x in ins])
    return [o.reshape(T, D_MODEL) for o in outs]


def _block_scan(A, U, reverse):
    row = lax.broadcasted_iota(jnp.int32, A.shape, 0)
    for s in (1, 2, 4):
        shift = SUBLANES - s if reverse else s
        valid = (row < SUBLANES - s) if reverse else (row >= s)
        a_sh = jnp.where(valid, pltpu.roll(A, shift, 0), 1.0)
        u_sh = jnp.where(valid, pltpu.roll(U, shift, 0), 0.0)
        U = A * u_sh + U
        A = A * a_sh
    return A, U


_LAST = SUBLANES - 1
SCAN_UNROLL = 8


def _loop_blocks(nb, step, init):
    def group(g, carry):
        for k in range(SCAN_UNROLL):
            carry = step(g * SCAN_UNROLL + k, carry)
        return carry

    return lax.fori_loop(0, nb // SCAN_UNROLL, group, init)


def _scan_bwd(dy, a_f, h_f, a_b, h_b, B, L, tc=256):
    def body(nb, dy_r, af, hf, ab, hb, duf, daf, dub, dab):
        def step(i, carry):
            c1, c2 = carry
            ir = nb - 1 - i
            row = lax.broadcasted_iota(jnp.int32, (SUBLANES, tc), 0)
            a_up = jnp.where(row == _LAST, af[jnp.minimum(ir + 1, nb - 1), :1, :], pltpu.roll(af[ir], _LAST, 0))
            p, lam = _block_scan(a_up, dy_r[ir], True)
            lam = lam + p * c1
            before = hf[jnp.maximum(ir - 1, 0), _LAST:, :] * (ir > 0).astype(F32)
            duf[ir] = lam
            daf[ir] = lam * jnp.where(row == 0, before, pltpu.roll(hf[ir], 1, 0))
            a_dn = jnp.where(row == 0, ab[jnp.maximum(i - 1, 0), _LAST:, :], pltpu.roll(ab[i], 1, 0))
            p2, lam2 = _block_scan(a_dn, dy_r[i], False)
            lam2 = lam2 + p2 * c2
            after = hb[jnp.minimum(i + 1, nb - 1), :1, :] * (i < nb - 1).astype(F32)
            dub[i] = lam2
            dab[i] = lam2 * jnp.where(row == _LAST, after, pltpu.roll(hb[i], _LAST, 0))
            return lam[:1, :], lam2[_LAST:, :]

        zero = jnp.zeros((1, tc), F32)
        _loop_blocks(nb, step, (zero, zero))

    return _scan_call(body, [dy, a_f, h_f, a_b, h_b], 4, B, L, tc, "rg_scan_bwd")


_GELU_C = math.sqrt(2.0 / math.pi)


def _gelu_parts(x):
    th = jnp.tanh(_GELU_C * (x + 0.044715 * x * x * x))
    return 0.5 * x * (1.0 + th), th


def _mm_gated_out_bwd(dx, w_out, h_f, h_b, z, name, after=None):
    def epilogue(acc, e_refs, b_refs, o_refs, a_refs):
        x = e_refs[2][...]
        gl, th = _gelu_parts(x)
        dgl = 0.5 * (1.0 + th) + 0.5 * x * (1.0 - th * th) * (_GELU_C * (1.0 + 3.0 * 0.044715 * x * x))
        o_refs[0][...] = acc * gl
        o_refs[1][...] = (acc * (e_refs[0][...] + e_refs[1][...]) * dgl).astype(BF16)

    return _mm(dx, w_out, mode="nt", out_dtypes=(F32, BF16), out_cols=(D_MODEL, 2 * D_MODEL), extras=(h_f, h_b, z),
               ref_epi=epilogue, name=name, after=after)


def _row_block(i):
    return pl.ds(pl.multiple_of(i * SUBLANES, SUBLANES), SUBLANES)


def _rg_mix_fwd(z, conv_wb, wcat, gvec, B, L):
    nb = L // SUBLANES
    n_g = D_MODEL // LRU_BW

    def body(zg_ref, zr_ref, cw_ref, w_ref, gv_ref, rec_ref, af_s, ab_s, hf_ref, hb_ref, yg_ref, uf_s, ub_s):
        rec = _conv_apply(zr_ref[...], cw_ref, L)
        rec_ref[...] = rec
        pre = jnp.dot(rec.astype(BF16), w_ref[...], preferred_element_type=F32)
        for d, (a_s, u_s) in enumerate(((af_s, uf_s), (ab_s, ub_s))):
            a, u, _ = _gate_math(rec, pre, gv_ref, d, slice(None))
            a_s[...] = a
            u_s[...] = u

        def step(i, carry):
            c1, c2 = carry
            rows, rows_b = _row_block(i), _row_block(nb - 1 - i)
            p, h = _block_scan(af_s[rows, :], uf_s[rows, :], False)
            h = h + p * c1
            hf_ref[rows, :] = h
            p2, h2 = _block_scan(ab_s[rows_b, :], ub_s[rows_b, :], True)
            h2 = h2 + p2 * c2
            hb_ref[rows_b, :] = h2
            return h[_LAST:, :], h2[:1, :]

        zero = jnp.zeros((1, LRU_BW), F32)
        _loop_blocks(nb, step, (zero, zero))
        gl, _ = _gelu_parts(zg_ref[...])
        yg_ref[...] = ((hf_ref[...] + hb_ref[...]) * gl).astype(BF16)

    seq = lambda off: pl.BlockSpec((L, LRU_BW), lambda b, g: (b, off + g))
    vec = pl.BlockSpec((SUBLANES, LRU_BW), lambda b, g: (0, g))
    T = B * L
    return pl.pallas_call(
        body, name="rg_mix", grid=(B, n_g),
        in_specs=[seq(0), seq(n_g), vec, pl.BlockSpec((LRU_BW, 4 * LRU_BW), lambda b, g: (g, 0)), vec],
        out_specs=[seq(0)] * 6,
        out_shape=[jax.ShapeDtypeStruct((T, D_MODEL), F32)] * 5 + [jax.ShapeDtypeStruct((T, D_MODEL), BF16)],
        scratch_shapes=[pltpu.VMEM((L, LRU_BW), F32)] * 2,
        compiler_params=_params(("parallel", "parallel")),
    )(z, z, conv_wb, wcat, gvec)


def _make_wcat(w_a, w_x):
    g = jnp.stack([w_a[0, 0], w_x[0, 0], w_a[0, 1], w_x[0, 1]])
    return jnp.transpose(g, (1, 2, 0, 3)).reshape(D_MODEL, 4 * LRU_BW)


def _rows_at(part, first):
    return jnp.pad(part, ((first, SUBLANES - first - part.shape[0]), (0, 0)))


def _qk_slot(q_g, k_g):
    wide = lambda v, at: jnp.pad(v, ((0, SUBLANES - 1), (at, D_MODEL - at - HEAD_DIM)))
    return wide(q_g, 0) + wide(k_g, HEAD_DIM)


def _local_step(x, target, P, fetch, emit, B, L, after=None):
    g_mix, g_mlp = P["norm_mix_g"], P["norm_mlp_g"]
    h0 = _rms_fwd(x, g_mix[0:1], "rg_norm", after=after)
    w_in, conv_wb, wcat, gvec = fetch("rg", h0)
    z = _mm(h0, w_in, mode="nn", b_shard=True, name="rg_in")
    rec, a_f, a_b, h_f, h_b, yg = _rg_mix_fwd(z, conv_wb, wcat, gvec, B, L)
    w_out = fetch("rg_out", yg)
    x1, h1 = _mm_res_norm(yg, w_out, x, g_mlp[0:1], "rg_out")
    (x2, h3), mlp0 = _mlp_fwd(x1, h1, fetch, 0, lambda a, w, res, name: _mm_res_norm(a, w, res, g_mix[1:2], name))
    w_qkv, w_o = fetch("att", h3)
    qkv = _mm(h3, w_qkv, mode="nn", b_shard=True, name="attn_qkv")
    cos, sin = _rope_tables(L, B)
    qh, kh, vh = _qk_prep(qkv, cos, sin, P["q_g"], P["k_g"])
    o = _attn_fwd(qh, kh, vh, B, L)
    x3, h4 = _mm_res_norm(o, w_o, x2, g_mlp[1:2], "attn_out")
    (dx4, dx4_bf, loss_acc, d_final_g), mlp1 = _mlp_fwd(
        x3, h4, fetch, 1, lambda a, w, res, name: _mm_final_loss(a, w, res, target, P["final_g"], name))

    dx3, dx3_bf, dg_mlp1, d_up1, d_down1 = _mlp_bwd(x3, g_mlp[1:2], mlp1, dx4, dx4_bf, 1, None)
    tok = emit("mlp1", [d_up1, d_down1])
    d_wo = _mm(o, dx3_bf, mode="tn", out_dtypes=(BF16,), name="attn_dwo", after=tok)
    do = _mm(dx3_bf, w_o, mode="nt", out_dtypes=(BF16,), name="attn_do")
    dq, dk, dv = _attn_bwd(qh, kh, vh, o, do, B, L)
    dqkv, dq_g, dk_g = _qk_prep_bwd(qkv, dq, dk, dv, cos, sin, P["q_g"], P["k_g"])
    d_wqkv = _mm(h3, dqkv, mode="tn", o_shard=True, out_dtypes=(BF16,), name="attn_dwqkv")
    tok = emit("att", [d_wqkv, d_wo])
    dx2, dx2_bf, dg_mix1 = _mm_norm_bwd(dqkv, w_qkv, x2, dx3, g_mix[1:2], "attn_dh", after=tok)
    tok = emit("point_attn_done", [dx2_bf])
    dx1, dx1_bf, dg_mlp0, d_up0, d_down0 = _mlp_bwd(x1, g_mlp[0:1], mlp0, dx2, dx2_bf, 0, tok)
    d_wout = _mm(yg, dx1_bf, mode="tn", out_dtypes=(BF16,), name="rg_dwout")
    tok = emit("mlp0", [d_up0, d_down0, d_wout])
    dy, dgate = _mm_gated_out_bwd(dx1_bf, w_out, h_f, h_b, z, "rg_dyg", after=tok)
    du_f, da_f, du_b, da_b = _scan_bwd(dy, a_f, h_f, a_b, h_b, B, L)
    drec_c, d_wa, d_wx, d_gvec = _gate_bwd(rec, du_f, da_f, du_b, da_b, wcat, gvec)
    tok = emit("gates", [d_wa, d_wx])
    dz, d_convwb = _conv_bwd(z, drec_c, conv_wb, dgate, B, L, after=tok)
    tok = emit("point_mix_done", [dz])
    d_win = _mm(h0, dz, mode="tn", o_shard=True, out_dtypes=(BF16,), name="rg_dwin", after=tok)
    tok = emit("rg_in", [d_win])
    grad_x, _, dg_mix0 = _mm_norm_bwd(dz, w_in, x, dx1, g_mix[0:1], "rg_dh", after=tok)

    norms = (_rows_at(dg_mix0, 0) + _rows_at(dg_mix1, 1) + _rows_at(dg_mlp0, 2) + _rows_at(dg_mlp1, 3)
             + _rows_at(d_final_g, 4)
             + jnp.pad(loss_acc, ((LOSS_ROW, SUBLANES - 1 - LOSS_ROW), (0, D_MODEL - LANES))))
    vec = jnp.concatenate([norms, d_convwb, d_gvec, _qk_slot(dq_g, dk_g)], axis=0)
    return grad_x, vec


_MESH = pl.DeviceIdType.MESH


def _place():
    x, y, c = lax.axis_index("x"), lax.axis_index("y"), lax.axis_index("c")
    peers = [((1 - x) if j & 2 else x, (1 - y) if j & 1 else y) for j in (1, 2, 3)]
    return x, y, c, peers


def _sum_leading(slots, name):
    def body(s_ref, o_ref):
        acc = s_ref[0]
        for d in range(1, slots.shape[0]):
            acc = acc + s_ref[d]
        o_ref[...] = acc

    return pl.pallas_call(body, name=name, out_shape=jax.ShapeDtypeStruct(slots.shape[1:], slots.dtype))(slots)


_HBM = pl.BlockSpec(memory_space=pltpu.HBM)
_SEM = pl.BlockSpec(memory_space=pltpu.SEMAPHORE)
_EFFECT = pltpu.SideEffectType.DATAFLOW_SIDE_EFFECTING


_COPIES = dict(gather=N_CHIPS - 1, scatter=N_CHIPS - 1, swap=1, spread=N_DEVICES - 1,
               gather_half=N_CHIPS - 1, share_half=N_CHIPS - 1)


def _split_copies(kind, srcs, lands, send, recv):
    x, y, c, peers = _place()
    me = 2 * x + y
    per = _COPIES[kind]
    out = []
    for a in range(len(lands)):
        for j in range(per):
            if kind == "swap":
                src, there, here, dev = srcs[a], lands[a], lands[a], (x, y, 1 - c)
            elif kind == "spread":
                k = j + 1
                dev = ((1 - x) if k & 4 else x, (1 - y) if k & 2 else y, (1 - c) if k & 1 else c)
                mine = lands[a].at[4 * x + 2 * y + c]
                src, there, here = mine, mine, lands[a].at[4 * dev[0] + 2 * dev[1] + dev[2]]
            else:
                px, py = peers[j]
                dev = (px, py, c)
                if kind == "gather":
                    src, there, here = lands[a].at[me], lands[a].at[me], lands[a].at[2 * px + py]
                elif kind in ("gather_half", "share_half"):
                    half = lands[a].shape[1] // 2
                    mine, other = pl.ds(c * half, half), pl.ds((1 - c) * half, half)
                    if kind == "gather_half":
                        src = there = lands[a].at[me, mine]
                        here = lands[a].at[2 * px + py, mine]
                    else:
                        src = there = lands[a].at[2 * px + py, mine]
                        here = lands[a].at[2 * px + py, other]
                        dev = (x, y, 1 - c)
                else:
                    src, there, here = srcs[a].at[2 * px + py], lands[a].at[j], lands[a].at[j]
            mk = functools.partial(
                pltpu.make_async_remote_copy, src_ref=src, send_sem=send.at[per * a + j],
                recv_sem=recv.at[per * a + j], device_id=dev, device_id_type=_MESH)
            out.append((functools.partial(mk, dst_ref=there), functools.partial(mk, dst_ref=here)))
    return out


def _exchange_start(kind, srcs, lands, name, after=None):
    arrays = list(srcs) + list(lands)
    n_s, n, n_all = len(srcs), len(lands), len(srcs) + len(lands)
    n_sem = _COPIES[kind] * n
    order = _after_operand(after)
    n_x = len(order)

    def body(*refs):
        send, recv = refs[n_all + n_x], refs[n_all + n_x + 1]
        token = refs[-1]
        for started, _ in _split_copies(kind, refs[:n_s], refs[n_s:n_all], send, recv):
            started().start()
        token[...] = jnp.zeros(token.shape, F32)

    res = pl.pallas_call(
        body, name=name,
        out_shape=(pltpu.SemaphoreType.DMA((n_sem,)), pltpu.SemaphoreType.DMA((n_sem,)),
                   *[pltpu.HBM(a.shape, a.dtype) for a in arrays], jax.ShapeDtypeStruct((SUBLANES, LANES), F32)),
        in_specs=[_HBM] * n_all + [_ANY] * n_x,
        out_specs=(_SEM, _SEM, *[_HBM] * n_all, pl.BlockSpec(memory_space=pltpu.VMEM)),
        input_output_aliases={i: 2 + i for i in range(n_all)},
        compiler_params=pltpu.CompilerParams(has_side_effects=_EFFECT),
    )(*[pltpu.with_memory_space_constraint(a, pltpu.HBM) for a in arrays], *order)
    return (res[0], res[1], res[2:2 + n_s], res[2 + n_s:2 + n_all]), res[-1]


def _gather_start_groups(land_groups, name, after=None, kind="gather"):
    arrays = [a for group in land_groups for a in group]
    n_all, n_g = len(arrays), len(land_groups)
    order = _after_operand(after)
    n_x = len(order)

    def body(*refs):
        first = 0
        for gi, group in enumerate(land_groups):
            send, recv = refs[n_all + n_x + 2 * gi], refs[n_all + n_x + 2 * gi + 1]
            for started, _ in _split_copies(kind, [], refs[first:first + len(group)], send, recv):
                started().start()
            first += len(group)
        refs[-1][...] = jnp.zeros(refs[-1].shape, F32)

    sems = [pltpu.SemaphoreType.DMA((_COPIES[kind] * len(group),)) for group in land_groups for _ in range(2)]
    res = pl.pallas_call(
        body, name=name,
        out_shape=(*sems, *[pltpu.HBM(a.shape, a.dtype) for a in arrays], jax.ShapeDtypeStruct((SUBLANES, LANES), F32)),
        in_specs=[_HBM] * n_all + [_ANY] * n_x,
        out_specs=(*[_SEM] * (2 * n_g), *[_HBM] * n_all, pl.BlockSpec(memory_space=pltpu.VMEM)),
        input_output_aliases={i: 2 * n_g + i for i in range(n_all)},
        compiler_params=pltpu.CompilerParams(has_side_effects=_EFFECT),
    )(*[pltpu.with_memory_space_constraint(a, pltpu.HBM) for a in arrays], *order)
    handles, first = [], 2 * n_g
    for gi, group in enumerate(land_groups):
        handles.append((res[2 * gi], res[2 * gi + 1], [], res[first:first + len(group)]))
        first += len(group)
    return handles, res[-1]


def _exchange_wait(kind, handle, after, name):
    send, recv, srcs, lands = handle
    arrays = list(srcs) + list(lands)
    n_s, n_all = len(srcs), len(arrays)
    order = list(after) if isinstance(after, (list, tuple)) else [after]

    def body(*refs):
        for started, landing in _split_copies(kind, refs[:n_s], refs[n_s:n_all], refs[n_all], refs[n_all + 1]):
            started().wait_send()
            landing().wait_recv()

    res = pl.pallas_call(
        body, name=name, out_shape=[pltpu.HBM(a.shape, a.dtype) for a in arrays],
        in_specs=[_HBM] * n_all + [_SEM, _SEM] + [_ANY] * len(order), out_specs=[_HBM] * n_all,
        input_output_aliases={i: i for i in range(n_all)},
        compiler_params=pltpu.CompilerParams(has_side_effects=_EFFECT),
    )(*arrays, send, recv, *order)
    return res[:n_s], res[n_s:]


def _index_operand(i):
    return jnp.reshape(i, (1,)).astype(jnp.int32)


def _cast_into_slot(src, row0, rows, me, dtype, name, after=None, add=None, n_slots=N_CHIPS):
    cols = src.shape[1]
    tm = min(512, rows)
    order = _after_operand(after)
    terms = [src] + ([] if add is None else [add])

    def body(me_ref, *rest):
        val = rest[0][...]
        if add is not None:
            val = val + rest[1][...]
        rest[-1][...] = val.astype(dtype)

    return pl.pallas_call(
        body, name=name,
        grid_spec=pltpu.PrefetchScalarGridSpec(
            num_scalar_prefetch=1, grid=(rows // tm,),
            in_specs=[pl.BlockSpec((tm, cols), lambda i, me_ref: (i + row0 // tm, 0))] * len(terms)
            + [_ANY] * len(order),
            out_specs=pl.BlockSpec((None, tm, cols), lambda i, me_ref: (me_ref[0], i, 0))),
        out_shape=jax.ShapeDtypeStruct((n_slots, rows, cols), dtype), compiler_params=_params(("parallel",)),
    )(_index_operand(me), *terms, *order)


def _sum_slots(mine, r, me, name):
    _, rows, cols = r.shape
    tm = min(512, rows)

    def body(me_ref, own_ref, r_ref, o_ref):
        o_ref[...] = ((own_ref[...].astype(F32) + r_ref[0].astype(F32)) + r_ref[1].astype(F32)) + r_ref[2].astype(F32)

    return pl.pallas_call(
        body, name=name,
        grid_spec=pltpu.PrefetchScalarGridSpec(
            num_scalar_prefetch=1, grid=(rows // tm,),
            in_specs=[pl.BlockSpec((None, tm, cols), lambda i, me_ref: (me_ref[0], i, 0)),
                      pl.BlockSpec((N_CHIPS - 1, tm, cols), lambda i, me_ref: (0, i, 0))],
            out_specs=pl.BlockSpec((tm, cols), lambda i, me_ref: (i, 0))),
        out_shape=jax.ShapeDtypeStruct((rows, cols), F32), compiler_params=_params(("parallel",)),
    )(_index_operand(me), mine, r)


def _adamw(w, m, v, ps, qs, name):
    rows, cols = w.shape
    seg_rows = ps[0].shape[0]
    tm = min(512, seg_rows)
    while seg_rows % tm:
        tm -= SUBLANES
    per, n_seg = seg_rows // tm, len(ps)
    parts = list(ps) + ([] if qs is None else list(qs))

    def body(w_ref, m_ref, v_ref, *rest):
        g_refs, outs = rest[:len(parts)], rest[len(parts):]
        grad = lambda s: g_refs[s][...] if qs is None else g_refs[s][...] + g_refs[n_seg + s][...]
        g = grad(0)
        for s in range(1, n_seg):
            g = jnp.where(pl.program_id(0) >= s * per, grad(s), g)
        m1 = ADAM_B1 * m_ref[...] + (1.0 - ADAM_B1) * g
        v1 = ADAM_B2 * v_ref[...] + (1.0 - ADAM_B2) * (g * g)
        m_hat = m1 / (1.0 - ADAM_B1 ** ADAM_STEP)
        v_hat = v1 / (1.0 - ADAM_B2 ** ADAM_STEP)
        outs[0][...] = g
        outs[1][...] = (-ADAM_LR) * (m_hat / (jnp.sqrt(v_hat) + ADAM_EPS) + ADAM_WD * w_ref[...])
        outs[2][...] = m1
        outs[3][...] = v1

    row_spec = pl.BlockSpec((tm, cols), lambda i: (i, 0))
    seg_spec = lambda s: pl.BlockSpec((tm, cols), lambda i: (jnp.clip(i - s * per, 0, per - 1), 0))
    return pl.pallas_call(
        body, name=name, grid=(rows // tm,),
        in_specs=[row_spec] * 3 + [seg_spec(s) for s in range(n_seg)] * (1 if qs is None else 2),
        out_specs=[row_spec] * 4, out_shape=[jax.ShapeDtypeStruct((rows, cols), F32)] * 4,
        compiler_params=_params(("arbitrary",)),
    )(w, m, v, *parts)


def _put_cols(shard, me):
    full = jnp.zeros((shard.shape[0], D_MODEL), F32)
    return lax.dynamic_update_slice(full, shard, (0, me * (D_MODEL // N_CHIPS)))


def _gate_vec_slot(b_a, b_x, lam):
    return _rows_at(b_a, _ROW_BA) + _rows_at(b_x, _ROW_BX) + _rows_at(lam, _ROW_LAM)


def _pack_vec(p, me):
    return jnp.concatenate([
        _rows_at(p["norm_mix_g"], 0) + _rows_at(p["norm_mlp_g"], 2) + _rows_at(p["final_g"][None], 4),
        _rows_at(_put_cols(p["rg_conv_w"][0, :, 0, :], me), 0) + _rows_at(p["rg_conv_b"], 4),
        _gate_vec_slot(_put_cols(p["rg_b_a"][0], me), _put_cols(p["rg_b_x"][0], me), _put_cols(p["rg_lam"][0], me)),
        _qk_slot(p["at_q_g"], p["at_k_g"]),
    ], axis=0)


def _unpack_vec(r, me):
    def cols(rows):
        return lax.dynamic_slice(rows, (0, me * (D_MODEL // N_CHIPS)), (rows.shape[0], D_MODEL // N_CHIPS))

    gate = r[16:24]
    return dict(
        norm_mix_g=r[0:2], norm_mlp_g=r[2:4], final_g=r[4], rg_conv_w=cols(r[8:12])[None, :, None, :],
        rg_conv_b=r[12:13], rg_b_a=cols(gate[_ROW_BA:_ROW_BA + 2])[None], rg_b_x=cols(gate[_ROW_BX:_ROW_BX + 2])[None],
        rg_lam=cols(gate[_ROW_LAM:_ROW_LAM + 2])[None], at_q_g=r[24:25, 0:HEAD_DIM],
        at_k_g=r[24:25, HEAD_DIM:2 * HEAD_DIM])


_WEIGHTS = ['norm_mix_g', 'norm_mlp_g', 'rg_w_in', 'rg_conv_w', 'rg_conv_b', 'rg_w_a', 'rg_b_a', 'rg_w_x', 'rg_b_x',
            'rg_lam', 'rg_w_out', 'at_w_qkv', 'at_q_g', 'at_k_g', 'at_w_o', 'mlp_w_up', 'mlp_w_down', 'final_g']
_BIG = dict(rg_w_in=["rg_w_in"], rg_w_out=["rg_w_out"], at_w_qkv=["at_w_qkv"], at_w_o=["at_w_o"],
            mlp_w_up=["up0", "up1"], mlp_w_down=["down0", "down1"])


def kernel(x, *args):
    n_w = len(_WEIGHTS)
    w = dict(zip(_WEIGHTS, args[:n_w]))
    target = args[n_w]
    m = dict(zip(_WEIGHTS, args[n_w + 1:2 * n_w + 1]))
    v = dict(zip(_WEIGHTS, args[2 * n_w + 1:3 * n_w + 1]))
    B, L, _ = x.shape
    T = B * L
    me = 2 * lax.axis_index("x") + lax.axis_index("y")

    vec = jnp.concatenate([_gate_vec_slot(w["rg_b_a"][0], w["rg_b_x"][0], w["rg_lam"][0]),
                           _rows_at(w["rg_conv_w"][0, :, 0, :], 0)], axis=0)
    flat = lambda a: a.reshape(-1, a.shape[-1])
    rows_of = lambda k: w[k].shape[-2]
    groups = [("rg", [("rg_w_in", 0, BF16), (vec, 0, F32)]), ("rg_out", [("rg_w_out", 0, BF16)]),
              ("mlp0_up", [("mlp_w_up", 0, BF16)]), ("mlp0_down", [("mlp_w_down", 0, BF16)]),
              ("att", [("at_w_qkv", 0, BF16), ("at_w_o", 0, BF16)]),
              ("mlp1", [("mlp_w_up", 1, BF16), ("mlp_w_down", 1, BF16)])]

    def landing_zones(group, members, after):
        lands = []
        for n, (k, layer, dtype) in enumerate(members):
            src, rows = (flat(w[k]), rows_of(k)) if isinstance(k, str) else (k, k.shape[0])
            lands.append(_cast_into_slot(src, layer * rows, rows, me, dtype, f"place_{group}{n}", after=after))
        return lands

    halves, gathers = {}, {}
    halves["rg"], tok = _exchange_start("gather_half", [], landing_zones(*groups[0], None), "gather_rg_start")
    handles, tok = _gather_start_groups([landing_zones(g, members, tok) for g, members in groups[1:]],
                                        "gather_rest_start", after=tok, kind="gather_half")
    halves.update(zip([g for g, _ in groups[1:]], handles))
    wcat = _make_wcat(w["rg_w_a"], w["rg_w_x"]).astype(BF16)

    packs = [_pack_vec(p, me) for p in (w, m, v)]

    ready = {}

    def share(some, after, name):
        landed = [_exchange_wait("gather_half", halves[g], after, f"gather_{g}_landed")[1] for g in some]
        handles, _ = _gather_start_groups(landed, name, kind="share_half")
        gathers.update(zip(some, handles))

    def fetch(what, after):
        if what in ready:
            return ready[what]
        group = "mlp1" if what.startswith("mlp1") else what
        if group == "rg":
            share(["rg"], [after, wcat] + packs, "share_rg_start")
        elif group == "rg_out":
            share(["rg_out", "mlp0_up", "mlp0_down", "att"], after, "share_early_start")
        _, full = _exchange_wait("share_half", gathers[group], after, f"gather_{group}_wait")
        if group == "att":
            share(["mlp1"], after, "share_mlp1_start")
        if group == "rg":
            vec_full = jnp.transpose(full[1], (1, 0, 2)).reshape(2 * SUBLANES, D_MODEL)
            conv_wb = vec_full[SUBLANES:] + _rows_at(w["rg_conv_b"], 4)
            return full[0], conv_wb, wcat, vec_full[:SUBLANES]
        if group == "rg_out":
            return full[0].reshape(D_MODEL, D_MODEL)
        if group == "att":
            return full[0], full[1].reshape(D_MODEL, D_MODEL)
        if group == "mlp1":
            ready["mlp1_up"], ready["mlp1_down"] = full[0], full[1].reshape(4 * D_MODEL, D_MODEL)
            return ready[what]
        return full[0] if group == "mlp0_up" else full[0].reshape(4 * D_MODEL, D_MODEL)

    names = dict(mlp1=["up1", "down1"], att=["at_w_qkv", "at_w_o"], mlp0=["up0", "down0", "rg_w_out"],
                 rg_in=["rg_w_in"], gates=["rg_w_a", "rg_w_x"])
    scatters, swaps, P, Q, res = {}, [], {}, {}, {}

    def start_scatter(group, grads):
        srcs = [g.reshape(N_CHIPS, -1, g.shape[-1]) for g in grads]
        lands = [lax.empty((N_CHIPS - 1,) + s.shape[1:], s.dtype) for s in srcs]
        scatters[group], token = _exchange_start("scatter", srcs, lands, f"scatter_{group}_start")
        return token

    def settle(groups, after):
        keys, parts = [], []
        for group in groups:
            srcs, lands = _exchange_wait("scatter", scatters[group], after, f"scatter_{group}_wait")
            for k, s, r in zip(names[group], srcs, lands):
                keys.append(k)
                parts.append(_sum_slots(s, r, me, f"sum_{k}"))
        handle, token = _exchange_start("swap", parts, [lax.empty(p.shape, F32) for p in parts],
                                        f"swap_{groups[0]}_start")
        swaps.append((keys, handle, f"swap_{groups[0]}_wait"))
        return token

    def finish(after):
        for keys, handle, name in swaps:
            mine, theirs = _exchange_wait("swap", handle, after, name)
            P.update(zip(keys, mine))
            Q.update(zip(keys, theirs))
        swaps.clear()
        last = after
        for k, parts in _BIG.items():
            if k in res or any(p not in P for p in parts):
                continue
            shape = w[k].shape
            two_d = lambda a: a.reshape(-1, shape[-1])
            outs = _adamw(two_d(w[k]), two_d(m[k]), two_d(v[k]), [P[p] for p in parts], [Q[p] for p in parts],
                          f"adamw_{k}")
            res[k] = [o.reshape(shape) for o in outs]
            last = outs[0]
        if "rg_w_a" in P and "gates" not in gathers:
            lands = [_cast_into_slot(P[k], 0, P[k].shape[0], me, F32, f"place_{k}", after=last, add=Q[k])
                     for k in names["gates"]]
            gathers["gates"], last = _exchange_start("gather", [], lands, "gather_gates_start", after=last)
        return last

    def emit(event, arrays):
        if event == "point_attn_done":
            return None
        if event == "point_mix_done":
            return settle(["mlp1", "att", "mlp0"], arrays[0])
        token = start_scatter(event, arrays)
        if event == "rg_in":
            return finish(settle(["gates"], token))
        return token

    P_vec = dict(norm_mix_g=w["norm_mix_g"], norm_mlp_g=w["norm_mlp_g"], final_g=w["final_g"][None],
                 q_g=w["at_q_g"], k_g=w["at_k_g"])
    grad_x, vec_part = _local_step(x.reshape(T, D_MODEL), target.reshape(T, D_MODEL), P_vec, fetch, emit, B, L,
                                   after=tok)

    me8 = 2 * me + lax.axis_index("c")
    vec_slots = _cast_into_slot(vec_part, 0, VEC_ROWS, me8, F32, "place_vec", n_slots=N_DEVICES)
    spread, tok = _exchange_start("spread", [], [vec_slots], "spread_vec_start")
    last = finish(settle(["rg_in"], tok))
    _, gate_grads = _exchange_wait("gather", gathers["gates"], last, "gather_gates_wait")
    for k, g in zip(names["gates"], gate_grads):
        two_d = lambda a: a.reshape(g.shape[0] * g.shape[1], g.shape[2])
        outs = _adamw(two_d(w[k]), two_d(m[k]), two_d(v[k]), [two_d(g)], None, f"adamw_{k}")
        res[k] = [o.reshape(w[k].shape) for o in outs]
        last = outs[0]
    _, (vec_all,) = _exchange_wait("spread", spread, last, "spread_vec_wait")
    vec_grad = _sum_leading(vec_all, "sum_vec")
    loss = vec_grad[LOSS_ROW, 0]
    outs = _adamw(*packs, [vec_grad], None, "adamw_vec")
    unpacked = [_unpack_vec(o, me) for o in outs]
    for k in _WEIGHTS:
        if k not in res:
            res[k] = [u[k] for u in unpacked]

    result = [loss, grad_x.reshape(B, L, D_MODEL)]
    for slot in range(4):
        result += [res[k][slot] for k in _WEIGHTS]
    return tuple(result)
```

```python
import functools
import math

import jax
import jax.numpy as jnp
import numpy as np
from jax import lax
from jax.experimental import pallas as pl
from jax.experimental.pallas import tpu as pltpu

F32 = jnp.float32
BF16 = jnp.bfloat16

D_MODEL = 1024
HEAD_DIM = 128
N_HEADS = 8
N_KV = 2
GROUP = N_HEADS // N_KV
LRU_BLOCKS = 8
LRU_BW = 128
GRID_W = 64
ROPE_THETA = 10000.0
EPS = 1e-6
RG_C = 8.0
SCALE = 1.0 / math.sqrt(HEAD_DIM)
N_CHIPS = 4

ADAM_LR = 0.001
ADAM_B1 = 0.9
ADAM_B2 = 0.999
ADAM_EPS = 1e-08
ADAM_WD = 0.01
ADAM_STEP = 10

V7X_VMEM_BYTES = 64 * 1024 * 1024
VMEM_LIMIT = V7X_VMEM_BYTES * 3 // 4
LANES = 128
SUBLANES = 8

N_DEVICES = 8
VEC_ROWS = 32
LOSS_ROW = 5


def _params(sem):
    return pltpu.CompilerParams(dimension_semantics=sem, vmem_limit_bytes=VMEM_LIMIT)


_ANY = pl.BlockSpec(memory_space=pl.ANY)
_NN = (((1,), (0,)), ((), ()))
_NT = (((1,), (1,)), ((), ()))
_TN = (((0,), (0,)), ((), ()))


def _after_operand(after):
    return [] if after is None else [after]


def _fit(t, n):
    if n <= t:
        return n
    c = (t // LANES) * LANES
    while n % c:
        c -= LANES
    return c


MM_VMEM_BUDGET = VMEM_LIMIT * 3 // 4
def _mm_tiles(M, K, ns, n_total, out_dtypes, extras, whole_rows):
    for tm in (2048, 1024, 512, 256, 128):
        for tn in ((ns,) if whole_rows else (1024, 512, 256)):
            tn = _fit(tn, ns)
            per_row = 2 * (2 * K) + 4 * tn + sum(2 * tn * jnp.dtype(d).itemsize for d in out_dtypes)
            per_row += sum(2 * tn * e.dtype.itemsize for e in extras)
            b_buffers = 1 if tn == n_total else 2
            if M % tm == 0 and b_buffers * (2 * K * tn) + tm * per_row <= MM_VMEM_BUDGET:
                return tm, tn
    raise ValueError(f"no tile fits VMEM for M={M} K={K} N={ns}")


def _mm(a, b, *, mode, name, out_dtypes=(F32,), b_shard=False, o_shard=False, extras=(), epi=None, after=None,
        bcast=(), accs=(), ref_epi=None, out_cols=None):
    if mode == "tn":
        K, M = a.shape
        N = b.shape[1]
    else:
        M, K = a.shape
        if mode == "nn":
            N = b.shape[0] * b.shape[2] if b_shard else b.shape[1]
        else:
            N = b.shape[1] if b_shard else b.shape[0]
    ns = N
    if b_shard and mode == "nn":
        ns = b.shape[2]
    elif o_shard:
        ns = N // N_CHIPS
    tm, tn = _mm_tiles(M, K, ns, N, out_dtypes, extras, whole_rows=ref_epi is not None)
    if ref_epi is not None:
        tm = min(tm, 512)
    grid = (M // tm, N // tn)
    q = ns // tn
    once = dict(pipeline_mode=pl.Buffered(1)) if tn == N else {}

    if mode == "tn":
        a_spec = pl.BlockSpec((K, tm), lambda i, j: (0, i))
        b_spec = pl.BlockSpec((K, tn), lambda i, j: (0, j), **once)
        dims = _TN
    elif mode == "nn":
        a_spec = pl.BlockSpec((tm, K), lambda i, j: (i, 0))
        if b_shard:
            b_spec = pl.BlockSpec((None, K, tn), lambda i, j: (j // q, 0, j % q), **once)
        else:
            b_spec = pl.BlockSpec((K, tn), lambda i, j: (0, j), **once)
        dims = _NN
    else:
        a_spec = pl.BlockSpec((tm, K), lambda i, j: (i, 0))
        if b_shard:
            ks = b.shape[2]
            b_spec = pl.BlockSpec((N_CHIPS, tn, ks), lambda i, j: (0, j, 0), **once)
        else:
            b_spec = pl.BlockSpec((tn, K), lambda i, j: (j, 0), **once)
        dims = _NT

    if o_shard:
        o_specs = [pl.BlockSpec((None, tm, tn), lambda i, j: (j // q, i, j % q))]
        o_shapes = [jax.ShapeDtypeStruct((N_CHIPS, M, ns), out_dtypes[0])]
    else:
        o_specs = [pl.BlockSpec((tm, tn), lambda i, j: (i, j)) for _ in out_dtypes]
        o_shapes = [jax.ShapeDtypeStruct((M, N if out_cols is None else out_cols[n]), dt)
                    for n, dt in enumerate(out_dtypes)]
    e_specs = [pl.BlockSpec((tm, tn), lambda i, j: (i, j)) for _ in extras]
    e_specs += [pl.BlockSpec(v.shape, lambda i, j: (0, 0)) for v in bcast]
    o_specs += [pl.BlockSpec(s, lambda i, j: (0, 0)) for s in accs]
    o_shapes += [jax.ShapeDtypeStruct(s, F32) for s in accs]
    n_e, n_b, n_o, n_a = len(extras), len(bcast), len(out_dtypes), len(accs)
    order = _after_operand(after)
    n_x = len(order)
    if epi is None:
        epi = lambda acc: (acc,)

    def body(a_ref, b_ref, *rest):
        e_refs, b_refs = rest[:n_e], rest[n_e:n_e + n_b]
        o_refs = rest[n_e + n_b + n_x:n_e + n_b + n_x + n_o]
        a_refs = rest[n_e + n_b + n_x + n_o:]
        if n_a:
            @pl.when((pl.program_id(0) == 0) & (pl.program_id(1) == 0))
            def _():
                for r in a_refs:
                    r[...] = jnp.zeros(r.shape, F32)
        if mode == "nt" and b_shard:
            acc = None
            for s in range(N_CHIPS):
                part = lax.dot_general(a_ref[:, s * ks:(s + 1) * ks], b_ref[s], dims, preferred_element_type=F32)
                acc = part if acc is None else acc + part
        else:
            acc = lax.dot_general(a_ref[...], b_ref[...], dims, preferred_element_type=F32)
        if ref_epi is not None:
            ref_epi(acc, e_refs, b_refs, o_refs, a_refs)
            return
        outs = epi(acc, *[r[...] for r in e_refs])
        for r, o in zip(o_refs, outs):
            r[...] = o.astype(r.dtype)

    outs = pl.pallas_call(
        body, name=name, grid=grid, in_specs=[a_spec, b_spec] + e_specs + [_ANY] * n_x, out_specs=o_specs,
        out_shape=o_shapes, compiler_params=_params(("arbitrary", "arbitrary") if n_a else ("parallel", "parallel")),
    )(a, b, *extras, *bcast, *order)
    return outs[0] if n_o + n_a == 1 else outs


def _rowwise(fn, rows, bcast, outs, accs=(), *, tm, name, after=None):
    def norm(r):
        return r if isinstance(r, tuple) else (r, r.shape[1], 0)

    rows = [norm(r) for r in rows]
    T = rows[0][0].shape[0]
    tm = min(tm, T)
    while T % tm:
        tm -= SUBLANES
    n_r, n_b, n_o, n_a = len(rows), len(bcast), len(outs), len(accs)
    order = _after_operand(after)
    n_x = len(order)
    in_specs = [pl.BlockSpec((tm, c), functools.partial(lambda i, cb: (i, cb), cb=cb)) for _, c, cb in rows]
    in_specs += [pl.BlockSpec(b.shape, lambda i: (0, 0)) for b in bcast] + [_ANY] * n_x
    out_specs = [pl.BlockSpec((tm, o[0]), lambda i: (i, 0)) for o in outs]
    out_specs += [pl.BlockSpec(s, lambda i: (0, 0)) for s in accs]
    out_shape = [jax.ShapeDtypeStruct((T, o[2] if len(o) > 2 else o[0]), o[1]) for o in outs]
    out_shape += [jax.ShapeDtypeStruct(s, F32) for s in accs]

    def body(*refs):
        in_refs = refs[:n_r]
        b_refs = refs[n_r:n_r + n_b]
        o_refs = refs[n_r + n_b + n_x:n_r + n_b + n_x + n_o]
        a_refs = refs[n_r + n_b + n_x + n_o:]
        if n_a:
            @pl.when(pl.program_id(0) == 0)
            def _():
                for r in a_refs:
                    r[...] = jnp.zeros(r.shape, F32)
        fn(in_refs, b_refs, o_refs, a_refs)

    res = pl.pallas_call(
        body, name=name, grid=(T // tm,), in_specs=in_specs, out_specs=out_specs, out_shape=out_shape,
        compiler_params=_params(("arbitrary",) if n_a else ("parallel",)),
    )(*[r[0] for r in rows], *bcast, *order)
    return res


def _rsum(x):
    return jnp.sum(x, axis=0, keepdims=True)


def _rms_fwd(x, g, name, after=None):
    def fn(ins, bs, outs, accs):
        xv = ins[0][...]
        r = lax.rsqrt(jnp.mean(xv * xv, axis=-1, keepdims=True) + EPS)
        outs[0][...] = (xv * r * bs[0][...]).astype(BF16)

    return _rowwise(fn, [x], [g], [(D_MODEL, BF16)], tm=512, name=name, after=after)[0]


def _rms_bwd_math(xv, dh, g):
    r = lax.rsqrt(jnp.mean(xv * xv, axis=-1, keepdims=True) + EPS)
    hn = xv * r
    dgh = dh * g
    dx = r * (dgh - hn * jnp.mean(dgh * hn, axis=-1, keepdims=True))
    return dx, _rsum(dh * hn)


def _mm_norm_bwd(dy, w, x, dres, g, name, after=None):
    def epilogue(acc, e_refs, b_refs, o_refs, a_refs):
        dx, dg = _rms_bwd_math(e_refs[0][...], acc, b_refs[0][...])
        dx = dx + e_refs[1][...]
        o_refs[0][...] = dx
        o_refs[1][...] = dx.astype(BF16)
        a_refs[0][...] += dg

    return _mm(dy, w, mode="nt", b_shard=True, out_dtypes=(F32, BF16), extras=(x, dres), bcast=(g,),
               accs=((1, D_MODEL),), ref_epi=epilogue, name=name, after=after)


def _mm_res_norm(a, w, res, g, name):
    def epilogue(acc, e_refs, b_refs, o_refs, a_refs):
        xv = acc + e_refs[0][...]
        o_refs[0][...] = xv
        r = lax.rsqrt(jnp.mean(xv * xv, axis=-1, keepdims=True) + EPS)
        o_refs[1][...] = (xv * r * b_refs[0][...]).astype(BF16)

    return _mm(a, w, mode="nn", out_dtypes=(F32, BF16), extras=(res,), bcast=(g,), ref_epi=epilogue, name=name)


def _mm_final_loss(a, w, res, target, g, name):
    def epilogue(acc, e_refs, b_refs, o_refs, a_refs):
        xv = acc + e_refs[0][...]
        gv = b_refs[0][...]
        r = lax.rsqrt(jnp.mean(xv * xv, axis=-1, keepdims=True) + EPS)
        e = xv * r * gv - e_refs[1][...]
        tok = jnp.mean(e * e, axis=-1, keepdims=True)
        a_refs[0][...] += 0.5 * jnp.sum(tok, axis=0, keepdims=True) * jnp.ones((1, LANES), F32)
        dx, dg = _rms_bwd_math(xv, e * (1.0 / D_MODEL), gv)
        o_refs[0][...] = dx
        o_refs[1][...] = dx.astype(BF16)
        a_refs[1][...] += dg

    return _mm(a, w, mode="nn", out_dtypes=(F32, BF16), extras=(res, target), bcast=(g,),
               accs=((1, LANES), (1, D_MODEL)), ref_epi=epilogue, name=name)


def _relu2(acc):
    r = jnp.maximum(acc, 0.0)
    return r * r, r


def _mlp_fwd(x, h, fetch, tag, finish):
    w_up = fetch(f"mlp{tag}_up", h)
    a, r = _mm(h, w_up, mode="nn", b_shard=True, out_dtypes=(BF16, BF16), epi=_relu2, name=f"mlp{tag}_up")
    w_down = fetch(f"mlp{tag}_down", a)
    return finish(a, w_down, x, f"mlp{tag}_down"), (h, a, r, w_up, w_down)


def _mlp_bwd(x, g, saved, dx, dx_bf, tag, after):
    h, a, r, w_up, w_down = saved
    d_down = _mm(a, dx_bf, mode="tn", out_dtypes=(BF16,), name=f"mlp{tag}_dwdown", after=after)
    dup = _mm(dx_bf, w_down, mode="nt", extras=(r,), out_dtypes=(BF16,),
              epi=lambda acc, rv: (acc * (2.0 * rv.astype(F32)),), name=f"mlp{tag}_dup")
    d_up = _mm(h, dup, mode="tn", o_shard=True, out_dtypes=(BF16,), name=f"mlp{tag}_dwup")
    dx_new, dx_new_bf, dg = _mm_norm_bwd(dup, w_up, x, dx, g, f"mlp{tag}_dh")
    return dx_new, dx_new_bf, dg, d_up, d_down


def _rope_tables(L, B):
    rows = L // GRID_W
    row = np.repeat(np.arange(rows, dtype=np.float32), GRID_W)
    col = np.tile(np.arange(GRID_W, dtype=np.float32), rows)
    inv = (ROPE_THETA ** (-np.arange(HEAD_DIM // 4, dtype=np.float32) / (HEAD_DIM // 4))).astype(np.float32)
    ar, ac = row[:, None] * inv, col[:, None] * inv
    cos = np.concatenate([np.cos(ar), np.cos(ar), np.cos(ac), np.cos(ac)], axis=-1)
    sin = np.concatenate([-np.sin(ar), np.sin(ar), -np.sin(ac), np.sin(ac)], axis=-1)
    return jnp.asarray(np.tile(cos, (B, 1)), F32), jnp.asarray(np.tile(sin, (B, 1)), F32)


def _swap_halves(x):
    lane = lax.broadcasted_iota(jnp.int32, x.shape, 1)
    return jnp.where((lane % 64) < 32, pltpu.roll(x, HEAD_DIM - 32, 1), pltpu.roll(x, 32, 1))


def _qk_prep(qkv, cos, sin, q_g, k_g):
    def fn(ins, bs, outs, accs):
        c, s = ins[1][...], ins[2][...]
        for h in range(N_HEADS + N_KV):
            xv = ins[0][:, h * HEAD_DIM:(h + 1) * HEAD_DIM]
            g = bs[0][...] if h < N_HEADS else bs[1][...]
            r = lax.rsqrt(jnp.mean(xv * xv, axis=-1, keepdims=True) + EPS)
            z = xv * r * g
            y = (z * c + _swap_halves(z) * s).astype(BF16)
            if h < N_HEADS:
                outs[0][:, h * HEAD_DIM:(h + 1) * HEAD_DIM] = y
            else:
                outs[1][:, (h - N_HEADS) * HEAD_DIM:(h - N_HEADS + 1) * HEAD_DIM] = y
        outs[2][...] = ins[0][:, (N_HEADS + N_KV) * HEAD_DIM:].astype(BF16)

    kvw = N_KV * HEAD_DIM
    return _rowwise(fn, [qkv, cos, sin], [q_g, k_g], [(D_MODEL, BF16), (kvw, BF16), (kvw, BF16)], tm=512,
                    name="attn_qk_prep")


def _qk_prep_bwd(qkv, dq, dk, dv, cos, sin, q_g, k_g):
    def fn(ins, bs, outs, accs):
        c, s = ins[4][...], ins[5][...]
        for h in range(N_HEADS + N_KV):
            sl = slice(h * HEAD_DIM, (h + 1) * HEAD_DIM)
            xv = ins[0][:, sl]
            if h < N_HEADS:
                g, dy, acc = bs[0][...], ins[1][:, sl], accs[0]
            else:
                ks = slice((h - N_HEADS) * HEAD_DIM, (h - N_HEADS + 1) * HEAD_DIM)
                g, dy, acc = bs[1][...], ins[2][:, ks], accs[1]
            r = lax.rsqrt(jnp.mean(xv * xv, axis=-1, keepdims=True) + EPS)
            xn = xv * r
            dz = dy * c - _swap_halves(dy) * s
            acc[...] += _rsum(dz * xn)
            dxn = dz * g
            outs[0][:, sl] = (r * (dxn - xn * jnp.mean(dxn * xn, axis=-1, keepdims=True))).astype(BF16)
        outs[0][:, (N_HEADS + N_KV) * HEAD_DIM:] = ins[3][...].astype(BF16)

    return _rowwise(fn, [qkv, dq, dk, dv, cos, sin], [q_g, k_g], [(qkv.shape[1], BF16)],
                    [(1, HEAD_DIM), (1, HEAD_DIM)], tm=512, name="attn_qk_prep_bwd")


_EXP2_SCALE = SCALE * math.log2(math.e)


def _exp_rows(q, k):
    s = lax.dot_general(q, k, _NT, preferred_element_type=F32)
    p = jnp.exp2((s - jnp.max(s, axis=-1, keepdims=True)) * _EXP2_SCALE)
    return p, jnp.sum(p, axis=-1, keepdims=True)


def _attn_fwd(q, k, v, B, L, tq=2048, sub=256):
    tq = min(tq, L)
    sub = min(sub, tq)
    nq = L // tq

    def body(q_ref, k_ref, v_ref, o_ref):
        kv, vv = k_ref[...], v_ref[...]
        for c in range(tq // sub):
            rows = slice(c * sub, (c + 1) * sub)
            p, l = _exp_rows(q_ref[rows, :], kv)
            o = jnp.dot(p.astype(BF16), vv, preferred_element_type=F32)
            o_ref[rows, :] = (o * (1.0 / l)).astype(o_ref.dtype)

    return pl.pallas_call(
        body, name="attn_fwd", grid=(B, N_HEADS, nq),
        in_specs=[pl.BlockSpec((tq, HEAD_DIM), lambda b, h, i: (b * nq + i, h)),
                  pl.BlockSpec((L, HEAD_DIM), lambda b, h, i: (b, h // GROUP)),
                  pl.BlockSpec((L, HEAD_DIM), lambda b, h, i: (b, h // GROUP))],
        out_specs=pl.BlockSpec((tq, HEAD_DIM), lambda b, h, i: (b * nq + i, h)),
        out_shape=jax.ShapeDtypeStruct((B * L, D_MODEL), BF16),
        compiler_params=_params(("parallel", "parallel", "parallel")),
    )(q, k, v)


def _attn_bwd(q, k, v, o, do, B, L, tq=2048, sub=512):
    tq = min(tq, L)
    sub = min(sub, tq)
    nq = L // tq

    def body(q_ref, k_ref, v_ref, o_ref, do_ref, dq_ref, dk_ref, dv_ref):
        @pl.when((pl.program_id(2) == 0) & (pl.program_id(3) == 0))
        def _():
            dk_ref[...] = jnp.zeros(dk_ref.shape, F32)
            dv_ref[...] = jnp.zeros(dv_ref.shape, F32)

        kv, vv = k_ref[...], v_ref[...]
        ps, es, dos, qs = [], [], [], []
        for c in range(tq // sub):
            rows = slice(c * sub, (c + 1) * sub)
            qc, doc = q_ref[rows, :], do_ref[rows, :]
            p, l = _exp_rows(qc, kv)
            inv = 1.0 / l
            dp = lax.dot_general(doc, vv, _NT, preferred_element_type=F32)
            delta = jnp.sum(doc.astype(F32) * o_ref[rows, :].astype(F32), axis=-1, keepdims=True)
            e = (p * (dp - delta)).astype(BF16)
            dq_ref[rows, :] = jnp.dot(e, kv, preferred_element_type=F32) * (inv * SCALE)
            ps.append(p.astype(BF16))
            es.append(e)
            dos.append((doc.astype(F32) * inv).astype(BF16))
            qs.append((qc.astype(F32) * (inv * SCALE)).astype(BF16))
        cat = lambda xs: xs[0] if len(xs) == 1 else jnp.concatenate(xs, axis=0)
        dv_ref[...] += lax.dot_general(cat(ps), cat(dos), _TN, preferred_element_type=F32)
        dk_ref[...] += lax.dot_general(cat(es), cat(qs), _TN, preferred_element_type=F32)

    qmap = lambda b, kh, g, i: (b * nq + i, kh * GROUP + g)
    kmap = lambda b, kh, g, i: (b, kh)
    kvw = N_KV * HEAD_DIM
    return pl.pallas_call(
        body, name="attn_bwd", grid=(B, N_KV, GROUP, nq),
        in_specs=[pl.BlockSpec((tq, HEAD_DIM), qmap), pl.BlockSpec((L, HEAD_DIM), kmap),
                  pl.BlockSpec((L, HEAD_DIM), kmap), pl.BlockSpec((tq, HEAD_DIM), qmap),
                  pl.BlockSpec((tq, HEAD_DIM), qmap)],
        out_specs=[pl.BlockSpec((tq, HEAD_DIM), qmap), pl.BlockSpec((L, HEAD_DIM), kmap),
                   pl.BlockSpec((L, HEAD_DIM), kmap)],
        out_shape=[jax.ShapeDtypeStruct((B * L, D_MODEL), F32), jax.ShapeDtypeStruct((B * L, kvw), F32),
                   jax.ShapeDtypeStruct((B * L, kvw), F32)],
        compiler_params=_params(("parallel", "parallel", "arbitrary", "arbitrary")),
    )(q, k, v, o, do)


def _conv_shift(x, t, L, k):
    if k == 2:
        return x
    if k < 2:
        return jnp.where(t >= 2 - k, pltpu.roll(x, 2 - k, 0), 0.0)
    return jnp.where(t < L - (k - 2), pltpu.roll(x, L - (k - 2), 0), 0.0)


def _conv_apply(x, w_ref, L):
    t = lax.broadcasted_iota(jnp.int32, x.shape, 0)
    acc = w_ref[4:5, :] + w_ref[2:3, :] * x
    for k in (0, 1, 3):
        acc = acc + w_ref[k:k + 1, :] * _conv_shift(x, t, L, k)
    return acc


def _conv_bwd(z, g, wb, dz, B, L, tc=256, after=None):
    noff = D_MODEL // tc
    order = _after_operand(after)

    def body(z_ref, g_ref, w_ref, dz_in, *rest):
        dx_ref, dw_ref = rest[len(order):]

        @pl.when(pl.program_id(1) == 0)
        def _():
            dw_ref[...] = jnp.zeros(dw_ref.shape, F32)

        x, gv = z_ref[...], g_ref[...]
        t = lax.broadcasted_iota(jnp.int32, x.shape, 0)
        dx = w_ref[2:3, :] * gv
        for k in (0, 1, 3):
            dx = dx + w_ref[k:k + 1, :] * _conv_shift(gv, t, L, 4 - k)
        dx_ref[...] = dx.astype(BF16)
        for k in range(4):
            dw_ref[k:k + 1, :] += _rsum(_conv_shift(x, t, L, k) * gv)
        dw_ref[4:5, :] += _rsum(gv)

    return pl.pallas_call(
        body, name="rg_conv_bwd", grid=(noff, B),
        in_specs=[pl.BlockSpec((L, tc), lambda j, b: (b, noff + j)), pl.BlockSpec((L, tc), lambda j, b: (b, j)),
                  pl.BlockSpec((SUBLANES, tc), lambda j, b: (0, j)), _ANY] + [_ANY] * len(order),
        out_specs=[pl.BlockSpec((L, tc), lambda j, b: (b, noff + j)),
                   pl.BlockSpec((SUBLANES, tc), lambda j, b: (0, j))],
        out_shape=[jax.ShapeDtypeStruct(dz.shape, dz.dtype), jax.ShapeDtypeStruct((SUBLANES, D_MODEL), F32)],
        input_output_aliases={3: 0},
        compiler_params=_params(("parallel", "arbitrary")),
    )(z, g, wb, dz, *order)


def _softplus(x):
    return jnp.maximum(x, 0.0) + jnp.log1p(jnp.exp(-jnp.abs(x)))


_ROW_BA, _ROW_BX, _ROW_LAM = 0, 2, 4


def _gate_math(xb, pre, vec_ref, d, sl):
    pa = pre[:, (2 * d) * LRU_BW:(2 * d + 1) * LRU_BW] + vec_ref[_ROW_BA + d:_ROW_BA + d + 1, sl]
    px = pre[:, (2 * d + 1) * LRU_BW:(2 * d + 2) * LRU_BW] + vec_ref[_ROW_BX + d:_ROW_BX + d + 1, sl]
    r = 0.5 * jnp.tanh(0.5 * pa) + 0.5
    i = 0.5 * jnp.tanh(0.5 * px) + 0.5
    slope = (-RG_C) * _softplus(-vec_ref[_ROW_LAM + d:_ROW_LAM + d + 1, sl])
    log_a = r * slope
    a = jnp.exp(log_a)
    om = -jnp.tanh(log_a) * (1.0 + a * a)
    rs = lax.rsqrt(om)
    mult = jnp.where(om > 0.0, om * rs, 0.0)
    return a, mult * (i * xb), (r, i, slope, om, mult, rs)


def _gate_bwd(rec, du_f, da_f, du_b, da_b, wcat, gvec):
    def fn(ins, bs, outs, accs):
        for blk in range(LRU_BLOCKS):
            sl = slice(blk * LRU_BW, (blk + 1) * LRU_BW)
            xb = ins[0][:, sl]
            xb16 = xb.astype(BF16)
            w = bs[0][sl, :]
            pre = jnp.dot(xb16, w, preferred_element_type=F32)
            dx = jnp.zeros_like(xb)
            dpre = []
            for d in range(2):
                a, _, (r, i, slope, om, mult, rs) = _gate_math(xb, pre, bs[1], d, sl)
                du, da = ins[1 + 2 * d][:, sl], ins[2 + 2 * d][:, sl]
                t = du * xb
                d_i = t * mult
                dx = dx + du * mult * i
                dlog = da * a - (t * i) * ((1.0 - om) * rs)
                d_r = dlog * slope
                d_sp = _rsum(dlog * r) * (-RG_C)
                lam = bs[1][_ROW_LAM + d:_ROW_LAM + d + 1, sl]
                accs[2][_ROW_LAM + d:_ROW_LAM + d + 1, sl] += d_sp * (-jax.nn.sigmoid(-lam))
                dpa = d_r * r * (1.0 - r)
                dpx = d_i * i * (1.0 - i)
                accs[2][_ROW_BA + d:_ROW_BA + d + 1, sl] += _rsum(dpa)
                accs[2][_ROW_BX + d:_ROW_BX + d + 1, sl] += _rsum(dpx)
                dpre += [dpa, dpx]
            dpre = jnp.concatenate(dpre, axis=1).astype(BF16)
            dw = lax.dot_general(xb16, dpre, _TN, preferred_element_type=F32)
            for d in range(2):
                rows = slice(d * D_MODEL + blk * LRU_BW, d * D_MODEL + (blk + 1) * LRU_BW)
                accs[0][rows, :] += dw[:, (2 * d) * LRU_BW:(2 * d + 1) * LRU_BW]
                accs[1][rows, :] += dw[:, (2 * d + 1) * LRU_BW:(2 * d + 2) * LRU_BW]
            outs[0][:, sl] = dx + lax.dot_general(dpre, w, _NT, preferred_element_type=F32)

    gate_shape = (2 * D_MODEL, LRU_BW)
    return _rowwise(fn, [rec, du_f, da_f, du_b, da_b], [wcat, gvec], [(D_MODEL, F32)],
                    [gate_shape, gate_shape, (SUBLANES, D_MODEL)], tm=512, name="rg_gate_bwd")


def _as_time_blocks(x):
    return x.reshape(x.shape[0] // SUBLANES, SUBLANES, x.shape[1])


def _scan_call(body, ins, n_out, B, L, tc, name):
    nb = L // SUBLANES
    spec = pl.BlockSpec((nb, SUBLANES, tc), lambda b, j: (b, 0, j))
    T = ins[0].shape[0]
    outs = pl.pallas_call(
        functools.partial(body, nb), name=name, grid=(B, D_MODEL // tc),
        in_specs=[spec] * len(ins), out_specs=[spec] * n_out,
        out_shape=[jax.ShapeDtypeStruct((T // SUBLANES, SUBLANES, D_MODEL), F32)] * n_out,
        compiler_params=_params(("parallel", "parallel")),
    )(*[_as_time_blocks(x) for x in ins])
    return [o.reshape(T, D_MODEL) for o in outs]


def _block_scan(A, U, reverse):
    row = lax.broadcasted_iota(jnp.int32, A.shape, 0)
    for s in (1, 2, 4):
        shift = SUBLANES - s if reverse else s
        valid = (row < SUBLANES - s) if reverse else (row >= s)
        a_sh = jnp.where(valid, pltpu.roll(A, shift, 0), 1.0)
        u_sh = jnp.where(valid, pltpu.roll(U, shift, 0), 0.0)
        U = A * u_sh + U
        A = A * a_sh
    return A, U


_LAST = SUBLANES - 1
SCAN_UNROLL = 8


def _loop_blocks(nb, step, init):
    def group(g, carry):
        for k in range(SCAN_UNROLL):
            carry = step(g * SCAN_UNROLL + k, carry)
        return carry

    return lax.fori_loop(0, nb // SCAN_UNROLL, group, init)


def _scan_bwd(dy, a_f, h_f, a_b, h_b, B, L, tc=256):
    def body(nb, dy_r, af, hf, ab, hb, duf, daf, dub, dab):
        def step(i, carry):
            c1, c2 = carry
            ir = nb - 1 - i
            row = lax.broadcasted_iota(jnp.int32, (SUBLANES, tc), 0)
            a_up = jnp.where(row == _LAST, af[jnp.minimum(ir + 1, nb - 1), :1, :], pltpu.roll(af[ir], _LAST, 0))
            p, lam = _block_scan(a_up, dy_r[ir], True)
            lam = lam + p * c1
            before = hf[jnp.maximum(ir - 1, 0), _LAST:, :] * (ir > 0).astype(F32)
            duf[ir] = lam
            daf[ir] = lam * jnp.where(row == 0, before, pltpu.roll(hf[ir], 1, 0))
            a_dn = jnp.where(row == 0, ab[jnp.maximum(i - 1, 0), _LAST:, :], pltpu.roll(ab[i], 1, 0))
            p2, lam2 = _block_scan(a_dn, dy_r[i], False)
            lam2 = lam2 + p2 * c2
            after = hb[jnp.minimum(i + 1, nb - 1), :1, :] * (i < nb - 1).astype(F32)
            dub[i] = lam2
            dab[i] = lam2 * jnp.where(row == _LAST, after, pltpu.roll(hb[i], _LAST, 0))
            return lam[:1, :], lam2[_LAST:, :]

        zero = jnp.zeros((1, tc), F32)
        _loop_blocks(nb, step, (zero, zero))

    return _scan_call(body, [dy, a_f, h_f, a_b, h_b], 4, B, L, tc, "rg_scan_bwd")


_GELU_C = math.sqrt(2.0 / math.pi)


def _gelu_parts(x):
    th = jnp.tanh(_GELU_C * (x + 0.044715 * x * x * x))
    return 0.5 * x * (1.0 + th), th


def _mm_gated_out_bwd(dx, w_out, h_f, h_b, z, name, after=None):
    def epilogue(acc, e_refs, b_refs, o_refs, a_refs):
        x = e_refs[2][...]
        gl, th = _gelu_parts(x)
        dgl = 0.5 * (1.0 + th) + 0.5 * x * (1.0 - th * th) * (_GELU_C * (1.0 + 3.0 * 0.044715 * x * x))
        o_refs[0][...] = acc * gl
        o_refs[1][...] = (acc * (e_refs[0][...] + e_refs[1][...]) * dgl).astype(BF16)

    return _mm(dx, w_out, mode="nt", out_dtypes=(F32, BF16), out_cols=(D_MODEL, 2 * D_MODEL), extras=(h_f, h_b, z),
               ref_epi=epilogue, name=name, after=after)


def _row_block(i):
    return pl.ds(pl.multiple_of(i * SUBLANES, SUBLANES), SUBLANES)


def _rg_mix_fwd(z, conv_wb, wcat, gvec, B, L):
    nb = L // SUBLANES
    n_g = D_MODEL // LRU_BW

    def body(zg_ref, zr_ref, cw_ref, w_ref, gv_ref, rec_ref, af_s, ab_s, hf_ref, hb_ref, yg_ref, uf_s, ub_s):
        rec = _conv_apply(zr_ref[...], cw_ref, L)
        rec_ref[...] = rec
        pre = jnp.dot(rec.astype(BF16), w_ref[...], preferred_element_type=F32)
        for d, (a_s, u_s) in enumerate(((af_s, uf_s), (ab_s, ub_s))):
            a, u, _ = _gate_math(rec, pre, gv_ref, d, slice(None))
            a_s[...] = a
            u_s[...] = u

        def step(i, carry):
            c1, c2 = carry
            rows, rows_b = _row_block(i), _row_block(nb - 1 - i)
            p, h = _block_scan(af_s[rows, :], uf_s[rows, :], False)
            h = h + p * c1
            hf_ref[rows, :] = h
            p2, h2 = _block_scan(ab_s[rows_b, :], ub_s[rows_b, :], True)
            h2 = h2 + p2 * c2
            hb_ref[rows_b, :] = h2
            return h[_LAST:, :], h2[:1, :]

        zero = jnp.zeros((1, LRU_BW), F32)
        _loop_blocks(nb, step, (zero, zero))
        gl, _ = _gelu_parts(zg_ref[...])
        yg_ref[...] = ((hf_ref[...] + hb_ref[...]) * gl).astype(BF16)

    seq = lambda off: pl.BlockSpec((L, LRU_BW), lambda b, g: (b, off + g))
    vec = pl.BlockSpec((SUBLANES, LRU_BW), lambda b, g: (0, g))
    T = B * L
    return pl.pallas_call(
        body, name="rg_mix", grid=(B, n_g),
        in_specs=[seq(0), seq(n_g), vec, pl.BlockSpec((LRU_BW, 4 * LRU_BW), lambda b, g: (g, 0)), vec],
        out_specs=[seq(0)] * 6,
        out_shape=[jax.ShapeDtypeStruct((T, D_MODEL), F32)] * 5 + [jax.ShapeDtypeStruct((T, D_MODEL), BF16)],
        scratch_shapes=[pltpu.VMEM((L, LRU_BW), F32)] * 2,
        compiler_params=_params(("parallel", "parallel")),
    )(z, z, conv_wb, wcat, gvec)


def _make_wcat(w_a, w_x):
    g = jnp.stack([w_a[0, 0], w_x[0, 0], w_a[0, 1], w_x[0, 1]])
    return jnp.transpose(g, (1, 2, 0, 3)).reshape(D_MODEL, 4 * LRU_BW)


def _rows_at(part, first):
    return jnp.pad(part, ((first, SUBLANES - first - part.shape[0]), (0, 0)))


def _qk_slot(q_g, k_g):
    wide = lambda v, at: jnp.pad(v, ((0, SUBLANES - 1), (at, D_MODEL - at - HEAD_DIM)))
    return wide(q_g, 0) + wide(k_g, HEAD_DIM)


def _local_step(x, target, P, fetch, emit, B, L, after=None):
    g_mix, g_mlp = P["norm_mix_g"], P["norm_mlp_g"]
    h0 = _rms_fwd(x, g_mix[0:1], "rg_norm", after=after)
    w_in, conv_wb, wcat, gvec = fetch("rg", h0)
    z = _mm(h0, w_in, mode="nn", b_shard=True, name="rg_in")
    rec, a_f, a_b, h_f, h_b, yg = _rg_mix_fwd(z, conv_wb, wcat, gvec, B, L)
    w_out = fetch("rg_out", yg)
    x1, h1 = _mm_res_norm(yg, w_out, x, g_mlp[0:1], "rg_out")
    (x2, h3), mlp0 = _mlp_fwd(x1, h1, fetch, 0, lambda a, w, res, name: _mm_res_norm(a, w, res, g_mix[1:2], name))
    w_qkv, w_o = fetch("att", h3)
    qkv = _mm(h3, w_qkv, mode="nn", b_shard=True, name="attn_qkv")
    cos, sin = _rope_tables(L, B)
    qh, kh, vh = _qk_prep(qkv, cos, sin, P["q_g"], P["k_g"])
    o = _attn_fwd(qh, kh, vh, B, L)
    x3, h4 = _mm_res_norm(o, w_o, x2, g_mlp[1:2], "attn_out")
    (dx4, dx4_bf, loss_acc, d_final_g), mlp1 = _mlp_fwd(
        x3, h4, fetch, 1, lambda a, w, res, name: _mm_final_loss(a, w, res, target, P["final_g"], name))

    dx3, dx3_bf, dg_mlp1, d_up1, d_down1 = _mlp_bwd(x3, g_mlp[1:2], mlp1, dx4, dx4_bf, 1, None)
    tok = emit("mlp1", [d_up1, d_down1])
    d_wo = _mm(o, dx3_bf, mode="tn", out_dtypes=(BF16,), name="attn_dwo", after=tok)
    do = _mm(dx3_bf, w_o, mode="nt", out_dtypes=(BF16,), name="attn_do")
    dq, dk, dv = _attn_bwd(qh, kh, vh, o, do, B, L)
    dqkv, dq_g, dk_g = _qk_prep_bwd(qkv, dq, dk, dv, cos, sin, P["q_g"], P["k_g"])
    d_wqkv = _mm(h3, dqkv, mode="tn", o_shard=True, out_dtypes=(BF16,), name="attn_dwqkv")
    tok = emit("att", [d_wqkv, d_wo])
    dx2, dx2_bf, dg_mix1 = _mm_norm_bwd(dqkv, w_qkv, x2, dx3, g_mix[1:2], "attn_dh", after=tok)
    tok = emit("point_attn_done", [dx2_bf])
    dx1, dx1_bf, dg_mlp0, d_up0, d_down0 = _mlp_bwd(x1, g_mlp[0:1], mlp0, dx2, dx2_bf, 0, tok)
    d_wout = _mm(yg, dx1_bf, mode="tn", out_dtypes=(BF16,), name="rg_dwout")
    tok = emit("mlp0", [d_up0, d_down0, d_wout])
    dy, dgate = _mm_gated_out_bwd(dx1_bf, w_out, h_f, h_b, z, "rg_dyg", after=tok)
    du_f, da_f, du_b, da_b = _scan_bwd(dy, a_f, h_f, a_b, h_b, B, L)
    drec_c, d_wa, d_wx, d_gvec = _gate_bwd(rec, du_f, da_f, du_b, da_b, wcat, gvec)
    tok = emit("gates", [d_wa, d_wx])
    dz, d_convwb = _conv_bwd(z, drec_c, conv_wb, dgate, B, L, after=tok)
    tok = emit("point_mix_done", [dz])
    d_win = _mm(h0, dz, mode="tn", o_shard=True, out_dtypes=(BF16,), name="rg_dwin", after=tok)
    tok = emit("rg_in", [d_win])
    grad_x, _, dg_mix0 = _mm_norm_bwd(dz, w_in, x, dx1, g_mix[0:1], "rg_dh", after=tok)

    norms = (_rows_at(dg_mix0, 0) + _rows_at(dg_mix1, 1) + _rows_at(dg_mlp0, 2) + _rows_at(dg_mlp1, 3)
             + _rows_at(d_final_g, 4)
             + jnp.pad(loss_acc, ((LOSS_ROW, SUBLANES - 1 - LOSS_ROW), (0, D_MODEL - LANES))))
    vec = jnp.concatenate([norms, d_convwb, d_gvec, _qk_slot(dq_g, dk_g)], axis=0)
    return grad_x, vec


_MESH = pl.DeviceIdType.MESH


def _place():
    x, y, c = lax.axis_index("x"), lax.axis_index("y"), lax.axis_index("c")
    peers = [((1 - x) if j & 2 else x, (1 - y) if j & 1 else y) for j in (1, 2, 3)]
    return x, y, c, peers


def _sum_leading(slots, name):
    def body(s_ref, o_ref):
        acc = s_ref[0]
        for d in range(1, slots.shape[0]):
            acc = acc + s_ref[d]
        o_ref[...] = acc

    return pl.pallas_call(body, name=name, out_shape=jax.ShapeDtypeStruct(slots.shape[1:], slots.dtype))(slots)


_HBM = pl.BlockSpec(memory_space=pltpu.HBM)
_SEM = pl.BlockSpec(memory_space=pltpu.SEMAPHORE)
_EFFECT = pltpu.SideEffectType.DATAFLOW_SIDE_EFFECTING


_COPIES = dict(gather=N_CHIPS - 1, scatter=N_CHIPS - 1, swap=1, spread=N_DEVICES - 1,
               gather_half=N_CHIPS - 1, share_half=N_CHIPS - 1)


def _split_copies(kind, srcs, lands, send, recv):
    x, y, c, peers = _place()
    me = 2 * x + y
    per = _COPIES[kind]
    out = []
    for a in range(len(lands)):
        for j in range(per):
            if kind == "swap":
                src, there, here, dev = srcs[a], lands[a], lands[a], (x, y, 1 - c)
            elif kind == "spread":
                k = j + 1
                dev = ((1 - x) if k & 4 else x, (1 - y) if k & 2 else y, (1 - c) if k & 1 else c)
                mine = lands[a].at[4 * x + 2 * y + c]
                src, there, here = mine, mine, lands[a].at[4 * dev[0] + 2 * dev[1] + dev[2]]
            else:
                px, py = peers[j]
                dev = (px, py, c)
                if kind == "gather":
                    src, there, here = lands[a].at[me], lands[a].at[me], lands[a].at[2 * px + py]
                elif kind in ("gather_half", "share_half"):
                    half = lands[a].shape[1] // 2
                    mine, other = pl.ds(c * half, half), pl.ds((1 - c) * half, half)
                    if kind == "gather_half":
                        src = there = lands[a].at[me, mine]
                        here = lands[a].at[2 * px + py, mine]
                    else:
                        src = there = lands[a].at[2 * px + py, mine]
                        here = lands[a].at[2 * px + py, other]
                        dev = (x, y, 1 - c)
                else:
                    src, there, here = srcs[a].at[2 * px + py], lands[a].at[j], lands[a].at[j]
            mk = functools.partial(
                pltpu.make_async_remote_copy, src_ref=src, send_sem=send.at[per * a + j],
                recv_sem=recv.at[per * a + j], device_id=dev, device_id_type=_MESH)
            out.append((functools.partial(mk, dst_ref=there), functools.partial(mk, dst_ref=here)))
    return out


_CORE_PAIR = ("swap", "share_half")
CORE_PAIR_BARRIER_ID = 0


def _entry_params(kind):
    collective = dict(collective_id=CORE_PAIR_BARRIER_ID) if kind in _CORE_PAIR else {}
    return pltpu.CompilerParams(has_side_effects=_EFFECT, **collective)


def _entry_handshake(kind):
    if kind in _CORE_PAIR:
        barrier = pltpu.get_barrier_semaphore()
        x, y, c, _ = _place()
        pl.semaphore_signal(barrier, inc=1, device_id=(x, y, 1 - c), device_id_type=_MESH)
        pl.semaphore_wait(barrier, 1)


def _exchange_start(kind, srcs, lands, name, after=None):
    arrays = list(srcs) + list(lands)
    n_s, n, n_all = len(srcs), len(lands), len(srcs) + len(lands)
    n_sem = _COPIES[kind] * n
    order = _after_operand(after)
    n_x = len(order)

    def body(*refs):
        _entry_handshake(kind)
        send, recv = refs[n_all + n_x], refs[n_all + n_x + 1]
        token = refs[-1]
        for started, _ in _split_copies(kind, refs[:n_s], refs[n_s:n_all], send, recv):
            started().start()
        token[...] = jnp.zeros(token.shape, F32)

    res = pl.pallas_call(
        body, name=name,
        out_shape=(pltpu.SemaphoreType.DMA((n_sem,)), pltpu.SemaphoreType.DMA((n_sem,)),
                   *[pltpu.HBM(a.shape, a.dtype) for a in arrays], jax.ShapeDtypeStruct((SUBLANES, LANES), F32)),
        in_specs=[_HBM] * n_all + [_ANY] * n_x,
        out_specs=(_SEM, _SEM, *[_HBM] * n_all, pl.BlockSpec(memory_space=pltpu.VMEM)),
        input_output_aliases={i: 2 + i for i in range(n_all)},
        compiler_params=_entry_params(kind),
    )(*[pltpu.with_memory_space_constraint(a, pltpu.HBM) for a in arrays], *order)
    return (res[0], res[1], res[2:2 + n_s], res[2 + n_s:2 + n_all]), res[-1]


def _gather_start_groups(land_groups, name, after=None, kind="gather"):
    arrays = [a for group in land_groups for a in group]
    n_all, n_g = len(arrays), len(land_groups)
    order = _after_operand(after)
    n_x = len(order)

    def body(*refs):
        _entry_handshake(kind)
        first = 0
        for gi, group in enumerate(land_groups):
            send, recv = refs[n_all + n_x + 2 * gi], refs[n_all + n_x + 2 * gi + 1]
            for started, _ in _split_copies(kind, [], refs[first:first + len(group)], send, recv):
                started().start()
            first += len(group)
        refs[-1][...] = jnp.zeros(refs[-1].shape, F32)

    sems = [pltpu.SemaphoreType.DMA((_COPIES[kind] * len(group),)) for group in land_groups for _ in range(2)]
    res = pl.pallas_call(
        body, name=name,
        out_shape=(*sems, *[pltpu.HBM(a.shape, a.dtype) for a in arrays], jax.ShapeDtypeStruct((SUBLANES, LANES), F32)),
        in_specs=[_HBM] * n_all + [_ANY] * n_x,
        out_specs=(*[_SEM] * (2 * n_g), *[_HBM] * n_all, pl.BlockSpec(memory_space=pltpu.VMEM)),
        input_output_aliases={i: 2 * n_g + i for i in range(n_all)},
        compiler_params=_entry_params(kind),
    )(*[pltpu.with_memory_space_constraint(a, pltpu.HBM) for a in arrays], *order)
    handles, first = [], 2 * n_g
    for gi, group in enumerate(land_groups):
        handles.append((res[2 * gi], res[2 * gi + 1], [], res[first:first + len(group)]))
        first += len(group)
    return handles, res[-1]


def _exchange_wait(kind, handle, after, name):
    send, recv, srcs, lands = handle
    arrays = list(srcs) + list(lands)
    n_s, n_all = len(srcs), len(arrays)
    order = list(after) if isinstance(after, (list, tuple)) else [after]

    def body(*refs):
        for started, landing in _split_copies(kind, refs[:n_s], refs[n_s:n_all], refs[n_all], refs[n_all + 1]):
            started().wait_send()
            landing().wait_recv()

    res = pl.pallas_call(
        body, name=name, out_shape=[pltpu.HBM(a.shape, a.dtype) for a in arrays],
        in_specs=[_HBM] * n_all + [_SEM, _SEM] + [_ANY] * len(order), out_specs=[_HBM] * n_all,
        input_output_aliases={i: i for i in range(n_all)},
        compiler_params=pltpu.CompilerParams(has_side_effects=_EFFECT),
    )(*arrays, send, recv, *order)
    return res[:n_s], res[n_s:]


def _index_operand(i):
    return jnp.reshape(i, (1,)).astype(jnp.int32)


def _cast_into_slot(src, row0, rows, me, dtype, name, after=None, add=None, n_slots=N_CHIPS):
    cols = src.shape[1]
    tm = min(512, rows)
    order = _after_operand(after)
    terms = [src] + ([] if add is None else [add])

    def body(me_ref, *rest):
        val = rest[0][...]
        if add is not None:
            val = val + rest[1][...]
        rest[-1][...] = val.astype(dtype)

    return pl.pallas_call(
        body, name=name,
        grid_spec=pltpu.PrefetchScalarGridSpec(
            num_scalar_prefetch=1, grid=(rows // tm,),
            in_specs=[pl.BlockSpec((tm, cols), lambda i, me_ref: (i + row0 // tm, 0))] * len(terms)
            + [_ANY] * len(order),
            out_specs=pl.BlockSpec((None, tm, cols), lambda i, me_ref: (me_ref[0], i, 0))),
        out_shape=jax.ShapeDtypeStruct((n_slots, rows, cols), dtype), compiler_params=_params(("parallel",)),
    )(_index_operand(me), *terms, *order)


def _sum_slots(mine, r, me, name):
    _, rows, cols = r.shape
    tm = min(512, rows)

    def body(me_ref, own_ref, r_ref, o_ref):
        o_ref[...] = ((own_ref[...].astype(F32) + r_ref[0].astype(F32)) + r_ref[1].astype(F32)) + r_ref[2].astype(F32)

    return pl.pallas_call(
        body, name=name,
        grid_spec=pltpu.PrefetchScalarGridSpec(
            num_scalar_prefetch=1, grid=(rows // tm,),
            in_specs=[pl.BlockSpec((None, tm, cols), lambda i, me_ref: (me_ref[0], i, 0)),
                      pl.BlockSpec((N_CHIPS - 1, tm, cols), lambda i, me_ref: (0, i, 0))],
            out_specs=pl.BlockSpec((tm, cols), lambda i, me_ref: (i, 0))),
        out_shape=jax.ShapeDtypeStruct((rows, cols), F32), compiler_params=_params(("parallel",)),
    )(_index_operand(me), mine, r)


def _adamw(w, m, v, ps, qs, name):
    rows, cols = w.shape
    seg_rows = ps[0].shape[0]
    tm = min(512, seg_rows)
    while seg_rows % tm:
        tm -= SUBLANES
    per, n_seg = seg_rows // tm, len(ps)
    parts = list(ps) + ([] if qs is None else list(qs))

    def body(w_ref, m_ref, v_ref, *rest):
        g_refs, outs = rest[:len(parts)], rest[len(parts):]
        grad = lambda s: g_refs[s][...] if qs is None else g_refs[s][...] + g_refs[n_seg + s][...]
        g = grad(0)
        for s in range(1, n_seg):
            g = jnp.where(pl.program_id(0) >= s * per, grad(s), g)
        m1 = ADAM_B1 * m_ref[...] + (1.0 - ADAM_B1) * g
        v1 = ADAM_B2 * v_ref[...] + (1.0 - ADAM_B2) * (g * g)
        m_hat = m1 / (1.0 - ADAM_B1 ** ADAM_STEP)
        v_hat = v1 / (1.0 - ADAM_B2 ** ADAM_STEP)
        outs[0][...] = g
        outs[1][...] = (-ADAM_LR) * (m_hat / (jnp.sqrt(v_hat) + ADAM_EPS) + ADAM_WD * w_ref[...])
        outs[2][...] = m1
        outs[3][...] = v1

    row_spec = pl.BlockSpec((tm, cols), lambda i: (i, 0))
    seg_spec = lambda s: pl.BlockSpec((tm, cols), lambda i: (jnp.clip(i - s * per, 0, per - 1), 0))
    return pl.pallas_call(
        body, name=name, grid=(rows // tm,),
        in_specs=[row_spec] * 3 + [seg_spec(s) for s in range(n_seg)] * (1 if qs is None else 2),
        out_specs=[row_spec] * 4, out_shape=[jax.ShapeDtypeStruct((rows, cols), F32)] * 4,
        compiler_params=_params(("arbitrary",)),
    )(w, m, v, *parts)


def _put_cols(shard, me):
    full = jnp.zeros((shard.shape[0], D_MODEL), F32)
    return lax.dynamic_update_slice(full, shard, (0, me * (D_MODEL // N_CHIPS)))


def _gate_vec_slot(b_a, b_x, lam):
    return _rows_at(b_a, _ROW_BA) + _rows_at(b_x, _ROW_BX) + _rows_at(lam, _ROW_LAM)


def _pack_vec(p, me):
    return jnp.concatenate([
        _rows_at(p["norm_mix_g"], 0) + _rows_at(p["norm_mlp_g"], 2) + _rows_at(p["final_g"][None], 4),
        _rows_at(_put_cols(p["rg_conv_w"][0, :, 0, :], me), 0) + _rows_at(p["rg_conv_b"], 4),
        _gate_vec_slot(_put_cols(p["rg_b_a"][0], me), _put_cols(p["rg_b_x"][0], me), _put_cols(p["rg_lam"][0], me)),
        _qk_slot(p["at_q_g"], p["at_k_g"]),
    ], axis=0)


def _unpack_vec(r, me):
    def cols(rows):
        return lax.dynamic_slice(rows, (0, me * (D_MODEL // N_CHIPS)), (rows.shape[0], D_MODEL // N_CHIPS))

    gate = r[16:24]
    return dict(
        norm_mix_g=r[0:2], norm_mlp_g=r[2:4], final_g=r[4], rg_conv_w=cols(r[8:12])[None, :, None, :],
        rg_conv_b=r[12:13], rg_b_a=cols(gate[_ROW_BA:_ROW_BA + 2])[None], rg_b_x=cols(gate[_ROW_BX:_ROW_BX + 2])[None],
        rg_lam=cols(gate[_ROW_LAM:_ROW_LAM + 2])[None], at_q_g=r[24:25, 0:HEAD_DIM],
        at_k_g=r[24:25, HEAD_DIM:2 * HEAD_DIM])


_WEIGHTS = ['norm_mix_g', 'norm_mlp_g', 'rg_w_in', 'rg_conv_w', 'rg_conv_b', 'rg_w_a', 'rg_b_a', 'rg_w_x', 'rg_b_x',
            'rg_lam', 'rg_w_out', 'at_w_qkv', 'at_q_g', 'at_k_g', 'at_w_o', 'mlp_w_up', 'mlp_w_down', 'final_g']
_BIG = dict(rg_w_in=["rg_w_in"], rg_w_out=["rg_w_out"], at_w_qkv=["at_w_qkv"], at_w_o=["at_w_o"],
            mlp_w_up=["up0", "up1"], mlp_w_down=["down0", "down1"])


def kernel(x, *args):
    n_w = len(_WEIGHTS)
    w = dict(zip(_WEIGHTS, args[:n_w]))
    target = args[n_w]
    m = dict(zip(_WEIGHTS, args[n_w + 1:2 * n_w + 1]))
    v = dict(zip(_WEIGHTS, args[2 * n_w + 1:3 * n_w + 1]))
    B, L, _ = x.shape
    T = B * L
    me = 2 * lax.axis_index("x") + lax.axis_index("y")

    vec = jnp.concatenate([_gate_vec_slot(w["rg_b_a"][0], w["rg_b_x"][0], w["rg_lam"][0]),
                           _rows_at(w["rg_conv_w"][0, :, 0, :], 0)], axis=0)
    flat = lambda a: a.reshape(-1, a.shape[-1])
    rows_of = lambda k: w[k].shape[-2]
    groups = [("rg", [("rg_w_in", 0, BF16), (vec, 0, F32)]), ("rg_out", [("rg_w_out", 0, BF16)]),
              ("mlp0_up", [("mlp_w_up", 0, BF16)]), ("mlp0_down", [("mlp_w_down", 0, BF16)]),
              ("att", [("at_w_qkv", 0, BF16), ("at_w_o", 0, BF16)]),
              ("mlp1", [("mlp_w_up", 1, BF16), ("mlp_w_down", 1, BF16)])]

    def landing_zones(group, members, after):
        lands = []
        for n, (k, layer, dtype) in enumerate(members):
            src, rows = (flat(w[k]), rows_of(k)) if isinstance(k, str) else (k, k.shape[0])
            lands.append(_cast_into_slot(src, layer * rows, rows, me, dtype, f"place_{group}{n}", after=after))
        return lands

    halves, gathers = {}, {}
    halves["rg"], tok = _exchange_start("gather_half", [], landing_zones(*groups[0], None), "gather_rg_start")
    handles, tok = _gather_start_groups([landing_zones(g, members, tok) for g, members in groups[1:]],
                                        "gather_rest_start", after=tok, kind="gather_half")
    halves.update(zip([g for g, _ in groups[1:]], handles))
    wcat = _make_wcat(w["rg_w_a"], w["rg_w_x"]).astype(BF16)

    packs = [_pack_vec(p, me) for p in (w, m, v)]

    ready = {}

    def share(some, after, name):
        landed = [_exchange_wait("gather_half", halves[g], after, f"gather_{g}_landed")[1] for g in some]
        handles, _ = _gather_start_groups(landed, name, kind="share_half")
        gathers.update(zip(some, handles))

    def fetch(what, after):
        if what in ready:
            return ready[what]
        group = "mlp1" if what.startswith("mlp1") else what
        if group == "rg":
            share(["rg"], [after, wcat] + packs, "share_rg_start")
        elif group == "rg_out":
            share(["rg_out", "mlp0_up", "mlp0_down", "att"], after, "share_early_start")
        _, full = _exchange_wait("share_half", gathers[group], after, f"gather_{group}_wait")
        if group == "att":
            share(["mlp1"], after, "share_mlp1_start")
        if group == "rg":
            vec_full = jnp.transpose(full[1], (1, 0, 2)).reshape(2 * SUBLANES, D_MODEL)
            conv_wb = vec_full[SUBLANES:] + _rows_at(w["rg_conv_b"], 4)
            return full[0], conv_wb, wcat, vec_full[:SUBLANES]
        if group == "rg_out":
            return full[0].reshape(D_MODEL, D_MODEL)
        if group == "att":
            return full[0], full[1].reshape(D_MODEL, D_MODEL)
        if group == "mlp1":
            ready["mlp1_up"], ready["mlp1_down"] = full[0], full[1].reshape(4 * D_MODEL, D_MODEL)
            return ready[what]
        return full[0] if group == "mlp0_up" else full[0].reshape(4 * D_MODEL, D_MODEL)

    names = dict(mlp1=["up1", "down1"], att=["at_w_qkv", "at_w_o"], mlp0=["up0", "down0", "rg_w_out"],
                 rg_in=["rg_w_in"], gates=["rg_w_a", "rg_w_x"])
    scatters, swaps, P, Q, res = {}, [], {}, {}, {}

    def start_scatter(group, grads):
        srcs = [g.reshape(N_CHIPS, -1, g.shape[-1]) for g in grads]
        lands = [lax.empty((N_CHIPS - 1,) + s.shape[1:], s.dtype) for s in srcs]
        scatters[group], token = _exchange_start("scatter", srcs, lands, f"scatter_{group}_start")
        return token

    def settle(groups, after):
        keys, parts = [], []
        for group in groups:
            srcs, lands = _exchange_wait("scatter", scatters[group], after, f"scatter_{group}_wait")
            for k, s, r in zip(names[group], srcs, lands):
                keys.append(k)
                parts.append(_sum_slots(s, r, me, f"sum_{k}"))
        handle, token = _exchange_start("swap", parts, [lax.empty(p.shape, F32) for p in parts],
                                        f"swap_{groups[0]}_start")
        swaps.append((keys, handle, f"swap_{groups[0]}_wait"))
        return token

    def finish(after):
        for keys, handle, name in swaps:
            mine, theirs = _exchange_wait("swap", handle, after, name)
            P.update(zip(keys, mine))
            Q.update(zip(keys, theirs))
        swaps.clear()
        last = after
        for k, parts in _BIG.items():
            if k in res or any(p not in P for p in parts):
                continue
            shape = w[k].shape
            two_d = lambda a: a.reshape(-1, shape[-1])
            outs = _adamw(two_d(w[k]), two_d(m[k]), two_d(v[k]), [P[p] for p in parts], [Q[p] for p in parts],
                          f"adamw_{k}")
            res[k] = [o.reshape(shape) for o in outs]
            last = outs[0]
        if "rg_w_a" in P and "gates" not in gathers:
            lands = [_cast_into_slot(P[k], 0, P[k].shape[0], me, F32, f"place_{k}", after=last, add=Q[k])
                     for k in names["gates"]]
            gathers["gates"], last = _exchange_start("gather", [], lands, "gather_gates_start", after=last)
        return last

    def emit(event, arrays):
        if event == "point_attn_done":
            return None
        if event == "point_mix_done":
            return settle(["mlp1", "att", "mlp0"], arrays[0])
        token = start_scatter(event, arrays)
        if event == "rg_in":
            return finish(settle(["gates"], token))
        return token

    P_vec = dict(norm_mix_g=w["norm_mix_g"], norm_mlp_g=w["norm_mlp_g"], final_g=w["final_g"][None],
                 q_g=w["at_q_g"], k_g=w["at_k_g"])
    grad_x, vec_part = _local_step(x.reshape(T, D_MODEL), target.reshape(T, D_MODEL), P_vec, fetch, emit, B, L,
                                   after=tok)

    me8 = 2 * me + lax.axis_index("c")
    vec_slots = _cast_into_slot(vec_part, 0, VEC_ROWS, me8, F32, "place_vec", n_slots=N_DEVICES)
    spread, tok = _exchange_start("spread", [], [vec_slots], "spread_vec_start")
    last = finish(settle(["rg_in"], tok))
    _, gate_grads = _exchange_wait("gather", gathers["gates"], last, "gather_gates_wait")
    for k, g in zip(names["gates"], gate_grads):
        two_d = lambda a: a.reshape(g.shape[0] * g.shape[1], g.shape[2])
        outs = _adamw(two_d(w[k]), two_d(m[k]), two_d(v[k]), [two_d(g)], None, f"adamw_{k}")
        res[k] = [o.reshape(w[k].shape) for o in outs]
        last = outs[0]
    _, (vec_all,) = _exchange_wait("spread", spread, last, "spread_vec_wait")
    vec_grad = _sum_leading(vec_all, "sum_vec")
    loss = vec_grad[LOSS_ROW, 0]
    outs = _adamw(*packs, [vec_grad], None, "adamw_vec")
    unpacked = [_unpack_vec(o, me) for o in outs]
    for k in _WEIGHTS:
        if k not in res:
            res[k] = [u[k] for u in unpacked]

    result = [loss, grad_x.reshape(B, L, D_MODEL)]
    for slot in range(4):
        result += [res[k][slot] for k in _WEIGHTS]
    return tuple(result)
```

```python
import functools
import math

import jax
import jax.numpy as jnp
import numpy as np
from jax import lax
from jax.experimental import pallas as pl
from jax.experimental.pallas import tpu as pltpu

F32 = jnp.float32
BF16 = jnp.bfloat16

D_MODEL = 1024
HEAD_DIM = 128
N_HEADS = 8
N_KV = 2
GROUP = N_HEADS // N_KV
LRU_BLOCKS = 8
LRU_BW = 128
GRID_W = 64
ROPE_THETA = 10000.0
EPS = 1e-6
RG_C = 8.0
SCALE = 1.0 / math.sqrt(HEAD_DIM)
N_CHIPS = 4

ADAM_LR = 0.001
ADAM_B1 = 0.9
ADAM_B2 = 0.999
ADAM_EPS = 1e-08
ADAM_WD = 0.01
ADAM_STEP = 10

V7X_VMEM_BYTES = 64 * 1024 * 1024
VMEM_LIMIT = V7X_VMEM_BYTES * 3 // 4
LANES = 128
SUBLANES = 8

N_DEVICES = 8
VEC_ROWS = 32
LOSS_ROW = 5


def _params(sem):
    return pltpu.CompilerParams(dimension_semantics=sem, vmem_limit_bytes=VMEM_LIMIT)


_ANY = pl.BlockSpec(memory_space=pl.ANY)
_NN = (((1,), (0,)), ((), ()))
_NT = (((1,), (1,)), ((), ()))
_TN = (((0,), (0,)), ((), ()))


def _after_operand(after):
    return [] if after is None else [after]


def _fit(t, n):
    if n <= t:
        return n
    c = (t // LANES) * LANES
    while n % c:
        c -= LANES
    return c


MM_VMEM_BUDGET = VMEM_LIMIT * 3 // 4
def _mm_tiles(M, K, ns, n_total, out_dtypes, extras, whole_rows):
    for tm in (2048, 1024, 512, 256, 128):
        for tn in ((ns,) if whole_rows else (1024, 512, 256)):
            tn = _fit(tn, ns)
            per_row = 2 * (2 * K) + 4 * tn + sum(2 * tn * jnp.dtype(d).itemsize for d in out_dtypes)
            per_row += sum(2 * tn * e.dtype.itemsize for e in extras)
            b_buffers = 1 if tn == n_total else 2
            if M % tm == 0 and b_buffers * (2 * K * tn) + tm * per_row <= MM_VMEM_BUDGET:
                return tm, tn
    raise ValueError(f"no tile fits VMEM for M={M} K={K} N={ns}")


def _mm(a, b, *, mode, name, out_dtypes=(F32,), b_shard=False, o_shard=False, extras=(), epi=None, after=None,
        bcast=(), accs=(), ref_epi=None, out_cols=None):
    if mode == "tn":
        K, M = a.shape
        N = b.shape[1]
    else:
        M, K = a.shape
        if mode == "nn":
            N = b.shape[0] * b.shape[2] if b_shard else b.shape[1]
        else:
            N = b.shape[1] if b_shard else b.shape[0]
    ns = N
    if b_shard and mode == "nn":
        ns = b.shape[2]
    elif o_shard:
        ns = N // N_CHIPS
    tm, tn = _mm_tiles(M, K, ns, N, out_dtypes, extras, whole_rows=ref_epi is not None)
    if ref_epi is not None:
        tm = min(tm, 512)
    grid = (M // tm, N // tn)
    q = ns // tn
    once = dict(pipeline_mode=pl.Buffered(1)) if tn == N else {}

    if mode == "tn":
        a_spec = pl.BlockSpec((K, tm), lambda i, j: (0, i))
        b_spec = pl.BlockSpec((K, tn), lambda i, j: (0, j), **once)
        dims = _TN
    elif mode == "nn":
        a_spec = pl.BlockSpec((tm, K), lambda i, j: (i, 0))
        if b_shard:
            b_spec = pl.BlockSpec((None, K, tn), lambda i, j: (j // q, 0, j % q), **once)
        else:
            b_spec = pl.BlockSpec((K, tn), lambda i, j: (0, j), **once)
        dims = _NN
    else:
        a_spec = pl.BlockSpec((tm, K), lambda i, j: (i, 0))
        if b_shard:
            ks = b.shape[2]
            b_spec = pl.BlockSpec((N_CHIPS, tn, ks), lambda i, j: (0, j, 0), **once)
        else:
            b_spec = pl.BlockSpec((tn, K), lambda i, j: (j, 0), **once)
        dims = _NT

    if o_shard:
        o_specs = [pl.BlockSpec((None, tm, tn), lambda i, j: (j // q, i, j % q))]
        o_shapes = [jax.ShapeDtypeStruct((N_CHIPS, M, ns), out_dtypes[0])]
    else:
        o_specs = [pl.BlockSpec((tm, tn), lambda i, j: (i, j)) for _ in out_dtypes]
        o_shapes = [jax.ShapeDtypeStruct((M, N if out_cols is None else out_cols[n]), dt)
                    for n, dt in enumerate(out_dtypes)]
    e_specs = [pl.BlockSpec((tm, tn), lambda i, j: (i, j)) for _ in extras]
    e_specs += [pl.BlockSpec(v.shape, lambda i, j: (0, 0)) for v in bcast]
    o_specs += [pl.BlockSpec(s, lambda i, j: (0, 0)) for s in accs]
    o_shapes += [jax.ShapeDtypeStruct(s, F32) for s in accs]
    n_e, n_b, n_o, n_a = len(extras), len(bcast), len(out_dtypes), len(accs)
    order = _after_operand(after)
    n_x = len(order)
    if epi is None:
        epi = lambda acc: (acc,)

    def body(a_ref, b_ref, *rest):
        e_refs, b_refs = rest[:n_e], rest[n_e:n_e + n_b]
        o_refs = rest[n_e + n_b + n_x:n_e + n_b + n_x + n_o]
        a_refs = rest[n_e + n_b + n_x + n_o:]
        if n_a:
            @pl.when((pl.program_id(0) == 0) & (pl.program_id(1) == 0))
            def _():
                for r in a_refs:
                    r[...] = jnp.zeros(r.shape, F32)
        if mode == "nt" and b_shard:
            acc = None
            for s in range(N_CHIPS):
                part = lax.dot_general(a_ref[:, s * ks:(s + 1) * ks], b_ref[s], dims, preferred_element_type=F32)
                acc = part if acc is None else acc + part
        else:
            acc = lax.dot_general(a_ref[...], b_ref[...], dims, preferred_element_type=F32)
        if ref_epi is not None:
            ref_epi(acc, e_refs, b_refs, o_refs, a_refs)
            return
        outs = epi(acc, *[r[...] for r in e_refs])
        for r, o in zip(o_refs, outs):
            r[...] = o.astype(r.dtype)

    outs = pl.pallas_call(
        body, name=name, grid=grid, in_specs=[a_spec, b_spec] + e_specs + [_ANY] * n_x, out_specs=o_specs,
        out_shape=o_shapes, compiler_params=_params(("arbitrary", "arbitrary") if n_a else ("parallel", "parallel")),
    )(a, b, *extras, *bcast, *order)
    return outs[0] if n_o + n_a == 1 else outs


def _rowwise(fn, rows, bcast, outs, accs=(), *, tm, name, after=None):
    def norm(r):
        return r if isinstance(r, tuple) else (r, r.shape[1], 0)

    rows = [norm(r) for r in rows]
    T = rows[0][0].shape[0]
    tm = min(tm, T)
    while T % tm:
        tm -= SUBLANES
    n_r, n_b, n_o, n_a = len(rows), len(bcast), len(outs), len(accs)
    order = _after_operand(after)
    n_x = len(order)
    in_specs = [pl.BlockSpec((tm, c), functools.partial(lambda i, cb: (i, cb), cb=cb)) for _, c, cb in rows]
    in_specs += [pl.BlockSpec(b.shape, lambda i: (0, 0)) for b in bcast] + [_ANY] * n_x
    out_specs = [pl.BlockSpec((tm, o[0]), lambda i: (i, 0)) for o in outs]
    out_specs += [pl.BlockSpec(s, lambda i: (0, 0)) for s in accs]
    out_shape = [jax.ShapeDtypeStruct((T, o[2] if len(o) > 2 else o[0]), o[1]) for o in outs]
    out_shape += [jax.ShapeDtypeStruct(s, F32) for s in accs]

    def body(*refs):
        in_refs = refs[:n_r]
        b_refs = refs[n_r:n_r + n_b]
        o_refs = refs[n_r + n_b + n_x:n_r + n_b + n_x + n_o]
        a_refs = refs[n_r + n_b + n_x + n_o:]
        if n_a:
            @pl.when(pl.program_id(0) == 0)
            def _():
                for r in a_refs:
                    r[...] = jnp.zeros(r.shape, F32)
        fn(in_refs, b_refs, o_refs, a_refs)

    res = pl.pallas_call(
        body, name=name, grid=(T // tm,), in_specs=in_specs, out_specs=out_specs, out_shape=out_shape,
        compiler_params=_params(("arbitrary",) if n_a else ("parallel",)),
    )(*[r[0] for r in rows], *bcast, *order)
    return res


def _rsum(x):
    return jnp.sum(x, axis=0, keepdims=True)


def _rms_fwd(x, g, name, after=None):
    def fn(ins, bs, outs, accs):
        xv = ins[0][...]
        r = lax.rsqrt(jnp.mean(xv * xv, axis=-1, keepdims=True) + EPS)
        outs[0][...] = (xv * r * bs[0][...]).astype(BF16)

    return _rowwise(fn, [x], [g], [(D_MODEL, BF16)], tm=512, name=name, after=after)[0]


def _rms_bwd_math(xv, dh, g):
    r = lax.rsqrt(jnp.mean(xv * xv, axis=-1, keepdims=True) + EPS)
    hn = xv * r
    dgh = dh * g
    dx = r * (dgh - hn * jnp.mean(dgh * hn, axis=-1, keepdims=True))
    return dx, _rsum(dh * hn)


def _mm_norm_bwd(dy, w, x, dres, g, name, after=None):
    def epilogue(acc, e_refs, b_refs, o_refs, a_refs):
        dx, dg = _rms_bwd_math(e_refs[0][...], acc, b_refs[0][...])
        dx = dx + e_refs[1][...]
        o_refs[0][...] = dx
        o_refs[1][...] = dx.astype(BF16)
        a_refs[0][...] += dg

    return _mm(dy, w, mode="nt", b_shard=True, out_dtypes=(F32, BF16), extras=(x, dres), bcast=(g,),
               accs=((1, D_MODEL),), ref_epi=epilogue, name=name, after=after)


def _mm_res_norm(a, w, res, g, name):
    def epilogue(acc, e_refs, b_refs, o_refs, a_refs):
        xv = acc + e_refs[0][...]
        o_refs[0][...] = xv
        r = lax.rsqrt(jnp.mean(xv * xv, axis=-1, keepdims=True) + EPS)
        o_refs[1][...] = (xv * r * b_refs[0][...]).astype(BF16)

    return _mm(a, w, mode="nn", out_dtypes=(F32, BF16), extras=(res,), bcast=(g,), ref_epi=epilogue, name=name)


def _mm_final_loss(a, w, res, target, g, name):
    def epilogue(acc, e_refs, b_refs, o_refs, a_refs):
        xv = acc + e_refs[0][...]
        gv = b_refs[0][...]
        r = lax.rsqrt(jnp.mean(xv * xv, axis=-1, keepdims=True) + EPS)
        e = xv * r * gv - e_refs[1][...]
        tok = jnp.mean(e * e, axis=-1, keepdims=True)
        a_refs[0][...] += 0.5 * jnp.sum(tok, axis=0, keepdims=True) * jnp.ones((1, LANES), F32)
        dx, dg = _rms_bwd_math(xv, e * (1.0 / D_MODEL), gv)
        o_refs[0][...] = dx
        o_refs[1][...] = dx.astype(BF16)
        a_refs[1][...] += dg

    return _mm(a, w, mode="nn", out_dtypes=(F32, BF16), extras=(res, target), bcast=(g,),
               accs=((1, LANES), (1, D_MODEL)), ref_epi=epilogue, name=name)


def _relu2(acc):
    r = jnp.maximum(acc, 0.0)
    return r * r, r


def _mlp_fwd(x, h, fetch, tag, finish):
    w_up = fetch(f"mlp{tag}_up", h)
    a, r = _mm(h, w_up, mode="nn", b_shard=True, out_dtypes=(BF16, BF16), epi=_relu2, name=f"mlp{tag}_up")
    w_down = fetch(f"mlp{tag}_down", a)
    return finish(a, w_down, x, f"mlp{tag}_down"), (h, a, r, w_up, w_down)


def _mlp_bwd(x, g, saved, dx, dx_bf, tag, after):
    h, a, r, w_up, w_down = saved
    d_down = _mm(a, dx_bf, mode="tn", out_dtypes=(BF16,), name=f"mlp{tag}_dwdown", after=after)
    dup = _mm(dx_bf, w_down, mode="nt", extras=(r,), out_dtypes=(BF16,),
              epi=lambda acc, rv: (acc * (2.0 * rv.astype(F32)),), name=f"mlp{tag}_dup")
    d_up = _mm(h, dup, mode="tn", o_shard=True, out_dtypes=(BF16,), name=f"mlp{tag}_dwup")
    dx_new, dx_new_bf, dg = _mm_norm_bwd(dup, w_up, x, dx, g, f"mlp{tag}_dh")
    return dx_new, dx_new_bf, dg, d_up, d_down


def _rope_tables(L, B):
    rows = L // GRID_W
    row = np.repeat(np.arange(rows, dtype=np.float32), GRID_W)
    col = np.tile(np.arange(GRID_W, dtype=np.float32), rows)
    inv = (ROPE_THETA ** (-np.arange(HEAD_DIM // 4, dtype=np.float32) / (HEAD_DIM // 4))).astype(np.float32)
    ar, ac = row[:, None] * inv, col[:, None] * inv
    cos = np.concatenate([np.cos(ar), np.cos(ar), np.cos(ac), np.cos(ac)], axis=-1)
    sin = np.concatenate([-np.sin(ar), np.sin(ar), -np.sin(ac), np.sin(ac)], axis=-1)
    return jnp.asarray(np.tile(cos, (B, 1)), F32), jnp.asarray(np.tile(sin, (B, 1)), F32)


def _swap_halves(x):
    lane = lax.broadcasted_iota(jnp.int32, x.shape, 1)
    return jnp.where((lane % 64) < 32, pltpu.roll(x, HEAD_DIM - 32, 1), pltpu.roll(x, 32, 1))


def _qk_prep(qkv, cos, sin, q_g, k_g):
    def fn(ins, bs, outs, accs):
        c, s = ins[1][...], ins[2][...]
        for h in range(N_HEADS + N_KV):
            xv = ins[0][:, h * HEAD_DIM:(h + 1) * HEAD_DIM]
            g = bs[0][...] if h < N_HEADS else bs[1][...]
            r = lax.rsqrt(jnp.mean(xv * xv, axis=-1, keepdims=True) + EPS)
            z = xv * r * g
            y = (z * c + _swap_halves(z) * s).astype(BF16)
            if h < N_HEADS:
                outs[0][:, h * HEAD_DIM:(h + 1) * HEAD_DIM] = y
            else:
                outs[1][:, (h - N_HEADS) * HEAD_DIM:(h - N_HEADS + 1) * HEAD_DIM] = y
        outs[2][...] = ins[0][:, (N_HEADS + N_KV) * HEAD_DIM:].astype(BF16)

    kvw = N_KV * HEAD_DIM
    return _rowwise(fn, [qkv, cos, sin], [q_g, k_g], [(D_MODEL, BF16), (kvw, BF16), (kvw, BF16)], tm=512,
                    name="attn_qk_prep")


def _qk_prep_bwd(qkv, dq, dk, dv, cos, sin, q_g, k_g):
    def fn(ins, bs, outs, accs):
        c, s = ins[4][...], ins[5][...]
        for h in range(N_HEADS + N_KV):
            sl = slice(h * HEAD_DIM, (h + 1) * HEAD_DIM)
            xv = ins[0][:, sl]
            if h < N_HEADS:
                g, dy, acc = bs[0][...], ins[1][:, sl], accs[0]
            else:
                ks = slice((h - N_HEADS) * HEAD_DIM, (h - N_HEADS + 1) * HEAD_DIM)
                g, dy, acc = bs[1][...], ins[2][:, ks], accs[1]
            r = lax.rsqrt(jnp.mean(xv * xv, axis=-1, keepdims=True) + EPS)
            xn = xv * r
            dz = dy * c - _swap_halves(dy) * s
            acc[...] += _rsum(dz * xn)
            dxn = dz * g
            outs[0][:, sl] = (r * (dxn - xn * jnp.mean(dxn * xn, axis=-1, keepdims=True))).astype(BF16)
        outs[0][:, (N_HEADS + N_KV) * HEAD_DIM:] = ins[3][...].astype(BF16)

    return _rowwise(fn, [qkv, dq, dk, dv, cos, sin], [q_g, k_g], [(qkv.shape[1], BF16)],
                    [(1, HEAD_DIM), (1, HEAD_DIM)], tm=512, name="attn_qk_prep_bwd")


_EXP2_SCALE = SCALE * math.log2(math.e)


def _exp_rows(q, k):
    s = lax.dot_general(q, k, _NT, preferred_element_type=F32)
    p = jnp.exp2((s - jnp.max(s, axis=-1, keepdims=True)) * _EXP2_SCALE)
    return p, jnp.sum(p, axis=-1, keepdims=True)


def _attn_fwd(q, k, v, B, L, tq=2048, sub=256):
    tq = min(tq, L)
    sub = min(sub, tq)
    nq = L // tq

    def body(q_ref, k_ref, v_ref, o_ref):
        kv, vv = k_ref[...], v_ref[...]
        for c in range(tq // sub):
            rows = slice(c * sub, (c + 1) * sub)
            p, l = _exp_rows(q_ref[rows, :], kv)
            o = jnp.dot(p.astype(BF16), vv, preferred_element_type=F32)
            o_ref[rows, :] = (o * (1.0 / l)).astype(o_ref.dtype)

    return pl.pallas_call(
        body, name="attn_fwd", grid=(B, N_HEADS, nq),
        in_specs=[pl.BlockSpec((tq, HEAD_DIM), lambda b, h, i: (b * nq + i, h)),
                  pl.BlockSpec((L, HEAD_DIM), lambda b, h, i: (b, h // GROUP)),
                  pl.BlockSpec((L, HEAD_DIM), lambda b, h, i: (b, h // GROUP))],
        out_specs=pl.BlockSpec((tq, HEAD_DIM), lambda b, h, i: (b * nq + i, h)),
        out_shape=jax.ShapeDtypeStruct((B * L, D_MODEL), BF16),
        compiler_params=_params(("parallel", "parallel", "parallel")),
    )(q, k, v)


def _attn_bwd(q, k, v, o, do, B, L, tq=2048, sub=512):
    tq = min(tq, L)
    sub = min(sub, tq)
    nq = L // tq

    def body(q_ref, k_ref, v_ref, o_ref, do_ref, dq_ref, dk_ref, dv_ref):
        @pl.when((pl.program_id(2) == 0) & (pl.program_id(3) == 0))
        def _():
            dk_ref[...] = jnp.zeros(dk_ref.shape, F32)
            dv_ref[...] = jnp.zeros(dv_ref.shape, F32)

        kv, vv = k_ref[...], v_ref[...]
        ps, es, dos, qs = [], [], [], []
        for c in range(tq // sub):
            rows = slice(c * sub, (c + 1) * sub)
            qc, doc = q_ref[rows, :], do_ref[rows, :]
            p, l = _exp_rows(qc, kv)
            inv = 1.0 / l
            dp = lax.dot_general(doc, vv, _NT, preferred_element_type=F32)
            delta = jnp.sum(doc.astype(F32) * o_ref[rows, :].astype(F32), axis=-1, keepdims=True)
            e = (p * (dp - delta)).astype(BF16)
            dq_ref[rows, :] = jnp.dot(e, kv, preferred_element_type=F32) * (inv * SCALE)
            ps.append(p.astype(BF16))
            es.append(e)
            dos.append((doc.astype(F32) * inv).astype(BF16))
            qs.append((qc.astype(F32) * (inv * SCALE)).astype(BF16))
        cat = lambda xs: xs[0] if len(xs) == 1 else jnp.concatenate(xs, axis=0)
        dv_ref[...] += lax.dot_general(cat(ps), cat(dos), _TN, preferred_element_type=F32)
        dk_ref[...] += lax.dot_general(cat(es), cat(qs), _TN, preferred_element_type=F32)

    qmap = lambda b, kh, g, i: (b * nq + i, kh * GROUP + g)
    kmap = lambda b, kh, g, i: (b, kh)
    kvw = N_KV * HEAD_DIM
    return pl.pallas_call(
        body, name="attn_bwd", grid=(B, N_KV, GROUP, nq),
        in_specs=[pl.BlockSpec((tq, HEAD_DIM), qmap), pl.BlockSpec((L, HEAD_DIM), kmap),
                  pl.BlockSpec((L, HEAD_DIM), kmap), pl.BlockSpec((tq, HEAD_DIM), qmap),
                  pl.BlockSpec((tq, HEAD_DIM), qmap)],
        out_specs=[pl.BlockSpec((tq, HEAD_DIM), qmap), pl.BlockSpec((L, HEAD_DIM), kmap),
                   pl.BlockSpec((L, HEAD_DIM), kmap)],
        out_shape=[jax.ShapeDtypeStruct((B * L, D_MODEL), F32), jax.ShapeDtypeStruct((B * L, kvw), F32),
                   jax.ShapeDtypeStruct((B * L, kvw), F32)],
        compiler_params=_params(("parallel", "parallel", "arbitrary", "arbitrary")),
    )(q, k, v, o, do)


def _conv_shift(x, t, L, k):
    if k == 2:
        return x
    if k < 2:
        return jnp.where(t >= 2 - k, pltpu.roll(x, 2 - k, 0), 0.0)
    return jnp.where(t < L - (k - 2), pltpu.roll(x, L - (k - 2), 0), 0.0)


def _conv_apply(x, w_ref, L):
    t = lax.broadcasted_iota(jnp.int32, x.shape, 0)
    acc = w_ref[4:5, :] + w_ref[2:3, :] * x
    for k in (0, 1, 3):
        acc = acc + w_ref[k:k + 1, :] * _conv_shift(x, t, L, k)
    return acc


def _conv_bwd(z, g, wb, dz, B, L, tc=256, after=None):
    noff = D_MODEL // tc
    order = _after_operand(after)

    def body(z_ref, g_ref, w_ref, dz_in, *rest):
        dx_ref, dw_ref = rest[len(order):]

        @pl.when(pl.program_id(1) == 0)
        def _():
            dw_ref[...] = jnp.zeros(dw_ref.shape, F32)

        x, gv = z_ref[...], g_ref[...]
        t = lax.broadcasted_iota(jnp.int32, x.shape, 0)
        dx = w_ref[2:3, :] * gv
        for k in (0, 1, 3):
            dx = dx + w_ref[k:k + 1, :] * _conv_shift(gv, t, L, 4 - k)
        dx_ref[...] = dx.astype(BF16)
        for k in range(4):
            dw_ref[k:k + 1, :] += _rsum(_conv_shift(x, t, L, k) * gv)
        dw_ref[4:5, :] += _rsum(gv)

    return pl.pallas_call(
        body, name="rg_conv_bwd", grid=(noff, B),
        in_specs=[pl.BlockSpec((L, tc), lambda j, b: (b, noff + j)), pl.BlockSpec((L, tc), lambda j, b: (b, j)),
                  pl.BlockSpec((SUBLANES, tc), lambda j, b: (0, j)), _ANY] + [_ANY] * len(order),
        out_specs=[pl.BlockSpec((L, tc), lambda j, b: (b, noff + j)),
                   pl.BlockSpec((SUBLANES, tc), lambda j, b: (0, j))],
        out_shape=[jax.ShapeDtypeStruct(dz.shape, dz.dtype), jax.ShapeDtypeStruct((SUBLANES, D_MODEL), F32)],
        input_output_aliases={3: 0},
        compiler_params=_params(("parallel", "arbitrary")),
    )(z, g, wb, dz, *order)


def _softplus(x):
    return jnp.maximum(x, 0.0) + jnp.log1p(jnp.exp(-jnp.abs(x)))


_ROW_BA, _ROW_BX, _ROW_LAM = 0, 2, 4


def _gate_math(xb, pre, vec_ref, d, sl):
    pa = pre[:, (2 * d) * LRU_BW:(2 * d + 1) * LRU_BW] + vec_ref[_ROW_BA + d:_ROW_BA + d + 1, sl]
    px = pre[:, (2 * d + 1) * LRU_BW:(2 * d + 2) * LRU_BW] + vec_ref[_ROW_BX + d:_ROW_BX + d + 1, sl]
    r = 0.5 * jnp.tanh(0.5 * pa) + 0.5
    i = 0.5 * jnp.tanh(0.5 * px) + 0.5
    slope = (-RG_C) * _softplus(-vec_ref[_ROW_LAM + d:_ROW_LAM + d + 1, sl])
    log_a = r * slope
    a = jnp.exp(log_a)
    om = -jnp.tanh(log_a) * (1.0 + a * a)
    rs = lax.rsqrt(om)
    mult = jnp.where(om > 0.0, om * rs, 0.0)
    return a, mult * (i * xb), (r, i, slope, om, mult, rs)


def _gate_bwd(rec, du_f, da_f, du_b, da_b, wcat, gvec):
    def fn(ins, bs, outs, accs):
        for blk in range(LRU_BLOCKS):
            sl = slice(blk * LRU_BW, (blk + 1) * LRU_BW)
            xb = ins[0][:, sl]
            xb16 = xb.astype(BF16)
            w = bs[0][sl, :]
            pre = jnp.dot(xb16, w, preferred_element_type=F32)
            dx = jnp.zeros_like(xb)
            dpre = []
            for d in range(2):
                a, _, (r, i, slope, om, mult, rs) = _gate_math(xb, pre, bs[1], d, sl)
                du, da = ins[1 + 2 * d][:, sl], ins[2 + 2 * d][:, sl]
                t = du * xb
                d_i = t * mult
                dx = dx + du * mult * i
                dlog = da * a - (t * i) * ((1.0 - om) * rs)
                d_r = dlog * slope
                d_sp = _rsum(dlog * r) * (-RG_C)
                lam = bs[1][_ROW_LAM + d:_ROW_LAM + d + 1, sl]
                accs[2][_ROW_LAM + d:_ROW_LAM + d + 1, sl] += d_sp * (-jax.nn.sigmoid(-lam))
                dpa = d_r * r * (1.0 - r)
                dpx = d_i * i * (1.0 - i)
                accs[2][_ROW_BA + d:_ROW_BA + d + 1, sl] += _rsum(dpa)
                accs[2][_ROW_BX + d:_ROW_BX + d + 1, sl] += _rsum(dpx)
                dpre += [dpa, dpx]
            dpre = jnp.concatenate(dpre, axis=1).astype(BF16)
            dw = lax.dot_general(xb16, dpre, _TN, preferred_element_type=F32)
            for d in range(2):
                rows = slice(d * D_MODEL + blk * LRU_BW, d * D_MODEL + (blk + 1) * LRU_BW)
                accs[0][rows, :] += dw[:, (2 * d) * LRU_BW:(2 * d + 1) * LRU_BW]
                accs[1][rows, :] += dw[:, (2 * d + 1) * LRU_BW:(2 * d + 2) * LRU_BW]
            outs[0][:, sl] = dx + lax.dot_general(dpre, w, _NT, preferred_element_type=F32)

    gate_shape = (2 * D_MODEL, LRU_BW)
    return _rowwise(fn, [rec, du_f, da_f, du_b, da_b], [wcat, gvec], [(D_MODEL, F32)],
                    [gate_shape, gate_shape, (SUBLANES, D_MODEL)], tm=512, name="rg_gate_bwd")


def _as_time_blocks(x):
    return x.reshape(x.shape[0] // SUBLANES, SUBLANES, x.shape[1])


def _scan_call(body, ins, n_out, B, L, tc, name):
    nb = L // SUBLANES
    spec = pl.BlockSpec((nb, SUBLANES, tc), lambda b, j: (b, 0, j))
    T = ins[0].shape[0]
    outs = pl.pallas_call(
        functools.partial(body, nb), name=name, grid=(B, D_MODEL // tc),
        in_specs=[spec] * len(ins), out_specs=[spec] * n_out,
        out_shape=[jax.ShapeDtypeStruct((T // SUBLANES, SUBLANES, D_MODEL), F32)] * n_out,
        compiler_params=_params(("parallel", "parallel")),
    )(*[_as_time_blocks(x) for x in ins])
    return [o.reshape(T, D_MODEL) for o in outs]


def _block_scan(A, U, reverse):
    row = lax.broadcasted_iota(jnp.int32, A.shape, 0)
    for s in (1, 2, 4):
        shift = SUBLANES - s if reverse else s
        valid = (row < SUBLANES - s) if reverse else (row >= s)
        a_sh = jnp.where(valid, pltpu.roll(A, shift, 0), 1.0)
        u_sh = jnp.where(valid, pltpu.roll(U, shift, 0), 0.0)
        U = A * u_sh + U
        A = A * a_sh
    return A, U


_LAST = SUBLANES - 1
SCAN_UNROLL = 8


def _loop_blocks(nb, step, init):
    def group(g, carry):
        for k in range(SCAN_UNROLL):
            carry = step(g * SCAN_UNROLL + k, carry)
        return carry

    return lax.fori_loop(0, nb // SCAN_UNROLL, group, init)


def _scan_bwd(dy, a_f, h_f, a_b, h_b, B, L, tc=256):
    def body(nb, dy_r, af, hf, ab, hb, duf, daf, dub, dab):
        def step(i, carry):
            c1, c2 = carry
            ir = nb - 1 - i
            row = lax.broadcasted_iota(jnp.int32, (SUBLANES, tc), 0)
            a_up = jnp.where(row == _LAST, af[jnp.minimum(ir + 1, nb - 1), :1, :], pltpu.roll(af[ir], _LAST, 0))
            p, lam = _block_scan(a_up, dy_r[ir], True)
            lam = lam + p * c1
            before = hf[jnp.maximum(ir - 1, 0), _LAST:, :] * (ir > 0).astype(F32)
            duf[ir] = lam
            daf[ir] = lam * jnp.where(row == 0, before, pltpu.roll(hf[ir], 1, 0))
            a_dn = jnp.where(row == 0, ab[jnp.maximum(i - 1, 0), _LAST:, :], pltpu.roll(ab[i], 1, 0))
            p2, lam2 = _block_scan(a_dn, dy_r[i], False)
            lam2 = lam2 + p2 * c2
            after = hb[jnp.minimum(i + 1, nb - 1), :1, :] * (i < nb - 1).astype(F32)
            dub[i] = lam2
            dab[i] = lam2 * jnp.where(row == _LAST, after, pltpu.roll(hb[i], _LAST, 0))
            return lam[:1, :], lam2[_LAST:, :]

        zero = jnp.zeros((1, tc), F32)
        _loop_blocks(nb, step, (zero, zero))

    return _scan_call(body, [dy, a_f, h_f, a_b, h_b], 4, B, L, tc, "rg_scan_bwd")


_GELU_C = math.sqrt(2.0 / math.pi)


def _gelu_parts(x):
    th = jnp.tanh(_GELU_C * (x + 0.044715 * x * x * x))
    return 0.5 * x * (1.0 + th), th


def _mm_gated_out_bwd(dx, w_out, h_f, h_b, z, name, after=None):
    def epilogue(acc, e_refs, b_refs, o_refs, a_refs):
        x = e_refs[2][...]
        gl, th = _gelu_parts(x)
        dgl = 0.5 * (1.0 + th) + 0.5 * x * (1.0 - th * th) * (_GELU_C * (1.0 + 3.0 * 0.044715 * x * x))
        o_refs[0][...] = acc * gl
        o_refs[1][...] = (acc * (e_refs[0][...] + e_refs[1][...]) * dgl).astype(BF16)

    return _mm(dx, w_out, mode="nt", out_dtypes=(F32, BF16), out_cols=(D_MODEL, 2 * D_MODEL), extras=(h_f, h_b, z),
               ref_epi=epilogue, name=name, after=after)


def _row_block(i):
    return pl.ds(pl.multiple_of(i * SUBLANES, SUBLANES), SUBLANES)


def _rg_mix_fwd(z, conv_wb, wcat, gvec, B, L):
    nb = L // SUBLANES
    n_g = D_MODEL // LRU_BW

    def body(zg_ref, zr_ref, cw_ref, w_ref, gv_ref, rec_ref, af_s, ab_s, hf_ref, hb_ref, yg_ref, uf_s, ub_s):
        rec = _conv_apply(zr_ref[...], cw_ref, L)
        rec_ref[...] = rec
        pre = jnp.dot(rec.astype(BF16), w_ref[...], preferred_element_type=F32)
        for d, (a_s, u_s) in enumerate(((af_s, uf_s), (ab_s, ub_s))):
            a, u, _ = _gate_math(rec, pre, gv_ref, d, slice(None))
            a_s[...] = a
            u_s[...] = u

        def step(i, carry):
            c1, c2 = carry
            rows, rows_b = _row_block(i), _row_block(nb - 1 - i)
            p, h = _block_scan(af_s[rows, :], uf_s[rows, :], False)
            h = h + p * c1
            hf_ref[rows, :] = h
            p2, h2 = _block_scan(ab_s[rows_b, :], ub_s[rows_b, :], True)
            h2 = h2 + p2 * c2
            hb_ref[rows_b, :] = h2
            return h[_LAST:, :], h2[:1, :]

        zero = jnp.zeros((1, LRU_BW), F32)
        _loop_blocks(nb, step, (zero, zero))
        gl, _ = _gelu_parts(zg_ref[...])
        yg_ref[...] = ((hf_ref[...] + hb_ref[...]) * gl).astype(BF16)

    seq = lambda off: pl.BlockSpec((L, LRU_BW), lambda b, g: (b, off + g))
    vec = pl.BlockSpec((SUBLANES, LRU_BW), lambda b, g: (0, g))
    T = B * L
    return pl.pallas_call(
        body, name="rg_mix", grid=(B, n_g),
        in_specs=[seq(0), seq(n_g), vec, pl.BlockSpec((LRU_BW, 4 * LRU_BW), lambda b, g: (g, 0)), vec],
        out_specs=[seq(0)] * 6,
        out_shape=[jax.ShapeDtypeStruct((T, D_MODEL), F32)] * 5 + [jax.ShapeDtypeStruct((T, D_MODEL), BF16)],
        scratch_shapes=[pltpu.VMEM((L, LRU_BW), F32)] * 2,
        compiler_params=_params(("parallel", "parallel")),
    )(z, z, conv_wb, wcat, gvec)


def _make_wcat(w_a, w_x):
    g = jnp.stack([w_a[0, 0], w_x[0, 0], w_a[0, 1], w_x[0, 1]])
    return jnp.transpose(g, (1, 2, 0, 3)).reshape(D_MODEL, 4 * LRU_BW)


def _rows_at(part, first):
    return jnp.pad(part, ((first, SUBLANES - first - part.shape[0]), (0, 0)))


def _qk_slot(q_g, k_g):
    wide = lambda v, at: jnp.pad(v, ((0, SUBLANES - 1), (at, D_MODEL - at - HEAD_DIM)))
    return wide(q_g, 0) + wide(k_g, HEAD_DIM)


def _local_step(x, target, P, fetch, emit, B, L, after=None):
    g_mix, g_mlp = P["norm_mix_g"], P["norm_mlp_g"]
    h0 = _rms_fwd(x, g_mix[0:1], "rg_norm", after=after)
    w_in, conv_wb, wcat, gvec = fetch("rg", h0)
    z = _mm(h0, w_in, mode="nn", b_shard=True, name="rg_in")
    rec, a_f, a_b, h_f, h_b, yg = _rg_mix_fwd(z, conv_wb, wcat, gvec, B, L)
    w_out = fetch("rg_out", yg)
    x1, h1 = _mm_res_norm(yg, w_out, x, g_mlp[0:1], "rg_out")
    (x2, h3), mlp0 = _mlp_fwd(x1, h1, fetch, 0, lambda a, w, res, name: _mm_res_norm(a, w, res, g_mix[1:2], name))
    w_qkv, w_o = fetch("att", h3)
    qkv = _mm(h3, w_qkv, mode="nn", b_shard=True, name="attn_qkv")
    cos, sin = _rope_tables(L, B)
    qh, kh, vh = _qk_prep(qkv, cos, sin, P["q_g"], P["k_g"])
    o = _attn_fwd(qh, kh, vh, B, L)
    x3, h4 = _mm_res_norm(o, w_o, x2, g_mlp[1:2], "attn_out")
    (dx4, dx4_bf, loss_acc, d_final_g), mlp1 = _mlp_fwd(
        x3, h4, fetch, 1, lambda a, w, res, name: _mm_final_loss(a, w, res, target, P["final_g"], name))

    dx3, dx3_bf, dg_mlp1, d_up1, d_down1 = _mlp_bwd(x3, g_mlp[1:2], mlp1, dx4, dx4_bf, 1, None)
    tok = emit("mlp1", [d_up1, d_down1])
    d_wo = _mm(o, dx3_bf, mode="tn", out_dtypes=(BF16,), name="attn_dwo", after=tok)
    do = _mm(dx3_bf, w_o, mode="nt", out_dtypes=(BF16,), name="attn_do")
    dq, dk, dv = _attn_bwd(qh, kh, vh, o, do, B, L)
    dqkv, dq_g, dk_g = _qk_prep_bwd(qkv, dq, dk, dv, cos, sin, P["q_g"], P["k_g"])
    d_wqkv = _mm(h3, dqkv, mode="tn", o_shard=True, out_dtypes=(BF16,), name="attn_dwqkv")
    tok = emit("att", [d_wqkv, d_wo])
    dx2, dx2_bf, dg_mix1 = _mm_norm_bwd(dqkv, w_qkv, x2, dx3, g_mix[1:2], "attn_dh", after=tok)
    tok = emit("point_attn_done", [dx2_bf])
    dx1, dx1_bf, dg_mlp0, d_up0, d_down0 = _mlp_bwd(x1, g_mlp[0:1], mlp0, dx2, dx2_bf, 0, tok)
    d_wout = _mm(yg, dx1_bf, mode="tn", out_dtypes=(BF16,), name="rg_dwout")
    tok = emit("mlp0", [d_up0, d_down0, d_wout])
    dy, dgate = _mm_gated_out_bwd(dx1_bf, w_out, h_f, h_b, z, "rg_dyg", after=tok)
    du_f, da_f, du_b, da_b = _scan_bwd(dy, a_f, h_f, a_b, h_b, B, L)
    drec_c, d_wa, d_wx, d_gvec = _gate_bwd(rec, du_f, da_f, du_b, da_b, wcat, gvec)
    tok = emit("gates", [d_wa, d_wx])
    dz, d_convwb = _conv_bwd(z, drec_c, conv_wb, dgate, B, L, after=tok)
    tok = emit("point_mix_done", [dz])
    d_win = _mm(h0, dz, mode="tn", o_shard=True, out_dtypes=(BF16,), name="rg_dwin", after=tok)
    tok = emit("rg_in", [d_win])
    grad_x, _, dg_mix0 = _mm_norm_bwd(dz, w_in, x, dx1, g_mix[0:1], "rg_dh", after=tok)

    norms = (_rows_at(dg_mix0, 0) + _rows_at(dg_mix1, 1) + _rows_at(dg_mlp0, 2) + _rows_at(dg_mlp1, 3)
             + _rows_at(d_final_g, 4)
             + jnp.pad(loss_acc, ((LOSS_ROW, SUBLANES - 1 - LOSS_ROW), (0, D_MODEL - LANES))))
    vec = jnp.concatenate([norms, d_convwb, d_gvec, _qk_slot(dq_g, dk_g)], axis=0)
    return grad_x, vec


_MESH = pl.DeviceIdType.MESH


def _place():
    x, y, c = lax.axis_index("x"), lax.axis_index("y"), lax.axis_index("c")
    peers = [((1 - x) if j & 2 else x, (1 - y) if j & 1 else y) for j in (1, 2, 3)]
    return x, y, c, peers


def _sum_leading(slots, name):
    def body(s_ref, o_ref):
        acc = s_ref[0]
        for d in range(1, slots.shape[0]):
            acc = acc + s_ref[d]
        o_ref[...] = acc

    return pl.pallas_call(body, name=name, out_shape=jax.ShapeDtypeStruct(slots.shape[1:], slots.dtype))(slots)


_HBM = pl.BlockSpec(memory_space=pltpu.HBM)
_SEM = pl.BlockSpec(memory_space=pltpu.SEMAPHORE)
_EFFECT = pltpu.SideEffectType.DATAFLOW_SIDE_EFFECTING


_COPIES = dict(gather=N_CHIPS - 1, scatter=N_CHIPS - 1, swap=1, spread=N_DEVICES - 1,
               gather_half=N_CHIPS - 1, share_half=N_CHIPS - 1)


def _split_copies(kind, srcs, lands, send, recv):
    x, y, c, peers = _place()
    me = 2 * x + y
    per = _COPIES[kind]
    out = []
    for a in range(len(lands)):
        for j in range(per):
            if kind == "swap":
                src, there, here, dev = srcs[a], lands[a], lands[a], (x, y, 1 - c)
            elif kind == "spread":
                k = j + 1
                dev = ((1 - x) if k & 4 else x, (1 - y) if k & 2 else y, (1 - c) if k & 1 else c)
                mine = lands[a].at[4 * x + 2 * y + c]
                src, there, here = mine, mine, lands[a].at[4 * dev[0] + 2 * dev[1] + dev[2]]
            else:
                px, py = peers[j]
                dev = (px, py, c)
                if kind == "gather":
                    src, there, here = lands[a].at[me], lands[a].at[me], lands[a].at[2 * px + py]
                elif kind in ("gather_half", "share_half"):
                    half = lands[a].shape[1] // 2
                    mine, other = pl.ds(c * half, half), pl.ds((1 - c) * half, half)
                    if kind == "gather_half":
                        src = there = lands[a].at[me, mine]
                        here = lands[a].at[2 * px + py, mine]
                    else:
                        src = there = lands[a].at[2 * px + py, mine]
                        here = lands[a].at[2 * px + py, other]
                        dev = (x, y, 1 - c)
                else:
                    src, there, here = srcs[a].at[2 * px + py], lands[a].at[j], lands[a].at[j]
            mk = functools.partial(
                pltpu.make_async_remote_copy, src_ref=src, send_sem=send.at[per * a + j],
                recv_sem=recv.at[per * a + j], device_id=dev, device_id_type=_MESH)
            out.append((functools.partial(mk, dst_ref=there), functools.partial(mk, dst_ref=here)))
    return out


_CORE_PAIR = ("swap", "share_half")
CORE_PAIR_BARRIER_ID = 0
BARRIER_IDS = dict(gather_rest=1, scatter_mlp1=2, scatter_att=3, scatter_mlp0=4, scatter_gates=5, scatter_rg_in=6,
                   gather_gates=7, spread_vec=8)


def _entry_peers(kind):
    x, y, c, peers = _place()
    if kind in _CORE_PAIR:
        return [(x, y, 1 - c)]
    if kind == "spread":
        return [((1 - x) if k & 4 else x, (1 - y) if k & 2 else y, (1 - c) if k & 1 else c)
                for k in range(1, N_DEVICES)]
    return [(px, py, c) for px, py in peers]


def _entry_params(kind, barrier_id):
    if kind in _CORE_PAIR:
        barrier_id = CORE_PAIR_BARRIER_ID
    collective = {} if barrier_id is None else dict(collective_id=barrier_id)
    return pltpu.CompilerParams(has_side_effects=_EFFECT, **collective)


def _entry_handshake(kind, barrier_id):
    if kind in _CORE_PAIR or barrier_id is not None:
        barrier = pltpu.get_barrier_semaphore()
        peers = _entry_peers(kind)
        for peer in peers:
            pl.semaphore_signal(barrier, inc=1, device_id=peer, device_id_type=_MESH)
        pl.semaphore_wait(barrier, len(peers))


def _exchange_start(kind, srcs, lands, name, after=None, barrier_id=None):
    arrays = list(srcs) + list(lands)
    n_s, n, n_all = len(srcs), len(lands), len(srcs) + len(lands)
    n_sem = _COPIES[kind] * n
    order = _after_operand(after)
    n_x = len(order)

    def body(*refs):
        _entry_handshake(kind, barrier_id)
        send, recv = refs[n_all + n_x], refs[n_all + n_x + 1]
        token = refs[-1]
        for started, _ in _split_copies(kind, refs[:n_s], refs[n_s:n_all], send, recv):
            started().start()
        token[...] = jnp.zeros(token.shape, F32)

    res = pl.pallas_call(
        body, name=name,
        out_shape=(pltpu.SemaphoreType.DMA((n_sem,)), pltpu.SemaphoreType.DMA((n_sem,)),
                   *[pltpu.HBM(a.shape, a.dtype) for a in arrays], jax.ShapeDtypeStruct((SUBLANES, LANES), F32)),
        in_specs=[_HBM] * n_all + [_ANY] * n_x,
        out_specs=(_SEM, _SEM, *[_HBM] * n_all, pl.BlockSpec(memory_space=pltpu.VMEM)),
        input_output_aliases={i: 2 + i for i in range(n_all)},
        compiler_params=_entry_params(kind, barrier_id),
    )(*[pltpu.with_memory_space_constraint(a, pltpu.HBM) for a in arrays], *order)
    return (res[0], res[1], res[2:2 + n_s], res[2 + n_s:2 + n_all]), res[-1]


def _gather_start_groups(land_groups, name, after=None, kind="gather", barrier_id=None):
    arrays = [a for group in land_groups for a in group]
    n_all, n_g = len(arrays), len(land_groups)
    order = _after_operand(after)
    n_x = len(order)

    def body(*refs):
        _entry_handshake(kind, barrier_id)
        first = 0
        for gi, group in enumerate(land_groups):
            send, recv = refs[n_all + n_x + 2 * gi], refs[n_all + n_x + 2 * gi + 1]
            for started, _ in _split_copies(kind, [], refs[first:first + len(group)], send, recv):
                started().start()
            first += len(group)
        refs[-1][...] = jnp.zeros(refs[-1].shape, F32)

    sems = [pltpu.SemaphoreType.DMA((_COPIES[kind] * len(group),)) for group in land_groups for _ in range(2)]
    res = pl.pallas_call(
        body, name=name,
        out_shape=(*sems, *[pltpu.HBM(a.shape, a.dtype) for a in arrays], jax.ShapeDtypeStruct((SUBLANES, LANES), F32)),
        in_specs=[_HBM] * n_all + [_ANY] * n_x,
        out_specs=(*[_SEM] * (2 * n_g), *[_HBM] * n_all, pl.BlockSpec(memory_space=pltpu.VMEM)),
        input_output_aliases={i: 2 * n_g + i for i in range(n_all)},
        compiler_params=_entry_params(kind, barrier_id),
    )(*[pltpu.with_memory_space_constraint(a, pltpu.HBM) for a in arrays], *order)
    handles, first = [], 2 * n_g
    for gi, group in enumerate(land_groups):
        handles.append((res[2 * gi], res[2 * gi + 1], [], res[first:first + len(group)]))
        first += len(group)
    return handles, res[-1]


def _exchange_wait(kind, handle, after, name):
    send, recv, srcs, lands = handle
    arrays = list(srcs) + list(lands)
    n_s, n_all = len(srcs), len(arrays)
    order = list(after) if isinstance(after, (list, tuple)) else [after]

    def body(*refs):
        for started, landing in _split_copies(kind, refs[:n_s], refs[n_s:n_all], refs[n_all], refs[n_all + 1]):
            started().wait_send()
            landing().wait_recv()

    res = pl.pallas_call(
        body, name=name, out_shape=[pltpu.HBM(a.shape, a.dtype) for a in arrays],
        in_specs=[_HBM] * n_all + [_SEM, _SEM] + [_ANY] * len(order), out_specs=[_HBM] * n_all,
        input_output_aliases={i: i for i in range(n_all)},
        compiler_params=pltpu.CompilerParams(has_side_effects=_EFFECT),
    )(*arrays, send, recv, *order)
    return res[:n_s], res[n_s:]


def _index_operand(i):
    return jnp.reshape(i, (1,)).astype(jnp.int32)


def _cast_into_slot(src, row0, rows, me, dtype, name, after=None, add=None, n_slots=N_CHIPS):
    cols = src.shape[1]
    tm = min(512, rows)
    order = _after_operand(after)
    terms = [src] + ([] if add is None else [add])

    def body(me_ref, *rest):
        val = rest[0][...]
        if add is not None:
            val = val + rest[1][...]
        rest[-1][...] = val.astype(dtype)

    return pl.pallas_call(
        body, name=name,
        grid_spec=pltpu.PrefetchScalarGridSpec(
            num_scalar_prefetch=1, grid=(rows // tm,),
            in_specs=[pl.BlockSpec((tm, cols), lambda i, me_ref: (i + row0 // tm, 0))] * len(terms)
            + [_ANY] * len(order),
            out_specs=pl.BlockSpec((None, tm, cols), lambda i, me_ref: (me_ref[0], i, 0))),
        out_shape=jax.ShapeDtypeStruct((n_slots, rows, cols), dtype), compiler_params=_params(("parallel",)),
    )(_index_operand(me), *terms, *order)


def _sum_slots(mine, r, me, name):
    _, rows, cols = r.shape
    tm = min(512, rows)

    def body(me_ref, own_ref, r_ref, o_ref):
        o_ref[...] = ((own_ref[...].astype(F32) + r_ref[0].astype(F32)) + r_ref[1].astype(F32)) + r_ref[2].astype(F32)

    return pl.pallas_call(
        body, name=name,
        grid_spec=pltpu.PrefetchScalarGridSpec(
            num_scalar_prefetch=1, grid=(rows // tm,),
            in_specs=[pl.BlockSpec((None, tm, cols), lambda i, me_ref: (me_ref[0], i, 0)),
                      pl.BlockSpec((N_CHIPS - 1, tm, cols), lambda i, me_ref: (0, i, 0))],
            out_specs=pl.BlockSpec((tm, cols), lambda i, me_ref: (i, 0))),
        out_shape=jax.ShapeDtypeStruct((rows, cols), F32), compiler_params=_params(("parallel",)),
    )(_index_operand(me), mine, r)


def _adamw(w, m, v, ps, qs, name):
    rows, cols = w.shape
    seg_rows = ps[0].shape[0]
    tm = min(512, seg_rows)
    while seg_rows % tm:
        tm -= SUBLANES
    per, n_seg = seg_rows // tm, len(ps)
    parts = list(ps) + ([] if qs is None else list(qs))

    def body(w_ref, m_ref, v_ref, *rest):
        g_refs, outs = rest[:len(parts)], rest[len(parts):]
        grad = lambda s: g_refs[s][...] if qs is None else g_refs[s][...] + g_refs[n_seg + s][...]
        g = grad(0)
        for s in range(1, n_seg):
            g = jnp.where(pl.program_id(0) >= s * per, grad(s), g)
        m1 = ADAM_B1 * m_ref[...] + (1.0 - ADAM_B1) * g
        v1 = ADAM_B2 * v_ref[...] + (1.0 - ADAM_B2) * (g * g)
        m_hat = m1 / (1.0 - ADAM_B1 ** ADAM_STEP)
        v_hat = v1 / (1.0 - ADAM_B2 ** ADAM_STEP)
        outs[0][...] = g
        outs[1][...] = (-ADAM_LR) * (m_hat / (jnp.sqrt(v_hat) + ADAM_EPS) + ADAM_WD * w_ref[...])
        outs[2][...] = m1
        outs[3][...] = v1

    row_spec = pl.BlockSpec((tm, cols), lambda i: (i, 0))
    seg_spec = lambda s: pl.BlockSpec((tm, cols), lambda i: (jnp.clip(i - s * per, 0, per - 1), 0))
    return pl.pallas_call(
        body, name=name, grid=(rows // tm,),
        in_specs=[row_spec] * 3 + [seg_spec(s) for s in range(n_seg)] * (1 if qs is None else 2),
        out_specs=[row_spec] * 4, out_shape=[jax.ShapeDtypeStruct((rows, cols), F32)] * 4,
        compiler_params=_params(("arbitrary",)),
    )(w, m, v, *parts)


def _put_cols(shard, me):
    full = jnp.zeros((shard.shape[0], D_MODEL), F32)
    return lax.dynamic_update_slice(full, shard, (0, me * (D_MODEL // N_CHIPS)))


def _gate_vec_slot(b_a, b_x, lam):
    return _rows_at(b_a, _ROW_BA) + _rows_at(b_x, _ROW_BX) + _rows_at(lam, _ROW_LAM)


def _pack_vec(p, me):
    return jnp.concatenate([
        _rows_at(p["norm_mix_g"], 0) + _rows_at(p["norm_mlp_g"], 2) + _rows_at(p["final_g"][None], 4),
        _rows_at(_put_cols(p["rg_conv_w"][0, :, 0, :], me), 0) + _rows_at(p["rg_conv_b"], 4),
        _gate_vec_slot(_put_cols(p["rg_b_a"][0], me), _put_cols(p["rg_b_x"][0], me), _put_cols(p["rg_lam"][0], me)),
        _qk_slot(p["at_q_g"], p["at_k_g"]),
    ], axis=0)


def _unpack_vec(r, me):
    def cols(rows):
        return lax.dynamic_slice(rows, (0, me * (D_MODEL // N_CHIPS)), (rows.shape[0], D_MODEL // N_CHIPS))

    gate = r[16:24]
    return dict(
        norm_mix_g=r[0:2], norm_mlp_g=r[2:4], final_g=r[4], rg_conv_w=cols(r[8:12])[None, :, None, :],
        rg_conv_b=r[12:13], rg_b_a=cols(gate[_ROW_BA:_ROW_BA + 2])[None], rg_b_x=cols(gate[_ROW_BX:_ROW_BX + 2])[None],
        rg_lam=cols(gate[_ROW_LAM:_ROW_LAM + 2])[None], at_q_g=r[24:25, 0:HEAD_DIM],
        at_k_g=r[24:25, HEAD_DIM:2 * HEAD_DIM])


_WEIGHTS = ['norm_mix_g', 'norm_mlp_g', 'rg_w_in', 'rg_conv_w', 'rg_conv_b', 'rg_w_a', 'rg_b_a', 'rg_w_x', 'rg_b_x',
            'rg_lam', 'rg_w_out', 'at_w_qkv', 'at_q_g', 'at_k_g', 'at_w_o', 'mlp_w_up', 'mlp_w_down', 'final_g']
_BIG = dict(rg_w_in=["rg_w_in"], rg_w_out=["rg_w_out"], at_w_qkv=["at_w_qkv"], at_w_o=["at_w_o"],
            mlp_w_up=["up0", "up1"], mlp_w_down=["down0", "down1"])


def kernel(x, *args):
    n_w = len(_WEIGHTS)
    w = dict(zip(_WEIGHTS, args[:n_w]))
    target = args[n_w]
    m = dict(zip(_WEIGHTS, args[n_w + 1:2 * n_w + 1]))
    v = dict(zip(_WEIGHTS, args[2 * n_w + 1:3 * n_w + 1]))
    B, L, _ = x.shape
    T = B * L
    me = 2 * lax.axis_index("x") + lax.axis_index("y")

    vec = jnp.concatenate([_gate_vec_slot(w["rg_b_a"][0], w["rg_b_x"][0], w["rg_lam"][0]),
                           _rows_at(w["rg_conv_w"][0, :, 0, :], 0)], axis=0)
    flat = lambda a: a.reshape(-1, a.shape[-1])
    rows_of = lambda k: w[k].shape[-2]
    groups = [("rg", [("rg_w_in", 0, BF16), (vec, 0, F32)]), ("rg_out", [("rg_w_out", 0, BF16)]),
              ("mlp0_up", [("mlp_w_up", 0, BF16)]), ("mlp0_down", [("mlp_w_down", 0, BF16)]),
              ("att", [("at_w_qkv", 0, BF16), ("at_w_o", 0, BF16)]),
              ("mlp1", [("mlp_w_up", 1, BF16), ("mlp_w_down", 1, BF16)])]

    def landing_zones(group, members, after):
        lands = []
        for n, (k, layer, dtype) in enumerate(members):
            src, rows = (flat(w[k]), rows_of(k)) if isinstance(k, str) else (k, k.shape[0])
            lands.append(_cast_into_slot(src, layer * rows, rows, me, dtype, f"place_{group}{n}", after=after))
        return lands

    halves, gathers = {}, {}
    halves["rg"], tok = _exchange_start("gather_half", [], landing_zones(*groups[0], None), "gather_rg_start")
    handles, tok = _gather_start_groups([landing_zones(g, members, tok) for g, members in groups[1:]],
                                        "gather_rest_start", after=tok, kind="gather_half",
                                        barrier_id=BARRIER_IDS["gather_rest"])
    halves.update(zip([g for g, _ in groups[1:]], handles))
    wcat = _make_wcat(w["rg_w_a"], w["rg_w_x"]).astype(BF16)

    packs = [_pack_vec(p, me) for p in (w, m, v)]

    ready = {}

    def share(some, after, name):
        landed = [_exchange_wait("gather_half", halves[g], after, f"gather_{g}_landed")[1] for g in some]
        handles, _ = _gather_start_groups(landed, name, kind="share_half")
        gathers.update(zip(some, handles))

    def fetch(what, after):
        if what in ready:
            return ready[what]
        group = "mlp1" if what.startswith("mlp1") else what
        if group == "rg":
            share(["rg"], [after, wcat] + packs, "share_rg_start")
        elif group == "rg_out":
            share(["rg_out", "mlp0_up", "mlp0_down", "att"], after, "share_early_start")
        _, full = _exchange_wait("share_half", gathers[group], after, f"gather_{group}_wait")
        if group == "att":
            share(["mlp1"], after, "share_mlp1_start")
        if group == "rg":
            vec_full = jnp.transpose(full[1], (1, 0, 2)).reshape(2 * SUBLANES, D_MODEL)
            conv_wb = vec_full[SUBLANES:] + _rows_at(w["rg_conv_b"], 4)
            return full[0], conv_wb, wcat, vec_full[:SUBLANES]
        if group == "rg_out":
            return full[0].reshape(D_MODEL, D_MODEL)
        if group == "att":
            return full[0], full[1].reshape(D_MODEL, D_MODEL)
        if group == "mlp1":
            ready["mlp1_up"], ready["mlp1_down"] = full[0], full[1].reshape(4 * D_MODEL, D_MODEL)
            return ready[what]
        return full[0] if group == "mlp0_up" else full[0].reshape(4 * D_MODEL, D_MODEL)

    names = dict(mlp1=["up1", "down1"], att=["at_w_qkv", "at_w_o"], mlp0=["up0", "down0", "rg_w_out"],
                 rg_in=["rg_w_in"], gates=["rg_w_a", "rg_w_x"])
    scatters, swaps, P, Q, res = {}, [], {}, {}, {}

    def start_scatter(group, grads):
        srcs = [g.reshape(N_CHIPS, -1, g.shape[-1]) for g in grads]
        lands = [lax.empty((N_CHIPS - 1,) + s.shape[1:], s.dtype) for s in srcs]
        scatters[group], token = _exchange_start("scatter", srcs, lands, f"scatter_{group}_start",
                                                 barrier_id=BARRIER_IDS[f"scatter_{group}"])
        return token

    def settle(groups, after):
        keys, parts = [], []
        for group in groups:
            srcs, lands = _exchange_wait("scatter", scatters[group], after, f"scatter_{group}_wait")
            for k, s, r in zip(names[group], srcs, lands):
                keys.append(k)
                parts.append(_sum_slots(s, r, me, f"sum_{k}"))
        handle, token = _exchange_start("swap", parts, [lax.empty(p.shape, F32) for p in parts],
                                        f"swap_{groups[0]}_start")
        swaps.append((keys, handle, f"swap_{groups[0]}_wait"))
        return token

    def finish(after):
        for keys, handle, name in swaps:
            mine, theirs = _exchange_wait("swap", handle, after, name)
            P.update(zip(keys, mine))
            Q.update(zip(keys, theirs))
        swaps.clear()
        last = after
        for k, parts in _BIG.items():
            if k in res or any(p not in P for p in parts):
                continue
            shape = w[k].shape
            two_d = lambda a: a.reshape(-1, shape[-1])
            outs = _adamw(two_d(w[k]), two_d(m[k]), two_d(v[k]), [P[p] for p in parts], [Q[p] for p in parts],
                          f"adamw_{k}")
            res[k] = [o.reshape(shape) for o in outs]
            last = outs[0]
        if "rg_w_a" in P and "gates" not in gathers:
            lands = [_cast_into_slot(P[k], 0, P[k].shape[0], me, F32, f"place_{k}", after=last, add=Q[k])
                     for k in names["gates"]]
            gathers["gates"], last = _exchange_start("gather", [], lands, "gather_gates_start", after=last,
                                                     barrier_id=BARRIER_IDS["gather_gates"])
        return last

    def emit(event, arrays):
        if event == "point_attn_done":
            return None
        if event == "point_mix_done":
            return settle(["mlp1", "att", "mlp0"], arrays[0])
        token = start_scatter(event, arrays)
        if event == "rg_in":
            return finish(settle(["gates"], token))
        return token

    P_vec = dict(norm_mix_g=w["norm_mix_g"], norm_mlp_g=w["norm_mlp_g"], final_g=w["final_g"][None],
                 q_g=w["at_q_g"], k_g=w["at_k_g"])
    grad_x, vec_part = _local_step(x.reshape(T, D_MODEL), target.reshape(T, D_MODEL), P_vec, fetch, emit, B, L,
                                   after=tok)

    me8 = 2 * me + lax.axis_index("c")
    vec_slots = _cast_into_slot(vec_part, 0, VEC_ROWS, me8, F32, "place_vec", n_slots=N_DEVICES)
    spread, tok = _exchange_start("spread", [], [vec_slots], "spread_vec_start", barrier_id=BARRIER_IDS["spread_vec"])
    last = finish(settle(["rg_in"], tok))
    _, gate_grads = _exchange_wait("gather", gathers["gates"], last, "gather_gates_wait")
    for k, g in zip(names["gates"], gate_grads):
        two_d = lambda a: a.reshape(g.shape[0] * g.shape[1], g.shape[2])
        outs = _adamw(two_d(w[k]), two_d(m[k]), two_d(v[k]), [two_d(g)], None, f"adamw_{k}")
        res[k] = [o.reshape(w[k].shape) for o in outs]
        last = outs[0]
    _, (vec_all,) = _exchange_wait("spread", spread, last, "spread_vec_wait")
    vec_grad = _sum_leading(vec_all, "sum_vec")
    loss = vec_grad[LOSS_ROW, 0]
    outs = _adamw(*packs, [vec_grad], None, "adamw_vec")
    unpacked = [_unpack_vec(o, me) for o in outs]
    for k in _WEIGHTS:
        if k not in res:
            res[k] = [u[k] for u in unpacked]

    result = [loss, grad_x.reshape(B, L, D_MODEL)]
    for slot in range(4):
        result += [res[k][slot] for k in _WEIGHTS]
    return tuple(result)
```

```python
import functools
import math

import jax
import jax.numpy as jnp
import numpy as np
from jax import lax
from jax.experimental import pallas as pl
from jax.experimental.pallas import tpu as pltpu

F32 = jnp.float32
BF16 = jnp.bfloat16

D_MODEL = 1024
HEAD_DIM = 128
N_HEADS = 8
N_KV = 2
GROUP = N_HEADS // N_KV
LRU_BLOCKS = 8
LRU_BW = 128
GRID_W = 64
ROPE_THETA = 10000.0
EPS = 1e-6
RG_C = 8.0
SCALE = 1.0 / math.sqrt(HEAD_DIM)
N_CHIPS = 4

ADAM_LR = 0.001
ADAM_B1 = 0.9
ADAM_B2 = 0.999
ADAM_EPS = 1e-08
ADAM_WD = 0.01
ADAM_STEP = 10

V7X_VMEM_BYTES = 64 * 1024 * 1024
VMEM_LIMIT = V7X_VMEM_BYTES * 3 // 4
LANES = 128
SUBLANES = 8

N_DEVICES = 8
VEC_ROWS = 32
LOSS_ROW = 5


def _params(sem):
    return pltpu.CompilerParams(dimension_semantics=sem, vmem_limit_bytes=VMEM_LIMIT)


_ANY = pl.BlockSpec(memory_space=pl.ANY)
_NN = (((1,), (0,)), ((), ()))
_NT = (((1,), (1,)), ((), ()))
_TN = (((0,), (0,)), ((), ()))


def _after_operand(after):
    return [] if after is None else [after]


def _fit(t, n):
    if n <= t:
        return n
    c = (t // LANES) * LANES
    while n % c:
        c -= LANES
    return c


MM_VMEM_BUDGET = VMEM_LIMIT * 3 // 4
def _mm_tiles(M, K, ns, n_total, out_dtypes, extras, whole_rows):
    for tm in (2048, 1024, 512, 256, 128):
        for tn in ((ns,) if whole_rows else (1024, 512, 256)):
            tn = _fit(tn, ns)
            per_row = 2 * (2 * K) + 4 * tn + sum(2 * tn * jnp.dtype(d).itemsize for d in out_dtypes)
            per_row += sum(2 * tn * e.dtype.itemsize for e in extras)
            b_buffers = 1 if tn == n_total else 2
            if M % tm == 0 and b_buffers * (2 * K * tn) + tm * per_row <= MM_VMEM_BUDGET:
                return tm, tn
    raise ValueError(f"no tile fits VMEM for M={M} K={K} N={ns}")


def _mm(a, b, *, mode, name, out_dtypes=(F32,), b_shard=False, o_shard=False, extras=(), epi=None, after=None,
        bcast=(), accs=(), ref_epi=None, out_cols=None):
    if mode == "tn":
        K, M = a.shape
        N = b.shape[1]
    else:
        M, K = a.shape
        if mode == "nn":
            N = b.shape[0] * b.shape[2] if b_shard else b.shape[1]
        else:
            N = b.shape[1] if b_shard else b.shape[0]
    ns = N
    if b_shard and mode == "nn":
        ns = b.shape[2]
    elif o_shard:
        ns = N // N_CHIPS
    tm, tn = _mm_tiles(M, K, ns, N, out_dtypes, extras, whole_rows=ref_epi is not None)
    if ref_epi is not None:
        tm = min(tm, 512)
    grid = (M // tm, N // tn)
    q = ns // tn
    once = dict(pipeline_mode=pl.Buffered(1)) if tn == N else {}

    if mode == "tn":
        a_spec = pl.BlockSpec((K, tm), lambda i, j: (0, i))
        b_spec = pl.BlockSpec((K, tn), lambda i, j: (0, j), **once)
        dims = _TN
    elif mode == "nn":
        a_spec = pl.BlockSpec((tm, K), lambda i, j: (i, 0))
        if b_shard:
            b_spec = pl.BlockSpec((None, K, tn), lambda i, j: (j // q, 0, j % q), **once)
        else:
            b_spec = pl.BlockSpec((K, tn), lambda i, j: (0, j), **once)
        dims = _NN
    else:
        a_spec = pl.BlockSpec((tm, K), lambda i, j: (i, 0))
        if b_shard:
            ks = b.shape[2]
            b_spec = pl.BlockSpec((N_CHIPS, tn, ks), lambda i, j: (0, j, 0), **once)
        else:
            b_spec = pl.BlockSpec((tn, K), lambda i, j: (j, 0), **once)
        dims = _NT

    if o_shard:
        o_specs = [pl.BlockSpec((None, tm, tn), lambda i, j: (j // q, i, j % q))]
        o_shapes = [jax.ShapeDtypeStruct((N_CHIPS, M, ns), out_dtypes[0])]
    else:
        o_specs = [pl.BlockSpec((tm, tn), lambda i, j: (i, j)) for _ in out_dtypes]
        o_shapes = [jax.ShapeDtypeStruct((M, N if out_cols is None else out_cols[n]), dt)
                    for n, dt in enumerate(out_dtypes)]
    e_specs = [pl.BlockSpec((tm, tn), lambda i, j: (i, j)) for _ in extras]
    e_specs += [pl.BlockSpec(v.shape, lambda i, j: (0, 0)) for v in bcast]
    o_specs += [pl.BlockSpec(s, lambda i, j: (0, 0)) for s in accs]
    o_shapes += [jax.ShapeDtypeStruct(s, F32) for s in accs]
    n_e, n_b, n_o, n_a = len(extras), len(bcast), len(out_dtypes), len(accs)
    order = _after_operand(after)
    n_x = len(order)
    if epi is None:
        epi = lambda acc: (acc,)

    def body(a_ref, b_ref, *rest):
        e_refs, b_refs = rest[:n_e], rest[n_e:n_e + n_b]
        o_refs = rest[n_e + n_b + n_x:n_e + n_b + n_x + n_o]
        a_refs = rest[n_e + n_b + n_x + n_o:]
        if n_a:
            @pl.when((pl.program_id(0) == 0) & (pl.program_id(1) == 0))
            def _():
                for r in a_refs:
                    r[...] = jnp.zeros(r.shape, F32)
        if mode == "nt" and b_shard:
            acc = None
            for s in range(N_CHIPS):
                part = lax.dot_general(a_ref[:, s * ks:(s + 1) * ks], b_ref[s], dims, preferred_element_type=F32)
                acc = part if acc is None else acc + part
        else:
            acc = lax.dot_general(a_ref[...], b_ref[...], dims, preferred_element_type=F32)
        if ref_epi is not None:
            ref_epi(acc, e_refs, b_refs, o_refs, a_refs)
            return
        outs = epi(acc, *[r[...] for r in e_refs])
        for r, o in zip(o_refs, outs):
            r[...] = o.astype(r.dtype)

    outs = pl.pallas_call(
        body, name=name, grid=grid, in_specs=[a_spec, b_spec] + e_specs + [_ANY] * n_x, out_specs=o_specs,
        out_shape=o_shapes, compiler_params=_params(("arbitrary", "arbitrary") if n_a else ("parallel", "parallel")),
    )(a, b, *extras, *bcast, *order)
    return outs[0] if n_o + n_a == 1 else outs


def _rowwise(fn, rows, bcast, outs, accs=(), *, tm, name, after=None):
    def norm(r):
        return r if isinstance(r, tuple) else (r, r.shape[1], 0)

    rows = [norm(r) for r in rows]
    T = rows[0][0].shape[0]
    tm = min(tm, T)
    while T % tm:
        tm -= SUBLANES
    n_r, n_b, n_o, n_a = len(rows), len(bcast), len(outs), len(accs)
    order = _after_operand(after)
    n_x = len(order)
    in_specs = [pl.BlockSpec((tm, c), functools.partial(lambda i, cb: (i, cb), cb=cb)) for _, c, cb in rows]
    in_specs += [pl.BlockSpec(b.shape, lambda i: (0, 0)) for b in bcast] + [_ANY] * n_x
    out_specs = [pl.BlockSpec((tm, o[0]), lambda i: (i, 0)) for o in outs]
    out_specs += [pl.BlockSpec(s, lambda i: (0, 0)) for s in accs]
    out_shape = [jax.ShapeDtypeStruct((T, o[2] if len(o) > 2 else o[0]), o[1]) for o in outs]
    out_shape += [jax.ShapeDtypeStruct(s, F32) for s in accs]

    def body(*refs):
        in_refs = refs[:n_r]
        b_refs = refs[n_r:n_r + n_b]
        o_refs = refs[n_r + n_b + n_x:n_r + n_b + n_x + n_o]
        a_refs = refs[n_r + n_b + n_x + n_o:]
        if n_a:
            @pl.when(pl.program_id(0) == 0)
            def _():
                for r in a_refs:
                    r[...] = jnp.zeros(r.shape, F32)
        fn(in_refs, b_refs, o_refs, a_refs)

    res = pl.pallas_call(
        body, name=name, grid=(T // tm,), in_specs=in_specs, out_specs=out_specs, out_shape=out_shape,
        compiler_params=_params(("arbitrary",) if n_a else ("parallel",)),
    )(*[r[0] for r in rows], *bcast, *order)
    return res


def _rsum(x):
    return jnp.sum(x, axis=0, keepdims=True)


def _rms_fwd(x, g, name, after=None):
    def fn(ins, bs, outs, accs):
        xv = ins[0][...]
        r = lax.rsqrt(jnp.mean(xv * xv, axis=-1, keepdims=True) + EPS)
        outs[0][...] = (xv * r * bs[0][...]).astype(BF16)

    return _rowwise(fn, [x], [g], [(D_MODEL, BF16)], tm=512, name=name, after=after)[0]


def _rms_bwd_math(xv, dh, g):
    r = lax.rsqrt(jnp.mean(xv * xv, axis=-1, keepdims=True) + EPS)
    hn = xv * r
    dgh = dh * g
    dx = r * (dgh - hn * jnp.mean(dgh * hn, axis=-1, keepdims=True))
    return dx, _rsum(dh * hn)


def _mm_norm_bwd(dy, w, x, dres, g, name, after=None):
    def epilogue(acc, e_refs, b_refs, o_refs, a_refs):
        dx, dg = _rms_bwd_math(e_refs[0][...], acc, b_refs[0][...])
        dx = dx + e_refs[1][...]
        o_refs[0][...] = dx
        o_refs[1][...] = dx.astype(BF16)
        a_refs[0][...] += dg

    return _mm(dy, w, mode="nt", b_shard=True, out_dtypes=(F32, BF16), extras=(x, dres), bcast=(g,),
               accs=((1, D_MODEL),), ref_epi=epilogue, name=name, after=after)


def _mm_res_norm(a, w, res, g, name):
    def epilogue(acc, e_refs, b_refs, o_refs, a_refs):
        xv = acc + e_refs[0][...]
        o_refs[0][...] = xv
        r = lax.rsqrt(jnp.mean(xv * xv, axis=-1, keepdims=True) + EPS)
        o_refs[1][...] = (xv * r * b_refs[0][...]).astype(BF16)

    return _mm(a, w, mode="nn", out_dtypes=(F32, BF16), extras=(res,), bcast=(g,), ref_epi=epilogue, name=name)


def _mm_final_loss(a, w, res, target, g, name):
    def epilogue(acc, e_refs, b_refs, o_refs, a_refs):
        xv = acc + e_refs[0][...]
        gv = b_refs[0][...]
        r = lax.rsqrt(jnp.mean(xv * xv, axis=-1, keepdims=True) + EPS)
        e = xv * r * gv - e_refs[1][...]
        tok = jnp.mean(e * e, axis=-1, keepdims=True)
        a_refs[0][...] += 0.5 * jnp.sum(tok, axis=0, keepdims=True) * jnp.ones((1, LANES), F32)
        dx, dg = _rms_bwd_math(xv, e * (1.0 / D_MODEL), gv)
        o_refs[0][...] = dx
        o_refs[1][...] = dx.astype(BF16)
        a_refs[1][...] += dg

    return _mm(a, w, mode="nn", out_dtypes=(F32, BF16), extras=(res, target), bcast=(g,),
               accs=((1, LANES), (1, D_MODEL)), ref_epi=epilogue, name=name)


def _relu2(acc):
    r = jnp.maximum(acc, 0.0)
    return r * r, r


def _mlp_fwd(x, h, fetch, tag, finish):
    w_up = fetch(f"mlp{tag}_up", h)
    a, r = _mm(h, w_up, mode="nn", b_shard=True, out_dtypes=(BF16, BF16), epi=_relu2, name=f"mlp{tag}_up")
    w_down = fetch(f"mlp{tag}_down", a)
    return finish(a, w_down, x, f"mlp{tag}_down"), (h, a, r, w_up, w_down)


def _mlp_bwd(x, g, saved, dx, dx_bf, tag, after):
    h, a, r, w_up, w_down = saved
    d_down = _mm(a, dx_bf, mode="tn", out_dtypes=(BF16,), name=f"mlp{tag}_dwdown", after=after)
    dup = _mm(dx_bf, w_down, mode="nt", extras=(r,), out_dtypes=(BF16,),
              epi=lambda acc, rv: (acc * (2.0 * rv.astype(F32)),), name=f"mlp{tag}_dup")
    d_up = _mm(h, dup, mode="tn", o_shard=True, out_dtypes=(BF16,), name=f"mlp{tag}_dwup")
    dx_new, dx_new_bf, dg = _mm_norm_bwd(dup, w_up, x, dx, g, f"mlp{tag}_dh")
    return dx_new, dx_new_bf, dg, d_up, d_down


def _rope_tables(L, B):
    rows = L // GRID_W
    row = np.repeat(np.arange(rows, dtype=np.float32), GRID_W)
    col = np.tile(np.arange(GRID_W, dtype=np.float32), rows)
    inv = (ROPE_THETA ** (-np.arange(HEAD_DIM // 4, dtype=np.float32) / (HEAD_DIM // 4))).astype(np.float32)
    ar, ac = row[:, None] * inv, col[:, None] * inv
    cos = np.concatenate([np.cos(ar), np.cos(ar), np.cos(ac), np.cos(ac)], axis=-1)
    sin = np.concatenate([-np.sin(ar), np.sin(ar), -np.sin(ac), np.sin(ac)], axis=-1)
    return jnp.asarray(np.tile(cos, (B, 1)), F32), jnp.asarray(np.tile(sin, (B, 1)), F32)


def _swap_halves(x):
    lane = lax.broadcasted_iota(jnp.int32, x.shape, 1)
    return jnp.where((lane % 64) < 32, pltpu.roll(x, HEAD_DIM - 32, 1), pltpu.roll(x, 32, 1))


def _qk_prep(qkv, cos, sin, q_g, k_g):
    def fn(ins, bs, outs, accs):
        c, s = ins[1][...], ins[2][...]
        for h in range(N_HEADS + N_KV):
            xv = ins[0][:, h * HEAD_DIM:(h + 1) * HEAD_DIM]
            g = bs[0][...] if h < N_HEADS else bs[1][...]
            r = lax.rsqrt(jnp.mean(xv * xv, axis=-1, keepdims=True) + EPS)
            z = xv * r * g
            y = (z * c + _swap_halves(z) * s).astype(BF16)
            if h < N_HEADS:
                outs[0][:, h * HEAD_DIM:(h + 1) * HEAD_DIM] = y
            else:
                outs[1][:, (h - N_HEADS) * HEAD_DIM:(h - N_HEADS + 1) * HEAD_DIM] = y
        outs[2][...] = ins[0][:, (N_HEADS + N_KV) * HEAD_DIM:].astype(BF16)

    kvw = N_KV * HEAD_DIM
    return _rowwise(fn, [qkv, cos, sin], [q_g, k_g], [(D_MODEL, BF16), (kvw, BF16), (kvw, BF16)], tm=512,
                    name="attn_qk_prep")


def _qk_prep_bwd(qkv, dq, dk, dv, cos, sin, q_g, k_g):
    def fn(ins, bs, outs, accs):
        c, s = ins[4][...], ins[5][...]
        for h in range(N_HEADS + N_KV):
            sl = slice(h * HEAD_DIM, (h + 1) * HEAD_DIM)
            xv = ins[0][:, sl]
            if h < N_HEADS:
                g, dy, acc = bs[0][...], ins[1][:, sl], accs[0]
            else:
                ks = slice((h - N_HEADS) * HEAD_DIM, (h - N_HEADS + 1) * HEAD_DIM)
                g, dy, acc = bs[1][...], ins[2][:, ks], accs[1]
            r = lax.rsqrt(jnp.mean(xv * xv, axis=-1, keepdims=True) + EPS)
            xn = xv * r
            dz = dy * c - _swap_halves(dy) * s
            acc[...] += _rsum(dz * xn)
            dxn = dz * g
            outs[0][:, sl] = (r * (dxn - xn * jnp.mean(dxn * xn, axis=-1, keepdims=True))).astype(BF16)
        outs[0][:, (N_HEADS + N_KV) * HEAD_DIM:] = ins[3][...].astype(BF16)

    return _rowwise(fn, [qkv, dq, dk, dv, cos, sin], [q_g, k_g], [(qkv.shape[1], BF16)],
                    [(1, HEAD_DIM), (1, HEAD_DIM)], tm=512, name="attn_qk_prep_bwd")


_EXP2_SCALE = SCALE * math.log2(math.e)


def _exp_rows(q, k):
    s = lax.dot_general(q, k, _NT, preferred_element_type=F32)
    p = jnp.exp2((s - jnp.max(s, axis=-1, keepdims=True)) * _EXP2_SCALE)
    return p, jnp.sum(p, axis=-1, keepdims=True)


def _attn_fwd(q, k, v, B, L, tq=2048, sub=256):
    tq = min(tq, L)
    sub = min(sub, tq)
    nq = L // tq

    def body(q_ref, k_ref, v_ref, o_ref):
        kv, vv = k_ref[...], v_ref[...]
        for c in range(tq // sub):
            rows = slice(c * sub, (c + 1) * sub)
            p, l = _exp_rows(q_ref[rows, :], kv)
            o = jnp.dot(p.astype(BF16), vv, preferred_element_type=F32)
            o_ref[rows, :] = (o * (1.0 / l)).astype(o_ref.dtype)

    return pl.pallas_call(
        body, name="attn_fwd", grid=(B, N_HEADS, nq),
        in_specs=[pl.BlockSpec((tq, HEAD_DIM), lambda b, h, i: (b * nq + i, h)),
                  pl.BlockSpec((L, HEAD_DIM), lambda b, h, i: (b, h // GROUP)),
                  pl.BlockSpec((L, HEAD_DIM), lambda b, h, i: (b, h // GROUP))],
        out_specs=pl.BlockSpec((tq, HEAD_DIM), lambda b, h, i: (b * nq + i, h)),
        out_shape=jax.ShapeDtypeStruct((B * L, D_MODEL), BF16),
        compiler_params=_params(("parallel", "parallel", "parallel")),
    )(q, k, v)


def _attn_bwd(q, k, v, o, do, B, L, tq=2048, sub=512):
    tq = min(tq, L)
    sub = min(sub, tq)
    nq = L // tq

    def body(q_ref, k_ref, v_ref, o_ref, do_ref, dq_ref, dk_ref, dv_ref):
        @pl.when((pl.program_id(2) == 0) & (pl.program_id(3) == 0))
        def _():
            dk_ref[...] = jnp.zeros(dk_ref.shape, F32)
            dv_ref[...] = jnp.zeros(dv_ref.shape, F32)

        kv, vv = k_ref[...], v_ref[...]
        ps, es, dos, qs = [], [], [], []
        for c in range(tq // sub):
            rows = slice(c * sub, (c + 1) * sub)
            qc, doc = q_ref[rows, :], do_ref[rows, :]
            p, l = _exp_rows(qc, kv)
            inv = 1.0 / l
            dp = lax.dot_general(doc, vv, _NT, preferred_element_type=F32)
            delta = jnp.sum(doc.astype(F32) * o_ref[rows, :].astype(F32), axis=-1, keepdims=True)
            e = (p * (dp - delta)).astype(BF16)
            dq_ref[rows, :] = jnp.dot(e, kv, preferred_element_type=F32) * (inv * SCALE)
            ps.append(p.astype(BF16))
            es.append(e)
            dos.append((doc.astype(F32) * inv).astype(BF16))
            qs.append((qc.astype(F32) * (inv * SCALE)).astype(BF16))
        cat = lambda xs: xs[0] if len(xs) == 1 else jnp.concatenate(xs, axis=0)
        dv_ref[...] += lax.dot_general(cat(ps), cat(dos), _TN, preferred_element_type=F32)
        dk_ref[...] += lax.dot_general(cat(es), cat(qs), _TN, preferred_element_type=F32)

    qmap = lambda b, kh, g, i: (b * nq + i, kh * GROUP + g)
    kmap = lambda b, kh, g, i: (b, kh)
    kvw = N_KV * HEAD_DIM
    return pl.pallas_call(
        body, name="attn_bwd", grid=(B, N_KV, GROUP, nq),
        in_specs=[pl.BlockSpec((tq, HEAD_DIM), qmap), pl.BlockSpec((L, HEAD_DIM), kmap),
                  pl.BlockSpec((L, HEAD_DIM), kmap), pl.BlockSpec((tq, HEAD_DIM), qmap),
                  pl.BlockSpec((tq, HEAD_DIM), qmap)],
        out_specs=[pl.BlockSpec((tq, HEAD_DIM), qmap), pl.BlockSpec((L, HEAD_DIM), kmap),
                   pl.BlockSpec((L, HEAD_DIM), kmap)],
        out_shape=[jax.ShapeDtypeStruct((B * L, D_MODEL), F32), jax.ShapeDtypeStruct((B * L, kvw), F32),
                   jax.ShapeDtypeStruct((B * L, kvw), F32)],
        compiler_params=_params(("parallel", "parallel", "arbitrary", "arbitrary")),
    )(q, k, v, o, do)


def _conv_shift(x, t, L, k):
    if k == 2:
        return x
    if k < 2:
        return jnp.where(t >= 2 - k, pltpu.roll(x, 2 - k, 0), 0.0)
    return jnp.where(t < L - (k - 2), pltpu.roll(x, L - (k - 2), 0), 0.0)


def _conv_apply(x, w_ref, L):
    t = lax.broadcasted_iota(jnp.int32, x.shape, 0)
    acc = w_ref[4:5, :] + w_ref[2:3, :] * x
    for k in (0, 1, 3):
        acc = acc + w_ref[k:k + 1, :] * _conv_shift(x, t, L, k)
    return acc


def _conv_bwd(z, g, wb, dz, B, L, tc=256, after=None):
    noff = D_MODEL // tc
    order = _after_operand(after)

    def body(z_ref, g_ref, w_ref, dz_in, *rest):
        dx_ref, dw_ref = rest[len(order):]

        @pl.when(pl.program_id(1) == 0)
        def _():
            dw_ref[...] = jnp.zeros(dw_ref.shape, F32)

        x, gv = z_ref[...], g_ref[...]
        t = lax.broadcasted_iota(jnp.int32, x.shape, 0)
        dx = w_ref[2:3, :] * gv
        for k in (0, 1, 3):
            dx = dx + w_ref[k:k + 1, :] * _conv_shift(gv, t, L, 4 - k)
        dx_ref[...] = dx.astype(BF16)
        for k in range(4):
            dw_ref[k:k + 1, :] += _rsum(_conv_shift(x, t, L, k) * gv)
        dw_ref[4:5, :] += _rsum(gv)

    return pl.pallas_call(
        body, name="rg_conv_bwd", grid=(noff, B),
        in_specs=[pl.BlockSpec((L, tc), lambda j, b: (b, noff + j)), pl.BlockSpec((L, tc), lambda j, b: (b, j)),
                  pl.BlockSpec((SUBLANES, tc), lambda j, b: (0, j)), _ANY] + [_ANY] * len(order),
        out_specs=[pl.BlockSpec((L, tc), lambda j, b: (b, noff + j)),
                   pl.BlockSpec((SUBLANES, tc), lambda j, b: (0, j))],
        out_shape=[jax.ShapeDtypeStruct(dz.shape, dz.dtype), jax.ShapeDtypeStruct((SUBLANES, D_MODEL), F32)],
        input_output_aliases={3: 0},
        compiler_params=_params(("parallel", "arbitrary")),
    )(z, g, wb, dz, *order)


def _softplus(x):
    return jnp.maximum(x, 0.0) + jnp.log1p(jnp.exp(-jnp.abs(x)))


_ROW_BA, _ROW_BX, _ROW_LAM = 0, 2, 4


def _gate_math(xb, pre, vec_ref, d, sl):
    pa = pre[:, (2 * d) * LRU_BW:(2 * d + 1) * LRU_BW] + vec_ref[_ROW_BA + d:_ROW_BA + d + 1, sl]
    px = pre[:, (2 * d + 1) * LRU_BW:(2 * d + 2) * LRU_BW] + vec_ref[_ROW_BX + d:_ROW_BX + d + 1, sl]
    r = 0.5 * jnp.tanh(0.5 * pa) + 0.5
    i = 0.5 * jnp.tanh(0.5 * px) + 0.5
    slope = (-RG_C) * _softplus(-vec_ref[_ROW_LAM + d:_ROW_LAM + d + 1, sl])
    log_a = r * slope
    a = jnp.exp(log_a)
    om = -jnp.tanh(log_a) * (1.0 + a * a)
    rs = lax.rsqrt(om)
    mult = jnp.where(om > 0.0, om * rs, 0.0)
    return a, mult * (i * xb), (r, i, slope, om, mult, rs)


def _gate_bwd(rec, du_f, da_f, du_b, da_b, wcat, gvec):
    def fn(ins, bs, outs, accs):
        for blk in range(LRU_BLOCKS):
            sl = slice(blk * LRU_BW, (blk + 1) * LRU_BW)
            xb = ins[0][:, sl]
            xb16 = xb.astype(BF16)
            w = bs[0][sl, :]
            pre = jnp.dot(xb16, w, preferred_element_type=F32)
            dx = jnp.zeros_like(xb)
            dpre = []
            for d in range(2):
                a, _, (r, i, slope, om, mult, rs) = _gate_math(xb, pre, bs[1], d, sl)
                du, da = ins[1 + 2 * d][:, sl], ins[2 + 2 * d][:, sl]
                t = du * xb
                d_i = t * mult
                dx = dx + du * mult * i
                dlog = da * a - (t * i) * ((1.0 - om) * rs)
                d_r = dlog * slope
                d_sp = _rsum(dlog * r) * (-RG_C)
                lam = bs[1][_ROW_LAM + d:_ROW_LAM + d + 1, sl]
                accs[2][_ROW_LAM + d:_ROW_LAM + d + 1, sl] += d_sp * (-jax.nn.sigmoid(-lam))
                dpa = d_r * r * (1.0 - r)
                dpx = d_i * i * (1.0 - i)
                accs[2][_ROW_BA + d:_ROW_BA + d + 1, sl] += _rsum(dpa)
                accs[2][_ROW_BX + d:_ROW_BX + d + 1, sl] += _rsum(dpx)
                dpre += [dpa, dpx]
            dpre = jnp.concatenate(dpre, axis=1).astype(BF16)
            dw = lax.dot_general(xb16, dpre, _TN, preferred_element_type=F32)
            for d in range(2):
                rows = slice(d * D_MODEL + blk * LRU_BW, d * D_MODEL + (blk + 1) * LRU_BW)
                accs[0][rows, :] += dw[:, (2 * d) * LRU_BW:(2 * d + 1) * LRU_BW]
                accs[1][rows, :] += dw[:, (2 * d + 1) * LRU_BW:(2 * d + 2) * LRU_BW]
            outs[0][:, sl] = dx + lax.dot_general(dpre, w, _NT, preferred_element_type=F32)

    gate_shape = (2 * D_MODEL, LRU_BW)
    return _rowwise(fn, [rec, du_f, da_f, du_b, da_b], [wcat, gvec], [(D_MODEL, F32)],
                    [gate_shape, gate_shape, (SUBLANES, D_MODEL)], tm=512, name="rg_gate_bwd")


def _as_time_blocks(x):
    return x.reshape(x.shape[0] // SUBLANES, SUBLANES, x.shape[1])


def _scan_call(body, ins, n_out, B, L, tc, name):
    nb = L // SUBLANES
    spec = pl.BlockSpec((nb, SUBLANES, tc), lambda b, j: (b, 0, j))
    T = ins[0].shape[0]
    outs = pl.pallas_call(
        functools.partial(body, nb), name=name, grid=(B, D_MODEL // tc),
        in_specs=[spec] * len(ins), out_specs=[spec] * n_out,
        out_shape=[jax.ShapeDtypeStruct((T // SUBLANES, SUBLANES, D_MODEL), F32)] * n_out,
        compiler_params=_params(("parallel", "parallel")),
    )(*[_as_time_blocks(x) for x in ins])
    return [o.reshape(T, D_MODEL) for o in outs]


def _block_scan(A, U, reverse):
    row = lax.broadcasted_iota(jnp.int32, A.shape, 0)
    for s in (1, 2, 4):
        shift = SUBLANES - s if reverse else s
        valid = (row < SUBLANES - s) if reverse else (row >= s)
        a_sh = jnp.where(valid, pltpu.roll(A, shift, 0), 1.0)
        u_sh = jnp.where(valid, pltpu.roll(U, shift, 0), 0.0)
        U = A * u_sh + U
        A = A * a_sh
    return A, U


_LAST = SUBLANES - 1
SCAN_UNROLL = 8


def _loop_blocks(nb, step, init):
    def group(g, carry):
        for k in range(SCAN_UNROLL):
            carry = step(g * SCAN_UNROLL + k, carry)
        return carry

    return lax.fori_loop(0, nb // SCAN_UNROLL, group, init)


def _scan_bwd(dy, a_f, h_f, a_b, h_b, B, L, tc=256):
    def body(nb, dy_r, af, hf, ab, hb, duf, daf, dub, dab):
        def step(i, carry):
            c1, c2 = carry
            ir = nb - 1 - i
            row = lax.broadcasted_iota(jnp.int32, (SUBLANES, tc), 0)
            a_up = jnp.where(row == _LAST, af[jnp.minimum(ir + 1, nb - 1), :1, :], pltpu.roll(af[ir], _LAST, 0))
            p, lam = _block_scan(a_up, dy_r[ir], True)
            lam = lam + p * c1
            before = hf[jnp.maximum(ir - 1, 0), _LAST:, :] * (ir > 0).astype(F32)
            duf[ir] = lam
            daf[ir] = lam * jnp.where(row == 0, before, pltpu.roll(hf[ir], 1, 0))
            a_dn = jnp.where(row == 0, ab[jnp.maximum(i - 1, 0), _LAST:, :], pltpu.roll(ab[i], 1, 0))
            p2, lam2 = _block_scan(a_dn, dy_r[i], False)
            lam2 = lam2 + p2 * c2
            after = hb[jnp.minimum(i + 1, nb - 1), :1, :] * (i < nb - 1).astype(F32)
            dub[i] = lam2
            dab[i] = lam2 * jnp.where(row == _LAST, after, pltpu.roll(hb[i], _LAST, 0))
            return lam[:1, :], lam2[_LAST:, :]

        zero = jnp.zeros((1, tc), F32)
        _loop_blocks(nb, step, (zero, zero))

    return _scan_call(body, [dy, a_f, h_f, a_b, h_b], 4, B, L, tc, "rg_scan_bwd")


_GELU_C = math.sqrt(2.0 / math.pi)


def _gelu_parts(x):
    th = jnp.tanh(_GELU_C * (x + 0.044715 * x * x * x))
    return 0.5 * x * (1.0 + th), th


def _mm_gated_out_bwd(dx, w_out, h_f, h_b, z, name, after=None):
    def epilogue(acc, e_refs, b_refs, o_refs, a_refs):
        x = e_refs[2][...]
        gl, th = _gelu_parts(x)
        dgl = 0.5 * (1.0 + th) + 0.5 * x * (1.0 - th * th) * (_GELU_C * (1.0 + 3.0 * 0.044715 * x * x))
        o_refs[0][...] = acc * gl
        o_refs[1][...] = (acc * (e_refs[0][...] + e_refs[1][...]) * dgl).astype(BF16)

    return _mm(dx, w_out, mode="nt", out_dtypes=(F32, BF16), out_cols=(D_MODEL, 2 * D_MODEL), extras=(h_f, h_b, z),
               ref_epi=epilogue, name=name, after=after)


def _row_block(i):
    return pl.ds(pl.multiple_of(i * SUBLANES, SUBLANES), SUBLANES)


def _rg_mix_fwd(z, conv_wb, wcat, gvec, B, L):
    nb = L // SUBLANES
    n_g = D_MODEL // LRU_BW

    def body(zg_ref, zr_ref, cw_ref, w_ref, gv_ref, rec_ref, af_s, ab_s, hf_ref, hb_ref, yg_ref, uf_s, ub_s):
        rec = _conv_apply(zr_ref[...], cw_ref, L)
        rec_ref[...] = rec
        pre = jnp.dot(rec.astype(BF16), w_ref[...], preferred_element_type=F32)
        for d, (a_s, u_s) in enumerate(((af_s, uf_s), (ab_s, ub_s))):
            a, u, _ = _gate_math(rec, pre, gv_ref, d, slice(None))
            a_s[...] = a
            u_s[...] = u

        def step(i, carry):
            c1, c2 = carry
            rows, rows_b = _row_block(i), _row_block(nb - 1 - i)
            p, h = _block_scan(af_s[rows, :], uf_s[rows, :], False)
            h = h + p * c1
            hf_ref[rows, :] = h
            p2, h2 = _block_scan(ab_s[rows_b, :], ub_s[rows_b, :], True)
            h2 = h2 + p2 * c2
            hb_ref[rows_b, :] = h2
            return h[_LAST:, :], h2[:1, :]

        zero = jnp.zeros((1, LRU_BW), F32)
        _loop_blocks(nb, step, (zero, zero))
        gl, _ = _gelu_parts(zg_ref[...])
        yg_ref[...] = ((hf_ref[...] + hb_ref[...]) * gl).astype(BF16)

    seq = lambda off: pl.BlockSpec((L, LRU_BW), lambda b, g: (b, off + g))
    vec = pl.BlockSpec((SUBLANES, LRU_BW), lambda b, g: (0, g))
    T = B * L
    return pl.pallas_call(
        body, name="rg_mix", grid=(B, n_g),
        in_specs=[seq(0), seq(n_g), vec, pl.BlockSpec((LRU_BW, 4 * LRU_BW), lambda b, g: (g, 0)), vec],
        out_specs=[seq(0)] * 6,
        out_shape=[jax.ShapeDtypeStruct((T, D_MODEL), F32)] * 5 + [jax.ShapeDtypeStruct((T, D_MODEL), BF16)],
        scratch_shapes=[pltpu.VMEM((L, LRU_BW), F32)] * 2,
        compiler_params=_params(("parallel", "parallel")),
    )(z, z, conv_wb, wcat, gvec)


def _make_wcat(w_a, w_x):
    g = jnp.stack([w_a[0, 0], w_x[0, 0], w_a[0, 1], w_x[0, 1]])
    return jnp.transpose(g, (1, 2, 0, 3)).reshape(D_MODEL, 4 * LRU_BW)


def _rows_at(part, first):
    return jnp.pad(part, ((first, SUBLANES - first - part.shape[0]), (0, 0)))


def _qk_slot(q_g, k_g):
    wide = lambda v, at: jnp.pad(v, ((0, SUBLANES - 1), (at, D_MODEL - at - HEAD_DIM)))
    return wide(q_g, 0) + wide(k_g, HEAD_DIM)


def _local_step(x, target, P, fetch, emit, B, L, after=None):
    g_mix, g_mlp = P["norm_mix_g"], P["norm_mlp_g"]
    h0 = _rms_fwd(x, g_mix[0:1], "rg_norm", after=after)
    w_in, conv_wb, wcat, gvec = fetch("rg", h0)
    z = _mm(h0, w_in, mode="nn", b_shard=True, name="rg_in")
    rec, a_f, a_b, h_f, h_b, yg = _rg_mix_fwd(z, conv_wb, wcat, gvec, B, L)
    w_out = fetch("rg_out", yg)
    x1, h1 = _mm_res_norm(yg, w_out, x, g_mlp[0:1], "rg_out")
    (x2, h3), mlp0 = _mlp_fwd(x1, h1, fetch, 0, lambda a, w, res, name: _mm_res_norm(a, w, res, g_mix[1:2], name))
    w_qkv, w_o = fetch("att", h3)
    qkv = _mm(h3, w_qkv, mode="nn", b_shard=True, name="attn_qkv")
    cos, sin = _rope_tables(L, B)
    qh, kh, vh = _qk_prep(qkv, cos, sin, P["q_g"], P["k_g"])
    o = _attn_fwd(qh, kh, vh, B, L)
    x3, h4 = _mm_res_norm(o, w_o, x2, g_mlp[1:2], "attn_out")
    (dx4, dx4_bf, loss_acc, d_final_g), mlp1 = _mlp_fwd(
        x3, h4, fetch, 1, lambda a, w, res, name: _mm_final_loss(a, w, res, target, P["final_g"], name))

    dx3, dx3_bf, dg_mlp1, d_up1, d_down1 = _mlp_bwd(x3, g_mlp[1:2], mlp1, dx4, dx4_bf, 1, None)
    tok = emit("mlp1", [d_up1, d_down1])
    d_wo = _mm(o, dx3_bf, mode="tn", out_dtypes=(BF16,), name="attn_dwo", after=tok)
    do = _mm(dx3_bf, w_o, mode="nt", out_dtypes=(BF16,), name="attn_do")
    dq, dk, dv = _attn_bwd(qh, kh, vh, o, do, B, L)
    dqkv, dq_g, dk_g = _qk_prep_bwd(qkv, dq, dk, dv, cos, sin, P["q_g"], P["k_g"])
    d_wqkv = _mm(h3, dqkv, mode="tn", o_shard=True, out_dtypes=(BF16,), name="attn_dwqkv")
    tok = emit("att", [d_wqkv, d_wo])
    dx2, dx2_bf, dg_mix1 = _mm_norm_bwd(dqkv, w_qkv, x2, dx3, g_mix[1:2], "attn_dh", after=tok)
    tok = emit("point_attn_done", [dx2_bf])
    dx1, dx1_bf, dg_mlp0, d_up0, d_down0 = _mlp_bwd(x1, g_mlp[0:1], mlp0, dx2, dx2_bf, 0, tok)
    d_wout = _mm(yg, dx1_bf, mode="tn", out_dtypes=(BF16,), name="rg_dwout")
    tok = emit("mlp0", [d_up0, d_down0, d_wout])
    dy, dgate = _mm_gated_out_bwd(dx1_bf, w_out, h_f, h_b, z, "rg_dyg", after=tok)
    du_f, da_f, du_b, da_b = _scan_bwd(dy, a_f, h_f, a_b, h_b, B, L)
    drec_c, d_wa, d_wx, d_gvec = _gate_bwd(rec, du_f, da_f, du_b, da_b, wcat, gvec)
    tok = emit("gates", [d_wa, d_wx])
    dz, d_convwb = _conv_bwd(z, drec_c, conv_wb, dgate, B, L, after=tok)
    tok = emit("point_mix_done", [dz])
    d_win = _mm(h0, dz, mode="tn", o_shard=True, out_dtypes=(BF16,), name="rg_dwin", after=tok)
    tok = emit("rg_in", [d_win])
    grad_x, _, dg_mix0 = _mm_norm_bwd(dz, w_in, x, dx1, g_mix[0:1], "rg_dh", after=tok)

    norms = (_rows_at(dg_mix0, 0) + _rows_at(dg_mix1, 1) + _rows_at(dg_mlp0, 2) + _rows_at(dg_mlp1, 3)
             + _rows_at(d_final_g, 4)
             + jnp.pad(loss_acc, ((LOSS_ROW, SUBLANES - 1 - LOSS_ROW), (0, D_MODEL - LANES))))
    vec = jnp.concatenate([norms, d_convwb, d_gvec, _qk_slot(dq_g, dk_g)], axis=0)
    return grad_x, vec


_MESH = pl.DeviceIdType.MESH


def _place():
    x, y, c = lax.axis_index("x"), lax.axis_index("y"), lax.axis_index("c")
    peers = [((1 - x) if j & 2 else x, (1 - y) if j & 1 else y) for j in (1, 2, 3)]
    return x, y, c, peers


def _sum_leading(slots, name):
    def body(s_ref, o_ref):
        acc = s_ref[0]
        for d in range(1, slots.shape[0]):
            acc = acc + s_ref[d]
        o_ref[...] = acc

    return pl.pallas_call(body, name=name, out_shape=jax.ShapeDtypeStruct(slots.shape[1:], slots.dtype))(slots)


_HBM = pl.BlockSpec(memory_space=pltpu.HBM)
_SEM = pl.BlockSpec(memory_space=pltpu.SEMAPHORE)
_EFFECT = pltpu.SideEffectType.DATAFLOW_SIDE_EFFECTING


_COPIES = dict(gather=N_CHIPS - 1, scatter=N_CHIPS - 1, swap=1, spread=N_DEVICES - 1,
               gather_half=N_CHIPS - 1, share_half=N_CHIPS - 1)


def _split_copies(kind, srcs, lands, send, recv):
    x, y, c, peers = _place()
    me = 2 * x + y
    per = _COPIES[kind]
    out = []
    for a in range(len(lands)):
        for j in range(per):
            if kind == "swap":
                src, there, here, dev = srcs[a], lands[a], lands[a], (x, y, 1 - c)
            elif kind == "spread":
                k = j + 1
                dev = ((1 - x) if k & 4 else x, (1 - y) if k & 2 else y, (1 - c) if k & 1 else c)
                mine = lands[a].at[4 * x + 2 * y + c]
                src, there, here = mine, mine, lands[a].at[4 * dev[0] + 2 * dev[1] + dev[2]]
            else:
                px, py = peers[j]
                dev = (px, py, c)
                if kind == "gather":
                    src, there, here = lands[a].at[me], lands[a].at[me], lands[a].at[2 * px + py]
                elif kind in ("gather_half", "share_half"):
                    half = lands[a].shape[1] // 2
                    mine, other = pl.ds(c * half, half), pl.ds((1 - c) * half, half)
                    if kind == "gather_half":
                        src = there = lands[a].at[me, mine]
                        here = lands[a].at[2 * px + py, mine]
                    else:
                        src = there = lands[a].at[2 * px + py, mine]
                        here = lands[a].at[2 * px + py, other]
                        dev = (x, y, 1 - c)
                else:
                    src, there, here = srcs[a].at[2 * px + py], lands[a].at[j], lands[a].at[j]
            mk = functools.partial(
                pltpu.make_async_remote_copy, src_ref=src, send_sem=send.at[per * a + j],
                recv_sem=recv.at[per * a + j], device_id=dev, device_id_type=_MESH)
            out.append((functools.partial(mk, dst_ref=there), functools.partial(mk, dst_ref=here)))
    return out


_CORE_PAIR = ("swap", "share_half")
CORE_PAIR_BARRIER_ID = 0
BARRIER_IDS = dict(gather_rest=1, scatter_mlp1=2, scatter_att=3, scatter_mlp0=4, scatter_gates=5, scatter_rg_in=6,
                   gather_gates=7, spread_vec=8)


def _entry_peers(kind):
    x, y, c, peers = _place()
    if kind in _CORE_PAIR:
        return [(x, y, 1 - c)]
    if kind == "spread":
        return [((1 - x) if k & 4 else x, (1 - y) if k & 2 else y, (1 - c) if k & 1 else c)
                for k in range(1, N_DEVICES)]
    return [(px, py, c) for px, py in peers]


def _entry_params(kind, barrier_id):
    if kind in _CORE_PAIR:
        barrier_id = CORE_PAIR_BARRIER_ID
    collective = {} if barrier_id is None else dict(collective_id=barrier_id)
    return pltpu.CompilerParams(has_side_effects=_EFFECT, **collective)


def _entry_handshake(kind, barrier_id):
    if kind in _CORE_PAIR or barrier_id is not None:
        barrier = pltpu.get_barrier_semaphore()
        peers = _entry_peers(kind)
        for peer in peers:
            pl.semaphore_signal(barrier, inc=1, device_id=peer, device_id_type=_MESH)
        pl.semaphore_wait(barrier, len(peers))


def _exchange_start(kind, srcs, lands, name, after=None, barrier_id=None):
    arrays = list(srcs) + list(lands)
    n_s, n, n_all = len(srcs), len(lands), len(srcs) + len(lands)
    n_sem = _COPIES[kind] * n
    order = _after_operand(after)
    n_x = len(order)

    def body(*refs):
        _entry_handshake(kind, barrier_id)
        send, recv = refs[n_all + n_x], refs[n_all + n_x + 1]
        token = refs[-1]
        for started, _ in _split_copies(kind, refs[:n_s], refs[n_s:n_all], send, recv):
            started().start()
        token[...] = jnp.zeros(token.shape, F32)

    res = pl.pallas_call(
        body, name=name,
        out_shape=(pltpu.SemaphoreType.DMA((n_sem,)), pltpu.SemaphoreType.DMA((n_sem,)),
                   *[pltpu.HBM(a.shape, a.dtype) for a in arrays], jax.ShapeDtypeStruct((SUBLANES, LANES), F32)),
        in_specs=[_HBM] * n_all + [_ANY] * n_x,
        out_specs=(_SEM, _SEM, *[_HBM] * n_all, pl.BlockSpec(memory_space=pltpu.VMEM)),
        input_output_aliases={i: 2 + i for i in range(n_all)},
        compiler_params=_entry_params(kind, barrier_id),
    )(*[pltpu.with_memory_space_constraint(a, pltpu.HBM) for a in arrays], *order)
    return (res[0], res[1], res[2:2 + n_s], res[2 + n_s:2 + n_all]), res[-1]


def _gather_start_groups(land_groups, name, after=None, kind="gather", barrier_id=None):
    arrays = [a for group in land_groups for a in group]
    n_all, n_g = len(arrays), len(land_groups)
    order = _after_operand(after)
    n_x = len(order)

    def body(*refs):
        _entry_handshake(kind, barrier_id)
        first = 0
        for gi, group in enumerate(land_groups):
            send, recv = refs[n_all + n_x + 2 * gi], refs[n_all + n_x + 2 * gi + 1]
            for started, _ in _split_copies(kind, [], refs[first:first + len(group)], send, recv):
                started().start()
            first += len(group)
        refs[-1][...] = jnp.zeros(refs[-1].shape, F32)

    sems = [pltpu.SemaphoreType.DMA((_COPIES[kind] * len(group),)) for group in land_groups for _ in range(2)]
    res = pl.pallas_call(
        body, name=name,
        out_shape=(*sems, *[pltpu.HBM(a.shape, a.dtype) for a in arrays], jax.ShapeDtypeStruct((SUBLANES, LANES), F32)),
        in_specs=[_HBM] * n_all + [_ANY] * n_x,
        out_specs=(*[_SEM] * (2 * n_g), *[_HBM] * n_all, pl.BlockSpec(memory_space=pltpu.VMEM)),
        input_output_aliases={i: 2 * n_g + i for i in range(n_all)},
        compiler_params=_entry_params(kind, barrier_id),
    )(*[pltpu.with_memory_space_constraint(a, pltpu.HBM) for a in arrays], *order)
    handles, first = [], 2 * n_g
    for gi, group in enumerate(land_groups):
        handles.append((res[2 * gi], res[2 * gi + 1], [], res[first:first + len(group)]))
        first += len(group)
    return handles, res[-1]


def _exchange_wait(kind, handle, after, name):
    send, recv, srcs, lands = handle
    arrays = list(srcs) + list(lands)
    n_s, n_all = len(srcs), len(arrays)
    order = list(after) if isinstance(after, (list, tuple)) else [after]

    def body(*refs):
        for started, landing in _split_copies(kind, refs[:n_s], refs[n_s:n_all], refs[n_all], refs[n_all + 1]):
            started().wait_send()
            landing().wait_recv()

    res = pl.pallas_call(
        body, name=name, out_shape=[pltpu.HBM(a.shape, a.dtype) for a in arrays],
        in_specs=[_HBM] * n_all + [_SEM, _SEM] + [_ANY] * len(order), out_specs=[_HBM] * n_all,
        input_output_aliases={i: i for i in range(n_all)},
        compiler_params=pltpu.CompilerParams(has_side_effects=_EFFECT),
    )(*arrays, send, recv, *order)
    return res[:n_s], res[n_s:]


def _index_operand(i):
    return jnp.reshape(i, (1,)).astype(jnp.int32)


def _cast_into_slot(src, row0, rows, me, dtype, name, after=None, add=None, n_slots=N_CHIPS):
    cols = src.shape[1]
    tm = min(1024, rows)
    order = _after_operand(after)
    terms = [src] + ([] if add is None else [add])

    def body(me_ref, *rest):
        val = rest[0][...]
        if add is not None:
            val = val + rest[1][...]
        rest[-1][...] = val.astype(dtype)

    return pl.pallas_call(
        body, name=name,
        grid_spec=pltpu.PrefetchScalarGridSpec(
            num_scalar_prefetch=1, grid=(rows // tm,),
            in_specs=[pl.BlockSpec((tm, cols), lambda i, me_ref: (i + row0 // tm, 0))] * len(terms)
            + [_ANY] * len(order),
            out_specs=pl.BlockSpec((None, tm, cols), lambda i, me_ref: (me_ref[0], i, 0))),
        out_shape=jax.ShapeDtypeStruct((n_slots, rows, cols), dtype), compiler_params=_params(("parallel",)),
    )(_index_operand(me), *terms, *order)


def _sum_slots(mine, r, me, name):
    _, rows, cols = r.shape
    tm = min(1024, rows)

    def body(me_ref, own_ref, r_ref, o_ref):
        o_ref[...] = ((own_ref[...].astype(F32) + r_ref[0].astype(F32)) + r_ref[1].astype(F32)) + r_ref[2].astype(F32)

    return pl.pallas_call(
        body, name=name,
        grid_spec=pltpu.PrefetchScalarGridSpec(
            num_scalar_prefetch=1, grid=(rows // tm,),
            in_specs=[pl.BlockSpec((None, tm, cols), lambda i, me_ref: (me_ref[0], i, 0)),
                      pl.BlockSpec((N_CHIPS - 1, tm, cols), lambda i, me_ref: (0, i, 0))],
            out_specs=pl.BlockSpec((tm, cols), lambda i, me_ref: (i, 0))),
        out_shape=jax.ShapeDtypeStruct((rows, cols), F32), compiler_params=_params(("parallel",)),
    )(_index_operand(me), mine, r)


def _adamw(w, m, v, ps, qs, name):
    rows, cols = w.shape
    seg_rows = ps[0].shape[0]
    tm = min(512, seg_rows)
    while seg_rows % tm:
        tm -= SUBLANES
    per, n_seg = seg_rows // tm, len(ps)
    parts = list(ps) + ([] if qs is None else list(qs))

    def body(w_ref, m_ref, v_ref, *rest):
        g_refs, outs = rest[:len(parts)], rest[len(parts):]
        grad = lambda s: g_refs[s][...] if qs is None else g_refs[s][...] + g_refs[n_seg + s][...]
        g = grad(0)
        for s in range(1, n_seg):
            g = jnp.where(pl.program_id(0) >= s * per, grad(s), g)
        m1 = ADAM_B1 * m_ref[...] + (1.0 - ADAM_B1) * g
        v1 = ADAM_B2 * v_ref[...] + (1.0 - ADAM_B2) * (g * g)
        m_hat = m1 / (1.0 - ADAM_B1 ** ADAM_STEP)
        v_hat = v1 / (1.0 - ADAM_B2 ** ADAM_STEP)
        outs[0][...] = g
        outs[1][...] = (-ADAM_LR) * (m_hat / (jnp.sqrt(v_hat) + ADAM_EPS) + ADAM_WD * w_ref[...])
        outs[2][...] = m1
        outs[3][...] = v1

    row_spec = pl.BlockSpec((tm, cols), lambda i: (i, 0))
    seg_spec = lambda s: pl.BlockSpec((tm, cols), lambda i: (jnp.clip(i - s * per, 0, per - 1), 0))
    return pl.pallas_call(
        body, name=name, grid=(rows // tm,),
        in_specs=[row_spec] * 3 + [seg_spec(s) for s in range(n_seg)] * (1 if qs is None else 2),
        out_specs=[row_spec] * 4, out_shape=[jax.ShapeDtypeStruct((rows, cols), F32)] * 4,
        compiler_params=_params(("arbitrary",)),
    )(w, m, v, *parts)


def _put_cols(shard, me):
    full = jnp.zeros((shard.shape[0], D_MODEL), F32)
    return lax.dynamic_update_slice(full, shard, (0, me * (D_MODEL // N_CHIPS)))


def _gate_vec_slot(b_a, b_x, lam):
    return _rows_at(b_a, _ROW_BA) + _rows_at(b_x, _ROW_BX) + _rows_at(lam, _ROW_LAM)


def _pack_vec(p, me):
    return jnp.concatenate([
        _rows_at(p["norm_mix_g"], 0) + _rows_at(p["norm_mlp_g"], 2) + _rows_at(p["final_g"][None], 4),
        _rows_at(_put_cols(p["rg_conv_w"][0, :, 0, :], me), 0) + _rows_at(p["rg_conv_b"], 4),
        _gate_vec_slot(_put_cols(p["rg_b_a"][0], me), _put_cols(p["rg_b_x"][0], me), _put_cols(p["rg_lam"][0], me)),
        _qk_slot(p["at_q_g"], p["at_k_g"]),
    ], axis=0)


def _unpack_vec(r, me):
    def cols(rows):
        return lax.dynamic_slice(rows, (0, me * (D_MODEL // N_CHIPS)), (rows.shape[0], D_MODEL // N_CHIPS))

    gate = r[16:24]
    return dict(
        norm_mix_g=r[0:2], norm_mlp_g=r[2:4], final_g=r[4], rg_conv_w=cols(r[8:12])[None, :, None, :],
        rg_conv_b=r[12:13], rg_b_a=cols(gate[_ROW_BA:_ROW_BA + 2])[None], rg_b_x=cols(gate[_ROW_BX:_ROW_BX + 2])[None],
        rg_lam=cols(gate[_ROW_LAM:_ROW_LAM + 2])[None], at_q_g=r[24:25, 0:HEAD_DIM],
        at_k_g=r[24:25, HEAD_DIM:2 * HEAD_DIM])


_WEIGHTS = ['norm_mix_g', 'norm_mlp_g', 'rg_w_in', 'rg_conv_w', 'rg_conv_b', 'rg_w_a', 'rg_b_a', 'rg_w_x', 'rg_b_x',
            'rg_lam', 'rg_w_out', 'at_w_qkv', 'at_q_g', 'at_k_g', 'at_w_o', 'mlp_w_up', 'mlp_w_down', 'final_g']
_BIG = dict(rg_w_in=["rg_w_in"], rg_w_out=["rg_w_out"], at_w_qkv=["at_w_qkv"], at_w_o=["at_w_o"],
            mlp_w_up=["up0", "up1"], mlp_w_down=["down0", "down1"])


def kernel(x, *args):
    n_w = len(_WEIGHTS)
    w = dict(zip(_WEIGHTS, args[:n_w]))
    target = args[n_w]
    m = dict(zip(_WEIGHTS, args[n_w + 1:2 * n_w + 1]))
    v = dict(zip(_WEIGHTS, args[2 * n_w + 1:3 * n_w + 1]))
    B, L, _ = x.shape
    T = B * L
    me = 2 * lax.axis_index("x") + lax.axis_index("y")

    vec = jnp.concatenate([_gate_vec_slot(w["rg_b_a"][0], w["rg_b_x"][0], w["rg_lam"][0]),
                           _rows_at(w["rg_conv_w"][0, :, 0, :], 0)], axis=0)
    flat = lambda a: a.reshape(-1, a.shape[-1])
    rows_of = lambda k: w[k].shape[-2]
    groups = [("rg", [("rg_w_in", 0, BF16), (vec, 0, F32)]), ("rg_out", [("rg_w_out", 0, BF16)]),
              ("mlp0_up", [("mlp_w_up", 0, BF16)]), ("mlp0_down", [("mlp_w_down", 0, BF16)]),
              ("att", [("at_w_qkv", 0, BF16), ("at_w_o", 0, BF16)]),
              ("mlp1", [("mlp_w_up", 1, BF16), ("mlp_w_down", 1, BF16)])]

    def landing_zones(group, members, after):
        lands = []
        for n, (k, layer, dtype) in enumerate(members):
            src, rows = (flat(w[k]), rows_of(k)) if isinstance(k, str) else (k, k.shape[0])
            lands.append(_cast_into_slot(src, layer * rows, rows, me, dtype, f"place_{group}{n}", after=after))
        return lands

    halves, gathers = {}, {}
    halves["rg"], tok = _exchange_start("gather_half", [], landing_zones(*groups[0], None), "gather_rg_start")
    handles, tok = _gather_start_groups([landing_zones(g, members, tok) for g, members in groups[1:]],
                                        "gather_rest_start", after=tok, kind="gather_half",
                                        barrier_id=BARRIER_IDS["gather_rest"])
    halves.update(zip([g for g, _ in groups[1:]], handles))
    wcat = _make_wcat(w["rg_w_a"], w["rg_w_x"]).astype(BF16)

    packs = [_pack_vec(p, me) for p in (w, m, v)]

    ready = {}

    def share(some, after, name):
        landed = [_exchange_wait("gather_half", halves[g], after, f"gather_{g}_landed")[1] for g in some]
        handles, _ = _gather_start_groups(landed, name, kind="share_half")
        gathers.update(zip(some, handles))

    def fetch(what, after):
        if what in ready:
            return ready[what]
        group = "mlp1" if what.startswith("mlp1") else what
        if group == "rg":
            share(["rg"], [after, wcat] + packs, "share_rg_start")
        elif group == "rg_out":
            share(["rg_out", "mlp0_up", "mlp0_down", "att"], after, "share_early_start")
        _, full = _exchange_wait("share_half", gathers[group], after, f"gather_{group}_wait")
        if group == "att":
            share(["mlp1"], after, "share_mlp1_start")
        if group == "rg":
            vec_full = jnp.transpose(full[1], (1, 0, 2)).reshape(2 * SUBLANES, D_MODEL)
            conv_wb = vec_full[SUBLANES:] + _rows_at(w["rg_conv_b"], 4)
            return full[0], conv_wb, wcat, vec_full[:SUBLANES]
        if group == "rg_out":
            return full[0].reshape(D_MODEL, D_MODEL)
        if group == "att":
            return full[0], full[1].reshape(D_MODEL, D_MODEL)
        if group == "mlp1":
            ready["mlp1_up"], ready["mlp1_down"] = full[0], full[1].reshape(4 * D_MODEL, D_MODEL)
            return ready[what]
        return full[0] if group == "mlp0_up" else full[0].reshape(4 * D_MODEL, D_MODEL)

    names = dict(mlp1=["up1", "down1"], att=["at_w_qkv", "at_w_o"], mlp0=["up0", "down0", "rg_w_out"],
                 rg_in=["rg_w_in"], gates=["rg_w_a", "rg_w_x"])
    scatters, swaps, P, Q, res = {}, [], {}, {}, {}

    def start_scatter(group, grads):
        srcs = [g.reshape(N_CHIPS, -1, g.shape[-1]) for g in grads]
        lands = [lax.empty((N_CHIPS - 1,) + s.shape[1:], s.dtype) for s in srcs]
        scatters[group], token = _exchange_start("scatter", srcs, lands, f"scatter_{group}_start",
                                                 barrier_id=BARRIER_IDS[f"scatter_{group}"])
        return token

    def settle(groups, after):
        keys, parts = [], []
        for group in groups:
            srcs, lands = _exchange_wait("scatter", scatters[group], after, f"scatter_{group}_wait")
            for k, s, r in zip(names[group], srcs, lands):
                keys.append(k)
                parts.append(_sum_slots(s, r, me, f"sum_{k}"))
        handle, token = _exchange_start("swap", parts, [lax.empty(p.shape, F32) for p in parts],
                                        f"swap_{groups[0]}_start")
        swaps.append((keys, handle, f"swap_{groups[0]}_wait"))
        return token

    def finish(after):
        for keys, handle, name in swaps:
            mine, theirs = _exchange_wait("swap", handle, after, name)
            P.update(zip(keys, mine))
            Q.update(zip(keys, theirs))
        swaps.clear()
        last = after
        for k, parts in _BIG.items():
            if k in res or any(p not in P for p in parts):
                continue
            shape = w[k].shape
            two_d = lambda a: a.reshape(-1, shape[-1])
            outs = _adamw(two_d(w[k]), two_d(m[k]), two_d(v[k]), [P[p] for p in parts], [Q[p] for p in parts],
                          f"adamw_{k}")
            res[k] = [o.reshape(shape) for o in outs]
            last = outs[0]
        if "rg_w_a" in P and "gates" not in gathers:
            lands = [_cast_into_slot(P[k], 0, P[k].shape[0], me, F32, f"place_{k}", after=last, add=Q[k])
                     for k in names["gates"]]
            gathers["gates"], last = _exchange_start("gather", [], lands, "gather_gates_start", after=last,
                                                     barrier_id=BARRIER_IDS["gather_gates"])
        return last

    def emit(event, arrays):
        if event == "point_attn_done":
            return None
        if event == "point_mix_done":
            return settle(["mlp1", "att", "mlp0"], arrays[0])
        token = start_scatter(event, arrays)
        if event == "rg_in":
            return finish(settle(["gates"], token))
        return token

    P_vec = dict(norm_mix_g=w["norm_mix_g"], norm_mlp_g=w["norm_mlp_g"], final_g=w["final_g"][None],
                 q_g=w["at_q_g"], k_g=w["at_k_g"])
    grad_x, vec_part = _local_step(x.reshape(T, D_MODEL), target.reshape(T, D_MODEL), P_vec, fetch, emit, B, L,
                                   after=tok)

    me8 = 2 * me + lax.axis_index("c")
    vec_slots = _cast_into_slot(vec_part, 0, VEC_ROWS, me8, F32, "place_vec", n_slots=N_DEVICES)
    spread, tok = _exchange_start("spread", [], [vec_slots], "spread_vec_start", barrier_id=BARRIER_IDS["spread_vec"])
    last = finish(settle(["rg_in"], tok))
    _, gate_grads = _exchange_wait("gather", gathers["gates"], last, "gather_gates_wait")
    for k, g in zip(names["gates"], gate_grads):
        two_d = lambda a: a.reshape(g.shape[0] * g.shape[1], g.shape[2])
        outs = _adamw(two_d(w[k]), two_d(m[k]), two_d(v[k]), [two_d(g)], None, f"adamw_{k}")
        res[k] = [o.reshape(w[k].shape) for o in outs]
        last = outs[0]
    _, (vec_all,) = _exchange_wait("spread", spread, last, "spread_vec_wait")
    vec_grad = _sum_leading(vec_all, "sum_vec")
    loss = vec_grad[LOSS_ROW, 0]
    outs = _adamw(*packs, [vec_grad], None, "adamw_vec")
    unpacked = [_unpack_vec(o, me) for o in outs]
    for k in _WEIGHTS:
        if k not in res:
            res[k] = [u[k] for u in unpacked]

    result = [loss, grad_x.reshape(B, L, D_MODEL)]
    for slot in range(4):
        result += [res[k][slot] for k in _WEIGHTS]
    return tuple(result)
```

```python
import functools
import math

import jax
import jax.numpy as jnp
import numpy as np
from jax import lax
from jax.experimental import pallas as pl
from jax.experimental.pallas import tpu as pltpu

F32 = jnp.float32
BF16 = jnp.bfloat16

D_MODEL = 1024
HEAD_DIM = 128
N_HEADS = 8
N_KV = 2
GROUP = N_HEADS // N_KV
LRU_BLOCKS = 8
LRU_BW = 128
GRID_W = 64
ROPE_THETA = 10000.0
EPS = 1e-6
RG_C = 8.0
SCALE = 1.0 / math.sqrt(HEAD_DIM)
N_CHIPS = 4

ADAM_LR = 0.001
ADAM_B1 = 0.9
ADAM_B2 = 0.999
ADAM_EPS = 1e-08
ADAM_WD = 0.01
ADAM_STEP = 10

V7X_VMEM_BYTES = 64 * 1024 * 1024
VMEM_LIMIT = V7X_VMEM_BYTES * 3 // 4
LANES = 128
SUBLANES = 8

N_DEVICES = 8
VEC_ROWS = 32
LOSS_ROW = 5


def _params(sem):
    return pltpu.CompilerParams(dimension_semantics=sem, vmem_limit_bytes=VMEM_LIMIT)


_ANY = pl.BlockSpec(memory_space=pl.ANY)
_NN = (((1,), (0,)), ((), ()))
_NT = (((1,), (1,)), ((), ()))
_TN = (((0,), (0,)), ((), ()))


def _after_operand(after):
    return [] if after is None else [after]


def _fit(t, n):
    if n <= t:
        return n
    c = (t // LANES) * LANES
    while n % c:
        c -= LANES
    return c


MM_VMEM_BUDGET = VMEM_LIMIT * 3 // 4
def _mm_tiles(M, K, ns, n_total, out_dtypes, extras, whole_rows):
    for tm in (2048, 1024, 512, 256, 128):
        for tn in ((ns,) if whole_rows else (1024, 512, 256)):
            tn = _fit(tn, ns)
            per_row = 2 * (2 * K) + 4 * tn + sum(2 * tn * jnp.dtype(d).itemsize for d in out_dtypes)
            per_row += sum(2 * tn * e.dtype.itemsize for e in extras)
            b_buffers = 1 if tn == n_total else 2
            if M % tm == 0 and b_buffers * (2 * K * tn) + tm * per_row <= MM_VMEM_BUDGET:
                return tm, tn
    raise ValueError(f"no tile fits VMEM for M={M} K={K} N={ns}")


def _mm(a, b, *, mode, name, out_dtypes=(F32,), b_shard=False, o_shard=False, extras=(), epi=None, after=None,
        bcast=(), accs=(), ref_epi=None, out_cols=None):
    if mode == "tn":
        K, M = a.shape
        N = b.shape[1]
    else:
        M, K = a.shape
        if mode == "nn":
            N = b.shape[0] * b.shape[2] if b_shard else b.shape[1]
        else:
            N = b.shape[1] if b_shard else b.shape[0]
    ns = N
    if b_shard and mode == "nn":
        ns = b.shape[2]
    elif o_shard:
        ns = N // N_CHIPS
    tm, tn = _mm_tiles(M, K, ns, N, out_dtypes, extras, whole_rows=ref_epi is not None)
    if ref_epi is not None:
        tm = min(tm, 512)
    grid = (M // tm, N // tn)
    q = ns // tn
    once = dict(pipeline_mode=pl.Buffered(1)) if tn == N else {}

    if mode == "tn":
        a_spec = pl.BlockSpec((K, tm), lambda i, j: (0, i))
        b_spec = pl.BlockSpec((K, tn), lambda i, j: (0, j), **once)
        dims = _TN
    elif mode == "nn":
        a_spec = pl.BlockSpec((tm, K), lambda i, j: (i, 0))
        if b_shard:
            b_spec = pl.BlockSpec((None, K, tn), lambda i, j: (j // q, 0, j % q), **once)
        else:
            b_spec = pl.BlockSpec((K, tn), lambda i, j: (0, j), **once)
        dims = _NN
    else:
        a_spec = pl.BlockSpec((tm, K), lambda i, j: (i, 0))
        if b_shard:
            ks = b.shape[2]
            b_spec = pl.BlockSpec((N_CHIPS, tn, ks), lambda i, j: (0, j, 0), **once)
        else:
            b_spec = pl.BlockSpec((tn, K), lambda i, j: (j, 0), **once)
        dims = _NT

    if o_shard:
        o_specs = [pl.BlockSpec((None, tm, tn), lambda i, j: (j // q, i, j % q))]
        o_shapes = [jax.ShapeDtypeStruct((N_CHIPS, M, ns), out_dtypes[0])]
    else:
        o_specs = [pl.BlockSpec((tm, tn), lambda i, j: (i, j)) for _ in out_dtypes]
        o_shapes = [jax.ShapeDtypeStruct((M, N if out_cols is None else out_cols[n]), dt)
                    for n, dt in enumerate(out_dtypes)]
    e_specs = [pl.BlockSpec((tm, tn), lambda i, j: (i, j)) for _ in extras]
    e_specs += [pl.BlockSpec(v.shape, lambda i, j: (0, 0)) for v in bcast]
    o_specs += [pl.BlockSpec(s, lambda i, j: (0, 0)) for s in accs]
    o_shapes += [jax.ShapeDtypeStruct(s, F32) for s in accs]
    n_e, n_b, n_o, n_a = len(extras), len(bcast), len(out_dtypes), len(accs)
    order = _after_operand(after)
    n_x = len(order)
    if epi is None:
        epi = lambda acc: (acc,)

    def body(a_ref, b_ref, *rest):
        e_refs, b_refs = rest[:n_e], rest[n_e:n_e + n_b]
        o_refs = rest[n_e + n_b + n_x:n_e + n_b + n_x + n_o]
        a_refs = rest[n_e + n_b + n_x + n_o:]
        if n_a:
            @pl.when((pl.program_id(0) == 0) & (pl.program_id(1) == 0))
            def _():
                for r in a_refs:
                    r[...] = jnp.zeros(r.shape, F32)
        if mode == "nt" and b_shard:
            acc = None
            for s in range(N_CHIPS):
                part = lax.dot_general(a_ref[:, s * ks:(s + 1) * ks], b_ref[s], dims, preferred_element_type=F32)
                acc = part if acc is None else acc + part
        else:
            acc = lax.dot_general(a_ref[...], b_ref[...], dims, preferred_element_type=F32)
        if ref_epi is not None:
            ref_epi(acc, e_refs, b_refs, o_refs, a_refs)
            return
        outs = epi(acc, *[r[...] for r in e_refs])
        for r, o in zip(o_refs, outs):
            r[...] = o.astype(r.dtype)

    outs = pl.pallas_call(
        body, name=name, grid=grid, in_specs=[a_spec, b_spec] + e_specs + [_ANY] * n_x, out_specs=o_specs,
        out_shape=o_shapes, compiler_params=_params(("arbitrary", "arbitrary") if n_a else ("parallel", "parallel")),
    )(a, b, *extras, *bcast, *order)
    return outs[0] if n_o + n_a == 1 else outs


def _rowwise(fn, rows, bcast, outs, accs=(), *, tm, name, after=None):
    def norm(r):
        return r if isinstance(r, tuple) else (r, r.shape[1], 0)

    rows = [norm(r) for r in rows]
    T = rows[0][0].shape[0]
    tm = min(tm, T)
    while T % tm:
        tm -= SUBLANES
    n_r, n_b, n_o, n_a = len(rows), len(bcast), len(outs), len(accs)
    order = _after_operand(after)
    n_x = len(order)
    in_specs = [pl.BlockSpec((tm, c), functools.partial(lambda i, cb: (i, cb), cb=cb)) for _, c, cb in rows]
    in_specs += [pl.BlockSpec(b.shape, lambda i: (0, 0)) for b in bcast] + [_ANY] * n_x
    out_specs = [pl.BlockSpec((tm, o[0]), lambda i: (i, 0)) for o in outs]
    out_specs += [pl.BlockSpec(s, lambda i: (0, 0)) for s in accs]
    out_shape = [jax.ShapeDtypeStruct((T, o[2] if len(o) > 2 else o[0]), o[1]) for o in outs]
    out_shape += [jax.ShapeDtypeStruct(s, F32) for s in accs]

    def body(*refs):
        in_refs = refs[:n_r]
        b_refs = refs[n_r:n_r + n_b]
        o_refs = refs[n_r + n_b + n_x:n_r + n_b + n_x + n_o]
        a_refs = refs[n_r + n_b + n_x + n_o:]
        if n_a:
            @pl.when(pl.program_id(0) == 0)
            def _():
                for r in a_refs:
                    r[...] = jnp.zeros(r.shape, F32)
        fn(in_refs, b_refs, o_refs, a_refs)

    res = pl.pallas_call(
        body, name=name, grid=(T // tm,), in_specs=in_specs, out_specs=out_specs, out_shape=out_shape,
        compiler_params=_params(("arbitrary",) if n_a else ("parallel",)),
    )(*[r[0] for r in rows], *bcast, *order)
    return res


def _rsum(x):
    return jnp.sum(x, axis=0, keepdims=True)


def _rms_fwd(x, g, name, after=None):
    def fn(ins, bs, outs, accs):
        xv = ins[0][...]
        r = lax.rsqrt(jnp.mean(xv * xv, axis=-1, keepdims=True) + EPS)
        outs[0][...] = (xv * r * bs[0][...]).astype(BF16)

    return _rowwise(fn, [x], [g], [(D_MODEL, BF16)], tm=512, name=name, after=after)[0]


def _rms_bwd_math(xv, dh, g):
    r = lax.rsqrt(jnp.mean(xv * xv, axis=-1, keepdims=True) + EPS)
    hn = xv * r
    dgh = dh * g
    dx = r * (dgh - hn * jnp.mean(dgh * hn, axis=-1, keepdims=True))
    return dx, _rsum(dh * hn)


def _mm_norm_bwd(dy, w, x, dres, g, name, after=None):
    def epilogue(acc, e_refs, b_refs, o_refs, a_refs):
        dx, dg = _rms_bwd_math(e_refs[0][...], acc, b_refs[0][...])
        dx = dx + e_refs[1][...]
        o_refs[0][...] = dx
        o_refs[1][...] = dx.astype(BF16)
        a_refs[0][...] += dg

    return _mm(dy, w, mode="nt", b_shard=True, out_dtypes=(F32, BF16), extras=(x, dres), bcast=(g,),
               accs=((1, D_MODEL),), ref_epi=epilogue, name=name, after=after)


def _mm_res_norm(a, w, res, g, name):
    def epilogue(acc, e_refs, b_refs, o_refs, a_refs):
        xv = acc + e_refs[0][...]
        o_refs[0][...] = xv
        r = lax.rsqrt(jnp.mean(xv * xv, axis=-1, keepdims=True) + EPS)
        o_refs[1][...] = (xv * r * b_refs[0][...]).astype(BF16)

    return _mm(a, w, mode="nn", out_dtypes=(F32, BF16), extras=(res,), bcast=(g,), ref_epi=epilogue, name=name)


def _mm_final_loss(a, w, res, target, g, name):
    def epilogue(acc, e_refs, b_refs, o_refs, a_refs):
        xv = acc + e_refs[0][...]
        gv = b_refs[0][...]
        r = lax.rsqrt(jnp.mean(xv * xv, axis=-1, keepdims=True) + EPS)
        e = xv * r * gv - e_refs[1][...]
        tok = jnp.mean(e * e, axis=-1, keepdims=True)
        a_refs[0][...] += 0.5 * jnp.sum(tok, axis=0, keepdims=True) * jnp.ones((1, LANES), F32)
        dx, dg = _rms_bwd_math(xv, e * (1.0 / D_MODEL), gv)
        o_refs[0][...] = dx
        o_refs[1][...] = dx.astype(BF16)
        a_refs[1][...] += dg

    return _mm(a, w, mode="nn", out_dtypes=(F32, BF16), extras=(res, target), bcast=(g,),
               accs=((1, LANES), (1, D_MODEL)), ref_epi=epilogue, name=name)


def _relu2(acc):
    r = jnp.maximum(acc, 0.0)
    return r * r, r


def _mlp_fwd(x, h, fetch, tag, finish):
    w_up = fetch(f"mlp{tag}_up", h)
    a, r = _mm(h, w_up, mode="nn", b_shard=True, out_dtypes=(BF16, BF16), epi=_relu2, name=f"mlp{tag}_up")
    w_down = fetch(f"mlp{tag}_down", a)
    return finish(a, w_down, x, f"mlp{tag}_down"), (h, a, r, w_up, w_down)


def _mlp_bwd(x, g, saved, dx, dx_bf, tag, after):
    h, a, r, w_up, w_down = saved
    d_down = _mm(a, dx_bf, mode="tn", out_dtypes=(BF16,), name=f"mlp{tag}_dwdown", after=after)
    dup = _mm(dx_bf, w_down, mode="nt", extras=(r,), out_dtypes=(BF16,),
              epi=lambda acc, rv: (acc * (2.0 * rv.astype(F32)),), name=f"mlp{tag}_dup")
    d_up = _mm(h, dup, mode="tn", o_shard=True, out_dtypes=(BF16,), name=f"mlp{tag}_dwup")
    dx_new, dx_new_bf, dg = _mm_norm_bwd(dup, w_up, x, dx, g, f"mlp{tag}_dh")
    return dx_new, dx_new_bf, dg, d_up, d_down


def _rope_tables(L, B):
    rows = L // GRID_W
    row = np.repeat(np.arange(rows, dtype=np.float32), GRID_W)
    col = np.tile(np.arange(GRID_W, dtype=np.float32), rows)
    inv = (ROPE_THETA ** (-np.arange(HEAD_DIM // 4, dtype=np.float32) / (HEAD_DIM // 4))).astype(np.float32)
    ar, ac = row[:, None] * inv, col[:, None] * inv
    cos = np.concatenate([np.cos(ar), np.cos(ar), np.cos(ac), np.cos(ac)], axis=-1)
    sin = np.concatenate([-np.sin(ar), np.sin(ar), -np.sin(ac), np.sin(ac)], axis=-1)
    return jnp.asarray(np.tile(cos, (B, 1)), F32), jnp.asarray(np.tile(sin, (B, 1)), F32)


def _swap_halves(x):
    lane = lax.broadcasted_iota(jnp.int32, x.shape, 1)
    return jnp.where((lane % 64) < 32, pltpu.roll(x, HEAD_DIM - 32, 1), pltpu.roll(x, 32, 1))


def _qk_prep(qkv, cos, sin, q_g, k_g):
    def fn(ins, bs, outs, accs):
        c, s = ins[1][...], ins[2][...]
        for h in range(N_HEADS + N_KV):
            xv = ins[0][:, h * HEAD_DIM:(h + 1) * HEAD_DIM]
            g = bs[0][...] if h < N_HEADS else bs[1][...]
            r = lax.rsqrt(jnp.mean(xv * xv, axis=-1, keepdims=True) + EPS)
            z = xv * r * g
            y = (z * c + _swap_halves(z) * s).astype(BF16)
            if h < N_HEADS:
                outs[0][:, h * HEAD_DIM:(h + 1) * HEAD_DIM] = y
            else:
                outs[1][:, (h - N_HEADS) * HEAD_DIM:(h - N_HEADS + 1) * HEAD_DIM] = y
        outs[2][...] = ins[0][:, (N_HEADS + N_KV) * HEAD_DIM:].astype(BF16)

    kvw = N_KV * HEAD_DIM
    return _rowwise(fn, [qkv, cos, sin], [q_g, k_g], [(D_MODEL, BF16), (kvw, BF16), (kvw, BF16)], tm=512,
                    name="attn_qk_prep")


def _qk_prep_bwd(qkv, dq, dk, dv, cos, sin, q_g, k_g):
    def fn(ins, bs, outs, accs):
        c, s = ins[4][...], ins[5][...]
        for h in range(N_HEADS + N_KV):
            sl = slice(h * HEAD_DIM, (h + 1) * HEAD_DIM)
            xv = ins[0][:, sl]
            if h < N_HEADS:
                g, dy, acc = bs[0][...], ins[1][:, sl], accs[0]
            else:
                ks = slice((h - N_HEADS) * HEAD_DIM, (h - N_HEADS + 1) * HEAD_DIM)
                g, dy, acc = bs[1][...], ins[2][:, ks], accs[1]
            r = lax.rsqrt(jnp.mean(xv * xv, axis=-1, keepdims=True) + EPS)
            xn = xv * r
            dz = dy * c - _swap_halves(dy) * s
            acc[...] += _rsum(dz * xn)
            dxn = dz * g
            outs[0][:, sl] = (r * (dxn - xn * jnp.mean(dxn * xn, axis=-1, keepdims=True))).astype(BF16)
        outs[0][:, (N_HEADS + N_KV) * HEAD_DIM:] = ins[3][...].astype(BF16)

    return _rowwise(fn, [qkv, dq, dk, dv, cos, sin], [q_g, k_g], [(qkv.shape[1], BF16)],
                    [(1, HEAD_DIM), (1, HEAD_DIM)], tm=512, name="attn_qk_prep_bwd")


_EXP2_SCALE = SCALE * math.log2(math.e)


def _exp_rows(q, k):
    s = lax.dot_general(q, k, _NT, preferred_element_type=F32)
    p = jnp.exp2((s - jnp.max(s, axis=-1, keepdims=True)) * _EXP2_SCALE)
    return p, jnp.sum(p, axis=-1, keepdims=True)


def _attn_fwd(q, k, v, B, L, tq=2048, sub=256):
    tq = min(tq, L)
    sub = min(sub, tq)
    nq = L // tq

    def body(q_ref, k_ref, v_ref, o_ref):
        kv, vv = k_ref[...], v_ref[...]
        for c in range(tq // sub):
            rows = slice(c * sub, (c + 1) * sub)
            p, l = _exp_rows(q_ref[rows, :], kv)
            o = jnp.dot(p.astype(BF16), vv, preferred_element_type=F32)
            o_ref[rows, :] = (o * (1.0 / l)).astype(o_ref.dtype)

    return pl.pallas_call(
        body, name="attn_fwd", grid=(B, N_HEADS, nq),
        in_specs=[pl.BlockSpec((tq, HEAD_DIM), lambda b, h, i: (b * nq + i, h)),
                  pl.BlockSpec((L, HEAD_DIM), lambda b, h, i: (b, h // GROUP)),
                  pl.BlockSpec((L, HEAD_DIM), lambda b, h, i: (b, h // GROUP))],
        out_specs=pl.BlockSpec((tq, HEAD_DIM), lambda b, h, i: (b * nq + i, h)),
        out_shape=jax.ShapeDtypeStruct((B * L, D_MODEL), BF16),
        compiler_params=_params(("parallel", "parallel", "parallel")),
    )(q, k, v)


def _attn_bwd(q, k, v, o, do, B, L, tq=2048, sub=512):
    tq = min(tq, L)
    sub = min(sub, tq)
    nq = L // tq

    def body(q_ref, k_ref, v_ref, o_ref, do_ref, dq_ref, dk_ref, dv_ref):
        @pl.when((pl.program_id(2) == 0) & (pl.program_id(3) == 0))
        def _():
            dk_ref[...] = jnp.zeros(dk_ref.shape, F32)
            dv_ref[...] = jnp.zeros(dv_ref.shape, F32)

        kv, vv = k_ref[...], v_ref[...]
        ps, es, dos, qs = [], [], [], []
        for c in range(tq // sub):
            rows = slice(c * sub, (c + 1) * sub)
            qc, doc = q_ref[rows, :], do_ref[rows, :]
            p, l = _exp_rows(qc, kv)
            inv = 1.0 / l
            dp = lax.dot_general(doc, vv, _NT, preferred_element_type=F32)
            delta = jnp.sum(doc.astype(F32) * o_ref[rows, :].astype(F32), axis=-1, keepdims=True)
            e = (p * (dp - delta)).astype(BF16)
            dq_ref[rows, :] = jnp.dot(e, kv, preferred_element_type=F32) * (inv * SCALE)
            ps.append(p.astype(BF16))
            es.append(e)
            dos.append((doc.astype(F32) * inv).astype(BF16))
            qs.append((qc.astype(F32) * (inv * SCALE)).astype(BF16))
        cat = lambda xs: xs[0] if len(xs) == 1 else jnp.concatenate(xs, axis=0)
        dv_ref[...] += lax.dot_general(cat(ps), cat(dos), _TN, preferred_element_type=F32)
        dk_ref[...] += lax.dot_general(cat(es), cat(qs), _TN, preferred_element_type=F32)

    qmap = lambda b, kh, g, i: (b * nq + i, kh * GROUP + g)
    kmap = lambda b, kh, g, i: (b, kh)
    kvw = N_KV * HEAD_DIM
    return pl.pallas_call(
        body, name="attn_bwd", grid=(B, N_KV, GROUP, nq),
        in_specs=[pl.BlockSpec((tq, HEAD_DIM), qmap), pl.BlockSpec((L, HEAD_DIM), kmap),
                  pl.BlockSpec((L, HEAD_DIM), kmap), pl.BlockSpec((tq, HEAD_DIM), qmap),
                  pl.BlockSpec((tq, HEAD_DIM), qmap)],
        out_specs=[pl.BlockSpec((tq, HEAD_DIM), qmap), pl.BlockSpec((L, HEAD_DIM), kmap),
                   pl.BlockSpec((L, HEAD_DIM), kmap)],
        out_shape=[jax.ShapeDtypeStruct((B * L, D_MODEL), F32), jax.ShapeDtypeStruct((B * L, kvw), F32),
                   jax.ShapeDtypeStruct((B * L, kvw), F32)],
        compiler_params=_params(("parallel", "parallel", "arbitrary", "arbitrary")),
    )(q, k, v, o, do)


def _conv_shift(x, t, L, k):
    if k == 2:
        return x
    if k < 2:
        return jnp.where(t >= 2 - k, pltpu.roll(x, 2 - k, 0), 0.0)
    return jnp.where(t < L - (k - 2), pltpu.roll(x, L - (k - 2), 0), 0.0)


def _conv_apply(x, w_ref, L):
    t = lax.broadcasted_iota(jnp.int32, x.shape, 0)
    acc = w_ref[4:5, :] + w_ref[2:3, :] * x
    for k in (0, 1, 3):
        acc = acc + w_ref[k:k + 1, :] * _conv_shift(x, t, L, k)
    return acc


def _conv_bwd(z, g, wb, dz, B, L, tc=256, after=None):
    noff = D_MODEL // tc
    order = _after_operand(after)

    def body(z_ref, g_ref, w_ref, dz_in, *rest):
        dx_ref, dw_ref = rest[len(order):]

        @pl.when(pl.program_id(1) == 0)
        def _():
            dw_ref[...] = jnp.zeros(dw_ref.shape, F32)

        x, gv = z_ref[...], g_ref[...]
        t = lax.broadcasted_iota(jnp.int32, x.shape, 0)
        dx = w_ref[2:3, :] * gv
        for k in (0, 1, 3):
            dx = dx + w_ref[k:k + 1, :] * _conv_shift(gv, t, L, 4 - k)
        dx_ref[...] = dx.astype(BF16)
        for k in range(4):
            dw_ref[k:k + 1, :] += _rsum(_conv_shift(x, t, L, k) * gv)
        dw_ref[4:5, :] += _rsum(gv)

    return pl.pallas_call(
        body, name="rg_conv_bwd", grid=(noff, B),
        in_specs=[pl.BlockSpec((L, tc), lambda j, b: (b, noff + j)), pl.BlockSpec((L, tc), lambda j, b: (b, j)),
                  pl.BlockSpec((SUBLANES, tc), lambda j, b: (0, j)), _ANY] + [_ANY] * len(order),
        out_specs=[pl.BlockSpec((L, tc), lambda j, b: (b, noff + j)),
                   pl.BlockSpec((SUBLANES, tc), lambda j, b: (0, j))],
        out_shape=[jax.ShapeDtypeStruct(dz.shape, dz.dtype), jax.ShapeDtypeStruct((SUBLANES, D_MODEL), F32)],
        input_output_aliases={3: 0},
        compiler_params=_params(("parallel", "arbitrary")),
    )(z, g, wb, dz, *order)


def _softplus(x):
    return jnp.maximum(x, 0.0) + jnp.log1p(jnp.exp(-jnp.abs(x)))


_ROW_BA, _ROW_BX, _ROW_LAM = 0, 2, 4


def _gate_math(xb, pre, vec_ref, d, sl):
    pa = pre[:, (2 * d) * LRU_BW:(2 * d + 1) * LRU_BW] + vec_ref[_ROW_BA + d:_ROW_BA + d + 1, sl]
    px = pre[:, (2 * d + 1) * LRU_BW:(2 * d + 2) * LRU_BW] + vec_ref[_ROW_BX + d:_ROW_BX + d + 1, sl]
    r = 0.5 * jnp.tanh(0.5 * pa) + 0.5
    i = 0.5 * jnp.tanh(0.5 * px) + 0.5
    slope = (-RG_C) * _softplus(-vec_ref[_ROW_LAM + d:_ROW_LAM + d + 1, sl])
    log_a = r * slope
    a = jnp.exp(log_a)
    om = -jnp.tanh(log_a) * (1.0 + a * a)
    rs = lax.rsqrt(om)
    mult = jnp.where(om > 0.0, om * rs, 0.0)
    return a, mult * (i * xb), (r, i, slope, om, mult, rs)


def _gate_bwd(rec, du_f, da_f, du_b, da_b, wcat, gvec):
    def fn(ins, bs, outs, accs):
        for blk in range(LRU_BLOCKS):
            sl = slice(blk * LRU_BW, (blk + 1) * LRU_BW)
            xb = ins[0][:, sl]
            xb16 = xb.astype(BF16)
            w = bs[0][sl, :]
            pre = jnp.dot(xb16, w, preferred_element_type=F32)
            dx = jnp.zeros_like(xb)
            dpre = []
            for d in range(2):
                a, _, (r, i, slope, om, mult, rs) = _gate_math(xb, pre, bs[1], d, sl)
                du, da = ins[1 + 2 * d][:, sl], ins[2 + 2 * d][:, sl]
                t = du * xb
                d_i = t * mult
                dx = dx + du * mult * i
                dlog = da * a - (t * i) * ((1.0 - om) * rs)
                d_r = dlog * slope
                d_sp = _rsum(dlog * r) * (-RG_C)
                lam = bs[1][_ROW_LAM + d:_ROW_LAM + d + 1, sl]
                accs[2][_ROW_LAM + d:_ROW_LAM + d + 1, sl] += d_sp * (-jax.nn.sigmoid(-lam))
                dpa = d_r * r * (1.0 - r)
                dpx = d_i * i * (1.0 - i)
                accs[2][_ROW_BA + d:_ROW_BA + d + 1, sl] += _rsum(dpa)
                accs[2][_ROW_BX + d:_ROW_BX + d + 1, sl] += _rsum(dpx)
                dpre += [dpa, dpx]
            dpre = jnp.concatenate(dpre, axis=1).astype(BF16)
            dw = lax.dot_general(xb16, dpre, _TN, preferred_element_type=F32)
            for d in range(2):
                rows = slice(d * D_MODEL + blk * LRU_BW, d * D_MODEL + (blk + 1) * LRU_BW)
                accs[0][rows, :] += dw[:, (2 * d) * LRU_BW:(2 * d + 1) * LRU_BW]
                accs[1][rows, :] += dw[:, (2 * d + 1) * LRU_BW:(2 * d + 2) * LRU_BW]
            outs[0][:, sl] = dx + lax.dot_general(dpre, w, _NT, preferred_element_type=F32)

    gate_shape = (2 * D_MODEL, LRU_BW)
    return _rowwise(fn, [rec, du_f, da_f, du_b, da_b], [wcat, gvec], [(D_MODEL, F32)],
                    [gate_shape, gate_shape, (SUBLANES, D_MODEL)], tm=512, name="rg_gate_bwd")


def _as_time_blocks(x):
    return x.reshape(x.shape[0] // SUBLANES, SUBLANES, x.shape[1])


def _scan_call(body, ins, n_out, B, L, tc, name):
    nb = L // SUBLANES
    spec = pl.BlockSpec((nb, SUBLANES, tc), lambda b, j: (b, 0, j))
    T = ins[0].shape[0]
    outs = pl.pallas_call(
        functools.partial(body, nb), name=name, grid=(B, D_MODEL // tc),
        in_specs=[spec] * len(ins), out_specs=[spec] * n_out,
        out_shape=[jax.ShapeDtypeStruct((T // SUBLANES, SUBLANES, D_MODEL), F32)] * n_out,
        compiler_params=_params(("parallel", "parallel")),
    )(*[_as_time_blocks(x) for x in ins])
    return [o.reshape(T, D_MODEL) for o in outs]


def _block_scan(A, U, reverse):
    row = lax.broadcasted_iota(jnp.int32, A.shape, 0)
    for s in (1, 2, 4):
        shift = SUBLANES - s if reverse else s
        valid = (row < SUBLANES - s) if reverse else (row >= s)
        a_sh = jnp.where(valid, pltpu.roll(A, shift, 0), 1.0)
        u_sh = jnp.where(valid, pltpu.roll(U, shift, 0), 0.0)
        U = A * u_sh + U
        A = A * a_sh
    return A, U


_LAST = SUBLANES - 1
SCAN_UNROLL = 8


def _loop_blocks(nb, step, init):
    def group(g, carry):
        for k in range(SCAN_UNROLL):
            carry = step(g * SCAN_UNROLL + k, carry)
        return carry

    return lax.fori_loop(0, nb // SCAN_UNROLL, group, init)


def _scan_bwd(dy, a_f, h_f, a_b, h_b, B, L, tc=256):
    def body(nb, dy_r, af, hf, ab, hb, duf, daf, dub, dab):
        def step(i, carry):
            c1, c2 = carry
            ir = nb - 1 - i
            row = lax.broadcasted_iota(jnp.int32, (SUBLANES, tc), 0)
            a_up = jnp.where(row == _LAST, af[jnp.minimum(ir + 1, nb - 1), :1, :], pltpu.roll(af[ir], _LAST, 0))
            p, lam = _block_scan(a_up, dy_r[ir], True)
            lam = lam + p * c1
            before = hf[jnp.maximum(ir - 1, 0), _LAST:, :] * (ir > 0).astype(F32)
            duf[ir] = lam
            daf[ir] = lam * jnp.where(row == 0, before, pltpu.roll(hf[ir], 1, 0))
            a_dn = jnp.where(row == 0, ab[jnp.maximum(i - 1, 0), _LAST:, :], pltpu.roll(ab[i], 1, 0))
            p2, lam2 = _block_scan(a_dn, dy_r[i], False)
            lam2 = lam2 + p2 * c2
            after = hb[jnp.minimum(i + 1, nb - 1), :1, :] * (i < nb - 1).astype(F32)
            dub[i] = lam2
            dab[i] = lam2 * jnp.where(row == _LAST, after, pltpu.roll(hb[i], _LAST, 0))
            return lam[:1, :], lam2[_LAST:, :]

        zero = jnp.zeros((1, tc), F32)
        _loop_blocks(nb, step, (zero, zero))

    return _scan_call(body, [dy, a_f, h_f, a_b, h_b], 4, B, L, tc, "rg_scan_bwd")


_GELU_C = math.sqrt(2.0 / math.pi)


def _gelu_parts(x):
    th = jnp.tanh(_GELU_C * (x + 0.044715 * x * x * x))
    return 0.5 * x * (1.0 + th), th


def _mm_gated_out_bwd(dx, w_out, h_f, h_b, z, name, after=None):
    def epilogue(acc, e_refs, b_refs, o_refs, a_refs):
        x = e_refs[2][...]
        gl, th = _gelu_parts(x)
        dgl = 0.5 * (1.0 + th) + 0.5 * x * (1.0 - th * th) * (_GELU_C * (1.0 + 3.0 * 0.044715 * x * x))
        o_refs[0][...] = acc * gl
        o_refs[1][...] = (acc * (e_refs[0][...] + e_refs[1][...]) * dgl).astype(BF16)

    return _mm(dx, w_out, mode="nt", out_dtypes=(F32, BF16), out_cols=(D_MODEL, 2 * D_MODEL), extras=(h_f, h_b, z),
               ref_epi=epilogue, name=name, after=after)


def _row_block(i):
    return pl.ds(pl.multiple_of(i * SUBLANES, SUBLANES), SUBLANES)


def _rg_mix_fwd(z, conv_wb, wcat, gvec, B, L):
    nb = L // SUBLANES
    n_g = D_MODEL // LRU_BW

    def body(zg_ref, zr_ref, cw_ref, w_ref, gv_ref, rec_ref, af_s, ab_s, hf_ref, hb_ref, yg_ref, uf_s, ub_s):
        rec = _conv_apply(zr_ref[...], cw_ref, L)
        rec_ref[...] = rec
        pre = jnp.dot(rec.astype(BF16), w_ref[...], preferred_element_type=F32)
        for d, (a_s, u_s) in enumerate(((af_s, uf_s), (ab_s, ub_s))):
            a, u, _ = _gate_math(rec, pre, gv_ref, d, slice(None))
            a_s[...] = a
            u_s[...] = u

        def step(i, carry):
            c1, c2 = carry
            rows, rows_b = _row_block(i), _row_block(nb - 1 - i)
            p, h = _block_scan(af_s[rows, :], uf_s[rows, :], False)
            h = h + p * c1
            hf_ref[rows, :] = h
            p2, h2 = _block_scan(ab_s[rows_b, :], ub_s[rows_b, :], True)
            h2 = h2 + p2 * c2
            hb_ref[rows_b, :] = h2
            return h[_LAST:, :], h2[:1, :]

        zero = jnp.zeros((1, LRU_BW), F32)
        _loop_blocks(nb, step, (zero, zero))
        gl, _ = _gelu_parts(zg_ref[...])
        yg_ref[...] = ((hf_ref[...] + hb_ref[...]) * gl).astype(BF16)

    seq = lambda off: pl.BlockSpec((L, LRU_BW), lambda b, g: (b, off + g))
    vec = pl.BlockSpec((SUBLANES, LRU_BW), lambda b, g: (0, g))
    T = B * L
    return pl.pallas_call(
        body, name="rg_mix", grid=(B, n_g),
        in_specs=[seq(0), seq(n_g), vec, pl.BlockSpec((LRU_BW, 4 * LRU_BW), lambda b, g: (g, 0)), vec],
        out_specs=[seq(0)] * 6,
        out_shape=[jax.ShapeDtypeStruct((T, D_MODEL), F32)] * 5 + [jax.ShapeDtypeStruct((T, D_MODEL), BF16)],
        scratch_shapes=[pltpu.VMEM((L, LRU_BW), F32)] * 2,
        compiler_params=_params(("parallel", "parallel")),
    )(z, z, conv_wb, wcat, gvec)


def _make_wcat(w_a, w_x):
    g = jnp.stack([w_a[0, 0], w_x[0, 0], w_a[0, 1], w_x[0, 1]])
    return jnp.transpose(g, (1, 2, 0, 3)).reshape(D_MODEL, 4 * LRU_BW)


def _rows_at(part, first):
    return jnp.pad(part, ((first, SUBLANES - first - part.shape[0]), (0, 0)))


def _qk_slot(q_g, k_g):
    wide = lambda v, at: jnp.pad(v, ((0, SUBLANES - 1), (at, D_MODEL - at - HEAD_DIM)))
    return wide(q_g, 0) + wide(k_g, HEAD_DIM)


def _local_step(x, target, P, fetch, emit, B, L, after=None):
    g_mix, g_mlp = P["norm_mix_g"], P["norm_mlp_g"]
    h0 = _rms_fwd(x, g_mix[0:1], "rg_norm", after=after)
    w_in, conv_wb, wcat, gvec = fetch("rg", h0)
    z = _mm(h0, w_in, mode="nn", b_shard=True, name="rg_in")
    rec, a_f, a_b, h_f, h_b, yg = _rg_mix_fwd(z, conv_wb, wcat, gvec, B, L)
    w_out = fetch("rg_out", yg)
    x1, h1 = _mm_res_norm(yg, w_out, x, g_mlp[0:1], "rg_out")
    (x2, h3), mlp0 = _mlp_fwd(x1, h1, fetch, 0, lambda a, w, res, name: _mm_res_norm(a, w, res, g_mix[1:2], name))
    w_qkv, w_o = fetch("att", h3)
    qkv = _mm(h3, w_qkv, mode="nn", b_shard=True, name="attn_qkv")
    cos, sin = _rope_tables(L, B)
    qh, kh, vh = _qk_prep(qkv, cos, sin, P["q_g"], P["k_g"])
    o = _attn_fwd(qh, kh, vh, B, L)
    x3, h4 = _mm_res_norm(o, w_o, x2, g_mlp[1:2], "attn_out")
    (dx4, dx4_bf, loss_acc, d_final_g), mlp1 = _mlp_fwd(
        x3, h4, fetch, 1, lambda a, w, res, name: _mm_final_loss(a, w, res, target, P["final_g"], name))

    dx3, dx3_bf, dg_mlp1, d_up1, d_down1 = _mlp_bwd(x3, g_mlp[1:2], mlp1, dx4, dx4_bf, 1, None)
    tok = emit("mlp1", [d_up1, d_down1])
    d_wo = _mm(o, dx3_bf, mode="tn", out_dtypes=(BF16,), name="attn_dwo", after=tok)
    do = _mm(dx3_bf, w_o, mode="nt", out_dtypes=(BF16,), name="attn_do")
    dq, dk, dv = _attn_bwd(qh, kh, vh, o, do, B, L)
    dqkv, dq_g, dk_g = _qk_prep_bwd(qkv, dq, dk, dv, cos, sin, P["q_g"], P["k_g"])
    d_wqkv = _mm(h3, dqkv, mode="tn", o_shard=True, out_dtypes=(BF16,), name="attn_dwqkv")
    tok = emit("att", [d_wqkv, d_wo])
    dx2, dx2_bf, dg_mix1 = _mm_norm_bwd(dqkv, w_qkv, x2, dx3, g_mix[1:2], "attn_dh", after=tok)
    tok = emit("point_attn_done", [dx2_bf])
    dx1, dx1_bf, dg_mlp0, d_up0, d_down0 = _mlp_bwd(x1, g_mlp[0:1], mlp0, dx2, dx2_bf, 0, tok)
    d_wout = _mm(yg, dx1_bf, mode="tn", out_dtypes=(BF16,), name="rg_dwout")
    tok = emit("mlp0", [d_up0, d_down0, d_wout])
    dy, dgate = _mm_gated_out_bwd(dx1_bf, w_out, h_f, h_b, z, "rg_dyg", after=tok)
    du_f, da_f, du_b, da_b = _scan_bwd(dy, a_f, h_f, a_b, h_b, B, L)
    drec_c, d_wa, d_wx, d_gvec = _gate_bwd(rec, du_f, da_f, du_b, da_b, wcat, gvec)
    tok = emit("gates", [d_wa, d_wx])
    dz, d_convwb = _conv_bwd(z, drec_c, conv_wb, dgate, B, L, after=tok)
    tok = emit("point_mix_done", [dz])
    d_win = _mm(h0, dz, mode="tn", o_shard=True, out_dtypes=(BF16,), name="rg_dwin", after=tok)
    tok = emit("rg_in", [d_win])
    grad_x, _, dg_mix0 = _mm_norm_bwd(dz, w_in, x, dx1, g_mix[0:1], "rg_dh", after=tok)

    norms = (_rows_at(dg_mix0, 0) + _rows_at(dg_mix1, 1) + _rows_at(dg_mlp0, 2) + _rows_at(dg_mlp1, 3)
             + _rows_at(d_final_g, 4)
             + jnp.pad(loss_acc, ((LOSS_ROW, SUBLANES - 1 - LOSS_ROW), (0, D_MODEL - LANES))))
    vec = jnp.concatenate([norms, d_convwb, d_gvec, _qk_slot(dq_g, dk_g)], axis=0)
    return grad_x, vec


_MESH = pl.DeviceIdType.MESH


def _place():
    x, y, c = lax.axis_index("x"), lax.axis_index("y"), lax.axis_index("c")
    peers = [((1 - x) if j & 2 else x, (1 - y) if j & 1 else y) for j in (1, 2, 3)]
    return x, y, c, peers


def _sum_leading(slots, name):
    def body(s_ref, o_ref):
        acc = s_ref[0]
        for d in range(1, slots.shape[0]):
            acc = acc + s_ref[d]
        o_ref[...] = acc

    return pl.pallas_call(body, name=name, out_shape=jax.ShapeDtypeStruct(slots.shape[1:], slots.dtype))(slots)


_HBM = pl.BlockSpec(memory_space=pltpu.HBM)
_SEM = pl.BlockSpec(memory_space=pltpu.SEMAPHORE)
_EFFECT = pltpu.SideEffectType.DATAFLOW_SIDE_EFFECTING


_COPIES = dict(gather=N_CHIPS - 1, scatter=N_CHIPS - 1, swap=1, spread=N_DEVICES - 1,
               gather_half=N_CHIPS - 1, share_half=N_CHIPS - 1)


def _split_copies(kind, srcs, lands, send, recv):
    x, y, c, peers = _place()
    me = 2 * x + y
    per = _COPIES[kind]
    out = []
    for a in range(len(lands)):
        for j in range(per):
            if kind == "swap":
                src, there, here, dev = srcs[a], lands[a], lands[a], (x, y, 1 - c)
            elif kind == "spread":
                k = j + 1
                dev = ((1 - x) if k & 4 else x, (1 - y) if k & 2 else y, (1 - c) if k & 1 else c)
                mine = lands[a].at[4 * x + 2 * y + c]
                src, there, here = mine, mine, lands[a].at[4 * dev[0] + 2 * dev[1] + dev[2]]
            else:
                px, py = peers[j]
                dev = (px, py, c)
                if kind == "gather":
                    src, there, here = lands[a].at[me], lands[a].at[me], lands[a].at[2 * px + py]
                elif kind in ("gather_half", "share_half"):
                    half = lands[a].shape[1] // 2
                    mine, other = pl.ds(c * half, half), pl.ds((1 - c) * half, half)
                    if kind == "gather_half":
                        src = there = lands[a].at[me, mine]
                        here = lands[a].at[2 * px + py, mine]
                    else:
                        src = there = lands[a].at[2 * px + py, mine]
                        here = lands[a].at[2 * px + py, other]
                        dev = (x, y, 1 - c)
                else:
                    src, there, here = srcs[a].at[2 * px + py], lands[a].at[j], lands[a].at[j]
            mk = functools.partial(
                pltpu.make_async_remote_copy, src_ref=src, send_sem=send.at[per * a + j],
                recv_sem=recv.at[per * a + j], device_id=dev, device_id_type=_MESH)
            out.append((functools.partial(mk, dst_ref=there), functools.partial(mk, dst_ref=here)))
    return out


_CORE_PAIR = ("swap", "share_half")
CORE_PAIR_BARRIER_ID = 0
BARRIER_IDS = dict(gather_rest=1, scatter_mlp1=2, scatter_att=3, scatter_mlp0=4, scatter_gates=5, scatter_rg_in=6,
                   gather_gates=7, spread_vec=8)


def _entry_peers(kind):
    x, y, c, peers = _place()
    if kind in _CORE_PAIR:
        return [(x, y, 1 - c)]
    if kind == "spread":
        return [((1 - x) if k & 4 else x, (1 - y) if k & 2 else y, (1 - c) if k & 1 else c)
                for k in range(1, N_DEVICES)]
    return [(px, py, c) for px, py in peers]


def _entry_params(kind, barrier_id):
    if kind in _CORE_PAIR:
        barrier_id = CORE_PAIR_BARRIER_ID
    collective = {} if barrier_id is None else dict(collective_id=barrier_id)
    return pltpu.CompilerParams(has_side_effects=_EFFECT, **collective)


def _entry_handshake(kind, barrier_id):
    if kind in _CORE_PAIR or barrier_id is not None:
        barrier = pltpu.get_barrier_semaphore()
        peers = _entry_peers(kind)
        for peer in peers:
            pl.semaphore_signal(barrier, inc=1, device_id=peer, device_id_type=_MESH)
        pl.semaphore_wait(barrier, len(peers))


def _exchange_start(kind, srcs, lands, name, after=None, barrier_id=None):
    arrays = list(srcs) + list(lands)
    n_s, n, n_all = len(srcs), len(lands), len(srcs) + len(lands)
    n_sem = _COPIES[kind] * n
    order = _after_operand(after)
    n_x = len(order)

    def body(*refs):
        _entry_handshake(kind, barrier_id)
        send, recv = refs[n_all + n_x], refs[n_all + n_x + 1]
        token = refs[-1]
        for started, _ in _split_copies(kind, refs[:n_s], refs[n_s:n_all], send, recv):
            started().start()
        token[...] = jnp.zeros(token.shape, F32)

    res = pl.pallas_call(
        body, name=name,
        out_shape=(pltpu.SemaphoreType.DMA((n_sem,)), pltpu.SemaphoreType.DMA((n_sem,)),
                   *[pltpu.HBM(a.shape, a.dtype) for a in arrays], jax.ShapeDtypeStruct((SUBLANES, LANES), F32)),
        in_specs=[_HBM] * n_all + [_ANY] * n_x,
        out_specs=(_SEM, _SEM, *[_HBM] * n_all, pl.BlockSpec(memory_space=pltpu.VMEM)),
        input_output_aliases={i: 2 + i for i in range(n_all)},
        compiler_params=_entry_params(kind, barrier_id),
    )(*[pltpu.with_memory_space_constraint(a, pltpu.HBM) for a in arrays], *order)
    return (res[0], res[1], res[2:2 + n_s], res[2 + n_s:2 + n_all]), res[-1]


def _gather_start_groups(land_groups, name, after=None, kind="gather", barrier_id=None):
    arrays = [a for group in land_groups for a in group]
    n_all, n_g = len(arrays), len(land_groups)
    order = _after_operand(after)
    n_x = len(order)

    def body(*refs):
        _entry_handshake(kind, barrier_id)
        first = 0
        for gi, group in enumerate(land_groups):
            send, recv = refs[n_all + n_x + 2 * gi], refs[n_all + n_x + 2 * gi + 1]
            for started, _ in _split_copies(kind, [], refs[first:first + len(group)], send, recv):
                started().start()
            first += len(group)
        refs[-1][...] = jnp.zeros(refs[-1].shape, F32)

    sems = [pltpu.SemaphoreType.DMA((_COPIES[kind] * len(group),)) for group in land_groups for _ in range(2)]
    res = pl.pallas_call(
        body, name=name,
        out_shape=(*sems, *[pltpu.HBM(a.shape, a.dtype) for a in arrays], jax.ShapeDtypeStruct((SUBLANES, LANES), F32)),
        in_specs=[_HBM] * n_all + [_ANY] * n_x,
        out_specs=(*[_SEM] * (2 * n_g), *[_HBM] * n_all, pl.BlockSpec(memory_space=pltpu.VMEM)),
        input_output_aliases={i: 2 * n_g + i for i in range(n_all)},
        compiler_params=_entry_params(kind, barrier_id),
    )(*[pltpu.with_memory_space_constraint(a, pltpu.HBM) for a in arrays], *order)
    handles, first = [], 2 * n_g
    for gi, group in enumerate(land_groups):
        handles.append((res[2 * gi], res[2 * gi + 1], [], res[first:first + len(group)]))
        first += len(group)
    return handles, res[-1]


def _exchange_wait(kind, handle, after, name):
    send, recv, srcs, lands = handle
    arrays = list(srcs) + list(lands)
    n_s, n_all = len(srcs), len(arrays)
    order = list(after) if isinstance(after, (list, tuple)) else [after]

    def body(*refs):
        for started, landing in _split_copies(kind, refs[:n_s], refs[n_s:n_all], refs[n_all], refs[n_all + 1]):
            started().wait_send()
            landing().wait_recv()

    res = pl.pallas_call(
        body, name=name, out_shape=[pltpu.HBM(a.shape, a.dtype) for a in arrays],
        in_specs=[_HBM] * n_all + [_SEM, _SEM] + [_ANY] * len(order), out_specs=[_HBM] * n_all,
        input_output_aliases={i: i for i in range(n_all)},
        compiler_params=pltpu.CompilerParams(has_side_effects=_EFFECT),
    )(*arrays, send, recv, *order)
    return res[:n_s], res[n_s:]


def _index_operand(i):
    return jnp.reshape(i, (1,)).astype(jnp.int32)


STREAM_ROWS = 256


def _cast_into_slot(src, row0, rows, me, dtype, name, after=None, add=None, n_slots=N_CHIPS):
    cols = src.shape[1]
    tm = min(STREAM_ROWS, rows)
    order = _after_operand(after)
    terms = [src] + ([] if add is None else [add])

    def body(me_ref, *rest):
        val = rest[0][...]
        if add is not None:
            val = val + rest[1][...]
        rest[-1][...] = val.astype(dtype)

    return pl.pallas_call(
        body, name=name,
        grid_spec=pltpu.PrefetchScalarGridSpec(
            num_scalar_prefetch=1, grid=(rows // tm,),
            in_specs=[pl.BlockSpec((tm, cols), lambda i, me_ref: (i + row0 // tm, 0))] * len(terms)
            + [_ANY] * len(order),
            out_specs=pl.BlockSpec((None, tm, cols), lambda i, me_ref: (me_ref[0], i, 0))),
        out_shape=jax.ShapeDtypeStruct((n_slots, rows, cols), dtype), compiler_params=_params(("parallel",)),
    )(_index_operand(me), *terms, *order)


def _sum_slots(mine, r, me, name):
    _, rows, cols = r.shape
    tm = min(STREAM_ROWS, rows)

    def body(me_ref, own_ref, r_ref, o_ref):
        o_ref[...] = ((own_ref[...].astype(F32) + r_ref[0].astype(F32)) + r_ref[1].astype(F32)) + r_ref[2].astype(F32)

    return pl.pallas_call(
        body, name=name,
        grid_spec=pltpu.PrefetchScalarGridSpec(
            num_scalar_prefetch=1, grid=(rows // tm,),
            in_specs=[pl.BlockSpec((None, tm, cols), lambda i, me_ref: (me_ref[0], i, 0)),
                      pl.BlockSpec((N_CHIPS - 1, tm, cols), lambda i, me_ref: (0, i, 0))],
            out_specs=pl.BlockSpec((tm, cols), lambda i, me_ref: (i, 0))),
        out_shape=jax.ShapeDtypeStruct((rows, cols), F32), compiler_params=_params(("parallel",)),
    )(_index_operand(me), mine, r)


def _adamw(w, m, v, ps, qs, name):
    rows, cols = w.shape
    seg_rows = ps[0].shape[0]
    tm = min(STREAM_ROWS, seg_rows)
    while seg_rows % tm:
        tm -= SUBLANES
    per, n_seg = seg_rows // tm, len(ps)
    parts = list(ps) + ([] if qs is None else list(qs))

    def body(w_ref, m_ref, v_ref, *rest):
        g_refs, outs = rest[:len(parts)], rest[len(parts):]
        grad = lambda s: g_refs[s][...] if qs is None else g_refs[s][...] + g_refs[n_seg + s][...]
        g = grad(0)
        for s in range(1, n_seg):
            g = jnp.where(pl.program_id(0) >= s * per, grad(s), g)
        m1 = ADAM_B1 * m_ref[...] + (1.0 - ADAM_B1) * g
        v1 = ADAM_B2 * v_ref[...] + (1.0 - ADAM_B2) * (g * g)
        m_hat = m1 / (1.0 - ADAM_B1 ** ADAM_STEP)
        v_hat = v1 / (1.0 - ADAM_B2 ** ADAM_STEP)
        outs[0][...] = g
        outs[1][...] = (-ADAM_LR) * (m_hat / (jnp.sqrt(v_hat) + ADAM_EPS) + ADAM_WD * w_ref[...])
        outs[2][...] = m1
        outs[3][...] = v1

    row_spec = pl.BlockSpec((tm, cols), lambda i: (i, 0))
    seg_spec = lambda s: pl.BlockSpec((tm, cols), lambda i: (jnp.clip(i - s * per, 0, per - 1), 0))
    return pl.pallas_call(
        body, name=name, grid=(rows // tm,),
        in_specs=[row_spec] * 3 + [seg_spec(s) for s in range(n_seg)] * (1 if qs is None else 2),
        out_specs=[row_spec] * 4, out_shape=[jax.ShapeDtypeStruct((rows, cols), F32)] * 4,
        compiler_params=_params(("arbitrary",)),
    )(w, m, v, *parts)


def _put_cols(shard, me):
    full = jnp.zeros((shard.shape[0], D_MODEL), F32)
    return lax.dynamic_update_slice(full, shard, (0, me * (D_MODEL // N_CHIPS)))


def _gate_vec_slot(b_a, b_x, lam):
    return _rows_at(b_a, _ROW_BA) + _rows_at(b_x, _ROW_BX) + _rows_at(lam, _ROW_LAM)


def _pack_vec(p, me):
    return jnp.concatenate([
        _rows_at(p["norm_mix_g"], 0) + _rows_at(p["norm_mlp_g"], 2) + _rows_at(p["final_g"][None], 4),
        _rows_at(_put_cols(p["rg_conv_w"][0, :, 0, :], me), 0) + _rows_at(p["rg_conv_b"], 4),
        _gate_vec_slot(_put_cols(p["rg_b_a"][0], me), _put_cols(p["rg_b_x"][0], me), _put_cols(p["rg_lam"][0], me)),
        _qk_slot(p["at_q_g"], p["at_k_g"]),
    ], axis=0)


def _unpack_vec(r, me):
    def cols(rows):
        return lax.dynamic_slice(rows, (0, me * (D_MODEL // N_CHIPS)), (rows.shape[0], D_MODEL // N_CHIPS))

    gate = r[16:24]
    return dict(
        norm_mix_g=r[0:2], norm_mlp_g=r[2:4], final_g=r[4], rg_conv_w=cols(r[8:12])[None, :, None, :],
        rg_conv_b=r[12:13], rg_b_a=cols(gate[_ROW_BA:_ROW_BA + 2])[None], rg_b_x=cols(gate[_ROW_BX:_ROW_BX + 2])[None],
        rg_lam=cols(gate[_ROW_LAM:_ROW_LAM + 2])[None], at_q_g=r[24:25, 0:HEAD_DIM],
        at_k_g=r[24:25, HEAD_DIM:2 * HEAD_DIM])


_WEIGHTS = ['norm_mix_g', 'norm_mlp_g', 'rg_w_in', 'rg_conv_w', 'rg_conv_b', 'rg_w_a', 'rg_b_a', 'rg_w_x', 'rg_b_x',
            'rg_lam', 'rg_w_out', 'at_w_qkv', 'at_q_g', 'at_k_g', 'at_w_o', 'mlp_w_up', 'mlp_w_down', 'final_g']
_BIG = dict(rg_w_in=["rg_w_in"], rg_w_out=["rg_w_out"], at_w_qkv=["at_w_qkv"], at_w_o=["at_w_o"],
            mlp_w_up=["up0", "up1"], mlp_w_down=["down0", "down1"])


def kernel(x, *args):
    n_w = len(_WEIGHTS)
    w = dict(zip(_WEIGHTS, args[:n_w]))
    target = args[n_w]
    m = dict(zip(_WEIGHTS, args[n_w + 1:2 * n_w + 1]))
    v = dict(zip(_WEIGHTS, args[2 * n_w + 1:3 * n_w + 1]))
    B, L, _ = x.shape
    T = B * L
    me = 2 * lax.axis_index("x") + lax.axis_index("y")

    vec = jnp.concatenate([_gate_vec_slot(w["rg_b_a"][0], w["rg_b_x"][0], w["rg_lam"][0]),
                           _rows_at(w["rg_conv_w"][0, :, 0, :], 0)], axis=0)
    flat = lambda a: a.reshape(-1, a.shape[-1])
    rows_of = lambda k: w[k].shape[-2]
    groups = [("rg", [("rg_w_in", 0, BF16), (vec, 0, F32)]), ("rg_out", [("rg_w_out", 0, BF16)]),
              ("mlp0_up", [("mlp_w_up", 0, BF16)]), ("mlp0_down", [("mlp_w_down", 0, BF16)]),
              ("att", [("at_w_qkv", 0, BF16), ("at_w_o", 0, BF16)]),
              ("mlp1", [("mlp_w_up", 1, BF16), ("mlp_w_down", 1, BF16)])]

    def landing_zones(group, members, after):
        lands = []
        for n, (k, layer, dtype) in enumerate(members):
            src, rows = (flat(w[k]), rows_of(k)) if isinstance(k, str) else (k, k.shape[0])
            lands.append(_cast_into_slot(src, layer * rows, rows, me, dtype, f"place_{group}{n}", after=after))
        return lands

    halves, gathers = {}, {}
    halves["rg"], tok = _exchange_start("gather_half", [], landing_zones(*groups[0], None), "gather_rg_start")
    handles, tok = _gather_start_groups([landing_zones(g, members, tok) for g, members in groups[1:]],
                                        "gather_rest_start", after=tok, kind="gather_half",
                                        barrier_id=BARRIER_IDS["gather_rest"])
    halves.update(zip([g for g, _ in groups[1:]], handles))
    wcat = _make_wcat(w["rg_w_a"], w["rg_w_x"]).astype(BF16)

    packs = [_pack_vec(p, me) for p in (w, m, v)]

    ready = {}

    def share(some, after, name):
        landed = [_exchange_wait("gather_half", halves[g], after, f"gather_{g}_landed")[1] for g in some]
        handles, _ = _gather_start_groups(landed, name, kind="share_half")
        gathers.update(zip(some, handles))

    def fetch(what, after):
        if what in ready:
            return ready[what]
        group = "mlp1" if what.startswith("mlp1") else what
        if group == "rg":
            share(["rg"], [after, wcat] + packs, "share_rg_start")
        elif group == "rg_out":
            share(["rg_out", "mlp0_up", "mlp0_down", "att"], after, "share_early_start")
        _, full = _exchange_wait("share_half", gathers[group], after, f"gather_{group}_wait")
        if group == "att":
            share(["mlp1"], after, "share_mlp1_start")
        if group == "rg":
            vec_full = jnp.transpose(full[1], (1, 0, 2)).reshape(2 * SUBLANES, D_MODEL)
            conv_wb = vec_full[SUBLANES:] + _rows_at(w["rg_conv_b"], 4)
            return full[0], conv_wb, wcat, vec_full[:SUBLANES]
        if group == "rg_out":
            return full[0].reshape(D_MODEL, D_MODEL)
        if group == "att":
            return full[0], full[1].reshape(D_MODEL, D_MODEL)
        if group == "mlp1":
            ready["mlp1_up"], ready["mlp1_down"] = full[0], full[1].reshape(4 * D_MODEL, D_MODEL)
            return ready[what]
        return full[0] if group == "mlp0_up" else full[0].reshape(4 * D_MODEL, D_MODEL)

    names = dict(mlp1=["up1", "down1"], att=["at_w_qkv", "at_w_o"], mlp0=["up0", "down0", "rg_w_out"],
                 rg_in=["rg_w_in"], gates=["rg_w_a", "rg_w_x"])
    scatters, swaps, P, Q, res = {}, [], {}, {}, {}

    def start_scatter(group, grads):
        srcs = [g.reshape(N_CHIPS, -1, g.shape[-1]) for g in grads]
        lands = [lax.empty((N_CHIPS - 1,) + s.shape[1:], s.dtype) for s in srcs]
        scatters[group], token = _exchange_start("scatter", srcs, lands, f"scatter_{group}_start",
                                                 barrier_id=BARRIER_IDS[f"scatter_{group}"])
        return token

    def settle(groups, after):
        keys, parts = [], []
        for group in groups:
            srcs, lands = _exchange_wait("scatter", scatters[group], after, f"scatter_{group}_wait")
            for k, s, r in zip(names[group], srcs, lands):
                keys.append(k)
                parts.append(_sum_slots(s, r, me, f"sum_{k}"))
        handle, token = _exchange_start("swap", parts, [lax.empty(p.shape, F32) for p in parts],
                                        f"swap_{groups[0]}_start")
        swaps.append((keys, handle, f"swap_{groups[0]}_wait"))
        return token

    def finish(after):
        for keys, handle, name in swaps:
            mine, theirs = _exchange_wait("swap", handle, after, name)
            P.update(zip(keys, mine))
            Q.update(zip(keys, theirs))
        swaps.clear()
        last = after
        for k, parts in _BIG.items():
            if k in res or any(p not in P for p in parts):
                continue
            shape = w[k].shape
            two_d = lambda a: a.reshape(-1, shape[-1])
            outs = _adamw(two_d(w[k]), two_d(m[k]), two_d(v[k]), [P[p] for p in parts], [Q[p] for p in parts],
                          f"adamw_{k}")
            res[k] = [o.reshape(shape) for o in outs]
            last = outs[0]
        if "rg_w_a" in P and "gates" not in gathers:
            lands = [_cast_into_slot(P[k], 0, P[k].shape[0], me, F32, f"place_{k}", after=last, add=Q[k])
                     for k in names["gates"]]
            gathers["gates"], last = _exchange_start("gather", [], lands, "gather_gates_start", after=last,
                                                     barrier_id=BARRIER_IDS["gather_gates"])
        return last

    def emit(event, arrays):
        if event == "point_attn_done":
            return None
        if event == "point_mix_done":
            return settle(["mlp1", "att", "mlp0"], arrays[0])
        token = start_scatter(event, arrays)
        if event == "rg_in":
            return finish(settle(["gates"], token))
        return token

    P_vec = dict(norm_mix_g=w["norm_mix_g"], norm_mlp_g=w["norm_mlp_g"], final_g=w["final_g"][None],
                 q_g=w["at_q_g"], k_g=w["at_k_g"])
    grad_x, vec_part = _local_step(x.reshape(T, D_MODEL), target.reshape(T, D_MODEL), P_vec, fetch, emit, B, L,
                                   after=tok)

    me8 = 2 * me + lax.axis_index("c")
    vec_slots = _cast_into_slot(vec_part, 0, VEC_ROWS, me8, F32, "place_vec", n_slots=N_DEVICES)
    spread, tok = _exchange_start("spread", [], [vec_slots], "spread_vec_start", barrier_id=BARRIER_IDS["spread_vec"])
    last = finish(settle(["rg_in"], tok))
    _, gate_grads = _exchange_wait("gather", gathers["gates"], last, "gather_gates_wait")
    for k, g in zip(names["gates"], gate_grads):
        two_d = lambda a: a.reshape(g.shape[0] * g.shape[1], g.shape[2])
        outs = _adamw(two_d(w[k]), two_d(m[k]), two_d(v[k]), [two_d(g)], None, f"adamw_{k}")
        res[k] = [o.reshape(w[k].shape) for o in outs]
        last = outs[0]
    _, (vec_all,) = _exchange_wait("spread", spread, last, "spread_vec_wait")
    vec_grad = _sum_leading(vec_all, "sum_vec")
    loss = vec_grad[LOSS_ROW, 0]
    outs = _adamw(*packs, [vec_grad], None, "adamw_vec")
    unpacked = [_unpack_vec(o, me) for o in outs]
    for k in _WEIGHTS:
        if k not in res:
            res[k] = [u[k] for u in unpacked]

    result = [loss, grad_x.reshape(B, L, D_MODEL)]
    for slot in range(4):
        result += [res[k][slot] for k in _WEIGHTS]
    return tuple(result)
```

```python
import functools
import math

import jax
import jax.numpy as jnp
import numpy as np
from jax import lax
from jax.experimental import pallas as pl
from jax.experimental.pallas import tpu as pltpu

F32 = jnp.float32
BF16 = jnp.bfloat16

D_MODEL = 1024
HEAD_DIM = 128
N_HEADS = 8
N_KV = 2
GROUP = N_HEADS // N_KV
LRU_BLOCKS = 8
LRU_BW = 128
GRID_W = 64
ROPE_THETA = 10000.0
EPS = 1e-6
RG_C = 8.0
SCALE = 1.0 / math.sqrt(HEAD_DIM)
N_CHIPS = 4

ADAM_LR = 0.001
ADAM_B1 = 0.9
ADAM_B2 = 0.999
ADAM_EPS = 1e-08
ADAM_WD = 0.01
ADAM_STEP = 10

V7X_VMEM_BYTES = 64 * 1024 * 1024
VMEM_LIMIT = V7X_VMEM_BYTES * 3 // 4
LANES = 128
SUBLANES = 8

N_DEVICES = 8
VEC_ROWS = 32
LOSS_ROW = 5


def _params(sem):
    return pltpu.CompilerParams(dimension_semantics=sem, vmem_limit_bytes=VMEM_LIMIT)


_ANY = pl.BlockSpec(memory_space=pl.ANY)
_NN = (((1,), (0,)), ((), ()))
_NT = (((1,), (1,)), ((), ()))
_TN = (((0,), (0,)), ((), ()))


def _after_operand(after):
    return [] if after is None else [after]


def _fit(t, n):
    if n <= t:
        return n
    c = (t // LANES) * LANES
    while n % c:
        c -= LANES
    return c


MM_VMEM_BUDGET = VMEM_LIMIT * 3 // 4
def _mm_tiles(M, K, ns, n_total, out_dtypes, extras, whole_rows):
    for tm in (2048, 1024, 512, 256, 128):
        for tn in ((ns,) if whole_rows else (1024, 512, 256)):
            tn = _fit(tn, ns)
            per_row = 2 * (2 * K) + 4 * tn + sum(2 * tn * jnp.dtype(d).itemsize for d in out_dtypes)
            per_row += sum(2 * tn * e.dtype.itemsize for e in extras)
            b_buffers = 1 if tn == n_total else 2
            if M % tm == 0 and b_buffers * (2 * K * tn) + tm * per_row <= MM_VMEM_BUDGET:
                return tm, tn
    raise ValueError(f"no tile fits VMEM for M={M} K={K} N={ns}")


def _mm(a, b, *, mode, name, out_dtypes=(F32,), b_shard=False, o_shard=False, extras=(), epi=None, after=None,
        bcast=(), accs=(), ref_epi=None, out_cols=None):
    if mode == "tn":
        K, M = a.shape
        N = b.shape[1]
    else:
        M, K = a.shape
        if mode == "nn":
            N = b.shape[0] * b.shape[2] if b_shard else b.shape[1]
        else:
            N = b.shape[1] if b_shard else b.shape[0]
    ns = N
    if b_shard and mode == "nn":
        ns = b.shape[2]
    elif o_shard:
        ns = N // N_CHIPS
    tm, tn = _mm_tiles(M, K, ns, N, out_dtypes, extras, whole_rows=ref_epi is not None)
    if ref_epi is not None:
        tm = min(tm, 512)
    grid = (M // tm, N // tn)
    q = ns // tn
    once = dict(pipeline_mode=pl.Buffered(1)) if tn == N else {}

    if mode == "tn":
        a_spec = pl.BlockSpec((K, tm), lambda i, j: (0, i))
        b_spec = pl.BlockSpec((K, tn), lambda i, j: (0, j), **once)
        dims = _TN
    elif mode == "nn":
        a_spec = pl.BlockSpec((tm, K), lambda i, j: (i, 0))
        if b_shard:
            b_spec = pl.BlockSpec((None, K, tn), lambda i, j: (j // q, 0, j % q), **once)
        else:
            b_spec = pl.BlockSpec((K, tn), lambda i, j: (0, j), **once)
        dims = _NN
    else:
        a_spec = pl.BlockSpec((tm, K), lambda i, j: (i, 0))
        if b_shard:
            ks = b.shape[2]
            b_spec = pl.BlockSpec((N_CHIPS, tn, ks), lambda i, j: (0, j, 0), **once)
        else:
            b_spec = pl.BlockSpec((tn, K), lambda i, j: (j, 0), **once)
        dims = _NT

    if o_shard:
        o_specs = [pl.BlockSpec((None, tm, tn), lambda i, j: (j // q, i, j % q))]
        o_shapes = [jax.ShapeDtypeStruct((N_CHIPS, M, ns), out_dtypes[0])]
    else:
        o_specs = [pl.BlockSpec((tm, tn), lambda i, j: (i, j)) for _ in out_dtypes]
        o_shapes = [jax.ShapeDtypeStruct((M, N if out_cols is None else out_cols[n]), dt)
                    for n, dt in enumerate(out_dtypes)]
    e_specs = [pl.BlockSpec((tm, tn), lambda i, j: (i, j)) for _ in extras]
    e_specs += [pl.BlockSpec(v.shape, lambda i, j: (0, 0)) for v in bcast]
    o_specs += [pl.BlockSpec(s, lambda i, j: (0, 0)) for s in accs]
    o_shapes += [jax.ShapeDtypeStruct(s, F32) for s in accs]
    n_e, n_b, n_o, n_a = len(extras), len(bcast), len(out_dtypes), len(accs)
    order = _after_operand(after)
    n_x = len(order)
    if epi is None:
        epi = lambda acc: (acc,)

    def body(a_ref, b_ref, *rest):
        e_refs, b_refs = rest[:n_e], rest[n_e:n_e + n_b]
        o_refs = rest[n_e + n_b + n_x:n_e + n_b + n_x + n_o]
        a_refs = rest[n_e + n_b + n_x + n_o:]
        if n_a:
            @pl.when((pl.program_id(0) == 0) & (pl.program_id(1) == 0))
            def _():
                for r in a_refs:
                    r[...] = jnp.zeros(r.shape, F32)
        if mode == "nt" and b_shard:
            acc = None
            for s in range(N_CHIPS):
                part = lax.dot_general(a_ref[:, s * ks:(s + 1) * ks], b_ref[s], dims, preferred_element_type=F32)
                acc = part if acc is None else acc + part
        else:
            acc = lax.dot_general(a_ref[...], b_ref[...], dims, preferred_element_type=F32)
        if ref_epi is not None:
            ref_epi(acc, e_refs, b_refs, o_refs, a_refs)
            return
        outs = epi(acc, *[r[...] for r in e_refs])
        for r, o in zip(o_refs, outs):
            r[...] = o.astype(r.dtype)

    outs = pl.pallas_call(
        body, name=name, grid=grid, in_specs=[a_spec, b_spec] + e_specs + [_ANY] * n_x, out_specs=o_specs,
        out_shape=o_shapes, compiler_params=_params(("arbitrary", "arbitrary") if n_a else ("parallel", "parallel")),
    )(a, b, *extras, *bcast, *order)
    return outs[0] if n_o + n_a == 1 else outs


def _rowwise(fn, rows, bcast, outs, accs=(), *, tm, name, after=None):
    def norm(r):
        return r if isinstance(r, tuple) else (r, r.shape[1], 0)

    rows = [norm(r) for r in rows]
    T = rows[0][0].shape[0]
    tm = min(tm, T)
    while T % tm:
        tm -= SUBLANES
    n_r, n_b, n_o, n_a = len(rows), len(bcast), len(outs), len(accs)
    order = _after_operand(after)
    n_x = len(order)
    in_specs = [pl.BlockSpec((tm, c), functools.partial(lambda i, cb: (i, cb), cb=cb)) for _, c, cb in rows]
    in_specs += [pl.BlockSpec(b.shape, lambda i: (0, 0)) for b in bcast] + [_ANY] * n_x
    out_specs = [pl.BlockSpec((tm, o[0]), lambda i: (i, 0)) for o in outs]
    out_specs += [pl.BlockSpec(s, lambda i: (0, 0)) for s in accs]
    out_shape = [jax.ShapeDtypeStruct((T, o[2] if len(o) > 2 else o[0]), o[1]) for o in outs]
    out_shape += [jax.ShapeDtypeStruct(s, F32) for s in accs]

    def body(*refs):
        in_refs = refs[:n_r]
        b_refs = refs[n_r:n_r + n_b]
        o_refs = refs[n_r + n_b + n_x:n_r + n_b + n_x + n_o]
        a_refs = refs[n_r + n_b + n_x + n_o:]
        if n_a:
            @pl.when(pl.program_id(0) == 0)
            def _():
                for r in a_refs:
                    r[...] = jnp.zeros(r.shape, F32)
        fn(in_refs, b_refs, o_refs, a_refs)

    res = pl.pallas_call(
        body, name=name, grid=(T // tm,), in_specs=in_specs, out_specs=out_specs, out_shape=out_shape,
        compiler_params=_params(("arbitrary",) if n_a else ("parallel",)),
    )(*[r[0] for r in rows], *bcast, *order)
    return res


def _rsum(x):
    return jnp.sum(x, axis=0, keepdims=True)


def _rms_fwd(x, g, name, after=None):
    def fn(ins, bs, outs, accs):
        xv = ins[0][...]
        r = lax.rsqrt(jnp.mean(xv * xv, axis=-1, keepdims=True) + EPS)
        outs[0][...] = (xv * r * bs[0][...]).astype(BF16)

    return _rowwise(fn, [x], [g], [(D_MODEL, BF16)], tm=512, name=name, after=after)[0]


def _rms_bwd_math(xv, dh, g):
    r = lax.rsqrt(jnp.mean(xv * xv, axis=-1, keepdims=True) + EPS)
    hn = xv * r
    dgh = dh * g
    dx = r * (dgh - hn * jnp.mean(dgh * hn, axis=-1, keepdims=True))
    return dx, _rsum(dh * hn)


def _mm_norm_bwd(dy, w, x, dres, g, name, after=None):
    def epilogue(acc, e_refs, b_refs, o_refs, a_refs):
        dx, dg = _rms_bwd_math(e_refs[0][...], acc, b_refs[0][...])
        dx = dx + e_refs[1][...]
        o_refs[0][...] = dx
        o_refs[1][...] = dx.astype(BF16)
        a_refs[0][...] += dg

    return _mm(dy, w, mode="nt", b_shard=True, out_dtypes=(F32, BF16), extras=(x, dres), bcast=(g,),
               accs=((1, D_MODEL),), ref_epi=epilogue, name=name, after=after)


def _mm_res_norm(a, w, res, g, name):
    def epilogue(acc, e_refs, b_refs, o_refs, a_refs):
        xv = acc + e_refs[0][...]
        o_refs[0][...] = xv
        r = lax.rsqrt(jnp.mean(xv * xv, axis=-1, keepdims=True) + EPS)
        o_refs[1][...] = (xv * r * b_refs[0][...]).astype(BF16)

    return _mm(a, w, mode="nn", out_dtypes=(F32, BF16), extras=(res,), bcast=(g,), ref_epi=epilogue, name=name)


def _mm_final_loss(a, w, res, target, g, name):
    def epilogue(acc, e_refs, b_refs, o_refs, a_refs):
        xv = acc + e_refs[0][...]
        gv = b_refs[0][...]
        r = lax.rsqrt(jnp.mean(xv * xv, axis=-1, keepdims=True) + EPS)
        e = xv * r * gv - e_refs[1][...]
        tok = jnp.mean(e * e, axis=-1, keepdims=True)
        a_refs[0][...] += 0.5 * jnp.sum(tok, axis=0, keepdims=True) * jnp.ones((1, LANES), F32)
        dx, dg = _rms_bwd_math(xv, e * (1.0 / D_MODEL), gv)
        o_refs[0][...] = dx
        o_refs[1][...] = dx.astype(BF16)
        a_refs[1][...] += dg

    return _mm(a, w, mode="nn", out_dtypes=(F32, BF16), extras=(res, target), bcast=(g,),
               accs=((1, LANES), (1, D_MODEL)), ref_epi=epilogue, name=name)


def _relu2(acc):
    r = jnp.maximum(acc, 0.0)
    return r * r, r


def _mlp_fwd(x, h, fetch, tag, finish):
    w_up = fetch(f"mlp{tag}_up", h)
    a, r = _mm(h, w_up, mode="nn", b_shard=True, out_dtypes=(BF16, BF16), epi=_relu2, name=f"mlp{tag}_up")
    w_down = fetch(f"mlp{tag}_down", a)
    return finish(a, w_down, x, f"mlp{tag}_down"), (h, a, r, w_up, w_down)


def _mlp_bwd(x, g, saved, dx, dx_bf, tag, after):
    h, a, r, w_up, w_down = saved
    d_down = _mm(a, dx_bf, mode="tn", out_dtypes=(BF16,), name=f"mlp{tag}_dwdown", after=after)
    dup = _mm(dx_bf, w_down, mode="nt", extras=(r,), out_dtypes=(BF16,),
              epi=lambda acc, rv: (acc * (2.0 * rv.astype(F32)),), name=f"mlp{tag}_dup")
    d_up = _mm(h, dup, mode="tn", o_shard=True, out_dtypes=(BF16,), name=f"mlp{tag}_dwup")
    dx_new, dx_new_bf, dg = _mm_norm_bwd(dup, w_up, x, dx, g, f"mlp{tag}_dh")
    return dx_new, dx_new_bf, dg, d_up, d_down


def _rope_tables(L, B):
    rows = L // GRID_W
    row = np.repeat(np.arange(rows, dtype=np.float32), GRID_W)
    col = np.tile(np.arange(GRID_W, dtype=np.float32), rows)
    inv = (ROPE_THETA ** (-np.arange(HEAD_DIM // 4, dtype=np.float32) / (HEAD_DIM // 4))).astype(np.float32)
    ar, ac = row[:, None] * inv, col[:, None] * inv
    cos = np.concatenate([np.cos(ar), np.cos(ar), np.cos(ac), np.cos(ac)], axis=-1)
    sin = np.concatenate([-np.sin(ar), np.sin(ar), -np.sin(ac), np.sin(ac)], axis=-1)
    return jnp.asarray(np.tile(cos, (B, 1)), F32), jnp.asarray(np.tile(sin, (B, 1)), F32)


def _swap_halves(x):
    lane = lax.broadcasted_iota(jnp.int32, x.shape, 1)
    return jnp.where((lane % 64) < 32, pltpu.roll(x, HEAD_DIM - 32, 1), pltpu.roll(x, 32, 1))


def _qk_prep(qkv, cos, sin, q_g, k_g):
    def fn(ins, bs, outs, accs):
        c, s = ins[1][...], ins[2][...]
        for h in range(N_HEADS + N_KV):
            xv = ins[0][:, h * HEAD_DIM:(h + 1) * HEAD_DIM]
            g = bs[0][...] if h < N_HEADS else bs[1][...]
            r = lax.rsqrt(jnp.mean(xv * xv, axis=-1, keepdims=True) + EPS)
            z = xv * r * g
            y = (z * c + _swap_halves(z) * s).astype(BF16)
            if h < N_HEADS:
                outs[0][:, h * HEAD_DIM:(h + 1) * HEAD_DIM] = y
            else:
                outs[1][:, (h - N_HEADS) * HEAD_DIM:(h - N_HEADS + 1) * HEAD_DIM] = y
        outs[2][...] = ins[0][:, (N_HEADS + N_KV) * HEAD_DIM:].astype(BF16)

    kvw = N_KV * HEAD_DIM
    return _rowwise(fn, [qkv, cos, sin], [q_g, k_g], [(D_MODEL, BF16), (kvw, BF16), (kvw, BF16)], tm=512,
                    name="attn_qk_prep")


def _qk_prep_bwd(qkv, dq, dk, dv, cos, sin, q_g, k_g):
    def fn(ins, bs, outs, accs):
        c, s = ins[4][...], ins[5][...]
        for h in range(N_HEADS + N_KV):
            sl = slice(h * HEAD_DIM, (h + 1) * HEAD_DIM)
            xv = ins[0][:, sl]
            if h < N_HEADS:
                g, dy, acc = bs[0][...], ins[1][:, sl], accs[0]
            else:
                ks = slice((h - N_HEADS) * HEAD_DIM, (h - N_HEADS + 1) * HEAD_DIM)
                g, dy, acc = bs[1][...], ins[2][:, ks], accs[1]
            r = lax.rsqrt(jnp.mean(xv * xv, axis=-1, keepdims=True) + EPS)
            xn = xv * r
            dz = dy * c - _swap_halves(dy) * s
            acc[...] += _rsum(dz * xn)
            dxn = dz * g
            outs[0][:, sl] = (r * (dxn - xn * jnp.mean(dxn * xn, axis=-1, keepdims=True))).astype(BF16)
        outs[0][:, (N_HEADS + N_KV) * HEAD_DIM:] = ins[3][...].astype(BF16)

    return _rowwise(fn, [qkv, dq, dk, dv, cos, sin], [q_g, k_g], [(qkv.shape[1], BF16)],
                    [(1, HEAD_DIM), (1, HEAD_DIM)], tm=512, name="attn_qk_prep_bwd")


_EXP2_SCALE = SCALE * math.log2(math.e)


def _exp_rows(q, k):
    s = lax.dot_general(q, k, _NT, preferred_element_type=F32)
    p = jnp.exp2((s - jnp.max(s, axis=-1, keepdims=True)) * _EXP2_SCALE)
    return p, jnp.sum(p, axis=-1, keepdims=True)


def _attn_fwd(q, k, v, B, L, tq=2048, sub=256):
    tq = min(tq, L)
    sub = min(sub, tq)
    nq = L // tq

    def body(q_ref, k_ref, v_ref, o_ref):
        kv, vv = k_ref[...], v_ref[...]
        for c in range(tq // sub):
            rows = slice(c * sub, (c + 1) * sub)
            p, l = _exp_rows(q_ref[rows, :], kv)
            o = jnp.dot(p.astype(BF16), vv, preferred_element_type=F32)
            o_ref[rows, :] = (o * (1.0 / l)).astype(o_ref.dtype)

    return pl.pallas_call(
        body, name="attn_fwd", grid=(B, N_HEADS, nq),
        in_specs=[pl.BlockSpec((tq, HEAD_DIM), lambda b, h, i: (b * nq + i, h)),
                  pl.BlockSpec((L, HEAD_DIM), lambda b, h, i: (b, h // GROUP)),
                  pl.BlockSpec((L, HEAD_DIM), lambda b, h, i: (b, h // GROUP))],
        out_specs=pl.BlockSpec((tq, HEAD_DIM), lambda b, h, i: (b * nq + i, h)),
        out_shape=jax.ShapeDtypeStruct((B * L, D_MODEL), BF16),
        compiler_params=_params(("parallel", "parallel", "parallel")),
    )(q, k, v)


def _attn_bwd(q, k, v, o, do, B, L, tq=2048, sub=512):
    tq = min(tq, L)
    sub = min(sub, tq)
    nq = L // tq

    def body(q_ref, k_ref, v_ref, o_ref, do_ref, dq_ref, dk_ref, dv_ref):
        @pl.when((pl.program_id(2) == 0) & (pl.program_id(3) == 0))
        def _():
            dk_ref[...] = jnp.zeros(dk_ref.shape, F32)
            dv_ref[...] = jnp.zeros(dv_ref.shape, F32)

        kv, vv = k_ref[...], v_ref[...]
        ps, es, dos, qs = [], [], [], []
        for c in range(tq // sub):
            rows = slice(c * sub, (c + 1) * sub)
            qc, doc = q_ref[rows, :], do_ref[rows, :]
            p, l = _exp_rows(qc, kv)
            inv = 1.0 / l
            dp = lax.dot_general(doc, vv, _NT, preferred_element_type=F32)
            delta = jnp.sum(doc.astype(F32) * o_ref[rows, :].astype(F32), axis=-1, keepdims=True)
            e = (p * (dp - delta)).astype(BF16)
            dq_ref[rows, :] = jnp.dot(e, kv, preferred_element_type=F32) * (inv * SCALE)
            ps.append(p.astype(BF16))
            es.append(e)
            dos.append((doc.astype(F32) * inv).astype(BF16))
            qs.append((qc.astype(F32) * (inv * SCALE)).astype(BF16))
        cat = lambda xs: xs[0] if len(xs) == 1 else jnp.concatenate(xs, axis=0)
        dv_ref[...] += lax.dot_general(cat(ps), cat(dos), _TN, preferred_element_type=F32)
        dk_ref[...] += lax.dot_general(cat(es), cat(qs), _TN, preferred_element_type=F32)

    qmap = lambda b, kh, g, i: (b * nq + i, kh * GROUP + g)
    kmap = lambda b, kh, g, i: (b, kh)
    kvw = N_KV * HEAD_DIM
    return pl.pallas_call(
        body, name="attn_bwd", grid=(B, N_KV, GROUP, nq),
        in_specs=[pl.BlockSpec((tq, HEAD_DIM), qmap), pl.BlockSpec((L, HEAD_DIM), kmap),
                  pl.BlockSpec((L, HEAD_DIM), kmap), pl.BlockSpec((tq, HEAD_DIM), qmap),
                  pl.BlockSpec((tq, HEAD_DIM), qmap)],
        out_specs=[pl.BlockSpec((tq, HEAD_DIM), qmap), pl.BlockSpec((L, HEAD_DIM), kmap),
                   pl.BlockSpec((L, HEAD_DIM), kmap)],
        out_shape=[jax.ShapeDtypeStruct((B * L, D_MODEL), F32), jax.ShapeDtypeStruct((B * L, kvw), F32),
                   jax.ShapeDtypeStruct((B * L, kvw), F32)],
        compiler_params=_params(("parallel", "parallel", "arbitrary", "arbitrary")),
    )(q, k, v, o, do)


def _conv_shift(x, t, L, k):
    if k == 2:
        return x
    if k < 2:
        return jnp.where(t >= 2 - k, pltpu.roll(x, 2 - k, 0), 0.0)
    return jnp.where(t < L - (k - 2), pltpu.roll(x, L - (k - 2), 0), 0.0)


def _conv_apply(x, w_ref, L):
    t = lax.broadcasted_iota(jnp.int32, x.shape, 0)
    acc = w_ref[4:5, :] + w_ref[2:3, :] * x
    for k in (0, 1, 3):
        acc = acc + w_ref[k:k + 1, :] * _conv_shift(x, t, L, k)
    return acc


def _conv_bwd(z, g, wb, dz, B, L, tc=256, after=None):
    noff = D_MODEL // tc
    order = _after_operand(after)

    def body(z_ref, g_ref, w_ref, dz_in, *rest):
        dx_ref, dw_ref = rest[len(order):]

        @pl.when(pl.program_id(1) == 0)
        def _():
            dw_ref[...] = jnp.zeros(dw_ref.shape, F32)

        x, gv = z_ref[...], g_ref[...]
        t = lax.broadcasted_iota(jnp.int32, x.shape, 0)
        dx = w_ref[2:3, :] * gv
        for k in (0, 1, 3):
            dx = dx + w_ref[k:k + 1, :] * _conv_shift(gv, t, L, 4 - k)
        dx_ref[...] = dx.astype(BF16)
        for k in range(4):
            dw_ref[k:k + 1, :] += _rsum(_conv_shift(x, t, L, k) * gv)
        dw_ref[4:5, :] += _rsum(gv)

    return pl.pallas_call(
        body, name="rg_conv_bwd", grid=(noff, B),
        in_specs=[pl.BlockSpec((L, tc), lambda j, b: (b, noff + j)), pl.BlockSpec((L, tc), lambda j, b: (b, j)),
                  pl.BlockSpec((SUBLANES, tc), lambda j, b: (0, j)), _ANY] + [_ANY] * len(order),
        out_specs=[pl.BlockSpec((L, tc), lambda j, b: (b, noff + j)),
                   pl.BlockSpec((SUBLANES, tc), lambda j, b: (0, j))],
        out_shape=[jax.ShapeDtypeStruct(dz.shape, dz.dtype), jax.ShapeDtypeStruct((SUBLANES, D_MODEL), F32)],
        input_output_aliases={3: 0},
        compiler_params=_params(("parallel", "arbitrary")),
    )(z, g, wb, dz, *order)


def _softplus(x):
    return jnp.maximum(x, 0.0) + jnp.log1p(jnp.exp(-jnp.abs(x)))


_ROW_BA, _ROW_BX, _ROW_LAM = 0, 2, 4


def _gate_math(xb, pre, vec_ref, d, sl):
    pa = pre[:, (2 * d) * LRU_BW:(2 * d + 1) * LRU_BW] + vec_ref[_ROW_BA + d:_ROW_BA + d + 1, sl]
    px = pre[:, (2 * d + 1) * LRU_BW:(2 * d + 2) * LRU_BW] + vec_ref[_ROW_BX + d:_ROW_BX + d + 1, sl]
    r = 0.5 * jnp.tanh(0.5 * pa) + 0.5
    i = 0.5 * jnp.tanh(0.5 * px) + 0.5
    slope = (-RG_C) * _softplus(-vec_ref[_ROW_LAM + d:_ROW_LAM + d + 1, sl])
    log_a = r * slope
    a = jnp.exp(log_a)
    om = -jnp.tanh(log_a) * (1.0 + a * a)
    rs = lax.rsqrt(om)
    mult = jnp.where(om > 0.0, om * rs, 0.0)
    return a, mult * (i * xb), (r, i, slope, om, mult, rs)


def _gate_bwd(rec, du_f, da_f, du_b, da_b, wcat, gvec):
    def fn(ins, bs, outs, accs):
        for blk in range(LRU_BLOCKS):
            sl = slice(blk * LRU_BW, (blk + 1) * LRU_BW)
            xb = ins[0][:, sl]
            xb16 = xb.astype(BF16)
            w = bs[0][sl, :]
            pre = jnp.dot(xb16, w, preferred_element_type=F32)
            dx = jnp.zeros_like(xb)
            dpre = []
            for d in range(2):
                a, _, (r, i, slope, om, mult, rs) = _gate_math(xb, pre, bs[1], d, sl)
                du, da = ins[1 + 2 * d][:, sl], ins[2 + 2 * d][:, sl]
                t = du * xb
                d_i = t * mult
                dx = dx + du * mult * i
                dlog = da * a - (t * i) * ((1.0 - om) * rs)
                d_r = dlog * slope
                d_sp = _rsum(dlog * r) * (-RG_C)
                lam = bs[1][_ROW_LAM + d:_ROW_LAM + d + 1, sl]
                accs[2][_ROW_LAM + d:_ROW_LAM + d + 1, sl] += d_sp * (-jax.nn.sigmoid(-lam))
                dpa = d_r * r * (1.0 - r)
                dpx = d_i * i * (1.0 - i)
                accs[2][_ROW_BA + d:_ROW_BA + d + 1, sl] += _rsum(dpa)
                accs[2][_ROW_BX + d:_ROW_BX + d + 1, sl] += _rsum(dpx)
                dpre += [dpa, dpx]
            dpre = jnp.concatenate(dpre, axis=1).astype(BF16)
            dw = lax.dot_general(xb16, dpre, _TN, preferred_element_type=F32)
            for d in range(2):
                rows = slice(d * D_MODEL + blk * LRU_BW, d * D_MODEL + (blk + 1) * LRU_BW)
                accs[0][rows, :] += dw[:, (2 * d) * LRU_BW:(2 * d + 1) * LRU_BW]
                accs[1][rows, :] += dw[:, (2 * d + 1) * LRU_BW:(2 * d + 2) * LRU_BW]
            outs[0][:, sl] = dx + lax.dot_general(dpre, w, _NT, preferred_element_type=F32)

    gate_shape = (2 * D_MODEL, LRU_BW)
    return _rowwise(fn, [rec, du_f, da_f, du_b, da_b], [wcat, gvec], [(D_MODEL, F32)],
                    [gate_shape, gate_shape, (SUBLANES, D_MODEL)], tm=512, name="rg_gate_bwd")


def _as_time_blocks(x):
    return x.reshape(x.shape[0] // SUBLANES, SUBLANES, x.shape[1])


def _scan_call(body, ins, n_out, B, L, tc, name):
    nb = L // SUBLANES
    spec = pl.BlockSpec((nb, SUBLANES, tc), lambda b, j: (b, 0, j))
    T = ins[0].shape[0]
    outs = pl.pallas_call(
        functools.partial(body, nb), name=name, grid=(B, D_MODEL // tc),
        in_specs=[spec] * len(ins), out_specs=[spec] * n_out,
        out_shape=[jax.ShapeDtypeStruct((T // SUBLANES, SUBLANES, D_MODEL), F32)] * n_out,
        compiler_params=_params(("parallel", "parallel")),
    )(*[_as_time_blocks(x) for x in ins])
    return [o.reshape(T, D_MODEL) for o in outs]


def _block_scan(A, U, reverse):
    row = lax.broadcasted_iota(jnp.int32, A.shape, 0)
    for s in (1, 2, 4):
        shift = SUBLANES - s if reverse else s
        valid = (row < SUBLANES - s) if reverse else (row >= s)
        a_sh = jnp.where(valid, pltpu.roll(A, shift, 0), 1.0)
        u_sh = jnp.where(valid, pltpu.roll(U, shift, 0), 0.0)
        U = A * u_sh + U
        A = A * a_sh
    return A, U


_LAST = SUBLANES - 1
SCAN_UNROLL = 8


def _loop_blocks(nb, step, init):
    def group(g, carry):
        for k in range(SCAN_UNROLL):
            carry = step(g * SCAN_UNROLL + k, carry)
        return carry

    return lax.fori_loop(0, nb // SCAN_UNROLL, group, init)


def _scan_bwd(dy, a_f, h_f, a_b, h_b, B, L, tc=256):
    def body(nb, dy_r, af, hf, ab, hb, duf, daf, dub, dab):
        def step(i, carry):
            c1, c2 = carry
            ir = nb - 1 - i
            row = lax.broadcasted_iota(jnp.int32, (SUBLANES, tc), 0)
            a_up = jnp.where(row == _LAST, af[jnp.minimum(ir + 1, nb - 1), :1, :], pltpu.roll(af[ir], _LAST, 0))
            p, lam = _block_scan(a_up, dy_r[ir], True)
            lam = lam + p * c1
            before = hf[jnp.maximum(ir - 1, 0), _LAST:, :] * (ir > 0).astype(F32)
            duf[ir] = lam
            daf[ir] = lam * jnp.where(row == 0, before, pltpu.roll(hf[ir], 1, 0))
            a_dn = jnp.where(row == 0, ab[jnp.maximum(i - 1, 0), _LAST:, :], pltpu.roll(ab[i], 1, 0))
            p2, lam2 = _block_scan(a_dn, dy_r[i], False)
            lam2 = lam2 + p2 * c2
            after = hb[jnp.minimum(i + 1, nb - 1), :1, :] * (i < nb - 1).astype(F32)
            dub[i] = lam2
            dab[i] = lam2 * jnp.where(row == _LAST, after, pltpu.roll(hb[i], _LAST, 0))
            return lam[:1, :], lam2[_LAST:, :]

        zero = jnp.zeros((1, tc), F32)
        _loop_blocks(nb, step, (zero, zero))

    return _scan_call(body, [dy, a_f, h_f, a_b, h_b], 4, B, L, tc, "rg_scan_bwd")


_GELU_C = math.sqrt(2.0 / math.pi)


def _gelu_parts(x):
    th = jnp.tanh(_GELU_C * (x + 0.044715 * x * x * x))
    return 0.5 * x * (1.0 + th), th


def _mm_gated_out_bwd(dx, w_out, h_f, h_b, z, name, after=None):
    def epilogue(acc, e_refs, b_refs, o_refs, a_refs):
        x = e_refs[2][...]
        gl, th = _gelu_parts(x)
        dgl = 0.5 * (1.0 + th) + 0.5 * x * (1.0 - th * th) * (_GELU_C * (1.0 + 3.0 * 0.044715 * x * x))
        o_refs[0][...] = acc * gl
        o_refs[1][...] = (acc * (e_refs[0][...] + e_refs[1][...]) * dgl).astype(BF16)

    return _mm(dx, w_out, mode="nt", out_dtypes=(F32, BF16), out_cols=(D_MODEL, 2 * D_MODEL), extras=(h_f, h_b, z),
               ref_epi=epilogue, name=name, after=after)


def _row_block(i):
    return pl.ds(pl.multiple_of(i * SUBLANES, SUBLANES), SUBLANES)


def _rg_mix_fwd(z, conv_wb, wcat, gvec, B, L):
    nb = L // SUBLANES
    n_g = D_MODEL // LRU_BW

    def body(zg_ref, zr_ref, cw_ref, w_ref, gv_ref, rec_ref, af_s, ab_s, hf_ref, hb_ref, yg_ref, uf_s, ub_s):
        rec = _conv_apply(zr_ref[...], cw_ref, L)
        rec_ref[...] = rec
        pre = jnp.dot(rec.astype(BF16), w_ref[...], preferred_element_type=F32)
        for d, (a_s, u_s) in enumerate(((af_s, uf_s), (ab_s, ub_s))):
            a, u, _ = _gate_math(rec, pre, gv_ref, d, slice(None))
            a_s[...] = a
            u_s[...] = u

        def step(i, carry):
            c1, c2 = carry
            rows, rows_b = _row_block(i), _row_block(nb - 1 - i)
            p, h = _block_scan(af_s[rows, :], uf_s[rows, :], False)
            h = h + p * c1
            hf_ref[rows, :] = h
            p2, h2 = _block_scan(ab_s[rows_b, :], ub_s[rows_b, :], True)
            h2 = h2 + p2 * c2
            hb_ref[rows_b, :] = h2
            return h[_LAST:, :], h2[:1, :]

        zero = jnp.zeros((1, LRU_BW), F32)
        _loop_blocks(nb, step, (zero, zero))
        gl, _ = _gelu_parts(zg_ref[...])
        yg_ref[...] = ((hf_ref[...] + hb_ref[...]) * gl).astype(BF16)

    seq = lambda off: pl.BlockSpec((L, LRU_BW), lambda b, g: (b, off + g))
    vec = pl.BlockSpec((SUBLANES, LRU_BW), lambda b, g: (0, g))
    T = B * L
    return pl.pallas_call(
        body, name="rg_mix", grid=(B, n_g),
        in_specs=[seq(0), seq(n_g), vec, pl.BlockSpec((LRU_BW, 4 * LRU_BW), lambda b, g: (g, 0)), vec],
        out_specs=[seq(0)] * 6,
        out_shape=[jax.ShapeDtypeStruct((T, D_MODEL), F32)] * 5 + [jax.ShapeDtypeStruct((T, D_MODEL), BF16)],
        scratch_shapes=[pltpu.VMEM((L, LRU_BW), F32)] * 2,
        compiler_params=_params(("parallel", "parallel")),
    )(z, z, conv_wb, wcat, gvec)


def _make_wcat(w_a, w_x):
    g = jnp.stack([w_a[0, 0], w_x[0, 0], w_a[0, 1], w_x[0, 1]])
    return jnp.transpose(g, (1, 2, 0, 3)).reshape(D_MODEL, 4 * LRU_BW)


def _rows_at(part, first):
    return jnp.pad(part, ((first, SUBLANES - first - part.shape[0]), (0, 0)))


def _qk_slot(q_g, k_g):
    wide = lambda v, at: jnp.pad(v, ((0, SUBLANES - 1), (at, D_MODEL - at - HEAD_DIM)))
    return wide(q_g, 0) + wide(k_g, HEAD_DIM)


def _local_step(x, target, P, fetch, emit, B, L, after=None):
    g_mix, g_mlp = P["norm_mix_g"], P["norm_mlp_g"]
    h0 = _rms_fwd(x, g_mix[0:1], "rg_norm", after=after)
    w_in, conv_wb, wcat, gvec = fetch("rg", h0)
    z = _mm(h0, w_in, mode="nn", b_shard=True, name="rg_in")
    rec, a_f, a_b, h_f, h_b, yg = _rg_mix_fwd(z, conv_wb, wcat, gvec, B, L)
    w_out = fetch("rg_out", yg)
    x1, h1 = _mm_res_norm(yg, w_out, x, g_mlp[0:1], "rg_out")
    (x2, h3), mlp0 = _mlp_fwd(x1, h1, fetch, 0, lambda a, w, res, name: _mm_res_norm(a, w, res, g_mix[1:2], name))
    w_qkv, w_o = fetch("att", h3)
    qkv = _mm(h3, w_qkv, mode="nn", b_shard=True, name="attn_qkv")
    cos, sin = _rope_tables(L, B)
    qh, kh, vh = _qk_prep(qkv, cos, sin, P["q_g"], P["k_g"])
    o = _attn_fwd(qh, kh, vh, B, L)
    x3, h4 = _mm_res_norm(o, w_o, x2, g_mlp[1:2], "attn_out")
    (dx4, dx4_bf, loss_acc, d_final_g), mlp1 = _mlp_fwd(
        x3, h4, fetch, 1, lambda a, w, res, name: _mm_final_loss(a, w, res, target, P["final_g"], name))

    dx3, dx3_bf, dg_mlp1, d_up1, d_down1 = _mlp_bwd(x3, g_mlp[1:2], mlp1, dx4, dx4_bf, 1, None)
    tok = emit("mlp1", [d_up1, d_down1])
    d_wo = _mm(o, dx3_bf, mode="tn", out_dtypes=(BF16,), name="attn_dwo", after=tok)
    do = _mm(dx3_bf, w_o, mode="nt", out_dtypes=(BF16,), name="attn_do")
    dq, dk, dv = _attn_bwd(qh, kh, vh, o, do, B, L)
    dqkv, dq_g, dk_g = _qk_prep_bwd(qkv, dq, dk, dv, cos, sin, P["q_g"], P["k_g"])
    d_wqkv = _mm(h3, dqkv, mode="tn", o_shard=True, out_dtypes=(BF16,), name="attn_dwqkv")
    tok = emit("att", [d_wqkv, d_wo])
    dx2, dx2_bf, dg_mix1 = _mm_norm_bwd(dqkv, w_qkv, x2, dx3, g_mix[1:2], "attn_dh", after=tok)
    tok = emit("point_attn_done", [dx2_bf])
    dx1, dx1_bf, dg_mlp0, d_up0, d_down0 = _mlp_bwd(x1, g_mlp[0:1], mlp0, dx2, dx2_bf, 0, tok)
    d_wout = _mm(yg, dx1_bf, mode="tn", out_dtypes=(BF16,), name="rg_dwout")
    tok = emit("mlp0", [d_up0, d_down0, d_wout])
    dy, dgate = _mm_gated_out_bwd(dx1_bf, w_out, h_f, h_b, z, "rg_dyg", after=tok)
    du_f, da_f, du_b, da_b = _scan_bwd(dy, a_f, h_f, a_b, h_b, B, L)
    drec_c, d_wa, d_wx, d_gvec = _gate_bwd(rec, du_f, da_f, du_b, da_b, wcat, gvec)
    tok = emit("gates", [d_wa, d_wx])
    dz, d_convwb = _conv_bwd(z, drec_c, conv_wb, dgate, B, L, after=tok)
    tok = emit("point_mix_done", [dz])
    d_win = _mm(h0, dz, mode="tn", o_shard=True, out_dtypes=(BF16,), name="rg_dwin", after=tok)
    tok = emit("rg_in", [d_win])
    grad_x, _, dg_mix0 = _mm_norm_bwd(dz, w_in, x, dx1, g_mix[0:1], "rg_dh", after=tok)

    norms = (_rows_at(dg_mix0, 0) + _rows_at(dg_mix1, 1) + _rows_at(dg_mlp0, 2) + _rows_at(dg_mlp1, 3)
             + _rows_at(d_final_g, 4)
             + jnp.pad(loss_acc, ((LOSS_ROW, SUBLANES - 1 - LOSS_ROW), (0, D_MODEL - LANES))))
    vec = jnp.concatenate([norms, d_convwb, d_gvec, _qk_slot(dq_g, dk_g)], axis=0)
    return grad_x, vec


_MESH = pl.DeviceIdType.MESH


def _place():
    x, y, c = lax.axis_index("x"), lax.axis_index("y"), lax.axis_index("c")
    peers = [((1 - x) if j & 2 else x, (1 - y) if j & 1 else y) for j in (1, 2, 3)]
    return x, y, c, peers


def _sum_leading(slots, name):
    def body(s_ref, o_ref):
        acc = s_ref[0]
        for d in range(1, slots.shape[0]):
            acc = acc + s_ref[d]
        o_ref[...] = acc

    return pl.pallas_call(body, name=name, out_shape=jax.ShapeDtypeStruct(slots.shape[1:], slots.dtype))(slots)


_HBM = pl.BlockSpec(memory_space=pltpu.HBM)
_SEM = pl.BlockSpec(memory_space=pltpu.SEMAPHORE)
_EFFECT = pltpu.SideEffectType.DATAFLOW_SIDE_EFFECTING


_COPIES = dict(gather=N_CHIPS - 1, scatter=N_CHIPS - 1, swap=1, spread=N_DEVICES - 1,
               gather_half=N_CHIPS - 1, share_half=N_CHIPS - 1)


def _split_copies(kind, srcs, lands, send, recv):
    x, y, c, peers = _place()
    me = 2 * x + y
    per = _COPIES[kind]
    out = []
    for a in range(len(lands)):
        for j in range(per):
            if kind == "swap":
                src, there, here, dev = srcs[a], lands[a], lands[a], (x, y, 1 - c)
            elif kind == "spread":
                k = j + 1
                dev = ((1 - x) if k & 4 else x, (1 - y) if k & 2 else y, (1 - c) if k & 1 else c)
                mine = lands[a].at[4 * x + 2 * y + c]
                src, there, here = mine, mine, lands[a].at[4 * dev[0] + 2 * dev[1] + dev[2]]
            else:
                px, py = peers[j]
                dev = (px, py, c)
                if kind == "gather":
                    src, there, here = lands[a].at[me], lands[a].at[me], lands[a].at[2 * px + py]
                elif kind in ("gather_half", "share_half"):
                    half = lands[a].shape[1] // 2
                    mine, other = pl.ds(c * half, half), pl.ds((1 - c) * half, half)
                    if kind == "gather_half":
                        src = there = lands[a].at[me, mine]
                        here = lands[a].at[2 * px + py, mine]
                    else:
                        src = there = lands[a].at[2 * px + py, mine]
                        here = lands[a].at[2 * px + py, other]
                        dev = (x, y, 1 - c)
                else:
                    src, there, here = srcs[a].at[2 * px + py], lands[a].at[j], lands[a].at[j]
            mk = functools.partial(
                pltpu.make_async_remote_copy, src_ref=src, send_sem=send.at[per * a + j],
                recv_sem=recv.at[per * a + j], device_id=dev, device_id_type=_MESH)
            out.append((functools.partial(mk, dst_ref=there), functools.partial(mk, dst_ref=here)))
    return out


_CORE_PAIR = ("swap", "share_half")
CORE_PAIR_BARRIER_ID = 0
BARRIER_IDS = dict(gather_rest=1, scatter_mlp1=2, scatter_att=3, scatter_mlp0=4, scatter_gates=5, scatter_rg_in=6,
                   gather_gates=7, spread_vec=8)


def _entry_peers(kind):
    x, y, c, peers = _place()
    if kind in _CORE_PAIR:
        return [(x, y, 1 - c)]
    if kind == "spread":
        return [((1 - x) if k & 4 else x, (1 - y) if k & 2 else y, (1 - c) if k & 1 else c)
                for k in range(1, N_DEVICES)]
    return [(px, py, c) for px, py in peers]


def _entry_params(kind, barrier_id):
    if kind in _CORE_PAIR:
        barrier_id = CORE_PAIR_BARRIER_ID
    collective = {} if barrier_id is None else dict(collective_id=barrier_id)
    return pltpu.CompilerParams(has_side_effects=_EFFECT, **collective)


def _entry_handshake(kind, barrier_id):
    if kind in _CORE_PAIR or barrier_id is not None:
        barrier = pltpu.get_barrier_semaphore()
        peers = _entry_peers(kind)
        for peer in peers:
            pl.semaphore_signal(barrier, inc=1, device_id=peer, device_id_type=_MESH)
        pl.semaphore_wait(barrier, len(peers))


def _exchange_start(kind, srcs, lands, name, after=None, barrier_id=None):
    arrays = list(srcs) + list(lands)
    n_s, n, n_all = len(srcs), len(lands), len(srcs) + len(lands)
    n_sem = _COPIES[kind] * n
    order = _after_operand(after)
    n_x = len(order)

    def body(*refs):
        _entry_handshake(kind, barrier_id)
        send, recv = refs[n_all + n_x], refs[n_all + n_x + 1]
        token = refs[-1]
        for started, _ in _split_copies(kind, refs[:n_s], refs[n_s:n_all], send, recv):
            started().start()
        token[...] = jnp.zeros(token.shape, F32)

    res = pl.pallas_call(
        body, name=name,
        out_shape=(pltpu.SemaphoreType.DMA((n_sem,)), pltpu.SemaphoreType.DMA((n_sem,)),
                   *[pltpu.HBM(a.shape, a.dtype) for a in arrays], jax.ShapeDtypeStruct((SUBLANES, LANES), F32)),
        in_specs=[_HBM] * n_all + [_ANY] * n_x,
        out_specs=(_SEM, _SEM, *[_HBM] * n_all, pl.BlockSpec(memory_space=pltpu.VMEM)),
        input_output_aliases={i: 2 + i for i in range(n_all)},
        compiler_params=_entry_params(kind, barrier_id),
    )(*[pltpu.with_memory_space_constraint(a, pltpu.HBM) for a in arrays], *order)
    return (res[0], res[1], res[2:2 + n_s], res[2 + n_s:2 + n_all]), res[-1]


def _gather_start_groups(land_groups, name, after=None, kind="gather", barrier_id=None):
    arrays = [a for group in land_groups for a in group]
    n_all, n_g = len(arrays), len(land_groups)
    order = _after_operand(after)
    n_x = len(order)

    def body(*refs):
        _entry_handshake(kind, barrier_id)
        first = 0
        for gi, group in enumerate(land_groups):
            send, recv = refs[n_all + n_x + 2 * gi], refs[n_all + n_x + 2 * gi + 1]
            for started, _ in _split_copies(kind, [], refs[first:first + len(group)], send, recv):
                started().start()
            first += len(group)
        refs[-1][...] = jnp.zeros(refs[-1].shape, F32)

    sems = [pltpu.SemaphoreType.DMA((_COPIES[kind] * len(group),)) for group in land_groups for _ in range(2)]
    res = pl.pallas_call(
        body, name=name,
        out_shape=(*sems, *[pltpu.HBM(a.shape, a.dtype) for a in arrays], jax.ShapeDtypeStruct((SUBLANES, LANES), F32)),
        in_specs=[_HBM] * n_all + [_ANY] * n_x,
        out_specs=(*[_SEM] * (2 * n_g), *[_HBM] * n_all, pl.BlockSpec(memory_space=pltpu.VMEM)),
        input_output_aliases={i: 2 * n_g + i for i in range(n_all)},
        compiler_params=_entry_params(kind, barrier_id),
    )(*[pltpu.with_memory_space_constraint(a, pltpu.HBM) for a in arrays], *order)
    handles, first = [], 2 * n_g
    for gi, group in enumerate(land_groups):
        handles.append((res[2 * gi], res[2 * gi + 1], [], res[first:first + len(group)]))
        first += len(group)
    return handles, res[-1]


def _exchange_wait(kind, handle, after, name):
    send, recv, srcs, lands = handle
    arrays = list(srcs) + list(lands)
    n_s, n_all = len(srcs), len(arrays)
    order = list(after) if isinstance(after, (list, tuple)) else [after]

    def body(*refs):
        for started, landing in _split_copies(kind, refs[:n_s], refs[n_s:n_all], refs[n_all], refs[n_all + 1]):
            started().wait_send()
            landing().wait_recv()

    res = pl.pallas_call(
        body, name=name, out_shape=[pltpu.HBM(a.shape, a.dtype) for a in arrays],
        in_specs=[_HBM] * n_all + [_SEM, _SEM] + [_ANY] * len(order), out_specs=[_HBM] * n_all,
        input_output_aliases={i: i for i in range(n_all)},
        compiler_params=pltpu.CompilerParams(has_side_effects=_EFFECT),
    )(*arrays, send, recv, *order)
    return res[:n_s], res[n_s:]


def _index_operand(i):
    return jnp.reshape(i, (1,)).astype(jnp.int32)


def _cast_into_slot(src, row0, rows, me, dtype, name, after=None, add=None, n_slots=N_CHIPS):
    cols = src.shape[1]
    tm = min(512, rows)
    order = _after_operand(after)
    terms = [src] + ([] if add is None else [add])

    def body(me_ref, *rest):
        val = rest[0][...]
        if add is not None:
            val = val + rest[1][...]
        rest[-1][...] = val.astype(dtype)

    return pl.pallas_call(
        body, name=name,
        grid_spec=pltpu.PrefetchScalarGridSpec(
            num_scalar_prefetch=1, grid=(rows // tm,),
            in_specs=[pl.BlockSpec((tm, cols), lambda i, me_ref: (i + row0 // tm, 0))] * len(terms)
            + [_ANY] * len(order),
            out_specs=pl.BlockSpec((None, tm, cols), lambda i, me_ref: (me_ref[0], i, 0))),
        out_shape=jax.ShapeDtypeStruct((n_slots, rows, cols), dtype), compiler_params=_params(("parallel",)),
    )(_index_operand(me), *terms, *order)


def _sum_slots(mine, r, me, name):
    _, rows, cols = r.shape
    tm = min(512, rows)

    def body(me_ref, own_ref, r_ref, o_ref):
        o_ref[...] = ((own_ref[...].astype(F32) + r_ref[0].astype(F32)) + r_ref[1].astype(F32)) + r_ref[2].astype(F32)

    return pl.pallas_call(
        body, name=name,
        grid_spec=pltpu.PrefetchScalarGridSpec(
            num_scalar_prefetch=1, grid=(rows // tm,),
            in_specs=[pl.BlockSpec((None, tm, cols), lambda i, me_ref: (me_ref[0], i, 0)),
                      pl.BlockSpec((N_CHIPS - 1, tm, cols), lambda i, me_ref: (0, i, 0))],
            out_specs=pl.BlockSpec((tm, cols), lambda i, me_ref: (i, 0))),
        out_shape=jax.ShapeDtypeStruct((rows, cols), F32), compiler_params=_params(("parallel",)),
    )(_index_operand(me), mine, r)


def _adamw(w, m, v, ps, qs, name):
    rows, cols = w.shape
    seg_rows = ps[0].shape[0]
    tm = min(512, seg_rows)
    while seg_rows % tm:
        tm -= SUBLANES
    per, n_seg = seg_rows // tm, len(ps)
    parts = list(ps) + ([] if qs is None else list(qs))

    def body(w_ref, m_ref, v_ref, *rest):
        g_refs, outs = rest[:len(parts)], rest[len(parts):]
        grad = lambda s: g_refs[s][...] if qs is None else g_refs[s][...] + g_refs[n_seg + s][...]
        g = grad(0)
        for s in range(1, n_seg):
            g = jnp.where(pl.program_id(0) >= s * per, grad(s), g)
        m1 = ADAM_B1 * m_ref[...] + (1.0 - ADAM_B1) * g
        v1 = ADAM_B2 * v_ref[...] + (1.0 - ADAM_B2) * (g * g)
        m_hat = m1 / (1.0 - ADAM_B1 ** ADAM_STEP)
        v_hat = v1 / (1.0 - ADAM_B2 ** ADAM_STEP)
        outs[0][...] = g
        outs[1][...] = (-ADAM_LR) * (m_hat / (jnp.sqrt(v_hat) + ADAM_EPS) + ADAM_WD * w_ref[...])
        outs[2][...] = m1
        outs[3][...] = v1

    row_spec = pl.BlockSpec((tm, cols), lambda i: (i, 0))
    seg_spec = lambda s: pl.BlockSpec((tm, cols), lambda i: (jnp.clip(i - s * per, 0, per - 1), 0))
    return pl.pallas_call(
        body, name=name, grid=(rows // tm,),
        in_specs=[row_spec] * 3 + [seg_spec(s) for s in range(n_seg)] * (1 if qs is None else 2),
        out_specs=[row_spec] * 4, out_shape=[jax.ShapeDtypeStruct((rows, cols), F32)] * 4,
        compiler_params=_params(("arbitrary",)),
    )(w, m, v, *parts)


def _put_cols(shard, me):
    full = jnp.zeros((shard.shape[0], D_MODEL), F32)
    return lax.dynamic_update_slice(full, shard, (0, me * (D_MODEL // N_CHIPS)))


def _gate_vec_slot(b_a, b_x, lam):
    return _rows_at(b_a, _ROW_BA) + _rows_at(b_x, _ROW_BX) + _rows_at(lam, _ROW_LAM)


def _pack_vec(p, me):
    return jnp.concatenate([
        _rows_at(p["norm_mix_g"], 0) + _rows_at(p["norm_mlp_g"], 2) + _rows_at(p["final_g"][None], 4),
        _rows_at(_put_cols(p["rg_conv_w"][0, :, 0, :], me), 0) + _rows_at(p["rg_conv_b"], 4),
        _gate_vec_slot(_put_cols(p["rg_b_a"][0], me), _put_cols(p["rg_b_x"][0], me), _put_cols(p["rg_lam"][0], me)),
        _qk_slot(p["at_q_g"], p["at_k_g"]),
    ], axis=0)


def _unpack_vec(r, me):
    def cols(rows):
        return lax.dynamic_slice(rows, (0, me * (D_MODEL // N_CHIPS)), (rows.shape[0], D_MODEL // N_CHIPS))

    gate = r[16:24]
    return dict(
        norm_mix_g=r[0:2], norm_mlp_g=r[2:4], final_g=r[4], rg_conv_w=cols(r[8:12])[None, :, None, :],
        rg_conv_b=r[12:13], rg_b_a=cols(gate[_ROW_BA:_ROW_BA + 2])[None], rg_b_x=cols(gate[_ROW_BX:_ROW_BX + 2])[None],
        rg_lam=cols(gate[_ROW_LAM:_ROW_LAM + 2])[None], at_q_g=r[24:25, 0:HEAD_DIM],
        at_k_g=r[24:25, HEAD_DIM:2 * HEAD_DIM])


_WEIGHTS = ['norm_mix_g', 'norm_mlp_g', 'rg_w_in', 'rg_conv_w', 'rg_conv_b', 'rg_w_a', 'rg_b_a', 'rg_w_x', 'rg_b_x',
            'rg_lam', 'rg_w_out', 'at_w_qkv', 'at_q_g', 'at_k_g', 'at_w_o', 'mlp_w_up', 'mlp_w_down', 'final_g']
_BIG = dict(rg_w_in=["rg_w_in"], rg_w_out=["rg_w_out"], at_w_qkv=["at_w_qkv"], at_w_o=["at_w_o"],
            mlp_w_up=["up0", "up1"], mlp_w_down=["down0", "down1"])


def kernel(x, *args):
    n_w = len(_WEIGHTS)
    w = dict(zip(_WEIGHTS, args[:n_w]))
    target = args[n_w]
    m = dict(zip(_WEIGHTS, args[n_w + 1:2 * n_w + 1]))
    v = dict(zip(_WEIGHTS, args[2 * n_w + 1:3 * n_w + 1]))
    B, L, _ = x.shape
    T = B * L
    me = 2 * lax.axis_index("x") + lax.axis_index("y")

    vec = jnp.concatenate([_gate_vec_slot(w["rg_b_a"][0], w["rg_b_x"][0], w["rg_lam"][0]),
                           _rows_at(w["rg_conv_w"][0, :, 0, :], 0)], axis=0)
    flat = lambda a: a.reshape(-1, a.shape[-1])
    rows_of = lambda k: w[k].shape[-2]
    groups = [("rg", [("rg_w_in", 0, BF16), (vec, 0, F32)]), ("rg_out", [("rg_w_out", 0, BF16)]),
              ("mlp0_up", [("mlp_w_up", 0, BF16)]), ("mlp0_down", [("mlp_w_down", 0, BF16)]),
              ("att", [("at_w_qkv", 0, BF16), ("at_w_o", 0, BF16)]),
              ("mlp1", [("mlp_w_up", 1, BF16), ("mlp_w_down", 1, BF16)])]

    def landing_zones(group, members, after):
        lands = []
        for n, (k, layer, dtype) in enumerate(members):
            src, rows = (flat(w[k]), rows_of(k)) if isinstance(k, str) else (k, k.shape[0])
            lands.append(_cast_into_slot(src, layer * rows, rows, me, dtype, f"place_{group}{n}", after=after))
        return lands

    halves, gathers = {}, {}
    halves["rg"], tok = _exchange_start("gather_half", [], landing_zones(*groups[0], None), "gather_rg_start")
    handles, tok = _gather_start_groups([landing_zones(g, members, tok) for g, members in groups[1:]],
                                        "gather_rest_start", after=tok, kind="gather_half",
                                        barrier_id=BARRIER_IDS["gather_rest"])
    halves.update(zip([g for g, _ in groups[1:]], handles))
    wcat = _make_wcat(w["rg_w_a"], w["rg_w_x"]).astype(BF16)

    packs = [_pack_vec(p, me) for p in (w, m, v)]

    ready = {}

    def share(some, after, name):
        landed = [_exchange_wait("gather_half", halves[g], after, f"gather_{g}_landed")[1] for g in some]
        handles, _ = _gather_start_groups(landed, name, kind="share_half")
        gathers.update(zip(some, handles))

    def fetch(what, after):
        if what in ready:
            return ready[what]
        group = "mlp1" if what.startswith("mlp1") else what
        if group == "rg":
            share(["rg"], [after, wcat] + packs, "share_rg_start")
        elif group == "rg_out":
            share(["rg_out", "mlp0_up", "mlp0_down", "att"], after, "share_early_start")
        _, full = _exchange_wait("share_half", gathers[group], after, f"gather_{group}_wait")
        if group == "att":
            share(["mlp1"], after, "share_mlp1_start")
        if group == "rg":
            vec_full = jnp.transpose(full[1], (1, 0, 2)).reshape(2 * SUBLANES, D_MODEL)
            conv_wb = vec_full[SUBLANES:] + _rows_at(w["rg_conv_b"], 4)
            return full[0], conv_wb, wcat, vec_full[:SUBLANES]
        if group == "rg_out":
            return full[0].reshape(D_MODEL, D_MODEL)
        if group == "att":
            return full[0], full[1].reshape(D_MODEL, D_MODEL)
        if group == "mlp1":
            ready["mlp1_up"], ready["mlp1_down"] = full[0], full[1].reshape(4 * D_MODEL, D_MODEL)
            return ready[what]
        return full[0] if group == "mlp0_up" else full[0].reshape(4 * D_MODEL, D_MODEL)

    names = dict(mlp1=["up1", "down1"], att=["at_w_qkv", "at_w_o"], mlp0=["up0", "down0", "rg_w_out"],
                 rg_in=["rg_w_in"], gates=["rg_w_a", "rg_w_x"])
    scatters, swaps, P, Q, res = {}, [], {}, {}, {}

    def start_scatter(group, grads):
        srcs = [g.reshape(N_CHIPS, -1, g.shape[-1]) for g in grads]
        lands = [lax.empty((N_CHIPS - 1,) + s.shape[1:], s.dtype) for s in srcs]
        scatters[group], token = _exchange_start("scatter", srcs, lands, f"scatter_{group}_start",
                                                 barrier_id=BARRIER_IDS[f"scatter_{group}"])
        return token

    def settle(groups, after):
        keys, parts = [], []
        for group in groups:
            srcs, lands = _exchange_wait("scatter", scatters[group], after, f"scatter_{group}_wait")
            for k, s, r in zip(names[group], srcs, lands):
                keys.append(k)
                parts.append(_sum_slots(s, r, me, f"sum_{k}"))
        handle, token = _exchange_start("swap", parts, [lax.empty(p.shape, F32) for p in parts],
                                        f"swap_{groups[0]}_start")
        swaps.append((keys, handle, f"swap_{groups[0]}_wait"))
        return token

    def finish(after):
        for keys, handle, name in swaps:
            mine, theirs = _exchange_wait("swap", handle, after, name)
            P.update(zip(keys, mine))
            Q.update(zip(keys, theirs))
        swaps.clear()
        last = after
        for k, parts in _BIG.items():
            if k in res or any(p not in P for p in parts):
                continue
            shape = w[k].shape
            two_d = lambda a: a.reshape(-1, shape[-1])
            outs = _adamw(two_d(w[k]), two_d(m[k]), two_d(v[k]), [P[p] for p in parts], [Q[p] for p in parts],
                          f"adamw_{k}")
            res[k] = [o.reshape(shape) for o in outs]
            last = outs[0]
        if "rg_w_a" in P and "gates" not in gathers:
            lands = [_cast_into_slot(P[k], 0, P[k].shape[0], me, F32, f"place_{k}", after=last, add=Q[k])
                     for k in names["gates"]]
            gathers["gates"], last = _exchange_start("gather", [], lands, "gather_gates_start", after=last,
                                                     barrier_id=BARRIER_IDS["gather_gates"])
        return last

    def emit(event, arrays):
        if event == "point_attn_done":
            return None
        if event == "point_mix_done":
            return settle(["mlp1", "att", "mlp0"], arrays[0])
        token = start_scatter(event, arrays)
        if event == "rg_in":
            return finish(settle(["gates"], token))
        return token

    P_vec = dict(norm_mix_g=w["norm_mix_g"], norm_mlp_g=w["norm_mlp_g"], final_g=w["final_g"][None],
                 q_g=w["at_q_g"], k_g=w["at_k_g"])
    grad_x, vec_part = _local_step(x.reshape(T, D_MODEL), target.reshape(T, D_MODEL), P_vec, fetch, emit, B, L,
                                   after=tok)

    me8 = 2 * me + lax.axis_index("c")
    vec_slots = _cast_into_slot(vec_part, 0, VEC_ROWS, me8, F32, "place_vec", n_slots=N_DEVICES)
    spread, tok = _exchange_start("spread", [], [vec_slots], "spread_vec_start", barrier_id=BARRIER_IDS["spread_vec"])
    last = settle(["rg_in"], tok)
    _, gate_grads = _exchange_wait("gather", gathers["gates"], last, "gather_gates_wait")
    for k, g in zip(names["gates"], gate_grads):
        two_d = lambda a: a.reshape(g.shape[0] * g.shape[1], g.shape[2])
        outs = _adamw(two_d(w[k]), two_d(m[k]), two_d(v[k]), [two_d(g)], None, f"adamw_{k}")
        res[k] = [o.reshape(w[k].shape) for o in outs]
        last = outs[0]
    _, (vec_all,) = _exchange_wait("spread", spread, last, "spread_vec_wait")
    vec_grad = _sum_leading(vec_all, "sum_vec")
    loss = vec_grad[LOSS_ROW, 0]
    outs = _adamw(*packs, [vec_grad], None, "adamw_vec")
    finish(outs[0])
    unpacked = [_unpack_vec(o, me) for o in outs]
    for k in _WEIGHTS:
        if k not in res:
            res[k] = [u[k] for u in unpacked]

    result = [loss, grad_x.reshape(B, L, D_MODEL)]
    for slot in range(4):
        result += [res[k][slot] for k in _WEIGHTS]
    return tuple(result)
```

```python
import functools
import math

import jax
import jax.numpy as jnp
import numpy as np
from jax import lax
from jax.experimental import pallas as pl
from jax.experimental.pallas import tpu as pltpu

F32 = jnp.float32
BF16 = jnp.bfloat16

D_MODEL = 1024
HEAD_DIM = 128
N_HEADS = 8
N_KV = 2
GROUP = N_HEADS // N_KV
LRU_BLOCKS = 8
LRU_BW = 128
GRID_W = 64
ROPE_THETA = 10000.0
EPS = 1e-6
RG_C = 8.0
SCALE = 1.0 / math.sqrt(HEAD_DIM)
N_CHIPS = 4

ADAM_LR = 0.001
ADAM_B1 = 0.9
ADAM_B2 = 0.999
ADAM_EPS = 1e-08
ADAM_WD = 0.01
ADAM_STEP = 10

V7X_VMEM_BYTES = 64 * 1024 * 1024
VMEM_LIMIT = V7X_VMEM_BYTES * 3 // 4
LANES = 128
SUBLANES = 8

N_DEVICES = 8
VEC_ROWS = 32
LOSS_ROW = 5


def _params(sem):
    return pltpu.CompilerParams(dimension_semantics=sem, vmem_limit_bytes=VMEM_LIMIT)


_ANY = pl.BlockSpec(memory_space=pl.ANY)
_NN = (((1,), (0,)), ((), ()))
_NT = (((1,), (1,)), ((), ()))
_TN = (((0,), (0,)), ((), ()))


def _after_operand(after):
    return [] if after is None else [after]


def _fit(t, n):
    if n <= t:
        return n
    c = (t // LANES) * LANES
    while n % c:
        c -= LANES
    return c


MM_VMEM_BUDGET = VMEM_LIMIT * 3 // 4
def _mm_tiles(M, K, ns, n_total, out_dtypes, extras, whole_rows):
    for tm in (2048, 1024, 512, 256, 128):
        for tn in ((ns,) if whole_rows else (1024, 512, 256)):
            tn = _fit(tn, ns)
            per_row = 2 * (2 * K) + 4 * tn + sum(2 * tn * jnp.dtype(d).itemsize for d in out_dtypes)
            per_row += sum(2 * tn * e.dtype.itemsize for e in extras)
            b_buffers = 1 if tn == n_total else 2
            if M % tm == 0 and b_buffers * (2 * K * tn) + tm * per_row <= MM_VMEM_BUDGET:
                return tm, tn
    raise ValueError(f"no tile fits VMEM for M={M} K={K} N={ns}")


def _mm(a, b, *, mode, name, out_dtypes=(F32,), b_shard=False, o_shard=False, extras=(), epi=None, after=None,
        bcast=(), accs=(), ref_epi=None, out_cols=None):
    if mode == "tn":
        K, M = a.shape
        N = b.shape[1]
    else:
        M, K = a.shape
        if mode == "nn":
            N = b.shape[0] * b.shape[2] if b_shard else b.shape[1]
        else:
            N = b.shape[1] if b_shard else b.shape[0]
    ns = N
    if b_shard and mode == "nn":
        ns = b.shape[2]
    elif o_shard:
        ns = N // N_CHIPS
    tm, tn = _mm_tiles(M, K, ns, N, out_dtypes, extras, whole_rows=ref_epi is not None)
    if ref_epi is not None:
        tm = min(tm, 512)
    grid = (M // tm, N // tn)
    q = ns // tn
    once = dict(pipeline_mode=pl.Buffered(1)) if tn == N else {}

    if mode == "tn":
        a_spec = pl.BlockSpec((K, tm), lambda i, j: (0, i))
        b_spec = pl.BlockSpec((K, tn), lambda i, j: (0, j), **once)
        dims = _TN
    elif mode == "nn":
        a_spec = pl.BlockSpec((tm, K), lambda i, j: (i, 0))
        if b_shard:
            b_spec = pl.BlockSpec((None, K, tn), lambda i, j: (j // q, 0, j % q), **once)
        else:
            b_spec = pl.BlockSpec((K, tn), lambda i, j: (0, j), **once)
        dims = _NN
    else:
        a_spec = pl.BlockSpec((tm, K), lambda i, j: (i, 0))
        if b_shard:
            ks = b.shape[2]
            b_spec = pl.BlockSpec((N_CHIPS, tn, ks), lambda i, j: (0, j, 0), **once)
        else:
            b_spec = pl.BlockSpec((tn, K), lambda i, j: (j, 0), **once)
        dims = _NT

    if o_shard:
        o_specs = [pl.BlockSpec((None, tm, tn), lambda i, j: (j // q, i, j % q))]
        o_shapes = [jax.ShapeDtypeStruct((N_CHIPS, M, ns), out_dtypes[0])]
    else:
        o_specs = [pl.BlockSpec((tm, tn), lambda i, j: (i, j)) for _ in out_dtypes]
        o_shapes = [jax.ShapeDtypeStruct((M, N if out_cols is None else out_cols[n]), dt)
                    for n, dt in enumerate(out_dtypes)]
    e_specs = [pl.BlockSpec((tm, tn), lambda i, j: (i, j)) for _ in extras]
    e_specs += [pl.BlockSpec(v.shape, lambda i, j: (0, 0)) for v in bcast]
    o_specs += [pl.BlockSpec(s, lambda i, j: (0, 0)) for s in accs]
    o_shapes += [jax.ShapeDtypeStruct(s, F32) for s in accs]
    n_e, n_b, n_o, n_a = len(extras), len(bcast), len(out_dtypes), len(accs)
    order = _after_operand(after)
    n_x = len(order)
    if epi is None:
        epi = lambda acc: (acc,)

    def body(a_ref, b_ref, *rest):
        e_refs, b_refs = rest[:n_e], rest[n_e:n_e + n_b]
        o_refs = rest[n_e + n_b + n_x:n_e + n_b + n_x + n_o]
        a_refs = rest[n_e + n_b + n_x + n_o:]
        if n_a:
            @pl.when((pl.program_id(0) == 0) & (pl.program_id(1) == 0))
            def _():
                for r in a_refs:
                    r[...] = jnp.zeros(r.shape, F32)
        if mode == "nt" and b_shard:
            acc = None
            for s in range(N_CHIPS):
                part = lax.dot_general(a_ref[:, s * ks:(s + 1) * ks], b_ref[s], dims, preferred_element_type=F32)
                acc = part if acc is None else acc + part
        else:
            acc = lax.dot_general(a_ref[...], b_ref[...], dims, preferred_element_type=F32)
        if ref_epi is not None:
            ref_epi(acc, e_refs, b_refs, o_refs, a_refs)
            return
        outs = epi(acc, *[r[...] for r in e_refs])
        for r, o in zip(o_refs, outs):
            r[...] = o.astype(r.dtype)

    outs = pl.pallas_call(
        body, name=name, grid=grid, in_specs=[a_spec, b_spec] + e_specs + [_ANY] * n_x, out_specs=o_specs,
        out_shape=o_shapes, compiler_params=_params(("arbitrary", "arbitrary") if n_a else ("parallel", "parallel")),
    )(a, b, *extras, *bcast, *order)
    return outs[0] if n_o + n_a == 1 else outs


def _rowwise(fn, rows, bcast, outs, accs=(), *, tm, name, after=None):
    def norm(r):
        return r if isinstance(r, tuple) else (r, r.shape[1], 0)

    rows = [norm(r) for r in rows]
    T = rows[0][0].shape[0]
    tm = min(tm, T)
    while T % tm:
        tm -= SUBLANES
    n_r, n_b, n_o, n_a = len(rows), len(bcast), len(outs), len(accs)
    order = _after_operand(after)
    n_x = len(order)
    in_specs = [pl.BlockSpec((tm, c), functools.partial(lambda i, cb: (i, cb), cb=cb)) for _, c, cb in rows]
    in_specs += [pl.BlockSpec(b.shape, lambda i: (0, 0)) for b in bcast] + [_ANY] * n_x
    out_specs = [pl.BlockSpec((tm, o[0]), lambda i: (i, 0)) for o in outs]
    out_specs += [pl.BlockSpec(s, lambda i: (0, 0)) for s in accs]
    out_shape = [jax.ShapeDtypeStruct((T, o[2] if len(o) > 2 else o[0]), o[1]) for o in outs]
    out_shape += [jax.ShapeDtypeStruct(s, F32) for s in accs]

    def body(*refs):
        in_refs = refs[:n_r]
        b_refs = refs[n_r:n_r + n_b]
        o_refs = refs[n_r + n_b + n_x:n_r + n_b + n_x + n_o]
        a_refs = refs[n_r + n_b + n_x + n_o:]
        if n_a:
            @pl.when(pl.program_id(0) == 0)
            def _():
                for r in a_refs:
                    r[...] = jnp.zeros(r.shape, F32)
        fn(in_refs, b_refs, o_refs, a_refs)

    res = pl.pallas_call(
        body, name=name, grid=(T // tm,), in_specs=in_specs, out_specs=out_specs, out_shape=out_shape,
        compiler_params=_params(("arbitrary",) if n_a else ("parallel",)),
    )(*[r[0] for r in rows], *bcast, *order)
    return res


def _rsum(x):
    return jnp.sum(x, axis=0, keepdims=True)


def _rms_fwd(x, g, name, after=None):
    def fn(ins, bs, outs, accs):
        xv = ins[0][...]
        r = lax.rsqrt(jnp.mean(xv * xv, axis=-1, keepdims=True) + EPS)
        outs[0][...] = (xv * r * bs[0][...]).astype(BF16)

    return _rowwise(fn, [x], [g], [(D_MODEL, BF16)], tm=512, name=name, after=after)[0]


def _rms_bwd_math(xv, dh, g):
    r = lax.rsqrt(jnp.mean(xv * xv, axis=-1, keepdims=True) + EPS)
    hn = xv * r
    dgh = dh * g
    dx = r * (dgh - hn * jnp.mean(dgh * hn, axis=-1, keepdims=True))
    return dx, _rsum(dh * hn)


def _mm_norm_bwd(dy, w, x, dres, g, name, after=None):
    def epilogue(acc, e_refs, b_refs, o_refs, a_refs):
        dx, dg = _rms_bwd_math(e_refs[0][...], acc, b_refs[0][...])
        dx = dx + e_refs[1][...]
        o_refs[0][...] = dx
        o_refs[1][...] = dx.astype(BF16)
        a_refs[0][...] += dg

    return _mm(dy, w, mode="nt", b_shard=True, out_dtypes=(F32, BF16), extras=(x, dres), bcast=(g,),
               accs=((1, D_MODEL),), ref_epi=epilogue, name=name, after=after)


def _mm_res_norm(a, w, res, g, name):
    def epilogue(acc, e_refs, b_refs, o_refs, a_refs):
        xv = acc + e_refs[0][...]
        o_refs[0][...] = xv
        r = lax.rsqrt(jnp.mean(xv * xv, axis=-1, keepdims=True) + EPS)
        o_refs[1][...] = (xv * r * b_refs[0][...]).astype(BF16)

    return _mm(a, w, mode="nn", out_dtypes=(F32, BF16), extras=(res,), bcast=(g,), ref_epi=epilogue, name=name)


def _mm_final_loss(a, w, res, target, g, name):
    def epilogue(acc, e_refs, b_refs, o_refs, a_refs):
        xv = acc + e_refs[0][...]
        gv = b_refs[0][...]
        r = lax.rsqrt(jnp.mean(xv * xv, axis=-1, keepdims=True) + EPS)
        e = xv * r * gv - e_refs[1][...]
        tok = jnp.mean(e * e, axis=-1, keepdims=True)
        a_refs[0][...] += 0.5 * jnp.sum(tok, axis=0, keepdims=True) * jnp.ones((1, LANES), F32)
        dx, dg = _rms_bwd_math(xv, e * (1.0 / D_MODEL), gv)
        o_refs[0][...] = dx
        o_refs[1][...] = dx.astype(BF16)
        a_refs[1][...] += dg

    return _mm(a, w, mode="nn", out_dtypes=(F32, BF16), extras=(res, target), bcast=(g,),
               accs=((1, LANES), (1, D_MODEL)), ref_epi=epilogue, name=name)


def _relu2(acc):
    r = jnp.maximum(acc, 0.0)
    return r * r, r


def _mlp_fwd(x, h, fetch, tag, finish):
    w_up = fetch(f"mlp{tag}_up", h)
    a, r = _mm(h, w_up, mode="nn", b_shard=True, out_dtypes=(BF16, BF16), epi=_relu2, name=f"mlp{tag}_up")
    w_down = fetch(f"mlp{tag}_down", a)
    return finish(a, w_down, x, f"mlp{tag}_down"), (h, a, r, w_up, w_down)


def _mlp_bwd(x, g, saved, dx, dx_bf, tag, after):
    h, a, r, w_up, w_down = saved
    d_down = _mm(a, dx_bf, mode="tn", out_dtypes=(BF16,), name=f"mlp{tag}_dwdown", after=after)
    dup = _mm(dx_bf, w_down, mode="nt", extras=(r,), out_dtypes=(BF16,),
              epi=lambda acc, rv: (acc * (2.0 * rv.astype(F32)),), name=f"mlp{tag}_dup")
    d_up = _mm(h, dup, mode="tn", o_shard=True, out_dtypes=(BF16,), name=f"mlp{tag}_dwup")
    dx_new, dx_new_bf, dg = _mm_norm_bwd(dup, w_up, x, dx, g, f"mlp{tag}_dh")
    return dx_new, dx_new_bf, dg, d_up, d_down


def _rope_tables(L, B):
    rows = L // GRID_W
    row = np.repeat(np.arange(rows, dtype=np.float32), GRID_W)
    col = np.tile(np.arange(GRID_W, dtype=np.float32), rows)
    inv = (ROPE_THETA ** (-np.arange(HEAD_DIM // 4, dtype=np.float32) / (HEAD_DIM // 4))).astype(np.float32)
    ar, ac = row[:, None] * inv, col[:, None] * inv
    cos = np.concatenate([np.cos(ar), np.cos(ar), np.cos(ac), np.cos(ac)], axis=-1)
    sin = np.concatenate([-np.sin(ar), np.sin(ar), -np.sin(ac), np.sin(ac)], axis=-1)
    return jnp.asarray(np.tile(cos, (B, 1)), F32), jnp.asarray(np.tile(sin, (B, 1)), F32)


def _swap_halves(x):
    lane = lax.broadcasted_iota(jnp.int32, x.shape, 1)
    return jnp.where((lane % 64) < 32, pltpu.roll(x, HEAD_DIM - 32, 1), pltpu.roll(x, 32, 1))


def _qk_prep(qkv, cos, sin, q_g, k_g):
    def fn(ins, bs, outs, accs):
        c, s = ins[1][...], ins[2][...]
        for h in range(N_HEADS + N_KV):
            xv = ins[0][:, h * HEAD_DIM:(h + 1) * HEAD_DIM]
            g = bs[0][...] if h < N_HEADS else bs[1][...]
            r = lax.rsqrt(jnp.mean(xv * xv, axis=-1, keepdims=True) + EPS)
            z = xv * r * g
            y = (z * c + _swap_halves(z) * s).astype(BF16)
            if h < N_HEADS:
                outs[0][:, h * HEAD_DIM:(h + 1) * HEAD_DIM] = y
            else:
                outs[1][:, (h - N_HEADS) * HEAD_DIM:(h - N_HEADS + 1) * HEAD_DIM] = y
        outs[2][...] = ins[0][:, (N_HEADS + N_KV) * HEAD_DIM:].astype(BF16)

    kvw = N_KV * HEAD_DIM
    return _rowwise(fn, [qkv, cos, sin], [q_g, k_g], [(D_MODEL, BF16), (kvw, BF16), (kvw, BF16)], tm=512,
                    name="attn_qk_prep")


def _qk_prep_bwd(qkv, dq, dk, dv, cos, sin, q_g, k_g):
    def fn(ins, bs, outs, accs):
        c, s = ins[4][...], ins[5][...]
        for h in range(N_HEADS + N_KV):
            sl = slice(h * HEAD_DIM, (h + 1) * HEAD_DIM)
            xv = ins[0][:, sl]
            if h < N_HEADS:
                g, dy, acc = bs[0][...], ins[1][:, sl], accs[0]
            else:
                ks = slice((h - N_HEADS) * HEAD_DIM, (h - N_HEADS + 1) * HEAD_DIM)
                g, dy, acc = bs[1][...], ins[2][:, ks], accs[1]
            r = lax.rsqrt(jnp.mean(xv * xv, axis=-1, keepdims=True) + EPS)
            xn = xv * r
            dz = dy * c - _swap_halves(dy) * s
            acc[...] += _rsum(dz * xn)
            dxn = dz * g
            outs[0][:, sl] = (r * (dxn - xn * jnp.mean(dxn * xn, axis=-1, keepdims=True))).astype(BF16)
        outs[0][:, (N_HEADS + N_KV) * HEAD_DIM:] = ins[3][...].astype(BF16)

    return _rowwise(fn, [qkv, dq, dk, dv, cos, sin], [q_g, k_g], [(qkv.shape[1], BF16)],
                    [(1, HEAD_DIM), (1, HEAD_DIM)], tm=512, name="attn_qk_prep_bwd")


_EXP2_SCALE = SCALE * math.log2(math.e)


def _exp_rows(q, k):
    s = lax.dot_general(q, k, _NT, preferred_element_type=F32)
    p = jnp.exp2((s - jnp.max(s, axis=-1, keepdims=True)) * _EXP2_SCALE)
    return p, jnp.sum(p, axis=-1, keepdims=True)


def _attn_fwd(q, k, v, B, L, tq=2048, sub=256):
    tq = min(tq, L)
    sub = min(sub, tq)
    nq = L // tq

    def body(q_ref, k_ref, v_ref, o_ref):
        kv, vv = k_ref[...], v_ref[...]
        for c in range(tq // sub):
            rows = slice(c * sub, (c + 1) * sub)
            p, l = _exp_rows(q_ref[rows, :], kv)
            o = jnp.dot(p.astype(BF16), vv, preferred_element_type=F32)
            o_ref[rows, :] = (o * (1.0 / l)).astype(o_ref.dtype)

    return pl.pallas_call(
        body, name="attn_fwd", grid=(B, N_HEADS, nq),
        in_specs=[pl.BlockSpec((tq, HEAD_DIM), lambda b, h, i: (b * nq + i, h)),
                  pl.BlockSpec((L, HEAD_DIM), lambda b, h, i: (b, h // GROUP)),
                  pl.BlockSpec((L, HEAD_DIM), lambda b, h, i: (b, h // GROUP))],
        out_specs=pl.BlockSpec((tq, HEAD_DIM), lambda b, h, i: (b * nq + i, h)),
        out_shape=jax.ShapeDtypeStruct((B * L, D_MODEL), BF16),
        compiler_params=_params(("parallel", "parallel", "parallel")),
    )(q, k, v)


def _attn_bwd(q, k, v, o, do, B, L, tq=2048, sub=512):
    tq = min(tq, L)
    sub = min(sub, tq)
    nq = L // tq

    def body(q_ref, k_ref, v_ref, o_ref, do_ref, dq_ref, dk_ref, dv_ref):
        @pl.when((pl.program_id(2) == 0) & (pl.program_id(3) == 0))
        def _():
            dk_ref[...] = jnp.zeros(dk_ref.shape, F32)
            dv_ref[...] = jnp.zeros(dv_ref.shape, F32)

        kv, vv = k_ref[...], v_ref[...]
        ps, es, dos, qs = [], [], [], []
        for c in range(tq // sub):
            rows = slice(c * sub, (c + 1) * sub)
            qc, doc = q_ref[rows, :], do_ref[rows, :]
            p, l = _exp_rows(qc, kv)
            inv = 1.0 / l
            dp = lax.dot_general(doc, vv, _NT, preferred_element_type=F32)
            delta = jnp.sum(doc.astype(F32) * o_ref[rows, :].astype(F32), axis=-1, keepdims=True)
            e = (p * (dp - delta)).astype(BF16)
            dq_ref[rows, :] = jnp.dot(e, kv, preferred_element_type=F32) * (inv * SCALE)
            ps.append(p.astype(BF16))
            es.append(e)
            dos.append((doc.astype(F32) * inv).astype(BF16))
            qs.append((qc.astype(F32) * (inv * SCALE)).astype(BF16))
        cat = lambda xs: xs[0] if len(xs) == 1 else jnp.concatenate(xs, axis=0)
        dv_ref[...] += lax.dot_general(cat(ps), cat(dos), _TN, preferred_element_type=F32)
        dk_ref[...] += lax.dot_general(cat(es), cat(qs), _TN, preferred_element_type=F32)

    qmap = lambda b, kh, g, i: (b * nq + i, kh * GROUP + g)
    kmap = lambda b, kh, g, i: (b, kh)
    kvw = N_KV * HEAD_DIM
    return pl.pallas_call(
        body, name="attn_bwd", grid=(B, N_KV, GROUP, nq),
        in_specs=[pl.BlockSpec((tq, HEAD_DIM), qmap), pl.BlockSpec((L, HEAD_DIM), kmap),
                  pl.BlockSpec((L, HEAD_DIM), kmap), pl.BlockSpec((tq, HEAD_DIM), qmap),
                  pl.BlockSpec((tq, HEAD_DIM), qmap)],
        out_specs=[pl.BlockSpec((tq, HEAD_DIM), qmap), pl.BlockSpec((L, HEAD_DIM), kmap),
                   pl.BlockSpec((L, HEAD_DIM), kmap)],
        out_shape=[jax.ShapeDtypeStruct((B * L, D_MODEL), F32), jax.ShapeDtypeStruct((B * L, kvw), F32),
                   jax.ShapeDtypeStruct((B * L, kvw), F32)],
        compiler_params=_params(("parallel", "parallel", "arbitrary", "arbitrary")),
    )(q, k, v, o, do)


def _conv_shift(x, t, L, k):
    if k == 2:
        return x
    if k < 2:
        return jnp.where(t >= 2 - k, pltpu.roll(x, 2 - k, 0), 0.0)
    return jnp.where(t < L - (k - 2), pltpu.roll(x, L - (k - 2), 0), 0.0)


def _conv_apply(x, w_ref, L):
    t = lax.broadcasted_iota(jnp.int32, x.shape, 0)
    acc = w_ref[4:5, :] + w_ref[2:3, :] * x
    for k in (0, 1, 3):
        acc = acc + w_ref[k:k + 1, :] * _conv_shift(x, t, L, k)
    return acc


def _conv_bwd(z, g, wb, dz, B, L, tc=256, after=None):
    noff = D_MODEL // tc
    order = _after_operand(after)

    def body(z_ref, g_ref, w_ref, dz_in, *rest):
        dx_ref, dw_ref = rest[len(order):]

        @pl.when(pl.program_id(1) == 0)
        def _():
            dw_ref[...] = jnp.zeros(dw_ref.shape, F32)

        x, gv = z_ref[...], g_ref[...]
        t = lax.broadcasted_iota(jnp.int32, x.shape, 0)
        dx = w_ref[2:3, :] * gv
        for k in (0, 1, 3):
            dx = dx + w_ref[k:k + 1, :] * _conv_shift(gv, t, L, 4 - k)
        dx_ref[...] = dx.astype(BF16)
        for k in range(4):
            dw_ref[k:k + 1, :] += _rsum(_conv_shift(x, t, L, k) * gv)
        dw_ref[4:5, :] += _rsum(gv)

    return pl.pallas_call(
        body, name="rg_conv_bwd", grid=(noff, B),
        in_specs=[pl.BlockSpec((L, tc), lambda j, b: (b, noff + j)), pl.BlockSpec((L, tc), lambda j, b: (b, j)),
                  pl.BlockSpec((SUBLANES, tc), lambda j, b: (0, j)), _ANY] + [_ANY] * len(order),
        out_specs=[pl.BlockSpec((L, tc), lambda j, b: (b, noff + j)),
                   pl.BlockSpec((SUBLANES, tc), lambda j, b: (0, j))],
        out_shape=[jax.ShapeDtypeStruct(dz.shape, dz.dtype), jax.ShapeDtypeStruct((SUBLANES, D_MODEL), F32)],
        input_output_aliases={3: 0},
        compiler_params=_params(("parallel", "arbitrary")),
    )(z, g, wb, dz, *order)


def _softplus(x):
    return jnp.maximum(x, 0.0) + jnp.log1p(jnp.exp(-jnp.abs(x)))


_ROW_BA, _ROW_BX, _ROW_LAM = 0, 2, 4


def _gate_math(xb, pre, vec_ref, d, sl):
    pa = pre[:, (2 * d) * LRU_BW:(2 * d + 1) * LRU_BW] + vec_ref[_ROW_BA + d:_ROW_BA + d + 1, sl]
    px = pre[:, (2 * d + 1) * LRU_BW:(2 * d + 2) * LRU_BW] + vec_ref[_ROW_BX + d:_ROW_BX + d + 1, sl]
    r = 0.5 * jnp.tanh(0.5 * pa) + 0.5
    i = 0.5 * jnp.tanh(0.5 * px) + 0.5
    slope = (-RG_C) * _softplus(-vec_ref[_ROW_LAM + d:_ROW_LAM + d + 1, sl])
    log_a = r * slope
    a = jnp.exp(log_a)
    om = -jnp.tanh(log_a) * (1.0 + a * a)
    rs = lax.rsqrt(om)
    mult = jnp.where(om > 0.0, om * rs, 0.0)
    return a, mult * (i * xb), (r, i, slope, om, mult, rs)


def _gate_bwd(rec, du_f, da_f, du_b, da_b, wcat, gvec):
    def fn(ins, bs, outs, accs):
        for blk in range(LRU_BLOCKS):
            sl = slice(blk * LRU_BW, (blk + 1) * LRU_BW)
            xb = ins[0][:, sl]
            xb16 = xb.astype(BF16)
            w = bs[0][sl, :]
            pre = jnp.dot(xb16, w, preferred_element_type=F32)
            dx = jnp.zeros_like(xb)
            dpre = []
            for d in range(2):
                a, _, (r, i, slope, om, mult, rs) = _gate_math(xb, pre, bs[1], d, sl)
                du, da = ins[1 + 2 * d][:, sl], ins[2 + 2 * d][:, sl]
                t = du * xb
                d_i = t * mult
                dx = dx + du * mult * i
                dlog = da * a - (t * i) * ((1.0 - om) * rs)
                d_r = dlog * slope
                d_sp = _rsum(dlog * r) * (-RG_C)
                lam = bs[1][_ROW_LAM + d:_ROW_LAM + d + 1, sl]
                accs[2][_ROW_LAM + d:_ROW_LAM + d + 1, sl] += d_sp * (-jax.nn.sigmoid(-lam))
                dpa = d_r * r * (1.0 - r)
                dpx = d_i * i * (1.0 - i)
                accs[2][_ROW_BA + d:_ROW_BA + d + 1, sl] += _rsum(dpa)
                accs[2][_ROW_BX + d:_ROW_BX + d + 1, sl] += _rsum(dpx)
                dpre += [dpa, dpx]
            dpre = jnp.concatenate(dpre, axis=1).astype(BF16)
            dw = lax.dot_general(xb16, dpre, _TN, preferred_element_type=F32)
            for d in range(2):
                rows = slice(d * D_MODEL + blk * LRU_BW, d * D_MODEL + (blk + 1) * LRU_BW)
                accs[0][rows, :] += dw[:, (2 * d) * LRU_BW:(2 * d + 1) * LRU_BW]
                accs[1][rows, :] += dw[:, (2 * d + 1) * LRU_BW:(2 * d + 2) * LRU_BW]
            outs[0][:, sl] = dx + lax.dot_general(dpre, w, _NT, preferred_element_type=F32)

    gate_shape = (2 * D_MODEL, LRU_BW)
    return _rowwise(fn, [rec, du_f, da_f, du_b, da_b], [wcat, gvec], [(D_MODEL, F32)],
                    [gate_shape, gate_shape, (SUBLANES, D_MODEL)], tm=512, name="rg_gate_bwd")


def _as_time_blocks(x):
    return x.reshape(x.shape[0] // SUBLANES, SUBLANES, x.shape[1])


def _scan_call(body, ins, n_out, B, L, tc, name):
    nb = L // SUBLANES
    spec = pl.BlockSpec((nb, SUBLANES, tc), lambda b, j: (b, 0, j))
    T = ins[0].shape[0]
    outs = pl.pallas_call(
        functools.partial(body, nb), name=name, grid=(B, D_MODEL // tc),
        in_specs=[spec] * len(ins), out_specs=[spec] * n_out,
        out_shape=[jax.ShapeDtypeStruct((T // SUBLANES, SUBLANES, D_MODEL), F32)] * n_out,
        compiler_params=_params(("parallel", "parallel")),
    )(*[_as_time_blocks(x) for x in ins])
    return [o.reshape(T, D_MODEL) for o in outs]


def _block_scan(A, U, reverse):
    row = lax.broadcasted_iota(jnp.int32, A.shape, 0)
    for s in (1, 2, 4):
        shift = SUBLANES - s if reverse else s
        valid = (row < SUBLANES - s) if reverse else (row >= s)
        a_sh = jnp.where(valid, pltpu.roll(A, shift, 0), 1.0)
        u_sh = jnp.where(valid, pltpu.roll(U, shift, 0), 0.0)
        U = A * u_sh + U
        A = A * a_sh
    return A, U


_LAST = SUBLANES - 1
SCAN_UNROLL = 8


def _loop_blocks(nb, step, init):
    def group(g, carry):
        for k in range(SCAN_UNROLL):
            carry = step(g * SCAN_UNROLL + k, carry)
        return carry

    return lax.fori_loop(0, nb // SCAN_UNROLL, group, init)


def _scan_bwd(dy, a_f, h_f, a_b, h_b, B, L, tc=256):
    def body(nb, dy_r, af, hf, ab, hb, duf, daf, dub, dab):
        def step(i, carry):
            c1, c2 = carry
            ir = nb - 1 - i
            row = lax.broadcasted_iota(jnp.int32, (SUBLANES, tc), 0)
            a_up = jnp.where(row == _LAST, af[jnp.minimum(ir + 1, nb - 1), :1, :], pltpu.roll(af[ir], _LAST, 0))
            p, lam = _block_scan(a_up, dy_r[ir], True)
            lam = lam + p * c1
            before = hf[jnp.maximum(ir - 1, 0), _LAST:, :] * (ir > 0).astype(F32)
            duf[ir] = lam
            daf[ir] = lam * jnp.where(row == 0, before, pltpu.roll(hf[ir], 1, 0))
            a_dn = jnp.where(row == 0, ab[jnp.maximum(i - 1, 0), _LAST:, :], pltpu.roll(ab[i], 1, 0))
            p2, lam2 = _block_scan(a_dn, dy_r[i], False)
            lam2 = lam2 + p2 * c2
            after = hb[jnp.minimum(i + 1, nb - 1), :1, :] * (i < nb - 1).astype(F32)
            dub[i] = lam2
            dab[i] = lam2 * jnp.where(row == _LAST, after, pltpu.roll(hb[i], _LAST, 0))
            return lam[:1, :], lam2[_LAST:, :]

        zero = jnp.zeros((1, tc), F32)
        _loop_blocks(nb, step, (zero, zero))

    return _scan_call(body, [dy, a_f, h_f, a_b, h_b], 4, B, L, tc, "rg_scan_bwd")


_GELU_C = math.sqrt(2.0 / math.pi)


def _gelu_parts(x):
    th = jnp.tanh(_GELU_C * (x + 0.044715 * x * x * x))
    return 0.5 * x * (1.0 + th), th


def _mm_gated_out_bwd(dx, w_out, h_f, h_b, z, name, after=None):
    def epilogue(acc, e_refs, b_refs, o_refs, a_refs):
        x = e_refs[2][...]
        gl, th = _gelu_parts(x)
        dgl = 0.5 * (1.0 + th) + 0.5 * x * (1.0 - th * th) * (_GELU_C * (1.0 + 3.0 * 0.044715 * x * x))
        o_refs[0][...] = acc * gl
        o_refs[1][...] = (acc * (e_refs[0][...] + e_refs[1][...]) * dgl).astype(BF16)

    return _mm(dx, w_out, mode="nt", out_dtypes=(F32, BF16), out_cols=(D_MODEL, 2 * D_MODEL), extras=(h_f, h_b, z),
               ref_epi=epilogue, name=name, after=after)


def _row_block(i):
    return pl.ds(pl.multiple_of(i * SUBLANES, SUBLANES), SUBLANES)


def _rg_mix_fwd(z, conv_wb, wcat, gvec, B, L):
    nb = L // SUBLANES
    n_g = D_MODEL // LRU_BW

    def body(zg_ref, zr_ref, cw_ref, w_ref, gv_ref, rec_ref, af_s, ab_s, hf_ref, hb_ref, yg_ref, uf_s, ub_s):
        rec = _conv_apply(zr_ref[...], cw_ref, L)
        rec_ref[...] = rec
        pre = jnp.dot(rec.astype(BF16), w_ref[...], preferred_element_type=F32)
        for d, (a_s, u_s) in enumerate(((af_s, uf_s), (ab_s, ub_s))):
            a, u, _ = _gate_math(rec, pre, gv_ref, d, slice(None))
            a_s[...] = a
            u_s[...] = u

        def step(i, carry):
            c1, c2 = carry
            rows, rows_b = _row_block(i), _row_block(nb - 1 - i)
            p, h = _block_scan(af_s[rows, :], uf_s[rows, :], False)
            h = h + p * c1
            hf_ref[rows, :] = h
            p2, h2 = _block_scan(ab_s[rows_b, :], ub_s[rows_b, :], True)
            h2 = h2 + p2 * c2
            hb_ref[rows_b, :] = h2
            return h[_LAST:, :], h2[:1, :]

        zero = jnp.zeros((1, LRU_BW), F32)
        _loop_blocks(nb, step, (zero, zero))
        gl, _ = _gelu_parts(zg_ref[...])
        yg_ref[...] = ((hf_ref[...] + hb_ref[...]) * gl).astype(BF16)

    seq = lambda off: pl.BlockSpec((L, LRU_BW), lambda b, g: (b, off + g))
    vec = pl.BlockSpec((SUBLANES, LRU_BW), lambda b, g: (0, g))
    T = B * L
    return pl.pallas_call(
        body, name="rg_mix", grid=(B, n_g),
        in_specs=[seq(0), seq(n_g), vec, pl.BlockSpec((LRU_BW, 4 * LRU_BW), lambda b, g: (g, 0)), vec],
        out_specs=[seq(0)] * 6,
        out_shape=[jax.ShapeDtypeStruct((T, D_MODEL), F32)] * 5 + [jax.ShapeDtypeStruct((T, D_MODEL), BF16)],
        scratch_shapes=[pltpu.VMEM((L, LRU_BW), F32)] * 2,
        compiler_params=_params(("parallel", "parallel")),
    )(z, z, conv_wb, wcat, gvec)


def _make_wcat(w_a, w_x):
    g = jnp.stack([w_a[0, 0], w_x[0, 0], w_a[0, 1], w_x[0, 1]])
    return jnp.transpose(g, (1, 2, 0, 3)).reshape(D_MODEL, 4 * LRU_BW)


def _rows_at(part, first):
    return jnp.pad(part, ((first, SUBLANES - first - part.shape[0]), (0, 0)))


def _qk_slot(q_g, k_g):
    wide = lambda v, at: jnp.pad(v, ((0, SUBLANES - 1), (at, D_MODEL - at - HEAD_DIM)))
    return wide(q_g, 0) + wide(k_g, HEAD_DIM)


def _local_step(x, target, P, fetch, emit, B, L, after=None):
    g_mix, g_mlp = P["norm_mix_g"], P["norm_mlp_g"]
    h0 = _rms_fwd(x, g_mix[0:1], "rg_norm", after=after)
    w_in, conv_wb, wcat, gvec = fetch("rg", h0)
    z = _mm(h0, w_in, mode="nn", b_shard=True, name="rg_in")
    rec, a_f, a_b, h_f, h_b, yg = _rg_mix_fwd(z, conv_wb, wcat, gvec, B, L)
    w_out = fetch("rg_out", yg)
    x1, h1 = _mm_res_norm(yg, w_out, x, g_mlp[0:1], "rg_out")
    (x2, h3), mlp0 = _mlp_fwd(x1, h1, fetch, 0, lambda a, w, res, name: _mm_res_norm(a, w, res, g_mix[1:2], name))
    w_qkv, w_o = fetch("att", h3)
    qkv = _mm(h3, w_qkv, mode="nn", b_shard=True, name="attn_qkv")
    cos, sin = _rope_tables(L, B)
    qh, kh, vh = _qk_prep(qkv, cos, sin, P["q_g"], P["k_g"])
    o = _attn_fwd(qh, kh, vh, B, L)
    x3, h4 = _mm_res_norm(o, w_o, x2, g_mlp[1:2], "attn_out")
    (dx4, dx4_bf, loss_acc, d_final_g), mlp1 = _mlp_fwd(
        x3, h4, fetch, 1, lambda a, w, res, name: _mm_final_loss(a, w, res, target, P["final_g"], name))

    dx3, dx3_bf, dg_mlp1, d_up1, d_down1 = _mlp_bwd(x3, g_mlp[1:2], mlp1, dx4, dx4_bf, 1, None)
    tok = emit("mlp1", [d_up1, d_down1])
    d_wo = _mm(o, dx3_bf, mode="tn", out_dtypes=(BF16,), name="attn_dwo", after=tok)
    do = _mm(dx3_bf, w_o, mode="nt", out_dtypes=(BF16,), name="attn_do")
    dq, dk, dv = _attn_bwd(qh, kh, vh, o, do, B, L)
    dqkv, dq_g, dk_g = _qk_prep_bwd(qkv, dq, dk, dv, cos, sin, P["q_g"], P["k_g"])
    d_wqkv = _mm(h3, dqkv, mode="tn", o_shard=True, out_dtypes=(BF16,), name="attn_dwqkv")
    tok = emit("att", [d_wqkv, d_wo])
    dx2, dx2_bf, dg_mix1 = _mm_norm_bwd(dqkv, w_qkv, x2, dx3, g_mix[1:2], "attn_dh", after=tok)
    tok = emit("point_attn_done", [dx2_bf])
    dx1, dx1_bf, dg_mlp0, d_up0, d_down0 = _mlp_bwd(x1, g_mlp[0:1], mlp0, dx2, dx2_bf, 0, tok)
    d_wout = _mm(yg, dx1_bf, mode="tn", out_dtypes=(BF16,), name="rg_dwout")
    tok = emit("mlp0", [d_up0, d_down0, d_wout])
    dy, dgate = _mm_gated_out_bwd(dx1_bf, w_out, h_f, h_b, z, "rg_dyg", after=tok)
    du_f, da_f, du_b, da_b = _scan_bwd(dy, a_f, h_f, a_b, h_b, B, L)
    drec_c, d_wa, d_wx, d_gvec = _gate_bwd(rec, du_f, da_f, du_b, da_b, wcat, gvec)
    tok = emit("gates", [d_wa, d_wx])
    dz, d_convwb = _conv_bwd(z, drec_c, conv_wb, dgate, B, L, after=tok)
    tok = emit("point_mix_done", [dz])
    d_win = _mm(h0, dz, mode="tn", o_shard=True, out_dtypes=(BF16,), name="rg_dwin", after=tok)
    tok = emit("rg_in", [d_win])
    grad_x, _, dg_mix0 = _mm_norm_bwd(dz, w_in, x, dx1, g_mix[0:1], "rg_dh", after=tok)

    norms = (_rows_at(dg_mix0, 0) + _rows_at(dg_mix1, 1) + _rows_at(dg_mlp0, 2) + _rows_at(dg_mlp1, 3)
             + _rows_at(d_final_g, 4)
             + jnp.pad(loss_acc, ((LOSS_ROW, SUBLANES - 1 - LOSS_ROW), (0, D_MODEL - LANES))))
    vec = jnp.concatenate([norms, d_convwb, d_gvec, _qk_slot(dq_g, dk_g)], axis=0)
    return grad_x, vec


_MESH = pl.DeviceIdType.MESH


def _place():
    x, y, c = lax.axis_index("x"), lax.axis_index("y"), lax.axis_index("c")
    peers = [((1 - x) if j & 2 else x, (1 - y) if j & 1 else y) for j in (1, 2, 3)]
    return x, y, c, peers


def _sum_leading(slots, name):
    def body(s_ref, o_ref):
        acc = s_ref[0]
        for d in range(1, slots.shape[0]):
            acc = acc + s_ref[d]
        o_ref[...] = acc

    return pl.pallas_call(body, name=name, out_shape=jax.ShapeDtypeStruct(slots.shape[1:], slots.dtype))(slots)


_HBM = pl.BlockSpec(memory_space=pltpu.HBM)
_SEM = pl.BlockSpec(memory_space=pltpu.SEMAPHORE)
_EFFECT = pltpu.SideEffectType.DATAFLOW_SIDE_EFFECTING


_COPIES = dict(gather=N_CHIPS - 1, scatter=N_CHIPS - 1, swap=1, spread=N_DEVICES - 1,
               gather_half=N_CHIPS - 1, share_half=N_CHIPS - 1)


def _split_copies(kind, srcs, lands, send, recv):
    x, y, c, peers = _place()
    me = 2 * x + y
    per = _COPIES[kind]
    out = []
    for a in range(len(lands)):
        for j in range(per):
            if kind == "swap":
                src, there, here, dev = srcs[a], lands[a], lands[a], (x, y, 1 - c)
            elif kind == "spread":
                k = j + 1
                dev = ((1 - x) if k & 4 else x, (1 - y) if k & 2 else y, (1 - c) if k & 1 else c)
                mine = lands[a].at[4 * x + 2 * y + c]
                src, there, here = mine, mine, lands[a].at[4 * dev[0] + 2 * dev[1] + dev[2]]
            else:
                px, py = peers[j]
                dev = (px, py, c)
                if kind == "gather":
                    src, there, here = lands[a].at[me], lands[a].at[me], lands[a].at[2 * px + py]
                elif kind in ("gather_half", "share_half"):
                    half = lands[a].shape[1] // 2
                    mine, other = pl.ds(c * half, half), pl.ds((1 - c) * half, half)
                    if kind == "gather_half":
                        src = there = lands[a].at[me, mine]
                        here = lands[a].at[2 * px + py, mine]
                    else:
                        src = there = lands[a].at[2 * px + py, mine]
                        here = lands[a].at[2 * px + py, other]
                        dev = (x, y, 1 - c)
                else:
                    src, there, here = srcs[a].at[2 * px + py], lands[a].at[j], lands[a].at[j]
            mk = functools.partial(
                pltpu.make_async_remote_copy, src_ref=src, send_sem=send.at[per * a + j],
                recv_sem=recv.at[per * a + j], device_id=dev, device_id_type=_MESH)
            out.append((functools.partial(mk, dst_ref=there), functools.partial(mk, dst_ref=here)))
    return out


_CORE_PAIR = ("swap", "share_half")
CORE_PAIR_BARRIER_ID = 0
BARRIER_IDS = dict(gather_rest=1, scatter_mlp1=2, scatter_att=3, scatter_mlp0=4, scatter_gates=5, scatter_rg_in=6,
                   gather_gates=7, spread_vec=8)


def _entry_peers(kind):
    x, y, c, peers = _place()
    if kind in _CORE_PAIR:
        return [(x, y, 1 - c)]
    if kind == "spread":
        return [((1 - x) if k & 4 else x, (1 - y) if k & 2 else y, (1 - c) if k & 1 else c)
                for k in range(1, N_DEVICES)]
    return [(px, py, c) for px, py in peers]


def _entry_params(kind, barrier_id):
    if kind in _CORE_PAIR:
        barrier_id = CORE_PAIR_BARRIER_ID
    collective = {} if barrier_id is None else dict(collective_id=barrier_id)
    return pltpu.CompilerParams(has_side_effects=_EFFECT, **collective)


def _entry_handshake(kind, barrier_id):
    if kind in _CORE_PAIR or barrier_id is not None:
        barrier = pltpu.get_barrier_semaphore()
        peers = _entry_peers(kind)
        for peer in peers:
            pl.semaphore_signal(barrier, inc=1, device_id=peer, device_id_type=_MESH)
        pl.semaphore_wait(barrier, len(peers))


def _exchange_start(kind, srcs, lands, name, after=None, barrier_id=None):
    arrays = list(srcs) + list(lands)
    n_s, n, n_all = len(srcs), len(lands), len(srcs) + len(lands)
    n_sem = _COPIES[kind] * n
    order = _after_operand(after)
    n_x = len(order)

    def body(*refs):
        _entry_handshake(kind, barrier_id)
        send, recv = refs[n_all + n_x], refs[n_all + n_x + 1]
        token = refs[-1]
        for started, _ in _split_copies(kind, refs[:n_s], refs[n_s:n_all], send, recv):
            started().start()
        token[...] = jnp.zeros(token.shape, F32)

    res = pl.pallas_call(
        body, name=name,
        out_shape=(pltpu.SemaphoreType.DMA((n_sem,)), pltpu.SemaphoreType.DMA((n_sem,)),
                   *[pltpu.HBM(a.shape, a.dtype) for a in arrays], jax.ShapeDtypeStruct((SUBLANES, LANES), F32)),
        in_specs=[_HBM] * n_all + [_ANY] * n_x,
        out_specs=(_SEM, _SEM, *[_HBM] * n_all, pl.BlockSpec(memory_space=pltpu.VMEM)),
        input_output_aliases={i: 2 + i for i in range(n_all)},
        compiler_params=_entry_params(kind, barrier_id),
    )(*[pltpu.with_memory_space_constraint(a, pltpu.HBM) for a in arrays], *order)
    return (res[0], res[1], res[2:2 + n_s], res[2 + n_s:2 + n_all]), res[-1]


def _gather_start_groups(land_groups, name, after=None, kind="gather", barrier_id=None):
    arrays = [a for group in land_groups for a in group]
    n_all, n_g = len(arrays), len(land_groups)
    order = _after_operand(after)
    n_x = len(order)

    def body(*refs):
        _entry_handshake(kind, barrier_id)
        first = 0
        for gi, group in enumerate(land_groups):
            send, recv = refs[n_all + n_x + 2 * gi], refs[n_all + n_x + 2 * gi + 1]
            for started, _ in _split_copies(kind, [], refs[first:first + len(group)], send, recv):
                started().start()
            first += len(group)
        refs[-1][...] = jnp.zeros(refs[-1].shape, F32)

    sems = [pltpu.SemaphoreType.DMA((_COPIES[kind] * len(group),)) for group in land_groups for _ in range(2)]
    res = pl.pallas_call(
        body, name=name,
        out_shape=(*sems, *[pltpu.HBM(a.shape, a.dtype) for a in arrays], jax.ShapeDtypeStruct((SUBLANES, LANES), F32)),
        in_specs=[_HBM] * n_all + [_ANY] * n_x,
        out_specs=(*[_SEM] * (2 * n_g), *[_HBM] * n_all, pl.BlockSpec(memory_space=pltpu.VMEM)),
        input_output_aliases={i: 2 * n_g + i for i in range(n_all)},
        compiler_params=_entry_params(kind, barrier_id),
    )(*[pltpu.with_memory_space_constraint(a, pltpu.HBM) for a in arrays], *order)
    handles, first = [], 2 * n_g
    for gi, group in enumerate(land_groups):
        handles.append((res[2 * gi], res[2 * gi + 1], [], res[first:first + len(group)]))
        first += len(group)
    return handles, res[-1]


def _exchange_wait(kind, handle, after, name):
    send, recv, srcs, lands = handle
    arrays = list(srcs) + list(lands)
    n_s, n_all = len(srcs), len(arrays)
    order = list(after) if isinstance(after, (list, tuple)) else [after]

    def body(*refs):
        for started, landing in _split_copies(kind, refs[:n_s], refs[n_s:n_all], refs[n_all], refs[n_all + 1]):
            started().wait_send()
            landing().wait_recv()

    res = pl.pallas_call(
        body, name=name, out_shape=[pltpu.HBM(a.shape, a.dtype) for a in arrays],
        in_specs=[_HBM] * n_all + [_SEM, _SEM] + [_ANY] * len(order), out_specs=[_HBM] * n_all,
        input_output_aliases={i: i for i in range(n_all)},
        compiler_params=pltpu.CompilerParams(has_side_effects=_EFFECT),
    )(*arrays, send, recv, *order)
    return res[:n_s], res[n_s:]


def _index_operand(i):
    return jnp.reshape(i, (1,)).astype(jnp.int32)


def _cast_into_slot(src, row0, rows, me, dtype, name, after=None, add=None, n_slots=N_CHIPS):
    cols = src.shape[1]
    tm = min(512, rows)
    order = _after_operand(after)
    terms = [src] + ([] if add is None else [add])

    def body(me_ref, *rest):
        val = rest[0][...]
        if add is not None:
            val = val + rest[1][...]
        rest[-1][...] = val.astype(dtype)

    return pl.pallas_call(
        body, name=name,
        grid_spec=pltpu.PrefetchScalarGridSpec(
            num_scalar_prefetch=1, grid=(rows // tm,),
            in_specs=[pl.BlockSpec((tm, cols), lambda i, me_ref: (i + row0 // tm, 0))] * len(terms)
            + [_ANY] * len(order),
            out_specs=pl.BlockSpec((None, tm, cols), lambda i, me_ref: (me_ref[0], i, 0))),
        out_shape=jax.ShapeDtypeStruct((n_slots, rows, cols), dtype), compiler_params=_params(("parallel",)),
    )(_index_operand(me), *terms, *order)


def _sum_slots(mine, r, me, name):
    _, rows, cols = r.shape
    tm = min(512, rows)

    def body(me_ref, own_ref, r_ref, o_ref):
        o_ref[...] = ((own_ref[...].astype(F32) + r_ref[0].astype(F32)) + r_ref[1].astype(F32)) + r_ref[2].astype(F32)

    return pl.pallas_call(
        body, name=name,
        grid_spec=pltpu.PrefetchScalarGridSpec(
            num_scalar_prefetch=1, grid=(rows // tm,),
            in_specs=[pl.BlockSpec((None, tm, cols), lambda i, me_ref: (me_ref[0], i, 0)),
                      pl.BlockSpec((N_CHIPS - 1, tm, cols), lambda i, me_ref: (0, i, 0))],
            out_specs=pl.BlockSpec((tm, cols), lambda i, me_ref: (i, 0))),
        out_shape=jax.ShapeDtypeStruct((rows, cols), F32), compiler_params=_params(("parallel",)),
    )(_index_operand(me), mine, r)


def _adamw(w, m, v, ps, qs, name):
    rows, cols = w.shape
    seg_rows = ps[0].shape[0]
    tm = min(512, seg_rows)
    while seg_rows % tm:
        tm -= SUBLANES
    per, n_seg = seg_rows // tm, len(ps)
    parts = list(ps) + ([] if qs is None else list(qs))

    def body(w_ref, m_ref, v_ref, *rest):
        g_refs, outs = rest[:len(parts)], rest[len(parts):]
        grad = lambda s: g_refs[s][...] if qs is None else g_refs[s][...] + g_refs[n_seg + s][...]
        g = grad(0)
        for s in range(1, n_seg):
            g = jnp.where(pl.program_id(0) >= s * per, grad(s), g)
        m1 = ADAM_B1 * m_ref[...] + (1.0 - ADAM_B1) * g
        v1 = ADAM_B2 * v_ref[...] + (1.0 - ADAM_B2) * (g * g)
        m_hat = m1 / (1.0 - ADAM_B1 ** ADAM_STEP)
        v_hat = v1 / (1.0 - ADAM_B2 ** ADAM_STEP)
        outs[0][...] = g
        outs[1][...] = (-ADAM_LR) * (m_hat / (jnp.sqrt(v_hat) + ADAM_EPS) + ADAM_WD * w_ref[...])
        outs[2][...] = m1
        outs[3][...] = v1

    row_spec = pl.BlockSpec((tm, cols), lambda i: (i, 0))
    seg_spec = lambda s: pl.BlockSpec((tm, cols), lambda i: (jnp.clip(i - s * per, 0, per - 1), 0))
    return pl.pallas_call(
        body, name=name, grid=(rows // tm,),
        in_specs=[row_spec] * 3 + [seg_spec(s) for s in range(n_seg)] * (1 if qs is None else 2),
        out_specs=[row_spec] * 4, out_shape=[jax.ShapeDtypeStruct((rows, cols), F32)] * 4,
        compiler_params=_params(("arbitrary",)),
    )(w, m, v, *parts)


def _put_cols(shard, me):
    full = jnp.zeros((shard.shape[0], D_MODEL), F32)
    return lax.dynamic_update_slice(full, shard, (0, me * (D_MODEL // N_CHIPS)))


def _gate_vec_slot(b_a, b_x, lam):
    return _rows_at(b_a, _ROW_BA) + _rows_at(b_x, _ROW_BX) + _rows_at(lam, _ROW_LAM)


def _pack_vec(p, me):
    return jnp.concatenate([
        _rows_at(p["norm_mix_g"], 0) + _rows_at(p["norm_mlp_g"], 2) + _rows_at(p["final_g"][None], 4),
        _rows_at(_put_cols(p["rg_conv_w"][0, :, 0, :], me), 0) + _rows_at(p["rg_conv_b"], 4),
        _gate_vec_slot(_put_cols(p["rg_b_a"][0], me), _put_cols(p["rg_b_x"][0], me), _put_cols(p["rg_lam"][0], me)),
        _qk_slot(p["at_q_g"], p["at_k_g"]),
    ], axis=0)


def _unpack_vec(r, me):
    def cols(rows):
        return lax.dynamic_slice(rows, (0, me * (D_MODEL // N_CHIPS)), (rows.shape[0], D_MODEL // N_CHIPS))

    gate = r[16:24]
    return dict(
        norm_mix_g=r[0:2], norm_mlp_g=r[2:4], final_g=r[4], rg_conv_w=cols(r[8:12])[None, :, None, :],
        rg_conv_b=r[12:13], rg_b_a=cols(gate[_ROW_BA:_ROW_BA + 2])[None], rg_b_x=cols(gate[_ROW_BX:_ROW_BX + 2])[None],
        rg_lam=cols(gate[_ROW_LAM:_ROW_LAM + 2])[None], at_q_g=r[24:25, 0:HEAD_DIM],
        at_k_g=r[24:25, HEAD_DIM:2 * HEAD_DIM])


_WEIGHTS = ['norm_mix_g', 'norm_mlp_g', 'rg_w_in', 'rg_conv_w', 'rg_conv_b', 'rg_w_a', 'rg_b_a', 'rg_w_x', 'rg_b_x',
            'rg_lam', 'rg_w_out', 'at_w_qkv', 'at_q_g', 'at_k_g', 'at_w_o', 'mlp_w_up', 'mlp_w_down', 'final_g']
_BIG = dict(rg_w_in=["rg_w_in"], rg_w_out=["rg_w_out"], at_w_qkv=["at_w_qkv"], at_w_o=["at_w_o"],
            mlp_w_up=["up0", "up1"], mlp_w_down=["down0", "down1"])


def kernel(x, *args):
    n_w = len(_WEIGHTS)
    w = dict(zip(_WEIGHTS, args[:n_w]))
    target = args[n_w]
    m = dict(zip(_WEIGHTS, args[n_w + 1:2 * n_w + 1]))
    v = dict(zip(_WEIGHTS, args[2 * n_w + 1:3 * n_w + 1]))
    B, L, _ = x.shape
    T = B * L
    me = 2 * lax.axis_index("x") + lax.axis_index("y")

    vec = jnp.concatenate([_gate_vec_slot(w["rg_b_a"][0], w["rg_b_x"][0], w["rg_lam"][0]),
                           _rows_at(w["rg_conv_w"][0, :, 0, :], 0)], axis=0)
    flat = lambda a: a.reshape(-1, a.shape[-1])
    rows_of = lambda k: w[k].shape[-2]
    groups = [("rg", [("rg_w_in", 0, BF16), (vec, 0, F32)]), ("rg_out", [("rg_w_out", 0, BF16)]),
              ("mlp0_up", [("mlp_w_up", 0, BF16)]), ("mlp0_down", [("mlp_w_down", 0, BF16)]),
              ("att", [("at_w_qkv", 0, BF16), ("at_w_o", 0, BF16)]),
              ("mlp1", [("mlp_w_up", 1, BF16), ("mlp_w_down", 1, BF16)])]

    def landing_zones(group, members, after):
        lands = []
        for n, (k, layer, dtype) in enumerate(members):
            src, rows = (flat(w[k]), rows_of(k)) if isinstance(k, str) else (k, k.shape[0])
            lands.append(_cast_into_slot(src, layer * rows, rows, me, dtype, f"place_{group}{n}", after=after))
        return lands

    halves, gathers = {}, {}
    halves["rg"], tok = _exchange_start("gather_half", [], landing_zones(*groups[0], None), "gather_rg_start")
    handles, tok = _gather_start_groups([landing_zones(g, members, tok) for g, members in groups[1:]],
                                        "gather_rest_start", after=tok, kind="gather_half",
                                        barrier_id=BARRIER_IDS["gather_rest"])
    halves.update(zip([g for g, _ in groups[1:]], handles))
    wcat = _make_wcat(w["rg_w_a"], w["rg_w_x"]).astype(BF16)

    packs = [_pack_vec(p, me) for p in (w, m, v)]

    ready = {}

    def share(some, after, name):
        landed = [_exchange_wait("gather_half", halves[g], after, f"gather_{g}_landed")[1] for g in some]
        handles, token = _gather_start_groups(landed, name, kind="share_half")
        gathers.update(zip(some, handles))
        return token

    tok = share(["rg"], [tok, wcat] + packs, "share_rg_start")

    def fetch(what, after):
        if what in ready:
            return ready[what]
        group = "mlp1" if what.startswith("mlp1") else what
        if group == "rg_out":
            share(["rg_out", "mlp0_up", "mlp0_down", "att"], after, "share_early_start")
        _, full = _exchange_wait("share_half", gathers[group], after, f"gather_{group}_wait")
        if group == "att":
            share(["mlp1"], after, "share_mlp1_start")
        if group == "rg":
            vec_full = jnp.transpose(full[1], (1, 0, 2)).reshape(2 * SUBLANES, D_MODEL)
            conv_wb = vec_full[SUBLANES:] + _rows_at(w["rg_conv_b"], 4)
            return full[0], conv_wb, wcat, vec_full[:SUBLANES]
        if group == "rg_out":
            return full[0].reshape(D_MODEL, D_MODEL)
        if group == "att":
            return full[0], full[1].reshape(D_MODEL, D_MODEL)
        if group == "mlp1":
            ready["mlp1_up"], ready["mlp1_down"] = full[0], full[1].reshape(4 * D_MODEL, D_MODEL)
            return ready[what]
        return full[0] if group == "mlp0_up" else full[0].reshape(4 * D_MODEL, D_MODEL)

    names = dict(mlp1=["up1", "down1"], att=["at_w_qkv", "at_w_o"], mlp0=["up0", "down0", "rg_w_out"],
                 rg_in=["rg_w_in"], gates=["rg_w_a", "rg_w_x"])
    scatters, swaps, P, Q, res = {}, [], {}, {}, {}

    def start_scatter(group, grads):
        srcs = [g.reshape(N_CHIPS, -1, g.shape[-1]) for g in grads]
        lands = [lax.empty((N_CHIPS - 1,) + s.shape[1:], s.dtype) for s in srcs]
        scatters[group], token = _exchange_start("scatter", srcs, lands, f"scatter_{group}_start",
                                                 barrier_id=BARRIER_IDS[f"scatter_{group}"])
        return token

    def settle(groups, after):
        keys, parts = [], []
        for group in groups:
            srcs, lands = _exchange_wait("scatter", scatters[group], after, f"scatter_{group}_wait")
            for k, s, r in zip(names[group], srcs, lands):
                keys.append(k)
                parts.append(_sum_slots(s, r, me, f"sum_{k}"))
        handle, token = _exchange_start("swap", parts, [lax.empty(p.shape, F32) for p in parts],
                                        f"swap_{groups[0]}_start")
        swaps.append((keys, handle, f"swap_{groups[0]}_wait"))
        return token

    def finish(after):
        for keys, handle, name in swaps:
            mine, theirs = _exchange_wait("swap", handle, after, name)
            P.update(zip(keys, mine))
            Q.update(zip(keys, theirs))
        swaps.clear()
        last = after
        for k, parts in _BIG.items():
            if k in res or any(p not in P for p in parts):
                continue
            shape = w[k].shape
            two_d = lambda a: a.reshape(-1, shape[-1])
            outs = _adamw(two_d(w[k]), two_d(m[k]), two_d(v[k]), [P[p] for p in parts], [Q[p] for p in parts],
                          f"adamw_{k}")
            res[k] = [o.reshape(shape) for o in outs]
            last = outs[0]
        if "rg_w_a" in P and "gates" not in gathers:
            lands = [_cast_into_slot(P[k], 0, P[k].shape[0], me, F32, f"place_{k}", after=last, add=Q[k])
                     for k in names["gates"]]
            gathers["gates"], last = _exchange_start("gather", [], lands, "gather_gates_start", after=last,
                                                     barrier_id=BARRIER_IDS["gather_gates"])
        return last

    def emit(event, arrays):
        if event == "point_attn_done":
            return None
        if event == "point_mix_done":
            return settle(["mlp1", "att", "mlp0"], arrays[0])
        token = start_scatter(event, arrays)
        if event == "rg_in":
            return finish(settle(["gates"], token))
        return token

    P_vec = dict(norm_mix_g=w["norm_mix_g"], norm_mlp_g=w["norm_mlp_g"], final_g=w["final_g"][None],
                 q_g=w["at_q_g"], k_g=w["at_k_g"])
    grad_x, vec_part = _local_step(x.reshape(T, D_MODEL), target.reshape(T, D_MODEL), P_vec, fetch, emit, B, L,
                                   after=tok)

    me8 = 2 * me + lax.axis_index("c")
    vec_slots = _cast_into_slot(vec_part, 0, VEC_ROWS, me8, F32, "place_vec", n_slots=N_DEVICES)
    spread, tok = _exchange_start("spread", [], [vec_slots], "spread_vec_start", barrier_id=BARRIER_IDS["spread_vec"])
    last = settle(["rg_in"], tok)
    _, gate_grads = _exchange_wait("gather", gathers["gates"], last, "gather_gates_wait")
    for k, g in zip(names["gates"], gate_grads):
        two_d = lambda a: a.reshape(g.shape[0] * g.shape[1], g.shape[2])
        outs = _adamw(two_d(w[k]), two_d(m[k]), two_d(v[k]), [two_d(g)], None, f"adamw_{k}")
        res[k] = [o.reshape(w[k].shape) for o in outs]
        last = outs[0]
    _, (vec_all,) = _exchange_wait("spread", spread, last, "spread_vec_wait")
    vec_grad = _sum_leading(vec_all, "sum_vec")
    loss = vec_grad[LOSS_ROW, 0]
    outs = _adamw(*packs, [vec_grad], None, "adamw_vec")
    finish(outs[0])
    unpacked = [_unpack_vec(o, me) for o in outs]
    for k in _WEIGHTS:
        if k not in res:
            res[k] = [u[k] for u in unpacked]

    result = [loss, grad_x.reshape(B, L, D_MODEL)]
    for slot in range(4):
        result += [res[k][slot] for k in _WEIGHTS]
    return tuple(result)
```

```python
import functools
import math

import jax
import jax.numpy as jnp
import numpy as np
from jax import lax
from jax.experimental import pallas as pl
from jax.experimental.pallas import tpu as pltpu

F32 = jnp.float32
BF16 = jnp.bfloat16

D_MODEL = 1024
HEAD_DIM = 128
N_HEADS = 8
N_KV = 2
GROUP = N_HEADS // N_KV
LRU_BLOCKS = 8
LRU_BW = 128
GRID_W = 64
ROPE_THETA = 10000.0
EPS = 1e-6
RG_C = 8.0
SCALE = 1.0 / math.sqrt(HEAD_DIM)
N_CHIPS = 4

ADAM_LR = 0.001
ADAM_B1 = 0.9
ADAM_B2 = 0.999
ADAM_EPS = 1e-08
ADAM_WD = 0.01
ADAM_STEP = 10

V7X_VMEM_BYTES = 64 * 1024 * 1024
VMEM_LIMIT = V7X_VMEM_BYTES * 3 // 4
LANES = 128
SUBLANES = 8

N_DEVICES = 8
VEC_ROWS = 32
LOSS_ROW = 5


def _params(sem):
    return pltpu.CompilerParams(dimension_semantics=sem, vmem_limit_bytes=VMEM_LIMIT)


_ANY = pl.BlockSpec(memory_space=pl.ANY)
_NN = (((1,), (0,)), ((), ()))
_NT = (((1,), (1,)), ((), ()))
_TN = (((0,), (0,)), ((), ()))


def _after_operand(after):
    return [] if after is None else [after]


def _fit(t, n):
    if n <= t:
        return n
    c = (t // LANES) * LANES
    while n % c:
        c -= LANES
    return c


MM_VMEM_BUDGET = VMEM_LIMIT * 3 // 4
def _mm_tiles(M, K, ns, n_total, out_dtypes, extras, whole_rows):
    for tm in (2048, 1024, 512, 256, 128):
        for tn in ((ns,) if whole_rows else (1024, 512, 256)):
            tn = _fit(tn, ns)
            per_row = 2 * (2 * K) + 4 * tn + sum(2 * tn * jnp.dtype(d).itemsize for d in out_dtypes)
            per_row += sum(2 * tn * e.dtype.itemsize for e in extras)
            b_buffers = 1 if tn == n_total else 2
            if M % tm == 0 and b_buffers * (2 * K * tn) + tm * per_row <= MM_VMEM_BUDGET:
                return tm, tn
    raise ValueError(f"no tile fits VMEM for M={M} K={K} N={ns}")


def _mm(a, b, *, mode, name, out_dtypes=(F32,), b_shard=False, o_shard=False, extras=(), epi=None, after=None,
        bcast=(), accs=(), ref_epi=None, out_cols=None):
    if mode == "tn":
        K, M = a.shape
        N = b.shape[1]
    else:
        M, K = a.shape
        if mode == "nn":
            N = b.shape[0] * b.shape[2] if b_shard else b.shape[1]
        else:
            N = b.shape[1] if b_shard else b.shape[0]
    ns = N
    if b_shard and mode == "nn":
        ns = b.shape[2]
    elif o_shard:
        ns = N // N_CHIPS
    tm, tn = _mm_tiles(M, K, ns, N, out_dtypes, extras, whole_rows=ref_epi is not None)
    if ref_epi is not None:
        tm = min(tm, 512)
    grid = (M // tm, N // tn)
    q = ns // tn
    once = dict(pipeline_mode=pl.Buffered(1)) if tn == N else {}

    if mode == "tn":
        a_spec = pl.BlockSpec((K, tm), lambda i, j: (0, i))
        b_spec = pl.BlockSpec((K, tn), lambda i, j: (0, j), **once)
        dims = _TN
    elif mode == "nn":
        a_spec = pl.BlockSpec((tm, K), lambda i, j: (i, 0))
        if b_shard:
            b_spec = pl.BlockSpec((None, K, tn), lambda i, j: (j // q, 0, j % q), **once)
        else:
            b_spec = pl.BlockSpec((K, tn), lambda i, j: (0, j), **once)
        dims = _NN
    else:
        a_spec = pl.BlockSpec((tm, K), lambda i, j: (i, 0))
        if b_shard:
            ks = b.shape[2]
            b_spec = pl.BlockSpec((N_CHIPS, tn, ks), lambda i, j: (0, j, 0), **once)
        else:
            b_spec = pl.BlockSpec((tn, K), lambda i, j: (j, 0), **once)
        dims = _NT

    if o_shard:
        o_specs = [pl.BlockSpec((None, tm, tn), lambda i, j: (j // q, i, j % q))]
        o_shapes = [jax.ShapeDtypeStruct((N_CHIPS, M, ns), out_dtypes[0])]
    else:
        o_specs = [pl.BlockSpec((tm, tn), lambda i, j: (i, j)) for _ in out_dtypes]
        o_shapes = [jax.ShapeDtypeStruct((M, N if out_cols is None else out_cols[n]), dt)
                    for n, dt in enumerate(out_dtypes)]
    e_specs = [pl.BlockSpec((tm, tn), lambda i, j: (i, j)) for _ in extras]
    e_specs += [pl.BlockSpec(v.shape, lambda i, j: (0, 0)) for v in bcast]
    o_specs += [pl.BlockSpec(s, lambda i, j: (0, 0)) for s in accs]
    o_shapes += [jax.ShapeDtypeStruct(s, F32) for s in accs]
    n_e, n_b, n_o, n_a = len(extras), len(bcast), len(out_dtypes), len(accs)
    order = _after_operand(after)
    n_x = len(order)
    if epi is None:
        epi = lambda acc: (acc,)

    def body(a_ref, b_ref, *rest):
        e_refs, b_refs = rest[:n_e], rest[n_e:n_e + n_b]
        o_refs = rest[n_e + n_b + n_x:n_e + n_b + n_x + n_o]
        a_refs = rest[n_e + n_b + n_x + n_o:]
        if n_a:
            @pl.when((pl.program_id(0) == 0) & (pl.program_id(1) == 0))
            def _():
                for r in a_refs:
                    r[...] = jnp.zeros(r.shape, F32)
        if mode == "nt" and b_shard:
            acc = None
            for s in range(N_CHIPS):
                part = lax.dot_general(a_ref[:, s * ks:(s + 1) * ks], b_ref[s], dims, preferred_element_type=F32)
                acc = part if acc is None else acc + part
        else:
            acc = lax.dot_general(a_ref[...], b_ref[...], dims, preferred_element_type=F32)
        if ref_epi is not None:
            ref_epi(acc, e_refs, b_refs, o_refs, a_refs)
            return
        outs = epi(acc, *[r[...] for r in e_refs])
        for r, o in zip(o_refs, outs):
            r[...] = o.astype(r.dtype)

    outs = pl.pallas_call(
        body, name=name, grid=grid, in_specs=[a_spec, b_spec] + e_specs + [_ANY] * n_x, out_specs=o_specs,
        out_shape=o_shapes, compiler_params=_params(("arbitrary", "arbitrary") if n_a else ("parallel", "parallel")),
    )(a, b, *extras, *bcast, *order)
    return outs[0] if n_o + n_a == 1 else outs


def _rowwise(fn, rows, bcast, outs, accs=(), *, tm, name, after=None):
    def norm(r):
        return r if isinstance(r, tuple) else (r, r.shape[1], 0)

    rows = [norm(r) for r in rows]
    T = rows[0][0].shape[0]
    tm = min(tm, T)
    while T % tm:
        tm -= SUBLANES
    n_r, n_b, n_o, n_a = len(rows), len(bcast), len(outs), len(accs)
    order = _after_operand(after)
    n_x = len(order)
    in_specs = [pl.BlockSpec((tm, c), functools.partial(lambda i, cb: (i, cb), cb=cb)) for _, c, cb in rows]
    in_specs += [pl.BlockSpec(b.shape, lambda i: (0, 0)) for b in bcast] + [_ANY] * n_x
    out_specs = [pl.BlockSpec((tm, o[0]), lambda i: (i, 0)) for o in outs]
    out_specs += [pl.BlockSpec(s, lambda i: (0, 0)) for s in accs]
    out_shape = [jax.ShapeDtypeStruct((T, o[2] if len(o) > 2 else o[0]), o[1]) for o in outs]
    out_shape += [jax.ShapeDtypeStruct(s, F32) for s in accs]

    def body(*refs):
        in_refs = refs[:n_r]
        b_refs = refs[n_r:n_r + n_b]
        o_refs = refs[n_r + n_b + n_x:n_r + n_b + n_x + n_o]
        a_refs = refs[n_r + n_b + n_x + n_o:]
        if n_a:
            @pl.when(pl.program_id(0) == 0)
            def _():
                for r in a_refs:
                    r[...] = jnp.zeros(r.shape, F32)
        fn(in_refs, b_refs, o_refs, a_refs)

    res = pl.pallas_call(
        body, name=name, grid=(T // tm,), in_specs=in_specs, out_specs=out_specs, out_shape=out_shape,
        compiler_params=_params(("arbitrary",) if n_a else ("parallel",)),
    )(*[r[0] for r in rows], *bcast, *order)
    return res


def _rsum(x):
    return jnp.sum(x, axis=0, keepdims=True)


def _rms_fwd(x, g, name, after=None):
    def fn(ins, bs, outs, accs):
        xv = ins[0][...]
        r = lax.rsqrt(jnp.mean(xv * xv, axis=-1, keepdims=True) + EPS)
        outs[0][...] = (xv * r * bs[0][...]).astype(BF16)

    return _rowwise(fn, [x], [g], [(D_MODEL, BF16)], tm=512, name=name, after=after)[0]


def _rms_bwd_math(xv, dh, g):
    r = lax.rsqrt(jnp.mean(xv * xv, axis=-1, keepdims=True) + EPS)
    hn = xv * r
    dgh = dh * g
    dx = r * (dgh - hn * jnp.mean(dgh * hn, axis=-1, keepdims=True))
    return dx, _rsum(dh * hn)


def _mm_norm_bwd(dy, w, x, dres, g, name, after=None):
    def epilogue(acc, e_refs, b_refs, o_refs, a_refs):
        dx, dg = _rms_bwd_math(e_refs[0][...], acc, b_refs[0][...])
        dx = dx + e_refs[1][...]
        o_refs[0][...] = dx
        o_refs[1][...] = dx.astype(BF16)
        a_refs[0][...] += dg

    return _mm(dy, w, mode="nt", b_shard=True, out_dtypes=(F32, BF16), extras=(x, dres), bcast=(g,),
               accs=((1, D_MODEL),), ref_epi=epilogue, name=name, after=after)


def _mm_res_norm(a, w, res, g, name):
    def epilogue(acc, e_refs, b_refs, o_refs, a_refs):
        xv = acc + e_refs[0][...]
        o_refs[0][...] = xv
        r = lax.rsqrt(jnp.mean(xv * xv, axis=-1, keepdims=True) + EPS)
        o_refs[1][...] = (xv * r * b_refs[0][...]).astype(BF16)

    return _mm(a, w, mode="nn", out_dtypes=(F32, BF16), extras=(res,), bcast=(g,), ref_epi=epilogue, name=name)


def _mm_final_loss(a, w, res, target, g, name):
    def epilogue(acc, e_refs, b_refs, o_refs, a_refs):
        xv = acc + e_refs[0][...]
        gv = b_refs[0][...]
        r = lax.rsqrt(jnp.mean(xv * xv, axis=-1, keepdims=True) + EPS)
        e = xv * r * gv - e_refs[1][...]
        tok = jnp.mean(e * e, axis=-1, keepdims=True)
        a_refs[0][...] += 0.5 * jnp.sum(tok, axis=0, keepdims=True) * jnp.ones((1, LANES), F32)
        dx, dg = _rms_bwd_math(xv, e * (1.0 / D_MODEL), gv)
        o_refs[0][...] = dx
        o_refs[1][...] = dx.astype(BF16)
        a_refs[1][...] += dg

    return _mm(a, w, mode="nn", out_dtypes=(F32, BF16), extras=(res, target), bcast=(g,),
               accs=((1, LANES), (1, D_MODEL)), ref_epi=epilogue, name=name)


def _relu2(acc):
    r = jnp.maximum(acc, 0.0)
    return r * r, r


def _mlp_fwd(x, h, fetch, tag, finish):
    w_up = fetch(f"mlp{tag}_up", h)
    a, r = _mm(h, w_up, mode="nn", b_shard=True, out_dtypes=(BF16, BF16), epi=_relu2, name=f"mlp{tag}_up")
    w_down = fetch(f"mlp{tag}_down", a)
    return finish(a, w_down, x, f"mlp{tag}_down"), (h, a, r, w_up, w_down)


def _mlp_bwd(x, g, saved, dx, dx_bf, tag, after):
    h, a, r, w_up, w_down = saved
    d_down = _mm(a, dx_bf, mode="tn", out_dtypes=(BF16,), name=f"mlp{tag}_dwdown", after=after)
    dup = _mm(dx_bf, w_down, mode="nt", extras=(r,), out_dtypes=(BF16,),
              epi=lambda acc, rv: (acc * (2.0 * rv.astype(F32)),), name=f"mlp{tag}_dup")
    d_up = _mm(h, dup, mode="tn", o_shard=True, out_dtypes=(BF16,), name=f"mlp{tag}_dwup")
    dx_new, dx_new_bf, dg = _mm_norm_bwd(dup, w_up, x, dx, g, f"mlp{tag}_dh")
    return dx_new, dx_new_bf, dg, d_up, d_down


def _rope_tables(L, B):
    rows = L // GRID_W
    row = np.repeat(np.arange(rows, dtype=np.float32), GRID_W)
    col = np.tile(np.arange(GRID_W, dtype=np.float32), rows)
    inv = (ROPE_THETA ** (-np.arange(HEAD_DIM // 4, dtype=np.float32) / (HEAD_DIM // 4))).astype(np.float32)
    ar, ac = row[:, None] * inv, col[:, None] * inv
    cos = np.concatenate([np.cos(ar), np.cos(ar), np.cos(ac), np.cos(ac)], axis=-1)
    sin = np.concatenate([-np.sin(ar), np.sin(ar), -np.sin(ac), np.sin(ac)], axis=-1)
    return jnp.asarray(np.tile(cos, (B, 1)), F32), jnp.asarray(np.tile(sin, (B, 1)), F32)


def _swap_halves(x):
    lane = lax.broadcasted_iota(jnp.int32, x.shape, 1)
    return jnp.where((lane % 64) < 32, pltpu.roll(x, HEAD_DIM - 32, 1), pltpu.roll(x, 32, 1))


def _qk_prep(qkv, cos, sin, q_g, k_g):
    def fn(ins, bs, outs, accs):
        c, s = ins[1][...], ins[2][...]
        for h in range(N_HEADS + N_KV):
            xv = ins[0][:, h * HEAD_DIM:(h + 1) * HEAD_DIM]
            g = bs[0][...] if h < N_HEADS else bs[1][...]
            r = lax.rsqrt(jnp.mean(xv * xv, axis=-1, keepdims=True) + EPS)
            z = xv * r * g
            y = (z * c + _swap_halves(z) * s).astype(BF16)
            if h < N_HEADS:
                outs[0][:, h * HEAD_DIM:(h + 1) * HEAD_DIM] = y
            else:
                outs[1][:, (h - N_HEADS) * HEAD_DIM:(h - N_HEADS + 1) * HEAD_DIM] = y
        outs[2][...] = ins[0][:, (N_HEADS + N_KV) * HEAD_DIM:].astype(BF16)

    kvw = N_KV * HEAD_DIM
    return _rowwise(fn, [qkv, cos, sin], [q_g, k_g], [(D_MODEL, BF16), (kvw, BF16), (kvw, BF16)], tm=512,
                    name="attn_qk_prep")


def _qk_prep_bwd(qkv, dq, dk, dv, cos, sin, q_g, k_g):
    def fn(ins, bs, outs, accs):
        c, s = ins[4][...], ins[5][...]
        for h in range(N_HEADS + N_KV):
            sl = slice(h * HEAD_DIM, (h + 1) * HEAD_DIM)
            xv = ins[0][:, sl]
            if h < N_HEADS:
                g, dy, acc = bs[0][...], ins[1][:, sl], accs[0]
            else:
                ks = slice((h - N_HEADS) * HEAD_DIM, (h - N_HEADS + 1) * HEAD_DIM)
                g, dy, acc = bs[1][...], ins[2][:, ks], accs[1]
            r = lax.rsqrt(jnp.mean(xv * xv, axis=-1, keepdims=True) + EPS)
            xn = xv * r
            dz = dy * c - _swap_halves(dy) * s
            acc[...] += _rsum(dz * xn)
            dxn = dz * g
            outs[0][:, sl] = (r * (dxn - xn * jnp.mean(dxn * xn, axis=-1, keepdims=True))).astype(BF16)
        outs[0][:, (N_HEADS + N_KV) * HEAD_DIM:] = ins[3][...].astype(BF16)

    return _rowwise(fn, [qkv, dq, dk, dv, cos, sin], [q_g, k_g], [(qkv.shape[1], BF16)],
                    [(1, HEAD_DIM), (1, HEAD_DIM)], tm=512, name="attn_qk_prep_bwd")


_EXP2_SCALE = SCALE * math.log2(math.e)


def _exp_rows(q, k):
    s = lax.dot_general(q, k, _NT, preferred_element_type=F32)
    p = jnp.exp2((s - jnp.max(s, axis=-1, keepdims=True)) * _EXP2_SCALE)
    return p, jnp.sum(p, axis=-1, keepdims=True)


def _attn_fwd(q, k, v, B, L, tq=2048, sub=256):
    tq = min(tq, L)
    sub = min(sub, tq)
    nq = L // tq

    def body(q_ref, k_ref, v_ref, o_ref):
        kv, vv = k_ref[...], v_ref[...]
        for c in range(tq // sub):
            rows = slice(c * sub, (c + 1) * sub)
            p, l = _exp_rows(q_ref[rows, :], kv)
            o = jnp.dot(p.astype(BF16), vv, preferred_element_type=F32)
            o_ref[rows, :] = (o * (1.0 / l)).astype(o_ref.dtype)

    return pl.pallas_call(
        body, name="attn_fwd", grid=(B, N_HEADS, nq),
        in_specs=[pl.BlockSpec((tq, HEAD_DIM), lambda b, h, i: (b * nq + i, h)),
                  pl.BlockSpec((L, HEAD_DIM), lambda b, h, i: (b, h // GROUP)),
                  pl.BlockSpec((L, HEAD_DIM), lambda b, h, i: (b, h // GROUP))],
        out_specs=pl.BlockSpec((tq, HEAD_DIM), lambda b, h, i: (b * nq + i, h)),
        out_shape=jax.ShapeDtypeStruct((B * L, D_MODEL), BF16),
        compiler_params=_params(("parallel", "parallel", "parallel")),
    )(q, k, v)


def _attn_bwd(q, k, v, o, do, B, L, tq=2048, sub=512):
    tq = min(tq, L)
    sub = min(sub, tq)
    nq = L // tq

    def body(q_ref, k_ref, v_ref, o_ref, do_ref, dq_ref, dk_ref, dv_ref):
        @pl.when((pl.program_id(2) == 0) & (pl.program_id(3) == 0))
        def _():
            dk_ref[...] = jnp.zeros(dk_ref.shape, F32)
            dv_ref[...] = jnp.zeros(dv_ref.shape, F32)

        kv, vv = k_ref[...], v_ref[...]
        ps, es, dos, qs = [], [], [], []
        for c in range(tq // sub):
            rows = slice(c * sub, (c + 1) * sub)
            qc, doc = q_ref[rows, :], do_ref[rows, :]
            p, l = _exp_rows(qc, kv)
            inv = 1.0 / l
            dp = lax.dot_general(doc, vv, _NT, preferred_element_type=F32)
            delta = jnp.sum(doc.astype(F32) * o_ref[rows, :].astype(F32), axis=-1, keepdims=True)
            e = (p * (dp - delta)).astype(BF16)
            dq_ref[rows, :] = jnp.dot(e, kv, preferred_element_type=F32) * (inv * SCALE)
            ps.append(p.astype(BF16))
            es.append(e)
            dos.append((doc.astype(F32) * inv).astype(BF16))
            qs.append((qc.astype(F32) * (inv * SCALE)).astype(BF16))
        cat = lambda xs: xs[0] if len(xs) == 1 else jnp.concatenate(xs, axis=0)
        dv_ref[...] += lax.dot_general(cat(ps), cat(dos), _TN, preferred_element_type=F32)
        dk_ref[...] += lax.dot_general(cat(es), cat(qs), _TN, preferred_element_type=F32)

    qmap = lambda b, kh, g, i: (b * nq + i, kh * GROUP + g)
    kmap = lambda b, kh, g, i: (b, kh)
    kvw = N_KV * HEAD_DIM
    return pl.pallas_call(
        body, name="attn_bwd", grid=(B, N_KV, GROUP, nq),
        in_specs=[pl.BlockSpec((tq, HEAD_DIM), qmap), pl.BlockSpec((L, HEAD_DIM), kmap),
                  pl.BlockSpec((L, HEAD_DIM), kmap), pl.BlockSpec((tq, HEAD_DIM), qmap),
                  pl.BlockSpec((tq, HEAD_DIM), qmap)],
        out_specs=[pl.BlockSpec((tq, HEAD_DIM), qmap), pl.BlockSpec((L, HEAD_DIM), kmap),
                   pl.BlockSpec((L, HEAD_DIM), kmap)],
        out_shape=[jax.ShapeDtypeStruct((B * L, D_MODEL), F32), jax.ShapeDtypeStruct((B * L, kvw), F32),
                   jax.ShapeDtypeStruct((B * L, kvw), F32)],
        compiler_params=_params(("parallel", "parallel", "arbitrary", "arbitrary")),
    )(q, k, v, o, do)


def _conv_shift(x, t, L, k):
    if k == 2:
        return x
    if k < 2:
        return jnp.where(t >= 2 - k, pltpu.roll(x, 2 - k, 0), 0.0)
    return jnp.where(t < L - (k - 2), pltpu.roll(x, L - (k - 2), 0), 0.0)


def _conv_apply(x, w_ref, L):
    t = lax.broadcasted_iota(jnp.int32, x.shape, 0)
    acc = w_ref[4:5, :] + w_ref[2:3, :] * x
    for k in (0, 1, 3):
        acc = acc + w_ref[k:k + 1, :] * _conv_shift(x, t, L, k)
    return acc


def _conv_bwd(z, g, wb, dz, B, L, tc=256, after=None):
    noff = D_MODEL // tc
    order = _after_operand(after)

    def body(z_ref, g_ref, w_ref, dz_in, *rest):
        dx_ref, dw_ref = rest[len(order):]

        @pl.when(pl.program_id(1) == 0)
        def _():
            dw_ref[...] = jnp.zeros(dw_ref.shape, F32)

        x, gv = z_ref[...], g_ref[...]
        t = lax.broadcasted_iota(jnp.int32, x.shape, 0)
        dx = w_ref[2:3, :] * gv
        for k in (0, 1, 3):
            dx = dx + w_ref[k:k + 1, :] * _conv_shift(gv, t, L, 4 - k)
        dx_ref[...] = dx.astype(BF16)
        for k in range(4):
            dw_ref[k:k + 1, :] += _rsum(_conv_shift(x, t, L, k) * gv)
        dw_ref[4:5, :] += _rsum(gv)

    return pl.pallas_call(
        body, name="rg_conv_bwd", grid=(noff, B),
        in_specs=[pl.BlockSpec((L, tc), lambda j, b: (b, noff + j)), pl.BlockSpec((L, tc), lambda j, b: (b, j)),
                  pl.BlockSpec((SUBLANES, tc), lambda j, b: (0, j)), _ANY] + [_ANY] * len(order),
        out_specs=[pl.BlockSpec((L, tc), lambda j, b: (b, noff + j)),
                   pl.BlockSpec((SUBLANES, tc), lambda j, b: (0, j))],
        out_shape=[jax.ShapeDtypeStruct(dz.shape, dz.dtype), jax.ShapeDtypeStruct((SUBLANES, D_MODEL), F32)],
        input_output_aliases={3: 0},
        compiler_params=_params(("parallel", "arbitrary")),
    )(z, g, wb, dz, *order)


def _softplus(x):
    return jnp.maximum(x, 0.0) + jnp.log1p(jnp.exp(-jnp.abs(x)))


_ROW_BA, _ROW_BX, _ROW_LAM = 0, 2, 4


def _gate_math(xb, pre, vec_ref, d, sl):
    pa = pre[:, (2 * d) * LRU_BW:(2 * d + 1) * LRU_BW] + vec_ref[_ROW_BA + d:_ROW_BA + d + 1, sl]
    px = pre[:, (2 * d + 1) * LRU_BW:(2 * d + 2) * LRU_BW] + vec_ref[_ROW_BX + d:_ROW_BX + d + 1, sl]
    r = 0.5 * jnp.tanh(0.5 * pa) + 0.5
    i = 0.5 * jnp.tanh(0.5 * px) + 0.5
    slope = (-RG_C) * _softplus(-vec_ref[_ROW_LAM + d:_ROW_LAM + d + 1, sl])
    log_a = r * slope
    a = jnp.exp(log_a)
    om = -jnp.tanh(log_a) * (1.0 + a * a)
    rs = lax.rsqrt(om)
    mult = jnp.where(om > 0.0, om * rs, 0.0)
    return a, mult * (i * xb), (r, i, slope, om, mult, rs)


def _gate_bwd(rec, du_f, da_f, du_b, da_b, wcat, gvec):
    def fn(ins, bs, outs, accs):
        for blk in range(LRU_BLOCKS):
            sl = slice(blk * LRU_BW, (blk + 1) * LRU_BW)
            xb = ins[0][:, sl]
            xb16 = xb.astype(BF16)
            w = bs[0][sl, :]
            pre = jnp.dot(xb16, w, preferred_element_type=F32)
            dx = jnp.zeros_like(xb)
            dpre = []
            for d in range(2):
                a, _, (r, i, slope, om, mult, rs) = _gate_math(xb, pre, bs[1], d, sl)
                du, da = ins[1 + 2 * d][:, sl], ins[2 + 2 * d][:, sl]
                t = du * xb
                d_i = t * mult
                dx = dx + du * mult * i
                dlog = da * a - (t * i) * ((1.0 - om) * rs)
                d_r = dlog * slope
                d_sp = _rsum(dlog * r) * (-RG_C)
                lam = bs[1][_ROW_LAM + d:_ROW_LAM + d + 1, sl]
                accs[2][_ROW_LAM + d:_ROW_LAM + d + 1, sl] += d_sp * (-jax.nn.sigmoid(-lam))
                dpa = d_r * r * (1.0 - r)
                dpx = d_i * i * (1.0 - i)
                accs[2][_ROW_BA + d:_ROW_BA + d + 1, sl] += _rsum(dpa)
                accs[2][_ROW_BX + d:_ROW_BX + d + 1, sl] += _rsum(dpx)
                dpre += [dpa, dpx]
            dpre = jnp.concatenate(dpre, axis=1).astype(BF16)
            dw = lax.dot_general(xb16, dpre, _TN, preferred_element_type=F32)
            for d in range(2):
                rows = slice(d * D_MODEL + blk * LRU_BW, d * D_MODEL + (blk + 1) * LRU_BW)
                accs[0][rows, :] += dw[:, (2 * d) * LRU_BW:(2 * d + 1) * LRU_BW]
                accs[1][rows, :] += dw[:, (2 * d + 1) * LRU_BW:(2 * d + 2) * LRU_BW]
            outs[0][:, sl] = dx + lax.dot_general(dpre, w, _NT, preferred_element_type=F32)

    gate_shape = (2 * D_MODEL, LRU_BW)
    return _rowwise(fn, [rec, du_f, da_f, du_b, da_b], [wcat, gvec], [(D_MODEL, F32)],
                    [gate_shape, gate_shape, (SUBLANES, D_MODEL)], tm=512, name="rg_gate_bwd")


def _as_time_blocks(x):
    return x.reshape(x.shape[0] // SUBLANES, SUBLANES, x.shape[1])


def _scan_call(body, ins, n_out, B, L, tc, name):
    nb = L // SUBLANES
    spec = pl.BlockSpec((nb, SUBLANES, tc), lambda b, j: (b, 0, j))
    T = ins[0].shape[0]
    outs = pl.pallas_call(
        functools.partial(body, nb), name=name, grid=(B, D_MODEL // tc),
        in_specs=[spec] * len(ins), out_specs=[spec] * n_out,
        out_shape=[jax.ShapeDtypeStruct((T // SUBLANES, SUBLANES, D_MODEL), F32)] * n_out,
        compiler_params=_params(("parallel", "parallel")),
    )(*[_as_time_blocks(x) for x in ins])
    return [o.reshape(T, D_MODEL) for o in outs]


def _block_scan(A, U, reverse):
    row = lax.broadcasted_iota(jnp.int32, A.shape, 0)
    for s in (1, 2, 4):
        shift = SUBLANES - s if reverse else s
        valid = (row < SUBLANES - s) if reverse else (row >= s)
        a_sh = jnp.where(valid, pltpu.roll(A, shift, 0), 1.0)
        u_sh = jnp.where(valid, pltpu.roll(U, shift, 0), 0.0)
        U = A * u_sh + U
        A = A * a_sh
    return A, U


_LAST = SUBLANES - 1
SCAN_UNROLL = 8


def _loop_blocks(nb, step, init):
    def group(g, carry):
        for k in range(SCAN_UNROLL):
            carry = step(g * SCAN_UNROLL + k, carry)
        return carry

    return lax.fori_loop(0, nb // SCAN_UNROLL, group, init)


def _scan_bwd(dy, a_f, h_f, a_b, h_b, B, L, tc=256):
    def body(nb, dy_r, af, hf, ab, hb, duf, daf, dub, dab):
        def step(i, carry):
            c1, c2 = carry
            ir = nb - 1 - i
            row = lax.broadcasted_iota(jnp.int32, (SUBLANES, tc), 0)
            a_up = jnp.where(row == _LAST, af[jnp.minimum(ir + 1, nb - 1), :1, :], pltpu.roll(af[ir], _LAST, 0))
            p, lam = _block_scan(a_up, dy_r[ir], True)
            lam = lam + p * c1
            before = hf[jnp.maximum(ir - 1, 0), _LAST:, :] * (ir > 0).astype(F32)
            duf[ir] = lam
            daf[ir] = lam * jnp.where(row == 0, before, pltpu.roll(hf[ir], 1, 0))
            a_dn = jnp.where(row == 0, ab[jnp.maximum(i - 1, 0), _LAST:, :], pltpu.roll(ab[i], 1, 0))
            p2, lam2 = _block_scan(a_dn, dy_r[i], False)
            lam2 = lam2 + p2 * c2
            after = hb[jnp.minimum(i + 1, nb - 1), :1, :] * (i < nb - 1).astype(F32)
            dub[i] = lam2
            dab[i] = lam2 * jnp.where(row == _LAST, after, pltpu.roll(hb[i], _LAST, 0))
            return lam[:1, :], lam2[_LAST:, :]

        zero = jnp.zeros((1, tc), F32)
        _loop_blocks(nb, step, (zero, zero))

    return _scan_call(body, [dy, a_f, h_f, a_b, h_b], 4, B, L, tc, "rg_scan_bwd")


_GELU_C = math.sqrt(2.0 / math.pi)


def _gelu_parts(x):
    th = jnp.tanh(_GELU_C * (x + 0.044715 * x * x * x))
    return 0.5 * x * (1.0 + th), th


def _mm_gated_out_bwd(dx, w_out, h_f, h_b, z, name, after=None):
    def epilogue(acc, e_refs, b_refs, o_refs, a_refs):
        x = e_refs[2][...]
        gl, th = _gelu_parts(x)
        dgl = 0.5 * (1.0 + th) + 0.5 * x * (1.0 - th * th) * (_GELU_C * (1.0 + 3.0 * 0.044715 * x * x))
        o_refs[0][...] = acc * gl
        o_refs[1][...] = (acc * (e_refs[0][...] + e_refs[1][...]) * dgl).astype(BF16)

    return _mm(dx, w_out, mode="nt", out_dtypes=(F32, BF16), out_cols=(D_MODEL, 2 * D_MODEL), extras=(h_f, h_b, z),
               ref_epi=epilogue, name=name, after=after)


def _row_block(i):
    return pl.ds(pl.multiple_of(i * SUBLANES, SUBLANES), SUBLANES)


def _rg_mix_fwd(z, conv_wb, wcat, gvec, B, L):
    nb = L // SUBLANES
    n_g = D_MODEL // LRU_BW

    def body(zg_ref, zr_ref, cw_ref, w_ref, gv_ref, rec_ref, af_s, ab_s, hf_ref, hb_ref, yg_ref, uf_s, ub_s):
        rec = _conv_apply(zr_ref[...], cw_ref, L)
        rec_ref[...] = rec
        pre = jnp.dot(rec.astype(BF16), w_ref[...], preferred_element_type=F32)
        for d, (a_s, u_s) in enumerate(((af_s, uf_s), (ab_s, ub_s))):
            a, u, _ = _gate_math(rec, pre, gv_ref, d, slice(None))
            a_s[...] = a
            u_s[...] = u

        def step(i, carry):
            c1, c2 = carry
            rows, rows_b = _row_block(i), _row_block(nb - 1 - i)
            p, h = _block_scan(af_s[rows, :], uf_s[rows, :], False)
            h = h + p * c1
            hf_ref[rows, :] = h
            p2, h2 = _block_scan(ab_s[rows_b, :], ub_s[rows_b, :], True)
            h2 = h2 + p2 * c2
            hb_ref[rows_b, :] = h2
            return h[_LAST:, :], h2[:1, :]

        zero = jnp.zeros((1, LRU_BW), F32)
        _loop_blocks(nb, step, (zero, zero))
        gl, _ = _gelu_parts(zg_ref[...])
        yg_ref[...] = ((hf_ref[...] + hb_ref[...]) * gl).astype(BF16)

    seq = lambda off: pl.BlockSpec((L, LRU_BW), lambda b, g: (b, off + g))
    vec = pl.BlockSpec((SUBLANES, LRU_BW), lambda b, g: (0, g))
    T = B * L
    return pl.pallas_call(
        body, name="rg_mix", grid=(B, n_g),
        in_specs=[seq(0), seq(n_g), vec, pl.BlockSpec((LRU_BW, 4 * LRU_BW), lambda b, g: (g, 0)), vec],
        out_specs=[seq(0)] * 6,
        out_shape=[jax.ShapeDtypeStruct((T, D_MODEL), F32)] * 5 + [jax.ShapeDtypeStruct((T, D_MODEL), BF16)],
        scratch_shapes=[pltpu.VMEM((L, LRU_BW), F32)] * 2,
        compiler_params=_params(("parallel", "parallel")),
    )(z, z, conv_wb, wcat, gvec)


def _make_wcat(w_a, w_x):
    g = jnp.stack([w_a[0, 0], w_x[0, 0], w_a[0, 1], w_x[0, 1]])
    return jnp.transpose(g, (1, 2, 0, 3)).reshape(D_MODEL, 4 * LRU_BW)


def _rows_at(part, first):
    return jnp.pad(part, ((first, SUBLANES - first - part.shape[0]), (0, 0)))


def _qk_slot(q_g, k_g):
    wide = lambda v, at: jnp.pad(v, ((0, SUBLANES - 1), (at, D_MODEL - at - HEAD_DIM)))
    return wide(q_g, 0) + wide(k_g, HEAD_DIM)


def _local_step(x, target, P, fetch, emit, B, L, after=None):
    g_mix, g_mlp = P["norm_mix_g"], P["norm_mlp_g"]
    h0 = _rms_fwd(x, g_mix[0:1], "rg_norm", after=after)
    w_in, conv_wb, wcat, gvec = fetch("rg", h0)
    z = _mm(h0, w_in, mode="nn", b_shard=True, name="rg_in")
    rec, a_f, a_b, h_f, h_b, yg = _rg_mix_fwd(z, conv_wb, wcat, gvec, B, L)
    w_out = fetch("rg_out", yg)
    x1, h1 = _mm_res_norm(yg, w_out, x, g_mlp[0:1], "rg_out")
    (x2, h3), mlp0 = _mlp_fwd(x1, h1, fetch, 0, lambda a, w, res, name: _mm_res_norm(a, w, res, g_mix[1:2], name))
    w_qkv, w_o = fetch("att", h3)
    qkv = _mm(h3, w_qkv, mode="nn", b_shard=True, name="attn_qkv")
    cos, sin = _rope_tables(L, B)
    qh, kh, vh = _qk_prep(qkv, cos, sin, P["q_g"], P["k_g"])
    o = _attn_fwd(qh, kh, vh, B, L)
    x3, h4 = _mm_res_norm(o, w_o, x2, g_mlp[1:2], "attn_out")
    (dx4, dx4_bf, loss_acc, d_final_g), mlp1 = _mlp_fwd(
        x3, h4, fetch, 1, lambda a, w, res, name: _mm_final_loss(a, w, res, target, P["final_g"], name))

    dx3, dx3_bf, dg_mlp1, d_up1, d_down1 = _mlp_bwd(x3, g_mlp[1:2], mlp1, dx4, dx4_bf, 1, None)
    tok = emit("mlp1", [d_up1, d_down1])
    d_wo = _mm(o, dx3_bf, mode="tn", out_dtypes=(BF16,), name="attn_dwo", after=tok)
    do = _mm(dx3_bf, w_o, mode="nt", out_dtypes=(BF16,), name="attn_do")
    dq, dk, dv = _attn_bwd(qh, kh, vh, o, do, B, L)
    dqkv, dq_g, dk_g = _qk_prep_bwd(qkv, dq, dk, dv, cos, sin, P["q_g"], P["k_g"])
    d_wqkv = _mm(h3, dqkv, mode="tn", o_shard=True, out_dtypes=(BF16,), name="attn_dwqkv")
    tok = emit("att", [d_wqkv, d_wo])
    dx2, dx2_bf, dg_mix1 = _mm_norm_bwd(dqkv, w_qkv, x2, dx3, g_mix[1:2], "attn_dh", after=tok)
    tok = emit("point_attn_done", [dx2_bf])
    dx1, dx1_bf, dg_mlp0, d_up0, d_down0 = _mlp_bwd(x1, g_mlp[0:1], mlp0, dx2, dx2_bf, 0, tok)
    d_wout = _mm(yg, dx1_bf, mode="tn", out_dtypes=(BF16,), name="rg_dwout")
    tok = emit("mlp0", [d_up0, d_down0, d_wout])
    dy, dgate = _mm_gated_out_bwd(dx1_bf, w_out, h_f, h_b, z, "rg_dyg", after=tok)
    du_f, da_f, du_b, da_b = _scan_bwd(dy, a_f, h_f, a_b, h_b, B, L)
    drec_c, d_wa, d_wx, d_gvec = _gate_bwd(rec, du_f, da_f, du_b, da_b, wcat, gvec)
    tok = emit("gates", [d_wa, d_wx])
    dz, d_convwb = _conv_bwd(z, drec_c, conv_wb, dgate, B, L, after=tok)
    tok = emit("point_mix_done", [dz])
    d_win = _mm(h0, dz, mode="tn", o_shard=True, out_dtypes=(BF16,), name="rg_dwin", after=tok)
    tok = emit("rg_in", [d_win])
    grad_x, _, dg_mix0 = _mm_norm_bwd(dz, w_in, x, dx1, g_mix[0:1], "rg_dh", after=tok)

    norms = (_rows_at(dg_mix0, 0) + _rows_at(dg_mix1, 1) + _rows_at(dg_mlp0, 2) + _rows_at(dg_mlp1, 3)
             + _rows_at(d_final_g, 4)
             + jnp.pad(loss_acc, ((LOSS_ROW, SUBLANES - 1 - LOSS_ROW), (0, D_MODEL - LANES))))
    vec = jnp.concatenate([norms, d_convwb, d_gvec, _qk_slot(dq_g, dk_g)], axis=0)
    return grad_x, vec


_MESH = pl.DeviceIdType.MESH


def _place():
    x, y, c = lax.axis_index("x"), lax.axis_index("y"), lax.axis_index("c")
    peers = [((1 - x) if j & 2 else x, (1 - y) if j & 1 else y) for j in (1, 2, 3)]
    return x, y, c, peers


def _sum_leading(slots, name):
    def body(s_ref, o_ref):
        acc = s_ref[0]
        for d in range(1, slots.shape[0]):
            acc = acc + s_ref[d]
        o_ref[...] = acc

    return pl.pallas_call(body, name=name, out_shape=jax.ShapeDtypeStruct(slots.shape[1:], slots.dtype))(slots)


_HBM = pl.BlockSpec(memory_space=pltpu.HBM)
_SEM = pl.BlockSpec(memory_space=pltpu.SEMAPHORE)
_EFFECT = pltpu.SideEffectType.DATAFLOW_SIDE_EFFECTING


_COPIES = dict(gather=N_CHIPS - 1, scatter=N_CHIPS - 1, swap=1, spread=N_DEVICES - 1,
               gather_half=N_CHIPS - 1, share_half=N_CHIPS - 1)


def _split_copies(kind, srcs, lands, send, recv):
    x, y, c, peers = _place()
    me = 2 * x + y
    per = _COPIES[kind]
    out = []
    for a in range(len(lands)):
        for j in range(per):
            if kind == "swap":
                src, there, here, dev = srcs[a], lands[a], lands[a], (x, y, 1 - c)
            elif kind == "spread":
                k = j + 1
                dev = ((1 - x) if k & 4 else x, (1 - y) if k & 2 else y, (1 - c) if k & 1 else c)
                mine = lands[a].at[4 * x + 2 * y + c]
                src, there, here = mine, mine, lands[a].at[4 * dev[0] + 2 * dev[1] + dev[2]]
            else:
                px, py = peers[j]
                dev = (px, py, c)
                if kind == "gather":
                    src, there, here = lands[a].at[me], lands[a].at[me], lands[a].at[2 * px + py]
                elif kind in ("gather_half", "share_half"):
                    half = lands[a].shape[1] // 2
                    mine, other = pl.ds(c * half, half), pl.ds((1 - c) * half, half)
                    if kind == "gather_half":
                        src = there = lands[a].at[me, mine]
                        here = lands[a].at[2 * px + py, mine]
                    else:
                        src = there = lands[a].at[2 * px + py, mine]
                        here = lands[a].at[2 * px + py, other]
                        dev = (x, y, 1 - c)
                else:
                    src, there, here = srcs[a].at[2 * px + py], lands[a].at[j], lands[a].at[j]
            mk = functools.partial(
                pltpu.make_async_remote_copy, src_ref=src, send_sem=send.at[per * a + j],
                recv_sem=recv.at[per * a + j], device_id=dev, device_id_type=_MESH)
            out.append((functools.partial(mk, dst_ref=there), functools.partial(mk, dst_ref=here)))
    return out


_CORE_PAIR = ("swap", "share_half")
CORE_PAIR_BARRIER_ID = 0
BARRIER_IDS = dict(gather_rest=1, scatter_mlp1=2, scatter_att=3, scatter_mlp0=4, scatter_gates=5, scatter_rg_in=6,
                   gather_gates=7, spread_vec=8)


def _entry_peers(kind):
    x, y, c, peers = _place()
    if kind in _CORE_PAIR:
        return [(x, y, 1 - c)]
    if kind == "spread":
        return [((1 - x) if k & 4 else x, (1 - y) if k & 2 else y, (1 - c) if k & 1 else c)
                for k in range(1, N_DEVICES)]
    return [(px, py, c) for px, py in peers]


def _entry_params(kind, barrier_id):
    if kind in _CORE_PAIR:
        barrier_id = CORE_PAIR_BARRIER_ID
    collective = {} if barrier_id is None else dict(collective_id=barrier_id)
    return pltpu.CompilerParams(has_side_effects=_EFFECT, **collective)


def _entry_handshake(kind, barrier_id):
    if kind in _CORE_PAIR or barrier_id is not None:
        barrier = pltpu.get_barrier_semaphore()
        peers = _entry_peers(kind)
        for peer in peers:
            pl.semaphore_signal(barrier, inc=1, device_id=peer, device_id_type=_MESH)
        pl.semaphore_wait(barrier, len(peers))


def _exchange_start(kind, srcs, lands, name, after=None, barrier_id=None):
    arrays = list(srcs) + list(lands)
    n_s, n, n_all = len(srcs), len(lands), len(srcs) + len(lands)
    n_sem = _COPIES[kind] * n
    order = _after_operand(after)
    n_x = len(order)

    def body(*refs):
        _entry_handshake(kind, barrier_id)
        send, recv = refs[n_all + n_x], refs[n_all + n_x + 1]
        token = refs[-1]
        for started, _ in _split_copies(kind, refs[:n_s], refs[n_s:n_all], send, recv):
            started().start()
        token[...] = jnp.zeros(token.shape, F32)

    res = pl.pallas_call(
        body, name=name,
        out_shape=(pltpu.SemaphoreType.DMA((n_sem,)), pltpu.SemaphoreType.DMA((n_sem,)),
                   *[pltpu.HBM(a.shape, a.dtype) for a in arrays], jax.ShapeDtypeStruct((SUBLANES, LANES), F32)),
        in_specs=[_HBM] * n_all + [_ANY] * n_x,
        out_specs=(_SEM, _SEM, *[_HBM] * n_all, pl.BlockSpec(memory_space=pltpu.VMEM)),
        input_output_aliases={i: 2 + i for i in range(n_all)},
        compiler_params=_entry_params(kind, barrier_id),
    )(*[pltpu.with_memory_space_constraint(a, pltpu.HBM) for a in arrays], *order)
    return (res[0], res[1], res[2:2 + n_s], res[2 + n_s:2 + n_all]), res[-1]


def _gather_start_groups(land_groups, name, after=None, kind="gather", barrier_id=None):
    arrays = [a for group in land_groups for a in group]
    n_all, n_g = len(arrays), len(land_groups)
    order = _after_operand(after)
    n_x = len(order)

    def body(*refs):
        _entry_handshake(kind, barrier_id)
        first = 0
        for gi, group in enumerate(land_groups):
            send, recv = refs[n_all + n_x + 2 * gi], refs[n_all + n_x + 2 * gi + 1]
            for started, _ in _split_copies(kind, [], refs[first:first + len(group)], send, recv):
                started().start()
            first += len(group)
        refs[-1][...] = jnp.zeros(refs[-1].shape, F32)

    sems = [pltpu.SemaphoreType.DMA((_COPIES[kind] * len(group),)) for group in land_groups for _ in range(2)]
    res = pl.pallas_call(
        body, name=name,
        out_shape=(*sems, *[pltpu.HBM(a.shape, a.dtype) for a in arrays], jax.ShapeDtypeStruct((SUBLANES, LANES), F32)),
        in_specs=[_HBM] * n_all + [_ANY] * n_x,
        out_specs=(*[_SEM] * (2 * n_g), *[_HBM] * n_all, pl.BlockSpec(memory_space=pltpu.VMEM)),
        input_output_aliases={i: 2 * n_g + i for i in range(n_all)},
        compiler_params=_entry_params(kind, barrier_id),
    )(*[pltpu.with_memory_space_constraint(a, pltpu.HBM) for a in arrays], *order)
    handles, first = [], 2 * n_g
    for gi, group in enumerate(land_groups):
        handles.append((res[2 * gi], res[2 * gi + 1], [], res[first:first + len(group)]))
        first += len(group)
    return handles, res[-1]


def _exchange_wait(kind, handle, after, name):
    send, recv, srcs, lands = handle
    arrays = list(srcs) + list(lands)
    n_s, n_all = len(srcs), len(arrays)
    order = list(after) if isinstance(after, (list, tuple)) else [after]

    def body(*refs):
        for started, landing in _split_copies(kind, refs[:n_s], refs[n_s:n_all], refs[n_all], refs[n_all + 1]):
            started().wait_send()
            landing().wait_recv()

    res = pl.pallas_call(
        body, name=name, out_shape=[pltpu.HBM(a.shape, a.dtype) for a in arrays],
        in_specs=[_HBM] * n_all + [_SEM, _SEM] + [_ANY] * len(order), out_specs=[_HBM] * n_all,
        input_output_aliases={i: i for i in range(n_all)},
        compiler_params=pltpu.CompilerParams(has_side_effects=_EFFECT),
    )(*arrays, send, recv, *order)
    return res[:n_s], res[n_s:]


def _index_operand(i):
    return jnp.reshape(i, (1,)).astype(jnp.int32)


def _cast_into_slot(src, row0, rows, me, dtype, name, after=None, add=None, n_slots=N_CHIPS):
    cols = src.shape[1]
    tm = min(512, rows)
    order = _after_operand(after)
    terms = [src] + ([] if add is None else [add])

    def body(me_ref, *rest):
        val = rest[0][...]
        if add is not None:
            val = val + rest[1][...]
        rest[-1][...] = val.astype(dtype)

    return pl.pallas_call(
        body, name=name,
        grid_spec=pltpu.PrefetchScalarGridSpec(
            num_scalar_prefetch=1, grid=(rows // tm,),
            in_specs=[pl.BlockSpec((tm, cols), lambda i, me_ref: (i + row0 // tm, 0))] * len(terms)
            + [_ANY] * len(order),
            out_specs=pl.BlockSpec((None, tm, cols), lambda i, me_ref: (me_ref[0], i, 0))),
        out_shape=jax.ShapeDtypeStruct((n_slots, rows, cols), dtype), compiler_params=_params(("parallel",)),
    )(_index_operand(me), *terms, *order)


def _sum_slots(mine, r, me, name):
    _, rows, cols = r.shape
    tm = min(512, rows)

    def body(me_ref, own_ref, r_ref, o_ref):
        o_ref[...] = ((own_ref[...].astype(F32) + r_ref[0].astype(F32)) + r_ref[1].astype(F32)) + r_ref[2].astype(F32)

    return pl.pallas_call(
        body, name=name,
        grid_spec=pltpu.PrefetchScalarGridSpec(
            num_scalar_prefetch=1, grid=(rows // tm,),
            in_specs=[pl.BlockSpec((None, tm, cols), lambda i, me_ref: (me_ref[0], i, 0)),
                      pl.BlockSpec((N_CHIPS - 1, tm, cols), lambda i, me_ref: (0, i, 0))],
            out_specs=pl.BlockSpec((tm, cols), lambda i, me_ref: (i, 0))),
        out_shape=jax.ShapeDtypeStruct((rows, cols), F32), compiler_params=_params(("parallel",)),
    )(_index_operand(me), mine, r)


def _adamw(w, m, v, ps, qs, name):
    rows, cols = w.shape
    seg_rows = ps[0].shape[0]
    tm = min(512, seg_rows)
    while seg_rows % tm:
        tm -= SUBLANES
    per, n_seg = seg_rows // tm, len(ps)
    parts = list(ps) + ([] if qs is None else list(qs))

    def body(w_ref, m_ref, v_ref, *rest):
        g_refs, outs = rest[:len(parts)], rest[len(parts):]
        grad = lambda s: g_refs[s][...] if qs is None else g_refs[s][...] + g_refs[n_seg + s][...]
        g = grad(0)
        for s in range(1, n_seg):
            g = jnp.where(pl.program_id(0) >= s * per, grad(s), g)
        m1 = ADAM_B1 * m_ref[...] + (1.0 - ADAM_B1) * g
        v1 = ADAM_B2 * v_ref[...] + (1.0 - ADAM_B2) * (g * g)
        m_hat = m1 / (1.0 - ADAM_B1 ** ADAM_STEP)
        v_hat = v1 / (1.0 - ADAM_B2 ** ADAM_STEP)
        outs[0][...] = g
        outs[1][...] = (-ADAM_LR) * (m_hat / (jnp.sqrt(v_hat) + ADAM_EPS) + ADAM_WD * w_ref[...])
        outs[2][...] = m1
        outs[3][...] = v1

    row_spec = pl.BlockSpec((tm, cols), lambda i: (i, 0))
    seg_spec = lambda s: pl.BlockSpec((tm, cols), lambda i: (jnp.clip(i - s * per, 0, per - 1), 0))
    return pl.pallas_call(
        body, name=name, grid=(rows // tm,),
        in_specs=[row_spec] * 3 + [seg_spec(s) for s in range(n_seg)] * (1 if qs is None else 2),
        out_specs=[row_spec] * 4, out_shape=[jax.ShapeDtypeStruct((rows, cols), F32)] * 4,
        compiler_params=_params(("arbitrary",)),
    )(w, m, v, *parts)


def _put_cols(shard, me):
    full = jnp.zeros((shard.shape[0], D_MODEL), F32)
    return lax.dynamic_update_slice(full, shard, (0, me * (D_MODEL // N_CHIPS)))


def _gate_vec_slot(b_a, b_x, lam):
    return _rows_at(b_a, _ROW_BA) + _rows_at(b_x, _ROW_BX) + _rows_at(lam, _ROW_LAM)


def _pack_vec(p, me):
    return jnp.concatenate([
        _rows_at(p["norm_mix_g"], 0) + _rows_at(p["norm_mlp_g"], 2) + _rows_at(p["final_g"][None], 4),
        _rows_at(_put_cols(p["rg_conv_w"][0, :, 0, :], me), 0) + _rows_at(p["rg_conv_b"], 4),
        _gate_vec_slot(_put_cols(p["rg_b_a"][0], me), _put_cols(p["rg_b_x"][0], me), _put_cols(p["rg_lam"][0], me)),
        _qk_slot(p["at_q_g"], p["at_k_g"]),
    ], axis=0)


def _unpack_vec(r, me):
    def cols(rows):
        return lax.dynamic_slice(rows, (0, me * (D_MODEL // N_CHIPS)), (rows.shape[0], D_MODEL // N_CHIPS))

    gate = r[16:24]
    return dict(
        norm_mix_g=r[0:2], norm_mlp_g=r[2:4], final_g=r[4], rg_conv_w=cols(r[8:12])[None, :, None, :],
        rg_conv_b=r[12:13], rg_b_a=cols(gate[_ROW_BA:_ROW_BA + 2])[None], rg_b_x=cols(gate[_ROW_BX:_ROW_BX + 2])[None],
        rg_lam=cols(gate[_ROW_LAM:_ROW_LAM + 2])[None], at_q_g=r[24:25, 0:HEAD_DIM],
        at_k_g=r[24:25, HEAD_DIM:2 * HEAD_DIM])


_WEIGHTS = ['norm_mix_g', 'norm_mlp_g', 'rg_w_in', 'rg_conv_w', 'rg_conv_b', 'rg_w_a', 'rg_b_a', 'rg_w_x', 'rg_b_x',
            'rg_lam', 'rg_w_out', 'at_w_qkv', 'at_q_g', 'at_k_g', 'at_w_o', 'mlp_w_up', 'mlp_w_down', 'final_g']
_BIG = dict(rg_w_in=["rg_w_in"], rg_w_out=["rg_w_out"], at_w_qkv=["at_w_qkv"], at_w_o=["at_w_o"],
            mlp_w_up=["up0", "up1"], mlp_w_down=["down0", "down1"])


def kernel(x, *args):
    n_w = len(_WEIGHTS)
    w = dict(zip(_WEIGHTS, args[:n_w]))
    target = args[n_w]
    m = dict(zip(_WEIGHTS, args[n_w + 1:2 * n_w + 1]))
    v = dict(zip(_WEIGHTS, args[2 * n_w + 1:3 * n_w + 1]))
    B, L, _ = x.shape
    T = B * L
    me = 2 * lax.axis_index("x") + lax.axis_index("y")

    vec = jnp.concatenate([_gate_vec_slot(w["rg_b_a"][0], w["rg_b_x"][0], w["rg_lam"][0]),
                           _rows_at(w["rg_conv_w"][0, :, 0, :], 0)], axis=0)
    flat = lambda a: a.reshape(-1, a.shape[-1])
    rows_of = lambda k: w[k].shape[-2]
    groups = [("rg", [("rg_w_in", 0, BF16), (vec, 0, F32)]), ("rg_out", [("rg_w_out", 0, BF16)]),
              ("mlp0_up", [("mlp_w_up", 0, BF16)]), ("mlp0_down", [("mlp_w_down", 0, BF16)]),
              ("att", [("at_w_qkv", 0, BF16), ("at_w_o", 0, BF16)]),
              ("mlp1", [("mlp_w_up", 1, BF16), ("mlp_w_down", 1, BF16)])]

    def landing_zones(group, members, after):
        lands = []
        for n, (k, layer, dtype) in enumerate(members):
            src, rows = (flat(w[k]), rows_of(k)) if isinstance(k, str) else (k, k.shape[0])
            lands.append(_cast_into_slot(src, layer * rows, rows, me, dtype, f"place_{group}{n}", after=after))
        return lands

    halves, gathers = {}, {}
    halves["rg"], tok = _exchange_start("gather_half", [], landing_zones(*groups[0], None), "gather_rg_start")
    handles, tok = _gather_start_groups([landing_zones(g, members, tok) for g, members in groups[1:]],
                                        "gather_rest_start", after=tok, kind="gather_half",
                                        barrier_id=BARRIER_IDS["gather_rest"])
    halves.update(zip([g for g, _ in groups[1:]], handles))
    wcat = _make_wcat(w["rg_w_a"], w["rg_w_x"]).astype(BF16)

    packs = [_pack_vec(p, me) for p in (w, m, v)]

    ready = {}

    def share(some, after, name):
        landed = [_exchange_wait("gather_half", halves[g], after, f"gather_{g}_landed")[1] for g in some]
        handles, token = _gather_start_groups(landed, name, kind="share_half")
        gathers.update(zip(some, handles))
        return token

    tok = share(["rg"], [tok, wcat] + packs, "share_rg_start")

    def fetch(what, after):
        if what in ready:
            return ready[what]
        group = "mlp1" if what.startswith("mlp1") else what
        if group == "rg_out":
            share(["rg_out", "mlp0_up"], after, "share_early_start")
        elif group == "mlp0_up":
            share(["mlp0_down", "att"], after, "share_mid_start")
        _, full = _exchange_wait("share_half", gathers[group], after, f"gather_{group}_wait")
        if group == "att":
            share(["mlp1"], after, "share_mlp1_start")
        if group == "rg":
            vec_full = jnp.transpose(full[1], (1, 0, 2)).reshape(2 * SUBLANES, D_MODEL)
            conv_wb = vec_full[SUBLANES:] + _rows_at(w["rg_conv_b"], 4)
            return full[0], conv_wb, wcat, vec_full[:SUBLANES]
        if group == "rg_out":
            return full[0].reshape(D_MODEL, D_MODEL)
        if group == "att":
            return full[0], full[1].reshape(D_MODEL, D_MODEL)
        if group == "mlp1":
            ready["mlp1_up"], ready["mlp1_down"] = full[0], full[1].reshape(4 * D_MODEL, D_MODEL)
            return ready[what]
        return full[0] if group == "mlp0_up" else full[0].reshape(4 * D_MODEL, D_MODEL)

    names = dict(mlp1=["up1", "down1"], att=["at_w_qkv", "at_w_o"], mlp0=["up0", "down0", "rg_w_out"],
                 rg_in=["rg_w_in"], gates=["rg_w_a", "rg_w_x"])
    scatters, swaps, P, Q, res = {}, [], {}, {}, {}

    def start_scatter(group, grads):
        srcs = [g.reshape(N_CHIPS, -1, g.shape[-1]) for g in grads]
        lands = [lax.empty((N_CHIPS - 1,) + s.shape[1:], s.dtype) for s in srcs]
        scatters[group], token = _exchange_start("scatter", srcs, lands, f"scatter_{group}_start",
                                                 barrier_id=BARRIER_IDS[f"scatter_{group}"])
        return token

    def settle(groups, after):
        keys, parts = [], []
        for group in groups:
            srcs, lands = _exchange_wait("scatter", scatters[group], after, f"scatter_{group}_wait")
            for k, s, r in zip(names[group], srcs, lands):
                keys.append(k)
                parts.append(_sum_slots(s, r, me, f"sum_{k}"))
        handle, token = _exchange_start("swap", parts, [lax.empty(p.shape, F32) for p in parts],
                                        f"swap_{groups[0]}_start")
        swaps.append((keys, handle, f"swap_{groups[0]}_wait"))
        return token

    def finish(after):
        for keys, handle, name in swaps:
            mine, theirs = _exchange_wait("swap", handle, after, name)
            P.update(zip(keys, mine))
            Q.update(zip(keys, theirs))
        swaps.clear()
        last = after
        for k, parts in _BIG.items():
            if k in res or any(p not in P for p in parts):
                continue
            shape = w[k].shape
            two_d = lambda a: a.reshape(-1, shape[-1])
            outs = _adamw(two_d(w[k]), two_d(m[k]), two_d(v[k]), [P[p] for p in parts], [Q[p] for p in parts],
                          f"adamw_{k}")
            res[k] = [o.reshape(shape) for o in outs]
            last = outs[0]
        if "rg_w_a" in P and "gates" not in gathers:
            lands = [_cast_into_slot(P[k], 0, P[k].shape[0], me, F32, f"place_{k}", after=last, add=Q[k])
                     for k in names["gates"]]
            gathers["gates"], last = _exchange_start("gather", [], lands, "gather_gates_start", after=last,
                                                     barrier_id=BARRIER_IDS["gather_gates"])
        return last

    def emit(event, arrays):
        if event == "point_attn_done":
            return None
        if event == "point_mix_done":
            return settle(["mlp1", "att", "mlp0"], arrays[0])
        token = start_scatter(event, arrays)
        if event == "rg_in":
            return finish(settle(["gates"], token))
        return token

    P_vec = dict(norm_mix_g=w["norm_mix_g"], norm_mlp_g=w["norm_mlp_g"], final_g=w["final_g"][None],
                 q_g=w["at_q_g"], k_g=w["at_k_g"])
    grad_x, vec_part = _local_step(x.reshape(T, D_MODEL), target.reshape(T, D_MODEL), P_vec, fetch, emit, B, L,
                                   after=tok)

    me8 = 2 * me + lax.axis_index("c")
    vec_slots = _cast_into_slot(vec_part, 0, VEC_ROWS, me8, F32, "place_vec", n_slots=N_DEVICES)
    spread, tok = _exchange_start("spread", [], [vec_slots], "spread_vec_start", barrier_id=BARRIER_IDS["spread_vec"])
    last = settle(["rg_in"], tok)
    _, gate_grads = _exchange_wait("gather", gathers["gates"], last, "gather_gates_wait")
    for k, g in zip(names["gates"], gate_grads):
        two_d = lambda a: a.reshape(g.shape[0] * g.shape[1], g.shape[2])
        outs = _adamw(two_d(w[k]), two_d(m[k]), two_d(v[k]), [two_d(g)], None, f"adamw_{k}")
        res[k] = [o.reshape(w[k].shape) for o in outs]
        last = outs[0]
    _, (vec_all,) = _exchange_wait("spread", spread, last, "spread_vec_wait")
    vec_grad = _sum_leading(vec_all, "sum_vec")
    loss = vec_grad[LOSS_ROW, 0]
    outs = _adamw(*packs, [vec_grad], None, "adamw_vec")
    finish(outs[0])
    unpacked = [_unpack_vec(o, me) for o in outs]
    for k in _WEIGHTS:
        if k not in res:
            res[k] = [u[k] for u in unpacked]

    result = [loss, grad_x.reshape(B, L, D_MODEL)]
    for slot in range(4):
        result += [res[k][slot] for k in _WEIGHTS]
    return tuple(result)
```

```python
import functools
import math

import jax
import jax.numpy as jnp
import numpy as np
from jax import lax
from jax.experimental import pallas as pl
from jax.experimental.pallas import tpu as pltpu

F32 = jnp.float32
BF16 = jnp.bfloat16

D_MODEL = 1024
HEAD_DIM = 128
N_HEADS = 8
N_KV = 2
GROUP = N_HEADS // N_KV
LRU_BLOCKS = 8
LRU_BW = 128
GRID_W = 64
ROPE_THETA = 10000.0
EPS = 1e-6
RG_C = 8.0
SCALE = 1.0 / math.sqrt(HEAD_DIM)
N_CHIPS = 4

ADAM_LR = 0.001
ADAM_B1 = 0.9
ADAM_B2 = 0.999
ADAM_EPS = 1e-08
ADAM_WD = 0.01
ADAM_STEP = 10

V7X_VMEM_BYTES = 64 * 1024 * 1024
VMEM_LIMIT = V7X_VMEM_BYTES * 3 // 4
LANES = 128
SUBLANES = 8

N_DEVICES = 8
VEC_ROWS = 32
LOSS_ROW = 5


def _params(sem):
    return pltpu.CompilerParams(dimension_semantics=sem, vmem_limit_bytes=VMEM_LIMIT)


_ANY = pl.BlockSpec(memory_space=pl.ANY)
_NN = (((1,), (0,)), ((), ()))
_NT = (((1,), (1,)), ((), ()))
_TN = (((0,), (0,)), ((), ()))


def _after_operand(after):
    return [] if after is None else [after]


def _fit(t, n):
    if n <= t:
        return n
    c = (t // LANES) * LANES
    while n % c:
        c -= LANES
    return c


MM_VMEM_BUDGET = VMEM_LIMIT * 3 // 4
def _mm_tiles(M, K, ns, n_total, out_dtypes, extras, whole_rows):
    for tm in (2048, 1024, 512, 256, 128):
        for tn in ((ns,) if whole_rows else (1024, 512, 256)):
            tn = _fit(tn, ns)
            per_row = 2 * (2 * K) + 4 * tn + sum(2 * tn * jnp.dtype(d).itemsize for d in out_dtypes)
            per_row += sum(2 * tn * e.dtype.itemsize for e in extras)
            b_buffers = 1 if tn == n_total else 2
            if M % tm == 0 and b_buffers * (2 * K * tn) + tm * per_row <= MM_VMEM_BUDGET:
                return tm, tn
    raise ValueError(f"no tile fits VMEM for M={M} K={K} N={ns}")


def _mm(a, b, *, mode, name, out_dtypes=(F32,), b_shard=False, o_shard=False, extras=(), epi=None, after=None,
        bcast=(), accs=(), ref_epi=None, out_cols=None):
    if mode == "tn":
        K, M = a.shape
        N = b.shape[1]
    else:
        M, K = a.shape
        if mode == "nn":
            N = b.shape[0] * b.shape[2] if b_shard else b.shape[1]
        else:
            N = b.shape[1] if b_shard else b.shape[0]
    ns = N
    if b_shard and mode == "nn":
        ns = b.shape[2]
    elif o_shard:
        ns = N // N_CHIPS
    tm, tn = _mm_tiles(M, K, ns, N, out_dtypes, extras, whole_rows=ref_epi is not None)
    if ref_epi is not None:
        tm = min(tm, 512)
    grid = (M // tm, N // tn)
    q = ns // tn
    once = dict(pipeline_mode=pl.Buffered(1)) if tn == N else {}

    if mode == "tn":
        a_spec = pl.BlockSpec((K, tm), lambda i, j: (0, i))
        b_spec = pl.BlockSpec((K, tn), lambda i, j: (0, j), **once)
        dims = _TN
    elif mode == "nn":
        a_spec = pl.BlockSpec((tm, K), lambda i, j: (i, 0))
        if b_shard:
            b_spec = pl.BlockSpec((None, K, tn), lambda i, j: (j // q, 0, j % q), **once)
        else:
            b_spec = pl.BlockSpec((K, tn), lambda i, j: (0, j), **once)
        dims = _NN
    else:
        a_spec = pl.BlockSpec((tm, K), lambda i, j: (i, 0))
        if b_shard:
            ks = b.shape[2]
            b_spec = pl.BlockSpec((N_CHIPS, tn, ks), lambda i, j: (0, j, 0), **once)
        else:
            b_spec = pl.BlockSpec((tn, K), lambda i, j: (j, 0), **once)
        dims = _NT

    if o_shard:
        o_specs = [pl.BlockSpec((None, tm, tn), lambda i, j: (j // q, i, j % q))]
        o_shapes = [jax.ShapeDtypeStruct((N_CHIPS, M, ns), out_dtypes[0])]
    else:
        o_specs = [pl.BlockSpec((tm, tn), lambda i, j: (i, j)) for _ in out_dtypes]
        o_shapes = [jax.ShapeDtypeStruct((M, N if out_cols is None else out_cols[n]), dt)
                    for n, dt in enumerate(out_dtypes)]
    e_specs = [pl.BlockSpec((tm, tn), lambda i, j: (i, j)) for _ in extras]
    e_specs += [pl.BlockSpec(v.shape, lambda i, j: (0, 0)) for v in bcast]
    o_specs += [pl.BlockSpec(s, lambda i, j: (0, 0)) for s in accs]
    o_shapes += [jax.ShapeDtypeStruct(s, F32) for s in accs]
    n_e, n_b, n_o, n_a = len(extras), len(bcast), len(out_dtypes), len(accs)
    order = _after_operand(after)
    n_x = len(order)
    if epi is None:
        epi = lambda acc: (acc,)

    def body(a_ref, b_ref, *rest):
        e_refs, b_refs = rest[:n_e], rest[n_e:n_e + n_b]
        o_refs = rest[n_e + n_b + n_x:n_e + n_b + n_x + n_o]
        a_refs = rest[n_e + n_b + n_x + n_o:]
        if n_a:
            @pl.when((pl.program_id(0) == 0) & (pl.program_id(1) == 0))
            def _():
                for r in a_refs:
                    r[...] = jnp.zeros(r.shape, F32)
        if mode == "nt" and b_shard:
            acc = None
            for s in range(N_CHIPS):
                part = lax.dot_general(a_ref[:, s * ks:(s + 1) * ks], b_ref[s], dims, preferred_element_type=F32)
                acc = part if acc is None else acc + part
        else:
            acc = lax.dot_general(a_ref[...], b_ref[...], dims, preferred_element_type=F32)
        if ref_epi is not None:
            ref_epi(acc, e_refs, b_refs, o_refs, a_refs)
            return
        outs = epi(acc, *[r[...] for r in e_refs])
        for r, o in zip(o_refs, outs):
            r[...] = o.astype(r.dtype)

    outs = pl.pallas_call(
        body, name=name, grid=grid, in_specs=[a_spec, b_spec] + e_specs + [_ANY] * n_x, out_specs=o_specs,
        out_shape=o_shapes, compiler_params=_params(("arbitrary", "arbitrary") if n_a else ("parallel", "parallel")),
    )(a, b, *extras, *bcast, *order)
    return outs[0] if n_o + n_a == 1 else outs


def _rowwise(fn, rows, bcast, outs, accs=(), *, tm, name, after=None):
    def norm(r):
        return r if isinstance(r, tuple) else (r, r.shape[1], 0)

    rows = [norm(r) for r in rows]
    T = rows[0][0].shape[0]
    tm = min(tm, T)
    while T % tm:
        tm -= SUBLANES
    n_r, n_b, n_o, n_a = len(rows), len(bcast), len(outs), len(accs)
    order = _after_operand(after)
    n_x = len(order)
    in_specs = [pl.BlockSpec((tm, c), functools.partial(lambda i, cb: (i, cb), cb=cb)) for _, c, cb in rows]
    in_specs += [pl.BlockSpec(b.shape, lambda i: (0, 0)) for b in bcast] + [_ANY] * n_x
    out_specs = [pl.BlockSpec((tm, o[0]), lambda i: (i, 0)) for o in outs]
    out_specs += [pl.BlockSpec(s, lambda i: (0, 0)) for s in accs]
    out_shape = [jax.ShapeDtypeStruct((T, o[2] if len(o) > 2 else o[0]), o[1]) for o in outs]
    out_shape += [jax.ShapeDtypeStruct(s, F32) for s in accs]

    def body(*refs):
        in_refs = refs[:n_r]
        b_refs = refs[n_r:n_r + n_b]
        o_refs = refs[n_r + n_b + n_x:n_r + n_b + n_x + n_o]
        a_refs = refs[n_r + n_b + n_x + n_o:]
        if n_a:
            @pl.when(pl.program_id(0) == 0)
            def _():
                for r in a_refs:
                    r[...] = jnp.zeros(r.shape, F32)
        fn(in_refs, b_refs, o_refs, a_refs)

    res = pl.pallas_call(
        body, name=name, grid=(T // tm,), in_specs=in_specs, out_specs=out_specs, out_shape=out_shape,
        compiler_params=_params(("arbitrary",) if n_a else ("parallel",)),
    )(*[r[0] for r in rows], *bcast, *order)
    return res


def _rsum(x):
    return jnp.sum(x, axis=0, keepdims=True)


def _rms_fwd(x, g, name, after=None):
    def fn(ins, bs, outs, accs):
        xv = ins[0][...]
        r = lax.rsqrt(jnp.mean(xv * xv, axis=-1, keepdims=True) + EPS)
        outs[0][...] = (xv * r * bs[0][...]).astype(BF16)

    return _rowwise(fn, [x], [g], [(D_MODEL, BF16)], tm=512, name=name, after=after)[0]


def _rms_bwd_math(xv, dh, g):
    r = lax.rsqrt(jnp.mean(xv * xv, axis=-1, keepdims=True) + EPS)
    hn = xv * r
    dgh = dh * g
    dx = r * (dgh - hn * jnp.mean(dgh * hn, axis=-1, keepdims=True))
    return dx, _rsum(dh * hn)


def _mm_norm_bwd(dy, w, x, dres, g, name, after=None):
    def epilogue(acc, e_refs, b_refs, o_refs, a_refs):
        dx, dg = _rms_bwd_math(e_refs[0][...], acc, b_refs[0][...])
        dx = dx + e_refs[1][...]
        o_refs[0][...] = dx
        o_refs[1][...] = dx.astype(BF16)
        a_refs[0][...] += dg

    return _mm(dy, w, mode="nt", b_shard=True, out_dtypes=(F32, BF16), extras=(x, dres), bcast=(g,),
               accs=((1, D_MODEL),), ref_epi=epilogue, name=name, after=after)


def _mm_res_norm(a, w, res, g, name):
    def epilogue(acc, e_refs, b_refs, o_refs, a_refs):
        xv = acc + e_refs[0][...]
        o_refs[0][...] = xv
        r = lax.rsqrt(jnp.mean(xv * xv, axis=-1, keepdims=True) + EPS)
        o_refs[1][...] = (xv * r * b_refs[0][...]).astype(BF16)

    return _mm(a, w, mode="nn", out_dtypes=(F32, BF16), extras=(res,), bcast=(g,), ref_epi=epilogue, name=name)


def _mm_final_loss(a, w, res, target, g, name):
    def epilogue(acc, e_refs, b_refs, o_refs, a_refs):
        xv = acc + e_refs[0][...]
        gv = b_refs[0][...]
        r = lax.rsqrt(jnp.mean(xv * xv, axis=-1, keepdims=True) + EPS)
        e = xv * r * gv - e_refs[1][...]
        tok = jnp.mean(e * e, axis=-1, keepdims=True)
        a_refs[0][...] += 0.5 * jnp.sum(tok, axis=0, keepdims=True) * jnp.ones((1, LANES), F32)
        dx, dg = _rms_bwd_math(xv, e * (1.0 / D_MODEL), gv)
        o_refs[0][...] = dx
        o_refs[1][...] = dx.astype(BF16)
        a_refs[1][...] += dg

    return _mm(a, w, mode="nn", out_dtypes=(F32, BF16), extras=(res, target), bcast=(g,),
               accs=((1, LANES), (1, D_MODEL)), ref_epi=epilogue, name=name)


def _relu2(acc):
    r = jnp.maximum(acc, 0.0)
    return r * r, r


def _mlp_fwd(x, h, fetch, tag, finish):
    w_up = fetch(f"mlp{tag}_up", h)
    a, r = _mm(h, w_up, mode="nn", b_shard=True, out_dtypes=(BF16, BF16), epi=_relu2, name=f"mlp{tag}_up")
    w_down = fetch(f"mlp{tag}_down", a)
    return finish(a, w_down, x, f"mlp{tag}_down"), (h, a, r, w_up, w_down)


def _mlp_bwd(x, g, saved, dx, dx_bf, tag, after):
    h, a, r, w_up, w_down = saved
    d_down = _mm(a, dx_bf, mode="tn", out_dtypes=(BF16,), name=f"mlp{tag}_dwdown", after=after)
    dup = _mm(dx_bf, w_down, mode="nt", extras=(r,), out_dtypes=(BF16,),
              epi=lambda acc, rv: (acc * (2.0 * rv.astype(F32)),), name=f"mlp{tag}_dup")
    d_up = _mm(h, dup, mode="tn", o_shard=True, out_dtypes=(BF16,), name=f"mlp{tag}_dwup")
    dx_new, dx_new_bf, dg = _mm_norm_bwd(dup, w_up, x, dx, g, f"mlp{tag}_dh")
    return dx_new, dx_new_bf, dg, d_up, d_down


def _rope_tables(L, B):
    rows = L // GRID_W
    row = np.repeat(np.arange(rows, dtype=np.float32), GRID_W)
    col = np.tile(np.arange(GRID_W, dtype=np.float32), rows)
    inv = (ROPE_THETA ** (-np.arange(HEAD_DIM // 4, dtype=np.float32) / (HEAD_DIM // 4))).astype(np.float32)
    ar, ac = row[:, None] * inv, col[:, None] * inv
    cos = np.concatenate([np.cos(ar), np.cos(ar), np.cos(ac), np.cos(ac)], axis=-1)
    sin = np.concatenate([-np.sin(ar), np.sin(ar), -np.sin(ac), np.sin(ac)], axis=-1)
    return jnp.asarray(np.tile(cos, (B, 1)), F32), jnp.asarray(np.tile(sin, (B, 1)), F32)


def _swap_halves(x):
    lane = lax.broadcasted_iota(jnp.int32, x.shape, 1)
    return jnp.where((lane % 64) < 32, pltpu.roll(x, HEAD_DIM - 32, 1), pltpu.roll(x, 32, 1))


def _qk_prep(qkv, cos, sin, q_g, k_g):
    def fn(ins, bs, outs, accs):
        c, s = ins[1][...], ins[2][...]
        for h in range(N_HEADS + N_KV):
            xv = ins[0][:, h * HEAD_DIM:(h + 1) * HEAD_DIM]
            g = bs[0][...] if h < N_HEADS else bs[1][...]
            r = lax.rsqrt(jnp.mean(xv * xv, axis=-1, keepdims=True) + EPS)
            z = xv * r * g
            y = (z * c + _swap_halves(z) * s).astype(BF16)
            if h < N_HEADS:
                outs[0][:, h * HEAD_DIM:(h + 1) * HEAD_DIM] = y
            else:
                outs[1][:, (h - N_HEADS) * HEAD_DIM:(h - N_HEADS + 1) * HEAD_DIM] = y
        outs[2][...] = ins[0][:, (N_HEADS + N_KV) * HEAD_DIM:].astype(BF16)

    kvw = N_KV * HEAD_DIM
    return _rowwise(fn, [qkv, cos, sin], [q_g, k_g], [(D_MODEL, BF16), (kvw, BF16), (kvw, BF16)], tm=512,
                    name="attn_qk_prep")


def _qk_prep_bwd(qkv, dq, dk, dv, cos, sin, q_g, k_g):
    def fn(ins, bs, outs, accs):
        c, s = ins[4][...], ins[5][...]
        for h in range(N_HEADS + N_KV):
            sl = slice(h * HEAD_DIM, (h + 1) * HEAD_DIM)
            xv = ins[0][:, sl]
            if h < N_HEADS:
                g, dy, acc = bs[0][...], ins[1][:, sl], accs[0]
            else:
                ks = slice((h - N_HEADS) * HEAD_DIM, (h - N_HEADS + 1) * HEAD_DIM)
                g, dy, acc = bs[1][...], ins[2][:, ks], accs[1]
            r = lax.rsqrt(jnp.mean(xv * xv, axis=-1, keepdims=True) + EPS)
            xn = xv * r
            dz = dy * c - _swap_halves(dy) * s
            acc[...] += _rsum(dz * xn)
            dxn = dz * g
            outs[0][:, sl] = (r * (dxn - xn * jnp.mean(dxn * xn, axis=-1, keepdims=True))).astype(BF16)
        outs[0][:, (N_HEADS + N_KV) * HEAD_DIM:] = ins[3][...].astype(BF16)

    return _rowwise(fn, [qkv, dq, dk, dv, cos, sin], [q_g, k_g], [(qkv.shape[1], BF16)],
                    [(1, HEAD_DIM), (1, HEAD_DIM)], tm=512, name="attn_qk_prep_bwd")


_EXP2_SCALE = SCALE * math.log2(math.e)


def _exp_rows(q, k):
    s = lax.dot_general(q, k, _NT, preferred_element_type=F32)
    p = jnp.exp2((s - jnp.max(s, axis=-1, keepdims=True)) * _EXP2_SCALE)
    return p, jnp.sum(p, axis=-1, keepdims=True)


def _attn_fwd(q, k, v, B, L, tq=2048, sub=256):
    tq = min(tq, L)
    sub = min(sub, tq)
    nq = L // tq

    def body(q_ref, k_ref, v_ref, o_ref):
        kv, vv = k_ref[...], v_ref[...]
        for c in range(tq // sub):
            rows = slice(c * sub, (c + 1) * sub)
            p, l = _exp_rows(q_ref[rows, :], kv)
            o = jnp.dot(p.astype(BF16), vv, preferred_element_type=F32)
            o_ref[rows, :] = (o * (1.0 / l)).astype(o_ref.dtype)

    return pl.pallas_call(
        body, name="attn_fwd", grid=(B, N_HEADS, nq),
        in_specs=[pl.BlockSpec((tq, HEAD_DIM), lambda b, h, i: (b * nq + i, h)),
                  pl.BlockSpec((L, HEAD_DIM), lambda b, h, i: (b, h // GROUP)),
                  pl.BlockSpec((L, HEAD_DIM), lambda b, h, i: (b, h // GROUP))],
        out_specs=pl.BlockSpec((tq, HEAD_DIM), lambda b, h, i: (b * nq + i, h)),
        out_shape=jax.ShapeDtypeStruct((B * L, D_MODEL), BF16),
        compiler_params=_params(("parallel", "parallel", "parallel")),
    )(q, k, v)


def _attn_bwd(q, k, v, o, do, B, L, tq=2048, sub=512):
    tq = min(tq, L)
    sub = min(sub, tq)
    nq = L // tq

    def body(q_ref, k_ref, v_ref, o_ref, do_ref, dq_ref, dk_ref, dv_ref):
        @pl.when((pl.program_id(2) == 0) & (pl.program_id(3) == 0))
        def _():
            dk_ref[...] = jnp.zeros(dk_ref.shape, F32)
            dv_ref[...] = jnp.zeros(dv_ref.shape, F32)

        kv, vv = k_ref[...], v_ref[...]
        ps, es, dos, qs = [], [], [], []
        for c in range(tq // sub):
            rows = slice(c * sub, (c + 1) * sub)
            qc, doc = q_ref[rows, :], do_ref[rows, :]
            p, l = _exp_rows(qc, kv)
            inv = 1.0 / l
            dp = lax.dot_general(doc, vv, _NT, preferred_element_type=F32)
            delta = jnp.sum(doc.astype(F32) * o_ref[rows, :].astype(F32), axis=-1, keepdims=True)
            e = (p * (dp - delta)).astype(BF16)
            dq_ref[rows, :] = jnp.dot(e, kv, preferred_element_type=F32) * (inv * SCALE)
            ps.append(p.astype(BF16))
            es.append(e)
            dos.append((doc.astype(F32) * inv).astype(BF16))
            qs.append((qc.astype(F32) * (inv * SCALE)).astype(BF16))
        cat = lambda xs: xs[0] if len(xs) == 1 else jnp.concatenate(xs, axis=0)
        dv_ref[...] += lax.dot_general(cat(ps), cat(dos), _TN, preferred_element_type=F32)
        dk_ref[...] += lax.dot_general(cat(es), cat(qs), _TN, preferred_element_type=F32)

    qmap = lambda b, kh, g, i: (b * nq + i, kh * GROUP + g)
    kmap = lambda b, kh, g, i: (b, kh)
    kvw = N_KV * HEAD_DIM
    return pl.pallas_call(
        body, name="attn_bwd", grid=(B, N_KV, GROUP, nq),
        in_specs=[pl.BlockSpec((tq, HEAD_DIM), qmap), pl.BlockSpec((L, HEAD_DIM), kmap),
                  pl.BlockSpec((L, HEAD_DIM), kmap), pl.BlockSpec((tq, HEAD_DIM), qmap),
                  pl.BlockSpec((tq, HEAD_DIM), qmap)],
        out_specs=[pl.BlockSpec((tq, HEAD_DIM), qmap), pl.BlockSpec((L, HEAD_DIM), kmap),
                   pl.BlockSpec((L, HEAD_DIM), kmap)],
        out_shape=[jax.ShapeDtypeStruct((B * L, D_MODEL), F32), jax.ShapeDtypeStruct((B * L, kvw), F32),
                   jax.ShapeDtypeStruct((B * L, kvw), F32)],
        compiler_params=_params(("parallel", "parallel", "arbitrary", "arbitrary")),
    )(q, k, v, o, do)


def _conv_shift(x, t, L, k):
    if k == 2:
        return x
    if k < 2:
        return jnp.where(t >= 2 - k, pltpu.roll(x, 2 - k, 0), 0.0)
    return jnp.where(t < L - (k - 2), pltpu.roll(x, L - (k - 2), 0), 0.0)


def _conv_apply(x, w_ref, L):
    t = lax.broadcasted_iota(jnp.int32, x.shape, 0)
    acc = w_ref[4:5, :] + w_ref[2:3, :] * x
    for k in (0, 1, 3):
        acc = acc + w_ref[k:k + 1, :] * _conv_shift(x, t, L, k)
    return acc


def _conv_bwd(z, g, wb, dz, B, L, tc=256, after=None):
    noff = D_MODEL // tc
    order = _after_operand(after)

    def body(z_ref, g_ref, w_ref, dz_in, *rest):
        dx_ref, dw_ref = rest[len(order):]

        @pl.when(pl.program_id(1) == 0)
        def _():
            dw_ref[...] = jnp.zeros(dw_ref.shape, F32)

        x, gv = z_ref[...], g_ref[...]
        t = lax.broadcasted_iota(jnp.int32, x.shape, 0)
        dx = w_ref[2:3, :] * gv
        for k in (0, 1, 3):
            dx = dx + w_ref[k:k + 1, :] * _conv_shift(gv, t, L, 4 - k)
        dx_ref[...] = dx.astype(BF16)
        for k in range(4):
            dw_ref[k:k + 1, :] += _rsum(_conv_shift(x, t, L, k) * gv)
        dw_ref[4:5, :] += _rsum(gv)

    return pl.pallas_call(
        body, name="rg_conv_bwd", grid=(noff, B),
        in_specs=[pl.BlockSpec((L, tc), lambda j, b: (b, noff + j)), pl.BlockSpec((L, tc), lambda j, b: (b, j)),
                  pl.BlockSpec((SUBLANES, tc), lambda j, b: (0, j)), _ANY] + [_ANY] * len(order),
        out_specs=[pl.BlockSpec((L, tc), lambda j, b: (b, noff + j)),
                   pl.BlockSpec((SUBLANES, tc), lambda j, b: (0, j))],
        out_shape=[jax.ShapeDtypeStruct(dz.shape, dz.dtype), jax.ShapeDtypeStruct((SUBLANES, D_MODEL), F32)],
        input_output_aliases={3: 0},
        compiler_params=_params(("parallel", "arbitrary")),
    )(z, g, wb, dz, *order)


def _softplus(x):
    return jnp.maximum(x, 0.0) + jnp.log1p(jnp.exp(-jnp.abs(x)))


_ROW_BA, _ROW_BX, _ROW_LAM = 0, 2, 4


def _gate_math(xb, pre, vec_ref, d, sl):
    pa = pre[:, (2 * d) * LRU_BW:(2 * d + 1) * LRU_BW] + vec_ref[_ROW_BA + d:_ROW_BA + d + 1, sl]
    px = pre[:, (2 * d + 1) * LRU_BW:(2 * d + 2) * LRU_BW] + vec_ref[_ROW_BX + d:_ROW_BX + d + 1, sl]
    r = 0.5 * jnp.tanh(0.5 * pa) + 0.5
    i = 0.5 * jnp.tanh(0.5 * px) + 0.5
    slope = (-RG_C) * _softplus(-vec_ref[_ROW_LAM + d:_ROW_LAM + d + 1, sl])
    log_a = r * slope
    a = jnp.exp(log_a)
    om = -jnp.tanh(log_a) * (1.0 + a * a)
    rs = lax.rsqrt(om)
    mult = jnp.where(om > 0.0, om * rs, 0.0)
    return a, mult * (i * xb), (r, i, slope, om, mult, rs)


def _gate_bwd(rec, du_f, da_f, du_b, da_b, wcat, gvec):
    def fn(ins, bs, outs, accs):
        for blk in range(LRU_BLOCKS):
            sl = slice(blk * LRU_BW, (blk + 1) * LRU_BW)
            xb = ins[0][:, sl]
            xb16 = xb.astype(BF16)
            w = bs[0][sl, :]
            pre = jnp.dot(xb16, w, preferred_element_type=F32)
            dx = jnp.zeros_like(xb)
            dpre = []
            for d in range(2):
                a, _, (r, i, slope, om, mult, rs) = _gate_math(xb, pre, bs[1], d, sl)
                du, da = ins[1 + 2 * d][:, sl], ins[2 + 2 * d][:, sl]
                t = du * xb
                d_i = t * mult
                dx = dx + du * mult * i
                dlog = da * a - (t * i) * ((1.0 - om) * rs)
                d_r = dlog * slope
                d_sp = _rsum(dlog * r) * (-RG_C)
                lam = bs[1][_ROW_LAM + d:_ROW_LAM + d + 1, sl]
                accs[2][_ROW_LAM + d:_ROW_LAM + d + 1, sl] += d_sp * (-jax.nn.sigmoid(-lam))
                dpa = d_r * r * (1.0 - r)
                dpx = d_i * i * (1.0 - i)
                accs[2][_ROW_BA + d:_ROW_BA + d + 1, sl] += _rsum(dpa)
                accs[2][_ROW_BX + d:_ROW_BX + d + 1, sl] += _rsum(dpx)
                dpre += [dpa, dpx]
            dpre = jnp.concatenate(dpre, axis=1).astype(BF16)
            dw = lax.dot_general(xb16, dpre, _TN, preferred_element_type=F32)
            for d in range(2):
                rows = slice(d * D_MODEL + blk * LRU_BW, d * D_MODEL + (blk + 1) * LRU_BW)
                accs[0][rows, :] += dw[:, (2 * d) * LRU_BW:(2 * d + 1) * LRU_BW]
                accs[1][rows, :] += dw[:, (2 * d + 1) * LRU_BW:(2 * d + 2) * LRU_BW]
            outs[0][:, sl] = dx + lax.dot_general(dpre, w, _NT, preferred_element_type=F32)

    gate_shape = (2 * D_MODEL, LRU_BW)
    return _rowwise(fn, [rec, du_f, da_f, du_b, da_b], [wcat, gvec], [(D_MODEL, F32)],
                    [gate_shape, gate_shape, (SUBLANES, D_MODEL)], tm=512, name="rg_gate_bwd")


def _as_time_blocks(x):
    return x.reshape(x.shape[0] // SUBLANES, SUBLANES, x.shape[1])


def _scan_call(body, ins, n_out, B, L, tc, name):
    nb = L // SUBLANES
    spec = pl.BlockSpec((nb, SUBLANES, tc), lambda b, j: (b, 0, j))
    T = ins[0].shape[0]
    outs = pl.pallas_call(
        functools.partial(body, nb), name=name, grid=(B, D_MODEL // tc),
        in_specs=[spec] * len(ins), out_specs=[spec] * n_out,
        out_shape=[jax.ShapeDtypeStruct((T // SUBLANES, SUBLANES, D_MODEL), F32)] * n_out,
        compiler_params=_params(("parallel", "parallel")),
    )(*[_as_time_blocks(x) for x in ins])
    return [o.reshape(T, D_MODEL) for o in outs]


def _block_scan(A, U, reverse):
    row = lax.broadcasted_iota(jnp.int32, A.shape, 0)
    for s in (1, 2, 4):
        shift = SUBLANES - s if reverse else s
        valid = (row < SUBLANES - s) if reverse else (row >= s)
        a_sh = jnp.where(valid, pltpu.roll(A, shift, 0), 1.0)
        u_sh = jnp.where(valid, pltpu.roll(U, shift, 0), 0.0)
        U = A * u_sh + U
        A = A * a_sh
    return A, U


_LAST = SUBLANES - 1
SCAN_UNROLL = 8


def _loop_blocks(nb, step, init):
    def group(g, carry):
        for k in range(SCAN_UNROLL):
            carry = step(g * SCAN_UNROLL + k, carry)
        return carry

    return lax.fori_loop(0, nb // SCAN_UNROLL, group, init)


def _scan_bwd(dy, a_f, h_f, a_b, h_b, B, L, tc=256):
    def body(nb, dy_r, af, hf, ab, hb, duf, daf, dub, dab):
        def step(i, carry):
            c1, c2 = carry
            ir = nb - 1 - i
            row = lax.broadcasted_iota(jnp.int32, (SUBLANES, tc), 0)
            a_up = jnp.where(row == _LAST, af[jnp.minimum(ir + 1, nb - 1), :1, :], pltpu.roll(af[ir], _LAST, 0))
            p, lam = _block_scan(a_up, dy_r[ir], True)
            lam = lam + p * c1
            before = hf[jnp.maximum(ir - 1, 0), _LAST:, :] * (ir > 0).astype(F32)
            duf[ir] = lam
            daf[ir] = lam * jnp.where(row == 0, before, pltpu.roll(hf[ir], 1, 0))
            a_dn = jnp.where(row == 0, ab[jnp.maximum(i - 1, 0), _LAST:, :], pltpu.roll(ab[i], 1, 0))
            p2, lam2 = _block_scan(a_dn, dy_r[i], False)
            lam2 = lam2 + p2 * c2
            after = hb[jnp.minimum(i + 1, nb - 1), :1, :] * (i < nb - 1).astype(F32)
            dub[i] = lam2
            dab[i] = lam2 * jnp.where(row == _LAST, after, pltpu.roll(hb[i], _LAST, 0))
            return lam[:1, :], lam2[_LAST:, :]

        zero = jnp.zeros((1, tc), F32)
        _loop_blocks(nb, step, (zero, zero))

    return _scan_call(body, [dy, a_f, h_f, a_b, h_b], 4, B, L, tc, "rg_scan_bwd")


_GELU_C = math.sqrt(2.0 / math.pi)


def _gelu_parts(x):
    th = jnp.tanh(_GELU_C * (x + 0.044715 * x * x * x))
    return 0.5 * x * (1.0 + th), th


def _mm_gated_out_bwd(dx, w_out, h_f, h_b, z, name, after=None):
    def epilogue(acc, e_refs, b_refs, o_refs, a_refs):
        x = e_refs[2][...]
        gl, th = _gelu_parts(x)
        dgl = 0.5 * (1.0 + th) + 0.5 * x * (1.0 - th * th) * (_GELU_C * (1.0 + 3.0 * 0.044715 * x * x))
        o_refs[0][...] = acc * gl
        o_refs[1][...] = (acc * (e_refs[0][...] + e_refs[1][...]) * dgl).astype(BF16)

    return _mm(dx, w_out, mode="nt", out_dtypes=(F32, BF16), out_cols=(D_MODEL, 2 * D_MODEL), extras=(h_f, h_b, z),
               ref_epi=epilogue, name=name, after=after)


def _row_block(i):
    return pl.ds(pl.multiple_of(i * SUBLANES, SUBLANES), SUBLANES)


def _rg_mix_fwd(z, conv_wb, wcat, gvec, B, L):
    nb = L // SUBLANES
    n_g = D_MODEL // LRU_BW

    def body(zg_ref, zr_ref, cw_ref, w_ref, gv_ref, rec_ref, af_s, ab_s, hf_ref, hb_ref, yg_ref, uf_s, ub_s):
        rec = _conv_apply(zr_ref[...], cw_ref, L)
        rec_ref[...] = rec
        pre = jnp.dot(rec.astype(BF16), w_ref[...], preferred_element_type=F32)
        for d, (a_s, u_s) in enumerate(((af_s, uf_s), (ab_s, ub_s))):
            a, u, _ = _gate_math(rec, pre, gv_ref, d, slice(None))
            a_s[...] = a
            u_s[...] = u

        def step(i, carry):
            c1, c2 = carry
            rows, rows_b = _row_block(i), _row_block(nb - 1 - i)
            p, h = _block_scan(af_s[rows, :], uf_s[rows, :], False)
            h = h + p * c1
            hf_ref[rows, :] = h
            p2, h2 = _block_scan(ab_s[rows_b, :], ub_s[rows_b, :], True)
            h2 = h2 + p2 * c2
            hb_ref[rows_b, :] = h2
            return h[_LAST:, :], h2[:1, :]

        zero = jnp.zeros((1, LRU_BW), F32)
        _loop_blocks(nb, step, (zero, zero))
        gl, _ = _gelu_parts(zg_ref[...])
        yg_ref[...] = ((hf_ref[...] + hb_ref[...]) * gl).astype(BF16)

    seq = lambda off: pl.BlockSpec((L, LRU_BW), lambda b, g: (b, off + g))
    vec = pl.BlockSpec((SUBLANES, LRU_BW), lambda b, g: (0, g))
    T = B * L
    return pl.pallas_call(
        body, name="rg_mix", grid=(B, n_g),
        in_specs=[seq(0), seq(n_g), vec, pl.BlockSpec((LRU_BW, 4 * LRU_BW), lambda b, g: (g, 0)), vec],
        out_specs=[seq(0)] * 6,
        out_shape=[jax.ShapeDtypeStruct((T, D_MODEL), F32)] * 5 + [jax.ShapeDtypeStruct((T, D_MODEL), BF16)],
        scratch_shapes=[pltpu.VMEM((L, LRU_BW), F32)] * 2,
        compiler_params=_params(("parallel", "parallel")),
    )(z, z, conv_wb, wcat, gvec)


def _make_wcat(w_a, w_x):
    g = jnp.stack([w_a[0, 0], w_x[0, 0], w_a[0, 1], w_x[0, 1]])
    return jnp.transpose(g, (1, 2, 0, 3)).reshape(D_MODEL, 4 * LRU_BW)


def _rows_at(part, first):
    return jnp.pad(part, ((first, SUBLANES - first - part.shape[0]), (0, 0)))


def _qk_slot(q_g, k_g):
    wide = lambda v, at: jnp.pad(v, ((0, SUBLANES - 1), (at, D_MODEL - at - HEAD_DIM)))
    return wide(q_g, 0) + wide(k_g, HEAD_DIM)


def _local_step(x, target, P, fetch, emit, B, L, after=None):
    g_mix, g_mlp = P["norm_mix_g"], P["norm_mlp_g"]
    h0 = _rms_fwd(x, g_mix[0:1], "rg_norm", after=after)
    w_in, conv_wb, wcat, gvec = fetch("rg", h0)
    z = _mm(h0, w_in, mode="nn", b_shard=True, name="rg_in")
    rec, a_f, a_b, h_f, h_b, yg = _rg_mix_fwd(z, conv_wb, wcat, gvec, B, L)
    w_out = fetch("rg_out", yg)
    x1, h1 = _mm_res_norm(yg, w_out, x, g_mlp[0:1], "rg_out")
    (x2, h3), mlp0 = _mlp_fwd(x1, h1, fetch, 0, lambda a, w, res, name: _mm_res_norm(a, w, res, g_mix[1:2], name))
    w_qkv, w_o = fetch("att", h3)
    qkv = _mm(h3, w_qkv, mode="nn", b_shard=True, name="attn_qkv")
    cos, sin = _rope_tables(L, B)
    qh, kh, vh = _qk_prep(qkv, cos, sin, P["q_g"], P["k_g"])
    o = _attn_fwd(qh, kh, vh, B, L)
    x3, h4 = _mm_res_norm(o, w_o, x2, g_mlp[1:2], "attn_out")
    (dx4, dx4_bf, loss_acc, d_final_g), mlp1 = _mlp_fwd(
        x3, h4, fetch, 1, lambda a, w, res, name: _mm_final_loss(a, w, res, target, P["final_g"], name))

    dx3, dx3_bf, dg_mlp1, d_up1, d_down1 = _mlp_bwd(x3, g_mlp[1:2], mlp1, dx4, dx4_bf, 1, None)
    tok = emit("mlp1", [d_up1, d_down1])
    d_wo = _mm(o, dx3_bf, mode="tn", out_dtypes=(BF16,), name="attn_dwo", after=tok)
    do = _mm(dx3_bf, w_o, mode="nt", out_dtypes=(BF16,), name="attn_do")
    dq, dk, dv = _attn_bwd(qh, kh, vh, o, do, B, L)
    dqkv, dq_g, dk_g = _qk_prep_bwd(qkv, dq, dk, dv, cos, sin, P["q_g"], P["k_g"])
    d_wqkv = _mm(h3, dqkv, mode="tn", o_shard=True, out_dtypes=(BF16,), name="attn_dwqkv")
    tok = emit("att", [d_wqkv, d_wo])
    dx2, dx2_bf, dg_mix1 = _mm_norm_bwd(dqkv, w_qkv, x2, dx3, g_mix[1:2], "attn_dh", after=tok)
    tok = emit("point_attn_done", [dx2_bf])
    dx1, dx1_bf, dg_mlp0, d_up0, d_down0 = _mlp_bwd(x1, g_mlp[0:1], mlp0, dx2, dx2_bf, 0, tok)
    d_wout = _mm(yg, dx1_bf, mode="tn", out_dtypes=(BF16,), name="rg_dwout")
    tok = emit("mlp0", [d_up0, d_down0, d_wout])
    dy, dgate = _mm_gated_out_bwd(dx1_bf, w_out, h_f, h_b, z, "rg_dyg", after=tok)
    du_f, da_f, du_b, da_b = _scan_bwd(dy, a_f, h_f, a_b, h_b, B, L)
    drec_c, d_wa, d_wx, d_gvec = _gate_bwd(rec, du_f, da_f, du_b, da_b, wcat, gvec)
    tok = emit("gates", [d_wa, d_wx])
    dz, d_convwb = _conv_bwd(z, drec_c, conv_wb, dgate, B, L, after=tok)
    tok = emit("point_mix_done", [dz])
    d_win = _mm(h0, dz, mode="tn", o_shard=True, out_dtypes=(BF16,), name="rg_dwin", after=tok)
    tok = emit("rg_in", [d_win])
    grad_x, _, dg_mix0 = _mm_norm_bwd(dz, w_in, x, dx1, g_mix[0:1], "rg_dh", after=tok)

    norms = (_rows_at(dg_mix0, 0) + _rows_at(dg_mix1, 1) + _rows_at(dg_mlp0, 2) + _rows_at(dg_mlp1, 3)
             + _rows_at(d_final_g, 4)
             + jnp.pad(loss_acc, ((LOSS_ROW, SUBLANES - 1 - LOSS_ROW), (0, D_MODEL - LANES))))
    vec = jnp.concatenate([norms, d_convwb, d_gvec, _qk_slot(dq_g, dk_g)], axis=0)
    return grad_x, vec


_MESH = pl.DeviceIdType.MESH


def _place():
    x, y, c = lax.axis_index("x"), lax.axis_index("y"), lax.axis_index("c")
    peers = [((1 - x) if j & 2 else x, (1 - y) if j & 1 else y) for j in (1, 2, 3)]
    return x, y, c, peers


def _sum_leading(slots, name):
    def body(s_ref, o_ref):
        acc = s_ref[0]
        for d in range(1, slots.shape[0]):
            acc = acc + s_ref[d]
        o_ref[...] = acc

    return pl.pallas_call(body, name=name, out_shape=jax.ShapeDtypeStruct(slots.shape[1:], slots.dtype))(slots)


_HBM = pl.BlockSpec(memory_space=pltpu.HBM)
_SEM = pl.BlockSpec(memory_space=pltpu.SEMAPHORE)
_EFFECT = pltpu.SideEffectType.DATAFLOW_SIDE_EFFECTING


_COPIES = dict(gather=N_CHIPS - 1, scatter=N_CHIPS - 1, swap=1, spread=N_DEVICES - 1,
               gather_half=N_CHIPS - 1, share_half=N_CHIPS - 1)


def _split_copies(kind, srcs, lands, send, recv):
    x, y, c, peers = _place()
    me = 2 * x + y
    per = _COPIES[kind]
    out = []
    for a in range(len(lands)):
        for j in range(per):
            if kind == "swap":
                src, there, here, dev = srcs[a], lands[a], lands[a], (x, y, 1 - c)
            elif kind == "spread":
                k = j + 1
                dev = ((1 - x) if k & 4 else x, (1 - y) if k & 2 else y, (1 - c) if k & 1 else c)
                mine = lands[a].at[4 * x + 2 * y + c]
                src, there, here = mine, mine, lands[a].at[4 * dev[0] + 2 * dev[1] + dev[2]]
            else:
                px, py = peers[j]
                dev = (px, py, c)
                if kind == "gather":
                    src, there, here = lands[a].at[me], lands[a].at[me], lands[a].at[2 * px + py]
                elif kind in ("gather_half", "share_half"):
                    half = lands[a].shape[1] // 2
                    mine, other = pl.ds(c * half, half), pl.ds((1 - c) * half, half)
                    if kind == "gather_half":
                        src = there = lands[a].at[me, mine]
                        here = lands[a].at[2 * px + py, mine]
                    else:
                        src = there = lands[a].at[2 * px + py, mine]
                        here = lands[a].at[2 * px + py, other]
                        dev = (x, y, 1 - c)
                else:
                    src, there, here = srcs[a].at[2 * px + py], lands[a].at[j], lands[a].at[j]
            mk = functools.partial(
                pltpu.make_async_remote_copy, src_ref=src, send_sem=send.at[per * a + j],
                recv_sem=recv.at[per * a + j], device_id=dev, device_id_type=_MESH)
            out.append((functools.partial(mk, dst_ref=there), functools.partial(mk, dst_ref=here)))
    return out


_CORE_PAIR = ("swap", "share_half")
CORE_PAIR_BARRIER_ID = 0
BARRIER_IDS = dict(gather_rest=1, scatter_mlp1=2, scatter_att=3, scatter_mlp0=4, scatter_gates=5, scatter_rg_in=6,
                   gather_gates=7, spread_vec=8)


def _entry_peers(kind):
    x, y, c, peers = _place()
    if kind in _CORE_PAIR:
        return [(x, y, 1 - c)]
    if kind == "spread":
        return [((1 - x) if k & 4 else x, (1 - y) if k & 2 else y, (1 - c) if k & 1 else c)
                for k in range(1, N_DEVICES)]
    return [(px, py, c) for px, py in peers]


def _entry_params(kind, barrier_id):
    if kind in _CORE_PAIR:
        barrier_id = CORE_PAIR_BARRIER_ID
    collective = {} if barrier_id is None else dict(collective_id=barrier_id)
    return pltpu.CompilerParams(has_side_effects=_EFFECT, **collective)


def _entry_handshake(kind, barrier_id):
    if kind in _CORE_PAIR or barrier_id is not None:
        barrier = pltpu.get_barrier_semaphore()
        peers = _entry_peers(kind)
        for peer in peers:
            pl.semaphore_signal(barrier, inc=1, device_id=peer, device_id_type=_MESH)
        pl.semaphore_wait(barrier, len(peers))


def _exchange_start(kind, srcs, lands, name, after=None, barrier_id=None):
    arrays = list(srcs) + list(lands)
    n_s, n, n_all = len(srcs), len(lands), len(srcs) + len(lands)
    n_sem = _COPIES[kind] * n
    order = _after_operand(after)
    n_x = len(order)

    def body(*refs):
        _entry_handshake(kind, barrier_id)
        send, recv = refs[n_all + n_x], refs[n_all + n_x + 1]
        token = refs[-1]
        for started, _ in _split_copies(kind, refs[:n_s], refs[n_s:n_all], send, recv):
            started().start()
        token[...] = jnp.zeros(token.shape, F32)

    res = pl.pallas_call(
        body, name=name,
        out_shape=(pltpu.SemaphoreType.DMA((n_sem,)), pltpu.SemaphoreType.DMA((n_sem,)),
                   *[pltpu.HBM(a.shape, a.dtype) for a in arrays], jax.ShapeDtypeStruct((SUBLANES, LANES), F32)),
        in_specs=[_HBM] * n_all + [_ANY] * n_x,
        out_specs=(_SEM, _SEM, *[_HBM] * n_all, pl.BlockSpec(memory_space=pltpu.VMEM)),
        input_output_aliases={i: 2 + i for i in range(n_all)},
        compiler_params=_entry_params(kind, barrier_id),
    )(*[pltpu.with_memory_space_constraint(a, pltpu.HBM) for a in arrays], *order)
    return (res[0], res[1], res[2:2 + n_s], res[2 + n_s:2 + n_all]), res[-1]


def _gather_start_groups(land_groups, name, after=None, kind="gather", barrier_id=None):
    arrays = [a for group in land_groups for a in group]
    n_all, n_g = len(arrays), len(land_groups)
    order = _after_operand(after)
    n_x = len(order)

    def body(*refs):
        _entry_handshake(kind, barrier_id)
        first = 0
        for gi, group in enumerate(land_groups):
            send, recv = refs[n_all + n_x + 2 * gi], refs[n_all + n_x + 2 * gi + 1]
            for started, _ in _split_copies(kind, [], refs[first:first + len(group)], send, recv):
                started().start()
            first += len(group)
        refs[-1][...] = jnp.zeros(refs[-1].shape, F32)

    sems = [pltpu.SemaphoreType.DMA((_COPIES[kind] * len(group),)) for group in land_groups for _ in range(2)]
    res = pl.pallas_call(
        body, name=name,
        out_shape=(*sems, *[pltpu.HBM(a.shape, a.dtype) for a in arrays], jax.ShapeDtypeStruct((SUBLANES, LANES), F32)),
        in_specs=[_HBM] * n_all + [_ANY] * n_x,
        out_specs=(*[_SEM] * (2 * n_g), *[_HBM] * n_all, pl.BlockSpec(memory_space=pltpu.VMEM)),
        input_output_aliases={i: 2 * n_g + i for i in range(n_all)},
        compiler_params=_entry_params(kind, barrier_id),
    )(*[pltpu.with_memory_space_constraint(a, pltpu.HBM) for a in arrays], *order)
    handles, first = [], 2 * n_g
    for gi, group in enumerate(land_groups):
        handles.append((res[2 * gi], res[2 * gi + 1], [], res[first:first + len(group)]))
        first += len(group)
    return handles, res[-1]


def _exchange_wait(kind, handle, after, name):
    send, recv, srcs, lands = handle
    arrays = list(srcs) + list(lands)
    n_s, n_all = len(srcs), len(arrays)
    order = list(after) if isinstance(after, (list, tuple)) else [after]

    def body(*refs):
        for started, landing in _split_copies(kind, refs[:n_s], refs[n_s:n_all], refs[n_all], refs[n_all + 1]):
            started().wait_send()
            landing().wait_recv()

    res = pl.pallas_call(
        body, name=name, out_shape=[pltpu.HBM(a.shape, a.dtype) for a in arrays],
        in_specs=[_HBM] * n_all + [_SEM, _SEM] + [_ANY] * len(order), out_specs=[_HBM] * n_all,
        input_output_aliases={i: i for i in range(n_all)},
        compiler_params=pltpu.CompilerParams(has_side_effects=_EFFECT),
    )(*arrays, send, recv, *order)
    return res[:n_s], res[n_s:]


def _index_operand(i):
    return jnp.reshape(i, (1,)).astype(jnp.int32)


def _cast_into_slot(src, row0, rows, me, dtype, name, after=None, add=None, n_slots=N_CHIPS):
    cols = src.shape[1]
    tm = min(512, rows)
    order = _after_operand(after)
    terms = [src] + ([] if add is None else [add])

    def body(me_ref, *rest):
        val = rest[0][...]
        if add is not None:
            val = val + rest[1][...]
        rest[-1][...] = val.astype(dtype)

    return pl.pallas_call(
        body, name=name,
        grid_spec=pltpu.PrefetchScalarGridSpec(
            num_scalar_prefetch=1, grid=(rows // tm,),
            in_specs=[pl.BlockSpec((tm, cols), lambda i, me_ref: (i + row0 // tm, 0))] * len(terms)
            + [_ANY] * len(order),
            out_specs=pl.BlockSpec((None, tm, cols), lambda i, me_ref: (me_ref[0], i, 0))),
        out_shape=jax.ShapeDtypeStruct((n_slots, rows, cols), dtype), compiler_params=_params(("parallel",)),
    )(_index_operand(me), *terms, *order)


def _sum_slots(mine, r, me, name):
    _, rows, cols = r.shape
    tm = min(512, rows)

    def body(me_ref, own_ref, r_ref, o_ref):
        o_ref[...] = ((own_ref[...].astype(F32) + r_ref[0].astype(F32)) + r_ref[1].astype(F32)) + r_ref[2].astype(F32)

    return pl.pallas_call(
        body, name=name,
        grid_spec=pltpu.PrefetchScalarGridSpec(
            num_scalar_prefetch=1, grid=(rows // tm,),
            in_specs=[pl.BlockSpec((None, tm, cols), lambda i, me_ref: (me_ref[0], i, 0)),
                      pl.BlockSpec((N_CHIPS - 1, tm, cols), lambda i, me_ref: (0, i, 0))],
            out_specs=pl.BlockSpec((tm, cols), lambda i, me_ref: (i, 0))),
        out_shape=jax.ShapeDtypeStruct((rows, cols), F32), compiler_params=_params(("parallel",)),
    )(_index_operand(me), mine, r)


def _adamw(w, m, v, ps, qs, name):
    rows, cols = w.shape
    seg_rows = ps[0].shape[0]
    tm = min(512, seg_rows)
    while seg_rows % tm:
        tm -= SUBLANES
    per, n_seg = seg_rows // tm, len(ps)
    parts = list(ps) + ([] if qs is None else list(qs))

    def body(w_ref, m_ref, v_ref, *rest):
        g_refs, outs = rest[:len(parts)], rest[len(parts):]
        grad = lambda s: g_refs[s][...] if qs is None else g_refs[s][...] + g_refs[n_seg + s][...]
        g = grad(0)
        for s in range(1, n_seg):
            g = jnp.where(pl.program_id(0) >= s * per, grad(s), g)
        m1 = ADAM_B1 * m_ref[...] + (1.0 - ADAM_B1) * g
        v1 = ADAM_B2 * v_ref[...] + (1.0 - ADAM_B2) * (g * g)
        m_hat = m1 / (1.0 - ADAM_B1 ** ADAM_STEP)
        v_hat = v1 / (1.0 - ADAM_B2 ** ADAM_STEP)
        outs[0][...] = g
        outs[1][...] = (-ADAM_LR) * (m_hat / (jnp.sqrt(v_hat) + ADAM_EPS) + ADAM_WD * w_ref[...])
        outs[2][...] = m1
        outs[3][...] = v1

    row_spec = pl.BlockSpec((tm, cols), lambda i: (i, 0))
    seg_spec = lambda s: pl.BlockSpec((tm, cols), lambda i: (jnp.clip(i - s * per, 0, per - 1), 0))
    return pl.pallas_call(
        body, name=name, grid=(rows // tm,),
        in_specs=[row_spec] * 3 + [seg_spec(s) for s in range(n_seg)] * (1 if qs is None else 2),
        out_specs=[row_spec] * 4, out_shape=[jax.ShapeDtypeStruct((rows, cols), F32)] * 4,
        compiler_params=_params(("arbitrary",)),
    )(w, m, v, *parts)


def _put_cols(shard, me):
    full = jnp.zeros((shard.shape[0], D_MODEL), F32)
    return lax.dynamic_update_slice(full, shard, (0, me * (D_MODEL // N_CHIPS)))


def _gate_vec_slot(b_a, b_x, lam):
    return _rows_at(b_a, _ROW_BA) + _rows_at(b_x, _ROW_BX) + _rows_at(lam, _ROW_LAM)


def _pack_vec(p, me):
    return jnp.concatenate([
        _rows_at(p["norm_mix_g"], 0) + _rows_at(p["norm_mlp_g"], 2) + _rows_at(p["final_g"][None], 4),
        _rows_at(_put_cols(p["rg_conv_w"][0, :, 0, :], me), 0) + _rows_at(p["rg_conv_b"], 4),
        _gate_vec_slot(_put_cols(p["rg_b_a"][0], me), _put_cols(p["rg_b_x"][0], me), _put_cols(p["rg_lam"][0], me)),
        _qk_slot(p["at_q_g"], p["at_k_g"]),
    ], axis=0)


def _unpack_vec(r, me):
    def cols(rows):
        return lax.dynamic_slice(rows, (0, me * (D_MODEL // N_CHIPS)), (rows.shape[0], D_MODEL // N_CHIPS))

    gate = r[16:24]
    return dict(
        norm_mix_g=r[0:2], norm_mlp_g=r[2:4], final_g=r[4], rg_conv_w=cols(r[8:12])[None, :, None, :],
        rg_conv_b=r[12:13], rg_b_a=cols(gate[_ROW_BA:_ROW_BA + 2])[None], rg_b_x=cols(gate[_ROW_BX:_ROW_BX + 2])[None],
        rg_lam=cols(gate[_ROW_LAM:_ROW_LAM + 2])[None], at_q_g=r[24:25, 0:HEAD_DIM],
        at_k_g=r[24:25, HEAD_DIM:2 * HEAD_DIM])


_WEIGHTS = ['norm_mix_g', 'norm_mlp_g', 'rg_w_in', 'rg_conv_w', 'rg_conv_b', 'rg_w_a', 'rg_b_a', 'rg_w_x', 'rg_b_x',
            'rg_lam', 'rg_w_out', 'at_w_qkv', 'at_q_g', 'at_k_g', 'at_w_o', 'mlp_w_up', 'mlp_w_down', 'final_g']
_BIG = dict(rg_w_in=["rg_w_in"], rg_w_out=["rg_w_out"], at_w_qkv=["at_w_qkv"], at_w_o=["at_w_o"],
            mlp_w_up=["up0", "up1"], mlp_w_down=["down0", "down1"])


def kernel(x, *args):
    n_w = len(_WEIGHTS)
    w = dict(zip(_WEIGHTS, args[:n_w]))
    target = args[n_w]
    m = dict(zip(_WEIGHTS, args[n_w + 1:2 * n_w + 1]))
    v = dict(zip(_WEIGHTS, args[2 * n_w + 1:3 * n_w + 1]))
    B, L, _ = x.shape
    T = B * L
    me = 2 * lax.axis_index("x") + lax.axis_index("y")

    vec = jnp.concatenate([_gate_vec_slot(w["rg_b_a"][0], w["rg_b_x"][0], w["rg_lam"][0]),
                           _rows_at(w["rg_conv_w"][0, :, 0, :], 0)], axis=0)
    flat = lambda a: a.reshape(-1, a.shape[-1])
    rows_of = lambda k: w[k].shape[-2]
    groups = [("rg", [("rg_w_in", 0, BF16), (vec, 0, F32)]), ("rg_out", [("rg_w_out", 0, BF16)]),
              ("mlp0_up", [("mlp_w_up", 0, BF16)]), ("mlp0_down", [("mlp_w_down", 0, BF16)]),
              ("att", [("at_w_qkv", 0, BF16), ("at_w_o", 0, BF16)]),
              ("mlp1", [("mlp_w_up", 1, BF16), ("mlp_w_down", 1, BF16)])]

    def landing_zones(group, members, after):
        lands = []
        for n, (k, layer, dtype) in enumerate(members):
            src, rows = (flat(w[k]), rows_of(k)) if isinstance(k, str) else (k, k.shape[0])
            lands.append(_cast_into_slot(src, layer * rows, rows, me, dtype, f"place_{group}{n}", after=after))
        return lands

    halves, gathers = {}, {}
    halves["rg"], tok = _exchange_start("gather_half", [], landing_zones(*groups[0], None), "gather_rg_start")
    handles, tok = _gather_start_groups([landing_zones(g, members, tok) for g, members in groups[1:]],
                                        "gather_rest_start", after=tok, kind="gather_half",
                                        barrier_id=BARRIER_IDS["gather_rest"])
    halves.update(zip([g for g, _ in groups[1:]], handles))
    wcat = _make_wcat(w["rg_w_a"], w["rg_w_x"]).astype(BF16)

    packs = [_pack_vec(p, me) for p in (w, m, v)]

    ready = {}

    def share(some, after, name):
        landed = [_exchange_wait("gather_half", halves[g], after, f"gather_{g}_landed")[1] for g in some]
        handles, token = _gather_start_groups(landed, name, kind="share_half")
        gathers.update(zip(some, handles))
        return token

    tok = share(["rg"], [tok, wcat] + packs, "share_rg_start")

    def fetch(what, after):
        if what in ready:
            return ready[what]
        group = "mlp1" if what.startswith("mlp1") else what
        if group == "rg_out":
            share(["mlp0_up"], after, "share_early_start")
        elif group == "mlp0_up":
            share(["mlp0_down", "att"], after, "share_mid_start")
        _, full = _exchange_wait("share_half", gathers[group], after, f"gather_{group}_wait")
        if group == "rg":
            share(["rg_out"], after, "share_rg_out_start")
        if group == "att":
            share(["mlp1"], after, "share_mlp1_start")
        if group == "rg":
            vec_full = jnp.transpose(full[1], (1, 0, 2)).reshape(2 * SUBLANES, D_MODEL)
            conv_wb = vec_full[SUBLANES:] + _rows_at(w["rg_conv_b"], 4)
            return full[0], conv_wb, wcat, vec_full[:SUBLANES]
        if group == "rg_out":
            return full[0].reshape(D_MODEL, D_MODEL)
        if group == "att":
            return full[0], full[1].reshape(D_MODEL, D_MODEL)
        if group == "mlp1":
            ready["mlp1_up"], ready["mlp1_down"] = full[0], full[1].reshape(4 * D_MODEL, D_MODEL)
            return ready[what]
        return full[0] if group == "mlp0_up" else full[0].reshape(4 * D_MODEL, D_MODEL)

    names = dict(mlp1=["up1", "down1"], att=["at_w_qkv", "at_w_o"], mlp0=["up0", "down0", "rg_w_out"],
                 rg_in=["rg_w_in"], gates=["rg_w_a", "rg_w_x"])
    scatters, swaps, P, Q, res = {}, [], {}, {}, {}

    def start_scatter(group, grads):
        srcs = [g.reshape(N_CHIPS, -1, g.shape[-1]) for g in grads]
        lands = [lax.empty((N_CHIPS - 1,) + s.shape[1:], s.dtype) for s in srcs]
        scatters[group], token = _exchange_start("scatter", srcs, lands, f"scatter_{group}_start",
                                                 barrier_id=BARRIER_IDS[f"scatter_{group}"])
        return token

    def settle(groups, after):
        keys, parts = [], []
        for group in groups:
            srcs, lands = _exchange_wait("scatter", scatters[group], after, f"scatter_{group}_wait")
            for k, s, r in zip(names[group], srcs, lands):
                keys.append(k)
                parts.append(_sum_slots(s, r, me, f"sum_{k}"))
        handle, token = _exchange_start("swap", parts, [lax.empty(p.shape, F32) for p in parts],
                                        f"swap_{groups[0]}_start")
        swaps.append((keys, handle, f"swap_{groups[0]}_wait"))
        return token

    def finish(after):
        for keys, handle, name in swaps:
            mine, theirs = _exchange_wait("swap", handle, after, name)
            P.update(zip(keys, mine))
            Q.update(zip(keys, theirs))
        swaps.clear()
        last = after
        for k, parts in _BIG.items():
            if k in res or any(p not in P for p in parts):
                continue
            shape = w[k].shape
            two_d = lambda a: a.reshape(-1, shape[-1])
            outs = _adamw(two_d(w[k]), two_d(m[k]), two_d(v[k]), [P[p] for p in parts], [Q[p] for p in parts],
                          f"adamw_{k}")
            res[k] = [o.reshape(shape) for o in outs]
            last = outs[0]
        if "rg_w_a" in P and "gates" not in gathers:
            lands = [_cast_into_slot(P[k], 0, P[k].shape[0], me, F32, f"place_{k}", after=last, add=Q[k])
                     for k in names["gates"]]
            gathers["gates"], last = _exchange_start("gather", [], lands, "gather_gates_start", after=last,
                                                     barrier_id=BARRIER_IDS["gather_gates"])
        return last

    def emit(event, arrays):
        if event == "point_attn_done":
            return None
        if event == "point_mix_done":
            return settle(["mlp1", "att", "mlp0"], arrays[0])
        token = start_scatter(event, arrays)
        if event == "rg_in":
            return finish(settle(["gates"], token))
        return token

    P_vec = dict(norm_mix_g=w["norm_mix_g"], norm_mlp_g=w["norm_mlp_g"], final_g=w["final_g"][None],
                 q_g=w["at_q_g"], k_g=w["at_k_g"])
    grad_x, vec_part = _local_step(x.reshape(T, D_MODEL), target.reshape(T, D_MODEL), P_vec, fetch, emit, B, L,
                                   after=tok)

    me8 = 2 * me + lax.axis_index("c")
    vec_slots = _cast_into_slot(vec_part, 0, VEC_ROWS, me8, F32, "place_vec", n_slots=N_DEVICES)
    spread, tok = _exchange_start("spread", [], [vec_slots], "spread_vec_start", barrier_id=BARRIER_IDS["spread_vec"])
    last = settle(["rg_in"], tok)
    _, gate_grads = _exchange_wait("gather", gathers["gates"], last, "gather_gates_wait")
    for k, g in zip(names["gates"], gate_grads):
        two_d = lambda a: a.reshape(g.shape[0] * g.shape[1], g.shape[2])
        outs = _adamw(two_d(w[k]), two_d(m[k]), two_d(v[k]), [two_d(g)], None, f"adamw_{k}")
        res[k] = [o.reshape(w[k].shape) for o in outs]
        last = outs[0]
    _, (vec_all,) = _exchange_wait("spread", spread, last, "spread_vec_wait")
    vec_grad = _sum_leading(vec_all, "sum_vec")
    loss = vec_grad[LOSS_ROW, 0]
    outs = _adamw(*packs, [vec_grad], None, "adamw_vec")
    finish(outs[0])
    unpacked = [_unpack_vec(o, me) for o in outs]
    for k in _WEIGHTS:
        if k not in res:
            res[k] = [u[k] for u in unpacked]

    result = [loss, grad_x.reshape(B, L, D_MODEL)]
    for slot in range(4):
        result += [res[k][slot] for k in _WEIGHTS]
    return tuple(result)
```

```python
import functools
import math

import jax
import jax.numpy as jnp
import numpy as np
from jax import lax
from jax.experimental import pallas as pl
from jax.experimental.pallas import tpu as pltpu

F32 = jnp.float32
BF16 = jnp.bfloat16

D_MODEL = 1024
HEAD_DIM = 128
N_HEADS = 8
N_KV = 2
GROUP = N_HEADS // N_KV
LRU_BLOCKS = 8
LRU_BW = 128
GRID_W = 64
ROPE_THETA = 10000.0
EPS = 1e-6
RG_C = 8.0
SCALE = 1.0 / math.sqrt(HEAD_DIM)
N_CHIPS = 4

ADAM_LR = 0.001
ADAM_B1 = 0.9
ADAM_B2 = 0.999
ADAM_EPS = 1e-08
ADAM_WD = 0.01
ADAM_STEP = 10

V7X_VMEM_BYTES = 64 * 1024 * 1024
VMEM_LIMIT = V7X_VMEM_BYTES * 3 // 4
LANES = 128
SUBLANES = 8

N_DEVICES = 8
VEC_ROWS = 32
LOSS_ROW = 5


def _params(sem):
    return pltpu.CompilerParams(dimension_semantics=sem, vmem_limit_bytes=VMEM_LIMIT)


_ANY = pl.BlockSpec(memory_space=pl.ANY)
_NN = (((1,), (0,)), ((), ()))
_NT = (((1,), (1,)), ((), ()))
_TN = (((0,), (0,)), ((), ()))


def _after_operand(after):
    return [] if after is None else [after]


def _fit(t, n):
    if n <= t:
        return n
    c = (t // LANES) * LANES
    while n % c:
        c -= LANES
    return c


MM_VMEM_BUDGET = VMEM_LIMIT * 3 // 4
def _mm_tiles(M, K, ns, n_total, out_dtypes, extras, whole_rows):
    for tm in (2048, 1024, 512, 256, 128):
        for tn in ((ns,) if whole_rows else (1024, 512, 256)):
            tn = _fit(tn, ns)
            per_row = 2 * (2 * K) + 4 * tn + sum(2 * tn * jnp.dtype(d).itemsize for d in out_dtypes)
            per_row += sum(2 * tn * e.dtype.itemsize for e in extras)
            b_buffers = 1 if tn == n_total else 2
            if M % tm == 0 and b_buffers * (2 * K * tn) + tm * per_row <= MM_VMEM_BUDGET:
                return tm, tn
    raise ValueError(f"no tile fits VMEM for M={M} K={K} N={ns}")


def _mm(a, b, *, mode, name, out_dtypes=(F32,), b_shard=False, o_shard=False, extras=(), epi=None, after=None,
        bcast=(), accs=(), ref_epi=None, out_cols=None):
    if mode == "tn":
        K, M = a.shape
        N = b.shape[1]
    else:
        M, K = a.shape
        if mode == "nn":
            N = b.shape[0] * b.shape[2] if b_shard else b.shape[1]
        else:
            N = b.shape[1] if b_shard else b.shape[0]
    ns = N
    if b_shard and mode == "nn":
        ns = b.shape[2]
    elif o_shard:
        ns = N // N_CHIPS
    tm, tn = _mm_tiles(M, K, ns, N, out_dtypes, extras, whole_rows=ref_epi is not None)
    if ref_epi is not None:
        tm = min(tm, 512)
    grid = (M // tm, N // tn)
    q = ns // tn
    once = dict(pipeline_mode=pl.Buffered(1)) if tn == N else {}

    if mode == "tn":
        a_spec = pl.BlockSpec((K, tm), lambda i, j: (0, i))
        b_spec = pl.BlockSpec((K, tn), lambda i, j: (0, j), **once)
        dims = _TN
    elif mode == "nn":
        a_spec = pl.BlockSpec((tm, K), lambda i, j: (i, 0))
        if b_shard:
            b_spec = pl.BlockSpec((None, K, tn), lambda i, j: (j // q, 0, j % q), **once)
        else:
            b_spec = pl.BlockSpec((K, tn), lambda i, j: (0, j), **once)
        dims = _NN
    else:
        a_spec = pl.BlockSpec((tm, K), lambda i, j: (i, 0))
        if b_shard:
            ks = b.shape[2]
            b_spec = pl.BlockSpec((N_CHIPS, tn, ks), lambda i, j: (0, j, 0), **once)
        else:
            b_spec = pl.BlockSpec((tn, K), lambda i, j: (j, 0), **once)
        dims = _NT

    if o_shard:
        o_specs = [pl.BlockSpec((None, tm, tn), lambda i, j: (j // q, i, j % q))]
        o_shapes = [jax.ShapeDtypeStruct((N_CHIPS, M, ns), out_dtypes[0])]
    else:
        o_specs = [pl.BlockSpec((tm, tn), lambda i, j: (i, j)) for _ in out_dtypes]
        o_shapes = [jax.ShapeDtypeStruct((M, N if out_cols is None else out_cols[n]), dt)
                    for n, dt in enumerate(out_dtypes)]
    e_specs = [pl.BlockSpec((tm, tn), lambda i, j: (i, j)) for _ in extras]
    e_specs += [pl.BlockSpec(v.shape, lambda i, j: (0, 0)) for v in bcast]
    o_specs += [pl.BlockSpec(s, lambda i, j: (0, 0)) for s in accs]
    o_shapes += [jax.ShapeDtypeStruct(s, F32) for s in accs]
    n_e, n_b, n_o, n_a = len(extras), len(bcast), len(out_dtypes), len(accs)
    order = _after_operand(after)
    n_x = len(order)
    if epi is None:
        epi = lambda acc: (acc,)

    def body(a_ref, b_ref, *rest):
        e_refs, b_refs = rest[:n_e], rest[n_e:n_e + n_b]
        o_refs = rest[n_e + n_b + n_x:n_e + n_b + n_x + n_o]
        a_refs = rest[n_e + n_b + n_x + n_o:]
        if n_a:
            @pl.when((pl.program_id(0) == 0) & (pl.program_id(1) == 0))
            def _():
                for r in a_refs:
                    r[...] = jnp.zeros(r.shape, F32)
        if mode == "nt" and b_shard:
            acc = None
            for s in range(N_CHIPS):
                part = lax.dot_general(a_ref[:, s * ks:(s + 1) * ks], b_ref[s], dims, preferred_element_type=F32)
                acc = part if acc is None else acc + part
        else:
            acc = lax.dot_general(a_ref[...], b_ref[...], dims, preferred_element_type=F32)
        if ref_epi is not None:
            ref_epi(acc, e_refs, b_refs, o_refs, a_refs)
            return
        outs = epi(acc, *[r[...] for r in e_refs])
        for r, o in zip(o_refs, outs):
            r[...] = o.astype(r.dtype)

    outs = pl.pallas_call(
        body, name=name, grid=grid, in_specs=[a_spec, b_spec] + e_specs + [_ANY] * n_x, out_specs=o_specs,
        out_shape=o_shapes, compiler_params=_params(("arbitrary", "arbitrary") if n_a else ("parallel", "parallel")),
    )(a, b, *extras, *bcast, *order)
    return outs[0] if n_o + n_a == 1 else outs


def _rowwise(fn, rows, bcast, outs, accs=(), *, tm, name, after=None):
    def norm(r):
        return r if isinstance(r, tuple) else (r, r.shape[1], 0)

    rows = [norm(r) for r in rows]
    T = rows[0][0].shape[0]
    tm = min(tm, T)
    while T % tm:
        tm -= SUBLANES
    n_r, n_b, n_o, n_a = len(rows), len(bcast), len(outs), len(accs)
    order = _after_operand(after)
    n_x = len(order)
    in_specs = [pl.BlockSpec((tm, c), functools.partial(lambda i, cb: (i, cb), cb=cb)) for _, c, cb in rows]
    in_specs += [pl.BlockSpec(b.shape, lambda i: (0, 0)) for b in bcast] + [_ANY] * n_x
    out_specs = [pl.BlockSpec((tm, o[0]), lambda i: (i, 0)) for o in outs]
    out_specs += [pl.BlockSpec(s, lambda i: (0, 0)) for s in accs]
    out_shape = [jax.ShapeDtypeStruct((T, o[2] if len(o) > 2 else o[0]), o[1]) for o in outs]
    out_shape += [jax.ShapeDtypeStruct(s, F32) for s in accs]

    def body(*refs):
        in_refs = refs[:n_r]
        b_refs = refs[n_r:n_r + n_b]
        o_refs = refs[n_r + n_b + n_x:n_r + n_b + n_x + n_o]
        a_refs = refs[n_r + n_b + n_x + n_o:]
        if n_a:
            @pl.when(pl.program_id(0) == 0)
            def _():
                for r in a_refs:
                    r[...] = jnp.zeros(r.shape, F32)
        fn(in_refs, b_refs, o_refs, a_refs)

    res = pl.pallas_call(
        body, name=name, grid=(T // tm,), in_specs=in_specs, out_specs=out_specs, out_shape=out_shape,
        compiler_params=_params(("arbitrary",) if n_a else ("parallel",)),
    )(*[r[0] for r in rows], *bcast, *order)
    return res


def _rsum(x):
    return jnp.sum(x, axis=0, keepdims=True)


def _rms_fwd(x, g, name, after=None):
    def fn(ins, bs, outs, accs):
        xv = ins[0][...]
        r = lax.rsqrt(jnp.mean(xv * xv, axis=-1, keepdims=True) + EPS)
        outs[0][...] = (xv * r * bs[0][...]).astype(BF16)

    return _rowwise(fn, [x], [g], [(D_MODEL, BF16)], tm=512, name=name, after=after)[0]


def _rms_bwd_math(xv, dh, g):
    r = lax.rsqrt(jnp.mean(xv * xv, axis=-1, keepdims=True) + EPS)
    hn = xv * r
    dgh = dh * g
    dx = r * (dgh - hn * jnp.mean(dgh * hn, axis=-1, keepdims=True))
    return dx, _rsum(dh * hn)


def _mm_norm_bwd(dy, w, x, dres, g, name, after=None):
    def epilogue(acc, e_refs, b_refs, o_refs, a_refs):
        dx, dg = _rms_bwd_math(e_refs[0][...], acc, b_refs[0][...])
        dx = dx + e_refs[1][...]
        o_refs[0][...] = dx
        o_refs[1][...] = dx.astype(BF16)
        a_refs[0][...] += dg

    return _mm(dy, w, mode="nt", b_shard=True, out_dtypes=(F32, BF16), extras=(x, dres), bcast=(g,),
               accs=((1, D_MODEL),), ref_epi=epilogue, name=name, after=after)


def _mm_res_norm(a, w, res, g, name):
    def epilogue(acc, e_refs, b_refs, o_refs, a_refs):
        xv = acc + e_refs[0][...]
        o_refs[0][...] = xv
        r = lax.rsqrt(jnp.mean(xv * xv, axis=-1, keepdims=True) + EPS)
        o_refs[1][...] = (xv * r * b_refs[0][...]).astype(BF16)

    return _mm(a, w, mode="nn", out_dtypes=(F32, BF16), extras=(res,), bcast=(g,), ref_epi=epilogue, name=name)


def _mm_final_loss(a, w, res, target, g, name):
    def epilogue(acc, e_refs, b_refs, o_refs, a_refs):
        xv = acc + e_refs[0][...]
        gv = b_refs[0][...]
        r = lax.rsqrt(jnp.mean(xv * xv, axis=-1, keepdims=True) + EPS)
        e = xv * r * gv - e_refs[1][...]
        tok = jnp.mean(e * e, axis=-1, keepdims=True)
        a_refs[0][...] += 0.5 * jnp.sum(tok, axis=0, keepdims=True) * jnp.ones((1, LANES), F32)
        dx, dg = _rms_bwd_math(xv, e * (1.0 / D_MODEL), gv)
        o_refs[0][...] = dx
        o_refs[1][...] = dx.astype(BF16)
        a_refs[1][...] += dg

    return _mm(a, w, mode="nn", out_dtypes=(F32, BF16), extras=(res, target), bcast=(g,),
               accs=((1, LANES), (1, D_MODEL)), ref_epi=epilogue, name=name)


def _relu2(acc):
    r = jnp.maximum(acc, 0.0)
    return r * r, r


def _mlp_fwd(x, h, fetch, tag, finish):
    w_up = fetch(f"mlp{tag}_up", h)
    a, r = _mm(h, w_up, mode="nn", b_shard=True, out_dtypes=(BF16, BF16), epi=_relu2, name=f"mlp{tag}_up")
    w_down = fetch(f"mlp{tag}_down", a)
    return finish(a, w_down, x, f"mlp{tag}_down"), (h, a, r, w_up, w_down)


def _mlp_bwd(x, g, saved, dx, dx_bf, tag, after):
    h, a, r, w_up, w_down = saved
    d_down = _mm(a, dx_bf, mode="tn", out_dtypes=(BF16,), name=f"mlp{tag}_dwdown", after=after)
    dup = _mm(dx_bf, w_down, mode="nt", extras=(r,), out_dtypes=(BF16,),
              epi=lambda acc, rv: (acc * (2.0 * rv.astype(F32)),), name=f"mlp{tag}_dup")
    d_up = _mm(h, dup, mode="tn", o_shard=True, out_dtypes=(BF16,), name=f"mlp{tag}_dwup")
    dx_new, dx_new_bf, dg = _mm_norm_bwd(dup, w_up, x, dx, g, f"mlp{tag}_dh")
    return dx_new, dx_new_bf, dg, d_up, d_down


def _rope_tables(L, B):
    rows = L // GRID_W
    row = np.repeat(np.arange(rows, dtype=np.float32), GRID_W)
    col = np.tile(np.arange(GRID_W, dtype=np.float32), rows)
    inv = (ROPE_THETA ** (-np.arange(HEAD_DIM // 4, dtype=np.float32) / (HEAD_DIM // 4))).astype(np.float32)
    ar, ac = row[:, None] * inv, col[:, None] * inv
    cos = np.concatenate([np.cos(ar), np.cos(ar), np.cos(ac), np.cos(ac)], axis=-1)
    sin = np.concatenate([-np.sin(ar), np.sin(ar), -np.sin(ac), np.sin(ac)], axis=-1)
    return jnp.asarray(np.tile(cos, (B, 1)), F32), jnp.asarray(np.tile(sin, (B, 1)), F32)


def _swap_halves(x):
    lane = lax.broadcasted_iota(jnp.int32, x.shape, 1)
    return jnp.where((lane % 64) < 32, pltpu.roll(x, HEAD_DIM - 32, 1), pltpu.roll(x, 32, 1))


def _qk_prep(qkv, cos, sin, q_g, k_g):
    def fn(ins, bs, outs, accs):
        c, s = ins[1][...], ins[2][...]
        for h in range(N_HEADS + N_KV):
            xv = ins[0][:, h * HEAD_DIM:(h + 1) * HEAD_DIM]
            g = bs[0][...] if h < N_HEADS else bs[1][...]
            r = lax.rsqrt(jnp.mean(xv * xv, axis=-1, keepdims=True) + EPS)
            z = xv * r * g
            y = (z * c + _swap_halves(z) * s).astype(BF16)
            if h < N_HEADS:
                outs[0][:, h * HEAD_DIM:(h + 1) * HEAD_DIM] = y
            else:
                outs[1][:, (h - N_HEADS) * HEAD_DIM:(h - N_HEADS + 1) * HEAD_DIM] = y
        outs[2][...] = ins[0][:, (N_HEADS + N_KV) * HEAD_DIM:].astype(BF16)

    kvw = N_KV * HEAD_DIM
    return _rowwise(fn, [qkv, cos, sin], [q_g, k_g], [(D_MODEL, BF16), (kvw, BF16), (kvw, BF16)], tm=512,
                    name="attn_qk_prep")


def _qk_prep_bwd(qkv, dq, dk, dv, cos, sin, q_g, k_g):
    def fn(ins, bs, outs, accs):
        c, s = ins[4][...], ins[5][...]
        for h in range(N_HEADS + N_KV):
            sl = slice(h * HEAD_DIM, (h + 1) * HEAD_DIM)
            xv = ins[0][:, sl]
            if h < N_HEADS:
                g, dy, acc = bs[0][...], ins[1][:, sl], accs[0]
            else:
                ks = slice((h - N_HEADS) * HEAD_DIM, (h - N_HEADS + 1) * HEAD_DIM)
                g, dy, acc = bs[1][...], ins[2][:, ks], accs[1]
            r = lax.rsqrt(jnp.mean(xv * xv, axis=-1, keepdims=True) + EPS)
            xn = xv * r
            dz = dy * c - _swap_halves(dy) * s
            acc[...] += _rsum(dz * xn)
            dxn = dz * g
            outs[0][:, sl] = (r * (dxn - xn * jnp.mean(dxn * xn, axis=-1, keepdims=True))).astype(BF16)
        outs[0][:, (N_HEADS + N_KV) * HEAD_DIM:] = ins[3][...].astype(BF16)

    return _rowwise(fn, [qkv, dq, dk, dv, cos, sin], [q_g, k_g], [(qkv.shape[1], BF16)],
                    [(1, HEAD_DIM), (1, HEAD_DIM)], tm=512, name="attn_qk_prep_bwd")


_EXP2_SCALE = SCALE * math.log2(math.e)


def _exp_rows(q, k):
    s = lax.dot_general(q, k, _NT, preferred_element_type=F32)
    p = jnp.exp2((s - jnp.max(s, axis=-1, keepdims=True)) * _EXP2_SCALE)
    return p, jnp.sum(p, axis=-1, keepdims=True)


def _attn_fwd(q, k, v, B, L, tq=2048, sub=256):
    tq = min(tq, L)
    sub = min(sub, tq)
    nq = L // tq

    def body(q_ref, k_ref, v_ref, o_ref):
        kv, vv = k_ref[...], v_ref[...]
        for c in range(tq // sub):
            rows = slice(c * sub, (c + 1) * sub)
            p, l = _exp_rows(q_ref[rows, :], kv)
            o = jnp.dot(p.astype(BF16), vv, preferred_element_type=F32)
            o_ref[rows, :] = (o * (1.0 / l)).astype(o_ref.dtype)

    return pl.pallas_call(
        body, name="attn_fwd", grid=(B, N_HEADS, nq),
        in_specs=[pl.BlockSpec((tq, HEAD_DIM), lambda b, h, i: (b * nq + i, h)),
                  pl.BlockSpec((L, HEAD_DIM), lambda b, h, i: (b, h // GROUP)),
                  pl.BlockSpec((L, HEAD_DIM), lambda b, h, i: (b, h // GROUP))],
        out_specs=pl.BlockSpec((tq, HEAD_DIM), lambda b, h, i: (b * nq + i, h)),
        out_shape=jax.ShapeDtypeStruct((B * L, D_MODEL), BF16),
        compiler_params=_params(("parallel", "parallel", "parallel")),
    )(q, k, v)


def _attn_bwd(q, k, v, o, do, B, L, tq=2048, sub=512):
    tq = min(tq, L)
    sub = min(sub, tq)
    nq = L // tq

    def body(q_ref, k_ref, v_ref, o_ref, do_ref, dq_ref, dk_ref, dv_ref):
        @pl.when((pl.program_id(2) == 0) & (pl.program_id(3) == 0))
        def _():
            dk_ref[...] = jnp.zeros(dk_ref.shape, F32)
            dv_ref[...] = jnp.zeros(dv_ref.shape, F32)

        kv, vv = k_ref[...], v_ref[...]
        ps, es, dos, qs = [], [], [], []
        for c in range(tq // sub):
            rows = slice(c * sub, (c + 1) * sub)
            qc, doc = q_ref[rows, :], do_ref[rows, :]
            p, l = _exp_rows(qc, kv)
            inv = 1.0 / l
            dp = lax.dot_general(doc, vv, _NT, preferred_element_type=F32)
            delta = jnp.sum(doc.astype(F32) * o_ref[rows, :].astype(F32), axis=-1, keepdims=True)
            e = (p * (dp - delta)).astype(BF16)
            dq_ref[rows, :] = jnp.dot(e, kv, preferred_element_type=F32) * (inv * SCALE)
            ps.append(p.astype(BF16))
            es.append(e)
            dos.append((doc.astype(F32) * inv).astype(BF16))
            qs.append((qc.astype(F32) * (inv * SCALE)).astype(BF16))
        cat = lambda xs: xs[0] if len(xs) == 1 else jnp.concatenate(xs, axis=0)
        dv_ref[...] += lax.dot_general(cat(ps), cat(dos), _TN, preferred_element_type=F32)
        dk_ref[...] += lax.dot_general(cat(es), cat(qs), _TN, preferred_element_type=F32)

    qmap = lambda b, kh, g, i: (b * nq + i, kh * GROUP + g)
    kmap = lambda b, kh, g, i: (b, kh)
    kvw = N_KV * HEAD_DIM
    return pl.pallas_call(
        body, name="attn_bwd", grid=(B, N_KV, GROUP, nq),
        in_specs=[pl.BlockSpec((tq, HEAD_DIM), qmap), pl.BlockSpec((L, HEAD_DIM), kmap),
                  pl.BlockSpec((L, HEAD_DIM), kmap), pl.BlockSpec((tq, HEAD_DIM), qmap),
                  pl.BlockSpec((tq, HEAD_DIM), qmap)],
        out_specs=[pl.BlockSpec((tq, HEAD_DIM), qmap), pl.BlockSpec((L, HEAD_DIM), kmap),
                   pl.BlockSpec((L, HEAD_DIM), kmap)],
        out_shape=[jax.ShapeDtypeStruct((B * L, D_MODEL), F32), jax.ShapeDtypeStruct((B * L, kvw), F32),
                   jax.ShapeDtypeStruct((B * L, kvw), F32)],
        compiler_params=_params(("parallel", "parallel", "arbitrary", "arbitrary")),
    )(q, k, v, o, do)


def _conv_shift(x, t, L, k):
    if k == 2:
        return x
    if k < 2:
        return jnp.where(t >= 2 - k, pltpu.roll(x, 2 - k, 0), 0.0)
    return jnp.where(t < L - (k - 2), pltpu.roll(x, L - (k - 2), 0), 0.0)


def _conv_apply(x, w_ref, L):
    t = lax.broadcasted_iota(jnp.int32, x.shape, 0)
    acc = w_ref[4:5, :] + w_ref[2:3, :] * x
    for k in (0, 1, 3):
        acc = acc + w_ref[k:k + 1, :] * _conv_shift(x, t, L, k)
    return acc


def _conv_bwd(z, g, wb, dz, B, L, tc=256, after=None):
    noff = D_MODEL // tc
    order = _after_operand(after)

    def body(z_ref, g_ref, w_ref, dz_in, *rest):
        dx_ref, dw_ref = rest[len(order):]

        @pl.when(pl.program_id(1) == 0)
        def _():
            dw_ref[...] = jnp.zeros(dw_ref.shape, F32)

        x, gv = z_ref[...], g_ref[...]
        t = lax.broadcasted_iota(jnp.int32, x.shape, 0)
        dx = w_ref[2:3, :] * gv
        for k in (0, 1, 3):
            dx = dx + w_ref[k:k + 1, :] * _conv_shift(gv, t, L, 4 - k)
        dx_ref[...] = dx.astype(BF16)
        for k in range(4):
            dw_ref[k:k + 1, :] += _rsum(_conv_shift(x, t, L, k) * gv)
        dw_ref[4:5, :] += _rsum(gv)

    return pl.pallas_call(
        body, name="rg_conv_bwd", grid=(noff, B),
        in_specs=[pl.BlockSpec((L, tc), lambda j, b: (b, noff + j)), pl.BlockSpec((L, tc), lambda j, b: (b, j)),
                  pl.BlockSpec((SUBLANES, tc), lambda j, b: (0, j)), _ANY] + [_ANY] * len(order),
        out_specs=[pl.BlockSpec((L, tc), lambda j, b: (b, noff + j)),
                   pl.BlockSpec((SUBLANES, tc), lambda j, b: (0, j))],
        out_shape=[jax.ShapeDtypeStruct(dz.shape, dz.dtype), jax.ShapeDtypeStruct((SUBLANES, D_MODEL), F32)],
        input_output_aliases={3: 0},
        compiler_params=_params(("parallel", "arbitrary")),
    )(z, g, wb, dz, *order)


def _softplus(x):
    return jnp.maximum(x, 0.0) + jnp.log1p(jnp.exp(-jnp.abs(x)))


_ROW_BA, _ROW_BX, _ROW_LAM = 0, 2, 4


def _gate_math(xb, pre, vec_ref, d, sl):
    pa = pre[:, (2 * d) * LRU_BW:(2 * d + 1) * LRU_BW] + vec_ref[_ROW_BA + d:_ROW_BA + d + 1, sl]
    px = pre[:, (2 * d + 1) * LRU_BW:(2 * d + 2) * LRU_BW] + vec_ref[_ROW_BX + d:_ROW_BX + d + 1, sl]
    r = 0.5 * jnp.tanh(0.5 * pa) + 0.5
    i = 0.5 * jnp.tanh(0.5 * px) + 0.5
    slope = (-RG_C) * _softplus(-vec_ref[_ROW_LAM + d:_ROW_LAM + d + 1, sl])
    log_a = r * slope
    a = jnp.exp(log_a)
    om = -jnp.tanh(log_a) * (1.0 + a * a)
    rs = lax.rsqrt(om)
    mult = jnp.where(om > 0.0, om * rs, 0.0)
    return a, mult * (i * xb), (r, i, slope, om, mult, rs)


def _gate_bwd(rec, du_f, da_f, du_b, da_b, wcat, gvec):
    def fn(ins, bs, outs, accs):
        for blk in range(LRU_BLOCKS):
            sl = slice(blk * LRU_BW, (blk + 1) * LRU_BW)
            xb = ins[0][:, sl]
            xb16 = xb.astype(BF16)
            w = bs[0][sl, :]
            pre = jnp.dot(xb16, w, preferred_element_type=F32)
            dx = jnp.zeros_like(xb)
            dpre = []
            for d in range(2):
                a, _, (r, i, slope, om, mult, rs) = _gate_math(xb, pre, bs[1], d, sl)
                du, da = ins[1 + 2 * d][:, sl], ins[2 + 2 * d][:, sl]
                t = du * xb
                d_i = t * mult
                dx = dx + du * mult * i
                dlog = da * a - (t * i) * ((1.0 - om) * rs)
                d_r = dlog * slope
                d_sp = _rsum(dlog * r) * (-RG_C)
                lam = bs[1][_ROW_LAM + d:_ROW_LAM + d + 1, sl]
                accs[2][_ROW_LAM + d:_ROW_LAM + d + 1, sl] += d_sp * (-jax.nn.sigmoid(-lam))
                dpa = d_r * r * (1.0 - r)
                dpx = d_i * i * (1.0 - i)
                accs[2][_ROW_BA + d:_ROW_BA + d + 1, sl] += _rsum(dpa)
                accs[2][_ROW_BX + d:_ROW_BX + d + 1, sl] += _rsum(dpx)
                dpre += [dpa, dpx]
            dpre = jnp.concatenate(dpre, axis=1).astype(BF16)
            dw = lax.dot_general(xb16, dpre, _TN, preferred_element_type=F32)
            for d in range(2):
                rows = slice(d * D_MODEL + blk * LRU_BW, d * D_MODEL + (blk + 1) * LRU_BW)
                accs[0][rows, :] += dw[:, (2 * d) * LRU_BW:(2 * d + 1) * LRU_BW]
                accs[1][rows, :] += dw[:, (2 * d + 1) * LRU_BW:(2 * d + 2) * LRU_BW]
            outs[0][:, sl] = dx + lax.dot_general(dpre, w, _NT, preferred_element_type=F32)

    gate_shape = (2 * D_MODEL, LRU_BW)
    return _rowwise(fn, [rec, du_f, da_f, du_b, da_b], [wcat, gvec], [(D_MODEL, F32)],
                    [gate_shape, gate_shape, (SUBLANES, D_MODEL)], tm=512, name="rg_gate_bwd")


def _as_time_blocks(x):
    return x.reshape(x.shape[0] // SUBLANES, SUBLANES, x.shape[1])


def _scan_call(body, ins, n_out, B, L, tc, name):
    nb = L // SUBLANES
    spec = pl.BlockSpec((nb, SUBLANES, tc), lambda b, j: (b, 0, j))
    T = ins[0].shape[0]
    outs = pl.pallas_call(
        functools.partial(body, nb), name=name, grid=(B, D_MODEL // tc),
        in_specs=[spec] * len(ins), out_specs=[spec] * n_out,
        out_shape=[jax.ShapeDtypeStruct((T // SUBLANES, SUBLANES, D_MODEL), F32)] * n_out,
        compiler_params=_params(("parallel", "parallel")),
    )(*[_as_time_blocks(x) for x in ins])
    return [o.reshape(T, D_MODEL) for o in outs]


def _block_scan(A, U, reverse):
    row = lax.broadcasted_iota(jnp.int32, A.shape, 0)
    for s in (1, 2, 4):
        shift = SUBLANES - s if reverse else s
        valid = (row < SUBLANES - s) if reverse else (row >= s)
        a_sh = jnp.where(valid, pltpu.roll(A, shift, 0), 1.0)
        u_sh = jnp.where(valid, pltpu.roll(U, shift, 0), 0.0)
        U = A * u_sh + U
        A = A * a_sh
    return A, U


_LAST = SUBLANES - 1
SCAN_UNROLL = 8


def _loop_blocks(nb, step, init):
    def group(g, carry):
        for k in range(SCAN_UNROLL):
            carry = step(g * SCAN_UNROLL + k, carry)
        return carry

    return lax.fori_loop(0, nb // SCAN_UNROLL, group, init)


def _scan_bwd(dy, a_f, h_f, a_b, h_b, B, L, tc=256):
    def body(nb, dy_r, af, hf, ab, hb, duf, daf, dub, dab):
        def step(i, carry):
            c1, c2 = carry
            ir = nb - 1 - i
            row = lax.broadcasted_iota(jnp.int32, (SUBLANES, tc), 0)
            a_up = jnp.where(row == _LAST, af[jnp.minimum(ir + 1, nb - 1), :1, :], pltpu.roll(af[ir], _LAST, 0))
            p, lam = _block_scan(a_up, dy_r[ir], True)
            lam = lam + p * c1
            before = hf[jnp.maximum(ir - 1, 0), _LAST:, :] * (ir > 0).astype(F32)
            duf[ir] = lam
            daf[ir] = lam * jnp.where(row == 0, before, pltpu.roll(hf[ir], 1, 0))
            a_dn = jnp.where(row == 0, ab[jnp.maximum(i - 1, 0), _LAST:, :], pltpu.roll(ab[i], 1, 0))
            p2, lam2 = _block_scan(a_dn, dy_r[i], False)
            lam2 = lam2 + p2 * c2
            after = hb[jnp.minimum(i + 1, nb - 1), :1, :] * (i < nb - 1).astype(F32)
            dub[i] = lam2
            dab[i] = lam2 * jnp.where(row == _LAST, after, pltpu.roll(hb[i], _LAST, 0))
            return lam[:1, :], lam2[_LAST:, :]

        zero = jnp.zeros((1, tc), F32)
        _loop_blocks(nb, step, (zero, zero))

    return _scan_call(body, [dy, a_f, h_f, a_b, h_b], 4, B, L, tc, "rg_scan_bwd")


_GELU_C = math.sqrt(2.0 / math.pi)


def _gelu_parts(x):
    th = jnp.tanh(_GELU_C * (x + 0.044715 * x * x * x))
    return 0.5 * x * (1.0 + th), th


def _mm_gated_out_bwd(dx, w_out, h_f, h_b, z, name, after=None):
    def epilogue(acc, e_refs, b_refs, o_refs, a_refs):
        x = e_refs[2][...]
        gl, th = _gelu_parts(x)
        dgl = 0.5 * (1.0 + th) + 0.5 * x * (1.0 - th * th) * (_GELU_C * (1.0 + 3.0 * 0.044715 * x * x))
        o_refs[0][...] = acc * gl
        o_refs[1][...] = (acc * (e_refs[0][...] + e_refs[1][...]) * dgl).astype(BF16)

    return _mm(dx, w_out, mode="nt", out_dtypes=(F32, BF16), out_cols=(D_MODEL, 2 * D_MODEL), extras=(h_f, h_b, z),
               ref_epi=epilogue, name=name, after=after)


def _row_block(i):
    return pl.ds(pl.multiple_of(i * SUBLANES, SUBLANES), SUBLANES)


def _rg_mix_fwd(z, conv_wb, wcat, gvec, B, L):
    nb = L // SUBLANES
    n_g = D_MODEL // LRU_BW

    def body(zg_ref, zr_ref, cw_ref, w_ref, gv_ref, rec_ref, af_s, ab_s, hf_ref, hb_ref, yg_ref, uf_s, ub_s):
        rec = _conv_apply(zr_ref[...], cw_ref, L)
        rec_ref[...] = rec
        pre = jnp.dot(rec.astype(BF16), w_ref[...], preferred_element_type=F32)
        for d, (a_s, u_s) in enumerate(((af_s, uf_s), (ab_s, ub_s))):
            a, u, _ = _gate_math(rec, pre, gv_ref, d, slice(None))
            a_s[...] = a
            u_s[...] = u

        def step(i, carry):
            c1, c2 = carry
            rows, rows_b = _row_block(i), _row_block(nb - 1 - i)
            p, h = _block_scan(af_s[rows, :], uf_s[rows, :], False)
            h = h + p * c1
            hf_ref[rows, :] = h
            p2, h2 = _block_scan(ab_s[rows_b, :], ub_s[rows_b, :], True)
            h2 = h2 + p2 * c2
            hb_ref[rows_b, :] = h2
            return h[_LAST:, :], h2[:1, :]

        zero = jnp.zeros((1, LRU_BW), F32)
        _loop_blocks(nb, step, (zero, zero))
        gl, _ = _gelu_parts(zg_ref[...])
        yg_ref[...] = ((hf_ref[...] + hb_ref[...]) * gl).astype(BF16)

    seq = lambda off: pl.BlockSpec((L, LRU_BW), lambda b, g: (b, off + g))
    vec = pl.BlockSpec((SUBLANES, LRU_BW), lambda b, g: (0, g))
    T = B * L
    return pl.pallas_call(
        body, name="rg_mix", grid=(B, n_g),
        in_specs=[seq(0), seq(n_g), vec, pl.BlockSpec((LRU_BW, 4 * LRU_BW), lambda b, g: (g, 0)), vec],
        out_specs=[seq(0)] * 6,
        out_shape=[jax.ShapeDtypeStruct((T, D_MODEL), F32)] * 5 + [jax.ShapeDtypeStruct((T, D_MODEL), BF16)],
        scratch_shapes=[pltpu.VMEM((L, LRU_BW), F32)] * 2,
        compiler_params=_params(("parallel", "parallel")),
    )(z, z, conv_wb, wcat, gvec)


def _make_wcat(w_a, w_x):
    g = jnp.stack([w_a[0, 0], w_x[0, 0], w_a[0, 1], w_x[0, 1]])
    return jnp.transpose(g, (1, 2, 0, 3)).reshape(D_MODEL, 4 * LRU_BW)


def _rows_at(part, first):
    return jnp.pad(part, ((first, SUBLANES - first - part.shape[0]), (0, 0)))


def _qk_slot(q_g, k_g):
    wide = lambda v, at: jnp.pad(v, ((0, SUBLANES - 1), (at, D_MODEL - at - HEAD_DIM)))
    return wide(q_g, 0) + wide(k_g, HEAD_DIM)


def _local_step(x, target, P, fetch, emit, B, L, after=None):
    g_mix, g_mlp = P["norm_mix_g"], P["norm_mlp_g"]
    h0 = _rms_fwd(x, g_mix[0:1], "rg_norm", after=after)
    w_in, conv_wb, wcat, gvec = fetch("rg", h0)
    z = _mm(h0, w_in, mode="nn", b_shard=True, name="rg_in")
    rec, a_f, a_b, h_f, h_b, yg = _rg_mix_fwd(z, conv_wb, wcat, gvec, B, L)
    w_out = fetch("rg_out", yg)
    x1, h1 = _mm_res_norm(yg, w_out, x, g_mlp[0:1], "rg_out")
    (x2, h3), mlp0 = _mlp_fwd(x1, h1, fetch, 0, lambda a, w, res, name: _mm_res_norm(a, w, res, g_mix[1:2], name))
    w_qkv, w_o = fetch("att", h3)
    qkv = _mm(h3, w_qkv, mode="nn", b_shard=True, name="attn_qkv")
    cos, sin = _rope_tables(L, B)
    qh, kh, vh = _qk_prep(qkv, cos, sin, P["q_g"], P["k_g"])
    o = _attn_fwd(qh, kh, vh, B, L)
    x3, h4 = _mm_res_norm(o, w_o, x2, g_mlp[1:2], "attn_out")
    (dx4, dx4_bf, loss_acc, d_final_g), mlp1 = _mlp_fwd(
        x3, h4, fetch, 1, lambda a, w, res, name: _mm_final_loss(a, w, res, target, P["final_g"], name))

    dx3, dx3_bf, dg_mlp1, d_up1, d_down1 = _mlp_bwd(x3, g_mlp[1:2], mlp1, dx4, dx4_bf, 1, None)
    tok = emit("mlp1", [d_up1, d_down1])
    d_wo = _mm(o, dx3_bf, mode="tn", out_dtypes=(BF16,), name="attn_dwo", after=tok)
    do = _mm(dx3_bf, w_o, mode="nt", out_dtypes=(BF16,), name="attn_do")
    dq, dk, dv = _attn_bwd(qh, kh, vh, o, do, B, L)
    dqkv, dq_g, dk_g = _qk_prep_bwd(qkv, dq, dk, dv, cos, sin, P["q_g"], P["k_g"])
    d_wqkv = _mm(h3, dqkv, mode="tn", o_shard=True, out_dtypes=(BF16,), name="attn_dwqkv")
    tok = emit("att", [d_wqkv, d_wo])
    dx2, dx2_bf, dg_mix1 = _mm_norm_bwd(dqkv, w_qkv, x2, dx3, g_mix[1:2], "attn_dh", after=tok)
    tok = emit("point_attn_done", [dx2_bf])
    dx1, dx1_bf, dg_mlp0, d_up0, d_down0 = _mlp_bwd(x1, g_mlp[0:1], mlp0, dx2, dx2_bf, 0, tok)
    d_wout = _mm(yg, dx1_bf, mode="tn", out_dtypes=(BF16,), name="rg_dwout")
    tok = emit("mlp0", [d_up0, d_down0, d_wout])
    dy, dgate = _mm_gated_out_bwd(dx1_bf, w_out, h_f, h_b, z, "rg_dyg", after=tok)
    du_f, da_f, du_b, da_b = _scan_bwd(dy, a_f, h_f, a_b, h_b, B, L)
    drec_c, d_wa, d_wx, d_gvec = _gate_bwd(rec, du_f, da_f, du_b, da_b, wcat, gvec)
    tok = emit("gates", [d_wa, d_wx])
    dz, d_convwb = _conv_bwd(z, drec_c, conv_wb, dgate, B, L, after=tok)
    tok = emit("point_mix_done", [dz])
    d_win = _mm(h0, dz, mode="tn", o_shard=True, out_dtypes=(BF16,), name="rg_dwin", after=tok)
    tok = emit("rg_in", [d_win])
    grad_x, _, dg_mix0 = _mm_norm_bwd(dz, w_in, x, dx1, g_mix[0:1], "rg_dh", after=tok)

    norms = (_rows_at(dg_mix0, 0) + _rows_at(dg_mix1, 1) + _rows_at(dg_mlp0, 2) + _rows_at(dg_mlp1, 3)
             + _rows_at(d_final_g, 4)
             + jnp.pad(loss_acc, ((LOSS_ROW, SUBLANES - 1 - LOSS_ROW), (0, D_MODEL - LANES))))
    vec = jnp.concatenate([norms, d_convwb, d_gvec, _qk_slot(dq_g, dk_g)], axis=0)
    return grad_x, vec


_MESH = pl.DeviceIdType.MESH


def _place():
    x, y, c = lax.axis_index("x"), lax.axis_index("y"), lax.axis_index("c")
    peers = [((1 - x) if j & 2 else x, (1 - y) if j & 1 else y) for j in (1, 2, 3)]
    return x, y, c, peers


def _sum_leading(slots, name):
    def body(s_ref, o_ref):
        acc = s_ref[0]
        for d in range(1, slots.shape[0]):
            acc = acc + s_ref[d]
        o_ref[...] = acc

    return pl.pallas_call(body, name=name, out_shape=jax.ShapeDtypeStruct(slots.shape[1:], slots.dtype))(slots)


_HBM = pl.BlockSpec(memory_space=pltpu.HBM)
_SEM = pl.BlockSpec(memory_space=pltpu.SEMAPHORE)
_EFFECT = pltpu.SideEffectType.DATAFLOW_SIDE_EFFECTING


_COPIES = dict(gather=N_CHIPS - 1, scatter=N_CHIPS - 1, swap=1, spread=N_DEVICES - 1,
               gather_half=N_CHIPS - 1, share_half=N_CHIPS - 1)


def _split_copies(kind, srcs, lands, send, recv):
    x, y, c, peers = _place()
    me = 2 * x + y
    per = _COPIES[kind]
    out = []
    for a in range(len(lands)):
        for j in range(per):
            if kind == "swap":
                src, there, here, dev = srcs[a], lands[a], lands[a], (x, y, 1 - c)
            elif kind == "spread":
                k = j + 1
                dev = ((1 - x) if k & 4 else x, (1 - y) if k & 2 else y, (1 - c) if k & 1 else c)
                mine = lands[a].at[4 * x + 2 * y + c]
                src, there, here = mine, mine, lands[a].at[4 * dev[0] + 2 * dev[1] + dev[2]]
            else:
                px, py = peers[j]
                dev = (px, py, c)
                if kind == "gather":
                    src, there, here = lands[a].at[me], lands[a].at[me], lands[a].at[2 * px + py]
                elif kind in ("gather_half", "share_half"):
                    half = lands[a].shape[1] // 2
                    mine, other = pl.ds(c * half, half), pl.ds((1 - c) * half, half)
                    if kind == "gather_half":
                        src = there = lands[a].at[me, mine]
                        here = lands[a].at[2 * px + py, mine]
                    else:
                        src = there = lands[a].at[2 * px + py, mine]
                        here = lands[a].at[2 * px + py, other]
                        dev = (x, y, 1 - c)
                else:
                    src, there, here = srcs[a].at[2 * px + py], lands[a].at[j], lands[a].at[j]
            mk = functools.partial(
                pltpu.make_async_remote_copy, src_ref=src, send_sem=send.at[per * a + j],
                recv_sem=recv.at[per * a + j], device_id=dev, device_id_type=_MESH)
            out.append((functools.partial(mk, dst_ref=there), functools.partial(mk, dst_ref=here)))
    return out


_CORE_PAIR = ("swap", "share_half")
CORE_PAIR_BARRIER_ID = 0
BARRIER_IDS = dict(gather_rest=1, scatter_mlp1=2, scatter_att=3, scatter_mlp0=4, scatter_gates=5, scatter_rg_in=6,
                   gather_gates=7, spread_vec=8)


def _entry_peers(kind):
    x, y, c, peers = _place()
    if kind in _CORE_PAIR:
        return [(x, y, 1 - c)]
    if kind == "spread":
        return [((1 - x) if k & 4 else x, (1 - y) if k & 2 else y, (1 - c) if k & 1 else c)
                for k in range(1, N_DEVICES)]
    return [(px, py, c) for px, py in peers]


def _entry_params(kind, barrier_id):
    if kind in _CORE_PAIR:
        barrier_id = CORE_PAIR_BARRIER_ID
    collective = {} if barrier_id is None else dict(collective_id=barrier_id)
    return pltpu.CompilerParams(has_side_effects=_EFFECT, **collective)


def _entry_handshake(kind, barrier_id):
    if kind in _CORE_PAIR or barrier_id is not None:
        barrier = pltpu.get_barrier_semaphore()
        peers = _entry_peers(kind)
        for peer in peers:
            pl.semaphore_signal(barrier, inc=1, device_id=peer, device_id_type=_MESH)
        pl.semaphore_wait(barrier, len(peers))


def _exchange_start(kind, srcs, lands, name, after=None, barrier_id=None):
    arrays = list(srcs) + list(lands)
    n_s, n, n_all = len(srcs), len(lands), len(srcs) + len(lands)
    n_sem = _COPIES[kind] * n
    order = _after_operand(after)
    n_x = len(order)

    def body(*refs):
        _entry_handshake(kind, barrier_id)
        send, recv = refs[n_all + n_x], refs[n_all + n_x + 1]
        token = refs[-1]
        for started, _ in _split_copies(kind, refs[:n_s], refs[n_s:n_all], send, recv):
            started().start()
        token[...] = jnp.zeros(token.shape, F32)

    res = pl.pallas_call(
        body, name=name,
        out_shape=(pltpu.SemaphoreType.DMA((n_sem,)), pltpu.SemaphoreType.DMA((n_sem,)),
                   *[pltpu.HBM(a.shape, a.dtype) for a in arrays], jax.ShapeDtypeStruct((SUBLANES, LANES), F32)),
        in_specs=[_HBM] * n_all + [_ANY] * n_x,
        out_specs=(_SEM, _SEM, *[_HBM] * n_all, pl.BlockSpec(memory_space=pltpu.VMEM)),
        input_output_aliases={i: 2 + i for i in range(n_all)},
        compiler_params=_entry_params(kind, barrier_id),
    )(*[pltpu.with_memory_space_constraint(a, pltpu.HBM) for a in arrays], *order)
    return (res[0], res[1], res[2:2 + n_s], res[2 + n_s:2 + n_all]), res[-1]


def _gather_start_groups(land_groups, name, after=None, kind="gather", barrier_id=None):
    arrays = [a for group in land_groups for a in group]
    n_all, n_g = len(arrays), len(land_groups)
    order = _after_operand(after)
    n_x = len(order)

    def body(*refs):
        _entry_handshake(kind, barrier_id)
        first = 0
        for gi, group in enumerate(land_groups):
            send, recv = refs[n_all + n_x + 2 * gi], refs[n_all + n_x + 2 * gi + 1]
            for started, _ in _split_copies(kind, [], refs[first:first + len(group)], send, recv):
                started().start()
            first += len(group)
        refs[-1][...] = jnp.zeros(refs[-1].shape, F32)

    sems = [pltpu.SemaphoreType.DMA((_COPIES[kind] * len(group),)) for group in land_groups for _ in range(2)]
    res = pl.pallas_call(
        body, name=name,
        out_shape=(*sems, *[pltpu.HBM(a.shape, a.dtype) for a in arrays], jax.ShapeDtypeStruct((SUBLANES, LANES), F32)),
        in_specs=[_HBM] * n_all + [_ANY] * n_x,
        out_specs=(*[_SEM] * (2 * n_g), *[_HBM] * n_all, pl.BlockSpec(memory_space=pltpu.VMEM)),
        input_output_aliases={i: 2 * n_g + i for i in range(n_all)},
        compiler_params=_entry_params(kind, barrier_id),
    )(*[pltpu.with_memory_space_constraint(a, pltpu.HBM) for a in arrays], *order)
    handles, first = [], 2 * n_g
    for gi, group in enumerate(land_groups):
        handles.append((res[2 * gi], res[2 * gi + 1], [], res[first:first + len(group)]))
        first += len(group)
    return handles, res[-1]


def _exchange_wait(kind, handle, after, name):
    send, recv, srcs, lands = handle
    arrays = list(srcs) + list(lands)
    n_s, n_all = len(srcs), len(arrays)
    order = list(after) if isinstance(after, (list, tuple)) else [after]

    def body(*refs):
        for started, landing in _split_copies(kind, refs[:n_s], refs[n_s:n_all], refs[n_all], refs[n_all + 1]):
            started().wait_send()
            landing().wait_recv()

    res = pl.pallas_call(
        body, name=name, out_shape=[pltpu.HBM(a.shape, a.dtype) for a in arrays],
        in_specs=[_HBM] * n_all + [_SEM, _SEM] + [_ANY] * len(order), out_specs=[_HBM] * n_all,
        input_output_aliases={i: i for i in range(n_all)},
        compiler_params=pltpu.CompilerParams(has_side_effects=_EFFECT),
    )(*arrays, send, recv, *order)
    return res[:n_s], res[n_s:]


def _index_operand(i):
    return jnp.reshape(i, (1,)).astype(jnp.int32)


def _cast_into_slot(src, row0, rows, me, dtype, name, after=None, add=None, n_slots=N_CHIPS):
    cols = src.shape[1]
    tm = min(512, rows)
    order = _after_operand(after)
    terms = [src] + ([] if add is None else [add])

    def body(me_ref, *rest):
        val = rest[0][...]
        if add is not None:
            val = val + rest[1][...]
        rest[-1][...] = val.astype(dtype)

    return pl.pallas_call(
        body, name=name,
        grid_spec=pltpu.PrefetchScalarGridSpec(
            num_scalar_prefetch=1, grid=(rows // tm,),
            in_specs=[pl.BlockSpec((tm, cols), lambda i, me_ref: (i + row0 // tm, 0))] * len(terms)
            + [_ANY] * len(order),
            out_specs=pl.BlockSpec((None, tm, cols), lambda i, me_ref: (me_ref[0], i, 0))),
        out_shape=jax.ShapeDtypeStruct((n_slots, rows, cols), dtype), compiler_params=_params(("parallel",)),
    )(_index_operand(me), *terms, *order)


def _sum_slots(mine, r, me, name):
    _, rows, cols = r.shape
    tm = min(512, rows)

    def body(me_ref, own_ref, r_ref, o_ref):
        o_ref[...] = ((own_ref[...].astype(F32) + r_ref[0].astype(F32)) + r_ref[1].astype(F32)) + r_ref[2].astype(F32)

    return pl.pallas_call(
        body, name=name,
        grid_spec=pltpu.PrefetchScalarGridSpec(
            num_scalar_prefetch=1, grid=(rows // tm,),
            in_specs=[pl.BlockSpec((None, tm, cols), lambda i, me_ref: (me_ref[0], i, 0)),
                      pl.BlockSpec((N_CHIPS - 1, tm, cols), lambda i, me_ref: (0, i, 0))],
            out_specs=pl.BlockSpec((tm, cols), lambda i, me_ref: (i, 0))),
        out_shape=jax.ShapeDtypeStruct((rows, cols), F32), compiler_params=_params(("parallel",)),
    )(_index_operand(me), mine, r)


ADAMW_INPUT_BUFFERS = 3


def _adamw(w, m, v, ps, qs, name):
    rows, cols = w.shape
    seg_rows = ps[0].shape[0]
    parts = list(ps) + ([] if qs is None else list(qs))
    tm = min(512, seg_rows)
    while seg_rows % tm:
        tm -= SUBLANES
    streams, ring = (4 if qs is None else 5), ADAMW_INPUT_BUFFERS
    while (streams * ring + 4 * 2) * tm * cols * 4 > MM_VMEM_BUDGET and tm % (2 * SUBLANES) == 0:
        tm //= 2
    per, n_seg, steps = seg_rows // tm, len(ps), rows // tm

    def body(*refs):
        in_refs, outs = refs[:3 + len(parts)], refs[3 + len(parts):7 + len(parts)]
        bufs, sems = refs[7 + len(parts):]
        i = pl.program_id(0)

        def tile_copy(j, src, row, slot):
            first = row * tm if isinstance(row, int) else pl.multiple_of(row * tm, tm)
            return pltpu.make_async_copy(src.at[pl.ds(first, tm), :], bufs.at[j, slot], sems.at[j, slot])

        def start(step, slot):
            for j in range(3):
                tile_copy(j, in_refs[j], step, slot).start()
            for s in range(n_seg):
                def go(s=s):
                    tile_copy(3, in_refs[3 + s], step - s * per, slot).start()
                    if qs is not None:
                        tile_copy(4, in_refs[3 + n_seg + s], step - s * per, slot).start()
                inside = (step >= s * per) & (step < (s + 1) * per)
                if isinstance(step, int):
                    if inside:
                        go()
                else:
                    pl.when(inside)(go)

        @pl.when(i == 0)
        def _():
            for k in range(min(ring, steps)):
                start(k, k)

        slot = lax.rem(i, ring)
        for j in range(streams):
            tile_copy(j, in_refs[0], 0, slot).wait()
        g = bufs[3, slot] if qs is None else bufs[3, slot] + bufs[4, slot]
        m1 = ADAM_B1 * bufs[1, slot] + (1.0 - ADAM_B1) * g
        v1 = ADAM_B2 * bufs[2, slot] + (1.0 - ADAM_B2) * (g * g)
        m_hat = m1 / (1.0 - ADAM_B1 ** ADAM_STEP)
        v_hat = v1 / (1.0 - ADAM_B2 ** ADAM_STEP)
        outs[0][...] = g
        outs[1][...] = (-ADAM_LR) * (m_hat / (jnp.sqrt(v_hat) + ADAM_EPS) + ADAM_WD * bufs[0, slot])
        outs[2][...] = m1
        outs[3][...] = v1

        @pl.when(i + ring < steps)
        def _():
            start(i + ring, slot)

    row_spec = pl.BlockSpec((tm, cols), lambda i: (i, 0))
    return pl.pallas_call(
        body, name=name, grid=(steps,), in_specs=[_ANY] * (3 + len(parts)),
        out_specs=[row_spec] * 4, out_shape=[jax.ShapeDtypeStruct((rows, cols), F32)] * 4,
        scratch_shapes=[pltpu.VMEM((streams, ring, tm, cols), F32), pltpu.SemaphoreType.DMA((streams, ring))],
        compiler_params=_params(("arbitrary",)),
    )(w, m, v, *parts)


def _put_cols(shard, me):
    full = jnp.zeros((shard.shape[0], D_MODEL), F32)
    return lax.dynamic_update_slice(full, shard, (0, me * (D_MODEL // N_CHIPS)))


def _gate_vec_slot(b_a, b_x, lam):
    return _rows_at(b_a, _ROW_BA) + _rows_at(b_x, _ROW_BX) + _rows_at(lam, _ROW_LAM)


def _pack_vec(p, me):
    return jnp.concatenate([
        _rows_at(p["norm_mix_g"], 0) + _rows_at(p["norm_mlp_g"], 2) + _rows_at(p["final_g"][None], 4),
        _rows_at(_put_cols(p["rg_conv_w"][0, :, 0, :], me), 0) + _rows_at(p["rg_conv_b"], 4),
        _gate_vec_slot(_put_cols(p["rg_b_a"][0], me), _put_cols(p["rg_b_x"][0], me), _put_cols(p["rg_lam"][0], me)),
        _qk_slot(p["at_q_g"], p["at_k_g"]),
    ], axis=0)


def _unpack_vec(r, me):
    def cols(rows):
        return lax.dynamic_slice(rows, (0, me * (D_MODEL // N_CHIPS)), (rows.shape[0], D_MODEL // N_CHIPS))

    gate = r[16:24]
    return dict(
        norm_mix_g=r[0:2], norm_mlp_g=r[2:4], final_g=r[4], rg_conv_w=cols(r[8:12])[None, :, None, :],
        rg_conv_b=r[12:13], rg_b_a=cols(gate[_ROW_BA:_ROW_BA + 2])[None], rg_b_x=cols(gate[_ROW_BX:_ROW_BX + 2])[None],
        rg_lam=cols(gate[_ROW_LAM:_ROW_LAM + 2])[None], at_q_g=r[24:25, 0:HEAD_DIM],
        at_k_g=r[24:25, HEAD_DIM:2 * HEAD_DIM])


_WEIGHTS = ['norm_mix_g', 'norm_mlp_g', 'rg_w_in', 'rg_conv_w', 'rg_conv_b', 'rg_w_a', 'rg_b_a', 'rg_w_x', 'rg_b_x',
            'rg_lam', 'rg_w_out', 'at_w_qkv', 'at_q_g', 'at_k_g', 'at_w_o', 'mlp_w_up', 'mlp_w_down', 'final_g']
_BIG = dict(rg_w_in=["rg_w_in"], rg_w_out=["rg_w_out"], at_w_qkv=["at_w_qkv"], at_w_o=["at_w_o"],
            mlp_w_up=["up0", "up1"], mlp_w_down=["down0", "down1"])


def kernel(x, *args):
    n_w = len(_WEIGHTS)
    w = dict(zip(_WEIGHTS, args[:n_w]))
    target = args[n_w]
    m = dict(zip(_WEIGHTS, args[n_w + 1:2 * n_w + 1]))
    v = dict(zip(_WEIGHTS, args[2 * n_w + 1:3 * n_w + 1]))
    B, L, _ = x.shape
    T = B * L
    me = 2 * lax.axis_index("x") + lax.axis_index("y")

    vec = jnp.concatenate([_gate_vec_slot(w["rg_b_a"][0], w["rg_b_x"][0], w["rg_lam"][0]),
                           _rows_at(w["rg_conv_w"][0, :, 0, :], 0)], axis=0)
    flat = lambda a: a.reshape(-1, a.shape[-1])
    rows_of = lambda k: w[k].shape[-2]
    groups = [("rg", [("rg_w_in", 0, BF16), (vec, 0, F32)]), ("rg_out", [("rg_w_out", 0, BF16)]),
              ("mlp0_up", [("mlp_w_up", 0, BF16)]), ("mlp0_down", [("mlp_w_down", 0, BF16)]),
              ("att", [("at_w_qkv", 0, BF16), ("at_w_o", 0, BF16)]),
              ("mlp1", [("mlp_w_up", 1, BF16), ("mlp_w_down", 1, BF16)])]

    def landing_zones(group, members, after):
        lands = []
        for n, (k, layer, dtype) in enumerate(members):
            src, rows = (flat(w[k]), rows_of(k)) if isinstance(k, str) else (k, k.shape[0])
            lands.append(_cast_into_slot(src, layer * rows, rows, me, dtype, f"place_{group}{n}", after=after))
        return lands

    halves, gathers = {}, {}
    halves["rg"], tok = _exchange_start("gather_half", [], landing_zones(*groups[0], None), "gather_rg_start")
    handles, tok = _gather_start_groups([landing_zones(g, members, tok) for g, members in groups[1:]],
                                        "gather_rest_start", after=tok, kind="gather_half",
                                        barrier_id=BARRIER_IDS["gather_rest"])
    halves.update(zip([g for g, _ in groups[1:]], handles))
    wcat = _make_wcat(w["rg_w_a"], w["rg_w_x"]).astype(BF16)

    packs = [_pack_vec(p, me) for p in (w, m, v)]

    ready = {}

    def share(some, after, name):
        landed = [_exchange_wait("gather_half", halves[g], after, f"gather_{g}_landed")[1] for g in some]
        handles, token = _gather_start_groups(landed, name, kind="share_half")
        gathers.update(zip(some, handles))
        return token

    tok = share(["rg"], [tok, wcat] + packs, "share_rg_start")

    def fetch(what, after):
        if what in ready:
            return ready[what]
        group = "mlp1" if what.startswith("mlp1") else what
        if group == "rg_out":
            share(["rg_out", "mlp0_up"], after, "share_early_start")
        elif group == "mlp0_up":
            share(["mlp0_down", "att"], after, "share_mid_start")
        _, full = _exchange_wait("share_half", gathers[group], after, f"gather_{group}_wait")
        if group == "att":
            share(["mlp1"], after, "share_mlp1_start")
        if group == "rg":
            vec_full = jnp.transpose(full[1], (1, 0, 2)).reshape(2 * SUBLANES, D_MODEL)
            conv_wb = vec_full[SUBLANES:] + _rows_at(w["rg_conv_b"], 4)
            return full[0], conv_wb, wcat, vec_full[:SUBLANES]
        if group == "rg_out":
            return full[0].reshape(D_MODEL, D_MODEL)
        if group == "att":
            return full[0], full[1].reshape(D_MODEL, D_MODEL)
        if group == "mlp1":
            ready["mlp1_up"], ready["mlp1_down"] = full[0], full[1].reshape(4 * D_MODEL, D_MODEL)
            return ready[what]
        return full[0] if group == "mlp0_up" else full[0].reshape(4 * D_MODEL, D_MODEL)

    names = dict(mlp1=["up1", "down1"], att=["at_w_qkv", "at_w_o"], mlp0=["up0", "down0", "rg_w_out"],
                 rg_in=["rg_w_in"], gates=["rg_w_a", "rg_w_x"])
    scatters, swaps, P, Q, res = {}, [], {}, {}, {}

    def start_scatter(group, grads):
        srcs = [g.reshape(N_CHIPS, -1, g.shape[-1]) for g in grads]
        lands = [lax.empty((N_CHIPS - 1,) + s.shape[1:], s.dtype) for s in srcs]
        scatters[group], token = _exchange_start("scatter", srcs, lands, f"scatter_{group}_start",
                                                 barrier_id=BARRIER_IDS[f"scatter_{group}"])
        return token

    def settle(groups, after):
        keys, parts = [], []
        for group in groups:
            srcs, lands = _exchange_wait("scatter", scatters[group], after, f"scatter_{group}_wait")
            for k, s, r in zip(names[group], srcs, lands):
                keys.append(k)
                parts.append(_sum_slots(s, r, me, f"sum_{k}"))
        handle, token = _exchange_start("swap", parts, [lax.empty(p.shape, F32) for p in parts],
                                        f"swap_{groups[0]}_start")
        swaps.append((keys, handle, f"swap_{groups[0]}_wait"))
        return token

    def finish(after):
        for keys, handle, name in swaps:
            mine, theirs = _exchange_wait("swap", handle, after, name)
            P.update(zip(keys, mine))
            Q.update(zip(keys, theirs))
        swaps.clear()
        last = after
        for k, parts in _BIG.items():
            if k in res or any(p not in P for p in parts):
                continue
            shape = w[k].shape
            two_d = lambda a: a.reshape(-1, shape[-1])
            outs = _adamw(two_d(w[k]), two_d(m[k]), two_d(v[k]), [P[p] for p in parts], [Q[p] for p in parts],
                          f"adamw_{k}")
            res[k] = [o.reshape(shape) for o in outs]
            last = outs[0]
        if "rg_w_a" in P and "gates" not in gathers:
            lands = [_cast_into_slot(P[k], 0, P[k].shape[0], me, F32, f"place_{k}", after=last, add=Q[k])
                     for k in names["gates"]]
            gathers["gates"], last = _exchange_start("gather", [], lands, "gather_gates_start", after=last,
                                                     barrier_id=BARRIER_IDS["gather_gates"])
        return last

    def emit(event, arrays):
        if event == "point_attn_done":
            return None
        if event == "point_mix_done":
            return settle(["mlp1", "att", "mlp0"], arrays[0])
        token = start_scatter(event, arrays)
        if event == "rg_in":
            return finish(settle(["gates"], token))
        return token

    P_vec = dict(norm_mix_g=w["norm_mix_g"], norm_mlp_g=w["norm_mlp_g"], final_g=w["final_g"][None],
                 q_g=w["at_q_g"], k_g=w["at_k_g"])
    grad_x, vec_part = _local_step(x.reshape(T, D_MODEL), target.reshape(T, D_MODEL), P_vec, fetch, emit, B, L,
                                   after=tok)

    me8 = 2 * me + lax.axis_index("c")
    vec_slots = _cast_into_slot(vec_part, 0, VEC_ROWS, me8, F32, "place_vec", n_slots=N_DEVICES)
    spread, tok = _exchange_start("spread", [], [vec_slots], "spread_vec_start", barrier_id=BARRIER_IDS["spread_vec"])
    last = settle(["rg_in"], tok)
    _, gate_grads = _exchange_wait("gather", gathers["gates"], last, "gather_gates_wait")
    for k, g in zip(names["gates"], gate_grads):
        two_d = lambda a: a.reshape(g.shape[0] * g.shape[1], g.shape[2])
        outs = _adamw(two_d(w[k]), two_d(m[k]), two_d(v[k]), [two_d(g)], None, f"adamw_{k}")
        res[k] = [o.reshape(w[k].shape) for o in outs]
        last = outs[0]
    _, (vec_all,) = _exchange_wait("spread", spread, last, "spread_vec_wait")
    vec_grad = _sum_leading(vec_all, "sum_vec")
    loss = vec_grad[LOSS_ROW, 0]
    outs = _adamw(*packs, [vec_grad], None, "adamw_vec")
    finish(outs[0])
    unpacked = [_unpack_vec(o, me) for o in outs]
    for k in _WEIGHTS:
        if k not in res:
            res[k] = [u[k] for u in unpacked]

    result = [loss, grad_x.reshape(B, L, D_MODEL)]
    for slot in range(4):
        result += [res[k][slot] for k in _WEIGHTS]
    return tuple(result)
```
